```python
import math
import jax
import jax.numpy as jnp
from jax import lax
import numpy as np

D_MODEL = 1024
BATCH = 8
SEQ = 4096
DEPTH = 1

HEAD_DIM = 64
A_GROUPS = ((128, 1), (512, 4), (2048, 16))
A_HEADS = 4
A_QKV_COLS = len(A_GROUPS) * 3 * A_HEADS * HEAD_DIM
A_OUT = A_HEADS * HEAD_DIM
B_Q_HEADS = 8
B_KV_HEADS = 2
B_GROUP = B_Q_HEADS // B_KV_HEADS
B_WINDOW = 128
B_Q_COLS = B_Q_HEADS * HEAD_DIM
B_KV_COLS = B_KV_HEADS * HEAD_DIM
M_HEADS = 4
M_HEAD_DIM = 128
M_Q_COLS = M_HEADS * M_HEAD_DIM
MEM_LEN = 256
IN_SPLITS = (A_QKV_COLS,
             A_QKV_COLS + B_Q_COLS,
             A_QKV_COLS + B_Q_COLS + B_KV_COLS,
             A_QKV_COLS + B_Q_COLS + 2 * B_KV_COLS)
IN_COLS = IN_SPLITS[-1] + M_Q_COLS
N_BRANCH = 3
D_FF = 2816
CONV_WIDTH = 3
ROPE_THETA = 500000.0
ROPE_DIM_FRAC = 4
BLOCK = 128
EPS = 1e-6

kernel_name = 'hybrid_dilated_swa_memory_convffn'


def rmsnorm(t, gain):
    tf = t.astype(jnp.float32)
    y = tf * lax.rsqrt(jnp.mean(tf * tf, axis=-1, keepdims=True) + EPS)
    return (y * gain.astype(jnp.float32)).astype(t.dtype)


def rope_partial(t, positions):
    dh = t.shape[-1]
    rot = dh // ROPE_DIM_FRAC
    half = rot // 2
    freqs = jnp.exp(jnp.arange(half, dtype=jnp.float32) * (-2.0 * math.log(ROPE_THETA) / rot))
    ang = positions.astype(jnp.float32)[:, :, None, None] * freqs
    cos, sin = jnp.cos(ang), jnp.sin(ang)
    t1 = t[..., :half].astype(jnp.float32)
    t2 = t[..., half:rot].astype(jnp.float32)
    rotated = jnp.concatenate([t1 * cos - t2 * sin, t2 * cos + t1 * sin], axis=-1).astype(t.dtype)
    return jnp.concatenate([rotated, t[..., rot:]], axis=-1)


def banded_attn(q, k, v, max_dist, sink=None):
    assert max_dist <= BLOCK
    n, hk, g, L, dh = q.shape
    nb = -(-L // BLOCK)
    pad = nb * BLOCK - L
    q = jnp.pad(q, ((0, 0), (0, 0), (0, 0), (0, pad), (0, 0))).reshape(n, hk, g, nb, BLOCK, dh)
    k = jnp.pad(k, ((0, 0), (0, 0), (0, pad), (0, 0))).reshape(n, hk, nb, BLOCK, dh)
    v = jnp.pad(v, ((0, 0), (0, 0), (0, pad), (0, 0))).reshape(n, hk, nb, BLOCK, dh)

    def with_prev(t):
        prev = jnp.pad(t, ((0, 0), (0, 0), (1, 0), (0, 0), (0, 0)))[:, :, :-1]
        return jnp.concatenate([prev, t], axis=3)

    kw, vw = with_prev(k), with_prev(v)
    s = jnp.einsum('nhgbqd,nhbkd->nhgbqk', q, kw).astype(jnp.float32) * (dh ** -0.5)
    qi = jnp.arange(BLOCK)[:, None]
    kj = jnp.arange(2 * BLOCK)[None, :]
    dist = qi + BLOCK - kj
    band = (dist >= 0) & (dist <= max_dist)
    blk = jnp.arange(nb)[:, None, None]
    mask = band[None] & ((blk > 0) | (kj[None] >= BLOCK))
    s = jnp.where(mask, s, -jnp.inf)
    lse = jax.nn.logsumexp(s, axis=-1)
    if sink is not None:
        lse = jnp.logaddexp(lse, sink.astype(jnp.float32)[None, :, :, None, None])
    p = jnp.exp(s - lse[..., None])
    o = jnp.einsum('nhgbqk,nhbkd->nhgbqd', p.astype(v.dtype), vw)
    o = o.reshape(n, hk, g, nb * BLOCK, dh)[:, :, :, :L]
    lse = lse.reshape(n, hk, g, nb * BLOCK)[:, :, :, :L]
    return o, lse


def to_residue(t, d):
    b, s, h, dh = t.shape
    return t.reshape(b, s // d, d, h, dh).transpose(0, 2, 3, 1, 4).reshape(b * d, h, s // d, dh)


def from_residue(t, d):
    nd, h, L = t.shape[:3]
    rest = t.shape[3:]
    t = t.reshape(nd // d, d, h, L, *rest)
    t = jnp.moveaxis(t, 3, 1)
    return t.reshape(nd // d, L * d, h, *rest)


def dilated_mixture_attn(qkv_a, positions, q_gain, k_gain):
    b, s = qkv_a.shape[:2]
    outs, lses = [], []
    for gi, (window, dil) in enumerate(A_GROUPS):
        q = rope_partial(rmsnorm(qkv_a[:, :, gi, 0], q_gain[gi]), positions)
        k = rope_partial(rmsnorm(qkv_a[:, :, gi, 1], k_gain[gi]), positions)
        v = qkv_a[:, :, gi, 2]
        o, lse = banded_attn(to_residue(q, dil)[:, :, None], to_residue(k, dil),
                             to_residue(v, dil), window // dil)
        outs.append(from_residue(o[:, :, 0], dil))
        lses.append(from_residue(lse[:, :, 0], dil))
    w = jax.nn.softmax(jnp.stack(lses), axis=0)
    o = jnp.einsum('gbshd,gbsh->bshd', jnp.stack(outs), w.astype(qkv_a.dtype))
    return o.reshape(b, s, A_OUT)


def sink_swa_gqa(q, k, v, positions, q_gain, k_gain, sinks):
    b, s = q.shape[:2]
    q = rope_partial(rmsnorm(q, q_gain), positions)
    k = rope_partial(rmsnorm(k, k_gain), positions)
    qg = q.reshape(b, s, B_KV_HEADS, B_GROUP, HEAD_DIM).transpose(0, 2, 3, 1, 4)
    o, _ = banded_attn(qg, k.transpose(0, 2, 1, 3), v.transpose(0, 2, 1, 3),
                       B_WINDOW - 1, sink=sinks.reshape(B_KV_HEADS, B_GROUP))
    return o.transpose(0, 3, 1, 2, 4).reshape(b, s, B_Q_COLS)


def memory_attn(mq, mem, mem_gain, w_kv, q_gain, k_gain):
    b, s = mq.shape[:2]
    m = mem.shape[1]
    kv = (rmsnorm(mem, mem_gain) @ w_kv).reshape(b, m, 2, M_HEADS, M_HEAD_DIM)
    mk = rmsnorm(kv[:, :, 0], k_gain)
    mv = kv[:, :, 1]
    q = rmsnorm(mq.reshape(b, s, M_HEADS, M_HEAD_DIM), q_gain)
    sc = jnp.einsum('bshd,bmhd->bhsm', q, mk).astype(jnp.float32) * (M_HEAD_DIM ** -0.5)
    p = jax.nn.softmax(sc, axis=-1)
    o = jnp.einsum('bhsm,bmhd->bshd', p.astype(mv.dtype), mv)
    return o.reshape(b, s, M_Q_COLS)


def conv_ffn(h, w_up, conv_w, conv_b, w_down):
    s = h.shape[1]
    u = h @ w_up
    up = jnp.pad(u, ((0, 0), (CONV_WIDTH - 1, 0), (0, 0)))
    c = conv_b
    for j in range(CONV_WIDTH):
        c = c + conv_w[j] * up[:, j:j + s]
    a, g = jnp.split(c, 2, axis=-1)
    return (jax.nn.silu(a) * g) @ w_down


def _fwd_setup_inputs(seed: int = 0) -> dict:
    key = jax.random.key(seed)
    ks = jax.random.split(key, 32)
    f32 = jnp.float32
    L = DEPTH

    def w(k, shape, fan_in):
        return jax.random.normal(k, shape, f32) * (fan_in ** -0.5)

    def gain(k, shape):
        return 1.0 + 0.02 * jax.random.normal(k, shape, f32)

    positions = (jax.random.randint(ks[2], (BATCH, 1), 0, 1024, dtype=jnp.int32)
                 + jnp.arange(SEQ, dtype=jnp.int32)[None, :])
    return {
        'x': jax.random.normal(ks[0], (BATCH, SEQ, D_MODEL), f32),
        'mem': jax.random.normal(ks[1], (BATCH, MEM_LEN, D_MODEL), f32),
        'positions': positions,
        'attn_norm': gain(ks[3], (L, D_MODEL)),
        'w_in': w(ks[4], (L, D_MODEL, IN_COLS), D_MODEL),
        'a_q_norm': gain(ks[5], (L, len(A_GROUPS), HEAD_DIM)),
        'a_k_norm': gain(ks[6], (L, len(A_GROUPS), HEAD_DIM)),
        'b_q_norm': gain(ks[7], (L, HEAD_DIM)),
        'b_k_norm': gain(ks[8], (L, HEAD_DIM)),
        'b_sinks': jax.random.normal(ks[9], (L, B_Q_HEADS), f32),
        'mem_norm': gain(ks[10], (L, D_MODEL)),
        'w_mem_kv': w(ks[11], (L, D_MODEL, 2 * M_Q_COLS), D_MODEL),
        'm_q_norm': gain(ks[12], (L, M_HEAD_DIM)),
        'm_k_norm': gain(ks[13], (L, M_HEAD_DIM)),
        'w_o_a': w(ks[14], (L, A_OUT, D_MODEL), A_OUT),
        'w_o_b': w(ks[15], (L, B_Q_COLS, D_MODEL), B_Q_COLS),
        'w_o_m': w(ks[16], (L, M_Q_COLS, D_MODEL), M_Q_COLS),
        'w_gate': w(ks[17], (L, D_MODEL, N_BRANCH * D_MODEL), D_MODEL),
        'b_gate': 0.01 * jax.random.normal(ks[18], (L, N_BRANCH * D_MODEL), f32),
        'w_out': w(ks[19], (L, D_MODEL, D_MODEL), D_MODEL),
        'ffn_norm': gain(ks[20], (L, D_MODEL)),
        'w_up': w(ks[21], (L, D_MODEL, 2 * D_FF), D_MODEL),
        'conv_w': w(ks[22], (L, CONV_WIDTH, 2 * D_FF), CONV_WIDTH),
        'conv_b': 0.01 * jax.random.normal(ks[23], (L, 2 * D_FF), f32),
        'w_down': w(ks[24], (L, D_FF, D_MODEL), D_FF),
    }


def _fwd_reference(x, mem, positions, attn_norm, w_in, a_q_norm, a_k_norm, b_q_norm, b_k_norm,
              b_sinks, mem_norm, w_mem_kv, m_q_norm, m_k_norm, w_o_a, w_o_b, w_o_m,
              w_gate, b_gate, w_out, ffn_norm, w_up, conv_w, conv_b, w_down):
    b, s, _ = x.shape
    for l in range(DEPTH):
        h = rmsnorm(x, attn_norm[l])
        proj = h @ w_in[l]
        qkv_a, b_q, b_k, b_v, m_q = jnp.split(proj, IN_SPLITS, axis=-1)
        qkv_a = qkv_a.reshape(b, s, len(A_GROUPS), 3, A_HEADS, HEAD_DIM)
        o_a = dilated_mixture_attn(qkv_a, positions, a_q_norm[l], a_k_norm[l])
        o_b = sink_swa_gqa(b_q.reshape(b, s, B_Q_HEADS, HEAD_DIM),
                           b_k.reshape(b, s, B_KV_HEADS, HEAD_DIM),
                           b_v.reshape(b, s, B_KV_HEADS, HEAD_DIM),
                           positions, b_q_norm[l], b_k_norm[l], b_sinks[l])
        o_m = memory_attn(m_q, mem, mem_norm[l], w_mem_kv[l], m_q_norm[l], m_k_norm[l])
        gates = jax.nn.sigmoid((h @ w_gate[l] + b_gate[l]).astype(jnp.float32))
        gates = gates.astype(x.dtype).reshape(b, s, N_BRANCH, D_MODEL)
        merged = (gates[:, :, 0] * (o_a @ w_o_a[l])
                  + gates[:, :, 1] * (o_b @ w_o_b[l])
                  + gates[:, :, 2] * (o_m @ w_o_m[l]))
        x = x + merged @ w_out[l]
        x = x + conv_ffn(rmsnorm(x, ffn_norm[l]), w_up[l], conv_w[l], conv_b[l], w_down[l])
    return x


import jax as _jax
import jax.numpy as _jnp

TWIN_FORMAT = 'train_step'
FWD_PARAMS = ['x', 'mem', 'positions', 'attn_norm', 'w_in', 'a_q_norm', 'a_k_norm', 'b_q_norm', 'b_k_norm', 'b_sinks', 'mem_norm', 'w_mem_kv', 'm_q_norm', 'm_k_norm', 'w_o_a', 'w_o_b', 'w_o_m', 'w_gate', 'b_gate', 'w_out', 'ffn_norm', 'w_up', 'conv_w', 'conv_b', 'w_down']
TWIN_WEIGHTS = ['attn_norm', 'w_in', 'a_q_norm', 'a_k_norm', 'b_q_norm', 'b_k_norm', 'b_sinks', 'mem_norm', 'w_mem_kv', 'm_q_norm', 'm_k_norm', 'w_o_a', 'w_o_b', 'w_o_m', 'w_gate', 'b_gate', 'w_out', 'ffn_norm', 'w_up', 'conv_w', 'conv_b', 'w_down']
TWIN_DIFF_INPUT = 'x'
TWIN_INPUTS = ['x', 'mem', 'positions', 'attn_norm', 'w_in', 'a_q_norm', 'a_k_norm', 'b_q_norm', 'b_k_norm', 'b_sinks', 'mem_norm', 'w_mem_kv', 'm_q_norm', 'm_k_norm', 'w_o_a', 'w_o_b', 'w_o_m', 'w_gate', 'b_gate', 'w_out', 'ffn_norm', 'w_up', 'conv_w', 'conv_b', 'w_down', 'loss_target', 'm_attn_norm', 'm_w_in', 'm_a_q_norm', 'm_a_k_norm', 'm_b_q_norm', 'm_b_k_norm', 'm_b_sinks', 'm_mem_norm', 'm_w_mem_kv', 'm_m_q_norm', 'm_m_k_norm', 'm_w_o_a', 'm_w_o_b', 'm_w_o_m', 'm_w_gate', 'm_b_gate', 'm_w_out', 'm_ffn_norm', 'm_w_up', 'm_conv_w', 'm_conv_b', 'm_w_down', 'v_attn_norm', 'v_w_in', 'v_a_q_norm', 'v_a_k_norm', 'v_b_q_norm', 'v_b_k_norm', 'v_b_sinks', 'v_mem_norm', 'v_w_mem_kv', 'v_m_q_norm', 'v_m_k_norm', 'v_w_o_a', 'v_w_o_b', 'v_w_o_m', 'v_w_gate', 'v_b_gate', 'v_w_out', 'v_ffn_norm', 'v_w_up', 'v_conv_w', 'v_conv_b', 'v_w_down']
TWIN_OUTPUTS = ['loss', 'grad_x', 'grad_attn_norm', 'grad_w_in', 'grad_a_q_norm', 'grad_a_k_norm', 'grad_b_q_norm', 'grad_b_k_norm', 'grad_b_sinks', 'grad_mem_norm', 'grad_w_mem_kv', 'grad_m_q_norm', 'grad_m_k_norm', 'grad_w_o_a', 'grad_w_o_b', 'grad_w_o_m', 'grad_w_gate', 'grad_b_gate', 'grad_w_out', 'grad_ffn_norm', 'grad_w_up', 'grad_conv_w', 'grad_conv_b', 'grad_w_down', 'delta_attn_norm', 'delta_w_in', 'delta_a_q_norm', 'delta_a_k_norm', 'delta_b_q_norm', 'delta_b_k_norm', 'delta_b_sinks', 'delta_mem_norm', 'delta_w_mem_kv', 'delta_m_q_norm', 'delta_m_k_norm', 'delta_w_o_a', 'delta_w_o_b', 'delta_w_o_m', 'delta_w_gate', 'delta_b_gate', 'delta_w_out', 'delta_ffn_norm', 'delta_w_up', 'delta_conv_w', 'delta_conv_b', 'delta_w_down', 'new_m_attn_norm', 'new_m_w_in', 'new_m_a_q_norm', 'new_m_a_k_norm', 'new_m_b_q_norm', 'new_m_b_k_norm', 'new_m_b_sinks', 'new_m_mem_norm', 'new_m_w_mem_kv', 'new_m_m_q_norm', 'new_m_m_k_norm', 'new_m_w_o_a', 'new_m_w_o_b', 'new_m_w_o_m', 'new_m_w_gate', 'new_m_b_gate', 'new_m_w_out', 'new_m_ffn_norm', 'new_m_w_up', 'new_m_conv_w', 'new_m_conv_b', 'new_m_w_down', 'new_v_attn_norm', 'new_v_w_in', 'new_v_a_q_norm', 'new_v_a_k_norm', 'new_v_b_q_norm', 'new_v_b_k_norm', 'new_v_b_sinks', 'new_v_mem_norm', 'new_v_w_mem_kv', 'new_v_m_q_norm', 'new_v_m_k_norm', 'new_v_w_o_a', 'new_v_w_o_b', 'new_v_w_o_m', 'new_v_w_gate', 'new_v_b_gate', 'new_v_w_out', 'new_v_ffn_norm', 'new_v_w_up', 'new_v_conv_w', 'new_v_conv_b', 'new_v_w_down']
TWIN_LEAF_KINDS = {'loss': 'loss', 'grad_x': 'grad_x', 'grad_attn_norm': 'grad_w', 'grad_w_in': 'grad_w', 'grad_a_q_norm': 'grad_w', 'grad_a_k_norm': 'grad_w', 'grad_b_q_norm': 'grad_w', 'grad_b_k_norm': 'grad_w', 'grad_b_sinks': 'grad_w', 'grad_mem_norm': 'grad_w', 'grad_w_mem_kv': 'grad_w', 'grad_m_q_norm': 'grad_w', 'grad_m_k_norm': 'grad_w', 'grad_w_o_a': 'grad_w', 'grad_w_o_b': 'grad_w', 'grad_w_o_m': 'grad_w', 'grad_w_gate': 'grad_w', 'grad_b_gate': 'grad_w', 'grad_w_out': 'grad_w', 'grad_ffn_norm': 'grad_w', 'grad_w_up': 'grad_w', 'grad_conv_w': 'grad_w', 'grad_conv_b': 'grad_w', 'grad_w_down': 'grad_w', 'delta_attn_norm': 'delta_w', 'delta_w_in': 'delta_w', 'delta_a_q_norm': 'delta_w', 'delta_a_k_norm': 'delta_w', 'delta_b_q_norm': 'delta_w', 'delta_b_k_norm': 'delta_w', 'delta_b_sinks': 'delta_w', 'delta_mem_norm': 'delta_w', 'delta_w_mem_kv': 'delta_w', 'delta_m_q_norm': 'delta_w', 'delta_m_k_norm': 'delta_w', 'delta_w_o_a': 'delta_w', 'delta_w_o_b': 'delta_w', 'delta_w_o_m': 'delta_w', 'delta_w_gate': 'delta_w', 'delta_b_gate': 'delta_w', 'delta_w_out': 'delta_w', 'delta_ffn_norm': 'delta_w', 'delta_w_up': 'delta_w', 'delta_conv_w': 'delta_w', 'delta_conv_b': 'delta_w', 'delta_w_down': 'delta_w', 'new_m_attn_norm': 'new_m', 'new_m_w_in': 'new_m', 'new_m_a_q_norm': 'new_m', 'new_m_a_k_norm': 'new_m', 'new_m_b_q_norm': 'new_m', 'new_m_b_k_norm': 'new_m', 'new_m_b_sinks': 'new_m', 'new_m_mem_norm': 'new_m', 'new_m_w_mem_kv': 'new_m', 'new_m_m_q_norm': 'new_m', 'new_m_m_k_norm': 'new_m', 'new_m_w_o_a': 'new_m', 'new_m_w_o_b': 'new_m', 'new_m_w_o_m': 'new_m', 'new_m_w_gate': 'new_m', 'new_m_b_gate': 'new_m', 'new_m_w_out': 'new_m', 'new_m_ffn_norm': 'new_m', 'new_m_w_up': 'new_m', 'new_m_conv_w': 'new_m', 'new_m_conv_b': 'new_m', 'new_m_w_down': 'new_m', 'new_v_attn_norm': 'new_v', 'new_v_w_in': 'new_v', 'new_v_a_q_norm': 'new_v', 'new_v_a_k_norm': 'new_v', 'new_v_b_q_norm': 'new_v', 'new_v_b_k_norm': 'new_v', 'new_v_b_sinks': 'new_v', 'new_v_mem_norm': 'new_v', 'new_v_w_mem_kv': 'new_v', 'new_v_m_q_norm': 'new_v', 'new_v_m_k_norm': 'new_v', 'new_v_w_o_a': 'new_v', 'new_v_w_o_b': 'new_v', 'new_v_w_o_m': 'new_v', 'new_v_w_gate': 'new_v', 'new_v_b_gate': 'new_v', 'new_v_w_out': 'new_v', 'new_v_ffn_norm': 'new_v', 'new_v_w_up': 'new_v', 'new_v_conv_w': 'new_v', 'new_v_conv_b': 'new_v', 'new_v_w_down': 'new_v'}


def _forward(args):
    return _fwd_reference(*[args[k] for k in FWD_PARAMS])


def _output_shape():
    def fwd():
        inp = _fwd_setup_inputs(0)
        return _fwd_reference(*[inp[k] for k in FWD_PARAMS])
    out = _jax.eval_shape(fwd)
    return out.shape, out.dtype

N_MICROBATCH = 1
ADAM_LR = 0.001
ADAM_B1 = 0.9
ADAM_B2 = 0.999
ADAM_EPS = 1e-08
ADAM_WD = 0.01
ADAM_STEP = 10
PER_EXAMPLE_BATCH_AXIS = {'x': 0, 'mem': 0, 'positions': 0, 'loss_target': 0}
SHARED_INPUTS = []
_WEIGHT_DTYPES = {'attn_norm': _jnp.float32, 'w_in': _jnp.float32, 'a_q_norm': _jnp.float32, 'a_k_norm': _jnp.float32, 'b_q_norm': _jnp.float32, 'b_k_norm': _jnp.float32, 'b_sinks': _jnp.float32, 'mem_norm': _jnp.float32, 'w_mem_kv': _jnp.float32, 'm_q_norm': _jnp.float32, 'm_k_norm': _jnp.float32, 'w_o_a': _jnp.float32, 'w_o_b': _jnp.float32, 'w_o_m': _jnp.float32, 'w_gate': _jnp.float32, 'b_gate': _jnp.float32, 'w_out': _jnp.float32, 'ffn_norm': _jnp.float32, 'w_up': _jnp.float32, 'conv_w': _jnp.float32, 'conv_b': _jnp.float32, 'w_down': _jnp.float32}
MOMENT_SCALE = {'attn_norm': 2.508787e-01, 'w_in': 7.026085e-02, 'a_q_norm': 3.683274e-01, 'a_k_norm': 3.714104e-01, 'b_q_norm': 2.046954e+00, 'b_k_norm': 2.043896e+00, 'b_sinks': 7.243313e-01, 'mem_norm': 1.465323e-01, 'w_mem_kv': 1.098724e-01, 'm_q_norm': 7.698957e-01, 'm_k_norm': 7.694313e-01, 'w_o_a': 5.673480e-02, 'w_o_b': 6.591127e-02, 'w_o_m': 1.132418e-01, 'w_gate': 1.680274e-02, 'b_gate': 4.308644e-02, 'w_out': 9.998715e-02, 'ffn_norm': 2.657442e+01, 'w_up': 2.620076e-01, 'conv_w': 3.704000e+00, 'conv_b': 3.322983e+00, 'w_down': 3.817182e-01}


def _to_microbatches(a, axis):
    t = _jnp.moveaxis(a, axis, 0)
    t = t.reshape((N_MICROBATCH, t.shape[0] // N_MICROBATCH) + t.shape[1:])
    return _jnp.moveaxis(t, 1, axis + 1)


def setup_inputs(seed: int = 0) -> dict:
    inp = _fwd_setup_inputs(seed)
    key = _jax.random.fold_in(_jax.random.key(seed), 7919)
    shape, _ = _output_shape()
    out = dict(inp)
    out["loss_target"] = _jax.random.normal(_jax.random.fold_in(key, 0), shape, _jnp.float32)
    for i, name in enumerate(TWIN_WEIGHTS):
        w = inp[name].astype(_jnp.float32)
        if MOMENT_SCALE is None:
            s = _jnp.sqrt(_jnp.mean(_jnp.square(w)) + 1e-30)
        else:
            s = MOMENT_SCALE[name]
        km, kv = _jax.random.split(_jax.random.fold_in(key, i + 1))
        out[name] = w
        out["m_" + name] = s * _jax.random.normal(km, w.shape, _jnp.float32)
        out["v_" + name] = (s * s) * _jax.random.uniform(kv, w.shape, _jnp.float32, 0.5, 1.5)
    if N_MICROBATCH > 1:
        for name, axis in PER_EXAMPLE_BATCH_AXIS.items():
            out[name] = _to_microbatches(out[name], axis)
    return {'x': out['x'], 'mem': out['mem'], 'positions': out['positions'], 'attn_norm': out['attn_norm'], 'w_in': out['w_in'], 'a_q_norm': out['a_q_norm'], 'a_k_norm': out['a_k_norm'], 'b_q_norm': out['b_q_norm'], 'b_k_norm': out['b_k_norm'], 'b_sinks': out['b_sinks'], 'mem_norm': out['mem_norm'], 'w_mem_kv': out['w_mem_kv'], 'm_q_norm': out['m_q_norm'], 'm_k_norm': out['m_k_norm'], 'w_o_a': out['w_o_a'], 'w_o_b': out['w_o_b'], 'w_o_m': out['w_o_m'], 'w_gate': out['w_gate'], 'b_gate': out['b_gate'], 'w_out': out['w_out'], 'ffn_norm': out['ffn_norm'], 'w_up': out['w_up'], 'conv_w': out['conv_w'], 'conv_b': out['conv_b'], 'w_down': out['w_down'], 'loss_target': out['loss_target'], 'm_attn_norm': out['m_attn_norm'], 'm_w_in': out['m_w_in'], 'm_a_q_norm': out['m_a_q_norm'], 'm_a_k_norm': out['m_a_k_norm'], 'm_b_q_norm': out['m_b_q_norm'], 'm_b_k_norm': out['m_b_k_norm'], 'm_b_sinks': out['m_b_sinks'], 'm_mem_norm': out['m_mem_norm'], 'm_w_mem_kv': out['m_w_mem_kv'], 'm_m_q_norm': out['m_m_q_norm'], 'm_m_k_norm': out['m_m_k_norm'], 'm_w_o_a': out['m_w_o_a'], 'm_w_o_b': out['m_w_o_b'], 'm_w_o_m': out['m_w_o_m'], 'm_w_gate': out['m_w_gate'], 'm_b_gate': out['m_b_gate'], 'm_w_out': out['m_w_out'], 'm_ffn_norm': out['m_ffn_norm'], 'm_w_up': out['m_w_up'], 'm_conv_w': out['m_conv_w'], 'm_conv_b': out['m_conv_b'], 'm_w_down': out['m_w_down'], 'v_attn_norm': out['v_attn_norm'], 'v_w_in': out['v_w_in'], 'v_a_q_norm': out['v_a_q_norm'], 'v_a_k_norm': out['v_a_k_norm'], 'v_b_q_norm': out['v_b_q_norm'], 'v_b_k_norm': out['v_b_k_norm'], 'v_b_sinks': out['v_b_sinks'], 'v_mem_norm': out['v_mem_norm'], 'v_w_mem_kv': out['v_w_mem_kv'], 'v_m_q_norm': out['v_m_q_norm'], 'v_m_k_norm': out['v_m_k_norm'], 'v_w_o_a': out['v_w_o_a'], 'v_w_o_b': out['v_w_o_b'], 'v_w_o_m': out['v_w_o_m'], 'v_w_gate': out['v_w_gate'], 'v_b_gate': out['v_b_gate'], 'v_w_out': out['v_w_out'], 'v_ffn_norm': out['v_ffn_norm'], 'v_w_up': out['v_w_up'], 'v_conv_w': out['v_conv_w'], 'v_conv_b': out['v_conv_b'], 'v_w_down': out['v_w_down']}


def _loss(weights, diff, rest, loss_target):
    with _jax.named_scope("forward"):
        args = {**rest, TWIN_DIFF_INPUT: diff, **{k: w.astype(_WEIGHT_DTYPES[k]) for k, w in weights.items()}}
        y = _forward(args)
    with _jax.named_scope("loss_head"):
        err = _jnp.square(y.astype(_jnp.float32) - loss_target)
        return 0.5 * _jnp.sum(_jnp.mean(err, axis=-1)) if err.ndim else 0.5 * err


def _adamw(w, g, m, v):
    m = ADAM_B1 * m + (1.0 - ADAM_B1) * g
    v = ADAM_B2 * v + (1.0 - ADAM_B2) * _jnp.square(g)
    m_hat = m / (1.0 - ADAM_B1 ** ADAM_STEP)
    v_hat = v / (1.0 - ADAM_B2 ** ADAM_STEP)
    delta = -ADAM_LR * (m_hat / (_jnp.sqrt(v_hat) + ADAM_EPS) + ADAM_WD * w)
    return delta, m, v


def reference(x, mem, positions, attn_norm, w_in, a_q_norm, a_k_norm, b_q_norm, b_k_norm, b_sinks, mem_norm, w_mem_kv, m_q_norm, m_k_norm, w_o_a, w_o_b, w_o_m, w_gate, b_gate, w_out, ffn_norm, w_up, conv_w, conv_b, w_down, loss_target, m_attn_norm, m_w_in, m_a_q_norm, m_a_k_norm, m_b_q_norm, m_b_k_norm, m_b_sinks, m_mem_norm, m_w_mem_kv, m_m_q_norm, m_m_k_norm, m_w_o_a, m_w_o_b, m_w_o_m, m_w_gate, m_b_gate, m_w_out, m_ffn_norm, m_w_up, m_conv_w, m_conv_b, m_w_down, v_attn_norm, v_w_in, v_a_q_norm, v_a_k_norm, v_b_q_norm, v_b_k_norm, v_b_sinks, v_mem_norm, v_w_mem_kv, v_m_q_norm, v_m_k_norm, v_w_o_a, v_w_o_b, v_w_o_m, v_w_gate, v_b_gate, v_w_out, v_ffn_norm, v_w_up, v_conv_w, v_conv_b, v_w_down):
    given = dict(x=x, mem=mem, positions=positions, attn_norm=attn_norm, w_in=w_in, a_q_norm=a_q_norm, a_k_norm=a_k_norm, b_q_norm=b_q_norm, b_k_norm=b_k_norm, b_sinks=b_sinks, mem_norm=mem_norm, w_mem_kv=w_mem_kv, m_q_norm=m_q_norm, m_k_norm=m_k_norm, w_o_a=w_o_a, w_o_b=w_o_b, w_o_m=w_o_m, w_gate=w_gate, b_gate=b_gate, w_out=w_out, ffn_norm=ffn_norm, w_up=w_up, conv_w=conv_w, conv_b=conv_b, w_down=w_down, loss_target=loss_target, m_attn_norm=m_attn_norm, m_w_in=m_w_in, m_a_q_norm=m_a_q_norm, m_a_k_norm=m_a_k_norm, m_b_q_norm=m_b_q_norm, m_b_k_norm=m_b_k_norm, m_b_sinks=m_b_sinks, m_mem_norm=m_mem_norm, m_w_mem_kv=m_w_mem_kv, m_m_q_norm=m_m_q_norm, m_m_k_norm=m_m_k_norm, m_w_o_a=m_w_o_a, m_w_o_b=m_w_o_b, m_w_o_m=m_w_o_m, m_w_gate=m_w_gate, m_b_gate=m_b_gate, m_w_out=m_w_out, m_ffn_norm=m_ffn_norm, m_w_up=m_w_up, m_conv_w=m_conv_w, m_conv_b=m_conv_b, m_w_down=m_w_down, v_attn_norm=v_attn_norm, v_w_in=v_w_in, v_a_q_norm=v_a_q_norm, v_a_k_norm=v_a_k_norm, v_b_q_norm=v_b_q_norm, v_b_k_norm=v_b_k_norm, v_b_sinks=v_b_sinks, v_mem_norm=v_mem_norm, v_w_mem_kv=v_w_mem_kv, v_m_q_norm=v_m_q_norm, v_m_k_norm=v_m_k_norm, v_w_o_a=v_w_o_a, v_w_o_b=v_w_o_b, v_w_o_m=v_w_o_m, v_w_gate=v_w_gate, v_b_gate=v_b_gate, v_w_out=v_w_out, v_ffn_norm=v_ffn_norm, v_w_up=v_w_up, v_conv_w=v_conv_w, v_conv_b=v_conv_b, v_w_down=v_w_down)
    weights = {n: given[n] for n in TWIN_WEIGHTS}
    shared = {n: given[n] for n in SHARED_INPUTS}
    per_example = {n: given[n] for n in ['x', 'mem', 'positions']}
    grad_fn = _jax.value_and_grad(_loss, argnums=(0, 1))

    def one_microbatch(ex, loss_target):
        ex = dict(ex)
        diff = ex.pop(TWIN_DIFF_INPUT)
        return grad_fn(weights, diff, {**shared, **ex}, loss_target)

    if N_MICROBATCH == 1:
        loss, (grad_w, grad_x) = one_microbatch(per_example, given["loss_target"])
    else:
        def body(carry, xs):
            loss_sum, grad_sum = carry
            l_k, (gw_k, gx_k) = one_microbatch(xs[0], xs[1])
            with _jax.named_scope("update"):
                return (loss_sum + l_k, _jax.tree.map(_jnp.add, grad_sum, gw_k)), gx_k

        init = (_jnp.zeros((), _jnp.float32), _jax.tree.map(_jnp.zeros_like, weights))
        (loss, grad_w), grad_x = _jax.lax.scan(body, init, (per_example, given["loss_target"]))
    with _jax.named_scope("update"):
        delta_w, new_m, new_v = {}, {}, {}
        for n in TWIN_WEIGHTS:
            delta_w[n], new_m[n], new_v[n] = _adamw(weights[n], grad_w[n], given["m_" + n], given["v_" + n])
    return (loss, grad_x, *[grad_w[n] for n in TWIN_WEIGHTS], *[delta_w[n] for n in TWIN_WEIGHTS],
            *[new_m[n] for n in TWIN_WEIGHTS], *[new_v[n] for n in TWIN_WEIGHTS])
```

```python
import functools
import math

import jax
import jax.numpy as jnp
from jax import lax
from jax.experimental import pallas as pl
from jax.experimental.pallas import tpu as pltpu

F32 = jnp.float32
BF16 = jnp.bfloat16

N_DEV = 8
D_MODEL = 1024
HEAD_DIM = 64
A_GROUPS = ((128, 1), (512, 4), (2048, 16))
B_WINDOW = 128
M_HEADS = 4
M_HEAD_DIM = 128
MEM_LEN = 256
D_FF = 2816
ROPE_THETA = 500000.0
ROPE_DIMS = 16
BLOCK = 128
EPS = 1e-6
LANES = 128

ADAM_LR = 0.001
ADAM_B1 = 0.9
ADAM_B2 = 0.999
ADAM_EPS = 1e-08
ADAM_WD = 0.01
ADAM_STEP = 10

VMEM_LIMIT_BYTES = 56 * 1024 * 1024
MESH = pl.DeviceIdType.MESH

WEIGHTS = ['attn_norm', 'w_in', 'a_q_norm', 'a_k_norm', 'b_q_norm', 'b_k_norm', 'b_sinks', 'mem_norm',
           'w_mem_kv', 'm_q_norm', 'm_k_norm', 'w_o_a', 'w_o_b', 'w_o_m', 'w_gate', 'b_gate', 'w_out',
           'ffn_norm', 'w_up', 'conv_w', 'conv_b', 'w_down']
BIG = {'w_in': 1, 'w_mem_kv': 0, 'w_o_a': 1, 'w_o_b': 1, 'w_o_m': 1, 'w_gate': 1, 'w_out': 0, 'w_up': 1,
       'conv_w': 1, 'w_down': 0}
SMALL = [n for n in WEIGHTS if n not in BIG]


def _cparams(n_grid):
    return pltpu.CompilerParams(dimension_semantics=("arbitrary",) * n_grid, vmem_limit_bytes=VMEM_LIMIT_BYTES)


def _pick(n, cands=(512, 256, 128)):
    for c in cands:
        if n % c == 0:
            return c
    return n


def _seg_matrix(width):
    shift = width.bit_length() - 1
    r = lax.shift_right_logical(lax.broadcasted_iota(jnp.int32, (LANES, LANES), 0), shift)
    c = lax.shift_right_logical(lax.broadcasted_iota(jnp.int32, (LANES, LANES), 1), shift)
    return jnp.where(r == c, 1.0, 0.0).astype(BF16)


def _seg_sum(x, seg):
    hi = x.astype(BF16)
    r1 = x - hi.astype(F32)
    mid = r1.astype(BF16)
    lo = (r1 - mid.astype(F32)).astype(BF16)
    dot = functools.partial(jnp.dot, preferred_element_type=F32)
    return dot(hi, seg) + dot(mid, seg) + dot(lo, seg)


def _rope(y, c, s1, s2):
    return y * c + pltpu.roll(y, LANES - ROPE_DIMS // 2, 1) * s1 + pltpu.roll(y, ROPE_DIMS // 2, 1) * s2


def _unrope(dy, c, s1, s2):
    return dy * c + pltpu.roll(dy * s1, ROPE_DIMS // 2, 1) + pltpu.roll(dy * s2, LANES - ROPE_DIMS // 2, 1)


def _sigmoid(x):
    return 1.0 / (1.0 + jnp.exp(-x))


def _rms_fwd(x, gain, name):
    s_len, d = x.shape
    tm = 512

    def body(x_ref, g_ref, h_ref, r_ref):
        xv = x_ref[...]
        r = lax.rsqrt(jnp.mean(xv * xv, axis=-1, keepdims=True) + EPS)
        h_ref[...] = ((xv * r) * g_ref[...]).astype(BF16)
        r_ref[...] = r

    return pl.pallas_call(
        body, name=name, grid=(s_len // tm,),
        in_specs=[pl.BlockSpec((tm, d), lambda i: (i, 0)), pl.BlockSpec((1, d), lambda i: (0, 0))],
        out_specs=(pl.BlockSpec((tm, d), lambda i: (i, 0)), pl.BlockSpec((tm, 1), lambda i: (i, 0))),
        out_shape=(jax.ShapeDtypeStruct((s_len, d), BF16), jax.ShapeDtypeStruct((s_len, 1), F32)),
        compiler_params=_cparams(1),
    )(x, gain)


def _rms_bwd(dh, x, r, gain, add, name):
    s_len, d = x.shape
    tm = 512

    def body(dh_ref, x_ref, r_ref, g_ref, add_ref, dx_ref, dg_ref):
        @pl.when(pl.program_id(0) == 0)
        def _():
            dg_ref[...] = jnp.zeros_like(dg_ref)
        rv = r_ref[...]
        xhat = x_ref[...] * rv
        dhv = dh_ref[...]
        dg_ref[...] += jnp.sum(dhv * xhat, axis=0, keepdims=True)
        dxhat = dhv * g_ref[...]
        dx_ref[...] = add_ref[...] + rv * (dxhat - xhat * jnp.mean(dxhat * xhat, axis=-1, keepdims=True))

    row = pl.BlockSpec((tm, d), lambda i: (i, 0))
    vec = pl.BlockSpec((1, d), lambda i: (0, 0))
    return pl.pallas_call(
        body, name=name, grid=(s_len // tm,),
        in_specs=[row, row, pl.BlockSpec((tm, 1), lambda i: (i, 0)), vec, row],
        out_specs=(row, vec),
        out_shape=(jax.ShapeDtypeStruct((s_len, d), F32), jax.ShapeDtypeStruct((1, d), F32)),
        compiler_params=_cparams(1),
    )(dh, x, r, gain, add)


def _mm_nn(a, b, name, bias=None, sigmoid=False, res=None, out_dtype=F32):
    m, k = a.shape
    n = b.shape[1]
    tm, tn = 512, _pick(n)
    has_bias, has_res = bias is not None, res is not None

    def body(*refs):
        a_ref, b_ref = refs[0], refs[1]
        o_ref = refs[-1]
        acc = jnp.dot(a_ref[...].astype(BF16), b_ref[...].astype(BF16), preferred_element_type=F32)
        pos = 2
        if has_bias:
            acc = acc + refs[pos][...]
            pos += 1
        if sigmoid:
            acc = _sigmoid(acc)
        if has_res:
            acc = refs[pos][...] + acc
        o_ref[...] = acc.astype(out_dtype)

    in_specs = [pl.BlockSpec((tm, k), lambda i, j: (i, 0)), pl.BlockSpec((k, tn), lambda i, j: (0, j))]
    args = [a, b]
    if has_bias:
        in_specs.append(pl.BlockSpec((1, tn), lambda i, j: (0, j)))
        args.append(bias)
    if has_res:
        in_specs.append(pl.BlockSpec((tm, tn), lambda i, j: (i, j)))
        args.append(res)
    return pl.pallas_call(
        body, name=name, grid=(m // tm, n // tn), in_specs=in_specs,
        out_specs=pl.BlockSpec((tm, tn), lambda i, j: (i, j)),
        out_shape=jax.ShapeDtypeStruct((m, n), out_dtype), compiler_params=_cparams(2),
    )(*args)


def _mm_nt(pairs, name):
    m = pairs[0][0].shape[0]
    n = pairs[0][1].shape[0]
    tm, tn = 512, _pick(n)
    n_pairs = len(pairs)

    def body(*refs):
        o_ref = refs[-1]
        acc = None
        for p in range(n_pairs):
            av = refs[2 * p][...].astype(BF16)
            bv = refs[2 * p + 1][...].astype(BF16)
            t = lax.dot_general(av, bv, (((1,), (1,)), ((), ())), preferred_element_type=F32)
            acc = t if acc is None else acc + t
        o_ref[...] = acc

    in_specs, args = [], []
    for a, b, cidx in pairs:
        k = a.shape[1]
        in_specs.append(pl.BlockSpec((tm, k), lambda i, j: (i, 0)))
        in_specs.append(pl.BlockSpec((tn, k), lambda i, j, cidx=cidx: (j, cidx)))
        args += [a, b]
    return pl.pallas_call(
        body, name=name, grid=(m // tm, n // tn), in_specs=in_specs,
        out_specs=pl.BlockSpec((tm, tn), lambda i, j: (i, j)),
        out_shape=jax.ShapeDtypeStruct((m, n), F32), compiler_params=_cparams(2),
    )(*args)


def _mm_tn(a, b, name):
    k, m = a.shape
    n = b.shape[1]
    tm, tn = _pick(m), _pick(n, (256, 128))

    def body(a_ref, b_ref, o_ref):
        o_ref[...] = lax.dot_general(a_ref[...].astype(BF16), b_ref[...].astype(BF16), (((0,), (0,)), ((), ())),
                                     preferred_element_type=F32)

    return pl.pallas_call(
        body, name=name, grid=(m // tm, n // tn),
        in_specs=[pl.BlockSpec((k, tm), lambda i, j: (0, i)), pl.BlockSpec((k, tn), lambda i, j: (0, j))],
        out_specs=pl.BlockSpec((tm, tn), lambda i, j: (i, j)),
        out_shape=jax.ShapeDtypeStruct((m, n), F32), compiler_params=_cparams(2),
    )(a, b)


def _norm_rope(t, gain, c, s1, s2, seg):
    rs = lax.rsqrt(_seg_sum(t * t, seg) * (1.0 / HEAD_DIM) + EPS)
    return _rope((t * rs) * gain, c, s1, s2)


def _dup_half(y, half):
    lane = lax.broadcasted_iota(jnp.int32, y.shape, 1)
    rolled = pltpu.roll(y, HEAD_DIM, 1)
    keep = (lane < HEAD_DIM) if half == 0 else (lane >= HEAD_DIM)
    return jnp.where(keep, y, rolled)


def _qk_prep(proj, cb0, d, gqa, gq, gk, tabs, name):
    s_len = proj.shape[0]
    tm = 512
    rows = tm // d
    n_units = 4 if gqa else 2 * d
    n_q = 4 if gqa else 2
    n_in = 6

    def body(*refs):
        in_refs = refs[:n_in]
        gq_ref, gk_ref, c_ref, s1_ref, s2_ref, o_ref = refs[n_in:]
        seg = _seg_matrix(HEAD_DIM)

        def rows_of(ref, r):
            return ref[...] if d == 1 else ref[pl.ds(r, rows, stride=d), :]

        def put(unit_col, y):
            o_ref[:, unit_col * LANES:(unit_col + 1) * LANES] = y.astype(BF16)

        for r in range(d):
            c, s1, s2 = rows_of(c_ref, r), rows_of(s1_ref, r), rows_of(s2_ref, r)
            for b in range(n_in):
                t = rows_of(in_refs[b], r)
                if b < n_q:
                    put((b * d + r) if not gqa else b, _norm_rope(t, gq_ref[...], c, s1, s2, seg))
                elif not gqa:
                    sec, pair = (1, b - 2) if b < 4 else (2, b - 4)
                    y = _norm_rope(t, gk_ref[...], c, s1, s2, seg) if sec == 1 else t
                    put(sec * n_units + pair * d + r, y)
                else:
                    sec = 1 if b == 4 else 2
                    y = _norm_rope(t, gk_ref[...], c, s1, s2, seg) if sec == 1 else t
                    for u in range(n_units):
                        put(sec * n_units + u, _dup_half(y, u // 2))

    in_specs = [pl.BlockSpec((tm, LANES), lambda i, b=b: (i, cb0 + b)) for b in range(n_in)]
    vec = pl.BlockSpec((1, LANES), lambda i: (0, 0))
    tab = pl.BlockSpec((tm, LANES), lambda i: (i, 0))
    width = 3 * n_units * LANES
    return pl.pallas_call(
        body, name=name, grid=(s_len // tm,), in_specs=in_specs + [vec, vec, tab, tab, tab],
        out_specs=pl.BlockSpec((rows, width), lambda i: (i, 0)),
        out_shape=jax.ShapeDtypeStruct((s_len // d, width), BF16), compiler_params=_cparams(1),
    )(*([proj] * n_in), gq, gk, *tabs)


def _qk_prep_bwd(dqkv, proj, cb0, d, gqa, gq, gk, tabs, name):
    s_len = proj.shape[0]
    tm = 512
    rows = tm // d
    n_units = 4 if gqa else 2 * d
    n_q = 4 if gqa else 2
    n_in = 6

    def body(*refs):
        d_refs = refs[0:3]
        in_refs = refs[3:3 + n_in]
        gq_ref, gk_ref, c_ref, s1_ref, s2_ref, o_ref, dgq_ref, dgk_ref, stage = refs[3 + n_in:]
        seg = _seg_matrix(HEAD_DIM)

        @pl.when(pl.program_id(0) == 0)
        def _():
            dgq_ref[...] = jnp.zeros_like(dgq_ref)
            dgk_ref[...] = jnp.zeros_like(dgk_ref)

        def rows_of(ref, r):
            return ref[...] if d == 1 else ref[pl.ds(r, rows, stride=d), :]

        def unit(col):
            sec, u = divmod(col, n_units)
            return d_refs[sec][:, u * LANES:(u + 1) * LANES]

        def norm_bwd(dyr, t, gain, c, s1, s2, dg_ref):
            rs = lax.rsqrt(_seg_sum(t * t, seg) * (1.0 / HEAD_DIM) + EPS)
            that = t * rs
            dy = _unrope(dyr, c, s1, s2)
            dg_ref[...] += jnp.sum(dy * that, axis=0, keepdims=True)
            dthat = dy * gain
            return rs * (dthat - that * (_seg_sum(dthat * that, seg) * (1.0 / HEAD_DIM)))

        def fold(sec):
            tot = []
            for u in range(n_units):
                v = unit(sec * n_units + u)
                tot.append(v + pltpu.roll(v, HEAD_DIM, 1))
            lane = lax.broadcasted_iota(jnp.int32, tot[0].shape, 1)
            return jnp.where(lane < HEAD_DIM, tot[0] + tot[1], tot[2] + tot[3])

        for b in range(n_in):
            for r in range(d):
                c, s1, s2 = rows_of(c_ref, r), rows_of(s1_ref, r), rows_of(s2_ref, r)
                t = rows_of(in_refs[b], r)
                if b < n_q:
                    g = unit((b * d + r) if not gqa else b)
                    out = norm_bwd(g, t, gq_ref[...], c, s1, s2, dgq_ref)
                elif not gqa:
                    sec, pair = (1, b - 2) if b < 4 else (2, b - 4)
                    g = unit(sec * n_units + pair * d + r)
                    out = norm_bwd(g, t, gk_ref[...], c, s1, s2, dgk_ref) if sec == 1 else g
                else:
                    sec = 1 if b == 4 else 2
                    g = fold(sec)
                    out = norm_bwd(g, t, gk_ref[...], c, s1, s2, dgk_ref) if sec == 1 else g
                if d == 1:
                    o_ref[:, b * LANES:(b + 1) * LANES] = out
                else:
                    stage[pl.ds(r, rows, stride=d), :] = out
            if d != 1:
                o_ref[:, b * LANES:(b + 1) * LANES] = stage[...]

    in_specs = [pl.BlockSpec((rows, n_units * LANES), lambda i: (i, 0))] * 3
    in_specs += [pl.BlockSpec((tm, LANES), lambda i, b=b: (i, cb0 + b)) for b in range(n_in)]
    vec = pl.BlockSpec((1, LANES), lambda i: (0, 0))
    tab = pl.BlockSpec((tm, LANES), lambda i: (i, 0))
    return pl.pallas_call(
        body, name=name, grid=(s_len // tm,), in_specs=in_specs + [vec, vec, tab, tab, tab],
        out_specs=(pl.BlockSpec((tm, n_in * LANES), lambda i: (i, 0)), vec, vec),
        out_shape=(jax.ShapeDtypeStruct((s_len, n_in * LANES), F32), jax.ShapeDtypeStruct((1, LANES), F32),
                   jax.ShapeDtypeStruct((1, LANES), F32)),
        scratch_shapes=[pltpu.VMEM((tm, LANES), F32)], compiler_params=_cparams(1),
    )(*dqkv, *([proj] * n_in), gq, gk, *tabs)


def _head_masks(shape):
    lane = lax.broadcasted_iota(jnp.int32, shape, 1)
    return lane < HEAD_DIM, lane >= HEAD_DIM


def _band_fwd(qkv, n_units, max_dist, sinks, name):
    n_rows = qkv.shape[0]
    nb = n_rows // BLOCK
    scale = HEAD_DIM ** -0.5
    has_sink = sinks is not None

    def body(*refs):
        q_ref, kp_ref, kc_ref, vp_ref, vc_ref = refs[:5]
        o_ref, lse_ref = refs[-2:]
        i = pl.program_id(1)
        q = q_ref[...]
        kk = jnp.concatenate([kp_ref[...], kc_ref[...]], axis=0)
        vv = jnp.concatenate([vp_ref[...], vc_ref[...]], axis=0)
        qi = lax.broadcasted_iota(jnp.int32, (BLOCK, 2 * BLOCK), 0)
        kj = lax.broadcasted_iota(jnp.int32, (BLOCK, 2 * BLOCK), 1)
        dist = qi + BLOCK - kj
        valid = (dist >= 0) & (dist <= max_dist) & ((i > 0) | (kj >= BLOCK))
        m0, m1 = _head_masks(q.shape)
        outs, lses = [], []
        for e, hm in enumerate((m0, m1)):
            qe = jnp.where(hm, q, jnp.zeros_like(q))
            s = lax.dot_general(qe, kk, (((1,), (1,)), ((), ())), preferred_element_type=F32) * scale
            s = jnp.where(valid, s, -jnp.inf)
            mx = jnp.max(s, axis=-1, keepdims=True)
            if has_sink:
                sk = refs[5][:, e * HEAD_DIM:e * HEAD_DIM + 1]
                mx = jnp.maximum(mx, sk)
            p = jnp.exp(s - mx)
            den = jnp.sum(p, axis=-1, keepdims=True)
            if has_sink:
                den = den + jnp.exp(sk - mx)
            pn = (p * (1.0 / den)).astype(BF16)
            outs.append(jnp.dot(pn, vv, preferred_element_type=F32))
            lses.append(mx + jnp.log(den))
        o_ref[...] = jnp.where(m0, outs[0], outs[1])
        lse_ref[...] = jnp.where(m0, jnp.broadcast_to(lses[0], (BLOCK, LANES)), jnp.broadcast_to(lses[1], (BLOCK, LANES)))

    def spec(sec, prev):
        if prev:
            return pl.BlockSpec((BLOCK, LANES), lambda u, i: (jnp.maximum(i - 1, 0), sec * n_units + u))
        return pl.BlockSpec((BLOCK, LANES), lambda u, i: (i, sec * n_units + u))

    in_specs = [spec(0, False), spec(1, True), spec(1, False), spec(2, True), spec(2, False)]
    args = [qkv] * 5
    if has_sink:
        in_specs.append(pl.BlockSpec((None, 1, LANES), lambda u, i: (u, 0, 0)))
        args.append(sinks)
    out = pl.BlockSpec((BLOCK, LANES), lambda u, i: (i, u))
    return pl.pallas_call(
        body, name=name, grid=(n_units, nb), in_specs=in_specs, out_specs=(out, out),
        out_shape=(jax.ShapeDtypeStruct((n_rows, n_units * LANES), F32),) * 2, compiler_params=_cparams(2),
    )(*args)


def _band_bwd(qkv, do, lse, delta, n_units, max_dist, name):
    n_rows = qkv.shape[0]
    nb = n_rows // BLOCK
    scale = HEAD_DIM ** -0.5

    def body(qc_ref, qn_ref, kp_ref, kc_ref, vp_ref, vc_ref, doc_ref, don_ref, lc_ref, ln_ref, dc_ref, dn_ref,
             dq_ref, dk_ref, dv_ref):
        i = pl.program_id(1)
        m0, m1 = _head_masks((BLOCK, LANES))
        zero = jnp.zeros((BLOCK, LANES), BF16)
        q = qc_ref[...]
        dob = doc_ref[...]
        kk = jnp.concatenate([kp_ref[...], kc_ref[...]], axis=0)
        vv = jnp.concatenate([vp_ref[...], vc_ref[...]], axis=0)
        qi = lax.broadcasted_iota(jnp.int32, (BLOCK, 2 * BLOCK), 0)
        kj = lax.broadcasted_iota(jnp.int32, (BLOCK, 2 * BLOCK), 1)
        dist = qi + BLOCK - kj
        valid = (dist >= 0) & (dist <= max_dist) & ((i > 0) | (kj >= BLOCK))
        dqs = []
        for e, hm in enumerate((m0, m1)):
            col = slice(e * HEAD_DIM, e * HEAD_DIM + 1)
            s = lax.dot_general(jnp.where(hm, q, zero), kk, (((1,), (1,)), ((), ())), preferred_element_type=F32) * scale
            p = jnp.where(valid, jnp.exp(s - lc_ref[:, col]), 0.0)
            dp = lax.dot_general(jnp.where(hm, dob, zero), vv, (((1,), (1,)), ((), ())), preferred_element_type=F32)
            ds = (p * (dp - dc_ref[:, col]) * scale).astype(BF16)
            dqs.append(jnp.dot(ds, kk, preferred_element_type=F32))
        dq_ref[...] = jnp.where(m0, dqs[0], dqs[1])
        qq = jnp.concatenate([q, qn_ref[...]], axis=0)
        dd = jnp.concatenate([dob, don_ref[...]], axis=0)
        ll = jnp.concatenate([lc_ref[...], ln_ref[...]], axis=0)
        de = jnp.concatenate([dc_ref[...], dn_ref[...]], axis=0)
        k = kc_ref[...]
        v = vc_ref[...]
        qr = lax.broadcasted_iota(jnp.int32, (2 * BLOCK, BLOCK), 0)
        kc = lax.broadcasted_iota(jnp.int32, (2 * BLOCK, BLOCK), 1)
        dist2 = qr - kc
        valid2 = (dist2 >= 0) & (dist2 <= max_dist) & ((qr < BLOCK) | (i < nb - 1))
        m0w, m1w = _head_masks((2 * BLOCK, LANES))
        zero2 = jnp.zeros((2 * BLOCK, LANES), BF16)
        dk = jnp.zeros((BLOCK, LANES), F32)
        dv = jnp.zeros((BLOCK, LANES), F32)
        for e, hm in enumerate((m0w, m1w)):
            col = slice(e * HEAD_DIM, e * HEAD_DIM + 1)
            qe = jnp.where(hm, qq, zero2)
            doe = jnp.where(hm, dd, zero2)
            s = lax.dot_general(qe, k, (((1,), (1,)), ((), ())), preferred_element_type=F32) * scale
            p = jnp.where(valid2, jnp.exp(s - ll[:, col]), 0.0)
            dp = lax.dot_general(doe, v, (((1,), (1,)), ((), ())), preferred_element_type=F32)
            ds = (p * (dp - de[:, col]) * scale).astype(BF16)
            dk = dk + lax.dot_general(ds, qe, (((0,), (0,)), ((), ())), preferred_element_type=F32)
            dv = dv + lax.dot_general(p.astype(BF16), doe, (((0,), (0,)), ((), ())), preferred_element_type=F32)
        dk_ref[...] = dk
        dv_ref[...] = dv

    def spec(sec, shift):
        if shift < 0:
            return pl.BlockSpec((BLOCK, LANES), lambda u, i: (jnp.maximum(i - 1, 0), sec * n_units + u))
        if shift > 0:
            return pl.BlockSpec((BLOCK, LANES), lambda u, i: (jnp.minimum(i + 1, nb - 1), sec * n_units + u))
        return pl.BlockSpec((BLOCK, LANES), lambda u, i: (i, sec * n_units + u))

    in_specs = [spec(0, 0), spec(0, 1), spec(1, -1), spec(1, 0), spec(2, -1), spec(2, 0),
                spec(0, 0), spec(0, 1), spec(0, 0), spec(0, 1), spec(0, 0), spec(0, 1)]
    args = [qkv] * 6 + [do, do, lse, lse, delta, delta]
    one = pl.BlockSpec((BLOCK, LANES), lambda u, i: (i, u))
    shp = jax.ShapeDtypeStruct((n_rows, n_units * LANES), F32)
    return pl.pallas_call(
        body, name=name, grid=(n_units, nb), in_specs=in_specs, out_specs=(one, one, one),
        out_shape=(shp, shp, shp), compiler_params=_cparams(2),
    )(*args)


def _merge_groups(os_, lses, dils, name):
    s_len = os_[0].shape[0] * dils[0]
    tm = 512

    def body(*refs):
        o_refs, l_refs = refs[0:3], refs[3:6]
        o_ref, lse_ref = refs[6:8]
        so, sl = refs[8:11], refs[11:14]
        for pair in range(2):
            for g, d in enumerate(dils):
                rows = tm // d
                for r in range(d):
                    col = slice((pair * d + r) * LANES, (pair * d + r + 1) * LANES)
                    if d == 1:
                        so[g][...] = o_refs[g][:, col]
                        sl[g][...] = l_refs[g][:, col]
                    else:
                        so[g][pl.ds(r, rows, stride=d), :] = o_refs[g][:, col]
                        sl[g][pl.ds(r, rows, stride=d), :] = l_refs[g][:, col]
            l0, l1, l2 = sl[0][...], sl[1][...], sl[2][...]
            mx = jnp.maximum(jnp.maximum(l0, l1), l2)
            e0, e1, e2 = jnp.exp(l0 - mx), jnp.exp(l1 - mx), jnp.exp(l2 - mx)
            den = e0 + e1 + e2
            inv = 1.0 / den
            o_ref[:, pair * LANES:(pair + 1) * LANES] = (so[0][...] * (e0 * inv) + so[1][...] * (e1 * inv)
                                                         + so[2][...] * (e2 * inv))
            lse_ref[:, pair * LANES:(pair + 1) * LANES] = mx + jnp.log(den)

    in_specs = [pl.BlockSpec((tm // d, 2 * d * LANES), lambda i: (i, 0)) for d in dils] * 2
    out = pl.BlockSpec((tm, 2 * LANES), lambda i: (i, 0))
    shp = jax.ShapeDtypeStruct((s_len, 2 * LANES), F32)
    return pl.pallas_call(
        body, name=name, grid=(s_len // tm,), in_specs=in_specs, out_specs=(out, out), out_shape=(shp, shp),
        scratch_shapes=[pltpu.VMEM((tm, LANES), F32)] * 6, compiler_params=_cparams(1),
    )(*os_, *lses)


def _bwd_prep(do, o, lse, dils, sinks, name):
    s_len, width = do.shape
    n_pairs = width // LANES
    tm = 512
    has_sink = sinks is not None
    n_g = len(dils)

    def body(*refs):
        do_ref, o_ref, lse_ref = refs[:3]
        pos = 3
        if has_sink:
            sink_ref = refs[pos]
            pos += 1
        outs = refs[pos:pos + 3 * n_g]
        pos += 3 * n_g
        if has_sink:
            dsink_ref = refs[pos]
            pos += 1
        s_do, s_l, s_d = refs[pos:pos + 3]
        seg = _seg_matrix(HEAD_DIM)

        if has_sink:
            @pl.when(pl.program_id(0) == 0)
            def _():
                dsink_ref[...] = jnp.zeros_like(dsink_ref)

        for pair in range(n_pairs):
            col = slice(pair * LANES, (pair + 1) * LANES)
            dov = do_ref[:, col]
            lv = lse_ref[:, col]
            delta = _seg_sum(dov * o_ref[:, col], seg)
            if has_sink:
                dsink_ref[pair] += -jnp.sum(jnp.exp(sink_ref[pair] - lv) * delta, axis=0, keepdims=True)
            s_do[...] = dov
            s_l[...] = lv
            s_d[...] = delta
            for g, d in enumerate(dils):
                rows = tm // d
                for r in range(d):
                    oc = slice((pair * d + r) * LANES, (pair * d + r + 1) * LANES)
                    if d == 1:
                        a, b, c = s_do[...], s_l[...], s_d[...]
                    else:
                        a = s_do[pl.ds(r, rows, stride=d), :]
                        b = s_l[pl.ds(r, rows, stride=d), :]
                        c = s_d[pl.ds(r, rows, stride=d), :]
                    outs[3 * g][:, oc] = a.astype(BF16)
                    outs[3 * g + 1][:, oc] = b
                    outs[3 * g + 2][:, oc] = c

    row = pl.BlockSpec((tm, width), lambda i: (i, 0))
    in_specs = [row, row, row]
    args = [do, o, lse]
    if has_sink:
        in_specs.append(pl.BlockSpec((n_pairs, 1, LANES), lambda i: (0, 0, 0)))
        args.append(sinks)
    out_specs, out_shape = [], []
    for d in dils:
        for dt in (BF16, F32, F32):
            out_specs.append(pl.BlockSpec((tm // d, n_pairs * d * LANES), lambda i: (i, 0)))
            out_shape.append(jax.ShapeDtypeStruct((s_len // d, n_pairs * d * LANES), dt))
    if has_sink:
        out_specs.append(pl.BlockSpec((n_pairs, 1, LANES), lambda i: (0, 0, 0)))
        out_shape.append(jax.ShapeDtypeStruct((n_pairs, 1, LANES), F32))
    return pl.pallas_call(
        body, name=name, grid=(s_len // tm,), in_specs=in_specs, out_specs=tuple(out_specs),
        out_shape=tuple(out_shape), scratch_shapes=[pltpu.VMEM((tm, LANES), F32)] * 3, compiler_params=_cparams(1),
    )(*args)


def _mem_kv(mem, mem_gain, w_kv, k_gain, name):
    m_len = mem.shape[0]
    kw = M_HEADS * M_HEAD_DIM

    def body(mem_ref, mg_ref, w_ref, kg_ref, k_ref, v_ref):
        mv = mem_ref[...]
        r = lax.rsqrt(jnp.mean(mv * mv, axis=-1, keepdims=True) + EPS)
        mn = ((mv * r) * mg_ref[...]).astype(BF16)
        kv = jnp.dot(mn, w_ref[...], preferred_element_type=F32)
        for h in range(M_HEADS):
            col = slice(h * M_HEAD_DIM, (h + 1) * M_HEAD_DIM)
            t = kv[:, col]
            rk = lax.rsqrt(jnp.mean(t * t, axis=-1, keepdims=True) + EPS)
            k_ref[:, col] = ((t * rk) * kg_ref[...]).astype(BF16)
        v_ref[...] = kv[:, kw:].astype(BF16)

    shp = jax.ShapeDtypeStruct((m_len, kw), BF16)
    return pl.pallas_call(body, name=name, out_shape=(shp, shp),
                          compiler_params=pltpu.CompilerParams(vmem_limit_bytes=VMEM_LIMIT_BYTES))(mem, mem_gain, w_kv, k_gain)


def _mem_kv_bwd(mem, mem_gain, w_kv, k_gain, dk, dv, name):
    m_len, d = mem.shape
    kw = M_HEADS * M_HEAD_DIM

    def body(mem_ref, mg_ref, w_ref, kg_ref, dk_ref, dv_ref, dw_ref, dmg_ref, dkg_ref, dkv_ref):
        mv = mem_ref[...]
        r = lax.rsqrt(jnp.mean(mv * mv, axis=-1, keepdims=True) + EPS)
        mhat = mv * r
        mn = (mhat * mg_ref[...]).astype(BF16)
        kv = jnp.dot(mn, w_ref[...], preferred_element_type=F32)
        dkg = jnp.zeros((1, M_HEAD_DIM), F32)
        for h in range(M_HEADS):
            col = slice(h * M_HEAD_DIM, (h + 1) * M_HEAD_DIM)
            t = kv[:, col]
            rk = lax.rsqrt(jnp.mean(t * t, axis=-1, keepdims=True) + EPS)
            that = t * rk
            dy = dk_ref[:, col]
            dkg = dkg + jnp.sum(dy * that, axis=0, keepdims=True)
            dthat = dy * kg_ref[...]
            dkv_ref[:, col] = (rk * (dthat - that * jnp.mean(dthat * that, axis=-1, keepdims=True))).astype(BF16)
        dkv_ref[:, kw:] = dv_ref[...].astype(BF16)
        dkg_ref[...] = dkg
        dkv = dkv_ref[...]
        dw_ref[...] = lax.dot_general(mn, dkv, (((0,), (0,)), ((), ())), preferred_element_type=F32)
        dmn = lax.dot_general(dkv, w_ref[...], (((1,), (1,)), ((), ())), preferred_element_type=F32)
        dmg_ref[...] = jnp.sum(dmn * mhat, axis=0, keepdims=True)

    return pl.pallas_call(
        body, name=name,
        out_shape=(jax.ShapeDtypeStruct((d, 2 * kw), F32), jax.ShapeDtypeStruct((1, d), F32),
                   jax.ShapeDtypeStruct((1, M_HEAD_DIM), F32)),
        scratch_shapes=[pltpu.VMEM((m_len, 2 * kw), BF16)],
        compiler_params=pltpu.CompilerParams(vmem_limit_bytes=VMEM_LIMIT_BYTES),
    )(mem, mem_gain, w_kv, k_gain, dk, dv)


def _mem_attn_fwd(proj, cidx, mk, mv, q_gain, name):
    s_len = proj.shape[0]
    kw = M_HEADS * M_HEAD_DIM
    tm = 512
    scale = M_HEAD_DIM ** -0.5

    def body(q_ref, k_ref, v_ref, g_ref, o_ref):
        for h in range(M_HEADS):
            col = slice(h * M_HEAD_DIM, (h + 1) * M_HEAD_DIM)
            t = q_ref[:, col]
            rs = lax.rsqrt(jnp.mean(t * t, axis=-1, keepdims=True) + EPS)
            qn = ((t * rs) * g_ref[...]).astype(BF16)
            s = lax.dot_general(qn, k_ref[:, col], (((1,), (1,)), ((), ())), preferred_element_type=F32) * scale
            mx = jnp.max(s, axis=-1, keepdims=True)
            p = jnp.exp(s - mx)
            pn = (p * (1.0 / jnp.sum(p, axis=-1, keepdims=True))).astype(BF16)
            o_ref[:, col] = jnp.dot(pn, v_ref[:, col], preferred_element_type=F32).astype(BF16)

    whole = pl.BlockSpec((MEM_LEN, kw), lambda i: (0, 0))
    return pl.pallas_call(
        body, name=name, grid=(s_len // tm,),
        in_specs=[pl.BlockSpec((tm, kw), lambda i: (i, cidx)), whole, whole, pl.BlockSpec((1, M_HEAD_DIM), lambda i: (0, 0))],
        out_specs=pl.BlockSpec((tm, kw), lambda i: (i, 0)),
        out_shape=jax.ShapeDtypeStruct((s_len, kw), BF16), compiler_params=_cparams(1),
    )(proj, mk, mv, q_gain)


def _mem_attn_bwd(proj, cidx, mk, mv, q_gain, do, name):
    s_len = proj.shape[0]
    kw = M_HEADS * M_HEAD_DIM
    tm = 512
    scale = M_HEAD_DIM ** -0.5

    def body(q_ref, k_ref, v_ref, g_ref, do_ref, dq_ref, dk_ref, dv_ref, dg_ref):
        @pl.when(pl.program_id(0) == 0)
        def _():
            dk_ref[...] = jnp.zeros_like(dk_ref)
            dv_ref[...] = jnp.zeros_like(dv_ref)
            dg_ref[...] = jnp.zeros_like(dg_ref)

        for h in range(M_HEADS):
            col = slice(h * M_HEAD_DIM, (h + 1) * M_HEAD_DIM)
            t = q_ref[:, col]
            rs = lax.rsqrt(jnp.mean(t * t, axis=-1, keepdims=True) + EPS)
            that = t * rs
            qn = (that * g_ref[...]).astype(BF16)
            kh, vh = k_ref[:, col], v_ref[:, col]
            dob = do_ref[:, col].astype(BF16)
            s = lax.dot_general(qn, kh, (((1,), (1,)), ((), ())), preferred_element_type=F32) * scale
            mx = jnp.max(s, axis=-1, keepdims=True)
            p = jnp.exp(s - mx)
            p = p * (1.0 / jnp.sum(p, axis=-1, keepdims=True))
            dp = lax.dot_general(dob, vh, (((1,), (1,)), ((), ())), preferred_element_type=F32)
            ds = (p * (dp - jnp.sum(p * dp, axis=-1, keepdims=True)) * scale).astype(BF16)
            dqn = jnp.dot(ds, kh, preferred_element_type=F32)
            dk_ref[:, col] += lax.dot_general(ds, qn, (((0,), (0,)), ((), ())), preferred_element_type=F32)
            dv_ref[:, col] += lax.dot_general(p.astype(BF16), dob, (((0,), (0,)), ((), ())), preferred_element_type=F32)
            dg_ref[...] += jnp.sum(dqn * that, axis=0, keepdims=True)
            dthat = dqn * g_ref[...]
            dq_ref[:, col] = rs * (dthat - that * jnp.mean(dthat * that, axis=-1, keepdims=True))

    whole = pl.BlockSpec((MEM_LEN, kw), lambda i: (0, 0))
    vec = pl.BlockSpec((1, M_HEAD_DIM), lambda i: (0, 0))
    row = pl.BlockSpec((tm, kw), lambda i: (i, 0))
    return pl.pallas_call(
        body, name=name, grid=(s_len // tm,),
        in_specs=[pl.BlockSpec((tm, kw), lambda i: (i, cidx)), whole, whole, vec, row],
        out_specs=(row, whole, whole, vec),
        out_shape=(jax.ShapeDtypeStruct((s_len, kw), F32), jax.ShapeDtypeStruct((MEM_LEN, kw), F32),
                   jax.ShapeDtypeStruct((MEM_LEN, kw), F32), jax.ShapeDtypeStruct((1, M_HEAD_DIM), F32)),
        compiler_params=_cparams(1),
    )(proj, mk, mv, q_gain, do)


def _gate_merge(gates, pa, pb, pm, name):
    s_len, d = pa.shape
    tm = 256

    def body(g_ref, a_ref, b_ref, m_ref, o_ref):
        o_ref[...] = (g_ref[:, 0:d] * a_ref[...] + g_ref[:, d:2 * d] * b_ref[...]
                      + g_ref[:, 2 * d:3 * d] * m_ref[...]).astype(BF16)

    row = pl.BlockSpec((tm, d), lambda i: (i, 0))
    return pl.pallas_call(
        body, name=name, grid=(s_len // tm,), in_specs=[pl.BlockSpec((tm, 3 * d), lambda i: (i, 0)), row, row, row],
        out_specs=row, out_shape=jax.ShapeDtypeStruct((s_len, d), BF16), compiler_params=_cparams(1),
    )(gates, pa, pb, pm)


def _gate_merge_bwd(dmerged, gates, pa, pb, pm, name):
    s_len, d = pa.shape
    tm = 256

    def body(dm_ref, g_ref, a_ref, b_ref, m_ref, da_ref, db_ref, dmm_ref, dg_ref, dbg_ref):
        @pl.when(pl.program_id(0) == 0)
        def _():
            dbg_ref[...] = jnp.zeros_like(dbg_ref)
        dm = dm_ref[...]
        for k, (p_ref, dp_ref) in enumerate(((a_ref, da_ref), (b_ref, db_ref), (m_ref, dmm_ref))):
            col = slice(k * d, (k + 1) * d)
            g = g_ref[:, col]
            dp_ref[...] = (dm * g).astype(BF16)
            dpre = (dm * p_ref[...]) * (g * (1.0 - g))
            dbg_ref[:, col] += jnp.sum(dpre, axis=0, keepdims=True)
            dg_ref[:, col] = dpre.astype(BF16)

    row = pl.BlockSpec((tm, d), lambda i: (i, 0))
    wide = pl.BlockSpec((tm, 3 * d), lambda i: (i, 0))
    shp = jax.ShapeDtypeStruct((s_len, d), BF16)
    return pl.pallas_call(
        body, name=name, grid=(s_len // tm,), in_specs=[row, wide, row, row, row],
        out_specs=(row, row, row, wide, pl.BlockSpec((1, 3 * d), lambda i: (0, 0))),
        out_shape=(shp, shp, shp, jax.ShapeDtypeStruct((s_len, 3 * d), BF16), jax.ShapeDtypeStruct((1, 3 * d), F32)),
        compiler_params=_cparams(1),
    )(dmerged, gates, pa, pb, pm)


CONV_CHUNK = 256


def _pick_row(tile, j):
    row = lax.broadcasted_iota(jnp.int32, tile.shape, 0)
    return jnp.sum(jnp.where(row == j, tile, jnp.zeros_like(tile)), axis=0, keepdims=True)


def _rows_before(ref, start, k):
    cur = ref[pl.ds(start, CONV_CHUNK), :]
    prev = ref[pl.ds(pl.multiple_of(jnp.maximum(start - 8, 0), 8), 8), :]
    prev = jnp.where(start > 0, prev, jnp.zeros_like(prev))
    rolled = pltpu.roll(cur, k, 0)
    row = lax.broadcasted_iota(jnp.int32, cur.shape, 0)
    for j in range(k):
        rolled = jnp.where(row == j, _pick_row(prev, 8 - k + j), rolled)
    return rolled


def _rows_after(ref, start, k):
    cur = ref[pl.ds(start, CONV_CHUNK), :]
    nxt = ref[pl.ds(pl.multiple_of(start + CONV_CHUNK, 8), 8), :]
    rolled = pltpu.roll(cur, CONV_CHUNK - k, 0)
    row = lax.broadcasted_iota(jnp.int32, cur.shape, 0)
    for j in range(k):
        rolled = jnp.where(row == CONV_CHUNK - k + j, _pick_row(nxt, j), rolled)
    return rolled


def _conv_pre(u_ref, w_ref, b_ref, start):
    u2 = _rows_before(u_ref, start, 2)
    u1 = _rows_before(u_ref, start, 1)
    u0 = u_ref[pl.ds(start, CONV_CHUNK), :]
    c = ((b_ref[...] + w_ref[0:1, :] * u2) + w_ref[1:2, :] * u1) + w_ref[2:3, :] * u0
    return c, (u2, u1, u0)


def _conv_glu(u, conv_w, conv_b, name):
    s_len = u.shape[0]
    nblk = D_FF // LANES

    def body(ua_ref, ug_ref, wa_ref, wg_ref, ba_ref, bg_ref, o_ref):
        def chunk(ci, carry):
            start = pl.multiple_of(ci * CONV_CHUNK, CONV_CHUNK)
            ca, _ = _conv_pre(ua_ref, wa_ref, ba_ref, start)
            cg, _ = _conv_pre(ug_ref, wg_ref, bg_ref, start)
            o_ref[pl.ds(start, CONV_CHUNK), :] = ((ca * _sigmoid(ca)) * cg).astype(BF16)
            return carry
        lax.fori_loop(0, s_len // CONV_CHUNK, chunk, 0)

    def col(rows, off):
        return pl.BlockSpec((rows, LANES), lambda j: (0, off + j))

    return pl.pallas_call(
        body, name=name, grid=(nblk,),
        in_specs=[col(s_len, 0), col(s_len, nblk), col(3, 0), col(3, nblk), col(1, 0), col(1, nblk)],
        out_specs=col(s_len, 0), out_shape=jax.ShapeDtypeStruct((s_len, D_FF), BF16), compiler_params=_cparams(1),
    )(u, u, conv_w, conv_w, conv_b, conv_b)


def _conv_glu_bwd(dact, u, conv_w, conv_b, name):
    s_len = u.shape[0]
    nblk = D_FF // LANES
    n_chunks = s_len // CONV_CHUNK

    def body(da_ref, ua_ref, ug_ref, wa_ref, wg_ref, ba_ref, bg_ref,
             dua_ref, dug_ref, dwa_ref, dwg_ref, dba_ref, dbg_ref, sa, sg):
        sa[pl.ds(s_len, 8), :] = jnp.zeros((8, LANES), F32)
        sg[pl.ds(s_len, 8), :] = jnp.zeros((8, LANES), F32)
        zero = jnp.zeros((1, LANES), F32)

        def chunk1(ci, carry):
            start = pl.multiple_of(ci * CONV_CHUNK, CONV_CHUNK)
            ca, ua = _conv_pre(ua_ref, wa_ref, ba_ref, start)
            cg, ug = _conv_pre(ug_ref, wg_ref, bg_ref, start)
            dact_v = da_ref[pl.ds(start, CONV_CHUNK), :]
            sig = _sigmoid(ca)
            dcg = dact_v * (ca * sig)
            dca = (dact_v * cg) * (sig * (1.0 + ca * (1.0 - sig)))
            sa[pl.ds(start, CONV_CHUNK), :] = dca
            sg[pl.ds(start, CONV_CHUNK), :] = dcg
            out = [carry[0] + jnp.sum(dca, axis=0, keepdims=True), carry[1] + jnp.sum(dcg, axis=0, keepdims=True)]
            for j in range(3):
                out.append(carry[2 + j] + jnp.sum(dca * ua[j], axis=0, keepdims=True))
            for j in range(3):
                out.append(carry[5 + j] + jnp.sum(dcg * ug[j], axis=0, keepdims=True))
            return tuple(out)

        acc = lax.fori_loop(0, n_chunks, chunk1, (zero,) * 8)
        dba_ref[...] = acc[0]
        dbg_ref[...] = acc[1]
        for j in range(3):
            dwa_ref[j:j + 1, :] = acc[2 + j]
            dwg_ref[j:j + 1, :] = acc[5 + j]

        def chunk2(ci, carry):
            start = pl.multiple_of(ci * CONV_CHUNK, CONV_CHUNK)
            for s_ref, w_ref, o_ref in ((sa, wa_ref, dua_ref), (sg, wg_ref, dug_ref)):
                d0 = s_ref[pl.ds(start, CONV_CHUNK), :]
                d1 = _rows_after(s_ref, start, 1)
                d2 = _rows_after(s_ref, start, 2)
                o_ref[pl.ds(start, CONV_CHUNK), :] = (w_ref[2:3, :] * d0 + w_ref[1:2, :] * d1
                                                      + w_ref[0:1, :] * d2).astype(BF16)
            return carry
        lax.fori_loop(0, n_chunks, chunk2, 0)

    def col(rows, off):
        return pl.BlockSpec((rows, LANES), lambda j: (0, off + j))

    big = jax.ShapeDtypeStruct((s_len, D_FF), BF16)
    return pl.pallas_call(
        body, name=name, grid=(nblk,),
        in_specs=[col(s_len, 0), col(s_len, 0), col(s_len, nblk), col(3, 0), col(3, nblk), col(1, 0), col(1, nblk)],
        out_specs=(col(s_len, 0), col(s_len, 0), col(3, 0), col(3, 0), col(1, 0), col(1, 0)),
        out_shape=(big, big, jax.ShapeDtypeStruct((3, D_FF), F32), jax.ShapeDtypeStruct((3, D_FF), F32),
                   jax.ShapeDtypeStruct((1, D_FF), F32), jax.ShapeDtypeStruct((1, D_FF), F32)),
        scratch_shapes=[pltpu.VMEM((s_len + 8, LANES), F32)] * 2, compiler_params=_cparams(1),
    )(dact, u, u, conv_w, conv_w, conv_b, conv_b)


def _loss_head(y, target, name):
    s_len, d = y.shape
    tm = 512

    def body(y_ref, t_ref, dy_ref, l_ref):
        @pl.when(pl.program_id(0) == 0)
        def _():
            l_ref[...] = jnp.zeros_like(l_ref)
        err = y_ref[...] - t_ref[...]
        dy_ref[...] = err * (1.0 / d)
        part = 0.5 * jnp.sum(jnp.mean(err * err, axis=-1, keepdims=True), axis=0, keepdims=True)
        l_ref[...] += jnp.broadcast_to(part, l_ref.shape)

    row = pl.BlockSpec((tm, d), lambda i: (i, 0))
    return pl.pallas_call(
        body, name=name, grid=(s_len // tm,), in_specs=[row, row],
        out_specs=(row, pl.BlockSpec((8, LANES), lambda i: (0, 0))),
        out_shape=(jax.ShapeDtypeStruct((s_len, d), F32), jax.ShapeDtypeStruct((8, LANES), F32)),
        compiler_params=_cparams(1),
    )(y, target)


def _rope_tables(positions):
    half = ROPE_DIMS // 2
    freqs = jnp.exp(jnp.arange(half, dtype=F32) * (-2.0 * math.log(ROPE_THETA) / ROPE_DIMS))
    ang = positions.reshape(-1).astype(F32)[:, None] * freqs
    cos, sin = jnp.cos(ang), jnp.sin(ang)
    n = ang.shape[0]
    zeros = lambda w: jnp.zeros((n, w), F32)
    c = jnp.concatenate([cos, cos, jnp.ones((n, HEAD_DIM - ROPE_DIMS), F32)], axis=1)
    s1 = jnp.concatenate([-sin, zeros(HEAD_DIM - half)], axis=1)
    s2 = jnp.concatenate([zeros(half), sin, zeros(HEAD_DIM - ROPE_DIMS)], axis=1)
    return tuple(jnp.tile(t, (1, 2)) for t in (c, s1, s2))


def _two(v):
    return jnp.tile(v.reshape(1, HEAD_DIM), (1, 2))


def _fold_heads(g):
    return g[0, :HEAD_DIM] + g[0, HEAD_DIM:]


def _device_step(x, mem, positions, target, w):
    tabs = _rope_tables(positions)
    dils = tuple(d for _, d in A_GROUPS)
    grads = {}

    h, r1 = _rms_fwd(x, w['attn_norm'], "rms1")
    proj = _mm_nn(h, w['w_in'], "mm_in")
    gates = _mm_nn(h, w['w_gate'], "mm_gate", bias=w['b_gate'], sigmoid=True)

    qkv_a, o_g, lse_g = [], [], []
    for gi, (window, d) in enumerate(A_GROUPS):
        gq, gk = _two(w['a_q_norm'][gi]), _two(w['a_k_norm'][gi])
        qkv = _qk_prep(proj, 6 * gi, d, False, gq, gk, tabs, f"qk_prep_a{gi}")
        o, lse = _band_fwd(qkv, 2 * d, window // d, None, f"band_fwd_a{gi}")
        qkv_a.append(qkv)
        o_g.append(o)
        lse_g.append(lse)
    o_a, lse_a = _merge_groups(o_g, lse_g, dils, "merge_a")

    gbq, gbk = _two(w['b_q_norm']), _two(w['b_k_norm'])
    sinks = jnp.repeat(w['b_sinks'].reshape(4, 2), HEAD_DIM, axis=1).reshape(4, 1, LANES)
    qkv_b = _qk_prep(proj, 18, 1, True, gbq, gbk, tabs, "qk_prep_b")
    o_b, lse_b = _band_fwd(qkv_b, 4, B_WINDOW - 1, sinks, "band_fwd_b")

    mk, mv = _mem_kv(mem, w['mem_norm'], w['w_mem_kv'], w['m_k_norm'], "mem_kv")
    o_m = _mem_attn_fwd(proj, 6, mk, mv, w['m_q_norm'], "mem_attn")

    pa = _mm_nn(o_a, w['w_o_a'], "mm_oa")
    pb = _mm_nn(o_b, w['w_o_b'], "mm_ob")
    pm = _mm_nn(o_m, w['w_o_m'], "mm_om")
    merged = _gate_merge(gates, pa, pb, pm, "gate_merge")
    x1 = _mm_nn(merged, w['w_out'], "mm_out", res=x)

    h2, r2 = _rms_fwd(x1, w['ffn_norm'], "rms2")
    u = _mm_nn(h2, w['w_up'], "mm_up")
    act = _conv_glu(u, w['conv_w'], w['conv_b'], "conv_glu")
    y = _mm_nn(act, w['w_down'], "mm_down", res=x1)
    dy, loss = _loss_head(y, target, "loss_head")

    dact = _mm_nt([(dy, w['w_down'], 0)], "mm_d_act")
    grads['w_down'] = _mm_tn(act, dy, "mm_dw_down")
    du_a, du_g, dcw_a, dcw_g, dcb_a, dcb_g = _conv_glu_bwd(dact, u, w['conv_w'], w['conv_b'], "conv_glu_bwd")
    grads['conv_w'] = jnp.concatenate([dcw_a, dcw_g], axis=1)
    grads['conv_b'] = jnp.concatenate([dcb_a, dcb_g], axis=1)
    dh2 = _mm_nt([(du_a, w['w_up'], 0), (du_g, w['w_up'], 1)], "mm_d_h2")
    grads['w_up'] = jnp.concatenate([_mm_tn(h2, du_a, "mm_dw_up_a"), _mm_tn(h2, du_g, "mm_dw_up_g")], axis=1)
    dx1, grads['ffn_norm'] = _rms_bwd(dh2, x1, r2, w['ffn_norm'], dy, "rms2_bwd")

    dmerged = _mm_nt([(dx1, w['w_out'], 0)], "mm_d_merged")
    grads['w_out'] = _mm_tn(merged, dx1, "mm_dw_out")
    dpa, dpb, dpm, dgpre, grads['b_gate'] = _gate_merge_bwd(dmerged, gates, pa, pb, pm, "gate_merge_bwd")
    do_a = _mm_nt([(dpa, w['w_o_a'], 0)], "mm_d_oa")
    do_b = _mm_nt([(dpb, w['w_o_b'], 0)], "mm_d_ob")
    do_m = _mm_nt([(dpm, w['w_o_m'], 0)], "mm_d_om")
    grads['w_o_a'] = _mm_tn(o_a, dpa, "mm_dw_oa")
    grads['w_o_b'] = _mm_tn(o_b, dpb, "mm_dw_ob")
    grads['w_o_m'] = _mm_tn(o_m, dpm, "mm_dw_om")

    prep = _bwd_prep(do_a, o_a, lse_a, dils, None, "bwd_prep_a")
    dproj, dgq_a, dgk_a = [], [], []
    for gi, (window, d) in enumerate(A_GROUPS):
        gq, gk = _two(w['a_q_norm'][gi]), _two(w['a_k_norm'][gi])
        dqkv = _band_bwd(qkv_a[gi], prep[3 * gi], prep[3 * gi + 1], prep[3 * gi + 2], 2 * d, window // d,
                         f"band_bwd_a{gi}")
        dp, dgq, dgk = _qk_prep_bwd(dqkv, proj, 6 * gi, d, False, gq, gk, tabs, f"qk_prep_bwd_a{gi}")
        dproj.append(dp)
        dgq_a.append(_fold_heads(dgq))
        dgk_a.append(_fold_heads(dgk))
    grads['a_q_norm'] = jnp.stack(dgq_a)
    grads['a_k_norm'] = jnp.stack(dgk_a)

    do_bu, lse_bu, delta_bu, dsink = _bwd_prep(do_b, o_b, lse_b, (1,), sinks, "bwd_prep_b")
    dqkv = _band_bwd(qkv_b, do_bu, lse_bu, delta_bu, 4, B_WINDOW - 1, "band_bwd_b")
    dp_b, dgq, dgk = _qk_prep_bwd(dqkv, proj, 18, 1, True, gbq, gbk, tabs, "qk_prep_bwd_b")
    dproj.append(dp_b)
    grads['b_q_norm'] = _fold_heads(dgq)
    grads['b_k_norm'] = _fold_heads(dgk)
    grads['b_sinks'] = jnp.stack([dsink[:, 0, 0], dsink[:, 0, HEAD_DIM]], axis=1).reshape(8)

    dq_m, dmk, dmv, grads['m_q_norm'] = _mem_attn_bwd(proj, 6, mk, mv, w['m_q_norm'], do_m, "mem_attn_bwd")
    dproj.append(dq_m)
    grads['w_mem_kv'], grads['mem_norm'], grads['m_k_norm'] = _mem_kv_bwd(
        mem, w['mem_norm'], w['w_mem_kv'], w['m_k_norm'], dmk, dmv, "mem_kv_bwd")

    cols = (0, 1, 2, 3, 6)
    grads['w_in'] = jnp.concatenate([_mm_tn(h, dp, f"mm_dw_in{k}") for k, dp in enumerate(dproj)], axis=1)
    grads['w_gate'] = _mm_tn(h, dgpre, "mm_dw_gate")
    dh = _mm_nt([(dp, w['w_in'], c) for dp, c in zip(dproj, cols)] + [(dgpre, w['w_gate'], 0)], "mm_d_h")
    grad_x, grads['attn_norm'] = _rms_bwd(dh, x, r1, w['attn_norm'], dx1, "rms1_bwd")
    return loss, grad_x, grads


def _coords():
    return lax.axis_index("x"), lax.axis_index("y"), lax.axis_index("c")


def _slot(p):
    return 4 * p[0] + 2 * p[1] + p[2]


def _peers(me):
    x, y, c = me
    out = []
    for mask in range(1, N_DEV):
        out.append((1 - x if mask & 4 else x, 1 - y if mask & 2 else y, 1 - c if mask & 1 else c))
    return out


HBM_SPEC = pl.BlockSpec(memory_space=pltpu.HBM)


def _all_gather(shards, name):
    n = len(shards)

    def body(*refs):
        ins, outs = refs[:n], refs[n:2 * n]
        send_sems, recv_sems, local_sems = refs[2 * n:]
        x, y, c = _coords()
        me, sibling = (x, y, c), (x, y, 1 - c)
        chips = [(1 - x, y), (x, 1 - y), (1 - x, 1 - y)]

        def copy(a, k, block, to, src=None):
            dst = outs[a].at[_slot(block)]
            return pltpu.make_async_remote_copy(
                src_ref=dst if src is None else src, dst_ref=dst, send_sem=send_sems.at[a, k],
                recv_sem=recv_sems.at[a, k], device_id=to, device_id_type=MESH)

        mine = [pltpu.make_async_copy(ins[a], outs[a].at[_slot(me)], local_sems.at[a]) for a in range(n)]
        for cp in mine:
            cp.start()
        first = []
        for a in range(n):
            first.append(copy(a, 0, me, sibling, src=ins[a]))
            first += [copy(a, 1 + j, me, (*chip, c), src=ins[a]) for j, chip in enumerate(chips)]
        for cp in first:
            cp.start()
        passed = []
        for a in range(n):
            for j, chip in enumerate(chips):
                copy(a, 1 + j, (*chip, c), me).wait_recv()
                fwd = copy(a, 4 + j, (*chip, c), sibling)
                fwd.start()
                passed.append(fwd)
        for a in range(n):
            copy(a, 0, sibling, me).wait_recv()
            for j, chip in enumerate(chips):
                copy(a, 4 + j, (*chip, 1 - c), me).wait_recv()
        for cp in first + passed:
            cp.wait_send()
        for cp in mine:
            cp.wait()

    return pl.pallas_call(
        body, name=name, in_specs=[HBM_SPEC] * n, out_specs=tuple([HBM_SPEC] * n),
        out_shape=tuple(jax.ShapeDtypeStruct((N_DEV,) + s.shape, s.dtype) for s in shards),
        scratch_shapes=[pltpu.SemaphoreType.DMA((n, 7)), pltpu.SemaphoreType.DMA((n, 7)), pltpu.SemaphoreType.DMA((n,))],
    )(*shards)


def _exchange(blocks, name):
    n = len(blocks)

    def body(*refs):
        ins, outs = refs[:n], refs[n:2 * n]
        send_sems, recv_sems, local_sems = refs[2 * n:]
        me = _coords()
        peers = _peers(me)
        mine = [pltpu.make_async_copy(ins[a].at[_slot(me)], outs[a].at[_slot(me)], local_sems.at[a]) for a in range(n)]
        for cp in mine:
            cp.start()

        def copy(a, k):
            return pltpu.make_async_remote_copy(
                src_ref=ins[a].at[_slot(peers[k])], dst_ref=outs[a].at[_slot(me)], send_sem=send_sems.at[a, k],
                recv_sem=recv_sems.at[a, k], device_id=peers[k], device_id_type=MESH)

        def arrival(a, k):
            return pltpu.make_async_remote_copy(
                src_ref=ins[a].at[_slot(me)], dst_ref=outs[a].at[_slot(peers[k])], send_sem=send_sems.at[a, k],
                recv_sem=recv_sems.at[a, k], device_id=peers[k], device_id_type=MESH)

        sends = [copy(a, k) for a in range(n) for k in range(N_DEV - 1)]
        for cp in sends:
            cp.start()
        for a in range(n):
            for k in range(N_DEV - 1):
                arrival(a, k).wait_recv()
        for cp in sends:
            cp.wait_send()
        for cp in mine:
            cp.wait()

    return pl.pallas_call(
        body, name=name, in_specs=[HBM_SPEC] * n, out_specs=tuple([HBM_SPEC] * n),
        out_shape=tuple(jax.ShapeDtypeStruct(b.shape, b.dtype) for b in blocks),
        scratch_shapes=[pltpu.SemaphoreType.DMA((n, 7)), pltpu.SemaphoreType.DMA((n, 7)), pltpu.SemaphoreType.DMA((n,))],
    )(*blocks)


def _all_sum(p, name):
    def body(p_ref, o_ref, recv, send_sems, recv_sems):
        me = _coords()
        peers = _peers(me)
        recv[_slot(me)] = p_ref[...]

        def copy(k, landing):
            return pltpu.make_async_remote_copy(
                src_ref=p_ref, dst_ref=recv.at[_slot(landing)], send_sem=send_sems.at[k], recv_sem=recv_sems.at[k],
                device_id=peers[k], device_id_type=MESH)

        sends = [copy(k, me) for k in range(N_DEV - 1)]
        for cp in sends:
            cp.start()
        for k in range(N_DEV - 1):
            copy(k, peers[k]).wait_recv()
        for cp in sends:
            cp.wait_send()
        acc = recv[0]
        for s in range(1, N_DEV):
            acc = acc + recv[s]
        o_ref[...] = acc

    vmem = pl.BlockSpec(memory_space=pltpu.VMEM)
    return pl.pallas_call(
        body, name=name, in_specs=[vmem], out_specs=vmem, out_shape=jax.ShapeDtypeStruct(p.shape, F32),
        scratch_shapes=[pltpu.VMEM((N_DEV,) + p.shape, F32), pltpu.SemaphoreType.DMA((N_DEV - 1,)),
                        pltpu.SemaphoreType.DMA((N_DEV - 1,))],
    )(p)


def _adam(w, g, m, v):
    m2 = ADAM_B1 * m + (1.0 - ADAM_B1) * g
    v2 = ADAM_B2 * v + (1.0 - ADAM_B2) * (g * g)
    m_hat = m2 / (1.0 - ADAM_B1 ** ADAM_STEP)
    v_hat = v2 / (1.0 - ADAM_B2 ** ADAM_STEP)
    delta = -ADAM_LR * (m_hat / (jnp.sqrt(v_hat) + ADAM_EPS) + ADAM_WD * w)
    return delta, m2, v2


def _row_tile(rows, cols):
    best = rows
    for t in range(16, rows, 16):
        if rows % t == 0 and t * cols * 4 <= (1 << 20):
            best = t
    return best


def _adam_reduce(parts, w, m, v, name):
    rows, cols = w.shape
    tr = _row_tile(rows, cols)

    def body(p_ref, w_ref, m_ref, v_ref, g_ref, d_ref, m2_ref, v2_ref):
        g = p_ref[0].astype(F32)
        for s in range(1, N_DEV):
            g = g + p_ref[s].astype(F32)
        g_ref[...] = g
        d_ref[...], m2_ref[...], v2_ref[...] = _adam(w_ref[...], g, m_ref[...], v_ref[...])

    blk = pl.BlockSpec((tr, cols), lambda i: (i, 0))
    shp = jax.ShapeDtypeStruct((rows, cols), F32)
    return pl.pallas_call(
        body, name=name, grid=(rows // tr,),
        in_specs=[pl.BlockSpec((N_DEV, tr, cols), lambda i: (0, i, 0)), blk, blk, blk],
        out_specs=(blk,) * 4, out_shape=(shp,) * 4, compiler_params=_cparams(1),
    )(parts, w, m, v)


PACK_COLS = 1024
PACK = {'attn_norm': (0, 1, 1024), 'mem_norm': (1, 1, 1024), 'ffn_norm': (2, 1, 1024), 'b_gate': (3, 3, 1024),
        'conv_b': (6, 6, 1024), 'a_q_norm': (12, 3, 64), 'a_k_norm': (15, 3, 64), 'b_q_norm': (18, 1, 64),
        'b_k_norm': (19, 1, 64), 'm_q_norm': (20, 1, 128), 'm_k_norm': (21, 1, 128), 'b_sinks': (22, 1, 8)}
PACK_LOSS_ROW = 23
PACK_ROWS = 24


def _pack_small(grads, loss_tile):
    rows = []
    for name, (r0, nr, lanes) in PACK.items():
        g = grads[name].reshape(-1)
        g = jnp.pad(g, (0, nr * PACK_COLS - g.shape[0])) if lanes == PACK_COLS else \
            jnp.pad(g.reshape(nr, lanes), ((0, 0), (0, PACK_COLS - lanes))).reshape(-1)
        rows.append(g.reshape(nr, PACK_COLS))
    rows.append(jnp.pad(loss_tile[0:1, 0:1], ((0, 0), (0, PACK_COLS - 1))))
    return jnp.concatenate(rows, axis=0)


def _adam_small(gsum, ws, ms, vs, name):
    names = list(PACK)
    n = len(names)

    def body(*refs):
        g_ref = refs[0]
        w_refs, m_refs, v_refs = refs[1:1 + n], refs[1 + n:1 + 2 * n], refs[1 + 2 * n:1 + 3 * n]
        outs = refs[1 + 3 * n:]
        outs[0][...] = g_ref[PACK_LOSS_ROW:PACK_LOSS_ROW + 1, 0:1]
        for k, nm in enumerate(names):
            r0, nr, lanes = PACK[nm]
            o_g, o_d, o_m, o_v = outs[1 + 4 * k:5 + 4 * k]
            for j in range(nr):
                if lanes == PACK_COLS:
                    width = min(PACK_COLS, w_refs[k].shape[1] - j * PACK_COLS)
                    src = (slice(0, 1), slice(j * PACK_COLS, j * PACK_COLS + width))
                else:
                    width = lanes
                    src = (slice(j, j + 1), slice(0, lanes))
                g = g_ref[r0 + j:r0 + j + 1, 0:width]
                d, m2, v2 = _adam(w_refs[k][src], g, m_refs[k][src], v_refs[k][src])
                o_g[src] = g
                o_d[src] = d
                o_m[src] = m2
                o_v[src] = v2

    vmem = pl.BlockSpec(memory_space=pltpu.VMEM)
    shapes = [jax.ShapeDtypeStruct((1, 1), F32)]
    for nm in names:
        shapes += [jax.ShapeDtypeStruct(ws[nm].shape, F32)] * 4
    args = [gsum] + [ws[nm] for nm in names] + [ms[nm] for nm in names] + [vs[nm] for nm in names]
    return pl.pallas_call(
        body, name=name, in_specs=[vmem] * len(args), out_specs=tuple([vmem] * len(shapes)), out_shape=tuple(shapes),
    )(*args)


def _as2d(name, a):
    return a.reshape(a.shape[-2], a.shape[-1]) if a.ndim == 3 else a


def kernel(x, mem, positions, attn_norm, w_in, a_q_norm, a_k_norm, b_q_norm, b_k_norm, b_sinks, mem_norm, w_mem_kv, m_q_norm, m_k_norm, w_o_a, w_o_b, w_o_m, w_gate, b_gate, w_out, ffn_norm, w_up, conv_w, conv_b, w_down, loss_target, m_attn_norm, m_w_in, m_a_q_norm, m_a_k_norm, m_b_q_norm, m_b_k_norm, m_b_sinks, m_mem_norm, m_w_mem_kv, m_m_q_norm, m_m_k_norm, m_w_o_a, m_w_o_b, m_w_o_m, m_w_gate, m_b_gate, m_w_out, m_ffn_norm, m_w_up, m_conv_w, m_conv_b, m_w_down, v_attn_norm, v_w_in, v_a_q_norm, v_a_k_norm, v_b_q_norm, v_b_k_norm, v_b_sinks, v_mem_norm, v_w_mem_kv, v_m_q_norm, v_m_k_norm, v_w_o_a, v_w_o_b, v_w_o_m, v_w_gate, v_b_gate, v_w_out, v_ffn_norm, v_w_up, v_conv_w, v_conv_b, v_w_down):
    given = dict(attn_norm=attn_norm, w_in=w_in, a_q_norm=a_q_norm, a_k_norm=a_k_norm, b_q_norm=b_q_norm, b_k_norm=b_k_norm, b_sinks=b_sinks, mem_norm=mem_norm, w_mem_kv=w_mem_kv, m_q_norm=m_q_norm, m_k_norm=m_k_norm, w_o_a=w_o_a, w_o_b=w_o_b, w_o_m=w_o_m, w_gate=w_gate, b_gate=b_gate, w_out=w_out, ffn_norm=ffn_norm, w_up=w_up, conv_w=conv_w, conv_b=conv_b, w_down=w_down)
    mom1 = dict(attn_norm=m_attn_norm, w_in=m_w_in, a_q_norm=m_a_q_norm, a_k_norm=m_a_k_norm, b_q_norm=m_b_q_norm, b_k_norm=m_b_k_norm, b_sinks=m_b_sinks, mem_norm=m_mem_norm, w_mem_kv=m_w_mem_kv, m_q_norm=m_m_q_norm, m_k_norm=m_m_k_norm, w_o_a=m_w_o_a, w_o_b=m_w_o_b, w_o_m=m_w_o_m, w_gate=m_w_gate, b_gate=m_b_gate, w_out=m_w_out, ffn_norm=m_ffn_norm, w_up=m_w_up, conv_w=m_conv_w, conv_b=m_conv_b, w_down=m_w_down)
    mom2 = dict(attn_norm=v_attn_norm, w_in=v_w_in, a_q_norm=v_a_q_norm, a_k_norm=v_a_k_norm, b_q_norm=v_b_q_norm, b_k_norm=v_b_k_norm, b_sinks=v_b_sinks, mem_norm=v_mem_norm, w_mem_kv=v_w_mem_kv, m_q_norm=v_m_q_norm, m_k_norm=v_m_k_norm, w_o_a=v_w_o_a, w_o_b=v_w_o_b, w_o_m=v_w_o_m, w_gate=v_w_gate, b_gate=v_b_gate, w_out=v_w_out, ffn_norm=v_ffn_norm, w_up=v_w_up, conv_w=v_conv_w, conv_b=v_conv_b, w_down=v_w_down)

    big = list(BIG)
    shards = [given[n][0] if n == 'conv_w' else given[n][0].astype(BF16) for n in big]
    gathered = _all_gather(shards, "gather_weights")
    w = {}
    for n, g in zip(big, gathered):
        _, r, c = g.shape
        w[n] = g.reshape(N_DEV * r, c) if BIG[n] == 0 else g.transpose(1, 0, 2).reshape(r, N_DEV * c)
    for n in SMALL:
        w[n] = given[n]
    w['a_q_norm'], w['a_k_norm'] = given['a_q_norm'][0], given['a_k_norm'][0]
    w['b_q_norm'], w['b_k_norm'], w['b_sinks'] = given['b_q_norm'][0], given['b_k_norm'][0], given['b_sinks'][0]

    loss_tile, grad_x, grads = _device_step(x[0], mem[0], positions[0], loss_target[0], w)

    blocks = []
    for n in big:
        g = grads[n]
        r, c = given[n].shape[1:]
        g = g.reshape(N_DEV, r, c) if BIG[n] == 0 else g.reshape(r, N_DEV, c).transpose(1, 0, 2)
        blocks.append(g if n == 'conv_w' else g.astype(BF16))
    parts = _exchange(blocks, "exchange_grads")
    out = {}
    for n, p in zip(big, parts):
        res = _adam_reduce(p, given[n][0], mom1[n][0], mom2[n][0], f"adam_{n}")
        out[n] = tuple(t[None] for t in res)

    gsum = _all_sum(_pack_small(grads, loss_tile), "sum_small")
    ws = {n: _as2d(n, given[n]) for n in PACK}
    ms = {n: _as2d(n, mom1[n]) for n in PACK}
    vs = {n: _as2d(n, mom2[n]) for n in PACK}
    res = _adam_small(gsum, ws, ms, vs, "adam_small")
    loss = res[0].reshape(())
    for k, n in enumerate(PACK):
        out[n] = tuple(t.reshape(given[n].shape) for t in res[1 + 4 * k:5 + 4 * k])

    outs = [loss, grad_x[None]]
    for field in range(4):
        outs += [out[n][field] for n in WEIGHTS]
    return tuple(outs)
```

```python
import functools
import math

import jax
import jax.numpy as jnp
from jax import lax
from jax.experimental import pallas as pl
from jax.experimental.pallas import tpu as pltpu

F32 = jnp.float32
BF16 = jnp.bfloat16

N_DEV = 8
D_MODEL = 1024
HEAD_DIM = 64
A_GROUPS = ((128, 1), (512, 4), (2048, 16))
B_WINDOW = 128
M_HEADS = 4
M_HEAD_DIM = 128
MEM_LEN = 256
D_FF = 2816
ROPE_THETA = 500000.0
ROPE_DIMS = 16
BLOCK = 128
EPS = 1e-6
LANES = 128
BAND_Q_BLOCKS = 4
BAND_UNITS = 2

ADAM_LR = 0.001
ADAM_B1 = 0.9
ADAM_B2 = 0.999
ADAM_EPS = 1e-08
ADAM_WD = 0.01
ADAM_STEP = 10

VMEM_LIMIT_BYTES = 56 * 1024 * 1024
MESH = pl.DeviceIdType.MESH

WEIGHTS = ['attn_norm', 'w_in', 'a_q_norm', 'a_k_norm', 'b_q_norm', 'b_k_norm', 'b_sinks', 'mem_norm',
           'w_mem_kv', 'm_q_norm', 'm_k_norm', 'w_o_a', 'w_o_b', 'w_o_m', 'w_gate', 'b_gate', 'w_out',
           'ffn_norm', 'w_up', 'conv_w', 'conv_b', 'w_down']
BIG = {'w_in': 1, 'w_mem_kv': 0, 'w_o_a': 1, 'w_o_b': 1, 'w_o_m': 1, 'w_gate': 1, 'w_out': 0, 'w_up': 1,
       'conv_w': 1, 'w_down': 0}
SMALL = [n for n in WEIGHTS if n not in BIG]


def _cparams(n_grid):
    return pltpu.CompilerParams(dimension_semantics=("arbitrary",) * n_grid, vmem_limit_bytes=VMEM_LIMIT_BYTES)


def _pick(n, cands=(512, 256, 128)):
    for c in cands:
        if n % c == 0:
            return c
    return n


def _seg_matrix(width):
    shift = width.bit_length() - 1
    r = lax.shift_right_logical(lax.broadcasted_iota(jnp.int32, (LANES, LANES), 0), shift)
    c = lax.shift_right_logical(lax.broadcasted_iota(jnp.int32, (LANES, LANES), 1), shift)
    return jnp.where(r == c, 1.0, 0.0).astype(BF16)


def _seg_sum(x, seg):
    hi = x.astype(BF16)
    r1 = x - hi.astype(F32)
    mid = r1.astype(BF16)
    lo = (r1 - mid.astype(F32)).astype(BF16)
    dot = functools.partial(jnp.dot, preferred_element_type=F32)
    return dot(hi, seg) + dot(mid, seg) + dot(lo, seg)


def _rope(y, c, s1, s2):
    return y * c + pltpu.roll(y, LANES - ROPE_DIMS // 2, 1) * s1 + pltpu.roll(y, ROPE_DIMS // 2, 1) * s2


def _unrope(dy, c, s1, s2):
    return dy * c + pltpu.roll(dy * s1, ROPE_DIMS // 2, 1) + pltpu.roll(dy * s2, LANES - ROPE_DIMS // 2, 1)


def _sigmoid(x):
    return 1.0 / (1.0 + jnp.exp(-x))


def _rms_fwd(x, gain, name):
    s_len, d = x.shape
    tm = 512

    def body(x_ref, g_ref, h_ref, r_ref):
        xv = x_ref[...]
        r = lax.rsqrt(jnp.mean(xv * xv, axis=-1, keepdims=True) + EPS)
        h_ref[...] = ((xv * r) * g_ref[...]).astype(BF16)
        r_ref[...] = r

    return pl.pallas_call(
        body, name=name, grid=(s_len // tm,),
        in_specs=[pl.BlockSpec((tm, d), lambda i: (i, 0)), pl.BlockSpec((1, d), lambda i: (0, 0))],
        out_specs=(pl.BlockSpec((tm, d), lambda i: (i, 0)), pl.BlockSpec((tm, 1), lambda i: (i, 0))),
        out_shape=(jax.ShapeDtypeStruct((s_len, d), BF16), jax.ShapeDtypeStruct((s_len, 1), F32)),
        compiler_params=_cparams(1),
    )(x, gain)


def _rms_bwd(dh, x, r, gain, add, name, bf16_copy=False):
    s_len, d = x.shape
    tm = 512

    def body(dh_ref, x_ref, r_ref, g_ref, add_ref, dx_ref, *rest):
        dg_ref = rest[-1]

        @pl.when(pl.program_id(0) == 0)
        def _():
            dg_ref[...] = jnp.zeros_like(dg_ref)
        rv = r_ref[...]
        xhat = x_ref[...] * rv
        dhv = dh_ref[...]
        dg_ref[...] += jnp.sum(dhv * xhat, axis=0, keepdims=True)
        dxhat = dhv * g_ref[...]
        dx = add_ref[...] + rv * (dxhat - xhat * jnp.mean(dxhat * xhat, axis=-1, keepdims=True))
        dx_ref[...] = dx
        if bf16_copy:
            rest[0][...] = dx.astype(BF16)

    row = pl.BlockSpec((tm, d), lambda i: (i, 0))
    vec = pl.BlockSpec((1, d), lambda i: (0, 0))
    out_specs = [row] + ([row] if bf16_copy else []) + [vec]
    out_shape = [jax.ShapeDtypeStruct((s_len, d), F32)] + ([jax.ShapeDtypeStruct((s_len, d), BF16)] if bf16_copy else [])
    out_shape.append(jax.ShapeDtypeStruct((1, d), F32))
    return pl.pallas_call(
        body, name=name, grid=(s_len // tm,),
        in_specs=[row, row, pl.BlockSpec((tm, 1), lambda i: (i, 0)), vec, row],
        out_specs=tuple(out_specs), out_shape=tuple(out_shape), compiler_params=_cparams(1),
    )(dh, x, r, gain, add)


def _resident(shape, index_map):
    return pl.BlockSpec(shape, index_map, pipeline_mode=pl.Buffered(1))


def _mm_rows(pairs, name, nt=False, tm=512, bias=None, sigmoid=False, res=None, out_dtypes=(F32,)):
    m = pairs[0][0].shape[0]
    n = pairs[0][1].shape[0] if nt else pairs[0][1].shape[1]
    n_pairs = len(pairs)
    has_bias, has_res = bias is not None, res is not None
    dims = (((1,), (1,)), ((), ())) if nt else (((1,), (0,)), ((), ()))

    def body(*refs):
        acc = None
        for p in range(n_pairs):
            t = lax.dot_general(refs[2 * p][...].astype(BF16), refs[2 * p + 1][...], dims, preferred_element_type=F32)
            acc = t if acc is None else acc + t
        pos = 2 * n_pairs
        if has_bias:
            acc = acc + refs[pos][...]
            pos += 1
        if sigmoid:
            acc = _sigmoid(acc)
        if has_res:
            acc = refs[pos][...] + acc
            pos += 1
        for o_ref in refs[pos:]:
            o_ref[...] = acc.astype(o_ref.dtype)

    in_specs, args = [], []
    for a, w, blk in pairs:
        k = a.shape[1]
        in_specs.append(pl.BlockSpec((tm, k), lambda i: (i, 0)))
        if nt:
            in_specs.append(_resident((n, k), lambda i, blk=blk: (0, blk)))
        else:
            in_specs.append(_resident((k, n), lambda i, blk=blk: (blk, 0)))
        args += [a, w]
    if has_bias:
        in_specs.append(_resident((1, n), lambda i: (0, 0)))
        args.append(bias)
    if has_res:
        in_specs.append(pl.BlockSpec((tm, n), lambda i: (i, 0)))
        args.append(res)
    out = pl.BlockSpec((tm, n), lambda i: (i, 0))
    outs = pl.pallas_call(
        body, name=name, grid=(m // tm,), in_specs=in_specs, out_specs=tuple([out] * len(out_dtypes)),
        out_shape=tuple(jax.ShapeDtypeStruct((m, n), dt) for dt in out_dtypes), compiler_params=_cparams(1),
    )(*args)
    return outs[0] if len(out_dtypes) == 1 else outs


def _mm_tn(a, b, name, tile=256):
    k, m = a.shape
    n = b.shape[1]
    dims = (((0,), (0,)), ((), ()))

    def body(a_ref, b_ref, o_ref):
        o_ref[...] = lax.dot_general(a_ref[...].astype(BF16), b_ref[...].astype(BF16), dims, preferred_element_type=F32)

    if n <= m:
        t = min(tile, m)
        grid, a_spec, b_spec = (m // t,), pl.BlockSpec((k, t), lambda i: (0, i)), _resident((k, n), lambda i: (0, 0))
        o_spec = pl.BlockSpec((t, n), lambda i: (i, 0))
    else:
        t = min(tile, n)
        grid, a_spec, b_spec = (n // t,), _resident((k, m), lambda i: (0, 0)), pl.BlockSpec((k, t), lambda i: (0, i))
        o_spec = pl.BlockSpec((m, t), lambda i: (0, i))
    return pl.pallas_call(
        body, name=name, grid=grid, in_specs=[a_spec, b_spec], out_specs=o_spec,
        out_shape=jax.ShapeDtypeStruct((m, n), F32), compiler_params=_cparams(1),
    )(a, b)


def _norm_rope(t, gain, c, s1, s2, seg):
    rs = lax.rsqrt(_seg_sum(t * t, seg) * (1.0 / HEAD_DIM) + EPS)
    return _rope((t * rs) * gain, c, s1, s2)


def _dup_half(y, half):
    lane = lax.broadcasted_iota(jnp.int32, y.shape, 1)
    rolled = pltpu.roll(y, HEAD_DIM, 1)
    keep = (lane < HEAD_DIM) if half == 0 else (lane >= HEAD_DIM)
    return jnp.where(keep, y, rolled)


def _qk_prep(proj, cb0, d, gqa, gq, gk, tabs, name):
    s_len = proj.shape[0]
    tm = 512
    rows = tm // d
    n_units = 4 if gqa else 2 * d
    n_q = 4 if gqa else 2
    n_in = 6

    def body(*refs):
        in_refs = refs[:n_in]
        gq_ref, gk_ref, c_ref, s1_ref, s2_ref, o_ref = refs[n_in:]
        seg = _seg_matrix(HEAD_DIM)

        def rows_of(ref, r):
            return ref[...] if d == 1 else ref[pl.ds(r, rows, stride=d), :]

        def put(unit_col, y):
            o_ref[:, unit_col * LANES:(unit_col + 1) * LANES] = y.astype(BF16)

        for r in range(d):
            c, s1, s2 = rows_of(c_ref, r), rows_of(s1_ref, r), rows_of(s2_ref, r)
            for b in range(n_in):
                t = rows_of(in_refs[b], r)
                if b < n_q:
                    put((b * d + r) if not gqa else b, _norm_rope(t, gq_ref[...], c, s1, s2, seg))
                elif not gqa:
                    sec, pair = (1, b - 2) if b < 4 else (2, b - 4)
                    y = _norm_rope(t, gk_ref[...], c, s1, s2, seg) if sec == 1 else t
                    put(sec * n_units + pair * d + r, y)
                else:
                    sec = 1 if b == 4 else 2
                    y = _norm_rope(t, gk_ref[...], c, s1, s2, seg) if sec == 1 else t
                    for u in range(n_units):
                        put(sec * n_units + u, _dup_half(y, u // 2))

    in_specs = [pl.BlockSpec((tm, LANES), lambda i, b=b: (i, cb0 + b)) for b in range(n_in)]
    vec = pl.BlockSpec((1, LANES), lambda i: (0, 0))
    tab = pl.BlockSpec((tm, LANES), lambda i: (i, 0))
    width = 3 * n_units * LANES
    return pl.pallas_call(
        body, name=name, grid=(s_len // tm,), in_specs=in_specs + [vec, vec, tab, tab, tab],
        out_specs=pl.BlockSpec((rows, width), lambda i: (i, 0)),
        out_shape=jax.ShapeDtypeStruct((s_len // d, width), BF16), compiler_params=_cparams(1),
    )(*([proj] * n_in), gq, gk, *tabs)


def _qk_prep_bwd(dqkv, proj, cb0, d, gqa, gq, gk, tabs, name):
    s_len = proj.shape[0]
    tm = 512
    rows = tm // d
    n_units = 4 if gqa else 2 * d
    n_q = 4 if gqa else 2
    n_in = 6

    def body(*refs):
        d_refs = refs[0:3]
        in_refs = refs[3:3 + n_in]
        gq_ref, gk_ref, c_ref, s1_ref, s2_ref, o_ref, dgq_ref, dgk_ref, stage = refs[3 + n_in:]
        seg = _seg_matrix(HEAD_DIM)

        @pl.when(pl.program_id(0) == 0)
        def _():
            dgq_ref[...] = jnp.zeros_like(dgq_ref)
            dgk_ref[...] = jnp.zeros_like(dgk_ref)

        def rows_of(ref, r):
            return ref[...] if d == 1 else ref[pl.ds(r, rows, stride=d), :]

        def unit(col):
            sec, u = divmod(col, n_units)
            return d_refs[sec][:, u * LANES:(u + 1) * LANES]

        def norm_bwd(dyr, t, gain, c, s1, s2, dg_ref):
            rs = lax.rsqrt(_seg_sum(t * t, seg) * (1.0 / HEAD_DIM) + EPS)
            that = t * rs
            dy = _unrope(dyr, c, s1, s2)
            dg_ref[...] += jnp.sum(dy * that, axis=0, keepdims=True)
            dthat = dy * gain
            return rs * (dthat - that * (_seg_sum(dthat * that, seg) * (1.0 / HEAD_DIM)))

        def fold(sec):
            tot = []
            for u in range(n_units):
                v = unit(sec * n_units + u)
                tot.append(v + pltpu.roll(v, HEAD_DIM, 1))
            lane = lax.broadcasted_iota(jnp.int32, tot[0].shape, 1)
            return jnp.where(lane < HEAD_DIM, tot[0] + tot[1], tot[2] + tot[3])

        for b in range(n_in):
            for r in range(d):
                c, s1, s2 = rows_of(c_ref, r), rows_of(s1_ref, r), rows_of(s2_ref, r)
                t = rows_of(in_refs[b], r)
                if b < n_q:
                    g = unit((b * d + r) if not gqa else b)
                    out = norm_bwd(g, t, gq_ref[...], c, s1, s2, dgq_ref)
                elif not gqa:
                    sec, pair = (1, b - 2) if b < 4 else (2, b - 4)
                    g = unit(sec * n_units + pair * d + r)
                    out = norm_bwd(g, t, gk_ref[...], c, s1, s2, dgk_ref) if sec == 1 else g
                else:
                    sec = 1 if b == 4 else 2
                    g = fold(sec)
                    out = norm_bwd(g, t, gk_ref[...], c, s1, s2, dgk_ref) if sec == 1 else g
                if d == 1:
                    o_ref[:, b * LANES:(b + 1) * LANES] = out.astype(BF16)
                else:
                    stage[pl.ds(r, rows, stride=d), :] = out
            if d != 1:
                o_ref[:, b * LANES:(b + 1) * LANES] = stage[...].astype(BF16)

    in_specs = [pl.BlockSpec((rows, n_units * LANES), lambda i: (i, 0))] * 3
    in_specs += [pl.BlockSpec((tm, LANES), lambda i, b=b: (i, cb0 + b)) for b in range(n_in)]
    vec = pl.BlockSpec((1, LANES), lambda i: (0, 0))
    tab = pl.BlockSpec((tm, LANES), lambda i: (i, 0))
    return pl.pallas_call(
        body, name=name, grid=(s_len // tm,), in_specs=in_specs + [vec, vec, tab, tab, tab],
        out_specs=(pl.BlockSpec((tm, n_in * LANES), lambda i: (i, 0)), vec, vec),
        out_shape=(jax.ShapeDtypeStruct((s_len, n_in * LANES), BF16), jax.ShapeDtypeStruct((1, LANES), F32),
                   jax.ShapeDtypeStruct((1, LANES), F32)),
        scratch_shapes=[pltpu.VMEM((tm, LANES), F32)], compiler_params=_cparams(1),
    )(*dqkv, *([proj] * n_in), gq, gk, *tabs)


def _head_masks(shape):
    lane = lax.broadcasted_iota(jnp.int32, shape, 1)
    return lane < HEAD_DIM, lane >= HEAD_DIM


def _band_fwd(qkv, n_units, max_dist, sinks, name):
    n_rows = qkv.shape[0]
    nb = n_rows // BLOCK
    scale = HEAD_DIM ** -0.5
    has_sink = sinks is not None

    qn, un = min(nb, BAND_Q_BLOCKS), BAND_UNITS
    ug = n_units // un

    def body(*refs):
        q_ref, kp_ref, km_ref, vp_ref, vm_ref = refs[:5]
        o_ref, lse_ref = refs[-2:]
        i = pl.program_id(1)
        qi = lax.broadcasted_iota(jnp.int32, (BLOCK, 2 * BLOCK), 0)
        kj = lax.broadcasted_iota(jnp.int32, (BLOCK, 2 * BLOCK), 1)
        dist = qi + BLOCK - kj
        band = (dist >= 0) & (dist <= max_dist)
        band_first = band & ((i > 0) | (kj >= BLOCK))
        m0, m1 = _head_masks((BLOCK, LANES))
        zero = jnp.zeros((BLOCK, LANES), BF16)
        for ub in range(un):
            cs = slice(ub * LANES, (ub + 1) * LANES)
            for qb in range(qn):
                rs = slice(qb * BLOCK, (qb + 1) * BLOCK)
                q = q_ref[rs, cs]
                if qb == 0:
                    kk = jnp.concatenate([kp_ref[:, cs], km_ref[0:BLOCK, cs]], axis=0)
                    vv = jnp.concatenate([vp_ref[:, cs], vm_ref[0:BLOCK, cs]], axis=0)
                    valid = band_first
                else:
                    kk = km_ref[(qb - 1) * BLOCK:(qb + 1) * BLOCK, cs]
                    vv = vm_ref[(qb - 1) * BLOCK:(qb + 1) * BLOCK, cs]
                    valid = band
                outs, lses = [], []
                for e, hm in enumerate((m0, m1)):
                    qe = jnp.where(hm, q, zero)
                    s = lax.dot_general(qe, kk, (((1,), (1,)), ((), ())), preferred_element_type=F32) * scale
                    s = jnp.where(valid, s, -jnp.inf)
                    mx = jnp.max(s, axis=-1, keepdims=True)
                    if has_sink:
                        sk = refs[5][ub][:, e * HEAD_DIM:e * HEAD_DIM + 1]
                        mx = jnp.maximum(mx, sk)
                    p = jnp.exp(s - mx)
                    den = jnp.sum(p, axis=-1, keepdims=True)
                    if has_sink:
                        den = den + jnp.exp(sk - mx)
                    pn = (p * (1.0 / den)).astype(BF16)
                    outs.append(jnp.dot(pn, vv, preferred_element_type=F32))
                    lses.append(mx + jnp.log(den))
                o_ref[rs, cs] = jnp.where(m0, outs[0], outs[1])
                lse_ref[rs, cs] = jnp.where(m0, jnp.broadcast_to(lses[0], (BLOCK, LANES)),
                                            jnp.broadcast_to(lses[1], (BLOCK, LANES)))

    def main(sec):
        return pl.BlockSpec((qn * BLOCK, un * LANES), lambda u, i: (i, sec * ug + u))

    def prev(sec):
        return pl.BlockSpec((BLOCK, un * LANES), lambda u, i: (jnp.maximum(i * qn - 1, 0), sec * ug + u))

    in_specs = [main(0), prev(1), main(1), prev(2), main(2)]
    args = [qkv] * 5
    if has_sink:
        in_specs.append(pl.BlockSpec((un, 1, LANES), lambda u, i: (u, 0, 0)))
        args.append(sinks)
    return pl.pallas_call(
        body, name=name, grid=(ug, nb // qn), in_specs=in_specs, out_specs=(main(0), main(0)),
        out_shape=(jax.ShapeDtypeStruct((n_rows, n_units * LANES), F32),) * 2, compiler_params=_cparams(2),
    )(*args)


def _band_bwd(qkv, do, lse, delta, n_units, max_dist, name):
    n_rows = qkv.shape[0]
    nb = n_rows // BLOCK
    scale = HEAD_DIM ** -0.5

    qn, un = min(nb, BAND_Q_BLOCKS), BAND_UNITS
    ug = n_units // un
    steps = nb // qn
    nt_dims = (((1,), (1,)), ((), ()))
    tn_dims = (((0,), (0,)), ((), ()))

    def body(qm_ref, qx_ref, kp_ref, km_ref, vp_ref, vm_ref, dom_ref, dox_ref, lm_ref, lx_ref, dm_ref, dx_ref,
             dq_ref, dk_ref, dv_ref):
        i = pl.program_id(1)
        m0, m1 = _head_masks((BLOCK, LANES))
        zero = jnp.zeros((BLOCK, LANES), BF16)
        qi = lax.broadcasted_iota(jnp.int32, (BLOCK, 2 * BLOCK), 0)
        kj = lax.broadcasted_iota(jnp.int32, (BLOCK, 2 * BLOCK), 1)
        dist = qi + BLOCK - kj
        band = (dist >= 0) & (dist <= max_dist)
        band_first = band & ((i > 0) | (kj >= BLOCK))
        qr = lax.broadcasted_iota(jnp.int32, (2 * BLOCK, BLOCK), 0)
        kc = lax.broadcasted_iota(jnp.int32, (2 * BLOCK, BLOCK), 1)
        dist2 = qr - kc
        band2 = (dist2 >= 0) & (dist2 <= max_dist)
        band2_last = band2 & ((qr < BLOCK) | (i < steps - 1))
        m0w, m1w = _head_masks((2 * BLOCK, LANES))
        zero2 = jnp.zeros((2 * BLOCK, LANES), BF16)

        def two(main_ref, next_ref, kb, cs):
            if kb < qn - 1:
                return main_ref[kb * BLOCK:(kb + 2) * BLOCK, cs]
            return jnp.concatenate([main_ref[kb * BLOCK:(kb + 1) * BLOCK, cs], next_ref[:, cs]], axis=0)

        for ub in range(un):
            cs = slice(ub * LANES, (ub + 1) * LANES)
            for qb in range(qn):
                rs = slice(qb * BLOCK, (qb + 1) * BLOCK)
                q = qm_ref[rs, cs]
                dob = dom_ref[rs, cs]
                lse_b = lm_ref[rs, cs]
                del_b = dm_ref[rs, cs]
                if qb == 0:
                    kk = jnp.concatenate([kp_ref[:, cs], km_ref[0:BLOCK, cs]], axis=0)
                    vv = jnp.concatenate([vp_ref[:, cs], vm_ref[0:BLOCK, cs]], axis=0)
                    valid = band_first
                else:
                    kk = km_ref[(qb - 1) * BLOCK:(qb + 1) * BLOCK, cs]
                    vv = vm_ref[(qb - 1) * BLOCK:(qb + 1) * BLOCK, cs]
                    valid = band
                dqs = []
                for e, hm in enumerate((m0, m1)):
                    col = slice(e * HEAD_DIM, e * HEAD_DIM + 1)
                    s = lax.dot_general(jnp.where(hm, q, zero), kk, nt_dims, preferred_element_type=F32) * scale
                    p = jnp.where(valid, jnp.exp(s - lse_b[:, col]), 0.0)
                    dp = lax.dot_general(jnp.where(hm, dob, zero), vv, nt_dims, preferred_element_type=F32)
                    ds = (p * (dp - del_b[:, col]) * scale).astype(BF16)
                    dqs.append(jnp.dot(ds, kk, preferred_element_type=F32))
                dq_ref[rs, cs] = jnp.where(m0, dqs[0], dqs[1])
            for kb in range(qn):
                rs = slice(kb * BLOCK, (kb + 1) * BLOCK)
                qq = two(qm_ref, qx_ref, kb, cs)
                dd = two(dom_ref, dox_ref, kb, cs)
                ll = two(lm_ref, lx_ref, kb, cs)
                de = two(dm_ref, dx_ref, kb, cs)
                k = km_ref[rs, cs]
                v = vm_ref[rs, cs]
                valid2 = band2 if kb < qn - 1 else band2_last
                dk = jnp.zeros((BLOCK, LANES), F32)
                dv = jnp.zeros((BLOCK, LANES), F32)
                for e, hm in enumerate((m0w, m1w)):
                    col = slice(e * HEAD_DIM, e * HEAD_DIM + 1)
                    qe = jnp.where(hm, qq, zero2)
                    doe = jnp.where(hm, dd, zero2)
                    s = lax.dot_general(qe, k, nt_dims, preferred_element_type=F32) * scale
                    p = jnp.where(valid2, jnp.exp(s - ll[:, col]), 0.0)
                    dp = lax.dot_general(doe, v, nt_dims, preferred_element_type=F32)
                    ds = (p * (dp - de[:, col]) * scale).astype(BF16)
                    dk = dk + lax.dot_general(ds, qe, tn_dims, preferred_element_type=F32)
                    dv = dv + lax.dot_general(p.astype(BF16), doe, tn_dims, preferred_element_type=F32)
                dk_ref[rs, cs] = dk
                dv_ref[rs, cs] = dv

    def main(sec):
        return pl.BlockSpec((qn * BLOCK, un * LANES), lambda u, i: (i, sec * ug + u))

    def prev(sec):
        return pl.BlockSpec((BLOCK, un * LANES), lambda u, i: (jnp.maximum(i * qn - 1, 0), sec * ug + u))

    def nxt(sec):
        return pl.BlockSpec((BLOCK, un * LANES), lambda u, i: (jnp.minimum((i + 1) * qn, nb - 1), sec * ug + u))

    in_specs = [main(0), nxt(0), prev(1), main(1), prev(2), main(2),
                main(0), nxt(0), main(0), nxt(0), main(0), nxt(0)]
    args = [qkv] * 6 + [do, do, lse, lse, delta, delta]
    shp = jax.ShapeDtypeStruct((n_rows, n_units * LANES), F32)
    return pl.pallas_call(
        body, name=name, grid=(ug, steps), in_specs=in_specs, out_specs=(main(0), main(0), main(0)),
        out_shape=(shp, shp, shp), compiler_params=_cparams(2),
    )(*args)


def _merge_groups(os_, lses, dils, name):
    s_len = os_[0].shape[0] * dils[0]
    tm = 512

    def body(*refs):
        o_refs, l_refs = refs[0:3], refs[3:6]
        o_ref, lse_ref = refs[6:8]
        so, sl = refs[8:11], refs[11:14]
        for pair in range(2):
            for g, d in enumerate(dils):
                rows = tm // d
                for r in range(d):
                    col = slice((pair * d + r) * LANES, (pair * d + r + 1) * LANES)
                    if d == 1:
                        so[g][...] = o_refs[g][:, col]
                        sl[g][...] = l_refs[g][:, col]
                    else:
                        so[g][pl.ds(r, rows, stride=d), :] = o_refs[g][:, col]
                        sl[g][pl.ds(r, rows, stride=d), :] = l_refs[g][:, col]
            l0, l1, l2 = sl[0][...], sl[1][...], sl[2][...]
            mx = jnp.maximum(jnp.maximum(l0, l1), l2)
            e0, e1, e2 = jnp.exp(l0 - mx), jnp.exp(l1 - mx), jnp.exp(l2 - mx)
            den = e0 + e1 + e2
            inv = 1.0 / den
            o_ref[:, pair * LANES:(pair + 1) * LANES] = (so[0][...] * (e0 * inv) + so[1][...] * (e1 * inv)
                                                         + so[2][...] * (e2 * inv))
            lse_ref[:, pair * LANES:(pair + 1) * LANES] = mx + jnp.log(den)

    in_specs = [pl.BlockSpec((tm // d, 2 * d * LANES), lambda i: (i, 0)) for d in dils] * 2
    out = pl.BlockSpec((tm, 2 * LANES), lambda i: (i, 0))
    shp = jax.ShapeDtypeStruct((s_len, 2 * LANES), F32)
    return pl.pallas_call(
        body, name=name, grid=(s_len // tm,), in_specs=in_specs, out_specs=(out, out), out_shape=(shp, shp),
        scratch_shapes=[pltpu.VMEM((tm, LANES), F32)] * 6, compiler_params=_cparams(1),
    )(*os_, *lses)


def _bwd_prep(do, o, lse, dils, sinks, name):
    s_len, width = do.shape
    n_pairs = width // LANES
    tm = 512
    has_sink = sinks is not None
    n_g = len(dils)

    def body(*refs):
        do_ref, o_ref, lse_ref = refs[:3]
        pos = 3
        if has_sink:
            sink_ref = refs[pos]
            pos += 1
        outs = refs[pos:pos + 3 * n_g]
        pos += 3 * n_g
        if has_sink:
            dsink_ref = refs[pos]
            pos += 1
        s_do, s_l, s_d = refs[pos:pos + 3]
        seg = _seg_matrix(HEAD_DIM)

        if has_sink:
            @pl.when(pl.program_id(0) == 0)
            def _():
                dsink_ref[...] = jnp.zeros_like(dsink_ref)

        for pair in range(n_pairs):
            col = slice(pair * LANES, (pair + 1) * LANES)
            dov = do_ref[:, col]
            lv = lse_ref[:, col]
            delta = _seg_sum(dov * o_ref[:, col], seg)
            if has_sink:
                dsink_ref[pair] += -jnp.sum(jnp.exp(sink_ref[pair] - lv) * delta, axis=0, keepdims=True)
            s_do[...] = dov
            s_l[...] = lv
            s_d[...] = delta
            for g, d in enumerate(dils):
                rows = tm // d
                for r in range(d):
                    oc = slice((pair * d + r) * LANES, (pair * d + r + 1) * LANES)
                    if d == 1:
                        a, b, c = s_do[...], s_l[...], s_d[...]
                    else:
                        a = s_do[pl.ds(r, rows, stride=d), :]
                        b = s_l[pl.ds(r, rows, stride=d), :]
                        c = s_d[pl.ds(r, rows, stride=d), :]
                    outs[3 * g][:, oc] = a.astype(BF16)
                    outs[3 * g + 1][:, oc] = b
                    outs[3 * g + 2][:, oc] = c

    row = pl.BlockSpec((tm, width), lambda i: (i, 0))
    in_specs = [row, row, row]
    args = [do, o, lse]
    if has_sink:
        in_specs.append(pl.BlockSpec((n_pairs, 1, LANES), lambda i: (0, 0, 0)))
        args.append(sinks)
    out_specs, out_shape = [], []
    for d in dils:
        for dt in (BF16, F32, F32):
            out_specs.append(pl.BlockSpec((tm // d, n_pairs * d * LANES), lambda i: (i, 0)))
            out_shape.append(jax.ShapeDtypeStruct((s_len // d, n_pairs * d * LANES), dt))
    if has_sink:
        out_specs.append(pl.BlockSpec((n_pairs, 1, LANES), lambda i: (0, 0, 0)))
        out_shape.append(jax.ShapeDtypeStruct((n_pairs, 1, LANES), F32))
    return pl.pallas_call(
        body, name=name, grid=(s_len // tm,), in_specs=in_specs, out_specs=tuple(out_specs),
        out_shape=tuple(out_shape), scratch_shapes=[pltpu.VMEM((tm, LANES), F32)] * 3, compiler_params=_cparams(1),
    )(*args)


def _mem_kv(mem, mem_gain, w_kv, k_gain, name):
    m_len = mem.shape[0]
    kw = M_HEADS * M_HEAD_DIM

    def body(mem_ref, mg_ref, w_ref, kg_ref, k_ref, v_ref):
        mv = mem_ref[...]
        r = lax.rsqrt(jnp.mean(mv * mv, axis=-1, keepdims=True) + EPS)
        mn = ((mv * r) * mg_ref[...]).astype(BF16)
        kv = jnp.dot(mn, w_ref[...], preferred_element_type=F32)
        for h in range(M_HEADS):
            col = slice(h * M_HEAD_DIM, (h + 1) * M_HEAD_DIM)
            t = kv[:, col]
            rk = lax.rsqrt(jnp.mean(t * t, axis=-1, keepdims=True) + EPS)
            k_ref[:, col] = ((t * rk) * kg_ref[...]).astype(BF16)
        v_ref[...] = kv[:, kw:].astype(BF16)

    shp = jax.ShapeDtypeStruct((m_len, kw), BF16)
    return pl.pallas_call(body, name=name, out_shape=(shp, shp),
                          compiler_params=pltpu.CompilerParams(vmem_limit_bytes=VMEM_LIMIT_BYTES))(mem, mem_gain, w_kv, k_gain)


def _mem_kv_bwd(mem, mem_gain, w_kv, k_gain, dk, dv, name):
    m_len, d = mem.shape
    kw = M_HEADS * M_HEAD_DIM

    def body(mem_ref, mg_ref, w_ref, kg_ref, dk_ref, dv_ref, dw_ref, dmg_ref, dkg_ref, dkv_ref):
        mv = mem_ref[...]
        r = lax.rsqrt(jnp.mean(mv * mv, axis=-1, keepdims=True) + EPS)
        mhat = mv * r
        mn = (mhat * mg_ref[...]).astype(BF16)
        kv = jnp.dot(mn, w_ref[...], preferred_element_type=F32)
        dkg = jnp.zeros((1, M_HEAD_DIM), F32)
        for h in range(M_HEADS):
            col = slice(h * M_HEAD_DIM, (h + 1) * M_HEAD_DIM)
            t = kv[:, col]
            rk = lax.rsqrt(jnp.mean(t * t, axis=-1, keepdims=True) + EPS)
            that = t * rk
            dy = dk_ref[:, col]
            dkg = dkg + jnp.sum(dy * that, axis=0, keepdims=True)
            dthat = dy * kg_ref[...]
            dkv_ref[:, col] = (rk * (dthat - that * jnp.mean(dthat * that, axis=-1, keepdims=True))).astype(BF16)
        dkv_ref[:, kw:] = dv_ref[...].astype(BF16)
        dkg_ref[...] = dkg
        dkv = dkv_ref[...]
        dw_ref[...] = lax.dot_general(mn, dkv, (((0,), (0,)), ((), ())), preferred_element_type=F32)
        dmn = lax.dot_general(dkv, w_ref[...], (((1,), (1,)), ((), ())), preferred_element_type=F32)
        dmg_ref[...] = jnp.sum(dmn * mhat, axis=0, keepdims=True)

    return pl.pallas_call(
        body, name=name,
        out_shape=(jax.ShapeDtypeStruct((d, 2 * kw), F32), jax.ShapeDtypeStruct((1, d), F32),
                   jax.ShapeDtypeStruct((1, M_HEAD_DIM), F32)),
        scratch_shapes=[pltpu.VMEM((m_len, 2 * kw), BF16)],
        compiler_params=pltpu.CompilerParams(vmem_limit_bytes=VMEM_LIMIT_BYTES),
    )(mem, mem_gain, w_kv, k_gain, dk, dv)


def _mem_attn_fwd(proj, cidx, mk, mv, q_gain, name):
    s_len = proj.shape[0]
    kw = M_HEADS * M_HEAD_DIM
    tm = 512
    scale = M_HEAD_DIM ** -0.5

    def body(q_ref, k_ref, v_ref, g_ref, o_ref):
        for h in range(M_HEADS):
            col = slice(h * M_HEAD_DIM, (h + 1) * M_HEAD_DIM)
            t = q_ref[:, col]
            rs = lax.rsqrt(jnp.mean(t * t, axis=-1, keepdims=True) + EPS)
            qn = ((t * rs) * g_ref[...]).astype(BF16)
            s = lax.dot_general(qn, k_ref[:, col], (((1,), (1,)), ((), ())), preferred_element_type=F32) * scale
            mx = jnp.max(s, axis=-1, keepdims=True)
            p = jnp.exp(s - mx)
            pn = (p * (1.0 / jnp.sum(p, axis=-1, keepdims=True))).astype(BF16)
            o_ref[:, col] = jnp.dot(pn, v_ref[:, col], preferred_element_type=F32).astype(BF16)

    whole = pl.BlockSpec((MEM_LEN, kw), lambda i: (0, 0))
    return pl.pallas_call(
        body, name=name, grid=(s_len // tm,),
        in_specs=[pl.BlockSpec((tm, kw), lambda i: (i, cidx)), whole, whole, pl.BlockSpec((1, M_HEAD_DIM), lambda i: (0, 0))],
        out_specs=pl.BlockSpec((tm, kw), lambda i: (i, 0)),
        out_shape=jax.ShapeDtypeStruct((s_len, kw), BF16), compiler_params=_cparams(1),
    )(proj, mk, mv, q_gain)


def _mem_attn_bwd(proj, cidx, mk, mv, q_gain, do, name):
    s_len = proj.shape[0]
    kw = M_HEADS * M_HEAD_DIM
    tm = 512
    scale = M_HEAD_DIM ** -0.5

    def body(q_ref, k_ref, v_ref, g_ref, do_ref, dq_ref, dk_ref, dv_ref, dg_ref):
        @pl.when(pl.program_id(0) == 0)
        def _():
            dk_ref[...] = jnp.zeros_like(dk_ref)
            dv_ref[...] = jnp.zeros_like(dv_ref)
            dg_ref[...] = jnp.zeros_like(dg_ref)

        for h in range(M_HEADS):
            col = slice(h * M_HEAD_DIM, (h + 1) * M_HEAD_DIM)
            t = q_ref[:, col]
            rs = lax.rsqrt(jnp.mean(t * t, axis=-1, keepdims=True) + EPS)
            that = t * rs
            qn = (that * g_ref[...]).astype(BF16)
            kh, vh = k_ref[:, col], v_ref[:, col]
            dob = do_ref[:, col].astype(BF16)
            s = lax.dot_general(qn, kh, (((1,), (1,)), ((), ())), preferred_element_type=F32) * scale
            mx = jnp.max(s, axis=-1, keepdims=True)
            p = jnp.exp(s - mx)
            p = p * (1.0 / jnp.sum(p, axis=-1, keepdims=True))
            dp = lax.dot_general(dob, vh, (((1,), (1,)), ((), ())), preferred_element_type=F32)
            ds = (p * (dp - jnp.sum(p * dp, axis=-1, keepdims=True)) * scale).astype(BF16)
            dqn = jnp.dot(ds, kh, preferred_element_type=F32)
            dk_ref[:, col] += lax.dot_general(ds, qn, (((0,), (0,)), ((), ())), preferred_element_type=F32)
            dv_ref[:, col] += lax.dot_general(p.astype(BF16), dob, (((0,), (0,)), ((), ())), preferred_element_type=F32)
            dg_ref[...] += jnp.sum(dqn * that, axis=0, keepdims=True)
            dthat = dqn * g_ref[...]
            dq_ref[:, col] = (rs * (dthat - that * jnp.mean(dthat * that, axis=-1, keepdims=True))).astype(BF16)

    whole = pl.BlockSpec((MEM_LEN, kw), lambda i: (0, 0))
    vec = pl.BlockSpec((1, M_HEAD_DIM), lambda i: (0, 0))
    row = pl.BlockSpec((tm, kw), lambda i: (i, 0))
    return pl.pallas_call(
        body, name=name, grid=(s_len // tm,),
        in_specs=[pl.BlockSpec((tm, kw), lambda i: (i, cidx)), whole, whole, vec, row],
        out_specs=(row, whole, whole, vec),
        out_shape=(jax.ShapeDtypeStruct((s_len, kw), BF16), jax.ShapeDtypeStruct((MEM_LEN, kw), F32),
                   jax.ShapeDtypeStruct((MEM_LEN, kw), F32), jax.ShapeDtypeStruct((1, M_HEAD_DIM), F32)),
        compiler_params=_cparams(1),
    )(proj, mk, mv, q_gain, do)


def _gate_merge(gates, pa, pb, pm, name):
    s_len, d = pa.shape
    tm = 256

    def body(g_ref, a_ref, b_ref, m_ref, o_ref):
        f = lambda v: v.astype(F32)
        o_ref[...] = (f(g_ref[:, 0:d]) * f(a_ref[...]) + f(g_ref[:, d:2 * d]) * f(b_ref[...])
                      + f(g_ref[:, 2 * d:3 * d]) * f(m_ref[...])).astype(BF16)

    row = pl.BlockSpec((tm, d), lambda i: (i, 0))
    return pl.pallas_call(
        body, name=name, grid=(s_len // tm,), in_specs=[pl.BlockSpec((tm, 3 * d), lambda i: (i, 0)), row, row, row],
        out_specs=row, out_shape=jax.ShapeDtypeStruct((s_len, d), BF16), compiler_params=_cparams(1),
    )(gates, pa, pb, pm)


def _gate_merge_bwd(dmerged, gates, pa, pb, pm, name):
    s_len, d = pa.shape
    tm = 256

    def body(dm_ref, g_ref, a_ref, b_ref, m_ref, da_ref, db_ref, dmm_ref, dg_ref, dbg_ref):
        @pl.when(pl.program_id(0) == 0)
        def _():
            dbg_ref[...] = jnp.zeros_like(dbg_ref)
        dm = dm_ref[...]
        for k, (p_ref, dp_ref) in enumerate(((a_ref, da_ref), (b_ref, db_ref), (m_ref, dmm_ref))):
            col = slice(k * d, (k + 1) * d)
            g = g_ref[:, col].astype(F32)
            dp_ref[...] = (dm * g).astype(BF16)
            dpre = (dm * p_ref[...].astype(F32)) * (g * (1.0 - g))
            dbg_ref[:, col] += jnp.sum(dpre, axis=0, keepdims=True)
            dg_ref[:, col] = dpre.astype(BF16)

    row = pl.BlockSpec((tm, d), lambda i: (i, 0))
    wide = pl.BlockSpec((tm, 3 * d), lambda i: (i, 0))
    shp = jax.ShapeDtypeStruct((s_len, d), BF16)
    return pl.pallas_call(
        body, name=name, grid=(s_len // tm,), in_specs=[row, wide, row, row, row],
        out_specs=(row, row, row, wide, pl.BlockSpec((1, 3 * d), lambda i: (0, 0))),
        out_shape=(shp, shp, shp, jax.ShapeDtypeStruct((s_len, 3 * d), BF16), jax.ShapeDtypeStruct((1, 3 * d), F32)),
        compiler_params=_cparams(1),
    )(dmerged, gates, pa, pb, pm)


CONV_CHUNK = 256


def _pick_row(tile, j):
    row = lax.broadcasted_iota(jnp.int32, tile.shape, 0)
    return jnp.sum(jnp.where(row == j, tile, jnp.zeros_like(tile)), axis=0, keepdims=True)


def _rows_before(ref, start, k):
    cur = ref[pl.ds(start, CONV_CHUNK), :].astype(F32)
    prev = ref[pl.ds(pl.multiple_of(jnp.maximum(start - 16, 0), 16), 16), :].astype(F32)
    prev = jnp.where(start > 0, prev, jnp.zeros_like(prev))
    rolled = pltpu.roll(cur, k, 0)
    row = lax.broadcasted_iota(jnp.int32, cur.shape, 0)
    for j in range(k):
        rolled = jnp.where(row == j, _pick_row(prev, 16 - k + j), rolled)
    return rolled


def _rows_after(ref, start, k):
    cur = ref[pl.ds(start, CONV_CHUNK), :]
    nxt = ref[pl.ds(pl.multiple_of(start + CONV_CHUNK, 8), 8), :]
    rolled = pltpu.roll(cur, CONV_CHUNK - k, 0)
    row = lax.broadcasted_iota(jnp.int32, cur.shape, 0)
    for j in range(k):
        rolled = jnp.where(row == CONV_CHUNK - k + j, _pick_row(nxt, j), rolled)
    return rolled


def _conv_pre(u_ref, w_ref, b_ref, start):
    u2 = _rows_before(u_ref, start, 2)
    u1 = _rows_before(u_ref, start, 1)
    u0 = u_ref[pl.ds(start, CONV_CHUNK), :].astype(F32)
    c = ((b_ref[...] + w_ref[0:1, :] * u2) + w_ref[1:2, :] * u1) + w_ref[2:3, :] * u0
    return c, (u2, u1, u0)


def _conv_glu(u, conv_w, conv_b, name):
    s_len = u.shape[0]
    nblk = D_FF // LANES

    def body(ua_ref, ug_ref, wa_ref, wg_ref, ba_ref, bg_ref, o_ref):
        def chunk(ci, carry):
            start = pl.multiple_of(ci * CONV_CHUNK, CONV_CHUNK)
            ca, _ = _conv_pre(ua_ref, wa_ref, ba_ref, start)
            cg, _ = _conv_pre(ug_ref, wg_ref, bg_ref, start)
            o_ref[pl.ds(start, CONV_CHUNK), :] = ((ca * _sigmoid(ca)) * cg).astype(BF16)
            return carry
        lax.fori_loop(0, s_len // CONV_CHUNK, chunk, 0)

    def col(rows, off):
        return pl.BlockSpec((rows, LANES), lambda j: (0, off + j))

    return pl.pallas_call(
        body, name=name, grid=(nblk,),
        in_specs=[col(s_len, 0), col(s_len, nblk), col(3, 0), col(3, nblk), col(1, 0), col(1, nblk)],
        out_specs=col(s_len, 0), out_shape=jax.ShapeDtypeStruct((s_len, D_FF), BF16), compiler_params=_cparams(1),
    )(u, u, conv_w, conv_w, conv_b, conv_b)


def _conv_glu_bwd(dact, u, conv_w, conv_b, name):
    s_len = u.shape[0]
    nblk = D_FF // LANES
    n_chunks = s_len // CONV_CHUNK

    def body(da_ref, ua_ref, ug_ref, wa_ref, wg_ref, ba_ref, bg_ref,
             dua_ref, dug_ref, dwa_ref, dwg_ref, dba_ref, dbg_ref, sa, sg):
        sa[pl.ds(s_len, 8), :] = jnp.zeros((8, LANES), F32)
        sg[pl.ds(s_len, 8), :] = jnp.zeros((8, LANES), F32)
        zero = jnp.zeros((1, LANES), F32)

        def chunk1(ci, carry):
            start = pl.multiple_of(ci * CONV_CHUNK, CONV_CHUNK)
            ca, ua = _conv_pre(ua_ref, wa_ref, ba_ref, start)
            cg, ug = _conv_pre(ug_ref, wg_ref, bg_ref, start)
            dact_v = da_ref[pl.ds(start, CONV_CHUNK), :].astype(F32)
            sig = _sigmoid(ca)
            dcg = dact_v * (ca * sig)
            dca = (dact_v * cg) * (sig * (1.0 + ca * (1.0 - sig)))
            sa[pl.ds(start, CONV_CHUNK), :] = dca
            sg[pl.ds(start, CONV_CHUNK), :] = dcg
            out = [carry[0] + jnp.sum(dca, axis=0, keepdims=True), carry[1] + jnp.sum(dcg, axis=0, keepdims=True)]
            for j in range(3):
                out.append(carry[2 + j] + jnp.sum(dca * ua[j], axis=0, keepdims=True))
            for j in range(3):
                out.append(carry[5 + j] + jnp.sum(dcg * ug[j], axis=0, keepdims=True))
            return tuple(out)

        acc = lax.fori_loop(0, n_chunks, chunk1, (zero,) * 8)
        dba_ref[...] = acc[0]
        dbg_ref[...] = acc[1]
        for j in range(3):
            dwa_ref[j:j + 1, :] = acc[2 + j]
            dwg_ref[j:j + 1, :] = acc[5 + j]

        def chunk2(ci, carry):
            start = pl.multiple_of(ci * CONV_CHUNK, CONV_CHUNK)
            for s_ref, w_ref, o_ref in ((sa, wa_ref, dua_ref), (sg, wg_ref, dug_ref)):
                d0 = s_ref[pl.ds(start, CONV_CHUNK), :]
                d1 = _rows_after(s_ref, start, 1)
                d2 = _rows_after(s_ref, start, 2)
                o_ref[pl.ds(start, CONV_CHUNK), :] = (w_ref[2:3, :] * d0 + w_ref[1:2, :] * d1
                                                      + w_ref[0:1, :] * d2).astype(BF16)
            return carry
        lax.fori_loop(0, n_chunks, chunk2, 0)

    def col(rows, off):
        return pl.BlockSpec((rows, LANES), lambda j: (0, off + j))

    big = jax.ShapeDtypeStruct((s_len, D_FF), BF16)
    return pl.pallas_call(
        body, name=name, grid=(nblk,),
        in_specs=[col(s_len, 0), col(s_len, 0), col(s_len, nblk), col(3, 0), col(3, nblk), col(1, 0), col(1, nblk)],
        out_specs=(col(s_len, 0), col(s_len, 0), col(3, 0), col(3, 0), col(1, 0), col(1, 0)),
        out_shape=(big, big, jax.ShapeDtypeStruct((3, D_FF), F32), jax.ShapeDtypeStruct((3, D_FF), F32),
                   jax.ShapeDtypeStruct((1, D_FF), F32), jax.ShapeDtypeStruct((1, D_FF), F32)),
        scratch_shapes=[pltpu.VMEM((s_len + 8, LANES), F32)] * 2, compiler_params=_cparams(1),
    )(dact, u, u, conv_w, conv_w, conv_b, conv_b)


def _loss_head(y, target, name):
    s_len, d = y.shape
    tm = 512

    def body(y_ref, t_ref, dy_ref, dyb_ref, l_ref):
        @pl.when(pl.program_id(0) == 0)
        def _():
            l_ref[...] = jnp.zeros_like(l_ref)
        err = y_ref[...] - t_ref[...]
        dy = err * (1.0 / d)
        dy_ref[...] = dy
        dyb_ref[...] = dy.astype(BF16)
        part = 0.5 * jnp.sum(jnp.mean(err * err, axis=-1, keepdims=True), axis=0, keepdims=True)
        l_ref[...] += jnp.broadcast_to(part, l_ref.shape)

    row = pl.BlockSpec((tm, d), lambda i: (i, 0))
    return pl.pallas_call(
        body, name=name, grid=(s_len // tm,), in_specs=[row, row],
        out_specs=(row, row, pl.BlockSpec((8, LANES), lambda i: (0, 0))),
        out_shape=(jax.ShapeDtypeStruct((s_len, d), F32), jax.ShapeDtypeStruct((s_len, d), BF16),
                   jax.ShapeDtypeStruct((8, LANES), F32)),
        compiler_params=_cparams(1),
    )(y, target)


def _rope_tables(positions):
    half = ROPE_DIMS // 2
    freqs = jnp.exp(jnp.arange(half, dtype=F32) * (-2.0 * math.log(ROPE_THETA) / ROPE_DIMS))
    ang = positions.reshape(-1).astype(F32)[:, None] * freqs
    cos, sin = jnp.cos(ang), jnp.sin(ang)
    n = ang.shape[0]
    zeros = lambda w: jnp.zeros((n, w), F32)
    c = jnp.concatenate([cos, cos, jnp.ones((n, HEAD_DIM - ROPE_DIMS), F32)], axis=1)
    s1 = jnp.concatenate([-sin, zeros(HEAD_DIM - half)], axis=1)
    s2 = jnp.concatenate([zeros(half), sin, zeros(HEAD_DIM - ROPE_DIMS)], axis=1)
    return tuple(jnp.tile(t, (1, 2)) for t in (c, s1, s2))


def _two(v):
    return jnp.tile(v.reshape(1, HEAD_DIM), (1, 2))


def _fold_heads(g):
    return g[0, :HEAD_DIM] + g[0, HEAD_DIM:]


def _device_step(x, mem, positions, target, w):
    tabs = _rope_tables(positions)
    dils = tuple(d for _, d in A_GROUPS)
    grads = {}

    h, r1 = _rms_fwd(x, w['attn_norm'], "rms1")
    proj = _mm_rows([(h, w['w_in'], 0)], "mm_in")
    gates = _mm_rows([(h, w['w_gate'], 0)], "mm_gate", bias=w['b_gate'], sigmoid=True, out_dtypes=(BF16,))

    qkv_a, o_g, lse_g = [], [], []
    for gi, (window, d) in enumerate(A_GROUPS):
        gq, gk = _two(w['a_q_norm'][gi]), _two(w['a_k_norm'][gi])
        qkv = _qk_prep(proj, 6 * gi, d, False, gq, gk, tabs, f"qk_prep_a{gi}")
        o, lse = _band_fwd(qkv, 2 * d, window // d, None, f"band_fwd_a{gi}")
        qkv_a.append(qkv)
        o_g.append(o)
        lse_g.append(lse)
    o_a, lse_a = _merge_groups(o_g, lse_g, dils, "merge_a")

    gbq, gbk = _two(w['b_q_norm']), _two(w['b_k_norm'])
    sinks = jnp.repeat(w['b_sinks'].reshape(4, 2), HEAD_DIM, axis=1).reshape(4, 1, LANES)
    qkv_b = _qk_prep(proj, 18, 1, True, gbq, gbk, tabs, "qk_prep_b")
    o_b, lse_b = _band_fwd(qkv_b, 4, B_WINDOW - 1, sinks, "band_fwd_b")

    mk, mv = _mem_kv(mem, w['mem_norm'], w['w_mem_kv'], w['m_k_norm'], "mem_kv")
    o_m = _mem_attn_fwd(proj, 6, mk, mv, w['m_q_norm'], "mem_attn")

    pa = _mm_rows([(o_a, w['w_o_a'], 0)], "mm_oa", out_dtypes=(BF16,))
    pb = _mm_rows([(o_b, w['w_o_b'], 0)], "mm_ob", out_dtypes=(BF16,))
    pm = _mm_rows([(o_m, w['w_o_m'], 0)], "mm_om", out_dtypes=(BF16,))
    merged = _gate_merge(gates, pa, pb, pm, "gate_merge")
    x1 = _mm_rows([(merged, w['w_out'], 0)], "mm_out", res=x)

    h2, r2 = _rms_fwd(x1, w['ffn_norm'], "rms2")
    u = _mm_rows([(h2, w['w_up'], 0)], "mm_up", out_dtypes=(BF16,))
    act = _conv_glu(u, w['conv_w'], w['conv_b'], "conv_glu")
    y = _mm_rows([(act, w['w_down'], 0)], "mm_down", res=x1)
    dy, dy_b, loss = _loss_head(y, target, "loss_head")

    dact = _mm_rows([(dy_b, w['w_down'], 0)], "mm_d_act", nt=True, out_dtypes=(BF16,))
    grads['w_down'] = _mm_tn(act, dy_b, "mm_dw_down")
    du_a, du_g, dcw_a, dcw_g, dcb_a, dcb_g = _conv_glu_bwd(dact, u, w['conv_w'], w['conv_b'], "conv_glu_bwd")
    grads['conv_w'] = jnp.concatenate([dcw_a, dcw_g], axis=1)
    grads['conv_b'] = jnp.concatenate([dcb_a, dcb_g], axis=1)
    dh2 = _mm_rows([(du_a, w['w_up'], 0), (du_g, w['w_up'], 1)], "mm_d_h2", nt=True)
    grads['w_up'] = jnp.concatenate([_mm_tn(h2, du_a, "mm_dw_up_a"), _mm_tn(h2, du_g, "mm_dw_up_g")], axis=1)
    dx1, dx1_b, grads['ffn_norm'] = _rms_bwd(dh2, x1, r2, w['ffn_norm'], dy, "rms2_bwd", bf16_copy=True)

    dmerged = _mm_rows([(dx1_b, w['w_out'], 0)], "mm_d_merged", nt=True)
    grads['w_out'] = _mm_tn(merged, dx1_b, "mm_dw_out")
    dpa, dpb, dpm, dgpre, grads['b_gate'] = _gate_merge_bwd(dmerged, gates, pa, pb, pm, "gate_merge_bwd")
    do_a = _mm_rows([(dpa, w['w_o_a'], 0)], "mm_d_oa", nt=True)
    do_b = _mm_rows([(dpb, w['w_o_b'], 0)], "mm_d_ob", nt=True)
    do_m = _mm_rows([(dpm, w['w_o_m'], 0)], "mm_d_om", nt=True)
    grads['w_o_a'] = _mm_tn(o_a, dpa, "mm_dw_oa")
    grads['w_o_b'] = _mm_tn(o_b, dpb, "mm_dw_ob")
    grads['w_o_m'] = _mm_tn(o_m, dpm, "mm_dw_om")

    prep = _bwd_prep(do_a, o_a, lse_a, dils, None, "bwd_prep_a")
    dproj, dgq_a, dgk_a = [], [], []
    for gi, (window, d) in enumerate(A_GROUPS):
        gq, gk = _two(w['a_q_norm'][gi]), _two(w['a_k_norm'][gi])
        dqkv = _band_bwd(qkv_a[gi], prep[3 * gi], prep[3 * gi + 1], prep[3 * gi + 2], 2 * d, window // d,
                         f"band_bwd_a{gi}")
        dp, dgq, dgk = _qk_prep_bwd(dqkv, proj, 6 * gi, d, False, gq, gk, tabs, f"qk_prep_bwd_a{gi}")
        dproj.append(dp)
        dgq_a.append(_fold_heads(dgq))
        dgk_a.append(_fold_heads(dgk))
    grads['a_q_norm'] = jnp.stack(dgq_a)
    grads['a_k_norm'] = jnp.stack(dgk_a)

    do_bu, lse_bu, delta_bu, dsink = _bwd_prep(do_b, o_b, lse_b, (1,), sinks, "bwd_prep_b")
    dqkv = _band_bwd(qkv_b, do_bu, lse_bu, delta_bu, 4, B_WINDOW - 1, "band_bwd_b")
    dp_b, dgq, dgk = _qk_prep_bwd(dqkv, proj, 18, 1, True, gbq, gbk, tabs, "qk_prep_bwd_b")
    dproj.append(dp_b)
    grads['b_q_norm'] = _fold_heads(dgq)
    grads['b_k_norm'] = _fold_heads(dgk)
    grads['b_sinks'] = jnp.stack([dsink[:, 0, 0], dsink[:, 0, HEAD_DIM]], axis=1).reshape(8)

    dq_m, dmk, dmv, grads['m_q_norm'] = _mem_attn_bwd(proj, 6, mk, mv, w['m_q_norm'], do_m, "mem_attn_bwd")
    dproj.append(dq_m)
    grads['w_mem_kv'], grads['mem_norm'], grads['m_k_norm'] = _mem_kv_bwd(
        mem, w['mem_norm'], w['w_mem_kv'], w['m_k_norm'], dmk, dmv, "mem_kv_bwd")

    cols = (0, 1, 2, 3, 6)
    grads['w_in'] = jnp.concatenate([_mm_tn(h, dp, f"mm_dw_in{k}") for k, dp in enumerate(dproj)], axis=1)
    grads['w_gate'] = _mm_tn(h, dgpre, "mm_dw_gate")
    dh = _mm_rows([(dp, w['w_in'], c) for dp, c in zip(dproj, cols)] + [(dgpre, w['w_gate'], 0)], "mm_d_h", nt=True)
    grad_x, grads['attn_norm'] = _rms_bwd(dh, x, r1, w['attn_norm'], dx1, "rms1_bwd")
    return loss, grad_x, grads


def _coords():
    return lax.axis_index("x"), lax.axis_index("y"), lax.axis_index("c")


def _slot(p):
    return 4 * p[0] + 2 * p[1] + p[2]


def _peers(me):
    x, y, c = me
    out = []
    for mask in range(1, N_DEV):
        out.append((1 - x if mask & 4 else x, 1 - y if mask & 2 else y, 1 - c if mask & 1 else c))
    return out


HBM_SPEC = pl.BlockSpec(memory_space=pltpu.HBM)


def _all_gather(shards, name):
    n = len(shards)

    def body(*refs):
        ins, outs = refs[:n], refs[n:2 * n]
        send_sems, recv_sems, local_sems = refs[2 * n:]
        x, y, c = _coords()
        me, sibling = (x, y, c), (x, y, 1 - c)
        chips = [(1 - x, y), (x, 1 - y), (1 - x, 1 - y)]

        def copy(a, k, block, to, src=None):
            dst = outs[a].at[_slot(block)]
            return pltpu.make_async_remote_copy(
                src_ref=dst if src is None else src, dst_ref=dst, send_sem=send_sems.at[a, k],
                recv_sem=recv_sems.at[a, k], device_id=to, device_id_type=MESH)

        mine = [pltpu.make_async_copy(ins[a], outs[a].at[_slot(me)], local_sems.at[a]) for a in range(n)]
        for cp in mine:
            cp.start()
        first = []
        for a in range(n):
            first.append(copy(a, 0, me, sibling, src=ins[a]))
            first += [copy(a, 1 + j, me, (*chip, c), src=ins[a]) for j, chip in enumerate(chips)]
        for cp in first:
            cp.start()
        passed = []
        for a in range(n):
            for j, chip in enumerate(chips):
                copy(a, 1 + j, (*chip, c), me).wait_recv()
                fwd = copy(a, 4 + j, (*chip, c), sibling)
                fwd.start()
                passed.append(fwd)
        for a in range(n):
            copy(a, 0, sibling, me).wait_recv()
            for j, chip in enumerate(chips):
                copy(a, 4 + j, (*chip, 1 - c), me).wait_recv()
        for cp in first + passed:
            cp.wait_send()
        for cp in mine:
            cp.wait()

    return pl.pallas_call(
        body, name=name, in_specs=[HBM_SPEC] * n, out_specs=tuple([HBM_SPEC] * n),
        out_shape=tuple(jax.ShapeDtypeStruct((N_DEV,) + s.shape, s.dtype) for s in shards),
        scratch_shapes=[pltpu.SemaphoreType.DMA((n, 7)), pltpu.SemaphoreType.DMA((n, 7)), pltpu.SemaphoreType.DMA((n,))],
    )(*shards)


def _exchange(blocks, name):
    n = len(blocks)

    def body(*refs):
        ins, outs = refs[:n], refs[n:2 * n]
        send_sems, recv_sems, local_sems = refs[2 * n:]
        me = _coords()
        peers = _peers(me)
        mine = [pltpu.make_async_copy(ins[a].at[_slot(me)], outs[a].at[_slot(me)], local_sems.at[a]) for a in range(n)]
        for cp in mine:
            cp.start()

        def copy(a, k):
            return pltpu.make_async_remote_copy(
                src_ref=ins[a].at[_slot(peers[k])], dst_ref=outs[a].at[_slot(me)], send_sem=send_sems.at[a, k],
                recv_sem=recv_sems.at[a, k], device_id=peers[k], device_id_type=MESH)

        def arrival(a, k):
            return pltpu.make_async_remote_copy(
                src_ref=ins[a].at[_slot(me)], dst_ref=outs[a].at[_slot(peers[k])], send_sem=send_sems.at[a, k],
                recv_sem=recv_sems.at[a, k], device_id=peers[k], device_id_type=MESH)

        sends = [copy(a, k) for a in range(n) for k in range(N_DEV - 1)]
        for cp in sends:
            cp.start()
        for a in range(n):
            for k in range(N_DEV - 1):
                arrival(a, k).wait_recv()
        for cp in sends:
            cp.wait_send()
        for cp in mine:
            cp.wait()

    return pl.pallas_call(
        body, name=name, in_specs=[HBM_SPEC] * n, out_specs=tuple([HBM_SPEC] * n),
        out_shape=tuple(jax.ShapeDtypeStruct(b.shape, b.dtype) for b in blocks),
        scratch_shapes=[pltpu.SemaphoreType.DMA((n, 7)), pltpu.SemaphoreType.DMA((n, 7)), pltpu.SemaphoreType.DMA((n,))],
    )(*blocks)


def _all_sum(p, name):
    def body(p_ref, o_ref, recv, send_sems, recv_sems):
        me = _coords()
        peers = _peers(me)
        recv[_slot(me)] = p_ref[...]

        def copy(k, landing):
            return pltpu.make_async_remote_copy(
                src_ref=p_ref, dst_ref=recv.at[_slot(landing)], send_sem=send_sems.at[k], recv_sem=recv_sems.at[k],
                device_id=peers[k], device_id_type=MESH)

        sends = [copy(k, me) for k in range(N_DEV - 1)]
        for cp in sends:
            cp.start()
        for k in range(N_DEV - 1):
            copy(k, peers[k]).wait_recv()
        for cp in sends:
            cp.wait_send()
        acc = recv[0]
        for s in range(1, N_DEV):
            acc = acc + recv[s]
        o_ref[...] = acc

    vmem = pl.BlockSpec(memory_space=pltpu.VMEM)
    return pl.pallas_call(
        body, name=name, in_specs=[vmem], out_specs=vmem, out_shape=jax.ShapeDtypeStruct(p.shape, F32),
        scratch_shapes=[pltpu.VMEM((N_DEV,) + p.shape, F32), pltpu.SemaphoreType.DMA((N_DEV - 1,)),
                        pltpu.SemaphoreType.DMA((N_DEV - 1,))],
    )(p)


def _adam(w, g, m, v):
    m2 = ADAM_B1 * m + (1.0 - ADAM_B1) * g
    v2 = ADAM_B2 * v + (1.0 - ADAM_B2) * (g * g)
    m_hat = m2 / (1.0 - ADAM_B1 ** ADAM_STEP)
    v_hat = v2 / (1.0 - ADAM_B2 ** ADAM_STEP)
    delta = -ADAM_LR * (m_hat / (jnp.sqrt(v_hat) + ADAM_EPS) + ADAM_WD * w)
    return delta, m2, v2


def _row_tile(rows, cols):
    best = rows
    for t in range(16, rows, 16):
        if rows % t == 0 and t * cols * 4 <= (1 << 20):
            best = t
    return best


def _adam_reduce(parts, w, m, v, name):
    rows, cols = w.shape
    tr = _row_tile(rows, cols)

    def body(p_ref, w_ref, m_ref, v_ref, g_ref, d_ref, m2_ref, v2_ref):
        g = p_ref[0].astype(F32)
        for s in range(1, N_DEV):
            g = g + p_ref[s].astype(F32)
        g_ref[...] = g
        d_ref[...], m2_ref[...], v2_ref[...] = _adam(w_ref[...], g, m_ref[...], v_ref[...])

    blk = pl.BlockSpec((tr, cols), lambda i: (i, 0))
    shp = jax.ShapeDtypeStruct((rows, cols), F32)
    return pl.pallas_call(
        body, name=name, grid=(rows // tr,),
        in_specs=[pl.BlockSpec((N_DEV, tr, cols), lambda i: (0, i, 0)), blk, blk, blk],
        out_specs=(blk,) * 4, out_shape=(shp,) * 4, compiler_params=_cparams(1),
    )(parts, w, m, v)


PACK_COLS = 1024
PACK = {'attn_norm': (0, 1, 1024), 'mem_norm': (1, 1, 1024), 'ffn_norm': (2, 1, 1024), 'b_gate': (3, 3, 1024),
        'conv_b': (6, 6, 1024), 'a_q_norm': (12, 3, 64), 'a_k_norm': (15, 3, 64), 'b_q_norm': (18, 1, 64),
        'b_k_norm': (19, 1, 64), 'm_q_norm': (20, 1, 128), 'm_k_norm': (21, 1, 128), 'b_sinks': (22, 1, 8)}
PACK_LOSS_ROW = 23
PACK_ROWS = 24


def _pack_small(grads, loss_tile):
    rows = []
    for name, (r0, nr, lanes) in PACK.items():
        g = grads[name].reshape(-1)
        g = jnp.pad(g, (0, nr * PACK_COLS - g.shape[0])) if lanes == PACK_COLS else \
            jnp.pad(g.reshape(nr, lanes), ((0, 0), (0, PACK_COLS - lanes))).reshape(-1)
        rows.append(g.reshape(nr, PACK_COLS))
    rows.append(jnp.pad(loss_tile[0:1, 0:1], ((0, 0), (0, PACK_COLS - 1))))
    return jnp.concatenate(rows, axis=0)


def _adam_small(gsum, ws, ms, vs, name):
    names = list(PACK)
    n = len(names)

    def body(*refs):
        g_ref = refs[0]
        w_refs, m_refs, v_refs = refs[1:1 + n], refs[1 + n:1 + 2 * n], refs[1 + 2 * n:1 + 3 * n]
        outs = refs[1 + 3 * n:]
        outs[0][...] = g_ref[PACK_LOSS_ROW:PACK_LOSS_ROW + 1, 0:1]
        for k, nm in enumerate(names):
            r0, nr, lanes = PACK[nm]
            o_g, o_d, o_m, o_v = outs[1 + 4 * k:5 + 4 * k]
            for j in range(nr):
                if lanes == PACK_COLS:
                    width = min(PACK_COLS, w_refs[k].shape[1] - j * PACK_COLS)
                    src = (slice(0, 1), slice(j * PACK_COLS, j * PACK_COLS + width))
                else:
                    width = lanes
                    src = (slice(j, j + 1), slice(0, lanes))
                g = g_ref[r0 + j:r0 + j + 1, 0:width]
                d, m2, v2 = _adam(w_refs[k][src], g, m_refs[k][src], v_refs[k][src])
                o_g[src] = g
                o_d[src] = d
                o_m[src] = m2
                o_v[src] = v2

    vmem = pl.BlockSpec(memory_space=pltpu.VMEM)
    shapes = [jax.ShapeDtypeStruct((1, 1), F32)]
    for nm in names:
        shapes += [jax.ShapeDtypeStruct(ws[nm].shape, F32)] * 4
    args = [gsum] + [ws[nm] for nm in names] + [ms[nm] for nm in names] + [vs[nm] for nm in names]
    return pl.pallas_call(
        body, name=name, in_specs=[vmem] * len(args), out_specs=tuple([vmem] * len(shapes)), out_shape=tuple(shapes),
    )(*args)


def _as2d(name, a):
    return a.reshape(a.shape[-2], a.shape[-1]) if a.ndim == 3 else a


def kernel(x, mem, positions, attn_norm, w_in, a_q_norm, a_k_norm, b_q_norm, b_k_norm, b_sinks, mem_norm, w_mem_kv, m_q_norm, m_k_norm, w_o_a, w_o_b, w_o_m, w_gate, b_gate, w_out, ffn_norm, w_up, conv_w, conv_b, w_down, loss_target, m_attn_norm, m_w_in, m_a_q_norm, m_a_k_norm, m_b_q_norm, m_b_k_norm, m_b_sinks, m_mem_norm, m_w_mem_kv, m_m_q_norm, m_m_k_norm, m_w_o_a, m_w_o_b, m_w_o_m, m_w_gate, m_b_gate, m_w_out, m_ffn_norm, m_w_up, m_conv_w, m_conv_b, m_w_down, v_attn_norm, v_w_in, v_a_q_norm, v_a_k_norm, v_b_q_norm, v_b_k_norm, v_b_sinks, v_mem_norm, v_w_mem_kv, v_m_q_norm, v_m_k_norm, v_w_o_a, v_w_o_b, v_w_o_m, v_w_gate, v_b_gate, v_w_out, v_ffn_norm, v_w_up, v_conv_w, v_conv_b, v_w_down):
    given = dict(attn_norm=attn_norm, w_in=w_in, a_q_norm=a_q_norm, a_k_norm=a_k_norm, b_q_norm=b_q_norm, b_k_norm=b_k_norm, b_sinks=b_sinks, mem_norm=mem_norm, w_mem_kv=w_mem_kv, m_q_norm=m_q_norm, m_k_norm=m_k_norm, w_o_a=w_o_a, w_o_b=w_o_b, w_o_m=w_o_m, w_gate=w_gate, b_gate=b_gate, w_out=w_out, ffn_norm=ffn_norm, w_up=w_up, conv_w=conv_w, conv_b=conv_b, w_down=w_down)
    mom1 = dict(attn_norm=m_attn_norm, w_in=m_w_in, a_q_norm=m_a_q_norm, a_k_norm=m_a_k_norm, b_q_norm=m_b_q_norm, b_k_norm=m_b_k_norm, b_sinks=m_b_sinks, mem_norm=m_mem_norm, w_mem_kv=m_w_mem_kv, m_q_norm=m_m_q_norm, m_k_norm=m_m_k_norm, w_o_a=m_w_o_a, w_o_b=m_w_o_b, w_o_m=m_w_o_m, w_gate=m_w_gate, b_gate=m_b_gate, w_out=m_w_out, ffn_norm=m_ffn_norm, w_up=m_w_up, conv_w=m_conv_w, conv_b=m_conv_b, w_down=m_w_down)
    mom2 = dict(attn_norm=v_attn_norm, w_in=v_w_in, a_q_norm=v_a_q_norm, a_k_norm=v_a_k_norm, b_q_norm=v_b_q_norm, b_k_norm=v_b_k_norm, b_sinks=v_b_sinks, mem_norm=v_mem_norm, w_mem_kv=v_w_mem_kv, m_q_norm=v_m_q_norm, m_k_norm=v_m_k_norm, w_o_a=v_w_o_a, w_o_b=v_w_o_b, w_o_m=v_w_o_m, w_gate=v_w_gate, b_gate=v_b_gate, w_out=v_w_out, ffn_norm=v_ffn_norm, w_up=v_w_up, conv_w=v_conv_w, conv_b=v_conv_b, w_down=v_w_down)

    big = list(BIG)
    shards = [given[n][0] if n == 'conv_w' else given[n][0].astype(BF16) for n in big]
    gathered = _all_gather(shards, "gather_weights")
    w = {}
    for n, g in zip(big, gathered):
        _, r, c = g.shape
        w[n] = g.reshape(N_DEV * r, c) if BIG[n] == 0 else g.transpose(1, 0, 2).reshape(r, N_DEV * c)
    for n in SMALL:
        w[n] = given[n]
    w['a_q_norm'], w['a_k_norm'] = given['a_q_norm'][0], given['a_k_norm'][0]
    w['b_q_norm'], w['b_k_norm'], w['b_sinks'] = given['b_q_norm'][0], given['b_k_norm'][0], given['b_sinks'][0]

    loss_tile, grad_x, grads = _device_step(x[0], mem[0], positions[0], loss_target[0], w)

    blocks = []
    for n in big:
        g = grads[n]
        r, c = given[n].shape[1:]
        g = g.reshape(N_DEV, r, c) if BIG[n] == 0 else g.reshape(r, N_DEV, c).transpose(1, 0, 2)
        blocks.append(g if n == 'conv_w' else g.astype(BF16))
    parts = _exchange(blocks, "exchange_grads")
    out = {}
    for n, p in zip(big, parts):
        res = _adam_reduce(p, given[n][0], mom1[n][0], mom2[n][0], f"adam_{n}")
        out[n] = tuple(t[None] for t in res)

    gsum = _all_sum(_pack_small(grads, loss_tile), "sum_small")
    ws = {n: _as2d(n, given[n]) for n in PACK}
    ms = {n: _as2d(n, mom1[n]) for n in PACK}
    vs = {n: _as2d(n, mom2[n]) for n in PACK}
    res = _adam_small(gsum, ws, ms, vs, "adam_small")
    loss = res[0].reshape(())
    for k, n in enumerate(PACK):
        out[n] = tuple(t.reshape(given[n].shape) for t in res[1 + 4 * k:5 + 4 * k])

    outs = [loss, grad_x[None]]
    for field in range(4):
        outs += [out[n][field] for n in WEIGHTS]
    return tuple(outs)
```

```python
import functools
import math

import jax
import jax.numpy as jnp
from jax import lax
from jax.experimental import pallas as pl
from jax.experimental.pallas import tpu as pltpu

F32 = jnp.float32
BF16 = jnp.bfloat16

N_DEV = 8
D_MODEL = 1024
HEAD_DIM = 64
A_GROUPS = ((128, 1), (512, 4), (2048, 16))
B_WINDOW = 128
M_HEADS = 4
M_HEAD_DIM = 128
MEM_LEN = 256
D_FF = 2816
ROPE_THETA = 500000.0
ROPE_DIMS = 16
BLOCK = 128
EPS = 1e-6
LANES = 128
BAND_Q_BLOCKS = 4
BAND_UNITS = 2

ADAM_LR = 0.001
ADAM_B1 = 0.9
ADAM_B2 = 0.999
ADAM_EPS = 1e-08
ADAM_WD = 0.01
ADAM_STEP = 10

VMEM_LIMIT_BYTES = 56 * 1024 * 1024
MESH = pl.DeviceIdType.MESH

WEIGHTS = ['attn_norm', 'w_in', 'a_q_norm', 'a_k_norm', 'b_q_norm', 'b_k_norm', 'b_sinks', 'mem_norm',
           'w_mem_kv', 'm_q_norm', 'm_k_norm', 'w_o_a', 'w_o_b', 'w_o_m', 'w_gate', 'b_gate', 'w_out',
           'ffn_norm', 'w_up', 'conv_w', 'conv_b', 'w_down']
BIG = {'w_in': 1, 'w_mem_kv': 0, 'w_o_a': 1, 'w_o_b': 1, 'w_o_m': 1, 'w_gate': 1, 'w_out': 0, 'w_up': 1,
       'conv_w': 1, 'w_down': 0}
SMALL = [n for n in WEIGHTS if n not in BIG]


def _cparams(n_grid):
    return pltpu.CompilerParams(dimension_semantics=("arbitrary",) * n_grid, vmem_limit_bytes=VMEM_LIMIT_BYTES)


def _pick(n, cands=(512, 256, 128)):
    for c in cands:
        if n % c == 0:
            return c
    return n


def _seg_matrix(width):
    shift = width.bit_length() - 1
    r = lax.shift_right_logical(lax.broadcasted_iota(jnp.int32, (LANES, LANES), 0), shift)
    c = lax.shift_right_logical(lax.broadcasted_iota(jnp.int32, (LANES, LANES), 1), shift)
    return jnp.where(r == c, 1.0, 0.0).astype(BF16)


def _seg_sum(x, seg):
    hi = x.astype(BF16)
    r1 = x - hi.astype(F32)
    mid = r1.astype(BF16)
    lo = (r1 - mid.astype(F32)).astype(BF16)
    dot = functools.partial(jnp.dot, preferred_element_type=F32)
    return dot(hi, seg) + dot(mid, seg) + dot(lo, seg)


def _rope(y, c, s1, s2):
    return y * c + pltpu.roll(y, LANES - ROPE_DIMS // 2, 1) * s1 + pltpu.roll(y, ROPE_DIMS // 2, 1) * s2


def _unrope(dy, c, s1, s2):
    return dy * c + pltpu.roll(dy * s1, ROPE_DIMS // 2, 1) + pltpu.roll(dy * s2, LANES - ROPE_DIMS // 2, 1)


def _sigmoid(x):
    return 1.0 / (1.0 + jnp.exp(-x))


def _rms_fwd(x, gain, name):
    s_len, d = x.shape
    tm = 512

    def body(x_ref, g_ref, h_ref, r_ref):
        xv = x_ref[...]
        r = lax.rsqrt(jnp.mean(xv * xv, axis=-1, keepdims=True) + EPS)
        h_ref[...] = ((xv * r) * g_ref[...]).astype(BF16)
        r_ref[...] = r

    return pl.pallas_call(
        body, name=name, grid=(s_len // tm,),
        in_specs=[pl.BlockSpec((tm, d), lambda i: (i, 0)), pl.BlockSpec((1, d), lambda i: (0, 0))],
        out_specs=(pl.BlockSpec((tm, d), lambda i: (i, 0)), pl.BlockSpec((tm, 1), lambda i: (i, 0))),
        out_shape=(jax.ShapeDtypeStruct((s_len, d), BF16), jax.ShapeDtypeStruct((s_len, 1), F32)),
        compiler_params=_cparams(1),
    )(x, gain)


def _rms_bwd(dh, x, r, gain, add, name, bf16_copy=False):
    s_len, d = x.shape
    tm = 512

    def body(dh_ref, x_ref, r_ref, g_ref, add_ref, dx_ref, *rest):
        dg_ref = rest[-1]

        @pl.when(pl.program_id(0) == 0)
        def _():
            dg_ref[...] = jnp.zeros_like(dg_ref)
        rv = r_ref[...]
        xhat = x_ref[...] * rv
        dhv = dh_ref[...]
        dg_ref[...] += jnp.sum(dhv * xhat, axis=0, keepdims=True)
        dxhat = dhv * g_ref[...]
        dx = add_ref[...] + rv * (dxhat - xhat * jnp.mean(dxhat * xhat, axis=-1, keepdims=True))
        dx_ref[...] = dx
        if bf16_copy:
            rest[0][...] = dx.astype(BF16)

    row = pl.BlockSpec((tm, d), lambda i: (i, 0))
    vec = pl.BlockSpec((1, d), lambda i: (0, 0))
    out_specs = [row] + ([row] if bf16_copy else []) + [vec]
    out_shape = [jax.ShapeDtypeStruct((s_len, d), F32)] + ([jax.ShapeDtypeStruct((s_len, d), BF16)] if bf16_copy else [])
    out_shape.append(jax.ShapeDtypeStruct((1, d), F32))
    return pl.pallas_call(
        body, name=name, grid=(s_len // tm,),
        in_specs=[row, row, pl.BlockSpec((tm, 1), lambda i: (i, 0)), vec, row],
        out_specs=tuple(out_specs), out_shape=tuple(out_shape), compiler_params=_cparams(1),
    )(dh, x, r, gain, add)


def _resident(shape, index_map):
    return pl.BlockSpec(shape, index_map, pipeline_mode=pl.Buffered(1))


def _mm_rows(pairs, name, nt=False, tm=512, bias=None, sigmoid=False, res=None, out_dtypes=(F32,)):
    m = pairs[0][0].shape[0]
    n = pairs[0][1].shape[0] if nt else pairs[0][1].shape[1]
    n_pairs = len(pairs)
    has_bias, has_res = bias is not None, res is not None
    dims = (((1,), (1,)), ((), ())) if nt else (((1,), (0,)), ((), ()))

    def body(*refs):
        acc = None
        for p in range(n_pairs):
            t = lax.dot_general(refs[2 * p][...].astype(BF16), refs[2 * p + 1][...], dims, preferred_element_type=F32)
            acc = t if acc is None else acc + t
        pos = 2 * n_pairs
        if has_bias:
            acc = acc + refs[pos][...]
            pos += 1
        if sigmoid:
            acc = _sigmoid(acc)
        if has_res:
            acc = refs[pos][...] + acc
            pos += 1
        for o_ref in refs[pos:]:
            o_ref[...] = acc.astype(o_ref.dtype)

    in_specs, args = [], []
    for a, w, blk in pairs:
        k = a.shape[1]
        in_specs.append(pl.BlockSpec((tm, k), lambda i: (i, 0)))
        if nt:
            in_specs.append(_resident((n, k), lambda i, blk=blk: (0, blk)))
        else:
            in_specs.append(_resident((k, n), lambda i, blk=blk: (blk, 0)))
        args += [a, w]
    if has_bias:
        in_specs.append(_resident((1, n), lambda i: (0, 0)))
        args.append(bias)
    if has_res:
        in_specs.append(pl.BlockSpec((tm, n), lambda i: (i, 0)))
        args.append(res)
    out = pl.BlockSpec((tm, n), lambda i: (i, 0))
    outs = pl.pallas_call(
        body, name=name, grid=(m // tm,), in_specs=in_specs, out_specs=tuple([out] * len(out_dtypes)),
        out_shape=tuple(jax.ShapeDtypeStruct((m, n), dt) for dt in out_dtypes), compiler_params=_cparams(1),
    )(*args)
    return outs[0] if len(out_dtypes) == 1 else outs


def _mm_tn(a, b, name, tile=256):
    k, m = a.shape
    n = b.shape[1]
    dims = (((0,), (0,)), ((), ()))

    def body(a_ref, b_ref, o_ref):
        o_ref[...] = lax.dot_general(a_ref[...].astype(BF16), b_ref[...].astype(BF16), dims, preferred_element_type=F32)

    if n <= m:
        t = min(tile, m)
        grid, a_spec, b_spec = (m // t,), pl.BlockSpec((k, t), lambda i: (0, i)), _resident((k, n), lambda i: (0, 0))
        o_spec = pl.BlockSpec((t, n), lambda i: (i, 0))
    else:
        t = min(tile, n)
        grid, a_spec, b_spec = (n // t,), _resident((k, m), lambda i: (0, 0)), pl.BlockSpec((k, t), lambda i: (0, i))
        o_spec = pl.BlockSpec((m, t), lambda i: (0, i))
    return pl.pallas_call(
        body, name=name, grid=grid, in_specs=[a_spec, b_spec], out_specs=o_spec,
        out_shape=jax.ShapeDtypeStruct((m, n), F32), compiler_params=_cparams(1),
    )(a, b)


def _norm_rope(t, gain, c, s1, s2, seg):
    rs = lax.rsqrt(_seg_sum(t * t, seg) * (1.0 / HEAD_DIM) + EPS)
    return _rope((t * rs) * gain, c, s1, s2)


def _dup_half(y, half):
    lane = lax.broadcasted_iota(jnp.int32, y.shape, 1)
    rolled = pltpu.roll(y, HEAD_DIM, 1)
    keep = (lane < HEAD_DIM) if half == 0 else (lane >= HEAD_DIM)
    return jnp.where(keep, y, rolled)


def _qk_prep(proj, cb0, d, gqa, gq, gk, tabs, name):
    s_len = proj.shape[0]
    tm = 512
    rows = tm // d
    n_units = 4 if gqa else 2 * d
    n_q = 4 if gqa else 2
    n_in = 6

    def body(*refs):
        in_refs = refs[:n_in]
        gq_ref, gk_ref, c_ref, s1_ref, s2_ref, o_ref = refs[n_in:]
        seg = _seg_matrix(HEAD_DIM)

        def rows_of(ref, r):
            return ref[...] if d == 1 else ref[pl.ds(r, rows, stride=d), :]

        def put(unit_col, y):
            o_ref[:, unit_col * LANES:(unit_col + 1) * LANES] = y.astype(BF16)

        for r in range(d):
            c, s1, s2 = rows_of(c_ref, r), rows_of(s1_ref, r), rows_of(s2_ref, r)
            for b in range(n_in):
                t = rows_of(in_refs[b], r)
                if b < n_q:
                    put((b * d + r) if not gqa else b, _norm_rope(t, gq_ref[...], c, s1, s2, seg))
                elif not gqa:
                    sec, pair = (1, b - 2) if b < 4 else (2, b - 4)
                    y = _norm_rope(t, gk_ref[...], c, s1, s2, seg) if sec == 1 else t
                    put(sec * n_units + pair * d + r, y)
                else:
                    sec = 1 if b == 4 else 2
                    y = _norm_rope(t, gk_ref[...], c, s1, s2, seg) if sec == 1 else t
                    for u in range(n_units):
                        put(sec * n_units + u, _dup_half(y, u // 2))

    in_specs = [pl.BlockSpec((tm, LANES), lambda i, b=b: (i, cb0 + b)) for b in range(n_in)]
    vec = pl.BlockSpec((1, LANES), lambda i: (0, 0))
    tab = pl.BlockSpec((tm, LANES), lambda i: (i, 0))
    width = 3 * n_units * LANES
    return pl.pallas_call(
        body, name=name, grid=(s_len // tm,), in_specs=in_specs + [vec, vec, tab, tab, tab],
        out_specs=pl.BlockSpec((rows, width), lambda i: (i, 0)),
        out_shape=jax.ShapeDtypeStruct((s_len // d, width), BF16), compiler_params=_cparams(1),
    )(*([proj] * n_in), gq, gk, *tabs)


def _qk_prep_bwd(dqkv, proj, cb0, d, gqa, gq, gk, tabs, name):
    s_len = proj.shape[0]
    tm = 512
    rows = tm // d
    n_units = 4 if gqa else 2 * d
    n_q = 4 if gqa else 2
    n_in = 6

    def body(*refs):
        d_refs = refs[0:3]
        in_refs = refs[3:3 + n_in]
        gq_ref, gk_ref, c_ref, s1_ref, s2_ref, o_ref, dgq_ref, dgk_ref, stage = refs[3 + n_in:]
        seg = _seg_matrix(HEAD_DIM)

        @pl.when(pl.program_id(0) == 0)
        def _():
            dgq_ref[...] = jnp.zeros_like(dgq_ref)
            dgk_ref[...] = jnp.zeros_like(dgk_ref)

        def rows_of(ref, r):
            return ref[...] if d == 1 else ref[pl.ds(r, rows, stride=d), :]

        def unit(col):
            sec, u = divmod(col, n_units)
            return d_refs[sec][:, u * LANES:(u + 1) * LANES]

        def norm_bwd(dyr, t, gain, c, s1, s2, dg_ref):
            rs = lax.rsqrt(_seg_sum(t * t, seg) * (1.0 / HEAD_DIM) + EPS)
            that = t * rs
            dy = _unrope(dyr, c, s1, s2)
            dg_ref[...] += jnp.sum(dy * that, axis=0, keepdims=True)
            dthat = dy * gain
            return rs * (dthat - that * (_seg_sum(dthat * that, seg) * (1.0 / HEAD_DIM)))

        def fold(sec):
            tot = []
            for u in range(n_units):
                v = unit(sec * n_units + u)
                tot.append(v + pltpu.roll(v, HEAD_DIM, 1))
            lane = lax.broadcasted_iota(jnp.int32, tot[0].shape, 1)
            return jnp.where(lane < HEAD_DIM, tot[0] + tot[1], tot[2] + tot[3])

        for b in range(n_in):
            for r in range(d):
                c, s1, s2 = rows_of(c_ref, r), rows_of(s1_ref, r), rows_of(s2_ref, r)
                t = rows_of(in_refs[b], r)
                if b < n_q:
                    g = unit((b * d + r) if not gqa else b)
                    out = norm_bwd(g, t, gq_ref[...], c, s1, s2, dgq_ref)
                elif not gqa:
                    sec, pair = (1, b - 2) if b < 4 else (2, b - 4)
                    g = unit(sec * n_units + pair * d + r)
                    out = norm_bwd(g, t, gk_ref[...], c, s1, s2, dgk_ref) if sec == 1 else g
                else:
                    sec = 1 if b == 4 else 2
                    g = fold(sec)
                    out = norm_bwd(g, t, gk_ref[...], c, s1, s2, dgk_ref) if sec == 1 else g
                if d == 1:
                    o_ref[:, b * LANES:(b + 1) * LANES] = out.astype(BF16)
                else:
                    stage[pl.ds(r, rows, stride=d), :] = out
            if d != 1:
                o_ref[:, b * LANES:(b + 1) * LANES] = stage[...].astype(BF16)

    in_specs = [pl.BlockSpec((rows, n_units * LANES), lambda i: (i, 0))] * 3
    in_specs += [pl.BlockSpec((tm, LANES), lambda i, b=b: (i, cb0 + b)) for b in range(n_in)]
    vec = pl.BlockSpec((1, LANES), lambda i: (0, 0))
    tab = pl.BlockSpec((tm, LANES), lambda i: (i, 0))
    return pl.pallas_call(
        body, name=name, grid=(s_len // tm,), in_specs=in_specs + [vec, vec, tab, tab, tab],
        out_specs=(pl.BlockSpec((tm, n_in * LANES), lambda i: (i, 0)), vec, vec),
        out_shape=(jax.ShapeDtypeStruct((s_len, n_in * LANES), BF16), jax.ShapeDtypeStruct((1, LANES), F32),
                   jax.ShapeDtypeStruct((1, LANES), F32)),
        scratch_shapes=[pltpu.VMEM((tm, LANES), F32)], compiler_params=_cparams(1),
    )(*dqkv, *([proj] * n_in), gq, gk, *tabs)


def _head_masks(shape):
    lane = lax.broadcasted_iota(jnp.int32, shape, 1)
    return lane < HEAD_DIM, lane >= HEAD_DIM


def _band_fwd(qkv, n_units, max_dist, sinks, name):
    n_rows = qkv.shape[0]
    nb = n_rows // BLOCK
    scale = HEAD_DIM ** -0.5
    has_sink = sinks is not None

    qn, un = min(nb, BAND_Q_BLOCKS), BAND_UNITS
    ug = n_units // un

    def body(*refs):
        q_ref, kp_ref, km_ref, vp_ref, vm_ref = refs[:5]
        o_ref, lse_ref = refs[-2:]
        i = pl.program_id(1)
        qi = lax.broadcasted_iota(jnp.int32, (BLOCK, 2 * BLOCK), 0)
        kj = lax.broadcasted_iota(jnp.int32, (BLOCK, 2 * BLOCK), 1)
        dist = qi + BLOCK - kj
        band = (dist >= 0) & (dist <= max_dist)
        band_first = band & ((i > 0) | (kj >= BLOCK))
        m0, m1 = _head_masks((BLOCK, LANES))
        zero = jnp.zeros((BLOCK, LANES), BF16)
        for ub in range(un):
            cs = slice(ub * LANES, (ub + 1) * LANES)
            for qb in range(qn):
                rs = slice(qb * BLOCK, (qb + 1) * BLOCK)
                q = q_ref[rs, cs]
                if qb == 0:
                    kk = jnp.concatenate([kp_ref[:, cs], km_ref[0:BLOCK, cs]], axis=0)
                    vv = jnp.concatenate([vp_ref[:, cs], vm_ref[0:BLOCK, cs]], axis=0)
                    valid = band_first
                else:
                    kk = km_ref[(qb - 1) * BLOCK:(qb + 1) * BLOCK, cs]
                    vv = vm_ref[(qb - 1) * BLOCK:(qb + 1) * BLOCK, cs]
                    valid = band
                outs, lses = [], []
                for e, hm in enumerate((m0, m1)):
                    qe = jnp.where(hm, q, zero)
                    s = lax.dot_general(qe, kk, (((1,), (1,)), ((), ())), preferred_element_type=F32) * scale
                    s = jnp.where(valid, s, -jnp.inf)
                    mx = jnp.max(s, axis=-1, keepdims=True)
                    if has_sink:
                        sk = refs[5][ub][:, e * HEAD_DIM:e * HEAD_DIM + 1]
                        mx = jnp.maximum(mx, sk)
                    p = jnp.exp(s - mx)
                    den = jnp.sum(p, axis=-1, keepdims=True)
                    if has_sink:
                        den = den + jnp.exp(sk - mx)
                    pn = (p * (1.0 / den)).astype(BF16)
                    outs.append(jnp.dot(pn, vv, preferred_element_type=F32))
                    lses.append(mx + jnp.log(den))
                o_ref[rs, cs] = jnp.where(m0, outs[0], outs[1])
                lse_ref[rs, cs] = jnp.where(m0, jnp.broadcast_to(lses[0], (BLOCK, LANES)),
                                            jnp.broadcast_to(lses[1], (BLOCK, LANES)))

    def main(sec):
        return pl.BlockSpec((qn * BLOCK, un * LANES), lambda u, i: (i, sec * ug + u))

    def prev(sec):
        return pl.BlockSpec((BLOCK, un * LANES), lambda u, i: (jnp.maximum(i * qn - 1, 0), sec * ug + u))

    in_specs = [main(0), prev(1), main(1), prev(2), main(2)]
    args = [qkv] * 5
    if has_sink:
        in_specs.append(pl.BlockSpec((un, 1, LANES), lambda u, i: (u, 0, 0)))
        args.append(sinks)
    return pl.pallas_call(
        body, name=name, grid=(ug, nb // qn), in_specs=in_specs, out_specs=(main(0), main(0)),
        out_shape=(jax.ShapeDtypeStruct((n_rows, n_units * LANES), F32),) * 2, compiler_params=_cparams(2),
    )(*args)


def _band_bwd(qkv, do, lse, delta, n_units, max_dist, name):
    n_rows = qkv.shape[0]
    nb = n_rows // BLOCK
    scale = HEAD_DIM ** -0.5

    qn, un = min(nb, BAND_Q_BLOCKS), BAND_UNITS
    ug = n_units // un
    steps = nb // qn
    nt_dims = (((1,), (1,)), ((), ()))
    tn_dims = (((0,), (0,)), ((), ()))

    def body(qm_ref, qx_ref, kp_ref, km_ref, vp_ref, vm_ref, dom_ref, dox_ref, lm_ref, lx_ref, dm_ref, dx_ref,
             dq_ref, dk_ref, dv_ref):
        i = pl.program_id(1)
        m0, m1 = _head_masks((BLOCK, LANES))
        zero = jnp.zeros((BLOCK, LANES), BF16)
        qi = lax.broadcasted_iota(jnp.int32, (BLOCK, 2 * BLOCK), 0)
        kj = lax.broadcasted_iota(jnp.int32, (BLOCK, 2 * BLOCK), 1)
        dist = qi + BLOCK - kj
        band = (dist >= 0) & (dist <= max_dist)
        band_first = band & ((i > 0) | (kj >= BLOCK))
        qr = lax.broadcasted_iota(jnp.int32, (2 * BLOCK, BLOCK), 0)
        kc = lax.broadcasted_iota(jnp.int32, (2 * BLOCK, BLOCK), 1)
        dist2 = qr - kc
        band2 = (dist2 >= 0) & (dist2 <= max_dist)
        band2_last = band2 & ((qr < BLOCK) | (i < steps - 1))
        m0w, m1w = _head_masks((2 * BLOCK, LANES))
        zero2 = jnp.zeros((2 * BLOCK, LANES), BF16)

        def two(main_ref, next_ref, kb, cs):
            if kb < qn - 1:
                return main_ref[kb * BLOCK:(kb + 2) * BLOCK, cs]
            return jnp.concatenate([main_ref[kb * BLOCK:(kb + 1) * BLOCK, cs], next_ref[:, cs]], axis=0)

        for ub in range(un):
            cs = slice(ub * LANES, (ub + 1) * LANES)
            for qb in range(qn):
                rs = slice(qb * BLOCK, (qb + 1) * BLOCK)
                q = qm_ref[rs, cs]
                dob = dom_ref[rs, cs]
                lse_b = lm_ref[rs, cs]
                del_b = dm_ref[rs, cs]
                if qb == 0:
                    kk = jnp.concatenate([kp_ref[:, cs], km_ref[0:BLOCK, cs]], axis=0)
                    vv = jnp.concatenate([vp_ref[:, cs], vm_ref[0:BLOCK, cs]], axis=0)
                    valid = band_first
                else:
                    kk = km_ref[(qb - 1) * BLOCK:(qb + 1) * BLOCK, cs]
                    vv = vm_ref[(qb - 1) * BLOCK:(qb + 1) * BLOCK, cs]
                    valid = band
                dqs = []
                for e, hm in enumerate((m0, m1)):
                    col = slice(e * HEAD_DIM, e * HEAD_DIM + 1)
                    s = lax.dot_general(jnp.where(hm, q, zero), kk, nt_dims, preferred_element_type=F32) * scale
                    p = jnp.where(valid, jnp.exp(s - lse_b[:, col]), 0.0)
                    dp = lax.dot_general(jnp.where(hm, dob, zero), vv, nt_dims, preferred_element_type=F32)
                    ds = (p * (dp - del_b[:, col]) * scale).astype(BF16)
                    dqs.append(jnp.dot(ds, kk, preferred_element_type=F32))
                dq_ref[rs, cs] = jnp.where(m0, dqs[0], dqs[1])
            for kb in range(qn):
                rs = slice(kb * BLOCK, (kb + 1) * BLOCK)
                qq = two(qm_ref, qx_ref, kb, cs)
                dd = two(dom_ref, dox_ref, kb, cs)
                ll = two(lm_ref, lx_ref, kb, cs)
                de = two(dm_ref, dx_ref, kb, cs)
                k = km_ref[rs, cs]
                v = vm_ref[rs, cs]
                valid2 = band2 if kb < qn - 1 else band2_last
                dk = jnp.zeros((BLOCK, LANES), F32)
                dv = jnp.zeros((BLOCK, LANES), F32)
                for e, hm in enumerate((m0w, m1w)):
                    col = slice(e * HEAD_DIM, e * HEAD_DIM + 1)
                    qe = jnp.where(hm, qq, zero2)
                    doe = jnp.where(hm, dd, zero2)
                    s = lax.dot_general(qe, k, nt_dims, preferred_element_type=F32) * scale
                    p = jnp.where(valid2, jnp.exp(s - ll[:, col]), 0.0)
                    dp = lax.dot_general(doe, v, nt_dims, preferred_element_type=F32)
                    ds = (p * (dp - de[:, col]) * scale).astype(BF16)
                    dk = dk + lax.dot_general(ds, qe, tn_dims, preferred_element_type=F32)
                    dv = dv + lax.dot_general(p.astype(BF16), doe, tn_dims, preferred_element_type=F32)
                dk_ref[rs, cs] = dk
                dv_ref[rs, cs] = dv

    def main(sec):
        return pl.BlockSpec((qn * BLOCK, un * LANES), lambda u, i: (i, sec * ug + u))

    def prev(sec):
        return pl.BlockSpec((BLOCK, un * LANES), lambda u, i: (jnp.maximum(i * qn - 1, 0), sec * ug + u))

    def nxt(sec):
        return pl.BlockSpec((BLOCK, un * LANES), lambda u, i: (jnp.minimum((i + 1) * qn, nb - 1), sec * ug + u))

    in_specs = [main(0), nxt(0), prev(1), main(1), prev(2), main(2),
                main(0), nxt(0), main(0), nxt(0), main(0), nxt(0)]
    args = [qkv] * 6 + [do, do, lse, lse, delta, delta]
    shp = jax.ShapeDtypeStruct((n_rows, n_units * LANES), F32)
    return pl.pallas_call(
        body, name=name, grid=(ug, steps), in_specs=in_specs, out_specs=(main(0), main(0), main(0)),
        out_shape=(shp, shp, shp), compiler_params=_cparams(2),
    )(*args)


def _merge_groups(os_, lses, dils, name):
    s_len = os_[0].shape[0] * dils[0]
    tm = 512

    def body(*refs):
        o_refs, l_refs = refs[0:3], refs[3:6]
        o_ref, lse_ref = refs[6:8]
        so, sl = refs[8:11], refs[11:14]
        for pair in range(2):
            for g, d in enumerate(dils):
                rows = tm // d
                for r in range(d):
                    col = slice((pair * d + r) * LANES, (pair * d + r + 1) * LANES)
                    if d == 1:
                        so[g][...] = o_refs[g][:, col]
                        sl[g][...] = l_refs[g][:, col]
                    else:
                        so[g][pl.ds(r, rows, stride=d), :] = o_refs[g][:, col]
                        sl[g][pl.ds(r, rows, stride=d), :] = l_refs[g][:, col]
            l0, l1, l2 = sl[0][...], sl[1][...], sl[2][...]
            mx = jnp.maximum(jnp.maximum(l0, l1), l2)
            e0, e1, e2 = jnp.exp(l0 - mx), jnp.exp(l1 - mx), jnp.exp(l2 - mx)
            den = e0 + e1 + e2
            inv = 1.0 / den
            o_ref[:, pair * LANES:(pair + 1) * LANES] = (so[0][...] * (e0 * inv) + so[1][...] * (e1 * inv)
                                                         + so[2][...] * (e2 * inv))
            lse_ref[:, pair * LANES:(pair + 1) * LANES] = mx + jnp.log(den)

    in_specs = [pl.BlockSpec((tm // d, 2 * d * LANES), lambda i: (i, 0)) for d in dils] * 2
    out = pl.BlockSpec((tm, 2 * LANES), lambda i: (i, 0))
    shp = jax.ShapeDtypeStruct((s_len, 2 * LANES), F32)
    return pl.pallas_call(
        body, name=name, grid=(s_len // tm,), in_specs=in_specs, out_specs=(out, out), out_shape=(shp, shp),
        scratch_shapes=[pltpu.VMEM((tm, LANES), F32)] * 6, compiler_params=_cparams(1),
    )(*os_, *lses)


def _bwd_prep(do, o, lse, dils, sinks, name):
    s_len, width = do.shape
    n_pairs = width // LANES
    tm = 512
    has_sink = sinks is not None
    n_g = len(dils)

    def body(*refs):
        do_ref, o_ref, lse_ref = refs[:3]
        pos = 3
        if has_sink:
            sink_ref = refs[pos]
            pos += 1
        outs = refs[pos:pos + 3 * n_g]
        pos += 3 * n_g
        if has_sink:
            dsink_ref = refs[pos]
            pos += 1
        s_do, s_l, s_d = refs[pos:pos + 3]
        seg = _seg_matrix(HEAD_DIM)

        if has_sink:
            @pl.when(pl.program_id(0) == 0)
            def _():
                dsink_ref[...] = jnp.zeros_like(dsink_ref)

        for pair in range(n_pairs):
            col = slice(pair * LANES, (pair + 1) * LANES)
            dov = do_ref[:, col]
            lv = lse_ref[:, col]
            delta = _seg_sum(dov * o_ref[:, col], seg)
            if has_sink:
                dsink_ref[pair] += -jnp.sum(jnp.exp(sink_ref[pair] - lv) * delta, axis=0, keepdims=True)
            s_do[...] = dov
            s_l[...] = lv
            s_d[...] = delta
            for g, d in enumerate(dils):
                rows = tm // d
                for r in range(d):
                    oc = slice((pair * d + r) * LANES, (pair * d + r + 1) * LANES)
                    if d == 1:
                        a, b, c = s_do[...], s_l[...], s_d[...]
                    else:
                        a = s_do[pl.ds(r, rows, stride=d), :]
                        b = s_l[pl.ds(r, rows, stride=d), :]
                        c = s_d[pl.ds(r, rows, stride=d), :]
                    outs[3 * g][:, oc] = a.astype(BF16)
                    outs[3 * g + 1][:, oc] = b
                    outs[3 * g + 2][:, oc] = c

    row = pl.BlockSpec((tm, width), lambda i: (i, 0))
    in_specs = [row, row, row]
    args = [do, o, lse]
    if has_sink:
        in_specs.append(pl.BlockSpec((n_pairs, 1, LANES), lambda i: (0, 0, 0)))
        args.append(sinks)
    out_specs, out_shape = [], []
    for d in dils:
        for dt in (BF16, F32, F32):
            out_specs.append(pl.BlockSpec((tm // d, n_pairs * d * LANES), lambda i: (i, 0)))
            out_shape.append(jax.ShapeDtypeStruct((s_len // d, n_pairs * d * LANES), dt))
    if has_sink:
        out_specs.append(pl.BlockSpec((n_pairs, 1, LANES), lambda i: (0, 0, 0)))
        out_shape.append(jax.ShapeDtypeStruct((n_pairs, 1, LANES), F32))
    return pl.pallas_call(
        body, name=name, grid=(s_len // tm,), in_specs=in_specs, out_specs=tuple(out_specs),
        out_shape=tuple(out_shape), scratch_shapes=[pltpu.VMEM((tm, LANES), F32)] * 3, compiler_params=_cparams(1),
    )(*args)


def _mem_kv(mem, mem_gain, w_kv, k_gain, name):
    m_len = mem.shape[0]
    kw = M_HEADS * M_HEAD_DIM

    def body(mem_ref, mg_ref, w_ref, kg_ref, k_ref, v_ref):
        mv = mem_ref[...]
        r = lax.rsqrt(jnp.mean(mv * mv, axis=-1, keepdims=True) + EPS)
        mn = ((mv * r) * mg_ref[...]).astype(BF16)
        kv = jnp.dot(mn, w_ref[...], preferred_element_type=F32)
        for h in range(M_HEADS):
            col = slice(h * M_HEAD_DIM, (h + 1) * M_HEAD_DIM)
            t = kv[:, col]
            rk = lax.rsqrt(jnp.mean(t * t, axis=-1, keepdims=True) + EPS)
            k_ref[:, col] = ((t * rk) * kg_ref[...]).astype(BF16)
        v_ref[...] = kv[:, kw:].astype(BF16)

    shp = jax.ShapeDtypeStruct((m_len, kw), BF16)
    return pl.pallas_call(body, name=name, out_shape=(shp, shp),
                          compiler_params=pltpu.CompilerParams(vmem_limit_bytes=VMEM_LIMIT_BYTES))(mem, mem_gain, w_kv, k_gain)


def _mem_kv_bwd(mem, mem_gain, w_kv, k_gain, dk, dv, name):
    m_len, d = mem.shape
    kw = M_HEADS * M_HEAD_DIM

    def body(mem_ref, mg_ref, w_ref, kg_ref, dk_ref, dv_ref, dw_ref, dmg_ref, dkg_ref, dkv_ref):
        mv = mem_ref[...]
        r = lax.rsqrt(jnp.mean(mv * mv, axis=-1, keepdims=True) + EPS)
        mhat = mv * r
        mn = (mhat * mg_ref[...]).astype(BF16)
        kv = jnp.dot(mn, w_ref[...], preferred_element_type=F32)
        dkg = jnp.zeros((1, M_HEAD_DIM), F32)
        for h in range(M_HEADS):
            col = slice(h * M_HEAD_DIM, (h + 1) * M_HEAD_DIM)
            t = kv[:, col]
            rk = lax.rsqrt(jnp.mean(t * t, axis=-1, keepdims=True) + EPS)
            that = t * rk
            dy = dk_ref[:, col]
            dkg = dkg + jnp.sum(dy * that, axis=0, keepdims=True)
            dthat = dy * kg_ref[...]
            dkv_ref[:, col] = (rk * (dthat - that * jnp.mean(dthat * that, axis=-1, keepdims=True))).astype(BF16)
        dkv_ref[:, kw:] = dv_ref[...].astype(BF16)
        dkg_ref[...] = dkg
        dkv = dkv_ref[...]
        dw_ref[...] = lax.dot_general(mn, dkv, (((0,), (0,)), ((), ())), preferred_element_type=F32)
        dmn = lax.dot_general(dkv, w_ref[...], (((1,), (1,)), ((), ())), preferred_element_type=F32)
        dmg_ref[...] = jnp.sum(dmn * mhat, axis=0, keepdims=True)

    return pl.pallas_call(
        body, name=name,
        out_shape=(jax.ShapeDtypeStruct((d, 2 * kw), F32), jax.ShapeDtypeStruct((1, d), F32),
                   jax.ShapeDtypeStruct((1, M_HEAD_DIM), F32)),
        scratch_shapes=[pltpu.VMEM((m_len, 2 * kw), BF16)],
        compiler_params=pltpu.CompilerParams(vmem_limit_bytes=VMEM_LIMIT_BYTES),
    )(mem, mem_gain, w_kv, k_gain, dk, dv)


def _mem_attn_fwd(proj, cidx, mk, mv, q_gain, name):
    s_len = proj.shape[0]
    kw = M_HEADS * M_HEAD_DIM
    tm = 512
    scale = M_HEAD_DIM ** -0.5

    def body(q_ref, k_ref, v_ref, g_ref, o_ref):
        for h in range(M_HEADS):
            col = slice(h * M_HEAD_DIM, (h + 1) * M_HEAD_DIM)
            t = q_ref[:, col]
            rs = lax.rsqrt(jnp.mean(t * t, axis=-1, keepdims=True) + EPS)
            qn = ((t * rs) * g_ref[...]).astype(BF16)
            s = lax.dot_general(qn, k_ref[:, col], (((1,), (1,)), ((), ())), preferred_element_type=F32) * scale
            mx = jnp.max(s, axis=-1, keepdims=True)
            p = jnp.exp(s - mx)
            pn = (p * (1.0 / jnp.sum(p, axis=-1, keepdims=True))).astype(BF16)
            o_ref[:, col] = jnp.dot(pn, v_ref[:, col], preferred_element_type=F32).astype(BF16)

    whole = pl.BlockSpec((MEM_LEN, kw), lambda i: (0, 0))
    return pl.pallas_call(
        body, name=name, grid=(s_len // tm,),
        in_specs=[pl.BlockSpec((tm, kw), lambda i: (i, cidx)), whole, whole, pl.BlockSpec((1, M_HEAD_DIM), lambda i: (0, 0))],
        out_specs=pl.BlockSpec((tm, kw), lambda i: (i, 0)),
        out_shape=jax.ShapeDtypeStruct((s_len, kw), BF16), compiler_params=_cparams(1),
    )(proj, mk, mv, q_gain)


def _mem_attn_bwd(proj, cidx, mk, mv, q_gain, do, name):
    s_len = proj.shape[0]
    kw = M_HEADS * M_HEAD_DIM
    tm = 512
    scale = M_HEAD_DIM ** -0.5

    def body(q_ref, k_ref, v_ref, g_ref, do_ref, dq_ref, dk_ref, dv_ref, dg_ref):
        @pl.when(pl.program_id(0) == 0)
        def _():
            dk_ref[...] = jnp.zeros_like(dk_ref)
            dv_ref[...] = jnp.zeros_like(dv_ref)
            dg_ref[...] = jnp.zeros_like(dg_ref)

        for h in range(M_HEADS):
            col = slice(h * M_HEAD_DIM, (h + 1) * M_HEAD_DIM)
            t = q_ref[:, col]
            rs = lax.rsqrt(jnp.mean(t * t, axis=-1, keepdims=True) + EPS)
            that = t * rs
            qn = (that * g_ref[...]).astype(BF16)
            kh, vh = k_ref[:, col], v_ref[:, col]
            dob = do_ref[:, col].astype(BF16)
            s = lax.dot_general(qn, kh, (((1,), (1,)), ((), ())), preferred_element_type=F32) * scale
            mx = jnp.max(s, axis=-1, keepdims=True)
            p = jnp.exp(s - mx)
            p = p * (1.0 / jnp.sum(p, axis=-1, keepdims=True))
            dp = lax.dot_general(dob, vh, (((1,), (1,)), ((), ())), preferred_element_type=F32)
            ds = (p * (dp - jnp.sum(p * dp, axis=-1, keepdims=True)) * scale).astype(BF16)
            dqn = jnp.dot(ds, kh, preferred_element_type=F32)
            dk_ref[:, col] += lax.dot_general(ds, qn, (((0,), (0,)), ((), ())), preferred_element_type=F32)
            dv_ref[:, col] += lax.dot_general(p.astype(BF16), dob, (((0,), (0,)), ((), ())), preferred_element_type=F32)
            dg_ref[...] += jnp.sum(dqn * that, axis=0, keepdims=True)
            dthat = dqn * g_ref[...]
            dq_ref[:, col] = (rs * (dthat - that * jnp.mean(dthat * that, axis=-1, keepdims=True))).astype(BF16)

    whole = pl.BlockSpec((MEM_LEN, kw), lambda i: (0, 0))
    vec = pl.BlockSpec((1, M_HEAD_DIM), lambda i: (0, 0))
    row = pl.BlockSpec((tm, kw), lambda i: (i, 0))
    return pl.pallas_call(
        body, name=name, grid=(s_len // tm,),
        in_specs=[pl.BlockSpec((tm, kw), lambda i: (i, cidx)), whole, whole, vec, row],
        out_specs=(row, whole, whole, vec),
        out_shape=(jax.ShapeDtypeStruct((s_len, kw), BF16), jax.ShapeDtypeStruct((MEM_LEN, kw), F32),
                   jax.ShapeDtypeStruct((MEM_LEN, kw), F32), jax.ShapeDtypeStruct((1, M_HEAD_DIM), F32)),
        compiler_params=_cparams(1),
    )(proj, mk, mv, q_gain, do)


def _gate_merge(gates, pa, pb, pm, name):
    s_len, d = pa.shape
    tm = 256

    def body(g_ref, a_ref, b_ref, m_ref, o_ref):
        f = lambda v: v.astype(F32)
        o_ref[...] = (f(g_ref[:, 0:d]) * f(a_ref[...]) + f(g_ref[:, d:2 * d]) * f(b_ref[...])
                      + f(g_ref[:, 2 * d:3 * d]) * f(m_ref[...])).astype(BF16)

    row = pl.BlockSpec((tm, d), lambda i: (i, 0))
    return pl.pallas_call(
        body, name=name, grid=(s_len // tm,), in_specs=[pl.BlockSpec((tm, 3 * d), lambda i: (i, 0)), row, row, row],
        out_specs=row, out_shape=jax.ShapeDtypeStruct((s_len, d), BF16), compiler_params=_cparams(1),
    )(gates, pa, pb, pm)


def _gate_merge_bwd(dmerged, gates, pa, pb, pm, name):
    s_len, d = pa.shape
    tm = 256

    def body(dm_ref, g_ref, a_ref, b_ref, m_ref, da_ref, db_ref, dmm_ref, dg_ref, dbg_ref):
        @pl.when(pl.program_id(0) == 0)
        def _():
            dbg_ref[...] = jnp.zeros_like(dbg_ref)
        dm = dm_ref[...]
        for k, (p_ref, dp_ref) in enumerate(((a_ref, da_ref), (b_ref, db_ref), (m_ref, dmm_ref))):
            col = slice(k * d, (k + 1) * d)
            g = g_ref[:, col].astype(F32)
            dp_ref[...] = (dm * g).astype(BF16)
            dpre = (dm * p_ref[...].astype(F32)) * (g * (1.0 - g))
            dbg_ref[:, col] += jnp.sum(dpre, axis=0, keepdims=True)
            dg_ref[:, col] = dpre.astype(BF16)

    row = pl.BlockSpec((tm, d), lambda i: (i, 0))
    wide = pl.BlockSpec((tm, 3 * d), lambda i: (i, 0))
    shp = jax.ShapeDtypeStruct((s_len, d), BF16)
    return pl.pallas_call(
        body, name=name, grid=(s_len // tm,), in_specs=[row, wide, row, row, row],
        out_specs=(row, row, row, wide, pl.BlockSpec((1, 3 * d), lambda i: (0, 0))),
        out_shape=(shp, shp, shp, jax.ShapeDtypeStruct((s_len, 3 * d), BF16), jax.ShapeDtypeStruct((1, 3 * d), F32)),
        compiler_params=_cparams(1),
    )(dmerged, gates, pa, pb, pm)


CONV_CHUNK = 256


def _pick_row(tile, j):
    row = lax.broadcasted_iota(jnp.int32, tile.shape, 0)
    return jnp.sum(jnp.where(row == j, tile, jnp.zeros_like(tile)), axis=0, keepdims=True)


def _rows_before(ref, start, k):
    cur = ref[pl.ds(start, CONV_CHUNK), :].astype(F32)
    prev = ref[pl.ds(pl.multiple_of(jnp.maximum(start - 16, 0), 16), 16), :].astype(F32)
    prev = jnp.where(start > 0, prev, jnp.zeros_like(prev))
    rolled = pltpu.roll(cur, k, 0)
    row = lax.broadcasted_iota(jnp.int32, cur.shape, 0)
    for j in range(k):
        rolled = jnp.where(row == j, _pick_row(prev, 16 - k + j), rolled)
    return rolled


def _rows_after(ref, start, k):
    cur = ref[pl.ds(start, CONV_CHUNK), :]
    nxt = ref[pl.ds(pl.multiple_of(start + CONV_CHUNK, 8), 8), :]
    rolled = pltpu.roll(cur, CONV_CHUNK - k, 0)
    row = lax.broadcasted_iota(jnp.int32, cur.shape, 0)
    for j in range(k):
        rolled = jnp.where(row == CONV_CHUNK - k + j, _pick_row(nxt, j), rolled)
    return rolled


def _conv_pre(u_ref, w_ref, b_ref, start):
    u2 = _rows_before(u_ref, start, 2)
    u1 = _rows_before(u_ref, start, 1)
    u0 = u_ref[pl.ds(start, CONV_CHUNK), :].astype(F32)
    c = ((b_ref[...] + w_ref[0:1, :] * u2) + w_ref[1:2, :] * u1) + w_ref[2:3, :] * u0
    return c, (u2, u1, u0)


def _conv_glu(u, conv_w, conv_b, name):
    s_len = u.shape[0]
    nblk = D_FF // LANES

    def body(ua_ref, ug_ref, wa_ref, wg_ref, ba_ref, bg_ref, o_ref):
        def chunk(ci, carry):
            start = pl.multiple_of(ci * CONV_CHUNK, CONV_CHUNK)
            ca, _ = _conv_pre(ua_ref, wa_ref, ba_ref, start)
            cg, _ = _conv_pre(ug_ref, wg_ref, bg_ref, start)
            o_ref[pl.ds(start, CONV_CHUNK), :] = ((ca * _sigmoid(ca)) * cg).astype(BF16)
            return carry
        lax.fori_loop(0, s_len // CONV_CHUNK, chunk, 0)

    def col(rows, off):
        return pl.BlockSpec((rows, LANES), lambda j: (0, off + j))

    return pl.pallas_call(
        body, name=name, grid=(nblk,),
        in_specs=[col(s_len, 0), col(s_len, nblk), col(3, 0), col(3, nblk), col(1, 0), col(1, nblk)],
        out_specs=col(s_len, 0), out_shape=jax.ShapeDtypeStruct((s_len, D_FF), BF16), compiler_params=_cparams(1),
    )(u, u, conv_w, conv_w, conv_b, conv_b)


def _conv_glu_bwd(dact, u, conv_w, conv_b, name):
    s_len = u.shape[0]
    nblk = D_FF // LANES
    n_chunks = s_len // CONV_CHUNK

    def body(da_ref, ua_ref, ug_ref, wa_ref, wg_ref, ba_ref, bg_ref,
             dua_ref, dug_ref, dwa_ref, dwg_ref, dba_ref, dbg_ref, sa, sg):
        sa[pl.ds(s_len, 8), :] = jnp.zeros((8, LANES), F32)
        sg[pl.ds(s_len, 8), :] = jnp.zeros((8, LANES), F32)
        zero = jnp.zeros((1, LANES), F32)

        def chunk1(ci, carry):
            start = pl.multiple_of(ci * CONV_CHUNK, CONV_CHUNK)
            ca, ua = _conv_pre(ua_ref, wa_ref, ba_ref, start)
            cg, ug = _conv_pre(ug_ref, wg_ref, bg_ref, start)
            dact_v = da_ref[pl.ds(start, CONV_CHUNK), :].astype(F32)
            sig = _sigmoid(ca)
            dcg = dact_v * (ca * sig)
            dca = (dact_v * cg) * (sig * (1.0 + ca * (1.0 - sig)))
            sa[pl.ds(start, CONV_CHUNK), :] = dca
            sg[pl.ds(start, CONV_CHUNK), :] = dcg
            out = [carry[0] + jnp.sum(dca, axis=0, keepdims=True), carry[1] + jnp.sum(dcg, axis=0, keepdims=True)]
            for j in range(3):
                out.append(carry[2 + j] + jnp.sum(dca * ua[j], axis=0, keepdims=True))
            for j in range(3):
                out.append(carry[5 + j] + jnp.sum(dcg * ug[j], axis=0, keepdims=True))
            return tuple(out)

        acc = lax.fori_loop(0, n_chunks, chunk1, (zero,) * 8)
        dba_ref[...] = acc[0]
        dbg_ref[...] = acc[1]
        for j in range(3):
            dwa_ref[j:j + 1, :] = acc[2 + j]
            dwg_ref[j:j + 1, :] = acc[5 + j]

        def chunk2(ci, carry):
            start = pl.multiple_of(ci * CONV_CHUNK, CONV_CHUNK)
            for s_ref, w_ref, o_ref in ((sa, wa_ref, dua_ref), (sg, wg_ref, dug_ref)):
                d0 = s_ref[pl.ds(start, CONV_CHUNK), :]
                d1 = _rows_after(s_ref, start, 1)
                d2 = _rows_after(s_ref, start, 2)
                o_ref[pl.ds(start, CONV_CHUNK), :] = (w_ref[2:3, :] * d0 + w_ref[1:2, :] * d1
                                                      + w_ref[0:1, :] * d2).astype(BF16)
            return carry
        lax.fori_loop(0, n_chunks, chunk2, 0)

    def col(rows, off):
        return pl.BlockSpec((rows, LANES), lambda j: (0, off + j))

    big = jax.ShapeDtypeStruct((s_len, D_FF), BF16)
    return pl.pallas_call(
        body, name=name, grid=(nblk,),
        in_specs=[col(s_len, 0), col(s_len, 0), col(s_len, nblk), col(3, 0), col(3, nblk), col(1, 0), col(1, nblk)],
        out_specs=(col(s_len, 0), col(s_len, 0), col(3, 0), col(3, 0), col(1, 0), col(1, 0)),
        out_shape=(big, big, jax.ShapeDtypeStruct((3, D_FF), F32), jax.ShapeDtypeStruct((3, D_FF), F32),
                   jax.ShapeDtypeStruct((1, D_FF), F32), jax.ShapeDtypeStruct((1, D_FF), F32)),
        scratch_shapes=[pltpu.VMEM((s_len + 8, LANES), F32)] * 2, compiler_params=_cparams(1),
    )(dact, u, u, conv_w, conv_w, conv_b, conv_b)


def _loss_head(y, target, name):
    s_len, d = y.shape
    tm = 512

    def body(y_ref, t_ref, dy_ref, dyb_ref, l_ref):
        @pl.when(pl.program_id(0) == 0)
        def _():
            l_ref[...] = jnp.zeros_like(l_ref)
        err = y_ref[...] - t_ref[...]
        dy = err * (1.0 / d)
        dy_ref[...] = dy
        dyb_ref[...] = dy.astype(BF16)
        part = 0.5 * jnp.sum(jnp.mean(err * err, axis=-1, keepdims=True), axis=0, keepdims=True)
        l_ref[...] += jnp.broadcast_to(part, l_ref.shape)

    row = pl.BlockSpec((tm, d), lambda i: (i, 0))
    return pl.pallas_call(
        body, name=name, grid=(s_len // tm,), in_specs=[row, row],
        out_specs=(row, row, pl.BlockSpec((8, LANES), lambda i: (0, 0))),
        out_shape=(jax.ShapeDtypeStruct((s_len, d), F32), jax.ShapeDtypeStruct((s_len, d), BF16),
                   jax.ShapeDtypeStruct((8, LANES), F32)),
        compiler_params=_cparams(1),
    )(y, target)


def _rope_tables(positions):
    half = ROPE_DIMS // 2
    freqs = jnp.exp(jnp.arange(half, dtype=F32) * (-2.0 * math.log(ROPE_THETA) / ROPE_DIMS))
    ang = positions.reshape(-1).astype(F32)[:, None] * freqs
    cos, sin = jnp.cos(ang), jnp.sin(ang)
    n = ang.shape[0]
    zeros = lambda w: jnp.zeros((n, w), F32)
    c = jnp.concatenate([cos, cos, jnp.ones((n, HEAD_DIM - ROPE_DIMS), F32)], axis=1)
    s1 = jnp.concatenate([-sin, zeros(HEAD_DIM - half)], axis=1)
    s2 = jnp.concatenate([zeros(half), sin, zeros(HEAD_DIM - ROPE_DIMS)], axis=1)
    return tuple(jnp.tile(t, (1, 2)) for t in (c, s1, s2))


def _two(v):
    return jnp.tile(v.reshape(1, HEAD_DIM), (1, 2))


def _fold_heads(g):
    return g[0, :HEAD_DIM] + g[0, HEAD_DIM:]


def _device_step(x, mem, positions, target, w, on_ffn_grads=None):
    tabs = _rope_tables(positions)
    dils = tuple(d for _, d in A_GROUPS)
    grads = {}

    h, r1 = _rms_fwd(x, w['attn_norm'], "rms1")
    proj = _mm_rows([(h, w['w_in'], 0)], "mm_in")
    gates = _mm_rows([(h, w['w_gate'], 0)], "mm_gate", bias=w['b_gate'], sigmoid=True, out_dtypes=(BF16,))

    qkv_a, o_g, lse_g = [], [], []
    for gi, (window, d) in enumerate(A_GROUPS):
        gq, gk = _two(w['a_q_norm'][gi]), _two(w['a_k_norm'][gi])
        qkv = _qk_prep(proj, 6 * gi, d, False, gq, gk, tabs, f"qk_prep_a{gi}")
        o, lse = _band_fwd(qkv, 2 * d, window // d, None, f"band_fwd_a{gi}")
        qkv_a.append(qkv)
        o_g.append(o)
        lse_g.append(lse)
    o_a, lse_a = _merge_groups(o_g, lse_g, dils, "merge_a")

    gbq, gbk = _two(w['b_q_norm']), _two(w['b_k_norm'])
    sinks = jnp.repeat(w['b_sinks'].reshape(4, 2), HEAD_DIM, axis=1).reshape(4, 1, LANES)
    qkv_b = _qk_prep(proj, 18, 1, True, gbq, gbk, tabs, "qk_prep_b")
    o_b, lse_b = _band_fwd(qkv_b, 4, B_WINDOW - 1, sinks, "band_fwd_b")

    mk, mv = _mem_kv(mem, w['mem_norm'], w['w_mem_kv'], w['m_k_norm'], "mem_kv")
    o_m = _mem_attn_fwd(proj, 6, mk, mv, w['m_q_norm'], "mem_attn")

    pa = _mm_rows([(o_a, w['w_o_a'], 0)], "mm_oa", out_dtypes=(BF16,))
    pb = _mm_rows([(o_b, w['w_o_b'], 0)], "mm_ob", out_dtypes=(BF16,))
    pm = _mm_rows([(o_m, w['w_o_m'], 0)], "mm_om", out_dtypes=(BF16,))
    merged = _gate_merge(gates, pa, pb, pm, "gate_merge")
    x1 = _mm_rows([(merged, w['w_out'], 0)], "mm_out", res=x)

    h2, r2 = _rms_fwd(x1, w['ffn_norm'], "rms2")
    u = _mm_rows([(h2, w['w_up'], 0)], "mm_up", out_dtypes=(BF16,))
    act = _conv_glu(u, w['conv_w'], w['conv_b'], "conv_glu")
    y = _mm_rows([(act, w['w_down'], 0)], "mm_down", res=x1)
    dy, dy_b, loss = _loss_head(y, target, "loss_head")

    dact = _mm_rows([(dy_b, w['w_down'], 0)], "mm_d_act", nt=True, out_dtypes=(BF16,))
    grads['w_down'] = _mm_tn(act, dy_b, "mm_dw_down")
    du_a, du_g, dcw_a, dcw_g, dcb_a, dcb_g = _conv_glu_bwd(dact, u, w['conv_w'], w['conv_b'], "conv_glu_bwd")
    grads['conv_w'] = jnp.concatenate([dcw_a, dcw_g], axis=1)
    grads['conv_b'] = jnp.concatenate([dcb_a, dcb_g], axis=1)
    dh2 = _mm_rows([(du_a, w['w_up'], 0), (du_g, w['w_up'], 1)], "mm_d_h2", nt=True)
    grads['w_up'] = jnp.concatenate([_mm_tn(h2, du_a, "mm_dw_up_a"), _mm_tn(h2, du_g, "mm_dw_up_g")], axis=1)
    ffn_gain = w['ffn_norm']
    if on_ffn_grads is not None:
        ffn_gain = ffn_gain + on_ffn_grads(grads)[0:1, 0:1]
    dx1, dx1_b, grads['ffn_norm'] = _rms_bwd(dh2, x1, r2, ffn_gain, dy, "rms2_bwd", bf16_copy=True)

    dmerged = _mm_rows([(dx1_b, w['w_out'], 0)], "mm_d_merged", nt=True)
    grads['w_out'] = _mm_tn(merged, dx1_b, "mm_dw_out")
    dpa, dpb, dpm, dgpre, grads['b_gate'] = _gate_merge_bwd(dmerged, gates, pa, pb, pm, "gate_merge_bwd")
    do_a = _mm_rows([(dpa, w['w_o_a'], 0)], "mm_d_oa", nt=True)
    do_b = _mm_rows([(dpb, w['w_o_b'], 0)], "mm_d_ob", nt=True)
    do_m = _mm_rows([(dpm, w['w_o_m'], 0)], "mm_d_om", nt=True)
    grads['w_o_a'] = _mm_tn(o_a, dpa, "mm_dw_oa")
    grads['w_o_b'] = _mm_tn(o_b, dpb, "mm_dw_ob")
    grads['w_o_m'] = _mm_tn(o_m, dpm, "mm_dw_om")

    prep = _bwd_prep(do_a, o_a, lse_a, dils, None, "bwd_prep_a")
    dproj, dgq_a, dgk_a = [], [], []
    for gi, (window, d) in enumerate(A_GROUPS):
        gq, gk = _two(w['a_q_norm'][gi]), _two(w['a_k_norm'][gi])
        dqkv = _band_bwd(qkv_a[gi], prep[3 * gi], prep[3 * gi + 1], prep[3 * gi + 2], 2 * d, window // d,
                         f"band_bwd_a{gi}")
        dp, dgq, dgk = _qk_prep_bwd(dqkv, proj, 6 * gi, d, False, gq, gk, tabs, f"qk_prep_bwd_a{gi}")
        dproj.append(dp)
        dgq_a.append(_fold_heads(dgq))
        dgk_a.append(_fold_heads(dgk))
    grads['a_q_norm'] = jnp.stack(dgq_a)
    grads['a_k_norm'] = jnp.stack(dgk_a)

    do_bu, lse_bu, delta_bu, dsink = _bwd_prep(do_b, o_b, lse_b, (1,), sinks, "bwd_prep_b")
    dqkv = _band_bwd(qkv_b, do_bu, lse_bu, delta_bu, 4, B_WINDOW - 1, "band_bwd_b")
    dp_b, dgq, dgk = _qk_prep_bwd(dqkv, proj, 18, 1, True, gbq, gbk, tabs, "qk_prep_bwd_b")
    dproj.append(dp_b)
    grads['b_q_norm'] = _fold_heads(dgq)
    grads['b_k_norm'] = _fold_heads(dgk)
    grads['b_sinks'] = jnp.stack([dsink[:, 0, 0], dsink[:, 0, HEAD_DIM]], axis=1).reshape(8)

    dq_m, dmk, dmv, grads['m_q_norm'] = _mem_attn_bwd(proj, 6, mk, mv, w['m_q_norm'], do_m, "mem_attn_bwd")
    dproj.append(dq_m)
    grads['w_mem_kv'], grads['mem_norm'], grads['m_k_norm'] = _mem_kv_bwd(
        mem, w['mem_norm'], w['w_mem_kv'], w['m_k_norm'], dmk, dmv, "mem_kv_bwd")

    cols = (0, 1, 2, 3, 6)
    grads['w_in'] = jnp.concatenate([_mm_tn(h, dp, f"mm_dw_in{k}") for k, dp in enumerate(dproj)], axis=1)
    grads['w_gate'] = _mm_tn(h, dgpre, "mm_dw_gate")
    dh = _mm_rows([(dp, w['w_in'], c) for dp, c in zip(dproj, cols)] + [(dgpre, w['w_gate'], 0)], "mm_d_h", nt=True)
    grad_x, grads['attn_norm'] = _rms_bwd(dh, x, r1, w['attn_norm'], dx1, "rms1_bwd")
    return loss, grad_x, grads


def _coords():
    return lax.axis_index("x"), lax.axis_index("y"), lax.axis_index("c")


def _slot(p):
    return 4 * p[0] + 2 * p[1] + p[2]


def _peers(me):
    x, y, c = me
    out = []
    for mask in range(1, N_DEV):
        out.append((1 - x if mask & 4 else x, 1 - y if mask & 2 else y, 1 - c if mask & 1 else c))
    return out


HBM_SPEC = pl.BlockSpec(memory_space=pltpu.HBM)


def _all_gather(shards, name):
    n = len(shards)

    def body(*refs):
        ins, outs = refs[:n], refs[n:2 * n]
        send_sems, recv_sems, local_sems = refs[2 * n:]
        x, y, c = _coords()
        me, sibling = (x, y, c), (x, y, 1 - c)
        chips = [(1 - x, y), (x, 1 - y), (1 - x, 1 - y)]

        def copy(a, k, block, to, src=None):
            dst = outs[a].at[_slot(block)]
            return pltpu.make_async_remote_copy(
                src_ref=dst if src is None else src, dst_ref=dst, send_sem=send_sems.at[a, k],
                recv_sem=recv_sems.at[a, k], device_id=to, device_id_type=MESH)

        mine = [pltpu.make_async_copy(ins[a], outs[a].at[_slot(me)], local_sems.at[a]) for a in range(n)]
        for cp in mine:
            cp.start()
        first = []
        for a in range(n):
            first.append(copy(a, 0, me, sibling, src=ins[a]))
            first += [copy(a, 1 + j, me, (*chip, c), src=ins[a]) for j, chip in enumerate(chips)]
        for cp in first:
            cp.start()
        passed = []
        for a in range(n):
            for j, chip in enumerate(chips):
                copy(a, 1 + j, (*chip, c), me).wait_recv()
                fwd = copy(a, 4 + j, (*chip, c), sibling)
                fwd.start()
                passed.append(fwd)
        for a in range(n):
            copy(a, 0, sibling, me).wait_recv()
            for j, chip in enumerate(chips):
                copy(a, 4 + j, (*chip, 1 - c), me).wait_recv()
        for cp in first + passed:
            cp.wait_send()
        for cp in mine:
            cp.wait()

    return pl.pallas_call(
        body, name=name, in_specs=[HBM_SPEC] * n, out_specs=tuple([HBM_SPEC] * n),
        out_shape=tuple(jax.ShapeDtypeStruct((N_DEV,) + s.shape, s.dtype) for s in shards),
        scratch_shapes=[pltpu.SemaphoreType.DMA((n, 7)), pltpu.SemaphoreType.DMA((n, 7)), pltpu.SemaphoreType.DMA((n,))],
    )(*shards)


def _exchange(blocks, name):
    n = len(blocks)

    def body(*refs):
        ins, outs = refs[:n], refs[n:2 * n]
        send_sems, recv_sems, local_sems = refs[2 * n:]
        me = _coords()
        peers = _peers(me)
        mine = [pltpu.make_async_copy(ins[a].at[_slot(me)], outs[a].at[_slot(me)], local_sems.at[a]) for a in range(n)]
        for cp in mine:
            cp.start()

        def copy(a, k):
            return pltpu.make_async_remote_copy(
                src_ref=ins[a].at[_slot(peers[k])], dst_ref=outs[a].at[_slot(me)], send_sem=send_sems.at[a, k],
                recv_sem=recv_sems.at[a, k], device_id=peers[k], device_id_type=MESH)

        def arrival(a, k):
            return pltpu.make_async_remote_copy(
                src_ref=ins[a].at[_slot(me)], dst_ref=outs[a].at[_slot(peers[k])], send_sem=send_sems.at[a, k],
                recv_sem=recv_sems.at[a, k], device_id=peers[k], device_id_type=MESH)

        sends = [copy(a, k) for a in range(n) for k in range(N_DEV - 1)]
        for cp in sends:
            cp.start()
        for a in range(n):
            for k in range(N_DEV - 1):
                arrival(a, k).wait_recv()
        for cp in sends:
            cp.wait_send()
        for cp in mine:
            cp.wait()

    return pl.pallas_call(
        body, name=name, in_specs=[HBM_SPEC] * n, out_specs=tuple([HBM_SPEC] * n),
        out_shape=tuple(jax.ShapeDtypeStruct(b.shape, b.dtype) for b in blocks),
        scratch_shapes=[pltpu.SemaphoreType.DMA((n, 7)), pltpu.SemaphoreType.DMA((n, 7)), pltpu.SemaphoreType.DMA((n,))],
    )(*blocks)


SEM_SPEC = pl.BlockSpec(memory_space=pltpu.SEMAPHORE)
SIDE_EFFECT = pltpu.SideEffectType.DATAFLOW_SIDE_EFFECTING


def _exchange_start(blocks, name):
    n = len(blocks)

    def body(*refs):
        ins, lands = refs[:n], refs[n:2 * n]
        send_sems, recv_sems = refs[2 * n], refs[2 * n + 1]
        token = refs[-1]
        me = _coords()
        peers = _peers(me)
        for a in range(n):
            for k in range(N_DEV - 1):
                pltpu.make_async_remote_copy(
                    src_ref=ins[a].at[_slot(peers[k])], dst_ref=lands[a].at[_slot(me)],
                    send_sem=send_sems.at[a * (N_DEV - 1) + k], recv_sem=recv_sems.at[a * (N_DEV - 1) + k],
                    device_id=peers[k], device_id_type=MESH).start()
        token[...] = jnp.zeros_like(token)

    hbm = [pltpu.HBM(b.shape, b.dtype) for b in blocks]
    sems = pltpu.SemaphoreType.DMA((n * (N_DEV - 1),))
    ins = [pltpu.with_memory_space_constraint(b, pltpu.HBM) for b in blocks]
    lands = [pltpu.with_memory_space_constraint(lax.empty(b.shape, b.dtype), pltpu.HBM) for b in blocks]
    return pl.pallas_call(
        body, name=name, out_shape=(sems, sems, *hbm, *hbm, jax.ShapeDtypeStruct((8, LANES), F32)),
        in_specs=[HBM_SPEC] * (2 * n),
        out_specs=(SEM_SPEC, SEM_SPEC, *([HBM_SPEC] * (2 * n)), pl.BlockSpec(memory_space=pltpu.VMEM)),
        input_output_aliases={i: 2 + i for i in range(2 * n)},
        compiler_params=pltpu.CompilerParams(has_side_effects=SIDE_EFFECT),
    )(*ins, *lands)


def _exchange_wait(started, after, name):
    n = (len(started) - 3) // 2
    send_sems, recv_sems = started[0], started[1]
    thru = started[2:2 + 2 * n]

    def body(*refs):
        ins, lands = refs[:n], refs[n:2 * n]
        send_ref, recv_ref = refs[2 * n], refs[2 * n + 1]
        me = _coords()
        peers = _peers(me)
        for a in range(n):
            for k in range(N_DEV - 1):
                cp = pltpu.make_async_remote_copy(
                    src_ref=ins[a].at[_slot(peers[k])], dst_ref=lands[a].at[_slot(peers[k])],
                    send_sem=send_ref.at[a * (N_DEV - 1) + k], recv_sem=recv_ref.at[a * (N_DEV - 1) + k],
                    device_id=peers[k], device_id_type=MESH)
                cp.wait_send()
                cp.wait_recv()

    hbm = [pltpu.HBM(t.shape, t.dtype) for t in thru]
    res = pl.pallas_call(
        body, name=name, out_shape=tuple(hbm),
        in_specs=[HBM_SPEC] * (2 * n) + [SEM_SPEC, SEM_SPEC, pl.BlockSpec(memory_space=pl.ANY)],
        out_specs=tuple([HBM_SPEC] * (2 * n)), input_output_aliases={i: i for i in range(2 * n)},
        compiler_params=pltpu.CompilerParams(has_side_effects=SIDE_EFFECT),
    )(*thru, send_sems, recv_sems, after)
    return res[n:]


def _all_sum(p, name):
    def body(p_ref, o_ref, recv, send_sems, recv_sems):
        me = _coords()
        peers = _peers(me)
        recv[_slot(me)] = p_ref[...]

        def copy(k, landing):
            return pltpu.make_async_remote_copy(
                src_ref=p_ref, dst_ref=recv.at[_slot(landing)], send_sem=send_sems.at[k], recv_sem=recv_sems.at[k],
                device_id=peers[k], device_id_type=MESH)

        sends = [copy(k, me) for k in range(N_DEV - 1)]
        for cp in sends:
            cp.start()
        for k in range(N_DEV - 1):
            copy(k, peers[k]).wait_recv()
        for cp in sends:
            cp.wait_send()
        acc = recv[0]
        for s in range(1, N_DEV):
            acc = acc + recv[s]
        o_ref[...] = acc

    vmem = pl.BlockSpec(memory_space=pltpu.VMEM)
    return pl.pallas_call(
        body, name=name, in_specs=[vmem], out_specs=vmem, out_shape=jax.ShapeDtypeStruct(p.shape, F32),
        scratch_shapes=[pltpu.VMEM((N_DEV,) + p.shape, F32), pltpu.SemaphoreType.DMA((N_DEV - 1,)),
                        pltpu.SemaphoreType.DMA((N_DEV - 1,))],
    )(p)


def _adam(w, g, m, v):
    m2 = ADAM_B1 * m + (1.0 - ADAM_B1) * g
    v2 = ADAM_B2 * v + (1.0 - ADAM_B2) * (g * g)
    m_hat = m2 / (1.0 - ADAM_B1 ** ADAM_STEP)
    v_hat = v2 / (1.0 - ADAM_B2 ** ADAM_STEP)
    delta = -ADAM_LR * (m_hat / (jnp.sqrt(v_hat) + ADAM_EPS) + ADAM_WD * w)
    return delta, m2, v2


def _row_tile(rows, cols):
    best = rows
    for t in range(16, rows, 16):
        if rows % t == 0 and t * cols * 4 <= (1 << 20):
            best = t
    return best


def _adam_reduce(parts, w, m, v, name):
    rows, cols = w.shape
    tr = _row_tile(rows, cols)

    def body(p_ref, w_ref, m_ref, v_ref, g_ref, d_ref, m2_ref, v2_ref):
        g = p_ref[0].astype(F32)
        for s in range(1, N_DEV):
            g = g + p_ref[s].astype(F32)
        g_ref[...] = g
        d_ref[...], m2_ref[...], v2_ref[...] = _adam(w_ref[...], g, m_ref[...], v_ref[...])

    blk = pl.BlockSpec((tr, cols), lambda i: (i, 0))
    shp = jax.ShapeDtypeStruct((rows, cols), F32)
    return pl.pallas_call(
        body, name=name, grid=(rows // tr,),
        in_specs=[pl.BlockSpec((N_DEV, tr, cols), lambda i: (0, i, 0)), blk, blk, blk],
        out_specs=(blk,) * 4, out_shape=(shp,) * 4, compiler_params=_cparams(1),
    )(parts, w, m, v)


PACK_COLS = 1024
PACK = {'attn_norm': (0, 1, 1024), 'mem_norm': (1, 1, 1024), 'ffn_norm': (2, 1, 1024), 'b_gate': (3, 3, 1024),
        'conv_b': (6, 6, 1024), 'a_q_norm': (12, 3, 64), 'a_k_norm': (15, 3, 64), 'b_q_norm': (18, 1, 64),
        'b_k_norm': (19, 1, 64), 'm_q_norm': (20, 1, 128), 'm_k_norm': (21, 1, 128), 'b_sinks': (22, 1, 8)}
PACK_LOSS_ROW = 23
PACK_ROWS = 24


def _pack_pieces(name, width):
    r0, nr, lanes = PACK[name]
    out = []
    for j in range(nr):
        if lanes == PACK_COLS:
            w = min(PACK_COLS, width - j * PACK_COLS)
            out.append((r0 + j, slice(0, 1), slice(j * PACK_COLS, j * PACK_COLS + w), w))
        else:
            out.append((r0 + j, slice(j, j + 1), slice(0, lanes), lanes))
    return out


def _pack_small(grads, loss_tile, name):
    names = list(PACK)

    def body(*refs):
        o_ref = refs[-1]
        o_ref[...] = jnp.zeros_like(o_ref)
        for k, nm in enumerate(names):
            for row, rs, ls, w in _pack_pieces(nm, refs[k].shape[1]):
                o_ref[row:row + 1, 0:w] = refs[k][rs, ls]
        o_ref[PACK_LOSS_ROW:PACK_LOSS_ROW + 1, 0:1] = refs[len(names)][0:1, 0:1]

    vmem = pl.BlockSpec(memory_space=pltpu.VMEM)
    args = [grads[nm] for nm in names] + [loss_tile]
    return pl.pallas_call(body, name=name, in_specs=[vmem] * len(args), out_specs=vmem,
                          out_shape=jax.ShapeDtypeStruct((PACK_ROWS, PACK_COLS), F32))(*args)


def _adam_small(gsum, ws, ms, vs, name):
    names = list(PACK)
    n = len(names)

    def body(*refs):
        g_ref = refs[0]
        w_refs, m_refs, v_refs = refs[1:1 + n], refs[1 + n:1 + 2 * n], refs[1 + 2 * n:1 + 3 * n]
        outs = refs[1 + 3 * n:]
        outs[0][...] = g_ref[PACK_LOSS_ROW:PACK_LOSS_ROW + 1, 0:1]
        for k, nm in enumerate(names):
            o_g, o_d, o_m, o_v = outs[1 + 4 * k:5 + 4 * k]
            for row, rs, ls, width in _pack_pieces(nm, w_refs[k].shape[1]):
                src = (rs, ls)
                g = g_ref[row:row + 1, 0:width]
                d, m2, v2 = _adam(w_refs[k][src], g, m_refs[k][src], v_refs[k][src])
                o_g[src] = g
                o_d[src] = d
                o_m[src] = m2
                o_v[src] = v2

    vmem = pl.BlockSpec(memory_space=pltpu.VMEM)
    shapes = [jax.ShapeDtypeStruct((1, 1), F32)]
    for nm in names:
        shapes += [jax.ShapeDtypeStruct(ws[nm].shape, F32)] * 4
    args = [gsum] + [ws[nm] for nm in names] + [ms[nm] for nm in names] + [vs[nm] for nm in names]
    return pl.pallas_call(
        body, name=name, in_specs=[vmem] * len(args), out_specs=tuple([vmem] * len(shapes)), out_shape=tuple(shapes),
    )(*args)


def _as2d(name, a):
    return a.reshape(a.shape[-2], a.shape[-1]) if a.ndim == 3 else a


def kernel(x, mem, positions, attn_norm, w_in, a_q_norm, a_k_norm, b_q_norm, b_k_norm, b_sinks, mem_norm, w_mem_kv, m_q_norm, m_k_norm, w_o_a, w_o_b, w_o_m, w_gate, b_gate, w_out, ffn_norm, w_up, conv_w, conv_b, w_down, loss_target, m_attn_norm, m_w_in, m_a_q_norm, m_a_k_norm, m_b_q_norm, m_b_k_norm, m_b_sinks, m_mem_norm, m_w_mem_kv, m_m_q_norm, m_m_k_norm, m_w_o_a, m_w_o_b, m_w_o_m, m_w_gate, m_b_gate, m_w_out, m_ffn_norm, m_w_up, m_conv_w, m_conv_b, m_w_down, v_attn_norm, v_w_in, v_a_q_norm, v_a_k_norm, v_b_q_norm, v_b_k_norm, v_b_sinks, v_mem_norm, v_w_mem_kv, v_m_q_norm, v_m_k_norm, v_w_o_a, v_w_o_b, v_w_o_m, v_w_gate, v_b_gate, v_w_out, v_ffn_norm, v_w_up, v_conv_w, v_conv_b, v_w_down):
    given = dict(attn_norm=attn_norm, w_in=w_in, a_q_norm=a_q_norm, a_k_norm=a_k_norm, b_q_norm=b_q_norm, b_k_norm=b_k_norm, b_sinks=b_sinks, mem_norm=mem_norm, w_mem_kv=w_mem_kv, m_q_norm=m_q_norm, m_k_norm=m_k_norm, w_o_a=w_o_a, w_o_b=w_o_b, w_o_m=w_o_m, w_gate=w_gate, b_gate=b_gate, w_out=w_out, ffn_norm=ffn_norm, w_up=w_up, conv_w=conv_w, conv_b=conv_b, w_down=w_down)
    mom1 = dict(attn_norm=m_attn_norm, w_in=m_w_in, a_q_norm=m_a_q_norm, a_k_norm=m_a_k_norm, b_q_norm=m_b_q_norm, b_k_norm=m_b_k_norm, b_sinks=m_b_sinks, mem_norm=m_mem_norm, w_mem_kv=m_w_mem_kv, m_q_norm=m_m_q_norm, m_k_norm=m_m_k_norm, w_o_a=m_w_o_a, w_o_b=m_w_o_b, w_o_m=m_w_o_m, w_gate=m_w_gate, b_gate=m_b_gate, w_out=m_w_out, ffn_norm=m_ffn_norm, w_up=m_w_up, conv_w=m_conv_w, conv_b=m_conv_b, w_down=m_w_down)
    mom2 = dict(attn_norm=v_attn_norm, w_in=v_w_in, a_q_norm=v_a_q_norm, a_k_norm=v_a_k_norm, b_q_norm=v_b_q_norm, b_k_norm=v_b_k_norm, b_sinks=v_b_sinks, mem_norm=v_mem_norm, w_mem_kv=v_w_mem_kv, m_q_norm=v_m_q_norm, m_k_norm=v_m_k_norm, w_o_a=v_w_o_a, w_o_b=v_w_o_b, w_o_m=v_w_o_m, w_gate=v_w_gate, b_gate=v_b_gate, w_out=v_w_out, ffn_norm=v_ffn_norm, w_up=v_w_up, conv_w=v_conv_w, conv_b=v_conv_b, w_down=v_w_down)

    big = list(BIG)
    shards = [given[n][0] if n == 'conv_w' else given[n][0].astype(BF16) for n in big]
    gathered = _all_gather(shards, "gather_weights")
    w = {}
    for n, g in zip(big, gathered):
        _, r, c = g.shape
        w[n] = g.reshape(N_DEV * r, c) if BIG[n] == 0 else g.transpose(1, 0, 2).reshape(r, N_DEV * c)
    for n in SMALL:
        w[n] = given[n]
    w['a_q_norm'], w['a_k_norm'] = given['a_q_norm'][0], given['a_k_norm'][0]
    w['b_q_norm'], w['b_k_norm'], w['b_sinks'] = given['b_q_norm'][0], given['b_k_norm'][0], given['b_sinks'][0]

    def to_blocks(n, g):
        r, c = given[n].shape[1:]
        g = g.reshape(N_DEV, r, c) if BIG[n] == 0 else g.reshape(r, N_DEV, c).transpose(1, 0, 2)
        return g if n == 'conv_w' else g.astype(BF16)

    early = ['w_down', 'w_up', 'conv_w']
    late = [n for n in big if n not in early]
    my_slot = _slot(_coords())
    sent = {}

    def on_ffn_grads(g):
        blocks = [to_blocks(n, g[n]) for n in early]
        sent['own'] = [lax.dynamic_slice_in_dim(b, my_slot, 1, axis=0) for b in blocks]
        sent['started'] = _exchange_start(blocks, "exchange_ffn_start")
        return sent['started'][-1]

    loss_tile, grad_x, grads = _device_step(x[0], mem[0], positions[0], loss_target[0], w, on_ffn_grads)
    parts = dict(zip(late, _exchange([to_blocks(n, grads[n]) for n in late], "exchange_grads")))
    landed = _exchange_wait(sent['started'], grad_x, "exchange_ffn_wait")
    for n, land, own in zip(early, landed, sent['own']):
        parts[n] = lax.dynamic_update_slice_in_dim(land, own, my_slot, axis=0)

    out = {}
    for n in big:
        res = _adam_reduce(parts[n], given[n][0], mom1[n][0], mom2[n][0], f"adam_{n}")
        out[n] = tuple(t[None] for t in res)

    small = {n: grads[n] for n in PACK}
    small['b_q_norm'], small['b_k_norm'] = grads['b_q_norm'].reshape(1, -1), grads['b_k_norm'].reshape(1, -1)
    small['b_sinks'] = grads['b_sinks'].reshape(1, -1)
    gsum = _all_sum(_pack_small(small, loss_tile, "pack_small"), "sum_small")
    ws = {n: _as2d(n, given[n]) for n in PACK}
    ms = {n: _as2d(n, mom1[n]) for n in PACK}
    vs = {n: _as2d(n, mom2[n]) for n in PACK}
    res = _adam_small(gsum, ws, ms, vs, "adam_small")
    loss = res[0].reshape(())
    for k, n in enumerate(PACK):
        out[n] = tuple(t.reshape(given[n].shape) for t in res[1 + 4 * k:5 + 4 * k])

    outs = [loss, grad_x[None]]
    for field in range(4):
        outs += [out[n][field] for n in WEIGHTS]
    return tuple(outs)
```

```python
import functools
import math

import jax
import jax.numpy as jnp
from jax import lax
from jax.experimental import pallas as pl
from jax.experimental.pallas import tpu as pltpu

F32 = jnp.float32
BF16 = jnp.bfloat16

N_DEV = 8
D_MODEL = 1024
HEAD_DIM = 64
A_GROUPS = ((128, 1), (512, 4), (2048, 16))
B_WINDOW = 128
M_HEADS = 4
M_HEAD_DIM = 128
MEM_LEN = 256
D_FF = 2816
ROPE_THETA = 500000.0
ROPE_DIMS = 16
BLOCK = 128
EPS = 1e-6
LANES = 128
BAND_Q_BLOCKS = 4
BAND_UNITS = 2

ADAM_LR = 0.001
ADAM_B1 = 0.9
ADAM_B2 = 0.999
ADAM_EPS = 1e-08
ADAM_WD = 0.01
ADAM_STEP = 10

VMEM_LIMIT_BYTES = 56 * 1024 * 1024
MESH = pl.DeviceIdType.MESH

WEIGHTS = ['attn_norm', 'w_in', 'a_q_norm', 'a_k_norm', 'b_q_norm', 'b_k_norm', 'b_sinks', 'mem_norm',
           'w_mem_kv', 'm_q_norm', 'm_k_norm', 'w_o_a', 'w_o_b', 'w_o_m', 'w_gate', 'b_gate', 'w_out',
           'ffn_norm', 'w_up', 'conv_w', 'conv_b', 'w_down']
BIG = {'w_in': 1, 'w_mem_kv': 0, 'w_o_a': 1, 'w_o_b': 1, 'w_o_m': 1, 'w_gate': 1, 'w_out': 0, 'w_up': 1,
       'conv_w': 1, 'w_down': 0}
SMALL = [n for n in WEIGHTS if n not in BIG]


def _cparams(n_grid):
    return pltpu.CompilerParams(dimension_semantics=("arbitrary",) * n_grid, vmem_limit_bytes=VMEM_LIMIT_BYTES)


def _pick(n, cands=(512, 256, 128)):
    for c in cands:
        if n % c == 0:
            return c
    return n


def _seg_matrix(width):
    shift = width.bit_length() - 1
    r = lax.shift_right_logical(lax.broadcasted_iota(jnp.int32, (LANES, LANES), 0), shift)
    c = lax.shift_right_logical(lax.broadcasted_iota(jnp.int32, (LANES, LANES), 1), shift)
    return jnp.where(r == c, 1.0, 0.0).astype(BF16)


def _seg_sum(x, seg):
    hi = x.astype(BF16)
    r1 = x - hi.astype(F32)
    mid = r1.astype(BF16)
    lo = (r1 - mid.astype(F32)).astype(BF16)
    dot = functools.partial(jnp.dot, preferred_element_type=F32)
    return dot(hi, seg) + dot(mid, seg) + dot(lo, seg)


def _rope(y, c, s1, s2):
    return y * c + pltpu.roll(y, LANES - ROPE_DIMS // 2, 1) * s1 + pltpu.roll(y, ROPE_DIMS // 2, 1) * s2


def _unrope(dy, c, s1, s2):
    return dy * c + pltpu.roll(dy * s1, ROPE_DIMS // 2, 1) + pltpu.roll(dy * s2, LANES - ROPE_DIMS // 2, 1)


def _sigmoid(x):
    return 1.0 / (1.0 + jnp.exp(-x))


def _rms_fwd(x, gain, name):
    s_len, d = x.shape
    tm = 512

    def body(x_ref, g_ref, h_ref, r_ref):
        xv = x_ref[...]
        r = lax.rsqrt(jnp.mean(xv * xv, axis=-1, keepdims=True) + EPS)
        h_ref[...] = ((xv * r) * g_ref[...]).astype(BF16)
        r_ref[...] = r

    return pl.pallas_call(
        body, name=name, grid=(s_len // tm,),
        in_specs=[pl.BlockSpec((tm, d), lambda i: (i, 0)), pl.BlockSpec((1, d), lambda i: (0, 0))],
        out_specs=(pl.BlockSpec((tm, d), lambda i: (i, 0)), pl.BlockSpec((tm, 1), lambda i: (i, 0))),
        out_shape=(jax.ShapeDtypeStruct((s_len, d), BF16), jax.ShapeDtypeStruct((s_len, 1), F32)),
        compiler_params=_cparams(1),
    )(x, gain)


def _rms_bwd(dh, x, r, gain, add, name, bf16_copy=False):
    s_len, d = x.shape
    tm = 512

    def body(dh_ref, x_ref, r_ref, g_ref, add_ref, dx_ref, *rest):
        dg_ref = rest[-1]

        @pl.when(pl.program_id(0) == 0)
        def _():
            dg_ref[...] = jnp.zeros_like(dg_ref)
        rv = r_ref[...]
        xhat = x_ref[...] * rv
        dhv = dh_ref[...]
        dg_ref[...] += jnp.sum(dhv * xhat, axis=0, keepdims=True)
        dxhat = dhv * g_ref[...]
        dx = add_ref[...] + rv * (dxhat - xhat * jnp.mean(dxhat * xhat, axis=-1, keepdims=True))
        dx_ref[...] = dx
        if bf16_copy:
            rest[0][...] = dx.astype(BF16)

    row = pl.BlockSpec((tm, d), lambda i: (i, 0))
    vec = pl.BlockSpec((1, d), lambda i: (0, 0))
    out_specs = [row] + ([row] if bf16_copy else []) + [vec]
    out_shape = [jax.ShapeDtypeStruct((s_len, d), F32)] + ([jax.ShapeDtypeStruct((s_len, d), BF16)] if bf16_copy else [])
    out_shape.append(jax.ShapeDtypeStruct((1, d), F32))
    return pl.pallas_call(
        body, name=name, grid=(s_len // tm,),
        in_specs=[row, row, pl.BlockSpec((tm, 1), lambda i: (i, 0)), vec, row],
        out_specs=tuple(out_specs), out_shape=tuple(out_shape), compiler_params=_cparams(1),
    )(dh, x, r, gain, add)


def _resident(shape, index_map):
    return pl.BlockSpec(shape, index_map, pipeline_mode=pl.Buffered(1))


def _mm_rows(pairs, name, nt=False, tm=512, bias=None, sigmoid=False, res=None, out_dtypes=(F32,)):
    m = pairs[0][0].shape[0]
    n = pairs[0][1].shape[0] if nt else pairs[0][1].shape[1]
    n_pairs = len(pairs)
    has_bias, has_res = bias is not None, res is not None
    dims = (((1,), (1,)), ((), ())) if nt else (((1,), (0,)), ((), ()))

    def body(*refs):
        acc = None
        for p in range(n_pairs):
            t = lax.dot_general(refs[2 * p][...].astype(BF16), refs[2 * p + 1][...], dims, preferred_element_type=F32)
            acc = t if acc is None else acc + t
        pos = 2 * n_pairs
        if has_bias:
            acc = acc + refs[pos][...]
            pos += 1
        if sigmoid:
            acc = _sigmoid(acc)
        if has_res:
            acc = refs[pos][...] + acc
            pos += 1
        for o_ref in refs[pos:]:
            o_ref[...] = acc.astype(o_ref.dtype)

    in_specs, args = [], []
    for a, w, blk in pairs:
        k = a.shape[1]
        in_specs.append(pl.BlockSpec((tm, k), lambda i: (i, 0)))
        if nt:
            in_specs.append(_resident((n, k), lambda i, blk=blk: (0, blk)))
        else:
            in_specs.append(_resident((k, n), lambda i, blk=blk: (blk, 0)))
        args += [a, w]
    if has_bias:
        in_specs.append(_resident((1, n), lambda i: (0, 0)))
        args.append(bias)
    if has_res:
        in_specs.append(pl.BlockSpec((tm, n), lambda i: (i, 0)))
        args.append(res)
    out = pl.BlockSpec((tm, n), lambda i: (i, 0))
    outs = pl.pallas_call(
        body, name=name, grid=(m // tm,), in_specs=in_specs, out_specs=tuple([out] * len(out_dtypes)),
        out_shape=tuple(jax.ShapeDtypeStruct((m, n), dt) for dt in out_dtypes), compiler_params=_cparams(1),
    )(*args)
    return outs[0] if len(out_dtypes) == 1 else outs


def _mm_tn(a, b, name, tile=256):
    k, m = a.shape
    n = b.shape[1]
    dims = (((0,), (0,)), ((), ()))

    def body(a_ref, b_ref, o_ref):
        o_ref[...] = lax.dot_general(a_ref[...].astype(BF16), b_ref[...].astype(BF16), dims, preferred_element_type=F32)

    if n <= m:
        t = min(tile, m)
        grid, a_spec, b_spec = (m // t,), pl.BlockSpec((k, t), lambda i: (0, i)), _resident((k, n), lambda i: (0, 0))
        o_spec = pl.BlockSpec((t, n), lambda i: (i, 0))
    else:
        t = min(tile, n)
        grid, a_spec, b_spec = (n // t,), _resident((k, m), lambda i: (0, 0)), pl.BlockSpec((k, t), lambda i: (0, i))
        o_spec = pl.BlockSpec((m, t), lambda i: (0, i))
    return pl.pallas_call(
        body, name=name, grid=grid, in_specs=[a_spec, b_spec], out_specs=o_spec,
        out_shape=jax.ShapeDtypeStruct((m, n), F32), compiler_params=_cparams(1),
    )(a, b)


def _norm_rope(t, gain, c, s1, s2, seg):
    rs = lax.rsqrt(_seg_sum(t * t, seg) * (1.0 / HEAD_DIM) + EPS)
    return _rope((t * rs) * gain, c, s1, s2)


def _dup_half(y, half):
    lane = lax.broadcasted_iota(jnp.int32, y.shape, 1)
    rolled = pltpu.roll(y, HEAD_DIM, 1)
    keep = (lane < HEAD_DIM) if half == 0 else (lane >= HEAD_DIM)
    return jnp.where(keep, y, rolled)


def _qk_prep(proj, cb0, d, gqa, gq, gk, tabs, name):
    s_len = proj.shape[0]
    tm = 512
    rows = tm // d
    n_units = 4 if gqa else 2 * d
    n_q = 4 if gqa else 2
    n_in = 6

    def body(*refs):
        in_refs = refs[:n_in]
        gq_ref, gk_ref, c_ref, s1_ref, s2_ref, o_ref = refs[n_in:]
        seg = _seg_matrix(HEAD_DIM)

        def rows_of(ref, r):
            return ref[...] if d == 1 else ref[pl.ds(r, rows, stride=d), :]

        def put(unit_col, y):
            o_ref[:, unit_col * LANES:(unit_col + 1) * LANES] = y.astype(BF16)

        for r in range(d):
            c, s1, s2 = rows_of(c_ref, r), rows_of(s1_ref, r), rows_of(s2_ref, r)
            for b in range(n_in):
                t = rows_of(in_refs[b], r)
                if b < n_q:
                    put((b * d + r) if not gqa else b, _norm_rope(t, gq_ref[...], c, s1, s2, seg))
                elif not gqa:
                    sec, pair = (1, b - 2) if b < 4 else (2, b - 4)
                    y = _norm_rope(t, gk_ref[...], c, s1, s2, seg) if sec == 1 else t
                    put(sec * n_units + pair * d + r, y)
                else:
                    sec = 1 if b == 4 else 2
                    y = _norm_rope(t, gk_ref[...], c, s1, s2, seg) if sec == 1 else t
                    for u in range(n_units):
                        put(sec * n_units + u, _dup_half(y, u // 2))

    in_specs = [pl.BlockSpec((tm, LANES), lambda i, b=b: (i, cb0 + b)) for b in range(n_in)]
    vec = pl.BlockSpec((1, LANES), lambda i: (0, 0))
    tab = pl.BlockSpec((tm, LANES), lambda i: (i, 0))
    width = 3 * n_units * LANES
    return pl.pallas_call(
        body, name=name, grid=(s_len // tm,), in_specs=in_specs + [vec, vec, tab, tab, tab],
        out_specs=pl.BlockSpec((rows, width), lambda i: (i, 0)),
        out_shape=jax.ShapeDtypeStruct((s_len // d, width), BF16), compiler_params=_cparams(1),
    )(*([proj] * n_in), gq, gk, *tabs)


def _qk_prep_bwd(dqkv, proj, cb0, d, gqa, gq, gk, tabs, name):
    s_len = proj.shape[0]
    tm = 512
    rows = tm // d
    n_units = 4 if gqa else 2 * d
    n_q = 4 if gqa else 2
    n_in = 6

    def body(*refs):
        d_refs = refs[0:3]
        in_refs = refs[3:3 + n_in]
        gq_ref, gk_ref, c_ref, s1_ref, s2_ref, o_ref, dgq_ref, dgk_ref, stage = refs[3 + n_in:]
        seg = _seg_matrix(HEAD_DIM)

        @pl.when(pl.program_id(0) == 0)
        def _():
            dgq_ref[...] = jnp.zeros_like(dgq_ref)
            dgk_ref[...] = jnp.zeros_like(dgk_ref)

        def rows_of(ref, r):
            return ref[...] if d == 1 else ref[pl.ds(r, rows, stride=d), :]

        def unit(col):
            sec, u = divmod(col, n_units)
            return d_refs[sec][:, u * LANES:(u + 1) * LANES]

        def norm_bwd(dyr, t, gain, c, s1, s2, dg_ref):
            rs = lax.rsqrt(_seg_sum(t * t, seg) * (1.0 / HEAD_DIM) + EPS)
            that = t * rs
            dy = _unrope(dyr, c, s1, s2)
            dg_ref[...] += jnp.sum(dy * that, axis=0, keepdims=True)
            dthat = dy * gain
            return rs * (dthat - that * (_seg_sum(dthat * that, seg) * (1.0 / HEAD_DIM)))

        def fold(sec):
            tot = []
            for u in range(n_units):
                v = unit(sec * n_units + u)
                tot.append(v + pltpu.roll(v, HEAD_DIM, 1))
            lane = lax.broadcasted_iota(jnp.int32, tot[0].shape, 1)
            return jnp.where(lane < HEAD_DIM, tot[0] + tot[1], tot[2] + tot[3])

        for b in range(n_in):
            for r in range(d):
                c, s1, s2 = rows_of(c_ref, r), rows_of(s1_ref, r), rows_of(s2_ref, r)
                t = rows_of(in_refs[b], r)
                if b < n_q:
                    g = unit((b * d + r) if not gqa else b)
                    out = norm_bwd(g, t, gq_ref[...], c, s1, s2, dgq_ref)
                elif not gqa:
                    sec, pair = (1, b - 2) if b < 4 else (2, b - 4)
                    g = unit(sec * n_units + pair * d + r)
                    out = norm_bwd(g, t, gk_ref[...], c, s1, s2, dgk_ref) if sec == 1 else g
                else:
                    sec = 1 if b == 4 else 2
                    g = fold(sec)
                    out = norm_bwd(g, t, gk_ref[...], c, s1, s2, dgk_ref) if sec == 1 else g
                if d == 1:
                    o_ref[:, b * LANES:(b + 1) * LANES] = out.astype(BF16)
                else:
                    stage[pl.ds(r, rows, stride=d), :] = out
            if d != 1:
                o_ref[:, b * LANES:(b + 1) * LANES] = stage[...].astype(BF16)

    in_specs = [pl.BlockSpec((rows, n_units * LANES), lambda i: (i, 0))] * 3
    in_specs += [pl.BlockSpec((tm, LANES), lambda i, b=b: (i, cb0 + b)) for b in range(n_in)]
    vec = pl.BlockSpec((1, LANES), lambda i: (0, 0))
    tab = pl.BlockSpec((tm, LANES), lambda i: (i, 0))
    return pl.pallas_call(
        body, name=name, grid=(s_len // tm,), in_specs=in_specs + [vec, vec, tab, tab, tab],
        out_specs=(pl.BlockSpec((tm, n_in * LANES), lambda i: (i, 0)), vec, vec),
        out_shape=(jax.ShapeDtypeStruct((s_len, n_in * LANES), BF16), jax.ShapeDtypeStruct((1, LANES), F32),
                   jax.ShapeDtypeStruct((1, LANES), F32)),
        scratch_shapes=[pltpu.VMEM((tm, LANES), F32)], compiler_params=_cparams(1),
    )(*dqkv, *([proj] * n_in), gq, gk, *tabs)


def _head_masks(shape):
    lane = lax.broadcasted_iota(jnp.int32, shape, 1)
    return lane < HEAD_DIM, lane >= HEAD_DIM


def _band_fwd(qkv, n_units, max_dist, sinks, name):
    n_rows = qkv.shape[0]
    nb = n_rows // BLOCK
    scale = HEAD_DIM ** -0.5
    has_sink = sinks is not None

    qn, un = min(nb, BAND_Q_BLOCKS), BAND_UNITS
    ug = n_units // un

    def body(*refs):
        q_ref, kp_ref, km_ref, vp_ref, vm_ref = refs[:5]
        o_ref, lse_ref = refs[-2:]
        i = pl.program_id(1)
        qi = lax.broadcasted_iota(jnp.int32, (BLOCK, 2 * BLOCK), 0)
        kj = lax.broadcasted_iota(jnp.int32, (BLOCK, 2 * BLOCK), 1)
        dist = qi + BLOCK - kj
        band = (dist >= 0) & (dist <= max_dist)
        band_first = band & ((i > 0) | (kj >= BLOCK))
        m0, m1 = _head_masks((BLOCK, LANES))
        zero = jnp.zeros((BLOCK, LANES), BF16)
        for ub in range(un):
            cs = slice(ub * LANES, (ub + 1) * LANES)
            for qb in range(qn):
                rs = slice(qb * BLOCK, (qb + 1) * BLOCK)
                q = q_ref[rs, cs]
                if qb == 0:
                    kk = jnp.concatenate([kp_ref[:, cs], km_ref[0:BLOCK, cs]], axis=0)
                    vv = jnp.concatenate([vp_ref[:, cs], vm_ref[0:BLOCK, cs]], axis=0)
                    valid = band_first
                else:
                    kk = km_ref[(qb - 1) * BLOCK:(qb + 1) * BLOCK, cs]
                    vv = vm_ref[(qb - 1) * BLOCK:(qb + 1) * BLOCK, cs]
                    valid = band
                outs, lses = [], []
                for e, hm in enumerate((m0, m1)):
                    qe = jnp.where(hm, q, zero)
                    s = lax.dot_general(qe, kk, (((1,), (1,)), ((), ())), preferred_element_type=F32) * scale
                    s = jnp.where(valid, s, -jnp.inf)
                    mx = jnp.max(s, axis=-1, keepdims=True)
                    if has_sink:
                        sk = refs[5][ub][:, e * HEAD_DIM:e * HEAD_DIM + 1]
                        mx = jnp.maximum(mx, sk)
                    p = jnp.exp(s - mx)
                    den = jnp.sum(p, axis=-1, keepdims=True)
                    if has_sink:
                        den = den + jnp.exp(sk - mx)
                    pn = (p * (1.0 / den)).astype(BF16)
                    outs.append(jnp.dot(pn, vv, preferred_element_type=F32))
                    lses.append(mx + jnp.log(den))
                o_ref[rs, cs] = jnp.where(m0, outs[0], outs[1])
                lse_ref[rs, cs] = jnp.where(m0, jnp.broadcast_to(lses[0], (BLOCK, LANES)),
                                            jnp.broadcast_to(lses[1], (BLOCK, LANES)))

    def main(sec):
        return pl.BlockSpec((qn * BLOCK, un * LANES), lambda u, i: (i, sec * ug + u))

    def prev(sec):
        return pl.BlockSpec((BLOCK, un * LANES), lambda u, i: (jnp.maximum(i * qn - 1, 0), sec * ug + u))

    in_specs = [main(0), prev(1), main(1), prev(2), main(2)]
    args = [qkv] * 5
    if has_sink:
        in_specs.append(pl.BlockSpec((un, 1, LANES), lambda u, i: (u, 0, 0)))
        args.append(sinks)
    return pl.pallas_call(
        body, name=name, grid=(ug, nb // qn), in_specs=in_specs, out_specs=(main(0), main(0)),
        out_shape=(jax.ShapeDtypeStruct((n_rows, n_units * LANES), F32),) * 2, compiler_params=_cparams(2),
    )(*args)


def _band_bwd(qkv, do, lse, delta, n_units, max_dist, name):
    n_rows = qkv.shape[0]
    nb = n_rows // BLOCK
    scale = HEAD_DIM ** -0.5

    qn, un = min(nb, BAND_Q_BLOCKS), BAND_UNITS
    ug = n_units // un
    steps = nb // qn
    nt_dims = (((1,), (1,)), ((), ()))
    tn_dims = (((0,), (0,)), ((), ()))

    def body(qm_ref, qx_ref, kp_ref, km_ref, vp_ref, vm_ref, dom_ref, dox_ref, lm_ref, lx_ref, dm_ref, dx_ref,
             dq_ref, dk_ref, dv_ref):
        i = pl.program_id(1)
        m0, m1 = _head_masks((BLOCK, LANES))
        zero = jnp.zeros((BLOCK, LANES), BF16)
        qi = lax.broadcasted_iota(jnp.int32, (BLOCK, 2 * BLOCK), 0)
        kj = lax.broadcasted_iota(jnp.int32, (BLOCK, 2 * BLOCK), 1)
        dist = qi + BLOCK - kj
        band = (dist >= 0) & (dist <= max_dist)
        band_first = band & ((i > 0) | (kj >= BLOCK))
        qr = lax.broadcasted_iota(jnp.int32, (2 * BLOCK, BLOCK), 0)
        kc = lax.broadcasted_iota(jnp.int32, (2 * BLOCK, BLOCK), 1)
        dist2 = qr - kc
        band2 = (dist2 >= 0) & (dist2 <= max_dist)
        band2_last = band2 & ((qr < BLOCK) | (i < steps - 1))
        m0w, m1w = _head_masks((2 * BLOCK, LANES))
        zero2 = jnp.zeros((2 * BLOCK, LANES), BF16)

        def two(main_ref, next_ref, kb, cs):
            if kb < qn - 1:
                return main_ref[kb * BLOCK:(kb + 2) * BLOCK, cs]
            return jnp.concatenate([main_ref[kb * BLOCK:(kb + 1) * BLOCK, cs], next_ref[:, cs]], axis=0)

        for ub in range(un):
            cs = slice(ub * LANES, (ub + 1) * LANES)
            for qb in range(qn):
                rs = slice(qb * BLOCK, (qb + 1) * BLOCK)
                q = qm_ref[rs, cs]
                dob = dom_ref[rs, cs]
                lse_b = lm_ref[rs, cs]
                del_b = dm_ref[rs, cs]
                if qb == 0:
                    kk = jnp.concatenate([kp_ref[:, cs], km_ref[0:BLOCK, cs]], axis=0)
                    vv = jnp.concatenate([vp_ref[:, cs], vm_ref[0:BLOCK, cs]], axis=0)
                    valid = band_first
                else:
                    kk = km_ref[(qb - 1) * BLOCK:(qb + 1) * BLOCK, cs]
                    vv = vm_ref[(qb - 1) * BLOCK:(qb + 1) * BLOCK, cs]
                    valid = band
                dqs = []
                for e, hm in enumerate((m0, m1)):
                    col = slice(e * HEAD_DIM, e * HEAD_DIM + 1)
                    s = lax.dot_general(jnp.where(hm, q, zero), kk, nt_dims, preferred_element_type=F32) * scale
                    p = jnp.where(valid, jnp.exp(s - lse_b[:, col]), 0.0)
                    dp = lax.dot_general(jnp.where(hm, dob, zero), vv, nt_dims, preferred_element_type=F32)
                    ds = (p * (dp - del_b[:, col]) * scale).astype(BF16)
                    dqs.append(jnp.dot(ds, kk, preferred_element_type=F32))
                dq_ref[rs, cs] = jnp.where(m0, dqs[0], dqs[1])
            for kb in range(qn):
                rs = slice(kb * BLOCK, (kb + 1) * BLOCK)
                qq = two(qm_ref, qx_ref, kb, cs)
                dd = two(dom_ref, dox_ref, kb, cs)
                ll = two(lm_ref, lx_ref, kb, cs)
                de = two(dm_ref, dx_ref, kb, cs)
                k = km_ref[rs, cs]
                v = vm_ref[rs, cs]
                valid2 = band2 if kb < qn - 1 else band2_last
                dk = jnp.zeros((BLOCK, LANES), F32)
                dv = jnp.zeros((BLOCK, LANES), F32)
                for e, hm in enumerate((m0w, m1w)):
                    col = slice(e * HEAD_DIM, e * HEAD_DIM + 1)
                    qe = jnp.where(hm, qq, zero2)
                    doe = jnp.where(hm, dd, zero2)
                    s = lax.dot_general(qe, k, nt_dims, preferred_element_type=F32) * scale
                    p = jnp.where(valid2, jnp.exp(s - ll[:, col]), 0.0)
                    dp = lax.dot_general(doe, v, nt_dims, preferred_element_type=F32)
                    ds = (p * (dp - de[:, col]) * scale).astype(BF16)
                    dk = dk + lax.dot_general(ds, qe, tn_dims, preferred_element_type=F32)
                    dv = dv + lax.dot_general(p.astype(BF16), doe, tn_dims, preferred_element_type=F32)
                dk_ref[rs, cs] = dk
                dv_ref[rs, cs] = dv

    def main(sec):
        return pl.BlockSpec((qn * BLOCK, un * LANES), lambda u, i: (i, sec * ug + u))

    def prev(sec):
        return pl.BlockSpec((BLOCK, un * LANES), lambda u, i: (jnp.maximum(i * qn - 1, 0), sec * ug + u))

    def nxt(sec):
        return pl.BlockSpec((BLOCK, un * LANES), lambda u, i: (jnp.minimum((i + 1) * qn, nb - 1), sec * ug + u))

    in_specs = [main(0), nxt(0), prev(1), main(1), prev(2), main(2),
                main(0), nxt(0), main(0), nxt(0), main(0), nxt(0)]
    args = [qkv] * 6 + [do, do, lse, lse, delta, delta]
    shp = jax.ShapeDtypeStruct((n_rows, n_units * LANES), F32)
    return pl.pallas_call(
        body, name=name, grid=(ug, steps), in_specs=in_specs, out_specs=(main(0), main(0), main(0)),
        out_shape=(shp, shp, shp), compiler_params=_cparams(2),
    )(*args)


def _merge_groups(os_, lses, dils, name):
    s_len = os_[0].shape[0] * dils[0]
    tm = 512

    def body(*refs):
        o_refs, l_refs = refs[0:3], refs[3:6]
        o_ref, lse_ref = refs[6:8]
        so, sl = refs[8:11], refs[11:14]
        for pair in range(2):
            for g, d in enumerate(dils):
                rows = tm // d
                for r in range(d):
                    col = slice((pair * d + r) * LANES, (pair * d + r + 1) * LANES)
                    if d == 1:
                        so[g][...] = o_refs[g][:, col]
                        sl[g][...] = l_refs[g][:, col]
                    else:
                        so[g][pl.ds(r, rows, stride=d), :] = o_refs[g][:, col]
                        sl[g][pl.ds(r, rows, stride=d), :] = l_refs[g][:, col]
            l0, l1, l2 = sl[0][...], sl[1][...], sl[2][...]
            mx = jnp.maximum(jnp.maximum(l0, l1), l2)
            e0, e1, e2 = jnp.exp(l0 - mx), jnp.exp(l1 - mx), jnp.exp(l2 - mx)
            den = e0 + e1 + e2
            inv = 1.0 / den
            o_ref[:, pair * LANES:(pair + 1) * LANES] = (so[0][...] * (e0 * inv) + so[1][...] * (e1 * inv)
                                                         + so[2][...] * (e2 * inv))
            lse_ref[:, pair * LANES:(pair + 1) * LANES] = mx + jnp.log(den)

    in_specs = [pl.BlockSpec((tm // d, 2 * d * LANES), lambda i: (i, 0)) for d in dils] * 2
    out = pl.BlockSpec((tm, 2 * LANES), lambda i: (i, 0))
    shp = jax.ShapeDtypeStruct((s_len, 2 * LANES), F32)
    return pl.pallas_call(
        body, name=name, grid=(s_len // tm,), in_specs=in_specs, out_specs=(out, out), out_shape=(shp, shp),
        scratch_shapes=[pltpu.VMEM((tm, LANES), F32)] * 6, compiler_params=_cparams(1),
    )(*os_, *lses)


def _bwd_prep(do, o, lse, dils, sinks, name):
    s_len, width = do.shape
    n_pairs = width // LANES
    tm = 512
    has_sink = sinks is not None
    n_g = len(dils)

    def body(*refs):
        do_ref, o_ref, lse_ref = refs[:3]
        pos = 3
        if has_sink:
            sink_ref = refs[pos]
            pos += 1
        outs = refs[pos:pos + 3 * n_g]
        pos += 3 * n_g
        if has_sink:
            dsink_ref = refs[pos]
            pos += 1
        s_do, s_l, s_d = refs[pos:pos + 3]
        seg = _seg_matrix(HEAD_DIM)

        if has_sink:
            @pl.when(pl.program_id(0) == 0)
            def _():
                dsink_ref[...] = jnp.zeros_like(dsink_ref)

        for pair in range(n_pairs):
            col = slice(pair * LANES, (pair + 1) * LANES)
            dov = do_ref[:, col]
            lv = lse_ref[:, col]
            delta = _seg_sum(dov * o_ref[:, col], seg)
            if has_sink:
                dsink_ref[pair] += -jnp.sum(jnp.exp(sink_ref[pair] - lv) * delta, axis=0, keepdims=True)
            s_do[...] = dov
            s_l[...] = lv
            s_d[...] = delta
            for g, d in enumerate(dils):
                rows = tm // d
                for r in range(d):
                    oc = slice((pair * d + r) * LANES, (pair * d + r + 1) * LANES)
                    if d == 1:
                        a, b, c = s_do[...], s_l[...], s_d[...]
                    else:
                        a = s_do[pl.ds(r, rows, stride=d), :]
                        b = s_l[pl.ds(r, rows, stride=d), :]
                        c = s_d[pl.ds(r, rows, stride=d), :]
                    outs[3 * g][:, oc] = a.astype(BF16)
                    outs[3 * g + 1][:, oc] = b
                    outs[3 * g + 2][:, oc] = c

    row = pl.BlockSpec((tm, width), lambda i: (i, 0))
    in_specs = [row, row, row]
    args = [do, o, lse]
    if has_sink:
        in_specs.append(pl.BlockSpec((n_pairs, 1, LANES), lambda i: (0, 0, 0)))
        args.append(sinks)
    out_specs, out_shape = [], []
    for d in dils:
        for dt in (BF16, F32, F32):
            out_specs.append(pl.BlockSpec((tm // d, n_pairs * d * LANES), lambda i: (i, 0)))
            out_shape.append(jax.ShapeDtypeStruct((s_len // d, n_pairs * d * LANES), dt))
    if has_sink:
        out_specs.append(pl.BlockSpec((n_pairs, 1, LANES), lambda i: (0, 0, 0)))
        out_shape.append(jax.ShapeDtypeStruct((n_pairs, 1, LANES), F32))
    return pl.pallas_call(
        body, name=name, grid=(s_len // tm,), in_specs=in_specs, out_specs=tuple(out_specs),
        out_shape=tuple(out_shape), scratch_shapes=[pltpu.VMEM((tm, LANES), F32)] * 3, compiler_params=_cparams(1),
    )(*args)


def _mem_kv(mem, mem_gain, w_kv, k_gain, name):
    m_len = mem.shape[0]
    kw = M_HEADS * M_HEAD_DIM

    def body(mem_ref, mg_ref, w_ref, kg_ref, k_ref, v_ref):
        mv = mem_ref[...]
        r = lax.rsqrt(jnp.mean(mv * mv, axis=-1, keepdims=True) + EPS)
        mn = ((mv * r) * mg_ref[...]).astype(BF16)
        kv = jnp.dot(mn, w_ref[...], preferred_element_type=F32)
        for h in range(M_HEADS):
            col = slice(h * M_HEAD_DIM, (h + 1) * M_HEAD_DIM)
            t = kv[:, col]
            rk = lax.rsqrt(jnp.mean(t * t, axis=-1, keepdims=True) + EPS)
            k_ref[:, col] = ((t * rk) * kg_ref[...]).astype(BF16)
        v_ref[...] = kv[:, kw:].astype(BF16)

    shp = jax.ShapeDtypeStruct((m_len, kw), BF16)
    return pl.pallas_call(body, name=name, out_shape=(shp, shp),
                          compiler_params=pltpu.CompilerParams(vmem_limit_bytes=VMEM_LIMIT_BYTES))(mem, mem_gain, w_kv, k_gain)


def _mem_kv_bwd(mem, mem_gain, w_kv, k_gain, dk, dv, name):
    m_len, d = mem.shape
    kw = M_HEADS * M_HEAD_DIM

    def body(mem_ref, mg_ref, w_ref, kg_ref, dk_ref, dv_ref, dw_ref, dmg_ref, dkg_ref, dkv_ref):
        mv = mem_ref[...]
        r = lax.rsqrt(jnp.mean(mv * mv, axis=-1, keepdims=True) + EPS)
        mhat = mv * r
        mn = (mhat * mg_ref[...]).astype(BF16)
        kv = jnp.dot(mn, w_ref[...], preferred_element_type=F32)
        dkg = jnp.zeros((1, M_HEAD_DIM), F32)
        for h in range(M_HEADS):
            col = slice(h * M_HEAD_DIM, (h + 1) * M_HEAD_DIM)
            t = kv[:, col]
            rk = lax.rsqrt(jnp.mean(t * t, axis=-1, keepdims=True) + EPS)
            that = t * rk
            dy = dk_ref[:, col]
            dkg = dkg + jnp.sum(dy * that, axis=0, keepdims=True)
            dthat = dy * kg_ref[...]
            dkv_ref[:, col] = (rk * (dthat - that * jnp.mean(dthat * that, axis=-1, keepdims=True))).astype(BF16)
        dkv_ref[:, kw:] = dv_ref[...].astype(BF16)
        dkg_ref[...] = dkg
        dkv = dkv_ref[...]
        dw_ref[...] = lax.dot_general(mn, dkv, (((0,), (0,)), ((), ())), preferred_element_type=F32)
        dmn = lax.dot_general(dkv, w_ref[...], (((1,), (1,)), ((), ())), preferred_element_type=F32)
        dmg_ref[...] = jnp.sum(dmn * mhat, axis=0, keepdims=True)

    return pl.pallas_call(
        body, name=name,
        out_shape=(jax.ShapeDtypeStruct((d, 2 * kw), F32), jax.ShapeDtypeStruct((1, d), F32),
                   jax.ShapeDtypeStruct((1, M_HEAD_DIM), F32)),
        scratch_shapes=[pltpu.VMEM((m_len, 2 * kw), BF16)],
        compiler_params=pltpu.CompilerParams(vmem_limit_bytes=VMEM_LIMIT_BYTES),
    )(mem, mem_gain, w_kv, k_gain, dk, dv)


def _mem_attn_fwd(proj, cidx, mk, mv, q_gain, name):
    s_len = proj.shape[0]
    kw = M_HEADS * M_HEAD_DIM
    tm = 512
    scale = M_HEAD_DIM ** -0.5

    def body(q_ref, k_ref, v_ref, g_ref, o_ref):
        for h in range(M_HEADS):
            col = slice(h * M_HEAD_DIM, (h + 1) * M_HEAD_DIM)
            t = q_ref[:, col]
            rs = lax.rsqrt(jnp.mean(t * t, axis=-1, keepdims=True) + EPS)
            qn = ((t * rs) * g_ref[...]).astype(BF16)
            s = lax.dot_general(qn, k_ref[:, col], (((1,), (1,)), ((), ())), preferred_element_type=F32) * scale
            mx = jnp.max(s, axis=-1, keepdims=True)
            p = jnp.exp(s - mx)
            pn = (p * (1.0 / jnp.sum(p, axis=-1, keepdims=True))).astype(BF16)
            o_ref[:, col] = jnp.dot(pn, v_ref[:, col], preferred_element_type=F32).astype(BF16)

    whole = pl.BlockSpec((MEM_LEN, kw), lambda i: (0, 0))
    return pl.pallas_call(
        body, name=name, grid=(s_len // tm,),
        in_specs=[pl.BlockSpec((tm, kw), lambda i: (i, cidx)), whole, whole, pl.BlockSpec((1, M_HEAD_DIM), lambda i: (0, 0))],
        out_specs=pl.BlockSpec((tm, kw), lambda i: (i, 0)),
        out_shape=jax.ShapeDtypeStruct((s_len, kw), BF16), compiler_params=_cparams(1),
    )(proj, mk, mv, q_gain)


def _mem_attn_bwd(proj, cidx, mk, mv, q_gain, do, name):
    s_len = proj.shape[0]
    kw = M_HEADS * M_HEAD_DIM
    tm = 512
    scale = M_HEAD_DIM ** -0.5

    def body(q_ref, k_ref, v_ref, g_ref, do_ref, dq_ref, dk_ref, dv_ref, dg_ref):
        @pl.when(pl.program_id(0) == 0)
        def _():
            dk_ref[...] = jnp.zeros_like(dk_ref)
            dv_ref[...] = jnp.zeros_like(dv_ref)
            dg_ref[...] = jnp.zeros_like(dg_ref)

        for h in range(M_HEADS):
            col = slice(h * M_HEAD_DIM, (h + 1) * M_HEAD_DIM)
            t = q_ref[:, col]
            rs = lax.rsqrt(jnp.mean(t * t, axis=-1, keepdims=True) + EPS)
            that = t * rs
            qn = (that * g_ref[...]).astype(BF16)
            kh, vh = k_ref[:, col], v_ref[:, col]
            dob = do_ref[:, col].astype(BF16)
            s = lax.dot_general(qn, kh, (((1,), (1,)), ((), ())), preferred_element_type=F32) * scale
            mx = jnp.max(s, axis=-1, keepdims=True)
            p = jnp.exp(s - mx)
            p = p * (1.0 / jnp.sum(p, axis=-1, keepdims=True))
            dp = lax.dot_general(dob, vh, (((1,), (1,)), ((), ())), preferred_element_type=F32)
            ds = (p * (dp - jnp.sum(p * dp, axis=-1, keepdims=True)) * scale).astype(BF16)
            dqn = jnp.dot(ds, kh, preferred_element_type=F32)
            dk_ref[:, col] += lax.dot_general(ds, qn, (((0,), (0,)), ((), ())), preferred_element_type=F32)
            dv_ref[:, col] += lax.dot_general(p.astype(BF16), dob, (((0,), (0,)), ((), ())), preferred_element_type=F32)
            dg_ref[...] += jnp.sum(dqn * that, axis=0, keepdims=True)
            dthat = dqn * g_ref[...]
            dq_ref[:, col] = (rs * (dthat - that * jnp.mean(dthat * that, axis=-1, keepdims=True))).astype(BF16)

    whole = pl.BlockSpec((MEM_LEN, kw), lambda i: (0, 0))
    vec = pl.BlockSpec((1, M_HEAD_DIM), lambda i: (0, 0))
    row = pl.BlockSpec((tm, kw), lambda i: (i, 0))
    return pl.pallas_call(
        body, name=name, grid=(s_len // tm,),
        in_specs=[pl.BlockSpec((tm, kw), lambda i: (i, cidx)), whole, whole, vec, row],
        out_specs=(row, whole, whole, vec),
        out_shape=(jax.ShapeDtypeStruct((s_len, kw), BF16), jax.ShapeDtypeStruct((MEM_LEN, kw), F32),
                   jax.ShapeDtypeStruct((MEM_LEN, kw), F32), jax.ShapeDtypeStruct((1, M_HEAD_DIM), F32)),
        compiler_params=_cparams(1),
    )(proj, mk, mv, q_gain, do)


def _gate_merge(gates, pa, pb, pm, name):
    s_len, d = pa.shape
    tm = 256

    def body(g_ref, a_ref, b_ref, m_ref, o_ref):
        f = lambda v: v.astype(F32)
        o_ref[...] = (f(g_ref[:, 0:d]) * f(a_ref[...]) + f(g_ref[:, d:2 * d]) * f(b_ref[...])
                      + f(g_ref[:, 2 * d:3 * d]) * f(m_ref[...])).astype(BF16)

    row = pl.BlockSpec((tm, d), lambda i: (i, 0))
    return pl.pallas_call(
        body, name=name, grid=(s_len // tm,), in_specs=[pl.BlockSpec((tm, 3 * d), lambda i: (i, 0)), row, row, row],
        out_specs=row, out_shape=jax.ShapeDtypeStruct((s_len, d), BF16), compiler_params=_cparams(1),
    )(gates, pa, pb, pm)


def _gate_merge_bwd(dmerged, gates, pa, pb, pm, name):
    s_len, d = pa.shape
    tm = 256

    def body(dm_ref, g_ref, a_ref, b_ref, m_ref, da_ref, db_ref, dmm_ref, dg_ref, dbg_ref):
        @pl.when(pl.program_id(0) == 0)
        def _():
            dbg_ref[...] = jnp.zeros_like(dbg_ref)
        dm = dm_ref[...]
        for k, (p_ref, dp_ref) in enumerate(((a_ref, da_ref), (b_ref, db_ref), (m_ref, dmm_ref))):
            col = slice(k * d, (k + 1) * d)
            g = g_ref[:, col].astype(F32)
            dp_ref[...] = (dm * g).astype(BF16)
            dpre = (dm * p_ref[...].astype(F32)) * (g * (1.0 - g))
            dbg_ref[:, col] += jnp.sum(dpre, axis=0, keepdims=True)
            dg_ref[:, col] = dpre.astype(BF16)

    row = pl.BlockSpec((tm, d), lambda i: (i, 0))
    wide = pl.BlockSpec((tm, 3 * d), lambda i: (i, 0))
    shp = jax.ShapeDtypeStruct((s_len, d), BF16)
    return pl.pallas_call(
        body, name=name, grid=(s_len // tm,), in_specs=[row, wide, row, row, row],
        out_specs=(row, row, row, wide, pl.BlockSpec((1, 3 * d), lambda i: (0, 0))),
        out_shape=(shp, shp, shp, jax.ShapeDtypeStruct((s_len, 3 * d), BF16), jax.ShapeDtypeStruct((1, 3 * d), F32)),
        compiler_params=_cparams(1),
    )(dmerged, gates, pa, pb, pm)


CONV_CHUNK = 256


def _pick_row(tile, j):
    row = lax.broadcasted_iota(jnp.int32, tile.shape, 0)
    return jnp.sum(jnp.where(row == j, tile, jnp.zeros_like(tile)), axis=0, keepdims=True)


def _rows_before(ref, start, k):
    cur = ref[pl.ds(start, CONV_CHUNK), :].astype(F32)
    prev = ref[pl.ds(pl.multiple_of(jnp.maximum(start - 16, 0), 16), 16), :].astype(F32)
    prev = jnp.where(start > 0, prev, jnp.zeros_like(prev))
    rolled = pltpu.roll(cur, k, 0)
    row = lax.broadcasted_iota(jnp.int32, cur.shape, 0)
    for j in range(k):
        rolled = jnp.where(row == j, _pick_row(prev, 16 - k + j), rolled)
    return rolled


def _rows_after(ref, start, k):
    cur = ref[pl.ds(start, CONV_CHUNK), :]
    nxt = ref[pl.ds(pl.multiple_of(start + CONV_CHUNK, 8), 8), :]
    rolled = pltpu.roll(cur, CONV_CHUNK - k, 0)
    row = lax.broadcasted_iota(jnp.int32, cur.shape, 0)
    for j in range(k):
        rolled = jnp.where(row == CONV_CHUNK - k + j, _pick_row(nxt, j), rolled)
    return rolled


def _conv_pre(u_ref, w_ref, b_ref, start):
    u2 = _rows_before(u_ref, start, 2)
    u1 = _rows_before(u_ref, start, 1)
    u0 = u_ref[pl.ds(start, CONV_CHUNK), :].astype(F32)
    c = ((b_ref[...] + w_ref[0:1, :] * u2) + w_ref[1:2, :] * u1) + w_ref[2:3, :] * u0
    return c, (u2, u1, u0)


def _conv_glu(u, conv_w, conv_b, name):
    s_len = u.shape[0]
    nblk = D_FF // LANES

    def body(ua_ref, ug_ref, wa_ref, wg_ref, ba_ref, bg_ref, o_ref):
        def chunk(ci, carry):
            start = pl.multiple_of(ci * CONV_CHUNK, CONV_CHUNK)
            ca, _ = _conv_pre(ua_ref, wa_ref, ba_ref, start)
            cg, _ = _conv_pre(ug_ref, wg_ref, bg_ref, start)
            o_ref[pl.ds(start, CONV_CHUNK), :] = ((ca * _sigmoid(ca)) * cg).astype(BF16)
            return carry
        lax.fori_loop(0, s_len // CONV_CHUNK, chunk, 0)

    def col(rows, off):
        return pl.BlockSpec((rows, LANES), lambda j: (0, off + j))

    return pl.pallas_call(
        body, name=name, grid=(nblk,),
        in_specs=[col(s_len, 0), col(s_len, nblk), col(3, 0), col(3, nblk), col(1, 0), col(1, nblk)],
        out_specs=col(s_len, 0), out_shape=jax.ShapeDtypeStruct((s_len, D_FF), BF16), compiler_params=_cparams(1),
    )(u, u, conv_w, conv_w, conv_b, conv_b)


def _conv_glu_bwd(dact, u, conv_w, conv_b, name):
    s_len = u.shape[0]
    nblk = D_FF // LANES
    n_chunks = s_len // CONV_CHUNK

    def body(da_ref, ua_ref, ug_ref, wa_ref, wg_ref, ba_ref, bg_ref,
             dua_ref, dug_ref, dwa_ref, dwg_ref, dba_ref, dbg_ref, sa, sg):
        sa[pl.ds(s_len, 8), :] = jnp.zeros((8, LANES), F32)
        sg[pl.ds(s_len, 8), :] = jnp.zeros((8, LANES), F32)
        zero = jnp.zeros((1, LANES), F32)

        def chunk1(ci, carry):
            start = pl.multiple_of(ci * CONV_CHUNK, CONV_CHUNK)
            ca, ua = _conv_pre(ua_ref, wa_ref, ba_ref, start)
            cg, ug = _conv_pre(ug_ref, wg_ref, bg_ref, start)
            dact_v = da_ref[pl.ds(start, CONV_CHUNK), :].astype(F32)
            sig = _sigmoid(ca)
            dcg = dact_v * (ca * sig)
            dca = (dact_v * cg) * (sig * (1.0 + ca * (1.0 - sig)))
            sa[pl.ds(start, CONV_CHUNK), :] = dca
            sg[pl.ds(start, CONV_CHUNK), :] = dcg
            out = [carry[0] + jnp.sum(dca, axis=0, keepdims=True), carry[1] + jnp.sum(dcg, axis=0, keepdims=True)]
            for j in range(3):
                out.append(carry[2 + j] + jnp.sum(dca * ua[j], axis=0, keepdims=True))
            for j in range(3):
                out.append(carry[5 + j] + jnp.sum(dcg * ug[j], axis=0, keepdims=True))
            return tuple(out)

        acc = lax.fori_loop(0, n_chunks, chunk1, (zero,) * 8)
        dba_ref[...] = acc[0]
        dbg_ref[...] = acc[1]
        for j in range(3):
            dwa_ref[j:j + 1, :] = acc[2 + j]
            dwg_ref[j:j + 1, :] = acc[5 + j]

        def chunk2(ci, carry):
            start = pl.multiple_of(ci * CONV_CHUNK, CONV_CHUNK)
            for s_ref, w_ref, o_ref in ((sa, wa_ref, dua_ref), (sg, wg_ref, dug_ref)):
                d0 = s_ref[pl.ds(start, CONV_CHUNK), :]
                d1 = _rows_after(s_ref, start, 1)
                d2 = _rows_after(s_ref, start, 2)
                o_ref[pl.ds(start, CONV_CHUNK), :] = (w_ref[2:3, :] * d0 + w_ref[1:2, :] * d1
                                                      + w_ref[0:1, :] * d2).astype(BF16)
            return carry
        lax.fori_loop(0, n_chunks, chunk2, 0)

    def col(rows, off):
        return pl.BlockSpec((rows, LANES), lambda j: (0, off + j))

    big = jax.ShapeDtypeStruct((s_len, D_FF), BF16)
    return pl.pallas_call(
        body, name=name, grid=(nblk,),
        in_specs=[col(s_len, 0), col(s_len, 0), col(s_len, nblk), col(3, 0), col(3, nblk), col(1, 0), col(1, nblk)],
        out_specs=(col(s_len, 0), col(s_len, 0), col(3, 0), col(3, 0), col(1, 0), col(1, 0)),
        out_shape=(big, big, jax.ShapeDtypeStruct((3, D_FF), F32), jax.ShapeDtypeStruct((3, D_FF), F32),
                   jax.ShapeDtypeStruct((1, D_FF), F32), jax.ShapeDtypeStruct((1, D_FF), F32)),
        scratch_shapes=[pltpu.VMEM((s_len + 8, LANES), F32)] * 2, compiler_params=_cparams(1),
    )(dact, u, u, conv_w, conv_w, conv_b, conv_b)


def _loss_head(y, target, name):
    s_len, d = y.shape
    tm = 512

    def body(y_ref, t_ref, dy_ref, dyb_ref, l_ref):
        @pl.when(pl.program_id(0) == 0)
        def _():
            l_ref[...] = jnp.zeros_like(l_ref)
        err = y_ref[...] - t_ref[...]
        dy = err * (1.0 / d)
        dy_ref[...] = dy
        dyb_ref[...] = dy.astype(BF16)
        part = 0.5 * jnp.sum(jnp.mean(err * err, axis=-1, keepdims=True), axis=0, keepdims=True)
        l_ref[...] += jnp.broadcast_to(part, l_ref.shape)

    row = pl.BlockSpec((tm, d), lambda i: (i, 0))
    return pl.pallas_call(
        body, name=name, grid=(s_len // tm,), in_specs=[row, row],
        out_specs=(row, row, pl.BlockSpec((8, LANES), lambda i: (0, 0))),
        out_shape=(jax.ShapeDtypeStruct((s_len, d), F32), jax.ShapeDtypeStruct((s_len, d), BF16),
                   jax.ShapeDtypeStruct((8, LANES), F32)),
        compiler_params=_cparams(1),
    )(y, target)


def _rope_tables(positions):
    half = ROPE_DIMS // 2
    freqs = jnp.exp(jnp.arange(half, dtype=F32) * (-2.0 * math.log(ROPE_THETA) / ROPE_DIMS))
    ang = positions.reshape(-1).astype(F32)[:, None] * freqs
    cos, sin = jnp.cos(ang), jnp.sin(ang)
    n = ang.shape[0]
    zeros = lambda w: jnp.zeros((n, w), F32)
    c = jnp.concatenate([cos, cos, jnp.ones((n, HEAD_DIM - ROPE_DIMS), F32)], axis=1)
    s1 = jnp.concatenate([-sin, zeros(HEAD_DIM - half)], axis=1)
    s2 = jnp.concatenate([zeros(half), sin, zeros(HEAD_DIM - ROPE_DIMS)], axis=1)
    return tuple(jnp.tile(t, (1, 2)) for t in (c, s1, s2))


def _two(v):
    return jnp.tile(v.reshape(1, HEAD_DIM), (1, 2))


def _fold_heads(g):
    return g[0, :HEAD_DIM] + g[0, HEAD_DIM:]


MIX_WEIGHTS = ('w_gate', 'w_mem_kv', 'w_o_a', 'w_o_b', 'w_o_m', 'w_out')
FFN_WEIGHTS = ('w_up', 'conv_w', 'w_down')


def _device_step(x, mem, positions, target, w, hooks=None):
    tabs = _rope_tables(positions)
    dils = tuple(d for _, d in A_GROUPS)
    grads = {}
    w = dict(w)

    h, r1 = _rms_fwd(x, w['attn_norm'], "rms1")
    proj = _mm_rows([(h, w['w_in'], 0)], "mm_in")

    qkv_a, o_g, lse_g = [], [], []
    for gi, (window, d) in enumerate(A_GROUPS):
        gq, gk = _two(w['a_q_norm'][gi]), _two(w['a_k_norm'][gi])
        qkv = _qk_prep(proj, 6 * gi, d, False, gq, gk, tabs, f"qk_prep_a{gi}")
        o, lse = _band_fwd(qkv, 2 * d, window // d, None, f"band_fwd_a{gi}")
        qkv_a.append(qkv)
        o_g.append(o)
        lse_g.append(lse)
    o_a, lse_a = _merge_groups(o_g, lse_g, dils, "merge_a")

    gbq, gbk = _two(w['b_q_norm']), _two(w['b_k_norm'])
    sinks = jnp.repeat(w['b_sinks'].reshape(4, 2), HEAD_DIM, axis=1).reshape(4, 1, LANES)
    qkv_b = _qk_prep(proj, 18, 1, True, gbq, gbk, tabs, "qk_prep_b")
    o_b, lse_b = _band_fwd(qkv_b, 4, B_WINDOW - 1, sinks, "band_fwd_b")

    if hooks is not None:
        w.update(hooks.weights('mix', o_b))
    gates = _mm_rows([(h, w['w_gate'], 0)], "mm_gate", bias=w['b_gate'], sigmoid=True, out_dtypes=(BF16,))
    mk, mv = _mem_kv(mem, w['mem_norm'], w['w_mem_kv'], w['m_k_norm'], "mem_kv")
    o_m = _mem_attn_fwd(proj, 6, mk, mv, w['m_q_norm'], "mem_attn")

    pa = _mm_rows([(o_a, w['w_o_a'], 0)], "mm_oa", out_dtypes=(BF16,))
    pb = _mm_rows([(o_b, w['w_o_b'], 0)], "mm_ob", out_dtypes=(BF16,))
    pm = _mm_rows([(o_m, w['w_o_m'], 0)], "mm_om", out_dtypes=(BF16,))
    merged = _gate_merge(gates, pa, pb, pm, "gate_merge")
    x1 = _mm_rows([(merged, w['w_out'], 0)], "mm_out", res=x)

    if hooks is not None:
        w.update(hooks.weights('ffn', x1))
    h2, r2 = _rms_fwd(x1, w['ffn_norm'], "rms2")
    u = _mm_rows([(h2, w['w_up'], 0)], "mm_up", out_dtypes=(BF16,))
    act = _conv_glu(u, w['conv_w'], w['conv_b'], "conv_glu")
    y = _mm_rows([(act, w['w_down'], 0)], "mm_down", res=x1)
    dy, dy_b, loss = _loss_head(y, target, "loss_head")

    dact = _mm_rows([(dy_b, w['w_down'], 0)], "mm_d_act", nt=True, out_dtypes=(BF16,))
    grads['w_down'] = _mm_tn(act, dy_b, "mm_dw_down")
    du_a, du_g, dcw_a, dcw_g, dcb_a, dcb_g = _conv_glu_bwd(dact, u, w['conv_w'], w['conv_b'], "conv_glu_bwd")
    grads['conv_w'] = jnp.concatenate([dcw_a, dcw_g], axis=1)
    grads['conv_b'] = jnp.concatenate([dcb_a, dcb_g], axis=1)
    dh2 = _mm_rows([(du_a, w['w_up'], 0), (du_g, w['w_up'], 1)], "mm_d_h2", nt=True)
    grads['w_up'] = jnp.concatenate([_mm_tn(h2, du_a, "mm_dw_up_a"), _mm_tn(h2, du_g, "mm_dw_up_g")], axis=1)
    ffn_gain = w['ffn_norm']
    if hooks is not None:
        ffn_gain = ffn_gain + hooks.grads('ffn', grads)[0:1, 0:1]
    dx1, dx1_b, grads['ffn_norm'] = _rms_bwd(dh2, x1, r2, ffn_gain, dy, "rms2_bwd", bf16_copy=True)

    dmerged = _mm_rows([(dx1_b, w['w_out'], 0)], "mm_d_merged", nt=True)
    grads['w_out'] = _mm_tn(merged, dx1_b, "mm_dw_out")
    dpa, dpb, dpm, dgpre, grads['b_gate'] = _gate_merge_bwd(dmerged, gates, pa, pb, pm, "gate_merge_bwd")
    do_a = _mm_rows([(dpa, w['w_o_a'], 0)], "mm_d_oa", nt=True)
    do_b = _mm_rows([(dpb, w['w_o_b'], 0)], "mm_d_ob", nt=True)
    do_m = _mm_rows([(dpm, w['w_o_m'], 0)], "mm_d_om", nt=True)
    grads['w_o_a'] = _mm_tn(o_a, dpa, "mm_dw_oa")
    grads['w_o_b'] = _mm_tn(o_b, dpb, "mm_dw_ob")
    grads['w_o_m'] = _mm_tn(o_m, dpm, "mm_dw_om")
    grads['w_gate'] = _mm_tn(h, dgpre, "mm_dw_gate")
    dq_m, dmk, dmv, grads['m_q_norm'] = _mem_attn_bwd(proj, 6, mk, mv, w['m_q_norm'], do_m, "mem_attn_bwd")
    grads['w_mem_kv'], grads['mem_norm'], grads['m_k_norm'] = _mem_kv_bwd(
        mem, w['mem_norm'], w['w_mem_kv'], w['m_k_norm'], dmk, dmv, "mem_kv_bwd")
    a_gain = w['a_q_norm']
    if hooks is not None:
        a_gain = a_gain + hooks.grads('mix', grads)[0:1, 0:1]

    prep = _bwd_prep(do_a, o_a, lse_a, dils, None, "bwd_prep_a")
    dproj, dgq_a, dgk_a = [], [], []
    for gi, (window, d) in enumerate(A_GROUPS):
        gq, gk = _two(a_gain[gi]), _two(w['a_k_norm'][gi])
        dqkv = _band_bwd(qkv_a[gi], prep[3 * gi], prep[3 * gi + 1], prep[3 * gi + 2], 2 * d, window // d,
                         f"band_bwd_a{gi}")
        dp, dgq, dgk = _qk_prep_bwd(dqkv, proj, 6 * gi, d, False, gq, gk, tabs, f"qk_prep_bwd_a{gi}")
        dproj.append(dp)
        dgq_a.append(_fold_heads(dgq))
        dgk_a.append(_fold_heads(dgk))
    grads['a_q_norm'] = jnp.stack(dgq_a)
    grads['a_k_norm'] = jnp.stack(dgk_a)

    do_bu, lse_bu, delta_bu, dsink = _bwd_prep(do_b, o_b, lse_b, (1,), sinks, "bwd_prep_b")
    dqkv = _band_bwd(qkv_b, do_bu, lse_bu, delta_bu, 4, B_WINDOW - 1, "band_bwd_b")
    dp_b, dgq, dgk = _qk_prep_bwd(dqkv, proj, 18, 1, True, gbq, gbk, tabs, "qk_prep_bwd_b")
    dproj.append(dp_b)
    grads['b_q_norm'] = _fold_heads(dgq)
    grads['b_k_norm'] = _fold_heads(dgk)
    grads['b_sinks'] = jnp.stack([dsink[:, 0, 0], dsink[:, 0, HEAD_DIM]], axis=1).reshape(8)

    dproj.append(dq_m)

    cols = (0, 1, 2, 3, 6)
    grads['w_in'] = jnp.concatenate([_mm_tn(h, dp, f"mm_dw_in{k}") for k, dp in enumerate(dproj)], axis=1)
    dh = _mm_rows([(dp, w['w_in'], c) for dp, c in zip(dproj, cols)] + [(dgpre, w['w_gate'], 0)], "mm_d_h", nt=True)
    grad_x, grads['attn_norm'] = _rms_bwd(dh, x, r1, w['attn_norm'], dx1, "rms1_bwd")
    return loss, grad_x, grads


def _coords():
    return lax.axis_index("x"), lax.axis_index("y"), lax.axis_index("c")


def _slot(p):
    return 4 * p[0] + 2 * p[1] + p[2]


def _peers(me):
    x, y, c = me
    out = []
    for mask in range(1, N_DEV):
        out.append((1 - x if mask & 4 else x, 1 - y if mask & 2 else y, 1 - c if mask & 1 else c))
    return out


HBM_SPEC = pl.BlockSpec(memory_space=pltpu.HBM)


def _all_gather(shards, name):
    n = len(shards)

    def body(*refs):
        ins, outs = refs[:n], refs[n:2 * n]
        send_sems, recv_sems, local_sems = refs[2 * n:]
        x, y, c = _coords()
        me, sibling = (x, y, c), (x, y, 1 - c)
        chips = [(1 - x, y), (x, 1 - y), (1 - x, 1 - y)]

        def copy(a, k, block, to, src=None):
            dst = outs[a].at[_slot(block)]
            return pltpu.make_async_remote_copy(
                src_ref=dst if src is None else src, dst_ref=dst, send_sem=send_sems.at[a, k],
                recv_sem=recv_sems.at[a, k], device_id=to, device_id_type=MESH)

        mine = [pltpu.make_async_copy(ins[a], outs[a].at[_slot(me)], local_sems.at[a]) for a in range(n)]
        for cp in mine:
            cp.start()
        first = []
        for a in range(n):
            first.append(copy(a, 0, me, sibling, src=ins[a]))
            first += [copy(a, 1 + j, me, (*chip, c), src=ins[a]) for j, chip in enumerate(chips)]
        for cp in first:
            cp.start()
        passed = []
        for a in range(n):
            for j, chip in enumerate(chips):
                copy(a, 1 + j, (*chip, c), me).wait_recv()
                fwd = copy(a, 4 + j, (*chip, c), sibling)
                fwd.start()
                passed.append(fwd)
        for a in range(n):
            copy(a, 0, sibling, me).wait_recv()
            for j, chip in enumerate(chips):
                copy(a, 4 + j, (*chip, 1 - c), me).wait_recv()
        for cp in first + passed:
            cp.wait_send()
        for cp in mine:
            cp.wait()

    return pl.pallas_call(
        body, name=name, in_specs=[HBM_SPEC] * n, out_specs=tuple([HBM_SPEC] * n),
        out_shape=tuple(jax.ShapeDtypeStruct((N_DEV,) + s.shape, s.dtype) for s in shards),
        scratch_shapes=[pltpu.SemaphoreType.DMA((n, 7)), pltpu.SemaphoreType.DMA((n, 7)), pltpu.SemaphoreType.DMA((n,))],
    )(*shards)


def _exchange(blocks, name):
    n = len(blocks)

    def body(*refs):
        ins, outs = refs[:n], refs[n:2 * n]
        send_sems, recv_sems, local_sems = refs[2 * n:]
        me = _coords()
        peers = _peers(me)
        mine = [pltpu.make_async_copy(ins[a].at[_slot(me)], outs[a].at[_slot(me)], local_sems.at[a]) for a in range(n)]
        for cp in mine:
            cp.start()

        def copy(a, k):
            return pltpu.make_async_remote_copy(
                src_ref=ins[a].at[_slot(peers[k])], dst_ref=outs[a].at[_slot(me)], send_sem=send_sems.at[a, k],
                recv_sem=recv_sems.at[a, k], device_id=peers[k], device_id_type=MESH)

        def arrival(a, k):
            return pltpu.make_async_remote_copy(
                src_ref=ins[a].at[_slot(me)], dst_ref=outs[a].at[_slot(peers[k])], send_sem=send_sems.at[a, k],
                recv_sem=recv_sems.at[a, k], device_id=peers[k], device_id_type=MESH)

        sends = [copy(a, k) for a in range(n) for k in range(N_DEV - 1)]
        for cp in sends:
            cp.start()
        for a in range(n):
            for k in range(N_DEV - 1):
                arrival(a, k).wait_recv()
        for cp in sends:
            cp.wait_send()
        for cp in mine:
            cp.wait()

    return pl.pallas_call(
        body, name=name, in_specs=[HBM_SPEC] * n, out_specs=tuple([HBM_SPEC] * n),
        out_shape=tuple(jax.ShapeDtypeStruct(b.shape, b.dtype) for b in blocks),
        scratch_shapes=[pltpu.SemaphoreType.DMA((n, 7)), pltpu.SemaphoreType.DMA((n, 7)), pltpu.SemaphoreType.DMA((n,))],
    )(*blocks)


SEM_SPEC = pl.BlockSpec(memory_space=pltpu.SEMAPHORE)
SIDE_EFFECT = pltpu.SideEffectType.DATAFLOW_SIDE_EFFECTING


def _exchange_start(blocks, name, gather=False):
    n = len(blocks)

    def body(*refs):
        ins, lands = refs[:n], refs[n:2 * n]
        send_sems, recv_sems = refs[2 * n], refs[2 * n + 1]
        token = refs[-1]
        me = _coords()
        peers = _peers(me)
        for a in range(n):
            for k in range(N_DEV - 1):
                pltpu.make_async_remote_copy(
                    src_ref=ins[a] if gather else ins[a].at[_slot(peers[k])], dst_ref=lands[a].at[_slot(me)],
                    send_sem=send_sems.at[a * (N_DEV - 1) + k], recv_sem=recv_sems.at[a * (N_DEV - 1) + k],
                    device_id=peers[k], device_id_type=MESH).start()
        token[...] = jnp.zeros_like(token)

    land_shapes = [((N_DEV,) + b.shape) if gather else b.shape for b in blocks]
    hbm_in = [pltpu.HBM(b.shape, b.dtype) for b in blocks]
    hbm_land = [pltpu.HBM(s, b.dtype) for s, b in zip(land_shapes, blocks)]
    sems = pltpu.SemaphoreType.DMA((n * (N_DEV - 1),))
    ins = [pltpu.with_memory_space_constraint(b, pltpu.HBM) for b in blocks]
    lands = [pltpu.with_memory_space_constraint(lax.empty(s, b.dtype), pltpu.HBM) for s, b in zip(land_shapes, blocks)]
    return pl.pallas_call(
        body, name=name, out_shape=(sems, sems, *hbm_in, *hbm_land, jax.ShapeDtypeStruct((8, LANES), F32)),
        in_specs=[HBM_SPEC] * (2 * n),
        out_specs=(SEM_SPEC, SEM_SPEC, *([HBM_SPEC] * (2 * n)), pl.BlockSpec(memory_space=pltpu.VMEM)),
        input_output_aliases={i: 2 + i for i in range(2 * n)},
        compiler_params=pltpu.CompilerParams(has_side_effects=SIDE_EFFECT),
    )(*ins, *lands)


def _exchange_wait(started, after, name, gather=False):
    n = (len(started) - 3) // 2
    send_sems, recv_sems = started[0], started[1]
    thru = started[2:2 + 2 * n]

    def body(*refs):
        ins, lands = refs[:n], refs[n:2 * n]
        send_ref, recv_ref = refs[2 * n], refs[2 * n + 1]
        me = _coords()
        peers = _peers(me)
        for a in range(n):
            for k in range(N_DEV - 1):
                cp = pltpu.make_async_remote_copy(
                    src_ref=ins[a] if gather else ins[a].at[_slot(peers[k])], dst_ref=lands[a].at[_slot(peers[k])],
                    send_sem=send_ref.at[a * (N_DEV - 1) + k], recv_sem=recv_ref.at[a * (N_DEV - 1) + k],
                    device_id=peers[k], device_id_type=MESH)
                cp.wait_send()
                cp.wait_recv()

    hbm = [pltpu.HBM(t.shape, t.dtype) for t in thru]
    res = pl.pallas_call(
        body, name=name, out_shape=tuple(hbm),
        in_specs=[HBM_SPEC] * (2 * n) + [SEM_SPEC, SEM_SPEC, pl.BlockSpec(memory_space=pl.ANY)],
        out_specs=tuple([HBM_SPEC] * (2 * n)), input_output_aliases={i: i for i in range(2 * n)},
        compiler_params=pltpu.CompilerParams(has_side_effects=SIDE_EFFECT),
    )(*thru, send_sems, recv_sems, after)
    return res[n:]


def _all_sum(p, name):
    def body(p_ref, o_ref, recv, send_sems, recv_sems):
        me = _coords()
        peers = _peers(me)
        recv[_slot(me)] = p_ref[...]

        def copy(k, landing):
            return pltpu.make_async_remote_copy(
                src_ref=p_ref, dst_ref=recv.at[_slot(landing)], send_sem=send_sems.at[k], recv_sem=recv_sems.at[k],
                device_id=peers[k], device_id_type=MESH)

        sends = [copy(k, me) for k in range(N_DEV - 1)]
        for cp in sends:
            cp.start()
        for k in range(N_DEV - 1):
            copy(k, peers[k]).wait_recv()
        for cp in sends:
            cp.wait_send()
        acc = recv[0]
        for s in range(1, N_DEV):
            acc = acc + recv[s]
        o_ref[...] = acc

    vmem = pl.BlockSpec(memory_space=pltpu.VMEM)
    return pl.pallas_call(
        body, name=name, in_specs=[vmem], out_specs=vmem, out_shape=jax.ShapeDtypeStruct(p.shape, F32),
        scratch_shapes=[pltpu.VMEM((N_DEV,) + p.shape, F32), pltpu.SemaphoreType.DMA((N_DEV - 1,)),
                        pltpu.SemaphoreType.DMA((N_DEV - 1,))],
    )(p)


def _adam(w, g, m, v):
    m2 = ADAM_B1 * m + (1.0 - ADAM_B1) * g
    v2 = ADAM_B2 * v + (1.0 - ADAM_B2) * (g * g)
    m_hat = m2 / (1.0 - ADAM_B1 ** ADAM_STEP)
    v_hat = v2 / (1.0 - ADAM_B2 ** ADAM_STEP)
    delta = -ADAM_LR * (m_hat / (jnp.sqrt(v_hat) + ADAM_EPS) + ADAM_WD * w)
    return delta, m2, v2


def _row_tile(rows, cols):
    best = rows
    for t in range(16, rows, 16):
        if rows % t == 0 and t * cols * 4 <= (1 << 20):
            best = t
    return best


def _adam_reduce(parts, w, m, v, name):
    rows, cols = w.shape
    tr = _row_tile(rows, cols)

    def body(p_ref, w_ref, m_ref, v_ref, g_ref, d_ref, m2_ref, v2_ref):
        g = p_ref[0].astype(F32)
        for s in range(1, N_DEV):
            g = g + p_ref[s].astype(F32)
        g_ref[...] = g
        d_ref[...], m2_ref[...], v2_ref[...] = _adam(w_ref[...], g, m_ref[...], v_ref[...])

    blk = pl.BlockSpec((tr, cols), lambda i: (i, 0))
    shp = jax.ShapeDtypeStruct((rows, cols), F32)
    return pl.pallas_call(
        body, name=name, grid=(rows // tr,),
        in_specs=[pl.BlockSpec((N_DEV, tr, cols), lambda i: (0, i, 0)), blk, blk, blk],
        out_specs=(blk,) * 4, out_shape=(shp,) * 4, compiler_params=_cparams(1),
    )(parts, w, m, v)


PACK_COLS = 1024
PACK = {'attn_norm': (0, 1, 1024), 'mem_norm': (1, 1, 1024), 'ffn_norm': (2, 1, 1024), 'b_gate': (3, 3, 1024),
        'conv_b': (6, 6, 1024), 'a_q_norm': (12, 3, 64), 'a_k_norm': (15, 3, 64), 'b_q_norm': (18, 1, 64),
        'b_k_norm': (19, 1, 64), 'm_q_norm': (20, 1, 128), 'm_k_norm': (21, 1, 128), 'b_sinks': (22, 1, 8)}
PACK_LOSS_ROW = 23
PACK_ROWS = 24


def _pack_pieces(name, width):
    r0, nr, lanes = PACK[name]
    out = []
    for j in range(nr):
        if lanes == PACK_COLS:
            w = min(PACK_COLS, width - j * PACK_COLS)
            out.append((r0 + j, slice(0, 1), slice(j * PACK_COLS, j * PACK_COLS + w), w))
        else:
            out.append((r0 + j, slice(j, j + 1), slice(0, lanes), lanes))
    return out


def _pack_small(grads, loss_tile, name):
    names = list(PACK)

    def body(*refs):
        o_ref = refs[-1]
        o_ref[...] = jnp.zeros_like(o_ref)
        for k, nm in enumerate(names):
            for row, rs, ls, w in _pack_pieces(nm, refs[k].shape[1]):
                o_ref[row:row + 1, 0:w] = refs[k][rs, ls]
        o_ref[PACK_LOSS_ROW:PACK_LOSS_ROW + 1, 0:1] = refs[len(names)][0:1, 0:1]

    vmem = pl.BlockSpec(memory_space=pltpu.VMEM)
    args = [grads[nm] for nm in names] + [loss_tile]
    return pl.pallas_call(body, name=name, in_specs=[vmem] * len(args), out_specs=vmem,
                          out_shape=jax.ShapeDtypeStruct((PACK_ROWS, PACK_COLS), F32))(*args)


def _adam_small(gsum, ws, ms, vs, name):
    names = list(PACK)
    n = len(names)

    def body(*refs):
        g_ref = refs[0]
        w_refs, m_refs, v_refs = refs[1:1 + n], refs[1 + n:1 + 2 * n], refs[1 + 2 * n:1 + 3 * n]
        outs = refs[1 + 3 * n:]
        outs[0][...] = g_ref[PACK_LOSS_ROW:PACK_LOSS_ROW + 1, 0:1]
        for k, nm in enumerate(names):
            o_g, o_d, o_m, o_v = outs[1 + 4 * k:5 + 4 * k]
            for row, rs, ls, width in _pack_pieces(nm, w_refs[k].shape[1]):
                src = (rs, ls)
                g = g_ref[row:row + 1, 0:width]
                d, m2, v2 = _adam(w_refs[k][src], g, m_refs[k][src], v_refs[k][src])
                o_g[src] = g
                o_d[src] = d
                o_m[src] = m2
                o_v[src] = v2

    vmem = pl.BlockSpec(memory_space=pltpu.VMEM)
    shapes = [jax.ShapeDtypeStruct((1, 1), F32)]
    for nm in names:
        shapes += [jax.ShapeDtypeStruct(ws[nm].shape, F32)] * 4
    args = [gsum] + [ws[nm] for nm in names] + [ms[nm] for nm in names] + [vs[nm] for nm in names]
    return pl.pallas_call(
        body, name=name, in_specs=[vmem] * len(args), out_specs=tuple([vmem] * len(shapes)), out_shape=tuple(shapes),
    )(*args)


def _as2d(name, a):
    return a.reshape(a.shape[-2], a.shape[-1]) if a.ndim == 3 else a


def kernel(x, mem, positions, attn_norm, w_in, a_q_norm, a_k_norm, b_q_norm, b_k_norm, b_sinks, mem_norm, w_mem_kv, m_q_norm, m_k_norm, w_o_a, w_o_b, w_o_m, w_gate, b_gate, w_out, ffn_norm, w_up, conv_w, conv_b, w_down, loss_target, m_attn_norm, m_w_in, m_a_q_norm, m_a_k_norm, m_b_q_norm, m_b_k_norm, m_b_sinks, m_mem_norm, m_w_mem_kv, m_m_q_norm, m_m_k_norm, m_w_o_a, m_w_o_b, m_w_o_m, m_w_gate, m_b_gate, m_w_out, m_ffn_norm, m_w_up, m_conv_w, m_conv_b, m_w_down, v_attn_norm, v_w_in, v_a_q_norm, v_a_k_norm, v_b_q_norm, v_b_k_norm, v_b_sinks, v_mem_norm, v_w_mem_kv, v_m_q_norm, v_m_k_norm, v_w_o_a, v_w_o_b, v_w_o_m, v_w_gate, v_b_gate, v_w_out, v_ffn_norm, v_w_up, v_conv_w, v_conv_b, v_w_down):
    given = dict(attn_norm=attn_norm, w_in=w_in, a_q_norm=a_q_norm, a_k_norm=a_k_norm, b_q_norm=b_q_norm, b_k_norm=b_k_norm, b_sinks=b_sinks, mem_norm=mem_norm, w_mem_kv=w_mem_kv, m_q_norm=m_q_norm, m_k_norm=m_k_norm, w_o_a=w_o_a, w_o_b=w_o_b, w_o_m=w_o_m, w_gate=w_gate, b_gate=b_gate, w_out=w_out, ffn_norm=ffn_norm, w_up=w_up, conv_w=conv_w, conv_b=conv_b, w_down=w_down)
    mom1 = dict(attn_norm=m_attn_norm, w_in=m_w_in, a_q_norm=m_a_q_norm, a_k_norm=m_a_k_norm, b_q_norm=m_b_q_norm, b_k_norm=m_b_k_norm, b_sinks=m_b_sinks, mem_norm=m_mem_norm, w_mem_kv=m_w_mem_kv, m_q_norm=m_m_q_norm, m_k_norm=m_m_k_norm, w_o_a=m_w_o_a, w_o_b=m_w_o_b, w_o_m=m_w_o_m, w_gate=m_w_gate, b_gate=m_b_gate, w_out=m_w_out, ffn_norm=m_ffn_norm, w_up=m_w_up, conv_w=m_conv_w, conv_b=m_conv_b, w_down=m_w_down)
    mom2 = dict(attn_norm=v_attn_norm, w_in=v_w_in, a_q_norm=v_a_q_norm, a_k_norm=v_a_k_norm, b_q_norm=v_b_q_norm, b_k_norm=v_b_k_norm, b_sinks=v_b_sinks, mem_norm=v_mem_norm, w_mem_kv=v_w_mem_kv, m_q_norm=v_m_q_norm, m_k_norm=v_m_k_norm, w_o_a=v_w_o_a, w_o_b=v_w_o_b, w_o_m=v_w_o_m, w_gate=v_w_gate, b_gate=v_b_gate, w_out=v_w_out, ffn_norm=v_ffn_norm, w_up=v_w_up, conv_w=v_conv_w, conv_b=v_conv_b, w_down=v_w_down)

    big = list(BIG)
    stages = {'mix': list(MIX_WEIGHTS), 'ffn': list(FFN_WEIGHTS)}
    my_slot = _slot(_coords())

    def shard(n):
        return given[n][0] if n == 'conv_w' else given[n][0].astype(BF16)

    def whole(n, g):
        _, r, c = g.shape
        return g.reshape(N_DEV * r, c) if BIG[n] == 0 else g.transpose(1, 0, 2).reshape(r, N_DEV * c)

    def to_blocks(n, g):
        r, c = given[n].shape[1:]
        g = g.reshape(N_DEV, r, c) if BIG[n] == 0 else g.reshape(r, N_DEV, c).transpose(1, 0, 2)
        return g if n == 'conv_w' else g.astype(BF16)

    class Hooks:
        def __init__(self):
            self.coming, self.sent = {}, {}
            token = jnp.zeros((), F32)
            for stage, names in stages.items():
                self.coming[stage] = _exchange_start([shard(n) for n in names], f"gather_{stage}_start", gather=True)
                token = token + self.coming[stage][-1][0, 0]
            self.token = token

        def weights(self, stage, after):
            names = stages[stage]
            landed = _exchange_wait(self.coming[stage], after, f"gather_{stage}_wait", gather=True)
            return {n: whole(n, lax.dynamic_update_slice_in_dim(land, shard(n)[None], my_slot, axis=0))
                    for n, land in zip(names, landed)}

        def grads(self, stage, g):
            blocks = [to_blocks(n, g[n]) for n in stages[stage]]
            own = [lax.dynamic_slice_in_dim(b, my_slot, 1, axis=0) for b in blocks]
            self.sent[stage] = (_exchange_start(blocks, f"exchange_{stage}_start"), own)
            return self.sent[stage][0][-1]

        def parts(self, stage, after):
            started, own = self.sent[stage]
            landed = _exchange_wait(started, after, f"exchange_{stage}_wait")
            return {n: lax.dynamic_update_slice_in_dim(land, o, my_slot, axis=0)
                    for n, land, o in zip(stages[stage], landed, own)}

    hooks = Hooks()
    w_in_shard = (given['w_in'][0] + hooks.token).astype(BF16)
    w = {'w_in': whole('w_in', _all_gather([w_in_shard], "gather_w_in")[0])}
    for n in SMALL:
        w[n] = given[n]
    w['a_q_norm'], w['a_k_norm'] = given['a_q_norm'][0], given['a_k_norm'][0]
    w['b_q_norm'], w['b_k_norm'], w['b_sinks'] = given['b_q_norm'][0], given['b_k_norm'][0], given['b_sinks'][0]

    loss_tile, grad_x, grads = _device_step(x[0], mem[0], positions[0], loss_target[0], w, hooks)
    parts = {'w_in': _exchange([to_blocks('w_in', grads['w_in'])], "exchange_w_in")[0]}
    parts.update(hooks.parts('ffn', parts['w_in']))
    parts.update(hooks.parts('mix', parts['w_in']))

    out = {}
    for n in big:
        res = _adam_reduce(parts[n], given[n][0], mom1[n][0], mom2[n][0], f"adam_{n}")
        out[n] = tuple(t[None] for t in res)

    small = {n: grads[n] for n in PACK}
    small['b_q_norm'], small['b_k_norm'] = grads['b_q_norm'].reshape(1, -1), grads['b_k_norm'].reshape(1, -1)
    small['b_sinks'] = grads['b_sinks'].reshape(1, -1)
    gsum = _all_sum(_pack_small(small, loss_tile, "pack_small"), "sum_small")
    ws = {n: _as2d(n, given[n]) for n in PACK}
    ms = {n: _as2d(n, mom1[n]) for n in PACK}
    vs = {n: _as2d(n, mom2[n]) for n in PACK}
    res = _adam_small(gsum, ws, ms, vs, "adam_small")
    loss = res[0].reshape(())
    for k, n in enumerate(PACK):
        out[n] = tuple(t.reshape(given[n].shape) for t in res[1 + 4 * k:5 + 4 * k])

    outs = [loss, grad_x[None]]
    for field in range(4):
        outs += [out[n][field] for n in WEIGHTS]
    return tuple(outs)
```

```python
import functools
import math

import jax
import jax.numpy as jnp
from jax import lax
from jax.experimental import pallas as pl
from jax.experimental.pallas import tpu as pltpu

F32 = jnp.float32
BF16 = jnp.bfloat16

N_DEV = 8
D_MODEL = 1024
HEAD_DIM = 64
A_GROUPS = ((128, 1), (512, 4), (2048, 16))
B_WINDOW = 128
M_HEADS = 4
M_HEAD_DIM = 128
MEM_LEN = 256
D_FF = 2816
ROPE_THETA = 500000.0
ROPE_DIMS = 16
BLOCK = 128
EPS = 1e-6
LANES = 128
BAND_Q_BLOCKS = 4
BAND_UNITS = 2

ADAM_LR = 0.001
ADAM_B1 = 0.9
ADAM_B2 = 0.999
ADAM_EPS = 1e-08
ADAM_WD = 0.01
ADAM_STEP = 10

VMEM_LIMIT_BYTES = 56 * 1024 * 1024
MESH = pl.DeviceIdType.MESH

WEIGHTS = ['attn_norm', 'w_in', 'a_q_norm', 'a_k_norm', 'b_q_norm', 'b_k_norm', 'b_sinks', 'mem_norm',
           'w_mem_kv', 'm_q_norm', 'm_k_norm', 'w_o_a', 'w_o_b', 'w_o_m', 'w_gate', 'b_gate', 'w_out',
           'ffn_norm', 'w_up', 'conv_w', 'conv_b', 'w_down']
BIG = {'w_in': 1, 'w_mem_kv': 0, 'w_o_a': 1, 'w_o_b': 1, 'w_o_m': 1, 'w_gate': 1, 'w_out': 0, 'w_up': 1,
       'conv_w': 1, 'w_down': 0}
SMALL = [n for n in WEIGHTS if n not in BIG]


def _cparams(n_grid):
    return pltpu.CompilerParams(dimension_semantics=("arbitrary",) * n_grid, vmem_limit_bytes=VMEM_LIMIT_BYTES)


def _pick(n, cands=(512, 256, 128)):
    for c in cands:
        if n % c == 0:
            return c
    return n


def _seg_matrix(width):
    shift = width.bit_length() - 1
    r = lax.shift_right_logical(lax.broadcasted_iota(jnp.int32, (LANES, LANES), 0), shift)
    c = lax.shift_right_logical(lax.broadcasted_iota(jnp.int32, (LANES, LANES), 1), shift)
    return jnp.where(r == c, 1.0, 0.0).astype(BF16)


def _seg_sum(x, seg):
    hi = x.astype(BF16)
    r1 = x - hi.astype(F32)
    mid = r1.astype(BF16)
    lo = (r1 - mid.astype(F32)).astype(BF16)
    dot = functools.partial(jnp.dot, preferred_element_type=F32)
    return dot(hi, seg) + dot(mid, seg) + dot(lo, seg)


def _rope(y, c, s1, s2):
    return y * c + pltpu.roll(y, LANES - ROPE_DIMS // 2, 1) * s1 + pltpu.roll(y, ROPE_DIMS // 2, 1) * s2


def _unrope(dy, c, s1, s2):
    return dy * c + pltpu.roll(dy * s1, ROPE_DIMS // 2, 1) + pltpu.roll(dy * s2, LANES - ROPE_DIMS // 2, 1)


def _sigmoid(x):
    return 1.0 / (1.0 + jnp.exp(-x))


def _rms_fwd(x, gain, name):
    s_len, d = x.shape
    tm = 512

    def body(x_ref, g_ref, h_ref, r_ref):
        xv = x_ref[...]
        r = lax.rsqrt(jnp.mean(xv * xv, axis=-1, keepdims=True) + EPS)
        h_ref[...] = ((xv * r) * g_ref[...]).astype(BF16)
        r_ref[...] = r

    return pl.pallas_call(
        body, name=name, grid=(s_len // tm,),
        in_specs=[pl.BlockSpec((tm, d), lambda i: (i, 0)), pl.BlockSpec((1, d), lambda i: (0, 0))],
        out_specs=(pl.BlockSpec((tm, d), lambda i: (i, 0)), pl.BlockSpec((tm, 1), lambda i: (i, 0))),
        out_shape=(jax.ShapeDtypeStruct((s_len, d), BF16), jax.ShapeDtypeStruct((s_len, 1), F32)),
        compiler_params=_cparams(1),
    )(x, gain)


def _rms_bwd(dh, x, r, gain, add, name, bf16_copy=False):
    s_len, d = x.shape
    tm = 512

    def body(dh_ref, x_ref, r_ref, g_ref, add_ref, dx_ref, *rest):
        dg_ref = rest[-1]

        @pl.when(pl.program_id(0) == 0)
        def _():
            dg_ref[...] = jnp.zeros_like(dg_ref)
        rv = r_ref[...]
        xhat = x_ref[...] * rv
        dhv = dh_ref[...]
        dg_ref[...] += jnp.sum(dhv * xhat, axis=0, keepdims=True)
        dxhat = dhv * g_ref[...]
        dx = add_ref[...] + rv * (dxhat - xhat * jnp.mean(dxhat * xhat, axis=-1, keepdims=True))
        dx_ref[...] = dx
        if bf16_copy:
            rest[0][...] = dx.astype(BF16)

    row = pl.BlockSpec((tm, d), lambda i: (i, 0))
    vec = pl.BlockSpec((1, d), lambda i: (0, 0))
    out_specs = [row] + ([row] if bf16_copy else []) + [vec]
    out_shape = [jax.ShapeDtypeStruct((s_len, d), F32)] + ([jax.ShapeDtypeStruct((s_len, d), BF16)] if bf16_copy else [])
    out_shape.append(jax.ShapeDtypeStruct((1, d), F32))
    return pl.pallas_call(
        body, name=name, grid=(s_len // tm,),
        in_specs=[row, row, pl.BlockSpec((tm, 1), lambda i: (i, 0)), vec, row],
        out_specs=tuple(out_specs), out_shape=tuple(out_shape), compiler_params=_cparams(1),
    )(dh, x, r, gain, add)


def _resident(shape, index_map):
    return pl.BlockSpec(shape, index_map, pipeline_mode=pl.Buffered(1))


def _mm_rows(pairs, name, nt=False, tm=512, bias=None, sigmoid=False, res=None, out_dtypes=(F32,)):
    m = pairs[0][0].shape[0]
    n = pairs[0][1].shape[0] if nt else pairs[0][1].shape[1]
    n_pairs = len(pairs)
    has_bias, has_res = bias is not None, res is not None
    dims = (((1,), (1,)), ((), ())) if nt else (((1,), (0,)), ((), ()))

    def body(*refs):
        acc = None
        for p in range(n_pairs):
            t = lax.dot_general(refs[2 * p][...].astype(BF16), refs[2 * p + 1][...], dims, preferred_element_type=F32)
            acc = t if acc is None else acc + t
        pos = 2 * n_pairs
        if has_bias:
            acc = acc + refs[pos][...]
            pos += 1
        if sigmoid:
            acc = _sigmoid(acc)
        if has_res:
            acc = refs[pos][...] + acc
            pos += 1
        for o_ref in refs[pos:]:
            o_ref[...] = acc.astype(o_ref.dtype)

    in_specs, args = [], []
    for a, w, blk in pairs:
        k = a.shape[1]
        in_specs.append(pl.BlockSpec((tm, k), lambda i: (i, 0)))
        if nt:
            in_specs.append(_resident((n, k), lambda i, blk=blk: (0, blk)))
        else:
            in_specs.append(_resident((k, n), lambda i, blk=blk: (blk, 0)))
        args += [a, w]
    if has_bias:
        in_specs.append(_resident((1, n), lambda i: (0, 0)))
        args.append(bias)
    if has_res:
        in_specs.append(pl.BlockSpec((tm, n), lambda i: (i, 0)))
        args.append(res)
    out = pl.BlockSpec((tm, n), lambda i: (i, 0))
    outs = pl.pallas_call(
        body, name=name, grid=(m // tm,), in_specs=in_specs, out_specs=tuple([out] * len(out_dtypes)),
        out_shape=tuple(jax.ShapeDtypeStruct((m, n), dt) for dt in out_dtypes), compiler_params=_cparams(1),
    )(*args)
    return outs[0] if len(out_dtypes) == 1 else outs


def _mm_tn(a, b, name, tile=256):
    k, m = a.shape
    n = b.shape[1]
    dims = (((0,), (0,)), ((), ()))

    def body(a_ref, b_ref, o_ref):
        o_ref[...] = lax.dot_general(a_ref[...].astype(BF16), b_ref[...].astype(BF16), dims, preferred_element_type=F32)

    if n <= m:
        t = min(tile, m)
        grid, a_spec, b_spec = (m // t,), pl.BlockSpec((k, t), lambda i: (0, i)), _resident((k, n), lambda i: (0, 0))
        o_spec = pl.BlockSpec((t, n), lambda i: (i, 0))
    else:
        t = min(tile, n)
        grid, a_spec, b_spec = (n // t,), _resident((k, m), lambda i: (0, 0)), pl.BlockSpec((k, t), lambda i: (0, i))
        o_spec = pl.BlockSpec((m, t), lambda i: (0, i))
    return pl.pallas_call(
        body, name=name, grid=grid, in_specs=[a_spec, b_spec], out_specs=o_spec,
        out_shape=jax.ShapeDtypeStruct((m, n), F32), compiler_params=_cparams(1),
    )(a, b)


def _norm_rope(t, gain, c, s1, s2, seg):
    rs = lax.rsqrt(_seg_sum(t * t, seg) * (1.0 / HEAD_DIM) + EPS)
    return _rope((t * rs) * gain, c, s1, s2)


def _dup_half(y, half):
    lane = lax.broadcasted_iota(jnp.int32, y.shape, 1)
    rolled = pltpu.roll(y, HEAD_DIM, 1)
    keep = (lane < HEAD_DIM) if half == 0 else (lane >= HEAD_DIM)
    return jnp.where(keep, y, rolled)


def _qk_prep(proj, cb0, d, gqa, gq, gk, tabs, name):
    s_len = proj.shape[0]
    tm = 512
    rows = tm // d
    n_units = 4 if gqa else 2 * d
    n_q = 4 if gqa else 2
    n_in = 6

    def body(*refs):
        in_refs = refs[:n_in]
        gq_ref, gk_ref, c_ref, s1_ref, s2_ref, o_ref = refs[n_in:]
        seg = _seg_matrix(HEAD_DIM)

        def rows_of(ref, r):
            return ref[...] if d == 1 else ref[pl.ds(r, rows, stride=d), :]

        def put(unit_col, y):
            o_ref[:, unit_col * LANES:(unit_col + 1) * LANES] = y.astype(BF16)

        for r in range(d):
            c, s1, s2 = rows_of(c_ref, r), rows_of(s1_ref, r), rows_of(s2_ref, r)
            for b in range(n_in):
                t = rows_of(in_refs[b], r)
                if b < n_q:
                    put((b * d + r) if not gqa else b, _norm_rope(t, gq_ref[...], c, s1, s2, seg))
                elif not gqa:
                    sec, pair = (1, b - 2) if b < 4 else (2, b - 4)
                    y = _norm_rope(t, gk_ref[...], c, s1, s2, seg) if sec == 1 else t
                    put(sec * n_units + pair * d + r, y)
                else:
                    sec = 1 if b == 4 else 2
                    y = _norm_rope(t, gk_ref[...], c, s1, s2, seg) if sec == 1 else t
                    for u in range(n_units):
                        put(sec * n_units + u, _dup_half(y, u // 2))

    in_specs = [pl.BlockSpec((tm, LANES), lambda i, b=b: (i, cb0 + b)) for b in range(n_in)]
    vec = pl.BlockSpec((1, LANES), lambda i: (0, 0))
    tab = pl.BlockSpec((tm, LANES), lambda i: (i, 0))
    width = 3 * n_units * LANES
    return pl.pallas_call(
        body, name=name, grid=(s_len // tm,), in_specs=in_specs + [vec, vec, tab, tab, tab],
        out_specs=pl.BlockSpec((rows, width), lambda i: (i, 0)),
        out_shape=jax.ShapeDtypeStruct((s_len // d, width), BF16), compiler_params=_cparams(1),
    )(*([proj] * n_in), gq, gk, *tabs)


def _qk_prep_bwd(dqkv, proj, cb0, d, gqa, gq, gk, tabs, name):
    s_len = proj.shape[0]
    tm = 512
    rows = tm // d
    n_units = 4 if gqa else 2 * d
    n_q = 4 if gqa else 2
    n_in = 6

    def body(*refs):
        d_refs = refs[0:3]
        in_refs = refs[3:3 + n_in]
        gq_ref, gk_ref, c_ref, s1_ref, s2_ref, o_ref, dgq_ref, dgk_ref, stage = refs[3 + n_in:]
        seg = _seg_matrix(HEAD_DIM)

        @pl.when(pl.program_id(0) == 0)
        def _():
            dgq_ref[...] = jnp.zeros_like(dgq_ref)
            dgk_ref[...] = jnp.zeros_like(dgk_ref)

        def rows_of(ref, r):
            return ref[...] if d == 1 else ref[pl.ds(r, rows, stride=d), :]

        def unit(col):
            sec, u = divmod(col, n_units)
            return d_refs[sec][:, u * LANES:(u + 1) * LANES]

        def norm_bwd(dyr, t, gain, c, s1, s2, dg_ref):
            rs = lax.rsqrt(_seg_sum(t * t, seg) * (1.0 / HEAD_DIM) + EPS)
            that = t * rs
            dy = _unrope(dyr, c, s1, s2)
            dg_ref[...] += jnp.sum(dy * that, axis=0, keepdims=True)
            dthat = dy * gain
            return rs * (dthat - that * (_seg_sum(dthat * that, seg) * (1.0 / HEAD_DIM)))

        def fold(sec):
            tot = []
            for u in range(n_units):
                v = unit(sec * n_units + u)
                tot.append(v + pltpu.roll(v, HEAD_DIM, 1))
            lane = lax.broadcasted_iota(jnp.int32, tot[0].shape, 1)
            return jnp.where(lane < HEAD_DIM, tot[0] + tot[1], tot[2] + tot[3])

        for b in range(n_in):
            for r in range(d):
                c, s1, s2 = rows_of(c_ref, r), rows_of(s1_ref, r), rows_of(s2_ref, r)
                t = rows_of(in_refs[b], r)
                if b < n_q:
                    g = unit((b * d + r) if not gqa else b)
                    out = norm_bwd(g, t, gq_ref[...], c, s1, s2, dgq_ref)
                elif not gqa:
                    sec, pair = (1, b - 2) if b < 4 else (2, b - 4)
                    g = unit(sec * n_units + pair * d + r)
                    out = norm_bwd(g, t, gk_ref[...], c, s1, s2, dgk_ref) if sec == 1 else g
                else:
                    sec = 1 if b == 4 else 2
                    g = fold(sec)
                    out = norm_bwd(g, t, gk_ref[...], c, s1, s2, dgk_ref) if sec == 1 else g
                if d == 1:
                    o_ref[:, b * LANES:(b + 1) * LANES] = out.astype(BF16)
                else:
                    stage[pl.ds(r, rows, stride=d), :] = out
            if d != 1:
                o_ref[:, b * LANES:(b + 1) * LANES] = stage[...].astype(BF16)

    in_specs = [pl.BlockSpec((rows, n_units * LANES), lambda i: (i, 0))] * 3
    in_specs += [pl.BlockSpec((tm, LANES), lambda i, b=b: (i, cb0 + b)) for b in range(n_in)]
    vec = pl.BlockSpec((1, LANES), lambda i: (0, 0))
    tab = pl.BlockSpec((tm, LANES), lambda i: (i, 0))
    return pl.pallas_call(
        body, name=name, grid=(s_len // tm,), in_specs=in_specs + [vec, vec, tab, tab, tab],
        out_specs=(pl.BlockSpec((tm, n_in * LANES), lambda i: (i, 0)), vec, vec),
        out_shape=(jax.ShapeDtypeStruct((s_len, n_in * LANES), BF16), jax.ShapeDtypeStruct((1, LANES), F32),
                   jax.ShapeDtypeStruct((1, LANES), F32)),
        scratch_shapes=[pltpu.VMEM((tm, LANES), F32)], compiler_params=_cparams(1),
    )(*dqkv, *([proj] * n_in), gq, gk, *tabs)


def _head_masks(shape):
    lane = lax.broadcasted_iota(jnp.int32, shape, 1)
    return lane < HEAD_DIM, lane >= HEAD_DIM


def _band_fwd(qkv, n_units, max_dist, sinks, name):
    n_rows = qkv.shape[0]
    nb = n_rows // BLOCK
    scale = HEAD_DIM ** -0.5
    has_sink = sinks is not None

    qn, un = min(nb, BAND_Q_BLOCKS), BAND_UNITS
    ug = n_units // un

    def body(*refs):
        q_ref, kp_ref, km_ref, vp_ref, vm_ref = refs[:5]
        o_ref, lse_ref = refs[-2:]
        i = pl.program_id(1)
        qi = lax.broadcasted_iota(jnp.int32, (BLOCK, 2 * BLOCK), 0)
        kj = lax.broadcasted_iota(jnp.int32, (BLOCK, 2 * BLOCK), 1)
        dist = qi + BLOCK - kj
        band = (dist >= 0) & (dist <= max_dist)
        band_first = band & ((i > 0) | (kj >= BLOCK))
        m0, m1 = _head_masks((BLOCK, LANES))
        zero = jnp.zeros((BLOCK, LANES), BF16)
        for ub in range(un):
            cs = slice(ub * LANES, (ub + 1) * LANES)
            for qb in range(qn):
                rs = slice(qb * BLOCK, (qb + 1) * BLOCK)
                q = q_ref[rs, cs]
                if qb == 0:
                    kk = jnp.concatenate([kp_ref[:, cs], km_ref[0:BLOCK, cs]], axis=0)
                    vv = jnp.concatenate([vp_ref[:, cs], vm_ref[0:BLOCK, cs]], axis=0)
                    valid = band_first
                else:
                    kk = km_ref[(qb - 1) * BLOCK:(qb + 1) * BLOCK, cs]
                    vv = vm_ref[(qb - 1) * BLOCK:(qb + 1) * BLOCK, cs]
                    valid = band
                outs, lses = [], []
                for e, hm in enumerate((m0, m1)):
                    qe = jnp.where(hm, q, zero)
                    s = lax.dot_general(qe, kk, (((1,), (1,)), ((), ())), preferred_element_type=F32) * scale
                    s = jnp.where(valid, s, -jnp.inf)
                    mx = jnp.max(s, axis=-1, keepdims=True)
                    if has_sink:
                        sk = refs[5][ub][:, e * HEAD_DIM:e * HEAD_DIM + 1]
                        mx = jnp.maximum(mx, sk)
                    p = jnp.exp(s - mx)
                    den = jnp.sum(p, axis=-1, keepdims=True)
                    if has_sink:
                        den = den + jnp.exp(sk - mx)
                    pn = (p * (1.0 / den)).astype(BF16)
                    outs.append(jnp.dot(pn, vv, preferred_element_type=F32))
                    lses.append(mx + jnp.log(den))
                o_ref[rs, cs] = jnp.where(m0, outs[0], outs[1])
                lse_ref[rs, cs] = jnp.where(m0, jnp.broadcast_to(lses[0], (BLOCK, LANES)),
                                            jnp.broadcast_to(lses[1], (BLOCK, LANES)))

    def main(sec):
        return pl.BlockSpec((qn * BLOCK, un * LANES), lambda u, i: (i, sec * ug + u))

    def prev(sec):
        return pl.BlockSpec((BLOCK, un * LANES), lambda u, i: (jnp.maximum(i * qn - 1, 0), sec * ug + u))

    in_specs = [main(0), prev(1), main(1), prev(2), main(2)]
    args = [qkv] * 5
    if has_sink:
        in_specs.append(pl.BlockSpec((un, 1, LANES), lambda u, i: (u, 0, 0)))
        args.append(sinks)
    return pl.pallas_call(
        body, name=name, grid=(ug, nb // qn), in_specs=in_specs, out_specs=(main(0), main(0)),
        out_shape=(jax.ShapeDtypeStruct((n_rows, n_units * LANES), F32),) * 2, compiler_params=_cparams(2),
    )(*args)


def _band_bwd(qkv, do, lse, delta, n_units, max_dist, name):
    n_rows = qkv.shape[0]
    nb = n_rows // BLOCK
    scale = HEAD_DIM ** -0.5

    qn, un = min(nb, BAND_Q_BLOCKS), BAND_UNITS
    ug = n_units // un
    steps = nb // qn
    nt_dims = (((1,), (1,)), ((), ()))
    tn_dims = (((0,), (0,)), ((), ()))

    def body(qm_ref, qx_ref, kp_ref, km_ref, vp_ref, vm_ref, dom_ref, dox_ref, lm_ref, lx_ref, dm_ref, dx_ref,
             dq_ref, dk_ref, dv_ref):
        i = pl.program_id(1)
        m0, m1 = _head_masks((BLOCK, LANES))
        zero = jnp.zeros((BLOCK, LANES), BF16)
        qi = lax.broadcasted_iota(jnp.int32, (BLOCK, 2 * BLOCK), 0)
        kj = lax.broadcasted_iota(jnp.int32, (BLOCK, 2 * BLOCK), 1)
        dist = qi + BLOCK - kj
        band = (dist >= 0) & (dist <= max_dist)
        band_first = band & ((i > 0) | (kj >= BLOCK))
        qr = lax.broadcasted_iota(jnp.int32, (2 * BLOCK, BLOCK), 0)
        kc = lax.broadcasted_iota(jnp.int32, (2 * BLOCK, BLOCK), 1)
        dist2 = qr - kc
        band2 = (dist2 >= 0) & (dist2 <= max_dist)
        band2_last = band2 & ((qr < BLOCK) | (i < steps - 1))
        m0w, m1w = _head_masks((2 * BLOCK, LANES))
        zero2 = jnp.zeros((2 * BLOCK, LANES), BF16)

        def two(main_ref, next_ref, kb, cs):
            if kb < qn - 1:
                return main_ref[kb * BLOCK:(kb + 2) * BLOCK, cs]
            return jnp.concatenate([main_ref[kb * BLOCK:(kb + 1) * BLOCK, cs], next_ref[:, cs]], axis=0)

        for ub in range(un):
            cs = slice(ub * LANES, (ub + 1) * LANES)
            for qb in range(qn):
                rs = slice(qb * BLOCK, (qb + 1) * BLOCK)
                q = qm_ref[rs, cs]
                dob = dom_ref[rs, cs]
                lse_b = lm_ref[rs, cs]
                del_b = dm_ref[rs, cs]
                if qb == 0:
                    kk = jnp.concatenate([kp_ref[:, cs], km_ref[0:BLOCK, cs]], axis=0)
                    vv = jnp.concatenate([vp_ref[:, cs], vm_ref[0:BLOCK, cs]], axis=0)
                    valid = band_first
                else:
                    kk = km_ref[(qb - 1) * BLOCK:(qb + 1) * BLOCK, cs]
                    vv = vm_ref[(qb - 1) * BLOCK:(qb + 1) * BLOCK, cs]
                    valid = band
                dqs = []
                for e, hm in enumerate((m0, m1)):
                    col = slice(e * HEAD_DIM, e * HEAD_DIM + 1)
                    s = lax.dot_general(jnp.where(hm, q, zero), kk, nt_dims, preferred_element_type=F32) * scale
                    p = jnp.where(valid, jnp.exp(s - lse_b[:, col]), 0.0)
                    dp = lax.dot_general(jnp.where(hm, dob, zero), vv, nt_dims, preferred_element_type=F32)
                    ds = (p * (dp - del_b[:, col]) * scale).astype(BF16)
                    dqs.append(jnp.dot(ds, kk, preferred_element_type=F32))
                dq_ref[rs, cs] = jnp.where(m0, dqs[0], dqs[1])
            for kb in range(qn):
                rs = slice(kb * BLOCK, (kb + 1) * BLOCK)
                qq = two(qm_ref, qx_ref, kb, cs)
                dd = two(dom_ref, dox_ref, kb, cs)
                ll = two(lm_ref, lx_ref, kb, cs)
                de = two(dm_ref, dx_ref, kb, cs)
                k = km_ref[rs, cs]
                v = vm_ref[rs, cs]
                valid2 = band2 if kb < qn - 1 else band2_last
                dk = jnp.zeros((BLOCK, LANES), F32)
                dv = jnp.zeros((BLOCK, LANES), F32)
                for e, hm in enumerate((m0w, m1w)):
                    col = slice(e * HEAD_DIM, e * HEAD_DIM + 1)
                    qe = jnp.where(hm, qq, zero2)
                    doe = jnp.where(hm, dd, zero2)
                    s = lax.dot_general(qe, k, nt_dims, preferred_element_type=F32) * scale
                    p = jnp.where(valid2, jnp.exp(s - ll[:, col]), 0.0)
                    dp = lax.dot_general(doe, v, nt_dims, preferred_element_type=F32)
                    ds = (p * (dp - de[:, col]) * scale).astype(BF16)
                    dk = dk + lax.dot_general(ds, qe, tn_dims, preferred_element_type=F32)
                    dv = dv + lax.dot_general(p.astype(BF16), doe, tn_dims, preferred_element_type=F32)
                dk_ref[rs, cs] = dk
                dv_ref[rs, cs] = dv

    def main(sec):
        return pl.BlockSpec((qn * BLOCK, un * LANES), lambda u, i: (i, sec * ug + u))

    def prev(sec):
        return pl.BlockSpec((BLOCK, un * LANES), lambda u, i: (jnp.maximum(i * qn - 1, 0), sec * ug + u))

    def nxt(sec):
        return pl.BlockSpec((BLOCK, un * LANES), lambda u, i: (jnp.minimum((i + 1) * qn, nb - 1), sec * ug + u))

    in_specs = [main(0), nxt(0), prev(1), main(1), prev(2), main(2),
                main(0), nxt(0), main(0), nxt(0), main(0), nxt(0)]
    args = [qkv] * 6 + [do, do, lse, lse, delta, delta]
    shp = jax.ShapeDtypeStruct((n_rows, n_units * LANES), F32)
    return pl.pallas_call(
        body, name=name, grid=(ug, steps), in_specs=in_specs, out_specs=(main(0), main(0), main(0)),
        out_shape=(shp, shp, shp), compiler_params=_cparams(2),
    )(*args)


def _merge_groups(os_, lses, dils, name):
    s_len = os_[0].shape[0] * dils[0]
    tm = 512

    def body(*refs):
        o_refs, l_refs = refs[0:3], refs[3:6]
        o_ref, lse_ref = refs[6:8]
        so, sl = refs[8:11], refs[11:14]
        for pair in range(2):
            for g, d in enumerate(dils):
                rows = tm // d
                for r in range(d):
                    col = slice((pair * d + r) * LANES, (pair * d + r + 1) * LANES)
                    if d == 1:
                        so[g][...] = o_refs[g][:, col]
                        sl[g][...] = l_refs[g][:, col]
                    else:
                        so[g][pl.ds(r, rows, stride=d), :] = o_refs[g][:, col]
                        sl[g][pl.ds(r, rows, stride=d), :] = l_refs[g][:, col]
            l0, l1, l2 = sl[0][...], sl[1][...], sl[2][...]
            mx = jnp.maximum(jnp.maximum(l0, l1), l2)
            e0, e1, e2 = jnp.exp(l0 - mx), jnp.exp(l1 - mx), jnp.exp(l2 - mx)
            den = e0 + e1 + e2
            inv = 1.0 / den
            o_ref[:, pair * LANES:(pair + 1) * LANES] = (so[0][...] * (e0 * inv) + so[1][...] * (e1 * inv)
                                                         + so[2][...] * (e2 * inv))
            lse_ref[:, pair * LANES:(pair + 1) * LANES] = mx + jnp.log(den)

    in_specs = [pl.BlockSpec((tm // d, 2 * d * LANES), lambda i: (i, 0)) for d in dils] * 2
    out = pl.BlockSpec((tm, 2 * LANES), lambda i: (i, 0))
    shp = jax.ShapeDtypeStruct((s_len, 2 * LANES), F32)
    return pl.pallas_call(
        body, name=name, grid=(s_len // tm,), in_specs=in_specs, out_specs=(out, out), out_shape=(shp, shp),
        scratch_shapes=[pltpu.VMEM((tm, LANES), F32)] * 6, compiler_params=_cparams(1),
    )(*os_, *lses)


def _bwd_prep(do, o, lse, dils, sinks, name):
    s_len, width = do.shape
    n_pairs = width // LANES
    tm = 512
    has_sink = sinks is not None
    n_g = len(dils)

    def body(*refs):
        do_ref, o_ref, lse_ref = refs[:3]
        pos = 3
        if has_sink:
            sink_ref = refs[pos]
            pos += 1
        outs = refs[pos:pos + 3 * n_g]
        pos += 3 * n_g
        if has_sink:
            dsink_ref = refs[pos]
            pos += 1
        s_do, s_l, s_d = refs[pos:pos + 3]
        seg = _seg_matrix(HEAD_DIM)

        if has_sink:
            @pl.when(pl.program_id(0) == 0)
            def _():
                dsink_ref[...] = jnp.zeros_like(dsink_ref)

        for pair in range(n_pairs):
            col = slice(pair * LANES, (pair + 1) * LANES)
            dov = do_ref[:, col]
            lv = lse_ref[:, col]
            delta = _seg_sum(dov * o_ref[:, col], seg)
            if has_sink:
                dsink_ref[pair] += -jnp.sum(jnp.exp(sink_ref[pair] - lv) * delta, axis=0, keepdims=True)
            s_do[...] = dov
            s_l[...] = lv
            s_d[...] = delta
            for g, d in enumerate(dils):
                rows = tm // d
                for r in range(d):
                    oc = slice((pair * d + r) * LANES, (pair * d + r + 1) * LANES)
                    if d == 1:
                        a, b, c = s_do[...], s_l[...], s_d[...]
                    else:
                        a = s_do[pl.ds(r, rows, stride=d), :]
                        b = s_l[pl.ds(r, rows, stride=d), :]
                        c = s_d[pl.ds(r, rows, stride=d), :]
                    outs[3 * g][:, oc] = a.astype(BF16)
                    outs[3 * g + 1][:, oc] = b
                    outs[3 * g + 2][:, oc] = c

    row = pl.BlockSpec((tm, width), lambda i: (i, 0))
    in_specs = [row, row, row]
    args = [do, o, lse]
    if has_sink:
        in_specs.append(pl.BlockSpec((n_pairs, 1, LANES), lambda i: (0, 0, 0)))
        args.append(sinks)
    out_specs, out_shape = [], []
    for d in dils:
        for dt in (BF16, F32, F32):
            out_specs.append(pl.BlockSpec((tm // d, n_pairs * d * LANES), lambda i: (i, 0)))
            out_shape.append(jax.ShapeDtypeStruct((s_len // d, n_pairs * d * LANES), dt))
    if has_sink:
        out_specs.append(pl.BlockSpec((n_pairs, 1, LANES), lambda i: (0, 0, 0)))
        out_shape.append(jax.ShapeDtypeStruct((n_pairs, 1, LANES), F32))
    return pl.pallas_call(
        body, name=name, grid=(s_len // tm,), in_specs=in_specs, out_specs=tuple(out_specs),
        out_shape=tuple(out_shape), scratch_shapes=[pltpu.VMEM((tm, LANES), F32)] * 3, compiler_params=_cparams(1),
    )(*args)


def _mem_kv(mem, mem_gain, w_kv, k_gain, name):
    m_len = mem.shape[0]
    kw = M_HEADS * M_HEAD_DIM

    def body(mem_ref, mg_ref, w_ref, kg_ref, k_ref, v_ref):
        mv = mem_ref[...]
        r = lax.rsqrt(jnp.mean(mv * mv, axis=-1, keepdims=True) + EPS)
        mn = ((mv * r) * mg_ref[...]).astype(BF16)
        kv = jnp.dot(mn, w_ref[...], preferred_element_type=F32)
        for h in range(M_HEADS):
            col = slice(h * M_HEAD_DIM, (h + 1) * M_HEAD_DIM)
            t = kv[:, col]
            rk = lax.rsqrt(jnp.mean(t * t, axis=-1, keepdims=True) + EPS)
            k_ref[:, col] = ((t * rk) * kg_ref[...]).astype(BF16)
        v_ref[...] = kv[:, kw:].astype(BF16)

    shp = jax.ShapeDtypeStruct((m_len, kw), BF16)
    return pl.pallas_call(body, name=name, out_shape=(shp, shp),
                          compiler_params=pltpu.CompilerParams(vmem_limit_bytes=VMEM_LIMIT_BYTES))(mem, mem_gain, w_kv, k_gain)


def _mem_kv_bwd(mem, mem_gain, w_kv, k_gain, dk, dv, name):
    m_len, d = mem.shape
    kw = M_HEADS * M_HEAD_DIM

    def body(mem_ref, mg_ref, w_ref, kg_ref, dk_ref, dv_ref, dw_ref, dmg_ref, dkg_ref, dkv_ref):
        mv = mem_ref[...]
        r = lax.rsqrt(jnp.mean(mv * mv, axis=-1, keepdims=True) + EPS)
        mhat = mv * r
        mn = (mhat * mg_ref[...]).astype(BF16)
        kv = jnp.dot(mn, w_ref[...], preferred_element_type=F32)
        dkg = jnp.zeros((1, M_HEAD_DIM), F32)
        for h in range(M_HEADS):
            col = slice(h * M_HEAD_DIM, (h + 1) * M_HEAD_DIM)
            t = kv[:, col]
            rk = lax.rsqrt(jnp.mean(t * t, axis=-1, keepdims=True) + EPS)
            that = t * rk
            dy = dk_ref[:, col]
            dkg = dkg + jnp.sum(dy * that, axis=0, keepdims=True)
            dthat = dy * kg_ref[...]
            dkv_ref[:, col] = (rk * (dthat - that * jnp.mean(dthat * that, axis=-1, keepdims=True))).astype(BF16)
        dkv_ref[:, kw:] = dv_ref[...].astype(BF16)
        dkg_ref[...] = dkg
        dkv = dkv_ref[...]
        dw_ref[...] = lax.dot_general(mn, dkv, (((0,), (0,)), ((), ())), preferred_element_type=F32)
        dmn = lax.dot_general(dkv, w_ref[...], (((1,), (1,)), ((), ())), preferred_element_type=F32)
        dmg_ref[...] = jnp.sum(dmn * mhat, axis=0, keepdims=True)

    return pl.pallas_call(
        body, name=name,
        out_shape=(jax.ShapeDtypeStruct((d, 2 * kw), F32), jax.ShapeDtypeStruct((1, d), F32),
                   jax.ShapeDtypeStruct((1, M_HEAD_DIM), F32)),
        scratch_shapes=[pltpu.VMEM((m_len, 2 * kw), BF16)],
        compiler_params=pltpu.CompilerParams(vmem_limit_bytes=VMEM_LIMIT_BYTES),
    )(mem, mem_gain, w_kv, k_gain, dk, dv)


def _mem_attn_fwd(proj, cidx, mk, mv, q_gain, name):
    s_len = proj.shape[0]
    kw = M_HEADS * M_HEAD_DIM
    tm = 512
    scale = M_HEAD_DIM ** -0.5

    def body(q_ref, k_ref, v_ref, g_ref, o_ref):
        for h in range(M_HEADS):
            col = slice(h * M_HEAD_DIM, (h + 1) * M_HEAD_DIM)
            t = q_ref[:, col]
            rs = lax.rsqrt(jnp.mean(t * t, axis=-1, keepdims=True) + EPS)
            qn = ((t * rs) * g_ref[...]).astype(BF16)
            s = lax.dot_general(qn, k_ref[:, col], (((1,), (1,)), ((), ())), preferred_element_type=F32) * scale
            mx = jnp.max(s, axis=-1, keepdims=True)
            p = jnp.exp(s - mx)
            pn = (p * (1.0 / jnp.sum(p, axis=-1, keepdims=True))).astype(BF16)
            o_ref[:, col] = jnp.dot(pn, v_ref[:, col], preferred_element_type=F32).astype(BF16)

    whole = pl.BlockSpec((MEM_LEN, kw), lambda i: (0, 0))
    return pl.pallas_call(
        body, name=name, grid=(s_len // tm,),
        in_specs=[pl.BlockSpec((tm, kw), lambda i: (i, cidx)), whole, whole, pl.BlockSpec((1, M_HEAD_DIM), lambda i: (0, 0))],
        out_specs=pl.BlockSpec((tm, kw), lambda i: (i, 0)),
        out_shape=jax.ShapeDtypeStruct((s_len, kw), BF16), compiler_params=_cparams(1),
    )(proj, mk, mv, q_gain)


def _mem_attn_bwd(proj, cidx, mk, mv, q_gain, do, name):
    s_len = proj.shape[0]
    kw = M_HEADS * M_HEAD_DIM
    tm = 512
    scale = M_HEAD_DIM ** -0.5

    def body(q_ref, k_ref, v_ref, g_ref, do_ref, dq_ref, dk_ref, dv_ref, dg_ref):
        @pl.when(pl.program_id(0) == 0)
        def _():
            dk_ref[...] = jnp.zeros_like(dk_ref)
            dv_ref[...] = jnp.zeros_like(dv_ref)
            dg_ref[...] = jnp.zeros_like(dg_ref)

        for h in range(M_HEADS):
            col = slice(h * M_HEAD_DIM, (h + 1) * M_HEAD_DIM)
            t = q_ref[:, col]
            rs = lax.rsqrt(jnp.mean(t * t, axis=-1, keepdims=True) + EPS)
            that = t * rs
            qn = (that * g_ref[...]).astype(BF16)
            kh, vh = k_ref[:, col], v_ref[:, col]
            dob = do_ref[:, col].astype(BF16)
            s = lax.dot_general(qn, kh, (((1,), (1,)), ((), ())), preferred_element_type=F32) * scale
            mx = jnp.max(s, axis=-1, keepdims=True)
            p = jnp.exp(s - mx)
            p = p * (1.0 / jnp.sum(p, axis=-1, keepdims=True))
            dp = lax.dot_general(dob, vh, (((1,), (1,)), ((), ())), preferred_element_type=F32)
            ds = (p * (dp - jnp.sum(p * dp, axis=-1, keepdims=True)) * scale).astype(BF16)
            dqn = jnp.dot(ds, kh, preferred_element_type=F32)
            dk_ref[:, col] += lax.dot_general(ds, qn, (((0,), (0,)), ((), ())), preferred_element_type=F32)
            dv_ref[:, col] += lax.dot_general(p.astype(BF16), dob, (((0,), (0,)), ((), ())), preferred_element_type=F32)
            dg_ref[...] += jnp.sum(dqn * that, axis=0, keepdims=True)
            dthat = dqn * g_ref[...]
            dq_ref[:, col] = (rs * (dthat - that * jnp.mean(dthat * that, axis=-1, keepdims=True))).astype(BF16)

    whole = pl.BlockSpec((MEM_LEN, kw), lambda i: (0, 0))
    vec = pl.BlockSpec((1, M_HEAD_DIM), lambda i: (0, 0))
    row = pl.BlockSpec((tm, kw), lambda i: (i, 0))
    return pl.pallas_call(
        body, name=name, grid=(s_len // tm,),
        in_specs=[pl.BlockSpec((tm, kw), lambda i: (i, cidx)), whole, whole, vec, row],
        out_specs=(row, whole, whole, vec),
        out_shape=(jax.ShapeDtypeStruct((s_len, kw), BF16), jax.ShapeDtypeStruct((MEM_LEN, kw), F32),
                   jax.ShapeDtypeStruct((MEM_LEN, kw), F32), jax.ShapeDtypeStruct((1, M_HEAD_DIM), F32)),
        compiler_params=_cparams(1),
    )(proj, mk, mv, q_gain, do)


def _gate_merge(gates, pa, pb, pm, name):
    s_len, d = pa.shape
    tm = 256

    def body(g_ref, a_ref, b_ref, m_ref, o_ref):
        f = lambda v: v.astype(F32)
        o_ref[...] = (f(g_ref[:, 0:d]) * f(a_ref[...]) + f(g_ref[:, d:2 * d]) * f(b_ref[...])
                      + f(g_ref[:, 2 * d:3 * d]) * f(m_ref[...])).astype(BF16)

    row = pl.BlockSpec((tm, d), lambda i: (i, 0))
    return pl.pallas_call(
        body, name=name, grid=(s_len // tm,), in_specs=[pl.BlockSpec((tm, 3 * d), lambda i: (i, 0)), row, row, row],
        out_specs=row, out_shape=jax.ShapeDtypeStruct((s_len, d), BF16), compiler_params=_cparams(1),
    )(gates, pa, pb, pm)


def _gate_merge_bwd(dmerged, gates, pa, pb, pm, name):
    s_len, d = pa.shape
    tm = 256

    def body(dm_ref, g_ref, a_ref, b_ref, m_ref, da_ref, db_ref, dmm_ref, dg_ref, dbg_ref):
        @pl.when(pl.program_id(0) == 0)
        def _():
            dbg_ref[...] = jnp.zeros_like(dbg_ref)
        dm = dm_ref[...]
        for k, (p_ref, dp_ref) in enumerate(((a_ref, da_ref), (b_ref, db_ref), (m_ref, dmm_ref))):
            col = slice(k * d, (k + 1) * d)
            g = g_ref[:, col].astype(F32)
            dp_ref[...] = (dm * g).astype(BF16)
            dpre = (dm * p_ref[...].astype(F32)) * (g * (1.0 - g))
            dbg_ref[:, col] += jnp.sum(dpre, axis=0, keepdims=True)
            dg_ref[:, col] = dpre.astype(BF16)

    row = pl.BlockSpec((tm, d), lambda i: (i, 0))
    wide = pl.BlockSpec((tm, 3 * d), lambda i: (i, 0))
    shp = jax.ShapeDtypeStruct((s_len, d), BF16)
    return pl.pallas_call(
        body, name=name, grid=(s_len // tm,), in_specs=[row, wide, row, row, row],
        out_specs=(row, row, row, wide, pl.BlockSpec((1, 3 * d), lambda i: (0, 0))),
        out_shape=(shp, shp, shp, jax.ShapeDtypeStruct((s_len, 3 * d), BF16), jax.ShapeDtypeStruct((1, 3 * d), F32)),
        compiler_params=_cparams(1),
    )(dmerged, gates, pa, pb, pm)


CONV_CHUNK = 256


def _pick_row(tile, j):
    row = lax.broadcasted_iota(jnp.int32, tile.shape, 0)
    return jnp.sum(jnp.where(row == j, tile, jnp.zeros_like(tile)), axis=0, keepdims=True)


def _rows_before(ref, start, k):
    cur = ref[pl.ds(start, CONV_CHUNK), :].astype(F32)
    prev = ref[pl.ds(pl.multiple_of(jnp.maximum(start - 16, 0), 16), 16), :].astype(F32)
    prev = jnp.where(start > 0, prev, jnp.zeros_like(prev))
    rolled = pltpu.roll(cur, k, 0)
    row = lax.broadcasted_iota(jnp.int32, cur.shape, 0)
    for j in range(k):
        rolled = jnp.where(row == j, _pick_row(prev, 16 - k + j), rolled)
    return rolled


def _rows_after(ref, start, k):
    cur = ref[pl.ds(start, CONV_CHUNK), :]
    nxt = ref[pl.ds(pl.multiple_of(start + CONV_CHUNK, 8), 8), :]
    rolled = pltpu.roll(cur, CONV_CHUNK - k, 0)
    row = lax.broadcasted_iota(jnp.int32, cur.shape, 0)
    for j in range(k):
        rolled = jnp.where(row == CONV_CHUNK - k + j, _pick_row(nxt, j), rolled)
    return rolled


def _conv_pre(u_ref, w_ref, b_ref, start):
    u2 = _rows_before(u_ref, start, 2)
    u1 = _rows_before(u_ref, start, 1)
    u0 = u_ref[pl.ds(start, CONV_CHUNK), :].astype(F32)
    c = ((b_ref[...] + w_ref[0:1, :] * u2) + w_ref[1:2, :] * u1) + w_ref[2:3, :] * u0
    return c, (u2, u1, u0)


def _conv_glu(u, conv_w, conv_b, name):
    s_len = u.shape[0]
    nblk = D_FF // LANES

    def body(ua_ref, ug_ref, wa_ref, wg_ref, ba_ref, bg_ref, o_ref):
        def chunk(ci, carry):
            start = pl.multiple_of(ci * CONV_CHUNK, CONV_CHUNK)
            ca, _ = _conv_pre(ua_ref, wa_ref, ba_ref, start)
            cg, _ = _conv_pre(ug_ref, wg_ref, bg_ref, start)
            o_ref[pl.ds(start, CONV_CHUNK), :] = ((ca * _sigmoid(ca)) * cg).astype(BF16)
            return carry
        lax.fori_loop(0, s_len // CONV_CHUNK, chunk, 0)

    def col(rows, off):
        return pl.BlockSpec((rows, LANES), lambda j: (0, off + j))

    return pl.pallas_call(
        body, name=name, grid=(nblk,),
        in_specs=[col(s_len, 0), col(s_len, nblk), col(3, 0), col(3, nblk), col(1, 0), col(1, nblk)],
        out_specs=col(s_len, 0), out_shape=jax.ShapeDtypeStruct((s_len, D_FF), BF16), compiler_params=_cparams(1),
    )(u, u, conv_w, conv_w, conv_b, conv_b)


def _conv_glu_bwd(dact, u, conv_w, conv_b, name):
    s_len = u.shape[0]
    nblk = D_FF // LANES
    n_chunks = s_len // CONV_CHUNK

    def body(da_ref, ua_ref, ug_ref, wa_ref, wg_ref, ba_ref, bg_ref,
             dua_ref, dug_ref, dwa_ref, dwg_ref, dba_ref, dbg_ref, sa, sg):
        sa[pl.ds(s_len, 8), :] = jnp.zeros((8, LANES), F32)
        sg[pl.ds(s_len, 8), :] = jnp.zeros((8, LANES), F32)
        zero = jnp.zeros((1, LANES), F32)

        def chunk1(ci, carry):
            start = pl.multiple_of(ci * CONV_CHUNK, CONV_CHUNK)
            ca, ua = _conv_pre(ua_ref, wa_ref, ba_ref, start)
            cg, ug = _conv_pre(ug_ref, wg_ref, bg_ref, start)
            dact_v = da_ref[pl.ds(start, CONV_CHUNK), :].astype(F32)
            sig = _sigmoid(ca)
            dcg = dact_v * (ca * sig)
            dca = (dact_v * cg) * (sig * (1.0 + ca * (1.0 - sig)))
            sa[pl.ds(start, CONV_CHUNK), :] = dca
            sg[pl.ds(start, CONV_CHUNK), :] = dcg
            out = [carry[0] + jnp.sum(dca, axis=0, keepdims=True), carry[1] + jnp.sum(dcg, axis=0, keepdims=True)]
            for j in range(3):
                out.append(carry[2 + j] + jnp.sum(dca * ua[j], axis=0, keepdims=True))
            for j in range(3):
                out.append(carry[5 + j] + jnp.sum(dcg * ug[j], axis=0, keepdims=True))
            return tuple(out)

        acc = lax.fori_loop(0, n_chunks, chunk1, (zero,) * 8)
        dba_ref[...] = acc[0]
        dbg_ref[...] = acc[1]
        for j in range(3):
            dwa_ref[j:j + 1, :] = acc[2 + j]
            dwg_ref[j:j + 1, :] = acc[5 + j]

        def chunk2(ci, carry):
            start = pl.multiple_of(ci * CONV_CHUNK, CONV_CHUNK)
            for s_ref, w_ref, o_ref in ((sa, wa_ref, dua_ref), (sg, wg_ref, dug_ref)):
                d0 = s_ref[pl.ds(start, CONV_CHUNK), :]
                d1 = _rows_after(s_ref, start, 1)
                d2 = _rows_after(s_ref, start, 2)
                o_ref[pl.ds(start, CONV_CHUNK), :] = (w_ref[2:3, :] * d0 + w_ref[1:2, :] * d1
                                                      + w_ref[0:1, :] * d2).astype(BF16)
            return carry
        lax.fori_loop(0, n_chunks, chunk2, 0)

    def col(rows, off):
        return pl.BlockSpec((rows, LANES), lambda j: (0, off + j))

    big = jax.ShapeDtypeStruct((s_len, D_FF), BF16)
    return pl.pallas_call(
        body, name=name, grid=(nblk,),
        in_specs=[col(s_len, 0), col(s_len, 0), col(s_len, nblk), col(3, 0), col(3, nblk), col(1, 0), col(1, nblk)],
        out_specs=(col(s_len, 0), col(s_len, 0), col(3, 0), col(3, 0), col(1, 0), col(1, 0)),
        out_shape=(big, big, jax.ShapeDtypeStruct((3, D_FF), F32), jax.ShapeDtypeStruct((3, D_FF), F32),
                   jax.ShapeDtypeStruct((1, D_FF), F32), jax.ShapeDtypeStruct((1, D_FF), F32)),
        scratch_shapes=[pltpu.VMEM((s_len + 8, LANES), F32)] * 2, compiler_params=_cparams(1),
    )(dact, u, u, conv_w, conv_w, conv_b, conv_b)


def _loss_head(y, target, name):
    s_len, d = y.shape
    tm = 512

    def body(y_ref, t_ref, dy_ref, dyb_ref, l_ref):
        @pl.when(pl.program_id(0) == 0)
        def _():
            l_ref[...] = jnp.zeros_like(l_ref)
        err = y_ref[...] - t_ref[...]
        dy = err * (1.0 / d)
        dy_ref[...] = dy
        dyb_ref[...] = dy.astype(BF16)
        part = 0.5 * jnp.sum(jnp.mean(err * err, axis=-1, keepdims=True), axis=0, keepdims=True)
        l_ref[...] += jnp.broadcast_to(part, l_ref.shape)

    row = pl.BlockSpec((tm, d), lambda i: (i, 0))
    return pl.pallas_call(
        body, name=name, grid=(s_len // tm,), in_specs=[row, row],
        out_specs=(row, row, pl.BlockSpec((8, LANES), lambda i: (0, 0))),
        out_shape=(jax.ShapeDtypeStruct((s_len, d), F32), jax.ShapeDtypeStruct((s_len, d), BF16),
                   jax.ShapeDtypeStruct((8, LANES), F32)),
        compiler_params=_cparams(1),
    )(y, target)


def _rope_tables(positions):
    half = ROPE_DIMS // 2
    freqs = jnp.exp(jnp.arange(half, dtype=F32) * (-2.0 * math.log(ROPE_THETA) / ROPE_DIMS))
    ang = positions.reshape(-1).astype(F32)[:, None] * freqs
    cos, sin = jnp.cos(ang), jnp.sin(ang)
    n = ang.shape[0]
    zeros = lambda w: jnp.zeros((n, w), F32)
    c = jnp.concatenate([cos, cos, jnp.ones((n, HEAD_DIM - ROPE_DIMS), F32)], axis=1)
    s1 = jnp.concatenate([-sin, zeros(HEAD_DIM - half)], axis=1)
    s2 = jnp.concatenate([zeros(half), sin, zeros(HEAD_DIM - ROPE_DIMS)], axis=1)
    return tuple(jnp.tile(t, (1, 2)) for t in (c, s1, s2))


def _two(v):
    return jnp.tile(v.reshape(1, HEAD_DIM), (1, 2))


def _fold_heads(g):
    return g[0, :HEAD_DIM] + g[0, HEAD_DIM:]


MIX_WEIGHTS = ('w_gate', 'w_mem_kv', 'w_o_a', 'w_o_b', 'w_o_m', 'w_out')
FFN_WEIGHTS = ('w_up', 'conv_w', 'w_down')


def _device_step(x, mem, positions, target, w, hooks=None):
    tabs = _rope_tables(positions)
    dils = tuple(d for _, d in A_GROUPS)
    grads = {}
    w = dict(w)

    h, r1 = _rms_fwd(x, w['attn_norm'], "rms1")
    proj = _mm_rows([(h, w['w_in'], 0)], "mm_in")

    qkv_a, o_g, lse_g = [], [], []
    for gi, (window, d) in enumerate(A_GROUPS):
        gq, gk = _two(w['a_q_norm'][gi]), _two(w['a_k_norm'][gi])
        qkv = _qk_prep(proj, 6 * gi, d, False, gq, gk, tabs, f"qk_prep_a{gi}")
        o, lse = _band_fwd(qkv, 2 * d, window // d, None, f"band_fwd_a{gi}")
        qkv_a.append(qkv)
        o_g.append(o)
        lse_g.append(lse)
    o_a, lse_a = _merge_groups(o_g, lse_g, dils, "merge_a")

    gbq, gbk = _two(w['b_q_norm']), _two(w['b_k_norm'])
    sinks = jnp.repeat(w['b_sinks'].reshape(4, 2), HEAD_DIM, axis=1).reshape(4, 1, LANES)
    qkv_b = _qk_prep(proj, 18, 1, True, gbq, gbk, tabs, "qk_prep_b")
    o_b, lse_b = _band_fwd(qkv_b, 4, B_WINDOW - 1, sinks, "band_fwd_b")

    if hooks is not None:
        w.update(hooks.weights('mix', o_b))
    gates = _mm_rows([(h, w['w_gate'], 0)], "mm_gate", bias=w['b_gate'], sigmoid=True, out_dtypes=(BF16,))
    mk, mv = _mem_kv(mem, w['mem_norm'], w['w_mem_kv'], w['m_k_norm'], "mem_kv")
    o_m = _mem_attn_fwd(proj, 6, mk, mv, w['m_q_norm'], "mem_attn")

    pa = _mm_rows([(o_a, w['w_o_a'], 0)], "mm_oa", out_dtypes=(BF16,))
    pb = _mm_rows([(o_b, w['w_o_b'], 0)], "mm_ob", out_dtypes=(BF16,))
    pm = _mm_rows([(o_m, w['w_o_m'], 0)], "mm_om", out_dtypes=(BF16,))
    merged = _gate_merge(gates, pa, pb, pm, "gate_merge")
    x1 = _mm_rows([(merged, w['w_out'], 0)], "mm_out", res=x)

    if hooks is not None:
        w.update(hooks.weights('ffn', x1))
    h2, r2 = _rms_fwd(x1, w['ffn_norm'], "rms2")
    u = _mm_rows([(h2, w['w_up'], 0)], "mm_up", out_dtypes=(BF16,))
    act = _conv_glu(u, w['conv_w'], w['conv_b'], "conv_glu")
    y = _mm_rows([(act, w['w_down'], 0)], "mm_down", res=x1)
    dy, dy_b, loss = _loss_head(y, target, "loss_head")

    dact = _mm_rows([(dy_b, w['w_down'], 0)], "mm_d_act", nt=True, out_dtypes=(BF16,))
    grads['w_down'] = _mm_tn(act, dy_b, "mm_dw_down")
    du_a, du_g, dcw_a, dcw_g, dcb_a, dcb_g = _conv_glu_bwd(dact, u, w['conv_w'], w['conv_b'], "conv_glu_bwd")
    grads['conv_w'] = jnp.concatenate([dcw_a, dcw_g], axis=1)
    grads['conv_b'] = jnp.concatenate([dcb_a, dcb_g], axis=1)
    dh2 = _mm_rows([(du_a, w['w_up'], 0), (du_g, w['w_up'], 1)], "mm_d_h2", nt=True)
    grads['w_up'] = jnp.concatenate([_mm_tn(h2, du_a, "mm_dw_up_a"), _mm_tn(h2, du_g, "mm_dw_up_g")], axis=1)
    ffn_gain = w['ffn_norm']
    if hooks is not None:
        ffn_gain = ffn_gain + hooks.grads('ffn', grads)[0:1, 0:1]
    dx1, dx1_b, grads['ffn_norm'] = _rms_bwd(dh2, x1, r2, ffn_gain, dy, "rms2_bwd", bf16_copy=True)

    dmerged = _mm_rows([(dx1_b, w['w_out'], 0)], "mm_d_merged", nt=True)
    grads['w_out'] = _mm_tn(merged, dx1_b, "mm_dw_out")
    dpa, dpb, dpm, dgpre, grads['b_gate'] = _gate_merge_bwd(dmerged, gates, pa, pb, pm, "gate_merge_bwd")
    do_a = _mm_rows([(dpa, w['w_o_a'], 0)], "mm_d_oa", nt=True)
    do_b = _mm_rows([(dpb, w['w_o_b'], 0)], "mm_d_ob", nt=True)
    do_m = _mm_rows([(dpm, w['w_o_m'], 0)], "mm_d_om", nt=True)
    grads['w_o_a'] = _mm_tn(o_a, dpa, "mm_dw_oa")
    grads['w_o_b'] = _mm_tn(o_b, dpb, "mm_dw_ob")
    grads['w_o_m'] = _mm_tn(o_m, dpm, "mm_dw_om")
    grads['w_gate'] = _mm_tn(h, dgpre, "mm_dw_gate")
    dq_m, dmk, dmv, grads['m_q_norm'] = _mem_attn_bwd(proj, 6, mk, mv, w['m_q_norm'], do_m, "mem_attn_bwd")
    grads['w_mem_kv'], grads['mem_norm'], grads['m_k_norm'] = _mem_kv_bwd(
        mem, w['mem_norm'], w['w_mem_kv'], w['m_k_norm'], dmk, dmv, "mem_kv_bwd")
    a_gain = w['a_q_norm']
    if hooks is not None:
        a_gain = a_gain + hooks.grads('mix', grads)[0:1, 0:1]

    prep = _bwd_prep(do_a, o_a, lse_a, dils, None, "bwd_prep_a")
    dproj, dgq_a, dgk_a = [], [], []
    for gi, (window, d) in enumerate(A_GROUPS):
        gq, gk = _two(a_gain[gi]), _two(w['a_k_norm'][gi])
        dqkv = _band_bwd(qkv_a[gi], prep[3 * gi], prep[3 * gi + 1], prep[3 * gi + 2], 2 * d, window // d,
                         f"band_bwd_a{gi}")
        dp, dgq, dgk = _qk_prep_bwd(dqkv, proj, 6 * gi, d, False, gq, gk, tabs, f"qk_prep_bwd_a{gi}")
        dproj.append(dp)
        dgq_a.append(_fold_heads(dgq))
        dgk_a.append(_fold_heads(dgk))
    grads['a_q_norm'] = jnp.stack(dgq_a)
    grads['a_k_norm'] = jnp.stack(dgk_a)

    do_bu, lse_bu, delta_bu, dsink = _bwd_prep(do_b, o_b, lse_b, (1,), sinks, "bwd_prep_b")
    dqkv = _band_bwd(qkv_b, do_bu, lse_bu, delta_bu, 4, B_WINDOW - 1, "band_bwd_b")
    dp_b, dgq, dgk = _qk_prep_bwd(dqkv, proj, 18, 1, True, gbq, gbk, tabs, "qk_prep_bwd_b")
    dproj.append(dp_b)
    grads['b_q_norm'] = _fold_heads(dgq)
    grads['b_k_norm'] = _fold_heads(dgk)
    grads['b_sinks'] = jnp.stack([dsink[:, 0, 0], dsink[:, 0, HEAD_DIM]], axis=1).reshape(8)

    dproj.append(dq_m)

    cols = (0, 1, 2, 3, 6)
    grads['w_in'] = jnp.concatenate([_mm_tn(h, dp, f"mm_dw_in{k}") for k, dp in enumerate(dproj)], axis=1)
    dh = _mm_rows([(dp, w['w_in'], c) for dp, c in zip(dproj, cols)] + [(dgpre, w['w_gate'], 0)], "mm_d_h", nt=True)
    grad_x, grads['attn_norm'] = _rms_bwd(dh, x, r1, w['attn_norm'], dx1, "rms1_bwd")
    return loss, grad_x, grads


def _coords():
    return lax.axis_index("x"), lax.axis_index("y"), lax.axis_index("c")


def _slot(p):
    return 4 * p[0] + 2 * p[1] + p[2]


def _peers(me):
    x, y, c = me
    out = []
    for mask in range(1, N_DEV):
        out.append((1 - x if mask & 4 else x, 1 - y if mask & 2 else y, 1 - c if mask & 1 else c))
    return out


HBM_SPEC = pl.BlockSpec(memory_space=pltpu.HBM)


def _all_gather(shards, name):
    n = len(shards)

    def body(*refs):
        ins, outs = refs[:n], refs[n:2 * n]
        token, send_sems, recv_sems, local_sems = refs[2 * n:]
        token[...] = jnp.zeros_like(token)
        x, y, c = _coords()
        me, sibling = (x, y, c), (x, y, 1 - c)
        chips = [(1 - x, y), (x, 1 - y), (1 - x, 1 - y)]

        def copy(a, k, block, to, src=None):
            dst = outs[a].at[_slot(block)]
            return pltpu.make_async_remote_copy(
                src_ref=dst if src is None else src, dst_ref=dst, send_sem=send_sems.at[a, k],
                recv_sem=recv_sems.at[a, k], device_id=to, device_id_type=MESH)

        mine = [pltpu.make_async_copy(ins[a], outs[a].at[_slot(me)], local_sems.at[a]) for a in range(n)]
        for cp in mine:
            cp.start()
        first = []
        for a in range(n):
            first.append(copy(a, 0, me, sibling, src=ins[a]))
            first += [copy(a, 1 + j, me, (*chip, c), src=ins[a]) for j, chip in enumerate(chips)]
        for cp in first:
            cp.start()
        passed = []
        for a in range(n):
            for j, chip in enumerate(chips):
                copy(a, 1 + j, (*chip, c), me).wait_recv()
                fwd = copy(a, 4 + j, (*chip, c), sibling)
                fwd.start()
                passed.append(fwd)
        for a in range(n):
            copy(a, 0, sibling, me).wait_recv()
            for j, chip in enumerate(chips):
                copy(a, 4 + j, (*chip, 1 - c), me).wait_recv()
        for cp in first + passed:
            cp.wait_send()
        for cp in mine:
            cp.wait()

    return pl.pallas_call(
        body, name=name, in_specs=[HBM_SPEC] * n,
        out_specs=tuple([HBM_SPEC] * n + [pl.BlockSpec(memory_space=pltpu.VMEM)]),
        out_shape=tuple([jax.ShapeDtypeStruct((N_DEV,) + s.shape, s.dtype) for s in shards]
                        + [jax.ShapeDtypeStruct((8, LANES), F32)]),
        scratch_shapes=[pltpu.SemaphoreType.DMA((n, 7)), pltpu.SemaphoreType.DMA((n, 7)), pltpu.SemaphoreType.DMA((n,))],
    )(*shards)


def _exchange(blocks, name):
    n = len(blocks)

    def body(*refs):
        ins, outs = refs[:n], refs[n:2 * n]
        send_sems, recv_sems, local_sems = refs[2 * n:]
        me = _coords()
        peers = _peers(me)
        mine = [pltpu.make_async_copy(ins[a].at[_slot(me)], outs[a].at[_slot(me)], local_sems.at[a]) for a in range(n)]
        for cp in mine:
            cp.start()

        def copy(a, k):
            return pltpu.make_async_remote_copy(
                src_ref=ins[a].at[_slot(peers[k])], dst_ref=outs[a].at[_slot(me)], send_sem=send_sems.at[a, k],
                recv_sem=recv_sems.at[a, k], device_id=peers[k], device_id_type=MESH)

        def arrival(a, k):
            return pltpu.make_async_remote_copy(
                src_ref=ins[a].at[_slot(me)], dst_ref=outs[a].at[_slot(peers[k])], send_sem=send_sems.at[a, k],
                recv_sem=recv_sems.at[a, k], device_id=peers[k], device_id_type=MESH)

        sends = [copy(a, k) for a in range(n) for k in range(N_DEV - 1)]
        for cp in sends:
            cp.start()
        for a in range(n):
            for k in range(N_DEV - 1):
                arrival(a, k).wait_recv()
        for cp in sends:
            cp.wait_send()
        for cp in mine:
            cp.wait()

    return pl.pallas_call(
        body, name=name, in_specs=[HBM_SPEC] * n, out_specs=tuple([HBM_SPEC] * n),
        out_shape=tuple(jax.ShapeDtypeStruct(b.shape, b.dtype) for b in blocks),
        scratch_shapes=[pltpu.SemaphoreType.DMA((n, 7)), pltpu.SemaphoreType.DMA((n, 7)), pltpu.SemaphoreType.DMA((n,))],
    )(*blocks)


SEM_SPEC = pl.BlockSpec(memory_space=pltpu.SEMAPHORE)
SIDE_EFFECT = pltpu.SideEffectType.DATAFLOW_SIDE_EFFECTING


def _exchange_start(blocks, name, gather=False):
    n = len(blocks)

    def body(*refs):
        ins, lands = refs[:n], refs[n:2 * n]
        send_sems, recv_sems = refs[2 * n], refs[2 * n + 1]
        token = refs[-1]
        me = _coords()
        peers = _peers(me)
        for a in range(n):
            for k in range(N_DEV - 1):
                pltpu.make_async_remote_copy(
                    src_ref=ins[a] if gather else ins[a].at[_slot(peers[k])], dst_ref=lands[a].at[_slot(me)],
                    send_sem=send_sems.at[a * (N_DEV - 1) + k], recv_sem=recv_sems.at[a * (N_DEV - 1) + k],
                    device_id=peers[k], device_id_type=MESH).start()
        token[...] = jnp.zeros_like(token)

    land_shapes = [((N_DEV,) + b.shape) if gather else b.shape for b in blocks]
    hbm_in = [pltpu.HBM(b.shape, b.dtype) for b in blocks]
    hbm_land = [pltpu.HBM(s, b.dtype) for s, b in zip(land_shapes, blocks)]
    sems = pltpu.SemaphoreType.DMA((n * (N_DEV - 1),))
    ins = [pltpu.with_memory_space_constraint(b, pltpu.HBM) for b in blocks]
    lands = [pltpu.with_memory_space_constraint(lax.empty(s, b.dtype), pltpu.HBM) for s, b in zip(land_shapes, blocks)]
    return pl.pallas_call(
        body, name=name, out_shape=(sems, sems, *hbm_in, *hbm_land, jax.ShapeDtypeStruct((8, LANES), F32)),
        in_specs=[HBM_SPEC] * (2 * n),
        out_specs=(SEM_SPEC, SEM_SPEC, *([HBM_SPEC] * (2 * n)), pl.BlockSpec(memory_space=pltpu.VMEM)),
        input_output_aliases={i: 2 + i for i in range(2 * n)},
        compiler_params=pltpu.CompilerParams(has_side_effects=SIDE_EFFECT),
    )(*ins, *lands)


def _exchange_wait(started, after, name, gather=False):
    n = (len(started) - 3) // 2
    send_sems, recv_sems = started[0], started[1]
    thru = started[2:2 + 2 * n]

    def body(*refs):
        ins, lands = refs[:n], refs[n:2 * n]
        send_ref, recv_ref = refs[2 * n], refs[2 * n + 1]
        me = _coords()
        peers = _peers(me)
        for a in range(n):
            for k in range(N_DEV - 1):
                cp = pltpu.make_async_remote_copy(
                    src_ref=ins[a] if gather else ins[a].at[_slot(peers[k])], dst_ref=lands[a].at[_slot(peers[k])],
                    send_sem=send_ref.at[a * (N_DEV - 1) + k], recv_sem=recv_ref.at[a * (N_DEV - 1) + k],
                    device_id=peers[k], device_id_type=MESH)
                cp.wait_send()
                cp.wait_recv()

    hbm = [pltpu.HBM(t.shape, t.dtype) for t in thru]
    res = pl.pallas_call(
        body, name=name, out_shape=tuple(hbm),
        in_specs=[HBM_SPEC] * (2 * n) + [SEM_SPEC, SEM_SPEC, pl.BlockSpec(memory_space=pl.ANY)],
        out_specs=tuple([HBM_SPEC] * (2 * n)), input_output_aliases={i: i for i in range(2 * n)},
        compiler_params=pltpu.CompilerParams(has_side_effects=SIDE_EFFECT),
    )(*thru, send_sems, recv_sems, after)
    return res[n:]


def _all_sum(p, name):
    def body(p_ref, o_ref, recv, send_sems, recv_sems):
        me = _coords()
        peers = _peers(me)
        recv[_slot(me)] = p_ref[...]

        def copy(k, landing):
            return pltpu.make_async_remote_copy(
                src_ref=p_ref, dst_ref=recv.at[_slot(landing)], send_sem=send_sems.at[k], recv_sem=recv_sems.at[k],
                device_id=peers[k], device_id_type=MESH)

        sends = [copy(k, me) for k in range(N_DEV - 1)]
        for cp in sends:
            cp.start()
        for k in range(N_DEV - 1):
            copy(k, peers[k]).wait_recv()
        for cp in sends:
            cp.wait_send()
        acc = recv[0]
        for s in range(1, N_DEV):
            acc = acc + recv[s]
        o_ref[...] = acc

    vmem = pl.BlockSpec(memory_space=pltpu.VMEM)
    return pl.pallas_call(
        body, name=name, in_specs=[vmem], out_specs=vmem, out_shape=jax.ShapeDtypeStruct(p.shape, F32),
        scratch_shapes=[pltpu.VMEM((N_DEV,) + p.shape, F32), pltpu.SemaphoreType.DMA((N_DEV - 1,)),
                        pltpu.SemaphoreType.DMA((N_DEV - 1,))],
    )(p)


def _adam(w, g, m, v):
    m2 = ADAM_B1 * m + (1.0 - ADAM_B1) * g
    v2 = ADAM_B2 * v + (1.0 - ADAM_B2) * (g * g)
    m_hat = m2 / (1.0 - ADAM_B1 ** ADAM_STEP)
    v_hat = v2 / (1.0 - ADAM_B2 ** ADAM_STEP)
    delta = -ADAM_LR * (m_hat / (jnp.sqrt(v_hat) + ADAM_EPS) + ADAM_WD * w)
    return delta, m2, v2


def _row_tile(rows, cols):
    best = rows
    for t in range(16, rows, 16):
        if rows % t == 0 and t * cols * 4 <= (1 << 20):
            best = t
    return best


def _adam_reduce(parts, w, m, v, name):
    rows, cols = w.shape
    tr = _row_tile(rows, cols)

    def body(p_ref, w_ref, m_ref, v_ref, g_ref, d_ref, m2_ref, v2_ref):
        g = p_ref[0].astype(F32)
        for s in range(1, N_DEV):
            g = g + p_ref[s].astype(F32)
        g_ref[...] = g
        d_ref[...], m2_ref[...], v2_ref[...] = _adam(w_ref[...], g, m_ref[...], v_ref[...])

    blk = pl.BlockSpec((tr, cols), lambda i: (i, 0))
    shp = jax.ShapeDtypeStruct((rows, cols), F32)
    return pl.pallas_call(
        body, name=name, grid=(rows // tr,),
        in_specs=[pl.BlockSpec((N_DEV, tr, cols), lambda i: (0, i, 0)), blk, blk, blk],
        out_specs=(blk,) * 4, out_shape=(shp,) * 4, compiler_params=_cparams(1),
    )(parts, w, m, v)


PACK_COLS = 1024
PACK = {'attn_norm': (0, 1, 1024), 'mem_norm': (1, 1, 1024), 'ffn_norm': (2, 1, 1024), 'b_gate': (3, 3, 1024),
        'conv_b': (6, 6, 1024), 'a_q_norm': (12, 3, 64), 'a_k_norm': (15, 3, 64), 'b_q_norm': (18, 1, 64),
        'b_k_norm': (19, 1, 64), 'm_q_norm': (20, 1, 128), 'm_k_norm': (21, 1, 128), 'b_sinks': (22, 1, 8)}
PACK_LOSS_ROW = 23
PACK_ROWS = 24


def _pack_pieces(name, width):
    r0, nr, lanes = PACK[name]
    out = []
    for j in range(nr):
        if lanes == PACK_COLS:
            w = min(PACK_COLS, width - j * PACK_COLS)
            out.append((r0 + j, slice(0, 1), slice(j * PACK_COLS, j * PACK_COLS + w), w))
        else:
            out.append((r0 + j, slice(j, j + 1), slice(0, lanes), lanes))
    return out


def _pack_small(grads, loss_tile, name):
    names = list(PACK)

    def body(*refs):
        o_ref = refs[-1]
        o_ref[...] = jnp.zeros_like(o_ref)
        for k, nm in enumerate(names):
            for row, rs, ls, w in _pack_pieces(nm, refs[k].shape[1]):
                o_ref[row:row + 1, 0:w] = refs[k][rs, ls]
        o_ref[PACK_LOSS_ROW:PACK_LOSS_ROW + 1, 0:1] = refs[len(names)][0:1, 0:1]

    vmem = pl.BlockSpec(memory_space=pltpu.VMEM)
    args = [grads[nm] for nm in names] + [loss_tile]
    return pl.pallas_call(body, name=name, in_specs=[vmem] * len(args), out_specs=vmem,
                          out_shape=jax.ShapeDtypeStruct((PACK_ROWS, PACK_COLS), F32))(*args)


def _adam_small(gsum, ws, ms, vs, name):
    names = list(PACK)
    n = len(names)

    def body(*refs):
        g_ref = refs[0]
        w_refs, m_refs, v_refs = refs[1:1 + n], refs[1 + n:1 + 2 * n], refs[1 + 2 * n:1 + 3 * n]
        outs = refs[1 + 3 * n:]
        outs[0][...] = g_ref[PACK_LOSS_ROW:PACK_LOSS_ROW + 1, 0:1]
        for k, nm in enumerate(names):
            o_g, o_d, o_m, o_v = outs[1 + 4 * k:5 + 4 * k]
            for row, rs, ls, width in _pack_pieces(nm, w_refs[k].shape[1]):
                src = (rs, ls)
                g = g_ref[row:row + 1, 0:width]
                d, m2, v2 = _adam(w_refs[k][src], g, m_refs[k][src], v_refs[k][src])
                o_g[src] = g
                o_d[src] = d
                o_m[src] = m2
                o_v[src] = v2

    vmem = pl.BlockSpec(memory_space=pltpu.VMEM)
    shapes = [jax.ShapeDtypeStruct((1, 1), F32)]
    for nm in names:
        shapes += [jax.ShapeDtypeStruct(ws[nm].shape, F32)] * 4
    args = [gsum] + [ws[nm] for nm in names] + [ms[nm] for nm in names] + [vs[nm] for nm in names]
    return pl.pallas_call(
        body, name=name, in_specs=[vmem] * len(args), out_specs=tuple([vmem] * len(shapes)), out_shape=tuple(shapes),
    )(*args)


def _as2d(name, a):
    return a.reshape(a.shape[-2], a.shape[-1]) if a.ndim == 3 else a


def kernel(x, mem, positions, attn_norm, w_in, a_q_norm, a_k_norm, b_q_norm, b_k_norm, b_sinks, mem_norm, w_mem_kv, m_q_norm, m_k_norm, w_o_a, w_o_b, w_o_m, w_gate, b_gate, w_out, ffn_norm, w_up, conv_w, conv_b, w_down, loss_target, m_attn_norm, m_w_in, m_a_q_norm, m_a_k_norm, m_b_q_norm, m_b_k_norm, m_b_sinks, m_mem_norm, m_w_mem_kv, m_m_q_norm, m_m_k_norm, m_w_o_a, m_w_o_b, m_w_o_m, m_w_gate, m_b_gate, m_w_out, m_ffn_norm, m_w_up, m_conv_w, m_conv_b, m_w_down, v_attn_norm, v_w_in, v_a_q_norm, v_a_k_norm, v_b_q_norm, v_b_k_norm, v_b_sinks, v_mem_norm, v_w_mem_kv, v_m_q_norm, v_m_k_norm, v_w_o_a, v_w_o_b, v_w_o_m, v_w_gate, v_b_gate, v_w_out, v_ffn_norm, v_w_up, v_conv_w, v_conv_b, v_w_down):
    given = dict(attn_norm=attn_norm, w_in=w_in, a_q_norm=a_q_norm, a_k_norm=a_k_norm, b_q_norm=b_q_norm, b_k_norm=b_k_norm, b_sinks=b_sinks, mem_norm=mem_norm, w_mem_kv=w_mem_kv, m_q_norm=m_q_norm, m_k_norm=m_k_norm, w_o_a=w_o_a, w_o_b=w_o_b, w_o_m=w_o_m, w_gate=w_gate, b_gate=b_gate, w_out=w_out, ffn_norm=ffn_norm, w_up=w_up, conv_w=conv_w, conv_b=conv_b, w_down=w_down)
    mom1 = dict(attn_norm=m_attn_norm, w_in=m_w_in, a_q_norm=m_a_q_norm, a_k_norm=m_a_k_norm, b_q_norm=m_b_q_norm, b_k_norm=m_b_k_norm, b_sinks=m_b_sinks, mem_norm=m_mem_norm, w_mem_kv=m_w_mem_kv, m_q_norm=m_m_q_norm, m_k_norm=m_m_k_norm, w_o_a=m_w_o_a, w_o_b=m_w_o_b, w_o_m=m_w_o_m, w_gate=m_w_gate, b_gate=m_b_gate, w_out=m_w_out, ffn_norm=m_ffn_norm, w_up=m_w_up, conv_w=m_conv_w, conv_b=m_conv_b, w_down=m_w_down)
    mom2 = dict(attn_norm=v_attn_norm, w_in=v_w_in, a_q_norm=v_a_q_norm, a_k_norm=v_a_k_norm, b_q_norm=v_b_q_norm, b_k_norm=v_b_k_norm, b_sinks=v_b_sinks, mem_norm=v_mem_norm, w_mem_kv=v_w_mem_kv, m_q_norm=v_m_q_norm, m_k_norm=v_m_k_norm, w_o_a=v_w_o_a, w_o_b=v_w_o_b, w_o_m=v_w_o_m, w_gate=v_w_gate, b_gate=v_b_gate, w_out=v_w_out, ffn_norm=v_ffn_norm, w_up=v_w_up, conv_w=v_conv_w, conv_b=v_conv_b, w_down=v_w_down)

    big = list(BIG)
    stages = {'mix': list(MIX_WEIGHTS), 'ffn': list(FFN_WEIGHTS)}
    my_slot = _slot(_coords())

    def shard(n):
        return given[n][0] if n == 'conv_w' else given[n][0].astype(BF16)

    def whole(n, g):
        _, r, c = g.shape
        return g.reshape(N_DEV * r, c) if BIG[n] == 0 else g.transpose(1, 0, 2).reshape(r, N_DEV * c)

    def to_blocks(n, g):
        r, c = given[n].shape[1:]
        g = g.reshape(N_DEV, r, c) if BIG[n] == 0 else g.reshape(r, N_DEV, c).transpose(1, 0, 2)
        return g if n == 'conv_w' else g.astype(BF16)

    class Hooks:
        def __init__(self, token):
            self.coming, self.sent = {}, {}
            for stage, names in stages.items():
                src = [shard(n) if n == 'conv_w' else (given[n][0] + token).astype(BF16) for n in names]
                self.coming[stage] = _exchange_start(src, f"gather_{stage}_start", gather=True)

        def weights(self, stage, after):
            names = stages[stage]
            landed = _exchange_wait(self.coming[stage], after, f"gather_{stage}_wait", gather=True)
            return {n: whole(n, lax.dynamic_update_slice_in_dim(land, shard(n)[None], my_slot, axis=0))
                    for n, land in zip(names, landed)}

        def grads(self, stage, g):
            blocks = [to_blocks(n, g[n]) for n in stages[stage]]
            own = [lax.dynamic_slice_in_dim(b, my_slot, 1, axis=0) for b in blocks]
            self.sent[stage] = (_exchange_start(blocks, f"exchange_{stage}_start"), own)
            return self.sent[stage][0][-1]

        def parts(self, stage, after):
            started, own = self.sent[stage]
            landed = _exchange_wait(started, after, f"exchange_{stage}_wait")
            return {n: lax.dynamic_update_slice_in_dim(land, o, my_slot, axis=0)
                    for n, land, o in zip(stages[stage], landed, own)}

    w_in_all, token = _all_gather([shard('w_in')], "gather_w_in")
    hooks = Hooks(token[0, 0])
    w = {'w_in': whole('w_in', w_in_all)}
    for n in SMALL:
        w[n] = given[n]
    w['a_q_norm'], w['a_k_norm'] = given['a_q_norm'][0], given['a_k_norm'][0]
    w['b_q_norm'], w['b_k_norm'], w['b_sinks'] = given['b_q_norm'][0], given['b_k_norm'][0], given['b_sinks'][0]

    loss_tile, grad_x, grads = _device_step(x[0], mem[0], positions[0], loss_target[0], w, hooks)
    parts = {'w_in': _exchange([to_blocks('w_in', grads['w_in'])], "exchange_w_in")[0]}
    parts.update(hooks.parts('ffn', parts['w_in']))
    parts.update(hooks.parts('mix', parts['w_in']))

    out = {}
    for n in big:
        res = _adam_reduce(parts[n], given[n][0], mom1[n][0], mom2[n][0], f"adam_{n}")
        out[n] = tuple(t[None] for t in res)

    small = {n: grads[n] for n in PACK}
    small['b_q_norm'], small['b_k_norm'] = grads['b_q_norm'].reshape(1, -1), grads['b_k_norm'].reshape(1, -1)
    small['b_sinks'] = grads['b_sinks'].reshape(1, -1)
    gsum = _all_sum(_pack_small(small, loss_tile, "pack_small"), "sum_small")
    ws = {n: _as2d(n, given[n]) for n in PACK}
    ms = {n: _as2d(n, mom1[n]) for n in PACK}
    vs = {n: _as2d(n, mom2[n]) for n in PACK}
    res = _adam_small(gsum, ws, ms, vs, "adam_small")
    loss = res[0].reshape(())
    for k, n in enumerate(PACK):
        out[n] = tuple(t.reshape(given[n].shape) for t in res[1 + 4 * k:5 + 4 * k])

    outs = [loss, grad_x[None]]
    for field in range(4):
        outs += [out[n][field] for n in WEIGHTS]
    return tuple(outs)
```

```python
import functools
import math

import jax
import jax.numpy as jnp
from jax import lax
from jax.experimental import pallas as pl
from jax.experimental.pallas import tpu as pltpu

F32 = jnp.float32
BF16 = jnp.bfloat16

N_DEV = 8
D_MODEL = 1024
HEAD_DIM = 64
A_GROUPS = ((128, 1), (512, 4), (2048, 16))
B_WINDOW = 128
M_HEADS = 4
M_HEAD_DIM = 128
MEM_LEN = 256
D_FF = 2816
ROPE_THETA = 500000.0
ROPE_DIMS = 16
BLOCK = 128
EPS = 1e-6
LANES = 128
BAND_Q_BLOCKS = 4
BAND_UNITS = 2

ADAM_LR = 0.001
ADAM_B1 = 0.9
ADAM_B2 = 0.999
ADAM_EPS = 1e-08
ADAM_WD = 0.01
ADAM_STEP = 10

VMEM_LIMIT_BYTES = 56 * 1024 * 1024
MESH = pl.DeviceIdType.MESH

WEIGHTS = ['attn_norm', 'w_in', 'a_q_norm', 'a_k_norm', 'b_q_norm', 'b_k_norm', 'b_sinks', 'mem_norm',
           'w_mem_kv', 'm_q_norm', 'm_k_norm', 'w_o_a', 'w_o_b', 'w_o_m', 'w_gate', 'b_gate', 'w_out',
           'ffn_norm', 'w_up', 'conv_w', 'conv_b', 'w_down']
BIG = {'w_in': 1, 'w_mem_kv': 0, 'w_o_a': 1, 'w_o_b': 1, 'w_o_m': 1, 'w_gate': 1, 'w_out': 0, 'w_up': 1,
       'conv_w': 1, 'w_down': 0}
SMALL = [n for n in WEIGHTS if n not in BIG]


def _cparams(n_grid):
    return pltpu.CompilerParams(dimension_semantics=("arbitrary",) * n_grid, vmem_limit_bytes=VMEM_LIMIT_BYTES)


def _pick(n, cands=(512, 256, 128)):
    for c in cands:
        if n % c == 0:
            return c
    return n


def _seg_matrix(width):
    shift = width.bit_length() - 1
    r = lax.shift_right_logical(lax.broadcasted_iota(jnp.int32, (LANES, LANES), 0), shift)
    c = lax.shift_right_logical(lax.broadcasted_iota(jnp.int32, (LANES, LANES), 1), shift)
    return jnp.where(r == c, 1.0, 0.0).astype(BF16)


def _seg_sum(x, seg):
    hi = x.astype(BF16)
    r1 = x - hi.astype(F32)
    mid = r1.astype(BF16)
    lo = (r1 - mid.astype(F32)).astype(BF16)
    dot = functools.partial(jnp.dot, preferred_element_type=F32)
    return dot(hi, seg) + dot(mid, seg) + dot(lo, seg)


def _rope(y, c, s1, s2):
    return y * c + pltpu.roll(y, LANES - ROPE_DIMS // 2, 1) * s1 + pltpu.roll(y, ROPE_DIMS // 2, 1) * s2


def _unrope(dy, c, s1, s2):
    return dy * c + pltpu.roll(dy * s1, ROPE_DIMS // 2, 1) + pltpu.roll(dy * s2, LANES - ROPE_DIMS // 2, 1)


def _sigmoid(x):
    return 1.0 / (1.0 + jnp.exp(-x))


def _rms_fwd(x, gain, name):
    s_len, d = x.shape
    tm = 512

    def body(x_ref, g_ref, h_ref, r_ref):
        xv = x_ref[...]
        r = lax.rsqrt(jnp.mean(xv * xv, axis=-1, keepdims=True) + EPS)
        h_ref[...] = ((xv * r) * g_ref[...]).astype(BF16)
        r_ref[...] = r

    return pl.pallas_call(
        body, name=name, grid=(s_len // tm,),
        in_specs=[pl.BlockSpec((tm, d), lambda i: (i, 0)), pl.BlockSpec((1, d), lambda i: (0, 0))],
        out_specs=(pl.BlockSpec((tm, d), lambda i: (i, 0)), pl.BlockSpec((tm, 1), lambda i: (i, 0))),
        out_shape=(jax.ShapeDtypeStruct((s_len, d), BF16), jax.ShapeDtypeStruct((s_len, 1), F32)),
        compiler_params=_cparams(1),
    )(x, gain)


def _rms_bwd(dh, x, r, gain, add, name, bf16_copy=False):
    s_len, d = x.shape
    tm = 512

    def body(dh_ref, x_ref, r_ref, g_ref, add_ref, dx_ref, *rest):
        dg_ref = rest[-1]

        @pl.when(pl.program_id(0) == 0)
        def _():
            dg_ref[...] = jnp.zeros_like(dg_ref)
        rv = r_ref[...]
        xhat = x_ref[...] * rv
        dhv = dh_ref[...]
        dg_ref[...] += jnp.sum(dhv * xhat, axis=0, keepdims=True)
        dxhat = dhv * g_ref[...]
        dx = add_ref[...] + rv * (dxhat - xhat * jnp.mean(dxhat * xhat, axis=-1, keepdims=True))
        dx_ref[...] = dx
        if bf16_copy:
            rest[0][...] = dx.astype(BF16)

    row = pl.BlockSpec((tm, d), lambda i: (i, 0))
    vec = pl.BlockSpec((1, d), lambda i: (0, 0))
    out_specs = [row] + ([row] if bf16_copy else []) + [vec]
    out_shape = [jax.ShapeDtypeStruct((s_len, d), F32)] + ([jax.ShapeDtypeStruct((s_len, d), BF16)] if bf16_copy else [])
    out_shape.append(jax.ShapeDtypeStruct((1, d), F32))
    return pl.pallas_call(
        body, name=name, grid=(s_len // tm,),
        in_specs=[row, row, pl.BlockSpec((tm, 1), lambda i: (i, 0)), vec, row],
        out_specs=tuple(out_specs), out_shape=tuple(out_shape), compiler_params=_cparams(1),
    )(dh, x, r, gain, add)


def _resident(shape, index_map):
    return pl.BlockSpec(shape, index_map, pipeline_mode=pl.Buffered(1))


def _mm_rows(pairs, name, nt=False, tm=512, bias=None, sigmoid=False, res=None, out_dtypes=(F32,)):
    m = pairs[0][0].shape[0]
    n = pairs[0][1].shape[0] if nt else pairs[0][1].shape[1]
    n_pairs = len(pairs)
    has_bias, has_res = bias is not None, res is not None
    dims = (((1,), (1,)), ((), ())) if nt else (((1,), (0,)), ((), ()))

    def body(*refs):
        acc = None
        for p in range(n_pairs):
            t = lax.dot_general(refs[2 * p][...].astype(BF16), refs[2 * p + 1][...], dims, preferred_element_type=F32)
            acc = t if acc is None else acc + t
        pos = 2 * n_pairs
        if has_bias:
            acc = acc + refs[pos][...]
            pos += 1
        if sigmoid:
            acc = _sigmoid(acc)
        if has_res:
            acc = refs[pos][...] + acc
            pos += 1
        for o_ref in refs[pos:]:
            o_ref[...] = acc.astype(o_ref.dtype)

    in_specs, args = [], []
    for a, w, blk in pairs:
        k = a.shape[1]
        in_specs.append(pl.BlockSpec((tm, k), lambda i: (i, 0)))
        if nt:
            in_specs.append(_resident((n, k), lambda i, blk=blk: (0, blk)))
        else:
            in_specs.append(_resident((k, n), lambda i, blk=blk: (blk, 0)))
        args += [a, w]
    if has_bias:
        in_specs.append(_resident((1, n), lambda i: (0, 0)))
        args.append(bias)
    if has_res:
        in_specs.append(pl.BlockSpec((tm, n), lambda i: (i, 0)))
        args.append(res)
    out = pl.BlockSpec((tm, n), lambda i: (i, 0))
    outs = pl.pallas_call(
        body, name=name, grid=(m // tm,), in_specs=in_specs, out_specs=tuple([out] * len(out_dtypes)),
        out_shape=tuple(jax.ShapeDtypeStruct((m, n), dt) for dt in out_dtypes), compiler_params=_cparams(1),
    )(*args)
    return outs[0] if len(out_dtypes) == 1 else outs


def _mm_tn(a, b, name, tile=256):
    k, m = a.shape
    n = b.shape[1]
    dims = (((0,), (0,)), ((), ()))

    def body(a_ref, b_ref, o_ref):
        o_ref[...] = lax.dot_general(a_ref[...].astype(BF16), b_ref[...].astype(BF16), dims, preferred_element_type=F32)

    if n <= m:
        t = min(tile, m)
        grid, a_spec, b_spec = (m // t,), pl.BlockSpec((k, t), lambda i: (0, i)), _resident((k, n), lambda i: (0, 0))
        o_spec = pl.BlockSpec((t, n), lambda i: (i, 0))
    else:
        t = min(tile, n)
        grid, a_spec, b_spec = (n // t,), _resident((k, m), lambda i: (0, 0)), pl.BlockSpec((k, t), lambda i: (0, i))
        o_spec = pl.BlockSpec((m, t), lambda i: (0, i))
    return pl.pallas_call(
        body, name=name, grid=grid, in_specs=[a_spec, b_spec], out_specs=o_spec,
        out_shape=jax.ShapeDtypeStruct((m, n), F32), compiler_params=_cparams(1),
    )(a, b)


def _norm_rope(t, gain, c, s1, s2, seg):
    rs = lax.rsqrt(_seg_sum(t * t, seg) * (1.0 / HEAD_DIM) + EPS)
    return _rope((t * rs) * gain, c, s1, s2)


def _dup_half(y, half):
    lane = lax.broadcasted_iota(jnp.int32, y.shape, 1)
    rolled = pltpu.roll(y, HEAD_DIM, 1)
    keep = (lane < HEAD_DIM) if half == 0 else (lane >= HEAD_DIM)
    return jnp.where(keep, y, rolled)


def _qk_prep(proj, cb0, d, gqa, gq, gk, tabs, name):
    s_len = proj.shape[0]
    tm = 512
    rows = tm // d
    n_units = 4 if gqa else 2 * d
    n_q = 4 if gqa else 2
    n_in = 6

    def body(*refs):
        in_refs = refs[:n_in]
        gq_ref, gk_ref, c_ref, s1_ref, s2_ref, o_ref = refs[n_in:]
        seg = _seg_matrix(HEAD_DIM)

        def rows_of(ref, r):
            return ref[...] if d == 1 else ref[pl.ds(r, rows, stride=d), :]

        def put(unit_col, y):
            o_ref[:, unit_col * LANES:(unit_col + 1) * LANES] = y.astype(BF16)

        for r in range(d):
            c, s1, s2 = rows_of(c_ref, r), rows_of(s1_ref, r), rows_of(s2_ref, r)
            for b in range(n_in):
                t = rows_of(in_refs[b], r)
                if b < n_q:
                    put((b * d + r) if not gqa else b, _norm_rope(t, gq_ref[...], c, s1, s2, seg))
                elif not gqa:
                    sec, pair = (1, b - 2) if b < 4 else (2, b - 4)
                    y = _norm_rope(t, gk_ref[...], c, s1, s2, seg) if sec == 1 else t
                    put(sec * n_units + pair * d + r, y)
                else:
                    sec = 1 if b == 4 else 2
                    y = _norm_rope(t, gk_ref[...], c, s1, s2, seg) if sec == 1 else t
                    for u in range(n_units):
                        put(sec * n_units + u, _dup_half(y, u // 2))

    in_specs = [pl.BlockSpec((tm, LANES), lambda i, b=b: (i, cb0 + b)) for b in range(n_in)]
    vec = pl.BlockSpec((1, LANES), lambda i: (0, 0))
    tab = pl.BlockSpec((tm, LANES), lambda i: (i, 0))
    width = 3 * n_units * LANES
    return pl.pallas_call(
        body, name=name, grid=(s_len // tm,), in_specs=in_specs + [vec, vec, tab, tab, tab],
        out_specs=pl.BlockSpec((rows, width), lambda i: (i, 0)),
        out_shape=jax.ShapeDtypeStruct((s_len // d, width), BF16), compiler_params=_cparams(1),
    )(*([proj] * n_in), gq, gk, *tabs)


def _qk_prep_bwd(dqkv, proj, cb0, d, gqa, gq, gk, tabs, name):
    s_len = proj.shape[0]
    tm = 512
    rows = tm // d
    n_units = 4 if gqa else 2 * d
    n_q = 4 if gqa else 2
    n_in = 6

    def body(*refs):
        d_refs = refs[0:3]
        in_refs = refs[3:3 + n_in]
        gq_ref, gk_ref, c_ref, s1_ref, s2_ref, o_ref, dgq_ref, dgk_ref, stage = refs[3 + n_in:]
        seg = _seg_matrix(HEAD_DIM)

        @pl.when(pl.program_id(0) == 0)
        def _():
            dgq_ref[...] = jnp.zeros_like(dgq_ref)
            dgk_ref[...] = jnp.zeros_like(dgk_ref)

        def rows_of(ref, r):
            return ref[...] if d == 1 else ref[pl.ds(r, rows, stride=d), :]

        def unit(col):
            sec, u = divmod(col, n_units)
            return d_refs[sec][:, u * LANES:(u + 1) * LANES]

        def norm_bwd(dyr, t, gain, c, s1, s2, dg_ref):
            rs = lax.rsqrt(_seg_sum(t * t, seg) * (1.0 / HEAD_DIM) + EPS)
            that = t * rs
            dy = _unrope(dyr, c, s1, s2)
            dg_ref[...] += jnp.sum(dy * that, axis=0, keepdims=True)
            dthat = dy * gain
            return rs * (dthat - that * (_seg_sum(dthat * that, seg) * (1.0 / HEAD_DIM)))

        def fold(sec):
            tot = []
            for u in range(n_units):
                v = unit(sec * n_units + u)
                tot.append(v + pltpu.roll(v, HEAD_DIM, 1))
            lane = lax.broadcasted_iota(jnp.int32, tot[0].shape, 1)
            return jnp.where(lane < HEAD_DIM, tot[0] + tot[1], tot[2] + tot[3])

        for b in range(n_in):
            for r in range(d):
                c, s1, s2 = rows_of(c_ref, r), rows_of(s1_ref, r), rows_of(s2_ref, r)
                t = rows_of(in_refs[b], r)
                if b < n_q:
                    g = unit((b * d + r) if not gqa else b)
                    out = norm_bwd(g, t, gq_ref[...], c, s1, s2, dgq_ref)
                elif not gqa:
                    sec, pair = (1, b - 2) if b < 4 else (2, b - 4)
                    g = unit(sec * n_units + pair * d + r)
                    out = norm_bwd(g, t, gk_ref[...], c, s1, s2, dgk_ref) if sec == 1 else g
                else:
                    sec = 1 if b == 4 else 2
                    g = fold(sec)
                    out = norm_bwd(g, t, gk_ref[...], c, s1, s2, dgk_ref) if sec == 1 else g
                if d == 1:
                    o_ref[:, b * LANES:(b + 1) * LANES] = out.astype(BF16)
                else:
                    stage[pl.ds(r, rows, stride=d), :] = out
            if d != 1:
                o_ref[:, b * LANES:(b + 1) * LANES] = stage[...].astype(BF16)

    in_specs = [pl.BlockSpec((rows, n_units * LANES), lambda i: (i, 0))] * 3
    in_specs += [pl.BlockSpec((tm, LANES), lambda i, b=b: (i, cb0 + b)) for b in range(n_in)]
    vec = pl.BlockSpec((1, LANES), lambda i: (0, 0))
    tab = pl.BlockSpec((tm, LANES), lambda i: (i, 0))
    return pl.pallas_call(
        body, name=name, grid=(s_len // tm,), in_specs=in_specs + [vec, vec, tab, tab, tab],
        out_specs=(pl.BlockSpec((tm, n_in * LANES), lambda i: (i, 0)), vec, vec),
        out_shape=(jax.ShapeDtypeStruct((s_len, n_in * LANES), BF16), jax.ShapeDtypeStruct((1, LANES), F32),
                   jax.ShapeDtypeStruct((1, LANES), F32)),
        scratch_shapes=[pltpu.VMEM((tm, LANES), F32)], compiler_params=_cparams(1),
    )(*dqkv, *([proj] * n_in), gq, gk, *tabs)


def _head_masks(shape):
    lane = lax.broadcasted_iota(jnp.int32, shape, 1)
    return lane < HEAD_DIM, lane >= HEAD_DIM


def _band_fwd(qkv, n_units, max_dist, sinks, name):
    n_rows = qkv.shape[0]
    nb = n_rows // BLOCK
    scale = HEAD_DIM ** -0.5
    has_sink = sinks is not None

    qn, un = min(nb, BAND_Q_BLOCKS), BAND_UNITS
    ug = n_units // un

    def body(*refs):
        q_ref, kp_ref, km_ref, vp_ref, vm_ref = refs[:5]
        o_ref, lse_ref = refs[-2:]
        i = pl.program_id(1)
        qi = lax.broadcasted_iota(jnp.int32, (BLOCK, 2 * BLOCK), 0)
        kj = lax.broadcasted_iota(jnp.int32, (BLOCK, 2 * BLOCK), 1)
        dist = qi + BLOCK - kj
        band = (dist >= 0) & (dist <= max_dist)
        band_first = band & ((i > 0) | (kj >= BLOCK))
        m0, m1 = _head_masks((BLOCK, LANES))
        zero = jnp.zeros((BLOCK, LANES), BF16)
        for ub in range(un):
            cs = slice(ub * LANES, (ub + 1) * LANES)
            for qb in range(qn):
                rs = slice(qb * BLOCK, (qb + 1) * BLOCK)
                q = q_ref[rs, cs]
                if qb == 0:
                    kk = jnp.concatenate([kp_ref[:, cs], km_ref[0:BLOCK, cs]], axis=0)
                    vv = jnp.concatenate([vp_ref[:, cs], vm_ref[0:BLOCK, cs]], axis=0)
                    valid = band_first
                else:
                    kk = km_ref[(qb - 1) * BLOCK:(qb + 1) * BLOCK, cs]
                    vv = vm_ref[(qb - 1) * BLOCK:(qb + 1) * BLOCK, cs]
                    valid = band
                outs, lses = [], []
                for e, hm in enumerate((m0, m1)):
                    qe = jnp.where(hm, q, zero)
                    s = lax.dot_general(qe, kk, (((1,), (1,)), ((), ())), preferred_element_type=F32) * scale
                    s = jnp.where(valid, s, -jnp.inf)
                    mx = jnp.max(s, axis=-1, keepdims=True)
                    if has_sink:
                        sk = refs[5][ub][:, e * HEAD_DIM:e * HEAD_DIM + 1]
                        mx = jnp.maximum(mx, sk)
                    p = jnp.exp(s - mx)
                    den = jnp.sum(p, axis=-1, keepdims=True)
                    if has_sink:
                        den = den + jnp.exp(sk - mx)
                    pn = (p * (1.0 / den)).astype(BF16)
                    outs.append(jnp.dot(pn, vv, preferred_element_type=F32))
                    lses.append(mx + jnp.log(den))
                o_ref[rs, cs] = jnp.where(m0, outs[0], outs[1])
                lse_ref[rs, cs] = jnp.where(m0, jnp.broadcast_to(lses[0], (BLOCK, LANES)),
                                            jnp.broadcast_to(lses[1], (BLOCK, LANES)))

    def main(sec):
        return pl.BlockSpec((qn * BLOCK, un * LANES), lambda u, i: (i, sec * ug + u))

    def prev(sec):
        return pl.BlockSpec((BLOCK, un * LANES), lambda u, i: (jnp.maximum(i * qn - 1, 0), sec * ug + u))

    in_specs = [main(0), prev(1), main(1), prev(2), main(2)]
    args = [qkv] * 5
    if has_sink:
        in_specs.append(pl.BlockSpec((un, 1, LANES), lambda u, i: (u, 0, 0)))
        args.append(sinks)
    return pl.pallas_call(
        body, name=name, grid=(ug, nb // qn), in_specs=in_specs, out_specs=(main(0), main(0)),
        out_shape=(jax.ShapeDtypeStruct((n_rows, n_units * LANES), F32),) * 2, compiler_params=_cparams(2),
    )(*args)


def _band_bwd(qkv, do, lse, delta, n_units, max_dist, name):
    n_rows = qkv.shape[0]
    nb = n_rows // BLOCK
    scale = HEAD_DIM ** -0.5

    qn, un = min(nb, BAND_Q_BLOCKS), BAND_UNITS
    ug = n_units // un
    steps = nb // qn
    nt_dims = (((1,), (1,)), ((), ()))
    tn_dims = (((0,), (0,)), ((), ()))

    def body(qm_ref, qx_ref, kp_ref, km_ref, vp_ref, vm_ref, dom_ref, dox_ref, lm_ref, lx_ref, dm_ref, dx_ref,
             dq_ref, dk_ref, dv_ref):
        i = pl.program_id(1)
        m0, m1 = _head_masks((BLOCK, LANES))
        zero = jnp.zeros((BLOCK, LANES), BF16)
        qi = lax.broadcasted_iota(jnp.int32, (BLOCK, 2 * BLOCK), 0)
        kj = lax.broadcasted_iota(jnp.int32, (BLOCK, 2 * BLOCK), 1)
        dist = qi + BLOCK - kj
        band = (dist >= 0) & (dist <= max_dist)
        band_first = band & ((i > 0) | (kj >= BLOCK))
        qr = lax.broadcasted_iota(jnp.int32, (2 * BLOCK, BLOCK), 0)
        kc = lax.broadcasted_iota(jnp.int32, (2 * BLOCK, BLOCK), 1)
        dist2 = qr - kc
        band2 = (dist2 >= 0) & (dist2 <= max_dist)
        band2_last = band2 & ((qr < BLOCK) | (i < steps - 1))
        m0w, m1w = _head_masks((2 * BLOCK, LANES))
        zero2 = jnp.zeros((2 * BLOCK, LANES), BF16)

        def two(main_ref, next_ref, kb, cs):
            if kb < qn - 1:
                return main_ref[kb * BLOCK:(kb + 2) * BLOCK, cs]
            return jnp.concatenate([main_ref[kb * BLOCK:(kb + 1) * BLOCK, cs], next_ref[:, cs]], axis=0)

        for ub in range(un):
            cs = slice(ub * LANES, (ub + 1) * LANES)
            for qb in range(qn):
                rs = slice(qb * BLOCK, (qb + 1) * BLOCK)
                q = qm_ref[rs, cs]
                dob = dom_ref[rs, cs]
                lse_b = lm_ref[rs, cs]
                del_b = dm_ref[rs, cs]
                if qb == 0:
                    kk = jnp.concatenate([kp_ref[:, cs], km_ref[0:BLOCK, cs]], axis=0)
                    vv = jnp.concatenate([vp_ref[:, cs], vm_ref[0:BLOCK, cs]], axis=0)
                    valid = band_first
                else:
                    kk = km_ref[(qb - 1) * BLOCK:(qb + 1) * BLOCK, cs]
                    vv = vm_ref[(qb - 1) * BLOCK:(qb + 1) * BLOCK, cs]
                    valid = band
                dqs = []
                for e, hm in enumerate((m0, m1)):
                    col = slice(e * HEAD_DIM, e * HEAD_DIM + 1)
                    s = lax.dot_general(jnp.where(hm, q, zero), kk, nt_dims, preferred_element_type=F32) * scale
                    p = jnp.where(valid, jnp.exp(s - lse_b[:, col]), 0.0)
                    dp = lax.dot_general(jnp.where(hm, dob, zero), vv, nt_dims, preferred_element_type=F32)
                    ds = (p * (dp - del_b[:, col]) * scale).astype(BF16)
                    dqs.append(jnp.dot(ds, kk, preferred_element_type=F32))
                dq_ref[rs, cs] = jnp.where(m0, dqs[0], dqs[1])
            for kb in range(qn):
                rs = slice(kb * BLOCK, (kb + 1) * BLOCK)
                qq = two(qm_ref, qx_ref, kb, cs)
                dd = two(dom_ref, dox_ref, kb, cs)
                ll = two(lm_ref, lx_ref, kb, cs)
                de = two(dm_ref, dx_ref, kb, cs)
                k = km_ref[rs, cs]
                v = vm_ref[rs, cs]
                valid2 = band2 if kb < qn - 1 else band2_last
                dk = jnp.zeros((BLOCK, LANES), F32)
                dv = jnp.zeros((BLOCK, LANES), F32)
                for e, hm in enumerate((m0w, m1w)):
                    col = slice(e * HEAD_DIM, e * HEAD_DIM + 1)
                    qe = jnp.where(hm, qq, zero2)
                    doe = jnp.where(hm, dd, zero2)
                    s = lax.dot_general(qe, k, nt_dims, preferred_element_type=F32) * scale
                    p = jnp.where(valid2, jnp.exp(s - ll[:, col]), 0.0)
                    dp = lax.dot_general(doe, v, nt_dims, preferred_element_type=F32)
                    ds = (p * (dp - de[:, col]) * scale).astype(BF16)
                    dk = dk + lax.dot_general(ds, qe, tn_dims, preferred_element_type=F32)
                    dv = dv + lax.dot_general(p.astype(BF16), doe, tn_dims, preferred_element_type=F32)
                dk_ref[rs, cs] = dk
                dv_ref[rs, cs] = dv

    def main(sec):
        return pl.BlockSpec((qn * BLOCK, un * LANES), lambda u, i: (i, sec * ug + u))

    def prev(sec):
        return pl.BlockSpec((BLOCK, un * LANES), lambda u, i: (jnp.maximum(i * qn - 1, 0), sec * ug + u))

    def nxt(sec):
        return pl.BlockSpec((BLOCK, un * LANES), lambda u, i: (jnp.minimum((i + 1) * qn, nb - 1), sec * ug + u))

    in_specs = [main(0), nxt(0), prev(1), main(1), prev(2), main(2),
                main(0), nxt(0), main(0), nxt(0), main(0), nxt(0)]
    args = [qkv] * 6 + [do, do, lse, lse, delta, delta]
    shp = jax.ShapeDtypeStruct((n_rows, n_units * LANES), F32)
    return pl.pallas_call(
        body, name=name, grid=(ug, steps), in_specs=in_specs, out_specs=(main(0), main(0), main(0)),
        out_shape=(shp, shp, shp), compiler_params=_cparams(2),
    )(*args)


def _merge_groups(os_, lses, dils, name):
    s_len = os_[0].shape[0] * dils[0]
    tm = 512

    def body(*refs):
        o_refs, l_refs = refs[0:3], refs[3:6]
        o_ref, lse_ref = refs[6:8]
        so, sl = refs[8:11], refs[11:14]
        for pair in range(2):
            for g, d in enumerate(dils):
                rows = tm // d
                for r in range(d):
                    col = slice((pair * d + r) * LANES, (pair * d + r + 1) * LANES)
                    if d == 1:
                        so[g][...] = o_refs[g][:, col]
                        sl[g][...] = l_refs[g][:, col]
                    else:
                        so[g][pl.ds(r, rows, stride=d), :] = o_refs[g][:, col]
                        sl[g][pl.ds(r, rows, stride=d), :] = l_refs[g][:, col]
            l0, l1, l2 = sl[0][...], sl[1][...], sl[2][...]
            mx = jnp.maximum(jnp.maximum(l0, l1), l2)
            e0, e1, e2 = jnp.exp(l0 - mx), jnp.exp(l1 - mx), jnp.exp(l2 - mx)
            den = e0 + e1 + e2
            inv = 1.0 / den
            o_ref[:, pair * LANES:(pair + 1) * LANES] = (so[0][...] * (e0 * inv) + so[1][...] * (e1 * inv)
                                                         + so[2][...] * (e2 * inv))
            lse_ref[:, pair * LANES:(pair + 1) * LANES] = mx + jnp.log(den)

    in_specs = [pl.BlockSpec((tm // d, 2 * d * LANES), lambda i: (i, 0)) for d in dils] * 2
    out = pl.BlockSpec((tm, 2 * LANES), lambda i: (i, 0))
    shp = jax.ShapeDtypeStruct((s_len, 2 * LANES), F32)
    return pl.pallas_call(
        body, name=name, grid=(s_len // tm,), in_specs=in_specs, out_specs=(out, out), out_shape=(shp, shp),
        scratch_shapes=[pltpu.VMEM((tm, LANES), F32)] * 6, compiler_params=_cparams(1),
    )(*os_, *lses)


def _bwd_prep(do, o, lse, dils, sinks, name):
    s_len, width = do.shape
    n_pairs = width // LANES
    tm = 512
    has_sink = sinks is not None
    n_g = len(dils)

    def body(*refs):
        do_ref, o_ref, lse_ref = refs[:3]
        pos = 3
        if has_sink:
            sink_ref = refs[pos]
            pos += 1
        outs = refs[pos:pos + 3 * n_g]
        pos += 3 * n_g
        if has_sink:
            dsink_ref = refs[pos]
            pos += 1
        s_do, s_l, s_d = refs[pos:pos + 3]
        seg = _seg_matrix(HEAD_DIM)

        if has_sink:
            @pl.when(pl.program_id(0) == 0)
            def _():
                dsink_ref[...] = jnp.zeros_like(dsink_ref)

        for pair in range(n_pairs):
            col = slice(pair * LANES, (pair + 1) * LANES)
            dov = do_ref[:, col]
            lv = lse_ref[:, col]
            delta = _seg_sum(dov * o_ref[:, col], seg)
            if has_sink:
                dsink_ref[pair] += -jnp.sum(jnp.exp(sink_ref[pair] - lv) * delta, axis=0, keepdims=True)
            s_do[...] = dov
            s_l[...] = lv
            s_d[...] = delta
            for g, d in enumerate(dils):
                rows = tm // d
                for r in range(d):
                    oc = slice((pair * d + r) * LANES, (pair * d + r + 1) * LANES)
                    if d == 1:
                        a, b, c = s_do[...], s_l[...], s_d[...]
                    else:
                        a = s_do[pl.ds(r, rows, stride=d), :]
                        b = s_l[pl.ds(r, rows, stride=d), :]
                        c = s_d[pl.ds(r, rows, stride=d), :]
                    outs[3 * g][:, oc] = a.astype(BF16)
                    outs[3 * g + 1][:, oc] = b
                    outs[3 * g + 2][:, oc] = c

    row = pl.BlockSpec((tm, width), lambda i: (i, 0))
    in_specs = [row, row, row]
    args = [do, o, lse]
    if has_sink:
        in_specs.append(pl.BlockSpec((n_pairs, 1, LANES), lambda i: (0, 0, 0)))
        args.append(sinks)
    out_specs, out_shape = [], []
    for d in dils:
        for dt in (BF16, F32, F32):
            out_specs.append(pl.BlockSpec((tm // d, n_pairs * d * LANES), lambda i: (i, 0)))
            out_shape.append(jax.ShapeDtypeStruct((s_len // d, n_pairs * d * LANES), dt))
    if has_sink:
        out_specs.append(pl.BlockSpec((n_pairs, 1, LANES), lambda i: (0, 0, 0)))
        out_shape.append(jax.ShapeDtypeStruct((n_pairs, 1, LANES), F32))
    return pl.pallas_call(
        body, name=name, grid=(s_len // tm,), in_specs=in_specs, out_specs=tuple(out_specs),
        out_shape=tuple(out_shape), scratch_shapes=[pltpu.VMEM((tm, LANES), F32)] * 3, compiler_params=_cparams(1),
    )(*args)


def _mem_kv(mem, mem_gain, w_kv, k_gain, name):
    m_len = mem.shape[0]
    kw = M_HEADS * M_HEAD_DIM

    def body(mem_ref, mg_ref, w_ref, kg_ref, k_ref, v_ref):
        mv = mem_ref[...]
        r = lax.rsqrt(jnp.mean(mv * mv, axis=-1, keepdims=True) + EPS)
        mn = ((mv * r) * mg_ref[...]).astype(BF16)
        kv = jnp.dot(mn, w_ref[...], preferred_element_type=F32)
        for h in range(M_HEADS):
            col = slice(h * M_HEAD_DIM, (h + 1) * M_HEAD_DIM)
            t = kv[:, col]
            rk = lax.rsqrt(jnp.mean(t * t, axis=-1, keepdims=True) + EPS)
            k_ref[:, col] = ((t * rk) * kg_ref[...]).astype(BF16)
        v_ref[...] = kv[:, kw:].astype(BF16)

    shp = jax.ShapeDtypeStruct((m_len, kw), BF16)
    return pl.pallas_call(body, name=name, out_shape=(shp, shp),
                          compiler_params=pltpu.CompilerParams(vmem_limit_bytes=VMEM_LIMIT_BYTES))(mem, mem_gain, w_kv, k_gain)


def _mem_kv_bwd(mem, mem_gain, w_kv, k_gain, dk, dv, name):
    m_len, d = mem.shape
    kw = M_HEADS * M_HEAD_DIM

    def body(mem_ref, mg_ref, w_ref, kg_ref, dk_ref, dv_ref, dw_ref, dmg_ref, dkg_ref, dkv_ref):
        mv = mem_ref[...]
        r = lax.rsqrt(jnp.mean(mv * mv, axis=-1, keepdims=True) + EPS)
        mhat = mv * r
        mn = (mhat * mg_ref[...]).astype(BF16)
        kv = jnp.dot(mn, w_ref[...], preferred_element_type=F32)
        dkg = jnp.zeros((1, M_HEAD_DIM), F32)
        for h in range(M_HEADS):
            col = slice(h * M_HEAD_DIM, (h + 1) * M_HEAD_DIM)
            t = kv[:, col]
            rk = lax.rsqrt(jnp.mean(t * t, axis=-1, keepdims=True) + EPS)
            that = t * rk
            dy = dk_ref[:, col]
            dkg = dkg + jnp.sum(dy * that, axis=0, keepdims=True)
            dthat = dy * kg_ref[...]
            dkv_ref[:, col] = (rk * (dthat - that * jnp.mean(dthat * that, axis=-1, keepdims=True))).astype(BF16)
        dkv_ref[:, kw:] = dv_ref[...].astype(BF16)
        dkg_ref[...] = dkg
        dkv = dkv_ref[...]
        dw_ref[...] = lax.dot_general(mn, dkv, (((0,), (0,)), ((), ())), preferred_element_type=F32)
        dmn = lax.dot_general(dkv, w_ref[...], (((1,), (1,)), ((), ())), preferred_element_type=F32)
        dmg_ref[...] = jnp.sum(dmn * mhat, axis=0, keepdims=True)

    return pl.pallas_call(
        body, name=name,
        out_shape=(jax.ShapeDtypeStruct((d, 2 * kw), F32), jax.ShapeDtypeStruct((1, d), F32),
                   jax.ShapeDtypeStruct((1, M_HEAD_DIM), F32)),
        scratch_shapes=[pltpu.VMEM((m_len, 2 * kw), BF16)],
        compiler_params=pltpu.CompilerParams(vmem_limit_bytes=VMEM_LIMIT_BYTES),
    )(mem, mem_gain, w_kv, k_gain, dk, dv)


def _mem_attn_fwd(proj, cidx, mk, mv, q_gain, name):
    s_len = proj.shape[0]
    kw = M_HEADS * M_HEAD_DIM
    tm = 512
    scale = M_HEAD_DIM ** -0.5

    def body(q_ref, k_ref, v_ref, g_ref, o_ref):
        for h in range(M_HEADS):
            col = slice(h * M_HEAD_DIM, (h + 1) * M_HEAD_DIM)
            t = q_ref[:, col]
            rs = lax.rsqrt(jnp.mean(t * t, axis=-1, keepdims=True) + EPS)
            qn = ((t * rs) * g_ref[...]).astype(BF16)
            s = lax.dot_general(qn, k_ref[:, col], (((1,), (1,)), ((), ())), preferred_element_type=F32) * scale
            mx = jnp.max(s, axis=-1, keepdims=True)
            p = jnp.exp(s - mx)
            pn = (p * (1.0 / jnp.sum(p, axis=-1, keepdims=True))).astype(BF16)
            o_ref[:, col] = jnp.dot(pn, v_ref[:, col], preferred_element_type=F32).astype(BF16)

    whole = pl.BlockSpec((MEM_LEN, kw), lambda i: (0, 0))
    return pl.pallas_call(
        body, name=name, grid=(s_len // tm,),
        in_specs=[pl.BlockSpec((tm, kw), lambda i: (i, cidx)), whole, whole, pl.BlockSpec((1, M_HEAD_DIM), lambda i: (0, 0))],
        out_specs=pl.BlockSpec((tm, kw), lambda i: (i, 0)),
        out_shape=jax.ShapeDtypeStruct((s_len, kw), BF16), compiler_params=_cparams(1),
    )(proj, mk, mv, q_gain)


def _mem_attn_bwd(proj, cidx, mk, mv, q_gain, do, name):
    s_len = proj.shape[0]
    kw = M_HEADS * M_HEAD_DIM
    tm = 512
    scale = M_HEAD_DIM ** -0.5

    def body(q_ref, k_ref, v_ref, g_ref, do_ref, dq_ref, dk_ref, dv_ref, dg_ref):
        @pl.when(pl.program_id(0) == 0)
        def _():
            dk_ref[...] = jnp.zeros_like(dk_ref)
            dv_ref[...] = jnp.zeros_like(dv_ref)
            dg_ref[...] = jnp.zeros_like(dg_ref)

        for h in range(M_HEADS):
            col = slice(h * M_HEAD_DIM, (h + 1) * M_HEAD_DIM)
            t = q_ref[:, col]
            rs = lax.rsqrt(jnp.mean(t * t, axis=-1, keepdims=True) + EPS)
            that = t * rs
            qn = (that * g_ref[...]).astype(BF16)
            kh, vh = k_ref[:, col], v_ref[:, col]
            dob = do_ref[:, col].astype(BF16)
            s = lax.dot_general(qn, kh, (((1,), (1,)), ((), ())), preferred_element_type=F32) * scale
            mx = jnp.max(s, axis=-1, keepdims=True)
            p = jnp.exp(s - mx)
            p = p * (1.0 / jnp.sum(p, axis=-1, keepdims=True))
            dp = lax.dot_general(dob, vh, (((1,), (1,)), ((), ())), preferred_element_type=F32)
            ds = (p * (dp - jnp.sum(p * dp, axis=-1, keepdims=True)) * scale).astype(BF16)
            dqn = jnp.dot(ds, kh, preferred_element_type=F32)
            dk_ref[:, col] += lax.dot_general(ds, qn, (((0,), (0,)), ((), ())), preferred_element_type=F32)
            dv_ref[:, col] += lax.dot_general(p.astype(BF16), dob, (((0,), (0,)), ((), ())), preferred_element_type=F32)
            dg_ref[...] += jnp.sum(dqn * that, axis=0, keepdims=True)
            dthat = dqn * g_ref[...]
            dq_ref[:, col] = (rs * (dthat - that * jnp.mean(dthat * that, axis=-1, keepdims=True))).astype(BF16)

    whole = pl.BlockSpec((MEM_LEN, kw), lambda i: (0, 0))
    vec = pl.BlockSpec((1, M_HEAD_DIM), lambda i: (0, 0))
    row = pl.BlockSpec((tm, kw), lambda i: (i, 0))
    return pl.pallas_call(
        body, name=name, grid=(s_len // tm,),
        in_specs=[pl.BlockSpec((tm, kw), lambda i: (i, cidx)), whole, whole, vec, row],
        out_specs=(row, whole, whole, vec),
        out_shape=(jax.ShapeDtypeStruct((s_len, kw), BF16), jax.ShapeDtypeStruct((MEM_LEN, kw), F32),
                   jax.ShapeDtypeStruct((MEM_LEN, kw), F32), jax.ShapeDtypeStruct((1, M_HEAD_DIM), F32)),
        compiler_params=_cparams(1),
    )(proj, mk, mv, q_gain, do)


def _gate_merge(gates, pa, pb, pm, name):
    s_len, d = pa.shape
    tm = 256

    def body(g_ref, a_ref, b_ref, m_ref, o_ref):
        f = lambda v: v.astype(F32)
        o_ref[...] = (f(g_ref[:, 0:d]) * f(a_ref[...]) + f(g_ref[:, d:2 * d]) * f(b_ref[...])
                      + f(g_ref[:, 2 * d:3 * d]) * f(m_ref[...])).astype(BF16)

    row = pl.BlockSpec((tm, d), lambda i: (i, 0))
    return pl.pallas_call(
        body, name=name, grid=(s_len // tm,), in_specs=[pl.BlockSpec((tm, 3 * d), lambda i: (i, 0)), row, row, row],
        out_specs=row, out_shape=jax.ShapeDtypeStruct((s_len, d), BF16), compiler_params=_cparams(1),
    )(gates, pa, pb, pm)


def _gate_merge_bwd(dmerged, gates, pa, pb, pm, name):
    s_len, d = pa.shape
    tm = 256

    def body(dm_ref, g_ref, a_ref, b_ref, m_ref, da_ref, db_ref, dmm_ref, dg_ref, dbg_ref):
        @pl.when(pl.program_id(0) == 0)
        def _():
            dbg_ref[...] = jnp.zeros_like(dbg_ref)
        dm = dm_ref[...]
        for k, (p_ref, dp_ref) in enumerate(((a_ref, da_ref), (b_ref, db_ref), (m_ref, dmm_ref))):
            col = slice(k * d, (k + 1) * d)
            g = g_ref[:, col].astype(F32)
            dp_ref[...] = (dm * g).astype(BF16)
            dpre = (dm * p_ref[...].astype(F32)) * (g * (1.0 - g))
            dbg_ref[:, col] += jnp.sum(dpre, axis=0, keepdims=True)
            dg_ref[:, col] = dpre.astype(BF16)

    row = pl.BlockSpec((tm, d), lambda i: (i, 0))
    wide = pl.BlockSpec((tm, 3 * d), lambda i: (i, 0))
    shp = jax.ShapeDtypeStruct((s_len, d), BF16)
    return pl.pallas_call(
        body, name=name, grid=(s_len // tm,), in_specs=[row, wide, row, row, row],
        out_specs=(row, row, row, wide, pl.BlockSpec((1, 3 * d), lambda i: (0, 0))),
        out_shape=(shp, shp, shp, jax.ShapeDtypeStruct((s_len, 3 * d), BF16), jax.ShapeDtypeStruct((1, 3 * d), F32)),
        compiler_params=_cparams(1),
    )(dmerged, gates, pa, pb, pm)


CONV_CHUNK = 256


def _pick_row(tile, j):
    row = lax.broadcasted_iota(jnp.int32, tile.shape, 0)
    return jnp.sum(jnp.where(row == j, tile, jnp.zeros_like(tile)), axis=0, keepdims=True)


def _rows_before(ref, start, k):
    cur = ref[pl.ds(start, CONV_CHUNK), :].astype(F32)
    prev = ref[pl.ds(pl.multiple_of(jnp.maximum(start - 16, 0), 16), 16), :].astype(F32)
    prev = jnp.where(start > 0, prev, jnp.zeros_like(prev))
    rolled = pltpu.roll(cur, k, 0)
    row = lax.broadcasted_iota(jnp.int32, cur.shape, 0)
    for j in range(k):
        rolled = jnp.where(row == j, _pick_row(prev, 16 - k + j), rolled)
    return rolled


def _rows_after(ref, start, k):
    cur = ref[pl.ds(start, CONV_CHUNK), :]
    nxt = ref[pl.ds(pl.multiple_of(start + CONV_CHUNK, 8), 8), :]
    rolled = pltpu.roll(cur, CONV_CHUNK - k, 0)
    row = lax.broadcasted_iota(jnp.int32, cur.shape, 0)
    for j in range(k):
        rolled = jnp.where(row == CONV_CHUNK - k + j, _pick_row(nxt, j), rolled)
    return rolled


def _conv_pre(u_ref, w_ref, b_ref, start):
    u2 = _rows_before(u_ref, start, 2)
    u1 = _rows_before(u_ref, start, 1)
    u0 = u_ref[pl.ds(start, CONV_CHUNK), :].astype(F32)
    c = ((b_ref[...] + w_ref[0:1, :] * u2) + w_ref[1:2, :] * u1) + w_ref[2:3, :] * u0
    return c, (u2, u1, u0)


def _conv_glu(u, conv_w, conv_b, name):
    s_len = u.shape[0]
    nblk = D_FF // LANES

    def body(ua_ref, ug_ref, wa_ref, wg_ref, ba_ref, bg_ref, o_ref):
        def chunk(ci, carry):
            start = pl.multiple_of(ci * CONV_CHUNK, CONV_CHUNK)
            ca, _ = _conv_pre(ua_ref, wa_ref, ba_ref, start)
            cg, _ = _conv_pre(ug_ref, wg_ref, bg_ref, start)
            o_ref[pl.ds(start, CONV_CHUNK), :] = ((ca * _sigmoid(ca)) * cg).astype(BF16)
            return carry
        lax.fori_loop(0, s_len // CONV_CHUNK, chunk, 0)

    def col(rows, off):
        return pl.BlockSpec((rows, LANES), lambda j: (0, off + j))

    return pl.pallas_call(
        body, name=name, grid=(nblk,),
        in_specs=[col(s_len, 0), col(s_len, nblk), col(3, 0), col(3, nblk), col(1, 0), col(1, nblk)],
        out_specs=col(s_len, 0), out_shape=jax.ShapeDtypeStruct((s_len, D_FF), BF16), compiler_params=_cparams(1),
    )(u, u, conv_w, conv_w, conv_b, conv_b)


def _conv_glu_bwd(dact, u, conv_w, conv_b, name):
    s_len = u.shape[0]
    nblk = D_FF // LANES
    n_chunks = s_len // CONV_CHUNK

    def body(da_ref, ua_ref, ug_ref, wa_ref, wg_ref, ba_ref, bg_ref,
             dua_ref, dug_ref, dwa_ref, dwg_ref, dba_ref, dbg_ref, sa, sg):
        sa[pl.ds(s_len, 8), :] = jnp.zeros((8, LANES), F32)
        sg[pl.ds(s_len, 8), :] = jnp.zeros((8, LANES), F32)
        zero = jnp.zeros((1, LANES), F32)

        def chunk1(ci, carry):
            start = pl.multiple_of(ci * CONV_CHUNK, CONV_CHUNK)
            ca, ua = _conv_pre(ua_ref, wa_ref, ba_ref, start)
            cg, ug = _conv_pre(ug_ref, wg_ref, bg_ref, start)
            dact_v = da_ref[pl.ds(start, CONV_CHUNK), :].astype(F32)
            sig = _sigmoid(ca)
            dcg = dact_v * (ca * sig)
            dca = (dact_v * cg) * (sig * (1.0 + ca * (1.0 - sig)))
            sa[pl.ds(start, CONV_CHUNK), :] = dca
            sg[pl.ds(start, CONV_CHUNK), :] = dcg
            out = [carry[0] + jnp.sum(dca, axis=0, keepdims=True), carry[1] + jnp.sum(dcg, axis=0, keepdims=True)]
            for j in range(3):
                out.append(carry[2 + j] + jnp.sum(dca * ua[j], axis=0, keepdims=True))
            for j in range(3):
                out.append(carry[5 + j] + jnp.sum(dcg * ug[j], axis=0, keepdims=True))
            return tuple(out)

        acc = lax.fori_loop(0, n_chunks, chunk1, (zero,) * 8)
        dba_ref[...] = acc[0]
        dbg_ref[...] = acc[1]
        for j in range(3):
            dwa_ref[j:j + 1, :] = acc[2 + j]
            dwg_ref[j:j + 1, :] = acc[5 + j]

        def chunk2(ci, carry):
            start = pl.multiple_of(ci * CONV_CHUNK, CONV_CHUNK)
            for s_ref, w_ref, o_ref in ((sa, wa_ref, dua_ref), (sg, wg_ref, dug_ref)):
                d0 = s_ref[pl.ds(start, CONV_CHUNK), :]
                d1 = _rows_after(s_ref, start, 1)
                d2 = _rows_after(s_ref, start, 2)
                o_ref[pl.ds(start, CONV_CHUNK), :] = (w_ref[2:3, :] * d0 + w_ref[1:2, :] * d1
                                                      + w_ref[0:1, :] * d2).astype(BF16)
            return carry
        lax.fori_loop(0, n_chunks, chunk2, 0)

    def col(rows, off):
        return pl.BlockSpec((rows, LANES), lambda j: (0, off + j))

    big = jax.ShapeDtypeStruct((s_len, D_FF), BF16)
    return pl.pallas_call(
        body, name=name, grid=(nblk,),
        in_specs=[col(s_len, 0), col(s_len, 0), col(s_len, nblk), col(3, 0), col(3, nblk), col(1, 0), col(1, nblk)],
        out_specs=(col(s_len, 0), col(s_len, 0), col(3, 0), col(3, 0), col(1, 0), col(1, 0)),
        out_shape=(big, big, jax.ShapeDtypeStruct((3, D_FF), F32), jax.ShapeDtypeStruct((3, D_FF), F32),
                   jax.ShapeDtypeStruct((1, D_FF), F32), jax.ShapeDtypeStruct((1, D_FF), F32)),
        scratch_shapes=[pltpu.VMEM((s_len + 8, LANES), F32)] * 2, compiler_params=_cparams(1),
    )(dact, u, u, conv_w, conv_w, conv_b, conv_b)


def _loss_head(y, target, name):
    s_len, d = y.shape
    tm = 512

    def body(y_ref, t_ref, dy_ref, dyb_ref, l_ref):
        @pl.when(pl.program_id(0) == 0)
        def _():
            l_ref[...] = jnp.zeros_like(l_ref)
        err = y_ref[...] - t_ref[...]
        dy = err * (1.0 / d)
        dy_ref[...] = dy
        dyb_ref[...] = dy.astype(BF16)
        part = 0.5 * jnp.sum(jnp.mean(err * err, axis=-1, keepdims=True), axis=0, keepdims=True)
        l_ref[...] += jnp.broadcast_to(part, l_ref.shape)

    row = pl.BlockSpec((tm, d), lambda i: (i, 0))
    return pl.pallas_call(
        body, name=name, grid=(s_len // tm,), in_specs=[row, row],
        out_specs=(row, row, pl.BlockSpec((8, LANES), lambda i: (0, 0))),
        out_shape=(jax.ShapeDtypeStruct((s_len, d), F32), jax.ShapeDtypeStruct((s_len, d), BF16),
                   jax.ShapeDtypeStruct((8, LANES), F32)),
        compiler_params=_cparams(1),
    )(y, target)


def _rope_tables(positions):
    half = ROPE_DIMS // 2
    freqs = jnp.exp(jnp.arange(half, dtype=F32) * (-2.0 * math.log(ROPE_THETA) / ROPE_DIMS))
    ang = positions.reshape(-1).astype(F32)[:, None] * freqs
    cos, sin = jnp.cos(ang), jnp.sin(ang)
    n = ang.shape[0]
    zeros = lambda w: jnp.zeros((n, w), F32)
    c = jnp.concatenate([cos, cos, jnp.ones((n, HEAD_DIM - ROPE_DIMS), F32)], axis=1)
    s1 = jnp.concatenate([-sin, zeros(HEAD_DIM - half)], axis=1)
    s2 = jnp.concatenate([zeros(half), sin, zeros(HEAD_DIM - ROPE_DIMS)], axis=1)
    return tuple(jnp.tile(t, (1, 2)) for t in (c, s1, s2))


def _two(v):
    return jnp.tile(v.reshape(1, HEAD_DIM), (1, 2))


def _fold_heads(g):
    return g[0, :HEAD_DIM] + g[0, HEAD_DIM:]


MIX_WEIGHTS = ('w_gate', 'w_mem_kv', 'w_o_a', 'w_o_b', 'w_o_m', 'w_out')
FFN_WEIGHTS = ('w_up', 'conv_w', 'w_down')


def _device_step(x, mem, positions, target, w, hooks=None):
    tabs = _rope_tables(positions)
    dils = tuple(d for _, d in A_GROUPS)
    grads = {}
    w = dict(w)

    h, r1 = _rms_fwd(x, w['attn_norm'], "rms1")
    proj = _mm_rows([(h, w['w_in'], 0)], "mm_in")

    qkv_a, o_g, lse_g = [], [], []
    for gi, (window, d) in enumerate(A_GROUPS):
        gq, gk = _two(w['a_q_norm'][gi]), _two(w['a_k_norm'][gi])
        qkv = _qk_prep(proj, 6 * gi, d, False, gq, gk, tabs, f"qk_prep_a{gi}")
        o, lse = _band_fwd(qkv, 2 * d, window // d, None, f"band_fwd_a{gi}")
        qkv_a.append(qkv)
        o_g.append(o)
        lse_g.append(lse)
    o_a, lse_a = _merge_groups(o_g, lse_g, dils, "merge_a")

    gbq, gbk = _two(w['b_q_norm']), _two(w['b_k_norm'])
    sinks = jnp.repeat(w['b_sinks'].reshape(4, 2), HEAD_DIM, axis=1).reshape(4, 1, LANES)
    qkv_b = _qk_prep(proj, 18, 1, True, gbq, gbk, tabs, "qk_prep_b")
    o_b, lse_b = _band_fwd(qkv_b, 4, B_WINDOW - 1, sinks, "band_fwd_b")

    if hooks is not None:
        w.update(hooks.weights('mix', o_b))
    gates = _mm_rows([(h, w['w_gate'], 0)], "mm_gate", bias=w['b_gate'], sigmoid=True, out_dtypes=(BF16,))
    mk, mv = _mem_kv(mem, w['mem_norm'], w['w_mem_kv'], w['m_k_norm'], "mem_kv")
    o_m = _mem_attn_fwd(proj, 6, mk, mv, w['m_q_norm'], "mem_attn")

    pa = _mm_rows([(o_a, w['w_o_a'], 0)], "mm_oa", out_dtypes=(BF16,))
    pb = _mm_rows([(o_b, w['w_o_b'], 0)], "mm_ob", out_dtypes=(BF16,))
    pm = _mm_rows([(o_m, w['w_o_m'], 0)], "mm_om", out_dtypes=(BF16,))
    merged = _gate_merge(gates, pa, pb, pm, "gate_merge")
    x1 = _mm_rows([(merged, w['w_out'], 0)], "mm_out", res=x)

    if hooks is not None:
        w.update(hooks.weights('ffn', x1))
    h2, r2 = _rms_fwd(x1, w['ffn_norm'], "rms2")
    u = _mm_rows([(h2, w['w_up'], 0)], "mm_up", out_dtypes=(BF16,))
    act = _conv_glu(u, w['conv_w'], w['conv_b'], "conv_glu")
    y = _mm_rows([(act, w['w_down'], 0)], "mm_down", res=x1)
    dy, dy_b, loss = _loss_head(y, target, "loss_head")

    dact = _mm_rows([(dy_b, w['w_down'], 0)], "mm_d_act", nt=True, out_dtypes=(BF16,))
    grads['w_down'] = _mm_tn(act, dy_b, "mm_dw_down")
    du_a, du_g, dcw_a, dcw_g, dcb_a, dcb_g = _conv_glu_bwd(dact, u, w['conv_w'], w['conv_b'], "conv_glu_bwd")
    grads['conv_w'] = jnp.concatenate([dcw_a, dcw_g], axis=1)
    grads['conv_b'] = jnp.concatenate([dcb_a, dcb_g], axis=1)
    dh2 = _mm_rows([(du_a, w['w_up'], 0), (du_g, w['w_up'], 1)], "mm_d_h2", nt=True)
    grads['w_up'] = jnp.concatenate([_mm_tn(h2, du_a, "mm_dw_up_a"), _mm_tn(h2, du_g, "mm_dw_up_g")], axis=1)
    ffn_gain = w['ffn_norm']
    if hooks is not None:
        ffn_gain = ffn_gain + hooks.grads('ffn', grads)[0:1, 0:1]
    dx1, dx1_b, grads['ffn_norm'] = _rms_bwd(dh2, x1, r2, ffn_gain, dy, "rms2_bwd", bf16_copy=True)

    dmerged = _mm_rows([(dx1_b, w['w_out'], 0)], "mm_d_merged", nt=True)
    grads['w_out'] = _mm_tn(merged, dx1_b, "mm_dw_out")
    dpa, dpb, dpm, dgpre, grads['b_gate'] = _gate_merge_bwd(dmerged, gates, pa, pb, pm, "gate_merge_bwd")
    do_a = _mm_rows([(dpa, w['w_o_a'], 0)], "mm_d_oa", nt=True)
    do_b = _mm_rows([(dpb, w['w_o_b'], 0)], "mm_d_ob", nt=True)
    do_m = _mm_rows([(dpm, w['w_o_m'], 0)], "mm_d_om", nt=True)
    grads['w_o_a'] = _mm_tn(o_a, dpa, "mm_dw_oa")
    grads['w_o_b'] = _mm_tn(o_b, dpb, "mm_dw_ob")
    grads['w_o_m'] = _mm_tn(o_m, dpm, "mm_dw_om")
    grads['w_gate'] = _mm_tn(h, dgpre, "mm_dw_gate")
    dq_m, dmk, dmv, grads['m_q_norm'] = _mem_attn_bwd(proj, 6, mk, mv, w['m_q_norm'], do_m, "mem_attn_bwd")
    grads['w_mem_kv'], grads['mem_norm'], grads['m_k_norm'] = _mem_kv_bwd(
        mem, w['mem_norm'], w['w_mem_kv'], w['m_k_norm'], dmk, dmv, "mem_kv_bwd")
    a_gain = w['a_q_norm']
    if hooks is not None:
        a_gain = a_gain + hooks.grads('mix', grads)[0:1, 0:1]

    prep = _bwd_prep(do_a, o_a, lse_a, dils, None, "bwd_prep_a")
    dproj, dgq_a, dgk_a = [], [], []
    for gi, (window, d) in enumerate(A_GROUPS):
        gq, gk = _two(a_gain[gi]), _two(w['a_k_norm'][gi])
        dqkv = _band_bwd(qkv_a[gi], prep[3 * gi], prep[3 * gi + 1], prep[3 * gi + 2], 2 * d, window // d,
                         f"band_bwd_a{gi}")
        dp, dgq, dgk = _qk_prep_bwd(dqkv, proj, 6 * gi, d, False, gq, gk, tabs, f"qk_prep_bwd_a{gi}")
        dproj.append(dp)
        dgq_a.append(_fold_heads(dgq))
        dgk_a.append(_fold_heads(dgk))
    grads['a_q_norm'] = jnp.stack(dgq_a)
    grads['a_k_norm'] = jnp.stack(dgk_a)

    do_bu, lse_bu, delta_bu, dsink = _bwd_prep(do_b, o_b, lse_b, (1,), sinks, "bwd_prep_b")
    dqkv = _band_bwd(qkv_b, do_bu, lse_bu, delta_bu, 4, B_WINDOW - 1, "band_bwd_b")
    dp_b, dgq, dgk = _qk_prep_bwd(dqkv, proj, 18, 1, True, gbq, gbk, tabs, "qk_prep_bwd_b")
    dproj.append(dp_b)
    grads['b_q_norm'] = _fold_heads(dgq)
    grads['b_k_norm'] = _fold_heads(dgk)
    grads['b_sinks'] = jnp.stack([dsink[:, 0, 0], dsink[:, 0, HEAD_DIM]], axis=1).reshape(8)

    dproj.append(dq_m)

    cols = (0, 1, 2, 3, 6)
    grads['w_in'] = jnp.concatenate([_mm_tn(h, dp, f"mm_dw_in{k}") for k, dp in enumerate(dproj)], axis=1)
    dh = _mm_rows([(dp, w['w_in'], c) for dp, c in zip(dproj, cols)] + [(dgpre, w['w_gate'], 0)], "mm_d_h", nt=True)
    grad_x, grads['attn_norm'] = _rms_bwd(dh, x, r1, w['attn_norm'], dx1, "rms1_bwd")
    return loss, grad_x, grads


def _coords():
    return lax.axis_index("x"), lax.axis_index("y"), lax.axis_index("c")


def _slot(p):
    return 4 * p[0] + 2 * p[1] + p[2]


ALL_PEERS = tuple(range(1, N_DEV))
CHIP_PEERS = (1, 4, 2, 6)
OTHER_CHIPS = (4, 2, 6)


def _peers(me, masks=ALL_PEERS):
    x, y, c = me
    return [(1 - x if mask & 4 else x, 1 - y if mask & 2 else y, 1 - c if mask & 1 else c) for mask in masks]


HBM_SPEC = pl.BlockSpec(memory_space=pltpu.HBM)


def _all_gather(shards, name):
    n = len(shards)

    def body(*refs):
        ins, outs = refs[:n], refs[n:2 * n]
        token, send_sems, recv_sems, local_sems = refs[2 * n:]
        token[...] = jnp.zeros_like(token)
        x, y, c = _coords()
        me, sibling = (x, y, c), (x, y, 1 - c)
        chips = [(1 - x, y), (x, 1 - y), (1 - x, 1 - y)]

        def copy(a, k, block, to, src=None):
            dst = outs[a].at[_slot(block)]
            return pltpu.make_async_remote_copy(
                src_ref=dst if src is None else src, dst_ref=dst, send_sem=send_sems.at[a, k],
                recv_sem=recv_sems.at[a, k], device_id=to, device_id_type=MESH)

        mine = [pltpu.make_async_copy(ins[a], outs[a].at[_slot(me)], local_sems.at[a]) for a in range(n)]
        for cp in mine:
            cp.start()
        first = []
        for a in range(n):
            first.append(copy(a, 0, me, sibling, src=ins[a]))
            first += [copy(a, 1 + j, me, (*chip, c), src=ins[a]) for j, chip in enumerate(chips)]
        for cp in first:
            cp.start()
        passed = []
        for a in range(n):
            for j, chip in enumerate(chips):
                copy(a, 1 + j, (*chip, c), me).wait_recv()
                fwd = copy(a, 4 + j, (*chip, c), sibling)
                fwd.start()
                passed.append(fwd)
        for a in range(n):
            copy(a, 0, sibling, me).wait_recv()
            for j, chip in enumerate(chips):
                copy(a, 4 + j, (*chip, 1 - c), me).wait_recv()
        for cp in first + passed:
            cp.wait_send()
        for cp in mine:
            cp.wait()

    return pl.pallas_call(
        body, name=name, in_specs=[HBM_SPEC] * n,
        out_specs=tuple([HBM_SPEC] * n + [pl.BlockSpec(memory_space=pltpu.VMEM)]),
        out_shape=tuple([jax.ShapeDtypeStruct((N_DEV,) + s.shape, s.dtype) for s in shards]
                        + [jax.ShapeDtypeStruct((8, LANES), F32)]),
        scratch_shapes=[pltpu.SemaphoreType.DMA((n, 7)), pltpu.SemaphoreType.DMA((n, 7)), pltpu.SemaphoreType.DMA((n,))],
    )(*shards)


def _exchange(blocks, name):
    n = len(blocks)

    def body(*refs):
        ins, outs = refs[:n], refs[n:2 * n]
        send_sems, recv_sems, local_sems = refs[2 * n:]
        me = _coords()
        peers = _peers(me)
        mine = [pltpu.make_async_copy(ins[a].at[_slot(me)], outs[a].at[_slot(me)], local_sems.at[a]) for a in range(n)]
        for cp in mine:
            cp.start()

        def copy(a, k):
            return pltpu.make_async_remote_copy(
                src_ref=ins[a].at[_slot(peers[k])], dst_ref=outs[a].at[_slot(me)], send_sem=send_sems.at[a, k],
                recv_sem=recv_sems.at[a, k], device_id=peers[k], device_id_type=MESH)

        def arrival(a, k):
            return pltpu.make_async_remote_copy(
                src_ref=ins[a].at[_slot(me)], dst_ref=outs[a].at[_slot(peers[k])], send_sem=send_sems.at[a, k],
                recv_sem=recv_sems.at[a, k], device_id=peers[k], device_id_type=MESH)

        sends = [copy(a, k) for a in range(n) for k in range(N_DEV - 1)]
        for cp in sends:
            cp.start()
        for a in range(n):
            for k in range(N_DEV - 1):
                arrival(a, k).wait_recv()
        for cp in sends:
            cp.wait_send()
        for cp in mine:
            cp.wait()

    return pl.pallas_call(
        body, name=name, in_specs=[HBM_SPEC] * n, out_specs=tuple([HBM_SPEC] * n),
        out_shape=tuple(jax.ShapeDtypeStruct(b.shape, b.dtype) for b in blocks),
        scratch_shapes=[pltpu.SemaphoreType.DMA((n, 7)), pltpu.SemaphoreType.DMA((n, 7)), pltpu.SemaphoreType.DMA((n,))],
    )(*blocks)


SEM_SPEC = pl.BlockSpec(memory_space=pltpu.SEMAPHORE)
SIDE_EFFECT = pltpu.SideEffectType.DATAFLOW_SIDE_EFFECTING


def _exchange_start(blocks, name, gather=False, masks=ALL_PEERS):
    n = len(blocks)
    n_peers = len(masks)

    def body(*refs):
        ins, lands = refs[:n], refs[n:2 * n]
        send_sems, recv_sems = refs[2 * n], refs[2 * n + 1]
        token = refs[-1]
        me = _coords()
        peers = _peers(me, masks)
        for a in range(n):
            for k in range(n_peers):
                pltpu.make_async_remote_copy(
                    src_ref=ins[a] if gather else ins[a].at[_slot(peers[k])], dst_ref=lands[a].at[_slot(me)],
                    send_sem=send_sems.at[a * n_peers + k], recv_sem=recv_sems.at[a * n_peers + k],
                    device_id=peers[k], device_id_type=MESH).start()
        token[...] = jnp.zeros_like(token)

    land_shapes = [((N_DEV,) + b.shape) if gather else b.shape for b in blocks]
    hbm_in = [pltpu.HBM(b.shape, b.dtype) for b in blocks]
    hbm_land = [pltpu.HBM(s, b.dtype) for s, b in zip(land_shapes, blocks)]
    sems = pltpu.SemaphoreType.DMA((n * n_peers,))
    ins = [pltpu.with_memory_space_constraint(b, pltpu.HBM) for b in blocks]
    lands = [pltpu.with_memory_space_constraint(lax.empty(s, b.dtype), pltpu.HBM) for s, b in zip(land_shapes, blocks)]
    return pl.pallas_call(
        body, name=name, out_shape=(sems, sems, *hbm_in, *hbm_land, jax.ShapeDtypeStruct((8, LANES), F32)),
        in_specs=[HBM_SPEC] * (2 * n),
        out_specs=(SEM_SPEC, SEM_SPEC, *([HBM_SPEC] * (2 * n)), pl.BlockSpec(memory_space=pltpu.VMEM)),
        input_output_aliases={i: 2 + i for i in range(2 * n)},
        compiler_params=pltpu.CompilerParams(has_side_effects=SIDE_EFFECT),
    )(*ins, *lands)


def _exchange_wait(started, after, name, gather=False, masks=ALL_PEERS):
    n = (len(started) - 3) // 2
    n_peers = len(masks)
    send_sems, recv_sems = started[0], started[1]
    thru = started[2:2 + 2 * n]

    def body(*refs):
        ins, lands = refs[:n], refs[n:2 * n]
        send_ref, recv_ref = refs[2 * n], refs[2 * n + 1]
        me = _coords()
        peers = _peers(me, masks)
        for a in range(n):
            for k in range(n_peers):
                cp = pltpu.make_async_remote_copy(
                    src_ref=ins[a] if gather else ins[a].at[_slot(peers[k])], dst_ref=lands[a].at[_slot(peers[k])],
                    send_sem=send_ref.at[a * n_peers + k], recv_sem=recv_ref.at[a * n_peers + k],
                    device_id=peers[k], device_id_type=MESH)
                cp.wait_send()
                cp.wait_recv()

    hbm = [pltpu.HBM(t.shape, t.dtype) for t in thru]
    res = pl.pallas_call(
        body, name=name, out_shape=tuple(hbm),
        in_specs=[HBM_SPEC] * (2 * n) + [SEM_SPEC, SEM_SPEC, pl.BlockSpec(memory_space=pl.ANY)],
        out_specs=tuple([HBM_SPEC] * (2 * n)), input_output_aliases={i: i for i in range(2 * n)},
        compiler_params=pltpu.CompilerParams(has_side_effects=SIDE_EFFECT),
    )(*thru, send_sems, recv_sems, after)
    return res[n:]


def _sibling_forward(arrays, name):
    n = len(arrays)
    n_fwd = len(OTHER_CHIPS)

    def body(*refs):
        bufs = refs[n:2 * n]
        send_sems, recv_sems = refs[2 * n:]
        x, y, c = _coords()
        sibling = (x, y, 1 - c)
        mine = _peers((x, y, c), OTHER_CHIPS)
        theirs = _peers(sibling, OTHER_CHIPS)

        def copy(a, k, block):
            rows = bufs[a].at[_slot(block)]
            return pltpu.make_async_remote_copy(
                src_ref=rows, dst_ref=rows, send_sem=send_sems.at[a * n_fwd + k], recv_sem=recv_sems.at[a * n_fwd + k],
                device_id=sibling, device_id_type=MESH)

        sends = [copy(a, k, mine[k]) for a in range(n) for k in range(n_fwd)]
        for cp in sends:
            cp.start()
        for a in range(n):
            for k in range(n_fwd):
                copy(a, k, theirs[k]).wait_recv()
        for cp in sends:
            cp.wait_send()

    return pl.pallas_call(
        body, name=name, in_specs=[HBM_SPEC] * n, out_specs=tuple([HBM_SPEC] * n),
        out_shape=tuple(jax.ShapeDtypeStruct(a.shape, a.dtype) for a in arrays),
        input_output_aliases={i: i for i in range(n)},
        scratch_shapes=[pltpu.SemaphoreType.DMA((n * n_fwd,)), pltpu.SemaphoreType.DMA((n * n_fwd,))],
    )(*arrays)


def _all_sum(p, name):
    def body(p_ref, o_ref, recv, send_sems, recv_sems):
        me = _coords()
        peers = _peers(me)
        recv[_slot(me)] = p_ref[...]

        def copy(k, landing):
            return pltpu.make_async_remote_copy(
                src_ref=p_ref, dst_ref=recv.at[_slot(landing)], send_sem=send_sems.at[k], recv_sem=recv_sems.at[k],
                device_id=peers[k], device_id_type=MESH)

        sends = [copy(k, me) for k in range(N_DEV - 1)]
        for cp in sends:
            cp.start()
        for k in range(N_DEV - 1):
            copy(k, peers[k]).wait_recv()
        for cp in sends:
            cp.wait_send()
        acc = recv[0]
        for s in range(1, N_DEV):
            acc = acc + recv[s]
        o_ref[...] = acc

    vmem = pl.BlockSpec(memory_space=pltpu.VMEM)
    return pl.pallas_call(
        body, name=name, in_specs=[vmem], out_specs=vmem, out_shape=jax.ShapeDtypeStruct(p.shape, F32),
        scratch_shapes=[pltpu.VMEM((N_DEV,) + p.shape, F32), pltpu.SemaphoreType.DMA((N_DEV - 1,)),
                        pltpu.SemaphoreType.DMA((N_DEV - 1,))],
    )(p)


def _adam(w, g, m, v):
    m2 = ADAM_B1 * m + (1.0 - ADAM_B1) * g
    v2 = ADAM_B2 * v + (1.0 - ADAM_B2) * (g * g)
    m_hat = m2 / (1.0 - ADAM_B1 ** ADAM_STEP)
    v_hat = v2 / (1.0 - ADAM_B2 ** ADAM_STEP)
    delta = -ADAM_LR * (m_hat / (jnp.sqrt(v_hat) + ADAM_EPS) + ADAM_WD * w)
    return delta, m2, v2


def _row_tile(rows, cols):
    best = rows
    for t in range(16, rows, 16):
        if rows % t == 0 and t * cols * 4 <= (1 << 20):
            best = t
    return best


def _adam_reduce(parts, w, m, v, name):
    rows, cols = w.shape
    tr = _row_tile(rows, cols)

    def body(p_ref, w_ref, m_ref, v_ref, g_ref, d_ref, m2_ref, v2_ref):
        g = p_ref[0].astype(F32)
        for s in range(1, N_DEV):
            g = g + p_ref[s].astype(F32)
        g_ref[...] = g
        d_ref[...], m2_ref[...], v2_ref[...] = _adam(w_ref[...], g, m_ref[...], v_ref[...])

    blk = pl.BlockSpec((tr, cols), lambda i: (i, 0))
    shp = jax.ShapeDtypeStruct((rows, cols), F32)
    return pl.pallas_call(
        body, name=name, grid=(rows // tr,),
        in_specs=[pl.BlockSpec((N_DEV, tr, cols), lambda i: (0, i, 0)), blk, blk, blk],
        out_specs=(blk,) * 4, out_shape=(shp,) * 4, compiler_params=_cparams(1),
    )(parts, w, m, v)


PACK_COLS = 1024
PACK = {'attn_norm': (0, 1, 1024), 'mem_norm': (1, 1, 1024), 'ffn_norm': (2, 1, 1024), 'b_gate': (3, 3, 1024),
        'conv_b': (6, 6, 1024), 'a_q_norm': (12, 3, 64), 'a_k_norm': (15, 3, 64), 'b_q_norm': (18, 1, 64),
        'b_k_norm': (19, 1, 64), 'm_q_norm': (20, 1, 128), 'm_k_norm': (21, 1, 128), 'b_sinks': (22, 1, 8)}
PACK_LOSS_ROW = 23
PACK_ROWS = 24


def _pack_pieces(name, width):
    r0, nr, lanes = PACK[name]
    out = []
    for j in range(nr):
        if lanes == PACK_COLS:
            w = min(PACK_COLS, width - j * PACK_COLS)
            out.append((r0 + j, slice(0, 1), slice(j * PACK_COLS, j * PACK_COLS + w), w))
        else:
            out.append((r0 + j, slice(j, j + 1), slice(0, lanes), lanes))
    return out


def _pack_small(grads, loss_tile, name):
    names = list(PACK)

    def body(*refs):
        o_ref = refs[-1]
        o_ref[...] = jnp.zeros_like(o_ref)
        for k, nm in enumerate(names):
            for row, rs, ls, w in _pack_pieces(nm, refs[k].shape[1]):
                o_ref[row:row + 1, 0:w] = refs[k][rs, ls]
        o_ref[PACK_LOSS_ROW:PACK_LOSS_ROW + 1, 0:1] = refs[len(names)][0:1, 0:1]

    vmem = pl.BlockSpec(memory_space=pltpu.VMEM)
    args = [grads[nm] for nm in names] + [loss_tile]
    return pl.pallas_call(body, name=name, in_specs=[vmem] * len(args), out_specs=vmem,
                          out_shape=jax.ShapeDtypeStruct((PACK_ROWS, PACK_COLS), F32))(*args)


def _adam_small(gsum, ws, ms, vs, name):
    names = list(PACK)
    n = len(names)

    def body(*refs):
        g_ref = refs[0]
        w_refs, m_refs, v_refs = refs[1:1 + n], refs[1 + n:1 + 2 * n], refs[1 + 2 * n:1 + 3 * n]
        outs = refs[1 + 3 * n:]
        outs[0][...] = g_ref[PACK_LOSS_ROW:PACK_LOSS_ROW + 1, 0:1]
        for k, nm in enumerate(names):
            o_g, o_d, o_m, o_v = outs[1 + 4 * k:5 + 4 * k]
            for row, rs, ls, width in _pack_pieces(nm, w_refs[k].shape[1]):
                src = (rs, ls)
                g = g_ref[row:row + 1, 0:width]
                d, m2, v2 = _adam(w_refs[k][src], g, m_refs[k][src], v_refs[k][src])
                o_g[src] = g
                o_d[src] = d
                o_m[src] = m2
                o_v[src] = v2

    vmem = pl.BlockSpec(memory_space=pltpu.VMEM)
    shapes = [jax.ShapeDtypeStruct((1, 1), F32)]
    for nm in names:
        shapes += [jax.ShapeDtypeStruct(ws[nm].shape, F32)] * 4
    args = [gsum] + [ws[nm] for nm in names] + [ms[nm] for nm in names] + [vs[nm] for nm in names]
    return pl.pallas_call(
        body, name=name, in_specs=[vmem] * len(args), out_specs=tuple([vmem] * len(shapes)), out_shape=tuple(shapes),
    )(*args)


def _as2d(name, a):
    return a.reshape(a.shape[-2], a.shape[-1]) if a.ndim == 3 else a


def kernel(x, mem, positions, attn_norm, w_in, a_q_norm, a_k_norm, b_q_norm, b_k_norm, b_sinks, mem_norm, w_mem_kv, m_q_norm, m_k_norm, w_o_a, w_o_b, w_o_m, w_gate, b_gate, w_out, ffn_norm, w_up, conv_w, conv_b, w_down, loss_target, m_attn_norm, m_w_in, m_a_q_norm, m_a_k_norm, m_b_q_norm, m_b_k_norm, m_b_sinks, m_mem_norm, m_w_mem_kv, m_m_q_norm, m_m_k_norm, m_w_o_a, m_w_o_b, m_w_o_m, m_w_gate, m_b_gate, m_w_out, m_ffn_norm, m_w_up, m_conv_w, m_conv_b, m_w_down, v_attn_norm, v_w_in, v_a_q_norm, v_a_k_norm, v_b_q_norm, v_b_k_norm, v_b_sinks, v_mem_norm, v_w_mem_kv, v_m_q_norm, v_m_k_norm, v_w_o_a, v_w_o_b, v_w_o_m, v_w_gate, v_b_gate, v_w_out, v_ffn_norm, v_w_up, v_conv_w, v_conv_b, v_w_down):
    given = dict(attn_norm=attn_norm, w_in=w_in, a_q_norm=a_q_norm, a_k_norm=a_k_norm, b_q_norm=b_q_norm, b_k_norm=b_k_norm, b_sinks=b_sinks, mem_norm=mem_norm, w_mem_kv=w_mem_kv, m_q_norm=m_q_norm, m_k_norm=m_k_norm, w_o_a=w_o_a, w_o_b=w_o_b, w_o_m=w_o_m, w_gate=w_gate, b_gate=b_gate, w_out=w_out, ffn_norm=ffn_norm, w_up=w_up, conv_w=conv_w, conv_b=conv_b, w_down=w_down)
    mom1 = dict(attn_norm=m_attn_norm, w_in=m_w_in, a_q_norm=m_a_q_norm, a_k_norm=m_a_k_norm, b_q_norm=m_b_q_norm, b_k_norm=m_b_k_norm, b_sinks=m_b_sinks, mem_norm=m_mem_norm, w_mem_kv=m_w_mem_kv, m_q_norm=m_m_q_norm, m_k_norm=m_m_k_norm, w_o_a=m_w_o_a, w_o_b=m_w_o_b, w_o_m=m_w_o_m, w_gate=m_w_gate, b_gate=m_b_gate, w_out=m_w_out, ffn_norm=m_ffn_norm, w_up=m_w_up, conv_w=m_conv_w, conv_b=m_conv_b, w_down=m_w_down)
    mom2 = dict(attn_norm=v_attn_norm, w_in=v_w_in, a_q_norm=v_a_q_norm, a_k_norm=v_a_k_norm, b_q_norm=v_b_q_norm, b_k_norm=v_b_k_norm, b_sinks=v_b_sinks, mem_norm=v_mem_norm, w_mem_kv=v_w_mem_kv, m_q_norm=v_m_q_norm, m_k_norm=v_m_k_norm, w_o_a=v_w_o_a, w_o_b=v_w_o_b, w_o_m=v_w_o_m, w_gate=v_w_gate, b_gate=v_b_gate, w_out=v_w_out, ffn_norm=v_ffn_norm, w_up=v_w_up, conv_w=v_conv_w, conv_b=v_conv_b, w_down=v_w_down)

    big = list(BIG)
    stages = {'mix': list(MIX_WEIGHTS), 'ffn': list(FFN_WEIGHTS)}
    my_slot = _slot(_coords())

    def shard(n):
        return given[n][0] if n == 'conv_w' else given[n][0].astype(BF16)

    def whole(n, g):
        _, r, c = g.shape
        return g.reshape(N_DEV * r, c) if BIG[n] == 0 else g.transpose(1, 0, 2).reshape(r, N_DEV * c)

    def to_blocks(n, g):
        r, c = given[n].shape[1:]
        g = g.reshape(N_DEV, r, c) if BIG[n] == 0 else g.reshape(r, N_DEV, c).transpose(1, 0, 2)
        return g if n == 'conv_w' else g.astype(BF16)

    class Hooks:
        def __init__(self, token):
            self.coming, self.sent = {}, {}
            for stage, names in stages.items():
                src = [shard(n) if n == 'conv_w' else (given[n][0] + token).astype(BF16) for n in names]
                self.coming[stage] = _exchange_start(src, f"gather_{stage}_start", gather=True, masks=CHIP_PEERS)

        def weights(self, stage, after):
            names = stages[stage]
            landed = _exchange_wait(self.coming[stage], after, f"gather_{stage}_wait", gather=True, masks=CHIP_PEERS)
            landed = _sibling_forward(landed, f"gather_{stage}_forward")
            return {n: whole(n, lax.dynamic_update_slice_in_dim(land, shard(n)[None], my_slot, axis=0))
                    for n, land in zip(names, landed)}

        def grads(self, stage, g):
            blocks = [to_blocks(n, g[n]) for n in stages[stage]]
            own = [lax.dynamic_slice_in_dim(b, my_slot, 1, axis=0) for b in blocks]
            self.sent[stage] = (_exchange_start(blocks, f"exchange_{stage}_start"), own)
            return self.sent[stage][0][-1]

        def parts(self, stage, after):
            started, own = self.sent[stage]
            landed = _exchange_wait(started, after, f"exchange_{stage}_wait")
            return {n: lax.dynamic_update_slice_in_dim(land, o, my_slot, axis=0)
                    for n, land, o in zip(stages[stage], landed, own)}

    w_in_all, token = _all_gather([shard('w_in')], "gather_w_in")
    hooks = Hooks(token[0, 0])
    w = {'w_in': whole('w_in', w_in_all)}
    for n in SMALL:
        w[n] = given[n]
    w['a_q_norm'], w['a_k_norm'] = given['a_q_norm'][0], given['a_k_norm'][0]
    w['b_q_norm'], w['b_k_norm'], w['b_sinks'] = given['b_q_norm'][0], given['b_k_norm'][0], given['b_sinks'][0]

    loss_tile, grad_x, grads = _device_step(x[0], mem[0], positions[0], loss_target[0], w, hooks)
    parts = {'w_in': _exchange([to_blocks('w_in', grads['w_in'])], "exchange_w_in")[0]}
    parts.update(hooks.parts('ffn', parts['w_in']))
    parts.update(hooks.parts('mix', parts['w_in']))

    out = {}
    for n in big:
        res = _adam_reduce(parts[n], given[n][0], mom1[n][0], mom2[n][0], f"adam_{n}")
        out[n] = tuple(t[None] for t in res)

    small = {n: grads[n] for n in PACK}
    small['b_q_norm'], small['b_k_norm'] = grads['b_q_norm'].reshape(1, -1), grads['b_k_norm'].reshape(1, -1)
    small['b_sinks'] = grads['b_sinks'].reshape(1, -1)
    gsum = _all_sum(_pack_small(small, loss_tile, "pack_small"), "sum_small")
    ws = {n: _as2d(n, given[n]) for n in PACK}
    ms = {n: _as2d(n, mom1[n]) for n in PACK}
    vs = {n: _as2d(n, mom2[n]) for n in PACK}
    res = _adam_small(gsum, ws, ms, vs, "adam_small")
    loss = res[0].reshape(())
    for k, n in enumerate(PACK):
        out[n] = tuple(t.reshape(given[n].shape) for t in res[1 + 4 * k:5 + 4 * k])

    outs = [loss, grad_x[None]]
    for field in range(4):
        outs += [out[n][field] for n in WEIGHTS]
    return tuple(outs)
```

```python
import functools
import math

import jax
import jax.numpy as jnp
from jax import lax
from jax.experimental import pallas as pl
from jax.experimental.pallas import tpu as pltpu

F32 = jnp.float32
BF16 = jnp.bfloat16

N_DEV = 8
D_MODEL = 1024
HEAD_DIM = 64
A_GROUPS = ((128, 1), (512, 4), (2048, 16))
B_WINDOW = 128
M_HEADS = 4
M_HEAD_DIM = 128
MEM_LEN = 256
D_FF = 2816
ROPE_THETA = 500000.0
ROPE_DIMS = 16
BLOCK = 128
EPS = 1e-6
LANES = 128
BAND_Q_BLOCKS = 4
BAND_UNITS = 2

ADAM_LR = 0.001
ADAM_B1 = 0.9
ADAM_B2 = 0.999
ADAM_EPS = 1e-08
ADAM_WD = 0.01
ADAM_STEP = 10

VMEM_LIMIT_BYTES = 56 * 1024 * 1024
MESH = pl.DeviceIdType.MESH

WEIGHTS = ['attn_norm', 'w_in', 'a_q_norm', 'a_k_norm', 'b_q_norm', 'b_k_norm', 'b_sinks', 'mem_norm',
           'w_mem_kv', 'm_q_norm', 'm_k_norm', 'w_o_a', 'w_o_b', 'w_o_m', 'w_gate', 'b_gate', 'w_out',
           'ffn_norm', 'w_up', 'conv_w', 'conv_b', 'w_down']
BIG = {'w_in': 1, 'w_mem_kv': 0, 'w_o_a': 1, 'w_o_b': 1, 'w_o_m': 1, 'w_gate': 1, 'w_out': 0, 'w_up': 1,
       'conv_w': 1, 'w_down': 0}
SMALL = [n for n in WEIGHTS if n not in BIG]


def _cparams(n_grid):
    return pltpu.CompilerParams(dimension_semantics=("arbitrary",) * n_grid, vmem_limit_bytes=VMEM_LIMIT_BYTES)


def _seg_matrix(width):
    shift = width.bit_length() - 1
    r = lax.shift_right_logical(lax.broadcasted_iota(jnp.int32, (LANES, LANES), 0), shift)
    c = lax.shift_right_logical(lax.broadcasted_iota(jnp.int32, (LANES, LANES), 1), shift)
    return jnp.where(r == c, 1.0, 0.0).astype(BF16)


def _seg_sum(x, seg):
    hi = x.astype(BF16)
    r1 = x - hi.astype(F32)
    mid = r1.astype(BF16)
    lo = (r1 - mid.astype(F32)).astype(BF16)
    dot = functools.partial(jnp.dot, preferred_element_type=F32)
    return dot(hi, seg) + dot(mid, seg) + dot(lo, seg)


def _rope(y, c, s1, s2):
    return y * c + pltpu.roll(y, LANES - ROPE_DIMS // 2, 1) * s1 + pltpu.roll(y, ROPE_DIMS // 2, 1) * s2


def _unrope(dy, c, s1, s2):
    return dy * c + pltpu.roll(dy * s1, ROPE_DIMS // 2, 1) + pltpu.roll(dy * s2, LANES - ROPE_DIMS // 2, 1)


def _sigmoid(x):
    return 1.0 / (1.0 + jnp.exp(-x))


def _rms_fwd(x, gain, name):
    s_len, d = x.shape
    tm = 512

    def body(x_ref, g_ref, h_ref, r_ref):
        xv = x_ref[...]
        r = lax.rsqrt(jnp.mean(xv * xv, axis=-1, keepdims=True) + EPS)
        h_ref[...] = ((xv * r) * g_ref[...]).astype(BF16)
        r_ref[...] = r

    return pl.pallas_call(
        body, name=name, grid=(s_len // tm,),
        in_specs=[pl.BlockSpec((tm, d), lambda i: (i, 0)), pl.BlockSpec((1, d), lambda i: (0, 0))],
        out_specs=(pl.BlockSpec((tm, d), lambda i: (i, 0)), pl.BlockSpec((tm, 1), lambda i: (i, 0))),
        out_shape=(jax.ShapeDtypeStruct((s_len, d), BF16), jax.ShapeDtypeStruct((s_len, 1), F32)),
        compiler_params=_cparams(1),
    )(x, gain)


def _rms_bwd(dh, x, r, gain, add, name, bf16_copy=False):
    s_len, d = x.shape
    tm = 512

    def body(dh_ref, x_ref, r_ref, g_ref, add_ref, dx_ref, *rest):
        dg_ref = rest[-1]

        @pl.when(pl.program_id(0) == 0)
        def _():
            dg_ref[...] = jnp.zeros_like(dg_ref)
        rv = r_ref[...]
        xhat = x_ref[...] * rv
        dhv = dh_ref[...]
        dg_ref[...] += jnp.sum(dhv * xhat, axis=0, keepdims=True)
        dxhat = dhv * g_ref[...]
        dx = add_ref[...] + rv * (dxhat - xhat * jnp.mean(dxhat * xhat, axis=-1, keepdims=True))
        dx_ref[...] = dx
        if bf16_copy:
            rest[0][...] = dx.astype(BF16)

    row = pl.BlockSpec((tm, d), lambda i: (i, 0))
    vec = pl.BlockSpec((1, d), lambda i: (0, 0))
    out_specs = [row] + ([row] if bf16_copy else []) + [vec]
    out_shape = [jax.ShapeDtypeStruct((s_len, d), F32)] + ([jax.ShapeDtypeStruct((s_len, d), BF16)] if bf16_copy else [])
    out_shape.append(jax.ShapeDtypeStruct((1, d), F32))
    return pl.pallas_call(
        body, name=name, grid=(s_len // tm,),
        in_specs=[row, row, pl.BlockSpec((tm, 1), lambda i: (i, 0)), vec, row],
        out_specs=tuple(out_specs), out_shape=tuple(out_shape), compiler_params=_cparams(1),
    )(dh, x, r, gain, add)


def _resident(shape, index_map):
    return pl.BlockSpec(shape, index_map, pipeline_mode=pl.Buffered(1))


def _mm_rows(pairs, name, nt=False, tm=512, bias=None, sigmoid=False, res=None, out_dtypes=(F32,)):
    m = pairs[0][0].shape[0]
    n = pairs[0][1].shape[0] if nt else pairs[0][1].shape[1]
    n_pairs = len(pairs)
    has_bias, has_res = bias is not None, res is not None
    dims = (((1,), (1,)), ((), ())) if nt else (((1,), (0,)), ((), ()))

    def body(*refs):
        acc = None
        for p in range(n_pairs):
            t = lax.dot_general(refs[2 * p][...].astype(BF16), refs[2 * p + 1][...], dims, preferred_element_type=F32)
            acc = t if acc is None else acc + t
        pos = 2 * n_pairs
        if has_bias:
            acc = acc + refs[pos][...]
            pos += 1
        if sigmoid:
            acc = _sigmoid(acc)
        if has_res:
            acc = refs[pos][...] + acc
            pos += 1
        for o_ref in refs[pos:]:
            o_ref[...] = acc.astype(o_ref.dtype)

    in_specs, args = [], []
    for a, w, blk in pairs:
        k = a.shape[1]
        in_specs.append(pl.BlockSpec((tm, k), lambda i: (i, 0)))
        if nt:
            in_specs.append(_resident((n, k), lambda i, blk=blk: (0, blk)))
        else:
            in_specs.append(_resident((k, n), lambda i, blk=blk: (blk, 0)))
        args += [a, w]
    if has_bias:
        in_specs.append(_resident((1, n), lambda i: (0, 0)))
        args.append(bias)
    if has_res:
        in_specs.append(pl.BlockSpec((tm, n), lambda i: (i, 0)))
        args.append(res)
    out = pl.BlockSpec((tm, n), lambda i: (i, 0))
    outs = pl.pallas_call(
        body, name=name, grid=(m // tm,), in_specs=in_specs, out_specs=tuple([out] * len(out_dtypes)),
        out_shape=tuple(jax.ShapeDtypeStruct((m, n), dt) for dt in out_dtypes), compiler_params=_cparams(1),
    )(*args)
    return outs[0] if len(out_dtypes) == 1 else outs


def _mm_tn(a, b, name, tile=256):
    k, m = a.shape
    n = b.shape[1]
    dims = (((0,), (0,)), ((), ()))

    def body(a_ref, b_ref, o_ref):
        o_ref[...] = lax.dot_general(a_ref[...].astype(BF16), b_ref[...].astype(BF16), dims, preferred_element_type=F32)

    if n <= m:
        t = min(tile, m)
        grid, a_spec, b_spec = (m // t,), pl.BlockSpec((k, t), lambda i: (0, i)), _resident((k, n), lambda i: (0, 0))
        o_spec = pl.BlockSpec((t, n), lambda i: (i, 0))
    else:
        t = min(tile, n)
        grid, a_spec, b_spec = (n // t,), _resident((k, m), lambda i: (0, 0)), pl.BlockSpec((k, t), lambda i: (0, i))
        o_spec = pl.BlockSpec((m, t), lambda i: (0, i))
    return pl.pallas_call(
        body, name=name, grid=grid, in_specs=[a_spec, b_spec], out_specs=o_spec,
        out_shape=jax.ShapeDtypeStruct((m, n), F32), compiler_params=_cparams(1),
    )(a, b)


def _norm_rope(t, gain, c, s1, s2, seg):
    rs = lax.rsqrt(_seg_sum(t * t, seg) * (1.0 / HEAD_DIM) + EPS)
    return _rope((t * rs) * gain, c, s1, s2)


def _dup_half(y, half):
    lane = lax.broadcasted_iota(jnp.int32, y.shape, 1)
    rolled = pltpu.roll(y, HEAD_DIM, 1)
    keep = (lane < HEAD_DIM) if half == 0 else (lane >= HEAD_DIM)
    return jnp.where(keep, y, rolled)


def _qk_prep(proj, cb0, d, gqa, gq, gk, tabs, name):
    s_len = proj.shape[0]
    tm = 512
    rows = tm // d
    n_units = 4 if gqa else 2 * d
    n_q = 4 if gqa else 2
    n_in = 6

    def body(*refs):
        in_refs = refs[:n_in]
        gq_ref, gk_ref, c_ref, s1_ref, s2_ref, o_ref = refs[n_in:]
        seg = _seg_matrix(HEAD_DIM)

        def rows_of(ref, r):
            return ref[...] if d == 1 else ref[pl.ds(r, rows, stride=d), :]

        def put(unit_col, y):
            o_ref[:, unit_col * LANES:(unit_col + 1) * LANES] = y.astype(BF16)

        for r in range(d):
            c, s1, s2 = rows_of(c_ref, r), rows_of(s1_ref, r), rows_of(s2_ref, r)
            for b in range(n_in):
                t = rows_of(in_refs[b], r)
                if b < n_q:
                    put((b * d + r) if not gqa else b, _norm_rope(t, gq_ref[...], c, s1, s2, seg))
                elif not gqa:
                    sec, pair = (1, b - 2) if b < 4 else (2, b - 4)
                    y = _norm_rope(t, gk_ref[...], c, s1, s2, seg) if sec == 1 else t
                    put(sec * n_units + pair * d + r, y)
                else:
                    sec = 1 if b == 4 else 2
                    y = _norm_rope(t, gk_ref[...], c, s1, s2, seg) if sec == 1 else t
                    for u in range(n_units):
                        put(sec * n_units + u, _dup_half(y, u // 2))

    in_specs = [pl.BlockSpec((tm, LANES), lambda i, b=b: (i, cb0 + b)) for b in range(n_in)]
    vec = pl.BlockSpec((1, LANES), lambda i: (0, 0))
    tab = pl.BlockSpec((tm, LANES), lambda i: (i, 0))
    width = 3 * n_units * LANES
    return pl.pallas_call(
        body, name=name, grid=(s_len // tm,), in_specs=in_specs + [vec, vec, tab, tab, tab],
        out_specs=pl.BlockSpec((rows, width), lambda i: (i, 0)),
        out_shape=jax.ShapeDtypeStruct((s_len // d, width), BF16), compiler_params=_cparams(1),
    )(*([proj] * n_in), gq, gk, *tabs)


def _qk_prep_bwd(dqkv, proj, cb0, d, gqa, gq, gk, tabs, name):
    s_len = proj.shape[0]
    tm = 512
    rows = tm // d
    n_units = 4 if gqa else 2 * d
    n_q = 4 if gqa else 2
    n_in = 6

    def body(*refs):
        d_refs = refs[0:3]
        in_refs = refs[3:3 + n_in]
        gq_ref, gk_ref, c_ref, s1_ref, s2_ref, o_ref, dgq_ref, dgk_ref, stage = refs[3 + n_in:]
        seg = _seg_matrix(HEAD_DIM)

        @pl.when(pl.program_id(0) == 0)
        def _():
            dgq_ref[...] = jnp.zeros_like(dgq_ref)
            dgk_ref[...] = jnp.zeros_like(dgk_ref)

        def rows_of(ref, r):
            return ref[...] if d == 1 else ref[pl.ds(r, rows, stride=d), :]

        def unit(col):
            sec, u = divmod(col, n_units)
            return d_refs[sec][:, u * LANES:(u + 1) * LANES]

        def norm_bwd(dyr, t, gain, c, s1, s2, dg_ref):
            rs = lax.rsqrt(_seg_sum(t * t, seg) * (1.0 / HEAD_DIM) + EPS)
            that = t * rs
            dy = _unrope(dyr, c, s1, s2)
            dg_ref[...] += jnp.sum(dy * that, axis=0, keepdims=True)
            dthat = dy * gain
            return rs * (dthat - that * (_seg_sum(dthat * that, seg) * (1.0 / HEAD_DIM)))

        def fold(sec):
            tot = []
            for u in range(n_units):
                v = unit(sec * n_units + u)
                tot.append(v + pltpu.roll(v, HEAD_DIM, 1))
            lane = lax.broadcasted_iota(jnp.int32, tot[0].shape, 1)
            return jnp.where(lane < HEAD_DIM, tot[0] + tot[1], tot[2] + tot[3])

        for b in range(n_in):
            for r in range(d):
                c, s1, s2 = rows_of(c_ref, r), rows_of(s1_ref, r), rows_of(s2_ref, r)
                t = rows_of(in_refs[b], r)
                if b < n_q:
                    g = unit((b * d + r) if not gqa else b)
                    out = norm_bwd(g, t, gq_ref[...], c, s1, s2, dgq_ref)
                elif not gqa:
                    sec, pair = (1, b - 2) if b < 4 else (2, b - 4)
                    g = unit(sec * n_units + pair * d + r)
                    out = norm_bwd(g, t, gk_ref[...], c, s1, s2, dgk_ref) if sec == 1 else g
                else:
                    sec = 1 if b == 4 else 2
                    g = fold(sec)
                    out = norm_bwd(g, t, gk_ref[...], c, s1, s2, dgk_ref) if sec == 1 else g
                if d == 1:
                    o_ref[:, b * LANES:(b + 1) * LANES] = out.astype(BF16)
                else:
                    stage[pl.ds(r, rows, stride=d), :] = out
            if d != 1:
                o_ref[:, b * LANES:(b + 1) * LANES] = stage[...].astype(BF16)

    in_specs = [pl.BlockSpec((rows, n_units * LANES), lambda i: (i, 0))] * 3
    in_specs += [pl.BlockSpec((tm, LANES), lambda i, b=b: (i, cb0 + b)) for b in range(n_in)]
    vec = pl.BlockSpec((1, LANES), lambda i: (0, 0))
    tab = pl.BlockSpec((tm, LANES), lambda i: (i, 0))
    return pl.pallas_call(
        body, name=name, grid=(s_len // tm,), in_specs=in_specs + [vec, vec, tab, tab, tab],
        out_specs=(pl.BlockSpec((tm, n_in * LANES), lambda i: (i, 0)), vec, vec),
        out_shape=(jax.ShapeDtypeStruct((s_len, n_in * LANES), BF16), jax.ShapeDtypeStruct((1, LANES), F32),
                   jax.ShapeDtypeStruct((1, LANES), F32)),
        scratch_shapes=[pltpu.VMEM((tm, LANES), F32)], compiler_params=_cparams(1),
    )(*dqkv, *([proj] * n_in), gq, gk, *tabs)


def _head_masks(shape):
    lane = lax.broadcasted_iota(jnp.int32, shape, 1)
    return lane < HEAD_DIM, lane >= HEAD_DIM


def _band_fwd(qkv, n_units, max_dist, sinks, name):
    n_rows = qkv.shape[0]
    nb = n_rows // BLOCK
    scale = HEAD_DIM ** -0.5
    has_sink = sinks is not None
    assert not has_sink or max_dist < BLOCK

    qn, un = min(nb, BAND_Q_BLOCKS), BAND_UNITS
    ug = n_units // un

    def body(*refs):
        q_ref, kp_ref, km_ref, vp_ref, vm_ref = refs[:5]
        o_ref, lse_ref = refs[-2:]
        i = pl.program_id(1)
        qi = lax.broadcasted_iota(jnp.int32, (BLOCK, 2 * BLOCK), 0)
        kj = lax.broadcasted_iota(jnp.int32, (BLOCK, 2 * BLOCK), 1)
        dist = qi + BLOCK - kj
        band = (dist >= 0) & (dist <= max_dist)
        band_first = band & ((i > 0) | (kj >= BLOCK))
        m0, m1 = _head_masks((BLOCK, LANES))
        zero = jnp.zeros((BLOCK, LANES), BF16)
        for ub in range(un):
            cs = slice(ub * LANES, (ub + 1) * LANES)
            for qb in range(qn):
                rs = slice(qb * BLOCK, (qb + 1) * BLOCK)
                q = q_ref[rs, cs]
                if qb == 0:
                    kk = jnp.concatenate([kp_ref[:, cs], km_ref[0:BLOCK, cs]], axis=0)
                    vv = jnp.concatenate([vp_ref[:, cs], vm_ref[0:BLOCK, cs]], axis=0)
                    valid = band_first
                else:
                    kk = km_ref[(qb - 1) * BLOCK:(qb + 1) * BLOCK, cs]
                    vv = vm_ref[(qb - 1) * BLOCK:(qb + 1) * BLOCK, cs]
                    valid = band
                outs, lses = [], []
                for e, hm in enumerate((m0, m1)):
                    qe = jnp.where(hm, q, zero)
                    s = lax.dot_general(qe, kk, (((1,), (1,)), ((), ())), preferred_element_type=F32) * scale
                    s = jnp.where(valid, s, -jnp.inf)
                    if has_sink:
                        s = jnp.where(kj == 0, refs[5][ub][:, e * HEAD_DIM:e * HEAD_DIM + 1], s)
                    mx = jnp.max(s, axis=-1, keepdims=True)
                    p = jnp.exp(s - mx)
                    den = jnp.sum(p, axis=-1, keepdims=True)
                    pn = p * (1.0 / den)
                    if has_sink:
                        pn = jnp.where(kj == 0, 0.0, pn)
                    pn = pn.astype(BF16)
                    outs.append(jnp.dot(pn, vv, preferred_element_type=F32))
                    lses.append(mx + jnp.log(den))
                o_ref[rs, cs] = jnp.where(m0, outs[0], outs[1])
                lse_ref[rs, cs] = jnp.where(m0, jnp.broadcast_to(lses[0], (BLOCK, LANES)),
                                            jnp.broadcast_to(lses[1], (BLOCK, LANES)))

    def main(sec):
        return pl.BlockSpec((qn * BLOCK, un * LANES), lambda u, i: (i, sec * ug + u))

    def prev(sec):
        return pl.BlockSpec((BLOCK, un * LANES), lambda u, i: (jnp.maximum(i * qn - 1, 0), sec * ug + u))

    in_specs = [main(0), prev(1), main(1), prev(2), main(2)]
    args = [qkv] * 5
    if has_sink:
        in_specs.append(pl.BlockSpec((un, 1, LANES), lambda u, i: (u, 0, 0)))
        args.append(sinks)
    return pl.pallas_call(
        body, name=name, grid=(ug, nb // qn), in_specs=in_specs, out_specs=(main(0), main(0)),
        out_shape=(jax.ShapeDtypeStruct((n_rows, n_units * LANES), F32),) * 2, compiler_params=_cparams(2),
    )(*args)


def _band_bwd(qkv, do, lse, delta, n_units, max_dist, name):
    n_rows = qkv.shape[0]
    nb = n_rows // BLOCK
    scale = HEAD_DIM ** -0.5

    qn, un = min(nb, BAND_Q_BLOCKS), BAND_UNITS
    ug = n_units // un
    steps = nb // qn
    nt_dims = (((1,), (1,)), ((), ()))
    tn_dims = (((0,), (0,)), ((), ()))

    def body(qm_ref, qx_ref, kp_ref, km_ref, vp_ref, vm_ref, dom_ref, dox_ref, lm_ref, lx_ref, dm_ref, dx_ref,
             dq_ref, dk_ref, dv_ref):
        i = pl.program_id(1)
        m0, m1 = _head_masks((BLOCK, LANES))
        zero = jnp.zeros((BLOCK, LANES), BF16)
        qi = lax.broadcasted_iota(jnp.int32, (BLOCK, 2 * BLOCK), 0)
        kj = lax.broadcasted_iota(jnp.int32, (BLOCK, 2 * BLOCK), 1)
        dist = qi + BLOCK - kj
        band = (dist >= 0) & (dist <= max_dist)
        band_first = band & ((i > 0) | (kj >= BLOCK))
        qr = lax.broadcasted_iota(jnp.int32, (2 * BLOCK, BLOCK), 0)
        kc = lax.broadcasted_iota(jnp.int32, (2 * BLOCK, BLOCK), 1)
        dist2 = qr - kc
        band2 = (dist2 >= 0) & (dist2 <= max_dist)
        band2_last = band2 & ((qr < BLOCK) | (i < steps - 1))
        m0w, m1w = _head_masks((2 * BLOCK, LANES))
        zero2 = jnp.zeros((2 * BLOCK, LANES), BF16)

        def two(main_ref, next_ref, kb, cs):
            if kb < qn - 1:
                return main_ref[kb * BLOCK:(kb + 2) * BLOCK, cs]
            return jnp.concatenate([main_ref[kb * BLOCK:(kb + 1) * BLOCK, cs], next_ref[:, cs]], axis=0)

        for ub in range(un):
            cs = slice(ub * LANES, (ub + 1) * LANES)
            for qb in range(qn):
                rs = slice(qb * BLOCK, (qb + 1) * BLOCK)
                q = qm_ref[rs, cs]
                dob = dom_ref[rs, cs]
                lse_b = lm_ref[rs, cs]
                del_b = dm_ref[rs, cs]
                if qb == 0:
                    kk = jnp.concatenate([kp_ref[:, cs], km_ref[0:BLOCK, cs]], axis=0)
                    vv = jnp.concatenate([vp_ref[:, cs], vm_ref[0:BLOCK, cs]], axis=0)
                    valid = band_first
                else:
                    kk = km_ref[(qb - 1) * BLOCK:(qb + 1) * BLOCK, cs]
                    vv = vm_ref[(qb - 1) * BLOCK:(qb + 1) * BLOCK, cs]
                    valid = band
                dqs = []
                for e, hm in enumerate((m0, m1)):
                    col = slice(e * HEAD_DIM, e * HEAD_DIM + 1)
                    s = lax.dot_general(jnp.where(hm, q, zero), kk, nt_dims, preferred_element_type=F32) * scale
                    p = jnp.where(valid, jnp.exp(s - lse_b[:, col]), 0.0)
                    dp = lax.dot_general(jnp.where(hm, dob, zero), vv, nt_dims, preferred_element_type=F32)
                    ds = (p * (dp - del_b[:, col]) * scale).astype(BF16)
                    dqs.append(jnp.dot(ds, kk, preferred_element_type=F32))
                dq_ref[rs, cs] = jnp.where(m0, dqs[0], dqs[1])
            for kb in range(qn):
                rs = slice(kb * BLOCK, (kb + 1) * BLOCK)
                qq = two(qm_ref, qx_ref, kb, cs)
                dd = two(dom_ref, dox_ref, kb, cs)
                ll = two(lm_ref, lx_ref, kb, cs)
                de = two(dm_ref, dx_ref, kb, cs)
                k = km_ref[rs, cs]
                v = vm_ref[rs, cs]
                valid2 = band2 if kb < qn - 1 else band2_last
                dk = jnp.zeros((BLOCK, LANES), F32)
                dv = jnp.zeros((BLOCK, LANES), F32)
                for e, hm in enumerate((m0w, m1w)):
                    col = slice(e * HEAD_DIM, e * HEAD_DIM + 1)
                    qe = jnp.where(hm, qq, zero2)
                    doe = jnp.where(hm, dd, zero2)
                    s = lax.dot_general(qe, k, nt_dims, preferred_element_type=F32) * scale
                    p = jnp.where(valid2, jnp.exp(s - ll[:, col]), 0.0)
                    dp = lax.dot_general(doe, v, nt_dims, preferred_element_type=F32)
                    ds = (p * (dp - de[:, col]) * scale).astype(BF16)
                    dk = dk + lax.dot_general(ds, qe, tn_dims, preferred_element_type=F32)
                    dv = dv + lax.dot_general(p.astype(BF16), doe, tn_dims, preferred_element_type=F32)
                dk_ref[rs, cs] = dk
                dv_ref[rs, cs] = dv

    def main(sec):
        return pl.BlockSpec((qn * BLOCK, un * LANES), lambda u, i: (i, sec * ug + u))

    def prev(sec):
        return pl.BlockSpec((BLOCK, un * LANES), lambda u, i: (jnp.maximum(i * qn - 1, 0), sec * ug + u))

    def nxt(sec):
        return pl.BlockSpec((BLOCK, un * LANES), lambda u, i: (jnp.minimum((i + 1) * qn, nb - 1), sec * ug + u))

    in_specs = [main(0), nxt(0), prev(1), main(1), prev(2), main(2),
                main(0), nxt(0), main(0), nxt(0), main(0), nxt(0)]
    args = [qkv] * 6 + [do, do, lse, lse, delta, delta]
    shp = jax.ShapeDtypeStruct((n_rows, n_units * LANES), F32)
    return pl.pallas_call(
        body, name=name, grid=(ug, steps), in_specs=in_specs, out_specs=(main(0), main(0), main(0)),
        out_shape=(shp, shp, shp), compiler_params=_cparams(2),
    )(*args)


def _merge_groups(os_, lses, dils, name):
    s_len = os_[0].shape[0] * dils[0]
    tm = 512

    def body(*refs):
        o_refs, l_refs = refs[0:3], refs[3:6]
        o_ref, lse_ref = refs[6:8]
        so, sl = refs[8:11], refs[11:14]
        for pair in range(2):
            for g, d in enumerate(dils):
                rows = tm // d
                for r in range(d):
                    col = slice((pair * d + r) * LANES, (pair * d + r + 1) * LANES)
                    if d == 1:
                        so[g][...] = o_refs[g][:, col]
                        sl[g][...] = l_refs[g][:, col]
                    else:
                        so[g][pl.ds(r, rows, stride=d), :] = o_refs[g][:, col]
                        sl[g][pl.ds(r, rows, stride=d), :] = l_refs[g][:, col]
            l0, l1, l2 = sl[0][...], sl[1][...], sl[2][...]
            mx = jnp.maximum(jnp.maximum(l0, l1), l2)
            e0, e1, e2 = jnp.exp(l0 - mx), jnp.exp(l1 - mx), jnp.exp(l2 - mx)
            den = e0 + e1 + e2
            inv = 1.0 / den
            o_ref[:, pair * LANES:(pair + 1) * LANES] = (so[0][...] * (e0 * inv) + so[1][...] * (e1 * inv)
                                                         + so[2][...] * (e2 * inv))
            lse_ref[:, pair * LANES:(pair + 1) * LANES] = mx + jnp.log(den)

    in_specs = [pl.BlockSpec((tm // d, 2 * d * LANES), lambda i: (i, 0)) for d in dils] * 2
    out = pl.BlockSpec((tm, 2 * LANES), lambda i: (i, 0))
    shp = jax.ShapeDtypeStruct((s_len, 2 * LANES), F32)
    return pl.pallas_call(
        body, name=name, grid=(s_len // tm,), in_specs=in_specs, out_specs=(out, out), out_shape=(shp, shp),
        scratch_shapes=[pltpu.VMEM((tm, LANES), F32)] * 6, compiler_params=_cparams(1),
    )(*os_, *lses)


def _bwd_prep(do, o, lse, dils, sinks, name):
    s_len, width = do.shape
    n_pairs = width // LANES
    tm = 512
    has_sink = sinks is not None
    n_g = len(dils)

    def body(*refs):
        do_ref, o_ref, lse_ref = refs[:3]
        pos = 3
        if has_sink:
            sink_ref = refs[pos]
            pos += 1
        outs = refs[pos:pos + 3 * n_g]
        pos += 3 * n_g
        if has_sink:
            dsink_ref = refs[pos]
            pos += 1
        s_do, s_l, s_d = refs[pos:pos + 3]
        seg = _seg_matrix(HEAD_DIM)

        if has_sink:
            @pl.when(pl.program_id(0) == 0)
            def _():
                dsink_ref[...] = jnp.zeros_like(dsink_ref)

        for pair in range(n_pairs):
            col = slice(pair * LANES, (pair + 1) * LANES)
            dov = do_ref[:, col]
            lv = lse_ref[:, col]
            delta = _seg_sum(dov * o_ref[:, col], seg)
            if has_sink:
                dsink_ref[pair] += -jnp.sum(jnp.exp(sink_ref[pair] - lv) * delta, axis=0, keepdims=True)
            s_do[...] = dov
            s_l[...] = lv
            s_d[...] = delta
            for g, d in enumerate(dils):
                rows = tm // d
                for r in range(d):
                    oc = slice((pair * d + r) * LANES, (pair * d + r + 1) * LANES)
                    if d == 1:
                        a, b, c = s_do[...], s_l[...], s_d[...]
                    else:
                        a = s_do[pl.ds(r, rows, stride=d), :]
                        b = s_l[pl.ds(r, rows, stride=d), :]
                        c = s_d[pl.ds(r, rows, stride=d), :]
                    outs[3 * g][:, oc] = a.astype(BF16)
                    outs[3 * g + 1][:, oc] = b
                    outs[3 * g + 2][:, oc] = c

    row = pl.BlockSpec((tm, width), lambda i: (i, 0))
    in_specs = [row, row, row]
    args = [do, o, lse]
    if has_sink:
        in_specs.append(pl.BlockSpec((n_pairs, 1, LANES), lambda i: (0, 0, 0)))
        args.append(sinks)
    out_specs, out_shape = [], []
    for d in dils:
        for dt in (BF16, F32, F32):
            out_specs.append(pl.BlockSpec((tm // d, n_pairs * d * LANES), lambda i: (i, 0)))
            out_shape.append(jax.ShapeDtypeStruct((s_len // d, n_pairs * d * LANES), dt))
    if has_sink:
        out_specs.append(pl.BlockSpec((n_pairs, 1, LANES), lambda i: (0, 0, 0)))
        out_shape.append(jax.ShapeDtypeStruct((n_pairs, 1, LANES), F32))
    return pl.pallas_call(
        body, name=name, grid=(s_len // tm,), in_specs=in_specs, out_specs=tuple(out_specs),
        out_shape=tuple(out_shape), scratch_shapes=[pltpu.VMEM((tm, LANES), F32)] * 3, compiler_params=_cparams(1),
    )(*args)


def _mem_kv(mem, mem_gain, w_kv, k_gain, name):
    m_len = mem.shape[0]
    kw = M_HEADS * M_HEAD_DIM

    def body(mem_ref, mg_ref, w_ref, kg_ref, k_ref, v_ref):
        mv = mem_ref[...]
        r = lax.rsqrt(jnp.mean(mv * mv, axis=-1, keepdims=True) + EPS)
        mn = ((mv * r) * mg_ref[...]).astype(BF16)
        kv = jnp.dot(mn, w_ref[...], preferred_element_type=F32)
        for h in range(M_HEADS):
            col = slice(h * M_HEAD_DIM, (h + 1) * M_HEAD_DIM)
            t = kv[:, col]
            rk = lax.rsqrt(jnp.mean(t * t, axis=-1, keepdims=True) + EPS)
            k_ref[:, col] = ((t * rk) * kg_ref[...]).astype(BF16)
        v_ref[...] = kv[:, kw:].astype(BF16)

    shp = jax.ShapeDtypeStruct((m_len, kw), BF16)
    return pl.pallas_call(body, name=name, out_shape=(shp, shp),
                          compiler_params=pltpu.CompilerParams(vmem_limit_bytes=VMEM_LIMIT_BYTES))(mem, mem_gain, w_kv, k_gain)


def _mem_kv_bwd(mem, mem_gain, w_kv, k_gain, dk, dv, name):
    m_len, d = mem.shape
    kw = M_HEADS * M_HEAD_DIM

    def body(mem_ref, mg_ref, w_ref, kg_ref, dk_ref, dv_ref, dw_ref, dmg_ref, dkg_ref, dkv_ref):
        mv = mem_ref[...]
        r = lax.rsqrt(jnp.mean(mv * mv, axis=-1, keepdims=True) + EPS)
        mhat = mv * r
        mn = (mhat * mg_ref[...]).astype(BF16)
        kv = jnp.dot(mn, w_ref[...], preferred_element_type=F32)
        dkg = jnp.zeros((1, M_HEAD_DIM), F32)
        for h in range(M_HEADS):
            col = slice(h * M_HEAD_DIM, (h + 1) * M_HEAD_DIM)
            t = kv[:, col]
            rk = lax.rsqrt(jnp.mean(t * t, axis=-1, keepdims=True) + EPS)
            that = t * rk
            dy = dk_ref[:, col]
            dkg = dkg + jnp.sum(dy * that, axis=0, keepdims=True)
            dthat = dy * kg_ref[...]
            dkv_ref[:, col] = (rk * (dthat - that * jnp.mean(dthat * that, axis=-1, keepdims=True))).astype(BF16)
        dkv_ref[:, kw:] = dv_ref[...].astype(BF16)
        dkg_ref[...] = dkg
        dkv = dkv_ref[...]
        dw_ref[...] = lax.dot_general(mn, dkv, (((0,), (0,)), ((), ())), preferred_element_type=F32)
        dmn = lax.dot_general(dkv, w_ref[...], (((1,), (1,)), ((), ())), preferred_element_type=F32)
        dmg_ref[...] = jnp.sum(dmn * mhat, axis=0, keepdims=True)

    return pl.pallas_call(
        body, name=name,
        out_shape=(jax.ShapeDtypeStruct((d, 2 * kw), F32), jax.ShapeDtypeStruct((1, d), F32),
                   jax.ShapeDtypeStruct((1, M_HEAD_DIM), F32)),
        scratch_shapes=[pltpu.VMEM((m_len, 2 * kw), BF16)],
        compiler_params=pltpu.CompilerParams(vmem_limit_bytes=VMEM_LIMIT_BYTES),
    )(mem, mem_gain, w_kv, k_gain, dk, dv)


def _mem_attn_fwd(proj, cidx, mk, mv, q_gain, name):
    s_len = proj.shape[0]
    kw = M_HEADS * M_HEAD_DIM
    tm = 512
    scale = M_HEAD_DIM ** -0.5

    def body(q_ref, k_ref, v_ref, g_ref, o_ref):
        for h in range(M_HEADS):
            col = slice(h * M_HEAD_DIM, (h + 1) * M_HEAD_DIM)
            t = q_ref[:, col]
            rs = lax.rsqrt(jnp.mean(t * t, axis=-1, keepdims=True) + EPS)
            qn = ((t * rs) * g_ref[...]).astype(BF16)
            s = lax.dot_general(qn, k_ref[:, col], (((1,), (1,)), ((), ())), preferred_element_type=F32) * scale
            mx = jnp.max(s, axis=-1, keepdims=True)
            p = jnp.exp(s - mx)
            pn = (p * (1.0 / jnp.sum(p, axis=-1, keepdims=True))).astype(BF16)
            o_ref[:, col] = jnp.dot(pn, v_ref[:, col], preferred_element_type=F32).astype(BF16)

    whole = pl.BlockSpec((MEM_LEN, kw), lambda i: (0, 0))
    return pl.pallas_call(
        body, name=name, grid=(s_len // tm,),
        in_specs=[pl.BlockSpec((tm, kw), lambda i: (i, cidx)), whole, whole, pl.BlockSpec((1, M_HEAD_DIM), lambda i: (0, 0))],
        out_specs=pl.BlockSpec((tm, kw), lambda i: (i, 0)),
        out_shape=jax.ShapeDtypeStruct((s_len, kw), BF16), compiler_params=_cparams(1),
    )(proj, mk, mv, q_gain)


def _mem_attn_bwd(proj, cidx, mk, mv, q_gain, do, name):
    s_len = proj.shape[0]
    kw = M_HEADS * M_HEAD_DIM
    tm = 512
    scale = M_HEAD_DIM ** -0.5

    def body(q_ref, k_ref, v_ref, g_ref, do_ref, dq_ref, dk_ref, dv_ref, dg_ref):
        @pl.when(pl.program_id(0) == 0)
        def _():
            dk_ref[...] = jnp.zeros_like(dk_ref)
            dv_ref[...] = jnp.zeros_like(dv_ref)
            dg_ref[...] = jnp.zeros_like(dg_ref)

        for h in range(M_HEADS):
            col = slice(h * M_HEAD_DIM, (h + 1) * M_HEAD_DIM)
            t = q_ref[:, col]
            rs = lax.rsqrt(jnp.mean(t * t, axis=-1, keepdims=True) + EPS)
            that = t * rs
            qn = (that * g_ref[...]).astype(BF16)
            kh, vh = k_ref[:, col], v_ref[:, col]
            dob = do_ref[:, col].astype(BF16)
            s = lax.dot_general(qn, kh, (((1,), (1,)), ((), ())), preferred_element_type=F32) * scale
            mx = jnp.max(s, axis=-1, keepdims=True)
            p = jnp.exp(s - mx)
            p = p * (1.0 / jnp.sum(p, axis=-1, keepdims=True))
            dp = lax.dot_general(dob, vh, (((1,), (1,)), ((), ())), preferred_element_type=F32)
            ds = (p * (dp - jnp.sum(p * dp, axis=-1, keepdims=True)) * scale).astype(BF16)
            dqn = jnp.dot(ds, kh, preferred_element_type=F32)
            dk_ref[:, col] += lax.dot_general(ds, qn, (((0,), (0,)), ((), ())), preferred_element_type=F32)
            dv_ref[:, col] += lax.dot_general(p.astype(BF16), dob, (((0,), (0,)), ((), ())), preferred_element_type=F32)
            dg_ref[...] += jnp.sum(dqn * that, axis=0, keepdims=True)
            dthat = dqn * g_ref[...]
            dq_ref[:, col] = (rs * (dthat - that * jnp.mean(dthat * that, axis=-1, keepdims=True))).astype(BF16)

    whole = pl.BlockSpec((MEM_LEN, kw), lambda i: (0, 0))
    vec = pl.BlockSpec((1, M_HEAD_DIM), lambda i: (0, 0))
    row = pl.BlockSpec((tm, kw), lambda i: (i, 0))
    return pl.pallas_call(
        body, name=name, grid=(s_len // tm,),
        in_specs=[pl.BlockSpec((tm, kw), lambda i: (i, cidx)), whole, whole, vec, row],
        out_specs=(row, whole, whole, vec),
        out_shape=(jax.ShapeDtypeStruct((s_len, kw), BF16), jax.ShapeDtypeStruct((MEM_LEN, kw), F32),
                   jax.ShapeDtypeStruct((MEM_LEN, kw), F32), jax.ShapeDtypeStruct((1, M_HEAD_DIM), F32)),
        compiler_params=_cparams(1),
    )(proj, mk, mv, q_gain, do)


def _gate_merge(gates, pa, pb, pm, name):
    s_len, d = pa.shape
    tm = 256

    def body(g_ref, a_ref, b_ref, m_ref, o_ref):
        f = lambda v: v.astype(F32)
        o_ref[...] = (f(g_ref[:, 0:d]) * f(a_ref[...]) + f(g_ref[:, d:2 * d]) * f(b_ref[...])
                      + f(g_ref[:, 2 * d:3 * d]) * f(m_ref[...])).astype(BF16)

    row = pl.BlockSpec((tm, d), lambda i: (i, 0))
    return pl.pallas_call(
        body, name=name, grid=(s_len // tm,), in_specs=[pl.BlockSpec((tm, 3 * d), lambda i: (i, 0)), row, row, row],
        out_specs=row, out_shape=jax.ShapeDtypeStruct((s_len, d), BF16), compiler_params=_cparams(1),
    )(gates, pa, pb, pm)


def _gate_merge_bwd(dmerged, gates, pa, pb, pm, name):
    s_len, d = pa.shape
    tm = 256

    def body(dm_ref, g_ref, a_ref, b_ref, m_ref, da_ref, db_ref, dmm_ref, dg_ref, dbg_ref):
        @pl.when(pl.program_id(0) == 0)
        def _():
            dbg_ref[...] = jnp.zeros_like(dbg_ref)
        dm = dm_ref[...]
        for k, (p_ref, dp_ref) in enumerate(((a_ref, da_ref), (b_ref, db_ref), (m_ref, dmm_ref))):
            col = slice(k * d, (k + 1) * d)
            g = g_ref[:, col].astype(F32)
            dp_ref[...] = (dm * g).astype(BF16)
            dpre = (dm * p_ref[...].astype(F32)) * (g * (1.0 - g))
            dbg_ref[:, col] += jnp.sum(dpre, axis=0, keepdims=True)
            dg_ref[:, col] = dpre.astype(BF16)

    row = pl.BlockSpec((tm, d), lambda i: (i, 0))
    wide = pl.BlockSpec((tm, 3 * d), lambda i: (i, 0))
    shp = jax.ShapeDtypeStruct((s_len, d), BF16)
    return pl.pallas_call(
        body, name=name, grid=(s_len // tm,), in_specs=[row, wide, row, row, row],
        out_specs=(row, row, row, wide, pl.BlockSpec((1, 3 * d), lambda i: (0, 0))),
        out_shape=(shp, shp, shp, jax.ShapeDtypeStruct((s_len, 3 * d), BF16), jax.ShapeDtypeStruct((1, 3 * d), F32)),
        compiler_params=_cparams(1),
    )(dmerged, gates, pa, pb, pm)


CONV_CHUNK = 256


def _pick_row(tile, j):
    row = lax.broadcasted_iota(jnp.int32, tile.shape, 0)
    return jnp.sum(jnp.where(row == j, tile, jnp.zeros_like(tile)), axis=0, keepdims=True)


def _rows_before(ref, start, k):
    cur = ref[pl.ds(start, CONV_CHUNK), :].astype(F32)
    prev = ref[pl.ds(pl.multiple_of(jnp.maximum(start - 16, 0), 16), 16), :].astype(F32)
    prev = jnp.where(start > 0, prev, jnp.zeros_like(prev))
    rolled = pltpu.roll(cur, k, 0)
    row = lax.broadcasted_iota(jnp.int32, cur.shape, 0)
    for j in range(k):
        rolled = jnp.where(row == j, _pick_row(prev, 16 - k + j), rolled)
    return rolled


def _rows_after(ref, start, k):
    cur = ref[pl.ds(start, CONV_CHUNK), :]
    nxt = ref[pl.ds(pl.multiple_of(start + CONV_CHUNK, 8), 8), :]
    rolled = pltpu.roll(cur, CONV_CHUNK - k, 0)
    row = lax.broadcasted_iota(jnp.int32, cur.shape, 0)
    for j in range(k):
        rolled = jnp.where(row == CONV_CHUNK - k + j, _pick_row(nxt, j), rolled)
    return rolled


def _conv_pre(u_ref, w_ref, b_ref, start):
    u2 = _rows_before(u_ref, start, 2)
    u1 = _rows_before(u_ref, start, 1)
    u0 = u_ref[pl.ds(start, CONV_CHUNK), :].astype(F32)
    c = ((b_ref[...] + w_ref[0:1, :] * u2) + w_ref[1:2, :] * u1) + w_ref[2:3, :] * u0
    return c, (u2, u1, u0)


def _conv_glu(u, conv_w, conv_b, name):
    s_len = u.shape[0]
    nblk = D_FF // LANES

    def body(ua_ref, ug_ref, wa_ref, wg_ref, ba_ref, bg_ref, o_ref):
        def chunk(ci, carry):
            start = pl.multiple_of(ci * CONV_CHUNK, CONV_CHUNK)
            ca, _ = _conv_pre(ua_ref, wa_ref, ba_ref, start)
            cg, _ = _conv_pre(ug_ref, wg_ref, bg_ref, start)
            o_ref[pl.ds(start, CONV_CHUNK), :] = ((ca * _sigmoid(ca)) * cg).astype(BF16)
            return carry
        lax.fori_loop(0, s_len // CONV_CHUNK, chunk, 0)

    def col(rows, off):
        return pl.BlockSpec((rows, LANES), lambda j: (0, off + j))

    return pl.pallas_call(
        body, name=name, grid=(nblk,),
        in_specs=[col(s_len, 0), col(s_len, nblk), col(3, 0), col(3, nblk), col(1, 0), col(1, nblk)],
        out_specs=col(s_len, 0), out_shape=jax.ShapeDtypeStruct((s_len, D_FF), BF16), compiler_params=_cparams(1),
    )(u, u, conv_w, conv_w, conv_b, conv_b)


def _conv_glu_bwd(dact, u, conv_w, conv_b, name):
    s_len = u.shape[0]
    nblk = D_FF // LANES
    n_chunks = s_len // CONV_CHUNK

    def body(da_ref, ua_ref, ug_ref, wa_ref, wg_ref, ba_ref, bg_ref,
             dua_ref, dug_ref, dwa_ref, dwg_ref, dba_ref, dbg_ref, sa, sg):
        sa[pl.ds(s_len, 8), :] = jnp.zeros((8, LANES), F32)
        sg[pl.ds(s_len, 8), :] = jnp.zeros((8, LANES), F32)
        zero = jnp.zeros((1, LANES), F32)

        def chunk1(ci, carry):
            start = pl.multiple_of(ci * CONV_CHUNK, CONV_CHUNK)
            ca, ua = _conv_pre(ua_ref, wa_ref, ba_ref, start)
            cg, ug = _conv_pre(ug_ref, wg_ref, bg_ref, start)
            dact_v = da_ref[pl.ds(start, CONV_CHUNK), :].astype(F32)
            sig = _sigmoid(ca)
            dcg = dact_v * (ca * sig)
            dca = (dact_v * cg) * (sig * (1.0 + ca * (1.0 - sig)))
            sa[pl.ds(start, CONV_CHUNK), :] = dca
            sg[pl.ds(start, CONV_CHUNK), :] = dcg
            out = [carry[0] + jnp.sum(dca, axis=0, keepdims=True), carry[1] + jnp.sum(dcg, axis=0, keepdims=True)]
            for j in range(3):
                out.append(carry[2 + j] + jnp.sum(dca * ua[j], axis=0, keepdims=True))
            for j in range(3):
                out.append(carry[5 + j] + jnp.sum(dcg * ug[j], axis=0, keepdims=True))
            return tuple(out)

        acc = lax.fori_loop(0, n_chunks, chunk1, (zero,) * 8)
        dba_ref[...] = acc[0]
        dbg_ref[...] = acc[1]
        for j in range(3):
            dwa_ref[j:j + 1, :] = acc[2 + j]
            dwg_ref[j:j + 1, :] = acc[5 + j]

        def chunk2(ci, carry):
            start = pl.multiple_of(ci * CONV_CHUNK, CONV_CHUNK)
            for s_ref, w_ref, o_ref in ((sa, wa_ref, dua_ref), (sg, wg_ref, dug_ref)):
                d0 = s_ref[pl.ds(start, CONV_CHUNK), :]
                d1 = _rows_after(s_ref, start, 1)
                d2 = _rows_after(s_ref, start, 2)
                o_ref[pl.ds(start, CONV_CHUNK), :] = (w_ref[2:3, :] * d0 + w_ref[1:2, :] * d1
                                                      + w_ref[0:1, :] * d2).astype(BF16)
            return carry
        lax.fori_loop(0, n_chunks, chunk2, 0)

    def col(rows, off):
        return pl.BlockSpec((rows, LANES), lambda j: (0, off + j))

    big = jax.ShapeDtypeStruct((s_len, D_FF), BF16)
    return pl.pallas_call(
        body, name=name, grid=(nblk,),
        in_specs=[col(s_len, 0), col(s_len, 0), col(s_len, nblk), col(3, 0), col(3, nblk), col(1, 0), col(1, nblk)],
        out_specs=(col(s_len, 0), col(s_len, 0), col(3, 0), col(3, 0), col(1, 0), col(1, 0)),
        out_shape=(big, big, jax.ShapeDtypeStruct((3, D_FF), F32), jax.ShapeDtypeStruct((3, D_FF), F32),
                   jax.ShapeDtypeStruct((1, D_FF), F32), jax.ShapeDtypeStruct((1, D_FF), F32)),
        scratch_shapes=[pltpu.VMEM((s_len + 8, LANES), F32)] * 2, compiler_params=_cparams(1),
    )(dact, u, u, conv_w, conv_w, conv_b, conv_b)


def _loss_head(y, target, name):
    s_len, d = y.shape
    tm = 512

    def body(y_ref, t_ref, dy_ref, dyb_ref, l_ref):
        @pl.when(pl.program_id(0) == 0)
        def _():
            l_ref[...] = jnp.zeros_like(l_ref)
        err = y_ref[...] - t_ref[...]
        dy = err * (1.0 / d)
        dy_ref[...] = dy
        dyb_ref[...] = dy.astype(BF16)
        part = 0.5 * jnp.sum(jnp.mean(err * err, axis=-1, keepdims=True), axis=0, keepdims=True)
        l_ref[...] += jnp.broadcast_to(part, l_ref.shape)

    row = pl.BlockSpec((tm, d), lambda i: (i, 0))
    return pl.pallas_call(
        body, name=name, grid=(s_len // tm,), in_specs=[row, row],
        out_specs=(row, row, pl.BlockSpec((8, LANES), lambda i: (0, 0))),
        out_shape=(jax.ShapeDtypeStruct((s_len, d), F32), jax.ShapeDtypeStruct((s_len, d), BF16),
                   jax.ShapeDtypeStruct((8, LANES), F32)),
        compiler_params=_cparams(1),
    )(y, target)


def _rope_tables(positions):
    half = ROPE_DIMS // 2
    freqs = jnp.exp(jnp.arange(half, dtype=F32) * (-2.0 * math.log(ROPE_THETA) / ROPE_DIMS))
    ang = positions.reshape(-1).astype(F32)[:, None] * freqs
    cos, sin = jnp.cos(ang), jnp.sin(ang)
    n = ang.shape[0]
    zeros = lambda w: jnp.zeros((n, w), F32)
    c = jnp.concatenate([cos, cos, jnp.ones((n, HEAD_DIM - ROPE_DIMS), F32)], axis=1)
    s1 = jnp.concatenate([-sin, zeros(HEAD_DIM - half)], axis=1)
    s2 = jnp.concatenate([zeros(half), sin, zeros(HEAD_DIM - ROPE_DIMS)], axis=1)
    return tuple(jnp.tile(t, (1, 2)) for t in (c, s1, s2))


def _two(v):
    return jnp.tile(v.reshape(1, HEAD_DIM), (1, 2))


def _fold_heads(g):
    return g[0, :HEAD_DIM] + g[0, HEAD_DIM:]


MIX_WEIGHTS = ('w_gate', 'w_mem_kv', 'w_o_a', 'w_o_b', 'w_o_m', 'w_out')
FFN_WEIGHTS = ('w_up', 'conv_w', 'w_down')


def _device_step(x, mem, positions, target, w, hooks=None):
    tabs = _rope_tables(positions)
    dils = tuple(d for _, d in A_GROUPS)
    grads = {}
    w = dict(w)

    h, r1 = _rms_fwd(x, w['attn_norm'], "rms1")
    proj = _mm_rows([(h, w['w_in'], 0)], "mm_in")

    qkv_a, o_g, lse_g = [], [], []
    for gi, (window, d) in enumerate(A_GROUPS):
        gq, gk = _two(w['a_q_norm'][gi]), _two(w['a_k_norm'][gi])
        qkv = _qk_prep(proj, 6 * gi, d, False, gq, gk, tabs, f"qk_prep_a{gi}")
        o, lse = _band_fwd(qkv, 2 * d, window // d, None, f"band_fwd_a{gi}")
        qkv_a.append(qkv)
        o_g.append(o)
        lse_g.append(lse)
    o_a, lse_a = _merge_groups(o_g, lse_g, dils, "merge_a")

    gbq, gbk = _two(w['b_q_norm']), _two(w['b_k_norm'])
    sinks = jnp.repeat(w['b_sinks'].reshape(4, 2), HEAD_DIM, axis=1).reshape(4, 1, LANES)
    qkv_b = _qk_prep(proj, 18, 1, True, gbq, gbk, tabs, "qk_prep_b")
    o_b, lse_b = _band_fwd(qkv_b, 4, B_WINDOW - 1, sinks, "band_fwd_b")

    if hooks is not None:
        w.update(hooks.weights('mix', o_b))
    gates = _mm_rows([(h, w['w_gate'], 0)], "mm_gate", bias=w['b_gate'], sigmoid=True, out_dtypes=(BF16,))
    mk, mv = _mem_kv(mem, w['mem_norm'], w['w_mem_kv'], w['m_k_norm'], "mem_kv")
    o_m = _mem_attn_fwd(proj, 6, mk, mv, w['m_q_norm'], "mem_attn")

    pa = _mm_rows([(o_a, w['w_o_a'], 0)], "mm_oa", out_dtypes=(BF16,))
    pb = _mm_rows([(o_b, w['w_o_b'], 0)], "mm_ob", out_dtypes=(BF16,))
    pm = _mm_rows([(o_m, w['w_o_m'], 0)], "mm_om", out_dtypes=(BF16,))
    merged = _gate_merge(gates, pa, pb, pm, "gate_merge")
    x1 = _mm_rows([(merged, w['w_out'], 0)], "mm_out", res=x)

    if hooks is not None:
        w.update(hooks.weights('ffn', x1))
    h2, r2 = _rms_fwd(x1, w['ffn_norm'], "rms2")
    u = _mm_rows([(h2, w['w_up'], 0)], "mm_up", out_dtypes=(BF16,))
    act = _conv_glu(u, w['conv_w'], w['conv_b'], "conv_glu")
    y = _mm_rows([(act, w['w_down'], 0)], "mm_down", res=x1)
    dy, dy_b, loss = _loss_head(y, target, "loss_head")

    dact = _mm_rows([(dy_b, w['w_down'], 0)], "mm_d_act", nt=True, out_dtypes=(BF16,))
    grads['w_down'] = _mm_tn(act, dy_b, "mm_dw_down")
    du_a, du_g, dcw_a, dcw_g, dcb_a, dcb_g = _conv_glu_bwd(dact, u, w['conv_w'], w['conv_b'], "conv_glu_bwd")
    grads['conv_w'] = jnp.concatenate([dcw_a, dcw_g], axis=1)
    grads['conv_b'] = jnp.concatenate([dcb_a, dcb_g], axis=1)
    dh2 = _mm_rows([(du_a, w['w_up'], 0), (du_g, w['w_up'], 1)], "mm_d_h2", nt=True)
    grads['w_up'] = jnp.concatenate([_mm_tn(h2, du_a, "mm_dw_up_a"), _mm_tn(h2, du_g, "mm_dw_up_g")], axis=1)
    ffn_gain = w['ffn_norm']
    if hooks is not None:
        ffn_gain = ffn_gain + hooks.grads('ffn', grads)[0:1, 0:1]
    dx1, dx1_b, grads['ffn_norm'] = _rms_bwd(dh2, x1, r2, ffn_gain, dy, "rms2_bwd", bf16_copy=True)

    dmerged = _mm_rows([(dx1_b, w['w_out'], 0)], "mm_d_merged", nt=True)
    grads['w_out'] = _mm_tn(merged, dx1_b, "mm_dw_out")
    dpa, dpb, dpm, dgpre, grads['b_gate'] = _gate_merge_bwd(dmerged, gates, pa, pb, pm, "gate_merge_bwd")
    do_a = _mm_rows([(dpa, w['w_o_a'], 0)], "mm_d_oa", nt=True)
    do_b = _mm_rows([(dpb, w['w_o_b'], 0)], "mm_d_ob", nt=True)
    do_m = _mm_rows([(dpm, w['w_o_m'], 0)], "mm_d_om", nt=True)
    grads['w_o_a'] = _mm_tn(o_a, dpa, "mm_dw_oa")
    grads['w_o_b'] = _mm_tn(o_b, dpb, "mm_dw_ob")
    grads['w_o_m'] = _mm_tn(o_m, dpm, "mm_dw_om")
    grads['w_gate'] = _mm_tn(h, dgpre, "mm_dw_gate")
    dq_m, dmk, dmv, grads['m_q_norm'] = _mem_attn_bwd(proj, 6, mk, mv, w['m_q_norm'], do_m, "mem_attn_bwd")
    grads['w_mem_kv'], grads['mem_norm'], grads['m_k_norm'] = _mem_kv_bwd(
        mem, w['mem_norm'], w['w_mem_kv'], w['m_k_norm'], dmk, dmv, "mem_kv_bwd")
    a_gain = w['a_q_norm']
    if hooks is not None:
        a_gain = a_gain + hooks.grads('mix', grads)[0:1, 0:1]

    prep = _bwd_prep(do_a, o_a, lse_a, dils, None, "bwd_prep_a")
    dproj, dgq_a, dgk_a = [], [], []
    for gi, (window, d) in enumerate(A_GROUPS):
        gq, gk = _two(a_gain[gi]), _two(w['a_k_norm'][gi])
        dqkv = _band_bwd(qkv_a[gi], prep[3 * gi], prep[3 * gi + 1], prep[3 * gi + 2], 2 * d, window // d,
                         f"band_bwd_a{gi}")
        dp, dgq, dgk = _qk_prep_bwd(dqkv, proj, 6 * gi, d, False, gq, gk, tabs, f"qk_prep_bwd_a{gi}")
        dproj.append(dp)
        dgq_a.append(_fold_heads(dgq))
        dgk_a.append(_fold_heads(dgk))
    grads['a_q_norm'] = jnp.stack(dgq_a)
    grads['a_k_norm'] = jnp.stack(dgk_a)

    do_bu, lse_bu, delta_bu, dsink = _bwd_prep(do_b, o_b, lse_b, (1,), sinks, "bwd_prep_b")
    dqkv = _band_bwd(qkv_b, do_bu, lse_bu, delta_bu, 4, B_WINDOW - 1, "band_bwd_b")
    dp_b, dgq, dgk = _qk_prep_bwd(dqkv, proj, 18, 1, True, gbq, gbk, tabs, "qk_prep_bwd_b")
    dproj.append(dp_b)
    grads['b_q_norm'] = _fold_heads(dgq)
    grads['b_k_norm'] = _fold_heads(dgk)
    grads['b_sinks'] = jnp.stack([dsink[:, 0, 0], dsink[:, 0, HEAD_DIM]], axis=1).reshape(8)

    dproj.append(dq_m)

    cols = (0, 1, 2, 3, 6)
    grads['w_in'] = jnp.concatenate([_mm_tn(h, dp, f"mm_dw_in{k}") for k, dp in enumerate(dproj)], axis=1)
    attn_gain = w['attn_norm']
    if hooks is not None:
        attn_gain = attn_gain + hooks.grads('in', grads)[0:1, 0:1]
    dh = _mm_rows([(dp, w['w_in'], c) for dp, c in zip(dproj, cols)] + [(dgpre, w['w_gate'], 0)], "mm_d_h", nt=True)
    grad_x, grads['attn_norm'] = _rms_bwd(dh, x, r1, attn_gain, dx1, "rms1_bwd")
    return loss, grad_x, grads


def _coords():
    return lax.axis_index("x"), lax.axis_index("y"), lax.axis_index("c")


def _slot(p):
    return 4 * p[0] + 2 * p[1] + p[2]


ALL_PEERS = tuple(range(1, N_DEV))
CHIP_PEERS = (1, 4, 2, 6)
OTHER_CHIPS = (4, 2, 6)


def _peers(me, masks=ALL_PEERS):
    x, y, c = me
    return [(1 - x if mask & 4 else x, 1 - y if mask & 2 else y, 1 - c if mask & 1 else c) for mask in masks]


HBM_SPEC = pl.BlockSpec(memory_space=pltpu.HBM)


def _all_gather(shards, name):
    n = len(shards)

    def body(*refs):
        ins, outs = refs[:n], refs[n:2 * n]
        token, send_sems, recv_sems, local_sems = refs[2 * n:]
        token[...] = jnp.zeros_like(token)
        x, y, c = _coords()
        me, sibling = (x, y, c), (x, y, 1 - c)
        chips = [(1 - x, y), (x, 1 - y), (1 - x, 1 - y)]

        def copy(a, k, block, to, src=None):
            dst = outs[a].at[_slot(block)]
            return pltpu.make_async_remote_copy(
                src_ref=dst if src is None else src, dst_ref=dst, send_sem=send_sems.at[a, k],
                recv_sem=recv_sems.at[a, k], device_id=to, device_id_type=MESH)

        mine = [pltpu.make_async_copy(ins[a], outs[a].at[_slot(me)], local_sems.at[a]) for a in range(n)]
        for cp in mine:
            cp.start()
        first = []
        for a in range(n):
            first.append(copy(a, 0, me, sibling, src=ins[a]))
            first += [copy(a, 1 + j, me, (*chip, c), src=ins[a]) for j, chip in enumerate(chips)]
        for cp in first:
            cp.start()
        passed = []
        for a in range(n):
            for j, chip in enumerate(chips):
                copy(a, 1 + j, (*chip, c), me).wait_recv()
                fwd = copy(a, 4 + j, (*chip, c), sibling)
                fwd.start()
                passed.append(fwd)
        for a in range(n):
            copy(a, 0, sibling, me).wait_recv()
            for j, chip in enumerate(chips):
                copy(a, 4 + j, (*chip, 1 - c), me).wait_recv()
        for cp in first + passed:
            cp.wait_send()
        for cp in mine:
            cp.wait()

    return pl.pallas_call(
        body, name=name, in_specs=[HBM_SPEC] * n,
        out_specs=tuple([HBM_SPEC] * n + [pl.BlockSpec(memory_space=pltpu.VMEM)]),
        out_shape=tuple([jax.ShapeDtypeStruct((N_DEV,) + s.shape, s.dtype) for s in shards]
                        + [jax.ShapeDtypeStruct((8, LANES), F32)]),
        scratch_shapes=[pltpu.SemaphoreType.DMA((n, 7)), pltpu.SemaphoreType.DMA((n, 7)), pltpu.SemaphoreType.DMA((n,))],
    )(*shards)


SEM_SPEC = pl.BlockSpec(memory_space=pltpu.SEMAPHORE)
SIDE_EFFECT = pltpu.SideEffectType.DATAFLOW_SIDE_EFFECTING


def _exchange_start(blocks, name, gather=False, masks=ALL_PEERS):
    n = len(blocks)
    n_peers = len(masks)

    def body(*refs):
        ins, lands = refs[:n], refs[n:2 * n]
        send_sems, recv_sems = refs[2 * n], refs[2 * n + 1]
        token = refs[-1]
        me = _coords()
        peers = _peers(me, masks)
        for a in range(n):
            for k in range(n_peers):
                pltpu.make_async_remote_copy(
                    src_ref=ins[a] if gather else ins[a].at[_slot(peers[k])], dst_ref=lands[a].at[_slot(me)],
                    send_sem=send_sems.at[a * n_peers + k], recv_sem=recv_sems.at[a * n_peers + k],
                    device_id=peers[k], device_id_type=MESH).start()
        token[...] = jnp.zeros_like(token)

    land_shapes = [((N_DEV,) + b.shape) if gather else b.shape for b in blocks]
    hbm_in = [pltpu.HBM(b.shape, b.dtype) for b in blocks]
    hbm_land = [pltpu.HBM(s, b.dtype) for s, b in zip(land_shapes, blocks)]
    sems = pltpu.SemaphoreType.DMA((n * n_peers,))
    ins = [pltpu.with_memory_space_constraint(b, pltpu.HBM) for b in blocks]
    lands = [pltpu.with_memory_space_constraint(lax.empty(s, b.dtype), pltpu.HBM) for s, b in zip(land_shapes, blocks)]
    return pl.pallas_call(
        body, name=name, out_shape=(sems, sems, *hbm_in, *hbm_land, jax.ShapeDtypeStruct((8, LANES), F32)),
        in_specs=[HBM_SPEC] * (2 * n),
        out_specs=(SEM_SPEC, SEM_SPEC, *([HBM_SPEC] * (2 * n)), pl.BlockSpec(memory_space=pltpu.VMEM)),
        input_output_aliases={i: 2 + i for i in range(2 * n)},
        compiler_params=pltpu.CompilerParams(has_side_effects=SIDE_EFFECT),
    )(*ins, *lands)


def _exchange_wait(started, after, name, gather=False, masks=ALL_PEERS):
    n = (len(started) - 3) // 2
    n_peers = len(masks)
    send_sems, recv_sems = started[0], started[1]
    thru = started[2:2 + 2 * n]

    def body(*refs):
        ins, lands = refs[:n], refs[n:2 * n]
        send_ref, recv_ref = refs[2 * n], refs[2 * n + 1]
        me = _coords()
        peers = _peers(me, masks)
        for a in range(n):
            for k in range(n_peers):
                cp = pltpu.make_async_remote_copy(
                    src_ref=ins[a] if gather else ins[a].at[_slot(peers[k])], dst_ref=lands[a].at[_slot(peers[k])],
                    send_sem=send_ref.at[a * n_peers + k], recv_sem=recv_ref.at[a * n_peers + k],
                    device_id=peers[k], device_id_type=MESH)
                cp.wait_send()
                cp.wait_recv()

    hbm = [pltpu.HBM(t.shape, t.dtype) for t in thru]
    res = pl.pallas_call(
        body, name=name, out_shape=tuple(hbm),
        in_specs=[HBM_SPEC] * (2 * n) + [SEM_SPEC, SEM_SPEC, pl.BlockSpec(memory_space=pl.ANY)],
        out_specs=tuple([HBM_SPEC] * (2 * n)), input_output_aliases={i: i for i in range(2 * n)},
        compiler_params=pltpu.CompilerParams(has_side_effects=SIDE_EFFECT),
    )(*thru, send_sems, recv_sems, after)
    return res[n:]


def _sibling_forward(arrays, name):
    n = len(arrays)
    n_fwd = len(OTHER_CHIPS)

    def body(*refs):
        bufs = refs[n:2 * n]
        send_sems, recv_sems = refs[2 * n:]
        x, y, c = _coords()
        sibling = (x, y, 1 - c)
        mine = _peers((x, y, c), OTHER_CHIPS)
        theirs = _peers(sibling, OTHER_CHIPS)

        def copy(a, k, block):
            rows = bufs[a].at[_slot(block)]
            return pltpu.make_async_remote_copy(
                src_ref=rows, dst_ref=rows, send_sem=send_sems.at[a * n_fwd + k], recv_sem=recv_sems.at[a * n_fwd + k],
                device_id=sibling, device_id_type=MESH)

        sends = [copy(a, k, mine[k]) for a in range(n) for k in range(n_fwd)]
        for cp in sends:
            cp.start()
        for a in range(n):
            for k in range(n_fwd):
                copy(a, k, theirs[k]).wait_recv()
        for cp in sends:
            cp.wait_send()

    return pl.pallas_call(
        body, name=name, in_specs=[HBM_SPEC] * n, out_specs=tuple([HBM_SPEC] * n),
        out_shape=tuple(jax.ShapeDtypeStruct(a.shape, a.dtype) for a in arrays),
        input_output_aliases={i: i for i in range(n)},
        scratch_shapes=[pltpu.SemaphoreType.DMA((n * n_fwd,)), pltpu.SemaphoreType.DMA((n * n_fwd,))],
    )(*arrays)


def _all_sum(p, name):
    def body(p_ref, o_ref, recv, send_sems, recv_sems):
        me = _coords()
        peers = _peers(me)
        recv[_slot(me)] = p_ref[...]

        def copy(k, landing):
            return pltpu.make_async_remote_copy(
                src_ref=p_ref, dst_ref=recv.at[_slot(landing)], send_sem=send_sems.at[k], recv_sem=recv_sems.at[k],
                device_id=peers[k], device_id_type=MESH)

        sends = [copy(k, me) for k in range(N_DEV - 1)]
        for cp in sends:
            cp.start()
        for k in range(N_DEV - 1):
            copy(k, peers[k]).wait_recv()
        for cp in sends:
            cp.wait_send()
        acc = recv[0]
        for s in range(1, N_DEV):
            acc = acc + recv[s]
        o_ref[...] = acc

    vmem = pl.BlockSpec(memory_space=pltpu.VMEM)
    return pl.pallas_call(
        body, name=name, in_specs=[vmem], out_specs=vmem, out_shape=jax.ShapeDtypeStruct(p.shape, F32),
        scratch_shapes=[pltpu.VMEM((N_DEV,) + p.shape, F32), pltpu.SemaphoreType.DMA((N_DEV - 1,)),
                        pltpu.SemaphoreType.DMA((N_DEV - 1,))],
    )(p)


def _adam(w, g, m, v):
    m2 = ADAM_B1 * m + (1.0 - ADAM_B1) * g
    v2 = ADAM_B2 * v + (1.0 - ADAM_B2) * (g * g)
    m_hat = m2 / (1.0 - ADAM_B1 ** ADAM_STEP)
    v_hat = v2 / (1.0 - ADAM_B2 ** ADAM_STEP)
    delta = -ADAM_LR * (m_hat / (jnp.sqrt(v_hat) + ADAM_EPS) + ADAM_WD * w)
    return delta, m2, v2


def _row_tile(rows, cols):
    best = rows
    for t in range(16, rows, 16):
        if rows % t == 0 and t * cols * 4 <= (1 << 20):
            best = t
    return best


def _adam_reduce(parts, w, m, v, name):
    rows, cols = w.shape
    tr = _row_tile(rows, cols)

    def body(p_ref, w_ref, m_ref, v_ref, g_ref, d_ref, m2_ref, v2_ref):
        g = p_ref[0].astype(F32)
        for s in range(1, N_DEV):
            g = g + p_ref[s].astype(F32)
        g_ref[...] = g
        d_ref[...], m2_ref[...], v2_ref[...] = _adam(w_ref[...], g, m_ref[...], v_ref[...])

    blk = pl.BlockSpec((tr, cols), lambda i: (i, 0))
    shp = jax.ShapeDtypeStruct((rows, cols), F32)
    return pl.pallas_call(
        body, name=name, grid=(rows // tr,),
        in_specs=[pl.BlockSpec((N_DEV, tr, cols), lambda i: (0, i, 0)), blk, blk, blk],
        out_specs=(blk,) * 4, out_shape=(shp,) * 4, compiler_params=_cparams(1),
    )(parts, w, m, v)


PACK_COLS = 1024
PACK = {'attn_norm': (0, 1, 1024), 'mem_norm': (1, 1, 1024), 'ffn_norm': (2, 1, 1024), 'b_gate': (3, 3, 1024),
        'conv_b': (6, 6, 1024), 'a_q_norm': (12, 3, 64), 'a_k_norm': (15, 3, 64), 'b_q_norm': (18, 1, 64),
        'b_k_norm': (19, 1, 64), 'm_q_norm': (20, 1, 128), 'm_k_norm': (21, 1, 128), 'b_sinks': (22, 1, 8)}
PACK_LOSS_ROW = 23
PACK_ROWS = 24


def _pack_pieces(name, width):
    r0, nr, lanes = PACK[name]
    out = []
    for j in range(nr):
        if lanes == PACK_COLS:
            w = min(PACK_COLS, width - j * PACK_COLS)
            out.append((r0 + j, slice(0, 1), slice(j * PACK_COLS, j * PACK_COLS + w), w))
        else:
            out.append((r0 + j, slice(j, j + 1), slice(0, lanes), lanes))
    return out


def _pack_small(grads, loss_tile, name):
    names = list(PACK)

    def body(*refs):
        o_ref = refs[-1]
        o_ref[...] = jnp.zeros_like(o_ref)
        for k, nm in enumerate(names):
            for row, rs, ls, w in _pack_pieces(nm, refs[k].shape[1]):
                o_ref[row:row + 1, 0:w] = refs[k][rs, ls]
        o_ref[PACK_LOSS_ROW:PACK_LOSS_ROW + 1, 0:1] = refs[len(names)][0:1, 0:1]

    vmem = pl.BlockSpec(memory_space=pltpu.VMEM)
    args = [grads[nm] for nm in names] + [loss_tile]
    return pl.pallas_call(body, name=name, in_specs=[vmem] * len(args), out_specs=vmem,
                          out_shape=jax.ShapeDtypeStruct((PACK_ROWS, PACK_COLS), F32))(*args)


def _adam_small(gsum, ws, ms, vs, name):
    names = list(PACK)
    n = len(names)

    def body(*refs):
        g_ref = refs[0]
        w_refs, m_refs, v_refs = refs[1:1 + n], refs[1 + n:1 + 2 * n], refs[1 + 2 * n:1 + 3 * n]
        outs = refs[1 + 3 * n:]
        outs[0][...] = g_ref[PACK_LOSS_ROW:PACK_LOSS_ROW + 1, 0:1]
        for k, nm in enumerate(names):
            o_g, o_d, o_m, o_v = outs[1 + 4 * k:5 + 4 * k]
            for row, rs, ls, width in _pack_pieces(nm, w_refs[k].shape[1]):
                src = (rs, ls)
                g = g_ref[row:row + 1, 0:width]
                d, m2, v2 = _adam(w_refs[k][src], g, m_refs[k][src], v_refs[k][src])
                o_g[src] = g
                o_d[src] = d
                o_m[src] = m2
                o_v[src] = v2

    vmem = pl.BlockSpec(memory_space=pltpu.VMEM)
    shapes = [jax.ShapeDtypeStruct((1, 1), F32)]
    for nm in names:
        shapes += [jax.ShapeDtypeStruct(ws[nm].shape, F32)] * 4
    args = [gsum] + [ws[nm] for nm in names] + [ms[nm] for nm in names] + [vs[nm] for nm in names]
    return pl.pallas_call(
        body, name=name, in_specs=[vmem] * len(args), out_specs=tuple([vmem] * len(shapes)), out_shape=tuple(shapes),
    )(*args)


def _as2d(name, a):
    return a.reshape(a.shape[-2], a.shape[-1]) if a.ndim == 3 else a


def kernel(x, mem, positions, attn_norm, w_in, a_q_norm, a_k_norm, b_q_norm, b_k_norm, b_sinks, mem_norm, w_mem_kv, m_q_norm, m_k_norm, w_o_a, w_o_b, w_o_m, w_gate, b_gate, w_out, ffn_norm, w_up, conv_w, conv_b, w_down, loss_target, m_attn_norm, m_w_in, m_a_q_norm, m_a_k_norm, m_b_q_norm, m_b_k_norm, m_b_sinks, m_mem_norm, m_w_mem_kv, m_m_q_norm, m_m_k_norm, m_w_o_a, m_w_o_b, m_w_o_m, m_w_gate, m_b_gate, m_w_out, m_ffn_norm, m_w_up, m_conv_w, m_conv_b, m_w_down, v_attn_norm, v_w_in, v_a_q_norm, v_a_k_norm, v_b_q_norm, v_b_k_norm, v_b_sinks, v_mem_norm, v_w_mem_kv, v_m_q_norm, v_m_k_norm, v_w_o_a, v_w_o_b, v_w_o_m, v_w_gate, v_b_gate, v_w_out, v_ffn_norm, v_w_up, v_conv_w, v_conv_b, v_w_down):
    given = dict(attn_norm=attn_norm, w_in=w_in, a_q_norm=a_q_norm, a_k_norm=a_k_norm, b_q_norm=b_q_norm, b_k_norm=b_k_norm, b_sinks=b_sinks, mem_norm=mem_norm, w_mem_kv=w_mem_kv, m_q_norm=m_q_norm, m_k_norm=m_k_norm, w_o_a=w_o_a, w_o_b=w_o_b, w_o_m=w_o_m, w_gate=w_gate, b_gate=b_gate, w_out=w_out, ffn_norm=ffn_norm, w_up=w_up, conv_w=conv_w, conv_b=conv_b, w_down=w_down)
    mom1 = dict(attn_norm=m_attn_norm, w_in=m_w_in, a_q_norm=m_a_q_norm, a_k_norm=m_a_k_norm, b_q_norm=m_b_q_norm, b_k_norm=m_b_k_norm, b_sinks=m_b_sinks, mem_norm=m_mem_norm, w_mem_kv=m_w_mem_kv, m_q_norm=m_m_q_norm, m_k_norm=m_m_k_norm, w_o_a=m_w_o_a, w_o_b=m_w_o_b, w_o_m=m_w_o_m, w_gate=m_w_gate, b_gate=m_b_gate, w_out=m_w_out, ffn_norm=m_ffn_norm, w_up=m_w_up, conv_w=m_conv_w, conv_b=m_conv_b, w_down=m_w_down)
    mom2 = dict(attn_norm=v_attn_norm, w_in=v_w_in, a_q_norm=v_a_q_norm, a_k_norm=v_a_k_norm, b_q_norm=v_b_q_norm, b_k_norm=v_b_k_norm, b_sinks=v_b_sinks, mem_norm=v_mem_norm, w_mem_kv=v_w_mem_kv, m_q_norm=v_m_q_norm, m_k_norm=v_m_k_norm, w_o_a=v_w_o_a, w_o_b=v_w_o_b, w_o_m=v_w_o_m, w_gate=v_w_gate, b_gate=v_b_gate, w_out=v_w_out, ffn_norm=v_ffn_norm, w_up=v_w_up, conv_w=v_conv_w, conv_b=v_conv_b, w_down=v_w_down)

    big = list(BIG)
    stages = {'mix': list(MIX_WEIGHTS), 'ffn': list(FFN_WEIGHTS), 'in': ['w_in']}
    my_slot = _slot(_coords())

    def shard(n):
        return given[n][0] if n == 'conv_w' else given[n][0].astype(BF16)

    def whole(n, g):
        _, r, c = g.shape
        return g.reshape(N_DEV * r, c) if BIG[n] == 0 else g.transpose(1, 0, 2).reshape(r, N_DEV * c)

    def to_blocks(n, g):
        r, c = given[n].shape[1:]
        g = g.reshape(N_DEV, r, c) if BIG[n] == 0 else g.reshape(r, N_DEV, c).transpose(1, 0, 2)
        return g if n == 'conv_w' else g.astype(BF16)

    class Hooks:
        def __init__(self, token):
            self.coming, self.sent = {}, {}
            for stage in ('mix', 'ffn'):
                names = stages[stage]
                src = [shard(n) if n == 'conv_w' else (given[n][0] + token).astype(BF16) for n in names]
                self.coming[stage] = _exchange_start(src, f"gather_{stage}_start", gather=True, masks=CHIP_PEERS)

        def weights(self, stage, after):
            names = stages[stage]
            landed = _exchange_wait(self.coming[stage], after, f"gather_{stage}_wait", gather=True, masks=CHIP_PEERS)
            landed = _sibling_forward(landed, f"gather_{stage}_forward")
            return {n: whole(n, lax.dynamic_update_slice_in_dim(land, shard(n)[None], my_slot, axis=0))
                    for n, land in zip(names, landed)}

        def grads(self, stage, g):
            blocks = [to_blocks(n, g[n]) for n in stages[stage]]
            own = [lax.dynamic_slice_in_dim(b, my_slot, 1, axis=0) for b in blocks]
            self.sent[stage] = (_exchange_start(blocks, f"exchange_{stage}_start"), own)
            return self.sent[stage][0][-1]

        def parts(self, stage, after):
            started, own = self.sent[stage]
            landed = _exchange_wait(started, after, f"exchange_{stage}_wait")
            return {n: lax.dynamic_update_slice_in_dim(land, o, my_slot, axis=0)
                    for n, land, o in zip(stages[stage], landed, own)}

    w_in_all, token = _all_gather([shard('w_in')], "gather_w_in")
    hooks = Hooks(token[0, 0])
    w = {'w_in': whole('w_in', w_in_all)}
    for n in SMALL:
        w[n] = given[n]
    w['a_q_norm'], w['a_k_norm'] = given['a_q_norm'][0], given['a_k_norm'][0]
    w['b_q_norm'], w['b_k_norm'], w['b_sinks'] = given['b_q_norm'][0], given['b_k_norm'][0], given['b_sinks'][0]

    loss_tile, grad_x, grads = _device_step(x[0], mem[0], positions[0], loss_target[0], w, hooks)
    out = {}
    after = grad_x
    for stage in ('ffn', 'mix', 'in'):
        for n, p in hooks.parts(stage, after).items():
            res = _adam_reduce(p, given[n][0], mom1[n][0], mom2[n][0], f"adam_{n}")
            out[n] = tuple(t[None] for t in res)
            after = res[0]

    small = {n: grads[n] for n in PACK}
    small['b_q_norm'], small['b_k_norm'] = grads['b_q_norm'].reshape(1, -1), grads['b_k_norm'].reshape(1, -1)
    small['b_sinks'] = grads['b_sinks'].reshape(1, -1)
    gsum = _all_sum(_pack_small(small, loss_tile, "pack_small"), "sum_small")
    ws = {n: _as2d(n, given[n]) for n in PACK}
    ms = {n: _as2d(n, mom1[n]) for n in PACK}
    vs = {n: _as2d(n, mom2[n]) for n in PACK}
    res = _adam_small(gsum, ws, ms, vs, "adam_small")
    loss = res[0].reshape(())
    for k, n in enumerate(PACK):
        out[n] = tuple(t.reshape(given[n].shape) for t in res[1 + 4 * k:5 + 4 * k])

    outs = [loss, grad_x[None]]
    for field in range(4):
        outs += [out[n][field] for n in WEIGHTS]
    return tuple(outs)
```

```python
import functools
import math

import jax
import jax.numpy as jnp
from jax import lax
from jax.experimental import pallas as pl
from jax.experimental.pallas import tpu as pltpu

F32 = jnp.float32
BF16 = jnp.bfloat16

N_DEV = 8
D_MODEL = 1024
HEAD_DIM = 64
A_GROUPS = ((128, 1), (512, 4), (2048, 16))
B_WINDOW = 128
M_HEADS = 4
M_HEAD_DIM = 128
MEM_LEN = 256
D_FF = 2816
ROPE_THETA = 500000.0
ROPE_DIMS = 16
BLOCK = 128
EPS = 1e-6
LANES = 128
BAND_Q_BLOCKS = 4
BAND_UNITS = 2

ADAM_LR = 0.001
ADAM_B1 = 0.9
ADAM_B2 = 0.999
ADAM_EPS = 1e-08
ADAM_WD = 0.01
ADAM_STEP = 10

VMEM_LIMIT_BYTES = 56 * 1024 * 1024
MESH = pl.DeviceIdType.MESH

WEIGHTS = ['attn_norm', 'w_in', 'a_q_norm', 'a_k_norm', 'b_q_norm', 'b_k_norm', 'b_sinks', 'mem_norm',
           'w_mem_kv', 'm_q_norm', 'm_k_norm', 'w_o_a', 'w_o_b', 'w_o_m', 'w_gate', 'b_gate', 'w_out',
           'ffn_norm', 'w_up', 'conv_w', 'conv_b', 'w_down']
BIG = {'w_in': 1, 'w_mem_kv': 0, 'w_o_a': 1, 'w_o_b': 1, 'w_o_m': 1, 'w_gate': 1, 'w_out': 0, 'w_up': 1,
       'conv_w': 1, 'w_down': 0}
SMALL = [n for n in WEIGHTS if n not in BIG]


def _cparams(n_grid):
    return pltpu.CompilerParams(dimension_semantics=("arbitrary",) * n_grid, vmem_limit_bytes=VMEM_LIMIT_BYTES)


def _seg_matrix(width):
    shift = width.bit_length() - 1
    r = lax.shift_right_logical(lax.broadcasted_iota(jnp.int32, (LANES, LANES), 0), shift)
    c = lax.shift_right_logical(lax.broadcasted_iota(jnp.int32, (LANES, LANES), 1), shift)
    return jnp.where(r == c, 1.0, 0.0).astype(BF16)


def _seg_sum(x, seg):
    hi = x.astype(BF16)
    r1 = x - hi.astype(F32)
    mid = r1.astype(BF16)
    lo = (r1 - mid.astype(F32)).astype(BF16)
    dot = functools.partial(jnp.dot, preferred_element_type=F32)
    return dot(hi, seg) + dot(mid, seg) + dot(lo, seg)


def _rope(y, c, s1, s2):
    return y * c + pltpu.roll(y, LANES - ROPE_DIMS // 2, 1) * s1 + pltpu.roll(y, ROPE_DIMS // 2, 1) * s2


def _unrope(dy, c, s1, s2):
    return dy * c + pltpu.roll(dy * s1, ROPE_DIMS // 2, 1) + pltpu.roll(dy * s2, LANES - ROPE_DIMS // 2, 1)


def _sigmoid(x):
    return 1.0 / (1.0 + jnp.exp(-x))


def _rms_fwd(x, gain, name):
    s_len, d = x.shape
    tm = 512

    def body(x_ref, g_ref, h_ref, ht_ref, r_ref):
        xv = x_ref[...]
        r = lax.rsqrt(jnp.mean(xv * xv, axis=-1, keepdims=True) + EPS)
        h = ((xv * r) * g_ref[...]).astype(BF16)
        h_ref[...] = h
        ht_ref[...] = h.T
        r_ref[...] = r

    return pl.pallas_call(
        body, name=name, grid=(s_len // tm,),
        in_specs=[pl.BlockSpec((tm, d), lambda i: (i, 0)), pl.BlockSpec((1, d), lambda i: (0, 0))],
        out_specs=(pl.BlockSpec((tm, d), lambda i: (i, 0)), pl.BlockSpec((d, tm), lambda i: (0, i)),
                   pl.BlockSpec((tm, 1), lambda i: (i, 0))),
        out_shape=(jax.ShapeDtypeStruct((s_len, d), BF16), jax.ShapeDtypeStruct((d, s_len), BF16),
                   jax.ShapeDtypeStruct((s_len, 1), F32)),
        compiler_params=_cparams(1),
    )(x, gain)


def _rms_bwd(dh, x, r, gain, add, name, bf16_copy=False):
    s_len, d = x.shape
    tm = 512

    def body(dh_ref, x_ref, r_ref, g_ref, add_ref, dx_ref, *rest):
        dg_ref = rest[-1]

        @pl.when(pl.program_id(0) == 0)
        def _():
            dg_ref[...] = jnp.zeros_like(dg_ref)
        rv = r_ref[...]
        xhat = x_ref[...] * rv
        dhv = dh_ref[...]
        dg_ref[...] += jnp.sum(dhv * xhat, axis=0, keepdims=True)
        dxhat = dhv * g_ref[...]
        dx = add_ref[...] + rv * (dxhat - xhat * jnp.mean(dxhat * xhat, axis=-1, keepdims=True))
        dx_ref[...] = dx
        if bf16_copy:
            rest[0][...] = dx.astype(BF16)

    row = pl.BlockSpec((tm, d), lambda i: (i, 0))
    vec = pl.BlockSpec((1, d), lambda i: (0, 0))
    out_specs = [row] + ([row] if bf16_copy else []) + [vec]
    out_shape = [jax.ShapeDtypeStruct((s_len, d), F32)] + ([jax.ShapeDtypeStruct((s_len, d), BF16)] if bf16_copy else [])
    out_shape.append(jax.ShapeDtypeStruct((1, d), F32))
    return pl.pallas_call(
        body, name=name, grid=(s_len // tm,),
        in_specs=[row, row, pl.BlockSpec((tm, 1), lambda i: (i, 0)), vec, row],
        out_specs=tuple(out_specs), out_shape=tuple(out_shape), compiler_params=_cparams(1),
    )(dh, x, r, gain, add)


def _resident(shape, index_map):
    return pl.BlockSpec(shape, index_map, pipeline_mode=pl.Buffered(1))


def _mm_rows(pairs, name, nt=False, tm=512, bias=None, sigmoid=False, res=None, out_dtypes=(F32,), loss_target=None):
    m = pairs[0][0].shape[0]
    n = pairs[0][1].shape[0] if nt else pairs[0][1].shape[1]
    n_pairs = len(pairs)
    has_bias, has_res, has_loss = bias is not None, res is not None, loss_target is not None
    dims = (((1,), (1,)), ((), ())) if nt else (((1,), (0,)), ((), ()))

    def body(*refs):
        acc = None
        for p in range(n_pairs):
            t = lax.dot_general(refs[2 * p][...].astype(BF16), refs[2 * p + 1][...], dims, preferred_element_type=F32)
            acc = t if acc is None else acc + t
        pos = 2 * n_pairs
        if has_bias:
            acc = acc + refs[pos][...]
            pos += 1
        if sigmoid:
            acc = _sigmoid(acc)
        if has_res:
            acc = refs[pos][...] + acc
            pos += 1
        if has_loss:
            dy_ref, dyb_ref, l_ref = refs[pos + 1:]

            @pl.when(pl.program_id(0) == 0)
            def _():
                l_ref[...] = jnp.zeros_like(l_ref)
            err = acc - refs[pos][...]
            dy = err * (1.0 / n)
            dy_ref[...] = dy
            dyb_ref[...] = dy.astype(BF16)
            part = 0.5 * jnp.sum(jnp.mean(err * err, axis=-1, keepdims=True), axis=0, keepdims=True)
            l_ref[...] += jnp.broadcast_to(part, l_ref.shape)
            return
        for o_ref in refs[pos:]:
            o_ref[...] = acc.astype(o_ref.dtype)

    in_specs, args = [], []
    for a, w, blk in pairs:
        k = a.shape[1]
        in_specs.append(pl.BlockSpec((tm, k), lambda i: (i, 0)))
        if nt:
            in_specs.append(_resident((n, k), lambda i, blk=blk: (0, blk)))
        else:
            in_specs.append(_resident((k, n), lambda i, blk=blk: (blk, 0)))
        args += [a, w]
    if has_bias:
        in_specs.append(_resident((1, n), lambda i: (0, 0)))
        args.append(bias)
    if has_res:
        in_specs.append(pl.BlockSpec((tm, n), lambda i: (i, 0)))
        args.append(res)
    out = pl.BlockSpec((tm, n), lambda i: (i, 0))
    if has_loss:
        return pl.pallas_call(
            body, name=name, grid=(m // tm,), in_specs=in_specs + [out],
            out_specs=(out, out, pl.BlockSpec((8, LANES), lambda i: (0, 0))),
            out_shape=(jax.ShapeDtypeStruct((m, n), F32), jax.ShapeDtypeStruct((m, n), BF16),
                       jax.ShapeDtypeStruct((8, LANES), F32)),
            compiler_params=_cparams(1),
        )(*args, loss_target)
    outs = pl.pallas_call(
        body, name=name, grid=(m // tm,), in_specs=in_specs, out_specs=tuple([out] * len(out_dtypes)),
        out_shape=tuple(jax.ShapeDtypeStruct((m, n), dt) for dt in out_dtypes), compiler_params=_cparams(1),
    )(*args)
    return outs[0] if len(out_dtypes) == 1 else outs


def _mm_cols(a, b, name, tn=256):
    m, k = a.shape
    n = b.shape[1]

    def body(a_ref, b_ref, o_ref):
        o_ref[...] = jnp.dot(a_ref[...], b_ref[...].astype(BF16), preferred_element_type=F32)

    return pl.pallas_call(
        body, name=name, grid=(n // tn,),
        in_specs=[_resident((m, k), lambda j: (0, 0)), pl.BlockSpec((k, tn), lambda j: (0, j))],
        out_specs=pl.BlockSpec((m, tn), lambda j: (0, j)),
        out_shape=jax.ShapeDtypeStruct((m, n), F32), compiler_params=_cparams(1),
    )(a, b)


def _mm_tn(a, b, name, tile=256):
    k, m = a.shape
    n = b.shape[1]
    dims = (((0,), (0,)), ((), ()))

    def body(a_ref, b_ref, o_ref):
        o_ref[...] = lax.dot_general(a_ref[...].astype(BF16), b_ref[...].astype(BF16), dims, preferred_element_type=F32)

    if n <= m:
        t = min(tile, m)
        grid, a_spec, b_spec = (m // t,), pl.BlockSpec((k, t), lambda i: (0, i)), _resident((k, n), lambda i: (0, 0))
        o_spec = pl.BlockSpec((t, n), lambda i: (i, 0))
    else:
        t = min(tile, n)
        grid, a_spec, b_spec = (n // t,), _resident((k, m), lambda i: (0, 0)), pl.BlockSpec((k, t), lambda i: (0, i))
        o_spec = pl.BlockSpec((m, t), lambda i: (0, i))
    return pl.pallas_call(
        body, name=name, grid=grid, in_specs=[a_spec, b_spec], out_specs=o_spec,
        out_shape=jax.ShapeDtypeStruct((m, n), F32), compiler_params=_cparams(1),
    )(a, b)


def _norm_rope(t, gain, c, s1, s2, seg):
    rs = lax.rsqrt(_seg_sum(t * t, seg) * (1.0 / HEAD_DIM) + EPS)
    return _rope((t * rs) * gain, c, s1, s2)


def _dup_half(y, half):
    lane = lax.broadcasted_iota(jnp.int32, y.shape, 1)
    rolled = pltpu.roll(y, HEAD_DIM, 1)
    keep = (lane < HEAD_DIM) if half == 0 else (lane >= HEAD_DIM)
    return jnp.where(keep, y, rolled)


def _qk_prep(proj, cb0, d, gqa, gq, gk, tabs, name):
    s_len = proj.shape[0]
    tm = 512
    rows = tm // d
    n_units = 4 if gqa else 2 * d
    n_q = 4 if gqa else 2
    n_in = 6

    def body(*refs):
        in_refs = refs[:n_in]
        gq_ref, gk_ref, c_ref, s1_ref, s2_ref, o_ref = refs[n_in:]
        seg = _seg_matrix(HEAD_DIM)

        def rows_of(ref, r):
            return ref[...] if d == 1 else ref[pl.ds(r, rows, stride=d), :]

        def put(unit_col, y):
            o_ref[:, unit_col * LANES:(unit_col + 1) * LANES] = y.astype(BF16)

        for r in range(d):
            c, s1, s2 = rows_of(c_ref, r), rows_of(s1_ref, r), rows_of(s2_ref, r)
            for b in range(n_in):
                t = rows_of(in_refs[b], r)
                if b < n_q:
                    put((b * d + r) if not gqa else b, _norm_rope(t, gq_ref[...], c, s1, s2, seg))
                elif not gqa:
                    sec, pair = (1, b - 2) if b < 4 else (2, b - 4)
                    y = _norm_rope(t, gk_ref[...], c, s1, s2, seg) if sec == 1 else t
                    put(sec * n_units + pair * d + r, y)
                else:
                    sec = 1 if b == 4 else 2
                    y = _norm_rope(t, gk_ref[...], c, s1, s2, seg) if sec == 1 else t
                    for u in range(n_units):
                        put(sec * n_units + u, _dup_half(y, u // 2))

    in_specs = [pl.BlockSpec((tm, LANES), lambda i, b=b: (i, cb0 + b)) for b in range(n_in)]
    vec = pl.BlockSpec((1, LANES), lambda i: (0, 0))
    tab = pl.BlockSpec((tm, LANES), lambda i: (i, 0))
    width = 3 * n_units * LANES
    return pl.pallas_call(
        body, name=name, grid=(s_len // tm,), in_specs=in_specs + [vec, vec, tab, tab, tab],
        out_specs=pl.BlockSpec((rows, width), lambda i: (i, 0)),
        out_shape=jax.ShapeDtypeStruct((s_len // d, width), BF16), compiler_params=_cparams(1),
    )(*([proj] * n_in), gq, gk, *tabs)


def _qk_prep_bwd(dqkv, proj, cb0, d, gqa, gq, gk, tabs, name):
    s_len = proj.shape[0]
    tm = 512
    rows = tm // d
    n_units = 4 if gqa else 2 * d
    n_q = 4 if gqa else 2
    n_in = 6

    def body(*refs):
        d_refs = refs[0:3]
        in_refs = refs[3:3 + n_in]
        gq_ref, gk_ref, c_ref, s1_ref, s2_ref, o_ref, dgq_ref, dgk_ref, stage = refs[3 + n_in:]
        seg = _seg_matrix(HEAD_DIM)

        @pl.when(pl.program_id(0) == 0)
        def _():
            dgq_ref[...] = jnp.zeros_like(dgq_ref)
            dgk_ref[...] = jnp.zeros_like(dgk_ref)

        def rows_of(ref, r):
            return ref[...] if d == 1 else ref[pl.ds(r, rows, stride=d), :]

        def unit(col):
            sec, u = divmod(col, n_units)
            return d_refs[sec][:, u * LANES:(u + 1) * LANES]

        def norm_bwd(dyr, t, gain, c, s1, s2, dg_ref):
            rs = lax.rsqrt(_seg_sum(t * t, seg) * (1.0 / HEAD_DIM) + EPS)
            that = t * rs
            dy = _unrope(dyr, c, s1, s2)
            dg_ref[...] += jnp.sum(dy * that, axis=0, keepdims=True)
            dthat = dy * gain
            return rs * (dthat - that * (_seg_sum(dthat * that, seg) * (1.0 / HEAD_DIM)))

        def fold(sec):
            tot = []
            for u in range(n_units):
                v = unit(sec * n_units + u)
                tot.append(v + pltpu.roll(v, HEAD_DIM, 1))
            lane = lax.broadcasted_iota(jnp.int32, tot[0].shape, 1)
            return jnp.where(lane < HEAD_DIM, tot[0] + tot[1], tot[2] + tot[3])

        for b in range(n_in):
            for r in range(d):
                c, s1, s2 = rows_of(c_ref, r), rows_of(s1_ref, r), rows_of(s2_ref, r)
                t = rows_of(in_refs[b], r)
                if b < n_q:
                    g = unit((b * d + r) if not gqa else b)
                    out = norm_bwd(g, t, gq_ref[...], c, s1, s2, dgq_ref)
                elif not gqa:
                    sec, pair = (1, b - 2) if b < 4 else (2, b - 4)
                    g = unit(sec * n_units + pair * d + r)
                    out = norm_bwd(g, t, gk_ref[...], c, s1, s2, dgk_ref) if sec == 1 else g
                else:
                    sec = 1 if b == 4 else 2
                    g = fold(sec)
                    out = norm_bwd(g, t, gk_ref[...], c, s1, s2, dgk_ref) if sec == 1 else g
                if d == 1:
                    o_ref[:, b * LANES:(b + 1) * LANES] = out.astype(BF16)
                else:
                    stage[pl.ds(r, rows, stride=d), :] = out
            if d != 1:
                o_ref[:, b * LANES:(b + 1) * LANES] = stage[...].astype(BF16)

    in_specs = [pl.BlockSpec((rows, n_units * LANES), lambda i: (i, 0))] * 3
    in_specs += [pl.BlockSpec((tm, LANES), lambda i, b=b: (i, cb0 + b)) for b in range(n_in)]
    vec = pl.BlockSpec((1, LANES), lambda i: (0, 0))
    tab = pl.BlockSpec((tm, LANES), lambda i: (i, 0))
    return pl.pallas_call(
        body, name=name, grid=(s_len // tm,), in_specs=in_specs + [vec, vec, tab, tab, tab],
        out_specs=(pl.BlockSpec((tm, n_in * LANES), lambda i: (i, 0)), vec, vec),
        out_shape=(jax.ShapeDtypeStruct((s_len, n_in * LANES), BF16), jax.ShapeDtypeStruct((1, LANES), F32),
                   jax.ShapeDtypeStruct((1, LANES), F32)),
        scratch_shapes=[pltpu.VMEM((tm, LANES), F32)], compiler_params=_cparams(1),
    )(*dqkv, *([proj] * n_in), gq, gk, *tabs)


def _head_masks(shape):
    lane = lax.broadcasted_iota(jnp.int32, shape, 1)
    return lane < HEAD_DIM, lane >= HEAD_DIM


def _band_fwd(qkv, n_units, max_dist, sinks, name):
    n_rows = qkv.shape[0]
    nb = n_rows // BLOCK
    scale = HEAD_DIM ** -0.5
    has_sink = sinks is not None
    assert not has_sink or max_dist < BLOCK

    qn, un = min(nb, BAND_Q_BLOCKS), BAND_UNITS
    ug = n_units // un

    def body(*refs):
        q_ref, kp_ref, km_ref, vp_ref, vm_ref = refs[:5]
        o_ref, lse_ref = refs[-2:]
        i = pl.program_id(1)
        qi = lax.broadcasted_iota(jnp.int32, (BLOCK, 2 * BLOCK), 0)
        kj = lax.broadcasted_iota(jnp.int32, (BLOCK, 2 * BLOCK), 1)
        dist = qi + BLOCK - kj
        band = (dist >= 0) & (dist <= max_dist)
        band_first = band & ((i > 0) | (kj >= BLOCK))
        m0, m1 = _head_masks((BLOCK, LANES))
        zero = jnp.zeros((BLOCK, LANES), BF16)
        for ub in range(un):
            cs = slice(ub * LANES, (ub + 1) * LANES)
            for qb in range(qn):
                rs = slice(qb * BLOCK, (qb + 1) * BLOCK)
                q = q_ref[rs, cs]
                if qb == 0:
                    kk = jnp.concatenate([kp_ref[:, cs], km_ref[0:BLOCK, cs]], axis=0)
                    vv = jnp.concatenate([vp_ref[:, cs], vm_ref[0:BLOCK, cs]], axis=0)
                    valid = band_first
                else:
                    kk = km_ref[(qb - 1) * BLOCK:(qb + 1) * BLOCK, cs]
                    vv = vm_ref[(qb - 1) * BLOCK:(qb + 1) * BLOCK, cs]
                    valid = band
                outs, lses = [], []
                for e, hm in enumerate((m0, m1)):
                    qe = jnp.where(hm, q, zero)
                    s = lax.dot_general(qe, kk, (((1,), (1,)), ((), ())), preferred_element_type=F32) * scale
                    s = jnp.where(valid, s, -jnp.inf)
                    if has_sink:
                        s = jnp.where(kj == 0, refs[5][ub][:, e * HEAD_DIM:e * HEAD_DIM + 1], s)
                    mx = jnp.max(s, axis=-1, keepdims=True)
                    p = jnp.exp(s - mx)
                    den = jnp.sum(p, axis=-1, keepdims=True)
                    pn = p * (1.0 / den)
                    if has_sink:
                        pn = jnp.where(kj == 0, 0.0, pn)
                    pn = pn.astype(BF16)
                    outs.append(jnp.dot(pn, vv, preferred_element_type=F32))
                    lses.append(mx + jnp.log(den))
                o_ref[rs, cs] = jnp.where(m0, outs[0], outs[1])
                lse_ref[rs, cs] = jnp.where(m0, jnp.broadcast_to(lses[0], (BLOCK, LANES)),
                                            jnp.broadcast_to(lses[1], (BLOCK, LANES)))

    def main(sec):
        return pl.BlockSpec((qn * BLOCK, un * LANES), lambda u, i: (i, sec * ug + u))

    def prev(sec):
        return pl.BlockSpec((BLOCK, un * LANES), lambda u, i: (jnp.maximum(i * qn - 1, 0), sec * ug + u))

    in_specs = [main(0), prev(1), main(1), prev(2), main(2)]
    args = [qkv] * 5
    if has_sink:
        in_specs.append(pl.BlockSpec((un, 1, LANES), lambda u, i: (u, 0, 0)))
        args.append(sinks)
    return pl.pallas_call(
        body, name=name, grid=(ug, nb // qn), in_specs=in_specs, out_specs=(main(0), main(0)),
        out_shape=(jax.ShapeDtypeStruct((n_rows, n_units * LANES), F32),) * 2, compiler_params=_cparams(2),
    )(*args)


def _band_bwd(qkv, do, lse, delta, n_units, max_dist, name):
    n_rows = qkv.shape[0]
    nb = n_rows // BLOCK
    scale = HEAD_DIM ** -0.5

    qn, un = min(nb, BAND_Q_BLOCKS), BAND_UNITS
    ug = n_units // un
    steps = nb // qn
    nt_dims = (((1,), (1,)), ((), ()))
    tn_dims = (((0,), (0,)), ((), ()))

    def body(qm_ref, qx_ref, kp_ref, km_ref, vp_ref, vm_ref, dom_ref, dox_ref, lm_ref, lx_ref, dm_ref, dx_ref,
             dq_ref, dk_ref, dv_ref):
        i = pl.program_id(1)
        m0, m1 = _head_masks((BLOCK, LANES))
        zero = jnp.zeros((BLOCK, LANES), BF16)
        qi = lax.broadcasted_iota(jnp.int32, (BLOCK, 2 * BLOCK), 0)
        kj = lax.broadcasted_iota(jnp.int32, (BLOCK, 2 * BLOCK), 1)
        dist = qi + BLOCK - kj
        band = (dist >= 0) & (dist <= max_dist)
        band_first = band & ((i > 0) | (kj >= BLOCK))
        qr = lax.broadcasted_iota(jnp.int32, (2 * BLOCK, BLOCK), 0)
        kc = lax.broadcasted_iota(jnp.int32, (2 * BLOCK, BLOCK), 1)
        dist2 = qr - kc
        band2 = (dist2 >= 0) & (dist2 <= max_dist)
        band2_last = band2 & ((qr < BLOCK) | (i < steps - 1))
        m0w, m1w = _head_masks((2 * BLOCK, LANES))
        zero2 = jnp.zeros((2 * BLOCK, LANES), BF16)

        def two(main_ref, next_ref, kb, cs):
            if kb < qn - 1:
                return main_ref[kb * BLOCK:(kb + 2) * BLOCK, cs]
            return jnp.concatenate([main_ref[kb * BLOCK:(kb + 1) * BLOCK, cs], next_ref[:, cs]], axis=0)

        for ub in range(un):
            cs = slice(ub * LANES, (ub + 1) * LANES)
            for qb in range(qn):
                rs = slice(qb * BLOCK, (qb + 1) * BLOCK)
                q = qm_ref[rs, cs]
                dob = dom_ref[rs, cs]
                lse_b = lm_ref[rs, cs]
                del_b = dm_ref[rs, cs]
                if qb == 0:
                    kk = jnp.concatenate([kp_ref[:, cs], km_ref[0:BLOCK, cs]], axis=0)
                    vv = jnp.concatenate([vp_ref[:, cs], vm_ref[0:BLOCK, cs]], axis=0)
                    valid = band_first
                else:
                    kk = km_ref[(qb - 1) * BLOCK:(qb + 1) * BLOCK, cs]
                    vv = vm_ref[(qb - 1) * BLOCK:(qb + 1) * BLOCK, cs]
                    valid = band
                dqs = []
                for e, hm in enumerate((m0, m1)):
                    col = slice(e * HEAD_DIM, e * HEAD_DIM + 1)
                    s = lax.dot_general(jnp.where(hm, q, zero), kk, nt_dims, preferred_element_type=F32) * scale
                    p = jnp.where(valid, jnp.exp(s - lse_b[:, col]), 0.0)
                    dp = lax.dot_general(jnp.where(hm, dob, zero), vv, nt_dims, preferred_element_type=F32)
                    ds = (p * (dp - del_b[:, col]) * scale).astype(BF16)
                    dqs.append(jnp.dot(ds, kk, preferred_element_type=F32))
                dq_ref[rs, cs] = jnp.where(m0, dqs[0], dqs[1])
            for kb in range(qn):
                rs = slice(kb * BLOCK, (kb + 1) * BLOCK)
                qq = two(qm_ref, qx_ref, kb, cs)
                dd = two(dom_ref, dox_ref, kb, cs)
                ll = two(lm_ref, lx_ref, kb, cs)
                de = two(dm_ref, dx_ref, kb, cs)
                k = km_ref[rs, cs]
                v = vm_ref[rs, cs]
                valid2 = band2 if kb < qn - 1 else band2_last
                dk = jnp.zeros((BLOCK, LANES), F32)
                dv = jnp.zeros((BLOCK, LANES), F32)
                for e, hm in enumerate((m0w, m1w)):
                    col = slice(e * HEAD_DIM, e * HEAD_DIM + 1)
                    qe = jnp.where(hm, qq, zero2)
                    doe = jnp.where(hm, dd, zero2)
                    s = lax.dot_general(qe, k, nt_dims, preferred_element_type=F32) * scale
                    p = jnp.where(valid2, jnp.exp(s - ll[:, col]), 0.0)
                    dp = lax.dot_general(doe, v, nt_dims, preferred_element_type=F32)
                    ds = (p * (dp - de[:, col]) * scale).astype(BF16)
                    dk = dk + lax.dot_general(ds, qe, tn_dims, preferred_element_type=F32)
                    dv = dv + lax.dot_general(p.astype(BF16), doe, tn_dims, preferred_element_type=F32)
                dk_ref[rs, cs] = dk
                dv_ref[rs, cs] = dv

    def main(sec):
        return pl.BlockSpec((qn * BLOCK, un * LANES), lambda u, i: (i, sec * ug + u))

    def prev(sec):
        return pl.BlockSpec((BLOCK, un * LANES), lambda u, i: (jnp.maximum(i * qn - 1, 0), sec * ug + u))

    def nxt(sec):
        return pl.BlockSpec((BLOCK, un * LANES), lambda u, i: (jnp.minimum((i + 1) * qn, nb - 1), sec * ug + u))

    in_specs = [main(0), nxt(0), prev(1), main(1), prev(2), main(2),
                main(0), nxt(0), main(0), nxt(0), main(0), nxt(0)]
    args = [qkv] * 6 + [do, do, lse, lse, delta, delta]
    shp = jax.ShapeDtypeStruct((n_rows, n_units * LANES), F32)
    return pl.pallas_call(
        body, name=name, grid=(ug, steps), in_specs=in_specs, out_specs=(main(0), main(0), main(0)),
        out_shape=(shp, shp, shp), compiler_params=_cparams(2),
    )(*args)


def _merge_groups(os_, lses, dils, name):
    s_len = os_[0].shape[0] * dils[0]
    tm = 512

    def body(*refs):
        o_refs, l_refs = refs[0:3], refs[3:6]
        o_ref, lse_ref = refs[6:8]
        so, sl = refs[8:11], refs[11:14]
        for pair in range(2):
            for g, d in enumerate(dils):
                rows = tm // d
                for r in range(d):
                    col = slice((pair * d + r) * LANES, (pair * d + r + 1) * LANES)
                    if d == 1:
                        so[g][...] = o_refs[g][:, col]
                        sl[g][...] = l_refs[g][:, col]
                    else:
                        so[g][pl.ds(r, rows, stride=d), :] = o_refs[g][:, col]
                        sl[g][pl.ds(r, rows, stride=d), :] = l_refs[g][:, col]
            l0, l1, l2 = sl[0][...], sl[1][...], sl[2][...]
            mx = jnp.maximum(jnp.maximum(l0, l1), l2)
            e0, e1, e2 = jnp.exp(l0 - mx), jnp.exp(l1 - mx), jnp.exp(l2 - mx)
            den = e0 + e1 + e2
            inv = 1.0 / den
            o_ref[:, pair * LANES:(pair + 1) * LANES] = (so[0][...] * (e0 * inv) + so[1][...] * (e1 * inv)
                                                         + so[2][...] * (e2 * inv))
            lse_ref[:, pair * LANES:(pair + 1) * LANES] = mx + jnp.log(den)

    in_specs = [pl.BlockSpec((tm // d, 2 * d * LANES), lambda i: (i, 0)) for d in dils] * 2
    out = pl.BlockSpec((tm, 2 * LANES), lambda i: (i, 0))
    shp = jax.ShapeDtypeStruct((s_len, 2 * LANES), F32)
    return pl.pallas_call(
        body, name=name, grid=(s_len // tm,), in_specs=in_specs, out_specs=(out, out), out_shape=(shp, shp),
        scratch_shapes=[pltpu.VMEM((tm, LANES), F32)] * 6, compiler_params=_cparams(1),
    )(*os_, *lses)


def _bwd_prep(do, o, lse, dils, sinks, name):
    s_len, width = do.shape
    n_pairs = width // LANES
    tm = 512
    has_sink = sinks is not None
    n_g = len(dils)

    def body(*refs):
        do_ref, o_ref, lse_ref = refs[:3]
        pos = 3
        if has_sink:
            sink_ref = refs[pos]
            pos += 1
        outs = refs[pos:pos + 3 * n_g]
        pos += 3 * n_g
        if has_sink:
            dsink_ref = refs[pos]
            pos += 1
        s_do, s_l, s_d = refs[pos:pos + 3]
        seg = _seg_matrix(HEAD_DIM)

        if has_sink:
            @pl.when(pl.program_id(0) == 0)
            def _():
                dsink_ref[...] = jnp.zeros_like(dsink_ref)

        for pair in range(n_pairs):
            col = slice(pair * LANES, (pair + 1) * LANES)
            dov = do_ref[:, col]
            lv = lse_ref[:, col]
            delta = _seg_sum(dov * o_ref[:, col], seg)
            if has_sink:
                dsink_ref[pair] += -jnp.sum(jnp.exp(sink_ref[pair] - lv) * delta, axis=0, keepdims=True)
            s_do[...] = dov
            s_l[...] = lv
            s_d[...] = delta
            for g, d in enumerate(dils):
                rows = tm // d
                for r in range(d):
                    oc = slice((pair * d + r) * LANES, (pair * d + r + 1) * LANES)
                    if d == 1:
                        a, b, c = s_do[...], s_l[...], s_d[...]
                    else:
                        a = s_do[pl.ds(r, rows, stride=d), :]
                        b = s_l[pl.ds(r, rows, stride=d), :]
                        c = s_d[pl.ds(r, rows, stride=d), :]
                    outs[3 * g][:, oc] = a.astype(BF16)
                    outs[3 * g + 1][:, oc] = b
                    outs[3 * g + 2][:, oc] = c

    row = pl.BlockSpec((tm, width), lambda i: (i, 0))
    in_specs = [row, row, row]
    args = [do, o, lse]
    if has_sink:
        in_specs.append(pl.BlockSpec((n_pairs, 1, LANES), lambda i: (0, 0, 0)))
        args.append(sinks)
    out_specs, out_shape = [], []
    for d in dils:
        for dt in (BF16, F32, F32):
            out_specs.append(pl.BlockSpec((tm // d, n_pairs * d * LANES), lambda i: (i, 0)))
            out_shape.append(jax.ShapeDtypeStruct((s_len // d, n_pairs * d * LANES), dt))
    if has_sink:
        out_specs.append(pl.BlockSpec((n_pairs, 1, LANES), lambda i: (0, 0, 0)))
        out_shape.append(jax.ShapeDtypeStruct((n_pairs, 1, LANES), F32))
    return pl.pallas_call(
        body, name=name, grid=(s_len // tm,), in_specs=in_specs, out_specs=tuple(out_specs),
        out_shape=tuple(out_shape), scratch_shapes=[pltpu.VMEM((tm, LANES), F32)] * 3, compiler_params=_cparams(1),
    )(*args)


def _mem_kv(mem, mem_gain, w_kv, k_gain, name):
    m_len = mem.shape[0]
    kw = M_HEADS * M_HEAD_DIM

    def body(mem_ref, mg_ref, w_ref, kg_ref, k_ref, v_ref):
        mv = mem_ref[...]
        r = lax.rsqrt(jnp.mean(mv * mv, axis=-1, keepdims=True) + EPS)
        mn = ((mv * r) * mg_ref[...]).astype(BF16)
        kv = jnp.dot(mn, w_ref[...], preferred_element_type=F32)
        for h in range(M_HEADS):
            col = slice(h * M_HEAD_DIM, (h + 1) * M_HEAD_DIM)
            t = kv[:, col]
            rk = lax.rsqrt(jnp.mean(t * t, axis=-1, keepdims=True) + EPS)
            k_ref[:, col] = ((t * rk) * kg_ref[...]).astype(BF16)
        v_ref[...] = kv[:, kw:].astype(BF16)

    shp = jax.ShapeDtypeStruct((m_len, kw), BF16)
    return pl.pallas_call(body, name=name, out_shape=(shp, shp),
                          compiler_params=pltpu.CompilerParams(vmem_limit_bytes=VMEM_LIMIT_BYTES))(mem, mem_gain, w_kv, k_gain)


def _mem_kv_bwd(mem, mem_gain, w_kv, k_gain, dk, dv, name):
    m_len, d = mem.shape
    kw = M_HEADS * M_HEAD_DIM

    def body(mem_ref, mg_ref, w_ref, kg_ref, dk_ref, dv_ref, dw_ref, dmg_ref, dkg_ref, dkv_ref):
        mv = mem_ref[...]
        r = lax.rsqrt(jnp.mean(mv * mv, axis=-1, keepdims=True) + EPS)
        mhat = mv * r
        mn = (mhat * mg_ref[...]).astype(BF16)
        kv = jnp.dot(mn, w_ref[...], preferred_element_type=F32)
        dkg = jnp.zeros((1, M_HEAD_DIM), F32)
        for h in range(M_HEADS):
            col = slice(h * M_HEAD_DIM, (h + 1) * M_HEAD_DIM)
            t = kv[:, col]
            rk = lax.rsqrt(jnp.mean(t * t, axis=-1, keepdims=True) + EPS)
            that = t * rk
            dy = dk_ref[:, col]
            dkg = dkg + jnp.sum(dy * that, axis=0, keepdims=True)
            dthat = dy * kg_ref[...]
            dkv_ref[:, col] = (rk * (dthat - that * jnp.mean(dthat * that, axis=-1, keepdims=True))).astype(BF16)
        dkv_ref[:, kw:] = dv_ref[...].astype(BF16)
        dkg_ref[...] = dkg
        dkv = dkv_ref[...]
        dw_ref[...] = lax.dot_general(mn, dkv, (((0,), (0,)), ((), ())), preferred_element_type=F32)
        dmn = lax.dot_general(dkv, w_ref[...], (((1,), (1,)), ((), ())), preferred_element_type=F32)
        dmg_ref[...] = jnp.sum(dmn * mhat, axis=0, keepdims=True)

    return pl.pallas_call(
        body, name=name,
        out_shape=(jax.ShapeDtypeStruct((d, 2 * kw), F32), jax.ShapeDtypeStruct((1, d), F32),
                   jax.ShapeDtypeStruct((1, M_HEAD_DIM), F32)),
        scratch_shapes=[pltpu.VMEM((m_len, 2 * kw), BF16)],
        compiler_params=pltpu.CompilerParams(vmem_limit_bytes=VMEM_LIMIT_BYTES),
    )(mem, mem_gain, w_kv, k_gain, dk, dv)


def _mem_attn_fwd(proj, cidx, mk, mv, q_gain, name):
    s_len = proj.shape[0]
    kw = M_HEADS * M_HEAD_DIM
    tm = 512
    scale = M_HEAD_DIM ** -0.5

    def body(q_ref, k_ref, v_ref, g_ref, o_ref):
        for h in range(M_HEADS):
            col = slice(h * M_HEAD_DIM, (h + 1) * M_HEAD_DIM)
            t = q_ref[:, col]
            rs = lax.rsqrt(jnp.mean(t * t, axis=-1, keepdims=True) + EPS)
            qn = ((t * rs) * g_ref[...]).astype(BF16)
            s = lax.dot_general(qn, k_ref[:, col], (((1,), (1,)), ((), ())), preferred_element_type=F32) * scale
            mx = jnp.max(s, axis=-1, keepdims=True)
            p = jnp.exp(s - mx)
            pn = (p * (1.0 / jnp.sum(p, axis=-1, keepdims=True))).astype(BF16)
            o_ref[:, col] = jnp.dot(pn, v_ref[:, col], preferred_element_type=F32).astype(BF16)

    whole = pl.BlockSpec((MEM_LEN, kw), lambda i: (0, 0))
    return pl.pallas_call(
        body, name=name, grid=(s_len // tm,),
        in_specs=[pl.BlockSpec((tm, kw), lambda i: (i, cidx)), whole, whole, pl.BlockSpec((1, M_HEAD_DIM), lambda i: (0, 0))],
        out_specs=pl.BlockSpec((tm, kw), lambda i: (i, 0)),
        out_shape=jax.ShapeDtypeStruct((s_len, kw), BF16), compiler_params=_cparams(1),
    )(proj, mk, mv, q_gain)


def _mem_attn_bwd(proj, cidx, mk, mv, q_gain, do, name):
    s_len = proj.shape[0]
    kw = M_HEADS * M_HEAD_DIM
    tm = 512
    scale = M_HEAD_DIM ** -0.5

    def body(q_ref, k_ref, v_ref, g_ref, do_ref, dq_ref, dk_ref, dv_ref, dg_ref):
        @pl.when(pl.program_id(0) == 0)
        def _():
            dk_ref[...] = jnp.zeros_like(dk_ref)
            dv_ref[...] = jnp.zeros_like(dv_ref)
            dg_ref[...] = jnp.zeros_like(dg_ref)

        for h in range(M_HEADS):
            col = slice(h * M_HEAD_DIM, (h + 1) * M_HEAD_DIM)
            t = q_ref[:, col]
            rs = lax.rsqrt(jnp.mean(t * t, axis=-1, keepdims=True) + EPS)
            that = t * rs
            qn = (that * g_ref[...]).astype(BF16)
            kh, vh = k_ref[:, col], v_ref[:, col]
            dob = do_ref[:, col].astype(BF16)
            s = lax.dot_general(qn, kh, (((1,), (1,)), ((), ())), preferred_element_type=F32) * scale
            mx = jnp.max(s, axis=-1, keepdims=True)
            p = jnp.exp(s - mx)
            p = p * (1.0 / jnp.sum(p, axis=-1, keepdims=True))
            dp = lax.dot_general(dob, vh, (((1,), (1,)), ((), ())), preferred_element_type=F32)
            ds = (p * (dp - jnp.sum(p * dp, axis=-1, keepdims=True)) * scale).astype(BF16)
            dqn = jnp.dot(ds, kh, preferred_element_type=F32)
            dk_ref[:, col] += lax.dot_general(ds, qn, (((0,), (0,)), ((), ())), preferred_element_type=F32)
            dv_ref[:, col] += lax.dot_general(p.astype(BF16), dob, (((0,), (0,)), ((), ())), preferred_element_type=F32)
            dg_ref[...] += jnp.sum(dqn * that, axis=0, keepdims=True)
            dthat = dqn * g_ref[...]
            dq_ref[:, col] = (rs * (dthat - that * jnp.mean(dthat * that, axis=-1, keepdims=True))).astype(BF16)

    whole = pl.BlockSpec((MEM_LEN, kw), lambda i: (0, 0))
    vec = pl.BlockSpec((1, M_HEAD_DIM), lambda i: (0, 0))
    row = pl.BlockSpec((tm, kw), lambda i: (i, 0))
    return pl.pallas_call(
        body, name=name, grid=(s_len // tm,),
        in_specs=[pl.BlockSpec((tm, kw), lambda i: (i, cidx)), whole, whole, vec, row],
        out_specs=(row, whole, whole, vec),
        out_shape=(jax.ShapeDtypeStruct((s_len, kw), BF16), jax.ShapeDtypeStruct((MEM_LEN, kw), F32),
                   jax.ShapeDtypeStruct((MEM_LEN, kw), F32), jax.ShapeDtypeStruct((1, M_HEAD_DIM), F32)),
        compiler_params=_cparams(1),
    )(proj, mk, mv, q_gain, do)


def _gate_merge(gates, pa, pb, pm, name):
    s_len, d = pa.shape
    tm = 256

    def body(g_ref, a_ref, b_ref, m_ref, o_ref, ot_ref):
        f = lambda v: v.astype(F32)
        merged = (f(g_ref[:, 0:d]) * f(a_ref[...]) + f(g_ref[:, d:2 * d]) * f(b_ref[...])
                  + f(g_ref[:, 2 * d:3 * d]) * f(m_ref[...])).astype(BF16)
        o_ref[...] = merged
        ot_ref[...] = merged.T

    row = pl.BlockSpec((tm, d), lambda i: (i, 0))
    return pl.pallas_call(
        body, name=name, grid=(s_len // tm,), in_specs=[pl.BlockSpec((tm, 3 * d), lambda i: (i, 0)), row, row, row],
        out_specs=(row, pl.BlockSpec((d, tm), lambda i: (0, i))),
        out_shape=(jax.ShapeDtypeStruct((s_len, d), BF16), jax.ShapeDtypeStruct((d, s_len), BF16)),
        compiler_params=_cparams(1),
    )(gates, pa, pb, pm)


def _gate_merge_bwd(dmerged, gates, pa, pb, pm, name):
    s_len, d = pa.shape
    tm = 256

    def body(dm_ref, g_ref, a_ref, b_ref, m_ref, da_ref, db_ref, dmm_ref, dg_ref, dbg_ref):
        @pl.when(pl.program_id(0) == 0)
        def _():
            dbg_ref[...] = jnp.zeros_like(dbg_ref)
        dm = dm_ref[...]
        for k, (p_ref, dp_ref) in enumerate(((a_ref, da_ref), (b_ref, db_ref), (m_ref, dmm_ref))):
            col = slice(k * d, (k + 1) * d)
            g = g_ref[:, col].astype(F32)
            dp_ref[...] = (dm * g).astype(BF16)
            dpre = (dm * p_ref[...].astype(F32)) * (g * (1.0 - g))
            dbg_ref[:, col] += jnp.sum(dpre, axis=0, keepdims=True)
            dg_ref[:, col] = dpre.astype(BF16)

    row = pl.BlockSpec((tm, d), lambda i: (i, 0))
    wide = pl.BlockSpec((tm, 3 * d), lambda i: (i, 0))
    shp = jax.ShapeDtypeStruct((s_len, d), BF16)
    return pl.pallas_call(
        body, name=name, grid=(s_len // tm,), in_specs=[row, wide, row, row, row],
        out_specs=(row, row, row, wide, pl.BlockSpec((1, 3 * d), lambda i: (0, 0))),
        out_shape=(shp, shp, shp, jax.ShapeDtypeStruct((s_len, 3 * d), BF16), jax.ShapeDtypeStruct((1, 3 * d), F32)),
        compiler_params=_cparams(1),
    )(dmerged, gates, pa, pb, pm)


CONV_CHUNK = 256


def _pick_row(tile, j):
    row = lax.broadcasted_iota(jnp.int32, tile.shape, 0)
    return jnp.sum(jnp.where(row == j, tile, jnp.zeros_like(tile)), axis=0, keepdims=True)


def _rows_before(ref, start, k):
    cur = ref[pl.ds(start, CONV_CHUNK), :].astype(F32)
    prev = ref[pl.ds(pl.multiple_of(jnp.maximum(start - 16, 0), 16), 16), :].astype(F32)
    prev = jnp.where(start > 0, prev, jnp.zeros_like(prev))
    rolled = pltpu.roll(cur, k, 0)
    row = lax.broadcasted_iota(jnp.int32, cur.shape, 0)
    for j in range(k):
        rolled = jnp.where(row == j, _pick_row(prev, 16 - k + j), rolled)
    return rolled


def _rows_after(ref, start, k):
    cur = ref[pl.ds(start, CONV_CHUNK), :]
    nxt = ref[pl.ds(pl.multiple_of(start + CONV_CHUNK, 8), 8), :]
    rolled = pltpu.roll(cur, CONV_CHUNK - k, 0)
    row = lax.broadcasted_iota(jnp.int32, cur.shape, 0)
    for j in range(k):
        rolled = jnp.where(row == CONV_CHUNK - k + j, _pick_row(nxt, j), rolled)
    return rolled


def _conv_pre(u_ref, w_ref, b_ref, start):
    u2 = _rows_before(u_ref, start, 2)
    u1 = _rows_before(u_ref, start, 1)
    u0 = u_ref[pl.ds(start, CONV_CHUNK), :].astype(F32)
    c = ((b_ref[...] + w_ref[0:1, :] * u2) + w_ref[1:2, :] * u1) + w_ref[2:3, :] * u0
    return c, (u2, u1, u0)


def _conv_glu(u, conv_w, conv_b, name):
    s_len = u.shape[0]
    nblk = D_FF // LANES

    def body(ua_ref, ug_ref, wa_ref, wg_ref, ba_ref, bg_ref, o_ref, ot_ref):
        def chunk(ci, carry):
            start = pl.multiple_of(ci * CONV_CHUNK, CONV_CHUNK)
            ca, _ = _conv_pre(ua_ref, wa_ref, ba_ref, start)
            cg, _ = _conv_pre(ug_ref, wg_ref, bg_ref, start)
            act = ((ca * _sigmoid(ca)) * cg).astype(BF16)
            o_ref[pl.ds(start, CONV_CHUNK), :] = act
            ot_ref[:, pl.ds(start, CONV_CHUNK)] = act.T
            return carry
        lax.fori_loop(0, s_len // CONV_CHUNK, chunk, 0)

    def col(rows, off):
        return pl.BlockSpec((rows, LANES), lambda j: (0, off + j))

    return pl.pallas_call(
        body, name=name, grid=(nblk,),
        in_specs=[col(s_len, 0), col(s_len, nblk), col(3, 0), col(3, nblk), col(1, 0), col(1, nblk)],
        out_specs=(col(s_len, 0), pl.BlockSpec((LANES, s_len), lambda j: (j, 0))),
        out_shape=(jax.ShapeDtypeStruct((s_len, D_FF), BF16), jax.ShapeDtypeStruct((D_FF, s_len), BF16)),
        compiler_params=_cparams(1),
    )(u, u, conv_w, conv_w, conv_b, conv_b)


def _conv_glu_bwd(dact, u, conv_w, conv_b, name):
    s_len = u.shape[0]
    nblk = D_FF // LANES
    n_chunks = s_len // CONV_CHUNK

    def body(da_ref, ua_ref, ug_ref, wa_ref, wg_ref, ba_ref, bg_ref,
             dua_ref, dug_ref, dwa_ref, dwg_ref, dba_ref, dbg_ref, sa, sg):
        sa[pl.ds(s_len, 8), :] = jnp.zeros((8, LANES), F32)
        sg[pl.ds(s_len, 8), :] = jnp.zeros((8, LANES), F32)
        zero = jnp.zeros((1, LANES), F32)

        def chunk1(ci, carry):
            start = pl.multiple_of(ci * CONV_CHUNK, CONV_CHUNK)
            ca, ua = _conv_pre(ua_ref, wa_ref, ba_ref, start)
            cg, ug = _conv_pre(ug_ref, wg_ref, bg_ref, start)
            dact_v = da_ref[pl.ds(start, CONV_CHUNK), :].astype(F32)
            sig = _sigmoid(ca)
            dcg = dact_v * (ca * sig)
            dca = (dact_v * cg) * (sig * (1.0 + ca * (1.0 - sig)))
            sa[pl.ds(start, CONV_CHUNK), :] = dca
            sg[pl.ds(start, CONV_CHUNK), :] = dcg
            out = [carry[0] + jnp.sum(dca, axis=0, keepdims=True), carry[1] + jnp.sum(dcg, axis=0, keepdims=True)]
            for j in range(3):
                out.append(carry[2 + j] + jnp.sum(dca * ua[j], axis=0, keepdims=True))
            for j in range(3):
                out.append(carry[5 + j] + jnp.sum(dcg * ug[j], axis=0, keepdims=True))
            return tuple(out)

        acc = lax.fori_loop(0, n_chunks, chunk1, (zero,) * 8)
        dba_ref[...] = acc[0]
        dbg_ref[...] = acc[1]
        for j in range(3):
            dwa_ref[j:j + 1, :] = acc[2 + j]
            dwg_ref[j:j + 1, :] = acc[5 + j]

        def chunk2(ci, carry):
            start = pl.multiple_of(ci * CONV_CHUNK, CONV_CHUNK)
            for s_ref, w_ref, o_ref in ((sa, wa_ref, dua_ref), (sg, wg_ref, dug_ref)):
                d0 = s_ref[pl.ds(start, CONV_CHUNK), :]
                d1 = _rows_after(s_ref, start, 1)
                d2 = _rows_after(s_ref, start, 2)
                o_ref[pl.ds(start, CONV_CHUNK), :] = (w_ref[2:3, :] * d0 + w_ref[1:2, :] * d1
                                                      + w_ref[0:1, :] * d2).astype(BF16)
            return carry
        lax.fori_loop(0, n_chunks, chunk2, 0)

    def col(rows, off):
        return pl.BlockSpec((rows, LANES), lambda j: (0, off + j))

    big = jax.ShapeDtypeStruct((s_len, D_FF), BF16)
    return pl.pallas_call(
        body, name=name, grid=(nblk,),
        in_specs=[col(s_len, 0), col(s_len, 0), col(s_len, nblk), col(3, 0), col(3, nblk), col(1, 0), col(1, nblk)],
        out_specs=(col(s_len, 0), col(s_len, 0), col(3, 0), col(3, 0), col(1, 0), col(1, 0)),
        out_shape=(big, big, jax.ShapeDtypeStruct((3, D_FF), F32), jax.ShapeDtypeStruct((3, D_FF), F32),
                   jax.ShapeDtypeStruct((1, D_FF), F32), jax.ShapeDtypeStruct((1, D_FF), F32)),
        scratch_shapes=[pltpu.VMEM((s_len + 8, LANES), F32)] * 2, compiler_params=_cparams(1),
    )(dact, u, u, conv_w, conv_w, conv_b, conv_b)


def _rope_tables(positions):
    half = ROPE_DIMS // 2
    freqs = jnp.exp(jnp.arange(half, dtype=F32) * (-2.0 * math.log(ROPE_THETA) / ROPE_DIMS))
    ang = positions.reshape(-1).astype(F32)[:, None] * freqs
    cos, sin = jnp.cos(ang), jnp.sin(ang)
    n = ang.shape[0]
    zeros = lambda w: jnp.zeros((n, w), F32)
    c = jnp.concatenate([cos, cos, jnp.ones((n, HEAD_DIM - ROPE_DIMS), F32)], axis=1)
    s1 = jnp.concatenate([-sin, zeros(HEAD_DIM - half)], axis=1)
    s2 = jnp.concatenate([zeros(half), sin, zeros(HEAD_DIM - ROPE_DIMS)], axis=1)
    return tuple(jnp.tile(t, (1, 2)) for t in (c, s1, s2))


def _two(v):
    return jnp.tile(v.reshape(1, HEAD_DIM), (1, 2))


def _fold_heads(g):
    return g[0, :HEAD_DIM] + g[0, HEAD_DIM:]


MIX_WEIGHTS = ('w_gate', 'w_mem_kv', 'w_o_a', 'w_o_b', 'w_o_m', 'w_out')
FFN_WEIGHTS = ('w_up', 'conv_w', 'w_down')


def _device_step(x, mem, positions, target, w, hooks=None):
    tabs = _rope_tables(positions)
    dils = tuple(d for _, d in A_GROUPS)
    grads = {}
    w = dict(w)

    h, h_t, r1 = _rms_fwd(x, w['attn_norm'], "rms1")
    proj = _mm_rows([(h, w['w_in'], 0)], "mm_in")

    qkv_a, o_g, lse_g = [], [], []
    for gi, (window, d) in enumerate(A_GROUPS):
        gq, gk = _two(w['a_q_norm'][gi]), _two(w['a_k_norm'][gi])
        qkv = _qk_prep(proj, 6 * gi, d, False, gq, gk, tabs, f"qk_prep_a{gi}")
        o, lse = _band_fwd(qkv, 2 * d, window // d, None, f"band_fwd_a{gi}")
        qkv_a.append(qkv)
        o_g.append(o)
        lse_g.append(lse)
    o_a, lse_a = _merge_groups(o_g, lse_g, dils, "merge_a")
    if hooks is not None:
        w.update(hooks.weights('mix', o_a))

    gbq, gbk = _two(w['b_q_norm']), _two(w['b_k_norm'])
    sinks = jnp.repeat(w['b_sinks'].reshape(4, 2), HEAD_DIM, axis=1).reshape(4, 1, LANES)
    qkv_b = _qk_prep(proj, 18, 1, True, gbq, gbk, tabs, "qk_prep_b")
    o_b, lse_b = _band_fwd(qkv_b, 4, B_WINDOW - 1, sinks, "band_fwd_b")

    gates = _mm_rows([(h, w['w_gate'], 0)], "mm_gate", bias=w['b_gate'], sigmoid=True, out_dtypes=(BF16,))
    mk, mv = _mem_kv(mem, w['mem_norm'], w['w_mem_kv'], w['m_k_norm'], "mem_kv")
    o_m = _mem_attn_fwd(proj, 6, mk, mv, w['m_q_norm'], "mem_attn")

    pa = _mm_rows([(o_a, w['w_o_a'], 0)], "mm_oa", out_dtypes=(BF16,))
    pb = _mm_rows([(o_b, w['w_o_b'], 0)], "mm_ob", out_dtypes=(BF16,))
    pm = _mm_rows([(o_m, w['w_o_m'], 0)], "mm_om", out_dtypes=(BF16,))
    merged, merged_t = _gate_merge(gates, pa, pb, pm, "gate_merge")
    x1 = _mm_rows([(merged, w['w_out'], 0)], "mm_out", res=x)

    if hooks is not None:
        w.update(hooks.weights('ffn', x1))
    h2, h2_t, r2 = _rms_fwd(x1, w['ffn_norm'], "rms2")
    u = _mm_rows([(h2, w['w_up'], 0)], "mm_up", out_dtypes=(BF16,))
    act, act_t = _conv_glu(u, w['conv_w'], w['conv_b'], "conv_glu")
    dy, dy_b, loss = _mm_rows([(act, w['w_down'], 0)], "mm_down", res=x1, loss_target=target)

    dact = _mm_rows([(dy_b, w['w_down'], 0)], "mm_d_act", nt=True, out_dtypes=(BF16,))
    grads['w_down'] = _mm_rows([(act_t, dy_b, 0)], "mm_dw_down", tm=256)
    du_a, du_g, dcw_a, dcw_g, dcb_a, dcb_g = _conv_glu_bwd(dact, u, w['conv_w'], w['conv_b'], "conv_glu_bwd")
    grads['conv_w'] = jnp.concatenate([dcw_a, dcw_g], axis=1)
    grads['conv_b'] = jnp.concatenate([dcb_a, dcb_g], axis=1)
    dh2 = _mm_rows([(du_a, w['w_up'], 0), (du_g, w['w_up'], 1)], "mm_d_h2", nt=True)
    grads['w_up'] = jnp.concatenate([_mm_cols(h2_t, du_a, "mm_dw_up_a"), _mm_cols(h2_t, du_g, "mm_dw_up_g")], axis=1)
    ffn_gain = w['ffn_norm']
    if hooks is not None:
        ffn_gain = ffn_gain + hooks.grads('ffn', grads)[0:1, 0:1]
    dx1, dx1_b, grads['ffn_norm'] = _rms_bwd(dh2, x1, r2, ffn_gain, dy, "rms2_bwd", bf16_copy=True)

    dmerged = _mm_rows([(dx1_b, w['w_out'], 0)], "mm_d_merged", nt=True)
    grads['w_out'] = _mm_rows([(merged_t, dx1_b, 0)], "mm_dw_out", tm=256)
    dpa, dpb, dpm, dgpre, grads['b_gate'] = _gate_merge_bwd(dmerged, gates, pa, pb, pm, "gate_merge_bwd")
    do_a = _mm_rows([(dpa, w['w_o_a'], 0)], "mm_d_oa", nt=True)
    do_b = _mm_rows([(dpb, w['w_o_b'], 0)], "mm_d_ob", nt=True)
    do_m = _mm_rows([(dpm, w['w_o_m'], 0)], "mm_d_om", nt=True)
    grads['w_o_a'] = _mm_tn(o_a, dpa, "mm_dw_oa")
    grads['w_o_b'] = _mm_tn(o_b, dpb, "mm_dw_ob")
    grads['w_o_m'] = _mm_tn(o_m, dpm, "mm_dw_om")
    grads['w_gate'] = _mm_cols(h_t, dgpre, "mm_dw_gate")
    dq_m, dmk, dmv, grads['m_q_norm'] = _mem_attn_bwd(proj, 6, mk, mv, w['m_q_norm'], do_m, "mem_attn_bwd")
    grads['w_mem_kv'], grads['mem_norm'], grads['m_k_norm'] = _mem_kv_bwd(
        mem, w['mem_norm'], w['w_mem_kv'], w['m_k_norm'], dmk, dmv, "mem_kv_bwd")
    a_gain = w['a_q_norm']
    if hooks is not None:
        a_gain = a_gain + hooks.grads('mix', grads)[0:1, 0:1]

    prep = _bwd_prep(do_a, o_a, lse_a, dils, None, "bwd_prep_a")
    dproj, dgq_a, dgk_a = [], [], []
    for gi, (window, d) in enumerate(A_GROUPS):
        gq, gk = _two(a_gain[gi]), _two(w['a_k_norm'][gi])
        dqkv = _band_bwd(qkv_a[gi], prep[3 * gi], prep[3 * gi + 1], prep[3 * gi + 2], 2 * d, window // d,
                         f"band_bwd_a{gi}")
        dp, dgq, dgk = _qk_prep_bwd(dqkv, proj, 6 * gi, d, False, gq, gk, tabs, f"qk_prep_bwd_a{gi}")
        dproj.append(dp)
        dgq_a.append(_fold_heads(dgq))
        dgk_a.append(_fold_heads(dgk))
    grads['a_q_norm'] = jnp.stack(dgq_a)
    grads['a_k_norm'] = jnp.stack(dgk_a)

    do_bu, lse_bu, delta_bu, dsink = _bwd_prep(do_b, o_b, lse_b, (1,), sinks, "bwd_prep_b")
    dqkv = _band_bwd(qkv_b, do_bu, lse_bu, delta_bu, 4, B_WINDOW - 1, "band_bwd_b")
    dp_b, dgq, dgk = _qk_prep_bwd(dqkv, proj, 18, 1, True, gbq, gbk, tabs, "qk_prep_bwd_b")
    dproj.append(dp_b)
    grads['b_q_norm'] = _fold_heads(dgq)
    grads['b_k_norm'] = _fold_heads(dgk)
    grads['b_sinks'] = jnp.stack([dsink[:, 0, 0], dsink[:, 0, HEAD_DIM]], axis=1).reshape(8)

    dproj.append(dq_m)

    cols = (0, 1, 2, 3, 6)
    grads['w_in'] = jnp.concatenate(
        [_mm_rows([(h_t, dp, 0)], f"mm_dw_in{k}", tm=256) for k, dp in enumerate(dproj)], axis=1)
    attn_gain = w['attn_norm']
    if hooks is not None:
        attn_gain = attn_gain + hooks.grads('in', grads)[0:1, 0:1]
    dh = _mm_rows([(dp, w['w_in'], c) for dp, c in zip(dproj, cols)] + [(dgpre, w['w_gate'], 0)], "mm_d_h", nt=True)
    grad_x, grads['attn_norm'] = _rms_bwd(dh, x, r1, attn_gain, dx1, "rms1_bwd")
    return loss, grad_x, grads


def _coords():
    return lax.axis_index("x"), lax.axis_index("y"), lax.axis_index("c")


def _slot(p):
    return 4 * p[0] + 2 * p[1] + p[2]


ALL_PEERS = tuple(range(1, N_DEV))
CHIP_PEERS = (1, 4, 2, 6)
OTHER_CHIPS = (4, 2, 6)


def _peers(me, masks=ALL_PEERS):
    x, y, c = me
    return [(1 - x if mask & 4 else x, 1 - y if mask & 2 else y, 1 - c if mask & 1 else c) for mask in masks]


HBM_SPEC = pl.BlockSpec(memory_space=pltpu.HBM)


def _all_gather(shards, name):
    n = len(shards)

    def body(*refs):
        ins, outs = refs[:n], refs[n:2 * n]
        token, send_sems, recv_sems, local_sems = refs[2 * n:]
        token[...] = jnp.zeros_like(token)
        x, y, c = _coords()
        me, sibling = (x, y, c), (x, y, 1 - c)
        chips = [(1 - x, y), (x, 1 - y), (1 - x, 1 - y)]

        def copy(a, k, block, to, src=None):
            dst = outs[a].at[_slot(block)]
            return pltpu.make_async_remote_copy(
                src_ref=dst if src is None else src, dst_ref=dst, send_sem=send_sems.at[a, k],
                recv_sem=recv_sems.at[a, k], device_id=to, device_id_type=MESH)

        mine = [pltpu.make_async_copy(ins[a], outs[a].at[_slot(me)], local_sems.at[a]) for a in range(n)]
        for cp in mine:
            cp.start()
        first = []
        for a in range(n):
            first.append(copy(a, 0, me, sibling, src=ins[a]))
            first += [copy(a, 1 + j, me, (*chip, c), src=ins[a]) for j, chip in enumerate(chips)]
        for cp in first:
            cp.start()
        passed = []
        for a in range(n):
            for j, chip in enumerate(chips):
                copy(a, 1 + j, (*chip, c), me).wait_recv()
                fwd = copy(a, 4 + j, (*chip, c), sibling)
                fwd.start()
                passed.append(fwd)
        for a in range(n):
            copy(a, 0, sibling, me).wait_recv()
            for j, chip in enumerate(chips):
                copy(a, 4 + j, (*chip, 1 - c), me).wait_recv()
        for cp in first + passed:
            cp.wait_send()
        for cp in mine:
            cp.wait()

    return pl.pallas_call(
        body, name=name, in_specs=[HBM_SPEC] * n,
        out_specs=tuple([HBM_SPEC] * n + [pl.BlockSpec(memory_space=pltpu.VMEM)]),
        out_shape=tuple([jax.ShapeDtypeStruct((N_DEV,) + s.shape, s.dtype) for s in shards]
                        + [jax.ShapeDtypeStruct((8, LANES), F32)]),
        scratch_shapes=[pltpu.SemaphoreType.DMA((n, 7)), pltpu.SemaphoreType.DMA((n, 7)), pltpu.SemaphoreType.DMA((n,))],
    )(*shards)


SEM_SPEC = pl.BlockSpec(memory_space=pltpu.SEMAPHORE)
SIDE_EFFECT = pltpu.SideEffectType.DATAFLOW_SIDE_EFFECTING


def _exchange_start(blocks, name, gather=False, masks=ALL_PEERS):
    n = len(blocks)
    n_peers = len(masks)

    def body(*refs):
        ins, lands = refs[:n], refs[n:2 * n]
        send_sems, recv_sems = refs[2 * n], refs[2 * n + 1]
        token = refs[-1]
        me = _coords()
        peers = _peers(me, masks)
        for a in range(n):
            for k in range(n_peers):
                pltpu.make_async_remote_copy(
                    src_ref=ins[a] if gather else ins[a].at[_slot(peers[k])], dst_ref=lands[a].at[_slot(me)],
                    send_sem=send_sems.at[a * n_peers + k], recv_sem=recv_sems.at[a * n_peers + k],
                    device_id=peers[k], device_id_type=MESH).start()
        token[...] = jnp.zeros_like(token)

    land_shapes = [((N_DEV,) + b.shape) if gather else b.shape for b in blocks]
    hbm_in = [pltpu.HBM(b.shape, b.dtype) for b in blocks]
    hbm_land = [pltpu.HBM(s, b.dtype) for s, b in zip(land_shapes, blocks)]
    sems = pltpu.SemaphoreType.DMA((n * n_peers,))
    ins = [pltpu.with_memory_space_constraint(b, pltpu.HBM) for b in blocks]
    lands = [pltpu.with_memory_space_constraint(lax.empty(s, b.dtype), pltpu.HBM) for s, b in zip(land_shapes, blocks)]
    return pl.pallas_call(
        body, name=name, out_shape=(sems, sems, *hbm_in, *hbm_land, jax.ShapeDtypeStruct((8, LANES), F32)),
        in_specs=[HBM_SPEC] * (2 * n),
        out_specs=(SEM_SPEC, SEM_SPEC, *([HBM_SPEC] * (2 * n)), pl.BlockSpec(memory_space=pltpu.VMEM)),
        input_output_aliases={i: 2 + i for i in range(2 * n)},
        compiler_params=pltpu.CompilerParams(has_side_effects=SIDE_EFFECT),
    )(*ins, *lands)


def _exchange_wait(started, after, name, gather=False, masks=ALL_PEERS):
    n = (len(started) - 3) // 2
    n_peers = len(masks)
    send_sems, recv_sems = started[0], started[1]
    thru = started[2:2 + 2 * n]

    def body(*refs):
        ins, lands = refs[:n], refs[n:2 * n]
        send_ref, recv_ref = refs[2 * n], refs[2 * n + 1]
        me = _coords()
        peers = _peers(me, masks)
        for a in range(n):
            for k in range(n_peers):
                cp = pltpu.make_async_remote_copy(
                    src_ref=ins[a] if gather else ins[a].at[_slot(peers[k])], dst_ref=lands[a].at[_slot(peers[k])],
                    send_sem=send_ref.at[a * n_peers + k], recv_sem=recv_ref.at[a * n_peers + k],
                    device_id=peers[k], device_id_type=MESH)
                cp.wait_send()
                cp.wait_recv()

    hbm = [pltpu.HBM(t.shape, t.dtype) for t in thru]
    res = pl.pallas_call(
        body, name=name, out_shape=tuple(hbm),
        in_specs=[HBM_SPEC] * (2 * n) + [SEM_SPEC, SEM_SPEC, pl.BlockSpec(memory_space=pl.ANY)],
        out_specs=tuple([HBM_SPEC] * (2 * n)), input_output_aliases={i: i for i in range(2 * n)},
        compiler_params=pltpu.CompilerParams(has_side_effects=SIDE_EFFECT),
    )(*thru, send_sems, recv_sems, after)
    return res[n:]


def _sibling_forward(arrays, name):
    n = len(arrays)
    n_fwd = len(OTHER_CHIPS)

    def body(*refs):
        bufs = refs[n:2 * n]
        token, send_sems, recv_sems = refs[2 * n:]
        token[...] = jnp.zeros_like(token)
        x, y, c = _coords()
        sibling = (x, y, 1 - c)
        mine = _peers((x, y, c), OTHER_CHIPS)
        theirs = _peers(sibling, OTHER_CHIPS)

        def copy(a, k, block):
            rows = bufs[a].at[_slot(block)]
            return pltpu.make_async_remote_copy(
                src_ref=rows, dst_ref=rows, send_sem=send_sems.at[a * n_fwd + k], recv_sem=recv_sems.at[a * n_fwd + k],
                device_id=sibling, device_id_type=MESH)

        sends = [copy(a, k, mine[k]) for a in range(n) for k in range(n_fwd)]
        for cp in sends:
            cp.start()
        for a in range(n):
            for k in range(n_fwd):
                copy(a, k, theirs[k]).wait_recv()
        for cp in sends:
            cp.wait_send()

    res = pl.pallas_call(
        body, name=name, in_specs=[HBM_SPEC] * n,
        out_specs=tuple([HBM_SPEC] * n + [pl.BlockSpec(memory_space=pltpu.VMEM)]),
        out_shape=tuple([jax.ShapeDtypeStruct(a.shape, a.dtype) for a in arrays] + [jax.ShapeDtypeStruct((8, LANES), F32)]),
        input_output_aliases={i: i for i in range(n)},
        scratch_shapes=[pltpu.SemaphoreType.DMA((n * n_fwd,)), pltpu.SemaphoreType.DMA((n * n_fwd,))],
    )(*arrays)
    return res[:n], res[n]


def _all_sum(p, name):
    def body(p_ref, o_ref, recv, send_sems, recv_sems):
        me = _coords()
        peers = _peers(me)
        recv[_slot(me)] = p_ref[...]

        def copy(k, landing):
            return pltpu.make_async_remote_copy(
                src_ref=p_ref, dst_ref=recv.at[_slot(landing)], send_sem=send_sems.at[k], recv_sem=recv_sems.at[k],
                device_id=peers[k], device_id_type=MESH)

        sends = [copy(k, me) for k in range(N_DEV - 1)]
        for cp in sends:
            cp.start()
        for k in range(N_DEV - 1):
            copy(k, peers[k]).wait_recv()
        for cp in sends:
            cp.wait_send()
        acc = recv[0]
        for s in range(1, N_DEV):
            acc = acc + recv[s]
        o_ref[...] = acc

    vmem = pl.BlockSpec(memory_space=pltpu.VMEM)
    return pl.pallas_call(
        body, name=name, in_specs=[vmem], out_specs=vmem, out_shape=jax.ShapeDtypeStruct(p.shape, F32),
        scratch_shapes=[pltpu.VMEM((N_DEV,) + p.shape, F32), pltpu.SemaphoreType.DMA((N_DEV - 1,)),
                        pltpu.SemaphoreType.DMA((N_DEV - 1,))],
    )(p)


def _adam(w, g, m, v):
    m2 = ADAM_B1 * m + (1.0 - ADAM_B1) * g
    v2 = ADAM_B2 * v + (1.0 - ADAM_B2) * (g * g)
    m_hat = m2 / (1.0 - ADAM_B1 ** ADAM_STEP)
    v_hat = v2 / (1.0 - ADAM_B2 ** ADAM_STEP)
    delta = -ADAM_LR * (m_hat / (jnp.sqrt(v_hat) + ADAM_EPS) + ADAM_WD * w)
    return delta, m2, v2


def _row_tile(rows, cols):
    best = rows
    for t in range(16, rows, 16):
        if rows % t == 0 and t * cols * 4 <= (1 << 20):
            best = t
    return best


def _adam_reduce(parts, w, m, v, name):
    rows, cols = w.shape
    tr = _row_tile(rows, cols)

    def body(p_ref, w_ref, m_ref, v_ref, g_ref, d_ref, m2_ref, v2_ref):
        g = p_ref[0].astype(F32)
        for s in range(1, N_DEV):
            g = g + p_ref[s].astype(F32)
        g_ref[...] = g
        d_ref[...], m2_ref[...], v2_ref[...] = _adam(w_ref[...], g, m_ref[...], v_ref[...])

    blk = pl.BlockSpec((tr, cols), lambda i: (i, 0))
    shp = jax.ShapeDtypeStruct((rows, cols), F32)
    return pl.pallas_call(
        body, name=name, grid=(rows // tr,),
        in_specs=[pl.BlockSpec((N_DEV, tr, cols), lambda i: (0, i, 0)), blk, blk, blk],
        out_specs=(blk,) * 4, out_shape=(shp,) * 4, compiler_params=_cparams(1),
    )(parts, w, m, v)


PACK_COLS = 1024
PACK = {'attn_norm': (0, 1, 1024), 'mem_norm': (1, 1, 1024), 'ffn_norm': (2, 1, 1024), 'b_gate': (3, 3, 1024),
        'conv_b': (6, 6, 1024), 'a_q_norm': (12, 3, 64), 'a_k_norm': (15, 3, 64), 'b_q_norm': (18, 1, 64),
        'b_k_norm': (19, 1, 64), 'm_q_norm': (20, 1, 128), 'm_k_norm': (21, 1, 128), 'b_sinks': (22, 1, 8)}
PACK_LOSS_ROW = 23
PACK_ROWS = 24


def _pack_pieces(name, width):
    r0, nr, lanes = PACK[name]
    out = []
    for j in range(nr):
        if lanes == PACK_COLS:
            w = min(PACK_COLS, width - j * PACK_COLS)
            out.append((r0 + j, slice(0, 1), slice(j * PACK_COLS, j * PACK_COLS + w), w))
        else:
            out.append((r0 + j, slice(j, j + 1), slice(0, lanes), lanes))
    return out


def _pack_small(grads, loss_tile, name):
    names = list(PACK)

    def body(*refs):
        o_ref = refs[-1]
        o_ref[...] = jnp.zeros_like(o_ref)
        for k, nm in enumerate(names):
            for row, rs, ls, w in _pack_pieces(nm, refs[k].shape[1]):
                o_ref[row:row + 1, 0:w] = refs[k][rs, ls]
        o_ref[PACK_LOSS_ROW:PACK_LOSS_ROW + 1, 0:1] = refs[len(names)][0:1, 0:1]

    vmem = pl.BlockSpec(memory_space=pltpu.VMEM)
    args = [grads[nm] for nm in names] + [loss_tile]
    return pl.pallas_call(body, name=name, in_specs=[vmem] * len(args), out_specs=vmem,
                          out_shape=jax.ShapeDtypeStruct((PACK_ROWS, PACK_COLS), F32))(*args)


def _adam_small(gsum, ws, ms, vs, name):
    names = list(PACK)
    n = len(names)

    def body(*refs):
        g_ref = refs[0]
        w_refs, m_refs, v_refs = refs[1:1 + n], refs[1 + n:1 + 2 * n], refs[1 + 2 * n:1 + 3 * n]
        outs = refs[1 + 3 * n:]
        outs[0][...] = g_ref[PACK_LOSS_ROW:PACK_LOSS_ROW + 1, 0:1]
        for k, nm in enumerate(names):
            o_g, o_d, o_m, o_v = outs[1 + 4 * k:5 + 4 * k]
            for row, rs, ls, width in _pack_pieces(nm, w_refs[k].shape[1]):
                src = (rs, ls)
                g = g_ref[row:row + 1, 0:width]
                d, m2, v2 = _adam(w_refs[k][src], g, m_refs[k][src], v_refs[k][src])
                o_g[src] = g
                o_d[src] = d
                o_m[src] = m2
                o_v[src] = v2

    vmem = pl.BlockSpec(memory_space=pltpu.VMEM)
    shapes = [jax.ShapeDtypeStruct((1, 1), F32)]
    for nm in names:
        shapes += [jax.ShapeDtypeStruct(ws[nm].shape, F32)] * 4
    args = [gsum] + [ws[nm] for nm in names] + [ms[nm] for nm in names] + [vs[nm] for nm in names]
    return pl.pallas_call(
        body, name=name, in_specs=[vmem] * len(args), out_specs=tuple([vmem] * len(shapes)), out_shape=tuple(shapes),
    )(*args)


def _as2d(name, a):
    return a.reshape(a.shape[-2], a.shape[-1]) if a.ndim == 3 else a


def kernel(x, mem, positions, attn_norm, w_in, a_q_norm, a_k_norm, b_q_norm, b_k_norm, b_sinks, mem_norm, w_mem_kv, m_q_norm, m_k_norm, w_o_a, w_o_b, w_o_m, w_gate, b_gate, w_out, ffn_norm, w_up, conv_w, conv_b, w_down, loss_target, m_attn_norm, m_w_in, m_a_q_norm, m_a_k_norm, m_b_q_norm, m_b_k_norm, m_b_sinks, m_mem_norm, m_w_mem_kv, m_m_q_norm, m_m_k_norm, m_w_o_a, m_w_o_b, m_w_o_m, m_w_gate, m_b_gate, m_w_out, m_ffn_norm, m_w_up, m_conv_w, m_conv_b, m_w_down, v_attn_norm, v_w_in, v_a_q_norm, v_a_k_norm, v_b_q_norm, v_b_k_norm, v_b_sinks, v_mem_norm, v_w_mem_kv, v_m_q_norm, v_m_k_norm, v_w_o_a, v_w_o_b, v_w_o_m, v_w_gate, v_b_gate, v_w_out, v_ffn_norm, v_w_up, v_conv_w, v_conv_b, v_w_down):
    given = dict(attn_norm=attn_norm, w_in=w_in, a_q_norm=a_q_norm, a_k_norm=a_k_norm, b_q_norm=b_q_norm, b_k_norm=b_k_norm, b_sinks=b_sinks, mem_norm=mem_norm, w_mem_kv=w_mem_kv, m_q_norm=m_q_norm, m_k_norm=m_k_norm, w_o_a=w_o_a, w_o_b=w_o_b, w_o_m=w_o_m, w_gate=w_gate, b_gate=b_gate, w_out=w_out, ffn_norm=ffn_norm, w_up=w_up, conv_w=conv_w, conv_b=conv_b, w_down=w_down)
    mom1 = dict(attn_norm=m_attn_norm, w_in=m_w_in, a_q_norm=m_a_q_norm, a_k_norm=m_a_k_norm, b_q_norm=m_b_q_norm, b_k_norm=m_b_k_norm, b_sinks=m_b_sinks, mem_norm=m_mem_norm, w_mem_kv=m_w_mem_kv, m_q_norm=m_m_q_norm, m_k_norm=m_m_k_norm, w_o_a=m_w_o_a, w_o_b=m_w_o_b, w_o_m=m_w_o_m, w_gate=m_w_gate, b_gate=m_b_gate, w_out=m_w_out, ffn_norm=m_ffn_norm, w_up=m_w_up, conv_w=m_conv_w, conv_b=m_conv_b, w_down=m_w_down)
    mom2 = dict(attn_norm=v_attn_norm, w_in=v_w_in, a_q_norm=v_a_q_norm, a_k_norm=v_a_k_norm, b_q_norm=v_b_q_norm, b_k_norm=v_b_k_norm, b_sinks=v_b_sinks, mem_norm=v_mem_norm, w_mem_kv=v_w_mem_kv, m_q_norm=v_m_q_norm, m_k_norm=v_m_k_norm, w_o_a=v_w_o_a, w_o_b=v_w_o_b, w_o_m=v_w_o_m, w_gate=v_w_gate, b_gate=v_b_gate, w_out=v_w_out, ffn_norm=v_ffn_norm, w_up=v_w_up, conv_w=v_conv_w, conv_b=v_conv_b, w_down=v_w_down)

    big = list(BIG)
    stages = {'mix': list(MIX_WEIGHTS), 'ffn': list(FFN_WEIGHTS), 'in': ['w_in']}
    my_slot = _slot(_coords())

    def shard(n):
        return given[n][0] if n == 'conv_w' else given[n][0].astype(BF16)

    def whole(n, g):
        _, r, c = g.shape
        return g.reshape(N_DEV * r, c) if BIG[n] == 0 else g.transpose(1, 0, 2).reshape(r, N_DEV * c)

    def to_blocks(n, g):
        r, c = given[n].shape[1:]
        g = g.reshape(N_DEV, r, c) if BIG[n] == 0 else g.reshape(r, N_DEV, c).transpose(1, 0, 2)
        return g if n == 'conv_w' else g.astype(BF16)

    class Hooks:
        def __init__(self, token):
            self.coming, self.sent = {}, {}
            self.start_gather('mix', token)

        def start_gather(self, stage, token):
            src = [shard(n) if n == 'conv_w' else (given[n][0] + token).astype(BF16) for n in stages[stage]]
            self.coming[stage] = _exchange_start(src, f"gather_{stage}_start", gather=True, masks=CHIP_PEERS)

        def weights(self, stage, after):
            names = stages[stage]
            landed = _exchange_wait(self.coming[stage], after, f"gather_{stage}_wait", gather=True, masks=CHIP_PEERS)
            landed, token = _sibling_forward(landed, f"gather_{stage}_forward")
            if stage == 'mix':
                self.start_gather('ffn', token[0, 0])
            return {n: whole(n, lax.dynamic_update_slice_in_dim(land, shard(n)[None], my_slot, axis=0))
                    for n, land in zip(names, landed)}

        def grads(self, stage, g):
            blocks = [to_blocks(n, g[n]) for n in stages[stage]]
            own = [lax.dynamic_slice_in_dim(b, my_slot, 1, axis=0) for b in blocks]
            self.sent[stage] = (_exchange_start(blocks, f"exchange_{stage}_start"), own)
            return self.sent[stage][0][-1]

        def parts(self, stage, after):
            started, own = self.sent[stage]
            landed = _exchange_wait(started, after, f"exchange_{stage}_wait")
            return {n: lax.dynamic_update_slice_in_dim(land, o, my_slot, axis=0)
                    for n, land, o in zip(stages[stage], landed, own)}

    w_in_all, token = _all_gather([shard('w_in')], "gather_w_in")
    hooks = Hooks(token[0, 0])
    w = {'w_in': whole('w_in', w_in_all)}
    for n in SMALL:
        w[n] = given[n]
    w['a_q_norm'], w['a_k_norm'] = given['a_q_norm'][0], given['a_k_norm'][0]
    w['b_q_norm'], w['b_k_norm'], w['b_sinks'] = given['b_q_norm'][0], given['b_k_norm'][0], given['b_sinks'][0]

    loss_tile, grad_x, grads = _device_step(x[0], mem[0], positions[0], loss_target[0], w, hooks)
    out = {}
    after = grad_x
    for stage in ('ffn', 'mix', 'in'):
        for n, p in hooks.parts(stage, after).items():
            res = _adam_reduce(p, given[n][0], mom1[n][0], mom2[n][0], f"adam_{n}")
            out[n] = tuple(t[None] for t in res)
            after = res[0]

    small = {n: grads[n] for n in PACK}
    small['b_q_norm'], small['b_k_norm'] = grads['b_q_norm'].reshape(1, -1), grads['b_k_norm'].reshape(1, -1)
    small['b_sinks'] = grads['b_sinks'].reshape(1, -1)
    gsum = _all_sum(_pack_small(small, loss_tile, "pack_small"), "sum_small")
    ws = {n: _as2d(n, given[n]) for n in PACK}
    ms = {n: _as2d(n, mom1[n]) for n in PACK}
    vs = {n: _as2d(n, mom2[n]) for n in PACK}
    res = _adam_small(gsum, ws, ms, vs, "adam_small")
    loss = res[0].reshape(())
    for k, n in enumerate(PACK):
        out[n] = tuple(t.reshape(given[n].shape) for t in res[1 + 4 * k:5 + 4 * k])

    outs = [loss, grad_x[None]]
    for field in range(4):
        outs += [out[n][field] for n in WEIGHTS]
    return tuple(outs)
```

```python
import functools
import math

import jax
import jax.numpy as jnp
from jax import lax
from jax.experimental import pallas as pl
from jax.experimental.pallas import tpu as pltpu

F32 = jnp.float32
BF16 = jnp.bfloat16

N_DEV = 8
D_MODEL = 1024
HEAD_DIM = 64
A_GROUPS = ((128, 1), (512, 4), (2048, 16))
B_WINDOW = 128
M_HEADS = 4
M_HEAD_DIM = 128
MEM_LEN = 256
D_FF = 2816
ROPE_THETA = 500000.0
ROPE_DIMS = 16
BLOCK = 128
EPS = 1e-6
LANES = 128
BAND_Q_BLOCKS = 4
BAND_UNITS = 2

ADAM_LR = 0.001
ADAM_B1 = 0.9
ADAM_B2 = 0.999
ADAM_EPS = 1e-08
ADAM_WD = 0.01
ADAM_STEP = 10

VMEM_LIMIT_BYTES = 56 * 1024 * 1024
MESH = pl.DeviceIdType.MESH

WEIGHTS = ['attn_norm', 'w_in', 'a_q_norm', 'a_k_norm', 'b_q_norm', 'b_k_norm', 'b_sinks', 'mem_norm',
           'w_mem_kv', 'm_q_norm', 'm_k_norm', 'w_o_a', 'w_o_b', 'w_o_m', 'w_gate', 'b_gate', 'w_out',
           'ffn_norm', 'w_up', 'conv_w', 'conv_b', 'w_down']
BIG = {'w_in': 1, 'w_mem_kv': 0, 'w_o_a': 1, 'w_o_b': 1, 'w_o_m': 1, 'w_gate': 1, 'w_out': 0, 'w_up': 1,
       'conv_w': 1, 'w_down': 0}
SMALL = [n for n in WEIGHTS if n not in BIG]


def _cparams(n_grid):
    return pltpu.CompilerParams(dimension_semantics=("arbitrary",) * n_grid, vmem_limit_bytes=VMEM_LIMIT_BYTES)


def _seg_matrix(width):
    shift = width.bit_length() - 1
    r = lax.shift_right_logical(lax.broadcasted_iota(jnp.int32, (LANES, LANES), 0), shift)
    c = lax.shift_right_logical(lax.broadcasted_iota(jnp.int32, (LANES, LANES), 1), shift)
    return jnp.where(r == c, 1.0, 0.0).astype(BF16)


def _seg_sum(x, seg):
    hi = x.astype(BF16)
    r1 = x - hi.astype(F32)
    mid = r1.astype(BF16)
    lo = (r1 - mid.astype(F32)).astype(BF16)
    dot = functools.partial(jnp.dot, preferred_element_type=F32)
    return dot(hi, seg) + dot(mid, seg) + dot(lo, seg)


def _rope(y, c, s1, s2):
    return y * c + pltpu.roll(y, LANES - ROPE_DIMS // 2, 1) * s1 + pltpu.roll(y, ROPE_DIMS // 2, 1) * s2


def _unrope(dy, c, s1, s2):
    return dy * c + pltpu.roll(dy * s1, ROPE_DIMS // 2, 1) + pltpu.roll(dy * s2, LANES - ROPE_DIMS // 2, 1)


def _sigmoid(x):
    return 1.0 / (1.0 + jnp.exp(-x))


def _rms_fwd(x, gain, name):
    s_len, d = x.shape
    tm = 512

    def body(x_ref, g_ref, h_ref, ht_ref, r_ref):
        xv = x_ref[...]
        r = lax.rsqrt(jnp.mean(xv * xv, axis=-1, keepdims=True) + EPS)
        h = ((xv * r) * g_ref[...]).astype(BF16)
        h_ref[...] = h
        ht_ref[...] = h.T
        r_ref[...] = r

    return pl.pallas_call(
        body, name=name, grid=(s_len // tm,),
        in_specs=[pl.BlockSpec((tm, d), lambda i: (i, 0)), pl.BlockSpec((1, d), lambda i: (0, 0))],
        out_specs=(pl.BlockSpec((tm, d), lambda i: (i, 0)), pl.BlockSpec((d, tm), lambda i: (0, i)),
                   pl.BlockSpec((tm, 1), lambda i: (i, 0))),
        out_shape=(jax.ShapeDtypeStruct((s_len, d), BF16), jax.ShapeDtypeStruct((d, s_len), BF16),
                   jax.ShapeDtypeStruct((s_len, 1), F32)),
        compiler_params=_cparams(1),
    )(x, gain)


def _rms_bwd(dh, x, r, gain, add, name, bf16_copy=False):
    s_len, d = x.shape
    tm = 512

    def body(dh_ref, x_ref, r_ref, g_ref, add_ref, dx_ref, *rest):
        dg_ref = rest[-1]

        @pl.when(pl.program_id(0) == 0)
        def _():
            dg_ref[...] = jnp.zeros_like(dg_ref)
        rv = r_ref[...]
        xhat = x_ref[...] * rv
        dhv = dh_ref[...]
        dg_ref[...] += jnp.sum(dhv * xhat, axis=0, keepdims=True)
        dxhat = dhv * g_ref[...]
        dx = add_ref[...] + rv * (dxhat - xhat * jnp.mean(dxhat * xhat, axis=-1, keepdims=True))
        dx_ref[...] = dx
        if bf16_copy:
            rest[0][...] = dx.astype(BF16)

    row = pl.BlockSpec((tm, d), lambda i: (i, 0))
    vec = pl.BlockSpec((1, d), lambda i: (0, 0))
    out_specs = [row] + ([row] if bf16_copy else []) + [vec]
    out_shape = [jax.ShapeDtypeStruct((s_len, d), F32)] + ([jax.ShapeDtypeStruct((s_len, d), BF16)] if bf16_copy else [])
    out_shape.append(jax.ShapeDtypeStruct((1, d), F32))
    return pl.pallas_call(
        body, name=name, grid=(s_len // tm,),
        in_specs=[row, row, pl.BlockSpec((tm, 1), lambda i: (i, 0)), vec, row],
        out_specs=tuple(out_specs), out_shape=tuple(out_shape), compiler_params=_cparams(1),
    )(dh, x, r, gain, add)


def _resident(shape, index_map):
    return pl.BlockSpec(shape, index_map, pipeline_mode=pl.Buffered(1))


def _mm_rows(pairs, name, nt=False, tm=512, bias=None, sigmoid=False, res=None, out_dtypes=(F32,), loss_target=None):
    m = pairs[0][0].shape[0]
    n = pairs[0][1].shape[0] if nt else pairs[0][1].shape[1]
    n_pairs = len(pairs)
    has_bias, has_res, has_loss = bias is not None, res is not None, loss_target is not None
    dims = (((1,), (1,)), ((), ())) if nt else (((1,), (0,)), ((), ()))

    def body(*refs):
        acc = None
        for p in range(n_pairs):
            t = lax.dot_general(refs[2 * p][...].astype(BF16), refs[2 * p + 1][...], dims, preferred_element_type=F32)
            acc = t if acc is None else acc + t
        pos = 2 * n_pairs
        if has_bias:
            acc = acc + refs[pos][...]
            pos += 1
        if sigmoid:
            acc = _sigmoid(acc)
        if has_res:
            acc = refs[pos][...] + acc
            pos += 1
        if has_loss:
            dy_ref, dyb_ref, l_ref = refs[pos + 1:]

            @pl.when(pl.program_id(0) == 0)
            def _():
                l_ref[...] = jnp.zeros_like(l_ref)
            err = acc - refs[pos][...]
            dy = err * (1.0 / n)
            dy_ref[...] = dy
            dyb_ref[...] = dy.astype(BF16)
            part = 0.5 * jnp.sum(jnp.mean(err * err, axis=-1, keepdims=True), axis=0, keepdims=True)
            l_ref[...] += jnp.broadcast_to(part, l_ref.shape)
            return
        for o_ref in refs[pos:]:
            o_ref[...] = acc.astype(o_ref.dtype)

    in_specs, args = [], []
    for a, w, blk in pairs:
        k = a.shape[1]
        in_specs.append(pl.BlockSpec((tm, k), lambda i: (i, 0)))
        if nt:
            in_specs.append(_resident((n, k), lambda i, blk=blk: (0, blk)))
        else:
            in_specs.append(_resident((k, n), lambda i, blk=blk: (blk, 0)))
        args += [a, w]
    if has_bias:
        in_specs.append(_resident((1, n), lambda i: (0, 0)))
        args.append(bias)
    if has_res:
        in_specs.append(pl.BlockSpec((tm, n), lambda i: (i, 0)))
        args.append(res)
    out = pl.BlockSpec((tm, n), lambda i: (i, 0))
    if has_loss:
        return pl.pallas_call(
            body, name=name, grid=(m // tm,), in_specs=in_specs + [out],
            out_specs=(out, out, pl.BlockSpec((8, LANES), lambda i: (0, 0))),
            out_shape=(jax.ShapeDtypeStruct((m, n), F32), jax.ShapeDtypeStruct((m, n), BF16),
                       jax.ShapeDtypeStruct((8, LANES), F32)),
            compiler_params=_cparams(1),
        )(*args, loss_target)
    outs = pl.pallas_call(
        body, name=name, grid=(m // tm,), in_specs=in_specs, out_specs=tuple([out] * len(out_dtypes)),
        out_shape=tuple(jax.ShapeDtypeStruct((m, n), dt) for dt in out_dtypes), compiler_params=_cparams(1),
    )(*args)
    return outs[0] if len(out_dtypes) == 1 else outs


def _mm_rows_cat(a, ws, name, tm=256):
    m, k = a.shape
    widths = [w.shape[1] for w in ws]
    n = sum(widths)

    def body(*refs):
        a_ref, o_ref = refs[0], refs[-1]
        av = a_ref[...]
        off = 0
        for p, width in enumerate(widths):
            o_ref[:, off:off + width] = jnp.dot(av, refs[1 + p][...], preferred_element_type=F32)
            off += width

    return pl.pallas_call(
        body, name=name, grid=(m // tm,),
        in_specs=[pl.BlockSpec((tm, k), lambda i: (i, 0))] + [_resident((k, wd), lambda i: (0, 0)) for wd in widths],
        out_specs=pl.BlockSpec((tm, n), lambda i: (i, 0)),
        out_shape=jax.ShapeDtypeStruct((m, n), F32), compiler_params=_cparams(1),
    )(a, *ws)


def _mm_cols(a, b, name, tn=256):
    m, k = a.shape
    n = b.shape[1]

    def body(a_ref, b_ref, o_ref):
        o_ref[...] = jnp.dot(a_ref[...], b_ref[...].astype(BF16), preferred_element_type=F32)

    return pl.pallas_call(
        body, name=name, grid=(n // tn,),
        in_specs=[_resident((m, k), lambda j: (0, 0)), pl.BlockSpec((k, tn), lambda j: (0, j))],
        out_specs=pl.BlockSpec((m, tn), lambda j: (0, j)),
        out_shape=jax.ShapeDtypeStruct((m, n), F32), compiler_params=_cparams(1),
    )(a, b)


def _mm_tn(a, b, name, tile=256):
    k, m = a.shape
    n = b.shape[1]
    dims = (((0,), (0,)), ((), ()))

    def body(a_ref, b_ref, o_ref):
        o_ref[...] = lax.dot_general(a_ref[...].astype(BF16), b_ref[...].astype(BF16), dims, preferred_element_type=F32)

    if n <= m:
        t = min(tile, m)
        grid, a_spec, b_spec = (m // t,), pl.BlockSpec((k, t), lambda i: (0, i)), _resident((k, n), lambda i: (0, 0))
        o_spec = pl.BlockSpec((t, n), lambda i: (i, 0))
    else:
        t = min(tile, n)
        grid, a_spec, b_spec = (n // t,), _resident((k, m), lambda i: (0, 0)), pl.BlockSpec((k, t), lambda i: (0, i))
        o_spec = pl.BlockSpec((m, t), lambda i: (0, i))
    return pl.pallas_call(
        body, name=name, grid=grid, in_specs=[a_spec, b_spec], out_specs=o_spec,
        out_shape=jax.ShapeDtypeStruct((m, n), F32), compiler_params=_cparams(1),
    )(a, b)


def _norm_rope(t, gain, c, s1, s2, seg):
    rs = lax.rsqrt(_seg_sum(t * t, seg) * (1.0 / HEAD_DIM) + EPS)
    return _rope((t * rs) * gain, c, s1, s2)


def _dup_half(y, half):
    lane = lax.broadcasted_iota(jnp.int32, y.shape, 1)
    rolled = pltpu.roll(y, HEAD_DIM, 1)
    keep = (lane < HEAD_DIM) if half == 0 else (lane >= HEAD_DIM)
    return jnp.where(keep, y, rolled)


def _qk_prep(proj, cb0, d, gqa, gq, gk, tabs, name):
    s_len = proj.shape[0]
    tm = 512
    rows = tm // d
    n_units = 4 if gqa else 2 * d
    n_q = 4 if gqa else 2
    n_in = 6

    def body(*refs):
        in_refs = refs[:n_in]
        gq_ref, gk_ref, c_ref, s1_ref, s2_ref, o_ref = refs[n_in:]
        seg = _seg_matrix(HEAD_DIM)

        def rows_of(ref, r):
            return ref[...] if d == 1 else ref[pl.ds(r, rows, stride=d), :]

        def put(unit_col, y):
            o_ref[:, unit_col * LANES:(unit_col + 1) * LANES] = y.astype(BF16)

        for r in range(d):
            c, s1, s2 = rows_of(c_ref, r), rows_of(s1_ref, r), rows_of(s2_ref, r)
            for b in range(n_in):
                t = rows_of(in_refs[b], r)
                if b < n_q:
                    put((b * d + r) if not gqa else b, _norm_rope(t, gq_ref[...], c, s1, s2, seg))
                elif not gqa:
                    sec, pair = (1, b - 2) if b < 4 else (2, b - 4)
                    y = _norm_rope(t, gk_ref[...], c, s1, s2, seg) if sec == 1 else t
                    put(sec * n_units + pair * d + r, y)
                else:
                    sec = 1 if b == 4 else 2
                    y = _norm_rope(t, gk_ref[...], c, s1, s2, seg) if sec == 1 else t
                    for u in range(n_units):
                        put(sec * n_units + u, _dup_half(y, u // 2))

    in_specs = [pl.BlockSpec((tm, LANES), lambda i, b=b: (i, cb0 + b)) for b in range(n_in)]
    vec = pl.BlockSpec((1, LANES), lambda i: (0, 0))
    tab = pl.BlockSpec((tm, LANES), lambda i: (i, 0))
    width = 3 * n_units * LANES
    return pl.pallas_call(
        body, name=name, grid=(s_len // tm,), in_specs=in_specs + [vec, vec, tab, tab, tab],
        out_specs=pl.BlockSpec((rows, width), lambda i: (i, 0)),
        out_shape=jax.ShapeDtypeStruct((s_len // d, width), BF16), compiler_params=_cparams(1),
    )(*([proj] * n_in), gq, gk, *tabs)


def _qk_prep_bwd(dqkv, proj, cb0, d, gqa, gq, gk, tabs, name):
    s_len = proj.shape[0]
    tm = 512
    rows = tm // d
    n_units = 4 if gqa else 2 * d
    n_q = 4 if gqa else 2
    n_in = 6

    def body(*refs):
        d_refs = refs[0:3]
        in_refs = refs[3:3 + n_in]
        gq_ref, gk_ref, c_ref, s1_ref, s2_ref, o_ref, dgq_ref, dgk_ref, stage = refs[3 + n_in:]
        seg = _seg_matrix(HEAD_DIM)

        @pl.when(pl.program_id(0) == 0)
        def _():
            dgq_ref[...] = jnp.zeros_like(dgq_ref)
            dgk_ref[...] = jnp.zeros_like(dgk_ref)

        def rows_of(ref, r):
            return ref[...] if d == 1 else ref[pl.ds(r, rows, stride=d), :]

        def unit(col):
            sec, u = divmod(col, n_units)
            return d_refs[sec][:, u * LANES:(u + 1) * LANES]

        def norm_bwd(dyr, t, gain, c, s1, s2, dg_ref):
            rs = lax.rsqrt(_seg_sum(t * t, seg) * (1.0 / HEAD_DIM) + EPS)
            that = t * rs
            dy = _unrope(dyr, c, s1, s2)
            dg_ref[...] += jnp.sum(dy * that, axis=0, keepdims=True)
            dthat = dy * gain
            return rs * (dthat - that * (_seg_sum(dthat * that, seg) * (1.0 / HEAD_DIM)))

        def fold(sec):
            tot = []
            for u in range(n_units):
                v = unit(sec * n_units + u)
                tot.append(v + pltpu.roll(v, HEAD_DIM, 1))
            lane = lax.broadcasted_iota(jnp.int32, tot[0].shape, 1)
            return jnp.where(lane < HEAD_DIM, tot[0] + tot[1], tot[2] + tot[3])

        for b in range(n_in):
            for r in range(d):
                c, s1, s2 = rows_of(c_ref, r), rows_of(s1_ref, r), rows_of(s2_ref, r)
                t = rows_of(in_refs[b], r)
                if b < n_q:
                    g = unit((b * d + r) if not gqa else b)
                    out = norm_bwd(g, t, gq_ref[...], c, s1, s2, dgq_ref)
                elif not gqa:
                    sec, pair = (1, b - 2) if b < 4 else (2, b - 4)
                    g = unit(sec * n_units + pair * d + r)
                    out = norm_bwd(g, t, gk_ref[...], c, s1, s2, dgk_ref) if sec == 1 else g
                else:
                    sec = 1 if b == 4 else 2
                    g = fold(sec)
                    out = norm_bwd(g, t, gk_ref[...], c, s1, s2, dgk_ref) if sec == 1 else g
                if d == 1:
                    o_ref[:, b * LANES:(b + 1) * LANES] = out.astype(BF16)
                else:
                    stage[pl.ds(r, rows, stride=d), :] = out
            if d != 1:
                o_ref[:, b * LANES:(b + 1) * LANES] = stage[...].astype(BF16)

    in_specs = [pl.BlockSpec((rows, n_units * LANES), lambda i: (i, 0))] * 3
    in_specs += [pl.BlockSpec((tm, LANES), lambda i, b=b: (i, cb0 + b)) for b in range(n_in)]
    vec = pl.BlockSpec((1, LANES), lambda i: (0, 0))
    tab = pl.BlockSpec((tm, LANES), lambda i: (i, 0))
    return pl.pallas_call(
        body, name=name, grid=(s_len // tm,), in_specs=in_specs + [vec, vec, tab, tab, tab],
        out_specs=(pl.BlockSpec((tm, n_in * LANES), lambda i: (i, 0)), vec, vec),
        out_shape=(jax.ShapeDtypeStruct((s_len, n_in * LANES), BF16), jax.ShapeDtypeStruct((1, LANES), F32),
                   jax.ShapeDtypeStruct((1, LANES), F32)),
        scratch_shapes=[pltpu.VMEM((tm, LANES), F32)], compiler_params=_cparams(1),
    )(*dqkv, *([proj] * n_in), gq, gk, *tabs)


def _head_masks(shape):
    lane = lax.broadcasted_iota(jnp.int32, shape, 1)
    return lane < HEAD_DIM, lane >= HEAD_DIM


def _band_fwd(qkv, n_units, max_dist, sinks, name):
    n_rows = qkv.shape[0]
    nb = n_rows // BLOCK
    scale = HEAD_DIM ** -0.5
    has_sink = sinks is not None
    assert not has_sink or max_dist < BLOCK

    qn, un = min(nb, BAND_Q_BLOCKS), BAND_UNITS
    ug = n_units // un

    def body(*refs):
        q_ref, kp_ref, km_ref, vp_ref, vm_ref = refs[:5]
        o_ref, lse_ref = refs[-2:]
        i = pl.program_id(1)
        qi = lax.broadcasted_iota(jnp.int32, (BLOCK, 2 * BLOCK), 0)
        kj = lax.broadcasted_iota(jnp.int32, (BLOCK, 2 * BLOCK), 1)
        dist = qi + BLOCK - kj
        band = (dist >= 0) & (dist <= max_dist)
        band_first = band & ((i > 0) | (kj >= BLOCK))
        m0, m1 = _head_masks((BLOCK, LANES))
        zero = jnp.zeros((BLOCK, LANES), BF16)
        for ub in range(un):
            cs = slice(ub * LANES, (ub + 1) * LANES)
            for qb in range(qn):
                rs = slice(qb * BLOCK, (qb + 1) * BLOCK)
                q = q_ref[rs, cs]
                if qb == 0:
                    kk = jnp.concatenate([kp_ref[:, cs], km_ref[0:BLOCK, cs]], axis=0)
                    vv = jnp.concatenate([vp_ref[:, cs], vm_ref[0:BLOCK, cs]], axis=0)
                    valid = band_first
                else:
                    kk = km_ref[(qb - 1) * BLOCK:(qb + 1) * BLOCK, cs]
                    vv = vm_ref[(qb - 1) * BLOCK:(qb + 1) * BLOCK, cs]
                    valid = band
                outs, lses = [], []
                for e, hm in enumerate((m0, m1)):
                    qe = jnp.where(hm, q, zero)
                    s = lax.dot_general(qe, kk, (((1,), (1,)), ((), ())), preferred_element_type=F32) * scale
                    s = jnp.where(valid, s, -jnp.inf)
                    if has_sink:
                        s = jnp.where(kj == 0, refs[5][ub][:, e * HEAD_DIM:e * HEAD_DIM + 1], s)
                    mx = jnp.max(s, axis=-1, keepdims=True)
                    p = jnp.exp(s - mx)
                    den = jnp.sum(p, axis=-1, keepdims=True)
                    pn = p * (1.0 / den)
                    if has_sink:
                        pn = jnp.where(kj == 0, 0.0, pn)
                    pn = pn.astype(BF16)
                    outs.append(jnp.dot(pn, vv, preferred_element_type=F32))
                    lses.append(mx + jnp.log(den))
                o_ref[rs, cs] = jnp.where(m0, outs[0], outs[1])
                lse_ref[rs, cs] = jnp.where(m0, jnp.broadcast_to(lses[0], (BLOCK, LANES)),
                                            jnp.broadcast_to(lses[1], (BLOCK, LANES)))

    def main(sec):
        return pl.BlockSpec((qn * BLOCK, un * LANES), lambda u, i: (i, sec * ug + u))

    def prev(sec):
        return pl.BlockSpec((BLOCK, un * LANES), lambda u, i: (jnp.maximum(i * qn - 1, 0), sec * ug + u))

    in_specs = [main(0), prev(1), main(1), prev(2), main(2)]
    args = [qkv] * 5
    if has_sink:
        in_specs.append(pl.BlockSpec((un, 1, LANES), lambda u, i: (u, 0, 0)))
        args.append(sinks)
    return pl.pallas_call(
        body, name=name, grid=(ug, nb // qn), in_specs=in_specs, out_specs=(main(0), main(0)),
        out_shape=(jax.ShapeDtypeStruct((n_rows, n_units * LANES), F32),) * 2, compiler_params=_cparams(2),
    )(*args)


def _band_bwd(qkv, do, lse, delta, n_units, max_dist, name):
    n_rows = qkv.shape[0]
    nb = n_rows // BLOCK
    scale = HEAD_DIM ** -0.5

    qn, un = min(nb, BAND_Q_BLOCKS), BAND_UNITS
    ug = n_units // un
    steps = nb // qn
    nt_dims = (((1,), (1,)), ((), ()))
    tn_dims = (((0,), (0,)), ((), ()))

    def body(qm_ref, qx_ref, kp_ref, km_ref, vp_ref, vm_ref, dom_ref, dox_ref, lm_ref, lx_ref, dm_ref, dx_ref,
             dq_ref, dk_ref, dv_ref):
        i = pl.program_id(1)
        m0, m1 = _head_masks((BLOCK, LANES))
        zero = jnp.zeros((BLOCK, LANES), BF16)
        qi = lax.broadcasted_iota(jnp.int32, (BLOCK, 2 * BLOCK), 0)
        kj = lax.broadcasted_iota(jnp.int32, (BLOCK, 2 * BLOCK), 1)
        dist = qi + BLOCK - kj
        band = (dist >= 0) & (dist <= max_dist)
        band_first = band & ((i > 0) | (kj >= BLOCK))
        qr = lax.broadcasted_iota(jnp.int32, (BLOCK, BLOCK), 0)
        kc = lax.broadcasted_iota(jnp.int32, (BLOCK, BLOCK), 1)
        dist_x = qr + BLOCK - kc
        band_next = (dist_x >= 0) & (dist_x <= max_dist) & (i < steps - 1)

        def pair(q, dob, lse_b, del_b, kk, vv, valid):
            dqs, dk, dv = [], None, None
            for e, hm in enumerate((m0, m1)):
                col = slice(e * HEAD_DIM, e * HEAD_DIM + 1)
                qe = jnp.where(hm, q, zero)
                doe = jnp.where(hm, dob, zero)
                s = lax.dot_general(qe, kk, nt_dims, preferred_element_type=F32) * scale
                p = jnp.where(valid, jnp.exp(s - lse_b[:, col]), 0.0)
                dp = lax.dot_general(doe, vv, nt_dims, preferred_element_type=F32)
                ds = (p * (dp - del_b[:, col]) * scale).astype(BF16)
                dqs.append(jnp.dot(ds, kk, preferred_element_type=F32))
                dk_e = lax.dot_general(ds, qe, tn_dims, preferred_element_type=F32)
                dv_e = lax.dot_general(p.astype(BF16), doe, tn_dims, preferred_element_type=F32)
                dk = dk_e if dk is None else dk + dk_e
                dv = dv_e if dv is None else dv + dv_e
            return jnp.where(m0, dqs[0], dqs[1]), dk, dv

        for ub in range(un):
            cs = slice(ub * LANES, (ub + 1) * LANES)
            dk_acc, dv_acc = [None] * qn, [None] * qn

            def add(acc, kb, part):
                acc[kb] = part if acc[kb] is None else acc[kb] + part

            for qb in range(qn):
                rs = slice(qb * BLOCK, (qb + 1) * BLOCK)
                if qb == 0:
                    kk = jnp.concatenate([kp_ref[:, cs], km_ref[0:BLOCK, cs]], axis=0)
                    vv = jnp.concatenate([vp_ref[:, cs], vm_ref[0:BLOCK, cs]], axis=0)
                    valid = band_first
                else:
                    kk = km_ref[(qb - 1) * BLOCK:(qb + 1) * BLOCK, cs]
                    vv = vm_ref[(qb - 1) * BLOCK:(qb + 1) * BLOCK, cs]
                    valid = band
                dq, dk, dv = pair(qm_ref[rs, cs], dom_ref[rs, cs], lm_ref[rs, cs], dm_ref[rs, cs], kk, vv, valid)
                dq_ref[rs, cs] = dq
                if qb > 0:
                    add(dk_acc, qb - 1, dk[0:BLOCK])
                    add(dv_acc, qb - 1, dv[0:BLOCK])
                add(dk_acc, qb, dk[BLOCK:2 * BLOCK])
                add(dv_acc, qb, dv[BLOCK:2 * BLOCK])
            last = slice((qn - 1) * BLOCK, qn * BLOCK)
            _, dk, dv = pair(qx_ref[:, cs], dox_ref[:, cs], lx_ref[:, cs], dx_ref[:, cs], km_ref[last, cs], vm_ref[last, cs],
                             band_next)
            add(dk_acc, qn - 1, dk)
            add(dv_acc, qn - 1, dv)
            for kb in range(qn):
                dk_ref[kb * BLOCK:(kb + 1) * BLOCK, cs] = dk_acc[kb]
                dv_ref[kb * BLOCK:(kb + 1) * BLOCK, cs] = dv_acc[kb]

    def main(sec):
        return pl.BlockSpec((qn * BLOCK, un * LANES), lambda u, i: (i, sec * ug + u))

    def prev(sec):
        return pl.BlockSpec((BLOCK, un * LANES), lambda u, i: (jnp.maximum(i * qn - 1, 0), sec * ug + u))

    def nxt(sec):
        return pl.BlockSpec((BLOCK, un * LANES), lambda u, i: (jnp.minimum((i + 1) * qn, nb - 1), sec * ug + u))

    in_specs = [main(0), nxt(0), prev(1), main(1), prev(2), main(2),
                main(0), nxt(0), main(0), nxt(0), main(0), nxt(0)]
    args = [qkv] * 6 + [do, do, lse, lse, delta, delta]
    shp = jax.ShapeDtypeStruct((n_rows, n_units * LANES), F32)
    return pl.pallas_call(
        body, name=name, grid=(ug, steps), in_specs=in_specs, out_specs=(main(0), main(0), main(0)),
        out_shape=(shp, shp, shp), compiler_params=_cparams(2),
    )(*args)


def _merge_groups(os_, lses, dils, name):
    s_len = os_[0].shape[0] * dils[0]
    tm = 512

    def body(*refs):
        o_refs, l_refs = refs[0:3], refs[3:6]
        o_ref, lse_ref = refs[6:8]
        so, sl = refs[8:11], refs[11:14]
        for pair in range(2):
            for g, d in enumerate(dils):
                rows = tm // d
                for r in range(d):
                    col = slice((pair * d + r) * LANES, (pair * d + r + 1) * LANES)
                    if d == 1:
                        so[g][...] = o_refs[g][:, col]
                        sl[g][...] = l_refs[g][:, col]
                    else:
                        so[g][pl.ds(r, rows, stride=d), :] = o_refs[g][:, col]
                        sl[g][pl.ds(r, rows, stride=d), :] = l_refs[g][:, col]
            l0, l1, l2 = sl[0][...], sl[1][...], sl[2][...]
            mx = jnp.maximum(jnp.maximum(l0, l1), l2)
            e0, e1, e2 = jnp.exp(l0 - mx), jnp.exp(l1 - mx), jnp.exp(l2 - mx)
            den = e0 + e1 + e2
            inv = 1.0 / den
            o_ref[:, pair * LANES:(pair + 1) * LANES] = (so[0][...] * (e0 * inv) + so[1][...] * (e1 * inv)
                                                         + so[2][...] * (e2 * inv))
            lse_ref[:, pair * LANES:(pair + 1) * LANES] = mx + jnp.log(den)

    in_specs = [pl.BlockSpec((tm // d, 2 * d * LANES), lambda i: (i, 0)) for d in dils] * 2
    out = pl.BlockSpec((tm, 2 * LANES), lambda i: (i, 0))
    shp = jax.ShapeDtypeStruct((s_len, 2 * LANES), F32)
    return pl.pallas_call(
        body, name=name, grid=(s_len // tm,), in_specs=in_specs, out_specs=(out, out), out_shape=(shp, shp),
        scratch_shapes=[pltpu.VMEM((tm, LANES), F32)] * 6, compiler_params=_cparams(1),
    )(*os_, *lses)


def _bwd_prep(do, o, lse, dils, sinks, name):
    s_len, width = do.shape
    n_pairs = width // LANES
    tm = 512
    has_sink = sinks is not None
    n_g = len(dils)

    def body(*refs):
        do_ref, o_ref, lse_ref = refs[:3]
        pos = 3
        if has_sink:
            sink_ref = refs[pos]
            pos += 1
        outs = refs[pos:pos + 3 * n_g]
        pos += 3 * n_g
        if has_sink:
            dsink_ref = refs[pos]
            pos += 1
        s_do, s_l, s_d = refs[pos:pos + 3]
        seg = _seg_matrix(HEAD_DIM)

        if has_sink:
            @pl.when(pl.program_id(0) == 0)
            def _():
                dsink_ref[...] = jnp.zeros_like(dsink_ref)

        for pair in range(n_pairs):
            col = slice(pair * LANES, (pair + 1) * LANES)
            dov = do_ref[:, col]
            lv = lse_ref[:, col]
            delta = _seg_sum(dov * o_ref[:, col], seg)
            if has_sink:
                dsink_ref[pair] += -jnp.sum(jnp.exp(sink_ref[pair] - lv) * delta, axis=0, keepdims=True)
            s_do[...] = dov
            s_l[...] = lv
            s_d[...] = delta
            for g, d in enumerate(dils):
                rows = tm // d
                for r in range(d):
                    oc = slice((pair * d + r) * LANES, (pair * d + r + 1) * LANES)
                    if d == 1:
                        a, b, c = s_do[...], s_l[...], s_d[...]
                    else:
                        a = s_do[pl.ds(r, rows, stride=d), :]
                        b = s_l[pl.ds(r, rows, stride=d), :]
                        c = s_d[pl.ds(r, rows, stride=d), :]
                    outs[3 * g][:, oc] = a.astype(BF16)
                    outs[3 * g + 1][:, oc] = b
                    outs[3 * g + 2][:, oc] = c

    row = pl.BlockSpec((tm, width), lambda i: (i, 0))
    in_specs = [row, row, row]
    args = [do, o, lse]
    if has_sink:
        in_specs.append(pl.BlockSpec((n_pairs, 1, LANES), lambda i: (0, 0, 0)))
        args.append(sinks)
    out_specs, out_shape = [], []
    for d in dils:
        for dt in (BF16, F32, F32):
            out_specs.append(pl.BlockSpec((tm // d, n_pairs * d * LANES), lambda i: (i, 0)))
            out_shape.append(jax.ShapeDtypeStruct((s_len // d, n_pairs * d * LANES), dt))
    if has_sink:
        out_specs.append(pl.BlockSpec((n_pairs, 1, LANES), lambda i: (0, 0, 0)))
        out_shape.append(jax.ShapeDtypeStruct((n_pairs, 1, LANES), F32))
    return pl.pallas_call(
        body, name=name, grid=(s_len // tm,), in_specs=in_specs, out_specs=tuple(out_specs),
        out_shape=tuple(out_shape), scratch_shapes=[pltpu.VMEM((tm, LANES), F32)] * 3, compiler_params=_cparams(1),
    )(*args)


def _mem_kv(mem, mem_gain, w_kv, k_gain, name):
    m_len = mem.shape[0]
    kw = M_HEADS * M_HEAD_DIM

    def body(mem_ref, mg_ref, w_ref, kg_ref, k_ref, v_ref):
        mv = mem_ref[...]
        r = lax.rsqrt(jnp.mean(mv * mv, axis=-1, keepdims=True) + EPS)
        mn = ((mv * r) * mg_ref[...]).astype(BF16)
        kv = jnp.dot(mn, w_ref[...], preferred_element_type=F32)
        for h in range(M_HEADS):
            col = slice(h * M_HEAD_DIM, (h + 1) * M_HEAD_DIM)
            t = kv[:, col]
            rk = lax.rsqrt(jnp.mean(t * t, axis=-1, keepdims=True) + EPS)
            k_ref[:, col] = ((t * rk) * kg_ref[...]).astype(BF16)
        v_ref[...] = kv[:, kw:].astype(BF16)

    shp = jax.ShapeDtypeStruct((m_len, kw), BF16)
    return pl.pallas_call(body, name=name, out_shape=(shp, shp),
                          compiler_params=pltpu.CompilerParams(vmem_limit_bytes=VMEM_LIMIT_BYTES))(mem, mem_gain, w_kv, k_gain)


def _mem_kv_bwd(mem, mem_gain, w_kv, k_gain, dk, dv, name):
    m_len, d = mem.shape
    kw = M_HEADS * M_HEAD_DIM

    def body(mem_ref, mg_ref, w_ref, kg_ref, dk_ref, dv_ref, dw_ref, dmg_ref, dkg_ref, dkv_ref):
        mv = mem_ref[...]
        r = lax.rsqrt(jnp.mean(mv * mv, axis=-1, keepdims=True) + EPS)
        mhat = mv * r
        mn = (mhat * mg_ref[...]).astype(BF16)
        kv = jnp.dot(mn, w_ref[...], preferred_element_type=F32)
        dkg = jnp.zeros((1, M_HEAD_DIM), F32)
        for h in range(M_HEADS):
            col = slice(h * M_HEAD_DIM, (h + 1) * M_HEAD_DIM)
            t = kv[:, col]
            rk = lax.rsqrt(jnp.mean(t * t, axis=-1, keepdims=True) + EPS)
            that = t * rk
            dy = dk_ref[:, col]
            dkg = dkg + jnp.sum(dy * that, axis=0, keepdims=True)
            dthat = dy * kg_ref[...]
            dkv_ref[:, col] = (rk * (dthat - that * jnp.mean(dthat * that, axis=-1, keepdims=True))).astype(BF16)
        dkv_ref[:, kw:] = dv_ref[...].astype(BF16)
        dkg_ref[...] = dkg
        dkv = dkv_ref[...]
        dw_ref[...] = lax.dot_general(mn, dkv, (((0,), (0,)), ((), ())), preferred_element_type=F32)
        dmn = lax.dot_general(dkv, w_ref[...], (((1,), (1,)), ((), ())), preferred_element_type=F32)
        dmg_ref[...] = jnp.sum(dmn * mhat, axis=0, keepdims=True)

    return pl.pallas_call(
        body, name=name,
        out_shape=(jax.ShapeDtypeStruct((d, 2 * kw), F32), jax.ShapeDtypeStruct((1, d), F32),
                   jax.ShapeDtypeStruct((1, M_HEAD_DIM), F32)),
        scratch_shapes=[pltpu.VMEM((m_len, 2 * kw), BF16)],
        compiler_params=pltpu.CompilerParams(vmem_limit_bytes=VMEM_LIMIT_BYTES),
    )(mem, mem_gain, w_kv, k_gain, dk, dv)


def _mem_attn_fwd(proj, cidx, mk, mv, q_gain, name):
    s_len = proj.shape[0]
    kw = M_HEADS * M_HEAD_DIM
    tm = 512
    scale = M_HEAD_DIM ** -0.5

    def body(q_ref, k_ref, v_ref, g_ref, o_ref):
        for h in range(M_HEADS):
            col = slice(h * M_HEAD_DIM, (h + 1) * M_HEAD_DIM)
            t = q_ref[:, col]
            rs = lax.rsqrt(jnp.mean(t * t, axis=-1, keepdims=True) + EPS)
            qn = ((t * rs) * g_ref[...]).astype(BF16)
            s = lax.dot_general(qn, k_ref[:, col], (((1,), (1,)), ((), ())), preferred_element_type=F32) * scale
            mx = jnp.max(s, axis=-1, keepdims=True)
            p = jnp.exp(s - mx)
            pn = (p * (1.0 / jnp.sum(p, axis=-1, keepdims=True))).astype(BF16)
            o_ref[:, col] = jnp.dot(pn, v_ref[:, col], preferred_element_type=F32).astype(BF16)

    whole = pl.BlockSpec((MEM_LEN, kw), lambda i: (0, 0))
    return pl.pallas_call(
        body, name=name, grid=(s_len // tm,),
        in_specs=[pl.BlockSpec((tm, kw), lambda i: (i, cidx)), whole, whole, pl.BlockSpec((1, M_HEAD_DIM), lambda i: (0, 0))],
        out_specs=pl.BlockSpec((tm, kw), lambda i: (i, 0)),
        out_shape=jax.ShapeDtypeStruct((s_len, kw), BF16), compiler_params=_cparams(1),
    )(proj, mk, mv, q_gain)


def _mem_attn_bwd(proj, cidx, mk, mv, q_gain, do, name):
    s_len = proj.shape[0]
    kw = M_HEADS * M_HEAD_DIM
    tm = 512
    scale = M_HEAD_DIM ** -0.5

    def body(q_ref, k_ref, v_ref, g_ref, do_ref, dq_ref, dk_ref, dv_ref, dg_ref):
        @pl.when(pl.program_id(0) == 0)
        def _():
            dk_ref[...] = jnp.zeros_like(dk_ref)
            dv_ref[...] = jnp.zeros_like(dv_ref)
            dg_ref[...] = jnp.zeros_like(dg_ref)

        for h in range(M_HEADS):
            col = slice(h * M_HEAD_DIM, (h + 1) * M_HEAD_DIM)
            t = q_ref[:, col]
            rs = lax.rsqrt(jnp.mean(t * t, axis=-1, keepdims=True) + EPS)
            that = t * rs
            qn = (that * g_ref[...]).astype(BF16)
            kh, vh = k_ref[:, col], v_ref[:, col]
            dob = do_ref[:, col].astype(BF16)
            s = lax.dot_general(qn, kh, (((1,), (1,)), ((), ())), preferred_element_type=F32) * scale
            mx = jnp.max(s, axis=-1, keepdims=True)
            p = jnp.exp(s - mx)
            p = p * (1.0 / jnp.sum(p, axis=-1, keepdims=True))
            dp = lax.dot_general(dob, vh, (((1,), (1,)), ((), ())), preferred_element_type=F32)
            ds = (p * (dp - jnp.sum(p * dp, axis=-1, keepdims=True)) * scale).astype(BF16)
            dqn = jnp.dot(ds, kh, preferred_element_type=F32)
            dk_ref[:, col] += lax.dot_general(ds, qn, (((0,), (0,)), ((), ())), preferred_element_type=F32)
            dv_ref[:, col] += lax.dot_general(p.astype(BF16), dob, (((0,), (0,)), ((), ())), preferred_element_type=F32)
            dg_ref[...] += jnp.sum(dqn * that, axis=0, keepdims=True)
            dthat = dqn * g_ref[...]
            dq_ref[:, col] = (rs * (dthat - that * jnp.mean(dthat * that, axis=-1, keepdims=True))).astype(BF16)

    whole = pl.BlockSpec((MEM_LEN, kw), lambda i: (0, 0))
    vec = pl.BlockSpec((1, M_HEAD_DIM), lambda i: (0, 0))
    row = pl.BlockSpec((tm, kw), lambda i: (i, 0))
    return pl.pallas_call(
        body, name=name, grid=(s_len // tm,),
        in_specs=[pl.BlockSpec((tm, kw), lambda i: (i, cidx)), whole, whole, vec, row],
        out_specs=(row, whole, whole, vec),
        out_shape=(jax.ShapeDtypeStruct((s_len, kw), BF16), jax.ShapeDtypeStruct((MEM_LEN, kw), F32),
                   jax.ShapeDtypeStruct((MEM_LEN, kw), F32), jax.ShapeDtypeStruct((1, M_HEAD_DIM), F32)),
        compiler_params=_cparams(1),
    )(proj, mk, mv, q_gain, do)


def _gate_merge(gates, pa, pb, pm, name):
    s_len, d = pa.shape
    tm = 256

    def body(g_ref, a_ref, b_ref, m_ref, o_ref, ot_ref):
        f = lambda v: v.astype(F32)
        merged = (f(g_ref[:, 0:d]) * f(a_ref[...]) + f(g_ref[:, d:2 * d]) * f(b_ref[...])
                  + f(g_ref[:, 2 * d:3 * d]) * f(m_ref[...])).astype(BF16)
        o_ref[...] = merged
        ot_ref[...] = merged.T

    row = pl.BlockSpec((tm, d), lambda i: (i, 0))
    return pl.pallas_call(
        body, name=name, grid=(s_len // tm,), in_specs=[pl.BlockSpec((tm, 3 * d), lambda i: (i, 0)), row, row, row],
        out_specs=(row, pl.BlockSpec((d, tm), lambda i: (0, i))),
        out_shape=(jax.ShapeDtypeStruct((s_len, d), BF16), jax.ShapeDtypeStruct((d, s_len), BF16)),
        compiler_params=_cparams(1),
    )(gates, pa, pb, pm)


def _gate_merge_bwd(dmerged, gates, pa, pb, pm, name):
    s_len, d = pa.shape
    tm = 256

    def body(dm_ref, g_ref, a_ref, b_ref, m_ref, da_ref, db_ref, dmm_ref, dg_ref, dbg_ref):
        @pl.when(pl.program_id(0) == 0)
        def _():
            dbg_ref[...] = jnp.zeros_like(dbg_ref)
        dm = dm_ref[...]
        for k, (p_ref, dp_ref) in enumerate(((a_ref, da_ref), (b_ref, db_ref), (m_ref, dmm_ref))):
            col = slice(k * d, (k + 1) * d)
            g = g_ref[:, col].astype(F32)
            dp_ref[...] = (dm * g).astype(BF16)
            dpre = (dm * p_ref[...].astype(F32)) * (g * (1.0 - g))
            dbg_ref[:, col] += jnp.sum(dpre, axis=0, keepdims=True)
            dg_ref[:, col] = dpre.astype(BF16)

    row = pl.BlockSpec((tm, d), lambda i: (i, 0))
    wide = pl.BlockSpec((tm, 3 * d), lambda i: (i, 0))
    shp = jax.ShapeDtypeStruct((s_len, d), BF16)
    return pl.pallas_call(
        body, name=name, grid=(s_len // tm,), in_specs=[row, wide, row, row, row],
        out_specs=(row, row, row, wide, pl.BlockSpec((1, 3 * d), lambda i: (0, 0))),
        out_shape=(shp, shp, shp, jax.ShapeDtypeStruct((s_len, 3 * d), BF16), jax.ShapeDtypeStruct((1, 3 * d), F32)),
        compiler_params=_cparams(1),
    )(dmerged, gates, pa, pb, pm)


CONV_CHUNK = 256


def _pick_row(tile, j):
    row = lax.broadcasted_iota(jnp.int32, tile.shape, 0)
    return jnp.sum(jnp.where(row == j, tile, jnp.zeros_like(tile)), axis=0, keepdims=True)


def _rows_before(ref, start, k):
    cur = ref[pl.ds(start, CONV_CHUNK), :].astype(F32)
    prev = ref[pl.ds(pl.multiple_of(jnp.maximum(start - 16, 0), 16), 16), :].astype(F32)
    prev = jnp.where(start > 0, prev, jnp.zeros_like(prev))
    rolled = pltpu.roll(cur, k, 0)
    row = lax.broadcasted_iota(jnp.int32, cur.shape, 0)
    for j in range(k):
        rolled = jnp.where(row == j, _pick_row(prev, 16 - k + j), rolled)
    return rolled


def _rows_after(ref, start, k):
    cur = ref[pl.ds(start, CONV_CHUNK), :]
    nxt = ref[pl.ds(pl.multiple_of(start + CONV_CHUNK, 8), 8), :]
    rolled = pltpu.roll(cur, CONV_CHUNK - k, 0)
    row = lax.broadcasted_iota(jnp.int32, cur.shape, 0)
    for j in range(k):
        rolled = jnp.where(row == CONV_CHUNK - k + j, _pick_row(nxt, j), rolled)
    return rolled


def _conv_pre(u_ref, w_ref, b_ref, start):
    u2 = _rows_before(u_ref, start, 2)
    u1 = _rows_before(u_ref, start, 1)
    u0 = u_ref[pl.ds(start, CONV_CHUNK), :].astype(F32)
    c = ((b_ref[...] + w_ref[0:1, :] * u2) + w_ref[1:2, :] * u1) + w_ref[2:3, :] * u0
    return c, (u2, u1, u0)


def _conv_glu(u, conv_w, conv_b, name):
    s_len = u.shape[0]
    nblk = D_FF // LANES

    def body(ua_ref, ug_ref, wa_ref, wg_ref, ba_ref, bg_ref, o_ref, ot_ref):
        def chunk(ci, carry):
            start = pl.multiple_of(ci * CONV_CHUNK, CONV_CHUNK)
            ca, _ = _conv_pre(ua_ref, wa_ref, ba_ref, start)
            cg, _ = _conv_pre(ug_ref, wg_ref, bg_ref, start)
            act = ((ca * _sigmoid(ca)) * cg).astype(BF16)
            o_ref[pl.ds(start, CONV_CHUNK), :] = act
            ot_ref[:, pl.ds(start, CONV_CHUNK)] = act.T
            return carry
        lax.fori_loop(0, s_len // CONV_CHUNK, chunk, 0)

    def col(rows, off):
        return pl.BlockSpec((rows, LANES), lambda j: (0, off + j))

    return pl.pallas_call(
        body, name=name, grid=(nblk,),
        in_specs=[col(s_len, 0), col(s_len, nblk), col(3, 0), col(3, nblk), col(1, 0), col(1, nblk)],
        out_specs=(col(s_len, 0), pl.BlockSpec((LANES, s_len), lambda j: (j, 0))),
        out_shape=(jax.ShapeDtypeStruct((s_len, D_FF), BF16), jax.ShapeDtypeStruct((D_FF, s_len), BF16)),
        compiler_params=_cparams(1),
    )(u, u, conv_w, conv_w, conv_b, conv_b)


def _conv_glu_bwd(dact, u, conv_w, conv_b, name):
    s_len = u.shape[0]
    nblk = D_FF // LANES
    n_chunks = s_len // CONV_CHUNK

    def body(da_ref, ua_ref, ug_ref, wa_ref, wg_ref, ba_ref, bg_ref,
             dua_ref, dug_ref, dwa_ref, dwg_ref, dba_ref, dbg_ref, sa, sg):
        sa[pl.ds(s_len, 8), :] = jnp.zeros((8, LANES), F32)
        sg[pl.ds(s_len, 8), :] = jnp.zeros((8, LANES), F32)
        zero = jnp.zeros((1, LANES), F32)

        def chunk1(ci, carry):
            start = pl.multiple_of(ci * CONV_CHUNK, CONV_CHUNK)
            ca, ua = _conv_pre(ua_ref, wa_ref, ba_ref, start)
            cg, ug = _conv_pre(ug_ref, wg_ref, bg_ref, start)
            dact_v = da_ref[pl.ds(start, CONV_CHUNK), :].astype(F32)
            sig = _sigmoid(ca)
            dcg = dact_v * (ca * sig)
            dca = (dact_v * cg) * (sig * (1.0 + ca * (1.0 - sig)))
            sa[pl.ds(start, CONV_CHUNK), :] = dca
            sg[pl.ds(start, CONV_CHUNK), :] = dcg
            out = [carry[0] + jnp.sum(dca, axis=0, keepdims=True), carry[1] + jnp.sum(dcg, axis=0, keepdims=True)]
            for j in range(3):
                out.append(carry[2 + j] + jnp.sum(dca * ua[j], axis=0, keepdims=True))
            for j in range(3):
                out.append(carry[5 + j] + jnp.sum(dcg * ug[j], axis=0, keepdims=True))
            return tuple(out)

        acc = lax.fori_loop(0, n_chunks, chunk1, (zero,) * 8)
        dba_ref[...] = acc[0]
        dbg_ref[...] = acc[1]
        for j in range(3):
            dwa_ref[j:j + 1, :] = acc[2 + j]
            dwg_ref[j:j + 1, :] = acc[5 + j]

        def chunk2(ci, carry):
            start = pl.multiple_of(ci * CONV_CHUNK, CONV_CHUNK)
            for s_ref, w_ref, o_ref in ((sa, wa_ref, dua_ref), (sg, wg_ref, dug_ref)):
                d0 = s_ref[pl.ds(start, CONV_CHUNK), :]
                d1 = _rows_after(s_ref, start, 1)
                d2 = _rows_after(s_ref, start, 2)
                o_ref[pl.ds(start, CONV_CHUNK), :] = (w_ref[2:3, :] * d0 + w_ref[1:2, :] * d1
                                                      + w_ref[0:1, :] * d2).astype(BF16)
            return carry
        lax.fori_loop(0, n_chunks, chunk2, 0)

    def col(rows, off):
        return pl.BlockSpec((rows, LANES), lambda j: (0, off + j))

    big = jax.ShapeDtypeStruct((s_len, D_FF), BF16)
    return pl.pallas_call(
        body, name=name, grid=(nblk,),
        in_specs=[col(s_len, 0), col(s_len, 0), col(s_len, nblk), col(3, 0), col(3, nblk), col(1, 0), col(1, nblk)],
        out_specs=(col(s_len, 0), col(s_len, 0), col(3, 0), col(3, 0), col(1, 0), col(1, 0)),
        out_shape=(big, big, jax.ShapeDtypeStruct((3, D_FF), F32), jax.ShapeDtypeStruct((3, D_FF), F32),
                   jax.ShapeDtypeStruct((1, D_FF), F32), jax.ShapeDtypeStruct((1, D_FF), F32)),
        scratch_shapes=[pltpu.VMEM((s_len + 8, LANES), F32)] * 2, compiler_params=_cparams(1),
    )(dact, u, u, conv_w, conv_w, conv_b, conv_b)


def _rope_tables(positions):
    half = ROPE_DIMS // 2
    freqs = jnp.exp(jnp.arange(half, dtype=F32) * (-2.0 * math.log(ROPE_THETA) / ROPE_DIMS))
    ang = positions.reshape(-1).astype(F32)[:, None] * freqs
    cos, sin = jnp.cos(ang), jnp.sin(ang)
    n = ang.shape[0]
    zeros = lambda w: jnp.zeros((n, w), F32)
    c = jnp.concatenate([cos, cos, jnp.ones((n, HEAD_DIM - ROPE_DIMS), F32)], axis=1)
    s1 = jnp.concatenate([-sin, zeros(HEAD_DIM - half)], axis=1)
    s2 = jnp.concatenate([zeros(half), sin, zeros(HEAD_DIM - ROPE_DIMS)], axis=1)
    return tuple(jnp.tile(t, (1, 2)) for t in (c, s1, s2))


def _two(v):
    return jnp.tile(v.reshape(1, HEAD_DIM), (1, 2))


def _fold_heads(g):
    return g[0, :HEAD_DIM] + g[0, HEAD_DIM:]


MIX_WEIGHTS = ('w_gate', 'w_mem_kv', 'w_o_a', 'w_o_b', 'w_o_m', 'w_out')
FFN_WEIGHTS = ('w_up', 'conv_w', 'w_down')


def _device_step(x, mem, positions, target, w, hooks=None):
    tabs = _rope_tables(positions)
    dils = tuple(d for _, d in A_GROUPS)
    grads = {}
    w = dict(w)

    h, h_t, r1 = _rms_fwd(x, w['attn_norm'], "rms1")
    proj = _mm_rows([(h, w['w_in'], 0)], "mm_in")

    qkv_a, o_g, lse_g = [], [], []
    for gi, (window, d) in enumerate(A_GROUPS):
        gq, gk = _two(w['a_q_norm'][gi]), _two(w['a_k_norm'][gi])
        qkv = _qk_prep(proj, 6 * gi, d, False, gq, gk, tabs, f"qk_prep_a{gi}")
        o, lse = _band_fwd(qkv, 2 * d, window // d, None, f"band_fwd_a{gi}")
        qkv_a.append(qkv)
        o_g.append(o)
        lse_g.append(lse)
    o_a, lse_a = _merge_groups(o_g, lse_g, dils, "merge_a")
    if hooks is not None:
        w.update(hooks.weights('mix', o_a))

    gbq, gbk = _two(w['b_q_norm']), _two(w['b_k_norm'])
    sinks = jnp.repeat(w['b_sinks'].reshape(4, 2), HEAD_DIM, axis=1).reshape(4, 1, LANES)
    qkv_b = _qk_prep(proj, 18, 1, True, gbq, gbk, tabs, "qk_prep_b")
    o_b, lse_b = _band_fwd(qkv_b, 4, B_WINDOW - 1, sinks, "band_fwd_b")

    gates = _mm_rows([(h, w['w_gate'], 0)], "mm_gate", bias=w['b_gate'], sigmoid=True, out_dtypes=(BF16,))
    mk, mv = _mem_kv(mem, w['mem_norm'], w['w_mem_kv'], w['m_k_norm'], "mem_kv")
    o_m = _mem_attn_fwd(proj, 6, mk, mv, w['m_q_norm'], "mem_attn")

    pa = _mm_rows([(o_a, w['w_o_a'], 0)], "mm_oa", out_dtypes=(BF16,))
    pb = _mm_rows([(o_b, w['w_o_b'], 0)], "mm_ob", out_dtypes=(BF16,))
    pm = _mm_rows([(o_m, w['w_o_m'], 0)], "mm_om", out_dtypes=(BF16,))
    merged, merged_t = _gate_merge(gates, pa, pb, pm, "gate_merge")
    x1 = _mm_rows([(merged, w['w_out'], 0)], "mm_out", res=x)

    if hooks is not None:
        w.update(hooks.weights('ffn', x1))
    h2, h2_t, r2 = _rms_fwd(x1, w['ffn_norm'], "rms2")
    u = _mm_rows([(h2, w['w_up'], 0)], "mm_up", out_dtypes=(BF16,))
    act, act_t = _conv_glu(u, w['conv_w'], w['conv_b'], "conv_glu")
    dy, dy_b, loss = _mm_rows([(act, w['w_down'], 0)], "mm_down", res=x1, loss_target=target)

    dact = _mm_rows([(dy_b, w['w_down'], 0)], "mm_d_act", nt=True, out_dtypes=(BF16,))
    grads['w_down'] = _mm_rows([(act_t, dy_b, 0)], "mm_dw_down", tm=256)
    du_a, du_g, dcw_a, dcw_g, dcb_a, dcb_g = _conv_glu_bwd(dact, u, w['conv_w'], w['conv_b'], "conv_glu_bwd")
    grads['conv_w'] = jnp.concatenate([dcw_a, dcw_g], axis=1)
    grads['conv_b'] = jnp.concatenate([dcb_a, dcb_g], axis=1)
    dh2 = _mm_rows([(du_a, w['w_up'], 0), (du_g, w['w_up'], 1)], "mm_d_h2", nt=True)
    grads['w_up'] = jnp.concatenate([_mm_cols(h2_t, du_a, "mm_dw_up_a"), _mm_cols(h2_t, du_g, "mm_dw_up_g")], axis=1)
    ffn_gain = w['ffn_norm']
    if hooks is not None:
        ffn_gain = ffn_gain + hooks.grads('ffn', grads)[0:1, 0:1]
    dx1, dx1_b, grads['ffn_norm'] = _rms_bwd(dh2, x1, r2, ffn_gain, dy, "rms2_bwd", bf16_copy=True)

    dmerged = _mm_rows([(dx1_b, w['w_out'], 0)], "mm_d_merged", nt=True)
    grads['w_out'] = _mm_rows([(merged_t, dx1_b, 0)], "mm_dw_out", tm=256)
    dpa, dpb, dpm, dgpre, grads['b_gate'] = _gate_merge_bwd(dmerged, gates, pa, pb, pm, "gate_merge_bwd")
    do_a = _mm_rows([(dpa, w['w_o_a'], 0)], "mm_d_oa", nt=True)
    do_b = _mm_rows([(dpb, w['w_o_b'], 0)], "mm_d_ob", nt=True)
    do_m = _mm_rows([(dpm, w['w_o_m'], 0)], "mm_d_om", nt=True)
    grads['w_o_a'] = _mm_tn(o_a, dpa, "mm_dw_oa")
    grads['w_o_b'] = _mm_tn(o_b, dpb, "mm_dw_ob")
    grads['w_o_m'] = _mm_tn(o_m, dpm, "mm_dw_om")
    grads['w_gate'] = _mm_cols(h_t, dgpre, "mm_dw_gate")
    dq_m, dmk, dmv, grads['m_q_norm'] = _mem_attn_bwd(proj, 6, mk, mv, w['m_q_norm'], do_m, "mem_attn_bwd")
    grads['w_mem_kv'], grads['mem_norm'], grads['m_k_norm'] = _mem_kv_bwd(
        mem, w['mem_norm'], w['w_mem_kv'], w['m_k_norm'], dmk, dmv, "mem_kv_bwd")
    a_gain = w['a_q_norm']
    if hooks is not None:
        a_gain = a_gain + hooks.grads('mix', grads)[0:1, 0:1]

    prep = _bwd_prep(do_a, o_a, lse_a, dils, None, "bwd_prep_a")
    dproj, dgq_a, dgk_a = [], [], []
    for gi, (window, d) in enumerate(A_GROUPS):
        gq, gk = _two(a_gain[gi]), _two(w['a_k_norm'][gi])
        dqkv = _band_bwd(qkv_a[gi], prep[3 * gi], prep[3 * gi + 1], prep[3 * gi + 2], 2 * d, window // d,
                         f"band_bwd_a{gi}")
        dp, dgq, dgk = _qk_prep_bwd(dqkv, proj, 6 * gi, d, False, gq, gk, tabs, f"qk_prep_bwd_a{gi}")
        dproj.append(dp)
        dgq_a.append(_fold_heads(dgq))
        dgk_a.append(_fold_heads(dgk))
    grads['a_q_norm'] = jnp.stack(dgq_a)
    grads['a_k_norm'] = jnp.stack(dgk_a)

    do_bu, lse_bu, delta_bu, dsink = _bwd_prep(do_b, o_b, lse_b, (1,), sinks, "bwd_prep_b")
    dqkv = _band_bwd(qkv_b, do_bu, lse_bu, delta_bu, 4, B_WINDOW - 1, "band_bwd_b")
    dp_b, dgq, dgk = _qk_prep_bwd(dqkv, proj, 18, 1, True, gbq, gbk, tabs, "qk_prep_bwd_b")
    dproj.append(dp_b)
    grads['b_q_norm'] = _fold_heads(dgq)
    grads['b_k_norm'] = _fold_heads(dgk)
    grads['b_sinks'] = jnp.stack([dsink[:, 0, 0], dsink[:, 0, HEAD_DIM]], axis=1).reshape(8)

    dproj.append(dq_m)

    cols = (0, 1, 2, 3, 6)
    grads['w_in'] = _mm_rows_cat(h_t, dproj, "mm_dw_in")
    attn_gain = w['attn_norm']
    if hooks is not None:
        attn_gain = attn_gain + hooks.grads('in', grads)[0:1, 0:1]
    dh = _mm_rows([(dp, w['w_in'], c) for dp, c in zip(dproj, cols)] + [(dgpre, w['w_gate'], 0)], "mm_d_h", nt=True)
    grad_x, grads['attn_norm'] = _rms_bwd(dh, x, r1, attn_gain, dx1, "rms1_bwd")
    return loss, grad_x, grads


def _coords():
    return lax.axis_index("x"), lax.axis_index("y"), lax.axis_index("c")


def _slot(p):
    return 4 * p[0] + 2 * p[1] + p[2]


ALL_PEERS = tuple(range(1, N_DEV))
CHIP_PEERS = (1, 4, 2, 6)
OTHER_CHIPS = (4, 2, 6)


def _peers(me, masks=ALL_PEERS):
    x, y, c = me
    return [(1 - x if mask & 4 else x, 1 - y if mask & 2 else y, 1 - c if mask & 1 else c) for mask in masks]


HBM_SPEC = pl.BlockSpec(memory_space=pltpu.HBM)


def _all_gather(shards, name):
    n = len(shards)

    def body(*refs):
        ins, outs = refs[:n], refs[n:2 * n]
        token, send_sems, recv_sems, local_sems = refs[2 * n:]
        token[...] = jnp.zeros_like(token)
        x, y, c = _coords()
        me, sibling = (x, y, c), (x, y, 1 - c)
        chips = [(1 - x, y), (x, 1 - y), (1 - x, 1 - y)]

        def copy(a, k, block, to, src=None):
            dst = outs[a].at[_slot(block)]
            return pltpu.make_async_remote_copy(
                src_ref=dst if src is None else src, dst_ref=dst, send_sem=send_sems.at[a, k],
                recv_sem=recv_sems.at[a, k], device_id=to, device_id_type=MESH)

        mine = [pltpu.make_async_copy(ins[a], outs[a].at[_slot(me)], local_sems.at[a]) for a in range(n)]
        for cp in mine:
            cp.start()
        first = []
        for a in range(n):
            first.append(copy(a, 0, me, sibling, src=ins[a]))
            first += [copy(a, 1 + j, me, (*chip, c), src=ins[a]) for j, chip in enumerate(chips)]
        for cp in first:
            cp.start()
        passed = []
        for a in range(n):
            for j, chip in enumerate(chips):
                copy(a, 1 + j, (*chip, c), me).wait_recv()
                fwd = copy(a, 4 + j, (*chip, c), sibling)
                fwd.start()
                passed.append(fwd)
        for a in range(n):
            copy(a, 0, sibling, me).wait_recv()
            for j, chip in enumerate(chips):
                copy(a, 4 + j, (*chip, 1 - c), me).wait_recv()
        for cp in first + passed:
            cp.wait_send()
        for cp in mine:
            cp.wait()

    return pl.pallas_call(
        body, name=name, in_specs=[HBM_SPEC] * n,
        out_specs=tuple([HBM_SPEC] * n + [pl.BlockSpec(memory_space=pltpu.VMEM)]),
        out_shape=tuple([jax.ShapeDtypeStruct((N_DEV,) + s.shape, s.dtype) for s in shards]
                        + [jax.ShapeDtypeStruct((8, LANES), F32)]),
        scratch_shapes=[pltpu.SemaphoreType.DMA((n, 7)), pltpu.SemaphoreType.DMA((n, 7)), pltpu.SemaphoreType.DMA((n,))],
    )(*shards)


SEM_SPEC = pl.BlockSpec(memory_space=pltpu.SEMAPHORE)
SIDE_EFFECT = pltpu.SideEffectType.DATAFLOW_SIDE_EFFECTING


def _exchange_start(blocks, name, gather=False, masks=ALL_PEERS):
    n = len(blocks)
    n_peers = len(masks)

    def body(*refs):
        ins, lands = refs[:n], refs[n:2 * n]
        send_sems, recv_sems = refs[2 * n], refs[2 * n + 1]
        token = refs[-1]
        me = _coords()
        peers = _peers(me, masks)
        for a in range(n):
            for k in range(n_peers):
                pltpu.make_async_remote_copy(
                    src_ref=ins[a] if gather else ins[a].at[_slot(peers[k])], dst_ref=lands[a].at[_slot(me)],
                    send_sem=send_sems.at[a * n_peers + k], recv_sem=recv_sems.at[a * n_peers + k],
                    device_id=peers[k], device_id_type=MESH).start()
        token[...] = jnp.zeros_like(token)

    land_shapes = [((N_DEV,) + b.shape) if gather else b.shape for b in blocks]
    hbm_in = [pltpu.HBM(b.shape, b.dtype) for b in blocks]
    hbm_land = [pltpu.HBM(s, b.dtype) for s, b in zip(land_shapes, blocks)]
    sems = pltpu.SemaphoreType.DMA((n * n_peers,))
    ins = [pltpu.with_memory_space_constraint(b, pltpu.HBM) for b in blocks]
    lands = [pltpu.with_memory_space_constraint(lax.empty(s, b.dtype), pltpu.HBM) for s, b in zip(land_shapes, blocks)]
    return pl.pallas_call(
        body, name=name, out_shape=(sems, sems, *hbm_in, *hbm_land, jax.ShapeDtypeStruct((8, LANES), F32)),
        in_specs=[HBM_SPEC] * (2 * n),
        out_specs=(SEM_SPEC, SEM_SPEC, *([HBM_SPEC] * (2 * n)), pl.BlockSpec(memory_space=pltpu.VMEM)),
        input_output_aliases={i: 2 + i for i in range(2 * n)},
        compiler_params=pltpu.CompilerParams(has_side_effects=SIDE_EFFECT),
    )(*ins, *lands)


def _exchange_wait(started, after, name, gather=False, masks=ALL_PEERS):
    n = (len(started) - 3) // 2
    n_peers = len(masks)
    send_sems, recv_sems = started[0], started[1]
    thru = started[2:2 + 2 * n]

    def body(*refs):
        ins, lands = refs[:n], refs[n:2 * n]
        send_ref, recv_ref = refs[2 * n], refs[2 * n + 1]
        me = _coords()
        peers = _peers(me, masks)
        for a in range(n):
            for k in range(n_peers):
                cp = pltpu.make_async_remote_copy(
                    src_ref=ins[a] if gather else ins[a].at[_slot(peers[k])], dst_ref=lands[a].at[_slot(peers[k])],
                    send_sem=send_ref.at[a * n_peers + k], recv_sem=recv_ref.at[a * n_peers + k],
                    device_id=peers[k], device_id_type=MESH)
                cp.wait_send()
                cp.wait_recv()

    hbm = [pltpu.HBM(t.shape, t.dtype) for t in thru]
    res = pl.pallas_call(
        body, name=name, out_shape=tuple(hbm),
        in_specs=[HBM_SPEC] * (2 * n) + [SEM_SPEC, SEM_SPEC, pl.BlockSpec(memory_space=pl.ANY)],
        out_specs=tuple([HBM_SPEC] * (2 * n)), input_output_aliases={i: i for i in range(2 * n)},
        compiler_params=pltpu.CompilerParams(has_side_effects=SIDE_EFFECT),
    )(*thru, send_sems, recv_sems, after)
    return res[n:]


def _sibling_forward(arrays, name):
    n = len(arrays)
    n_fwd = len(OTHER_CHIPS)

    def body(*refs):
        bufs = refs[n:2 * n]
        token, send_sems, recv_sems = refs[2 * n:]
        token[...] = jnp.zeros_like(token)
        x, y, c = _coords()
        sibling = (x, y, 1 - c)
        mine = _peers((x, y, c), OTHER_CHIPS)
        theirs = _peers(sibling, OTHER_CHIPS)

        def copy(a, k, block):
            rows = bufs[a].at[_slot(block)]
            return pltpu.make_async_remote_copy(
                src_ref=rows, dst_ref=rows, send_sem=send_sems.at[a * n_fwd + k], recv_sem=recv_sems.at[a * n_fwd + k],
                device_id=sibling, device_id_type=MESH)

        sends = [copy(a, k, mine[k]) for a in range(n) for k in range(n_fwd)]
        for cp in sends:
            cp.start()
        for a in range(n):
            for k in range(n_fwd):
                copy(a, k, theirs[k]).wait_recv()
        for cp in sends:
            cp.wait_send()

    res = pl.pallas_call(
        body, name=name, in_specs=[HBM_SPEC] * n,
        out_specs=tuple([HBM_SPEC] * n + [pl.BlockSpec(memory_space=pltpu.VMEM)]),
        out_shape=tuple([jax.ShapeDtypeStruct(a.shape, a.dtype) for a in arrays] + [jax.ShapeDtypeStruct((8, LANES), F32)]),
        input_output_aliases={i: i for i in range(n)},
        scratch_shapes=[pltpu.SemaphoreType.DMA((n * n_fwd,)), pltpu.SemaphoreType.DMA((n * n_fwd,))],
    )(*arrays)
    return res[:n], res[n]


def _all_sum(p, name):
    def body(p_ref, o_ref, recv, send_sems, recv_sems):
        me = _coords()
        peers = _peers(me)
        recv[_slot(me)] = p_ref[...]

        def copy(k, landing):
            return pltpu.make_async_remote_copy(
                src_ref=p_ref, dst_ref=recv.at[_slot(landing)], send_sem=send_sems.at[k], recv_sem=recv_sems.at[k],
                device_id=peers[k], device_id_type=MESH)

        sends = [copy(k, me) for k in range(N_DEV - 1)]
        for cp in sends:
            cp.start()
        for k in range(N_DEV - 1):
            copy(k, peers[k]).wait_recv()
        for cp in sends:
            cp.wait_send()
        acc = recv[0]
        for s in range(1, N_DEV):
            acc = acc + recv[s]
        o_ref[...] = acc

    vmem = pl.BlockSpec(memory_space=pltpu.VMEM)
    return pl.pallas_call(
        body, name=name, in_specs=[vmem], out_specs=vmem, out_shape=jax.ShapeDtypeStruct(p.shape, F32),
        scratch_shapes=[pltpu.VMEM((N_DEV,) + p.shape, F32), pltpu.SemaphoreType.DMA((N_DEV - 1,)),
                        pltpu.SemaphoreType.DMA((N_DEV - 1,))],
    )(p)


def _adam(w, g, m, v):
    m2 = ADAM_B1 * m + (1.0 - ADAM_B1) * g
    v2 = ADAM_B2 * v + (1.0 - ADAM_B2) * (g * g)
    m_hat = m2 / (1.0 - ADAM_B1 ** ADAM_STEP)
    v_hat = v2 / (1.0 - ADAM_B2 ** ADAM_STEP)
    delta = -ADAM_LR * (m_hat / (jnp.sqrt(v_hat) + ADAM_EPS) + ADAM_WD * w)
    return delta, m2, v2


def _row_tile(rows, cols):
    best = rows
    for t in range(16, rows, 16):
        if rows % t == 0 and t * cols * 4 <= (1 << 20):
            best = t
    return best


def _adam_reduce(parts, w, m, v, name):
    rows, cols = w.shape
    tr = _row_tile(rows, cols)

    def body(p_ref, w_ref, m_ref, v_ref, g_ref, d_ref, m2_ref, v2_ref):
        g = p_ref[0].astype(F32)
        for s in range(1, N_DEV):
            g = g + p_ref[s].astype(F32)
        g_ref[...] = g
        d_ref[...], m2_ref[...], v2_ref[...] = _adam(w_ref[...], g, m_ref[...], v_ref[...])

    blk = pl.BlockSpec((tr, cols), lambda i: (i, 0))
    shp = jax.ShapeDtypeStruct((rows, cols), F32)
    return pl.pallas_call(
        body, name=name, grid=(rows // tr,),
        in_specs=[pl.BlockSpec((N_DEV, tr, cols), lambda i: (0, i, 0)), blk, blk, blk],
        out_specs=(blk,) * 4, out_shape=(shp,) * 4, compiler_params=_cparams(1),
    )(parts, w, m, v)


PACK_COLS = 1024
PACK = {'attn_norm': (0, 1, 1024), 'mem_norm': (1, 1, 1024), 'ffn_norm': (2, 1, 1024), 'b_gate': (3, 3, 1024),
        'conv_b': (6, 6, 1024), 'a_q_norm': (12, 3, 64), 'a_k_norm': (15, 3, 64), 'b_q_norm': (18, 1, 64),
        'b_k_norm': (19, 1, 64), 'm_q_norm': (20, 1, 128), 'm_k_norm': (21, 1, 128), 'b_sinks': (22, 1, 8)}
PACK_LOSS_ROW = 23
PACK_ROWS = 24


def _pack_pieces(name, width):
    r0, nr, lanes = PACK[name]
    out = []
    for j in range(nr):
        if lanes == PACK_COLS:
            w = min(PACK_COLS, width - j * PACK_COLS)
            out.append((r0 + j, slice(0, 1), slice(j * PACK_COLS, j * PACK_COLS + w), w))
        else:
            out.append((r0 + j, slice(j, j + 1), slice(0, lanes), lanes))
    return out


def _pack_small(grads, loss_tile, name):
    names = list(PACK)

    def body(*refs):
        o_ref = refs[-1]
        o_ref[...] = jnp.zeros_like(o_ref)
        for k, nm in enumerate(names):
            for row, rs, ls, w in _pack_pieces(nm, refs[k].shape[1]):
                o_ref[row:row + 1, 0:w] = refs[k][rs, ls]
        o_ref[PACK_LOSS_ROW:PACK_LOSS_ROW + 1, 0:1] = refs[len(names)][0:1, 0:1]

    vmem = pl.BlockSpec(memory_space=pltpu.VMEM)
    args = [grads[nm] for nm in names] + [loss_tile]
    return pl.pallas_call(body, name=name, in_specs=[vmem] * len(args), out_specs=vmem,
                          out_shape=jax.ShapeDtypeStruct((PACK_ROWS, PACK_COLS), F32))(*args)


def _adam_small(gsum, ws, ms, vs, name):
    names = list(PACK)
    n = len(names)

    def body(*refs):
        g_ref = refs[0]
        w_refs, m_refs, v_refs = refs[1:1 + n], refs[1 + n:1 + 2 * n], refs[1 + 2 * n:1 + 3 * n]
        outs = refs[1 + 3 * n:]
        outs[0][...] = g_ref[PACK_LOSS_ROW:PACK_LOSS_ROW + 1, 0:1]
        for k, nm in enumerate(names):
            o_g, o_d, o_m, o_v = outs[1 + 4 * k:5 + 4 * k]
            for row, rs, ls, width in _pack_pieces(nm, w_refs[k].shape[1]):
                src = (rs, ls)
                g = g_ref[row:row + 1, 0:width]
                d, m2, v2 = _adam(w_refs[k][src], g, m_refs[k][src], v_refs[k][src])
                o_g[src] = g
                o_d[src] = d
                o_m[src] = m2
                o_v[src] = v2

    vmem = pl.BlockSpec(memory_space=pltpu.VMEM)
    shapes = [jax.ShapeDtypeStruct((1, 1), F32)]
    for nm in names:
        shapes += [jax.ShapeDtypeStruct(ws[nm].shape, F32)] * 4
    args = [gsum] + [ws[nm] for nm in names] + [ms[nm] for nm in names] + [vs[nm] for nm in names]
    return pl.pallas_call(
        body, name=name, in_specs=[vmem] * len(args), out_specs=tuple([vmem] * len(shapes)), out_shape=tuple(shapes),
    )(*args)


def _as2d(name, a):
    return a.reshape(a.shape[-2], a.shape[-1]) if a.ndim == 3 else a


def kernel(x, mem, positions, attn_norm, w_in, a_q_norm, a_k_norm, b_q_norm, b_k_norm, b_sinks, mem_norm, w_mem_kv, m_q_norm, m_k_norm, w_o_a, w_o_b, w_o_m, w_gate, b_gate, w_out, ffn_norm, w_up, conv_w, conv_b, w_down, loss_target, m_attn_norm, m_w_in, m_a_q_norm, m_a_k_norm, m_b_q_norm, m_b_k_norm, m_b_sinks, m_mem_norm, m_w_mem_kv, m_m_q_norm, m_m_k_norm, m_w_o_a, m_w_o_b, m_w_o_m, m_w_gate, m_b_gate, m_w_out, m_ffn_norm, m_w_up, m_conv_w, m_conv_b, m_w_down, v_attn_norm, v_w_in, v_a_q_norm, v_a_k_norm, v_b_q_norm, v_b_k_norm, v_b_sinks, v_mem_norm, v_w_mem_kv, v_m_q_norm, v_m_k_norm, v_w_o_a, v_w_o_b, v_w_o_m, v_w_gate, v_b_gate, v_w_out, v_ffn_norm, v_w_up, v_conv_w, v_conv_b, v_w_down):
    given = dict(attn_norm=attn_norm, w_in=w_in, a_q_norm=a_q_norm, a_k_norm=a_k_norm, b_q_norm=b_q_norm, b_k_norm=b_k_norm, b_sinks=b_sinks, mem_norm=mem_norm, w_mem_kv=w_mem_kv, m_q_norm=m_q_norm, m_k_norm=m_k_norm, w_o_a=w_o_a, w_o_b=w_o_b, w_o_m=w_o_m, w_gate=w_gate, b_gate=b_gate, w_out=w_out, ffn_norm=ffn_norm, w_up=w_up, conv_w=conv_w, conv_b=conv_b, w_down=w_down)
    mom1 = dict(attn_norm=m_attn_norm, w_in=m_w_in, a_q_norm=m_a_q_norm, a_k_norm=m_a_k_norm, b_q_norm=m_b_q_norm, b_k_norm=m_b_k_norm, b_sinks=m_b_sinks, mem_norm=m_mem_norm, w_mem_kv=m_w_mem_kv, m_q_norm=m_m_q_norm, m_k_norm=m_m_k_norm, w_o_a=m_w_o_a, w_o_b=m_w_o_b, w_o_m=m_w_o_m, w_gate=m_w_gate, b_gate=m_b_gate, w_out=m_w_out, ffn_norm=m_ffn_norm, w_up=m_w_up, conv_w=m_conv_w, conv_b=m_conv_b, w_down=m_w_down)
    mom2 = dict(attn_norm=v_attn_norm, w_in=v_w_in, a_q_norm=v_a_q_norm, a_k_norm=v_a_k_norm, b_q_norm=v_b_q_norm, b_k_norm=v_b_k_norm, b_sinks=v_b_sinks, mem_norm=v_mem_norm, w_mem_kv=v_w_mem_kv, m_q_norm=v_m_q_norm, m_k_norm=v_m_k_norm, w_o_a=v_w_o_a, w_o_b=v_w_o_b, w_o_m=v_w_o_m, w_gate=v_w_gate, b_gate=v_b_gate, w_out=v_w_out, ffn_norm=v_ffn_norm, w_up=v_w_up, conv_w=v_conv_w, conv_b=v_conv_b, w_down=v_w_down)

    big = list(BIG)
    stages = {'mix': list(MIX_WEIGHTS), 'ffn': list(FFN_WEIGHTS), 'in': ['w_in']}
    my_slot = _slot(_coords())

    def shard(n):
        return given[n][0] if n == 'conv_w' else given[n][0].astype(BF16)

    def whole(n, g):
        _, r, c = g.shape
        return g.reshape(N_DEV * r, c) if BIG[n] == 0 else g.transpose(1, 0, 2).reshape(r, N_DEV * c)

    def to_blocks(n, g):
        r, c = given[n].shape[1:]
        g = g.reshape(N_DEV, r, c) if BIG[n] == 0 else g.reshape(r, N_DEV, c).transpose(1, 0, 2)
        return g if n == 'conv_w' else g.astype(BF16)

    class Hooks:
        def __init__(self, token):
            self.coming, self.sent = {}, {}
            self.start_gather('mix', token)

        def start_gather(self, stage, token):
            src = [shard(n) if n == 'conv_w' else (given[n][0] + token).astype(BF16) for n in stages[stage]]
            self.coming[stage] = _exchange_start(src, f"gather_{stage}_start", gather=True, masks=CHIP_PEERS)

        def weights(self, stage, after):
            names = stages[stage]
            landed = _exchange_wait(self.coming[stage], after, f"gather_{stage}_wait", gather=True, masks=CHIP_PEERS)
            landed, token = _sibling_forward(landed, f"gather_{stage}_forward")
            if stage == 'mix':
                self.start_gather('ffn', token[0, 0])
            return {n: whole(n, lax.dynamic_update_slice_in_dim(land, shard(n)[None], my_slot, axis=0))
                    for n, land in zip(names, landed)}

        def grads(self, stage, g):
            blocks = [to_blocks(n, g[n]) for n in stages[stage]]
            own = [lax.dynamic_slice_in_dim(b, my_slot, 1, axis=0) for b in blocks]
            self.sent[stage] = (_exchange_start(blocks, f"exchange_{stage}_start"), own)
            return self.sent[stage][0][-1]

        def parts(self, stage, after):
            started, own = self.sent[stage]
            landed = _exchange_wait(started, after, f"exchange_{stage}_wait")
            return {n: lax.dynamic_update_slice_in_dim(land, o, my_slot, axis=0)
                    for n, land, o in zip(stages[stage], landed, own)}

    w_in_all, token = _all_gather([shard('w_in')], "gather_w_in")
    hooks = Hooks(token[0, 0])
    w = {'w_in': whole('w_in', w_in_all)}
    for n in SMALL:
        w[n] = given[n]
    w['a_q_norm'], w['a_k_norm'] = given['a_q_norm'][0], given['a_k_norm'][0]
    w['b_q_norm'], w['b_k_norm'], w['b_sinks'] = given['b_q_norm'][0], given['b_k_norm'][0], given['b_sinks'][0]

    loss_tile, grad_x, grads = _device_step(x[0], mem[0], positions[0], loss_target[0], w, hooks)
    out = {}
    after = grad_x
    for stage in ('ffn', 'mix', 'in'):
        for n, p in hooks.parts(stage, after).items():
            res = _adam_reduce(p, given[n][0], mom1[n][0], mom2[n][0], f"adam_{n}")
            out[n] = tuple(t[None] for t in res)
            after = res[0]

    small = {n: grads[n] for n in PACK}
    small['b_q_norm'], small['b_k_norm'] = grads['b_q_norm'].reshape(1, -1), grads['b_k_norm'].reshape(1, -1)
    small['b_sinks'] = grads['b_sinks'].reshape(1, -1)
    gsum = _all_sum(_pack_small(small, loss_tile, "pack_small"), "sum_small")
    ws = {n: _as2d(n, given[n]) for n in PACK}
    ms = {n: _as2d(n, mom1[n]) for n in PACK}
    vs = {n: _as2d(n, mom2[n]) for n in PACK}
    res = _adam_small(gsum, ws, ms, vs, "adam_small")
    loss = res[0].reshape(())
    for k, n in enumerate(PACK):
        out[n] = tuple(t.reshape(given[n].shape) for t in res[1 + 4 * k:5 + 4 * k])

    outs = [loss, grad_x[None]]
    for field in range(4):
        outs += [out[n][field] for n in WEIGHTS]
    return tuple(outs)
```

```python
import functools
import math

import jax
import jax.numpy as jnp
from jax import lax
from jax.experimental import pallas as pl
from jax.experimental.pallas import tpu as pltpu

F32 = jnp.float32
BF16 = jnp.bfloat16

N_DEV = 8
D_MODEL = 1024
HEAD_DIM = 64
A_GROUPS = ((128, 1), (512, 4), (2048, 16))
B_WINDOW = 128
M_HEADS = 4
M_HEAD_DIM = 128
MEM_LEN = 256
D_FF = 2816
ROPE_THETA = 500000.0
ROPE_DIMS = 16
BLOCK = 128
EPS = 1e-6
LANES = 128
BAND_Q_BLOCKS = 4
BAND_UNITS = 2

ADAM_LR = 0.001
ADAM_B1 = 0.9
ADAM_B2 = 0.999
ADAM_EPS = 1e-08
ADAM_WD = 0.01
ADAM_STEP = 10

VMEM_LIMIT_BYTES = 56 * 1024 * 1024
MESH = pl.DeviceIdType.MESH

WEIGHTS = ['attn_norm', 'w_in', 'a_q_norm', 'a_k_norm', 'b_q_norm', 'b_k_norm', 'b_sinks', 'mem_norm',
           'w_mem_kv', 'm_q_norm', 'm_k_norm', 'w_o_a', 'w_o_b', 'w_o_m', 'w_gate', 'b_gate', 'w_out',
           'ffn_norm', 'w_up', 'conv_w', 'conv_b', 'w_down']
BIG = {'w_in': 1, 'w_mem_kv': 0, 'w_o_a': 1, 'w_o_b': 1, 'w_o_m': 1, 'w_gate': 1, 'w_out': 0, 'w_up': 1,
       'conv_w': 1, 'w_down': 0}
SMALL = [n for n in WEIGHTS if n not in BIG]


def _cparams(n_grid):
    return pltpu.CompilerParams(dimension_semantics=("arbitrary",) * n_grid, vmem_limit_bytes=VMEM_LIMIT_BYTES)


def _seg_matrix(width):
    shift = width.bit_length() - 1
    r = lax.shift_right_logical(lax.broadcasted_iota(jnp.int32, (LANES, LANES), 0), shift)
    c = lax.shift_right_logical(lax.broadcasted_iota(jnp.int32, (LANES, LANES), 1), shift)
    return jnp.where(r == c, 1.0, 0.0).astype(BF16)


def _seg_sum(x, seg):
    hi = x.astype(BF16)
    r1 = x - hi.astype(F32)
    mid = r1.astype(BF16)
    lo = (r1 - mid.astype(F32)).astype(BF16)
    dot = functools.partial(jnp.dot, preferred_element_type=F32)
    return dot(hi, seg) + dot(mid, seg) + dot(lo, seg)


def _rope(y, c, s1, s2):
    return y * c + pltpu.roll(y, LANES - ROPE_DIMS // 2, 1) * s1 + pltpu.roll(y, ROPE_DIMS // 2, 1) * s2


def _unrope(dy, c, s1, s2):
    return dy * c + pltpu.roll(dy * s1, ROPE_DIMS // 2, 1) + pltpu.roll(dy * s2, LANES - ROPE_DIMS // 2, 1)


def _sigmoid(x):
    return 1.0 / (1.0 + jnp.exp(-x))


def _rms_fwd(x, gain, name):
    s_len, d = x.shape
    tm = 512

    def body(x_ref, g_ref, h_ref, ht_ref, r_ref):
        xv = x_ref[...]
        r = lax.rsqrt(jnp.mean(xv * xv, axis=-1, keepdims=True) + EPS)
        h = ((xv * r) * g_ref[...]).astype(BF16)
        h_ref[...] = h
        ht_ref[...] = h.T
        r_ref[...] = r

    return pl.pallas_call(
        body, name=name, grid=(s_len // tm,),
        in_specs=[pl.BlockSpec((tm, d), lambda i: (i, 0)), pl.BlockSpec((1, d), lambda i: (0, 0))],
        out_specs=(pl.BlockSpec((tm, d), lambda i: (i, 0)), pl.BlockSpec((d, tm), lambda i: (0, i)),
                   pl.BlockSpec((tm, 1), lambda i: (i, 0))),
        out_shape=(jax.ShapeDtypeStruct((s_len, d), BF16), jax.ShapeDtypeStruct((d, s_len), BF16),
                   jax.ShapeDtypeStruct((s_len, 1), F32)),
        compiler_params=_cparams(1),
    )(x, gain)


def _rms_bwd(dh, x, r, gain, add, name, bf16_copy=False):
    s_len, d = x.shape
    tm = 512

    def body(dh_ref, x_ref, r_ref, g_ref, add_ref, dx_ref, *rest):
        dg_ref = rest[-1]

        @pl.when(pl.program_id(0) == 0)
        def _():
            dg_ref[...] = jnp.zeros_like(dg_ref)
        rv = r_ref[...]
        xhat = x_ref[...] * rv
        dhv = dh_ref[...]
        dg_ref[...] += jnp.sum(dhv * xhat, axis=0, keepdims=True)
        dxhat = dhv * g_ref[...]
        dx = add_ref[...] + rv * (dxhat - xhat * jnp.mean(dxhat * xhat, axis=-1, keepdims=True))
        dx_ref[...] = dx
        if bf16_copy:
            rest[0][...] = dx.astype(BF16)

    row = pl.BlockSpec((tm, d), lambda i: (i, 0))
    vec = pl.BlockSpec((1, d), lambda i: (0, 0))
    out_specs = [row] + ([row] if bf16_copy else []) + [vec]
    out_shape = [jax.ShapeDtypeStruct((s_len, d), F32)] + ([jax.ShapeDtypeStruct((s_len, d), BF16)] if bf16_copy else [])
    out_shape.append(jax.ShapeDtypeStruct((1, d), F32))
    return pl.pallas_call(
        body, name=name, grid=(s_len // tm,),
        in_specs=[row, row, pl.BlockSpec((tm, 1), lambda i: (i, 0)), vec, row],
        out_specs=tuple(out_specs), out_shape=tuple(out_shape), compiler_params=_cparams(1),
    )(dh, x, r, gain, add)


def _resident(shape, index_map):
    return pl.BlockSpec(shape, index_map, pipeline_mode=pl.Buffered(1))


def _mm_rows(pairs, name, nt=False, tm=512, bias=None, sigmoid=False, res=None, out_dtypes=(F32,), loss_target=None):
    m = pairs[0][0].shape[0]
    n = pairs[0][1].shape[0] if nt else pairs[0][1].shape[1]
    n_pairs = len(pairs)
    has_bias, has_res, has_loss = bias is not None, res is not None, loss_target is not None
    dims = (((1,), (1,)), ((), ())) if nt else (((1,), (0,)), ((), ()))

    def body(*refs):
        acc = None
        for p in range(n_pairs):
            t = lax.dot_general(refs[2 * p][...].astype(BF16), refs[2 * p + 1][...], dims, preferred_element_type=F32)
            acc = t if acc is None else acc + t
        pos = 2 * n_pairs
        if has_bias:
            acc = acc + refs[pos][...]
            pos += 1
        if sigmoid:
            acc = _sigmoid(acc)
        if has_res:
            acc = refs[pos][...] + acc
            pos += 1
        if has_loss:
            dy_ref, dyb_ref, l_ref = refs[pos + 1:]

            @pl.when(pl.program_id(0) == 0)
            def _():
                l_ref[...] = jnp.zeros_like(l_ref)
            err = acc - refs[pos][...]
            dy = err * (1.0 / n)
            dy_ref[...] = dy
            dyb_ref[...] = dy.astype(BF16)
            part = 0.5 * jnp.sum(jnp.mean(err * err, axis=-1, keepdims=True), axis=0, keepdims=True)
            l_ref[...] += jnp.broadcast_to(part, l_ref.shape)
            return
        for o_ref in refs[pos:]:
            o_ref[...] = acc.astype(o_ref.dtype)

    in_specs, args = [], []
    for a, w, blk in pairs:
        k = a.shape[1]
        in_specs.append(pl.BlockSpec((tm, k), lambda i: (i, 0)))
        if nt:
            in_specs.append(_resident((n, k), lambda i, blk=blk: (0, blk)))
        else:
            in_specs.append(_resident((k, n), lambda i, blk=blk: (blk, 0)))
        args += [a, w]
    if has_bias:
        in_specs.append(_resident((1, n), lambda i: (0, 0)))
        args.append(bias)
    if has_res:
        in_specs.append(pl.BlockSpec((tm, n), lambda i: (i, 0)))
        args.append(res)
    out = pl.BlockSpec((tm, n), lambda i: (i, 0))
    if has_loss:
        return pl.pallas_call(
            body, name=name, grid=(m // tm,), in_specs=in_specs + [out],
            out_specs=(out, out, pl.BlockSpec((8, LANES), lambda i: (0, 0))),
            out_shape=(jax.ShapeDtypeStruct((m, n), F32), jax.ShapeDtypeStruct((m, n), BF16),
                       jax.ShapeDtypeStruct((8, LANES), F32)),
            compiler_params=_cparams(1),
        )(*args, loss_target)
    outs = pl.pallas_call(
        body, name=name, grid=(m // tm,), in_specs=in_specs, out_specs=tuple([out] * len(out_dtypes)),
        out_shape=tuple(jax.ShapeDtypeStruct((m, n), dt) for dt in out_dtypes), compiler_params=_cparams(1),
    )(*args)
    return outs[0] if len(out_dtypes) == 1 else outs


def _mm_rows_cat(a, ws, name, tm=256):
    m, k = a.shape
    widths = [w.shape[1] for w in ws]
    n = sum(widths)

    def body(*refs):
        a_ref, o_ref = refs[0], refs[-1]
        av = a_ref[...]
        off = 0
        for p, width in enumerate(widths):
            o_ref[:, off:off + width] = jnp.dot(av, refs[1 + p][...], preferred_element_type=F32)
            off += width

    return pl.pallas_call(
        body, name=name, grid=(m // tm,),
        in_specs=[pl.BlockSpec((tm, k), lambda i: (i, 0))] + [_resident((k, wd), lambda i: (0, 0)) for wd in widths],
        out_specs=pl.BlockSpec((tm, n), lambda i: (i, 0)),
        out_shape=jax.ShapeDtypeStruct((m, n), F32), compiler_params=_cparams(1),
    )(a, *ws)


def _mm_cols(a, b, name, tn=256):
    m, k = a.shape
    n = b.shape[1]

    def body(a_ref, b_ref, o_ref):
        o_ref[...] = jnp.dot(a_ref[...], b_ref[...].astype(BF16), preferred_element_type=F32)

    return pl.pallas_call(
        body, name=name, grid=(n // tn,),
        in_specs=[_resident((m, k), lambda j: (0, 0)), pl.BlockSpec((k, tn), lambda j: (0, j))],
        out_specs=pl.BlockSpec((m, tn), lambda j: (0, j)),
        out_shape=jax.ShapeDtypeStruct((m, n), F32), compiler_params=_cparams(1),
    )(a, b)


def _mm_tn(a, b, name, tile=256):
    k, m = a.shape
    n = b.shape[1]
    dims = (((0,), (0,)), ((), ()))

    def body(a_ref, b_ref, o_ref):
        o_ref[...] = lax.dot_general(a_ref[...].astype(BF16), b_ref[...].astype(BF16), dims, preferred_element_type=F32)

    if n <= m:
        t = min(tile, m)
        grid, a_spec, b_spec = (m // t,), pl.BlockSpec((k, t), lambda i: (0, i)), _resident((k, n), lambda i: (0, 0))
        o_spec = pl.BlockSpec((t, n), lambda i: (i, 0))
    else:
        t = min(tile, n)
        grid, a_spec, b_spec = (n // t,), _resident((k, m), lambda i: (0, 0)), pl.BlockSpec((k, t), lambda i: (0, i))
        o_spec = pl.BlockSpec((m, t), lambda i: (0, i))
    return pl.pallas_call(
        body, name=name, grid=grid, in_specs=[a_spec, b_spec], out_specs=o_spec,
        out_shape=jax.ShapeDtypeStruct((m, n), F32), compiler_params=_cparams(1),
    )(a, b)


def _norm_rope(t, gain, c, s1, s2, seg):
    rs = lax.rsqrt(_seg_sum(t * t, seg) * (1.0 / HEAD_DIM) + EPS)
    return _rope((t * rs) * gain, c, s1, s2)


def _dup_half(y, half):
    lane = lax.broadcasted_iota(jnp.int32, y.shape, 1)
    rolled = pltpu.roll(y, HEAD_DIM, 1)
    keep = (lane < HEAD_DIM) if half == 0 else (lane >= HEAD_DIM)
    return jnp.where(keep, y, rolled)


def _qk_prep(proj, cb0, d, gqa, gq, gk, tabs, name):
    s_len = proj.shape[0]
    tm = 512
    rows = tm // d
    n_units = 4 if gqa else 2 * d
    n_q = 4 if gqa else 2
    n_in = 6

    def body(*refs):
        in_refs = refs[:n_in]
        gq_ref, gk_ref, c_ref, s1_ref, s2_ref, o_ref = refs[n_in:]
        seg = _seg_matrix(HEAD_DIM)

        def rows_of(ref, r):
            return ref[...] if d == 1 else ref[pl.ds(r, rows, stride=d), :]

        def put(unit_col, y):
            o_ref[:, unit_col * LANES:(unit_col + 1) * LANES] = y.astype(BF16)

        for r in range(d):
            c, s1, s2 = rows_of(c_ref, r), rows_of(s1_ref, r), rows_of(s2_ref, r)
            for b in range(n_in):
                t = rows_of(in_refs[b], r)
                if b < n_q:
                    put((b * d + r) if not gqa else b, _norm_rope(t, gq_ref[...], c, s1, s2, seg))
                elif not gqa:
                    sec, pair = (1, b - 2) if b < 4 else (2, b - 4)
                    y = _norm_rope(t, gk_ref[...], c, s1, s2, seg) if sec == 1 else t
                    put(sec * n_units + pair * d + r, y)
                else:
                    sec = 1 if b == 4 else 2
                    y = _norm_rope(t, gk_ref[...], c, s1, s2, seg) if sec == 1 else t
                    for u in range(n_units):
                        put(sec * n_units + u, _dup_half(y, u // 2))

    in_specs = [pl.BlockSpec((tm, LANES), lambda i, b=b: (i, cb0 + b)) for b in range(n_in)]
    vec = pl.BlockSpec((1, LANES), lambda i: (0, 0))
    tab = pl.BlockSpec((tm, LANES), lambda i: (i, 0))
    width = 3 * n_units * LANES
    return pl.pallas_call(
        body, name=name, grid=(s_len // tm,), in_specs=in_specs + [vec, vec, tab, tab, tab],
        out_specs=pl.BlockSpec((rows, width), lambda i: (i, 0)),
        out_shape=jax.ShapeDtypeStruct((s_len // d, width), BF16), compiler_params=_cparams(1),
    )(*([proj] * n_in), gq, gk, *tabs)


def _qk_prep_bwd(dqkv, proj, cb0, d, gqa, gq, gk, tabs, name):
    s_len = proj.shape[0]
    tm = 512
    rows = tm // d
    n_units = 4 if gqa else 2 * d
    n_q = 4 if gqa else 2
    n_in = 6

    def body(*refs):
        d_refs = refs[0:3]
        in_refs = refs[3:3 + n_in]
        gq_ref, gk_ref, c_ref, s1_ref, s2_ref, o_ref, dgq_ref, dgk_ref, stage = refs[3 + n_in:]
        seg = _seg_matrix(HEAD_DIM)

        @pl.when(pl.program_id(0) == 0)
        def _():
            dgq_ref[...] = jnp.zeros_like(dgq_ref)
            dgk_ref[...] = jnp.zeros_like(dgk_ref)

        def rows_of(ref, r):
            return ref[...] if d == 1 else ref[pl.ds(r, rows, stride=d), :]

        def unit(col):
            sec, u = divmod(col, n_units)
            return d_refs[sec][:, u * LANES:(u + 1) * LANES]

        def norm_bwd(dyr, t, gain, c, s1, s2, dg_ref):
            rs = lax.rsqrt(_seg_sum(t * t, seg) * (1.0 / HEAD_DIM) + EPS)
            that = t * rs
            dy = _unrope(dyr, c, s1, s2)
            dg_ref[...] += jnp.sum(dy * that, axis=0, keepdims=True)
            dthat = dy * gain
            return rs * (dthat - that * (_seg_sum(dthat * that, seg) * (1.0 / HEAD_DIM)))

        def fold(sec):
            tot = []
            for u in range(n_units):
                v = unit(sec * n_units + u)
                tot.append(v + pltpu.roll(v, HEAD_DIM, 1))
            lane = lax.broadcasted_iota(jnp.int32, tot[0].shape, 1)
            return jnp.where(lane < HEAD_DIM, tot[0] + tot[1], tot[2] + tot[3])

        for b in range(n_in):
            for r in range(d):
                c, s1, s2 = rows_of(c_ref, r), rows_of(s1_ref, r), rows_of(s2_ref, r)
                t = rows_of(in_refs[b], r)
                if b < n_q:
                    g = unit((b * d + r) if not gqa else b)
                    out = norm_bwd(g, t, gq_ref[...], c, s1, s2, dgq_ref)
                elif not gqa:
                    sec, pair = (1, b - 2) if b < 4 else (2, b - 4)
                    g = unit(sec * n_units + pair * d + r)
                    out = norm_bwd(g, t, gk_ref[...], c, s1, s2, dgk_ref) if sec == 1 else g
                else:
                    sec = 1 if b == 4 else 2
                    g = fold(sec)
                    out = norm_bwd(g, t, gk_ref[...], c, s1, s2, dgk_ref) if sec == 1 else g
                if d == 1:
                    o_ref[:, b * LANES:(b + 1) * LANES] = out.astype(BF16)
                else:
                    stage[pl.ds(r, rows, stride=d), :] = out
            if d != 1:
                o_ref[:, b * LANES:(b + 1) * LANES] = stage[...].astype(BF16)

    in_specs = [pl.BlockSpec((rows, n_units * LANES), lambda i: (i, 0))] * 3
    in_specs += [pl.BlockSpec((tm, LANES), lambda i, b=b: (i, cb0 + b)) for b in range(n_in)]
    vec = pl.BlockSpec((1, LANES), lambda i: (0, 0))
    tab = pl.BlockSpec((tm, LANES), lambda i: (i, 0))
    return pl.pallas_call(
        body, name=name, grid=(s_len // tm,), in_specs=in_specs + [vec, vec, tab, tab, tab],
        out_specs=(pl.BlockSpec((tm, n_in * LANES), lambda i: (i, 0)), vec, vec),
        out_shape=(jax.ShapeDtypeStruct((s_len, n_in * LANES), BF16), jax.ShapeDtypeStruct((1, LANES), F32),
                   jax.ShapeDtypeStruct((1, LANES), F32)),
        scratch_shapes=[pltpu.VMEM((tm, LANES), F32)], compiler_params=_cparams(1),
    )(*dqkv, *([proj] * n_in), gq, gk, *tabs)


def _head_masks(shape):
    lane = lax.broadcasted_iota(jnp.int32, shape, 1)
    return lane < HEAD_DIM, lane >= HEAD_DIM


def _band_fwd(qkv, n_units, max_dist, sinks, name):
    n_rows = qkv.shape[0]
    nb = n_rows // BLOCK
    scale = HEAD_DIM ** -0.5
    has_sink = sinks is not None
    assert not has_sink or max_dist < BLOCK

    qn, un = min(nb, BAND_Q_BLOCKS), BAND_UNITS
    ug = n_units // un

    def body(*refs):
        q_ref, kp_ref, km_ref, vp_ref, vm_ref = refs[:5]
        o_ref, lse_ref = refs[-2:]
        i = pl.program_id(1)
        qi = lax.broadcasted_iota(jnp.int32, (BLOCK, 2 * BLOCK), 0)
        kj = lax.broadcasted_iota(jnp.int32, (BLOCK, 2 * BLOCK), 1)
        dist = qi + BLOCK - kj
        band = (dist >= 0) & (dist <= max_dist)
        band_first = band & ((i > 0) | (kj >= BLOCK))
        m0, m1 = _head_masks((BLOCK, LANES))
        zero = jnp.zeros((BLOCK, LANES), BF16)
        for ub in range(un):
            cs = slice(ub * LANES, (ub + 1) * LANES)
            for qb in range(qn):
                rs = slice(qb * BLOCK, (qb + 1) * BLOCK)
                q = q_ref[rs, cs]
                if qb == 0:
                    kk = jnp.concatenate([kp_ref[:, cs], km_ref[0:BLOCK, cs]], axis=0)
                    vv = jnp.concatenate([vp_ref[:, cs], vm_ref[0:BLOCK, cs]], axis=0)
                    valid = band_first
                else:
                    kk = km_ref[(qb - 1) * BLOCK:(qb + 1) * BLOCK, cs]
                    vv = vm_ref[(qb - 1) * BLOCK:(qb + 1) * BLOCK, cs]
                    valid = band
                outs, lses = [], []
                for e, hm in enumerate((m0, m1)):
                    qe = jnp.where(hm, q, zero)
                    s = lax.dot_general(qe, kk, (((1,), (1,)), ((), ())), preferred_element_type=F32) * scale
                    s = jnp.where(valid, s, -jnp.inf)
                    if has_sink:
                        s = jnp.where(kj == 0, refs[5][ub][:, e * HEAD_DIM:e * HEAD_DIM + 1], s)
                    mx = jnp.max(s, axis=-1, keepdims=True)
                    p = jnp.exp(s - mx)
                    den = jnp.sum(p, axis=-1, keepdims=True)
                    pn = p * (1.0 / den)
                    if has_sink:
                        pn = jnp.where(kj == 0, 0.0, pn)
                    pn = pn.astype(BF16)
                    outs.append(jnp.dot(pn, vv, preferred_element_type=F32))
                    lses.append(mx + jnp.log(den))
                o_ref[rs, cs] = jnp.where(m0, outs[0], outs[1])
                lse_ref[rs, cs] = jnp.where(m0, jnp.broadcast_to(lses[0], (BLOCK, LANES)),
                                            jnp.broadcast_to(lses[1], (BLOCK, LANES)))

    def main(sec):
        return pl.BlockSpec((qn * BLOCK, un * LANES), lambda u, i: (i, sec * ug + u))

    def prev(sec):
        return pl.BlockSpec((BLOCK, un * LANES), lambda u, i: (jnp.maximum(i * qn - 1, 0), sec * ug + u))

    in_specs = [main(0), prev(1), main(1), prev(2), main(2)]
    args = [qkv] * 5
    if has_sink:
        in_specs.append(pl.BlockSpec((un, 1, LANES), lambda u, i: (u, 0, 0)))
        args.append(sinks)
    return pl.pallas_call(
        body, name=name, grid=(ug, nb // qn), in_specs=in_specs, out_specs=(main(0), main(0)),
        out_shape=(jax.ShapeDtypeStruct((n_rows, n_units * LANES), F32),) * 2, compiler_params=_cparams(2),
    )(*args)


def _band_bwd(qkv, do, lse, delta, n_units, max_dist, name):
    n_rows = qkv.shape[0]
    nb = n_rows // BLOCK
    scale = HEAD_DIM ** -0.5

    qn, un = min(nb, BAND_Q_BLOCKS), BAND_UNITS
    ug = n_units // un
    steps = nb // qn
    nt_dims = (((1,), (1,)), ((), ()))
    tn_dims = (((0,), (0,)), ((), ()))

    def body(qm_ref, qx_ref, kp_ref, km_ref, vp_ref, vm_ref, dom_ref, dox_ref, lm_ref, lx_ref, dm_ref, dx_ref,
             dq_ref, dk_ref, dv_ref):
        i = pl.program_id(1)
        m0, m1 = _head_masks((BLOCK, LANES))
        zero = jnp.zeros((BLOCK, LANES), BF16)
        qi = lax.broadcasted_iota(jnp.int32, (BLOCK, 2 * BLOCK), 0)
        kj = lax.broadcasted_iota(jnp.int32, (BLOCK, 2 * BLOCK), 1)
        dist = qi + BLOCK - kj
        band = (dist >= 0) & (dist <= max_dist)
        band_first = band & ((i > 0) | (kj >= BLOCK))
        qr = lax.broadcasted_iota(jnp.int32, (BLOCK, BLOCK), 0)
        kc = lax.broadcasted_iota(jnp.int32, (BLOCK, BLOCK), 1)
        dist_x = qr + BLOCK - kc
        band_next = (dist_x >= 0) & (dist_x <= max_dist) & (i < steps - 1)

        def pair(q, dob, lse_b, del_b, kk, vv, valid):
            dqs, dk, dv = [], None, None
            for e, hm in enumerate((m0, m1)):
                col = slice(e * HEAD_DIM, e * HEAD_DIM + 1)
                qe = jnp.where(hm, q, zero)
                doe = jnp.where(hm, dob, zero)
                s = lax.dot_general(qe, kk, nt_dims, preferred_element_type=F32) * scale
                p = jnp.where(valid, jnp.exp(s - lse_b[:, col]), 0.0)
                dp = lax.dot_general(doe, vv, nt_dims, preferred_element_type=F32)
                ds = (p * (dp - del_b[:, col]) * scale).astype(BF16)
                dqs.append(jnp.dot(ds, kk, preferred_element_type=F32))
                dk_e = lax.dot_general(ds, qe, tn_dims, preferred_element_type=F32)
                dv_e = lax.dot_general(p.astype(BF16), doe, tn_dims, preferred_element_type=F32)
                dk = dk_e if dk is None else dk + dk_e
                dv = dv_e if dv is None else dv + dv_e
            return jnp.where(m0, dqs[0], dqs[1]), dk, dv

        for ub in range(un):
            cs = slice(ub * LANES, (ub + 1) * LANES)
            dk_acc, dv_acc = [None] * qn, [None] * qn

            def add(acc, kb, part):
                acc[kb] = part if acc[kb] is None else acc[kb] + part

            for qb in range(qn):
                rs = slice(qb * BLOCK, (qb + 1) * BLOCK)
                if qb == 0:
                    kk = jnp.concatenate([kp_ref[:, cs], km_ref[0:BLOCK, cs]], axis=0)
                    vv = jnp.concatenate([vp_ref[:, cs], vm_ref[0:BLOCK, cs]], axis=0)
                    valid = band_first
                else:
                    kk = km_ref[(qb - 1) * BLOCK:(qb + 1) * BLOCK, cs]
                    vv = vm_ref[(qb - 1) * BLOCK:(qb + 1) * BLOCK, cs]
                    valid = band
                dq, dk, dv = pair(qm_ref[rs, cs], dom_ref[rs, cs], lm_ref[rs, cs], dm_ref[rs, cs], kk, vv, valid)
                dq_ref[rs, cs] = dq
                if qb > 0:
                    add(dk_acc, qb - 1, dk[0:BLOCK])
                    add(dv_acc, qb - 1, dv[0:BLOCK])
                add(dk_acc, qb, dk[BLOCK:2 * BLOCK])
                add(dv_acc, qb, dv[BLOCK:2 * BLOCK])
            last = slice((qn - 1) * BLOCK, qn * BLOCK)
            _, dk, dv = pair(qx_ref[:, cs], dox_ref[:, cs], lx_ref[:, cs], dx_ref[:, cs], km_ref[last, cs], vm_ref[last, cs],
                             band_next)
            add(dk_acc, qn - 1, dk)
            add(dv_acc, qn - 1, dv)
            for kb in range(qn):
                dk_ref[kb * BLOCK:(kb + 1) * BLOCK, cs] = dk_acc[kb]
                dv_ref[kb * BLOCK:(kb + 1) * BLOCK, cs] = dv_acc[kb]

    def main(sec):
        return pl.BlockSpec((qn * BLOCK, un * LANES), lambda u, i: (i, sec * ug + u))

    def prev(sec):
        return pl.BlockSpec((BLOCK, un * LANES), lambda u, i: (jnp.maximum(i * qn - 1, 0), sec * ug + u))

    def nxt(sec):
        return pl.BlockSpec((BLOCK, un * LANES), lambda u, i: (jnp.minimum((i + 1) * qn, nb - 1), sec * ug + u))

    in_specs = [main(0), nxt(0), prev(1), main(1), prev(2), main(2),
                main(0), nxt(0), main(0), nxt(0), main(0), nxt(0)]
    args = [qkv] * 6 + [do, do, lse, lse, delta, delta]
    shp = jax.ShapeDtypeStruct((n_rows, n_units * LANES), F32)
    return pl.pallas_call(
        body, name=name, grid=(ug, steps), in_specs=in_specs, out_specs=(main(0), main(0), main(0)),
        out_shape=(shp, shp, shp), compiler_params=_cparams(2),
    )(*args)


def _merge_groups(os_, lses, dils, name):
    s_len = os_[0].shape[0] * dils[0]
    tm = 512

    def body(*refs):
        o_refs, l_refs = refs[0:3], refs[3:6]
        o_ref, lse_ref = refs[6:8]
        so, sl = refs[8:11], refs[11:14]
        for pair in range(2):
            for g, d in enumerate(dils):
                rows = tm // d
                for r in range(d):
                    col = slice((pair * d + r) * LANES, (pair * d + r + 1) * LANES)
                    if d == 1:
                        so[g][...] = o_refs[g][:, col]
                        sl[g][...] = l_refs[g][:, col]
                    else:
                        so[g][pl.ds(r, rows, stride=d), :] = o_refs[g][:, col]
                        sl[g][pl.ds(r, rows, stride=d), :] = l_refs[g][:, col]
            l0, l1, l2 = sl[0][...], sl[1][...], sl[2][...]
            mx = jnp.maximum(jnp.maximum(l0, l1), l2)
            e0, e1, e2 = jnp.exp(l0 - mx), jnp.exp(l1 - mx), jnp.exp(l2 - mx)
            den = e0 + e1 + e2
            inv = 1.0 / den
            o_ref[:, pair * LANES:(pair + 1) * LANES] = (so[0][...] * (e0 * inv) + so[1][...] * (e1 * inv)
                                                         + so[2][...] * (e2 * inv))
            lse_ref[:, pair * LANES:(pair + 1) * LANES] = mx + jnp.log(den)

    in_specs = [pl.BlockSpec((tm // d, 2 * d * LANES), lambda i: (i, 0)) for d in dils] * 2
    out = pl.BlockSpec((tm, 2 * LANES), lambda i: (i, 0))
    shp = jax.ShapeDtypeStruct((s_len, 2 * LANES), F32)
    return pl.pallas_call(
        body, name=name, grid=(s_len // tm,), in_specs=in_specs, out_specs=(out, out), out_shape=(shp, shp),
        scratch_shapes=[pltpu.VMEM((tm, LANES), F32)] * 6, compiler_params=_cparams(1),
    )(*os_, *lses)


def _bwd_prep(do, o, lse, dils, sinks, name):
    s_len, width = do.shape
    n_pairs = width // LANES
    tm = 512
    has_sink = sinks is not None
    n_g = len(dils)

    def body(*refs):
        do_ref, o_ref, lse_ref = refs[:3]
        pos = 3
        if has_sink:
            sink_ref = refs[pos]
            pos += 1
        outs = refs[pos:pos + 3 * n_g]
        pos += 3 * n_g
        if has_sink:
            dsink_ref = refs[pos]
            pos += 1
        s_do, s_l, s_d = refs[pos:pos + 3]
        seg = _seg_matrix(HEAD_DIM)

        if has_sink:
            @pl.when(pl.program_id(0) == 0)
            def _():
                dsink_ref[...] = jnp.zeros_like(dsink_ref)

        for pair in range(n_pairs):
            col = slice(pair * LANES, (pair + 1) * LANES)
            dov = do_ref[:, col]
            lv = lse_ref[:, col]
            delta = _seg_sum(dov * o_ref[:, col], seg)
            if has_sink:
                dsink_ref[pair] += -jnp.sum(jnp.exp(sink_ref[pair] - lv) * delta, axis=0, keepdims=True)
            s_do[...] = dov
            s_l[...] = lv
            s_d[...] = delta
            for g, d in enumerate(dils):
                rows = tm // d
                for r in range(d):
                    oc = slice((pair * d + r) * LANES, (pair * d + r + 1) * LANES)
                    if d == 1:
                        a, b, c = s_do[...], s_l[...], s_d[...]
                    else:
                        a = s_do[pl.ds(r, rows, stride=d), :]
                        b = s_l[pl.ds(r, rows, stride=d), :]
                        c = s_d[pl.ds(r, rows, stride=d), :]
                    outs[3 * g][:, oc] = a.astype(BF16)
                    outs[3 * g + 1][:, oc] = b
                    outs[3 * g + 2][:, oc] = c

    row = pl.BlockSpec((tm, width), lambda i: (i, 0))
    in_specs = [row, row, row]
    args = [do, o, lse]
    if has_sink:
        in_specs.append(pl.BlockSpec((n_pairs, 1, LANES), lambda i: (0, 0, 0)))
        args.append(sinks)
    out_specs, out_shape = [], []
    for d in dils:
        for dt in (BF16, F32, F32):
            out_specs.append(pl.BlockSpec((tm // d, n_pairs * d * LANES), lambda i: (i, 0)))
            out_shape.append(jax.ShapeDtypeStruct((s_len // d, n_pairs * d * LANES), dt))
    if has_sink:
        out_specs.append(pl.BlockSpec((n_pairs, 1, LANES), lambda i: (0, 0, 0)))
        out_shape.append(jax.ShapeDtypeStruct((n_pairs, 1, LANES), F32))
    return pl.pallas_call(
        body, name=name, grid=(s_len // tm,), in_specs=in_specs, out_specs=tuple(out_specs),
        out_shape=tuple(out_shape), scratch_shapes=[pltpu.VMEM((tm, LANES), F32)] * 3, compiler_params=_cparams(1),
    )(*args)


def _mem_kv(mem, mem_gain, w_kv, k_gain, name):
    m_len = mem.shape[0]
    kw = M_HEADS * M_HEAD_DIM

    def body(mem_ref, mg_ref, w_ref, kg_ref, k_ref, v_ref):
        mv = mem_ref[...]
        r = lax.rsqrt(jnp.mean(mv * mv, axis=-1, keepdims=True) + EPS)
        mn = ((mv * r) * mg_ref[...]).astype(BF16)
        kv = jnp.dot(mn, w_ref[...], preferred_element_type=F32)
        for h in range(M_HEADS):
            col = slice(h * M_HEAD_DIM, (h + 1) * M_HEAD_DIM)
            t = kv[:, col]
            rk = lax.rsqrt(jnp.mean(t * t, axis=-1, keepdims=True) + EPS)
            k_ref[:, col] = ((t * rk) * kg_ref[...]).astype(BF16)
        v_ref[...] = kv[:, kw:].astype(BF16)

    shp = jax.ShapeDtypeStruct((m_len, kw), BF16)
    return pl.pallas_call(body, name=name, out_shape=(shp, shp),
                          compiler_params=pltpu.CompilerParams(vmem_limit_bytes=VMEM_LIMIT_BYTES))(mem, mem_gain, w_kv, k_gain)


def _mem_kv_bwd(mem, mem_gain, w_kv, k_gain, dk, dv, name):
    m_len, d = mem.shape
    kw = M_HEADS * M_HEAD_DIM

    def body(mem_ref, mg_ref, w_ref, kg_ref, dk_ref, dv_ref, dw_ref, dmg_ref, dkg_ref, dkv_ref):
        mv = mem_ref[...]
        r = lax.rsqrt(jnp.mean(mv * mv, axis=-1, keepdims=True) + EPS)
        mhat = mv * r
        mn = (mhat * mg_ref[...]).astype(BF16)
        kv = jnp.dot(mn, w_ref[...], preferred_element_type=F32)
        dkg = jnp.zeros((1, M_HEAD_DIM), F32)
        for h in range(M_HEADS):
            col = slice(h * M_HEAD_DIM, (h + 1) * M_HEAD_DIM)
            t = kv[:, col]
            rk = lax.rsqrt(jnp.mean(t * t, axis=-1, keepdims=True) + EPS)
            that = t * rk
            dy = dk_ref[:, col]
            dkg = dkg + jnp.sum(dy * that, axis=0, keepdims=True)
            dthat = dy * kg_ref[...]
            dkv_ref[:, col] = (rk * (dthat - that * jnp.mean(dthat * that, axis=-1, keepdims=True))).astype(BF16)
        dkv_ref[:, kw:] = dv_ref[...].astype(BF16)
        dkg_ref[...] = dkg
        dkv = dkv_ref[...]
        dw_ref[...] = lax.dot_general(mn, dkv, (((0,), (0,)), ((), ())), preferred_element_type=F32)
        dmn = lax.dot_general(dkv, w_ref[...], (((1,), (1,)), ((), ())), preferred_element_type=F32)
        dmg_ref[...] = jnp.sum(dmn * mhat, axis=0, keepdims=True)

    return pl.pallas_call(
        body, name=name,
        out_shape=(jax.ShapeDtypeStruct((d, 2 * kw), F32), jax.ShapeDtypeStruct((1, d), F32),
                   jax.ShapeDtypeStruct((1, M_HEAD_DIM), F32)),
        scratch_shapes=[pltpu.VMEM((m_len, 2 * kw), BF16)],
        compiler_params=pltpu.CompilerParams(vmem_limit_bytes=VMEM_LIMIT_BYTES),
    )(mem, mem_gain, w_kv, k_gain, dk, dv)


def _mem_attn_fwd(proj, cidx, mk, mv, q_gain, name):
    s_len = proj.shape[0]
    kw = M_HEADS * M_HEAD_DIM
    tm = 512
    scale = M_HEAD_DIM ** -0.5

    def body(q_ref, k_ref, v_ref, g_ref, o_ref):
        for h in range(M_HEADS):
            col = slice(h * M_HEAD_DIM, (h + 1) * M_HEAD_DIM)
            t = q_ref[:, col]
            rs = lax.rsqrt(jnp.mean(t * t, axis=-1, keepdims=True) + EPS)
            qn = ((t * rs) * g_ref[...]).astype(BF16)
            s = lax.dot_general(qn, k_ref[:, col], (((1,), (1,)), ((), ())), preferred_element_type=F32) * scale
            mx = jnp.max(s, axis=-1, keepdims=True)
            p = jnp.exp(s - mx)
            pn = (p * (1.0 / jnp.sum(p, axis=-1, keepdims=True))).astype(BF16)
            o_ref[:, col] = jnp.dot(pn, v_ref[:, col], preferred_element_type=F32).astype(BF16)

    whole = pl.BlockSpec((MEM_LEN, kw), lambda i: (0, 0))
    return pl.pallas_call(
        body, name=name, grid=(s_len // tm,),
        in_specs=[pl.BlockSpec((tm, kw), lambda i: (i, cidx)), whole, whole, pl.BlockSpec((1, M_HEAD_DIM), lambda i: (0, 0))],
        out_specs=pl.BlockSpec((tm, kw), lambda i: (i, 0)),
        out_shape=jax.ShapeDtypeStruct((s_len, kw), BF16), compiler_params=_cparams(1),
    )(proj, mk, mv, q_gain)


def _mem_attn_bwd(proj, cidx, mk, mv, q_gain, do, name):
    s_len = proj.shape[0]
    kw = M_HEADS * M_HEAD_DIM
    tm = 512
    scale = M_HEAD_DIM ** -0.5

    def body(q_ref, k_ref, v_ref, g_ref, do_ref, dq_ref, dk_ref, dv_ref, dg_ref):
        @pl.when(pl.program_id(0) == 0)
        def _():
            dk_ref[...] = jnp.zeros_like(dk_ref)
            dv_ref[...] = jnp.zeros_like(dv_ref)
            dg_ref[...] = jnp.zeros_like(dg_ref)

        for h in range(M_HEADS):
            col = slice(h * M_HEAD_DIM, (h + 1) * M_HEAD_DIM)
            t = q_ref[:, col]
            rs = lax.rsqrt(jnp.mean(t * t, axis=-1, keepdims=True) + EPS)
            that = t * rs
            qn = (that * g_ref[...]).astype(BF16)
            kh, vh = k_ref[:, col], v_ref[:, col]
            dob = do_ref[:, col].astype(BF16)
            s = lax.dot_general(qn, kh, (((1,), (1,)), ((), ())), preferred_element_type=F32) * scale
            mx = jnp.max(s, axis=-1, keepdims=True)
            p = jnp.exp(s - mx)
            p = p * (1.0 / jnp.sum(p, axis=-1, keepdims=True))
            dp = lax.dot_general(dob, vh, (((1,), (1,)), ((), ())), preferred_element_type=F32)
            ds = (p * (dp - jnp.sum(p * dp, axis=-1, keepdims=True)) * scale).astype(BF16)
            dqn = jnp.dot(ds, kh, preferred_element_type=F32)
            dk_ref[:, col] += lax.dot_general(ds, qn, (((0,), (0,)), ((), ())), preferred_element_type=F32)
            dv_ref[:, col] += lax.dot_general(p.astype(BF16), dob, (((0,), (0,)), ((), ())), preferred_element_type=F32)
            dg_ref[...] += jnp.sum(dqn * that, axis=0, keepdims=True)
            dthat = dqn * g_ref[...]
            dq_ref[:, col] = (rs * (dthat - that * jnp.mean(dthat * that, axis=-1, keepdims=True))).astype(BF16)

    whole = pl.BlockSpec((MEM_LEN, kw), lambda i: (0, 0))
    vec = pl.BlockSpec((1, M_HEAD_DIM), lambda i: (0, 0))
    row = pl.BlockSpec((tm, kw), lambda i: (i, 0))
    return pl.pallas_call(
        body, name=name, grid=(s_len // tm,),
        in_specs=[pl.BlockSpec((tm, kw), lambda i: (i, cidx)), whole, whole, vec, row],
        out_specs=(row, whole, whole, vec),
        out_shape=(jax.ShapeDtypeStruct((s_len, kw), BF16), jax.ShapeDtypeStruct((MEM_LEN, kw), F32),
                   jax.ShapeDtypeStruct((MEM_LEN, kw), F32), jax.ShapeDtypeStruct((1, M_HEAD_DIM), F32)),
        compiler_params=_cparams(1),
    )(proj, mk, mv, q_gain, do)


def _gate_merge(gates, pa, pb, pm, name):
    s_len, d = pa.shape
    tm = 256

    def body(g_ref, a_ref, b_ref, m_ref, o_ref):
        f = lambda v: v.astype(F32)
        o_ref[...] = (f(g_ref[:, 0:d]) * f(a_ref[...]) + f(g_ref[:, d:2 * d]) * f(b_ref[...])
                      + f(g_ref[:, 2 * d:3 * d]) * f(m_ref[...])).astype(BF16)

    row = pl.BlockSpec((tm, d), lambda i: (i, 0))
    return pl.pallas_call(
        body, name=name, grid=(s_len // tm,), in_specs=[pl.BlockSpec((tm, 3 * d), lambda i: (i, 0)), row, row, row],
        out_specs=row, out_shape=jax.ShapeDtypeStruct((s_len, d), BF16), compiler_params=_cparams(1),
    )(gates, pa, pb, pm)


def _gate_merge_bwd(dmerged, gates, pa, pb, pm, name):
    s_len, d = pa.shape
    tm = 256

    def body(dm_ref, g_ref, a_ref, b_ref, m_ref, da_ref, db_ref, dmm_ref, dg_ref, dbg_ref):
        @pl.when(pl.program_id(0) == 0)
        def _():
            dbg_ref[...] = jnp.zeros_like(dbg_ref)
        dm = dm_ref[...]
        for k, (p_ref, dp_ref) in enumerate(((a_ref, da_ref), (b_ref, db_ref), (m_ref, dmm_ref))):
            col = slice(k * d, (k + 1) * d)
            g = g_ref[:, col].astype(F32)
            dp_ref[...] = (dm * g).astype(BF16)
            dpre = (dm * p_ref[...].astype(F32)) * (g * (1.0 - g))
            dbg_ref[:, col] += jnp.sum(dpre, axis=0, keepdims=True)
            dg_ref[:, col] = dpre.astype(BF16)

    row = pl.BlockSpec((tm, d), lambda i: (i, 0))
    wide = pl.BlockSpec((tm, 3 * d), lambda i: (i, 0))
    shp = jax.ShapeDtypeStruct((s_len, d), BF16)
    return pl.pallas_call(
        body, name=name, grid=(s_len // tm,), in_specs=[row, wide, row, row, row],
        out_specs=(row, row, row, wide, pl.BlockSpec((1, 3 * d), lambda i: (0, 0))),
        out_shape=(shp, shp, shp, jax.ShapeDtypeStruct((s_len, 3 * d), BF16), jax.ShapeDtypeStruct((1, 3 * d), F32)),
        compiler_params=_cparams(1),
    )(dmerged, gates, pa, pb, pm)


CONV_CHUNK = 256


def _pick_row(tile, j):
    row = lax.broadcasted_iota(jnp.int32, tile.shape, 0)
    return jnp.sum(jnp.where(row == j, tile, jnp.zeros_like(tile)), axis=0, keepdims=True)


def _rows_before(ref, start, k):
    cur = ref[pl.ds(start, CONV_CHUNK), :].astype(F32)
    prev = ref[pl.ds(pl.multiple_of(jnp.maximum(start - 16, 0), 16), 16), :].astype(F32)
    prev = jnp.where(start > 0, prev, jnp.zeros_like(prev))
    rolled = pltpu.roll(cur, k, 0)
    row = lax.broadcasted_iota(jnp.int32, cur.shape, 0)
    for j in range(k):
        rolled = jnp.where(row == j, _pick_row(prev, 16 - k + j), rolled)
    return rolled


def _rows_after(ref, start, k):
    cur = ref[pl.ds(start, CONV_CHUNK), :]
    nxt = ref[pl.ds(pl.multiple_of(start + CONV_CHUNK, 8), 8), :]
    rolled = pltpu.roll(cur, CONV_CHUNK - k, 0)
    row = lax.broadcasted_iota(jnp.int32, cur.shape, 0)
    for j in range(k):
        rolled = jnp.where(row == CONV_CHUNK - k + j, _pick_row(nxt, j), rolled)
    return rolled


def _conv_pre(u_ref, w_ref, b_ref, start):
    u2 = _rows_before(u_ref, start, 2)
    u1 = _rows_before(u_ref, start, 1)
    u0 = u_ref[pl.ds(start, CONV_CHUNK), :].astype(F32)
    c = ((b_ref[...] + w_ref[0:1, :] * u2) + w_ref[1:2, :] * u1) + w_ref[2:3, :] * u0
    return c, (u2, u1, u0)


def _conv_glu(u, conv_w, conv_b, name):
    s_len = u.shape[0]
    nblk = D_FF // LANES

    def body(ua_ref, ug_ref, wa_ref, wg_ref, ba_ref, bg_ref, o_ref):
        def chunk(ci, carry):
            start = pl.multiple_of(ci * CONV_CHUNK, CONV_CHUNK)
            ca, _ = _conv_pre(ua_ref, wa_ref, ba_ref, start)
            cg, _ = _conv_pre(ug_ref, wg_ref, bg_ref, start)
            o_ref[pl.ds(start, CONV_CHUNK), :] = ((ca * _sigmoid(ca)) * cg).astype(BF16)
            return carry
        lax.fori_loop(0, s_len // CONV_CHUNK, chunk, 0)

    def col(rows, off):
        return pl.BlockSpec((rows, LANES), lambda j: (0, off + j))

    return pl.pallas_call(
        body, name=name, grid=(nblk,),
        in_specs=[col(s_len, 0), col(s_len, nblk), col(3, 0), col(3, nblk), col(1, 0), col(1, nblk)],
        out_specs=col(s_len, 0), out_shape=jax.ShapeDtypeStruct((s_len, D_FF), BF16), compiler_params=_cparams(1),
    )(u, u, conv_w, conv_w, conv_b, conv_b)


def _conv_glu_bwd(dact, u, conv_w, conv_b, name):
    s_len = u.shape[0]
    nblk = D_FF // LANES
    n_chunks = s_len // CONV_CHUNK

    def body(da_ref, ua_ref, ug_ref, wa_ref, wg_ref, ba_ref, bg_ref,
             dua_ref, dug_ref, dwa_ref, dwg_ref, dba_ref, dbg_ref, sa, sg):
        sa[pl.ds(s_len, 8), :] = jnp.zeros((8, LANES), F32)
        sg[pl.ds(s_len, 8), :] = jnp.zeros((8, LANES), F32)
        zero = jnp.zeros((1, LANES), F32)

        def chunk1(ci, carry):
            start = pl.multiple_of(ci * CONV_CHUNK, CONV_CHUNK)
            ca, ua = _conv_pre(ua_ref, wa_ref, ba_ref, start)
            cg, ug = _conv_pre(ug_ref, wg_ref, bg_ref, start)
            dact_v = da_ref[pl.ds(start, CONV_CHUNK), :].astype(F32)
            sig = _sigmoid(ca)
            dcg = dact_v * (ca * sig)
            dca = (dact_v * cg) * (sig * (1.0 + ca * (1.0 - sig)))
            sa[pl.ds(start, CONV_CHUNK), :] = dca
            sg[pl.ds(start, CONV_CHUNK), :] = dcg
            out = [carry[0] + jnp.sum(dca, axis=0, keepdims=True), carry[1] + jnp.sum(dcg, axis=0, keepdims=True)]
            for j in range(3):
                out.append(carry[2 + j] + jnp.sum(dca * ua[j], axis=0, keepdims=True))
            for j in range(3):
                out.append(carry[5 + j] + jnp.sum(dcg * ug[j], axis=0, keepdims=True))
            return tuple(out)

        acc = lax.fori_loop(0, n_chunks, chunk1, (zero,) * 8)
        dba_ref[...] = acc[0]
        dbg_ref[...] = acc[1]
        for j in range(3):
            dwa_ref[j:j + 1, :] = acc[2 + j]
            dwg_ref[j:j + 1, :] = acc[5 + j]

        def chunk2(ci, carry):
            start = pl.multiple_of(ci * CONV_CHUNK, CONV_CHUNK)
            for s_ref, w_ref, o_ref in ((sa, wa_ref, dua_ref), (sg, wg_ref, dug_ref)):
                d0 = s_ref[pl.ds(start, CONV_CHUNK), :]
                d1 = _rows_after(s_ref, start, 1)
                d2 = _rows_after(s_ref, start, 2)
                o_ref[pl.ds(start, CONV_CHUNK), :] = (w_ref[2:3, :] * d0 + w_ref[1:2, :] * d1
                                                      + w_ref[0:1, :] * d2).astype(BF16)
            return carry
        lax.fori_loop(0, n_chunks, chunk2, 0)

    def col(rows, off):
        return pl.BlockSpec((rows, LANES), lambda j: (0, off + j))

    big = jax.ShapeDtypeStruct((s_len, D_FF), BF16)
    return pl.pallas_call(
        body, name=name, grid=(nblk,),
        in_specs=[col(s_len, 0), col(s_len, 0), col(s_len, nblk), col(3, 0), col(3, nblk), col(1, 0), col(1, nblk)],
        out_specs=(col(s_len, 0), col(s_len, 0), col(3, 0), col(3, 0), col(1, 0), col(1, 0)),
        out_shape=(big, big, jax.ShapeDtypeStruct((3, D_FF), F32), jax.ShapeDtypeStruct((3, D_FF), F32),
                   jax.ShapeDtypeStruct((1, D_FF), F32), jax.ShapeDtypeStruct((1, D_FF), F32)),
        scratch_shapes=[pltpu.VMEM((s_len + 8, LANES), F32)] * 2, compiler_params=_cparams(1),
    )(dact, u, u, conv_w, conv_w, conv_b, conv_b)


def _rope_tables(positions):
    half = ROPE_DIMS // 2
    freqs = jnp.exp(jnp.arange(half, dtype=F32) * (-2.0 * math.log(ROPE_THETA) / ROPE_DIMS))
    ang = positions.reshape(-1).astype(F32)[:, None] * freqs
    cos, sin = jnp.cos(ang), jnp.sin(ang)
    n = ang.shape[0]
    zeros = lambda w: jnp.zeros((n, w), F32)
    c = jnp.concatenate([cos, cos, jnp.ones((n, HEAD_DIM - ROPE_DIMS), F32)], axis=1)
    s1 = jnp.concatenate([-sin, zeros(HEAD_DIM - half)], axis=1)
    s2 = jnp.concatenate([zeros(half), sin, zeros(HEAD_DIM - ROPE_DIMS)], axis=1)
    return tuple(jnp.tile(t, (1, 2)) for t in (c, s1, s2))


def _two(v):
    return jnp.tile(v.reshape(1, HEAD_DIM), (1, 2))


def _fold_heads(g):
    return g[0, :HEAD_DIM] + g[0, HEAD_DIM:]


MIX_WEIGHTS = ('w_gate', 'w_mem_kv', 'w_o_a', 'w_o_b', 'w_o_m', 'w_out')
FFN_WEIGHTS = ('w_up', 'conv_w', 'w_down')


def _device_step(x, mem, positions, target, w, hooks=None):
    tabs = _rope_tables(positions)
    dils = tuple(d for _, d in A_GROUPS)
    grads = {}
    w = dict(w)

    h, h_t, r1 = _rms_fwd(x, w['attn_norm'], "rms1")
    if hooks is not None:
        w.update(hooks.weights('in', h))
    proj = _mm_rows([(h, w['w_in'], 0)], "mm_in")

    qkv_a, o_g, lse_g = [], [], []
    for gi, (window, d) in enumerate(A_GROUPS):
        gq, gk = _two(w['a_q_norm'][gi]), _two(w['a_k_norm'][gi])
        qkv = _qk_prep(proj, 6 * gi, d, False, gq, gk, tabs, f"qk_prep_a{gi}")
        o, lse = _band_fwd(qkv, 2 * d, window // d, None, f"band_fwd_a{gi}")
        qkv_a.append(qkv)
        o_g.append(o)
        lse_g.append(lse)
    o_a, lse_a = _merge_groups(o_g, lse_g, dils, "merge_a")
    if hooks is not None:
        w.update(hooks.weights('mix', o_a))

    gbq, gbk = _two(w['b_q_norm']), _two(w['b_k_norm'])
    sinks = jnp.repeat(w['b_sinks'].reshape(4, 2), HEAD_DIM, axis=1).reshape(4, 1, LANES)
    qkv_b = _qk_prep(proj, 18, 1, True, gbq, gbk, tabs, "qk_prep_b")
    o_b, lse_b = _band_fwd(qkv_b, 4, B_WINDOW - 1, sinks, "band_fwd_b")

    gates = _mm_rows([(h, w['w_gate'], 0)], "mm_gate", bias=w['b_gate'], sigmoid=True, out_dtypes=(BF16,))
    mk, mv = _mem_kv(mem, w['mem_norm'], w['w_mem_kv'], w['m_k_norm'], "mem_kv")
    o_m = _mem_attn_fwd(proj, 6, mk, mv, w['m_q_norm'], "mem_attn")

    pa = _mm_rows([(o_a, w['w_o_a'], 0)], "mm_oa", out_dtypes=(BF16,))
    pb = _mm_rows([(o_b, w['w_o_b'], 0)], "mm_ob", out_dtypes=(BF16,))
    pm = _mm_rows([(o_m, w['w_o_m'], 0)], "mm_om", out_dtypes=(BF16,))
    merged = _gate_merge(gates, pa, pb, pm, "gate_merge")
    x1 = _mm_rows([(merged, w['w_out'], 0)], "mm_out", res=x)

    if hooks is not None:
        w.update(hooks.weights('ffn', x1))
    h2, h2_t, r2 = _rms_fwd(x1, w['ffn_norm'], "rms2")
    u = _mm_rows([(h2, w['w_up'], 0)], "mm_up", out_dtypes=(BF16,))
    act = _conv_glu(u, w['conv_w'], w['conv_b'], "conv_glu")
    dy, dy_b, loss = _mm_rows([(act, w['w_down'], 0)], "mm_down", res=x1, loss_target=target)

    dact = _mm_rows([(dy_b, w['w_down'], 0)], "mm_d_act", nt=True, out_dtypes=(BF16,))
    grads['w_down'] = _mm_tn(act, dy_b, "mm_dw_down")
    du_a, du_g, dcw_a, dcw_g, dcb_a, dcb_g = _conv_glu_bwd(dact, u, w['conv_w'], w['conv_b'], "conv_glu_bwd")
    grads['conv_w'] = jnp.concatenate([dcw_a, dcw_g], axis=1)
    grads['conv_b'] = jnp.concatenate([dcb_a, dcb_g], axis=1)
    dh2 = _mm_rows([(du_a, w['w_up'], 0), (du_g, w['w_up'], 1)], "mm_d_h2", nt=True)
    grads['w_up'] = jnp.concatenate([_mm_cols(h2_t, du_a, "mm_dw_up_a"), _mm_cols(h2_t, du_g, "mm_dw_up_g")], axis=1)
    ffn_gain = w['ffn_norm']
    if hooks is not None:
        ffn_gain = ffn_gain + hooks.grads('ffn', grads)[0:1, 0:1]
    dx1, dx1_b, grads['ffn_norm'] = _rms_bwd(dh2, x1, r2, ffn_gain, dy, "rms2_bwd", bf16_copy=True)

    dmerged = _mm_rows([(dx1_b, w['w_out'], 0)], "mm_d_merged", nt=True)
    grads['w_out'] = _mm_tn(merged, dx1_b, "mm_dw_out")
    dpa, dpb, dpm, dgpre, grads['b_gate'] = _gate_merge_bwd(dmerged, gates, pa, pb, pm, "gate_merge_bwd")
    do_a = _mm_rows([(dpa, w['w_o_a'], 0)], "mm_d_oa", nt=True)
    do_b = _mm_rows([(dpb, w['w_o_b'], 0)], "mm_d_ob", nt=True)
    do_m = _mm_rows([(dpm, w['w_o_m'], 0)], "mm_d_om", nt=True)
    grads['w_o_a'] = _mm_tn(o_a, dpa, "mm_dw_oa")
    grads['w_o_b'] = _mm_tn(o_b, dpb, "mm_dw_ob")
    grads['w_o_m'] = _mm_tn(o_m, dpm, "mm_dw_om")
    grads['w_gate'] = _mm_cols(h_t, dgpre, "mm_dw_gate")
    dq_m, dmk, dmv, grads['m_q_norm'] = _mem_attn_bwd(proj, 6, mk, mv, w['m_q_norm'], do_m, "mem_attn_bwd")
    grads['w_mem_kv'], grads['mem_norm'], grads['m_k_norm'] = _mem_kv_bwd(
        mem, w['mem_norm'], w['w_mem_kv'], w['m_k_norm'], dmk, dmv, "mem_kv_bwd")
    a_gain = w['a_q_norm']
    if hooks is not None:
        a_gain = a_gain + hooks.grads('mix', grads)[0:1, 0:1]

    prep = _bwd_prep(do_a, o_a, lse_a, dils, None, "bwd_prep_a")
    dproj, dgq_a, dgk_a = [], [], []
    for gi, (window, d) in enumerate(A_GROUPS):
        gq, gk = _two(a_gain[gi]), _two(w['a_k_norm'][gi])
        dqkv = _band_bwd(qkv_a[gi], prep[3 * gi], prep[3 * gi + 1], prep[3 * gi + 2], 2 * d, window // d,
                         f"band_bwd_a{gi}")
        dp, dgq, dgk = _qk_prep_bwd(dqkv, proj, 6 * gi, d, False, gq, gk, tabs, f"qk_prep_bwd_a{gi}")
        dproj.append(dp)
        dgq_a.append(_fold_heads(dgq))
        dgk_a.append(_fold_heads(dgk))
    grads['a_q_norm'] = jnp.stack(dgq_a)
    grads['a_k_norm'] = jnp.stack(dgk_a)

    do_bu, lse_bu, delta_bu, dsink = _bwd_prep(do_b, o_b, lse_b, (1,), sinks, "bwd_prep_b")
    dqkv = _band_bwd(qkv_b, do_bu, lse_bu, delta_bu, 4, B_WINDOW - 1, "band_bwd_b")
    dp_b, dgq, dgk = _qk_prep_bwd(dqkv, proj, 18, 1, True, gbq, gbk, tabs, "qk_prep_bwd_b")
    dproj.append(dp_b)
    grads['b_q_norm'] = _fold_heads(dgq)
    grads['b_k_norm'] = _fold_heads(dgk)
    grads['b_sinks'] = jnp.stack([dsink[:, 0, 0], dsink[:, 0, HEAD_DIM]], axis=1).reshape(8)

    dproj.append(dq_m)

    cols = (0, 1, 2, 3, 6)
    grads['w_in'] = _mm_rows_cat(h_t, dproj, "mm_dw_in")
    attn_gain = w['attn_norm']
    if hooks is not None:
        attn_gain = attn_gain + hooks.grads('in', grads)[0:1, 0:1]
    dh = _mm_rows([(dp, w['w_in'], c) for dp, c in zip(dproj, cols)] + [(dgpre, w['w_gate'], 0)], "mm_d_h", nt=True)
    grad_x, grads['attn_norm'] = _rms_bwd(dh, x, r1, attn_gain, dx1, "rms1_bwd")
    return loss, grad_x, grads


def _coords():
    return lax.axis_index("x"), lax.axis_index("y"), lax.axis_index("c")


def _slot(p):
    return 4 * p[0] + 2 * p[1] + p[2]


ALL_PEERS = tuple(range(1, N_DEV))
CHIP_PEERS = (1, 4, 2, 6)
OTHER_CHIPS = (4, 2, 6)


def _peers(me, masks=ALL_PEERS):
    x, y, c = me
    return [(1 - x if mask & 4 else x, 1 - y if mask & 2 else y, 1 - c if mask & 1 else c) for mask in masks]


HBM_SPEC = pl.BlockSpec(memory_space=pltpu.HBM)


SEM_SPEC = pl.BlockSpec(memory_space=pltpu.SEMAPHORE)
SIDE_EFFECT = pltpu.SideEffectType.DATAFLOW_SIDE_EFFECTING


def _exchange_start(blocks, name, gather=False, masks=ALL_PEERS, after=None):
    n = len(blocks)
    n_peers = len(masks)
    n_in = 2 * n + (0 if after is None else 1)

    def body(*refs):
        ins, lands = refs[:n], refs[n:2 * n]
        send_sems, recv_sems = refs[n_in], refs[n_in + 1]
        token = refs[-1]
        me = _coords()
        peers = _peers(me, masks)
        for a in range(n):
            for k in range(n_peers):
                pltpu.make_async_remote_copy(
                    src_ref=ins[a] if gather else ins[a].at[_slot(peers[k])], dst_ref=lands[a].at[_slot(me)],
                    send_sem=send_sems.at[a * n_peers + k], recv_sem=recv_sems.at[a * n_peers + k],
                    device_id=peers[k], device_id_type=MESH).start()
        token[...] = jnp.zeros_like(token)

    land_shapes = [((N_DEV,) + b.shape) if gather else b.shape for b in blocks]
    hbm_in = [pltpu.HBM(b.shape, b.dtype) for b in blocks]
    hbm_land = [pltpu.HBM(s, b.dtype) for s, b in zip(land_shapes, blocks)]
    sems = pltpu.SemaphoreType.DMA((n * n_peers,))
    ins = [pltpu.with_memory_space_constraint(b, pltpu.HBM) for b in blocks]
    lands = [pltpu.with_memory_space_constraint(lax.empty(s, b.dtype), pltpu.HBM) for s, b in zip(land_shapes, blocks)]
    return pl.pallas_call(
        body, name=name, out_shape=(sems, sems, *hbm_in, *hbm_land, jax.ShapeDtypeStruct((8, LANES), F32)),
        in_specs=[HBM_SPEC] * (2 * n) + ([] if after is None else [pl.BlockSpec(memory_space=pl.ANY)]),
        out_specs=(SEM_SPEC, SEM_SPEC, *([HBM_SPEC] * (2 * n)), pl.BlockSpec(memory_space=pltpu.VMEM)),
        input_output_aliases={i: 2 + i for i in range(2 * n)},
        compiler_params=pltpu.CompilerParams(has_side_effects=SIDE_EFFECT),
    )(*ins, *lands, *([] if after is None else [after]))


def _exchange_wait(started, after, name, gather=False, masks=ALL_PEERS):
    n = (len(started) - 3) // 2
    n_peers = len(masks)
    send_sems, recv_sems = started[0], started[1]
    thru = started[2:2 + 2 * n]

    def body(*refs):
        ins, lands = refs[:n], refs[n:2 * n]
        send_ref, recv_ref = refs[2 * n], refs[2 * n + 1]
        me = _coords()
        peers = _peers(me, masks)
        for a in range(n):
            for k in range(n_peers):
                cp = pltpu.make_async_remote_copy(
                    src_ref=ins[a] if gather else ins[a].at[_slot(peers[k])], dst_ref=lands[a].at[_slot(peers[k])],
                    send_sem=send_ref.at[a * n_peers + k], recv_sem=recv_ref.at[a * n_peers + k],
                    device_id=peers[k], device_id_type=MESH)
                cp.wait_send()
                cp.wait_recv()

    hbm = [pltpu.HBM(t.shape, t.dtype) for t in thru]
    res = pl.pallas_call(
        body, name=name, out_shape=tuple(hbm),
        in_specs=[HBM_SPEC] * (2 * n) + [SEM_SPEC, SEM_SPEC, pl.BlockSpec(memory_space=pl.ANY)],
        out_specs=tuple([HBM_SPEC] * (2 * n)), input_output_aliases={i: i for i in range(2 * n)},
        compiler_params=pltpu.CompilerParams(has_side_effects=SIDE_EFFECT),
    )(*thru, send_sems, recv_sems, after)
    return res[n:]


def _sibling_forward(arrays, name):
    n = len(arrays)
    n_fwd = len(OTHER_CHIPS)

    def body(*refs):
        bufs = refs[n:2 * n]
        token, send_sems, recv_sems = refs[2 * n:]
        token[...] = jnp.zeros_like(token)
        x, y, c = _coords()
        sibling = (x, y, 1 - c)
        mine = _peers((x, y, c), OTHER_CHIPS)
        theirs = _peers(sibling, OTHER_CHIPS)

        def copy(a, k, block):
            rows = bufs[a].at[_slot(block)]
            return pltpu.make_async_remote_copy(
                src_ref=rows, dst_ref=rows, send_sem=send_sems.at[a * n_fwd + k], recv_sem=recv_sems.at[a * n_fwd + k],
                device_id=sibling, device_id_type=MESH)

        sends = [copy(a, k, mine[k]) for a in range(n) for k in range(n_fwd)]
        for cp in sends:
            cp.start()
        for a in range(n):
            for k in range(n_fwd):
                copy(a, k, theirs[k]).wait_recv()
        for cp in sends:
            cp.wait_send()

    res = pl.pallas_call(
        body, name=name, in_specs=[HBM_SPEC] * n,
        out_specs=tuple([HBM_SPEC] * n + [pl.BlockSpec(memory_space=pltpu.VMEM)]),
        out_shape=tuple([jax.ShapeDtypeStruct(a.shape, a.dtype) for a in arrays] + [jax.ShapeDtypeStruct((8, LANES), F32)]),
        input_output_aliases={i: i for i in range(n)},
        scratch_shapes=[pltpu.SemaphoreType.DMA((n * n_fwd,)), pltpu.SemaphoreType.DMA((n * n_fwd,))],
    )(*arrays)
    return res[:n], res[n]


def _all_sum(p, name):
    def body(p_ref, o_ref, recv, send_sems, recv_sems):
        me = _coords()
        peers = _peers(me)
        recv[_slot(me)] = p_ref[...]

        def copy(k, landing):
            return pltpu.make_async_remote_copy(
                src_ref=p_ref, dst_ref=recv.at[_slot(landing)], send_sem=send_sems.at[k], recv_sem=recv_sems.at[k],
                device_id=peers[k], device_id_type=MESH)

        sends = [copy(k, me) for k in range(N_DEV - 1)]
        for cp in sends:
            cp.start()
        for k in range(N_DEV - 1):
            copy(k, peers[k]).wait_recv()
        for cp in sends:
            cp.wait_send()
        acc = recv[0]
        for s in range(1, N_DEV):
            acc = acc + recv[s]
        o_ref[...] = acc

    vmem = pl.BlockSpec(memory_space=pltpu.VMEM)
    return pl.pallas_call(
        body, name=name, in_specs=[vmem], out_specs=vmem, out_shape=jax.ShapeDtypeStruct(p.shape, F32),
        scratch_shapes=[pltpu.VMEM((N_DEV,) + p.shape, F32), pltpu.SemaphoreType.DMA((N_DEV - 1,)),
                        pltpu.SemaphoreType.DMA((N_DEV - 1,))],
    )(p)


def _adam(w, g, m, v):
    m2 = ADAM_B1 * m + (1.0 - ADAM_B1) * g
    v2 = ADAM_B2 * v + (1.0 - ADAM_B2) * (g * g)
    m_hat = m2 / (1.0 - ADAM_B1 ** ADAM_STEP)
    v_hat = v2 / (1.0 - ADAM_B2 ** ADAM_STEP)
    delta = -ADAM_LR * (m_hat / (jnp.sqrt(v_hat) + ADAM_EPS) + ADAM_WD * w)
    return delta, m2, v2


def _row_tile(rows, cols):
    best = rows
    for t in range(16, rows, 16):
        if rows % t == 0 and t * cols * 4 <= (1 << 20):
            best = t
    return best


def _adam_reduce(parts, w, m, v, name):
    rows, cols = w.shape
    tr = _row_tile(rows, cols)

    def body(p_ref, w_ref, m_ref, v_ref, g_ref, d_ref, m2_ref, v2_ref):
        g = p_ref[0].astype(F32)
        for s in range(1, N_DEV):
            g = g + p_ref[s].astype(F32)
        g_ref[...] = g
        d_ref[...], m2_ref[...], v2_ref[...] = _adam(w_ref[...], g, m_ref[...], v_ref[...])

    blk = pl.BlockSpec((tr, cols), lambda i: (i, 0))
    shp = jax.ShapeDtypeStruct((rows, cols), F32)
    return pl.pallas_call(
        body, name=name, grid=(rows // tr,),
        in_specs=[pl.BlockSpec((N_DEV, tr, cols), lambda i: (0, i, 0)), blk, blk, blk],
        out_specs=(blk,) * 4, out_shape=(shp,) * 4, compiler_params=_cparams(1),
    )(parts, w, m, v)


PACK_COLS = 1024
PACK = {'attn_norm': (0, 1, 1024), 'mem_norm': (1, 1, 1024), 'ffn_norm': (2, 1, 1024), 'b_gate': (3, 3, 1024),
        'conv_b': (6, 6, 1024), 'a_q_norm': (12, 3, 64), 'a_k_norm': (15, 3, 64), 'b_q_norm': (18, 1, 64),
        'b_k_norm': (19, 1, 64), 'm_q_norm': (20, 1, 128), 'm_k_norm': (21, 1, 128), 'b_sinks': (22, 1, 8)}
PACK_LOSS_ROW = 23
PACK_ROWS = 24


def _pack_pieces(name, width):
    r0, nr, lanes = PACK[name]
    out = []
    for j in range(nr):
        if lanes == PACK_COLS:
            w = min(PACK_COLS, width - j * PACK_COLS)
            out.append((r0 + j, slice(0, 1), slice(j * PACK_COLS, j * PACK_COLS + w), w))
        else:
            out.append((r0 + j, slice(j, j + 1), slice(0, lanes), lanes))
    return out


def _pack_small(grads, loss_tile, name):
    names = list(PACK)

    def body(*refs):
        o_ref = refs[-1]
        o_ref[...] = jnp.zeros_like(o_ref)
        for k, nm in enumerate(names):
            for row, rs, ls, w in _pack_pieces(nm, refs[k].shape[1]):
                o_ref[row:row + 1, 0:w] = refs[k][rs, ls]
        o_ref[PACK_LOSS_ROW:PACK_LOSS_ROW + 1, 0:1] = refs[len(names)][0:1, 0:1]

    vmem = pl.BlockSpec(memory_space=pltpu.VMEM)
    args = [grads[nm] for nm in names] + [loss_tile]
    return pl.pallas_call(body, name=name, in_specs=[vmem] * len(args), out_specs=vmem,
                          out_shape=jax.ShapeDtypeStruct((PACK_ROWS, PACK_COLS), F32))(*args)


def _adam_small(gsum, ws, ms, vs, name):
    names = list(PACK)
    n = len(names)

    def body(*refs):
        g_ref = refs[0]
        w_refs, m_refs, v_refs = refs[1:1 + n], refs[1 + n:1 + 2 * n], refs[1 + 2 * n:1 + 3 * n]
        outs = refs[1 + 3 * n:]
        outs[0][...] = g_ref[PACK_LOSS_ROW:PACK_LOSS_ROW + 1, 0:1]
        for k, nm in enumerate(names):
            o_g, o_d, o_m, o_v = outs[1 + 4 * k:5 + 4 * k]
            for row, rs, ls, width in _pack_pieces(nm, w_refs[k].shape[1]):
                src = (rs, ls)
                g = g_ref[row:row + 1, 0:width]
                d, m2, v2 = _adam(w_refs[k][src], g, m_refs[k][src], v_refs[k][src])
                o_g[src] = g
                o_d[src] = d
                o_m[src] = m2
                o_v[src] = v2

    vmem = pl.BlockSpec(memory_space=pltpu.VMEM)
    shapes = [jax.ShapeDtypeStruct((1, 1), F32)]
    for nm in names:
        shapes += [jax.ShapeDtypeStruct(ws[nm].shape, F32)] * 4
    args = [gsum] + [ws[nm] for nm in names] + [ms[nm] for nm in names] + [vs[nm] for nm in names]
    return pl.pallas_call(
        body, name=name, in_specs=[vmem] * len(args), out_specs=tuple([vmem] * len(shapes)), out_shape=tuple(shapes),
    )(*args)


def _as2d(name, a):
    return a.reshape(a.shape[-2], a.shape[-1]) if a.ndim == 3 else a


def kernel(x, mem, positions, attn_norm, w_in, a_q_norm, a_k_norm, b_q_norm, b_k_norm, b_sinks, mem_norm, w_mem_kv, m_q_norm, m_k_norm, w_o_a, w_o_b, w_o_m, w_gate, b_gate, w_out, ffn_norm, w_up, conv_w, conv_b, w_down, loss_target, m_attn_norm, m_w_in, m_a_q_norm, m_a_k_norm, m_b_q_norm, m_b_k_norm, m_b_sinks, m_mem_norm, m_w_mem_kv, m_m_q_norm, m_m_k_norm, m_w_o_a, m_w_o_b, m_w_o_m, m_w_gate, m_b_gate, m_w_out, m_ffn_norm, m_w_up, m_conv_w, m_conv_b, m_w_down, v_attn_norm, v_w_in, v_a_q_norm, v_a_k_norm, v_b_q_norm, v_b_k_norm, v_b_sinks, v_mem_norm, v_w_mem_kv, v_m_q_norm, v_m_k_norm, v_w_o_a, v_w_o_b, v_w_o_m, v_w_gate, v_b_gate, v_w_out, v_ffn_norm, v_w_up, v_conv_w, v_conv_b, v_w_down):
    given = dict(attn_norm=attn_norm, w_in=w_in, a_q_norm=a_q_norm, a_k_norm=a_k_norm, b_q_norm=b_q_norm, b_k_norm=b_k_norm, b_sinks=b_sinks, mem_norm=mem_norm, w_mem_kv=w_mem_kv, m_q_norm=m_q_norm, m_k_norm=m_k_norm, w_o_a=w_o_a, w_o_b=w_o_b, w_o_m=w_o_m, w_gate=w_gate, b_gate=b_gate, w_out=w_out, ffn_norm=ffn_norm, w_up=w_up, conv_w=conv_w, conv_b=conv_b, w_down=w_down)
    mom1 = dict(attn_norm=m_attn_norm, w_in=m_w_in, a_q_norm=m_a_q_norm, a_k_norm=m_a_k_norm, b_q_norm=m_b_q_norm, b_k_norm=m_b_k_norm, b_sinks=m_b_sinks, mem_norm=m_mem_norm, w_mem_kv=m_w_mem_kv, m_q_norm=m_m_q_norm, m_k_norm=m_m_k_norm, w_o_a=m_w_o_a, w_o_b=m_w_o_b, w_o_m=m_w_o_m, w_gate=m_w_gate, b_gate=m_b_gate, w_out=m_w_out, ffn_norm=m_ffn_norm, w_up=m_w_up, conv_w=m_conv_w, conv_b=m_conv_b, w_down=m_w_down)
    mom2 = dict(attn_norm=v_attn_norm, w_in=v_w_in, a_q_norm=v_a_q_norm, a_k_norm=v_a_k_norm, b_q_norm=v_b_q_norm, b_k_norm=v_b_k_norm, b_sinks=v_b_sinks, mem_norm=v_mem_norm, w_mem_kv=v_w_mem_kv, m_q_norm=v_m_q_norm, m_k_norm=v_m_k_norm, w_o_a=v_w_o_a, w_o_b=v_w_o_b, w_o_m=v_w_o_m, w_gate=v_w_gate, b_gate=v_b_gate, w_out=v_w_out, ffn_norm=v_ffn_norm, w_up=v_w_up, conv_w=v_conv_w, conv_b=v_conv_b, w_down=v_w_down)

    big = list(BIG)
    stages = {'mix': list(MIX_WEIGHTS), 'ffn': list(FFN_WEIGHTS), 'in': ['w_in']}
    my_slot = _slot(_coords())

    def shard(n):
        return given[n][0] if n == 'conv_w' else given[n][0].astype(BF16)

    def whole(n, g):
        _, r, c = g.shape
        return g.reshape(N_DEV * r, c) if BIG[n] == 0 else g.transpose(1, 0, 2).reshape(r, N_DEV * c)

    def to_blocks(n, g):
        r, c = given[n].shape[1:]
        g = g.reshape(N_DEV, r, c) if BIG[n] == 0 else g.reshape(r, N_DEV, c).transpose(1, 0, 2)
        return g if n == 'conv_w' else g.astype(BF16)

    class Hooks:
        next_stage = {'in': 'mix', 'mix': 'ffn'}

        def __init__(self):
            self.coming, self.sent = {}, {}
            self.shards = {n: shard(n) for n in big}
            self.start_gather('in', None)

        def start_gather(self, stage, after):
            src = [self.shards[n] for n in stages[stage]]
            self.coming[stage] = _exchange_start(src, f"gather_{stage}_start", gather=True, masks=CHIP_PEERS,
                                                 after=after)

        def weights(self, stage, after):
            names = stages[stage]
            landed = _exchange_wait(self.coming[stage], after, f"gather_{stage}_wait", gather=True, masks=CHIP_PEERS)
            landed, token = _sibling_forward(landed, f"gather_{stage}_forward")
            if stage in self.next_stage:
                self.start_gather(self.next_stage[stage], token)
            return {n: whole(n, lax.dynamic_update_slice_in_dim(land, self.shards[n][None], my_slot, axis=0))
                    for n, land in zip(names, landed)}

        def grads(self, stage, g):
            blocks = [to_blocks(n, g[n]) for n in stages[stage]]
            own = [lax.dynamic_slice_in_dim(b, my_slot, 1, axis=0) for b in blocks]
            self.sent[stage] = (_exchange_start(blocks, f"exchange_{stage}_start"), own)
            return self.sent[stage][0][-1]

        def parts(self, stage, after):
            started, own = self.sent[stage]
            landed = _exchange_wait(started, after, f"exchange_{stage}_wait")
            return {n: lax.dynamic_update_slice_in_dim(land, o, my_slot, axis=0)
                    for n, land, o in zip(stages[stage], landed, own)}

    hooks = Hooks()
    w = {}
    for n in SMALL:
        w[n] = given[n]
    w['a_q_norm'], w['a_k_norm'] = given['a_q_norm'][0], given['a_k_norm'][0]
    w['b_q_norm'], w['b_k_norm'], w['b_sinks'] = given['b_q_norm'][0], given['b_k_norm'][0], given['b_sinks'][0]

    loss_tile, grad_x, grads = _device_step(x[0], mem[0], positions[0], loss_target[0], w, hooks)
    out = {}
    after = grad_x
    for stage in ('ffn', 'mix', 'in'):
        for n, p in hooks.parts(stage, after).items():
            res = _adam_reduce(p, given[n][0], mom1[n][0], mom2[n][0], f"adam_{n}")
            out[n] = tuple(t[None] for t in res)
            after = res[0]

    small = {n: grads[n] for n in PACK}
    small['b_q_norm'], small['b_k_norm'] = grads['b_q_norm'].reshape(1, -1), grads['b_k_norm'].reshape(1, -1)
    small['b_sinks'] = grads['b_sinks'].reshape(1, -1)
    gsum = _all_sum(_pack_small(small, loss_tile, "pack_small"), "sum_small")
    ws = {n: _as2d(n, given[n]) for n in PACK}
    ms = {n: _as2d(n, mom1[n]) for n in PACK}
    vs = {n: _as2d(n, mom2[n]) for n in PACK}
    res = _adam_small(gsum, ws, ms, vs, "adam_small")
    loss = res[0].reshape(())
    for k, n in enumerate(PACK):
        out[n] = tuple(t.reshape(given[n].shape) for t in res[1 + 4 * k:5 + 4 * k])

    outs = [loss, grad_x[None]]
    for field in range(4):
        outs += [out[n][field] for n in WEIGHTS]
    return tuple(outs)
```

```python
import functools
import math

import jax
import jax.numpy as jnp
from jax import lax
from jax.experimental import pallas as pl
from jax.experimental.pallas import tpu as pltpu

F32 = jnp.float32
BF16 = jnp.bfloat16

N_DEV = 8
D_MODEL = 1024
HEAD_DIM = 64
A_GROUPS = ((128, 1), (512, 4), (2048, 16))
B_WINDOW = 128
M_HEADS = 4
M_HEAD_DIM = 128
MEM_LEN = 256
D_FF = 2816
ROPE_THETA = 500000.0
ROPE_DIMS = 16
BLOCK = 128
EPS = 1e-6
LANES = 128
BAND_Q_BLOCKS = 4
BAND_UNITS = 2

ADAM_LR = 0.001
ADAM_B1 = 0.9
ADAM_B2 = 0.999
ADAM_EPS = 1e-08
ADAM_WD = 0.01
ADAM_STEP = 10

VMEM_LIMIT_BYTES = 56 * 1024 * 1024
MESH = pl.DeviceIdType.MESH

WEIGHTS = ['attn_norm', 'w_in', 'a_q_norm', 'a_k_norm', 'b_q_norm', 'b_k_norm', 'b_sinks', 'mem_norm',
           'w_mem_kv', 'm_q_norm', 'm_k_norm', 'w_o_a', 'w_o_b', 'w_o_m', 'w_gate', 'b_gate', 'w_out',
           'ffn_norm', 'w_up', 'conv_w', 'conv_b', 'w_down']
BIG = {'w_in': 1, 'w_mem_kv': 0, 'w_o_a': 1, 'w_o_b': 1, 'w_o_m': 1, 'w_gate': 1, 'w_out': 0, 'w_up': 1,
       'conv_w': 1, 'w_down': 0}
SMALL = [n for n in WEIGHTS if n not in BIG]


def _cparams(n_grid):
    return pltpu.CompilerParams(dimension_semantics=("arbitrary",) * n_grid, vmem_limit_bytes=VMEM_LIMIT_BYTES)


def _seg_matrix(width):
    shift = width.bit_length() - 1
    r = lax.shift_right_logical(lax.broadcasted_iota(jnp.int32, (LANES, LANES), 0), shift)
    c = lax.shift_right_logical(lax.broadcasted_iota(jnp.int32, (LANES, LANES), 1), shift)
    return jnp.where(r == c, 1.0, 0.0).astype(BF16)


def _seg_sum(x, seg):
    hi = x.astype(BF16)
    r1 = x - hi.astype(F32)
    mid = r1.astype(BF16)
    lo = (r1 - mid.astype(F32)).astype(BF16)
    dot = functools.partial(jnp.dot, preferred_element_type=F32)
    return dot(hi, seg) + dot(mid, seg) + dot(lo, seg)


def _rope(y, c, s1, s2):
    return y * c + pltpu.roll(y, LANES - ROPE_DIMS // 2, 1) * s1 + pltpu.roll(y, ROPE_DIMS // 2, 1) * s2


def _unrope(dy, c, s1, s2):
    return dy * c + pltpu.roll(dy * s1, ROPE_DIMS // 2, 1) + pltpu.roll(dy * s2, LANES - ROPE_DIMS // 2, 1)


def _sigmoid(x):
    return 1.0 / (1.0 + jnp.exp(-x))


def _rms_fwd(x, gain, name):
    s_len, d = x.shape
    tm = 512

    def body(x_ref, g_ref, h_ref, ht_ref, r_ref):
        xv = x_ref[...]
        r = lax.rsqrt(jnp.mean(xv * xv, axis=-1, keepdims=True) + EPS)
        h = ((xv * r) * g_ref[...]).astype(BF16)
        h_ref[...] = h
        ht_ref[...] = h.T
        r_ref[...] = r

    return pl.pallas_call(
        body, name=name, grid=(s_len // tm,),
        in_specs=[pl.BlockSpec((tm, d), lambda i: (i, 0)), pl.BlockSpec((1, d), lambda i: (0, 0))],
        out_specs=(pl.BlockSpec((tm, d), lambda i: (i, 0)), pl.BlockSpec((d, tm), lambda i: (0, i)),
                   pl.BlockSpec((tm, 1), lambda i: (i, 0))),
        out_shape=(jax.ShapeDtypeStruct((s_len, d), BF16), jax.ShapeDtypeStruct((d, s_len), BF16),
                   jax.ShapeDtypeStruct((s_len, 1), F32)),
        compiler_params=_cparams(1),
    )(x, gain)


def _rms_bwd(dh, x, r, gain, add, name, bf16_copy=False):
    s_len, d = x.shape
    tm = 512

    def body(dh_ref, x_ref, r_ref, g_ref, add_ref, dx_ref, *rest):
        dg_ref = rest[-1]

        @pl.when(pl.program_id(0) == 0)
        def _():
            dg_ref[...] = jnp.zeros_like(dg_ref)
        rv = r_ref[...]
        xhat = x_ref[...] * rv
        dhv = dh_ref[...]
        dg_ref[...] += jnp.sum(dhv * xhat, axis=0, keepdims=True)
        dxhat = dhv * g_ref[...]
        dx = add_ref[...] + rv * (dxhat - xhat * jnp.mean(dxhat * xhat, axis=-1, keepdims=True))
        dx_ref[...] = dx
        if bf16_copy:
            rest[0][...] = dx.astype(BF16)

    row = pl.BlockSpec((tm, d), lambda i: (i, 0))
    vec = pl.BlockSpec((1, d), lambda i: (0, 0))
    out_specs = [row] + ([row] if bf16_copy else []) + [vec]
    out_shape = [jax.ShapeDtypeStruct((s_len, d), F32)] + ([jax.ShapeDtypeStruct((s_len, d), BF16)] if bf16_copy else [])
    out_shape.append(jax.ShapeDtypeStruct((1, d), F32))
    return pl.pallas_call(
        body, name=name, grid=(s_len // tm,),
        in_specs=[row, row, pl.BlockSpec((tm, 1), lambda i: (i, 0)), vec, row],
        out_specs=tuple(out_specs), out_shape=tuple(out_shape), compiler_params=_cparams(1),
    )(dh, x, r, gain, add)


def _resident(shape, index_map):
    return pl.BlockSpec(shape, index_map, pipeline_mode=pl.Buffered(1))


def _mm_rows(pairs, name, nt=False, tm=512, bias=None, sigmoid=False, res=None, out_dtypes=(F32,), loss_target=None):
    m = pairs[0][0].shape[0]
    n = pairs[0][1].shape[0] if nt else pairs[0][1].shape[1]
    n_pairs = len(pairs)
    has_bias, has_res, has_loss = bias is not None, res is not None, loss_target is not None
    dims = (((1,), (1,)), ((), ())) if nt else (((1,), (0,)), ((), ()))

    def body(*refs):
        acc = None
        for p in range(n_pairs):
            t = lax.dot_general(refs[2 * p][...].astype(BF16), refs[2 * p + 1][...], dims, preferred_element_type=F32)
            acc = t if acc is None else acc + t
        pos = 2 * n_pairs
        if has_bias:
            acc = acc + refs[pos][...]
            pos += 1
        if sigmoid:
            acc = _sigmoid(acc)
        if has_res:
            acc = refs[pos][...] + acc
            pos += 1
        if has_loss:
            dy_ref, dyb_ref, l_ref = refs[pos + 1:]

            @pl.when(pl.program_id(0) == 0)
            def _():
                l_ref[...] = jnp.zeros_like(l_ref)
            err = acc - refs[pos][...]
            dy = err * (1.0 / n)
            dy_ref[...] = dy
            dyb_ref[...] = dy.astype(BF16)
            part = 0.5 * jnp.sum(jnp.mean(err * err, axis=-1, keepdims=True), axis=0, keepdims=True)
            l_ref[...] += jnp.broadcast_to(part, l_ref.shape)
            return
        for o_ref in refs[pos:]:
            o_ref[...] = acc.astype(o_ref.dtype)

    in_specs, args = [], []
    for a, w, blk in pairs:
        k = a.shape[1]
        in_specs.append(pl.BlockSpec((tm, k), lambda i: (i, 0)))
        if nt:
            in_specs.append(_resident((n, k), lambda i, blk=blk: (0, blk)))
        else:
            in_specs.append(_resident((k, n), lambda i, blk=blk: (blk, 0)))
        args += [a, w]
    if has_bias:
        in_specs.append(_resident((1, n), lambda i: (0, 0)))
        args.append(bias)
    if has_res:
        in_specs.append(pl.BlockSpec((tm, n), lambda i: (i, 0)))
        args.append(res)
    out = pl.BlockSpec((tm, n), lambda i: (i, 0))
    if has_loss:
        return pl.pallas_call(
            body, name=name, grid=(m // tm,), in_specs=in_specs + [out],
            out_specs=(out, out, pl.BlockSpec((8, LANES), lambda i: (0, 0))),
            out_shape=(jax.ShapeDtypeStruct((m, n), F32), jax.ShapeDtypeStruct((m, n), BF16),
                       jax.ShapeDtypeStruct((8, LANES), F32)),
            compiler_params=_cparams(1),
        )(*args, loss_target)
    outs = pl.pallas_call(
        body, name=name, grid=(m // tm,), in_specs=in_specs, out_specs=tuple([out] * len(out_dtypes)),
        out_shape=tuple(jax.ShapeDtypeStruct((m, n), dt) for dt in out_dtypes), compiler_params=_cparams(1),
    )(*args)
    return outs[0] if len(out_dtypes) == 1 else outs


def _mm_rows_cat(a, ws, name, tm=256):
    m, k = a.shape
    widths = [w.shape[1] for w in ws]
    n = sum(widths)

    def body(*refs):
        a_ref, o_ref = refs[0], refs[-1]
        av = a_ref[...]
        off = 0
        for p, width in enumerate(widths):
            o_ref[:, off:off + width] = jnp.dot(av, refs[1 + p][...], preferred_element_type=F32)
            off += width

    return pl.pallas_call(
        body, name=name, grid=(m // tm,),
        in_specs=[pl.BlockSpec((tm, k), lambda i: (i, 0))] + [_resident((k, wd), lambda i: (0, 0)) for wd in widths],
        out_specs=pl.BlockSpec((tm, n), lambda i: (i, 0)),
        out_shape=jax.ShapeDtypeStruct((m, n), F32), compiler_params=_cparams(1),
    )(a, *ws)


def _mm_cols(a, b, name, tn=256):
    m, k = a.shape
    n = b.shape[1]

    def body(a_ref, b_ref, o_ref):
        o_ref[...] = jnp.dot(a_ref[...], b_ref[...].astype(BF16), preferred_element_type=F32)

    return pl.pallas_call(
        body, name=name, grid=(n // tn,),
        in_specs=[_resident((m, k), lambda j: (0, 0)), pl.BlockSpec((k, tn), lambda j: (0, j))],
        out_specs=pl.BlockSpec((m, tn), lambda j: (0, j)),
        out_shape=jax.ShapeDtypeStruct((m, n), F32), compiler_params=_cparams(1),
    )(a, b)


def _mm_tn(a, b, name, tile=256):
    k, m = a.shape
    n = b.shape[1]
    dims = (((0,), (0,)), ((), ()))

    def body(a_ref, b_ref, o_ref):
        o_ref[...] = lax.dot_general(a_ref[...].astype(BF16), b_ref[...].astype(BF16), dims, preferred_element_type=F32)

    if n <= m:
        t = min(tile, m)
        grid, a_spec, b_spec = (m // t,), pl.BlockSpec((k, t), lambda i: (0, i)), _resident((k, n), lambda i: (0, 0))
        o_spec = pl.BlockSpec((t, n), lambda i: (i, 0))
    else:
        t = min(tile, n)
        grid, a_spec, b_spec = (n // t,), _resident((k, m), lambda i: (0, 0)), pl.BlockSpec((k, t), lambda i: (0, i))
        o_spec = pl.BlockSpec((m, t), lambda i: (0, i))
    return pl.pallas_call(
        body, name=name, grid=grid, in_specs=[a_spec, b_spec], out_specs=o_spec,
        out_shape=jax.ShapeDtypeStruct((m, n), F32), compiler_params=_cparams(1),
    )(a, b)


def _norm_rope(t, gain, c, s1, s2, seg):
    rs = lax.rsqrt(_seg_sum(t * t, seg) * (1.0 / HEAD_DIM) + EPS)
    return _rope((t * rs) * gain, c, s1, s2)


def _dup_half(y, half):
    lane = lax.broadcasted_iota(jnp.int32, y.shape, 1)
    rolled = pltpu.roll(y, HEAD_DIM, 1)
    keep = (lane < HEAD_DIM) if half == 0 else (lane >= HEAD_DIM)
    return jnp.where(keep, y, rolled)


def _qk_prep(proj, cb0, d, gqa, gq, gk, tabs, name):
    s_len = proj.shape[0]
    tm = 512
    rows = tm // d
    n_units = 4 if gqa else 2 * d
    n_q = 4 if gqa else 2
    n_in = 6

    def body(*refs):
        in_refs = refs[:n_in]
        gq_ref, gk_ref, c_ref, s1_ref, s2_ref, o_ref = refs[n_in:]
        seg = _seg_matrix(HEAD_DIM)

        def rows_of(ref, r):
            return ref[...] if d == 1 else ref[pl.ds(r, rows, stride=d), :]

        def put(unit_col, y):
            o_ref[:, unit_col * LANES:(unit_col + 1) * LANES] = y.astype(BF16)

        for r in range(d):
            c, s1, s2 = rows_of(c_ref, r), rows_of(s1_ref, r), rows_of(s2_ref, r)
            for b in range(n_in):
                t = rows_of(in_refs[b], r)
                if b < n_q:
                    put((b * d + r) if not gqa else b, _norm_rope(t, gq_ref[...], c, s1, s2, seg))
                elif not gqa:
                    sec, pair = (1, b - 2) if b < 4 else (2, b - 4)
                    y = _norm_rope(t, gk_ref[...], c, s1, s2, seg) if sec == 1 else t
                    put(sec * n_units + pair * d + r, y)
                else:
                    sec = 1 if b == 4 else 2
                    y = _norm_rope(t, gk_ref[...], c, s1, s2, seg) if sec == 1 else t
                    for u in range(n_units):
                        put(sec * n_units + u, _dup_half(y, u // 2))

    in_specs = [pl.BlockSpec((tm, LANES), lambda i, b=b: (i, cb0 + b)) for b in range(n_in)]
    vec = pl.BlockSpec((1, LANES), lambda i: (0, 0))
    tab = pl.BlockSpec((tm, LANES), lambda i: (i, 0))
    width = 3 * n_units * LANES
    return pl.pallas_call(
        body, name=name, grid=(s_len // tm,), in_specs=in_specs + [vec, vec, tab, tab, tab],
        out_specs=pl.BlockSpec((rows, width), lambda i: (i, 0)),
        out_shape=jax.ShapeDtypeStruct((s_len // d, width), BF16), compiler_params=_cparams(1),
    )(*([proj] * n_in), gq, gk, *tabs)


def _qk_prep_bwd(dqkv, proj, cb0, d, gqa, gq, gk, tabs, name):
    s_len = proj.shape[0]
    tm = 512
    rows = tm // d
    n_units = 4 if gqa else 2 * d
    n_q = 4 if gqa else 2
    n_in = 6

    def body(*refs):
        d_refs = refs[0:3]
        in_refs = refs[3:3 + n_in]
        gq_ref, gk_ref, c_ref, s1_ref, s2_ref, o_ref, dgq_ref, dgk_ref, stage = refs[3 + n_in:]
        seg = _seg_matrix(HEAD_DIM)

        @pl.when(pl.program_id(0) == 0)
        def _():
            dgq_ref[...] = jnp.zeros_like(dgq_ref)
            dgk_ref[...] = jnp.zeros_like(dgk_ref)

        def rows_of(ref, r):
            return ref[...] if d == 1 else ref[pl.ds(r, rows, stride=d), :]

        def unit(col):
            sec, u = divmod(col, n_units)
            return d_refs[sec][:, u * LANES:(u + 1) * LANES]

        def norm_bwd(dyr, t, gain, c, s1, s2, dg_ref):
            rs = lax.rsqrt(_seg_sum(t * t, seg) * (1.0 / HEAD_DIM) + EPS)
            that = t * rs
            dy = _unrope(dyr, c, s1, s2)
            dg_ref[...] += jnp.sum(dy * that, axis=0, keepdims=True)
            dthat = dy * gain
            return rs * (dthat - that * (_seg_sum(dthat * that, seg) * (1.0 / HEAD_DIM)))

        def fold(sec):
            tot = []
            for u in range(n_units):
                v = unit(sec * n_units + u)
                tot.append(v + pltpu.roll(v, HEAD_DIM, 1))
            lane = lax.broadcasted_iota(jnp.int32, tot[0].shape, 1)
            return jnp.where(lane < HEAD_DIM, tot[0] + tot[1], tot[2] + tot[3])

        for b in range(n_in):
            for r in range(d):
                c, s1, s2 = rows_of(c_ref, r), rows_of(s1_ref, r), rows_of(s2_ref, r)
                t = rows_of(in_refs[b], r)
                if b < n_q:
                    g = unit((b * d + r) if not gqa else b)
                    out = norm_bwd(g, t, gq_ref[...], c, s1, s2, dgq_ref)
                elif not gqa:
                    sec, pair = (1, b - 2) if b < 4 else (2, b - 4)
                    g = unit(sec * n_units + pair * d + r)
                    out = norm_bwd(g, t, gk_ref[...], c, s1, s2, dgk_ref) if sec == 1 else g
                else:
                    sec = 1 if b == 4 else 2
                    g = fold(sec)
                    out = norm_bwd(g, t, gk_ref[...], c, s1, s2, dgk_ref) if sec == 1 else g
                if d == 1:
                    o_ref[:, b * LANES:(b + 1) * LANES] = out.astype(BF16)
                else:
                    stage[pl.ds(r, rows, stride=d), :] = out
            if d != 1:
                o_ref[:, b * LANES:(b + 1) * LANES] = stage[...].astype(BF16)

    in_specs = [pl.BlockSpec((rows, n_units * LANES), lambda i: (i, 0))] * 3
    in_specs += [pl.BlockSpec((tm, LANES), lambda i, b=b: (i, cb0 + b)) for b in range(n_in)]
    vec = pl.BlockSpec((1, LANES), lambda i: (0, 0))
    tab = pl.BlockSpec((tm, LANES), lambda i: (i, 0))
    return pl.pallas_call(
        body, name=name, grid=(s_len // tm,), in_specs=in_specs + [vec, vec, tab, tab, tab],
        out_specs=(pl.BlockSpec((tm, n_in * LANES), lambda i: (i, 0)), vec, vec),
        out_shape=(jax.ShapeDtypeStruct((s_len, n_in * LANES), BF16), jax.ShapeDtypeStruct((1, LANES), F32),
                   jax.ShapeDtypeStruct((1, LANES), F32)),
        scratch_shapes=[pltpu.VMEM((tm, LANES), F32)], compiler_params=_cparams(1),
    )(*dqkv, *([proj] * n_in), gq, gk, *tabs)


def _head_masks(shape):
    lane = lax.broadcasted_iota(jnp.int32, shape, 1)
    return lane < HEAD_DIM, lane >= HEAD_DIM


def _band_fwd(qkv, n_units, max_dist, sinks, name):
    n_rows = qkv.shape[0]
    nb = n_rows // BLOCK
    scale = HEAD_DIM ** -0.5
    has_sink = sinks is not None
    assert not has_sink or max_dist < BLOCK

    qn, un = min(nb, BAND_Q_BLOCKS), BAND_UNITS
    ug = n_units // un

    def body(*refs):
        q_ref, kp_ref, km_ref, vp_ref, vm_ref = refs[:5]
        o_ref, lse_ref = refs[-2:]
        i = pl.program_id(1)
        qi = lax.broadcasted_iota(jnp.int32, (BLOCK, 2 * BLOCK), 0)
        kj = lax.broadcasted_iota(jnp.int32, (BLOCK, 2 * BLOCK), 1)
        dist = qi + BLOCK - kj
        band = (dist >= 0) & (dist <= max_dist)
        band_first = band & ((i > 0) | (kj >= BLOCK))
        m0, m1 = _head_masks((BLOCK, LANES))
        zero = jnp.zeros((BLOCK, LANES), BF16)
        for ub in range(un):
            cs = slice(ub * LANES, (ub + 1) * LANES)
            for qb in range(qn):
                rs = slice(qb * BLOCK, (qb + 1) * BLOCK)
                q = q_ref[rs, cs]
                if qb == 0:
                    kk = jnp.concatenate([kp_ref[:, cs], km_ref[0:BLOCK, cs]], axis=0)
                    vv = jnp.concatenate([vp_ref[:, cs], vm_ref[0:BLOCK, cs]], axis=0)
                    valid = band_first
                else:
                    kk = km_ref[(qb - 1) * BLOCK:(qb + 1) * BLOCK, cs]
                    vv = vm_ref[(qb - 1) * BLOCK:(qb + 1) * BLOCK, cs]
                    valid = band
                outs, lses = [], []
                for e, hm in enumerate((m0, m1)):
                    qe = jnp.where(hm, q, zero)
                    s = lax.dot_general(qe, kk, (((1,), (1,)), ((), ())), preferred_element_type=F32) * scale
                    s = jnp.where(valid, s, -jnp.inf)
                    if has_sink:
                        s = jnp.where(kj == 0, refs[5][ub][:, e * HEAD_DIM:e * HEAD_DIM + 1], s)
                    mx = jnp.max(s, axis=-1, keepdims=True)
                    p = jnp.exp(s - mx)
                    den = jnp.sum(p, axis=-1, keepdims=True)
                    pn = p * (1.0 / den)
                    if has_sink:
                        pn = jnp.where(kj == 0, 0.0, pn)
                    pn = pn.astype(BF16)
                    outs.append(jnp.dot(pn, vv, preferred_element_type=F32))
                    lses.append(mx + jnp.log(den))
                o_ref[rs, cs] = jnp.where(m0, outs[0], outs[1])
                lse_ref[rs, cs] = jnp.where(m0, jnp.broadcast_to(lses[0], (BLOCK, LANES)),
                                            jnp.broadcast_to(lses[1], (BLOCK, LANES)))

    def main(sec):
        return pl.BlockSpec((qn * BLOCK, un * LANES), lambda u, i: (i, sec * ug + u))

    def prev(sec):
        return pl.BlockSpec((BLOCK, un * LANES), lambda u, i: (jnp.maximum(i * qn - 1, 0), sec * ug + u))

    in_specs = [main(0), prev(1), main(1), prev(2), main(2)]
    args = [qkv] * 5
    if has_sink:
        in_specs.append(pl.BlockSpec((un, 1, LANES), lambda u, i: (u, 0, 0)))
        args.append(sinks)
    return pl.pallas_call(
        body, name=name, grid=(ug, nb // qn), in_specs=in_specs, out_specs=(main(0), main(0)),
        out_shape=(jax.ShapeDtypeStruct((n_rows, n_units * LANES), F32),) * 2, compiler_params=_cparams(2),
    )(*args)


def _band_bwd(qkv, do, lse, delta, n_units, max_dist, name):
    n_rows = qkv.shape[0]
    nb = n_rows // BLOCK
    scale = HEAD_DIM ** -0.5

    qn, un = min(nb, BAND_Q_BLOCKS), BAND_UNITS
    ug = n_units // un
    steps = nb // qn
    nt_dims = (((1,), (1,)), ((), ()))
    tn_dims = (((0,), (0,)), ((), ()))

    def body(qm_ref, qx_ref, kp_ref, km_ref, vp_ref, vm_ref, dom_ref, dox_ref, lm_ref, lx_ref, dm_ref, dx_ref,
             dq_ref, dk_ref, dv_ref):
        i = pl.program_id(1)
        m0, m1 = _head_masks((BLOCK, LANES))
        zero = jnp.zeros((BLOCK, LANES), BF16)
        qi = lax.broadcasted_iota(jnp.int32, (BLOCK, 2 * BLOCK), 0)
        kj = lax.broadcasted_iota(jnp.int32, (BLOCK, 2 * BLOCK), 1)
        dist = qi + BLOCK - kj
        band = (dist >= 0) & (dist <= max_dist)
        band_first = band & ((i > 0) | (kj >= BLOCK))
        qr = lax.broadcasted_iota(jnp.int32, (BLOCK, BLOCK), 0)
        kc = lax.broadcasted_iota(jnp.int32, (BLOCK, BLOCK), 1)
        dist_x = qr + BLOCK - kc
        band_next = (dist_x >= 0) & (dist_x <= max_dist) & (i < steps - 1)

        def pair(q, dob, lse_b, del_b, kk, vv, valid):
            dqs, dk, dv = [], None, None
            for e, hm in enumerate((m0, m1)):
                col = slice(e * HEAD_DIM, e * HEAD_DIM + 1)
                qe = jnp.where(hm, q, zero)
                doe = jnp.where(hm, dob, zero)
                s = lax.dot_general(qe, kk, nt_dims, preferred_element_type=F32) * scale
                p = jnp.where(valid, jnp.exp(s - lse_b[:, col]), 0.0)
                dp = lax.dot_general(doe, vv, nt_dims, preferred_element_type=F32)
                ds = (p * (dp - del_b[:, col]) * scale).astype(BF16)
                dqs.append(jnp.dot(ds, kk, preferred_element_type=F32))
                dk_e = lax.dot_general(ds, qe, tn_dims, preferred_element_type=F32)
                dv_e = lax.dot_general(p.astype(BF16), doe, tn_dims, preferred_element_type=F32)
                dk = dk_e if dk is None else dk + dk_e
                dv = dv_e if dv is None else dv + dv_e
            return jnp.where(m0, dqs[0], dqs[1]), dk, dv

        for ub in range(un):
            cs = slice(ub * LANES, (ub + 1) * LANES)
            dk_acc, dv_acc = [None] * qn, [None] * qn

            def add(acc, kb, part):
                acc[kb] = part if acc[kb] is None else acc[kb] + part

            for qb in range(qn):
                rs = slice(qb * BLOCK, (qb + 1) * BLOCK)
                if qb == 0:
                    kk = jnp.concatenate([kp_ref[:, cs], km_ref[0:BLOCK, cs]], axis=0)
                    vv = jnp.concatenate([vp_ref[:, cs], vm_ref[0:BLOCK, cs]], axis=0)
                    valid = band_first
                else:
                    kk = km_ref[(qb - 1) * BLOCK:(qb + 1) * BLOCK, cs]
                    vv = vm_ref[(qb - 1) * BLOCK:(qb + 1) * BLOCK, cs]
                    valid = band
                dq, dk, dv = pair(qm_ref[rs, cs], dom_ref[rs, cs], lm_ref[rs, cs], dm_ref[rs, cs], kk, vv, valid)
                dq_ref[rs, cs] = dq
                if qb > 0:
                    add(dk_acc, qb - 1, dk[0:BLOCK])
                    add(dv_acc, qb - 1, dv[0:BLOCK])
                add(dk_acc, qb, dk[BLOCK:2 * BLOCK])
                add(dv_acc, qb, dv[BLOCK:2 * BLOCK])
            last = slice((qn - 1) * BLOCK, qn * BLOCK)
            _, dk, dv = pair(qx_ref[:, cs], dox_ref[:, cs], lx_ref[:, cs], dx_ref[:, cs], km_ref[last, cs], vm_ref[last, cs],
                             band_next)
            add(dk_acc, qn - 1, dk)
            add(dv_acc, qn - 1, dv)
            for kb in range(qn):
                dk_ref[kb * BLOCK:(kb + 1) * BLOCK, cs] = dk_acc[kb]
                dv_ref[kb * BLOCK:(kb + 1) * BLOCK, cs] = dv_acc[kb]

    def main(sec):
        return pl.BlockSpec((qn * BLOCK, un * LANES), lambda u, i: (i, sec * ug + u))

    def prev(sec):
        return pl.BlockSpec((BLOCK, un * LANES), lambda u, i: (jnp.maximum(i * qn - 1, 0), sec * ug + u))

    def nxt(sec):
        return pl.BlockSpec((BLOCK, un * LANES), lambda u, i: (jnp.minimum((i + 1) * qn, nb - 1), sec * ug + u))

    in_specs = [main(0), nxt(0), prev(1), main(1), prev(2), main(2),
                main(0), nxt(0), main(0), nxt(0), main(0), nxt(0)]
    args = [qkv] * 6 + [do, do, lse, lse, delta, delta]
    shp = jax.ShapeDtypeStruct((n_rows, n_units * LANES), F32)
    return pl.pallas_call(
        body, name=name, grid=(ug, steps), in_specs=in_specs, out_specs=(main(0), main(0), main(0)),
        out_shape=(shp, shp, shp), compiler_params=_cparams(2),
    )(*args)


def _merge_groups(os_, lses, dils, name):
    s_len = os_[0].shape[0] * dils[0]
    tm = 512

    def body(*refs):
        o_refs, l_refs = refs[0:3], refs[3:6]
        o_ref, lse_ref = refs[6:8]
        so, sl = refs[8:11], refs[11:14]
        for pair in range(2):
            for g, d in enumerate(dils):
                rows = tm // d
                for r in range(d):
                    col = slice((pair * d + r) * LANES, (pair * d + r + 1) * LANES)
                    if d == 1:
                        so[g][...] = o_refs[g][:, col]
                        sl[g][...] = l_refs[g][:, col]
                    else:
                        so[g][pl.ds(r, rows, stride=d), :] = o_refs[g][:, col]
                        sl[g][pl.ds(r, rows, stride=d), :] = l_refs[g][:, col]
            l0, l1, l2 = sl[0][...], sl[1][...], sl[2][...]
            mx = jnp.maximum(jnp.maximum(l0, l1), l2)
            e0, e1, e2 = jnp.exp(l0 - mx), jnp.exp(l1 - mx), jnp.exp(l2 - mx)
            den = e0 + e1 + e2
            inv = 1.0 / den
            o_ref[:, pair * LANES:(pair + 1) * LANES] = (so[0][...] * (e0 * inv) + so[1][...] * (e1 * inv)
                                                         + so[2][...] * (e2 * inv))
            lse_ref[:, pair * LANES:(pair + 1) * LANES] = mx + jnp.log(den)

    in_specs = [pl.BlockSpec((tm // d, 2 * d * LANES), lambda i: (i, 0)) for d in dils] * 2
    out = pl.BlockSpec((tm, 2 * LANES), lambda i: (i, 0))
    shp = jax.ShapeDtypeStruct((s_len, 2 * LANES), F32)
    return pl.pallas_call(
        body, name=name, grid=(s_len // tm,), in_specs=in_specs, out_specs=(out, out), out_shape=(shp, shp),
        scratch_shapes=[pltpu.VMEM((tm, LANES), F32)] * 6, compiler_params=_cparams(1),
    )(*os_, *lses)


def _bwd_prep(do, o, lse, dils, sinks, name):
    s_len, width = do.shape
    n_pairs = width // LANES
    tm = 512
    has_sink = sinks is not None
    n_g = len(dils)

    def body(*refs):
        do_ref, o_ref, lse_ref = refs[:3]
        pos = 3
        if has_sink:
            sink_ref = refs[pos]
            pos += 1
        outs = refs[pos:pos + 3 * n_g]
        pos += 3 * n_g
        if has_sink:
            dsink_ref = refs[pos]
            pos += 1
        s_do, s_l, s_d = refs[pos:pos + 3]
        seg = _seg_matrix(HEAD_DIM)

        if has_sink:
            @pl.when(pl.program_id(0) == 0)
            def _():
                dsink_ref[...] = jnp.zeros_like(dsink_ref)

        for pair in range(n_pairs):
            col = slice(pair * LANES, (pair + 1) * LANES)
            dov = do_ref[:, col]
            lv = lse_ref[:, col]
            delta = _seg_sum(dov * o_ref[:, col], seg)
            if has_sink:
                dsink_ref[pair] += -jnp.sum(jnp.exp(sink_ref[pair] - lv) * delta, axis=0, keepdims=True)
            s_do[...] = dov
            s_l[...] = lv
            s_d[...] = delta
            for g, d in enumerate(dils):
                rows = tm // d
                for r in range(d):
                    oc = slice((pair * d + r) * LANES, (pair * d + r + 1) * LANES)
                    if d == 1:
                        a, b, c = s_do[...], s_l[...], s_d[...]
                    else:
                        a = s_do[pl.ds(r, rows, stride=d), :]
                        b = s_l[pl.ds(r, rows, stride=d), :]
                        c = s_d[pl.ds(r, rows, stride=d), :]
                    outs[3 * g][:, oc] = a.astype(BF16)
                    outs[3 * g + 1][:, oc] = b
                    outs[3 * g + 2][:, oc] = c

    row = pl.BlockSpec((tm, width), lambda i: (i, 0))
    in_specs = [row, row, row]
    args = [do, o, lse]
    if has_sink:
        in_specs.append(pl.BlockSpec((n_pairs, 1, LANES), lambda i: (0, 0, 0)))
        args.append(sinks)
    out_specs, out_shape = [], []
    for d in dils:
        for dt in (BF16, F32, F32):
            out_specs.append(pl.BlockSpec((tm // d, n_pairs * d * LANES), lambda i: (i, 0)))
            out_shape.append(jax.ShapeDtypeStruct((s_len // d, n_pairs * d * LANES), dt))
    if has_sink:
        out_specs.append(pl.BlockSpec((n_pairs, 1, LANES), lambda i: (0, 0, 0)))
        out_shape.append(jax.ShapeDtypeStruct((n_pairs, 1, LANES), F32))
    return pl.pallas_call(
        body, name=name, grid=(s_len // tm,), in_specs=in_specs, out_specs=tuple(out_specs),
        out_shape=tuple(out_shape), scratch_shapes=[pltpu.VMEM((tm, LANES), F32)] * 3, compiler_params=_cparams(1),
    )(*args)


def _mem_kv(mem, mem_gain, w_kv, k_gain, name):
    m_len = mem.shape[0]
    kw = M_HEADS * M_HEAD_DIM

    def body(mem_ref, mg_ref, w_ref, kg_ref, k_ref, v_ref):
        mv = mem_ref[...]
        r = lax.rsqrt(jnp.mean(mv * mv, axis=-1, keepdims=True) + EPS)
        mn = ((mv * r) * mg_ref[...]).astype(BF16)
        kv = jnp.dot(mn, w_ref[...], preferred_element_type=F32)
        for h in range(M_HEADS):
            col = slice(h * M_HEAD_DIM, (h + 1) * M_HEAD_DIM)
            t = kv[:, col]
            rk = lax.rsqrt(jnp.mean(t * t, axis=-1, keepdims=True) + EPS)
            k_ref[:, col] = ((t * rk) * kg_ref[...]).astype(BF16)
        v_ref[...] = kv[:, kw:].astype(BF16)

    shp = jax.ShapeDtypeStruct((m_len, kw), BF16)
    return pl.pallas_call(body, name=name, out_shape=(shp, shp),
                          compiler_params=pltpu.CompilerParams(vmem_limit_bytes=VMEM_LIMIT_BYTES))(mem, mem_gain, w_kv, k_gain)


def _mem_kv_bwd(mem, mem_gain, w_kv, k_gain, dk, dv, name):
    m_len, d = mem.shape
    kw = M_HEADS * M_HEAD_DIM

    def body(mem_ref, mg_ref, w_ref, kg_ref, dk_ref, dv_ref, dw_ref, dmg_ref, dkg_ref, dkv_ref):
        mv = mem_ref[...]
        r = lax.rsqrt(jnp.mean(mv * mv, axis=-1, keepdims=True) + EPS)
        mhat = mv * r
        mn = (mhat * mg_ref[...]).astype(BF16)
        kv = jnp.dot(mn, w_ref[...], preferred_element_type=F32)
        dkg = jnp.zeros((1, M_HEAD_DIM), F32)
        for h in range(M_HEADS):
            col = slice(h * M_HEAD_DIM, (h + 1) * M_HEAD_DIM)
            t = kv[:, col]
            rk = lax.rsqrt(jnp.mean(t * t, axis=-1, keepdims=True) + EPS)
            that = t * rk
            dy = dk_ref[:, col]
            dkg = dkg + jnp.sum(dy * that, axis=0, keepdims=True)
            dthat = dy * kg_ref[...]
            dkv_ref[:, col] = (rk * (dthat - that * jnp.mean(dthat * that, axis=-1, keepdims=True))).astype(BF16)
        dkv_ref[:, kw:] = dv_ref[...].astype(BF16)
        dkg_ref[...] = dkg
        dkv = dkv_ref[...]
        dw_ref[...] = lax.dot_general(mn, dkv, (((0,), (0,)), ((), ())), preferred_element_type=F32)
        dmn = lax.dot_general(dkv, w_ref[...], (((1,), (1,)), ((), ())), preferred_element_type=F32)
        dmg_ref[...] = jnp.sum(dmn * mhat, axis=0, keepdims=True)

    return pl.pallas_call(
        body, name=name,
        out_shape=(jax.ShapeDtypeStruct((d, 2 * kw), F32), jax.ShapeDtypeStruct((1, d), F32),
                   jax.ShapeDtypeStruct((1, M_HEAD_DIM), F32)),
        scratch_shapes=[pltpu.VMEM((m_len, 2 * kw), BF16)],
        compiler_params=pltpu.CompilerParams(vmem_limit_bytes=VMEM_LIMIT_BYTES),
    )(mem, mem_gain, w_kv, k_gain, dk, dv)


def _mem_attn_fwd(proj, cidx, mk, mv, q_gain, name):
    s_len = proj.shape[0]
    kw = M_HEADS * M_HEAD_DIM
    tm = 512
    scale = M_HEAD_DIM ** -0.5

    def body(q_ref, k_ref, v_ref, g_ref, o_ref):
        for h in range(M_HEADS):
            col = slice(h * M_HEAD_DIM, (h + 1) * M_HEAD_DIM)
            t = q_ref[:, col]
            rs = lax.rsqrt(jnp.mean(t * t, axis=-1, keepdims=True) + EPS)
            qn = ((t * rs) * g_ref[...]).astype(BF16)
            s = lax.dot_general(qn, k_ref[:, col], (((1,), (1,)), ((), ())), preferred_element_type=F32) * scale
            mx = jnp.max(s, axis=-1, keepdims=True)
            p = jnp.exp(s - mx)
            pn = (p * (1.0 / jnp.sum(p, axis=-1, keepdims=True))).astype(BF16)
            o_ref[:, col] = jnp.dot(pn, v_ref[:, col], preferred_element_type=F32).astype(BF16)

    whole = pl.BlockSpec((MEM_LEN, kw), lambda i: (0, 0))
    return pl.pallas_call(
        body, name=name, grid=(s_len // tm,),
        in_specs=[pl.BlockSpec((tm, kw), lambda i: (i, cidx)), whole, whole, pl.BlockSpec((1, M_HEAD_DIM), lambda i: (0, 0))],
        out_specs=pl.BlockSpec((tm, kw), lambda i: (i, 0)),
        out_shape=jax.ShapeDtypeStruct((s_len, kw), BF16), compiler_params=_cparams(1),
    )(proj, mk, mv, q_gain)


def _mem_attn_bwd(proj, cidx, mk, mv, q_gain, do, name):
    s_len = proj.shape[0]
    kw = M_HEADS * M_HEAD_DIM
    tm = 512
    scale = M_HEAD_DIM ** -0.5

    def body(q_ref, k_ref, v_ref, g_ref, do_ref, dq_ref, dk_ref, dv_ref, dg_ref):
        @pl.when(pl.program_id(0) == 0)
        def _():
            dk_ref[...] = jnp.zeros_like(dk_ref)
            dv_ref[...] = jnp.zeros_like(dv_ref)
            dg_ref[...] = jnp.zeros_like(dg_ref)

        for h in range(M_HEADS):
            col = slice(h * M_HEAD_DIM, (h + 1) * M_HEAD_DIM)
            t = q_ref[:, col]
            rs = lax.rsqrt(jnp.mean(t * t, axis=-1, keepdims=True) + EPS)
            that = t * rs
            qn = (that * g_ref[...]).astype(BF16)
            kh, vh = k_ref[:, col], v_ref[:, col]
            dob = do_ref[:, col].astype(BF16)
            s = lax.dot_general(qn, kh, (((1,), (1,)), ((), ())), preferred_element_type=F32) * scale
            mx = jnp.max(s, axis=-1, keepdims=True)
            p = jnp.exp(s - mx)
            p = p * (1.0 / jnp.sum(p, axis=-1, keepdims=True))
            dp = lax.dot_general(dob, vh, (((1,), (1,)), ((), ())), preferred_element_type=F32)
            ds = (p * (dp - jnp.sum(p * dp, axis=-1, keepdims=True)) * scale).astype(BF16)
            dqn = jnp.dot(ds, kh, preferred_element_type=F32)
            dk_ref[:, col] += lax.dot_general(ds, qn, (((0,), (0,)), ((), ())), preferred_element_type=F32)
            dv_ref[:, col] += lax.dot_general(p.astype(BF16), dob, (((0,), (0,)), ((), ())), preferred_element_type=F32)
            dg_ref[...] += jnp.sum(dqn * that, axis=0, keepdims=True)
            dthat = dqn * g_ref[...]
            dq_ref[:, col] = (rs * (dthat - that * jnp.mean(dthat * that, axis=-1, keepdims=True))).astype(BF16)

    whole = pl.BlockSpec((MEM_LEN, kw), lambda i: (0, 0))
    vec = pl.BlockSpec((1, M_HEAD_DIM), lambda i: (0, 0))
    row = pl.BlockSpec((tm, kw), lambda i: (i, 0))
    return pl.pallas_call(
        body, name=name, grid=(s_len // tm,),
        in_specs=[pl.BlockSpec((tm, kw), lambda i: (i, cidx)), whole, whole, vec, row],
        out_specs=(row, whole, whole, vec),
        out_shape=(jax.ShapeDtypeStruct((s_len, kw), BF16), jax.ShapeDtypeStruct((MEM_LEN, kw), F32),
                   jax.ShapeDtypeStruct((MEM_LEN, kw), F32), jax.ShapeDtypeStruct((1, M_HEAD_DIM), F32)),
        compiler_params=_cparams(1),
    )(proj, mk, mv, q_gain, do)


def _gate_merge(gates, pa, pb, pm, name):
    s_len, d = pa.shape
    tm = 256

    def body(g_ref, a_ref, b_ref, m_ref, o_ref):
        f = lambda v: v.astype(F32)
        o_ref[...] = (f(g_ref[:, 0:d]) * f(a_ref[...]) + f(g_ref[:, d:2 * d]) * f(b_ref[...])
                      + f(g_ref[:, 2 * d:3 * d]) * f(m_ref[...])).astype(BF16)

    row = pl.BlockSpec((tm, d), lambda i: (i, 0))
    return pl.pallas_call(
        body, name=name, grid=(s_len // tm,), in_specs=[pl.BlockSpec((tm, 3 * d), lambda i: (i, 0)), row, row, row],
        out_specs=row, out_shape=jax.ShapeDtypeStruct((s_len, d), BF16), compiler_params=_cparams(1),
    )(gates, pa, pb, pm)


def _gate_merge_bwd(dmerged, gates, pa, pb, pm, name):
    s_len, d = pa.shape
    tm = 256

    def body(dm_ref, g_ref, a_ref, b_ref, m_ref, da_ref, db_ref, dmm_ref, dg_ref, dbg_ref):
        @pl.when(pl.program_id(0) == 0)
        def _():
            dbg_ref[...] = jnp.zeros_like(dbg_ref)
        dm = dm_ref[...]
        for k, (p_ref, dp_ref) in enumerate(((a_ref, da_ref), (b_ref, db_ref), (m_ref, dmm_ref))):
            col = slice(k * d, (k + 1) * d)
            g = g_ref[:, col].astype(F32)
            dp_ref[...] = (dm * g).astype(BF16)
            dpre = (dm * p_ref[...].astype(F32)) * (g * (1.0 - g))
            dbg_ref[:, col] += jnp.sum(dpre, axis=0, keepdims=True)
            dg_ref[:, col] = dpre.astype(BF16)

    row = pl.BlockSpec((tm, d), lambda i: (i, 0))
    wide = pl.BlockSpec((tm, 3 * d), lambda i: (i, 0))
    shp = jax.ShapeDtypeStruct((s_len, d), BF16)
    return pl.pallas_call(
        body, name=name, grid=(s_len // tm,), in_specs=[row, wide, row, row, row],
        out_specs=(row, row, row, wide, pl.BlockSpec((1, 3 * d), lambda i: (0, 0))),
        out_shape=(shp, shp, shp, jax.ShapeDtypeStruct((s_len, 3 * d), BF16), jax.ShapeDtypeStruct((1, 3 * d), F32)),
        compiler_params=_cparams(1),
    )(dmerged, gates, pa, pb, pm)


CONV_CHUNK = 256


def _pick_row(tile, j):
    row = lax.broadcasted_iota(jnp.int32, tile.shape, 0)
    return jnp.sum(jnp.where(row == j, tile, jnp.zeros_like(tile)), axis=0, keepdims=True)


def _rows_before(ref, start, k):
    cur = ref[pl.ds(start, CONV_CHUNK), :].astype(F32)
    prev = ref[pl.ds(pl.multiple_of(jnp.maximum(start - 16, 0), 16), 16), :].astype(F32)
    prev = jnp.where(start > 0, prev, jnp.zeros_like(prev))
    rolled = pltpu.roll(cur, k, 0)
    row = lax.broadcasted_iota(jnp.int32, cur.shape, 0)
    for j in range(k):
        rolled = jnp.where(row == j, _pick_row(prev, 16 - k + j), rolled)
    return rolled


def _rows_after(ref, start, k):
    cur = ref[pl.ds(start, CONV_CHUNK), :]
    nxt = ref[pl.ds(pl.multiple_of(start + CONV_CHUNK, 8), 8), :]
    rolled = pltpu.roll(cur, CONV_CHUNK - k, 0)
    row = lax.broadcasted_iota(jnp.int32, cur.shape, 0)
    for j in range(k):
        rolled = jnp.where(row == CONV_CHUNK - k + j, _pick_row(nxt, j), rolled)
    return rolled


def _conv_pre(u_ref, w_ref, b_ref, start):
    u2 = _rows_before(u_ref, start, 2)
    u1 = _rows_before(u_ref, start, 1)
    u0 = u_ref[pl.ds(start, CONV_CHUNK), :].astype(F32)
    c = ((b_ref[...] + w_ref[0:1, :] * u2) + w_ref[1:2, :] * u1) + w_ref[2:3, :] * u0
    return c, (u2, u1, u0)


def _conv_glu(u, conv_w, conv_b, name):
    s_len = u.shape[0]
    nblk = D_FF // LANES

    def body(ua_ref, ug_ref, wa_ref, wg_ref, ba_ref, bg_ref, o_ref):
        def chunk(ci, carry):
            start = pl.multiple_of(ci * CONV_CHUNK, CONV_CHUNK)
            ca, _ = _conv_pre(ua_ref, wa_ref, ba_ref, start)
            cg, _ = _conv_pre(ug_ref, wg_ref, bg_ref, start)
            o_ref[pl.ds(start, CONV_CHUNK), :] = ((ca * _sigmoid(ca)) * cg).astype(BF16)
            return carry
        lax.fori_loop(0, s_len // CONV_CHUNK, chunk, 0)

    def col(rows, off):
        return pl.BlockSpec((rows, LANES), lambda j: (0, off + j))

    return pl.pallas_call(
        body, name=name, grid=(nblk,),
        in_specs=[col(s_len, 0), col(s_len, nblk), col(3, 0), col(3, nblk), col(1, 0), col(1, nblk)],
        out_specs=col(s_len, 0), out_shape=jax.ShapeDtypeStruct((s_len, D_FF), BF16), compiler_params=_cparams(1),
    )(u, u, conv_w, conv_w, conv_b, conv_b)


def _conv_glu_bwd(dact, u, conv_w, conv_b, name):
    s_len = u.shape[0]
    nblk = D_FF // LANES
    n_chunks = s_len // CONV_CHUNK

    def body(da_ref, ua_ref, ug_ref, wa_ref, wg_ref, ba_ref, bg_ref,
             dua_ref, dug_ref, dwa_ref, dwg_ref, dba_ref, dbg_ref, sa, sg):
        sa[pl.ds(s_len, 8), :] = jnp.zeros((8, LANES), F32)
        sg[pl.ds(s_len, 8), :] = jnp.zeros((8, LANES), F32)
        zero = jnp.zeros((1, LANES), F32)

        def chunk1(ci, carry):
            start = pl.multiple_of(ci * CONV_CHUNK, CONV_CHUNK)
            ca, ua = _conv_pre(ua_ref, wa_ref, ba_ref, start)
            cg, ug = _conv_pre(ug_ref, wg_ref, bg_ref, start)
            dact_v = da_ref[pl.ds(start, CONV_CHUNK), :].astype(F32)
            sig = _sigmoid(ca)
            dcg = dact_v * (ca * sig)
            dca = (dact_v * cg) * (sig * (1.0 + ca * (1.0 - sig)))
            sa[pl.ds(start, CONV_CHUNK), :] = dca
            sg[pl.ds(start, CONV_CHUNK), :] = dcg
            out = [carry[0] + jnp.sum(dca, axis=0, keepdims=True), carry[1] + jnp.sum(dcg, axis=0, keepdims=True)]
            for j in range(3):
                out.append(carry[2 + j] + jnp.sum(dca * ua[j], axis=0, keepdims=True))
            for j in range(3):
                out.append(carry[5 + j] + jnp.sum(dcg * ug[j], axis=0, keepdims=True))
            return tuple(out)

        acc = lax.fori_loop(0, n_chunks, chunk1, (zero,) * 8)
        dba_ref[...] = acc[0]
        dbg_ref[...] = acc[1]
        for j in range(3):
            dwa_ref[j:j + 1, :] = acc[2 + j]
            dwg_ref[j:j + 1, :] = acc[5 + j]

        def chunk2(ci, carry):
            start = pl.multiple_of(ci * CONV_CHUNK, CONV_CHUNK)
            for s_ref, w_ref, o_ref in ((sa, wa_ref, dua_ref), (sg, wg_ref, dug_ref)):
                d0 = s_ref[pl.ds(start, CONV_CHUNK), :]
                d1 = _rows_after(s_ref, start, 1)
                d2 = _rows_after(s_ref, start, 2)
                o_ref[pl.ds(start, CONV_CHUNK), :] = (w_ref[2:3, :] * d0 + w_ref[1:2, :] * d1
                                                      + w_ref[0:1, :] * d2).astype(BF16)
            return carry
        lax.fori_loop(0, n_chunks, chunk2, 0)

    def col(rows, off):
        return pl.BlockSpec((rows, LANES), lambda j: (0, off + j))

    big = jax.ShapeDtypeStruct((s_len, D_FF), BF16)
    return pl.pallas_call(
        body, name=name, grid=(nblk,),
        in_specs=[col(s_len, 0), col(s_len, 0), col(s_len, nblk), col(3, 0), col(3, nblk), col(1, 0), col(1, nblk)],
        out_specs=(col(s_len, 0), col(s_len, 0), col(3, 0), col(3, 0), col(1, 0), col(1, 0)),
        out_shape=(big, big, jax.ShapeDtypeStruct((3, D_FF), F32), jax.ShapeDtypeStruct((3, D_FF), F32),
                   jax.ShapeDtypeStruct((1, D_FF), F32), jax.ShapeDtypeStruct((1, D_FF), F32)),
        scratch_shapes=[pltpu.VMEM((s_len + 8, LANES), F32)] * 2, compiler_params=_cparams(1),
    )(dact, u, u, conv_w, conv_w, conv_b, conv_b)


def _rope_tables(positions):
    half = ROPE_DIMS // 2
    freqs = jnp.exp(jnp.arange(half, dtype=F32) * (-2.0 * math.log(ROPE_THETA) / ROPE_DIMS))
    ang = positions.reshape(-1).astype(F32)[:, None] * freqs
    cos, sin = jnp.cos(ang), jnp.sin(ang)
    n = ang.shape[0]
    zeros = lambda w: jnp.zeros((n, w), F32)
    c = jnp.concatenate([cos, cos, jnp.ones((n, HEAD_DIM - ROPE_DIMS), F32)], axis=1)
    s1 = jnp.concatenate([-sin, zeros(HEAD_DIM - half)], axis=1)
    s2 = jnp.concatenate([zeros(half), sin, zeros(HEAD_DIM - ROPE_DIMS)], axis=1)
    return tuple(jnp.tile(t, (1, 2)) for t in (c, s1, s2))


def _two(v):
    return jnp.tile(v.reshape(1, HEAD_DIM), (1, 2))


def _fold_heads(g):
    return g[0, :HEAD_DIM] + g[0, HEAD_DIM:]


MIX_WEIGHTS = ('w_gate', 'w_mem_kv', 'w_o_a', 'w_o_b', 'w_o_m', 'w_out')
FFN_WEIGHTS = ('w_up', 'conv_w', 'w_down')


def _device_step(x, mem, positions, target, w, hooks=None):
    tabs = _rope_tables(positions)
    dils = tuple(d for _, d in A_GROUPS)
    grads = {}
    w = dict(w)

    h, h_t, r1 = _rms_fwd(x, w['attn_norm'], "rms1")
    if hooks is not None:
        w.update(hooks.weights('in', [h, *tabs]))
    proj = _mm_rows([(h, w['w_in'], 0)], "mm_in")

    qkv_a, o_g, lse_g = [], [], []
    for gi, (window, d) in enumerate(A_GROUPS):
        gq, gk = _two(w['a_q_norm'][gi]), _two(w['a_k_norm'][gi])
        qkv = _qk_prep(proj, 6 * gi, d, False, gq, gk, tabs, f"qk_prep_a{gi}")
        o, lse = _band_fwd(qkv, 2 * d, window // d, None, f"band_fwd_a{gi}")
        qkv_a.append(qkv)
        o_g.append(o)
        lse_g.append(lse)
    o_a, lse_a = _merge_groups(o_g, lse_g, dils, "merge_a")
    if hooks is not None:
        w.update(hooks.weights('mix', [o_a]))

    gbq, gbk = _two(w['b_q_norm']), _two(w['b_k_norm'])
    sinks = jnp.repeat(w['b_sinks'].reshape(4, 2), HEAD_DIM, axis=1).reshape(4, 1, LANES)
    qkv_b = _qk_prep(proj, 18, 1, True, gbq, gbk, tabs, "qk_prep_b")
    o_b, lse_b = _band_fwd(qkv_b, 4, B_WINDOW - 1, sinks, "band_fwd_b")

    gates = _mm_rows([(h, w['w_gate'], 0)], "mm_gate", bias=w['b_gate'], sigmoid=True, out_dtypes=(BF16,))
    mk, mv = _mem_kv(mem, w['mem_norm'], w['w_mem_kv'], w['m_k_norm'], "mem_kv")
    o_m = _mem_attn_fwd(proj, 6, mk, mv, w['m_q_norm'], "mem_attn")

    pa = _mm_rows([(o_a, w['w_o_a'], 0)], "mm_oa", out_dtypes=(BF16,))
    pb = _mm_rows([(o_b, w['w_o_b'], 0)], "mm_ob", out_dtypes=(BF16,))
    pm = _mm_rows([(o_m, w['w_o_m'], 0)], "mm_om", out_dtypes=(BF16,))
    merged = _gate_merge(gates, pa, pb, pm, "gate_merge")
    x1 = _mm_rows([(merged, w['w_out'], 0)], "mm_out", res=x)

    if hooks is not None:
        w.update(hooks.weights('ffn', [x1]))
    h2, h2_t, r2 = _rms_fwd(x1, w['ffn_norm'], "rms2")
    u = _mm_rows([(h2, w['w_up'], 0)], "mm_up", out_dtypes=(BF16,))
    act = _conv_glu(u, w['conv_w'], w['conv_b'], "conv_glu")
    dy, dy_b, loss = _mm_rows([(act, w['w_down'], 0)], "mm_down", res=x1, loss_target=target)

    dact = _mm_rows([(dy_b, w['w_down'], 0)], "mm_d_act", nt=True, out_dtypes=(BF16,))
    grads['w_down'] = _mm_tn(act, dy_b, "mm_dw_down")
    du_a, du_g, dcw_a, dcw_g, dcb_a, dcb_g = _conv_glu_bwd(dact, u, w['conv_w'], w['conv_b'], "conv_glu_bwd")
    grads['conv_w'] = jnp.concatenate([dcw_a, dcw_g], axis=1)
    grads['conv_b'] = jnp.concatenate([dcb_a, dcb_g], axis=1)
    dh2 = _mm_rows([(du_a, w['w_up'], 0), (du_g, w['w_up'], 1)], "mm_d_h2", nt=True)
    grads['w_up'] = jnp.concatenate([_mm_cols(h2_t, du_a, "mm_dw_up_a"), _mm_cols(h2_t, du_g, "mm_dw_up_g")], axis=1)
    ffn_gain = w['ffn_norm']
    if hooks is not None:
        ffn_gain = ffn_gain + hooks.grads('ffn', grads)[0:1, 0:1]
    dx1, dx1_b, grads['ffn_norm'] = _rms_bwd(dh2, x1, r2, ffn_gain, dy, "rms2_bwd", bf16_copy=True)

    dmerged = _mm_rows([(dx1_b, w['w_out'], 0)], "mm_d_merged", nt=True)
    grads['w_out'] = _mm_tn(merged, dx1_b, "mm_dw_out")
    dpa, dpb, dpm, dgpre, grads['b_gate'] = _gate_merge_bwd(dmerged, gates, pa, pb, pm, "gate_merge_bwd")
    do_a = _mm_rows([(dpa, w['w_o_a'], 0)], "mm_d_oa", nt=True)
    do_b = _mm_rows([(dpb, w['w_o_b'], 0)], "mm_d_ob", nt=True)
    do_m = _mm_rows([(dpm, w['w_o_m'], 0)], "mm_d_om", nt=True)
    grads['w_o_a'] = _mm_tn(o_a, dpa, "mm_dw_oa")
    grads['w_o_b'] = _mm_tn(o_b, dpb, "mm_dw_ob")
    grads['w_o_m'] = _mm_tn(o_m, dpm, "mm_dw_om")
    grads['w_gate'] = _mm_cols(h_t, dgpre, "mm_dw_gate")
    dq_m, dmk, dmv, grads['m_q_norm'] = _mem_attn_bwd(proj, 6, mk, mv, w['m_q_norm'], do_m, "mem_attn_bwd")
    grads['w_mem_kv'], grads['mem_norm'], grads['m_k_norm'] = _mem_kv_bwd(
        mem, w['mem_norm'], w['w_mem_kv'], w['m_k_norm'], dmk, dmv, "mem_kv_bwd")
    a_gain = w['a_q_norm']
    if hooks is not None:
        a_gain = a_gain + hooks.grads('mix', grads)[0:1, 0:1]

    prep = _bwd_prep(do_a, o_a, lse_a, dils, None, "bwd_prep_a")
    dproj, dgq_a, dgk_a = [], [], []
    for gi, (window, d) in enumerate(A_GROUPS):
        gq, gk = _two(a_gain[gi]), _two(w['a_k_norm'][gi])
        dqkv = _band_bwd(qkv_a[gi], prep[3 * gi], prep[3 * gi + 1], prep[3 * gi + 2], 2 * d, window // d,
                         f"band_bwd_a{gi}")
        dp, dgq, dgk = _qk_prep_bwd(dqkv, proj, 6 * gi, d, False, gq, gk, tabs, f"qk_prep_bwd_a{gi}")
        dproj.append(dp)
        dgq_a.append(_fold_heads(dgq))
        dgk_a.append(_fold_heads(dgk))
    grads['a_q_norm'] = jnp.stack(dgq_a)
    grads['a_k_norm'] = jnp.stack(dgk_a)

    do_bu, lse_bu, delta_bu, dsink = _bwd_prep(do_b, o_b, lse_b, (1,), sinks, "bwd_prep_b")
    dqkv = _band_bwd(qkv_b, do_bu, lse_bu, delta_bu, 4, B_WINDOW - 1, "band_bwd_b")
    dp_b, dgq, dgk = _qk_prep_bwd(dqkv, proj, 18, 1, True, gbq, gbk, tabs, "qk_prep_bwd_b")
    dproj.append(dp_b)
    grads['b_q_norm'] = _fold_heads(dgq)
    grads['b_k_norm'] = _fold_heads(dgk)
    grads['b_sinks'] = jnp.stack([dsink[:, 0, 0], dsink[:, 0, HEAD_DIM]], axis=1).reshape(8)

    dproj.append(dq_m)

    cols = (0, 1, 2, 3, 6)
    grads['w_in'] = _mm_rows_cat(h_t, dproj, "mm_dw_in")
    attn_gain = w['attn_norm']
    if hooks is not None:
        attn_gain = attn_gain + hooks.grads('in', grads)[0:1, 0:1]
    dh = _mm_rows([(dp, w['w_in'], c) for dp, c in zip(dproj, cols)] + [(dgpre, w['w_gate'], 0)], "mm_d_h", nt=True)
    grad_x, grads['attn_norm'] = _rms_bwd(dh, x, r1, attn_gain, dx1, "rms1_bwd")
    return loss, grad_x, grads


def _coords():
    return lax.axis_index("x"), lax.axis_index("y"), lax.axis_index("c")


def _slot(p):
    return 4 * p[0] + 2 * p[1] + p[2]


ALL_PEERS = tuple(range(1, N_DEV))
CHIP_PEERS = (1, 4, 2, 6)
OTHER_CHIPS = (4, 2, 6)


def _peers(me, masks=ALL_PEERS):
    x, y, c = me
    return [(1 - x if mask & 4 else x, 1 - y if mask & 2 else y, 1 - c if mask & 1 else c) for mask in masks]


HBM_SPEC = pl.BlockSpec(memory_space=pltpu.HBM)


SEM_SPEC = pl.BlockSpec(memory_space=pltpu.SEMAPHORE)
SIDE_EFFECT = pltpu.SideEffectType.DATAFLOW_SIDE_EFFECTING


def _exchange_start(blocks, name, gather=False, masks=ALL_PEERS, after=None):
    n = len(blocks)
    n_peers = len(masks)
    n_in = 2 * n + (0 if after is None else 1)

    def body(*refs):
        ins, lands = refs[:n], refs[n:2 * n]
        send_sems, recv_sems = refs[n_in], refs[n_in + 1]
        token = refs[-1]
        me = _coords()
        peers = _peers(me, masks)
        for a in range(n):
            for k in range(n_peers):
                pltpu.make_async_remote_copy(
                    src_ref=ins[a] if gather else ins[a].at[_slot(peers[k])], dst_ref=lands[a].at[_slot(me)],
                    send_sem=send_sems.at[a * n_peers + k], recv_sem=recv_sems.at[a * n_peers + k],
                    device_id=peers[k], device_id_type=MESH).start()
        token[...] = jnp.zeros_like(token)

    land_shapes = [((N_DEV,) + b.shape) if gather else b.shape for b in blocks]
    hbm_in = [pltpu.HBM(b.shape, b.dtype) for b in blocks]
    hbm_land = [pltpu.HBM(s, b.dtype) for s, b in zip(land_shapes, blocks)]
    sems = pltpu.SemaphoreType.DMA((n * n_peers,))
    ins = [pltpu.with_memory_space_constraint(b, pltpu.HBM) for b in blocks]
    lands = [pltpu.with_memory_space_constraint(lax.empty(s, b.dtype), pltpu.HBM) for s, b in zip(land_shapes, blocks)]
    return pl.pallas_call(
        body, name=name, out_shape=(sems, sems, *hbm_in, *hbm_land, jax.ShapeDtypeStruct((8, LANES), F32)),
        in_specs=[HBM_SPEC] * (2 * n) + ([] if after is None else [pl.BlockSpec(memory_space=pl.ANY)]),
        out_specs=(SEM_SPEC, SEM_SPEC, *([HBM_SPEC] * (2 * n)), pl.BlockSpec(memory_space=pltpu.VMEM)),
        input_output_aliases={i: 2 + i for i in range(2 * n)},
        compiler_params=pltpu.CompilerParams(has_side_effects=SIDE_EFFECT),
    )(*ins, *lands, *([] if after is None else [after]))


def _exchange_wait(started, after, name, gather=False, masks=ALL_PEERS):
    n = (len(started) - 3) // 2
    n_peers = len(masks)
    send_sems, recv_sems = started[0], started[1]
    thru = started[2:2 + 2 * n]

    def body(*refs):
        ins, lands = refs[:n], refs[n:2 * n]
        send_ref, recv_ref = refs[2 * n], refs[2 * n + 1]
        me = _coords()
        peers = _peers(me, masks)
        for a in range(n):
            for k in range(n_peers):
                cp = pltpu.make_async_remote_copy(
                    src_ref=ins[a] if gather else ins[a].at[_slot(peers[k])], dst_ref=lands[a].at[_slot(peers[k])],
                    send_sem=send_ref.at[a * n_peers + k], recv_sem=recv_ref.at[a * n_peers + k],
                    device_id=peers[k], device_id_type=MESH)
                cp.wait_send()
                cp.wait_recv()

    hbm = [pltpu.HBM(t.shape, t.dtype) for t in thru]
    res = pl.pallas_call(
        body, name=name, out_shape=tuple(hbm),
        in_specs=[HBM_SPEC] * (2 * n) + [SEM_SPEC, SEM_SPEC] + [pl.BlockSpec(memory_space=pl.ANY)] * len(after),
        out_specs=tuple([HBM_SPEC] * (2 * n)), input_output_aliases={i: i for i in range(2 * n)},
        compiler_params=pltpu.CompilerParams(has_side_effects=SIDE_EFFECT),
    )(*thru, send_sems, recv_sems, *after)
    return res[n:]


def _sibling_forward(arrays, name):
    n = len(arrays)
    n_fwd = len(OTHER_CHIPS)

    def body(*refs):
        bufs = refs[n:2 * n]
        token, send_sems, recv_sems = refs[2 * n:]
        token[...] = jnp.zeros_like(token)
        x, y, c = _coords()
        sibling = (x, y, 1 - c)
        mine = _peers((x, y, c), OTHER_CHIPS)
        theirs = _peers(sibling, OTHER_CHIPS)

        def copy(a, k, block):
            rows = bufs[a].at[_slot(block)]
            return pltpu.make_async_remote_copy(
                src_ref=rows, dst_ref=rows, send_sem=send_sems.at[a * n_fwd + k], recv_sem=recv_sems.at[a * n_fwd + k],
                device_id=sibling, device_id_type=MESH)

        sends = [copy(a, k, mine[k]) for a in range(n) for k in range(n_fwd)]
        for cp in sends:
            cp.start()
        for a in range(n):
            for k in range(n_fwd):
                copy(a, k, theirs[k]).wait_recv()
        for cp in sends:
            cp.wait_send()

    res = pl.pallas_call(
        body, name=name, in_specs=[HBM_SPEC] * n,
        out_specs=tuple([HBM_SPEC] * n + [pl.BlockSpec(memory_space=pltpu.VMEM)]),
        out_shape=tuple([jax.ShapeDtypeStruct(a.shape, a.dtype) for a in arrays] + [jax.ShapeDtypeStruct((8, LANES), F32)]),
        input_output_aliases={i: i for i in range(n)},
        scratch_shapes=[pltpu.SemaphoreType.DMA((n * n_fwd,)), pltpu.SemaphoreType.DMA((n * n_fwd,))],
    )(*arrays)
    return res[:n], res[n]


def _all_sum(p, name):
    def body(p_ref, o_ref, recv, send_sems, recv_sems):
        me = _coords()
        peers = _peers(me)
        recv[_slot(me)] = p_ref[...]

        def copy(k, landing):
            return pltpu.make_async_remote_copy(
                src_ref=p_ref, dst_ref=recv.at[_slot(landing)], send_sem=send_sems.at[k], recv_sem=recv_sems.at[k],
                device_id=peers[k], device_id_type=MESH)

        sends = [copy(k, me) for k in range(N_DEV - 1)]
        for cp in sends:
            cp.start()
        for k in range(N_DEV - 1):
            copy(k, peers[k]).wait_recv()
        for cp in sends:
            cp.wait_send()
        acc = recv[0]
        for s in range(1, N_DEV):
            acc = acc + recv[s]
        o_ref[...] = acc

    vmem = pl.BlockSpec(memory_space=pltpu.VMEM)
    return pl.pallas_call(
        body, name=name, in_specs=[vmem], out_specs=vmem, out_shape=jax.ShapeDtypeStruct(p.shape, F32),
        scratch_shapes=[pltpu.VMEM((N_DEV,) + p.shape, F32), pltpu.SemaphoreType.DMA((N_DEV - 1,)),
                        pltpu.SemaphoreType.DMA((N_DEV - 1,))],
    )(p)


def _adam(w, g, m, v):
    m2 = ADAM_B1 * m + (1.0 - ADAM_B1) * g
    v2 = ADAM_B2 * v + (1.0 - ADAM_B2) * (g * g)
    m_hat = m2 / (1.0 - ADAM_B1 ** ADAM_STEP)
    v_hat = v2 / (1.0 - ADAM_B2 ** ADAM_STEP)
    delta = -ADAM_LR * (m_hat / (jnp.sqrt(v_hat) + ADAM_EPS) + ADAM_WD * w)
    return delta, m2, v2


def _row_tile(rows, cols):
    best = rows
    for t in range(16, rows, 16):
        if rows % t == 0 and t * cols * 4 <= (1 << 20):
            best = t
    return best


def _adam_reduce(parts, w, m, v, name):
    rows, cols = w.shape
    tr = _row_tile(rows, cols)

    def body(p_ref, w_ref, m_ref, v_ref, g_ref, d_ref, m2_ref, v2_ref):
        g = p_ref[0].astype(F32)
        for s in range(1, N_DEV):
            g = g + p_ref[s].astype(F32)
        g_ref[...] = g
        d_ref[...], m2_ref[...], v2_ref[...] = _adam(w_ref[...], g, m_ref[...], v_ref[...])

    blk = pl.BlockSpec((tr, cols), lambda i: (i, 0))
    shp = jax.ShapeDtypeStruct((rows, cols), F32)
    return pl.pallas_call(
        body, name=name, grid=(rows // tr,),
        in_specs=[pl.BlockSpec((N_DEV, tr, cols), lambda i: (0, i, 0)), blk, blk, blk],
        out_specs=(blk,) * 4, out_shape=(shp,) * 4, compiler_params=_cparams(1),
    )(parts, w, m, v)


PACK_COLS = 1024
PACK = {'attn_norm': (0, 1, 1024), 'mem_norm': (1, 1, 1024), 'ffn_norm': (2, 1, 1024), 'b_gate': (3, 3, 1024),
        'conv_b': (6, 6, 1024), 'a_q_norm': (12, 3, 64), 'a_k_norm': (15, 3, 64), 'b_q_norm': (18, 1, 64),
        'b_k_norm': (19, 1, 64), 'm_q_norm': (20, 1, 128), 'm_k_norm': (21, 1, 128), 'b_sinks': (22, 1, 8)}
PACK_LOSS_ROW = 23
PACK_ROWS = 24


def _pack_pieces(name, width):
    r0, nr, lanes = PACK[name]
    out = []
    for j in range(nr):
        if lanes == PACK_COLS:
            w = min(PACK_COLS, width - j * PACK_COLS)
            out.append((r0 + j, slice(0, 1), slice(j * PACK_COLS, j * PACK_COLS + w), w))
        else:
            out.append((r0 + j, slice(j, j + 1), slice(0, lanes), lanes))
    return out


def _pack_small(grads, loss_tile, name):
    names = list(PACK)

    def body(*refs):
        o_ref = refs[-1]
        o_ref[...] = jnp.zeros_like(o_ref)
        for k, nm in enumerate(names):
            for row, rs, ls, w in _pack_pieces(nm, refs[k].shape[1]):
                o_ref[row:row + 1, 0:w] = refs[k][rs, ls]
        o_ref[PACK_LOSS_ROW:PACK_LOSS_ROW + 1, 0:1] = refs[len(names)][0:1, 0:1]

    vmem = pl.BlockSpec(memory_space=pltpu.VMEM)
    args = [grads[nm] for nm in names] + [loss_tile]
    return pl.pallas_call(body, name=name, in_specs=[vmem] * len(args), out_specs=vmem,
                          out_shape=jax.ShapeDtypeStruct((PACK_ROWS, PACK_COLS), F32))(*args)


def _adam_small(gsum, ws, ms, vs, name):
    names = list(PACK)
    n = len(names)

    def body(*refs):
        g_ref = refs[0]
        w_refs, m_refs, v_refs = refs[1:1 + n], refs[1 + n:1 + 2 * n], refs[1 + 2 * n:1 + 3 * n]
        outs = refs[1 + 3 * n:]
        outs[0][...] = g_ref[PACK_LOSS_ROW:PACK_LOSS_ROW + 1, 0:1]
        for k, nm in enumerate(names):
            o_g, o_d, o_m, o_v = outs[1 + 4 * k:5 + 4 * k]
            for row, rs, ls, width in _pack_pieces(nm, w_refs[k].shape[1]):
                src = (rs, ls)
                g = g_ref[row:row + 1, 0:width]
                d, m2, v2 = _adam(w_refs[k][src], g, m_refs[k][src], v_refs[k][src])
                o_g[src] = g
                o_d[src] = d
                o_m[src] = m2
                o_v[src] = v2

    vmem = pl.BlockSpec(memory_space=pltpu.VMEM)
    shapes = [jax.ShapeDtypeStruct((1, 1), F32)]
    for nm in names:
        shapes += [jax.ShapeDtypeStruct(ws[nm].shape, F32)] * 4
    args = [gsum] + [ws[nm] for nm in names] + [ms[nm] for nm in names] + [vs[nm] for nm in names]
    return pl.pallas_call(
        body, name=name, in_specs=[vmem] * len(args), out_specs=tuple([vmem] * len(shapes)), out_shape=tuple(shapes),
    )(*args)


def _as2d(name, a):
    return a.reshape(a.shape[-2], a.shape[-1]) if a.ndim == 3 else a


def kernel(x, mem, positions, attn_norm, w_in, a_q_norm, a_k_norm, b_q_norm, b_k_norm, b_sinks, mem_norm, w_mem_kv, m_q_norm, m_k_norm, w_o_a, w_o_b, w_o_m, w_gate, b_gate, w_out, ffn_norm, w_up, conv_w, conv_b, w_down, loss_target, m_attn_norm, m_w_in, m_a_q_norm, m_a_k_norm, m_b_q_norm, m_b_k_norm, m_b_sinks, m_mem_norm, m_w_mem_kv, m_m_q_norm, m_m_k_norm, m_w_o_a, m_w_o_b, m_w_o_m, m_w_gate, m_b_gate, m_w_out, m_ffn_norm, m_w_up, m_conv_w, m_conv_b, m_w_down, v_attn_norm, v_w_in, v_a_q_norm, v_a_k_norm, v_b_q_norm, v_b_k_norm, v_b_sinks, v_mem_norm, v_w_mem_kv, v_m_q_norm, v_m_k_norm, v_w_o_a, v_w_o_b, v_w_o_m, v_w_gate, v_b_gate, v_w_out, v_ffn_norm, v_w_up, v_conv_w, v_conv_b, v_w_down):
    given = dict(attn_norm=attn_norm, w_in=w_in, a_q_norm=a_q_norm, a_k_norm=a_k_norm, b_q_norm=b_q_norm, b_k_norm=b_k_norm, b_sinks=b_sinks, mem_norm=mem_norm, w_mem_kv=w_mem_kv, m_q_norm=m_q_norm, m_k_norm=m_k_norm, w_o_a=w_o_a, w_o_b=w_o_b, w_o_m=w_o_m, w_gate=w_gate, b_gate=b_gate, w_out=w_out, ffn_norm=ffn_norm, w_up=w_up, conv_w=conv_w, conv_b=conv_b, w_down=w_down)
    mom1 = dict(attn_norm=m_attn_norm, w_in=m_w_in, a_q_norm=m_a_q_norm, a_k_norm=m_a_k_norm, b_q_norm=m_b_q_norm, b_k_norm=m_b_k_norm, b_sinks=m_b_sinks, mem_norm=m_mem_norm, w_mem_kv=m_w_mem_kv, m_q_norm=m_m_q_norm, m_k_norm=m_m_k_norm, w_o_a=m_w_o_a, w_o_b=m_w_o_b, w_o_m=m_w_o_m, w_gate=m_w_gate, b_gate=m_b_gate, w_out=m_w_out, ffn_norm=m_ffn_norm, w_up=m_w_up, conv_w=m_conv_w, conv_b=m_conv_b, w_down=m_w_down)
    mom2 = dict(attn_norm=v_attn_norm, w_in=v_w_in, a_q_norm=v_a_q_norm, a_k_norm=v_a_k_norm, b_q_norm=v_b_q_norm, b_k_norm=v_b_k_norm, b_sinks=v_b_sinks, mem_norm=v_mem_norm, w_mem_kv=v_w_mem_kv, m_q_norm=v_m_q_norm, m_k_norm=v_m_k_norm, w_o_a=v_w_o_a, w_o_b=v_w_o_b, w_o_m=v_w_o_m, w_gate=v_w_gate, b_gate=v_b_gate, w_out=v_w_out, ffn_norm=v_ffn_norm, w_up=v_w_up, conv_w=v_conv_w, conv_b=v_conv_b, w_down=v_w_down)

    big = list(BIG)
    stages = {'mix': list(MIX_WEIGHTS), 'ffn': list(FFN_WEIGHTS), 'in': ['w_in']}
    my_slot = _slot(_coords())

    def shard(n):
        return given[n][0] if n == 'conv_w' else given[n][0].astype(BF16)

    def whole(n, g):
        _, r, c = g.shape
        return g.reshape(N_DEV * r, c) if BIG[n] == 0 else g.transpose(1, 0, 2).reshape(r, N_DEV * c)

    def to_blocks(n, g):
        r, c = given[n].shape[1:]
        g = g.reshape(N_DEV, r, c) if BIG[n] == 0 else g.reshape(r, N_DEV, c).transpose(1, 0, 2)
        return g if n == 'conv_w' else g.astype(BF16)

    class Hooks:
        next_stage = {'in': 'mix', 'mix': 'ffn'}

        def __init__(self):
            self.coming, self.sent = {}, {}
            self.shards = {n: shard(n) for n in big}
            self.start_gather('in', None)

        def start_gather(self, stage, after):
            src = [self.shards[n] for n in stages[stage]]
            self.coming[stage] = _exchange_start(src, f"gather_{stage}_start", gather=True, masks=CHIP_PEERS,
                                                 after=after)

        def weights(self, stage, after):
            names = stages[stage]
            after = list(after)
            if stage == 'in':
                after += [self.shards[n] for n in stages['mix'] + stages['ffn']]
            landed = _exchange_wait(self.coming[stage], after, f"gather_{stage}_wait", gather=True, masks=CHIP_PEERS)
            landed, token = _sibling_forward(landed, f"gather_{stage}_forward")
            if stage in self.next_stage:
                self.start_gather(self.next_stage[stage], token)
            return {n: whole(n, lax.dynamic_update_slice_in_dim(land, self.shards[n][None], my_slot, axis=0))
                    for n, land in zip(names, landed)}

        def grads(self, stage, g):
            blocks = [to_blocks(n, g[n]) for n in stages[stage]]
            own = [lax.dynamic_slice_in_dim(b, my_slot, 1, axis=0) for b in blocks]
            self.sent[stage] = (_exchange_start(blocks, f"exchange_{stage}_start"), own)
            return self.sent[stage][0][-1]

        def parts(self, stage, after):
            started, own = self.sent[stage]
            landed = _exchange_wait(started, [after], f"exchange_{stage}_wait")
            return {n: lax.dynamic_update_slice_in_dim(land, o, my_slot, axis=0)
                    for n, land, o in zip(stages[stage], landed, own)}

    hooks = Hooks()
    w = {}
    for n in SMALL:
        w[n] = given[n]
    w['a_q_norm'], w['a_k_norm'] = given['a_q_norm'][0], given['a_k_norm'][0]
    w['b_q_norm'], w['b_k_norm'], w['b_sinks'] = given['b_q_norm'][0], given['b_k_norm'][0], given['b_sinks'][0]

    loss_tile, grad_x, grads = _device_step(x[0], mem[0], positions[0], loss_target[0], w, hooks)
    out = {}
    after = grad_x
    for stage in ('ffn', 'mix', 'in'):
        for n, p in hooks.parts(stage, after).items():
            res = _adam_reduce(p, given[n][0], mom1[n][0], mom2[n][0], f"adam_{n}")
            out[n] = tuple(t[None] for t in res)
            after = res[0]

    small = {n: grads[n] for n in PACK}
    small['b_q_norm'], small['b_k_norm'] = grads['b_q_norm'].reshape(1, -1), grads['b_k_norm'].reshape(1, -1)
    small['b_sinks'] = grads['b_sinks'].reshape(1, -1)
    gsum = _all_sum(_pack_small(small, loss_tile, "pack_small"), "sum_small")
    ws = {n: _as2d(n, given[n]) for n in PACK}
    ms = {n: _as2d(n, mom1[n]) for n in PACK}
    vs = {n: _as2d(n, mom2[n]) for n in PACK}
    res = _adam_small(gsum, ws, ms, vs, "adam_small")
    loss = res[0].reshape(())
    for k, n in enumerate(PACK):
        out[n] = tuple(t.reshape(given[n].shape) for t in res[1 + 4 * k:5 + 4 * k])

    outs = [loss, grad_x[None]]
    for field in range(4):
        outs += [out[n][field] for n in WEIGHTS]
    return tuple(outs)
```

```python
import functools
import math

import jax
import jax.numpy as jnp
from jax import lax
from jax.experimental import pallas as pl
from jax.experimental.pallas import tpu as pltpu

F32 = jnp.float32
BF16 = jnp.bfloat16

N_DEV = 8
D_MODEL = 1024
HEAD_DIM = 64
A_GROUPS = ((128, 1), (512, 4), (2048, 16))
B_WINDOW = 128
M_HEADS = 4
M_HEAD_DIM = 128
MEM_LEN = 256
D_FF = 2816
ROPE_THETA = 500000.0
ROPE_DIMS = 16
BLOCK = 128
EPS = 1e-6
LANES = 128
BAND_Q_BLOCKS = 4
BAND_UNITS = 2

ADAM_LR = 0.001
ADAM_B1 = 0.9
ADAM_B2 = 0.999
ADAM_EPS = 1e-08
ADAM_WD = 0.01
ADAM_STEP = 10

VMEM_LIMIT_BYTES = 56 * 1024 * 1024
MESH = pl.DeviceIdType.MESH

WEIGHTS = ['attn_norm', 'w_in', 'a_q_norm', 'a_k_norm', 'b_q_norm', 'b_k_norm', 'b_sinks', 'mem_norm',
           'w_mem_kv', 'm_q_norm', 'm_k_norm', 'w_o_a', 'w_o_b', 'w_o_m', 'w_gate', 'b_gate', 'w_out',
           'ffn_norm', 'w_up', 'conv_w', 'conv_b', 'w_down']
BIG = {'w_in': 1, 'w_mem_kv': 0, 'w_o_a': 1, 'w_o_b': 1, 'w_o_m': 1, 'w_gate': 1, 'w_out': 0, 'w_up': 1,
       'conv_w': 1, 'w_down': 0}
SMALL = [n for n in WEIGHTS if n not in BIG]


def _cparams(n_grid):
    return pltpu.CompilerParams(dimension_semantics=("arbitrary",) * n_grid, vmem_limit_bytes=VMEM_LIMIT_BYTES)


def _seg_matrix(width):
    shift = width.bit_length() - 1
    r = lax.shift_right_logical(lax.broadcasted_iota(jnp.int32, (LANES, LANES), 0), shift)
    c = lax.shift_right_logical(lax.broadcasted_iota(jnp.int32, (LANES, LANES), 1), shift)
    return jnp.where(r == c, 1.0, 0.0).astype(BF16)


def _seg_sum(x, seg):
    hi = x.astype(BF16)
    r1 = x - hi.astype(F32)
    mid = r1.astype(BF16)
    lo = (r1 - mid.astype(F32)).astype(BF16)
    dot = functools.partial(jnp.dot, preferred_element_type=F32)
    return dot(hi, seg) + dot(mid, seg) + dot(lo, seg)


def _rope(y, c, s1, s2):
    return y * c + pltpu.roll(y, LANES - ROPE_DIMS // 2, 1) * s1 + pltpu.roll(y, ROPE_DIMS // 2, 1) * s2


def _unrope(dy, c, s1, s2):
    return dy * c + pltpu.roll(dy * s1, ROPE_DIMS // 2, 1) + pltpu.roll(dy * s2, LANES - ROPE_DIMS // 2, 1)


def _sigmoid(x):
    return 1.0 / (1.0 + jnp.exp(-x))


def _rms_fwd(x, gain, name):
    s_len, d = x.shape
    tm = 512

    def body(x_ref, g_ref, h_ref, ht_ref, r_ref):
        xv = x_ref[...]
        r = lax.rsqrt(jnp.mean(xv * xv, axis=-1, keepdims=True) + EPS)
        h = ((xv * r) * g_ref[...]).astype(BF16)
        h_ref[...] = h
        ht_ref[...] = h.T
        r_ref[...] = r

    return pl.pallas_call(
        body, name=name, grid=(s_len // tm,),
        in_specs=[pl.BlockSpec((tm, d), lambda i: (i, 0)), pl.BlockSpec((1, d), lambda i: (0, 0))],
        out_specs=(pl.BlockSpec((tm, d), lambda i: (i, 0)), pl.BlockSpec((d, tm), lambda i: (0, i)),
                   pl.BlockSpec((tm, 1), lambda i: (i, 0))),
        out_shape=(jax.ShapeDtypeStruct((s_len, d), BF16), jax.ShapeDtypeStruct((d, s_len), BF16),
                   jax.ShapeDtypeStruct((s_len, 1), F32)),
        compiler_params=_cparams(1),
    )(x, gain)


def _resident(shape, index_map):
    return pl.BlockSpec(shape, index_map, pipeline_mode=pl.Buffered(1))


def _mm_rows(pairs, name, nt=False, tm=512, bias=None, sigmoid=False, res=None, out_dtypes=(F32,), loss_target=None,
             rms_bwd=None):
    m = pairs[0][0].shape[0]
    n = pairs[0][1].shape[0] if nt else pairs[0][1].shape[1]
    n_pairs = len(pairs)
    has_bias, has_res, has_loss = bias is not None, res is not None, loss_target is not None
    has_rms = rms_bwd is not None
    dims = (((1,), (1,)), ((), ())) if nt else (((1,), (0,)), ((), ()))

    def body(*refs):
        acc = None
        for p in range(n_pairs):
            t = lax.dot_general(refs[2 * p][...].astype(BF16), refs[2 * p + 1][...], dims, preferred_element_type=F32)
            acc = t if acc is None else acc + t
        pos = 2 * n_pairs
        if has_bias:
            acc = acc + refs[pos][...]
            pos += 1
        if sigmoid:
            acc = _sigmoid(acc)
        if has_res:
            acc = refs[pos][...] + acc
            pos += 1
        if has_loss:
            dy_ref, dyb_ref, l_ref = refs[pos + 1:]

            @pl.when(pl.program_id(0) == 0)
            def _():
                l_ref[...] = jnp.zeros_like(l_ref)
            err = acc - refs[pos][...]
            dy = err * (1.0 / n)
            dy_ref[...] = dy
            dyb_ref[...] = dy.astype(BF16)
            part = 0.5 * jnp.sum(jnp.mean(err * err, axis=-1, keepdims=True), axis=0, keepdims=True)
            l_ref[...] += jnp.broadcast_to(part, l_ref.shape)
            return
        if has_rms:
            x_ref, r_ref, g_ref, add_ref = refs[pos:pos + 4]
            dg_ref = refs[-1]

            @pl.when(pl.program_id(0) == 0)
            def _():
                dg_ref[...] = jnp.zeros_like(dg_ref)
            rv = r_ref[...]
            xhat = x_ref[...] * rv
            dg_ref[...] += jnp.sum(acc * xhat, axis=0, keepdims=True)
            dxhat = acc * g_ref[...]
            acc = add_ref[...] + rv * (dxhat - xhat * jnp.mean(dxhat * xhat, axis=-1, keepdims=True))
            for o_ref in refs[pos + 4:-1]:
                o_ref[...] = acc.astype(o_ref.dtype)
            return
        for o_ref in refs[pos:]:
            o_ref[...] = acc.astype(o_ref.dtype)

    in_specs, args = [], []
    for a, w, blk in pairs:
        k = a.shape[1]
        in_specs.append(pl.BlockSpec((tm, k), lambda i: (i, 0)))
        if nt:
            in_specs.append(_resident((n, k), lambda i, blk=blk: (0, blk)))
        else:
            in_specs.append(_resident((k, n), lambda i, blk=blk: (blk, 0)))
        args += [a, w]
    if has_bias:
        in_specs.append(_resident((1, n), lambda i: (0, 0)))
        args.append(bias)
    if has_res:
        in_specs.append(pl.BlockSpec((tm, n), lambda i: (i, 0)))
        args.append(res)
    out = pl.BlockSpec((tm, n), lambda i: (i, 0))
    if has_loss:
        return pl.pallas_call(
            body, name=name, grid=(m // tm,), in_specs=in_specs + [out],
            out_specs=(out, out, pl.BlockSpec((8, LANES), lambda i: (0, 0))),
            out_shape=(jax.ShapeDtypeStruct((m, n), F32), jax.ShapeDtypeStruct((m, n), BF16),
                       jax.ShapeDtypeStruct((8, LANES), F32)),
            compiler_params=_cparams(1),
        )(*args, loss_target)
    if has_rms:
        x, r, gain, add = rms_bwd
        vec = _resident((1, n), lambda i: (0, 0))
        return pl.pallas_call(
            body, name=name, grid=(m // tm,),
            in_specs=in_specs + [out, pl.BlockSpec((tm, 1), lambda i: (i, 0)), vec, out],
            out_specs=tuple([out] * len(out_dtypes) + [pl.BlockSpec((1, n), lambda i: (0, 0))]),
            out_shape=tuple([jax.ShapeDtypeStruct((m, n), dt) for dt in out_dtypes] + [jax.ShapeDtypeStruct((1, n), F32)]),
            compiler_params=_cparams(1),
        )(*args, x, r, gain, add)
    outs = pl.pallas_call(
        body, name=name, grid=(m // tm,), in_specs=in_specs, out_specs=tuple([out] * len(out_dtypes)),
        out_shape=tuple(jax.ShapeDtypeStruct((m, n), dt) for dt in out_dtypes), compiler_params=_cparams(1),
    )(*args)
    return outs[0] if len(out_dtypes) == 1 else outs


def _mm_rows_cat(a, ws, name, tm=256):
    m, k = a.shape
    widths = [w.shape[1] for w in ws]
    n = sum(widths)

    def body(*refs):
        a_ref, o_ref = refs[0], refs[-1]
        av = a_ref[...]
        off = 0
        for p, width in enumerate(widths):
            o_ref[:, off:off + width] = jnp.dot(av, refs[1 + p][...], preferred_element_type=F32)
            off += width

    return pl.pallas_call(
        body, name=name, grid=(m // tm,),
        in_specs=[pl.BlockSpec((tm, k), lambda i: (i, 0))] + [_resident((k, wd), lambda i: (0, 0)) for wd in widths],
        out_specs=pl.BlockSpec((tm, n), lambda i: (i, 0)),
        out_shape=jax.ShapeDtypeStruct((m, n), F32), compiler_params=_cparams(1),
    )(a, *ws)


def _mm_cols(a, b, name, tn=256):
    m, k = a.shape
    n = b.shape[1]

    def body(a_ref, b_ref, o_ref):
        o_ref[...] = jnp.dot(a_ref[...], b_ref[...].astype(BF16), preferred_element_type=F32)

    return pl.pallas_call(
        body, name=name, grid=(n // tn,),
        in_specs=[_resident((m, k), lambda j: (0, 0)), pl.BlockSpec((k, tn), lambda j: (0, j))],
        out_specs=pl.BlockSpec((m, tn), lambda j: (0, j)),
        out_shape=jax.ShapeDtypeStruct((m, n), F32), compiler_params=_cparams(1),
    )(a, b)


def _mm_tn(a, b, name, tile=256):
    k, m = a.shape
    n = b.shape[1]
    dims = (((0,), (0,)), ((), ()))

    def body(a_ref, b_ref, o_ref):
        o_ref[...] = lax.dot_general(a_ref[...].astype(BF16), b_ref[...].astype(BF16), dims, preferred_element_type=F32)

    if n <= m:
        t = min(tile, m)
        grid, a_spec, b_spec = (m // t,), pl.BlockSpec((k, t), lambda i: (0, i)), _resident((k, n), lambda i: (0, 0))
        o_spec = pl.BlockSpec((t, n), lambda i: (i, 0))
    else:
        t = min(tile, n)
        grid, a_spec, b_spec = (n // t,), _resident((k, m), lambda i: (0, 0)), pl.BlockSpec((k, t), lambda i: (0, i))
        o_spec = pl.BlockSpec((m, t), lambda i: (0, i))
    return pl.pallas_call(
        body, name=name, grid=grid, in_specs=[a_spec, b_spec], out_specs=o_spec,
        out_shape=jax.ShapeDtypeStruct((m, n), F32), compiler_params=_cparams(1),
    )(a, b)


def _norm_rope(t, gain, c, s1, s2, seg):
    rs = lax.rsqrt(_seg_sum(t * t, seg) * (1.0 / HEAD_DIM) + EPS)
    return _rope((t * rs) * gain, c, s1, s2)


def _dup_half(y, half):
    lane = lax.broadcasted_iota(jnp.int32, y.shape, 1)
    rolled = pltpu.roll(y, HEAD_DIM, 1)
    keep = (lane < HEAD_DIM) if half == 0 else (lane >= HEAD_DIM)
    return jnp.where(keep, y, rolled)


def _qk_prep(proj, cb0, d, gqa, gq, gk, tabs, name):
    s_len = proj.shape[0]
    tm = 512
    rows = tm // d
    n_units = 4 if gqa else 2 * d
    n_q = 4 if gqa else 2
    n_in = 6

    def body(*refs):
        in_refs = refs[:n_in]
        gq_ref, gk_ref, c_ref, s1_ref, s2_ref, o_ref = refs[n_in:]
        seg = _seg_matrix(HEAD_DIM)

        def rows_of(ref, r):
            return ref[...] if d == 1 else ref[pl.ds(r, rows, stride=d), :]

        def put(unit_col, y):
            o_ref[:, unit_col * LANES:(unit_col + 1) * LANES] = y.astype(BF16)

        for r in range(d):
            c, s1, s2 = rows_of(c_ref, r), rows_of(s1_ref, r), rows_of(s2_ref, r)
            for b in range(n_in):
                t = rows_of(in_refs[b], r)
                if b < n_q:
                    put((b * d + r) if not gqa else b, _norm_rope(t, gq_ref[...], c, s1, s2, seg))
                elif not gqa:
                    sec, pair = (1, b - 2) if b < 4 else (2, b - 4)
                    y = _norm_rope(t, gk_ref[...], c, s1, s2, seg) if sec == 1 else t
                    put(sec * n_units + pair * d + r, y)
                else:
                    sec = 1 if b == 4 else 2
                    y = _norm_rope(t, gk_ref[...], c, s1, s2, seg) if sec == 1 else t
                    for u in range(n_units):
                        put(sec * n_units + u, _dup_half(y, u // 2))

    in_specs = [pl.BlockSpec((tm, LANES), lambda i, b=b: (i, cb0 + b)) for b in range(n_in)]
    vec = pl.BlockSpec((1, LANES), lambda i: (0, 0))
    tab = pl.BlockSpec((tm, LANES), lambda i: (i, 0))
    width = 3 * n_units * LANES
    return pl.pallas_call(
        body, name=name, grid=(s_len // tm,), in_specs=in_specs + [vec, vec, tab, tab, tab],
        out_specs=pl.BlockSpec((rows, width), lambda i: (i, 0)),
        out_shape=jax.ShapeDtypeStruct((s_len // d, width), BF16), compiler_params=_cparams(1),
    )(*([proj] * n_in), gq, gk, *tabs)


def _qk_prep_bwd(dqkv, proj, cb0, d, gqa, gq, gk, tabs, name):
    s_len = proj.shape[0]
    tm = 512
    rows = tm // d
    n_units = 4 if gqa else 2 * d
    n_q = 4 if gqa else 2
    n_in = 6

    def body(*refs):
        d_refs = refs[0:3]
        in_refs = refs[3:3 + n_in]
        gq_ref, gk_ref, c_ref, s1_ref, s2_ref, o_ref, dgq_ref, dgk_ref, stage = refs[3 + n_in:]
        seg = _seg_matrix(HEAD_DIM)

        @pl.when(pl.program_id(0) == 0)
        def _():
            dgq_ref[...] = jnp.zeros_like(dgq_ref)
            dgk_ref[...] = jnp.zeros_like(dgk_ref)

        def rows_of(ref, r):
            return ref[...] if d == 1 else ref[pl.ds(r, rows, stride=d), :]

        def unit(col):
            sec, u = divmod(col, n_units)
            return d_refs[sec][:, u * LANES:(u + 1) * LANES]

        def norm_bwd(dyr, t, gain, c, s1, s2, dg_ref):
            rs = lax.rsqrt(_seg_sum(t * t, seg) * (1.0 / HEAD_DIM) + EPS)
            that = t * rs
            dy = _unrope(dyr, c, s1, s2)
            dg_ref[...] += jnp.sum(dy * that, axis=0, keepdims=True)
            dthat = dy * gain
            return rs * (dthat - that * (_seg_sum(dthat * that, seg) * (1.0 / HEAD_DIM)))

        def fold(sec):
            tot = []
            for u in range(n_units):
                v = unit(sec * n_units + u)
                tot.append(v + pltpu.roll(v, HEAD_DIM, 1))
            lane = lax.broadcasted_iota(jnp.int32, tot[0].shape, 1)
            return jnp.where(lane < HEAD_DIM, tot[0] + tot[1], tot[2] + tot[3])

        for b in range(n_in):
            for r in range(d):
                c, s1, s2 = rows_of(c_ref, r), rows_of(s1_ref, r), rows_of(s2_ref, r)
                t = rows_of(in_refs[b], r)
                if b < n_q:
                    g = unit((b * d + r) if not gqa else b)
                    out = norm_bwd(g, t, gq_ref[...], c, s1, s2, dgq_ref)
                elif not gqa:
                    sec, pair = (1, b - 2) if b < 4 else (2, b - 4)
                    g = unit(sec * n_units + pair * d + r)
                    out = norm_bwd(g, t, gk_ref[...], c, s1, s2, dgk_ref) if sec == 1 else g
                else:
                    sec = 1 if b == 4 else 2
                    g = fold(sec)
                    out = norm_bwd(g, t, gk_ref[...], c, s1, s2, dgk_ref) if sec == 1 else g
                if d == 1:
                    o_ref[:, b * LANES:(b + 1) * LANES] = out.astype(BF16)
                else:
                    stage[pl.ds(r, rows, stride=d), :] = out
            if d != 1:
                o_ref[:, b * LANES:(b + 1) * LANES] = stage[...].astype(BF16)

    in_specs = [pl.BlockSpec((rows, n_units * LANES), lambda i: (i, 0))] * 3
    in_specs += [pl.BlockSpec((tm, LANES), lambda i, b=b: (i, cb0 + b)) for b in range(n_in)]
    vec = pl.BlockSpec((1, LANES), lambda i: (0, 0))
    tab = pl.BlockSpec((tm, LANES), lambda i: (i, 0))
    return pl.pallas_call(
        body, name=name, grid=(s_len // tm,), in_specs=in_specs + [vec, vec, tab, tab, tab],
        out_specs=(pl.BlockSpec((tm, n_in * LANES), lambda i: (i, 0)), vec, vec),
        out_shape=(jax.ShapeDtypeStruct((s_len, n_in * LANES), BF16), jax.ShapeDtypeStruct((1, LANES), F32),
                   jax.ShapeDtypeStruct((1, LANES), F32)),
        scratch_shapes=[pltpu.VMEM((tm, LANES), F32)], compiler_params=_cparams(1),
    )(*dqkv, *([proj] * n_in), gq, gk, *tabs)


def _head_masks(shape):
    lane = lax.broadcasted_iota(jnp.int32, shape, 1)
    return lane < HEAD_DIM, lane >= HEAD_DIM


def _band_fwd(qkv, n_units, max_dist, sinks, name):
    n_rows = qkv.shape[0]
    nb = n_rows // BLOCK
    scale = HEAD_DIM ** -0.5
    has_sink = sinks is not None
    assert not has_sink or max_dist < BLOCK

    qn, un = min(nb, BAND_Q_BLOCKS), BAND_UNITS
    ug = n_units // un

    def body(*refs):
        q_ref, kp_ref, km_ref, vp_ref, vm_ref = refs[:5]
        o_ref, lse_ref = refs[-2:]
        i = pl.program_id(1)
        qi = lax.broadcasted_iota(jnp.int32, (BLOCK, 2 * BLOCK), 0)
        kj = lax.broadcasted_iota(jnp.int32, (BLOCK, 2 * BLOCK), 1)
        dist = qi + BLOCK - kj
        band = (dist >= 0) & (dist <= max_dist)
        band_first = band & ((i > 0) | (kj >= BLOCK))
        m0, m1 = _head_masks((BLOCK, LANES))
        zero = jnp.zeros((BLOCK, LANES), BF16)
        for ub in range(un):
            cs = slice(ub * LANES, (ub + 1) * LANES)
            for qb in range(qn):
                rs = slice(qb * BLOCK, (qb + 1) * BLOCK)
                q = q_ref[rs, cs]
                if qb == 0:
                    kk = jnp.concatenate([kp_ref[:, cs], km_ref[0:BLOCK, cs]], axis=0)
                    vv = jnp.concatenate([vp_ref[:, cs], vm_ref[0:BLOCK, cs]], axis=0)
                    valid = band_first
                else:
                    kk = km_ref[(qb - 1) * BLOCK:(qb + 1) * BLOCK, cs]
                    vv = vm_ref[(qb - 1) * BLOCK:(qb + 1) * BLOCK, cs]
                    valid = band
                outs, lses = [], []
                for e, hm in enumerate((m0, m1)):
                    qe = jnp.where(hm, q, zero)
                    s = lax.dot_general(qe, kk, (((1,), (1,)), ((), ())), preferred_element_type=F32) * scale
                    s = jnp.where(valid, s, -jnp.inf)
                    if has_sink:
                        s = jnp.where(kj == 0, refs[5][ub][:, e * HEAD_DIM:e * HEAD_DIM + 1], s)
                    mx = jnp.max(s, axis=-1, keepdims=True)
                    p = jnp.exp(s - mx)
                    den = jnp.sum(p, axis=-1, keepdims=True)
                    pn = p * (1.0 / den)
                    if has_sink:
                        pn = jnp.where(kj == 0, 0.0, pn)
                    pn = pn.astype(BF16)
                    outs.append(jnp.dot(pn, vv, preferred_element_type=F32))
                    lses.append(mx + jnp.log(den))
                o_ref[rs, cs] = jnp.where(m0, outs[0], outs[1])
                lse_ref[rs, cs] = jnp.where(m0, jnp.broadcast_to(lses[0], (BLOCK, LANES)),
                                            jnp.broadcast_to(lses[1], (BLOCK, LANES)))

    def main(sec):
        return pl.BlockSpec((qn * BLOCK, un * LANES), lambda u, i: (i, sec * ug + u))

    def prev(sec):
        return pl.BlockSpec((BLOCK, un * LANES), lambda u, i: (jnp.maximum(i * qn - 1, 0), sec * ug + u))

    in_specs = [main(0), prev(1), main(1), prev(2), main(2)]
    args = [qkv] * 5
    if has_sink:
        in_specs.append(pl.BlockSpec((un, 1, LANES), lambda u, i: (u, 0, 0)))
        args.append(sinks)
    return pl.pallas_call(
        body, name=name, grid=(ug, nb // qn), in_specs=in_specs, out_specs=(main(0), main(0)),
        out_shape=(jax.ShapeDtypeStruct((n_rows, n_units * LANES), F32),) * 2, compiler_params=_cparams(2),
    )(*args)


def _band_bwd(qkv, do, lse, delta, n_units, max_dist, name):
    n_rows = qkv.shape[0]
    nb = n_rows // BLOCK
    scale = HEAD_DIM ** -0.5

    qn, un = min(nb, BAND_Q_BLOCKS), BAND_UNITS
    ug = n_units // un
    steps = nb // qn
    nt_dims = (((1,), (1,)), ((), ()))
    tn_dims = (((0,), (0,)), ((), ()))

    def body(qm_ref, qx_ref, kp_ref, km_ref, vp_ref, vm_ref, dom_ref, dox_ref, lm_ref, lx_ref, dm_ref, dx_ref,
             dq_ref, dk_ref, dv_ref):
        i = pl.program_id(1)
        m0, m1 = _head_masks((BLOCK, LANES))
        zero = jnp.zeros((BLOCK, LANES), BF16)
        qi = lax.broadcasted_iota(jnp.int32, (BLOCK, 2 * BLOCK), 0)
        kj = lax.broadcasted_iota(jnp.int32, (BLOCK, 2 * BLOCK), 1)
        dist = qi + BLOCK - kj
        band = (dist >= 0) & (dist <= max_dist)
        band_first = band & ((i > 0) | (kj >= BLOCK))
        qr = lax.broadcasted_iota(jnp.int32, (BLOCK, BLOCK), 0)
        kc = lax.broadcasted_iota(jnp.int32, (BLOCK, BLOCK), 1)
        dist_x = qr + BLOCK - kc
        band_next = (dist_x >= 0) & (dist_x <= max_dist) & (i < steps - 1)

        def pair(q, dob, lse_b, del_b, kk, vv, valid):
            dqs, dk, dv = [], None, None
            for e, hm in enumerate((m0, m1)):
                col = slice(e * HEAD_DIM, e * HEAD_DIM + 1)
                qe = jnp.where(hm, q, zero)
                doe = jnp.where(hm, dob, zero)
                s = lax.dot_general(qe, kk, nt_dims, preferred_element_type=F32) * scale
                p = jnp.where(valid, jnp.exp(s - lse_b[:, col]), 0.0)
                dp = lax.dot_general(doe, vv, nt_dims, preferred_element_type=F32)
                ds = (p * (dp - del_b[:, col]) * scale).astype(BF16)
                dqs.append(jnp.dot(ds, kk, preferred_element_type=F32))
                dk_e = lax.dot_general(ds, qe, tn_dims, preferred_element_type=F32)
                dv_e = lax.dot_general(p.astype(BF16), doe, tn_dims, preferred_element_type=F32)
                dk = dk_e if dk is None else dk + dk_e
                dv = dv_e if dv is None else dv + dv_e
            return jnp.where(m0, dqs[0], dqs[1]), dk, dv

        for ub in range(un):
            cs = slice(ub * LANES, (ub + 1) * LANES)
            dk_acc, dv_acc = [None] * qn, [None] * qn

            def add(acc, kb, part):
                acc[kb] = part if acc[kb] is None else acc[kb] + part

            for qb in range(qn):
                rs = slice(qb * BLOCK, (qb + 1) * BLOCK)
                if qb == 0:
                    kk = jnp.concatenate([kp_ref[:, cs], km_ref[0:BLOCK, cs]], axis=0)
                    vv = jnp.concatenate([vp_ref[:, cs], vm_ref[0:BLOCK, cs]], axis=0)
                    valid = band_first
                else:
                    kk = km_ref[(qb - 1) * BLOCK:(qb + 1) * BLOCK, cs]
                    vv = vm_ref[(qb - 1) * BLOCK:(qb + 1) * BLOCK, cs]
                    valid = band
                dq, dk, dv = pair(qm_ref[rs, cs], dom_ref[rs, cs], lm_ref[rs, cs], dm_ref[rs, cs], kk, vv, valid)
                dq_ref[rs, cs] = dq
                if qb > 0:
                    add(dk_acc, qb - 1, dk[0:BLOCK])
                    add(dv_acc, qb - 1, dv[0:BLOCK])
                add(dk_acc, qb, dk[BLOCK:2 * BLOCK])
                add(dv_acc, qb, dv[BLOCK:2 * BLOCK])
            last = slice((qn - 1) * BLOCK, qn * BLOCK)
            _, dk, dv = pair(qx_ref[:, cs], dox_ref[:, cs], lx_ref[:, cs], dx_ref[:, cs], km_ref[last, cs], vm_ref[last, cs],
                             band_next)
            add(dk_acc, qn - 1, dk)
            add(dv_acc, qn - 1, dv)
            for kb in range(qn):
                dk_ref[kb * BLOCK:(kb + 1) * BLOCK, cs] = dk_acc[kb]
                dv_ref[kb * BLOCK:(kb + 1) * BLOCK, cs] = dv_acc[kb]

    def main(sec):
        return pl.BlockSpec((qn * BLOCK, un * LANES), lambda u, i: (i, sec * ug + u))

    def prev(sec):
        return pl.BlockSpec((BLOCK, un * LANES), lambda u, i: (jnp.maximum(i * qn - 1, 0), sec * ug + u))

    def nxt(sec):
        return pl.BlockSpec((BLOCK, un * LANES), lambda u, i: (jnp.minimum((i + 1) * qn, nb - 1), sec * ug + u))

    in_specs = [main(0), nxt(0), prev(1), main(1), prev(2), main(2),
                main(0), nxt(0), main(0), nxt(0), main(0), nxt(0)]
    args = [qkv] * 6 + [do, do, lse, lse, delta, delta]
    shp = jax.ShapeDtypeStruct((n_rows, n_units * LANES), F32)
    return pl.pallas_call(
        body, name=name, grid=(ug, steps), in_specs=in_specs, out_specs=(main(0), main(0), main(0)),
        out_shape=(shp, shp, shp), compiler_params=_cparams(2),
    )(*args)


def _merge_groups(os_, lses, dils, name):
    s_len = os_[0].shape[0] * dils[0]
    tm = 512

    def body(*refs):
        o_refs, l_refs = refs[0:3], refs[3:6]
        o_ref, lse_ref = refs[6:8]
        so, sl = refs[8:11], refs[11:14]
        for pair in range(2):
            for g, d in enumerate(dils):
                rows = tm // d
                for r in range(d):
                    col = slice((pair * d + r) * LANES, (pair * d + r + 1) * LANES)
                    if d == 1:
                        so[g][...] = o_refs[g][:, col]
                        sl[g][...] = l_refs[g][:, col]
                    else:
                        so[g][pl.ds(r, rows, stride=d), :] = o_refs[g][:, col]
                        sl[g][pl.ds(r, rows, stride=d), :] = l_refs[g][:, col]
            l0, l1, l2 = sl[0][...], sl[1][...], sl[2][...]
            mx = jnp.maximum(jnp.maximum(l0, l1), l2)
            e0, e1, e2 = jnp.exp(l0 - mx), jnp.exp(l1 - mx), jnp.exp(l2 - mx)
            den = e0 + e1 + e2
            inv = 1.0 / den
            o_ref[:, pair * LANES:(pair + 1) * LANES] = (so[0][...] * (e0 * inv) + so[1][...] * (e1 * inv)
                                                         + so[2][...] * (e2 * inv))
            lse_ref[:, pair * LANES:(pair + 1) * LANES] = mx + jnp.log(den)

    in_specs = [pl.BlockSpec((tm // d, 2 * d * LANES), lambda i: (i, 0)) for d in dils] * 2
    out = pl.BlockSpec((tm, 2 * LANES), lambda i: (i, 0))
    shp = jax.ShapeDtypeStruct((s_len, 2 * LANES), F32)
    return pl.pallas_call(
        body, name=name, grid=(s_len // tm,), in_specs=in_specs, out_specs=(out, out), out_shape=(shp, shp),
        scratch_shapes=[pltpu.VMEM((tm, LANES), F32)] * 6, compiler_params=_cparams(1),
    )(*os_, *lses)


def _bwd_prep(do, o, lse, dils, sinks, name):
    s_len, width = do.shape
    n_pairs = width // LANES
    tm = 512
    has_sink = sinks is not None
    n_g = len(dils)

    def body(*refs):
        do_ref, o_ref, lse_ref = refs[:3]
        pos = 3
        if has_sink:
            sink_ref = refs[pos]
            pos += 1
        outs = refs[pos:pos + 3 * n_g]
        pos += 3 * n_g
        if has_sink:
            dsink_ref = refs[pos]
            pos += 1
        s_do, s_l, s_d = refs[pos:pos + 3]
        seg = _seg_matrix(HEAD_DIM)

        if has_sink:
            @pl.when(pl.program_id(0) == 0)
            def _():
                dsink_ref[...] = jnp.zeros_like(dsink_ref)

        for pair in range(n_pairs):
            col = slice(pair * LANES, (pair + 1) * LANES)
            dov = do_ref[:, col]
            lv = lse_ref[:, col]
            delta = _seg_sum(dov * o_ref[:, col], seg)
            if has_sink:
                dsink_ref[pair] += -jnp.sum(jnp.exp(sink_ref[pair] - lv) * delta, axis=0, keepdims=True)
            s_do[...] = dov
            s_l[...] = lv
            s_d[...] = delta
            for g, d in enumerate(dils):
                rows = tm // d
                for r in range(d):
                    oc = slice((pair * d + r) * LANES, (pair * d + r + 1) * LANES)
                    if d == 1:
                        a, b, c = s_do[...], s_l[...], s_d[...]
                    else:
                        a = s_do[pl.ds(r, rows, stride=d), :]
                        b = s_l[pl.ds(r, rows, stride=d), :]
                        c = s_d[pl.ds(r, rows, stride=d), :]
                    outs[3 * g][:, oc] = a.astype(BF16)
                    outs[3 * g + 1][:, oc] = b
                    outs[3 * g + 2][:, oc] = c

    row = pl.BlockSpec((tm, width), lambda i: (i, 0))
    in_specs = [row, row, row]
    args = [do, o, lse]
    if has_sink:
        in_specs.append(pl.BlockSpec((n_pairs, 1, LANES), lambda i: (0, 0, 0)))
        args.append(sinks)
    out_specs, out_shape = [], []
    for d in dils:
        for dt in (BF16, F32, F32):
            out_specs.append(pl.BlockSpec((tm // d, n_pairs * d * LANES), lambda i: (i, 0)))
            out_shape.append(jax.ShapeDtypeStruct((s_len // d, n_pairs * d * LANES), dt))
    if has_sink:
        out_specs.append(pl.BlockSpec((n_pairs, 1, LANES), lambda i: (0, 0, 0)))
        out_shape.append(jax.ShapeDtypeStruct((n_pairs, 1, LANES), F32))
    return pl.pallas_call(
        body, name=name, grid=(s_len // tm,), in_specs=in_specs, out_specs=tuple(out_specs),
        out_shape=tuple(out_shape), scratch_shapes=[pltpu.VMEM((tm, LANES), F32)] * 3, compiler_params=_cparams(1),
    )(*args)


def _mem_kv(mem, mem_gain, w_kv, k_gain, name):
    m_len = mem.shape[0]
    kw = M_HEADS * M_HEAD_DIM

    def body(mem_ref, mg_ref, w_ref, kg_ref, k_ref, v_ref):
        mv = mem_ref[...]
        r = lax.rsqrt(jnp.mean(mv * mv, axis=-1, keepdims=True) + EPS)
        mn = ((mv * r) * mg_ref[...]).astype(BF16)
        kv = jnp.dot(mn, w_ref[...], preferred_element_type=F32)
        for h in range(M_HEADS):
            col = slice(h * M_HEAD_DIM, (h + 1) * M_HEAD_DIM)
            t = kv[:, col]
            rk = lax.rsqrt(jnp.mean(t * t, axis=-1, keepdims=True) + EPS)
            k_ref[:, col] = ((t * rk) * kg_ref[...]).astype(BF16)
        v_ref[...] = kv[:, kw:].astype(BF16)

    shp = jax.ShapeDtypeStruct((m_len, kw), BF16)
    return pl.pallas_call(body, name=name, out_shape=(shp, shp),
                          compiler_params=pltpu.CompilerParams(vmem_limit_bytes=VMEM_LIMIT_BYTES))(mem, mem_gain, w_kv, k_gain)


def _mem_kv_bwd(mem, mem_gain, w_kv, k_gain, dk, dv, name):
    m_len, d = mem.shape
    kw = M_HEADS * M_HEAD_DIM

    def body(mem_ref, mg_ref, w_ref, kg_ref, dk_ref, dv_ref, dw_ref, dmg_ref, dkg_ref, dkv_ref):
        mv = mem_ref[...]
        r = lax.rsqrt(jnp.mean(mv * mv, axis=-1, keepdims=True) + EPS)
        mhat = mv * r
        mn = (mhat * mg_ref[...]).astype(BF16)
        kv = jnp.dot(mn, w_ref[...], preferred_element_type=F32)
        dkg = jnp.zeros((1, M_HEAD_DIM), F32)
        for h in range(M_HEADS):
            col = slice(h * M_HEAD_DIM, (h + 1) * M_HEAD_DIM)
            t = kv[:, col]
            rk = lax.rsqrt(jnp.mean(t * t, axis=-1, keepdims=True) + EPS)
            that = t * rk
            dy = dk_ref[:, col]
            dkg = dkg + jnp.sum(dy * that, axis=0, keepdims=True)
            dthat = dy * kg_ref[...]
            dkv_ref[:, col] = (rk * (dthat - that * jnp.mean(dthat * that, axis=-1, keepdims=True))).astype(BF16)
        dkv_ref[:, kw:] = dv_ref[...].astype(BF16)
        dkg_ref[...] = dkg
        dkv = dkv_ref[...]
        dw_ref[...] = lax.dot_general(mn, dkv, (((0,), (0,)), ((), ())), preferred_element_type=F32)
        dmn = lax.dot_general(dkv, w_ref[...], (((1,), (1,)), ((), ())), preferred_element_type=F32)
        dmg_ref[...] = jnp.sum(dmn * mhat, axis=0, keepdims=True)

    return pl.pallas_call(
        body, name=name,
        out_shape=(jax.ShapeDtypeStruct((d, 2 * kw), F32), jax.ShapeDtypeStruct((1, d), F32),
                   jax.ShapeDtypeStruct((1, M_HEAD_DIM), F32)),
        scratch_shapes=[pltpu.VMEM((m_len, 2 * kw), BF16)],
        compiler_params=pltpu.CompilerParams(vmem_limit_bytes=VMEM_LIMIT_BYTES),
    )(mem, mem_gain, w_kv, k_gain, dk, dv)


def _mem_attn_fwd(proj, cidx, mk, mv, q_gain, name):
    s_len = proj.shape[0]
    kw = M_HEADS * M_HEAD_DIM
    tm = 512
    scale = M_HEAD_DIM ** -0.5

    def body(q_ref, k_ref, v_ref, g_ref, o_ref):
        for h in range(M_HEADS):
            col = slice(h * M_HEAD_DIM, (h + 1) * M_HEAD_DIM)
            t = q_ref[:, col]
            rs = lax.rsqrt(jnp.mean(t * t, axis=-1, keepdims=True) + EPS)
            qn = ((t * rs) * g_ref[...]).astype(BF16)
            s = lax.dot_general(qn, k_ref[:, col], (((1,), (1,)), ((), ())), preferred_element_type=F32) * scale
            mx = jnp.max(s, axis=-1, keepdims=True)
            p = jnp.exp(s - mx)
            pn = (p * (1.0 / jnp.sum(p, axis=-1, keepdims=True))).astype(BF16)
            o_ref[:, col] = jnp.dot(pn, v_ref[:, col], preferred_element_type=F32).astype(BF16)

    whole = pl.BlockSpec((MEM_LEN, kw), lambda i: (0, 0))
    return pl.pallas_call(
        body, name=name, grid=(s_len // tm,),
        in_specs=[pl.BlockSpec((tm, kw), lambda i: (i, cidx)), whole, whole, pl.BlockSpec((1, M_HEAD_DIM), lambda i: (0, 0))],
        out_specs=pl.BlockSpec((tm, kw), lambda i: (i, 0)),
        out_shape=jax.ShapeDtypeStruct((s_len, kw), BF16), compiler_params=_cparams(1),
    )(proj, mk, mv, q_gain)


def _mem_attn_bwd(proj, cidx, mk, mv, q_gain, do, name):
    s_len = proj.shape[0]
    kw = M_HEADS * M_HEAD_DIM
    tm = 512
    scale = M_HEAD_DIM ** -0.5

    def body(q_ref, k_ref, v_ref, g_ref, do_ref, dq_ref, dk_ref, dv_ref, dg_ref):
        @pl.when(pl.program_id(0) == 0)
        def _():
            dk_ref[...] = jnp.zeros_like(dk_ref)
            dv_ref[...] = jnp.zeros_like(dv_ref)
            dg_ref[...] = jnp.zeros_like(dg_ref)

        for h in range(M_HEADS):
            col = slice(h * M_HEAD_DIM, (h + 1) * M_HEAD_DIM)
            t = q_ref[:, col]
            rs = lax.rsqrt(jnp.mean(t * t, axis=-1, keepdims=True) + EPS)
            that = t * rs
            qn = (that * g_ref[...]).astype(BF16)
            kh, vh = k_ref[:, col], v_ref[:, col]
            dob = do_ref[:, col].astype(BF16)
            s = lax.dot_general(qn, kh, (((1,), (1,)), ((), ())), preferred_element_type=F32) * scale
            mx = jnp.max(s, axis=-1, keepdims=True)
            p = jnp.exp(s - mx)
            p = p * (1.0 / jnp.sum(p, axis=-1, keepdims=True))
            dp = lax.dot_general(dob, vh, (((1,), (1,)), ((), ())), preferred_element_type=F32)
            ds = (p * (dp - jnp.sum(p * dp, axis=-1, keepdims=True)) * scale).astype(BF16)
            dqn = jnp.dot(ds, kh, preferred_element_type=F32)
            dk_ref[:, col] += lax.dot_general(ds, qn, (((0,), (0,)), ((), ())), preferred_element_type=F32)
            dv_ref[:, col] += lax.dot_general(p.astype(BF16), dob, (((0,), (0,)), ((), ())), preferred_element_type=F32)
            dg_ref[...] += jnp.sum(dqn * that, axis=0, keepdims=True)
            dthat = dqn * g_ref[...]
            dq_ref[:, col] = (rs * (dthat - that * jnp.mean(dthat * that, axis=-1, keepdims=True))).astype(BF16)

    whole = pl.BlockSpec((MEM_LEN, kw), lambda i: (0, 0))
    vec = pl.BlockSpec((1, M_HEAD_DIM), lambda i: (0, 0))
    row = pl.BlockSpec((tm, kw), lambda i: (i, 0))
    return pl.pallas_call(
        body, name=name, grid=(s_len // tm,),
        in_specs=[pl.BlockSpec((tm, kw), lambda i: (i, cidx)), whole, whole, vec, row],
        out_specs=(row, whole, whole, vec),
        out_shape=(jax.ShapeDtypeStruct((s_len, kw), BF16), jax.ShapeDtypeStruct((MEM_LEN, kw), F32),
                   jax.ShapeDtypeStruct((MEM_LEN, kw), F32), jax.ShapeDtypeStruct((1, M_HEAD_DIM), F32)),
        compiler_params=_cparams(1),
    )(proj, mk, mv, q_gain, do)


def _gate_merge(gates, pa, pb, pm, name):
    s_len, d = pa.shape
    tm = 256

    def body(g_ref, a_ref, b_ref, m_ref, o_ref):
        f = lambda v: v.astype(F32)
        o_ref[...] = (f(g_ref[:, 0:d]) * f(a_ref[...]) + f(g_ref[:, d:2 * d]) * f(b_ref[...])
                      + f(g_ref[:, 2 * d:3 * d]) * f(m_ref[...])).astype(BF16)

    row = pl.BlockSpec((tm, d), lambda i: (i, 0))
    return pl.pallas_call(
        body, name=name, grid=(s_len // tm,), in_specs=[pl.BlockSpec((tm, 3 * d), lambda i: (i, 0)), row, row, row],
        out_specs=row, out_shape=jax.ShapeDtypeStruct((s_len, d), BF16), compiler_params=_cparams(1),
    )(gates, pa, pb, pm)


def _gate_merge_bwd(dmerged, gates, pa, pb, pm, name):
    s_len, d = pa.shape
    tm = 256

    def body(dm_ref, g_ref, a_ref, b_ref, m_ref, da_ref, db_ref, dmm_ref, dg_ref, dbg_ref):
        @pl.when(pl.program_id(0) == 0)
        def _():
            dbg_ref[...] = jnp.zeros_like(dbg_ref)
        dm = dm_ref[...]
        for k, (p_ref, dp_ref) in enumerate(((a_ref, da_ref), (b_ref, db_ref), (m_ref, dmm_ref))):
            col = slice(k * d, (k + 1) * d)
            g = g_ref[:, col].astype(F32)
            dp_ref[...] = (dm * g).astype(BF16)
            dpre = (dm * p_ref[...].astype(F32)) * (g * (1.0 - g))
            dbg_ref[:, col] += jnp.sum(dpre, axis=0, keepdims=True)
            dg_ref[:, col] = dpre.astype(BF16)

    row = pl.BlockSpec((tm, d), lambda i: (i, 0))
    wide = pl.BlockSpec((tm, 3 * d), lambda i: (i, 0))
    shp = jax.ShapeDtypeStruct((s_len, d), BF16)
    return pl.pallas_call(
        body, name=name, grid=(s_len // tm,), in_specs=[row, wide, row, row, row],
        out_specs=(row, row, row, wide, pl.BlockSpec((1, 3 * d), lambda i: (0, 0))),
        out_shape=(shp, shp, shp, jax.ShapeDtypeStruct((s_len, 3 * d), BF16), jax.ShapeDtypeStruct((1, 3 * d), F32)),
        compiler_params=_cparams(1),
    )(dmerged, gates, pa, pb, pm)


CONV_CHUNK = 256


def _pick_row(tile, j):
    row = lax.broadcasted_iota(jnp.int32, tile.shape, 0)
    return jnp.sum(jnp.where(row == j, tile, jnp.zeros_like(tile)), axis=0, keepdims=True)


def _rows_before(ref, start, k):
    cur = ref[pl.ds(start, CONV_CHUNK), :].astype(F32)
    prev = ref[pl.ds(pl.multiple_of(jnp.maximum(start - 16, 0), 16), 16), :].astype(F32)
    prev = jnp.where(start > 0, prev, jnp.zeros_like(prev))
    rolled = pltpu.roll(cur, k, 0)
    row = lax.broadcasted_iota(jnp.int32, cur.shape, 0)
    for j in range(k):
        rolled = jnp.where(row == j, _pick_row(prev, 16 - k + j), rolled)
    return rolled


def _rows_after(ref, start, k):
    cur = ref[pl.ds(start, CONV_CHUNK), :]
    nxt = ref[pl.ds(pl.multiple_of(start + CONV_CHUNK, 8), 8), :]
    rolled = pltpu.roll(cur, CONV_CHUNK - k, 0)
    row = lax.broadcasted_iota(jnp.int32, cur.shape, 0)
    for j in range(k):
        rolled = jnp.where(row == CONV_CHUNK - k + j, _pick_row(nxt, j), rolled)
    return rolled


def _conv_pre(u_ref, w_ref, b_ref, start):
    u2 = _rows_before(u_ref, start, 2)
    u1 = _rows_before(u_ref, start, 1)
    u0 = u_ref[pl.ds(start, CONV_CHUNK), :].astype(F32)
    c = ((b_ref[...] + w_ref[0:1, :] * u2) + w_ref[1:2, :] * u1) + w_ref[2:3, :] * u0
    return c, (u2, u1, u0)


def _conv_glu(u, conv_w, conv_b, name):
    s_len = u.shape[0]
    nblk = D_FF // LANES

    def body(ua_ref, ug_ref, wa_ref, wg_ref, ba_ref, bg_ref, o_ref):
        def chunk(ci, carry):
            start = pl.multiple_of(ci * CONV_CHUNK, CONV_CHUNK)
            ca, _ = _conv_pre(ua_ref, wa_ref, ba_ref, start)
            cg, _ = _conv_pre(ug_ref, wg_ref, bg_ref, start)
            o_ref[pl.ds(start, CONV_CHUNK), :] = ((ca * _sigmoid(ca)) * cg).astype(BF16)
            return carry
        lax.fori_loop(0, s_len // CONV_CHUNK, chunk, 0)

    def col(rows, off):
        return pl.BlockSpec((rows, LANES), lambda j: (0, off + j))

    return pl.pallas_call(
        body, name=name, grid=(nblk,),
        in_specs=[col(s_len, 0), col(s_len, nblk), col(3, 0), col(3, nblk), col(1, 0), col(1, nblk)],
        out_specs=col(s_len, 0), out_shape=jax.ShapeDtypeStruct((s_len, D_FF), BF16), compiler_params=_cparams(1),
    )(u, u, conv_w, conv_w, conv_b, conv_b)


def _conv_glu_bwd(dact, u, conv_w, conv_b, name):
    s_len = u.shape[0]
    nblk = D_FF // LANES
    n_chunks = s_len // CONV_CHUNK

    def body(da_ref, ua_ref, ug_ref, wa_ref, wg_ref, ba_ref, bg_ref,
             dua_ref, dug_ref, dwa_ref, dwg_ref, dba_ref, dbg_ref, sa, sg):
        sa[pl.ds(s_len, 8), :] = jnp.zeros((8, LANES), F32)
        sg[pl.ds(s_len, 8), :] = jnp.zeros((8, LANES), F32)
        zero = jnp.zeros((1, LANES), F32)

        def chunk1(ci, carry):
            start = pl.multiple_of(ci * CONV_CHUNK, CONV_CHUNK)
            ca, ua = _conv_pre(ua_ref, wa_ref, ba_ref, start)
            cg, ug = _conv_pre(ug_ref, wg_ref, bg_ref, start)
            dact_v = da_ref[pl.ds(start, CONV_CHUNK), :].astype(F32)
            sig = _sigmoid(ca)
            dcg = dact_v * (ca * sig)
            dca = (dact_v * cg) * (sig * (1.0 + ca * (1.0 - sig)))
            sa[pl.ds(start, CONV_CHUNK), :] = dca
            sg[pl.ds(start, CONV_CHUNK), :] = dcg
            out = [carry[0] + jnp.sum(dca, axis=0, keepdims=True), carry[1] + jnp.sum(dcg, axis=0, keepdims=True)]
            for j in range(3):
                out.append(carry[2 + j] + jnp.sum(dca * ua[j], axis=0, keepdims=True))
            for j in range(3):
                out.append(carry[5 + j] + jnp.sum(dcg * ug[j], axis=0, keepdims=True))
            return tuple(out)

        acc = lax.fori_loop(0, n_chunks, chunk1, (zero,) * 8)
        dba_ref[...] = acc[0]
        dbg_ref[...] = acc[1]
        for j in range(3):
            dwa_ref[j:j + 1, :] = acc[2 + j]
            dwg_ref[j:j + 1, :] = acc[5 + j]

        def chunk2(ci, carry):
            start = pl.multiple_of(ci * CONV_CHUNK, CONV_CHUNK)
            for s_ref, w_ref, o_ref in ((sa, wa_ref, dua_ref), (sg, wg_ref, dug_ref)):
                d0 = s_ref[pl.ds(start, CONV_CHUNK), :]
                d1 = _rows_after(s_ref, start, 1)
                d2 = _rows_after(s_ref, start, 2)
                o_ref[pl.ds(start, CONV_CHUNK), :] = (w_ref[2:3, :] * d0 + w_ref[1:2, :] * d1
                                                      + w_ref[0:1, :] * d2).astype(BF16)
            return carry
        lax.fori_loop(0, n_chunks, chunk2, 0)

    def col(rows, off):
        return pl.BlockSpec((rows, LANES), lambda j: (0, off + j))

    big = jax.ShapeDtypeStruct((s_len, D_FF), BF16)
    return pl.pallas_call(
        body, name=name, grid=(nblk,),
        in_specs=[col(s_len, 0), col(s_len, 0), col(s_len, nblk), col(3, 0), col(3, nblk), col(1, 0), col(1, nblk)],
        out_specs=(col(s_len, 0), col(s_len, 0), col(3, 0), col(3, 0), col(1, 0), col(1, 0)),
        out_shape=(big, big, jax.ShapeDtypeStruct((3, D_FF), F32), jax.ShapeDtypeStruct((3, D_FF), F32),
                   jax.ShapeDtypeStruct((1, D_FF), F32), jax.ShapeDtypeStruct((1, D_FF), F32)),
        scratch_shapes=[pltpu.VMEM((s_len + 8, LANES), F32)] * 2, compiler_params=_cparams(1),
    )(dact, u, u, conv_w, conv_w, conv_b, conv_b)


def _rope_tables(positions):
    half = ROPE_DIMS // 2
    freqs = jnp.exp(jnp.arange(half, dtype=F32) * (-2.0 * math.log(ROPE_THETA) / ROPE_DIMS))
    ang = positions.reshape(-1).astype(F32)[:, None] * freqs
    cos, sin = jnp.cos(ang), jnp.sin(ang)
    n = ang.shape[0]
    zeros = lambda w: jnp.zeros((n, w), F32)
    c = jnp.concatenate([cos, cos, jnp.ones((n, HEAD_DIM - ROPE_DIMS), F32)], axis=1)
    s1 = jnp.concatenate([-sin, zeros(HEAD_DIM - half)], axis=1)
    s2 = jnp.concatenate([zeros(half), sin, zeros(HEAD_DIM - ROPE_DIMS)], axis=1)
    return tuple(jnp.tile(t, (1, 2)) for t in (c, s1, s2))


def _two(v):
    return jnp.tile(v.reshape(1, HEAD_DIM), (1, 2))


def _fold_heads(g):
    return g[0, :HEAD_DIM] + g[0, HEAD_DIM:]


MIX_WEIGHTS = ('w_gate', 'w_mem_kv', 'w_o_a', 'w_o_b', 'w_o_m', 'w_out')
FFN_WEIGHTS = ('w_up', 'conv_w', 'w_down')


def _device_step(x, mem, positions, target, w, hooks=None):
    tabs = _rope_tables(positions)
    dils = tuple(d for _, d in A_GROUPS)
    grads = {}
    w = dict(w)

    h, h_t, r1 = _rms_fwd(x, w['attn_norm'], "rms1")
    if hooks is not None:
        w.update(hooks.weights('in', [h, *tabs]))
    proj = _mm_rows([(h, w['w_in'], 0)], "mm_in")

    qkv_a, o_g, lse_g = [], [], []
    for gi, (window, d) in enumerate(A_GROUPS):
        gq, gk = _two(w['a_q_norm'][gi]), _two(w['a_k_norm'][gi])
        qkv = _qk_prep(proj, 6 * gi, d, False, gq, gk, tabs, f"qk_prep_a{gi}")
        o, lse = _band_fwd(qkv, 2 * d, window // d, None, f"band_fwd_a{gi}")
        qkv_a.append(qkv)
        o_g.append(o)
        lse_g.append(lse)
    o_a, lse_a = _merge_groups(o_g, lse_g, dils, "merge_a")
    if hooks is not None:
        w.update(hooks.weights('mix', [o_a]))

    gbq, gbk = _two(w['b_q_norm']), _two(w['b_k_norm'])
    sinks = jnp.repeat(w['b_sinks'].reshape(4, 2), HEAD_DIM, axis=1).reshape(4, 1, LANES)
    qkv_b = _qk_prep(proj, 18, 1, True, gbq, gbk, tabs, "qk_prep_b")
    o_b, lse_b = _band_fwd(qkv_b, 4, B_WINDOW - 1, sinks, "band_fwd_b")

    gates = _mm_rows([(h, w['w_gate'], 0)], "mm_gate", bias=w['b_gate'], sigmoid=True, out_dtypes=(BF16,))
    mk, mv = _mem_kv(mem, w['mem_norm'], w['w_mem_kv'], w['m_k_norm'], "mem_kv")
    o_m = _mem_attn_fwd(proj, 6, mk, mv, w['m_q_norm'], "mem_attn")

    pa = _mm_rows([(o_a, w['w_o_a'], 0)], "mm_oa", out_dtypes=(BF16,))
    pb = _mm_rows([(o_b, w['w_o_b'], 0)], "mm_ob", out_dtypes=(BF16,))
    pm = _mm_rows([(o_m, w['w_o_m'], 0)], "mm_om", out_dtypes=(BF16,))
    merged = _gate_merge(gates, pa, pb, pm, "gate_merge")
    x1 = _mm_rows([(merged, w['w_out'], 0)], "mm_out", res=x)

    if hooks is not None:
        w.update(hooks.weights('ffn', [x1]))
    h2, h2_t, r2 = _rms_fwd(x1, w['ffn_norm'], "rms2")
    u = _mm_rows([(h2, w['w_up'], 0)], "mm_up", out_dtypes=(BF16,))
    act = _conv_glu(u, w['conv_w'], w['conv_b'], "conv_glu")
    dy, dy_b, loss = _mm_rows([(act, w['w_down'], 0)], "mm_down", res=x1, loss_target=target)

    dact = _mm_rows([(dy_b, w['w_down'], 0)], "mm_d_act", nt=True, out_dtypes=(BF16,))
    grads['w_down'] = _mm_tn(act, dy_b, "mm_dw_down")
    du_a, du_g, dcw_a, dcw_g, dcb_a, dcb_g = _conv_glu_bwd(dact, u, w['conv_w'], w['conv_b'], "conv_glu_bwd")
    grads['conv_w'] = jnp.concatenate([dcw_a, dcw_g], axis=1)
    grads['conv_b'] = jnp.concatenate([dcb_a, dcb_g], axis=1)
    grads['w_up'] = jnp.concatenate([_mm_cols(h2_t, du_a, "mm_dw_up_a"), _mm_cols(h2_t, du_g, "mm_dw_up_g")], axis=1)
    ffn_gain = w['ffn_norm']
    if hooks is not None:
        ffn_gain = ffn_gain + hooks.grads('ffn', grads)[0:1, 0:1]
    dx1, dx1_b, grads['ffn_norm'] = _mm_rows([(du_a, w['w_up'], 0), (du_g, w['w_up'], 1)], "mm_d_h2", nt=True,
                                             rms_bwd=(x1, r2, ffn_gain, dy), out_dtypes=(F32, BF16))

    dmerged = _mm_rows([(dx1_b, w['w_out'], 0)], "mm_d_merged", nt=True)
    grads['w_out'] = _mm_tn(merged, dx1_b, "mm_dw_out")
    dpa, dpb, dpm, dgpre, grads['b_gate'] = _gate_merge_bwd(dmerged, gates, pa, pb, pm, "gate_merge_bwd")
    do_a = _mm_rows([(dpa, w['w_o_a'], 0)], "mm_d_oa", nt=True)
    do_b = _mm_rows([(dpb, w['w_o_b'], 0)], "mm_d_ob", nt=True)
    do_m = _mm_rows([(dpm, w['w_o_m'], 0)], "mm_d_om", nt=True)
    grads['w_o_a'] = _mm_tn(o_a, dpa, "mm_dw_oa")
    grads['w_o_b'] = _mm_tn(o_b, dpb, "mm_dw_ob")
    grads['w_o_m'] = _mm_tn(o_m, dpm, "mm_dw_om")
    grads['w_gate'] = _mm_cols(h_t, dgpre, "mm_dw_gate")
    dq_m, dmk, dmv, grads['m_q_norm'] = _mem_attn_bwd(proj, 6, mk, mv, w['m_q_norm'], do_m, "mem_attn_bwd")
    grads['w_mem_kv'], grads['mem_norm'], grads['m_k_norm'] = _mem_kv_bwd(
        mem, w['mem_norm'], w['w_mem_kv'], w['m_k_norm'], dmk, dmv, "mem_kv_bwd")
    a_gain = w['a_q_norm']
    if hooks is not None:
        a_gain = a_gain + hooks.grads('mix', grads)[0:1, 0:1]

    prep = _bwd_prep(do_a, o_a, lse_a, dils, None, "bwd_prep_a")
    dproj, dgq_a, dgk_a = [], [], []
    for gi, (window, d) in enumerate(A_GROUPS):
        gq, gk = _two(a_gain[gi]), _two(w['a_k_norm'][gi])
        dqkv = _band_bwd(qkv_a[gi], prep[3 * gi], prep[3 * gi + 1], prep[3 * gi + 2], 2 * d, window // d,
                         f"band_bwd_a{gi}")
        dp, dgq, dgk = _qk_prep_bwd(dqkv, proj, 6 * gi, d, False, gq, gk, tabs, f"qk_prep_bwd_a{gi}")
        dproj.append(dp)
        dgq_a.append(_fold_heads(dgq))
        dgk_a.append(_fold_heads(dgk))
    grads['a_q_norm'] = jnp.stack(dgq_a)
    grads['a_k_norm'] = jnp.stack(dgk_a)

    do_bu, lse_bu, delta_bu, dsink = _bwd_prep(do_b, o_b, lse_b, (1,), sinks, "bwd_prep_b")
    dqkv = _band_bwd(qkv_b, do_bu, lse_bu, delta_bu, 4, B_WINDOW - 1, "band_bwd_b")
    dp_b, dgq, dgk = _qk_prep_bwd(dqkv, proj, 18, 1, True, gbq, gbk, tabs, "qk_prep_bwd_b")
    dproj.append(dp_b)
    grads['b_q_norm'] = _fold_heads(dgq)
    grads['b_k_norm'] = _fold_heads(dgk)
    grads['b_sinks'] = jnp.stack([dsink[:, 0, 0], dsink[:, 0, HEAD_DIM]], axis=1).reshape(8)

    dproj.append(dq_m)

    cols = (0, 1, 2, 3, 6)
    grads['w_in'] = _mm_rows_cat(h_t, dproj, "mm_dw_in")
    attn_gain = w['attn_norm']
    if hooks is not None:
        attn_gain = attn_gain + hooks.grads('in', grads)[0:1, 0:1]
    grad_x, grads['attn_norm'] = _mm_rows(
        [(dp, w['w_in'], c) for dp, c in zip(dproj, cols)] + [(dgpre, w['w_gate'], 0)], "mm_d_h", nt=True,
        rms_bwd=(x, r1, attn_gain, dx1))
    return loss, grad_x, grads


def _coords():
    return lax.axis_index("x"), lax.axis_index("y"), lax.axis_index("c")


def _slot(p):
    return 4 * p[0] + 2 * p[1] + p[2]


ALL_PEERS = tuple(range(1, N_DEV))
CHIP_PEERS = (1, 4, 2, 6)
OTHER_CHIPS = (4, 2, 6)


def _peers(me, masks=ALL_PEERS):
    x, y, c = me
    return [(1 - x if mask & 4 else x, 1 - y if mask & 2 else y, 1 - c if mask & 1 else c) for mask in masks]


HBM_SPEC = pl.BlockSpec(memory_space=pltpu.HBM)


SEM_SPEC = pl.BlockSpec(memory_space=pltpu.SEMAPHORE)
SIDE_EFFECT = pltpu.SideEffectType.DATAFLOW_SIDE_EFFECTING


def _exchange_start(blocks, name, gather=False, masks=ALL_PEERS, after=None):
    n = len(blocks)
    n_peers = len(masks)
    n_in = 2 * n + (0 if after is None else 1)

    def body(*refs):
        ins, lands = refs[:n], refs[n:2 * n]
        send_sems, recv_sems = refs[n_in], refs[n_in + 1]
        token = refs[-1]
        me = _coords()
        peers = _peers(me, masks)
        for a in range(n):
            for k in range(n_peers):
                pltpu.make_async_remote_copy(
                    src_ref=ins[a] if gather else ins[a].at[_slot(peers[k])], dst_ref=lands[a].at[_slot(me)],
                    send_sem=send_sems.at[a * n_peers + k], recv_sem=recv_sems.at[a * n_peers + k],
                    device_id=peers[k], device_id_type=MESH).start()
        token[...] = jnp.zeros_like(token)

    land_shapes = [((N_DEV,) + b.shape) if gather else b.shape for b in blocks]
    hbm_in = [pltpu.HBM(b.shape, b.dtype) for b in blocks]
    hbm_land = [pltpu.HBM(s, b.dtype) for s, b in zip(land_shapes, blocks)]
    sems = pltpu.SemaphoreType.DMA((n * n_peers,))
    ins = [pltpu.with_memory_space_constraint(b, pltpu.HBM) for b in blocks]
    lands = [pltpu.with_memory_space_constraint(lax.empty(s, b.dtype), pltpu.HBM) for s, b in zip(land_shapes, blocks)]
    return pl.pallas_call(
        body, name=name, out_shape=(sems, sems, *hbm_in, *hbm_land, jax.ShapeDtypeStruct((8, LANES), F32)),
        in_specs=[HBM_SPEC] * (2 * n) + ([] if after is None else [pl.BlockSpec(memory_space=pl.ANY)]),
        out_specs=(SEM_SPEC, SEM_SPEC, *([HBM_SPEC] * (2 * n)), pl.BlockSpec(memory_space=pltpu.VMEM)),
        input_output_aliases={i: 2 + i for i in range(2 * n)},
        compiler_params=pltpu.CompilerParams(has_side_effects=SIDE_EFFECT),
    )(*ins, *lands, *([] if after is None else [after]))


def _exchange_wait(started, after, name, gather=False, masks=ALL_PEERS):
    n = (len(started) - 3) // 2
    n_peers = len(masks)
    send_sems, recv_sems = started[0], started[1]
    thru = started[2:2 + 2 * n]

    def body(*refs):
        ins, lands = refs[:n], refs[n:2 * n]
        send_ref, recv_ref = refs[2 * n], refs[2 * n + 1]
        me = _coords()
        peers = _peers(me, masks)
        for a in range(n):
            for k in range(n_peers):
                cp = pltpu.make_async_remote_copy(
                    src_ref=ins[a] if gather else ins[a].at[_slot(peers[k])], dst_ref=lands[a].at[_slot(peers[k])],
                    send_sem=send_ref.at[a * n_peers + k], recv_sem=recv_ref.at[a * n_peers + k],
                    device_id=peers[k], device_id_type=MESH)
                cp.wait_send()
                cp.wait_recv()

    hbm = [pltpu.HBM(t.shape, t.dtype) for t in thru]
    res = pl.pallas_call(
        body, name=name, out_shape=tuple(hbm),
        in_specs=[HBM_SPEC] * (2 * n) + [SEM_SPEC, SEM_SPEC] + [pl.BlockSpec(memory_space=pl.ANY)] * len(after),
        out_specs=tuple([HBM_SPEC] * (2 * n)), input_output_aliases={i: i for i in range(2 * n)},
        compiler_params=pltpu.CompilerParams(has_side_effects=SIDE_EFFECT),
    )(*thru, send_sems, recv_sems, *after)
    return res[n:]


def _sibling_forward(arrays, name):
    n = len(arrays)
    n_fwd = len(OTHER_CHIPS)

    def body(*refs):
        bufs = refs[n:2 * n]
        token, send_sems, recv_sems = refs[2 * n:]
        token[...] = jnp.zeros_like(token)
        x, y, c = _coords()
        sibling = (x, y, 1 - c)
        mine = _peers((x, y, c), OTHER_CHIPS)
        theirs = _peers(sibling, OTHER_CHIPS)

        def copy(a, k, block):
            rows = bufs[a].at[_slot(block)]
            return pltpu.make_async_remote_copy(
                src_ref=rows, dst_ref=rows, send_sem=send_sems.at[a * n_fwd + k], recv_sem=recv_sems.at[a * n_fwd + k],
                device_id=sibling, device_id_type=MESH)

        sends = [copy(a, k, mine[k]) for a in range(n) for k in range(n_fwd)]
        for cp in sends:
            cp.start()
        for a in range(n):
            for k in range(n_fwd):
                copy(a, k, theirs[k]).wait_recv()
        for cp in sends:
            cp.wait_send()

    res = pl.pallas_call(
        body, name=name, in_specs=[HBM_SPEC] * n,
        out_specs=tuple([HBM_SPEC] * n + [pl.BlockSpec(memory_space=pltpu.VMEM)]),
        out_shape=tuple([jax.ShapeDtypeStruct(a.shape, a.dtype) for a in arrays] + [jax.ShapeDtypeStruct((8, LANES), F32)]),
        input_output_aliases={i: i for i in range(n)},
        scratch_shapes=[pltpu.SemaphoreType.DMA((n * n_fwd,)), pltpu.SemaphoreType.DMA((n * n_fwd,))],
    )(*arrays)
    return res[:n], res[n]


def _all_sum(p, name):
    def body(p_ref, o_ref, recv, send_sems, recv_sems):
        me = _coords()
        peers = _peers(me)
        recv[_slot(me)] = p_ref[...]

        def copy(k, landing):
            return pltpu.make_async_remote_copy(
                src_ref=p_ref, dst_ref=recv.at[_slot(landing)], send_sem=send_sems.at[k], recv_sem=recv_sems.at[k],
                device_id=peers[k], device_id_type=MESH)

        sends = [copy(k, me) for k in range(N_DEV - 1)]
        for cp in sends:
            cp.start()
        for k in range(N_DEV - 1):
            copy(k, peers[k]).wait_recv()
        for cp in sends:
            cp.wait_send()
        acc = recv[0]
        for s in range(1, N_DEV):
            acc = acc + recv[s]
        o_ref[...] = acc

    vmem = pl.BlockSpec(memory_space=pltpu.VMEM)
    return pl.pallas_call(
        body, name=name, in_specs=[vmem], out_specs=vmem, out_shape=jax.ShapeDtypeStruct(p.shape, F32),
        scratch_shapes=[pltpu.VMEM((N_DEV,) + p.shape, F32), pltpu.SemaphoreType.DMA((N_DEV - 1,)),
                        pltpu.SemaphoreType.DMA((N_DEV - 1,))],
    )(p)


def _adam(w, g, m, v):
    m2 = ADAM_B1 * m + (1.0 - ADAM_B1) * g
    v2 = ADAM_B2 * v + (1.0 - ADAM_B2) * (g * g)
    m_hat = m2 / (1.0 - ADAM_B1 ** ADAM_STEP)
    v_hat = v2 / (1.0 - ADAM_B2 ** ADAM_STEP)
    delta = -ADAM_LR * (m_hat / (jnp.sqrt(v_hat) + ADAM_EPS) + ADAM_WD * w)
    return delta, m2, v2


def _row_tile(rows, cols):
    best = rows
    for t in range(16, rows, 16):
        if rows % t == 0 and t * cols * 4 <= (1 << 20):
            best = t
    return best


def _adam_reduce(parts, w, m, v, name):
    rows, cols = w.shape
    tr = _row_tile(rows, cols)

    def body(p_ref, w_ref, m_ref, v_ref, g_ref, d_ref, m2_ref, v2_ref):
        g = p_ref[0].astype(F32)
        for s in range(1, N_DEV):
            g = g + p_ref[s].astype(F32)
        g_ref[...] = g
        d_ref[...], m2_ref[...], v2_ref[...] = _adam(w_ref[...], g, m_ref[...], v_ref[...])

    blk = pl.BlockSpec((tr, cols), lambda i: (i, 0))
    shp = jax.ShapeDtypeStruct((rows, cols), F32)
    return pl.pallas_call(
        body, name=name, grid=(rows // tr,),
        in_specs=[pl.BlockSpec((N_DEV, tr, cols), lambda i: (0, i, 0)), blk, blk, blk],
        out_specs=(blk,) * 4, out_shape=(shp,) * 4, compiler_params=_cparams(1),
    )(parts, w, m, v)


PACK_COLS = 1024
PACK = {'attn_norm': (0, 1, 1024), 'mem_norm': (1, 1, 1024), 'ffn_norm': (2, 1, 1024), 'b_gate': (3, 3, 1024),
        'conv_b': (6, 6, 1024), 'a_q_norm': (12, 3, 64), 'a_k_norm': (15, 3, 64), 'b_q_norm': (18, 1, 64),
        'b_k_norm': (19, 1, 64), 'm_q_norm': (20, 1, 128), 'm_k_norm': (21, 1, 128), 'b_sinks': (22, 1, 8)}
PACK_LOSS_ROW = 23
PACK_ROWS = 24


def _pack_pieces(name, width):
    r0, nr, lanes = PACK[name]
    out = []
    for j in range(nr):
        if lanes == PACK_COLS:
            w = min(PACK_COLS, width - j * PACK_COLS)
            out.append((r0 + j, slice(0, 1), slice(j * PACK_COLS, j * PACK_COLS + w), w))
        else:
            out.append((r0 + j, slice(j, j + 1), slice(0, lanes), lanes))
    return out


def _pack_small(grads, loss_tile, name):
    names = list(PACK)

    def body(*refs):
        o_ref = refs[-1]
        o_ref[...] = jnp.zeros_like(o_ref)
        for k, nm in enumerate(names):
            for row, rs, ls, w in _pack_pieces(nm, refs[k].shape[1]):
                o_ref[row:row + 1, 0:w] = refs[k][rs, ls]
        o_ref[PACK_LOSS_ROW:PACK_LOSS_ROW + 1, 0:1] = refs[len(names)][0:1, 0:1]

    vmem = pl.BlockSpec(memory_space=pltpu.VMEM)
    args = [grads[nm] for nm in names] + [loss_tile]
    return pl.pallas_call(body, name=name, in_specs=[vmem] * len(args), out_specs=vmem,
                          out_shape=jax.ShapeDtypeStruct((PACK_ROWS, PACK_COLS), F32))(*args)


def _adam_small(gsum, ws, ms, vs, name):
    names = list(PACK)
    n = len(names)

    def body(*refs):
        g_ref = refs[0]
        w_refs, m_refs, v_refs = refs[1:1 + n], refs[1 + n:1 + 2 * n], refs[1 + 2 * n:1 + 3 * n]
        outs = refs[1 + 3 * n:]
        outs[0][...] = g_ref[PACK_LOSS_ROW:PACK_LOSS_ROW + 1, 0:1]
        for k, nm in enumerate(names):
            o_g, o_d, o_m, o_v = outs[1 + 4 * k:5 + 4 * k]
            for row, rs, ls, width in _pack_pieces(nm, w_refs[k].shape[1]):
                src = (rs, ls)
                g = g_ref[row:row + 1, 0:width]
                d, m2, v2 = _adam(w_refs[k][src], g, m_refs[k][src], v_refs[k][src])
                o_g[src] = g
                o_d[src] = d
                o_m[src] = m2
                o_v[src] = v2

    vmem = pl.BlockSpec(memory_space=pltpu.VMEM)
    shapes = [jax.ShapeDtypeStruct((1, 1), F32)]
    for nm in names:
        shapes += [jax.ShapeDtypeStruct(ws[nm].shape, F32)] * 4
    args = [gsum] + [ws[nm] for nm in names] + [ms[nm] for nm in names] + [vs[nm] for nm in names]
    return pl.pallas_call(
        body, name=name, in_specs=[vmem] * len(args), out_specs=tuple([vmem] * len(shapes)), out_shape=tuple(shapes),
    )(*args)


def _as2d(name, a):
    return a.reshape(a.shape[-2], a.shape[-1]) if a.ndim == 3 else a


def kernel(x, mem, positions, attn_norm, w_in, a_q_norm, a_k_norm, b_q_norm, b_k_norm, b_sinks, mem_norm, w_mem_kv, m_q_norm, m_k_norm, w_o_a, w_o_b, w_o_m, w_gate, b_gate, w_out, ffn_norm, w_up, conv_w, conv_b, w_down, loss_target, m_attn_norm, m_w_in, m_a_q_norm, m_a_k_norm, m_b_q_norm, m_b_k_norm, m_b_sinks, m_mem_norm, m_w_mem_kv, m_m_q_norm, m_m_k_norm, m_w_o_a, m_w_o_b, m_w_o_m, m_w_gate, m_b_gate, m_w_out, m_ffn_norm, m_w_up, m_conv_w, m_conv_b, m_w_down, v_attn_norm, v_w_in, v_a_q_norm, v_a_k_norm, v_b_q_norm, v_b_k_norm, v_b_sinks, v_mem_norm, v_w_mem_kv, v_m_q_norm, v_m_k_norm, v_w_o_a, v_w_o_b, v_w_o_m, v_w_gate, v_b_gate, v_w_out, v_ffn_norm, v_w_up, v_conv_w, v_conv_b, v_w_down):
    given = dict(attn_norm=attn_norm, w_in=w_in, a_q_norm=a_q_norm, a_k_norm=a_k_norm, b_q_norm=b_q_norm, b_k_norm=b_k_norm, b_sinks=b_sinks, mem_norm=mem_norm, w_mem_kv=w_mem_kv, m_q_norm=m_q_norm, m_k_norm=m_k_norm, w_o_a=w_o_a, w_o_b=w_o_b, w_o_m=w_o_m, w_gate=w_gate, b_gate=b_gate, w_out=w_out, ffn_norm=ffn_norm, w_up=w_up, conv_w=conv_w, conv_b=conv_b, w_down=w_down)
    mom1 = dict(attn_norm=m_attn_norm, w_in=m_w_in, a_q_norm=m_a_q_norm, a_k_norm=m_a_k_norm, b_q_norm=m_b_q_norm, b_k_norm=m_b_k_norm, b_sinks=m_b_sinks, mem_norm=m_mem_norm, w_mem_kv=m_w_mem_kv, m_q_norm=m_m_q_norm, m_k_norm=m_m_k_norm, w_o_a=m_w_o_a, w_o_b=m_w_o_b, w_o_m=m_w_o_m, w_gate=m_w_gate, b_gate=m_b_gate, w_out=m_w_out, ffn_norm=m_ffn_norm, w_up=m_w_up, conv_w=m_conv_w, conv_b=m_conv_b, w_down=m_w_down)
    mom2 = dict(attn_norm=v_attn_norm, w_in=v_w_in, a_q_norm=v_a_q_norm, a_k_norm=v_a_k_norm, b_q_norm=v_b_q_norm, b_k_norm=v_b_k_norm, b_sinks=v_b_sinks, mem_norm=v_mem_norm, w_mem_kv=v_w_mem_kv, m_q_norm=v_m_q_norm, m_k_norm=v_m_k_norm, w_o_a=v_w_o_a, w_o_b=v_w_o_b, w_o_m=v_w_o_m, w_gate=v_w_gate, b_gate=v_b_gate, w_out=v_w_out, ffn_norm=v_ffn_norm, w_up=v_w_up, conv_w=v_conv_w, conv_b=v_conv_b, w_down=v_w_down)

    big = list(BIG)
    stages = {'mix': list(MIX_WEIGHTS), 'ffn': list(FFN_WEIGHTS), 'in': ['w_in']}
    my_slot = _slot(_coords())

    def shard(n):
        return given[n][0] if n == 'conv_w' else given[n][0].astype(BF16)

    def whole(n, g):
        _, r, c = g.shape
        return g.reshape(N_DEV * r, c) if BIG[n] == 0 else g.transpose(1, 0, 2).reshape(r, N_DEV * c)

    def to_blocks(n, g):
        r, c = given[n].shape[1:]
        g = g.reshape(N_DEV, r, c) if BIG[n] == 0 else g.reshape(r, N_DEV, c).transpose(1, 0, 2)
        return g if n == 'conv_w' else g.astype(BF16)

    class Hooks:
        next_stage = {'in': 'mix', 'mix': 'ffn'}

        def __init__(self):
            self.coming, self.sent = {}, {}
            self.shards = {n: shard(n) for n in big}
            self.start_gather('in', None)

        def start_gather(self, stage, after):
            src = [self.shards[n] for n in stages[stage]]
            self.coming[stage] = _exchange_start(src, f"gather_{stage}_start", gather=True, masks=CHIP_PEERS,
                                                 after=after)

        def weights(self, stage, after):
            names = stages[stage]
            after = list(after)
            if stage == 'in':
                after += [self.shards[n] for n in stages['mix'] + stages['ffn']]
            landed = _exchange_wait(self.coming[stage], after, f"gather_{stage}_wait", gather=True, masks=CHIP_PEERS)
            landed, token = _sibling_forward(landed, f"gather_{stage}_forward")
            if stage in self.next_stage:
                self.start_gather(self.next_stage[stage], token)
            return {n: whole(n, lax.dynamic_update_slice_in_dim(land, self.shards[n][None], my_slot, axis=0))
                    for n, land in zip(names, landed)}

        def grads(self, stage, g):
            blocks = [to_blocks(n, g[n]) for n in stages[stage]]
            own = [lax.dynamic_slice_in_dim(b, my_slot, 1, axis=0) for b in blocks]
            self.sent[stage] = (_exchange_start(blocks, f"exchange_{stage}_start"), own)
            return self.sent[stage][0][-1]

        def parts(self, stage, after):
            started, own = self.sent[stage]
            landed = _exchange_wait(started, [after], f"exchange_{stage}_wait")
            return {n: lax.dynamic_update_slice_in_dim(land, o, my_slot, axis=0)
                    for n, land, o in zip(stages[stage], landed, own)}

    hooks = Hooks()
    w = {}
    for n in SMALL:
        w[n] = given[n]
    w['a_q_norm'], w['a_k_norm'] = given['a_q_norm'][0], given['a_k_norm'][0]
    w['b_q_norm'], w['b_k_norm'], w['b_sinks'] = given['b_q_norm'][0], given['b_k_norm'][0], given['b_sinks'][0]

    loss_tile, grad_x, grads = _device_step(x[0], mem[0], positions[0], loss_target[0], w, hooks)
    out = {}
    after = grad_x
    for stage in ('ffn', 'mix', 'in'):
        for n, p in hooks.parts(stage, after).items():
            res = _adam_reduce(p, given[n][0], mom1[n][0], mom2[n][0], f"adam_{n}")
            out[n] = tuple(t[None] for t in res)
            after = res[0]

    small = {n: grads[n] for n in PACK}
    small['b_q_norm'], small['b_k_norm'] = grads['b_q_norm'].reshape(1, -1), grads['b_k_norm'].reshape(1, -1)
    small['b_sinks'] = grads['b_sinks'].reshape(1, -1)
    gsum = _all_sum(_pack_small(small, loss_tile, "pack_small"), "sum_small")
    ws = {n: _as2d(n, given[n]) for n in PACK}
    ms = {n: _as2d(n, mom1[n]) for n in PACK}
    vs = {n: _as2d(n, mom2[n]) for n in PACK}
    res = _adam_small(gsum, ws, ms, vs, "adam_small")
    loss = res[0].reshape(())
    for k, n in enumerate(PACK):
        out[n] = tuple(t.reshape(given[n].shape) for t in res[1 + 4 * k:5 + 4 * k])

    outs = [loss, grad_x[None]]
    for field in range(4):
        outs += [out[n][field] for n in WEIGHTS]
    return tuple(outs)
```

```python
import functools
import math

import jax
import jax.numpy as jnp
from jax import lax
from jax.experimental import pallas as pl
from jax.experimental.pallas import tpu as pltpu

F32 = jnp.float32
BF16 = jnp.bfloat16

N_DEV = 8
D_MODEL = 1024
HEAD_DIM = 64
A_GROUPS = ((128, 1), (512, 4), (2048, 16))
B_WINDOW = 128
M_HEADS = 4
M_HEAD_DIM = 128
MEM_LEN = 256
D_FF = 2816
ROPE_THETA = 500000.0
ROPE_DIMS = 16
BLOCK = 128
EPS = 1e-6
LANES = 128
BAND_Q_BLOCKS = 4
BAND_UNITS = 2

ADAM_LR = 0.001
ADAM_B1 = 0.9
ADAM_B2 = 0.999
ADAM_EPS = 1e-08
ADAM_WD = 0.01
ADAM_STEP = 10

VMEM_LIMIT_BYTES = 56 * 1024 * 1024
MESH = pl.DeviceIdType.MESH

WEIGHTS = ['attn_norm', 'w_in', 'a_q_norm', 'a_k_norm', 'b_q_norm', 'b_k_norm', 'b_sinks', 'mem_norm',
           'w_mem_kv', 'm_q_norm', 'm_k_norm', 'w_o_a', 'w_o_b', 'w_o_m', 'w_gate', 'b_gate', 'w_out',
           'ffn_norm', 'w_up', 'conv_w', 'conv_b', 'w_down']
BIG = {'w_in': 1, 'w_mem_kv': 0, 'w_o_a': 1, 'w_o_b': 1, 'w_o_m': 1, 'w_gate': 1, 'w_out': 0, 'w_up': 1,
       'conv_w': 1, 'w_down': 0}
SMALL = [n for n in WEIGHTS if n not in BIG]


def _cparams(n_grid):
    return pltpu.CompilerParams(dimension_semantics=("arbitrary",) * n_grid, vmem_limit_bytes=VMEM_LIMIT_BYTES)


def _seg_matrix(width):
    shift = width.bit_length() - 1
    r = lax.shift_right_logical(lax.broadcasted_iota(jnp.int32, (LANES, LANES), 0), shift)
    c = lax.shift_right_logical(lax.broadcasted_iota(jnp.int32, (LANES, LANES), 1), shift)
    return jnp.where(r == c, 1.0, 0.0).astype(BF16)


def _seg_sum(x, seg):
    hi = x.astype(BF16)
    r1 = x - hi.astype(F32)
    mid = r1.astype(BF16)
    lo = (r1 - mid.astype(F32)).astype(BF16)
    dot = functools.partial(jnp.dot, preferred_element_type=F32)
    return dot(hi, seg) + dot(mid, seg) + dot(lo, seg)


def _rope(y, c, s1, s2):
    return y * c + pltpu.roll(y, LANES - ROPE_DIMS // 2, 1) * s1 + pltpu.roll(y, ROPE_DIMS // 2, 1) * s2


def _unrope(dy, c, s1, s2):
    return dy * c + pltpu.roll(dy * s1, ROPE_DIMS // 2, 1) + pltpu.roll(dy * s2, LANES - ROPE_DIMS // 2, 1)


def _sigmoid(x):
    return 1.0 / (1.0 + jnp.exp(-x))


def _rms_fwd(x, gain, name):
    s_len, d = x.shape
    tm = 512

    def body(x_ref, g_ref, h_ref, ht_ref, r_ref):
        xv = x_ref[...]
        r = lax.rsqrt(jnp.mean(xv * xv, axis=-1, keepdims=True) + EPS)
        h = ((xv * r) * g_ref[...]).astype(BF16)
        h_ref[...] = h
        ht_ref[...] = h.T
        r_ref[...] = r

    return pl.pallas_call(
        body, name=name, grid=(s_len // tm,),
        in_specs=[pl.BlockSpec((tm, d), lambda i: (i, 0)), pl.BlockSpec((1, d), lambda i: (0, 0))],
        out_specs=(pl.BlockSpec((tm, d), lambda i: (i, 0)), pl.BlockSpec((d, tm), lambda i: (0, i)),
                   pl.BlockSpec((tm, 1), lambda i: (i, 0))),
        out_shape=(jax.ShapeDtypeStruct((s_len, d), BF16), jax.ShapeDtypeStruct((d, s_len), BF16),
                   jax.ShapeDtypeStruct((s_len, 1), F32)),
        compiler_params=_cparams(1),
    )(x, gain)


def _resident(shape, index_map):
    return pl.BlockSpec(shape, index_map, pipeline_mode=pl.Buffered(1))


def _mm_rows(pairs, name, nt=False, tm=512, bias=None, sigmoid=False, res=None, out_dtypes=(F32,), loss_target=None,
             rms_bwd=None):
    m = pairs[0][0].shape[0]
    n = pairs[0][1].shape[0] if nt else pairs[0][1].shape[1]
    n_pairs = len(pairs)
    has_bias, has_res, has_loss = bias is not None, res is not None, loss_target is not None
    has_rms = rms_bwd is not None
    dims = (((1,), (1,)), ((), ())) if nt else (((1,), (0,)), ((), ()))

    def body(*refs):
        acc = None
        for p in range(n_pairs):
            t = lax.dot_general(refs[2 * p][...].astype(BF16), refs[2 * p + 1][...], dims, preferred_element_type=F32)
            acc = t if acc is None else acc + t
        pos = 2 * n_pairs
        if has_bias:
            acc = acc + refs[pos][...]
            pos += 1
        if sigmoid:
            acc = _sigmoid(acc)
        if has_res:
            acc = refs[pos][...] + acc
            pos += 1
        if has_loss:
            dy_ref, dyb_ref, l_ref = refs[pos + 1:]

            @pl.when(pl.program_id(0) == 0)
            def _():
                l_ref[...] = jnp.zeros_like(l_ref)
            err = acc - refs[pos][...]
            dy = err * (1.0 / n)
            dy_ref[...] = dy
            dyb_ref[...] = dy.astype(BF16)
            part = 0.5 * jnp.sum(jnp.mean(err * err, axis=-1, keepdims=True), axis=0, keepdims=True)
            l_ref[...] += jnp.broadcast_to(part, l_ref.shape)
            return
        if has_rms:
            x_ref, r_ref, g_ref, add_ref = refs[pos:pos + 4]
            dg_ref = refs[-1]

            @pl.when(pl.program_id(0) == 0)
            def _():
                dg_ref[...] = jnp.zeros_like(dg_ref)
            rv = r_ref[...]
            xhat = x_ref[...] * rv
            dg_ref[...] += jnp.sum(acc * xhat, axis=0, keepdims=True)
            dxhat = acc * g_ref[...]
            acc = add_ref[...] + rv * (dxhat - xhat * jnp.mean(dxhat * xhat, axis=-1, keepdims=True))
            for o_ref in refs[pos + 4:-1]:
                o_ref[...] = acc.astype(o_ref.dtype)
            return
        for o_ref in refs[pos:]:
            o_ref[...] = acc.astype(o_ref.dtype)

    in_specs, args = [], []
    for a, w, blk in pairs:
        k = a.shape[1]
        in_specs.append(pl.BlockSpec((tm, k), lambda i: (i, 0)))
        if nt:
            in_specs.append(_resident((n, k), lambda i, blk=blk: (0, blk)))
        else:
            in_specs.append(_resident((k, n), lambda i, blk=blk: (blk, 0)))
        args += [a, w]
    if has_bias:
        in_specs.append(_resident((1, n), lambda i: (0, 0)))
        args.append(bias)
    if has_res:
        in_specs.append(pl.BlockSpec((tm, n), lambda i: (i, 0)))
        args.append(res)
    out = pl.BlockSpec((tm, n), lambda i: (i, 0))
    if has_loss:
        return pl.pallas_call(
            body, name=name, grid=(m // tm,), in_specs=in_specs + [out],
            out_specs=(out, out, pl.BlockSpec((8, LANES), lambda i: (0, 0))),
            out_shape=(jax.ShapeDtypeStruct((m, n), F32), jax.ShapeDtypeStruct((m, n), BF16),
                       jax.ShapeDtypeStruct((8, LANES), F32)),
            compiler_params=_cparams(1),
        )(*args, loss_target)
    if has_rms:
        x, r, gain, add = rms_bwd
        vec = _resident((1, n), lambda i: (0, 0))
        return pl.pallas_call(
            body, name=name, grid=(m // tm,),
            in_specs=in_specs + [out, pl.BlockSpec((tm, 1), lambda i: (i, 0)), vec, out],
            out_specs=tuple([out] * len(out_dtypes) + [pl.BlockSpec((1, n), lambda i: (0, 0))]),
            out_shape=tuple([jax.ShapeDtypeStruct((m, n), dt) for dt in out_dtypes] + [jax.ShapeDtypeStruct((1, n), F32)]),
            compiler_params=_cparams(1),
        )(*args, x, r, gain, add)
    outs = pl.pallas_call(
        body, name=name, grid=(m // tm,), in_specs=in_specs, out_specs=tuple([out] * len(out_dtypes)),
        out_shape=tuple(jax.ShapeDtypeStruct((m, n), dt) for dt in out_dtypes), compiler_params=_cparams(1),
    )(*args)
    return outs[0] if len(out_dtypes) == 1 else outs


def _mm_rows_cat(a, ws, name, tm=256):
    m, k = a.shape
    widths = [w.shape[1] for w in ws]
    n = sum(widths)

    def body(*refs):
        a_ref, o_ref = refs[0], refs[-1]
        av = a_ref[...]
        off = 0
        for p, width in enumerate(widths):
            o_ref[:, off:off + width] = jnp.dot(av, refs[1 + p][...], preferred_element_type=F32)
            off += width

    return pl.pallas_call(
        body, name=name, grid=(m // tm,),
        in_specs=[pl.BlockSpec((tm, k), lambda i: (i, 0))] + [_resident((k, wd), lambda i: (0, 0)) for wd in widths],
        out_specs=pl.BlockSpec((tm, n), lambda i: (i, 0)),
        out_shape=jax.ShapeDtypeStruct((m, n), F32), compiler_params=_cparams(1),
    )(a, *ws)


def _mm_cols(a, b, name, tn=256):
    m, k = a.shape
    n = b.shape[1]

    def body(a_ref, b_ref, o_ref):
        o_ref[...] = jnp.dot(a_ref[...], b_ref[...].astype(BF16), preferred_element_type=F32)

    return pl.pallas_call(
        body, name=name, grid=(n // tn,),
        in_specs=[_resident((m, k), lambda j: (0, 0)), pl.BlockSpec((k, tn), lambda j: (0, j))],
        out_specs=pl.BlockSpec((m, tn), lambda j: (0, j)),
        out_shape=jax.ShapeDtypeStruct((m, n), F32), compiler_params=_cparams(1),
    )(a, b)


def _mm_tn(a, b, name, tile=256):
    k, m = a.shape
    n = b.shape[1]
    dims = (((0,), (0,)), ((), ()))

    def body(a_ref, b_ref, o_ref):
        o_ref[...] = lax.dot_general(a_ref[...].astype(BF16), b_ref[...].astype(BF16), dims, preferred_element_type=F32)

    if n <= m:
        t = min(tile, m)
        grid, a_spec, b_spec = (m // t,), pl.BlockSpec((k, t), lambda i: (0, i)), _resident((k, n), lambda i: (0, 0))
        o_spec = pl.BlockSpec((t, n), lambda i: (i, 0))
    else:
        t = min(tile, n)
        grid, a_spec, b_spec = (n // t,), _resident((k, m), lambda i: (0, 0)), pl.BlockSpec((k, t), lambda i: (0, i))
        o_spec = pl.BlockSpec((m, t), lambda i: (0, i))
    return pl.pallas_call(
        body, name=name, grid=grid, in_specs=[a_spec, b_spec], out_specs=o_spec,
        out_shape=jax.ShapeDtypeStruct((m, n), F32), compiler_params=_cparams(1),
    )(a, b)


def _norm_rope(t, gain, c, s1, s2, seg):
    rs = lax.rsqrt(_seg_sum(t * t, seg) * (1.0 / HEAD_DIM) + EPS)
    return _rope((t * rs) * gain, c, s1, s2)


def _dup_half(y, half):
    lane = lax.broadcasted_iota(jnp.int32, y.shape, 1)
    rolled = pltpu.roll(y, HEAD_DIM, 1)
    keep = (lane < HEAD_DIM) if half == 0 else (lane >= HEAD_DIM)
    return jnp.where(keep, y, rolled)


def _qk_prep(proj, cb0, d, gqa, gq, gk, tabs, name):
    s_len = proj.shape[0]
    tm = 512
    rows = tm // d
    n_units = 4 if gqa else 2 * d
    n_q = 4 if gqa else 2
    n_in = 6

    def body(*refs):
        in_refs = refs[:n_in]
        gq_ref, gk_ref, c_ref, s1_ref, s2_ref, o_ref = refs[n_in:]
        seg = _seg_matrix(HEAD_DIM)

        def rows_of(ref, r):
            return ref[...] if d == 1 else ref[pl.ds(r, rows, stride=d), :]

        def put(unit_col, y):
            o_ref[:, unit_col * LANES:(unit_col + 1) * LANES] = y.astype(BF16)

        for r in range(d):
            c, s1, s2 = rows_of(c_ref, r), rows_of(s1_ref, r), rows_of(s2_ref, r)
            for b in range(n_in):
                t = rows_of(in_refs[b], r)
                if b < n_q:
                    put((b * d + r) if not gqa else b, _norm_rope(t, gq_ref[...], c, s1, s2, seg))
                elif not gqa:
                    sec, pair = (1, b - 2) if b < 4 else (2, b - 4)
                    y = _norm_rope(t, gk_ref[...], c, s1, s2, seg) if sec == 1 else t
                    put(sec * n_units + pair * d + r, y)
                else:
                    sec = 1 if b == 4 else 2
                    y = _norm_rope(t, gk_ref[...], c, s1, s2, seg) if sec == 1 else t
                    for u in range(n_units):
                        put(sec * n_units + u, _dup_half(y, u // 2))

    in_specs = [pl.BlockSpec((tm, LANES), lambda i, b=b: (i, cb0 + b)) for b in range(n_in)]
    vec = pl.BlockSpec((1, LANES), lambda i: (0, 0))
    tab = pl.BlockSpec((tm, LANES), lambda i: (i, 0))
    width = 3 * n_units * LANES
    return pl.pallas_call(
        body, name=name, grid=(s_len // tm,), in_specs=in_specs + [vec, vec, tab, tab, tab],
        out_specs=pl.BlockSpec((rows, width), lambda i: (i, 0)),
        out_shape=jax.ShapeDtypeStruct((s_len // d, width), BF16), compiler_params=_cparams(1),
    )(*([proj] * n_in), gq, gk, *tabs)


def _qk_prep_bwd(dqkv, proj, cb0, d, gqa, gq, gk, tabs, name):
    s_len = proj.shape[0]
    tm = 512
    rows = tm // d
    n_units = 4 if gqa else 2 * d
    n_q = 4 if gqa else 2
    n_in = 6

    def body(*refs):
        d_refs = refs[0:3]
        in_refs = refs[3:3 + n_in]
        gq_ref, gk_ref, c_ref, s1_ref, s2_ref, o_ref, dgq_ref, dgk_ref, stage = refs[3 + n_in:]
        seg = _seg_matrix(HEAD_DIM)

        @pl.when(pl.program_id(0) == 0)
        def _():
            dgq_ref[...] = jnp.zeros_like(dgq_ref)
            dgk_ref[...] = jnp.zeros_like(dgk_ref)

        def rows_of(ref, r):
            return ref[...] if d == 1 else ref[pl.ds(r, rows, stride=d), :]

        def unit(col):
            sec, u = divmod(col, n_units)
            return d_refs[sec][:, u * LANES:(u + 1) * LANES]

        def norm_bwd(dyr, t, gain, c, s1, s2, dg_ref):
            rs = lax.rsqrt(_seg_sum(t * t, seg) * (1.0 / HEAD_DIM) + EPS)
            that = t * rs
            dy = _unrope(dyr, c, s1, s2)
            dg_ref[...] += jnp.sum(dy * that, axis=0, keepdims=True)
            dthat = dy * gain
            return rs * (dthat - that * (_seg_sum(dthat * that, seg) * (1.0 / HEAD_DIM)))

        def fold(sec):
            tot = []
            for u in range(n_units):
                v = unit(sec * n_units + u)
                tot.append(v + pltpu.roll(v, HEAD_DIM, 1))
            lane = lax.broadcasted_iota(jnp.int32, tot[0].shape, 1)
            return jnp.where(lane < HEAD_DIM, tot[0] + tot[1], tot[2] + tot[3])

        for b in range(n_in):
            for r in range(d):
                c, s1, s2 = rows_of(c_ref, r), rows_of(s1_ref, r), rows_of(s2_ref, r)
                t = rows_of(in_refs[b], r)
                if b < n_q:
                    g = unit((b * d + r) if not gqa else b)
                    out = norm_bwd(g, t, gq_ref[...], c, s1, s2, dgq_ref)
                elif not gqa:
                    sec, pair = (1, b - 2) if b < 4 else (2, b - 4)
                    g = unit(sec * n_units + pair * d + r)
                    out = norm_bwd(g, t, gk_ref[...], c, s1, s2, dgk_ref) if sec == 1 else g
                else:
                    sec = 1 if b == 4 else 2
                    g = fold(sec)
                    out = norm_bwd(g, t, gk_ref[...], c, s1, s2, dgk_ref) if sec == 1 else g
                if d == 1:
                    o_ref[:, b * LANES:(b + 1) * LANES] = out.astype(BF16)
                else:
                    stage[pl.ds(r, rows, stride=d), :] = out
            if d != 1:
                o_ref[:, b * LANES:(b + 1) * LANES] = stage[...].astype(BF16)

    in_specs = [pl.BlockSpec((rows, n_units * LANES), lambda i: (i, 0))] * 3
    in_specs += [pl.BlockSpec((tm, LANES), lambda i, b=b: (i, cb0 + b)) for b in range(n_in)]
    vec = pl.BlockSpec((1, LANES), lambda i: (0, 0))
    tab = pl.BlockSpec((tm, LANES), lambda i: (i, 0))
    return pl.pallas_call(
        body, name=name, grid=(s_len // tm,), in_specs=in_specs + [vec, vec, tab, tab, tab],
        out_specs=(pl.BlockSpec((tm, n_in * LANES), lambda i: (i, 0)), vec, vec),
        out_shape=(jax.ShapeDtypeStruct((s_len, n_in * LANES), BF16), jax.ShapeDtypeStruct((1, LANES), F32),
                   jax.ShapeDtypeStruct((1, LANES), F32)),
        scratch_shapes=[pltpu.VMEM((tm, LANES), F32)], compiler_params=_cparams(1),
    )(*dqkv, *([proj] * n_in), gq, gk, *tabs)


def _head_masks(shape):
    lane = lax.broadcasted_iota(jnp.int32, shape, 1)
    return lane < HEAD_DIM, lane >= HEAD_DIM


def _band_fwd(qkv, n_units, max_dist, sinks, name):
    n_rows = qkv.shape[0]
    nb = n_rows // BLOCK
    scale = HEAD_DIM ** -0.5
    has_sink = sinks is not None
    assert not has_sink or max_dist < BLOCK

    qn, un = min(nb, BAND_Q_BLOCKS), BAND_UNITS
    ug = n_units // un

    def body(*refs):
        q_ref, kp_ref, km_ref, vp_ref, vm_ref = refs[:5]
        o_ref, lse_ref = refs[-2:]
        i = pl.program_id(1)
        qi = lax.broadcasted_iota(jnp.int32, (BLOCK, 2 * BLOCK), 0)
        kj = lax.broadcasted_iota(jnp.int32, (BLOCK, 2 * BLOCK), 1)
        dist = qi + BLOCK - kj
        band = (dist >= 0) & (dist <= max_dist)
        band_first = band & ((i > 0) | (kj >= BLOCK))
        m0, m1 = _head_masks((BLOCK, LANES))
        zero = jnp.zeros((BLOCK, LANES), BF16)
        for ub in range(un):
            cs = slice(ub * LANES, (ub + 1) * LANES)
            for qb in range(qn):
                rs = slice(qb * BLOCK, (qb + 1) * BLOCK)
                q = q_ref[rs, cs]
                if qb == 0:
                    kk = jnp.concatenate([kp_ref[:, cs], km_ref[0:BLOCK, cs]], axis=0)
                    vv = jnp.concatenate([vp_ref[:, cs], vm_ref[0:BLOCK, cs]], axis=0)
                    valid = band_first
                else:
                    kk = km_ref[(qb - 1) * BLOCK:(qb + 1) * BLOCK, cs]
                    vv = vm_ref[(qb - 1) * BLOCK:(qb + 1) * BLOCK, cs]
                    valid = band
                outs, lses = [], []
                for e, hm in enumerate((m0, m1)):
                    qe = jnp.where(hm, q, zero)
                    s = lax.dot_general(qe, kk, (((1,), (1,)), ((), ())), preferred_element_type=F32) * scale
                    s = jnp.where(valid, s, -jnp.inf)
                    if has_sink:
                        s = jnp.where(kj == 0, refs[5][ub][:, e * HEAD_DIM:e * HEAD_DIM + 1], s)
                    mx = jnp.max(s, axis=-1, keepdims=True)
                    p = jnp.exp(s - mx)
                    den = jnp.sum(p, axis=-1, keepdims=True)
                    pn = p * (1.0 / den)
                    if has_sink:
                        pn = jnp.where(kj == 0, 0.0, pn)
                    pn = pn.astype(BF16)
                    outs.append(jnp.dot(pn, vv, preferred_element_type=F32))
                    lses.append(mx + jnp.log(den))
                o_ref[rs, cs] = jnp.where(m0, outs[0], outs[1])
                lse_ref[rs, cs] = jnp.where(m0, jnp.broadcast_to(lses[0], (BLOCK, LANES)),
                                            jnp.broadcast_to(lses[1], (BLOCK, LANES)))

    def main(sec):
        return pl.BlockSpec((qn * BLOCK, un * LANES), lambda u, i: (i, sec * ug + u))

    def prev(sec):
        return pl.BlockSpec((BLOCK, un * LANES), lambda u, i: (jnp.maximum(i * qn - 1, 0), sec * ug + u))

    in_specs = [main(0), prev(1), main(1), prev(2), main(2)]
    args = [qkv] * 5
    if has_sink:
        in_specs.append(pl.BlockSpec((un, 1, LANES), lambda u, i: (u, 0, 0)))
        args.append(sinks)
    return pl.pallas_call(
        body, name=name, grid=(ug, nb // qn), in_specs=in_specs, out_specs=(main(0), main(0)),
        out_shape=(jax.ShapeDtypeStruct((n_rows, n_units * LANES), F32),) * 2, compiler_params=_cparams(2),
    )(*args)


def _band_bwd(qkv, do, lse, delta, n_units, max_dist, name):
    n_rows = qkv.shape[0]
    nb = n_rows // BLOCK
    scale = HEAD_DIM ** -0.5

    qn, un = min(nb, BAND_Q_BLOCKS), BAND_UNITS
    ug = n_units // un
    steps = nb // qn
    nt_dims = (((1,), (1,)), ((), ()))
    tn_dims = (((0,), (0,)), ((), ()))

    def body(qm_ref, qx_ref, kp_ref, km_ref, vp_ref, vm_ref, dom_ref, dox_ref, lm_ref, lx_ref, dm_ref, dx_ref,
             dq_ref, dk_ref, dv_ref):
        i = pl.program_id(1)
        m0, m1 = _head_masks((BLOCK, LANES))
        zero = jnp.zeros((BLOCK, LANES), BF16)
        qi = lax.broadcasted_iota(jnp.int32, (BLOCK, 2 * BLOCK), 0)
        kj = lax.broadcasted_iota(jnp.int32, (BLOCK, 2 * BLOCK), 1)
        dist = qi + BLOCK - kj
        band = (dist >= 0) & (dist <= max_dist)
        band_first = band & ((i > 0) | (kj >= BLOCK))
        qr = lax.broadcasted_iota(jnp.int32, (BLOCK, BLOCK), 0)
        kc = lax.broadcasted_iota(jnp.int32, (BLOCK, BLOCK), 1)
        dist_x = qr + BLOCK - kc
        band_next = (dist_x >= 0) & (dist_x <= max_dist) & (i < steps - 1)

        def pair(q, dob, lse_b, del_b, kk, vv, valid):
            dqs, dk, dv = [], None, None
            for e, hm in enumerate((m0, m1)):
                col = slice(e * HEAD_DIM, e * HEAD_DIM + 1)
                qe = jnp.where(hm, q, zero)
                doe = jnp.where(hm, dob, zero)
                s = lax.dot_general(qe, kk, nt_dims, preferred_element_type=F32) * scale
                p = jnp.where(valid, jnp.exp(s - lse_b[:, col]), 0.0)
                dp = lax.dot_general(doe, vv, nt_dims, preferred_element_type=F32)
                ds = (p * (dp - del_b[:, col]) * scale).astype(BF16)
                dqs.append(jnp.dot(ds, kk, preferred_element_type=F32))
                dk_e = lax.dot_general(ds, qe, tn_dims, preferred_element_type=F32)
                dv_e = lax.dot_general(p.astype(BF16), doe, tn_dims, preferred_element_type=F32)
                dk = dk_e if dk is None else dk + dk_e
                dv = dv_e if dv is None else dv + dv_e
            return jnp.where(m0, dqs[0], dqs[1]), dk, dv

        for ub in range(un):
            cs = slice(ub * LANES, (ub + 1) * LANES)
            dk_acc, dv_acc = [None] * qn, [None] * qn

            def add(acc, kb, part):
                acc[kb] = part if acc[kb] is None else acc[kb] + part

            for qb in range(qn):
                rs = slice(qb * BLOCK, (qb + 1) * BLOCK)
                if qb == 0:
                    kk = jnp.concatenate([kp_ref[:, cs], km_ref[0:BLOCK, cs]], axis=0)
                    vv = jnp.concatenate([vp_ref[:, cs], vm_ref[0:BLOCK, cs]], axis=0)
                    valid = band_first
                else:
                    kk = km_ref[(qb - 1) * BLOCK:(qb + 1) * BLOCK, cs]
                    vv = vm_ref[(qb - 1) * BLOCK:(qb + 1) * BLOCK, cs]
                    valid = band
                dq, dk, dv = pair(qm_ref[rs, cs], dom_ref[rs, cs], lm_ref[rs, cs], dm_ref[rs, cs], kk, vv, valid)
                dq_ref[rs, cs] = dq
                if qb > 0:
                    add(dk_acc, qb - 1, dk[0:BLOCK])
                    add(dv_acc, qb - 1, dv[0:BLOCK])
                add(dk_acc, qb, dk[BLOCK:2 * BLOCK])
                add(dv_acc, qb, dv[BLOCK:2 * BLOCK])
            last = slice((qn - 1) * BLOCK, qn * BLOCK)
            _, dk, dv = pair(qx_ref[:, cs], dox_ref[:, cs], lx_ref[:, cs], dx_ref[:, cs], km_ref[last, cs], vm_ref[last, cs],
                             band_next)
            add(dk_acc, qn - 1, dk)
            add(dv_acc, qn - 1, dv)
            for kb in range(qn):
                dk_ref[kb * BLOCK:(kb + 1) * BLOCK, cs] = dk_acc[kb]
                dv_ref[kb * BLOCK:(kb + 1) * BLOCK, cs] = dv_acc[kb]

    def main(sec):
        return pl.BlockSpec((qn * BLOCK, un * LANES), lambda u, i: (i, sec * ug + u))

    def prev(sec):
        return pl.BlockSpec((BLOCK, un * LANES), lambda u, i: (jnp.maximum(i * qn - 1, 0), sec * ug + u))

    def nxt(sec):
        return pl.BlockSpec((BLOCK, un * LANES), lambda u, i: (jnp.minimum((i + 1) * qn, nb - 1), sec * ug + u))

    in_specs = [main(0), nxt(0), prev(1), main(1), prev(2), main(2),
                main(0), nxt(0), main(0), nxt(0), main(0), nxt(0)]
    args = [qkv] * 6 + [do, do, lse, lse, delta, delta]
    shp = jax.ShapeDtypeStruct((n_rows, n_units * LANES), F32)
    return pl.pallas_call(
        body, name=name, grid=(ug, steps), in_specs=in_specs, out_specs=(main(0), main(0), main(0)),
        out_shape=(shp, shp, shp), compiler_params=_cparams(2),
    )(*args)


def _merge_groups(os_, lses, dils, name):
    s_len = os_[0].shape[0] * dils[0]
    tm = 512

    def body(*refs):
        o_refs, l_refs = refs[0:3], refs[3:6]
        o_ref, lse_ref = refs[6:8]
        so, sl = refs[8:11], refs[11:14]
        for pair in range(2):
            for g, d in enumerate(dils):
                rows = tm // d
                for r in range(d):
                    col = slice((pair * d + r) * LANES, (pair * d + r + 1) * LANES)
                    if d == 1:
                        so[g][...] = o_refs[g][:, col]
                        sl[g][...] = l_refs[g][:, col]
                    else:
                        so[g][pl.ds(r, rows, stride=d), :] = o_refs[g][:, col]
                        sl[g][pl.ds(r, rows, stride=d), :] = l_refs[g][:, col]
            l0, l1, l2 = sl[0][...], sl[1][...], sl[2][...]
            mx = jnp.maximum(jnp.maximum(l0, l1), l2)
            e0, e1, e2 = jnp.exp(l0 - mx), jnp.exp(l1 - mx), jnp.exp(l2 - mx)
            den = e0 + e1 + e2
            inv = 1.0 / den
            o_ref[:, pair * LANES:(pair + 1) * LANES] = (so[0][...] * (e0 * inv) + so[1][...] * (e1 * inv)
                                                         + so[2][...] * (e2 * inv))
            lse_ref[:, pair * LANES:(pair + 1) * LANES] = mx + jnp.log(den)

    in_specs = [pl.BlockSpec((tm // d, 2 * d * LANES), lambda i: (i, 0)) for d in dils] * 2
    out = pl.BlockSpec((tm, 2 * LANES), lambda i: (i, 0))
    shp = jax.ShapeDtypeStruct((s_len, 2 * LANES), F32)
    return pl.pallas_call(
        body, name=name, grid=(s_len // tm,), in_specs=in_specs, out_specs=(out, out), out_shape=(shp, shp),
        scratch_shapes=[pltpu.VMEM((tm, LANES), F32)] * 6, compiler_params=_cparams(1),
    )(*os_, *lses)


def _bwd_prep(do, o, lse, dils, sinks, name):
    s_len, width = do.shape
    n_pairs = width // LANES
    tm = 512
    has_sink = sinks is not None
    n_g = len(dils)

    def body(*refs):
        do_ref, o_ref, lse_ref = refs[:3]
        pos = 3
        if has_sink:
            sink_ref = refs[pos]
            pos += 1
        outs = refs[pos:pos + 3 * n_g]
        pos += 3 * n_g
        if has_sink:
            dsink_ref = refs[pos]
            pos += 1
        s_do, s_l, s_d = refs[pos:pos + 3]
        seg = _seg_matrix(HEAD_DIM)

        if has_sink:
            @pl.when(pl.program_id(0) == 0)
            def _():
                dsink_ref[...] = jnp.zeros_like(dsink_ref)

        for pair in range(n_pairs):
            col = slice(pair * LANES, (pair + 1) * LANES)
            dov = do_ref[:, col]
            lv = lse_ref[:, col]
            delta = _seg_sum(dov * o_ref[:, col], seg)
            if has_sink:
                dsink_ref[pair] += -jnp.sum(jnp.exp(sink_ref[pair] - lv) * delta, axis=0, keepdims=True)
            s_do[...] = dov
            s_l[...] = lv
            s_d[...] = delta
            for g, d in enumerate(dils):
                rows = tm // d
                for r in range(d):
                    oc = slice((pair * d + r) * LANES, (pair * d + r + 1) * LANES)
                    if d == 1:
                        a, b, c = s_do[...], s_l[...], s_d[...]
                    else:
                        a = s_do[pl.ds(r, rows, stride=d), :]
                        b = s_l[pl.ds(r, rows, stride=d), :]
                        c = s_d[pl.ds(r, rows, stride=d), :]
                    outs[3 * g][:, oc] = a.astype(BF16)
                    outs[3 * g + 1][:, oc] = b
                    outs[3 * g + 2][:, oc] = c

    row = pl.BlockSpec((tm, width), lambda i: (i, 0))
    in_specs = [row, row, row]
    args = [do, o, lse]
    if has_sink:
        in_specs.append(pl.BlockSpec((n_pairs, 1, LANES), lambda i: (0, 0, 0)))
        args.append(sinks)
    out_specs, out_shape = [], []
    for d in dils:
        for dt in (BF16, F32, F32):
            out_specs.append(pl.BlockSpec((tm // d, n_pairs * d * LANES), lambda i: (i, 0)))
            out_shape.append(jax.ShapeDtypeStruct((s_len // d, n_pairs * d * LANES), dt))
    if has_sink:
        out_specs.append(pl.BlockSpec((n_pairs, 1, LANES), lambda i: (0, 0, 0)))
        out_shape.append(jax.ShapeDtypeStruct((n_pairs, 1, LANES), F32))
    return pl.pallas_call(
        body, name=name, grid=(s_len // tm,), in_specs=in_specs, out_specs=tuple(out_specs),
        out_shape=tuple(out_shape), scratch_shapes=[pltpu.VMEM((tm, LANES), F32)] * 3, compiler_params=_cparams(1),
    )(*args)


def _mem_kv(mem, mem_gain, w_kv, k_gain, name):
    m_len = mem.shape[0]
    kw = M_HEADS * M_HEAD_DIM

    def body(mem_ref, mg_ref, w_ref, kg_ref, k_ref, v_ref):
        mv = mem_ref[...]
        r = lax.rsqrt(jnp.mean(mv * mv, axis=-1, keepdims=True) + EPS)
        mn = ((mv * r) * mg_ref[...]).astype(BF16)
        kv = jnp.dot(mn, w_ref[...], preferred_element_type=F32)
        for h in range(M_HEADS):
            col = slice(h * M_HEAD_DIM, (h + 1) * M_HEAD_DIM)
            t = kv[:, col]
            rk = lax.rsqrt(jnp.mean(t * t, axis=-1, keepdims=True) + EPS)
            k_ref[:, col] = ((t * rk) * kg_ref[...]).astype(BF16)
        v_ref[...] = kv[:, kw:].astype(BF16)

    shp = jax.ShapeDtypeStruct((m_len, kw), BF16)
    return pl.pallas_call(body, name=name, out_shape=(shp, shp),
                          compiler_params=pltpu.CompilerParams(vmem_limit_bytes=VMEM_LIMIT_BYTES))(mem, mem_gain, w_kv, k_gain)


def _mem_kv_bwd(mem, mem_gain, w_kv, k_gain, dk, dv, name):
    m_len, d = mem.shape
    kw = M_HEADS * M_HEAD_DIM

    def body(mem_ref, mg_ref, w_ref, kg_ref, dk_ref, dv_ref, dw_ref, dmg_ref, dkg_ref, dkv_ref):
        mv = mem_ref[...]
        r = lax.rsqrt(jnp.mean(mv * mv, axis=-1, keepdims=True) + EPS)
        mhat = mv * r
        mn = (mhat * mg_ref[...]).astype(BF16)
        kv = jnp.dot(mn, w_ref[...], preferred_element_type=F32)
        dkg = jnp.zeros((1, M_HEAD_DIM), F32)
        for h in range(M_HEADS):
            col = slice(h * M_HEAD_DIM, (h + 1) * M_HEAD_DIM)
            t = kv[:, col]
            rk = lax.rsqrt(jnp.mean(t * t, axis=-1, keepdims=True) + EPS)
            that = t * rk
            dy = dk_ref[:, col]
            dkg = dkg + jnp.sum(dy * that, axis=0, keepdims=True)
            dthat = dy * kg_ref[...]
            dkv_ref[:, col] = (rk * (dthat - that * jnp.mean(dthat * that, axis=-1, keepdims=True))).astype(BF16)
        dkv_ref[:, kw:] = dv_ref[...].astype(BF16)
        dkg_ref[...] = dkg
        dkv = dkv_ref[...]
        dw_ref[...] = lax.dot_general(mn, dkv, (((0,), (0,)), ((), ())), preferred_element_type=F32)
        dmn = lax.dot_general(dkv, w_ref[...], (((1,), (1,)), ((), ())), preferred_element_type=F32)
        dmg_ref[...] = jnp.sum(dmn * mhat, axis=0, keepdims=True)

    return pl.pallas_call(
        body, name=name,
        out_shape=(jax.ShapeDtypeStruct((d, 2 * kw), F32), jax.ShapeDtypeStruct((1, d), F32),
                   jax.ShapeDtypeStruct((1, M_HEAD_DIM), F32)),
        scratch_shapes=[pltpu.VMEM((m_len, 2 * kw), BF16)],
        compiler_params=pltpu.CompilerParams(vmem_limit_bytes=VMEM_LIMIT_BYTES),
    )(mem, mem_gain, w_kv, k_gain, dk, dv)


def _mem_attn_fwd(proj, cidx, mk, mv, q_gain, name):
    s_len = proj.shape[0]
    kw = M_HEADS * M_HEAD_DIM
    tm = 512
    scale = M_HEAD_DIM ** -0.5

    def body(q_ref, k_ref, v_ref, g_ref, o_ref):
        for h in range(M_HEADS):
            col = slice(h * M_HEAD_DIM, (h + 1) * M_HEAD_DIM)
            t = q_ref[:, col]
            rs = lax.rsqrt(jnp.mean(t * t, axis=-1, keepdims=True) + EPS)
            qn = ((t * rs) * g_ref[...]).astype(BF16)
            s = lax.dot_general(qn, k_ref[:, col], (((1,), (1,)), ((), ())), preferred_element_type=F32) * scale
            mx = jnp.max(s, axis=-1, keepdims=True)
            p = jnp.exp(s - mx)
            pn = (p * (1.0 / jnp.sum(p, axis=-1, keepdims=True))).astype(BF16)
            o_ref[:, col] = jnp.dot(pn, v_ref[:, col], preferred_element_type=F32).astype(BF16)

    whole = pl.BlockSpec((MEM_LEN, kw), lambda i: (0, 0))
    return pl.pallas_call(
        body, name=name, grid=(s_len // tm,),
        in_specs=[pl.BlockSpec((tm, kw), lambda i: (i, cidx)), whole, whole, pl.BlockSpec((1, M_HEAD_DIM), lambda i: (0, 0))],
        out_specs=pl.BlockSpec((tm, kw), lambda i: (i, 0)),
        out_shape=jax.ShapeDtypeStruct((s_len, kw), BF16), compiler_params=_cparams(1),
    )(proj, mk, mv, q_gain)


def _mem_attn_bwd(proj, cidx, mk, mv, q_gain, do, name):
    s_len = proj.shape[0]
    kw = M_HEADS * M_HEAD_DIM
    tm = 512
    scale = M_HEAD_DIM ** -0.5

    def body(q_ref, k_ref, v_ref, g_ref, do_ref, dq_ref, dk_ref, dv_ref, dg_ref):
        @pl.when(pl.program_id(0) == 0)
        def _():
            dk_ref[...] = jnp.zeros_like(dk_ref)
            dv_ref[...] = jnp.zeros_like(dv_ref)
            dg_ref[...] = jnp.zeros_like(dg_ref)

        for h in range(M_HEADS):
            col = slice(h * M_HEAD_DIM, (h + 1) * M_HEAD_DIM)
            t = q_ref[:, col]
            rs = lax.rsqrt(jnp.mean(t * t, axis=-1, keepdims=True) + EPS)
            that = t * rs
            qn = (that * g_ref[...]).astype(BF16)
            kh, vh = k_ref[:, col], v_ref[:, col]
            dob = do_ref[:, col].astype(BF16)
            s = lax.dot_general(qn, kh, (((1,), (1,)), ((), ())), preferred_element_type=F32) * scale
            mx = jnp.max(s, axis=-1, keepdims=True)
            p = jnp.exp(s - mx)
            p = p * (1.0 / jnp.sum(p, axis=-1, keepdims=True))
            dp = lax.dot_general(dob, vh, (((1,), (1,)), ((), ())), preferred_element_type=F32)
            ds = (p * (dp - jnp.sum(p * dp, axis=-1, keepdims=True)) * scale).astype(BF16)
            dqn = jnp.dot(ds, kh, preferred_element_type=F32)
            dk_ref[:, col] += lax.dot_general(ds, qn, (((0,), (0,)), ((), ())), preferred_element_type=F32)
            dv_ref[:, col] += lax.dot_general(p.astype(BF16), dob, (((0,), (0,)), ((), ())), preferred_element_type=F32)
            dg_ref[...] += jnp.sum(dqn * that, axis=0, keepdims=True)
            dthat = dqn * g_ref[...]
            dq_ref[:, col] = (rs * (dthat - that * jnp.mean(dthat * that, axis=-1, keepdims=True))).astype(BF16)

    whole = pl.BlockSpec((MEM_LEN, kw), lambda i: (0, 0))
    vec = pl.BlockSpec((1, M_HEAD_DIM), lambda i: (0, 0))
    row = pl.BlockSpec((tm, kw), lambda i: (i, 0))
    return pl.pallas_call(
        body, name=name, grid=(s_len // tm,),
        in_specs=[pl.BlockSpec((tm, kw), lambda i: (i, cidx)), whole, whole, vec, row],
        out_specs=(row, whole, whole, vec),
        out_shape=(jax.ShapeDtypeStruct((s_len, kw), BF16), jax.ShapeDtypeStruct((MEM_LEN, kw), F32),
                   jax.ShapeDtypeStruct((MEM_LEN, kw), F32), jax.ShapeDtypeStruct((1, M_HEAD_DIM), F32)),
        compiler_params=_cparams(1),
    )(proj, mk, mv, q_gain, do)


def _gate_merge(gates, pa, pb, pm, name):
    s_len, d = pa.shape
    tm = 256

    def body(g_ref, a_ref, b_ref, m_ref, o_ref):
        f = lambda v: v.astype(F32)
        o_ref[...] = (f(g_ref[:, 0:d]) * f(a_ref[...]) + f(g_ref[:, d:2 * d]) * f(b_ref[...])
                      + f(g_ref[:, 2 * d:3 * d]) * f(m_ref[...])).astype(BF16)

    row = pl.BlockSpec((tm, d), lambda i: (i, 0))
    return pl.pallas_call(
        body, name=name, grid=(s_len // tm,), in_specs=[pl.BlockSpec((tm, 3 * d), lambda i: (i, 0)), row, row, row],
        out_specs=row, out_shape=jax.ShapeDtypeStruct((s_len, d), BF16), compiler_params=_cparams(1),
    )(gates, pa, pb, pm)


def _gate_merge_bwd(dmerged, gates, pa, pb, pm, name):
    s_len, d = pa.shape
    tm = 256

    def body(dm_ref, g_ref, a_ref, b_ref, m_ref, da_ref, db_ref, dmm_ref, dg_ref, dbg_ref):
        @pl.when(pl.program_id(0) == 0)
        def _():
            dbg_ref[...] = jnp.zeros_like(dbg_ref)
        dm = dm_ref[...]
        for k, (p_ref, dp_ref) in enumerate(((a_ref, da_ref), (b_ref, db_ref), (m_ref, dmm_ref))):
            col = slice(k * d, (k + 1) * d)
            g = g_ref[:, col].astype(F32)
            dp_ref[...] = (dm * g).astype(BF16)
            dpre = (dm * p_ref[...].astype(F32)) * (g * (1.0 - g))
            dbg_ref[:, col] += jnp.sum(dpre, axis=0, keepdims=True)
            dg_ref[:, col] = dpre.astype(BF16)

    row = pl.BlockSpec((tm, d), lambda i: (i, 0))
    wide = pl.BlockSpec((tm, 3 * d), lambda i: (i, 0))
    shp = jax.ShapeDtypeStruct((s_len, d), BF16)
    return pl.pallas_call(
        body, name=name, grid=(s_len // tm,), in_specs=[row, wide, row, row, row],
        out_specs=(row, row, row, wide, pl.BlockSpec((1, 3 * d), lambda i: (0, 0))),
        out_shape=(shp, shp, shp, jax.ShapeDtypeStruct((s_len, 3 * d), BF16), jax.ShapeDtypeStruct((1, 3 * d), F32)),
        compiler_params=_cparams(1),
    )(dmerged, gates, pa, pb, pm)


CONV_CHUNK = 256


def _pick_row(tile, j):
    row = lax.broadcasted_iota(jnp.int32, tile.shape, 0)
    return jnp.sum(jnp.where(row == j, tile, jnp.zeros_like(tile)), axis=0, keepdims=True)


def _rows_before(ref, start, k):
    cur = ref[pl.ds(start, CONV_CHUNK), :].astype(F32)
    prev = ref[pl.ds(pl.multiple_of(jnp.maximum(start - 16, 0), 16), 16), :].astype(F32)
    prev = jnp.where(start > 0, prev, jnp.zeros_like(prev))
    rolled = pltpu.roll(cur, k, 0)
    row = lax.broadcasted_iota(jnp.int32, cur.shape, 0)
    for j in range(k):
        rolled = jnp.where(row == j, _pick_row(prev, 16 - k + j), rolled)
    return rolled


def _rows_after(ref, start, k):
    cur = ref[pl.ds(start, CONV_CHUNK), :]
    nxt = ref[pl.ds(pl.multiple_of(start + CONV_CHUNK, 8), 8), :]
    rolled = pltpu.roll(cur, CONV_CHUNK - k, 0)
    row = lax.broadcasted_iota(jnp.int32, cur.shape, 0)
    for j in range(k):
        rolled = jnp.where(row == CONV_CHUNK - k + j, _pick_row(nxt, j), rolled)
    return rolled


def _conv_pre(u_ref, w_ref, b_ref, start):
    u2 = _rows_before(u_ref, start, 2)
    u1 = _rows_before(u_ref, start, 1)
    u0 = u_ref[pl.ds(start, CONV_CHUNK), :].astype(F32)
    c = ((b_ref[...] + w_ref[0:1, :] * u2) + w_ref[1:2, :] * u1) + w_ref[2:3, :] * u0
    return c, (u2, u1, u0)


def _up_conv_glu(h2, w_up, conv_w, conv_b, name):
    s_len, d = h2.shape
    tm, tn = 512, 2 * LANES
    nblk = D_FF // tn

    def body(h_ref, w_ref, cw_ref, cb_ref, u_ref, act_ref, halo):
        @pl.when(pl.program_id(0) == 0)
        def _():
            halo[...] = jnp.zeros_like(halo)
        hv = h_ref[...]
        row = lax.broadcasted_iota(jnp.int32, (tm, tn), 0)
        for j in range(nblk):
            conv = []
            for half in range(2):
                cb = half * nblk + j
                cols = slice(cb * tn, (cb + 1) * tn)
                ub = jnp.dot(hv, w_ref[:, cols], preferred_element_type=F32).astype(BF16)
                u_ref[:, cols] = ub
                u0 = ub.astype(F32)
                prev = halo[cb]
                u1 = jnp.where(row == 0, _pick_row(prev, 7), pltpu.roll(u0, 1, 0))
                u2 = pltpu.roll(u0, 2, 0)
                u2 = jnp.where(row == 0, _pick_row(prev, 6), jnp.where(row == 1, _pick_row(prev, 7), u2))
                halo[cb] = u0[tm - 8:tm, :]
                conv.append(((cb_ref[:, cols] + cw_ref[0:1, cols] * u2) + cw_ref[1:2, cols] * u1)
                            + cw_ref[2:3, cols] * u0)
            act_ref[:, j * tn:(j + 1) * tn] = ((conv[0] * _sigmoid(conv[0])) * conv[1]).astype(BF16)

    return pl.pallas_call(
        body, name=name, grid=(s_len // tm,),
        in_specs=[pl.BlockSpec((tm, d), lambda i: (i, 0)), _resident((d, 2 * D_FF), lambda i: (0, 0)),
                  _resident((3, 2 * D_FF), lambda i: (0, 0)), _resident((1, 2 * D_FF), lambda i: (0, 0))],
        out_specs=(pl.BlockSpec((tm, 2 * D_FF), lambda i: (i, 0)), pl.BlockSpec((tm, D_FF), lambda i: (i, 0))),
        out_shape=(jax.ShapeDtypeStruct((s_len, 2 * D_FF), BF16), jax.ShapeDtypeStruct((s_len, D_FF), BF16)),
        scratch_shapes=[pltpu.VMEM((2 * nblk, 8, tn), F32)], compiler_params=_cparams(1),
    )(h2, w_up, conv_w, conv_b)


def _conv_glu_bwd(dact, u, conv_w, conv_b, name):
    s_len = u.shape[0]
    nblk = D_FF // LANES
    n_chunks = s_len // CONV_CHUNK

    def body(da_ref, ua_ref, ug_ref, wa_ref, wg_ref, ba_ref, bg_ref,
             dua_ref, dug_ref, dwa_ref, dwg_ref, dba_ref, dbg_ref, sa, sg):
        sa[pl.ds(s_len, 8), :] = jnp.zeros((8, LANES), F32)
        sg[pl.ds(s_len, 8), :] = jnp.zeros((8, LANES), F32)
        zero = jnp.zeros((1, LANES), F32)

        def chunk1(ci, carry):
            start = pl.multiple_of(ci * CONV_CHUNK, CONV_CHUNK)
            ca, ua = _conv_pre(ua_ref, wa_ref, ba_ref, start)
            cg, ug = _conv_pre(ug_ref, wg_ref, bg_ref, start)
            dact_v = da_ref[pl.ds(start, CONV_CHUNK), :].astype(F32)
            sig = _sigmoid(ca)
            dcg = dact_v * (ca * sig)
            dca = (dact_v * cg) * (sig * (1.0 + ca * (1.0 - sig)))
            sa[pl.ds(start, CONV_CHUNK), :] = dca
            sg[pl.ds(start, CONV_CHUNK), :] = dcg
            out = [carry[0] + jnp.sum(dca, axis=0, keepdims=True), carry[1] + jnp.sum(dcg, axis=0, keepdims=True)]
            for j in range(3):
                out.append(carry[2 + j] + jnp.sum(dca * ua[j], axis=0, keepdims=True))
            for j in range(3):
                out.append(carry[5 + j] + jnp.sum(dcg * ug[j], axis=0, keepdims=True))
            return tuple(out)

        acc = lax.fori_loop(0, n_chunks, chunk1, (zero,) * 8)
        dba_ref[...] = acc[0]
        dbg_ref[...] = acc[1]
        for j in range(3):
            dwa_ref[j:j + 1, :] = acc[2 + j]
            dwg_ref[j:j + 1, :] = acc[5 + j]

        def chunk2(ci, carry):
            start = pl.multiple_of(ci * CONV_CHUNK, CONV_CHUNK)
            for s_ref, w_ref, o_ref in ((sa, wa_ref, dua_ref), (sg, wg_ref, dug_ref)):
                d0 = s_ref[pl.ds(start, CONV_CHUNK), :]
                d1 = _rows_after(s_ref, start, 1)
                d2 = _rows_after(s_ref, start, 2)
                o_ref[pl.ds(start, CONV_CHUNK), :] = (w_ref[2:3, :] * d0 + w_ref[1:2, :] * d1
                                                      + w_ref[0:1, :] * d2).astype(BF16)
            return carry
        lax.fori_loop(0, n_chunks, chunk2, 0)

    def col(rows, off):
        return pl.BlockSpec((rows, LANES), lambda j: (0, off + j))

    big = jax.ShapeDtypeStruct((s_len, D_FF), BF16)
    return pl.pallas_call(
        body, name=name, grid=(nblk,),
        in_specs=[col(s_len, 0), col(s_len, 0), col(s_len, nblk), col(3, 0), col(3, nblk), col(1, 0), col(1, nblk)],
        out_specs=(col(s_len, 0), col(s_len, 0), col(3, 0), col(3, 0), col(1, 0), col(1, 0)),
        out_shape=(big, big, jax.ShapeDtypeStruct((3, D_FF), F32), jax.ShapeDtypeStruct((3, D_FF), F32),
                   jax.ShapeDtypeStruct((1, D_FF), F32), jax.ShapeDtypeStruct((1, D_FF), F32)),
        scratch_shapes=[pltpu.VMEM((s_len + 8, LANES), F32)] * 2, compiler_params=_cparams(1),
    )(dact, u, u, conv_w, conv_w, conv_b, conv_b)


def _rope_tables(positions):
    half = ROPE_DIMS // 2
    freqs = jnp.exp(jnp.arange(half, dtype=F32) * (-2.0 * math.log(ROPE_THETA) / ROPE_DIMS))
    ang = positions.reshape(-1).astype(F32)[:, None] * freqs
    cos, sin = jnp.cos(ang), jnp.sin(ang)
    n = ang.shape[0]
    zeros = lambda w: jnp.zeros((n, w), F32)
    c = jnp.concatenate([cos, cos, jnp.ones((n, HEAD_DIM - ROPE_DIMS), F32)], axis=1)
    s1 = jnp.concatenate([-sin, zeros(HEAD_DIM - half)], axis=1)
    s2 = jnp.concatenate([zeros(half), sin, zeros(HEAD_DIM - ROPE_DIMS)], axis=1)
    return tuple(jnp.tile(t, (1, 2)) for t in (c, s1, s2))


def _two(v):
    return jnp.tile(v.reshape(1, HEAD_DIM), (1, 2))


def _fold_heads(g):
    return g[0, :HEAD_DIM] + g[0, HEAD_DIM:]


MIX_WEIGHTS = ('w_gate', 'w_mem_kv', 'w_o_a', 'w_o_b', 'w_o_m', 'w_out')
FFN_WEIGHTS = ('w_up', 'conv_w', 'w_down')


def _device_step(x, mem, positions, target, w, hooks=None):
    tabs = _rope_tables(positions)
    dils = tuple(d for _, d in A_GROUPS)
    grads = {}
    w = dict(w)

    h, h_t, r1 = _rms_fwd(x, w['attn_norm'], "rms1")
    if hooks is not None:
        w.update(hooks.weights('in', [h, *tabs]))
    proj = _mm_rows([(h, w['w_in'], 0)], "mm_in")

    qkv_a, o_g, lse_g = [], [], []
    for gi, (window, d) in enumerate(A_GROUPS):
        gq, gk = _two(w['a_q_norm'][gi]), _two(w['a_k_norm'][gi])
        qkv = _qk_prep(proj, 6 * gi, d, False, gq, gk, tabs, f"qk_prep_a{gi}")
        o, lse = _band_fwd(qkv, 2 * d, window // d, None, f"band_fwd_a{gi}")
        qkv_a.append(qkv)
        o_g.append(o)
        lse_g.append(lse)
    o_a, lse_a = _merge_groups(o_g, lse_g, dils, "merge_a")
    if hooks is not None:
        w.update(hooks.weights('mix', [o_a]))

    gbq, gbk = _two(w['b_q_norm']), _two(w['b_k_norm'])
    sinks = jnp.repeat(w['b_sinks'].reshape(4, 2), HEAD_DIM, axis=1).reshape(4, 1, LANES)
    qkv_b = _qk_prep(proj, 18, 1, True, gbq, gbk, tabs, "qk_prep_b")
    o_b, lse_b = _band_fwd(qkv_b, 4, B_WINDOW - 1, sinks, "band_fwd_b")

    gates = _mm_rows([(h, w['w_gate'], 0)], "mm_gate", bias=w['b_gate'], sigmoid=True, out_dtypes=(BF16,))
    mk, mv = _mem_kv(mem, w['mem_norm'], w['w_mem_kv'], w['m_k_norm'], "mem_kv")
    o_m = _mem_attn_fwd(proj, 6, mk, mv, w['m_q_norm'], "mem_attn")

    pa = _mm_rows([(o_a, w['w_o_a'], 0)], "mm_oa", out_dtypes=(BF16,))
    pb = _mm_rows([(o_b, w['w_o_b'], 0)], "mm_ob", out_dtypes=(BF16,))
    pm = _mm_rows([(o_m, w['w_o_m'], 0)], "mm_om", out_dtypes=(BF16,))
    merged = _gate_merge(gates, pa, pb, pm, "gate_merge")
    x1 = _mm_rows([(merged, w['w_out'], 0)], "mm_out", res=x)

    if hooks is not None:
        w.update(hooks.weights('ffn', [x1]))
    h2, h2_t, r2 = _rms_fwd(x1, w['ffn_norm'], "rms2")
    u, act = _up_conv_glu(h2, w['w_up'], w['conv_w'], w['conv_b'], "up_conv_glu")
    dy, dy_b, loss = _mm_rows([(act, w['w_down'], 0)], "mm_down", res=x1, loss_target=target)

    dact = _mm_rows([(dy_b, w['w_down'], 0)], "mm_d_act", nt=True, out_dtypes=(BF16,))
    grads['w_down'] = _mm_tn(act, dy_b, "mm_dw_down")
    du_a, du_g, dcw_a, dcw_g, dcb_a, dcb_g = _conv_glu_bwd(dact, u, w['conv_w'], w['conv_b'], "conv_glu_bwd")
    grads['conv_w'] = jnp.concatenate([dcw_a, dcw_g], axis=1)
    grads['conv_b'] = jnp.concatenate([dcb_a, dcb_g], axis=1)
    grads['w_up'] = jnp.concatenate([_mm_cols(h2_t, du_a, "mm_dw_up_a"), _mm_cols(h2_t, du_g, "mm_dw_up_g")], axis=1)
    ffn_gain = w['ffn_norm']
    if hooks is not None:
        ffn_gain = ffn_gain + hooks.grads('ffn', grads)[0:1, 0:1]
    dx1, dx1_b, grads['ffn_norm'] = _mm_rows([(du_a, w['w_up'], 0), (du_g, w['w_up'], 1)], "mm_d_h2", nt=True,
                                             rms_bwd=(x1, r2, ffn_gain, dy), out_dtypes=(F32, BF16))

    dmerged = _mm_rows([(dx1_b, w['w_out'], 0)], "mm_d_merged", nt=True)
    grads['w_out'] = _mm_tn(merged, dx1_b, "mm_dw_out")
    dpa, dpb, dpm, dgpre, grads['b_gate'] = _gate_merge_bwd(dmerged, gates, pa, pb, pm, "gate_merge_bwd")
    do_a = _mm_rows([(dpa, w['w_o_a'], 0)], "mm_d_oa", nt=True)
    do_b = _mm_rows([(dpb, w['w_o_b'], 0)], "mm_d_ob", nt=True)
    do_m = _mm_rows([(dpm, w['w_o_m'], 0)], "mm_d_om", nt=True)
    grads['w_o_a'] = _mm_tn(o_a, dpa, "mm_dw_oa")
    grads['w_o_b'] = _mm_tn(o_b, dpb, "mm_dw_ob")
    grads['w_o_m'] = _mm_tn(o_m, dpm, "mm_dw_om")
    grads['w_gate'] = _mm_cols(h_t, dgpre, "mm_dw_gate")
    dq_m, dmk, dmv, grads['m_q_norm'] = _mem_attn_bwd(proj, 6, mk, mv, w['m_q_norm'], do_m, "mem_attn_bwd")
    grads['w_mem_kv'], grads['mem_norm'], grads['m_k_norm'] = _mem_kv_bwd(
        mem, w['mem_norm'], w['w_mem_kv'], w['m_k_norm'], dmk, dmv, "mem_kv_bwd")
    a_gain = w['a_q_norm']
    if hooks is not None:
        a_gain = a_gain + hooks.grads('mix', grads)[0:1, 0:1]

    prep = _bwd_prep(do_a, o_a, lse_a, dils, None, "bwd_prep_a")
    dproj, dgq_a, dgk_a = [], [], []
    for gi, (window, d) in enumerate(A_GROUPS):
        gq, gk = _two(a_gain[gi]), _two(w['a_k_norm'][gi])
        dqkv = _band_bwd(qkv_a[gi], prep[3 * gi], prep[3 * gi + 1], prep[3 * gi + 2], 2 * d, window // d,
                         f"band_bwd_a{gi}")
        dp, dgq, dgk = _qk_prep_bwd(dqkv, proj, 6 * gi, d, False, gq, gk, tabs, f"qk_prep_bwd_a{gi}")
        dproj.append(dp)
        dgq_a.append(_fold_heads(dgq))
        dgk_a.append(_fold_heads(dgk))
    grads['a_q_norm'] = jnp.stack(dgq_a)
    grads['a_k_norm'] = jnp.stack(dgk_a)

    do_bu, lse_bu, delta_bu, dsink = _bwd_prep(do_b, o_b, lse_b, (1,), sinks, "bwd_prep_b")
    dqkv = _band_bwd(qkv_b, do_bu, lse_bu, delta_bu, 4, B_WINDOW - 1, "band_bwd_b")
    dp_b, dgq, dgk = _qk_prep_bwd(dqkv, proj, 18, 1, True, gbq, gbk, tabs, "qk_prep_bwd_b")
    dproj.append(dp_b)
    grads['b_q_norm'] = _fold_heads(dgq)
    grads['b_k_norm'] = _fold_heads(dgk)
    grads['b_sinks'] = jnp.stack([dsink[:, 0, 0], dsink[:, 0, HEAD_DIM]], axis=1).reshape(8)

    dproj.append(dq_m)

    cols = (0, 1, 2, 3, 6)
    grads['w_in'] = _mm_rows_cat(h_t, dproj, "mm_dw_in")
    attn_gain = w['attn_norm']
    if hooks is not None:
        attn_gain = attn_gain + hooks.grads('in', grads)[0:1, 0:1]
    grad_x, grads['attn_norm'] = _mm_rows(
        [(dp, w['w_in'], c) for dp, c in zip(dproj, cols)] + [(dgpre, w['w_gate'], 0)], "mm_d_h", nt=True,
        rms_bwd=(x, r1, attn_gain, dx1))
    return loss, grad_x, grads


def _coords():
    return lax.axis_index("x"), lax.axis_index("y"), lax.axis_index("c")


def _slot(p):
    return 4 * p[0] + 2 * p[1] + p[2]


ALL_PEERS = tuple(range(1, N_DEV))
CHIP_PEERS = (1, 4, 2, 6)
OTHER_CHIPS = (4, 2, 6)


def _peers(me, masks=ALL_PEERS):
    x, y, c = me
    return [(1 - x if mask & 4 else x, 1 - y if mask & 2 else y, 1 - c if mask & 1 else c) for mask in masks]


HBM_SPEC = pl.BlockSpec(memory_space=pltpu.HBM)


SEM_SPEC = pl.BlockSpec(memory_space=pltpu.SEMAPHORE)
SIDE_EFFECT = pltpu.SideEffectType.DATAFLOW_SIDE_EFFECTING


def _exchange_start(blocks, name, gather=False, masks=ALL_PEERS, after=None):
    n = len(blocks)
    n_peers = len(masks)
    n_in = 2 * n + (0 if after is None else 1)

    def body(*refs):
        ins, lands = refs[:n], refs[n:2 * n]
        send_sems, recv_sems = refs[n_in], refs[n_in + 1]
        token = refs[-1]
        me = _coords()
        peers = _peers(me, masks)
        for a in range(n):
            for k in range(n_peers):
                pltpu.make_async_remote_copy(
                    src_ref=ins[a] if gather else ins[a].at[_slot(peers[k])], dst_ref=lands[a].at[_slot(me)],
                    send_sem=send_sems.at[a * n_peers + k], recv_sem=recv_sems.at[a * n_peers + k],
                    device_id=peers[k], device_id_type=MESH).start()
        token[...] = jnp.zeros_like(token)

    land_shapes = [((N_DEV,) + b.shape) if gather else b.shape for b in blocks]
    hbm_in = [pltpu.HBM(b.shape, b.dtype) for b in blocks]
    hbm_land = [pltpu.HBM(s, b.dtype) for s, b in zip(land_shapes, blocks)]
    sems = pltpu.SemaphoreType.DMA((n * n_peers,))
    ins = [pltpu.with_memory_space_constraint(b, pltpu.HBM) for b in blocks]
    lands = [pltpu.with_memory_space_constraint(lax.empty(s, b.dtype), pltpu.HBM) for s, b in zip(land_shapes, blocks)]
    return pl.pallas_call(
        body, name=name, out_shape=(sems, sems, *hbm_in, *hbm_land, jax.ShapeDtypeStruct((8, LANES), F32)),
        in_specs=[HBM_SPEC] * (2 * n) + ([] if after is None else [pl.BlockSpec(memory_space=pl.ANY)]),
        out_specs=(SEM_SPEC, SEM_SPEC, *([HBM_SPEC] * (2 * n)), pl.BlockSpec(memory_space=pltpu.VMEM)),
        input_output_aliases={i: 2 + i for i in range(2 * n)},
        compiler_params=pltpu.CompilerParams(has_side_effects=SIDE_EFFECT),
    )(*ins, *lands, *([] if after is None else [after]))


def _exchange_wait(started, after, name, gather=False, masks=ALL_PEERS):
    n = (len(started) - 3) // 2
    n_peers = len(masks)
    send_sems, recv_sems = started[0], started[1]
    thru = started[2:2 + 2 * n]

    def body(*refs):
        ins, lands = refs[:n], refs[n:2 * n]
        send_ref, recv_ref = refs[2 * n], refs[2 * n + 1]
        me = _coords()
        peers = _peers(me, masks)
        for a in range(n):
            for k in range(n_peers):
                cp = pltpu.make_async_remote_copy(
                    src_ref=ins[a] if gather else ins[a].at[_slot(peers[k])], dst_ref=lands[a].at[_slot(peers[k])],
                    send_sem=send_ref.at[a * n_peers + k], recv_sem=recv_ref.at[a * n_peers + k],
                    device_id=peers[k], device_id_type=MESH)
                cp.wait_send()
                cp.wait_recv()

    hbm = [pltpu.HBM(t.shape, t.dtype) for t in thru]
    res = pl.pallas_call(
        body, name=name, out_shape=tuple(hbm),
        in_specs=[HBM_SPEC] * (2 * n) + [SEM_SPEC, SEM_SPEC] + [pl.BlockSpec(memory_space=pl.ANY)] * len(after),
        out_specs=tuple([HBM_SPEC] * (2 * n)), input_output_aliases={i: i for i in range(2 * n)},
        compiler_params=pltpu.CompilerParams(has_side_effects=SIDE_EFFECT),
    )(*thru, send_sems, recv_sems, *after)
    return res[n:]


def _sibling_forward(arrays, name):
    n = len(arrays)
    n_fwd = len(OTHER_CHIPS)

    def body(*refs):
        bufs = refs[n:2 * n]
        token, send_sems, recv_sems = refs[2 * n:]
        token[...] = jnp.zeros_like(token)
        x, y, c = _coords()
        sibling = (x, y, 1 - c)
        mine = _peers((x, y, c), OTHER_CHIPS)
        theirs = _peers(sibling, OTHER_CHIPS)

        def copy(a, k, block):
            rows = bufs[a].at[_slot(block)]
            return pltpu.make_async_remote_copy(
                src_ref=rows, dst_ref=rows, send_sem=send_sems.at[a * n_fwd + k], recv_sem=recv_sems.at[a * n_fwd + k],
                device_id=sibling, device_id_type=MESH)

        sends = [copy(a, k, mine[k]) for a in range(n) for k in range(n_fwd)]
        for cp in sends:
            cp.start()
        for a in range(n):
            for k in range(n_fwd):
                copy(a, k, theirs[k]).wait_recv()
        for cp in sends:
            cp.wait_send()

    res = pl.pallas_call(
        body, name=name, in_specs=[HBM_SPEC] * n,
        out_specs=tuple([HBM_SPEC] * n + [pl.BlockSpec(memory_space=pltpu.VMEM)]),
        out_shape=tuple([jax.ShapeDtypeStruct(a.shape, a.dtype) for a in arrays] + [jax.ShapeDtypeStruct((8, LANES), F32)]),
        input_output_aliases={i: i for i in range(n)},
        scratch_shapes=[pltpu.SemaphoreType.DMA((n * n_fwd,)), pltpu.SemaphoreType.DMA((n * n_fwd,))],
    )(*arrays)
    return res[:n], res[n]


def _all_sum(p, name):
    def body(p_ref, o_ref, recv, send_sems, recv_sems):
        me = _coords()
        peers = _peers(me)
        recv[_slot(me)] = p_ref[...]

        def copy(k, landing):
            return pltpu.make_async_remote_copy(
                src_ref=p_ref, dst_ref=recv.at[_slot(landing)], send_sem=send_sems.at[k], recv_sem=recv_sems.at[k],
                device_id=peers[k], device_id_type=MESH)

        sends = [copy(k, me) for k in range(N_DEV - 1)]
        for cp in sends:
            cp.start()
        for k in range(N_DEV - 1):
            copy(k, peers[k]).wait_recv()
        for cp in sends:
            cp.wait_send()
        acc = recv[0]
        for s in range(1, N_DEV):
            acc = acc + recv[s]
        o_ref[...] = acc

    vmem = pl.BlockSpec(memory_space=pltpu.VMEM)
    return pl.pallas_call(
        body, name=name, in_specs=[vmem], out_specs=vmem, out_shape=jax.ShapeDtypeStruct(p.shape, F32),
        scratch_shapes=[pltpu.VMEM((N_DEV,) + p.shape, F32), pltpu.SemaphoreType.DMA((N_DEV - 1,)),
                        pltpu.SemaphoreType.DMA((N_DEV - 1,))],
    )(p)


def _adam(w, g, m, v):
    m2 = ADAM_B1 * m + (1.0 - ADAM_B1) * g
    v2 = ADAM_B2 * v + (1.0 - ADAM_B2) * (g * g)
    m_hat = m2 / (1.0 - ADAM_B1 ** ADAM_STEP)
    v_hat = v2 / (1.0 - ADAM_B2 ** ADAM_STEP)
    delta = -ADAM_LR * (m_hat / (jnp.sqrt(v_hat) + ADAM_EPS) + ADAM_WD * w)
    return delta, m2, v2


def _row_tile(rows, cols):
    best = rows
    for t in range(16, rows, 16):
        if rows % t == 0 and t * cols * 4 <= (1 << 20):
            best = t
    return best


def _adam_reduce(parts, w, m, v, name):
    rows, cols = w.shape
    tr = _row_tile(rows, cols)

    def body(p_ref, w_ref, m_ref, v_ref, g_ref, d_ref, m2_ref, v2_ref):
        g = p_ref[0].astype(F32)
        for s in range(1, N_DEV):
            g = g + p_ref[s].astype(F32)
        g_ref[...] = g
        d_ref[...], m2_ref[...], v2_ref[...] = _adam(w_ref[...], g, m_ref[...], v_ref[...])

    blk = pl.BlockSpec((tr, cols), lambda i: (i, 0))
    shp = jax.ShapeDtypeStruct((rows, cols), F32)
    return pl.pallas_call(
        body, name=name, grid=(rows // tr,),
        in_specs=[pl.BlockSpec((N_DEV, tr, cols), lambda i: (0, i, 0)), blk, blk, blk],
        out_specs=(blk,) * 4, out_shape=(shp,) * 4, compiler_params=_cparams(1),
    )(parts, w, m, v)


PACK_COLS = 1024
PACK = {'attn_norm': (0, 1, 1024), 'mem_norm': (1, 1, 1024), 'ffn_norm': (2, 1, 1024), 'b_gate': (3, 3, 1024),
        'conv_b': (6, 6, 1024), 'a_q_norm': (12, 3, 64), 'a_k_norm': (15, 3, 64), 'b_q_norm': (18, 1, 64),
        'b_k_norm': (19, 1, 64), 'm_q_norm': (20, 1, 128), 'm_k_norm': (21, 1, 128), 'b_sinks': (22, 1, 8)}
PACK_LOSS_ROW = 23
PACK_ROWS = 24


def _pack_pieces(name, width):
    r0, nr, lanes = PACK[name]
    out = []
    for j in range(nr):
        if lanes == PACK_COLS:
            w = min(PACK_COLS, width - j * PACK_COLS)
            out.append((r0 + j, slice(0, 1), slice(j * PACK_COLS, j * PACK_COLS + w), w))
        else:
            out.append((r0 + j, slice(j, j + 1), slice(0, lanes), lanes))
    return out


def _pack_small(grads, loss_tile, name):
    names = list(PACK)

    def body(*refs):
        o_ref = refs[-1]
        o_ref[...] = jnp.zeros_like(o_ref)
        for k, nm in enumerate(names):
            for row, rs, ls, w in _pack_pieces(nm, refs[k].shape[1]):
                o_ref[row:row + 1, 0:w] = refs[k][rs, ls]
        o_ref[PACK_LOSS_ROW:PACK_LOSS_ROW + 1, 0:1] = refs[len(names)][0:1, 0:1]

    vmem = pl.BlockSpec(memory_space=pltpu.VMEM)
    args = [grads[nm] for nm in names] + [loss_tile]
    return pl.pallas_call(body, name=name, in_specs=[vmem] * len(args), out_specs=vmem,
                          out_shape=jax.ShapeDtypeStruct((PACK_ROWS, PACK_COLS), F32))(*args)


def _adam_small(gsum, ws, ms, vs, name):
    names = list(PACK)
    n = len(names)

    def body(*refs):
        g_ref = refs[0]
        w_refs, m_refs, v_refs = refs[1:1 + n], refs[1 + n:1 + 2 * n], refs[1 + 2 * n:1 + 3 * n]
        outs = refs[1 + 3 * n:]
        outs[0][...] = g_ref[PACK_LOSS_ROW:PACK_LOSS_ROW + 1, 0:1]
        for k, nm in enumerate(names):
            o_g, o_d, o_m, o_v = outs[1 + 4 * k:5 + 4 * k]
            for row, rs, ls, width in _pack_pieces(nm, w_refs[k].shape[1]):
                src = (rs, ls)
                g = g_ref[row:row + 1, 0:width]
                d, m2, v2 = _adam(w_refs[k][src], g, m_refs[k][src], v_refs[k][src])
                o_g[src] = g
                o_d[src] = d
                o_m[src] = m2
                o_v[src] = v2

    vmem = pl.BlockSpec(memory_space=pltpu.VMEM)
    shapes = [jax.ShapeDtypeStruct((1, 1), F32)]
    for nm in names:
        shapes += [jax.ShapeDtypeStruct(ws[nm].shape, F32)] * 4
    args = [gsum] + [ws[nm] for nm in names] + [ms[nm] for nm in names] + [vs[nm] for nm in names]
    return pl.pallas_call(
        body, name=name, in_specs=[vmem] * len(args), out_specs=tuple([vmem] * len(shapes)), out_shape=tuple(shapes),
    )(*args)


def _as2d(name, a):
    return a.reshape(a.shape[-2], a.shape[-1]) if a.ndim == 3 else a


def kernel(x, mem, positions, attn_norm, w_in, a_q_norm, a_k_norm, b_q_norm, b_k_norm, b_sinks, mem_norm, w_mem_kv, m_q_norm, m_k_norm, w_o_a, w_o_b, w_o_m, w_gate, b_gate, w_out, ffn_norm, w_up, conv_w, conv_b, w_down, loss_target, m_attn_norm, m_w_in, m_a_q_norm, m_a_k_norm, m_b_q_norm, m_b_k_norm, m_b_sinks, m_mem_norm, m_w_mem_kv, m_m_q_norm, m_m_k_norm, m_w_o_a, m_w_o_b, m_w_o_m, m_w_gate, m_b_gate, m_w_out, m_ffn_norm, m_w_up, m_conv_w, m_conv_b, m_w_down, v_attn_norm, v_w_in, v_a_q_norm, v_a_k_norm, v_b_q_norm, v_b_k_norm, v_b_sinks, v_mem_norm, v_w_mem_kv, v_m_q_norm, v_m_k_norm, v_w_o_a, v_w_o_b, v_w_o_m, v_w_gate, v_b_gate, v_w_out, v_ffn_norm, v_w_up, v_conv_w, v_conv_b, v_w_down):
    given = dict(attn_norm=attn_norm, w_in=w_in, a_q_norm=a_q_norm, a_k_norm=a_k_norm, b_q_norm=b_q_norm, b_k_norm=b_k_norm, b_sinks=b_sinks, mem_norm=mem_norm, w_mem_kv=w_mem_kv, m_q_norm=m_q_norm, m_k_norm=m_k_norm, w_o_a=w_o_a, w_o_b=w_o_b, w_o_m=w_o_m, w_gate=w_gate, b_gate=b_gate, w_out=w_out, ffn_norm=ffn_norm, w_up=w_up, conv_w=conv_w, conv_b=conv_b, w_down=w_down)
    mom1 = dict(attn_norm=m_attn_norm, w_in=m_w_in, a_q_norm=m_a_q_norm, a_k_norm=m_a_k_norm, b_q_norm=m_b_q_norm, b_k_norm=m_b_k_norm, b_sinks=m_b_sinks, mem_norm=m_mem_norm, w_mem_kv=m_w_mem_kv, m_q_norm=m_m_q_norm, m_k_norm=m_m_k_norm, w_o_a=m_w_o_a, w_o_b=m_w_o_b, w_o_m=m_w_o_m, w_gate=m_w_gate, b_gate=m_b_gate, w_out=m_w_out, ffn_norm=m_ffn_norm, w_up=m_w_up, conv_w=m_conv_w, conv_b=m_conv_b, w_down=m_w_down)
    mom2 = dict(attn_norm=v_attn_norm, w_in=v_w_in, a_q_norm=v_a_q_norm, a_k_norm=v_a_k_norm, b_q_norm=v_b_q_norm, b_k_norm=v_b_k_norm, b_sinks=v_b_sinks, mem_norm=v_mem_norm, w_mem_kv=v_w_mem_kv, m_q_norm=v_m_q_norm, m_k_norm=v_m_k_norm, w_o_a=v_w_o_a, w_o_b=v_w_o_b, w_o_m=v_w_o_m, w_gate=v_w_gate, b_gate=v_b_gate, w_out=v_w_out, ffn_norm=v_ffn_norm, w_up=v_w_up, conv_w=v_conv_w, conv_b=v_conv_b, w_down=v_w_down)

    big = list(BIG)
    stages = {'mix': list(MIX_WEIGHTS), 'ffn': list(FFN_WEIGHTS), 'in': ['w_in']}
    my_slot = _slot(_coords())

    def shard(n):
        return given[n][0] if n == 'conv_w' else given[n][0].astype(BF16)

    def whole(n, g):
        _, r, c = g.shape
        return g.reshape(N_DEV * r, c) if BIG[n] == 0 else g.transpose(1, 0, 2).reshape(r, N_DEV * c)

    def to_blocks(n, g):
        r, c = given[n].shape[1:]
        g = g.reshape(N_DEV, r, c) if BIG[n] == 0 else g.reshape(r, N_DEV, c).transpose(1, 0, 2)
        return g if n == 'conv_w' else g.astype(BF16)

    class Hooks:
        next_stage = {'in': 'mix', 'mix': 'ffn'}

        def __init__(self):
            self.coming, self.sent = {}, {}
            self.shards = {n: shard(n) for n in big}
            self.start_gather('in', None)

        def start_gather(self, stage, after):
            src = [self.shards[n] for n in stages[stage]]
            self.coming[stage] = _exchange_start(src, f"gather_{stage}_start", gather=True, masks=CHIP_PEERS,
                                                 after=after)

        def weights(self, stage, after):
            names = stages[stage]
            after = list(after)
            if stage == 'in':
                after += [self.shards[n] for n in stages['mix'] + stages['ffn']]
            landed = _exchange_wait(self.coming[stage], after, f"gather_{stage}_wait", gather=True, masks=CHIP_PEERS)
            landed, token = _sibling_forward(landed, f"gather_{stage}_forward")
            if stage in self.next_stage:
                self.start_gather(self.next_stage[stage], token)
            return {n: whole(n, lax.dynamic_update_slice_in_dim(land, self.shards[n][None], my_slot, axis=0))
                    for n, land in zip(names, landed)}

        def grads(self, stage, g):
            blocks = [to_blocks(n, g[n]) for n in stages[stage]]
            own = [lax.dynamic_slice_in_dim(b, my_slot, 1, axis=0) for b in blocks]
            self.sent[stage] = (_exchange_start(blocks, f"exchange_{stage}_start"), own)
            return self.sent[stage][0][-1]

        def parts(self, stage, after):
            started, own = self.sent[stage]
            landed = _exchange_wait(started, [after], f"exchange_{stage}_wait")
            return {n: lax.dynamic_update_slice_in_dim(land, o, my_slot, axis=0)
                    for n, land, o in zip(stages[stage], landed, own)}

    hooks = Hooks()
    w = {}
    for n in SMALL:
        w[n] = given[n]
    w['a_q_norm'], w['a_k_norm'] = given['a_q_norm'][0], given['a_k_norm'][0]
    w['b_q_norm'], w['b_k_norm'], w['b_sinks'] = given['b_q_norm'][0], given['b_k_norm'][0], given['b_sinks'][0]

    loss_tile, grad_x, grads = _device_step(x[0], mem[0], positions[0], loss_target[0], w, hooks)
    out = {}
    after = grad_x
    for stage in ('ffn', 'mix', 'in'):
        for n, p in hooks.parts(stage, after).items():
            res = _adam_reduce(p, given[n][0], mom1[n][0], mom2[n][0], f"adam_{n}")
            out[n] = tuple(t[None] for t in res)
            after = res[0]

    small = {n: grads[n] for n in PACK}
    small['b_q_norm'], small['b_k_norm'] = grads['b_q_norm'].reshape(1, -1), grads['b_k_norm'].reshape(1, -1)
    small['b_sinks'] = grads['b_sinks'].reshape(1, -1)
    gsum = _all_sum(_pack_small(small, loss_tile, "pack_small"), "sum_small")
    ws = {n: _as2d(n, given[n]) for n in PACK}
    ms = {n: _as2d(n, mom1[n]) for n in PACK}
    vs = {n: _as2d(n, mom2[n]) for n in PACK}
    res = _adam_small(gsum, ws, ms, vs, "adam_small")
    loss = res[0].reshape(())
    for k, n in enumerate(PACK):
        out[n] = tuple(t.reshape(given[n].shape) for t in res[1 + 4 * k:5 + 4 * k])

    outs = [loss, grad_x[None]]
    for field in range(4):
        outs += [out[n][field] for n in WEIGHTS]
    return tuple(outs)
```

```python
import functools
import math

import jax
import jax.numpy as jnp
from jax import lax
from jax.experimental import pallas as pl
from jax.experimental.pallas import tpu as pltpu

F32 = jnp.float32
BF16 = jnp.bfloat16

N_DEV = 8
D_MODEL = 1024
HEAD_DIM = 64
A_GROUPS = ((128, 1), (512, 4), (2048, 16))
B_WINDOW = 128
M_HEADS = 4
M_HEAD_DIM = 128
MEM_LEN = 256
D_FF = 2816
ROPE_THETA = 500000.0
ROPE_DIMS = 16
BLOCK = 128
EPS = 1e-6
LANES = 128
BAND_Q_BLOCKS = 4
BAND_UNITS = 2

ADAM_LR = 0.001
ADAM_B1 = 0.9
ADAM_B2 = 0.999
ADAM_EPS = 1e-08
ADAM_WD = 0.01
ADAM_STEP = 10

VMEM_LIMIT_BYTES = 56 * 1024 * 1024
MESH = pl.DeviceIdType.MESH

WEIGHTS = ['attn_norm', 'w_in', 'a_q_norm', 'a_k_norm', 'b_q_norm', 'b_k_norm', 'b_sinks', 'mem_norm',
           'w_mem_kv', 'm_q_norm', 'm_k_norm', 'w_o_a', 'w_o_b', 'w_o_m', 'w_gate', 'b_gate', 'w_out',
           'ffn_norm', 'w_up', 'conv_w', 'conv_b', 'w_down']
BIG = {'w_in': 1, 'w_mem_kv': 0, 'w_o_a': 1, 'w_o_b': 1, 'w_o_m': 1, 'w_gate': 1, 'w_out': 0, 'w_up': 1,
       'conv_w': 1, 'w_down': 0}
SMALL = [n for n in WEIGHTS if n not in BIG]


def _cparams(n_grid):
    return pltpu.CompilerParams(dimension_semantics=("arbitrary",) * n_grid, vmem_limit_bytes=VMEM_LIMIT_BYTES)


def _seg_matrix(width):
    shift = width.bit_length() - 1
    r = lax.shift_right_logical(lax.broadcasted_iota(jnp.int32, (LANES, LANES), 0), shift)
    c = lax.shift_right_logical(lax.broadcasted_iota(jnp.int32, (LANES, LANES), 1), shift)
    return jnp.where(r == c, 1.0, 0.0).astype(BF16)


def _seg_sum(x, seg):
    hi = x.astype(BF16)
    r1 = x - hi.astype(F32)
    mid = r1.astype(BF16)
    lo = (r1 - mid.astype(F32)).astype(BF16)
    dot = functools.partial(jnp.dot, preferred_element_type=F32)
    return dot(hi, seg) + dot(mid, seg) + dot(lo, seg)


def _rope(y, c, s1, s2):
    return y * c + pltpu.roll(y, LANES - ROPE_DIMS // 2, 1) * s1 + pltpu.roll(y, ROPE_DIMS // 2, 1) * s2


def _unrope(dy, c, s1, s2):
    return dy * c + pltpu.roll(dy * s1, ROPE_DIMS // 2, 1) + pltpu.roll(dy * s2, LANES - ROPE_DIMS // 2, 1)


def _sigmoid(x):
    return 1.0 / (1.0 + jnp.exp(-x))


def _rms_fwd(x, gain, name):
    s_len, d = x.shape
    tm = 512

    def body(x_ref, g_ref, h_ref, ht_ref, r_ref):
        xv = x_ref[...]
        r = lax.rsqrt(jnp.mean(xv * xv, axis=-1, keepdims=True) + EPS)
        h = ((xv * r) * g_ref[...]).astype(BF16)
        h_ref[...] = h
        ht_ref[...] = h.T
        r_ref[...] = r

    return pl.pallas_call(
        body, name=name, grid=(s_len // tm,),
        in_specs=[pl.BlockSpec((tm, d), lambda i: (i, 0)), pl.BlockSpec((1, d), lambda i: (0, 0))],
        out_specs=(pl.BlockSpec((tm, d), lambda i: (i, 0)), pl.BlockSpec((d, tm), lambda i: (0, i)),
                   pl.BlockSpec((tm, 1), lambda i: (i, 0))),
        out_shape=(jax.ShapeDtypeStruct((s_len, d), BF16), jax.ShapeDtypeStruct((d, s_len), BF16),
                   jax.ShapeDtypeStruct((s_len, 1), F32)),
        compiler_params=_cparams(1),
    )(x, gain)


def _resident(shape, index_map):
    return pl.BlockSpec(shape, index_map, pipeline_mode=pl.Buffered(1))


def _mm_rows(pairs, name, nt=False, tm=512, bias=None, sigmoid=False, res=None, out_dtypes=(F32,), loss_target=None,
             rms_bwd=None):
    m = pairs[0][0].shape[0]
    n = pairs[0][1].shape[0] if nt else pairs[0][1].shape[1]
    n_pairs = len(pairs)
    has_bias, has_res, has_loss = bias is not None, res is not None, loss_target is not None
    has_rms = rms_bwd is not None
    dims = (((1,), (1,)), ((), ())) if nt else (((1,), (0,)), ((), ()))

    def body(*refs):
        acc = None
        for p in range(n_pairs):
            t = lax.dot_general(refs[2 * p][...].astype(BF16), refs[2 * p + 1][...], dims, preferred_element_type=F32)
            acc = t if acc is None else acc + t
        pos = 2 * n_pairs
        if has_bias:
            acc = acc + refs[pos][...]
            pos += 1
        if sigmoid:
            acc = _sigmoid(acc)
        if has_res:
            acc = refs[pos][...] + acc
            pos += 1
        if has_loss:
            dy_ref, dyb_ref, l_ref = refs[pos + 1:]

            @pl.when(pl.program_id(0) == 0)
            def _():
                l_ref[...] = jnp.zeros_like(l_ref)
            err = acc - refs[pos][...]
            dy = err * (1.0 / n)
            dy_ref[...] = dy
            dyb_ref[...] = dy.astype(BF16)
            part = 0.5 * jnp.sum(jnp.mean(err * err, axis=-1, keepdims=True), axis=0, keepdims=True)
            l_ref[...] += jnp.broadcast_to(part, l_ref.shape)
            return
        if has_rms:
            x_ref, r_ref, g_ref, add_ref = refs[pos:pos + 4]
            dg_ref = refs[-1]

            @pl.when(pl.program_id(0) == 0)
            def _():
                dg_ref[...] = jnp.zeros_like(dg_ref)
            rv = r_ref[...]
            xhat = x_ref[...] * rv
            dg_ref[...] += jnp.sum(acc * xhat, axis=0, keepdims=True)
            dxhat = acc * g_ref[...]
            acc = add_ref[...] + rv * (dxhat - xhat * jnp.mean(dxhat * xhat, axis=-1, keepdims=True))
            for o_ref in refs[pos + 4:-1]:
                o_ref[...] = acc.astype(o_ref.dtype)
            return
        for o_ref in refs[pos:]:
            o_ref[...] = acc.astype(o_ref.dtype)

    in_specs, args = [], []
    for a, w, blk in pairs:
        k = a.shape[1]
        in_specs.append(pl.BlockSpec((tm, k), lambda i: (i, 0)))
        if nt:
            in_specs.append(_resident((n, k), lambda i, blk=blk: (0, blk)))
        else:
            in_specs.append(_resident((k, n), lambda i, blk=blk: (blk, 0)))
        args += [a, w]
    if has_bias:
        in_specs.append(_resident((1, n), lambda i: (0, 0)))
        args.append(bias)
    if has_res:
        in_specs.append(pl.BlockSpec((tm, n), lambda i: (i, 0)))
        args.append(res)
    out = pl.BlockSpec((tm, n), lambda i: (i, 0))
    if has_loss:
        return pl.pallas_call(
            body, name=name, grid=(m // tm,), in_specs=in_specs + [out],
            out_specs=(out, out, pl.BlockSpec((8, LANES), lambda i: (0, 0))),
            out_shape=(jax.ShapeDtypeStruct((m, n), F32), jax.ShapeDtypeStruct((m, n), BF16),
                       jax.ShapeDtypeStruct((8, LANES), F32)),
            compiler_params=_cparams(1),
        )(*args, loss_target)
    if has_rms:
        x, r, gain, add = rms_bwd
        vec = _resident((1, n), lambda i: (0, 0))
        return pl.pallas_call(
            body, name=name, grid=(m // tm,),
            in_specs=in_specs + [out, pl.BlockSpec((tm, 1), lambda i: (i, 0)), vec, out],
            out_specs=tuple([out] * len(out_dtypes) + [pl.BlockSpec((1, n), lambda i: (0, 0))]),
            out_shape=tuple([jax.ShapeDtypeStruct((m, n), dt) for dt in out_dtypes] + [jax.ShapeDtypeStruct((1, n), F32)]),
            compiler_params=_cparams(1),
        )(*args, x, r, gain, add)
    outs = pl.pallas_call(
        body, name=name, grid=(m // tm,), in_specs=in_specs, out_specs=tuple([out] * len(out_dtypes)),
        out_shape=tuple(jax.ShapeDtypeStruct((m, n), dt) for dt in out_dtypes), compiler_params=_cparams(1),
    )(*args)
    return outs[0] if len(out_dtypes) == 1 else outs


def _mm_rows_cat(a, ws, name, tm=256):
    m, k = a.shape
    widths = [w.shape[1] for w in ws]
    n = sum(widths)

    def body(*refs):
        a_ref, o_ref = refs[0], refs[-1]
        av = a_ref[...]
        off = 0
        for p, width in enumerate(widths):
            o_ref[:, off:off + width] = jnp.dot(av, refs[1 + p][...], preferred_element_type=F32)
            off += width

    return pl.pallas_call(
        body, name=name, grid=(m // tm,),
        in_specs=[pl.BlockSpec((tm, k), lambda i: (i, 0))] + [_resident((k, wd), lambda i: (0, 0)) for wd in widths],
        out_specs=pl.BlockSpec((tm, n), lambda i: (i, 0)),
        out_shape=jax.ShapeDtypeStruct((m, n), F32), compiler_params=_cparams(1),
    )(a, *ws)


def _mm_cols(a, b, name, tn=256):
    m, k = a.shape
    n = b.shape[1]

    def body(a_ref, b_ref, o_ref):
        o_ref[...] = jnp.dot(a_ref[...], b_ref[...].astype(BF16), preferred_element_type=F32)

    return pl.pallas_call(
        body, name=name, grid=(n // tn,),
        in_specs=[_resident((m, k), lambda j: (0, 0)), pl.BlockSpec((k, tn), lambda j: (0, j))],
        out_specs=pl.BlockSpec((m, tn), lambda j: (0, j)),
        out_shape=jax.ShapeDtypeStruct((m, n), F32), compiler_params=_cparams(1),
    )(a, b)


def _mm_tn(a, b, name, tile=256):
    k, m = a.shape
    n = b.shape[1]
    dims = (((0,), (0,)), ((), ()))

    def body(a_ref, b_ref, o_ref):
        o_ref[...] = lax.dot_general(a_ref[...].astype(BF16), b_ref[...].astype(BF16), dims, preferred_element_type=F32)

    if n <= m:
        t = min(tile, m)
        grid, a_spec, b_spec = (m // t,), pl.BlockSpec((k, t), lambda i: (0, i)), _resident((k, n), lambda i: (0, 0))
        o_spec = pl.BlockSpec((t, n), lambda i: (i, 0))
    else:
        t = min(tile, n)
        grid, a_spec, b_spec = (n // t,), _resident((k, m), lambda i: (0, 0)), pl.BlockSpec((k, t), lambda i: (0, i))
        o_spec = pl.BlockSpec((m, t), lambda i: (0, i))
    return pl.pallas_call(
        body, name=name, grid=grid, in_specs=[a_spec, b_spec], out_specs=o_spec,
        out_shape=jax.ShapeDtypeStruct((m, n), F32), compiler_params=_cparams(1),
    )(a, b)


def _norm_rope(t, gain, c, s1, s2, seg):
    rs = lax.rsqrt(_seg_sum(t * t, seg) * (1.0 / HEAD_DIM) + EPS)
    return _rope((t * rs) * gain, c, s1, s2)


def _dup_half(y, half):
    lane = lax.broadcasted_iota(jnp.int32, y.shape, 1)
    rolled = pltpu.roll(y, HEAD_DIM, 1)
    keep = (lane < HEAD_DIM) if half == 0 else (lane >= HEAD_DIM)
    return jnp.where(keep, y, rolled)


def _qk_prep(proj, cb0, d, gqa, gq, gk, tabs, name):
    s_len = proj.shape[0]
    tm = 512
    rows = tm // d
    n_units = 4 if gqa else 2 * d
    n_q = 4 if gqa else 2
    n_in = 6

    def body(*refs):
        in_refs = refs[:n_in]
        gq_ref, gk_ref, c_ref, s1_ref, s2_ref, o_ref = refs[n_in:]
        seg = _seg_matrix(HEAD_DIM)

        def rows_of(ref, r):
            return ref[...] if d == 1 else ref[pl.ds(r, rows, stride=d), :]

        def put(unit_col, y):
            o_ref[:, unit_col * LANES:(unit_col + 1) * LANES] = y.astype(BF16)

        for r in range(d):
            c, s1, s2 = rows_of(c_ref, r), rows_of(s1_ref, r), rows_of(s2_ref, r)
            for b in range(n_in):
                t = rows_of(in_refs[b], r)
                if b < n_q:
                    put((b * d + r) if not gqa else b, _norm_rope(t, gq_ref[...], c, s1, s2, seg))
                elif not gqa:
                    sec, pair = (1, b - 2) if b < 4 else (2, b - 4)
                    y = _norm_rope(t, gk_ref[...], c, s1, s2, seg) if sec == 1 else t
                    put(sec * n_units + pair * d + r, y)
                else:
                    sec = 1 if b == 4 else 2
                    y = _norm_rope(t, gk_ref[...], c, s1, s2, seg) if sec == 1 else t
                    for u in range(n_units):
                        put(sec * n_units + u, _dup_half(y, u // 2))

    in_specs = [pl.BlockSpec((tm, LANES), lambda i, b=b: (i, cb0 + b)) for b in range(n_in)]
    vec = pl.BlockSpec((1, LANES), lambda i: (0, 0))
    tab = pl.BlockSpec((tm, LANES), lambda i: (i, 0))
    width = 3 * n_units * LANES
    return pl.pallas_call(
        body, name=name, grid=(s_len // tm,), in_specs=in_specs + [vec, vec, tab, tab, tab],
        out_specs=pl.BlockSpec((rows, width), lambda i: (i, 0)),
        out_shape=jax.ShapeDtypeStruct((s_len // d, width), BF16), compiler_params=_cparams(1),
    )(*([proj] * n_in), gq, gk, *tabs)


def _qk_prep_bwd(dqkv, proj, cb0, d, gqa, gq, gk, tabs, name):
    s_len = proj.shape[0]
    tm = 512
    rows = tm // d
    n_units = 4 if gqa else 2 * d
    n_q = 4 if gqa else 2
    n_in = 6

    def body(*refs):
        d_refs = refs[0:3]
        in_refs = refs[3:3 + n_in]
        gq_ref, gk_ref, c_ref, s1_ref, s2_ref, o_ref, dgq_ref, dgk_ref, stage = refs[3 + n_in:]
        seg = _seg_matrix(HEAD_DIM)

        @pl.when(pl.program_id(0) == 0)
        def _():
            dgq_ref[...] = jnp.zeros_like(dgq_ref)
            dgk_ref[...] = jnp.zeros_like(dgk_ref)

        def rows_of(ref, r):
            return ref[...] if d == 1 else ref[pl.ds(r, rows, stride=d), :]

        def unit(col):
            sec, u = divmod(col, n_units)
            return d_refs[sec][:, u * LANES:(u + 1) * LANES]

        def norm_bwd(dyr, t, gain, c, s1, s2, dg_ref):
            rs = lax.rsqrt(_seg_sum(t * t, seg) * (1.0 / HEAD_DIM) + EPS)
            that = t * rs
            dy = _unrope(dyr, c, s1, s2)
            dg_ref[...] += jnp.sum(dy * that, axis=0, keepdims=True)
            dthat = dy * gain
            return rs * (dthat - that * (_seg_sum(dthat * that, seg) * (1.0 / HEAD_DIM)))

        def fold(sec):
            tot = []
            for u in range(n_units):
                v = unit(sec * n_units + u)
                tot.append(v + pltpu.roll(v, HEAD_DIM, 1))
            lane = lax.broadcasted_iota(jnp.int32, tot[0].shape, 1)
            return jnp.where(lane < HEAD_DIM, tot[0] + tot[1], tot[2] + tot[3])

        for b in range(n_in):
            for r in range(d):
                c, s1, s2 = rows_of(c_ref, r), rows_of(s1_ref, r), rows_of(s2_ref, r)
                t = rows_of(in_refs[b], r)
                if b < n_q:
                    g = unit((b * d + r) if not gqa else b)
                    out = norm_bwd(g, t, gq_ref[...], c, s1, s2, dgq_ref)
                elif not gqa:
                    sec, pair = (1, b - 2) if b < 4 else (2, b - 4)
                    g = unit(sec * n_units + pair * d + r)
                    out = norm_bwd(g, t, gk_ref[...], c, s1, s2, dgk_ref) if sec == 1 else g
                else:
                    sec = 1 if b == 4 else 2
                    g = fold(sec)
                    out = norm_bwd(g, t, gk_ref[...], c, s1, s2, dgk_ref) if sec == 1 else g
                if d == 1:
                    o_ref[:, b * LANES:(b + 1) * LANES] = out.astype(BF16)
                else:
                    stage[pl.ds(r, rows, stride=d), :] = out
            if d != 1:
                o_ref[:, b * LANES:(b + 1) * LANES] = stage[...].astype(BF16)

    in_specs = [pl.BlockSpec((rows, n_units * LANES), lambda i: (i, 0))] * 3
    in_specs += [pl.BlockSpec((tm, LANES), lambda i, b=b: (i, cb0 + b)) for b in range(n_in)]
    vec = pl.BlockSpec((1, LANES), lambda i: (0, 0))
    tab = pl.BlockSpec((tm, LANES), lambda i: (i, 0))
    return pl.pallas_call(
        body, name=name, grid=(s_len // tm,), in_specs=in_specs + [vec, vec, tab, tab, tab],
        out_specs=(pl.BlockSpec((tm, n_in * LANES), lambda i: (i, 0)), vec, vec),
        out_shape=(jax.ShapeDtypeStruct((s_len, n_in * LANES), BF16), jax.ShapeDtypeStruct((1, LANES), F32),
                   jax.ShapeDtypeStruct((1, LANES), F32)),
        scratch_shapes=[pltpu.VMEM((tm, LANES), F32)], compiler_params=_cparams(1),
    )(*dqkv, *([proj] * n_in), gq, gk, *tabs)


def _head_masks(shape):
    lane = lax.broadcasted_iota(jnp.int32, shape, 1)
    return lane < HEAD_DIM, lane >= HEAD_DIM


def _band_fwd(qkv, n_units, max_dist, sinks, name):
    n_rows = qkv.shape[0]
    nb = n_rows // BLOCK
    scale = HEAD_DIM ** -0.5
    has_sink = sinks is not None
    assert not has_sink or max_dist < BLOCK

    qn, un = min(nb, BAND_Q_BLOCKS), BAND_UNITS
    ug = n_units // un

    def body(*refs):
        q_ref, kp_ref, km_ref, vp_ref, vm_ref = refs[:5]
        o_ref, lse_ref = refs[-2:]
        i = pl.program_id(1)
        qi = lax.broadcasted_iota(jnp.int32, (BLOCK, 2 * BLOCK), 0)
        kj = lax.broadcasted_iota(jnp.int32, (BLOCK, 2 * BLOCK), 1)
        dist = qi + BLOCK - kj
        band = (dist >= 0) & (dist <= max_dist)
        band_first = band & ((i > 0) | (kj >= BLOCK))
        m0, m1 = _head_masks((BLOCK, LANES))
        zero = jnp.zeros((BLOCK, LANES), BF16)
        for ub in range(un):
            cs = slice(ub * LANES, (ub + 1) * LANES)
            for qb in range(qn):
                rs = slice(qb * BLOCK, (qb + 1) * BLOCK)
                q = q_ref[rs, cs]
                if qb == 0:
                    kk = jnp.concatenate([kp_ref[:, cs], km_ref[0:BLOCK, cs]], axis=0)
                    vv = jnp.concatenate([vp_ref[:, cs], vm_ref[0:BLOCK, cs]], axis=0)
                    valid = band_first
                else:
                    kk = km_ref[(qb - 1) * BLOCK:(qb + 1) * BLOCK, cs]
                    vv = vm_ref[(qb - 1) * BLOCK:(qb + 1) * BLOCK, cs]
                    valid = band
                outs, lses = [], []
                for e, hm in enumerate((m0, m1)):
                    qe = jnp.where(hm, q, zero)
                    s = lax.dot_general(qe, kk, (((1,), (1,)), ((), ())), preferred_element_type=F32) * scale
                    s = jnp.where(valid, s, -jnp.inf)
                    if has_sink:
                        s = jnp.where(kj == 0, refs[5][ub][:, e * HEAD_DIM:e * HEAD_DIM + 1], s)
                    mx = jnp.max(s, axis=-1, keepdims=True)
                    p = jnp.exp(s - mx)
                    den = jnp.sum(p, axis=-1, keepdims=True)
                    pn = p * (1.0 / den)
                    if has_sink:
                        pn = jnp.where(kj == 0, 0.0, pn)
                    pn = pn.astype(BF16)
                    outs.append(jnp.dot(pn, vv, preferred_element_type=F32))
                    lses.append(mx + jnp.log(den))
                o_ref[rs, cs] = jnp.where(m0, outs[0], outs[1])
                lse_ref[rs, cs] = jnp.where(m0, jnp.broadcast_to(lses[0], (BLOCK, LANES)),
                                            jnp.broadcast_to(lses[1], (BLOCK, LANES)))

    def main(sec):
        return pl.BlockSpec((qn * BLOCK, un * LANES), lambda u, i: (i, sec * ug + u))

    def prev(sec):
        return pl.BlockSpec((BLOCK, un * LANES), lambda u, i: (jnp.maximum(i * qn - 1, 0), sec * ug + u))

    in_specs = [main(0), prev(1), main(1), prev(2), main(2)]
    args = [qkv] * 5
    if has_sink:
        in_specs.append(pl.BlockSpec((un, 1, LANES), lambda u, i: (u, 0, 0)))
        args.append(sinks)
    return pl.pallas_call(
        body, name=name, grid=(ug, nb // qn), in_specs=in_specs, out_specs=(main(0), main(0)),
        out_shape=(jax.ShapeDtypeStruct((n_rows, n_units * LANES), F32),) * 2, compiler_params=_cparams(2),
    )(*args)


def _band_bwd(qkv, do, lse, delta, n_units, max_dist, name):
    n_rows = qkv.shape[0]
    nb = n_rows // BLOCK
    scale = HEAD_DIM ** -0.5

    qn, un = min(nb, BAND_Q_BLOCKS), BAND_UNITS
    ug = n_units // un
    steps = nb // qn
    nt_dims = (((1,), (1,)), ((), ()))
    tn_dims = (((0,), (0,)), ((), ()))

    def body(qm_ref, qx_ref, kp_ref, km_ref, vp_ref, vm_ref, dom_ref, dox_ref, lm_ref, lx_ref, dm_ref, dx_ref,
             dq_ref, dk_ref, dv_ref):
        i = pl.program_id(1)
        m0, m1 = _head_masks((BLOCK, LANES))
        zero = jnp.zeros((BLOCK, LANES), BF16)
        qi = lax.broadcasted_iota(jnp.int32, (BLOCK, 2 * BLOCK), 0)
        kj = lax.broadcasted_iota(jnp.int32, (BLOCK, 2 * BLOCK), 1)
        dist = qi + BLOCK - kj
        band = (dist >= 0) & (dist <= max_dist)
        band_first = band & ((i > 0) | (kj >= BLOCK))
        qr = lax.broadcasted_iota(jnp.int32, (BLOCK, BLOCK), 0)
        kc = lax.broadcasted_iota(jnp.int32, (BLOCK, BLOCK), 1)
        dist_x = qr + BLOCK - kc
        band_next = (dist_x >= 0) & (dist_x <= max_dist) & (i < steps - 1)

        def pair(q, dob, lse_b, del_b, kk, vv, valid):
            dqs, dk, dv = [], None, None
            for e, hm in enumerate((m0, m1)):
                col = slice(e * HEAD_DIM, e * HEAD_DIM + 1)
                qe = jnp.where(hm, q, zero)
                doe = jnp.where(hm, dob, zero)
                s = lax.dot_general(qe, kk, nt_dims, preferred_element_type=F32) * scale
                p = jnp.where(valid, jnp.exp(s - lse_b[:, col]), 0.0)
                dp = lax.dot_general(doe, vv, nt_dims, preferred_element_type=F32)
                ds = (p * (dp - del_b[:, col]) * scale).astype(BF16)
                dqs.append(jnp.dot(ds, kk, preferred_element_type=F32))
                dk_e = lax.dot_general(ds, qe, tn_dims, preferred_element_type=F32)
                dv_e = lax.dot_general(p.astype(BF16), doe, tn_dims, preferred_element_type=F32)
                dk = dk_e if dk is None else dk + dk_e
                dv = dv_e if dv is None else dv + dv_e
            return jnp.where(m0, dqs[0], dqs[1]), dk, dv

        for ub in range(un):
            cs = slice(ub * LANES, (ub + 1) * LANES)
            dk_acc, dv_acc = [None] * qn, [None] * qn

            def add(acc, kb, part):
                acc[kb] = part if acc[kb] is None else acc[kb] + part

            for qb in range(qn):
                rs = slice(qb * BLOCK, (qb + 1) * BLOCK)
                if qb == 0:
                    kk = jnp.concatenate([kp_ref[:, cs], km_ref[0:BLOCK, cs]], axis=0)
                    vv = jnp.concatenate([vp_ref[:, cs], vm_ref[0:BLOCK, cs]], axis=0)
                    valid = band_first
                else:
                    kk = km_ref[(qb - 1) * BLOCK:(qb + 1) * BLOCK, cs]
                    vv = vm_ref[(qb - 1) * BLOCK:(qb + 1) * BLOCK, cs]
                    valid = band
                dq, dk, dv = pair(qm_ref[rs, cs], dom_ref[rs, cs], lm_ref[rs, cs], dm_ref[rs, cs], kk, vv, valid)
                dq_ref[rs, cs] = dq
                if qb > 0:
                    add(dk_acc, qb - 1, dk[0:BLOCK])
                    add(dv_acc, qb - 1, dv[0:BLOCK])
                add(dk_acc, qb, dk[BLOCK:2 * BLOCK])
                add(dv_acc, qb, dv[BLOCK:2 * BLOCK])
            last = slice((qn - 1) * BLOCK, qn * BLOCK)
            _, dk, dv = pair(qx_ref[:, cs], dox_ref[:, cs], lx_ref[:, cs], dx_ref[:, cs], km_ref[last, cs], vm_ref[last, cs],
                             band_next)
            add(dk_acc, qn - 1, dk)
            add(dv_acc, qn - 1, dv)
            for kb in range(qn):
                dk_ref[kb * BLOCK:(kb + 1) * BLOCK, cs] = dk_acc[kb]
                dv_ref[kb * BLOCK:(kb + 1) * BLOCK, cs] = dv_acc[kb]

    def main(sec):
        return pl.BlockSpec((qn * BLOCK, un * LANES), lambda u, i: (i, sec * ug + u))

    def prev(sec):
        return pl.BlockSpec((BLOCK, un * LANES), lambda u, i: (jnp.maximum(i * qn - 1, 0), sec * ug + u))

    def nxt(sec):
        return pl.BlockSpec((BLOCK, un * LANES), lambda u, i: (jnp.minimum((i + 1) * qn, nb - 1), sec * ug + u))

    in_specs = [main(0), nxt(0), prev(1), main(1), prev(2), main(2),
                main(0), nxt(0), main(0), nxt(0), main(0), nxt(0)]
    args = [qkv] * 6 + [do, do, lse, lse, delta, delta]
    shp = jax.ShapeDtypeStruct((n_rows, n_units * LANES), F32)
    return pl.pallas_call(
        body, name=name, grid=(ug, steps), in_specs=in_specs, out_specs=(main(0), main(0), main(0)),
        out_shape=(shp, shp, shp), compiler_params=_cparams(2),
    )(*args)


def _merge_groups(os_, lses, dils, name):
    s_len = os_[0].shape[0] * dils[0]
    tm = 512

    def body(*refs):
        o_refs, l_refs = refs[0:3], refs[3:6]
        o_ref, lse_ref = refs[6:8]
        so, sl = refs[8:11], refs[11:14]
        for pair in range(2):
            for g, d in enumerate(dils):
                rows = tm // d
                for r in range(d):
                    col = slice((pair * d + r) * LANES, (pair * d + r + 1) * LANES)
                    if d == 1:
                        so[g][...] = o_refs[g][:, col]
                        sl[g][...] = l_refs[g][:, col]
                    else:
                        so[g][pl.ds(r, rows, stride=d), :] = o_refs[g][:, col]
                        sl[g][pl.ds(r, rows, stride=d), :] = l_refs[g][:, col]
            l0, l1, l2 = sl[0][...], sl[1][...], sl[2][...]
            mx = jnp.maximum(jnp.maximum(l0, l1), l2)
            e0, e1, e2 = jnp.exp(l0 - mx), jnp.exp(l1 - mx), jnp.exp(l2 - mx)
            den = e0 + e1 + e2
            inv = 1.0 / den
            o_ref[:, pair * LANES:(pair + 1) * LANES] = (so[0][...] * (e0 * inv) + so[1][...] * (e1 * inv)
                                                         + so[2][...] * (e2 * inv))
            lse_ref[:, pair * LANES:(pair + 1) * LANES] = mx + jnp.log(den)

    in_specs = [pl.BlockSpec((tm // d, 2 * d * LANES), lambda i: (i, 0)) for d in dils] * 2
    out = pl.BlockSpec((tm, 2 * LANES), lambda i: (i, 0))
    shp = jax.ShapeDtypeStruct((s_len, 2 * LANES), F32)
    return pl.pallas_call(
        body, name=name, grid=(s_len // tm,), in_specs=in_specs, out_specs=(out, out), out_shape=(shp, shp),
        scratch_shapes=[pltpu.VMEM((tm, LANES), F32)] * 6, compiler_params=_cparams(1),
    )(*os_, *lses)


def _bwd_prep(do, o, lse, dils, sinks, name):
    s_len, width = do.shape
    n_pairs = width // LANES
    tm = 512
    has_sink = sinks is not None
    n_g = len(dils)

    def body(*refs):
        do_ref, o_ref, lse_ref = refs[:3]
        pos = 3
        if has_sink:
            sink_ref = refs[pos]
            pos += 1
        outs = refs[pos:pos + 3 * n_g]
        pos += 3 * n_g
        if has_sink:
            dsink_ref = refs[pos]
            pos += 1
        s_do, s_l, s_d = refs[pos:pos + 3]
        seg = _seg_matrix(HEAD_DIM)

        if has_sink:
            @pl.when(pl.program_id(0) == 0)
            def _():
                dsink_ref[...] = jnp.zeros_like(dsink_ref)

        for pair in range(n_pairs):
            col = slice(pair * LANES, (pair + 1) * LANES)
            dov = do_ref[:, col]
            lv = lse_ref[:, col]
            delta = _seg_sum(dov * o_ref[:, col], seg)
            if has_sink:
                dsink_ref[pair] += -jnp.sum(jnp.exp(sink_ref[pair] - lv) * delta, axis=0, keepdims=True)
            s_do[...] = dov
            s_l[...] = lv
            s_d[...] = delta
            for g, d in enumerate(dils):
                rows = tm // d
                for r in range(d):
                    oc = slice((pair * d + r) * LANES, (pair * d + r + 1) * LANES)
                    if d == 1:
                        a, b, c = s_do[...], s_l[...], s_d[...]
                    else:
                        a = s_do[pl.ds(r, rows, stride=d), :]
                        b = s_l[pl.ds(r, rows, stride=d), :]
                        c = s_d[pl.ds(r, rows, stride=d), :]
                    outs[3 * g][:, oc] = a.astype(BF16)
                    outs[3 * g + 1][:, oc] = b
                    outs[3 * g + 2][:, oc] = c

    row = pl.BlockSpec((tm, width), lambda i: (i, 0))
    in_specs = [row, row, row]
    args = [do, o, lse]
    if has_sink:
        in_specs.append(pl.BlockSpec((n_pairs, 1, LANES), lambda i: (0, 0, 0)))
        args.append(sinks)
    out_specs, out_shape = [], []
    for d in dils:
        for dt in (BF16, F32, F32):
            out_specs.append(pl.BlockSpec((tm // d, n_pairs * d * LANES), lambda i: (i, 0)))
            out_shape.append(jax.ShapeDtypeStruct((s_len // d, n_pairs * d * LANES), dt))
    if has_sink:
        out_specs.append(pl.BlockSpec((n_pairs, 1, LANES), lambda i: (0, 0, 0)))
        out_shape.append(jax.ShapeDtypeStruct((n_pairs, 1, LANES), F32))
    return pl.pallas_call(
        body, name=name, grid=(s_len // tm,), in_specs=in_specs, out_specs=tuple(out_specs),
        out_shape=tuple(out_shape), scratch_shapes=[pltpu.VMEM((tm, LANES), F32)] * 3, compiler_params=_cparams(1),
    )(*args)


def _mem_kv(mem, mem_gain, w_kv, k_gain, name):
    m_len = mem.shape[0]
    kw = M_HEADS * M_HEAD_DIM

    def body(mem_ref, mg_ref, w_ref, kg_ref, k_ref, v_ref):
        mv = mem_ref[...]
        r = lax.rsqrt(jnp.mean(mv * mv, axis=-1, keepdims=True) + EPS)
        mn = ((mv * r) * mg_ref[...]).astype(BF16)
        kv = jnp.dot(mn, w_ref[...], preferred_element_type=F32)
        for h in range(M_HEADS):
            col = slice(h * M_HEAD_DIM, (h + 1) * M_HEAD_DIM)
            t = kv[:, col]
            rk = lax.rsqrt(jnp.mean(t * t, axis=-1, keepdims=True) + EPS)
            k_ref[:, col] = ((t * rk) * kg_ref[...]).astype(BF16)
        v_ref[...] = kv[:, kw:].astype(BF16)

    shp = jax.ShapeDtypeStruct((m_len, kw), BF16)
    return pl.pallas_call(body, name=name, out_shape=(shp, shp),
                          compiler_params=pltpu.CompilerParams(vmem_limit_bytes=VMEM_LIMIT_BYTES))(mem, mem_gain, w_kv, k_gain)


def _mem_kv_bwd(mem, mem_gain, w_kv, k_gain, dk, dv, name):
    m_len, d = mem.shape
    kw = M_HEADS * M_HEAD_DIM

    def body(mem_ref, mg_ref, w_ref, kg_ref, dk_ref, dv_ref, dw_ref, dmg_ref, dkg_ref, dkv_ref):
        mv = mem_ref[...]
        r = lax.rsqrt(jnp.mean(mv * mv, axis=-1, keepdims=True) + EPS)
        mhat = mv * r
        mn = (mhat * mg_ref[...]).astype(BF16)
        kv = jnp.dot(mn, w_ref[...], preferred_element_type=F32)
        dkg = jnp.zeros((1, M_HEAD_DIM), F32)
        for h in range(M_HEADS):
            col = slice(h * M_HEAD_DIM, (h + 1) * M_HEAD_DIM)
            t = kv[:, col]
            rk = lax.rsqrt(jnp.mean(t * t, axis=-1, keepdims=True) + EPS)
            that = t * rk
            dy = dk_ref[:, col]
            dkg = dkg + jnp.sum(dy * that, axis=0, keepdims=True)
            dthat = dy * kg_ref[...]
            dkv_ref[:, col] = (rk * (dthat - that * jnp.mean(dthat * that, axis=-1, keepdims=True))).astype(BF16)
        dkv_ref[:, kw:] = dv_ref[...].astype(BF16)
        dkg_ref[...] = dkg
        dkv = dkv_ref[...]
        dw_ref[...] = lax.dot_general(mn, dkv, (((0,), (0,)), ((), ())), preferred_element_type=F32)
        dmn = lax.dot_general(dkv, w_ref[...], (((1,), (1,)), ((), ())), preferred_element_type=F32)
        dmg_ref[...] = jnp.sum(dmn * mhat, axis=0, keepdims=True)

    return pl.pallas_call(
        body, name=name,
        out_shape=(jax.ShapeDtypeStruct((d, 2 * kw), F32), jax.ShapeDtypeStruct((1, d), F32),
                   jax.ShapeDtypeStruct((1, M_HEAD_DIM), F32)),
        scratch_shapes=[pltpu.VMEM((m_len, 2 * kw), BF16)],
        compiler_params=pltpu.CompilerParams(vmem_limit_bytes=VMEM_LIMIT_BYTES),
    )(mem, mem_gain, w_kv, k_gain, dk, dv)


def _mem_attn_fwd(proj, cidx, mk, mv, q_gain, name):
    s_len = proj.shape[0]
    kw = M_HEADS * M_HEAD_DIM
    tm = 512
    scale = M_HEAD_DIM ** -0.5

    def body(q_ref, k_ref, v_ref, g_ref, o_ref):
        for h in range(M_HEADS):
            col = slice(h * M_HEAD_DIM, (h + 1) * M_HEAD_DIM)
            t = q_ref[:, col]
            rs = lax.rsqrt(jnp.mean(t * t, axis=-1, keepdims=True) + EPS)
            qn = ((t * rs) * g_ref[...]).astype(BF16)
            s = lax.dot_general(qn, k_ref[:, col], (((1,), (1,)), ((), ())), preferred_element_type=F32) * scale
            mx = jnp.max(s, axis=-1, keepdims=True)
            p = jnp.exp(s - mx)
            pn = (p * (1.0 / jnp.sum(p, axis=-1, keepdims=True))).astype(BF16)
            o_ref[:, col] = jnp.dot(pn, v_ref[:, col], preferred_element_type=F32).astype(BF16)

    whole = pl.BlockSpec((MEM_LEN, kw), lambda i: (0, 0))
    return pl.pallas_call(
        body, name=name, grid=(s_len // tm,),
        in_specs=[pl.BlockSpec((tm, kw), lambda i: (i, cidx)), whole, whole, pl.BlockSpec((1, M_HEAD_DIM), lambda i: (0, 0))],
        out_specs=pl.BlockSpec((tm, kw), lambda i: (i, 0)),
        out_shape=jax.ShapeDtypeStruct((s_len, kw), BF16), compiler_params=_cparams(1),
    )(proj, mk, mv, q_gain)


def _mem_attn_bwd(proj, cidx, mk, mv, q_gain, do, name):
    s_len = proj.shape[0]
    kw = M_HEADS * M_HEAD_DIM
    tm = 512
    scale = M_HEAD_DIM ** -0.5

    def body(q_ref, k_ref, v_ref, g_ref, do_ref, dq_ref, dk_ref, dv_ref, dg_ref):
        @pl.when(pl.program_id(0) == 0)
        def _():
            dk_ref[...] = jnp.zeros_like(dk_ref)
            dv_ref[...] = jnp.zeros_like(dv_ref)
            dg_ref[...] = jnp.zeros_like(dg_ref)

        for h in range(M_HEADS):
            col = slice(h * M_HEAD_DIM, (h + 1) * M_HEAD_DIM)
            t = q_ref[:, col]
            rs = lax.rsqrt(jnp.mean(t * t, axis=-1, keepdims=True) + EPS)
            that = t * rs
            qn = (that * g_ref[...]).astype(BF16)
            kh, vh = k_ref[:, col], v_ref[:, col]
            dob = do_ref[:, col].astype(BF16)
            s = lax.dot_general(qn, kh, (((1,), (1,)), ((), ())), preferred_element_type=F32) * scale
            mx = jnp.max(s, axis=-1, keepdims=True)
            p = jnp.exp(s - mx)
            p = p * (1.0 / jnp.sum(p, axis=-1, keepdims=True))
            dp = lax.dot_general(dob, vh, (((1,), (1,)), ((), ())), preferred_element_type=F32)
            ds = (p * (dp - jnp.sum(p * dp, axis=-1, keepdims=True)) * scale).astype(BF16)
            dqn = jnp.dot(ds, kh, preferred_element_type=F32)
            dk_ref[:, col] += lax.dot_general(ds, qn, (((0,), (0,)), ((), ())), preferred_element_type=F32)
            dv_ref[:, col] += lax.dot_general(p.astype(BF16), dob, (((0,), (0,)), ((), ())), preferred_element_type=F32)
            dg_ref[...] += jnp.sum(dqn * that, axis=0, keepdims=True)
            dthat = dqn * g_ref[...]
            dq_ref[:, col] = (rs * (dthat - that * jnp.mean(dthat * that, axis=-1, keepdims=True))).astype(BF16)

    whole = pl.BlockSpec((MEM_LEN, kw), lambda i: (0, 0))
    vec = pl.BlockSpec((1, M_HEAD_DIM), lambda i: (0, 0))
    row = pl.BlockSpec((tm, kw), lambda i: (i, 0))
    return pl.pallas_call(
        body, name=name, grid=(s_len // tm,),
        in_specs=[pl.BlockSpec((tm, kw), lambda i: (i, cidx)), whole, whole, vec, row],
        out_specs=(row, whole, whole, vec),
        out_shape=(jax.ShapeDtypeStruct((s_len, kw), BF16), jax.ShapeDtypeStruct((MEM_LEN, kw), F32),
                   jax.ShapeDtypeStruct((MEM_LEN, kw), F32), jax.ShapeDtypeStruct((1, M_HEAD_DIM), F32)),
        compiler_params=_cparams(1),
    )(proj, mk, mv, q_gain, do)


def _project_merge(outs, w_outs, gates, name):
    s_len = gates.shape[0]
    d = w_outs[0].shape[1]
    tm = 512

    def body(oa_ref, ob_ref, om_ref, wa_ref, wb_ref, wm_ref, g_ref, pa_ref, pb_ref, pm_ref, merged_ref):
        merged = None
        for k, (o_ref, w_ref, p_ref) in enumerate(((oa_ref, wa_ref, pa_ref), (ob_ref, wb_ref, pb_ref), (om_ref, wm_ref, pm_ref))):
            p = jnp.dot(o_ref[...].astype(BF16), w_ref[...], preferred_element_type=F32).astype(BF16)
            p_ref[...] = p
            t = g_ref[:, k * d:(k + 1) * d].astype(F32) * p.astype(F32)
            merged = t if merged is None else merged + t
        merged_ref[...] = merged.astype(BF16)

    row = pl.BlockSpec((tm, d), lambda i: (i, 0))
    shp = jax.ShapeDtypeStruct((s_len, d), BF16)
    in_specs = [pl.BlockSpec((tm, o.shape[1]), lambda i: (i, 0)) for o in outs]
    in_specs += [_resident(w.shape, lambda i: (0, 0)) for w in w_outs]
    in_specs.append(pl.BlockSpec((tm, 3 * d), lambda i: (i, 0)))
    return pl.pallas_call(
        body, name=name, grid=(s_len // tm,), in_specs=in_specs, out_specs=(row, row, row, row),
        out_shape=(shp, shp, shp, shp), compiler_params=_cparams(1),
    )(*outs, *w_outs, gates)


def _project_merge_bwd(dx1, w_out, gates, pa, pb, pm, name):
    s_len, d = pa.shape
    tm = 512

    def body(dx_ref, w_ref, g_ref, a_ref, b_ref, m_ref, da_ref, db_ref, dmm_ref, dg_ref, dbg_ref):
        @pl.when(pl.program_id(0) == 0)
        def _():
            dbg_ref[...] = jnp.zeros_like(dbg_ref)
        dm = lax.dot_general(dx_ref[...], w_ref[...], (((1,), (1,)), ((), ())), preferred_element_type=F32)
        for k, (p_ref, dp_ref) in enumerate(((a_ref, da_ref), (b_ref, db_ref), (m_ref, dmm_ref))):
            col = slice(k * d, (k + 1) * d)
            g = g_ref[:, col].astype(F32)
            dp_ref[...] = (dm * g).astype(BF16)
            dpre = (dm * p_ref[...].astype(F32)) * (g * (1.0 - g))
            dbg_ref[:, col] += jnp.sum(dpre, axis=0, keepdims=True)
            dg_ref[:, col] = dpre.astype(BF16)

    row = pl.BlockSpec((tm, d), lambda i: (i, 0))
    wide = pl.BlockSpec((tm, 3 * d), lambda i: (i, 0))
    shp = jax.ShapeDtypeStruct((s_len, d), BF16)
    return pl.pallas_call(
        body, name=name, grid=(s_len // tm,), in_specs=[row, _resident(w_out.shape, lambda i: (0, 0)), wide, row, row, row],
        out_specs=(row, row, row, wide, pl.BlockSpec((1, 3 * d), lambda i: (0, 0))),
        out_shape=(shp, shp, shp, jax.ShapeDtypeStruct((s_len, 3 * d), BF16), jax.ShapeDtypeStruct((1, 3 * d), F32)),
        compiler_params=_cparams(1),
    )(dx1, w_out, gates, pa, pb, pm)


CONV_CHUNK = 256


def _pick_row(tile, j):
    row = lax.broadcasted_iota(jnp.int32, tile.shape, 0)
    return jnp.sum(jnp.where(row == j, tile, jnp.zeros_like(tile)), axis=0, keepdims=True)


def _rows_before(ref, start, k):
    cur = ref[pl.ds(start, CONV_CHUNK), :].astype(F32)
    prev = ref[pl.ds(pl.multiple_of(jnp.maximum(start - 16, 0), 16), 16), :].astype(F32)
    prev = jnp.where(start > 0, prev, jnp.zeros_like(prev))
    rolled = pltpu.roll(cur, k, 0)
    row = lax.broadcasted_iota(jnp.int32, cur.shape, 0)
    for j in range(k):
        rolled = jnp.where(row == j, _pick_row(prev, 16 - k + j), rolled)
    return rolled


def _rows_after(ref, start, k):
    cur = ref[pl.ds(start, CONV_CHUNK), :]
    nxt = ref[pl.ds(pl.multiple_of(start + CONV_CHUNK, 8), 8), :]
    rolled = pltpu.roll(cur, CONV_CHUNK - k, 0)
    row = lax.broadcasted_iota(jnp.int32, cur.shape, 0)
    for j in range(k):
        rolled = jnp.where(row == CONV_CHUNK - k + j, _pick_row(nxt, j), rolled)
    return rolled


def _conv_pre(u_ref, w_ref, b_ref, start):
    u2 = _rows_before(u_ref, start, 2)
    u1 = _rows_before(u_ref, start, 1)
    u0 = u_ref[pl.ds(start, CONV_CHUNK), :].astype(F32)
    c = ((b_ref[...] + w_ref[0:1, :] * u2) + w_ref[1:2, :] * u1) + w_ref[2:3, :] * u0
    return c, (u2, u1, u0)


def _up_conv_glu(h2, w_up, conv_w, conv_b, name):
    s_len, d = h2.shape
    tm, tn = 512, 2 * LANES
    nblk = D_FF // tn

    def body(h_ref, w_ref, cw_ref, cb_ref, u_ref, act_ref, halo):
        @pl.when(pl.program_id(0) == 0)
        def _():
            halo[...] = jnp.zeros_like(halo)
        hv = h_ref[...]
        row = lax.broadcasted_iota(jnp.int32, (tm, tn), 0)
        for j in range(nblk):
            conv = []
            for half in range(2):
                cb = half * nblk + j
                cols = slice(cb * tn, (cb + 1) * tn)
                ub = jnp.dot(hv, w_ref[:, cols], preferred_element_type=F32).astype(BF16)
                u_ref[:, cols] = ub
                u0 = ub.astype(F32)
                prev = halo[cb]
                u1 = jnp.where(row == 0, _pick_row(prev, 7), pltpu.roll(u0, 1, 0))
                u2 = pltpu.roll(u0, 2, 0)
                u2 = jnp.where(row == 0, _pick_row(prev, 6), jnp.where(row == 1, _pick_row(prev, 7), u2))
                halo[cb] = u0[tm - 8:tm, :]
                conv.append(((cb_ref[:, cols] + cw_ref[0:1, cols] * u2) + cw_ref[1:2, cols] * u1)
                            + cw_ref[2:3, cols] * u0)
            act_ref[:, j * tn:(j + 1) * tn] = ((conv[0] * _sigmoid(conv[0])) * conv[1]).astype(BF16)

    return pl.pallas_call(
        body, name=name, grid=(s_len // tm,),
        in_specs=[pl.BlockSpec((tm, d), lambda i: (i, 0)), _resident((d, 2 * D_FF), lambda i: (0, 0)),
                  _resident((3, 2 * D_FF), lambda i: (0, 0)), _resident((1, 2 * D_FF), lambda i: (0, 0))],
        out_specs=(pl.BlockSpec((tm, 2 * D_FF), lambda i: (i, 0)), pl.BlockSpec((tm, D_FF), lambda i: (i, 0))),
        out_shape=(jax.ShapeDtypeStruct((s_len, 2 * D_FF), BF16), jax.ShapeDtypeStruct((s_len, D_FF), BF16)),
        scratch_shapes=[pltpu.VMEM((2 * nblk, 8, tn), F32)], compiler_params=_cparams(1),
    )(h2, w_up, conv_w, conv_b)


def _conv_glu_bwd(dact, u, conv_w, conv_b, name):
    s_len = u.shape[0]
    nblk = D_FF // LANES
    n_chunks = s_len // CONV_CHUNK

    def body(da_ref, ua_ref, ug_ref, wa_ref, wg_ref, ba_ref, bg_ref,
             dua_ref, dug_ref, dwa_ref, dwg_ref, dba_ref, dbg_ref, sa, sg):
        sa[pl.ds(s_len, 8), :] = jnp.zeros((8, LANES), F32)
        sg[pl.ds(s_len, 8), :] = jnp.zeros((8, LANES), F32)
        zero = jnp.zeros((1, LANES), F32)

        def chunk1(ci, carry):
            start = pl.multiple_of(ci * CONV_CHUNK, CONV_CHUNK)
            ca, ua = _conv_pre(ua_ref, wa_ref, ba_ref, start)
            cg, ug = _conv_pre(ug_ref, wg_ref, bg_ref, start)
            dact_v = da_ref[pl.ds(start, CONV_CHUNK), :].astype(F32)
            sig = _sigmoid(ca)
            dcg = dact_v * (ca * sig)
            dca = (dact_v * cg) * (sig * (1.0 + ca * (1.0 - sig)))
            sa[pl.ds(start, CONV_CHUNK), :] = dca
            sg[pl.ds(start, CONV_CHUNK), :] = dcg
            out = [carry[0] + jnp.sum(dca, axis=0, keepdims=True), carry[1] + jnp.sum(dcg, axis=0, keepdims=True)]
            for j in range(3):
                out.append(carry[2 + j] + jnp.sum(dca * ua[j], axis=0, keepdims=True))
            for j in range(3):
                out.append(carry[5 + j] + jnp.sum(dcg * ug[j], axis=0, keepdims=True))
            return tuple(out)

        acc = lax.fori_loop(0, n_chunks, chunk1, (zero,) * 8)
        dba_ref[...] = acc[0]
        dbg_ref[...] = acc[1]
        for j in range(3):
            dwa_ref[j:j + 1, :] = acc[2 + j]
            dwg_ref[j:j + 1, :] = acc[5 + j]

        def chunk2(ci, carry):
            start = pl.multiple_of(ci * CONV_CHUNK, CONV_CHUNK)
            for s_ref, w_ref, o_ref in ((sa, wa_ref, dua_ref), (sg, wg_ref, dug_ref)):
                d0 = s_ref[pl.ds(start, CONV_CHUNK), :]
                d1 = _rows_after(s_ref, start, 1)
                d2 = _rows_after(s_ref, start, 2)
                o_ref[pl.ds(start, CONV_CHUNK), :] = (w_ref[2:3, :] * d0 + w_ref[1:2, :] * d1
                                                      + w_ref[0:1, :] * d2).astype(BF16)
            return carry
        lax.fori_loop(0, n_chunks, chunk2, 0)

    def col(rows, off):
        return pl.BlockSpec((rows, LANES), lambda j: (0, off + j))

    big = jax.ShapeDtypeStruct((s_len, D_FF), BF16)
    return pl.pallas_call(
        body, name=name, grid=(nblk,),
        in_specs=[col(s_len, 0), col(s_len, 0), col(s_len, nblk), col(3, 0), col(3, nblk), col(1, 0), col(1, nblk)],
        out_specs=(col(s_len, 0), col(s_len, 0), col(3, 0), col(3, 0), col(1, 0), col(1, 0)),
        out_shape=(big, big, jax.ShapeDtypeStruct((3, D_FF), F32), jax.ShapeDtypeStruct((3, D_FF), F32),
                   jax.ShapeDtypeStruct((1, D_FF), F32), jax.ShapeDtypeStruct((1, D_FF), F32)),
        scratch_shapes=[pltpu.VMEM((s_len + 8, LANES), F32)] * 2, compiler_params=_cparams(1),
    )(dact, u, u, conv_w, conv_w, conv_b, conv_b)


def _rope_tables(positions):
    half = ROPE_DIMS // 2
    freqs = jnp.exp(jnp.arange(half, dtype=F32) * (-2.0 * math.log(ROPE_THETA) / ROPE_DIMS))
    ang = positions.reshape(-1).astype(F32)[:, None] * freqs
    cos, sin = jnp.cos(ang), jnp.sin(ang)
    n = ang.shape[0]
    zeros = lambda w: jnp.zeros((n, w), F32)
    c = jnp.concatenate([cos, cos, jnp.ones((n, HEAD_DIM - ROPE_DIMS), F32)], axis=1)
    s1 = jnp.concatenate([-sin, zeros(HEAD_DIM - half)], axis=1)
    s2 = jnp.concatenate([zeros(half), sin, zeros(HEAD_DIM - ROPE_DIMS)], axis=1)
    return tuple(jnp.tile(t, (1, 2)) for t in (c, s1, s2))


def _two(v):
    return jnp.tile(v.reshape(1, HEAD_DIM), (1, 2))


def _fold_heads(g):
    return g[0, :HEAD_DIM] + g[0, HEAD_DIM:]


MIX_WEIGHTS = ('w_gate', 'w_mem_kv', 'w_o_a', 'w_o_b', 'w_o_m', 'w_out')
FFN_WEIGHTS = ('w_up', 'conv_w', 'w_down')


def _device_step(x, mem, positions, target, w, hooks=None):
    tabs = _rope_tables(positions)
    dils = tuple(d for _, d in A_GROUPS)
    grads = {}
    w = dict(w)

    h, h_t, r1 = _rms_fwd(x, w['attn_norm'], "rms1")
    if hooks is not None:
        w.update(hooks.weights('in', [h, *tabs]))
    proj = _mm_rows([(h, w['w_in'], 0)], "mm_in")

    qkv_a, o_g, lse_g = [], [], []
    for gi, (window, d) in enumerate(A_GROUPS):
        gq, gk = _two(w['a_q_norm'][gi]), _two(w['a_k_norm'][gi])
        qkv = _qk_prep(proj, 6 * gi, d, False, gq, gk, tabs, f"qk_prep_a{gi}")
        o, lse = _band_fwd(qkv, 2 * d, window // d, None, f"band_fwd_a{gi}")
        qkv_a.append(qkv)
        o_g.append(o)
        lse_g.append(lse)
    o_a, lse_a = _merge_groups(o_g, lse_g, dils, "merge_a")
    if hooks is not None:
        w.update(hooks.weights('mix', [o_a]))

    gbq, gbk = _two(w['b_q_norm']), _two(w['b_k_norm'])
    sinks = jnp.repeat(w['b_sinks'].reshape(4, 2), HEAD_DIM, axis=1).reshape(4, 1, LANES)
    qkv_b = _qk_prep(proj, 18, 1, True, gbq, gbk, tabs, "qk_prep_b")
    o_b, lse_b = _band_fwd(qkv_b, 4, B_WINDOW - 1, sinks, "band_fwd_b")

    gates = _mm_rows([(h, w['w_gate'], 0)], "mm_gate", bias=w['b_gate'], sigmoid=True, out_dtypes=(BF16,))
    mk, mv = _mem_kv(mem, w['mem_norm'], w['w_mem_kv'], w['m_k_norm'], "mem_kv")
    o_m = _mem_attn_fwd(proj, 6, mk, mv, w['m_q_norm'], "mem_attn")

    pa, pb, pm, merged = _project_merge((o_a, o_b, o_m), (w['w_o_a'], w['w_o_b'], w['w_o_m']), gates, "project_merge")
    x1 = _mm_rows([(merged, w['w_out'], 0)], "mm_out", res=x)

    if hooks is not None:
        w.update(hooks.weights('ffn', [x1]))
    h2, h2_t, r2 = _rms_fwd(x1, w['ffn_norm'], "rms2")
    u, act = _up_conv_glu(h2, w['w_up'], w['conv_w'], w['conv_b'], "up_conv_glu")
    dy, dy_b, loss = _mm_rows([(act, w['w_down'], 0)], "mm_down", res=x1, loss_target=target)

    dact = _mm_rows([(dy_b, w['w_down'], 0)], "mm_d_act", nt=True, out_dtypes=(BF16,))
    grads['w_down'] = _mm_tn(act, dy_b, "mm_dw_down")
    du_a, du_g, dcw_a, dcw_g, dcb_a, dcb_g = _conv_glu_bwd(dact, u, w['conv_w'], w['conv_b'], "conv_glu_bwd")
    grads['conv_w'] = jnp.concatenate([dcw_a, dcw_g], axis=1)
    grads['conv_b'] = jnp.concatenate([dcb_a, dcb_g], axis=1)
    grads['w_up'] = jnp.concatenate([_mm_cols(h2_t, du_a, "mm_dw_up_a"), _mm_cols(h2_t, du_g, "mm_dw_up_g")], axis=1)
    ffn_gain = w['ffn_norm']
    if hooks is not None:
        ffn_gain = ffn_gain + hooks.grads('ffn', grads)[0:1, 0:1]
    dx1, dx1_b, grads['ffn_norm'] = _mm_rows([(du_a, w['w_up'], 0), (du_g, w['w_up'], 1)], "mm_d_h2", nt=True,
                                             rms_bwd=(x1, r2, ffn_gain, dy), out_dtypes=(F32, BF16))

    grads['w_out'] = _mm_tn(merged, dx1_b, "mm_dw_out")
    dpa, dpb, dpm, dgpre, grads['b_gate'] = _project_merge_bwd(dx1_b, w['w_out'], gates, pa, pb, pm,
                                                               "project_merge_bwd")
    do_a = _mm_rows([(dpa, w['w_o_a'], 0)], "mm_d_oa", nt=True)
    do_b = _mm_rows([(dpb, w['w_o_b'], 0)], "mm_d_ob", nt=True)
    do_m = _mm_rows([(dpm, w['w_o_m'], 0)], "mm_d_om", nt=True)
    grads['w_o_a'] = _mm_tn(o_a, dpa, "mm_dw_oa")
    grads['w_o_b'] = _mm_tn(o_b, dpb, "mm_dw_ob")
    grads['w_o_m'] = _mm_tn(o_m, dpm, "mm_dw_om")
    grads['w_gate'] = _mm_cols(h_t, dgpre, "mm_dw_gate")
    dq_m, dmk, dmv, grads['m_q_norm'] = _mem_attn_bwd(proj, 6, mk, mv, w['m_q_norm'], do_m, "mem_attn_bwd")
    grads['w_mem_kv'], grads['mem_norm'], grads['m_k_norm'] = _mem_kv_bwd(
        mem, w['mem_norm'], w['w_mem_kv'], w['m_k_norm'], dmk, dmv, "mem_kv_bwd")
    a_gain = w['a_q_norm']
    if hooks is not None:
        a_gain = a_gain + hooks.grads('mix', grads)[0:1, 0:1]

    prep = _bwd_prep(do_a, o_a, lse_a, dils, None, "bwd_prep_a")
    dproj, dgq_a, dgk_a = [], [], []
    for gi, (window, d) in enumerate(A_GROUPS):
        gq, gk = _two(a_gain[gi]), _two(w['a_k_norm'][gi])
        dqkv = _band_bwd(qkv_a[gi], prep[3 * gi], prep[3 * gi + 1], prep[3 * gi + 2], 2 * d, window // d,
                         f"band_bwd_a{gi}")
        dp, dgq, dgk = _qk_prep_bwd(dqkv, proj, 6 * gi, d, False, gq, gk, tabs, f"qk_prep_bwd_a{gi}")
        dproj.append(dp)
        dgq_a.append(_fold_heads(dgq))
        dgk_a.append(_fold_heads(dgk))
    grads['a_q_norm'] = jnp.stack(dgq_a)
    grads['a_k_norm'] = jnp.stack(dgk_a)

    do_bu, lse_bu, delta_bu, dsink = _bwd_prep(do_b, o_b, lse_b, (1,), sinks, "bwd_prep_b")
    dqkv = _band_bwd(qkv_b, do_bu, lse_bu, delta_bu, 4, B_WINDOW - 1, "band_bwd_b")
    dp_b, dgq, dgk = _qk_prep_bwd(dqkv, proj, 18, 1, True, gbq, gbk, tabs, "qk_prep_bwd_b")
    dproj.append(dp_b)
    grads['b_q_norm'] = _fold_heads(dgq)
    grads['b_k_norm'] = _fold_heads(dgk)
    grads['b_sinks'] = jnp.stack([dsink[:, 0, 0], dsink[:, 0, HEAD_DIM]], axis=1).reshape(8)

    dproj.append(dq_m)

    cols = (0, 1, 2, 3, 6)
    grads['w_in'] = _mm_rows_cat(h_t, dproj, "mm_dw_in")
    attn_gain = w['attn_norm']
    if hooks is not None:
        attn_gain = attn_gain + hooks.grads('in', grads)[0:1, 0:1]
    grad_x, grads['attn_norm'] = _mm_rows(
        [(dp, w['w_in'], c) for dp, c in zip(dproj, cols)] + [(dgpre, w['w_gate'], 0)], "mm_d_h", nt=True,
        rms_bwd=(x, r1, attn_gain, dx1))
    return loss, grad_x, grads


def _coords():
    return lax.axis_index("x"), lax.axis_index("y"), lax.axis_index("c")


def _slot(p):
    return 4 * p[0] + 2 * p[1] + p[2]


ALL_PEERS = tuple(range(1, N_DEV))
CHIP_PEERS = (1, 4, 2, 6)
OTHER_CHIPS = (4, 2, 6)


def _peers(me, masks=ALL_PEERS):
    x, y, c = me
    return [(1 - x if mask & 4 else x, 1 - y if mask & 2 else y, 1 - c if mask & 1 else c) for mask in masks]


HBM_SPEC = pl.BlockSpec(memory_space=pltpu.HBM)


SEM_SPEC = pl.BlockSpec(memory_space=pltpu.SEMAPHORE)
SIDE_EFFECT = pltpu.SideEffectType.DATAFLOW_SIDE_EFFECTING


def _exchange_start(blocks, name, gather=False, masks=ALL_PEERS, after=None):
    n = len(blocks)
    n_peers = len(masks)
    n_in = 2 * n + (0 if after is None else 1)

    def body(*refs):
        ins, lands = refs[:n], refs[n:2 * n]
        send_sems, recv_sems = refs[n_in], refs[n_in + 1]
        token = refs[-1]
        me = _coords()
        peers = _peers(me, masks)
        for a in range(n):
            for k in range(n_peers):
                pltpu.make_async_remote_copy(
                    src_ref=ins[a] if gather else ins[a].at[_slot(peers[k])], dst_ref=lands[a].at[_slot(me)],
                    send_sem=send_sems.at[a * n_peers + k], recv_sem=recv_sems.at[a * n_peers + k],
                    device_id=peers[k], device_id_type=MESH).start()
        token[...] = jnp.zeros_like(token)

    land_shapes = [((N_DEV,) + b.shape) if gather else b.shape for b in blocks]
    hbm_in = [pltpu.HBM(b.shape, b.dtype) for b in blocks]
    hbm_land = [pltpu.HBM(s, b.dtype) for s, b in zip(land_shapes, blocks)]
    sems = pltpu.SemaphoreType.DMA((n * n_peers,))
    ins = [pltpu.with_memory_space_constraint(b, pltpu.HBM) for b in blocks]
    lands = [pltpu.with_memory_space_constraint(lax.empty(s, b.dtype), pltpu.HBM) for s, b in zip(land_shapes, blocks)]
    return pl.pallas_call(
        body, name=name, out_shape=(sems, sems, *hbm_in, *hbm_land, jax.ShapeDtypeStruct((8, LANES), F32)),
        in_specs=[HBM_SPEC] * (2 * n) + ([] if after is None else [pl.BlockSpec(memory_space=pl.ANY)]),
        out_specs=(SEM_SPEC, SEM_SPEC, *([HBM_SPEC] * (2 * n)), pl.BlockSpec(memory_space=pltpu.VMEM)),
        input_output_aliases={i: 2 + i for i in range(2 * n)},
        compiler_params=pltpu.CompilerParams(has_side_effects=SIDE_EFFECT),
    )(*ins, *lands, *([] if after is None else [after]))


def _exchange_wait(started, after, name, gather=False, masks=ALL_PEERS):
    n = (len(started) - 3) // 2
    n_peers = len(masks)
    send_sems, recv_sems = started[0], started[1]
    thru = started[2:2 + 2 * n]

    def body(*refs):
        ins, lands = refs[:n], refs[n:2 * n]
        send_ref, recv_ref = refs[2 * n], refs[2 * n + 1]
        me = _coords()
        peers = _peers(me, masks)
        for a in range(n):
            for k in range(n_peers):
                cp = pltpu.make_async_remote_copy(
                    src_ref=ins[a] if gather else ins[a].at[_slot(peers[k])], dst_ref=lands[a].at[_slot(peers[k])],
                    send_sem=send_ref.at[a * n_peers + k], recv_sem=recv_ref.at[a * n_peers + k],
                    device_id=peers[k], device_id_type=MESH)
                cp.wait_send()
                cp.wait_recv()

    hbm = [pltpu.HBM(t.shape, t.dtype) for t in thru]
    res = pl.pallas_call(
        body, name=name, out_shape=tuple(hbm),
        in_specs=[HBM_SPEC] * (2 * n) + [SEM_SPEC, SEM_SPEC] + [pl.BlockSpec(memory_space=pl.ANY)] * len(after),
        out_specs=tuple([HBM_SPEC] * (2 * n)), input_output_aliases={i: i for i in range(2 * n)},
        compiler_params=pltpu.CompilerParams(has_side_effects=SIDE_EFFECT),
    )(*thru, send_sems, recv_sems, *after)
    return res[n:]


def _sibling_forward(arrays, name):
    n = len(arrays)
    n_fwd = len(OTHER_CHIPS)

    def body(*refs):
        bufs = refs[n:2 * n]
        token, send_sems, recv_sems = refs[2 * n:]
        token[...] = jnp.zeros_like(token)
        x, y, c = _coords()
        sibling = (x, y, 1 - c)
        mine = _peers((x, y, c), OTHER_CHIPS)
        theirs = _peers(sibling, OTHER_CHIPS)

        def copy(a, k, block):
            rows = bufs[a].at[_slot(block)]
            return pltpu.make_async_remote_copy(
                src_ref=rows, dst_ref=rows, send_sem=send_sems.at[a * n_fwd + k], recv_sem=recv_sems.at[a * n_fwd + k],
                device_id=sibling, device_id_type=MESH)

        sends = [copy(a, k, mine[k]) for a in range(n) for k in range(n_fwd)]
        for cp in sends:
            cp.start()
        for a in range(n):
            for k in range(n_fwd):
                copy(a, k, theirs[k]).wait_recv()
        for cp in sends:
            cp.wait_send()

    res = pl.pallas_call(
        body, name=name, in_specs=[HBM_SPEC] * n,
        out_specs=tuple([HBM_SPEC] * n + [pl.BlockSpec(memory_space=pltpu.VMEM)]),
        out_shape=tuple([jax.ShapeDtypeStruct(a.shape, a.dtype) for a in arrays] + [jax.ShapeDtypeStruct((8, LANES), F32)]),
        input_output_aliases={i: i for i in range(n)},
        scratch_shapes=[pltpu.SemaphoreType.DMA((n * n_fwd,)), pltpu.SemaphoreType.DMA((n * n_fwd,))],
    )(*arrays)
    return res[:n], res[n]


def _all_sum(p, name):
    def body(p_ref, o_ref, recv, send_sems, recv_sems):
        me = _coords()
        peers = _peers(me)
        recv[_slot(me)] = p_ref[...]

        def copy(k, landing):
            return pltpu.make_async_remote_copy(
                src_ref=p_ref, dst_ref=recv.at[_slot(landing)], send_sem=send_sems.at[k], recv_sem=recv_sems.at[k],
                device_id=peers[k], device_id_type=MESH)

        sends = [copy(k, me) for k in range(N_DEV - 1)]
        for cp in sends:
            cp.start()
        for k in range(N_DEV - 1):
            copy(k, peers[k]).wait_recv()
        for cp in sends:
            cp.wait_send()
        acc = recv[0]
        for s in range(1, N_DEV):
            acc = acc + recv[s]
        o_ref[...] = acc

    vmem = pl.BlockSpec(memory_space=pltpu.VMEM)
    return pl.pallas_call(
        body, name=name, in_specs=[vmem], out_specs=vmem, out_shape=jax.ShapeDtypeStruct(p.shape, F32),
        scratch_shapes=[pltpu.VMEM((N_DEV,) + p.shape, F32), pltpu.SemaphoreType.DMA((N_DEV - 1,)),
                        pltpu.SemaphoreType.DMA((N_DEV - 1,))],
    )(p)


def _adam(w, g, m, v):
    m2 = ADAM_B1 * m + (1.0 - ADAM_B1) * g
    v2 = ADAM_B2 * v + (1.0 - ADAM_B2) * (g * g)
    m_hat = m2 / (1.0 - ADAM_B1 ** ADAM_STEP)
    v_hat = v2 / (1.0 - ADAM_B2 ** ADAM_STEP)
    delta = -ADAM_LR * (m_hat / (jnp.sqrt(v_hat) + ADAM_EPS) + ADAM_WD * w)
    return delta, m2, v2


def _row_tile(rows, cols):
    best = rows
    for t in range(16, rows, 16):
        if rows % t == 0 and t * cols * 4 <= (1 << 20):
            best = t
    return best


def _adam_reduce(parts, w, m, v, name):
    rows, cols = w.shape
    tr = _row_tile(rows, cols)

    def body(p_ref, w_ref, m_ref, v_ref, g_ref, d_ref, m2_ref, v2_ref):
        g = p_ref[0].astype(F32)
        for s in range(1, N_DEV):
            g = g + p_ref[s].astype(F32)
        g_ref[...] = g
        d_ref[...], m2_ref[...], v2_ref[...] = _adam(w_ref[...], g, m_ref[...], v_ref[...])

    blk = pl.BlockSpec((tr, cols), lambda i: (i, 0))
    shp = jax.ShapeDtypeStruct((rows, cols), F32)
    return pl.pallas_call(
        body, name=name, grid=(rows // tr,),
        in_specs=[pl.BlockSpec((N_DEV, tr, cols), lambda i: (0, i, 0)), blk, blk, blk],
        out_specs=(blk,) * 4, out_shape=(shp,) * 4, compiler_params=_cparams(1),
    )(parts, w, m, v)


PACK_COLS = 1024
PACK = {'attn_norm': (0, 1, 1024), 'mem_norm': (1, 1, 1024), 'ffn_norm': (2, 1, 1024), 'b_gate': (3, 3, 1024),
        'conv_b': (6, 6, 1024), 'a_q_norm': (12, 3, 64), 'a_k_norm': (15, 3, 64), 'b_q_norm': (18, 1, 64),
        'b_k_norm': (19, 1, 64), 'm_q_norm': (20, 1, 128), 'm_k_norm': (21, 1, 128), 'b_sinks': (22, 1, 8)}
PACK_LOSS_ROW = 23
PACK_ROWS = 24


def _pack_pieces(name, width):
    r0, nr, lanes = PACK[name]
    out = []
    for j in range(nr):
        if lanes == PACK_COLS:
            w = min(PACK_COLS, width - j * PACK_COLS)
            out.append((r0 + j, slice(0, 1), slice(j * PACK_COLS, j * PACK_COLS + w), w))
        else:
            out.append((r0 + j, slice(j, j + 1), slice(0, lanes), lanes))
    return out


def _pack_small(grads, loss_tile, name):
    names = list(PACK)

    def body(*refs):
        o_ref = refs[-1]
        o_ref[...] = jnp.zeros_like(o_ref)
        for k, nm in enumerate(names):
            for row, rs, ls, w in _pack_pieces(nm, refs[k].shape[1]):
                o_ref[row:row + 1, 0:w] = refs[k][rs, ls]
        o_ref[PACK_LOSS_ROW:PACK_LOSS_ROW + 1, 0:1] = refs[len(names)][0:1, 0:1]

    vmem = pl.BlockSpec(memory_space=pltpu.VMEM)
    args = [grads[nm] for nm in names] + [loss_tile]
    return pl.pallas_call(body, name=name, in_specs=[vmem] * len(args), out_specs=vmem,
                          out_shape=jax.ShapeDtypeStruct((PACK_ROWS, PACK_COLS), F32))(*args)


def _adam_small(gsum, ws, ms, vs, name):
    names = list(PACK)
    n = len(names)

    def body(*refs):
        g_ref = refs[0]
        w_refs, m_refs, v_refs = refs[1:1 + n], refs[1 + n:1 + 2 * n], refs[1 + 2 * n:1 + 3 * n]
        outs = refs[1 + 3 * n:]
        outs[0][...] = g_ref[PACK_LOSS_ROW:PACK_LOSS_ROW + 1, 0:1]
        for k, nm in enumerate(names):
            o_g, o_d, o_m, o_v = outs[1 + 4 * k:5 + 4 * k]
            for row, rs, ls, width in _pack_pieces(nm, w_refs[k].shape[1]):
                src = (rs, ls)
                g = g_ref[row:row + 1, 0:width]
                d, m2, v2 = _adam(w_refs[k][src], g, m_refs[k][src], v_refs[k][src])
                o_g[src] = g
                o_d[src] = d
                o_m[src] = m2
                o_v[src] = v2

    vmem = pl.BlockSpec(memory_space=pltpu.VMEM)
    shapes = [jax.ShapeDtypeStruct((1, 1), F32)]
    for nm in names:
        shapes += [jax.ShapeDtypeStruct(ws[nm].shape, F32)] * 4
    args = [gsum] + [ws[nm] for nm in names] + [ms[nm] for nm in names] + [vs[nm] for nm in names]
    return pl.pallas_call(
        body, name=name, in_specs=[vmem] * len(args), out_specs=tuple([vmem] * len(shapes)), out_shape=tuple(shapes),
    )(*args)


def _as2d(name, a):
    return a.reshape(a.shape[-2], a.shape[-1]) if a.ndim == 3 else a


def kernel(x, mem, positions, attn_norm, w_in, a_q_norm, a_k_norm, b_q_norm, b_k_norm, b_sinks, mem_norm, w_mem_kv, m_q_norm, m_k_norm, w_o_a, w_o_b, w_o_m, w_gate, b_gate, w_out, ffn_norm, w_up, conv_w, conv_b, w_down, loss_target, m_attn_norm, m_w_in, m_a_q_norm, m_a_k_norm, m_b_q_norm, m_b_k_norm, m_b_sinks, m_mem_norm, m_w_mem_kv, m_m_q_norm, m_m_k_norm, m_w_o_a, m_w_o_b, m_w_o_m, m_w_gate, m_b_gate, m_w_out, m_ffn_norm, m_w_up, m_conv_w, m_conv_b, m_w_down, v_attn_norm, v_w_in, v_a_q_norm, v_a_k_norm, v_b_q_norm, v_b_k_norm, v_b_sinks, v_mem_norm, v_w_mem_kv, v_m_q_norm, v_m_k_norm, v_w_o_a, v_w_o_b, v_w_o_m, v_w_gate, v_b_gate, v_w_out, v_ffn_norm, v_w_up, v_conv_w, v_conv_b, v_w_down):
    given = dict(attn_norm=attn_norm, w_in=w_in, a_q_norm=a_q_norm, a_k_norm=a_k_norm, b_q_norm=b_q_norm, b_k_norm=b_k_norm, b_sinks=b_sinks, mem_norm=mem_norm, w_mem_kv=w_mem_kv, m_q_norm=m_q_norm, m_k_norm=m_k_norm, w_o_a=w_o_a, w_o_b=w_o_b, w_o_m=w_o_m, w_gate=w_gate, b_gate=b_gate, w_out=w_out, ffn_norm=ffn_norm, w_up=w_up, conv_w=conv_w, conv_b=conv_b, w_down=w_down)
    mom1 = dict(attn_norm=m_attn_norm, w_in=m_w_in, a_q_norm=m_a_q_norm, a_k_norm=m_a_k_norm, b_q_norm=m_b_q_norm, b_k_norm=m_b_k_norm, b_sinks=m_b_sinks, mem_norm=m_mem_norm, w_mem_kv=m_w_mem_kv, m_q_norm=m_m_q_norm, m_k_norm=m_m_k_norm, w_o_a=m_w_o_a, w_o_b=m_w_o_b, w_o_m=m_w_o_m, w_gate=m_w_gate, b_gate=m_b_gate, w_out=m_w_out, ffn_norm=m_ffn_norm, w_up=m_w_up, conv_w=m_conv_w, conv_b=m_conv_b, w_down=m_w_down)
    mom2 = dict(attn_norm=v_attn_norm, w_in=v_w_in, a_q_norm=v_a_q_norm, a_k_norm=v_a_k_norm, b_q_norm=v_b_q_norm, b_k_norm=v_b_k_norm, b_sinks=v_b_sinks, mem_norm=v_mem_norm, w_mem_kv=v_w_mem_kv, m_q_norm=v_m_q_norm, m_k_norm=v_m_k_norm, w_o_a=v_w_o_a, w_o_b=v_w_o_b, w_o_m=v_w_o_m, w_gate=v_w_gate, b_gate=v_b_gate, w_out=v_w_out, ffn_norm=v_ffn_norm, w_up=v_w_up, conv_w=v_conv_w, conv_b=v_conv_b, w_down=v_w_down)

    big = list(BIG)
    stages = {'mix': list(MIX_WEIGHTS), 'ffn': list(FFN_WEIGHTS), 'in': ['w_in']}
    my_slot = _slot(_coords())

    def shard(n):
        return given[n][0] if n == 'conv_w' else given[n][0].astype(BF16)

    def whole(n, g):
        _, r, c = g.shape
        return g.reshape(N_DEV * r, c) if BIG[n] == 0 else g.transpose(1, 0, 2).reshape(r, N_DEV * c)

    def to_blocks(n, g):
        r, c = given[n].shape[1:]
        g = g.reshape(N_DEV, r, c) if BIG[n] == 0 else g.reshape(r, N_DEV, c).transpose(1, 0, 2)
        return g if n == 'conv_w' else g.astype(BF16)

    class Hooks:
        next_stage = {'in': 'mix', 'mix': 'ffn'}

        def __init__(self):
            self.coming, self.sent = {}, {}
            self.shards = {n: shard(n) for n in big}
            self.start_gather('in', None)

        def start_gather(self, stage, after):
            src = [self.shards[n] for n in stages[stage]]
            self.coming[stage] = _exchange_start(src, f"gather_{stage}_start", gather=True, masks=CHIP_PEERS,
                                                 after=after)

        def weights(self, stage, after):
            names = stages[stage]
            after = list(after)
            if stage == 'in':
                after += [self.shards[n] for n in stages['mix'] + stages['ffn']]
            landed = _exchange_wait(self.coming[stage], after, f"gather_{stage}_wait", gather=True, masks=CHIP_PEERS)
            landed, token = _sibling_forward(landed, f"gather_{stage}_forward")
            if stage in self.next_stage:
                self.start_gather(self.next_stage[stage], token)
            return {n: whole(n, lax.dynamic_update_slice_in_dim(land, self.shards[n][None], my_slot, axis=0))
                    for n, land in zip(names, landed)}

        def grads(self, stage, g):
            blocks = [to_blocks(n, g[n]) for n in stages[stage]]
            own = [lax.dynamic_slice_in_dim(b, my_slot, 1, axis=0) for b in blocks]
            self.sent[stage] = (_exchange_start(blocks, f"exchange_{stage}_start"), own)
            return self.sent[stage][0][-1]

        def parts(self, stage, after):
            started, own = self.sent[stage]
            landed = _exchange_wait(started, [after], f"exchange_{stage}_wait")
            return {n: lax.dynamic_update_slice_in_dim(land, o, my_slot, axis=0)
                    for n, land, o in zip(stages[stage], landed, own)}

    hooks = Hooks()
    w = {}
    for n in SMALL:
        w[n] = given[n]
    w['a_q_norm'], w['a_k_norm'] = given['a_q_norm'][0], given['a_k_norm'][0]
    w['b_q_norm'], w['b_k_norm'], w['b_sinks'] = given['b_q_norm'][0], given['b_k_norm'][0], given['b_sinks'][0]

    loss_tile, grad_x, grads = _device_step(x[0], mem[0], positions[0], loss_target[0], w, hooks)
    out = {}
    after = grad_x
    for stage in ('ffn', 'mix', 'in'):
        for n, p in hooks.parts(stage, after).items():
            res = _adam_reduce(p, given[n][0], mom1[n][0], mom2[n][0], f"adam_{n}")
            out[n] = tuple(t[None] for t in res)
            after = res[0]

    small = {n: grads[n] for n in PACK}
    small['b_q_norm'], small['b_k_norm'] = grads['b_q_norm'].reshape(1, -1), grads['b_k_norm'].reshape(1, -1)
    small['b_sinks'] = grads['b_sinks'].reshape(1, -1)
    gsum = _all_sum(_pack_small(small, loss_tile, "pack_small"), "sum_small")
    ws = {n: _as2d(n, given[n]) for n in PACK}
    ms = {n: _as2d(n, mom1[n]) for n in PACK}
    vs = {n: _as2d(n, mom2[n]) for n in PACK}
    res = _adam_small(gsum, ws, ms, vs, "adam_small")
    loss = res[0].reshape(())
    for k, n in enumerate(PACK):
        out[n] = tuple(t.reshape(given[n].shape) for t in res[1 + 4 * k:5 + 4 * k])

    outs = [loss, grad_x[None]]
    for field in range(4):
        outs += [out[n][field] for n in WEIGHTS]
    return tuple(outs)
```

```python
import functools
import math

import jax
import jax.numpy as jnp
from jax import lax
from jax.experimental import pallas as pl
from jax.experimental.pallas import tpu as pltpu

F32 = jnp.float32
BF16 = jnp.bfloat16

N_DEV = 8
D_MODEL = 1024
HEAD_DIM = 64
A_GROUPS = ((128, 1), (512, 4), (2048, 16))
B_WINDOW = 128
M_HEADS = 4
M_HEAD_DIM = 128
MEM_LEN = 256
D_FF = 2816
ROPE_THETA = 500000.0
ROPE_DIMS = 16
BLOCK = 128
EPS = 1e-6
LANES = 128
BAND_Q_BLOCKS = 4
BAND_UNITS = 2

ADAM_LR = 0.001
ADAM_B1 = 0.9
ADAM_B2 = 0.999
ADAM_EPS = 1e-08
ADAM_WD = 0.01
ADAM_STEP = 10

VMEM_LIMIT_BYTES = 56 * 1024 * 1024
MESH = pl.DeviceIdType.MESH

WEIGHTS = ['attn_norm', 'w_in', 'a_q_norm', 'a_k_norm', 'b_q_norm', 'b_k_norm', 'b_sinks', 'mem_norm',
           'w_mem_kv', 'm_q_norm', 'm_k_norm', 'w_o_a', 'w_o_b', 'w_o_m', 'w_gate', 'b_gate', 'w_out',
           'ffn_norm', 'w_up', 'conv_w', 'conv_b', 'w_down']
BIG = {'w_in': 1, 'w_mem_kv': 0, 'w_o_a': 1, 'w_o_b': 1, 'w_o_m': 1, 'w_gate': 1, 'w_out': 0, 'w_up': 1,
       'conv_w': 1, 'w_down': 0}
SMALL = [n for n in WEIGHTS if n not in BIG]


def _cparams(n_grid):
    return pltpu.CompilerParams(dimension_semantics=("arbitrary",) * n_grid, vmem_limit_bytes=VMEM_LIMIT_BYTES)


def _seg_matrix(width):
    shift = width.bit_length() - 1
    r = lax.shift_right_logical(lax.broadcasted_iota(jnp.int32, (LANES, LANES), 0), shift)
    c = lax.shift_right_logical(lax.broadcasted_iota(jnp.int32, (LANES, LANES), 1), shift)
    return jnp.where(r == c, 1.0, 0.0).astype(BF16)


def _seg_sum(x, seg):
    hi = x.astype(BF16)
    r1 = x - hi.astype(F32)
    mid = r1.astype(BF16)
    lo = (r1 - mid.astype(F32)).astype(BF16)
    dot = functools.partial(jnp.dot, preferred_element_type=F32)
    return dot(hi, seg) + dot(mid, seg) + dot(lo, seg)


def _rope(y, c, s1, s2):
    return y * c + pltpu.roll(y, LANES - ROPE_DIMS // 2, 1) * s1 + pltpu.roll(y, ROPE_DIMS // 2, 1) * s2


def _unrope(dy, c, s1, s2):
    return dy * c + pltpu.roll(dy * s1, ROPE_DIMS // 2, 1) + pltpu.roll(dy * s2, LANES - ROPE_DIMS // 2, 1)


def _sigmoid(x):
    return 1.0 / (1.0 + jnp.exp(-x))


def _rms_fwd(x, gain, name):
    s_len, d = x.shape
    tm = 512

    def body(x_ref, g_ref, h_ref, ht_ref, r_ref):
        xv = x_ref[...]
        r = lax.rsqrt(jnp.mean(xv * xv, axis=-1, keepdims=True) + EPS)
        h = ((xv * r) * g_ref[...]).astype(BF16)
        h_ref[...] = h
        ht_ref[...] = h.T
        r_ref[...] = r

    return pl.pallas_call(
        body, name=name, grid=(s_len // tm,),
        in_specs=[pl.BlockSpec((tm, d), lambda i: (i, 0)), pl.BlockSpec((1, d), lambda i: (0, 0))],
        out_specs=(pl.BlockSpec((tm, d), lambda i: (i, 0)), pl.BlockSpec((d, tm), lambda i: (0, i)),
                   pl.BlockSpec((tm, 1), lambda i: (i, 0))),
        out_shape=(jax.ShapeDtypeStruct((s_len, d), BF16), jax.ShapeDtypeStruct((d, s_len), BF16),
                   jax.ShapeDtypeStruct((s_len, 1), F32)),
        compiler_params=_cparams(1),
    )(x, gain)


def _resident(shape, index_map):
    return pl.BlockSpec(shape, index_map, pipeline_mode=pl.Buffered(1))


def _mm_rows(pairs, name, nt=False, tm=512, bias=None, sigmoid=False, res=None, out_dtypes=(F32,), loss_target=None,
             rms_bwd=None):
    m = pairs[0][0].shape[0]
    n = pairs[0][1].shape[0] if nt else pairs[0][1].shape[1]
    n_pairs = len(pairs)
    has_bias, has_res, has_loss = bias is not None, res is not None, loss_target is not None
    has_rms = rms_bwd is not None
    dims = (((1,), (1,)), ((), ())) if nt else (((1,), (0,)), ((), ()))

    def body(*refs):
        acc = None
        for p in range(n_pairs):
            t = lax.dot_general(refs[2 * p][...].astype(BF16), refs[2 * p + 1][...], dims, preferred_element_type=F32)
            acc = t if acc is None else acc + t
        pos = 2 * n_pairs
        if has_bias:
            acc = acc + refs[pos][...]
            pos += 1
        if sigmoid:
            acc = _sigmoid(acc)
        if has_res:
            acc = refs[pos][...] + acc
            pos += 1
        if has_loss:
            dy_ref, dyb_ref, l_ref = refs[pos + 1:]

            @pl.when(pl.program_id(0) == 0)
            def _():
                l_ref[...] = jnp.zeros_like(l_ref)
            err = acc - refs[pos][...]
            dy = err * (1.0 / n)
            dy_ref[...] = dy
            dyb_ref[...] = dy.astype(BF16)
            part = 0.5 * jnp.sum(jnp.mean(err * err, axis=-1, keepdims=True), axis=0, keepdims=True)
            l_ref[...] += jnp.broadcast_to(part, l_ref.shape)
            return
        if has_rms:
            x_ref, r_ref, g_ref, add_ref = refs[pos:pos + 4]
            dg_ref = refs[-1]

            @pl.when(pl.program_id(0) == 0)
            def _():
                dg_ref[...] = jnp.zeros_like(dg_ref)
            rv = r_ref[...]
            xhat = x_ref[...] * rv
            dg_ref[...] += jnp.sum(acc * xhat, axis=0, keepdims=True)
            dxhat = acc * g_ref[...]
            acc = add_ref[...] + rv * (dxhat - xhat * jnp.mean(dxhat * xhat, axis=-1, keepdims=True))
            for o_ref in refs[pos + 4:-1]:
                o_ref[...] = acc.astype(o_ref.dtype)
            return
        for o_ref in refs[pos:]:
            o_ref[...] = acc.astype(o_ref.dtype)

    in_specs, args = [], []
    for a, w, blk in pairs:
        k = a.shape[1]
        in_specs.append(pl.BlockSpec((tm, k), lambda i: (i, 0)))
        if nt:
            in_specs.append(_resident((n, k), lambda i, blk=blk: (0, blk)))
        else:
            in_specs.append(_resident((k, n), lambda i, blk=blk: (blk, 0)))
        args += [a, w]
    if has_bias:
        in_specs.append(_resident((1, n), lambda i: (0, 0)))
        args.append(bias)
    if has_res:
        in_specs.append(pl.BlockSpec((tm, n), lambda i: (i, 0)))
        args.append(res)
    out = pl.BlockSpec((tm, n), lambda i: (i, 0))
    if has_loss:
        return pl.pallas_call(
            body, name=name, grid=(m // tm,), in_specs=in_specs + [out],
            out_specs=(out, out, pl.BlockSpec((8, LANES), lambda i: (0, 0))),
            out_shape=(jax.ShapeDtypeStruct((m, n), F32), jax.ShapeDtypeStruct((m, n), BF16),
                       jax.ShapeDtypeStruct((8, LANES), F32)),
            compiler_params=_cparams(1),
        )(*args, loss_target)
    if has_rms:
        x, r, gain, add = rms_bwd
        vec = _resident((1, n), lambda i: (0, 0))
        return pl.pallas_call(
            body, name=name, grid=(m // tm,),
            in_specs=in_specs + [out, pl.BlockSpec((tm, 1), lambda i: (i, 0)), vec, out],
            out_specs=tuple([out] * len(out_dtypes) + [pl.BlockSpec((1, n), lambda i: (0, 0))]),
            out_shape=tuple([jax.ShapeDtypeStruct((m, n), dt) for dt in out_dtypes] + [jax.ShapeDtypeStruct((1, n), F32)]),
            compiler_params=_cparams(1),
        )(*args, x, r, gain, add)
    outs = pl.pallas_call(
        body, name=name, grid=(m // tm,), in_specs=in_specs, out_specs=tuple([out] * len(out_dtypes)),
        out_shape=tuple(jax.ShapeDtypeStruct((m, n), dt) for dt in out_dtypes), compiler_params=_cparams(1),
    )(*args)
    return outs[0] if len(out_dtypes) == 1 else outs


def _mm_rows_each(pairs, name, tm=512):
    m = pairs[0][0].shape[0]
    n_pairs = len(pairs)

    def body(*refs):
        for p in range(n_pairs):
            refs[2 * n_pairs + p][...] = lax.dot_general(refs[2 * p][...].astype(BF16), refs[2 * p + 1][...],
                                                         (((1,), (1,)), ((), ())), preferred_element_type=F32)

    in_specs, args = [], []
    for a, w in pairs:
        in_specs += [pl.BlockSpec((tm, a.shape[1]), lambda i: (i, 0)), _resident(w.shape, lambda i: (0, 0))]
        args += [a, w]
    return pl.pallas_call(
        body, name=name, grid=(m // tm,), in_specs=in_specs,
        out_specs=tuple(pl.BlockSpec((tm, w.shape[0]), lambda i: (i, 0)) for _, w in pairs),
        out_shape=tuple(jax.ShapeDtypeStruct((m, w.shape[0]), F32) for _, w in pairs), compiler_params=_cparams(1),
    )(*args)


def _mm_rows_cat(a, ws, name, tm=256):
    m, k = a.shape
    widths = [w.shape[1] for w in ws]
    n = sum(widths)

    def body(*refs):
        a_ref, o_ref = refs[0], refs[-1]
        av = a_ref[...]
        off = 0
        for p, width in enumerate(widths):
            o_ref[:, off:off + width] = jnp.dot(av, refs[1 + p][...], preferred_element_type=F32)
            off += width

    return pl.pallas_call(
        body, name=name, grid=(m // tm,),
        in_specs=[pl.BlockSpec((tm, k), lambda i: (i, 0))] + [_resident((k, wd), lambda i: (0, 0)) for wd in widths],
        out_specs=pl.BlockSpec((tm, n), lambda i: (i, 0)),
        out_shape=jax.ShapeDtypeStruct((m, n), F32), compiler_params=_cparams(1),
    )(a, *ws)


def _mm_cols(a, b, name, tn=256):
    m, k = a.shape
    n = b.shape[1]

    def body(a_ref, b_ref, o_ref):
        o_ref[...] = jnp.dot(a_ref[...], b_ref[...].astype(BF16), preferred_element_type=F32)

    return pl.pallas_call(
        body, name=name, grid=(n // tn,),
        in_specs=[_resident((m, k), lambda j: (0, 0)), pl.BlockSpec((k, tn), lambda j: (0, j))],
        out_specs=pl.BlockSpec((m, tn), lambda j: (0, j)),
        out_shape=jax.ShapeDtypeStruct((m, n), F32), compiler_params=_cparams(1),
    )(a, b)


def _mm_tn(a, b, name, tile=256):
    k, m = a.shape
    n = b.shape[1]
    dims = (((0,), (0,)), ((), ()))

    def body(a_ref, b_ref, o_ref):
        o_ref[...] = lax.dot_general(a_ref[...].astype(BF16), b_ref[...].astype(BF16), dims, preferred_element_type=F32)

    if n <= m:
        t = min(tile, m)
        grid, a_spec, b_spec = (m // t,), pl.BlockSpec((k, t), lambda i: (0, i)), _resident((k, n), lambda i: (0, 0))
        o_spec = pl.BlockSpec((t, n), lambda i: (i, 0))
    else:
        t = min(tile, n)
        grid, a_spec, b_spec = (n // t,), _resident((k, m), lambda i: (0, 0)), pl.BlockSpec((k, t), lambda i: (0, i))
        o_spec = pl.BlockSpec((m, t), lambda i: (0, i))
    return pl.pallas_call(
        body, name=name, grid=grid, in_specs=[a_spec, b_spec], out_specs=o_spec,
        out_shape=jax.ShapeDtypeStruct((m, n), F32), compiler_params=_cparams(1),
    )(a, b)


def _norm_rope(t, gain, c, s1, s2, seg):
    rs = lax.rsqrt(_seg_sum(t * t, seg) * (1.0 / HEAD_DIM) + EPS)
    return _rope((t * rs) * gain, c, s1, s2)


def _dup_half(y, half):
    lane = lax.broadcasted_iota(jnp.int32, y.shape, 1)
    rolled = pltpu.roll(y, HEAD_DIM, 1)
    keep = (lane < HEAD_DIM) if half == 0 else (lane >= HEAD_DIM)
    return jnp.where(keep, y, rolled)


def _qk_prep(proj, cb0, d, gqa, gq, gk, tabs, name):
    s_len = proj.shape[0]
    tm = 512
    rows = tm // d
    n_units = 4 if gqa else 2 * d
    n_q = 4 if gqa else 2
    n_in = 6

    def body(*refs):
        in_refs = refs[:n_in]
        gq_ref, gk_ref, c_ref, s1_ref, s2_ref, o_ref = refs[n_in:]
        seg = _seg_matrix(HEAD_DIM)

        def rows_of(ref, r):
            return ref[...] if d == 1 else ref[pl.ds(r, rows, stride=d), :]

        def put(unit_col, y):
            o_ref[:, unit_col * LANES:(unit_col + 1) * LANES] = y.astype(BF16)

        for r in range(d):
            c, s1, s2 = rows_of(c_ref, r), rows_of(s1_ref, r), rows_of(s2_ref, r)
            for b in range(n_in):
                t = rows_of(in_refs[b], r)
                if b < n_q:
                    put((b * d + r) if not gqa else b, _norm_rope(t, gq_ref[...], c, s1, s2, seg))
                elif not gqa:
                    sec, pair = (1, b - 2) if b < 4 else (2, b - 4)
                    y = _norm_rope(t, gk_ref[...], c, s1, s2, seg) if sec == 1 else t
                    put(sec * n_units + pair * d + r, y)
                else:
                    sec = 1 if b == 4 else 2
                    y = _norm_rope(t, gk_ref[...], c, s1, s2, seg) if sec == 1 else t
                    for u in range(n_units):
                        put(sec * n_units + u, _dup_half(y, u // 2))

    in_specs = [pl.BlockSpec((tm, LANES), lambda i, b=b: (i, cb0 + b)) for b in range(n_in)]
    vec = pl.BlockSpec((1, LANES), lambda i: (0, 0))
    tab = pl.BlockSpec((tm, LANES), lambda i: (i, 0))
    width = 3 * n_units * LANES
    return pl.pallas_call(
        body, name=name, grid=(s_len // tm,), in_specs=in_specs + [vec, vec, tab, tab, tab],
        out_specs=pl.BlockSpec((rows, width), lambda i: (i, 0)),
        out_shape=jax.ShapeDtypeStruct((s_len // d, width), BF16), compiler_params=_cparams(1),
    )(*([proj] * n_in), gq, gk, *tabs)


def _qk_prep_bwd(dqkv, proj, cb0, d, gqa, gq, gk, tabs, name):
    s_len = proj.shape[0]
    tm = 512
    rows = tm // d
    n_units = 4 if gqa else 2 * d
    n_q = 4 if gqa else 2
    n_in = 6

    def body(*refs):
        d_refs = refs[0:3]
        in_refs = refs[3:3 + n_in]
        gq_ref, gk_ref, c_ref, s1_ref, s2_ref, o_ref, dgq_ref, dgk_ref, stage = refs[3 + n_in:]
        seg = _seg_matrix(HEAD_DIM)

        @pl.when(pl.program_id(0) == 0)
        def _():
            dgq_ref[...] = jnp.zeros_like(dgq_ref)
            dgk_ref[...] = jnp.zeros_like(dgk_ref)

        def rows_of(ref, r):
            return ref[...] if d == 1 else ref[pl.ds(r, rows, stride=d), :]

        def unit(col):
            sec, u = divmod(col, n_units)
            return d_refs[sec][:, u * LANES:(u + 1) * LANES]

        def norm_bwd(dyr, t, gain, c, s1, s2, dg_ref):
            rs = lax.rsqrt(_seg_sum(t * t, seg) * (1.0 / HEAD_DIM) + EPS)
            that = t * rs
            dy = _unrope(dyr, c, s1, s2)
            dg_ref[...] += jnp.sum(dy * that, axis=0, keepdims=True)
            dthat = dy * gain
            return rs * (dthat - that * (_seg_sum(dthat * that, seg) * (1.0 / HEAD_DIM)))

        def fold(sec):
            tot = []
            for u in range(n_units):
                v = unit(sec * n_units + u)
                tot.append(v + pltpu.roll(v, HEAD_DIM, 1))
            lane = lax.broadcasted_iota(jnp.int32, tot[0].shape, 1)
            return jnp.where(lane < HEAD_DIM, tot[0] + tot[1], tot[2] + tot[3])

        for b in range(n_in):
            for r in range(d):
                c, s1, s2 = rows_of(c_ref, r), rows_of(s1_ref, r), rows_of(s2_ref, r)
                t = rows_of(in_refs[b], r)
                if b < n_q:
                    g = unit((b * d + r) if not gqa else b)
                    out = norm_bwd(g, t, gq_ref[...], c, s1, s2, dgq_ref)
                elif not gqa:
                    sec, pair = (1, b - 2) if b < 4 else (2, b - 4)
                    g = unit(sec * n_units + pair * d + r)
                    out = norm_bwd(g, t, gk_ref[...], c, s1, s2, dgk_ref) if sec == 1 else g
                else:
                    sec = 1 if b == 4 else 2
                    g = fold(sec)
                    out = norm_bwd(g, t, gk_ref[...], c, s1, s2, dgk_ref) if sec == 1 else g
                if d == 1:
                    o_ref[:, b * LANES:(b + 1) * LANES] = out.astype(BF16)
                else:
                    stage[pl.ds(r, rows, stride=d), :] = out
            if d != 1:
                o_ref[:, b * LANES:(b + 1) * LANES] = stage[...].astype(BF16)

    in_specs = [pl.BlockSpec((rows, n_units * LANES), lambda i: (i, 0))] * 3
    in_specs += [pl.BlockSpec((tm, LANES), lambda i, b=b: (i, cb0 + b)) for b in range(n_in)]
    vec = pl.BlockSpec((1, LANES), lambda i: (0, 0))
    tab = pl.BlockSpec((tm, LANES), lambda i: (i, 0))
    return pl.pallas_call(
        body, name=name, grid=(s_len // tm,), in_specs=in_specs + [vec, vec, tab, tab, tab],
        out_specs=(pl.BlockSpec((tm, n_in * LANES), lambda i: (i, 0)), vec, vec),
        out_shape=(jax.ShapeDtypeStruct((s_len, n_in * LANES), BF16), jax.ShapeDtypeStruct((1, LANES), F32),
                   jax.ShapeDtypeStruct((1, LANES), F32)),
        scratch_shapes=[pltpu.VMEM((tm, LANES), F32)], compiler_params=_cparams(1),
    )(*dqkv, *([proj] * n_in), gq, gk, *tabs)


def _head_masks(shape):
    lane = lax.broadcasted_iota(jnp.int32, shape, 1)
    return lane < HEAD_DIM, lane >= HEAD_DIM


def _band_fwd(qkv, n_units, max_dist, sinks, name):
    n_rows = qkv.shape[0]
    nb = n_rows // BLOCK
    scale = HEAD_DIM ** -0.5
    has_sink = sinks is not None
    assert not has_sink or max_dist < BLOCK

    qn, un = min(nb, BAND_Q_BLOCKS), BAND_UNITS
    ug = n_units // un

    def body(*refs):
        q_ref, kp_ref, km_ref, vp_ref, vm_ref = refs[:5]
        o_ref, lse_ref = refs[-2:]
        i = pl.program_id(1)
        qi = lax.broadcasted_iota(jnp.int32, (BLOCK, 2 * BLOCK), 0)
        kj = lax.broadcasted_iota(jnp.int32, (BLOCK, 2 * BLOCK), 1)
        dist = qi + BLOCK - kj
        band = (dist >= 0) & (dist <= max_dist)
        band_first = band & ((i > 0) | (kj >= BLOCK))
        m0, m1 = _head_masks((BLOCK, LANES))
        zero = jnp.zeros((BLOCK, LANES), BF16)
        for ub in range(un):
            cs = slice(ub * LANES, (ub + 1) * LANES)
            for qb in range(qn):
                rs = slice(qb * BLOCK, (qb + 1) * BLOCK)
                q = q_ref[rs, cs]
                if qb == 0:
                    kk = jnp.concatenate([kp_ref[:, cs], km_ref[0:BLOCK, cs]], axis=0)
                    vv = jnp.concatenate([vp_ref[:, cs], vm_ref[0:BLOCK, cs]], axis=0)
                    valid = band_first
                else:
                    kk = km_ref[(qb - 1) * BLOCK:(qb + 1) * BLOCK, cs]
                    vv = vm_ref[(qb - 1) * BLOCK:(qb + 1) * BLOCK, cs]
                    valid = band
                outs, lses = [], []
                for e, hm in enumerate((m0, m1)):
                    qe = jnp.where(hm, q, zero)
                    s = lax.dot_general(qe, kk, (((1,), (1,)), ((), ())), preferred_element_type=F32) * scale
                    s = jnp.where(valid, s, -jnp.inf)
                    if has_sink:
                        s = jnp.where(kj == 0, refs[5][ub][:, e * HEAD_DIM:e * HEAD_DIM + 1], s)
                    mx = jnp.max(s, axis=-1, keepdims=True)
                    p = jnp.exp(s - mx)
                    den = jnp.sum(p, axis=-1, keepdims=True)
                    pn = p * (1.0 / den)
                    if has_sink:
                        pn = jnp.where(kj == 0, 0.0, pn)
                    pn = pn.astype(BF16)
                    outs.append(jnp.dot(pn, vv, preferred_element_type=F32))
                    lses.append(mx + jnp.log(den))
                o_ref[rs, cs] = jnp.where(m0, outs[0], outs[1])
                lse_ref[rs, cs] = jnp.where(m0, jnp.broadcast_to(lses[0], (BLOCK, LANES)),
                                            jnp.broadcast_to(lses[1], (BLOCK, LANES)))

    def main(sec):
        return pl.BlockSpec((qn * BLOCK, un * LANES), lambda u, i: (i, sec * ug + u))

    def prev(sec):
        return pl.BlockSpec((BLOCK, un * LANES), lambda u, i: (jnp.maximum(i * qn - 1, 0), sec * ug + u))

    in_specs = [main(0), prev(1), main(1), prev(2), main(2)]
    args = [qkv] * 5
    if has_sink:
        in_specs.append(pl.BlockSpec((un, 1, LANES), lambda u, i: (u, 0, 0)))
        args.append(sinks)
    return pl.pallas_call(
        body, name=name, grid=(ug, nb // qn), in_specs=in_specs, out_specs=(main(0), main(0)),
        out_shape=(jax.ShapeDtypeStruct((n_rows, n_units * LANES), F32),) * 2, compiler_params=_cparams(2),
    )(*args)


def _band_bwd(qkv, do, lse, delta, n_units, max_dist, name):
    n_rows = qkv.shape[0]
    nb = n_rows // BLOCK
    scale = HEAD_DIM ** -0.5

    qn, un = min(nb, BAND_Q_BLOCKS), BAND_UNITS
    ug = n_units // un
    steps = nb // qn
    nt_dims = (((1,), (1,)), ((), ()))
    tn_dims = (((0,), (0,)), ((), ()))

    def body(qm_ref, qx_ref, kp_ref, km_ref, vp_ref, vm_ref, dom_ref, dox_ref, lm_ref, lx_ref, dm_ref, dx_ref,
             dq_ref, dk_ref, dv_ref):
        i = pl.program_id(1)
        m0, m1 = _head_masks((BLOCK, LANES))
        zero = jnp.zeros((BLOCK, LANES), BF16)
        qi = lax.broadcasted_iota(jnp.int32, (BLOCK, 2 * BLOCK), 0)
        kj = lax.broadcasted_iota(jnp.int32, (BLOCK, 2 * BLOCK), 1)
        dist = qi + BLOCK - kj
        band = (dist >= 0) & (dist <= max_dist)
        band_first = band & ((i > 0) | (kj >= BLOCK))
        qr = lax.broadcasted_iota(jnp.int32, (BLOCK, BLOCK), 0)
        kc = lax.broadcasted_iota(jnp.int32, (BLOCK, BLOCK), 1)
        dist_x = qr + BLOCK - kc
        band_next = (dist_x >= 0) & (dist_x <= max_dist) & (i < steps - 1)

        def pair(q, dob, lse_b, del_b, kk, vv, valid):
            dqs, dk, dv = [], None, None
            for e, hm in enumerate((m0, m1)):
                col = slice(e * HEAD_DIM, e * HEAD_DIM + 1)
                qe = jnp.where(hm, q, zero)
                doe = jnp.where(hm, dob, zero)
                s = lax.dot_general(qe, kk, nt_dims, preferred_element_type=F32) * scale
                p = jnp.where(valid, jnp.exp(s - lse_b[:, col]), 0.0)
                dp = lax.dot_general(doe, vv, nt_dims, preferred_element_type=F32)
                ds = (p * (dp - del_b[:, col]) * scale).astype(BF16)
                dqs.append(jnp.dot(ds, kk, preferred_element_type=F32))
                dk_e = lax.dot_general(ds, qe, tn_dims, preferred_element_type=F32)
                dv_e = lax.dot_general(p.astype(BF16), doe, tn_dims, preferred_element_type=F32)
                dk = dk_e if dk is None else dk + dk_e
                dv = dv_e if dv is None else dv + dv_e
            return jnp.where(m0, dqs[0], dqs[1]), dk, dv

        for ub in range(un):
            cs = slice(ub * LANES, (ub + 1) * LANES)
            dk_acc, dv_acc = [None] * qn, [None] * qn

            def add(acc, kb, part):
                acc[kb] = part if acc[kb] is None else acc[kb] + part

            for qb in range(qn):
                rs = slice(qb * BLOCK, (qb + 1) * BLOCK)
                if qb == 0:
                    kk = jnp.concatenate([kp_ref[:, cs], km_ref[0:BLOCK, cs]], axis=0)
                    vv = jnp.concatenate([vp_ref[:, cs], vm_ref[0:BLOCK, cs]], axis=0)
                    valid = band_first
                else:
                    kk = km_ref[(qb - 1) * BLOCK:(qb + 1) * BLOCK, cs]
                    vv = vm_ref[(qb - 1) * BLOCK:(qb + 1) * BLOCK, cs]
                    valid = band
                dq, dk, dv = pair(qm_ref[rs, cs], dom_ref[rs, cs], lm_ref[rs, cs], dm_ref[rs, cs], kk, vv, valid)
                dq_ref[rs, cs] = dq
                if qb > 0:
                    add(dk_acc, qb - 1, dk[0:BLOCK])
                    add(dv_acc, qb - 1, dv[0:BLOCK])
                add(dk_acc, qb, dk[BLOCK:2 * BLOCK])
                add(dv_acc, qb, dv[BLOCK:2 * BLOCK])
            last = slice((qn - 1) * BLOCK, qn * BLOCK)
            _, dk, dv = pair(qx_ref[:, cs], dox_ref[:, cs], lx_ref[:, cs], dx_ref[:, cs], km_ref[last, cs], vm_ref[last, cs],
                             band_next)
            add(dk_acc, qn - 1, dk)
            add(dv_acc, qn - 1, dv)
            for kb in range(qn):
                dk_ref[kb * BLOCK:(kb + 1) * BLOCK, cs] = dk_acc[kb]
                dv_ref[kb * BLOCK:(kb + 1) * BLOCK, cs] = dv_acc[kb]

    def main(sec):
        return pl.BlockSpec((qn * BLOCK, un * LANES), lambda u, i: (i, sec * ug + u))

    def prev(sec):
        return pl.BlockSpec((BLOCK, un * LANES), lambda u, i: (jnp.maximum(i * qn - 1, 0), sec * ug + u))

    def nxt(sec):
        return pl.BlockSpec((BLOCK, un * LANES), lambda u, i: (jnp.minimum((i + 1) * qn, nb - 1), sec * ug + u))

    in_specs = [main(0), nxt(0), prev(1), main(1), prev(2), main(2),
                main(0), nxt(0), main(0), nxt(0), main(0), nxt(0)]
    args = [qkv] * 6 + [do, do, lse, lse, delta, delta]
    shp = jax.ShapeDtypeStruct((n_rows, n_units * LANES), F32)
    return pl.pallas_call(
        body, name=name, grid=(ug, steps), in_specs=in_specs, out_specs=(main(0), main(0), main(0)),
        out_shape=(shp, shp, shp), compiler_params=_cparams(2),
    )(*args)


def _merge_groups(os_, lses, dils, name):
    s_len = os_[0].shape[0] * dils[0]
    tm = 512

    def body(*refs):
        o_refs, l_refs = refs[0:3], refs[3:6]
        o_ref, lse_ref = refs[6:8]
        so, sl = refs[8:11], refs[11:14]
        for pair in range(2):
            for g, d in enumerate(dils):
                rows = tm // d
                for r in range(d):
                    col = slice((pair * d + r) * LANES, (pair * d + r + 1) * LANES)
                    if d == 1:
                        so[g][...] = o_refs[g][:, col]
                        sl[g][...] = l_refs[g][:, col]
                    else:
                        so[g][pl.ds(r, rows, stride=d), :] = o_refs[g][:, col]
                        sl[g][pl.ds(r, rows, stride=d), :] = l_refs[g][:, col]
            l0, l1, l2 = sl[0][...], sl[1][...], sl[2][...]
            mx = jnp.maximum(jnp.maximum(l0, l1), l2)
            e0, e1, e2 = jnp.exp(l0 - mx), jnp.exp(l1 - mx), jnp.exp(l2 - mx)
            den = e0 + e1 + e2
            inv = 1.0 / den
            o_ref[:, pair * LANES:(pair + 1) * LANES] = (so[0][...] * (e0 * inv) + so[1][...] * (e1 * inv)
                                                         + so[2][...] * (e2 * inv))
            lse_ref[:, pair * LANES:(pair + 1) * LANES] = mx + jnp.log(den)

    in_specs = [pl.BlockSpec((tm // d, 2 * d * LANES), lambda i: (i, 0)) for d in dils] * 2
    out = pl.BlockSpec((tm, 2 * LANES), lambda i: (i, 0))
    shp = jax.ShapeDtypeStruct((s_len, 2 * LANES), F32)
    return pl.pallas_call(
        body, name=name, grid=(s_len // tm,), in_specs=in_specs, out_specs=(out, out), out_shape=(shp, shp),
        scratch_shapes=[pltpu.VMEM((tm, LANES), F32)] * 6, compiler_params=_cparams(1),
    )(*os_, *lses)


def _bwd_prep(do, o, lse, dils, sinks, name):
    s_len, width = do.shape
    n_pairs = width // LANES
    tm = 512
    has_sink = sinks is not None
    n_g = len(dils)

    def body(*refs):
        do_ref, o_ref, lse_ref = refs[:3]
        pos = 3
        if has_sink:
            sink_ref = refs[pos]
            pos += 1
        outs = refs[pos:pos + 3 * n_g]
        pos += 3 * n_g
        if has_sink:
            dsink_ref = refs[pos]
            pos += 1
        s_do, s_l, s_d = refs[pos:pos + 3]
        seg = _seg_matrix(HEAD_DIM)

        if has_sink:
            @pl.when(pl.program_id(0) == 0)
            def _():
                dsink_ref[...] = jnp.zeros_like(dsink_ref)

        for pair in range(n_pairs):
            col = slice(pair * LANES, (pair + 1) * LANES)
            dov = do_ref[:, col]
            lv = lse_ref[:, col]
            delta = _seg_sum(dov * o_ref[:, col], seg)
            if has_sink:
                dsink_ref[pair] += -jnp.sum(jnp.exp(sink_ref[pair] - lv) * delta, axis=0, keepdims=True)
            s_do[...] = dov
            s_l[...] = lv
            s_d[...] = delta
            for g, d in enumerate(dils):
                rows = tm // d
                for r in range(d):
                    oc = slice((pair * d + r) * LANES, (pair * d + r + 1) * LANES)
                    if d == 1:
                        a, b, c = s_do[...], s_l[...], s_d[...]
                    else:
                        a = s_do[pl.ds(r, rows, stride=d), :]
                        b = s_l[pl.ds(r, rows, stride=d), :]
                        c = s_d[pl.ds(r, rows, stride=d), :]
                    outs[3 * g][:, oc] = a.astype(BF16)
                    outs[3 * g + 1][:, oc] = b
                    outs[3 * g + 2][:, oc] = c

    row = pl.BlockSpec((tm, width), lambda i: (i, 0))
    in_specs = [row, row, row]
    args = [do, o, lse]
    if has_sink:
        in_specs.append(pl.BlockSpec((n_pairs, 1, LANES), lambda i: (0, 0, 0)))
        args.append(sinks)
    out_specs, out_shape = [], []
    for d in dils:
        for dt in (BF16, F32, F32):
            out_specs.append(pl.BlockSpec((tm // d, n_pairs * d * LANES), lambda i: (i, 0)))
            out_shape.append(jax.ShapeDtypeStruct((s_len // d, n_pairs * d * LANES), dt))
    if has_sink:
        out_specs.append(pl.BlockSpec((n_pairs, 1, LANES), lambda i: (0, 0, 0)))
        out_shape.append(jax.ShapeDtypeStruct((n_pairs, 1, LANES), F32))
    return pl.pallas_call(
        body, name=name, grid=(s_len // tm,), in_specs=in_specs, out_specs=tuple(out_specs),
        out_shape=tuple(out_shape), scratch_shapes=[pltpu.VMEM((tm, LANES), F32)] * 3, compiler_params=_cparams(1),
    )(*args)


def _mem_kv(mem, mem_gain, w_kv, k_gain, name):
    m_len = mem.shape[0]
    kw = M_HEADS * M_HEAD_DIM

    def body(mem_ref, mg_ref, w_ref, kg_ref, k_ref, v_ref):
        mv = mem_ref[...]
        r = lax.rsqrt(jnp.mean(mv * mv, axis=-1, keepdims=True) + EPS)
        mn = ((mv * r) * mg_ref[...]).astype(BF16)
        kv = jnp.dot(mn, w_ref[...], preferred_element_type=F32)
        for h in range(M_HEADS):
            col = slice(h * M_HEAD_DIM, (h + 1) * M_HEAD_DIM)
            t = kv[:, col]
            rk = lax.rsqrt(jnp.mean(t * t, axis=-1, keepdims=True) + EPS)
            k_ref[:, col] = ((t * rk) * kg_ref[...]).astype(BF16)
        v_ref[...] = kv[:, kw:].astype(BF16)

    shp = jax.ShapeDtypeStruct((m_len, kw), BF16)
    return pl.pallas_call(body, name=name, out_shape=(shp, shp),
                          compiler_params=pltpu.CompilerParams(vmem_limit_bytes=VMEM_LIMIT_BYTES))(mem, mem_gain, w_kv, k_gain)


def _mem_kv_bwd(mem, mem_gain, w_kv, k_gain, dk, dv, name):
    m_len, d = mem.shape
    kw = M_HEADS * M_HEAD_DIM

    def body(mem_ref, mg_ref, w_ref, kg_ref, dk_ref, dv_ref, dw_ref, dmg_ref, dkg_ref, dkv_ref):
        mv = mem_ref[...]
        r = lax.rsqrt(jnp.mean(mv * mv, axis=-1, keepdims=True) + EPS)
        mhat = mv * r
        mn = (mhat * mg_ref[...]).astype(BF16)
        kv = jnp.dot(mn, w_ref[...], preferred_element_type=F32)
        dkg = jnp.zeros((1, M_HEAD_DIM), F32)
        for h in range(M_HEADS):
            col = slice(h * M_HEAD_DIM, (h + 1) * M_HEAD_DIM)
            t = kv[:, col]
            rk = lax.rsqrt(jnp.mean(t * t, axis=-1, keepdims=True) + EPS)
            that = t * rk
            dy = dk_ref[:, col]
            dkg = dkg + jnp.sum(dy * that, axis=0, keepdims=True)
            dthat = dy * kg_ref[...]
            dkv_ref[:, col] = (rk * (dthat - that * jnp.mean(dthat * that, axis=-1, keepdims=True))).astype(BF16)
        dkv_ref[:, kw:] = dv_ref[...].astype(BF16)
        dkg_ref[...] = dkg
        dkv = dkv_ref[...]
        dw_ref[...] = lax.dot_general(mn, dkv, (((0,), (0,)), ((), ())), preferred_element_type=F32)
        dmn = lax.dot_general(dkv, w_ref[...], (((1,), (1,)), ((), ())), preferred_element_type=F32)
        dmg_ref[...] = jnp.sum(dmn * mhat, axis=0, keepdims=True)

    return pl.pallas_call(
        body, name=name,
        out_shape=(jax.ShapeDtypeStruct((d, 2 * kw), F32), jax.ShapeDtypeStruct((1, d), F32),
                   jax.ShapeDtypeStruct((1, M_HEAD_DIM), F32)),
        scratch_shapes=[pltpu.VMEM((m_len, 2 * kw), BF16)],
        compiler_params=pltpu.CompilerParams(vmem_limit_bytes=VMEM_LIMIT_BYTES),
    )(mem, mem_gain, w_kv, k_gain, dk, dv)


def _mem_attn_fwd(proj, cidx, mk, mv, q_gain, name):
    s_len = proj.shape[0]
    kw = M_HEADS * M_HEAD_DIM
    tm = 512
    scale = M_HEAD_DIM ** -0.5

    def body(q_ref, k_ref, v_ref, g_ref, o_ref):
        for h in range(M_HEADS):
            col = slice(h * M_HEAD_DIM, (h + 1) * M_HEAD_DIM)
            t = q_ref[:, col]
            rs = lax.rsqrt(jnp.mean(t * t, axis=-1, keepdims=True) + EPS)
            qn = ((t * rs) * g_ref[...]).astype(BF16)
            s = lax.dot_general(qn, k_ref[:, col], (((1,), (1,)), ((), ())), preferred_element_type=F32) * scale
            mx = jnp.max(s, axis=-1, keepdims=True)
            p = jnp.exp(s - mx)
            pn = (p * (1.0 / jnp.sum(p, axis=-1, keepdims=True))).astype(BF16)
            o_ref[:, col] = jnp.dot(pn, v_ref[:, col], preferred_element_type=F32).astype(BF16)

    whole = pl.BlockSpec((MEM_LEN, kw), lambda i: (0, 0))
    return pl.pallas_call(
        body, name=name, grid=(s_len // tm,),
        in_specs=[pl.BlockSpec((tm, kw), lambda i: (i, cidx)), whole, whole, pl.BlockSpec((1, M_HEAD_DIM), lambda i: (0, 0))],
        out_specs=pl.BlockSpec((tm, kw), lambda i: (i, 0)),
        out_shape=jax.ShapeDtypeStruct((s_len, kw), BF16), compiler_params=_cparams(1),
    )(proj, mk, mv, q_gain)


def _mem_attn_bwd(proj, cidx, mk, mv, q_gain, do, name):
    s_len = proj.shape[0]
    kw = M_HEADS * M_HEAD_DIM
    tm = 512
    scale = M_HEAD_DIM ** -0.5

    def body(q_ref, k_ref, v_ref, g_ref, do_ref, dq_ref, dk_ref, dv_ref, dg_ref):
        @pl.when(pl.program_id(0) == 0)
        def _():
            dk_ref[...] = jnp.zeros_like(dk_ref)
            dv_ref[...] = jnp.zeros_like(dv_ref)
            dg_ref[...] = jnp.zeros_like(dg_ref)

        for h in range(M_HEADS):
            col = slice(h * M_HEAD_DIM, (h + 1) * M_HEAD_DIM)
            t = q_ref[:, col]
            rs = lax.rsqrt(jnp.mean(t * t, axis=-1, keepdims=True) + EPS)
            that = t * rs
            qn = (that * g_ref[...]).astype(BF16)
            kh, vh = k_ref[:, col], v_ref[:, col]
            dob = do_ref[:, col].astype(BF16)
            s = lax.dot_general(qn, kh, (((1,), (1,)), ((), ())), preferred_element_type=F32) * scale
            mx = jnp.max(s, axis=-1, keepdims=True)
            p = jnp.exp(s - mx)
            p = p * (1.0 / jnp.sum(p, axis=-1, keepdims=True))
            dp = lax.dot_general(dob, vh, (((1,), (1,)), ((), ())), preferred_element_type=F32)
            ds = (p * (dp - jnp.sum(p * dp, axis=-1, keepdims=True)) * scale).astype(BF16)
            dqn = jnp.dot(ds, kh, preferred_element_type=F32)
            dk_ref[:, col] += lax.dot_general(ds, qn, (((0,), (0,)), ((), ())), preferred_element_type=F32)
            dv_ref[:, col] += lax.dot_general(p.astype(BF16), dob, (((0,), (0,)), ((), ())), preferred_element_type=F32)
            dg_ref[...] += jnp.sum(dqn * that, axis=0, keepdims=True)
            dthat = dqn * g_ref[...]
            dq_ref[:, col] = (rs * (dthat - that * jnp.mean(dthat * that, axis=-1, keepdims=True))).astype(BF16)

    whole = pl.BlockSpec((MEM_LEN, kw), lambda i: (0, 0))
    vec = pl.BlockSpec((1, M_HEAD_DIM), lambda i: (0, 0))
    row = pl.BlockSpec((tm, kw), lambda i: (i, 0))
    return pl.pallas_call(
        body, name=name, grid=(s_len // tm,),
        in_specs=[pl.BlockSpec((tm, kw), lambda i: (i, cidx)), whole, whole, vec, row],
        out_specs=(row, whole, whole, vec),
        out_shape=(jax.ShapeDtypeStruct((s_len, kw), BF16), jax.ShapeDtypeStruct((MEM_LEN, kw), F32),
                   jax.ShapeDtypeStruct((MEM_LEN, kw), F32), jax.ShapeDtypeStruct((1, M_HEAD_DIM), F32)),
        compiler_params=_cparams(1),
    )(proj, mk, mv, q_gain, do)


def _project_merge(outs, w_outs, gates, name):
    s_len = gates.shape[0]
    d = w_outs[0].shape[1]
    tm = 512

    def body(oa_ref, ob_ref, om_ref, wa_ref, wb_ref, wm_ref, g_ref, pa_ref, pb_ref, pm_ref, merged_ref):
        merged = None
        for k, (o_ref, w_ref, p_ref) in enumerate(((oa_ref, wa_ref, pa_ref), (ob_ref, wb_ref, pb_ref), (om_ref, wm_ref, pm_ref))):
            p = jnp.dot(o_ref[...].astype(BF16), w_ref[...], preferred_element_type=F32).astype(BF16)
            p_ref[...] = p
            t = g_ref[:, k * d:(k + 1) * d].astype(F32) * p.astype(F32)
            merged = t if merged is None else merged + t
        merged_ref[...] = merged.astype(BF16)

    row = pl.BlockSpec((tm, d), lambda i: (i, 0))
    shp = jax.ShapeDtypeStruct((s_len, d), BF16)
    in_specs = [pl.BlockSpec((tm, o.shape[1]), lambda i: (i, 0)) for o in outs]
    in_specs += [_resident(w.shape, lambda i: (0, 0)) for w in w_outs]
    in_specs.append(pl.BlockSpec((tm, 3 * d), lambda i: (i, 0)))
    return pl.pallas_call(
        body, name=name, grid=(s_len // tm,), in_specs=in_specs, out_specs=(row, row, row, row),
        out_shape=(shp, shp, shp, shp), compiler_params=_cparams(1),
    )(*outs, *w_outs, gates)


def _project_merge_bwd(dx1, w_out, gates, pa, pb, pm, name):
    s_len, d = pa.shape
    tm = 512

    def body(dx_ref, w_ref, g_ref, a_ref, b_ref, m_ref, da_ref, db_ref, dmm_ref, dg_ref, dbg_ref):
        @pl.when(pl.program_id(0) == 0)
        def _():
            dbg_ref[...] = jnp.zeros_like(dbg_ref)
        dm = lax.dot_general(dx_ref[...], w_ref[...], (((1,), (1,)), ((), ())), preferred_element_type=F32)
        for k, (p_ref, dp_ref) in enumerate(((a_ref, da_ref), (b_ref, db_ref), (m_ref, dmm_ref))):
            col = slice(k * d, (k + 1) * d)
            g = g_ref[:, col].astype(F32)
            dp_ref[...] = (dm * g).astype(BF16)
            dpre = (dm * p_ref[...].astype(F32)) * (g * (1.0 - g))
            dbg_ref[:, col] += jnp.sum(dpre, axis=0, keepdims=True)
            dg_ref[:, col] = dpre.astype(BF16)

    row = pl.BlockSpec((tm, d), lambda i: (i, 0))
    wide = pl.BlockSpec((tm, 3 * d), lambda i: (i, 0))
    shp = jax.ShapeDtypeStruct((s_len, d), BF16)
    return pl.pallas_call(
        body, name=name, grid=(s_len // tm,), in_specs=[row, _resident(w_out.shape, lambda i: (0, 0)), wide, row, row, row],
        out_specs=(row, row, row, wide, pl.BlockSpec((1, 3 * d), lambda i: (0, 0))),
        out_shape=(shp, shp, shp, jax.ShapeDtypeStruct((s_len, 3 * d), BF16), jax.ShapeDtypeStruct((1, 3 * d), F32)),
        compiler_params=_cparams(1),
    )(dx1, w_out, gates, pa, pb, pm)


CONV_CHUNK = 256


def _pick_row(tile, j):
    row = lax.broadcasted_iota(jnp.int32, tile.shape, 0)
    return jnp.sum(jnp.where(row == j, tile, jnp.zeros_like(tile)), axis=0, keepdims=True)


def _rows_before(ref, start, k):
    cur = ref[pl.ds(start, CONV_CHUNK), :].astype(F32)
    prev = ref[pl.ds(pl.multiple_of(jnp.maximum(start - 16, 0), 16), 16), :].astype(F32)
    prev = jnp.where(start > 0, prev, jnp.zeros_like(prev))
    rolled = pltpu.roll(cur, k, 0)
    row = lax.broadcasted_iota(jnp.int32, cur.shape, 0)
    for j in range(k):
        rolled = jnp.where(row == j, _pick_row(prev, 16 - k + j), rolled)
    return rolled


def _rows_after(ref, start, k):
    cur = ref[pl.ds(start, CONV_CHUNK), :]
    nxt = ref[pl.ds(pl.multiple_of(start + CONV_CHUNK, 8), 8), :]
    rolled = pltpu.roll(cur, CONV_CHUNK - k, 0)
    row = lax.broadcasted_iota(jnp.int32, cur.shape, 0)
    for j in range(k):
        rolled = jnp.where(row == CONV_CHUNK - k + j, _pick_row(nxt, j), rolled)
    return rolled


def _conv_pre(u_ref, w_ref, b_ref, start):
    u2 = _rows_before(u_ref, start, 2)
    u1 = _rows_before(u_ref, start, 1)
    u0 = u_ref[pl.ds(start, CONV_CHUNK), :].astype(F32)
    c = ((b_ref[...] + w_ref[0:1, :] * u2) + w_ref[1:2, :] * u1) + w_ref[2:3, :] * u0
    return c, (u2, u1, u0)


def _norm_up_conv_glu(x, gain, w_up, conv_w, conv_b, name):
    s_len, d = x.shape
    tm, tn = 512, 2 * LANES
    nblk = D_FF // tn

    def body(x_ref, g_ref, w_ref, cw_ref, cb_ref, ht_ref, r_ref, u_ref, act_ref, halo):
        @pl.when(pl.program_id(0) == 0)
        def _():
            halo[...] = jnp.zeros_like(halo)
        xv = x_ref[...]
        r = lax.rsqrt(jnp.mean(xv * xv, axis=-1, keepdims=True) + EPS)
        hv = ((xv * r) * g_ref[...]).astype(BF16)
        ht_ref[...] = hv.T
        r_ref[...] = r
        row = lax.broadcasted_iota(jnp.int32, (tm, tn), 0)
        for j in range(nblk):
            conv = []
            for half in range(2):
                cb = half * nblk + j
                cols = slice(cb * tn, (cb + 1) * tn)
                ub = jnp.dot(hv, w_ref[:, cols], preferred_element_type=F32).astype(BF16)
                u_ref[:, cols] = ub
                u0 = ub.astype(F32)
                prev = halo[cb]
                u1 = jnp.where(row == 0, _pick_row(prev, 7), pltpu.roll(u0, 1, 0))
                u2 = pltpu.roll(u0, 2, 0)
                u2 = jnp.where(row == 0, _pick_row(prev, 6), jnp.where(row == 1, _pick_row(prev, 7), u2))
                halo[cb] = u0[tm - 8:tm, :]
                conv.append(((cb_ref[:, cols] + cw_ref[0:1, cols] * u2) + cw_ref[1:2, cols] * u1)
                            + cw_ref[2:3, cols] * u0)
            act_ref[:, j * tn:(j + 1) * tn] = ((conv[0] * _sigmoid(conv[0])) * conv[1]).astype(BF16)

    return pl.pallas_call(
        body, name=name, grid=(s_len // tm,),
        in_specs=[pl.BlockSpec((tm, d), lambda i: (i, 0)), _resident((1, d), lambda i: (0, 0)),
                  _resident((d, 2 * D_FF), lambda i: (0, 0)),
                  _resident((3, 2 * D_FF), lambda i: (0, 0)), _resident((1, 2 * D_FF), lambda i: (0, 0))],
        out_specs=(pl.BlockSpec((d, tm), lambda i: (0, i)), pl.BlockSpec((tm, 1), lambda i: (i, 0)),
                   pl.BlockSpec((tm, 2 * D_FF), lambda i: (i, 0)), pl.BlockSpec((tm, D_FF), lambda i: (i, 0))),
        out_shape=(jax.ShapeDtypeStruct((d, s_len), BF16), jax.ShapeDtypeStruct((s_len, 1), F32),
                   jax.ShapeDtypeStruct((s_len, 2 * D_FF), BF16), jax.ShapeDtypeStruct((s_len, D_FF), BF16)),
        scratch_shapes=[pltpu.VMEM((2 * nblk, 8, tn), F32)], compiler_params=_cparams(1),
    )(x, gain, w_up, conv_w, conv_b)


def _conv_glu_bwd(dact, u, conv_w, conv_b, name):
    s_len = u.shape[0]
    nblk = D_FF // LANES
    n_chunks = s_len // CONV_CHUNK

    def body(da_ref, ua_ref, ug_ref, wa_ref, wg_ref, ba_ref, bg_ref,
             dua_ref, dug_ref, dwa_ref, dwg_ref, dba_ref, dbg_ref, sa, sg):
        sa[pl.ds(s_len, 8), :] = jnp.zeros((8, LANES), F32)
        sg[pl.ds(s_len, 8), :] = jnp.zeros((8, LANES), F32)
        zero = jnp.zeros((1, LANES), F32)

        def chunk1(ci, carry):
            start = pl.multiple_of(ci * CONV_CHUNK, CONV_CHUNK)
            ca, ua = _conv_pre(ua_ref, wa_ref, ba_ref, start)
            cg, ug = _conv_pre(ug_ref, wg_ref, bg_ref, start)
            dact_v = da_ref[pl.ds(start, CONV_CHUNK), :].astype(F32)
            sig = _sigmoid(ca)
            dcg = dact_v * (ca * sig)
            dca = (dact_v * cg) * (sig * (1.0 + ca * (1.0 - sig)))
            sa[pl.ds(start, CONV_CHUNK), :] = dca
            sg[pl.ds(start, CONV_CHUNK), :] = dcg
            out = [carry[0] + jnp.sum(dca, axis=0, keepdims=True), carry[1] + jnp.sum(dcg, axis=0, keepdims=True)]
            for j in range(3):
                out.append(carry[2 + j] + jnp.sum(dca * ua[j], axis=0, keepdims=True))
            for j in range(3):
                out.append(carry[5 + j] + jnp.sum(dcg * ug[j], axis=0, keepdims=True))
            return tuple(out)

        acc = lax.fori_loop(0, n_chunks, chunk1, (zero,) * 8)
        dba_ref[...] = acc[0]
        dbg_ref[...] = acc[1]
        for j in range(3):
            dwa_ref[j:j + 1, :] = acc[2 + j]
            dwg_ref[j:j + 1, :] = acc[5 + j]

        def chunk2(ci, carry):
            start = pl.multiple_of(ci * CONV_CHUNK, CONV_CHUNK)
            for s_ref, w_ref, o_ref in ((sa, wa_ref, dua_ref), (sg, wg_ref, dug_ref)):
                d0 = s_ref[pl.ds(start, CONV_CHUNK), :]
                d1 = _rows_after(s_ref, start, 1)
                d2 = _rows_after(s_ref, start, 2)
                o_ref[pl.ds(start, CONV_CHUNK), :] = (w_ref[2:3, :] * d0 + w_ref[1:2, :] * d1
                                                      + w_ref[0:1, :] * d2).astype(BF16)
            return carry
        lax.fori_loop(0, n_chunks, chunk2, 0)

    def col(rows, off):
        return pl.BlockSpec((rows, LANES), lambda j: (0, off + j))

    big = jax.ShapeDtypeStruct((s_len, D_FF), BF16)
    return pl.pallas_call(
        body, name=name, grid=(nblk,),
        in_specs=[col(s_len, 0), col(s_len, 0), col(s_len, nblk), col(3, 0), col(3, nblk), col(1, 0), col(1, nblk)],
        out_specs=(col(s_len, 0), col(s_len, 0), col(3, 0), col(3, 0), col(1, 0), col(1, 0)),
        out_shape=(big, big, jax.ShapeDtypeStruct((3, D_FF), F32), jax.ShapeDtypeStruct((3, D_FF), F32),
                   jax.ShapeDtypeStruct((1, D_FF), F32), jax.ShapeDtypeStruct((1, D_FF), F32)),
        scratch_shapes=[pltpu.VMEM((s_len + 8, LANES), F32)] * 2, compiler_params=_cparams(1),
    )(dact, u, u, conv_w, conv_w, conv_b, conv_b)


def _rope_tables(positions):
    half = ROPE_DIMS // 2
    freqs = jnp.exp(jnp.arange(half, dtype=F32) * (-2.0 * math.log(ROPE_THETA) / ROPE_DIMS))
    ang = positions.reshape(-1).astype(F32)[:, None] * freqs
    cos, sin = jnp.cos(ang), jnp.sin(ang)
    n = ang.shape[0]
    zeros = lambda w: jnp.zeros((n, w), F32)
    c = jnp.concatenate([cos, cos, jnp.ones((n, HEAD_DIM - ROPE_DIMS), F32)], axis=1)
    s1 = jnp.concatenate([-sin, zeros(HEAD_DIM - half)], axis=1)
    s2 = jnp.concatenate([zeros(half), sin, zeros(HEAD_DIM - ROPE_DIMS)], axis=1)
    return tuple(jnp.tile(t, (1, 2)) for t in (c, s1, s2))


def _two(v):
    return jnp.tile(v.reshape(1, HEAD_DIM), (1, 2))


def _fold_heads(g):
    return g[0, :HEAD_DIM] + g[0, HEAD_DIM:]


MIX_WEIGHTS = ('w_gate', 'w_mem_kv', 'w_o_a', 'w_o_b', 'w_o_m', 'w_out')
FFN_WEIGHTS = ('w_up', 'conv_w', 'w_down')


def _device_step(x, mem, positions, target, w, hooks=None):
    tabs = _rope_tables(positions)
    dils = tuple(d for _, d in A_GROUPS)
    grads = {}
    w = dict(w)

    h, h_t, r1 = _rms_fwd(x, w['attn_norm'], "rms1")
    if hooks is not None:
        w.update(hooks.weights('in', [h, *tabs]))
    proj = _mm_rows([(h, w['w_in'], 0)], "mm_in")

    qkv_a, o_g, lse_g = [], [], []
    for gi, (window, d) in enumerate(A_GROUPS):
        gq, gk = _two(w['a_q_norm'][gi]), _two(w['a_k_norm'][gi])
        qkv = _qk_prep(proj, 6 * gi, d, False, gq, gk, tabs, f"qk_prep_a{gi}")
        o, lse = _band_fwd(qkv, 2 * d, window // d, None, f"band_fwd_a{gi}")
        qkv_a.append(qkv)
        o_g.append(o)
        lse_g.append(lse)
    o_a, lse_a = _merge_groups(o_g, lse_g, dils, "merge_a")
    if hooks is not None:
        w.update(hooks.weights('mix', [o_a]))

    gbq, gbk = _two(w['b_q_norm']), _two(w['b_k_norm'])
    sinks = jnp.repeat(w['b_sinks'].reshape(4, 2), HEAD_DIM, axis=1).reshape(4, 1, LANES)
    qkv_b = _qk_prep(proj, 18, 1, True, gbq, gbk, tabs, "qk_prep_b")
    o_b, lse_b = _band_fwd(qkv_b, 4, B_WINDOW - 1, sinks, "band_fwd_b")

    gates = _mm_rows([(h, w['w_gate'], 0)], "mm_gate", bias=w['b_gate'], sigmoid=True, out_dtypes=(BF16,))
    mk, mv = _mem_kv(mem, w['mem_norm'], w['w_mem_kv'], w['m_k_norm'], "mem_kv")
    o_m = _mem_attn_fwd(proj, 6, mk, mv, w['m_q_norm'], "mem_attn")

    pa, pb, pm, merged = _project_merge((o_a, o_b, o_m), (w['w_o_a'], w['w_o_b'], w['w_o_m']), gates, "project_merge")
    x1 = _mm_rows([(merged, w['w_out'], 0)], "mm_out", res=x)

    if hooks is not None:
        w.update(hooks.weights('ffn', [x1]))
    h2_t, r2, u, act = _norm_up_conv_glu(x1, w['ffn_norm'], w['w_up'], w['conv_w'], w['conv_b'], "norm_up_conv_glu")
    dy, dy_b, loss = _mm_rows([(act, w['w_down'], 0)], "mm_down", res=x1, loss_target=target)

    dact = _mm_rows([(dy_b, w['w_down'], 0)], "mm_d_act", nt=True, out_dtypes=(BF16,))
    grads['w_down'] = _mm_tn(act, dy_b, "mm_dw_down")
    du_a, du_g, dcw_a, dcw_g, dcb_a, dcb_g = _conv_glu_bwd(dact, u, w['conv_w'], w['conv_b'], "conv_glu_bwd")
    grads['conv_w'] = jnp.concatenate([dcw_a, dcw_g], axis=1)
    grads['conv_b'] = jnp.concatenate([dcb_a, dcb_g], axis=1)
    grads['w_up'] = jnp.concatenate([_mm_cols(h2_t, du_a, "mm_dw_up_a"), _mm_cols(h2_t, du_g, "mm_dw_up_g")], axis=1)
    ffn_gain = w['ffn_norm']
    if hooks is not None:
        ffn_gain = ffn_gain + hooks.grads('ffn', grads)[0:1, 0:1]
    dx1, dx1_b, grads['ffn_norm'] = _mm_rows([(du_a, w['w_up'], 0), (du_g, w['w_up'], 1)], "mm_d_h2", nt=True,
                                             rms_bwd=(x1, r2, ffn_gain, dy), out_dtypes=(F32, BF16))

    grads['w_out'] = _mm_tn(merged, dx1_b, "mm_dw_out")
    dpa, dpb, dpm, dgpre, grads['b_gate'] = _project_merge_bwd(dx1_b, w['w_out'], gates, pa, pb, pm,
                                                               "project_merge_bwd")
    do_a, do_b, do_m = _mm_rows_each([(dpa, w['w_o_a']), (dpb, w['w_o_b']), (dpm, w['w_o_m'])], "mm_d_o")
    grads['w_o_a'] = _mm_tn(o_a, dpa, "mm_dw_oa")
    grads['w_o_b'] = _mm_tn(o_b, dpb, "mm_dw_ob")
    grads['w_o_m'] = _mm_tn(o_m, dpm, "mm_dw_om")
    grads['w_gate'] = _mm_cols(h_t, dgpre, "mm_dw_gate")
    dq_m, dmk, dmv, grads['m_q_norm'] = _mem_attn_bwd(proj, 6, mk, mv, w['m_q_norm'], do_m, "mem_attn_bwd")
    grads['w_mem_kv'], grads['mem_norm'], grads['m_k_norm'] = _mem_kv_bwd(
        mem, w['mem_norm'], w['w_mem_kv'], w['m_k_norm'], dmk, dmv, "mem_kv_bwd")
    a_gain = w['a_q_norm']
    if hooks is not None:
        a_gain = a_gain + hooks.grads('mix', grads)[0:1, 0:1]

    prep = _bwd_prep(do_a, o_a, lse_a, dils, None, "bwd_prep_a")
    dproj, dgq_a, dgk_a = [], [], []
    for gi, (window, d) in enumerate(A_GROUPS):
        gq, gk = _two(a_gain[gi]), _two(w['a_k_norm'][gi])
        dqkv = _band_bwd(qkv_a[gi], prep[3 * gi], prep[3 * gi + 1], prep[3 * gi + 2], 2 * d, window // d,
                         f"band_bwd_a{gi}")
        dp, dgq, dgk = _qk_prep_bwd(dqkv, proj, 6 * gi, d, False, gq, gk, tabs, f"qk_prep_bwd_a{gi}")
        dproj.append(dp)
        dgq_a.append(_fold_heads(dgq))
        dgk_a.append(_fold_heads(dgk))
    grads['a_q_norm'] = jnp.stack(dgq_a)
    grads['a_k_norm'] = jnp.stack(dgk_a)

    do_bu, lse_bu, delta_bu, dsink = _bwd_prep(do_b, o_b, lse_b, (1,), sinks, "bwd_prep_b")
    dqkv = _band_bwd(qkv_b, do_bu, lse_bu, delta_bu, 4, B_WINDOW - 1, "band_bwd_b")
    dp_b, dgq, dgk = _qk_prep_bwd(dqkv, proj, 18, 1, True, gbq, gbk, tabs, "qk_prep_bwd_b")
    dproj.append(dp_b)
    grads['b_q_norm'] = _fold_heads(dgq)
    grads['b_k_norm'] = _fold_heads(dgk)
    grads['b_sinks'] = jnp.stack([dsink[:, 0, 0], dsink[:, 0, HEAD_DIM]], axis=1).reshape(8)

    dproj.append(dq_m)

    cols = (0, 1, 2, 3, 6)
    grads['w_in'] = _mm_rows_cat(h_t, dproj, "mm_dw_in")
    attn_gain = w['attn_norm']
    if hooks is not None:
        attn_gain = attn_gain + hooks.grads('in', grads)[0:1, 0:1]
    grad_x, grads['attn_norm'] = _mm_rows(
        [(dp, w['w_in'], c) for dp, c in zip(dproj, cols)] + [(dgpre, w['w_gate'], 0)], "mm_d_h", nt=True,
        rms_bwd=(x, r1, attn_gain, dx1))
    return loss, grad_x, grads


def _coords():
    return lax.axis_index("x"), lax.axis_index("y"), lax.axis_index("c")


def _slot(p):
    return 4 * p[0] + 2 * p[1] + p[2]


ALL_PEERS = tuple(range(1, N_DEV))
CHIP_PEERS = (1, 4, 2, 6)
OTHER_CHIPS = (4, 2, 6)


def _peers(me, masks=ALL_PEERS):
    x, y, c = me
    return [(1 - x if mask & 4 else x, 1 - y if mask & 2 else y, 1 - c if mask & 1 else c) for mask in masks]


HBM_SPEC = pl.BlockSpec(memory_space=pltpu.HBM)


SEM_SPEC = pl.BlockSpec(memory_space=pltpu.SEMAPHORE)
SIDE_EFFECT = pltpu.SideEffectType.DATAFLOW_SIDE_EFFECTING


def _exchange_start(blocks, name, gather=False, masks=ALL_PEERS, after=None):
    n = len(blocks)
    n_peers = len(masks)
    n_in = 2 * n + (0 if after is None else 1)

    def body(*refs):
        ins, lands = refs[:n], refs[n:2 * n]
        send_sems, recv_sems = refs[n_in], refs[n_in + 1]
        token = refs[-1]
        me = _coords()
        peers = _peers(me, masks)
        for a in range(n):
            for k in range(n_peers):
                pltpu.make_async_remote_copy(
                    src_ref=ins[a] if gather else ins[a].at[_slot(peers[k])], dst_ref=lands[a].at[_slot(me)],
                    send_sem=send_sems.at[a * n_peers + k], recv_sem=recv_sems.at[a * n_peers + k],
                    device_id=peers[k], device_id_type=MESH).start()
        token[...] = jnp.zeros_like(token)

    land_shapes = [((N_DEV,) + b.shape) if gather else b.shape for b in blocks]
    hbm_in = [pltpu.HBM(b.shape, b.dtype) for b in blocks]
    hbm_land = [pltpu.HBM(s, b.dtype) for s, b in zip(land_shapes, blocks)]
    sems = pltpu.SemaphoreType.DMA((n * n_peers,))
    ins = [pltpu.with_memory_space_constraint(b, pltpu.HBM) for b in blocks]
    lands = [pltpu.with_memory_space_constraint(lax.empty(s, b.dtype), pltpu.HBM) for s, b in zip(land_shapes, blocks)]
    return pl.pallas_call(
        body, name=name, out_shape=(sems, sems, *hbm_in, *hbm_land, jax.ShapeDtypeStruct((8, LANES), F32)),
        in_specs=[HBM_SPEC] * (2 * n) + ([] if after is None else [pl.BlockSpec(memory_space=pl.ANY)]),
        out_specs=(SEM_SPEC, SEM_SPEC, *([HBM_SPEC] * (2 * n)), pl.BlockSpec(memory_space=pltpu.VMEM)),
        input_output_aliases={i: 2 + i for i in range(2 * n)},
        compiler_params=pltpu.CompilerParams(has_side_effects=SIDE_EFFECT),
    )(*ins, *lands, *([] if after is None else [after]))


def _exchange_wait(started, after, name, gather=False, masks=ALL_PEERS):
    n = (len(started) - 3) // 2
    n_peers = len(masks)
    send_sems, recv_sems = started[0], started[1]
    thru = started[2:2 + 2 * n]

    def body(*refs):
        ins, lands = refs[:n], refs[n:2 * n]
        send_ref, recv_ref = refs[2 * n], refs[2 * n + 1]
        me = _coords()
        peers = _peers(me, masks)
        for a in range(n):
            for k in range(n_peers):
                cp = pltpu.make_async_remote_copy(
                    src_ref=ins[a] if gather else ins[a].at[_slot(peers[k])], dst_ref=lands[a].at[_slot(peers[k])],
                    send_sem=send_ref.at[a * n_peers + k], recv_sem=recv_ref.at[a * n_peers + k],
                    device_id=peers[k], device_id_type=MESH)
                cp.wait_send()
                cp.wait_recv()

    hbm = [pltpu.HBM(t.shape, t.dtype) for t in thru]
    res = pl.pallas_call(
        body, name=name, out_shape=tuple(hbm),
        in_specs=[HBM_SPEC] * (2 * n) + [SEM_SPEC, SEM_SPEC] + [pl.BlockSpec(memory_space=pl.ANY)] * len(after),
        out_specs=tuple([HBM_SPEC] * (2 * n)), input_output_aliases={i: i for i in range(2 * n)},
        compiler_params=pltpu.CompilerParams(has_side_effects=SIDE_EFFECT),
    )(*thru, send_sems, recv_sems, *after)
    return res[n:]


def _sibling_forward(arrays, name):
    n = len(arrays)
    n_fwd = len(OTHER_CHIPS)

    def body(*refs):
        bufs = refs[n:2 * n]
        token, send_sems, recv_sems = refs[2 * n:]
        token[...] = jnp.zeros_like(token)
        x, y, c = _coords()
        sibling = (x, y, 1 - c)
        mine = _peers((x, y, c), OTHER_CHIPS)
        theirs = _peers(sibling, OTHER_CHIPS)

        def copy(a, k, block):
            rows = bufs[a].at[_slot(block)]
            return pltpu.make_async_remote_copy(
                src_ref=rows, dst_ref=rows, send_sem=send_sems.at[a * n_fwd + k], recv_sem=recv_sems.at[a * n_fwd + k],
                device_id=sibling, device_id_type=MESH)

        sends = [copy(a, k, mine[k]) for a in range(n) for k in range(n_fwd)]
        for cp in sends:
            cp.start()
        for a in range(n):
            for k in range(n_fwd):
                copy(a, k, theirs[k]).wait_recv()
        for cp in sends:
            cp.wait_send()

    res = pl.pallas_call(
        body, name=name, in_specs=[HBM_SPEC] * n,
        out_specs=tuple([HBM_SPEC] * n + [pl.BlockSpec(memory_space=pltpu.VMEM)]),
        out_shape=tuple([jax.ShapeDtypeStruct(a.shape, a.dtype) for a in arrays] + [jax.ShapeDtypeStruct((8, LANES), F32)]),
        input_output_aliases={i: i for i in range(n)},
        scratch_shapes=[pltpu.SemaphoreType.DMA((n * n_fwd,)), pltpu.SemaphoreType.DMA((n * n_fwd,))],
    )(*arrays)
    return res[:n], res[n]


def _all_sum(p, name):
    def body(p_ref, o_ref, recv, send_sems, recv_sems):
        me = _coords()
        peers = _peers(me)
        recv[_slot(me)] = p_ref[...]

        def copy(k, landing):
            return pltpu.make_async_remote_copy(
                src_ref=p_ref, dst_ref=recv.at[_slot(landing)], send_sem=send_sems.at[k], recv_sem=recv_sems.at[k],
                device_id=peers[k], device_id_type=MESH)

        sends = [copy(k, me) for k in range(N_DEV - 1)]
        for cp in sends:
            cp.start()
        for k in range(N_DEV - 1):
            copy(k, peers[k]).wait_recv()
        for cp in sends:
            cp.wait_send()
        acc = recv[0]
        for s in range(1, N_DEV):
            acc = acc + recv[s]
        o_ref[...] = acc

    vmem = pl.BlockSpec(memory_space=pltpu.VMEM)
    return pl.pallas_call(
        body, name=name, in_specs=[vmem], out_specs=vmem, out_shape=jax.ShapeDtypeStruct(p.shape, F32),
        scratch_shapes=[pltpu.VMEM((N_DEV,) + p.shape, F32), pltpu.SemaphoreType.DMA((N_DEV - 1,)),
                        pltpu.SemaphoreType.DMA((N_DEV - 1,))],
    )(p)


def _adam(w, g, m, v):
    m2 = ADAM_B1 * m + (1.0 - ADAM_B1) * g
    v2 = ADAM_B2 * v + (1.0 - ADAM_B2) * (g * g)
    m_hat = m2 / (1.0 - ADAM_B1 ** ADAM_STEP)
    v_hat = v2 / (1.0 - ADAM_B2 ** ADAM_STEP)
    delta = -ADAM_LR * (m_hat / (jnp.sqrt(v_hat) + ADAM_EPS) + ADAM_WD * w)
    return delta, m2, v2


def _row_tile(rows, cols):
    best = rows
    for t in range(16, rows, 16):
        if rows % t == 0 and t * cols * 4 <= (1 << 20):
            best = t
    return best


def _adam_reduce(parts, w, m, v, name):
    rows, cols = w.shape
    tr = _row_tile(rows, cols)

    def body(p_ref, w_ref, m_ref, v_ref, g_ref, d_ref, m2_ref, v2_ref):
        g = p_ref[0].astype(F32)
        for s in range(1, N_DEV):
            g = g + p_ref[s].astype(F32)
        g_ref[...] = g
        d_ref[...], m2_ref[...], v2_ref[...] = _adam(w_ref[...], g, m_ref[...], v_ref[...])

    blk = pl.BlockSpec((tr, cols), lambda i: (i, 0))
    shp = jax.ShapeDtypeStruct((rows, cols), F32)
    return pl.pallas_call(
        body, name=name, grid=(rows // tr,),
        in_specs=[pl.BlockSpec((N_DEV, tr, cols), lambda i: (0, i, 0)), blk, blk, blk],
        out_specs=(blk,) * 4, out_shape=(shp,) * 4, compiler_params=_cparams(1),
    )(parts, w, m, v)


PACK_COLS = 1024
PACK = {'attn_norm': (0, 1, 1024), 'mem_norm': (1, 1, 1024), 'ffn_norm': (2, 1, 1024), 'b_gate': (3, 3, 1024),
        'conv_b': (6, 6, 1024), 'a_q_norm': (12, 3, 64), 'a_k_norm': (15, 3, 64), 'b_q_norm': (18, 1, 64),
        'b_k_norm': (19, 1, 64), 'm_q_norm': (20, 1, 128), 'm_k_norm': (21, 1, 128), 'b_sinks': (22, 1, 8)}
PACK_LOSS_ROW = 23
PACK_ROWS = 24


def _pack_pieces(name, width):
    r0, nr, lanes = PACK[name]
    out = []
    for j in range(nr):
        if lanes == PACK_COLS:
            w = min(PACK_COLS, width - j * PACK_COLS)
            out.append((r0 + j, slice(0, 1), slice(j * PACK_COLS, j * PACK_COLS + w), w))
        else:
            out.append((r0 + j, slice(j, j + 1), slice(0, lanes), lanes))
    return out


def _pack_small(grads, loss_tile, name):
    names = list(PACK)

    def body(*refs):
        o_ref = refs[-1]
        o_ref[...] = jnp.zeros_like(o_ref)
        for k, nm in enumerate(names):
            for row, rs, ls, w in _pack_pieces(nm, refs[k].shape[1]):
                o_ref[row:row + 1, 0:w] = refs[k][rs, ls]
        o_ref[PACK_LOSS_ROW:PACK_LOSS_ROW + 1, 0:1] = refs[len(names)][0:1, 0:1]

    vmem = pl.BlockSpec(memory_space=pltpu.VMEM)
    args = [grads[nm] for nm in names] + [loss_tile]
    return pl.pallas_call(body, name=name, in_specs=[vmem] * len(args), out_specs=vmem,
                          out_shape=jax.ShapeDtypeStruct((PACK_ROWS, PACK_COLS), F32))(*args)


def _adam_small(gsum, ws, ms, vs, name):
    names = list(PACK)
    n = len(names)

    def body(*refs):
        g_ref = refs[0]
        w_refs, m_refs, v_refs = refs[1:1 + n], refs[1 + n:1 + 2 * n], refs[1 + 2 * n:1 + 3 * n]
        outs = refs[1 + 3 * n:]
        outs[0][...] = g_ref[PACK_LOSS_ROW:PACK_LOSS_ROW + 1, 0:1]
        for k, nm in enumerate(names):
            o_g, o_d, o_m, o_v = outs[1 + 4 * k:5 + 4 * k]
            for row, rs, ls, width in _pack_pieces(nm, w_refs[k].shape[1]):
                src = (rs, ls)
                g = g_ref[row:row + 1, 0:width]
                d, m2, v2 = _adam(w_refs[k][src], g, m_refs[k][src], v_refs[k][src])
                o_g[src] = g
                o_d[src] = d
                o_m[src] = m2
                o_v[src] = v2

    vmem = pl.BlockSpec(memory_space=pltpu.VMEM)
    shapes = [jax.ShapeDtypeStruct((1, 1), F32)]
    for nm in names:
        shapes += [jax.ShapeDtypeStruct(ws[nm].shape, F32)] * 4
    args = [gsum] + [ws[nm] for nm in names] + [ms[nm] for nm in names] + [vs[nm] for nm in names]
    return pl.pallas_call(
        body, name=name, in_specs=[vmem] * len(args), out_specs=tuple([vmem] * len(shapes)), out_shape=tuple(shapes),
    )(*args)


def _as2d(name, a):
    return a.reshape(a.shape[-2], a.shape[-1]) if a.ndim == 3 else a


def kernel(x, mem, positions, attn_norm, w_in, a_q_norm, a_k_norm, b_q_norm, b_k_norm, b_sinks, mem_norm, w_mem_kv, m_q_norm, m_k_norm, w_o_a, w_o_b, w_o_m, w_gate, b_gate, w_out, ffn_norm, w_up, conv_w, conv_b, w_down, loss_target, m_attn_norm, m_w_in, m_a_q_norm, m_a_k_norm, m_b_q_norm, m_b_k_norm, m_b_sinks, m_mem_norm, m_w_mem_kv, m_m_q_norm, m_m_k_norm, m_w_o_a, m_w_o_b, m_w_o_m, m_w_gate, m_b_gate, m_w_out, m_ffn_norm, m_w_up, m_conv_w, m_conv_b, m_w_down, v_attn_norm, v_w_in, v_a_q_norm, v_a_k_norm, v_b_q_norm, v_b_k_norm, v_b_sinks, v_mem_norm, v_w_mem_kv, v_m_q_norm, v_m_k_norm, v_w_o_a, v_w_o_b, v_w_o_m, v_w_gate, v_b_gate, v_w_out, v_ffn_norm, v_w_up, v_conv_w, v_conv_b, v_w_down):
    given = dict(attn_norm=attn_norm, w_in=w_in, a_q_norm=a_q_norm, a_k_norm=a_k_norm, b_q_norm=b_q_norm, b_k_norm=b_k_norm, b_sinks=b_sinks, mem_norm=mem_norm, w_mem_kv=w_mem_kv, m_q_norm=m_q_norm, m_k_norm=m_k_norm, w_o_a=w_o_a, w_o_b=w_o_b, w_o_m=w_o_m, w_gate=w_gate, b_gate=b_gate, w_out=w_out, ffn_norm=ffn_norm, w_up=w_up, conv_w=conv_w, conv_b=conv_b, w_down=w_down)
    mom1 = dict(attn_norm=m_attn_norm, w_in=m_w_in, a_q_norm=m_a_q_norm, a_k_norm=m_a_k_norm, b_q_norm=m_b_q_norm, b_k_norm=m_b_k_norm, b_sinks=m_b_sinks, mem_norm=m_mem_norm, w_mem_kv=m_w_mem_kv, m_q_norm=m_m_q_norm, m_k_norm=m_m_k_norm, w_o_a=m_w_o_a, w_o_b=m_w_o_b, w_o_m=m_w_o_m, w_gate=m_w_gate, b_gate=m_b_gate, w_out=m_w_out, ffn_norm=m_ffn_norm, w_up=m_w_up, conv_w=m_conv_w, conv_b=m_conv_b, w_down=m_w_down)
    mom2 = dict(attn_norm=v_attn_norm, w_in=v_w_in, a_q_norm=v_a_q_norm, a_k_norm=v_a_k_norm, b_q_norm=v_b_q_norm, b_k_norm=v_b_k_norm, b_sinks=v_b_sinks, mem_norm=v_mem_norm, w_mem_kv=v_w_mem_kv, m_q_norm=v_m_q_norm, m_k_norm=v_m_k_norm, w_o_a=v_w_o_a, w_o_b=v_w_o_b, w_o_m=v_w_o_m, w_gate=v_w_gate, b_gate=v_b_gate, w_out=v_w_out, ffn_norm=v_ffn_norm, w_up=v_w_up, conv_w=v_conv_w, conv_b=v_conv_b, w_down=v_w_down)

    big = list(BIG)
    stages = {'mix': list(MIX_WEIGHTS), 'ffn': list(FFN_WEIGHTS), 'in': ['w_in']}
    my_slot = _slot(_coords())

    def shard(n):
        return given[n][0] if n == 'conv_w' else given[n][0].astype(BF16)

    def whole(n, g):
        _, r, c = g.shape
        return g.reshape(N_DEV * r, c) if BIG[n] == 0 else g.transpose(1, 0, 2).reshape(r, N_DEV * c)

    def to_blocks(n, g):
        r, c = given[n].shape[1:]
        g = g.reshape(N_DEV, r, c) if BIG[n] == 0 else g.reshape(r, N_DEV, c).transpose(1, 0, 2)
        return g if n == 'conv_w' else g.astype(BF16)

    class Hooks:
        next_stage = {'in': 'mix', 'mix': 'ffn'}

        def __init__(self):
            self.coming, self.sent = {}, {}
            self.shards = {n: shard(n) for n in big}
            self.start_gather('in', None)

        def start_gather(self, stage, after):
            src = [self.shards[n] for n in stages[stage]]
            self.coming[stage] = _exchange_start(src, f"gather_{stage}_start", gather=True, masks=CHIP_PEERS,
                                                 after=after)

        def weights(self, stage, after):
            names = stages[stage]
            after = list(after)
            if stage == 'in':
                after += [self.shards[n] for n in stages['mix'] + stages['ffn']]
            landed = _exchange_wait(self.coming[stage], after, f"gather_{stage}_wait", gather=True, masks=CHIP_PEERS)
            landed, token = _sibling_forward(landed, f"gather_{stage}_forward")
            if stage in self.next_stage:
                self.start_gather(self.next_stage[stage], token)
            return {n: whole(n, lax.dynamic_update_slice_in_dim(land, self.shards[n][None], my_slot, axis=0))
                    for n, land in zip(names, landed)}

        def grads(self, stage, g):
            blocks = [to_blocks(n, g[n]) for n in stages[stage]]
            own = [lax.dynamic_slice_in_dim(b, my_slot, 1, axis=0) for b in blocks]
            self.sent[stage] = (_exchange_start(blocks, f"exchange_{stage}_start"), own)
            return self.sent[stage][0][-1]

        def parts(self, stage, after):
            started, own = self.sent[stage]
            landed = _exchange_wait(started, [after], f"exchange_{stage}_wait")
            return {n: lax.dynamic_update_slice_in_dim(land, o, my_slot, axis=0)
                    for n, land, o in zip(stages[stage], landed, own)}

    hooks = Hooks()
    w = {}
    for n in SMALL:
        w[n] = given[n]
    w['a_q_norm'], w['a_k_norm'] = given['a_q_norm'][0], given['a_k_norm'][0]
    w['b_q_norm'], w['b_k_norm'], w['b_sinks'] = given['b_q_norm'][0], given['b_k_norm'][0], given['b_sinks'][0]

    loss_tile, grad_x, grads = _device_step(x[0], mem[0], positions[0], loss_target[0], w, hooks)
    out = {}
    after = grad_x
    for stage in ('ffn', 'mix', 'in'):
        for n, p in hooks.parts(stage, after).items():
            res = _adam_reduce(p, given[n][0], mom1[n][0], mom2[n][0], f"adam_{n}")
            out[n] = tuple(t[None] for t in res)
            after = res[0]

    small = {n: grads[n] for n in PACK}
    small['b_q_norm'], small['b_k_norm'] = grads['b_q_norm'].reshape(1, -1), grads['b_k_norm'].reshape(1, -1)
    small['b_sinks'] = grads['b_sinks'].reshape(1, -1)
    gsum = _all_sum(_pack_small(small, loss_tile, "pack_small"), "sum_small")
    ws = {n: _as2d(n, given[n]) for n in PACK}
    ms = {n: _as2d(n, mom1[n]) for n in PACK}
    vs = {n: _as2d(n, mom2[n]) for n in PACK}
    res = _adam_small(gsum, ws, ms, vs, "adam_small")
    loss = res[0].reshape(())
    for k, n in enumerate(PACK):
        out[n] = tuple(t.reshape(given[n].shape) for t in res[1 + 4 * k:5 + 4 * k])

    outs = [loss, grad_x[None]]
    for field in range(4):
        outs += [out[n][field] for n in WEIGHTS]
    return tuple(outs)
```

```python
import functools
import math

import jax
import jax.numpy as jnp
from jax import lax
from jax.experimental import pallas as pl
from jax.experimental.pallas import tpu as pltpu

F32 = jnp.float32
BF16 = jnp.bfloat16

N_DEV = 8
D_MODEL = 1024
HEAD_DIM = 64
A_GROUPS = ((128, 1), (512, 4), (2048, 16))
B_WINDOW = 128
M_HEADS = 4
M_HEAD_DIM = 128
MEM_LEN = 256
D_FF = 2816
ROPE_THETA = 500000.0
ROPE_DIMS = 16
BLOCK = 128
EPS = 1e-6
LANES = 128
BAND_Q_BLOCKS = 4
BAND_UNITS = 2

ADAM_LR = 0.001
ADAM_B1 = 0.9
ADAM_B2 = 0.999
ADAM_EPS = 1e-08
ADAM_WD = 0.01
ADAM_STEP = 10

VMEM_LIMIT_BYTES = 56 * 1024 * 1024
MESH = pl.DeviceIdType.MESH

WEIGHTS = ['attn_norm', 'w_in', 'a_q_norm', 'a_k_norm', 'b_q_norm', 'b_k_norm', 'b_sinks', 'mem_norm',
           'w_mem_kv', 'm_q_norm', 'm_k_norm', 'w_o_a', 'w_o_b', 'w_o_m', 'w_gate', 'b_gate', 'w_out',
           'ffn_norm', 'w_up', 'conv_w', 'conv_b', 'w_down']
BIG = {'w_in': 1, 'w_mem_kv': 0, 'w_o_a': 1, 'w_o_b': 1, 'w_o_m': 1, 'w_gate': 1, 'w_out': 0, 'w_up': 1,
       'conv_w': 1, 'w_down': 0}
SMALL = [n for n in WEIGHTS if n not in BIG]


def _cparams(n_grid):
    return pltpu.CompilerParams(dimension_semantics=("arbitrary",) * n_grid, vmem_limit_bytes=VMEM_LIMIT_BYTES)


def _seg_matrix(width):
    shift = width.bit_length() - 1
    r = lax.shift_right_logical(lax.broadcasted_iota(jnp.int32, (LANES, LANES), 0), shift)
    c = lax.shift_right_logical(lax.broadcasted_iota(jnp.int32, (LANES, LANES), 1), shift)
    return jnp.where(r == c, 1.0, 0.0).astype(BF16)


def _seg_sum(x, seg):
    hi = x.astype(BF16)
    r1 = x - hi.astype(F32)
    mid = r1.astype(BF16)
    lo = (r1 - mid.astype(F32)).astype(BF16)
    dot = functools.partial(jnp.dot, preferred_element_type=F32)
    return dot(hi, seg) + dot(mid, seg) + dot(lo, seg)


def _rope(y, c, s1, s2):
    return y * c + pltpu.roll(y, LANES - ROPE_DIMS // 2, 1) * s1 + pltpu.roll(y, ROPE_DIMS // 2, 1) * s2


def _unrope(dy, c, s1, s2):
    return dy * c + pltpu.roll(dy * s1, ROPE_DIMS // 2, 1) + pltpu.roll(dy * s2, LANES - ROPE_DIMS // 2, 1)


def _sigmoid(x):
    return 1.0 / (1.0 + jnp.exp(-x))


def _rms_fwd(x, gain, name):
    s_len, d = x.shape
    tm = 512

    def body(x_ref, g_ref, h_ref, ht_ref, r_ref):
        xv = x_ref[...]
        r = lax.rsqrt(jnp.mean(xv * xv, axis=-1, keepdims=True) + EPS)
        h = ((xv * r) * g_ref[...]).astype(BF16)
        h_ref[...] = h
        ht_ref[...] = h.T
        r_ref[...] = r

    return pl.pallas_call(
        body, name=name, grid=(s_len // tm,),
        in_specs=[pl.BlockSpec((tm, d), lambda i: (i, 0)), pl.BlockSpec((1, d), lambda i: (0, 0))],
        out_specs=(pl.BlockSpec((tm, d), lambda i: (i, 0)), pl.BlockSpec((d, tm), lambda i: (0, i)),
                   pl.BlockSpec((tm, 1), lambda i: (i, 0))),
        out_shape=(jax.ShapeDtypeStruct((s_len, d), BF16), jax.ShapeDtypeStruct((d, s_len), BF16),
                   jax.ShapeDtypeStruct((s_len, 1), F32)),
        compiler_params=_cparams(1),
    )(x, gain)


def _resident(shape, index_map):
    return pl.BlockSpec(shape, index_map, pipeline_mode=pl.Buffered(1))


def _mm_rows(pairs, name, nt=False, tm=512, bias=None, sigmoid=False, res=None, out_dtypes=(F32,), loss_target=None,
             rms_bwd=None):
    m = pairs[0][0].shape[0]
    n = pairs[0][1].shape[0] if nt else pairs[0][1].shape[1]
    n_pairs = len(pairs)
    has_bias, has_res, has_loss = bias is not None, res is not None, loss_target is not None
    has_rms = rms_bwd is not None
    dims = (((1,), (1,)), ((), ())) if nt else (((1,), (0,)), ((), ()))

    def body(*refs):
        acc = None
        for p in range(n_pairs):
            t = lax.dot_general(refs[2 * p][...].astype(BF16), refs[2 * p + 1][...], dims, preferred_element_type=F32)
            acc = t if acc is None else acc + t
        pos = 2 * n_pairs
        if has_bias:
            acc = acc + refs[pos][...]
            pos += 1
        if sigmoid:
            acc = _sigmoid(acc)
        if has_res:
            acc = refs[pos][...] + acc
            pos += 1
        if has_loss:
            dy_ref, dyb_ref, da_ref, l_ref = refs[pos + 1:]

            @pl.when(pl.program_id(0) == 0)
            def _():
                l_ref[...] = jnp.zeros_like(l_ref)
            err = acc - refs[pos][...]
            dy = err * (1.0 / n)
            dy_ref[...] = dy
            dyb_ref[...] = dy.astype(BF16)
            da_ref[...] = lax.dot_general(dy.astype(BF16), refs[1][...], (((1,), (1,)), ((), ())),
                                          preferred_element_type=F32).astype(BF16)
            part = 0.5 * jnp.sum(jnp.mean(err * err, axis=-1, keepdims=True), axis=0, keepdims=True)
            l_ref[...] += jnp.broadcast_to(part, l_ref.shape)
            return
        if has_rms:
            x_ref, r_ref, g_ref, add_ref = refs[pos:pos + 4]
            dg_ref = refs[-1]

            @pl.when(pl.program_id(0) == 0)
            def _():
                dg_ref[...] = jnp.zeros_like(dg_ref)
            rv = r_ref[...]
            xhat = x_ref[...] * rv
            dg_ref[...] += jnp.sum(acc * xhat, axis=0, keepdims=True)
            dxhat = acc * g_ref[...]
            acc = add_ref[...] + rv * (dxhat - xhat * jnp.mean(dxhat * xhat, axis=-1, keepdims=True))
            for o_ref in refs[pos + 4:-1]:
                o_ref[...] = acc.astype(o_ref.dtype)
            return
        for o_ref in refs[pos:]:
            o_ref[...] = acc.astype(o_ref.dtype)

    in_specs, args = [], []
    for a, w, blk in pairs:
        k = a.shape[1]
        in_specs.append(pl.BlockSpec((tm, k), lambda i: (i, 0)))
        if nt:
            in_specs.append(_resident((n, k), lambda i, blk=blk: (0, blk)))
        else:
            in_specs.append(_resident((k, n), lambda i, blk=blk: (blk, 0)))
        args += [a, w]
    if has_bias:
        in_specs.append(_resident((1, n), lambda i: (0, 0)))
        args.append(bias)
    if has_res:
        in_specs.append(pl.BlockSpec((tm, n), lambda i: (i, 0)))
        args.append(res)
    out = pl.BlockSpec((tm, n), lambda i: (i, 0))
    if has_loss:
        k0 = pairs[0][0].shape[1]
        return pl.pallas_call(
            body, name=name, grid=(m // tm,), in_specs=in_specs + [out],
            out_specs=(out, out, pl.BlockSpec((tm, k0), lambda i: (i, 0)), pl.BlockSpec((8, LANES), lambda i: (0, 0))),
            out_shape=(jax.ShapeDtypeStruct((m, n), F32), jax.ShapeDtypeStruct((m, n), BF16),
                       jax.ShapeDtypeStruct((m, k0), BF16), jax.ShapeDtypeStruct((8, LANES), F32)),
            compiler_params=_cparams(1),
        )(*args, loss_target)
    if has_rms:
        x, r, gain, add = rms_bwd
        vec = _resident((1, n), lambda i: (0, 0))
        return pl.pallas_call(
            body, name=name, grid=(m // tm,),
            in_specs=in_specs + [out, pl.BlockSpec((tm, 1), lambda i: (i, 0)), vec, out],
            out_specs=tuple([out] * len(out_dtypes) + [pl.BlockSpec((1, n), lambda i: (0, 0))]),
            out_shape=tuple([jax.ShapeDtypeStruct((m, n), dt) for dt in out_dtypes] + [jax.ShapeDtypeStruct((1, n), F32)]),
            compiler_params=_cparams(1),
        )(*args, x, r, gain, add)
    outs = pl.pallas_call(
        body, name=name, grid=(m // tm,), in_specs=in_specs, out_specs=tuple([out] * len(out_dtypes)),
        out_shape=tuple(jax.ShapeDtypeStruct((m, n), dt) for dt in out_dtypes), compiler_params=_cparams(1),
    )(*args)
    return outs[0] if len(out_dtypes) == 1 else outs


def _mm_rows_each(pairs, name, tm=512):
    m = pairs[0][0].shape[0]
    n_pairs = len(pairs)

    def body(*refs):
        for p in range(n_pairs):
            refs[2 * n_pairs + p][...] = lax.dot_general(refs[2 * p][...].astype(BF16), refs[2 * p + 1][...],
                                                         (((1,), (1,)), ((), ())), preferred_element_type=F32)

    in_specs, args = [], []
    for a, w in pairs:
        in_specs += [pl.BlockSpec((tm, a.shape[1]), lambda i: (i, 0)), _resident(w.shape, lambda i: (0, 0))]
        args += [a, w]
    return pl.pallas_call(
        body, name=name, grid=(m // tm,), in_specs=in_specs,
        out_specs=tuple(pl.BlockSpec((tm, w.shape[0]), lambda i: (i, 0)) for _, w in pairs),
        out_shape=tuple(jax.ShapeDtypeStruct((m, w.shape[0]), F32) for _, w in pairs), compiler_params=_cparams(1),
    )(*args)


def _mm_rows_cat(a, ws, name, tm=256):
    m, k = a.shape
    widths = [w.shape[1] for w in ws]
    n = sum(widths)

    def body(*refs):
        a_ref, o_ref = refs[0], refs[-1]
        av = a_ref[...]
        off = 0
        for p, width in enumerate(widths):
            o_ref[:, off:off + width] = jnp.dot(av, refs[1 + p][...], preferred_element_type=F32)
            off += width

    return pl.pallas_call(
        body, name=name, grid=(m // tm,),
        in_specs=[pl.BlockSpec((tm, k), lambda i: (i, 0))] + [_resident((k, wd), lambda i: (0, 0)) for wd in widths],
        out_specs=pl.BlockSpec((tm, n), lambda i: (i, 0)),
        out_shape=jax.ShapeDtypeStruct((m, n), F32), compiler_params=_cparams(1),
    )(a, *ws)


def _mm_cols(a, b, name, tn=256):
    m, k = a.shape
    n = b.shape[1]

    def body(a_ref, b_ref, o_ref):
        o_ref[...] = jnp.dot(a_ref[...], b_ref[...].astype(BF16), preferred_element_type=F32)

    return pl.pallas_call(
        body, name=name, grid=(n // tn,),
        in_specs=[_resident((m, k), lambda j: (0, 0)), pl.BlockSpec((k, tn), lambda j: (0, j))],
        out_specs=pl.BlockSpec((m, tn), lambda j: (0, j)),
        out_shape=jax.ShapeDtypeStruct((m, n), F32), compiler_params=_cparams(1),
    )(a, b)


def _mm_tn(a, b, name, tile=256):
    k, m = a.shape
    n = b.shape[1]
    dims = (((0,), (0,)), ((), ()))

    def body(a_ref, b_ref, o_ref):
        o_ref[...] = lax.dot_general(a_ref[...].astype(BF16), b_ref[...].astype(BF16), dims, preferred_element_type=F32)

    if n <= m:
        t = min(tile, m)
        grid, a_spec, b_spec = (m // t,), pl.BlockSpec((k, t), lambda i: (0, i)), _resident((k, n), lambda i: (0, 0))
        o_spec = pl.BlockSpec((t, n), lambda i: (i, 0))
    else:
        t = min(tile, n)
        grid, a_spec, b_spec = (n // t,), _resident((k, m), lambda i: (0, 0)), pl.BlockSpec((k, t), lambda i: (0, i))
        o_spec = pl.BlockSpec((m, t), lambda i: (0, i))
    return pl.pallas_call(
        body, name=name, grid=grid, in_specs=[a_spec, b_spec], out_specs=o_spec,
        out_shape=jax.ShapeDtypeStruct((m, n), F32), compiler_params=_cparams(1),
    )(a, b)


def _norm_rope(t, gain, c, s1, s2, seg):
    rs = lax.rsqrt(_seg_sum(t * t, seg) * (1.0 / HEAD_DIM) + EPS)
    return _rope((t * rs) * gain, c, s1, s2)


def _dup_half(y, half):
    lane = lax.broadcasted_iota(jnp.int32, y.shape, 1)
    rolled = pltpu.roll(y, HEAD_DIM, 1)
    keep = (lane < HEAD_DIM) if half == 0 else (lane >= HEAD_DIM)
    return jnp.where(keep, y, rolled)


def _qk_prep(proj, cb0, d, gqa, gq, gk, tabs, name):
    s_len = proj.shape[0]
    tm = 512
    rows = tm // d
    n_units = 4 if gqa else 2 * d
    n_q = 4 if gqa else 2
    n_in = 6

    def body(*refs):
        in_refs = refs[:n_in]
        gq_ref, gk_ref, c_ref, s1_ref, s2_ref, o_ref = refs[n_in:]
        seg = _seg_matrix(HEAD_DIM)

        def rows_of(ref, r):
            return ref[...] if d == 1 else ref[pl.ds(r, rows, stride=d), :]

        def put(unit_col, y):
            o_ref[:, unit_col * LANES:(unit_col + 1) * LANES] = y.astype(BF16)

        for r in range(d):
            c, s1, s2 = rows_of(c_ref, r), rows_of(s1_ref, r), rows_of(s2_ref, r)
            for b in range(n_in):
                t = rows_of(in_refs[b], r)
                if b < n_q:
                    put((b * d + r) if not gqa else b, _norm_rope(t, gq_ref[...], c, s1, s2, seg))
                elif not gqa:
                    sec, pair = (1, b - 2) if b < 4 else (2, b - 4)
                    y = _norm_rope(t, gk_ref[...], c, s1, s2, seg) if sec == 1 else t
                    put(sec * n_units + pair * d + r, y)
                else:
                    sec = 1 if b == 4 else 2
                    y = _norm_rope(t, gk_ref[...], c, s1, s2, seg) if sec == 1 else t
                    for u in range(n_units):
                        put(sec * n_units + u, _dup_half(y, u // 2))

    in_specs = [pl.BlockSpec((tm, LANES), lambda i, b=b: (i, cb0 + b)) for b in range(n_in)]
    vec = pl.BlockSpec((1, LANES), lambda i: (0, 0))
    tab = pl.BlockSpec((tm, LANES), lambda i: (i, 0))
    width = 3 * n_units * LANES
    return pl.pallas_call(
        body, name=name, grid=(s_len // tm,), in_specs=in_specs + [vec, vec, tab, tab, tab],
        out_specs=pl.BlockSpec((rows, width), lambda i: (i, 0)),
        out_shape=jax.ShapeDtypeStruct((s_len // d, width), BF16), compiler_params=_cparams(1),
    )(*([proj] * n_in), gq, gk, *tabs)


def _qk_prep_bwd(dqkv, proj, cb0, d, gqa, gq, gk, tabs, name):
    s_len = proj.shape[0]
    tm = 512
    rows = tm // d
    n_units = 4 if gqa else 2 * d
    n_q = 4 if gqa else 2
    n_in = 6

    def body(*refs):
        d_refs = refs[0:3]
        in_refs = refs[3:3 + n_in]
        gq_ref, gk_ref, c_ref, s1_ref, s2_ref, o_ref, dgq_ref, dgk_ref, stage = refs[3 + n_in:]
        seg = _seg_matrix(HEAD_DIM)

        @pl.when(pl.program_id(0) == 0)
        def _():
            dgq_ref[...] = jnp.zeros_like(dgq_ref)
            dgk_ref[...] = jnp.zeros_like(dgk_ref)

        def rows_of(ref, r):
            return ref[...] if d == 1 else ref[pl.ds(r, rows, stride=d), :]

        def unit(col):
            sec, u = divmod(col, n_units)
            return d_refs[sec][:, u * LANES:(u + 1) * LANES]

        def norm_bwd(dyr, t, gain, c, s1, s2, dg_ref):
            rs = lax.rsqrt(_seg_sum(t * t, seg) * (1.0 / HEAD_DIM) + EPS)
            that = t * rs
            dy = _unrope(dyr, c, s1, s2)
            dg_ref[...] += jnp.sum(dy * that, axis=0, keepdims=True)
            dthat = dy * gain
            return rs * (dthat - that * (_seg_sum(dthat * that, seg) * (1.0 / HEAD_DIM)))

        def fold(sec):
            tot = []
            for u in range(n_units):
                v = unit(sec * n_units + u)
                tot.append(v + pltpu.roll(v, HEAD_DIM, 1))
            lane = lax.broadcasted_iota(jnp.int32, tot[0].shape, 1)
            return jnp.where(lane < HEAD_DIM, tot[0] + tot[1], tot[2] + tot[3])

        for b in range(n_in):
            for r in range(d):
                c, s1, s2 = rows_of(c_ref, r), rows_of(s1_ref, r), rows_of(s2_ref, r)
                t = rows_of(in_refs[b], r)
                if b < n_q:
                    g = unit((b * d + r) if not gqa else b)
                    out = norm_bwd(g, t, gq_ref[...], c, s1, s2, dgq_ref)
                elif not gqa:
                    sec, pair = (1, b - 2) if b < 4 else (2, b - 4)
                    g = unit(sec * n_units + pair * d + r)
                    out = norm_bwd(g, t, gk_ref[...], c, s1, s2, dgk_ref) if sec == 1 else g
                else:
                    sec = 1 if b == 4 else 2
                    g = fold(sec)
                    out = norm_bwd(g, t, gk_ref[...], c, s1, s2, dgk_ref) if sec == 1 else g
                if d == 1:
                    o_ref[:, b * LANES:(b + 1) * LANES] = out.astype(BF16)
                else:
                    stage[pl.ds(r, rows, stride=d), :] = out
            if d != 1:
                o_ref[:, b * LANES:(b + 1) * LANES] = stage[...].astype(BF16)

    in_specs = [pl.BlockSpec((rows, n_units * LANES), lambda i: (i, 0))] * 3
    in_specs += [pl.BlockSpec((tm, LANES), lambda i, b=b: (i, cb0 + b)) for b in range(n_in)]
    vec = pl.BlockSpec((1, LANES), lambda i: (0, 0))
    tab = pl.BlockSpec((tm, LANES), lambda i: (i, 0))
    return pl.pallas_call(
        body, name=name, grid=(s_len // tm,), in_specs=in_specs + [vec, vec, tab, tab, tab],
        out_specs=(pl.BlockSpec((tm, n_in * LANES), lambda i: (i, 0)), vec, vec),
        out_shape=(jax.ShapeDtypeStruct((s_len, n_in * LANES), BF16), jax.ShapeDtypeStruct((1, LANES), F32),
                   jax.ShapeDtypeStruct((1, LANES), F32)),
        scratch_shapes=[pltpu.VMEM((tm, LANES), F32)], compiler_params=_cparams(1),
    )(*dqkv, *([proj] * n_in), gq, gk, *tabs)


def _head_masks(shape):
    lane = lax.broadcasted_iota(jnp.int32, shape, 1)
    return lane < HEAD_DIM, lane >= HEAD_DIM


def _band_fwd(qkv, n_units, max_dist, sinks, name):
    n_rows = qkv.shape[0]
    nb = n_rows // BLOCK
    scale = HEAD_DIM ** -0.5
    has_sink = sinks is not None
    assert not has_sink or max_dist < BLOCK

    qn, un = min(nb, BAND_Q_BLOCKS), BAND_UNITS
    ug = n_units // un

    def body(*refs):
        q_ref, kp_ref, km_ref, vp_ref, vm_ref = refs[:5]
        o_ref, lse_ref = refs[-2:]
        i = pl.program_id(1)
        qi = lax.broadcasted_iota(jnp.int32, (BLOCK, 2 * BLOCK), 0)
        kj = lax.broadcasted_iota(jnp.int32, (BLOCK, 2 * BLOCK), 1)
        dist = qi + BLOCK - kj
        band = (dist >= 0) & (dist <= max_dist)
        band_first = band & ((i > 0) | (kj >= BLOCK))
        m0, m1 = _head_masks((BLOCK, LANES))
        zero = jnp.zeros((BLOCK, LANES), BF16)
        for ub in range(un):
            cs = slice(ub * LANES, (ub + 1) * LANES)
            for qb in range(qn):
                rs = slice(qb * BLOCK, (qb + 1) * BLOCK)
                q = q_ref[rs, cs]
                if qb == 0:
                    kk = jnp.concatenate([kp_ref[:, cs], km_ref[0:BLOCK, cs]], axis=0)
                    vv = jnp.concatenate([vp_ref[:, cs], vm_ref[0:BLOCK, cs]], axis=0)
                    valid = band_first
                else:
                    kk = km_ref[(qb - 1) * BLOCK:(qb + 1) * BLOCK, cs]
                    vv = vm_ref[(qb - 1) * BLOCK:(qb + 1) * BLOCK, cs]
                    valid = band
                outs, lses = [], []
                for e, hm in enumerate((m0, m1)):
                    qe = jnp.where(hm, q, zero)
                    s = lax.dot_general(qe, kk, (((1,), (1,)), ((), ())), preferred_element_type=F32) * scale
                    s = jnp.where(valid, s, -jnp.inf)
                    if has_sink:
                        s = jnp.where(kj == 0, refs[5][ub][:, e * HEAD_DIM:e * HEAD_DIM + 1], s)
                    mx = jnp.max(s, axis=-1, keepdims=True)
                    p = jnp.exp(s - mx)
                    den = jnp.sum(p, axis=-1, keepdims=True)
                    pn = p * (1.0 / den)
                    if has_sink:
                        pn = jnp.where(kj == 0, 0.0, pn)
                    pn = pn.astype(BF16)
                    outs.append(jnp.dot(pn, vv, preferred_element_type=F32))
                    lses.append(mx + jnp.log(den))
                o_ref[rs, cs] = jnp.where(m0, outs[0], outs[1])
                lse_ref[rs, cs] = jnp.where(m0, jnp.broadcast_to(lses[0], (BLOCK, LANES)),
                                            jnp.broadcast_to(lses[1], (BLOCK, LANES)))

    def main(sec):
        return pl.BlockSpec((qn * BLOCK, un * LANES), lambda u, i: (i, sec * ug + u))

    def prev(sec):
        return pl.BlockSpec((BLOCK, un * LANES), lambda u, i: (jnp.maximum(i * qn - 1, 0), sec * ug + u))

    in_specs = [main(0), prev(1), main(1), prev(2), main(2)]
    args = [qkv] * 5
    if has_sink:
        in_specs.append(pl.BlockSpec((un, 1, LANES), lambda u, i: (u, 0, 0)))
        args.append(sinks)
    return pl.pallas_call(
        body, name=name, grid=(ug, nb // qn), in_specs=in_specs, out_specs=(main(0), main(0)),
        out_shape=(jax.ShapeDtypeStruct((n_rows, n_units * LANES), F32),) * 2, compiler_params=_cparams(2),
    )(*args)


def _band_bwd(qkv, do, lse, delta, n_units, max_dist, name):
    n_rows = qkv.shape[0]
    nb = n_rows // BLOCK
    scale = HEAD_DIM ** -0.5

    qn, un = min(nb, BAND_Q_BLOCKS), BAND_UNITS
    ug = n_units // un
    steps = nb // qn
    nt_dims = (((1,), (1,)), ((), ()))
    tn_dims = (((0,), (0,)), ((), ()))

    def body(qm_ref, qx_ref, kp_ref, km_ref, vp_ref, vm_ref, dom_ref, dox_ref, lm_ref, lx_ref, dm_ref, dx_ref,
             dq_ref, dk_ref, dv_ref):
        i = pl.program_id(1)
        m0, m1 = _head_masks((BLOCK, LANES))
        zero = jnp.zeros((BLOCK, LANES), BF16)
        qi = lax.broadcasted_iota(jnp.int32, (BLOCK, 2 * BLOCK), 0)
        kj = lax.broadcasted_iota(jnp.int32, (BLOCK, 2 * BLOCK), 1)
        dist = qi + BLOCK - kj
        band = (dist >= 0) & (dist <= max_dist)
        band_first = band & ((i > 0) | (kj >= BLOCK))
        qr = lax.broadcasted_iota(jnp.int32, (BLOCK, BLOCK), 0)
        kc = lax.broadcasted_iota(jnp.int32, (BLOCK, BLOCK), 1)
        dist_x = qr + BLOCK - kc
        band_next = (dist_x >= 0) & (dist_x <= max_dist) & (i < steps - 1)

        def pair(q, dob, lse_b, del_b, kk, vv, valid):
            dqs, dk, dv = [], None, None
            for e, hm in enumerate((m0, m1)):
                col = slice(e * HEAD_DIM, e * HEAD_DIM + 1)
                qe = jnp.where(hm, q, zero)
                doe = jnp.where(hm, dob, zero)
                s = lax.dot_general(qe, kk, nt_dims, preferred_element_type=F32) * scale
                p = jnp.where(valid, jnp.exp(s - lse_b[:, col]), 0.0)
                dp = lax.dot_general(doe, vv, nt_dims, preferred_element_type=F32)
                ds = (p * (dp - del_b[:, col]) * scale).astype(BF16)
                dqs.append(jnp.dot(ds, kk, preferred_element_type=F32))
                dk_e = lax.dot_general(ds, qe, tn_dims, preferred_element_type=F32)
                dv_e = lax.dot_general(p.astype(BF16), doe, tn_dims, preferred_element_type=F32)
                dk = dk_e if dk is None else dk + dk_e
                dv = dv_e if dv is None else dv + dv_e
            return jnp.where(m0, dqs[0], dqs[1]), dk, dv

        for ub in range(un):
            cs = slice(ub * LANES, (ub + 1) * LANES)
            dk_acc, dv_acc = [None] * qn, [None] * qn

            def add(acc, kb, part):
                acc[kb] = part if acc[kb] is None else acc[kb] + part

            for qb in range(qn):
                rs = slice(qb * BLOCK, (qb + 1) * BLOCK)
                if qb == 0:
                    kk = jnp.concatenate([kp_ref[:, cs], km_ref[0:BLOCK, cs]], axis=0)
                    vv = jnp.concatenate([vp_ref[:, cs], vm_ref[0:BLOCK, cs]], axis=0)
                    valid = band_first
                else:
                    kk = km_ref[(qb - 1) * BLOCK:(qb + 1) * BLOCK, cs]
                    vv = vm_ref[(qb - 1) * BLOCK:(qb + 1) * BLOCK, cs]
                    valid = band
                dq, dk, dv = pair(qm_ref[rs, cs], dom_ref[rs, cs], lm_ref[rs, cs], dm_ref[rs, cs], kk, vv, valid)
                dq_ref[rs, cs] = dq
                if qb > 0:
                    add(dk_acc, qb - 1, dk[0:BLOCK])
                    add(dv_acc, qb - 1, dv[0:BLOCK])
                add(dk_acc, qb, dk[BLOCK:2 * BLOCK])
                add(dv_acc, qb, dv[BLOCK:2 * BLOCK])
            last = slice((qn - 1) * BLOCK, qn * BLOCK)
            _, dk, dv = pair(qx_ref[:, cs], dox_ref[:, cs], lx_ref[:, cs], dx_ref[:, cs], km_ref[last, cs], vm_ref[last, cs],
                             band_next)
            add(dk_acc, qn - 1, dk)
            add(dv_acc, qn - 1, dv)
            for kb in range(qn):
                dk_ref[kb * BLOCK:(kb + 1) * BLOCK, cs] = dk_acc[kb]
                dv_ref[kb * BLOCK:(kb + 1) * BLOCK, cs] = dv_acc[kb]

    def main(sec):
        return pl.BlockSpec((qn * BLOCK, un * LANES), lambda u, i: (i, sec * ug + u))

    def prev(sec):
        return pl.BlockSpec((BLOCK, un * LANES), lambda u, i: (jnp.maximum(i * qn - 1, 0), sec * ug + u))

    def nxt(sec):
        return pl.BlockSpec((BLOCK, un * LANES), lambda u, i: (jnp.minimum((i + 1) * qn, nb - 1), sec * ug + u))

    in_specs = [main(0), nxt(0), prev(1), main(1), prev(2), main(2),
                main(0), nxt(0), main(0), nxt(0), main(0), nxt(0)]
    args = [qkv] * 6 + [do, do, lse, lse, delta, delta]
    shp = jax.ShapeDtypeStruct((n_rows, n_units * LANES), F32)
    return pl.pallas_call(
        body, name=name, grid=(ug, steps), in_specs=in_specs, out_specs=(main(0), main(0), main(0)),
        out_shape=(shp, shp, shp), compiler_params=_cparams(2),
    )(*args)


def _merge_groups(os_, lses, dils, name):
    s_len = os_[0].shape[0] * dils[0]
    tm = 512

    def body(*refs):
        o_refs, l_refs = refs[0:3], refs[3:6]
        o_ref, lse_ref = refs[6:8]
        so, sl = refs[8:11], refs[11:14]
        for pair in range(2):
            for g, d in enumerate(dils):
                rows = tm // d
                for r in range(d):
                    col = slice((pair * d + r) * LANES, (pair * d + r + 1) * LANES)
                    if d == 1:
                        so[g][...] = o_refs[g][:, col]
                        sl[g][...] = l_refs[g][:, col]
                    else:
                        so[g][pl.ds(r, rows, stride=d), :] = o_refs[g][:, col]
                        sl[g][pl.ds(r, rows, stride=d), :] = l_refs[g][:, col]
            l0, l1, l2 = sl[0][...], sl[1][...], sl[2][...]
            mx = jnp.maximum(jnp.maximum(l0, l1), l2)
            e0, e1, e2 = jnp.exp(l0 - mx), jnp.exp(l1 - mx), jnp.exp(l2 - mx)
            den = e0 + e1 + e2
            inv = 1.0 / den
            o_ref[:, pair * LANES:(pair + 1) * LANES] = (so[0][...] * (e0 * inv) + so[1][...] * (e1 * inv)
                                                         + so[2][...] * (e2 * inv))
            lse_ref[:, pair * LANES:(pair + 1) * LANES] = mx + jnp.log(den)

    in_specs = [pl.BlockSpec((tm // d, 2 * d * LANES), lambda i: (i, 0)) for d in dils] * 2
    out = pl.BlockSpec((tm, 2 * LANES), lambda i: (i, 0))
    shp = jax.ShapeDtypeStruct((s_len, 2 * LANES), F32)
    return pl.pallas_call(
        body, name=name, grid=(s_len // tm,), in_specs=in_specs, out_specs=(out, out), out_shape=(shp, shp),
        scratch_shapes=[pltpu.VMEM((tm, LANES), F32)] * 6, compiler_params=_cparams(1),
    )(*os_, *lses)


def _bwd_prep(do, o, lse, dils, sinks, name):
    s_len, width = do.shape
    n_pairs = width // LANES
    tm = 512
    has_sink = sinks is not None
    n_g = len(dils)

    def body(*refs):
        do_ref, o_ref, lse_ref = refs[:3]
        pos = 3
        if has_sink:
            sink_ref = refs[pos]
            pos += 1
        outs = refs[pos:pos + 3 * n_g]
        pos += 3 * n_g
        if has_sink:
            dsink_ref = refs[pos]
            pos += 1
        s_do, s_l, s_d = refs[pos:pos + 3]
        seg = _seg_matrix(HEAD_DIM)

        if has_sink:
            @pl.when(pl.program_id(0) == 0)
            def _():
                dsink_ref[...] = jnp.zeros_like(dsink_ref)

        for pair in range(n_pairs):
            col = slice(pair * LANES, (pair + 1) * LANES)
            dov = do_ref[:, col]
            lv = lse_ref[:, col]
            delta = _seg_sum(dov * o_ref[:, col], seg)
            if has_sink:
                dsink_ref[pair] += -jnp.sum(jnp.exp(sink_ref[pair] - lv) * delta, axis=0, keepdims=True)
            s_do[...] = dov
            s_l[...] = lv
            s_d[...] = delta
            for g, d in enumerate(dils):
                rows = tm // d
                for r in range(d):
                    oc = slice((pair * d + r) * LANES, (pair * d + r + 1) * LANES)
                    if d == 1:
                        a, b, c = s_do[...], s_l[...], s_d[...]
                    else:
                        a = s_do[pl.ds(r, rows, stride=d), :]
                        b = s_l[pl.ds(r, rows, stride=d), :]
                        c = s_d[pl.ds(r, rows, stride=d), :]
                    outs[3 * g][:, oc] = a.astype(BF16)
                    outs[3 * g + 1][:, oc] = b
                    outs[3 * g + 2][:, oc] = c

    row = pl.BlockSpec((tm, width), lambda i: (i, 0))
    in_specs = [row, row, row]
    args = [do, o, lse]
    if has_sink:
        in_specs.append(pl.BlockSpec((n_pairs, 1, LANES), lambda i: (0, 0, 0)))
        args.append(sinks)
    out_specs, out_shape = [], []
    for d in dils:
        for dt in (BF16, F32, F32):
            out_specs.append(pl.BlockSpec((tm // d, n_pairs * d * LANES), lambda i: (i, 0)))
            out_shape.append(jax.ShapeDtypeStruct((s_len // d, n_pairs * d * LANES), dt))
    if has_sink:
        out_specs.append(pl.BlockSpec((n_pairs, 1, LANES), lambda i: (0, 0, 0)))
        out_shape.append(jax.ShapeDtypeStruct((n_pairs, 1, LANES), F32))
    return pl.pallas_call(
        body, name=name, grid=(s_len // tm,), in_specs=in_specs, out_specs=tuple(out_specs),
        out_shape=tuple(out_shape), scratch_shapes=[pltpu.VMEM((tm, LANES), F32)] * 3, compiler_params=_cparams(1),
    )(*args)


def _mem_kv(mem, mem_gain, w_kv, k_gain, name):
    m_len = mem.shape[0]
    kw = M_HEADS * M_HEAD_DIM

    def body(mem_ref, mg_ref, w_ref, kg_ref, k_ref, v_ref):
        mv = mem_ref[...]
        r = lax.rsqrt(jnp.mean(mv * mv, axis=-1, keepdims=True) + EPS)
        mn = ((mv * r) * mg_ref[...]).astype(BF16)
        kv = jnp.dot(mn, w_ref[...], preferred_element_type=F32)
        for h in range(M_HEADS):
            col = slice(h * M_HEAD_DIM, (h + 1) * M_HEAD_DIM)
            t = kv[:, col]
            rk = lax.rsqrt(jnp.mean(t * t, axis=-1, keepdims=True) + EPS)
            k_ref[:, col] = ((t * rk) * kg_ref[...]).astype(BF16)
        v_ref[...] = kv[:, kw:].astype(BF16)

    shp = jax.ShapeDtypeStruct((m_len, kw), BF16)
    return pl.pallas_call(body, name=name, out_shape=(shp, shp),
                          compiler_params=pltpu.CompilerParams(vmem_limit_bytes=VMEM_LIMIT_BYTES))(mem, mem_gain, w_kv, k_gain)


def _mem_kv_bwd(mem, mem_gain, w_kv, k_gain, dk, dv, name):
    m_len, d = mem.shape
    kw = M_HEADS * M_HEAD_DIM

    def body(mem_ref, mg_ref, w_ref, kg_ref, dk_ref, dv_ref, dw_ref, dmg_ref, dkg_ref, dkv_ref):
        mv = mem_ref[...]
        r = lax.rsqrt(jnp.mean(mv * mv, axis=-1, keepdims=True) + EPS)
        mhat = mv * r
        mn = (mhat * mg_ref[...]).astype(BF16)
        kv = jnp.dot(mn, w_ref[...], preferred_element_type=F32)
        dkg = jnp.zeros((1, M_HEAD_DIM), F32)
        for h in range(M_HEADS):
            col = slice(h * M_HEAD_DIM, (h + 1) * M_HEAD_DIM)
            t = kv[:, col]
            rk = lax.rsqrt(jnp.mean(t * t, axis=-1, keepdims=True) + EPS)
            that = t * rk
            dy = dk_ref[:, col]
            dkg = dkg + jnp.sum(dy * that, axis=0, keepdims=True)
            dthat = dy * kg_ref[...]
            dkv_ref[:, col] = (rk * (dthat - that * jnp.mean(dthat * that, axis=-1, keepdims=True))).astype(BF16)
        dkv_ref[:, kw:] = dv_ref[...].astype(BF16)
        dkg_ref[...] = dkg
        dkv = dkv_ref[...]
        dw_ref[...] = lax.dot_general(mn, dkv, (((0,), (0,)), ((), ())), preferred_element_type=F32)
        dmn = lax.dot_general(dkv, w_ref[...], (((1,), (1,)), ((), ())), preferred_element_type=F32)
        dmg_ref[...] = jnp.sum(dmn * mhat, axis=0, keepdims=True)

    return pl.pallas_call(
        body, name=name,
        out_shape=(jax.ShapeDtypeStruct((d, 2 * kw), F32), jax.ShapeDtypeStruct((1, d), F32),
                   jax.ShapeDtypeStruct((1, M_HEAD_DIM), F32)),
        scratch_shapes=[pltpu.VMEM((m_len, 2 * kw), BF16)],
        compiler_params=pltpu.CompilerParams(vmem_limit_bytes=VMEM_LIMIT_BYTES),
    )(mem, mem_gain, w_kv, k_gain, dk, dv)


def _mem_attn_fwd(proj, cidx, mk, mv, q_gain, name):
    s_len = proj.shape[0]
    kw = M_HEADS * M_HEAD_DIM
    tm = 512
    scale = M_HEAD_DIM ** -0.5

    def body(q_ref, k_ref, v_ref, g_ref, o_ref):
        for h in range(M_HEADS):
            col = slice(h * M_HEAD_DIM, (h + 1) * M_HEAD_DIM)
            t = q_ref[:, col]
            rs = lax.rsqrt(jnp.mean(t * t, axis=-1, keepdims=True) + EPS)
            qn = ((t * rs) * g_ref[...]).astype(BF16)
            s = lax.dot_general(qn, k_ref[:, col], (((1,), (1,)), ((), ())), preferred_element_type=F32) * scale
            mx = jnp.max(s, axis=-1, keepdims=True)
            p = jnp.exp(s - mx)
            pn = (p * (1.0 / jnp.sum(p, axis=-1, keepdims=True))).astype(BF16)
            o_ref[:, col] = jnp.dot(pn, v_ref[:, col], preferred_element_type=F32).astype(BF16)

    whole = pl.BlockSpec((MEM_LEN, kw), lambda i: (0, 0))
    return pl.pallas_call(
        body, name=name, grid=(s_len // tm,),
        in_specs=[pl.BlockSpec((tm, kw), lambda i: (i, cidx)), whole, whole, pl.BlockSpec((1, M_HEAD_DIM), lambda i: (0, 0))],
        out_specs=pl.BlockSpec((tm, kw), lambda i: (i, 0)),
        out_shape=jax.ShapeDtypeStruct((s_len, kw), BF16), compiler_params=_cparams(1),
    )(proj, mk, mv, q_gain)


def _mem_attn_bwd(proj, cidx, mk, mv, q_gain, do, name):
    s_len = proj.shape[0]
    kw = M_HEADS * M_HEAD_DIM
    tm = 512
    scale = M_HEAD_DIM ** -0.5

    def body(q_ref, k_ref, v_ref, g_ref, do_ref, dq_ref, dk_ref, dv_ref, dg_ref):
        @pl.when(pl.program_id(0) == 0)
        def _():
            dk_ref[...] = jnp.zeros_like(dk_ref)
            dv_ref[...] = jnp.zeros_like(dv_ref)
            dg_ref[...] = jnp.zeros_like(dg_ref)

        for h in range(M_HEADS):
            col = slice(h * M_HEAD_DIM, (h + 1) * M_HEAD_DIM)
            t = q_ref[:, col]
            rs = lax.rsqrt(jnp.mean(t * t, axis=-1, keepdims=True) + EPS)
            that = t * rs
            qn = (that * g_ref[...]).astype(BF16)
            kh, vh = k_ref[:, col], v_ref[:, col]
            dob = do_ref[:, col].astype(BF16)
            s = lax.dot_general(qn, kh, (((1,), (1,)), ((), ())), preferred_element_type=F32) * scale
            mx = jnp.max(s, axis=-1, keepdims=True)
            p = jnp.exp(s - mx)
            p = p * (1.0 / jnp.sum(p, axis=-1, keepdims=True))
            dp = lax.dot_general(dob, vh, (((1,), (1,)), ((), ())), preferred_element_type=F32)
            ds = (p * (dp - jnp.sum(p * dp, axis=-1, keepdims=True)) * scale).astype(BF16)
            dqn = jnp.dot(ds, kh, preferred_element_type=F32)
            dk_ref[:, col] += lax.dot_general(ds, qn, (((0,), (0,)), ((), ())), preferred_element_type=F32)
            dv_ref[:, col] += lax.dot_general(p.astype(BF16), dob, (((0,), (0,)), ((), ())), preferred_element_type=F32)
            dg_ref[...] += jnp.sum(dqn * that, axis=0, keepdims=True)
            dthat = dqn * g_ref[...]
            dq_ref[:, col] = (rs * (dthat - that * jnp.mean(dthat * that, axis=-1, keepdims=True))).astype(BF16)

    whole = pl.BlockSpec((MEM_LEN, kw), lambda i: (0, 0))
    vec = pl.BlockSpec((1, M_HEAD_DIM), lambda i: (0, 0))
    row = pl.BlockSpec((tm, kw), lambda i: (i, 0))
    return pl.pallas_call(
        body, name=name, grid=(s_len // tm,),
        in_specs=[pl.BlockSpec((tm, kw), lambda i: (i, cidx)), whole, whole, vec, row],
        out_specs=(row, whole, whole, vec),
        out_shape=(jax.ShapeDtypeStruct((s_len, kw), BF16), jax.ShapeDtypeStruct((MEM_LEN, kw), F32),
                   jax.ShapeDtypeStruct((MEM_LEN, kw), F32), jax.ShapeDtypeStruct((1, M_HEAD_DIM), F32)),
        compiler_params=_cparams(1),
    )(proj, mk, mv, q_gain, do)


def _project_merge(outs, w_outs, gates, w_out, x, name):
    s_len = gates.shape[0]
    d = w_outs[0].shape[1]
    tm = 512

    def body(oa_ref, ob_ref, om_ref, wa_ref, wb_ref, wm_ref, g_ref, wo_ref, x_ref,
             pa_ref, pb_ref, pm_ref, merged_ref, x1_ref):
        merged = None
        for k, (o_ref, w_ref, p_ref) in enumerate(((oa_ref, wa_ref, pa_ref), (ob_ref, wb_ref, pb_ref), (om_ref, wm_ref, pm_ref))):
            p = jnp.dot(o_ref[...].astype(BF16), w_ref[...], preferred_element_type=F32).astype(BF16)
            p_ref[...] = p
            t = g_ref[:, k * d:(k + 1) * d].astype(F32) * p.astype(F32)
            merged = t if merged is None else merged + t
        merged = merged.astype(BF16)
        merged_ref[...] = merged
        x1_ref[...] = x_ref[...] + jnp.dot(merged, wo_ref[...], preferred_element_type=F32)

    row = pl.BlockSpec((tm, d), lambda i: (i, 0))
    shp = jax.ShapeDtypeStruct((s_len, d), BF16)
    in_specs = [pl.BlockSpec((tm, o.shape[1]), lambda i: (i, 0)) for o in outs]
    in_specs += [_resident(w.shape, lambda i: (0, 0)) for w in w_outs]
    in_specs += [pl.BlockSpec((tm, 3 * d), lambda i: (i, 0)), _resident(w_out.shape, lambda i: (0, 0)), row]
    return pl.pallas_call(
        body, name=name, grid=(s_len // tm,), in_specs=in_specs, out_specs=(row, row, row, row, row),
        out_shape=(shp, shp, shp, shp, jax.ShapeDtypeStruct((s_len, d), F32)), compiler_params=_cparams(1),
    )(*outs, *w_outs, gates, w_out, x)


def _project_merge_bwd(dx1, w_out, gates, pa, pb, pm, name):
    s_len, d = pa.shape
    tm = 512

    def body(dx_ref, w_ref, g_ref, a_ref, b_ref, m_ref, da_ref, db_ref, dmm_ref, dg_ref, dbg_ref):
        @pl.when(pl.program_id(0) == 0)
        def _():
            dbg_ref[...] = jnp.zeros_like(dbg_ref)
        dm = lax.dot_general(dx_ref[...], w_ref[...], (((1,), (1,)), ((), ())), preferred_element_type=F32)
        for k, (p_ref, dp_ref) in enumerate(((a_ref, da_ref), (b_ref, db_ref), (m_ref, dmm_ref))):
            col = slice(k * d, (k + 1) * d)
            g = g_ref[:, col].astype(F32)
            dp_ref[...] = (dm * g).astype(BF16)
            dpre = (dm * p_ref[...].astype(F32)) * (g * (1.0 - g))
            dbg_ref[:, col] += jnp.sum(dpre, axis=0, keepdims=True)
            dg_ref[:, col] = dpre.astype(BF16)

    row = pl.BlockSpec((tm, d), lambda i: (i, 0))
    wide = pl.BlockSpec((tm, 3 * d), lambda i: (i, 0))
    shp = jax.ShapeDtypeStruct((s_len, d), BF16)
    return pl.pallas_call(
        body, name=name, grid=(s_len // tm,), in_specs=[row, _resident(w_out.shape, lambda i: (0, 0)), wide, row, row, row],
        out_specs=(row, row, row, wide, pl.BlockSpec((1, 3 * d), lambda i: (0, 0))),
        out_shape=(shp, shp, shp, jax.ShapeDtypeStruct((s_len, 3 * d), BF16), jax.ShapeDtypeStruct((1, 3 * d), F32)),
        compiler_params=_cparams(1),
    )(dx1, w_out, gates, pa, pb, pm)


CONV_CHUNK = 256


def _pick_row(tile, j):
    row = lax.broadcasted_iota(jnp.int32, tile.shape, 0)
    return jnp.sum(jnp.where(row == j, tile, jnp.zeros_like(tile)), axis=0, keepdims=True)


def _rows_before(ref, start, k):
    cur = ref[pl.ds(start, CONV_CHUNK), :].astype(F32)
    prev = ref[pl.ds(pl.multiple_of(jnp.maximum(start - 16, 0), 16), 16), :].astype(F32)
    prev = jnp.where(start > 0, prev, jnp.zeros_like(prev))
    rolled = pltpu.roll(cur, k, 0)
    row = lax.broadcasted_iota(jnp.int32, cur.shape, 0)
    for j in range(k):
        rolled = jnp.where(row == j, _pick_row(prev, 16 - k + j), rolled)
    return rolled


def _rows_after(ref, start, k):
    cur = ref[pl.ds(start, CONV_CHUNK), :]
    nxt = ref[pl.ds(pl.multiple_of(start + CONV_CHUNK, 8), 8), :]
    rolled = pltpu.roll(cur, CONV_CHUNK - k, 0)
    row = lax.broadcasted_iota(jnp.int32, cur.shape, 0)
    for j in range(k):
        rolled = jnp.where(row == CONV_CHUNK - k + j, _pick_row(nxt, j), rolled)
    return rolled


def _conv_pre(u_ref, w_ref, b_ref, start):
    u2 = _rows_before(u_ref, start, 2)
    u1 = _rows_before(u_ref, start, 1)
    u0 = u_ref[pl.ds(start, CONV_CHUNK), :].astype(F32)
    c = ((b_ref[...] + w_ref[0:1, :] * u2) + w_ref[1:2, :] * u1) + w_ref[2:3, :] * u0
    return c, (u2, u1, u0)


def _norm_up_conv_glu(x, gain, w_up, conv_w, conv_b, name):
    s_len, d = x.shape
    tm, tn = 512, 2 * LANES
    nblk = D_FF // tn

    def body(x_ref, g_ref, w_ref, cw_ref, cb_ref, ht_ref, r_ref, u_ref, act_ref, halo):
        @pl.when(pl.program_id(0) == 0)
        def _():
            halo[...] = jnp.zeros_like(halo)
        xv = x_ref[...]
        r = lax.rsqrt(jnp.mean(xv * xv, axis=-1, keepdims=True) + EPS)
        hv = ((xv * r) * g_ref[...]).astype(BF16)
        ht_ref[...] = hv.T
        r_ref[...] = r
        row = lax.broadcasted_iota(jnp.int32, (tm, tn), 0)
        for j in range(nblk):
            conv = []
            for half in range(2):
                cb = half * nblk + j
                cols = slice(cb * tn, (cb + 1) * tn)
                ub = jnp.dot(hv, w_ref[:, cols], preferred_element_type=F32).astype(BF16)
                u_ref[:, cols] = ub
                u0 = ub.astype(F32)
                prev = halo[cb]
                u1 = jnp.where(row == 0, _pick_row(prev, 7), pltpu.roll(u0, 1, 0))
                u2 = pltpu.roll(u0, 2, 0)
                u2 = jnp.where(row == 0, _pick_row(prev, 6), jnp.where(row == 1, _pick_row(prev, 7), u2))
                halo[cb] = u0[tm - 8:tm, :]
                conv.append(((cb_ref[:, cols] + cw_ref[0:1, cols] * u2) + cw_ref[1:2, cols] * u1)
                            + cw_ref[2:3, cols] * u0)
            act_ref[:, j * tn:(j + 1) * tn] = ((conv[0] * _sigmoid(conv[0])) * conv[1]).astype(BF16)

    return pl.pallas_call(
        body, name=name, grid=(s_len // tm,),
        in_specs=[pl.BlockSpec((tm, d), lambda i: (i, 0)), _resident((1, d), lambda i: (0, 0)),
                  _resident((d, 2 * D_FF), lambda i: (0, 0)),
                  _resident((3, 2 * D_FF), lambda i: (0, 0)), _resident((1, 2 * D_FF), lambda i: (0, 0))],
        out_specs=(pl.BlockSpec((d, tm), lambda i: (0, i)), pl.BlockSpec((tm, 1), lambda i: (i, 0)),
                   pl.BlockSpec((tm, 2 * D_FF), lambda i: (i, 0)), pl.BlockSpec((tm, D_FF), lambda i: (i, 0))),
        out_shape=(jax.ShapeDtypeStruct((d, s_len), BF16), jax.ShapeDtypeStruct((s_len, 1), F32),
                   jax.ShapeDtypeStruct((s_len, 2 * D_FF), BF16), jax.ShapeDtypeStruct((s_len, D_FF), BF16)),
        scratch_shapes=[pltpu.VMEM((2 * nblk, 8, tn), F32)], compiler_params=_cparams(1),
    )(x, gain, w_up, conv_w, conv_b)


def _conv_glu_bwd(dact, u, conv_w, conv_b, name):
    s_len = u.shape[0]
    nblk = D_FF // LANES
    n_chunks = s_len // CONV_CHUNK

    def body(da_ref, ua_ref, ug_ref, wa_ref, wg_ref, ba_ref, bg_ref,
             dua_ref, dug_ref, dwa_ref, dwg_ref, dba_ref, dbg_ref, sa, sg):
        sa[pl.ds(s_len, 8), :] = jnp.zeros((8, LANES), F32)
        sg[pl.ds(s_len, 8), :] = jnp.zeros((8, LANES), F32)
        zero = jnp.zeros((1, LANES), F32)

        def chunk1(ci, carry):
            start = pl.multiple_of(ci * CONV_CHUNK, CONV_CHUNK)
            ca, ua = _conv_pre(ua_ref, wa_ref, ba_ref, start)
            cg, ug = _conv_pre(ug_ref, wg_ref, bg_ref, start)
            dact_v = da_ref[pl.ds(start, CONV_CHUNK), :].astype(F32)
            sig = _sigmoid(ca)
            dcg = dact_v * (ca * sig)
            dca = (dact_v * cg) * (sig * (1.0 + ca * (1.0 - sig)))
            sa[pl.ds(start, CONV_CHUNK), :] = dca
            sg[pl.ds(start, CONV_CHUNK), :] = dcg
            out = [carry[0] + jnp.sum(dca, axis=0, keepdims=True), carry[1] + jnp.sum(dcg, axis=0, keepdims=True)]
            for j in range(3):
                out.append(carry[2 + j] + jnp.sum(dca * ua[j], axis=0, keepdims=True))
            for j in range(3):
                out.append(carry[5 + j] + jnp.sum(dcg * ug[j], axis=0, keepdims=True))
            return tuple(out)

        acc = lax.fori_loop(0, n_chunks, chunk1, (zero,) * 8)
        dba_ref[...] = acc[0]
        dbg_ref[...] = acc[1]
        for j in range(3):
            dwa_ref[j:j + 1, :] = acc[2 + j]
            dwg_ref[j:j + 1, :] = acc[5 + j]

        def chunk2(ci, carry):
            start = pl.multiple_of(ci * CONV_CHUNK, CONV_CHUNK)
            for s_ref, w_ref, o_ref in ((sa, wa_ref, dua_ref), (sg, wg_ref, dug_ref)):
                d0 = s_ref[pl.ds(start, CONV_CHUNK), :]
                d1 = _rows_after(s_ref, start, 1)
                d2 = _rows_after(s_ref, start, 2)
                o_ref[pl.ds(start, CONV_CHUNK), :] = (w_ref[2:3, :] * d0 + w_ref[1:2, :] * d1
                                                      + w_ref[0:1, :] * d2).astype(BF16)
            return carry
        lax.fori_loop(0, n_chunks, chunk2, 0)

    def col(rows, off):
        return pl.BlockSpec((rows, LANES), lambda j: (0, off + j))

    big = jax.ShapeDtypeStruct((s_len, D_FF), BF16)
    return pl.pallas_call(
        body, name=name, grid=(nblk,),
        in_specs=[col(s_len, 0), col(s_len, 0), col(s_len, nblk), col(3, 0), col(3, nblk), col(1, 0), col(1, nblk)],
        out_specs=(col(s_len, 0), col(s_len, 0), col(3, 0), col(3, 0), col(1, 0), col(1, 0)),
        out_shape=(big, big, jax.ShapeDtypeStruct((3, D_FF), F32), jax.ShapeDtypeStruct((3, D_FF), F32),
                   jax.ShapeDtypeStruct((1, D_FF), F32), jax.ShapeDtypeStruct((1, D_FF), F32)),
        scratch_shapes=[pltpu.VMEM((s_len + 8, LANES), F32)] * 2, compiler_params=_cparams(1),
    )(dact, u, u, conv_w, conv_w, conv_b, conv_b)


def _rope_tables(positions):
    half = ROPE_DIMS // 2
    freqs = jnp.exp(jnp.arange(half, dtype=F32) * (-2.0 * math.log(ROPE_THETA) / ROPE_DIMS))
    ang = positions.reshape(-1).astype(F32)[:, None] * freqs
    cos, sin = jnp.cos(ang), jnp.sin(ang)
    n = ang.shape[0]
    zeros = lambda w: jnp.zeros((n, w), F32)
    c = jnp.concatenate([cos, cos, jnp.ones((n, HEAD_DIM - ROPE_DIMS), F32)], axis=1)
    s1 = jnp.concatenate([-sin, zeros(HEAD_DIM - half)], axis=1)
    s2 = jnp.concatenate([zeros(half), sin, zeros(HEAD_DIM - ROPE_DIMS)], axis=1)
    return tuple(jnp.tile(t, (1, 2)) for t in (c, s1, s2))


def _two(v):
    return jnp.tile(v.reshape(1, HEAD_DIM), (1, 2))


def _fold_heads(g):
    return g[0, :HEAD_DIM] + g[0, HEAD_DIM:]


MIX_WEIGHTS = ('w_gate', 'w_mem_kv', 'w_o_a', 'w_o_b', 'w_o_m', 'w_out')
FFN_WEIGHTS = ('w_up', 'conv_w', 'w_down')


def _device_step(x, mem, positions, target, w, hooks=None):
    tabs = _rope_tables(positions)
    dils = tuple(d for _, d in A_GROUPS)
    grads = {}
    w = dict(w)

    h, h_t, r1 = _rms_fwd(x, w['attn_norm'], "rms1")
    if hooks is not None:
        w.update(hooks.weights('in', [h, *tabs]))
    proj = _mm_rows([(h, w['w_in'], 0)], "mm_in")

    qkv_a, o_g, lse_g = [], [], []
    for gi, (window, d) in enumerate(A_GROUPS):
        gq, gk = _two(w['a_q_norm'][gi]), _two(w['a_k_norm'][gi])
        qkv = _qk_prep(proj, 6 * gi, d, False, gq, gk, tabs, f"qk_prep_a{gi}")
        o, lse = _band_fwd(qkv, 2 * d, window // d, None, f"band_fwd_a{gi}")
        qkv_a.append(qkv)
        o_g.append(o)
        lse_g.append(lse)
    o_a, lse_a = _merge_groups(o_g, lse_g, dils, "merge_a")
    if hooks is not None:
        w.update(hooks.weights('mix', [o_a]))

    gbq, gbk = _two(w['b_q_norm']), _two(w['b_k_norm'])
    sinks = jnp.repeat(w['b_sinks'].reshape(4, 2), HEAD_DIM, axis=1).reshape(4, 1, LANES)
    qkv_b = _qk_prep(proj, 18, 1, True, gbq, gbk, tabs, "qk_prep_b")
    o_b, lse_b = _band_fwd(qkv_b, 4, B_WINDOW - 1, sinks, "band_fwd_b")

    gates = _mm_rows([(h, w['w_gate'], 0)], "mm_gate", bias=w['b_gate'], sigmoid=True, out_dtypes=(BF16,))
    mk, mv = _mem_kv(mem, w['mem_norm'], w['w_mem_kv'], w['m_k_norm'], "mem_kv")
    o_m = _mem_attn_fwd(proj, 6, mk, mv, w['m_q_norm'], "mem_attn")

    pa, pb, pm, merged, x1 = _project_merge((o_a, o_b, o_m), (w['w_o_a'], w['w_o_b'], w['w_o_m']), gates, w['w_out'], x,
                                            "project_merge")

    if hooks is not None:
        w.update(hooks.weights('ffn', [x1]))
    h2_t, r2, u, act = _norm_up_conv_glu(x1, w['ffn_norm'], w['w_up'], w['conv_w'], w['conv_b'], "norm_up_conv_glu")
    dy, dy_b, dact, loss = _mm_rows([(act, w['w_down'], 0)], "mm_down", res=x1, loss_target=target)

    grads['w_down'] = _mm_tn(act, dy_b, "mm_dw_down")
    du_a, du_g, dcw_a, dcw_g, dcb_a, dcb_g = _conv_glu_bwd(dact, u, w['conv_w'], w['conv_b'], "conv_glu_bwd")
    grads['conv_w'] = jnp.concatenate([dcw_a, dcw_g], axis=1)
    grads['conv_b'] = jnp.concatenate([dcb_a, dcb_g], axis=1)
    grads['w_up'] = jnp.concatenate([_mm_cols(h2_t, du_a, "mm_dw_up_a"), _mm_cols(h2_t, du_g, "mm_dw_up_g")], axis=1)
    ffn_gain = w['ffn_norm']
    if hooks is not None:
        ffn_gain = ffn_gain + hooks.grads('ffn', grads)[0:1, 0:1]
    dx1, dx1_b, grads['ffn_norm'] = _mm_rows([(du_a, w['w_up'], 0), (du_g, w['w_up'], 1)], "mm_d_h2", nt=True,
                                             rms_bwd=(x1, r2, ffn_gain, dy), out_dtypes=(F32, BF16))

    grads['w_out'] = _mm_tn(merged, dx1_b, "mm_dw_out")
    dpa, dpb, dpm, dgpre, grads['b_gate'] = _project_merge_bwd(dx1_b, w['w_out'], gates, pa, pb, pm,
                                                               "project_merge_bwd")
    do_a, do_b, do_m = _mm_rows_each([(dpa, w['w_o_a']), (dpb, w['w_o_b']), (dpm, w['w_o_m'])], "mm_d_o")
    grads['w_o_a'] = _mm_tn(o_a, dpa, "mm_dw_oa")
    grads['w_o_b'] = _mm_tn(o_b, dpb, "mm_dw_ob")
    grads['w_o_m'] = _mm_tn(o_m, dpm, "mm_dw_om")
    grads['w_gate'] = _mm_cols(h_t, dgpre, "mm_dw_gate")
    dq_m, dmk, dmv, grads['m_q_norm'] = _mem_attn_bwd(proj, 6, mk, mv, w['m_q_norm'], do_m, "mem_attn_bwd")
    grads['w_mem_kv'], grads['mem_norm'], grads['m_k_norm'] = _mem_kv_bwd(
        mem, w['mem_norm'], w['w_mem_kv'], w['m_k_norm'], dmk, dmv, "mem_kv_bwd")
    a_gain = w['a_q_norm']
    if hooks is not None:
        a_gain = a_gain + hooks.grads('mix', grads)[0:1, 0:1]

    prep = _bwd_prep(do_a, o_a, lse_a, dils, None, "bwd_prep_a")
    dproj, dgq_a, dgk_a = [], [], []
    for gi, (window, d) in enumerate(A_GROUPS):
        gq, gk = _two(a_gain[gi]), _two(w['a_k_norm'][gi])
        dqkv = _band_bwd(qkv_a[gi], prep[3 * gi], prep[3 * gi + 1], prep[3 * gi + 2], 2 * d, window // d,
                         f"band_bwd_a{gi}")
        dp, dgq, dgk = _qk_prep_bwd(dqkv, proj, 6 * gi, d, False, gq, gk, tabs, f"qk_prep_bwd_a{gi}")
        dproj.append(dp)
        dgq_a.append(_fold_heads(dgq))
        dgk_a.append(_fold_heads(dgk))
    grads['a_q_norm'] = jnp.stack(dgq_a)
    grads['a_k_norm'] = jnp.stack(dgk_a)

    do_bu, lse_bu, delta_bu, dsink = _bwd_prep(do_b, o_b, lse_b, (1,), sinks, "bwd_prep_b")
    dqkv = _band_bwd(qkv_b, do_bu, lse_bu, delta_bu, 4, B_WINDOW - 1, "band_bwd_b")
    dp_b, dgq, dgk = _qk_prep_bwd(dqkv, proj, 18, 1, True, gbq, gbk, tabs, "qk_prep_bwd_b")
    dproj.append(dp_b)
    grads['b_q_norm'] = _fold_heads(dgq)
    grads['b_k_norm'] = _fold_heads(dgk)
    grads['b_sinks'] = jnp.stack([dsink[:, 0, 0], dsink[:, 0, HEAD_DIM]], axis=1).reshape(8)

    dproj.append(dq_m)

    cols = (0, 1, 2, 3, 6)
    grads['w_in'] = _mm_rows_cat(h_t, dproj, "mm_dw_in")
    attn_gain = w['attn_norm']
    if hooks is not None:
        attn_gain = attn_gain + hooks.grads('in', grads)[0:1, 0:1]
    grad_x, grads['attn_norm'] = _mm_rows(
        [(dp, w['w_in'], c) for dp, c in zip(dproj, cols)] + [(dgpre, w['w_gate'], 0)], "mm_d_h", nt=True,
        rms_bwd=(x, r1, attn_gain, dx1))
    return loss, grad_x, grads


def _coords():
    return lax.axis_index("x"), lax.axis_index("y"), lax.axis_index("c")


def _slot(p):
    return 4 * p[0] + 2 * p[1] + p[2]


ALL_PEERS = tuple(range(1, N_DEV))
CHIP_PEERS = (1, 4, 2, 6)
OTHER_CHIPS = (4, 2, 6)


def _peers(me, masks=ALL_PEERS):
    x, y, c = me
    return [(1 - x if mask & 4 else x, 1 - y if mask & 2 else y, 1 - c if mask & 1 else c) for mask in masks]


HBM_SPEC = pl.BlockSpec(memory_space=pltpu.HBM)


SEM_SPEC = pl.BlockSpec(memory_space=pltpu.SEMAPHORE)
SIDE_EFFECT = pltpu.SideEffectType.DATAFLOW_SIDE_EFFECTING


def _exchange_start(blocks, name, gather=False, masks=ALL_PEERS, after=None):
    n = len(blocks)
    n_peers = len(masks)
    n_in = 2 * n + (0 if after is None else 1)

    def body(*refs):
        ins, lands = refs[:n], refs[n:2 * n]
        send_sems, recv_sems = refs[n_in], refs[n_in + 1]
        token = refs[-1]
        me = _coords()
        peers = _peers(me, masks)
        for a in range(n):
            for k in range(n_peers):
                pltpu.make_async_remote_copy(
                    src_ref=ins[a] if gather else ins[a].at[_slot(peers[k])], dst_ref=lands[a].at[_slot(me)],
                    send_sem=send_sems.at[a * n_peers + k], recv_sem=recv_sems.at[a * n_peers + k],
                    device_id=peers[k], device_id_type=MESH).start()
        token[...] = jnp.zeros_like(token)

    land_shapes = [((N_DEV,) + b.shape) if gather else b.shape for b in blocks]
    hbm_in = [pltpu.HBM(b.shape, b.dtype) for b in blocks]
    hbm_land = [pltpu.HBM(s, b.dtype) for s, b in zip(land_shapes, blocks)]
    sems = pltpu.SemaphoreType.DMA((n * n_peers,))
    ins = [pltpu.with_memory_space_constraint(b, pltpu.HBM) for b in blocks]
    lands = [pltpu.with_memory_space_constraint(lax.empty(s, b.dtype), pltpu.HBM) for s, b in zip(land_shapes, blocks)]
    return pl.pallas_call(
        body, name=name, out_shape=(sems, sems, *hbm_in, *hbm_land, jax.ShapeDtypeStruct((8, LANES), F32)),
        in_specs=[HBM_SPEC] * (2 * n) + ([] if after is None else [pl.BlockSpec(memory_space=pl.ANY)]),
        out_specs=(SEM_SPEC, SEM_SPEC, *([HBM_SPEC] * (2 * n)), pl.BlockSpec(memory_space=pltpu.VMEM)),
        input_output_aliases={i: 2 + i for i in range(2 * n)},
        compiler_params=pltpu.CompilerParams(has_side_effects=SIDE_EFFECT),
    )(*ins, *lands, *([] if after is None else [after]))


def _exchange_wait(started, after, name, gather=False, masks=ALL_PEERS):
    n = (len(started) - 3) // 2
    n_peers = len(masks)
    send_sems, recv_sems = started[0], started[1]
    thru = started[2:2 + 2 * n]

    def body(*refs):
        ins, lands = refs[:n], refs[n:2 * n]
        send_ref, recv_ref = refs[2 * n], refs[2 * n + 1]
        me = _coords()
        peers = _peers(me, masks)
        for a in range(n):
            for k in range(n_peers):
                cp = pltpu.make_async_remote_copy(
                    src_ref=ins[a] if gather else ins[a].at[_slot(peers[k])], dst_ref=lands[a].at[_slot(peers[k])],
                    send_sem=send_ref.at[a * n_peers + k], recv_sem=recv_ref.at[a * n_peers + k],
                    device_id=peers[k], device_id_type=MESH)
                cp.wait_send()
                cp.wait_recv()

    hbm = [pltpu.HBM(t.shape, t.dtype) for t in thru]
    res = pl.pallas_call(
        body, name=name, out_shape=tuple(hbm),
        in_specs=[HBM_SPEC] * (2 * n) + [SEM_SPEC, SEM_SPEC] + [pl.BlockSpec(memory_space=pl.ANY)] * len(after),
        out_specs=tuple([HBM_SPEC] * (2 * n)), input_output_aliases={i: i for i in range(2 * n)},
        compiler_params=pltpu.CompilerParams(has_side_effects=SIDE_EFFECT),
    )(*thru, send_sems, recv_sems, *after)
    return res[n:]


def _sibling_forward(arrays, name):
    n = len(arrays)
    n_fwd = len(OTHER_CHIPS)

    def body(*refs):
        bufs = refs[n:2 * n]
        token, send_sems, recv_sems = refs[2 * n:]
        token[...] = jnp.zeros_like(token)
        x, y, c = _coords()
        sibling = (x, y, 1 - c)
        mine = _peers((x, y, c), OTHER_CHIPS)
        theirs = _peers(sibling, OTHER_CHIPS)

        def copy(a, k, block):
            rows = bufs[a].at[_slot(block)]
            return pltpu.make_async_remote_copy(
                src_ref=rows, dst_ref=rows, send_sem=send_sems.at[a * n_fwd + k], recv_sem=recv_sems.at[a * n_fwd + k],
                device_id=sibling, device_id_type=MESH)

        sends = [copy(a, k, mine[k]) for a in range(n) for k in range(n_fwd)]
        for cp in sends:
            cp.start()
        for a in range(n):
            for k in range(n_fwd):
                copy(a, k, theirs[k]).wait_recv()
        for cp in sends:
            cp.wait_send()

    res = pl.pallas_call(
        body, name=name, in_specs=[HBM_SPEC] * n,
        out_specs=tuple([HBM_SPEC] * n + [pl.BlockSpec(memory_space=pltpu.VMEM)]),
        out_shape=tuple([jax.ShapeDtypeStruct(a.shape, a.dtype) for a in arrays] + [jax.ShapeDtypeStruct((8, LANES), F32)]),
        input_output_aliases={i: i for i in range(n)},
        scratch_shapes=[pltpu.SemaphoreType.DMA((n * n_fwd,)), pltpu.SemaphoreType.DMA((n * n_fwd,))],
    )(*arrays)
    return res[:n], res[n]


def _all_sum(p, name):
    def body(p_ref, o_ref, recv, send_sems, recv_sems):
        me = _coords()
        peers = _peers(me)
        recv[_slot(me)] = p_ref[...]

        def copy(k, landing):
            return pltpu.make_async_remote_copy(
                src_ref=p_ref, dst_ref=recv.at[_slot(landing)], send_sem=send_sems.at[k], recv_sem=recv_sems.at[k],
                device_id=peers[k], device_id_type=MESH)

        sends = [copy(k, me) for k in range(N_DEV - 1)]
        for cp in sends:
            cp.start()
        for k in range(N_DEV - 1):
            copy(k, peers[k]).wait_recv()
        for cp in sends:
            cp.wait_send()
        acc = recv[0]
        for s in range(1, N_DEV):
            acc = acc + recv[s]
        o_ref[...] = acc

    vmem = pl.BlockSpec(memory_space=pltpu.VMEM)
    return pl.pallas_call(
        body, name=name, in_specs=[vmem], out_specs=vmem, out_shape=jax.ShapeDtypeStruct(p.shape, F32),
        scratch_shapes=[pltpu.VMEM((N_DEV,) + p.shape, F32), pltpu.SemaphoreType.DMA((N_DEV - 1,)),
                        pltpu.SemaphoreType.DMA((N_DEV - 1,))],
    )(p)


def _adam(w, g, m, v):
    m2 = ADAM_B1 * m + (1.0 - ADAM_B1) * g
    v2 = ADAM_B2 * v + (1.0 - ADAM_B2) * (g * g)
    m_hat = m2 / (1.0 - ADAM_B1 ** ADAM_STEP)
    v_hat = v2 / (1.0 - ADAM_B2 ** ADAM_STEP)
    delta = -ADAM_LR * (m_hat / (jnp.sqrt(v_hat) + ADAM_EPS) + ADAM_WD * w)
    return delta, m2, v2


def _row_tile(rows, cols):
    best = rows
    for t in range(16, rows, 16):
        if rows % t == 0 and t * cols * 4 <= (1 << 20):
            best = t
    return best


def _adam_reduce(parts, w, m, v, name):
    rows, cols = w.shape
    tr = _row_tile(rows, cols)

    def body(p_ref, w_ref, m_ref, v_ref, g_ref, d_ref, m2_ref, v2_ref):
        g = p_ref[0].astype(F32)
        for s in range(1, N_DEV):
            g = g + p_ref[s].astype(F32)
        g_ref[...] = g
        d_ref[...], m2_ref[...], v2_ref[...] = _adam(w_ref[...], g, m_ref[...], v_ref[...])

    blk = pl.BlockSpec((tr, cols), lambda i: (i, 0))
    shp = jax.ShapeDtypeStruct((rows, cols), F32)
    return pl.pallas_call(
        body, name=name, grid=(rows // tr,),
        in_specs=[pl.BlockSpec((N_DEV, tr, cols), lambda i: (0, i, 0)), blk, blk, blk],
        out_specs=(blk,) * 4, out_shape=(shp,) * 4, compiler_params=_cparams(1),
    )(parts, w, m, v)


PACK_COLS = 1024
PACK = {'attn_norm': (0, 1, 1024), 'mem_norm': (1, 1, 1024), 'ffn_norm': (2, 1, 1024), 'b_gate': (3, 3, 1024),
        'conv_b': (6, 6, 1024), 'a_q_norm': (12, 3, 64), 'a_k_norm': (15, 3, 64), 'b_q_norm': (18, 1, 64),
        'b_k_norm': (19, 1, 64), 'm_q_norm': (20, 1, 128), 'm_k_norm': (21, 1, 128), 'b_sinks': (22, 1, 8)}
PACK_LOSS_ROW = 23
PACK_ROWS = 24


def _pack_pieces(name, width):
    r0, nr, lanes = PACK[name]
    out = []
    for j in range(nr):
        if lanes == PACK_COLS:
            w = min(PACK_COLS, width - j * PACK_COLS)
            out.append((r0 + j, slice(0, 1), slice(j * PACK_COLS, j * PACK_COLS + w), w))
        else:
            out.append((r0 + j, slice(j, j + 1), slice(0, lanes), lanes))
    return out


def _pack_small(grads, loss_tile, name):
    names = list(PACK)

    def body(*refs):
        o_ref = refs[-1]
        o_ref[...] = jnp.zeros_like(o_ref)
        for k, nm in enumerate(names):
            for row, rs, ls, w in _pack_pieces(nm, refs[k].shape[1]):
                o_ref[row:row + 1, 0:w] = refs[k][rs, ls]
        o_ref[PACK_LOSS_ROW:PACK_LOSS_ROW + 1, 0:1] = refs[len(names)][0:1, 0:1]

    vmem = pl.BlockSpec(memory_space=pltpu.VMEM)
    args = [grads[nm] for nm in names] + [loss_tile]
    return pl.pallas_call(body, name=name, in_specs=[vmem] * len(args), out_specs=vmem,
                          out_shape=jax.ShapeDtypeStruct((PACK_ROWS, PACK_COLS), F32))(*args)


def _adam_small(gsum, ws, ms, vs, name):
    names = list(PACK)
    n = len(names)

    def body(*refs):
        g_ref = refs[0]
        w_refs, m_refs, v_refs = refs[1:1 + n], refs[1 + n:1 + 2 * n], refs[1 + 2 * n:1 + 3 * n]
        outs = refs[1 + 3 * n:]
        outs[0][...] = g_ref[PACK_LOSS_ROW:PACK_LOSS_ROW + 1, 0:1]
        for k, nm in enumerate(names):
            o_g, o_d, o_m, o_v = outs[1 + 4 * k:5 + 4 * k]
            for row, rs, ls, width in _pack_pieces(nm, w_refs[k].shape[1]):
                src = (rs, ls)
                g = g_ref[row:row + 1, 0:width]
                d, m2, v2 = _adam(w_refs[k][src], g, m_refs[k][src], v_refs[k][src])
                o_g[src] = g
                o_d[src] = d
                o_m[src] = m2
                o_v[src] = v2

    vmem = pl.BlockSpec(memory_space=pltpu.VMEM)
    shapes = [jax.ShapeDtypeStruct((1, 1), F32)]
    for nm in names:
        shapes += [jax.ShapeDtypeStruct(ws[nm].shape, F32)] * 4
    args = [gsum] + [ws[nm] for nm in names] + [ms[nm] for nm in names] + [vs[nm] for nm in names]
    return pl.pallas_call(
        body, name=name, in_specs=[vmem] * len(args), out_specs=tuple([vmem] * len(shapes)), out_shape=tuple(shapes),
    )(*args)


def _as2d(name, a):
    return a.reshape(a.shape[-2], a.shape[-1]) if a.ndim == 3 else a


def kernel(x, mem, positions, attn_norm, w_in, a_q_norm, a_k_norm, b_q_norm, b_k_norm, b_sinks, mem_norm, w_mem_kv, m_q_norm, m_k_norm, w_o_a, w_o_b, w_o_m, w_gate, b_gate, w_out, ffn_norm, w_up, conv_w, conv_b, w_down, loss_target, m_attn_norm, m_w_in, m_a_q_norm, m_a_k_norm, m_b_q_norm, m_b_k_norm, m_b_sinks, m_mem_norm, m_w_mem_kv, m_m_q_norm, m_m_k_norm, m_w_o_a, m_w_o_b, m_w_o_m, m_w_gate, m_b_gate, m_w_out, m_ffn_norm, m_w_up, m_conv_w, m_conv_b, m_w_down, v_attn_norm, v_w_in, v_a_q_norm, v_a_k_norm, v_b_q_norm, v_b_k_norm, v_b_sinks, v_mem_norm, v_w_mem_kv, v_m_q_norm, v_m_k_norm, v_w_o_a, v_w_o_b, v_w_o_m, v_w_gate, v_b_gate, v_w_out, v_ffn_norm, v_w_up, v_conv_w, v_conv_b, v_w_down):
    given = dict(attn_norm=attn_norm, w_in=w_in, a_q_norm=a_q_norm, a_k_norm=a_k_norm, b_q_norm=b_q_norm, b_k_norm=b_k_norm, b_sinks=b_sinks, mem_norm=mem_norm, w_mem_kv=w_mem_kv, m_q_norm=m_q_norm, m_k_norm=m_k_norm, w_o_a=w_o_a, w_o_b=w_o_b, w_o_m=w_o_m, w_gate=w_gate, b_gate=b_gate, w_out=w_out, ffn_norm=ffn_norm, w_up=w_up, conv_w=conv_w, conv_b=conv_b, w_down=w_down)
    mom1 = dict(attn_norm=m_attn_norm, w_in=m_w_in, a_q_norm=m_a_q_norm, a_k_norm=m_a_k_norm, b_q_norm=m_b_q_norm, b_k_norm=m_b_k_norm, b_sinks=m_b_sinks, mem_norm=m_mem_norm, w_mem_kv=m_w_mem_kv, m_q_norm=m_m_q_norm, m_k_norm=m_m_k_norm, w_o_a=m_w_o_a, w_o_b=m_w_o_b, w_o_m=m_w_o_m, w_gate=m_w_gate, b_gate=m_b_gate, w_out=m_w_out, ffn_norm=m_ffn_norm, w_up=m_w_up, conv_w=m_conv_w, conv_b=m_conv_b, w_down=m_w_down)
    mom2 = dict(attn_norm=v_attn_norm, w_in=v_w_in, a_q_norm=v_a_q_norm, a_k_norm=v_a_k_norm, b_q_norm=v_b_q_norm, b_k_norm=v_b_k_norm, b_sinks=v_b_sinks, mem_norm=v_mem_norm, w_mem_kv=v_w_mem_kv, m_q_norm=v_m_q_norm, m_k_norm=v_m_k_norm, w_o_a=v_w_o_a, w_o_b=v_w_o_b, w_o_m=v_w_o_m, w_gate=v_w_gate, b_gate=v_b_gate, w_out=v_w_out, ffn_norm=v_ffn_norm, w_up=v_w_up, conv_w=v_conv_w, conv_b=v_conv_b, w_down=v_w_down)

    big = list(BIG)
    stages = {'mix': list(MIX_WEIGHTS), 'ffn': list(FFN_WEIGHTS), 'in': ['w_in']}
    my_slot = _slot(_coords())

    def shard(n):
        return given[n][0] if n == 'conv_w' else given[n][0].astype(BF16)

    def whole(n, g):
        _, r, c = g.shape
        return g.reshape(N_DEV * r, c) if BIG[n] == 0 else g.transpose(1, 0, 2).reshape(r, N_DEV * c)

    def to_blocks(n, g):
        r, c = given[n].shape[1:]
        g = g.reshape(N_DEV, r, c) if BIG[n] == 0 else g.reshape(r, N_DEV, c).transpose(1, 0, 2)
        return g if n == 'conv_w' else g.astype(BF16)

    class Hooks:
        next_stage = {'in': 'mix', 'mix': 'ffn'}

        def __init__(self):
            self.coming, self.sent = {}, {}
            self.shards = {n: shard(n) for n in big}
            self.start_gather('in', None)

        def start_gather(self, stage, after):
            src = [self.shards[n] for n in stages[stage]]
            self.coming[stage] = _exchange_start(src, f"gather_{stage}_start", gather=True, masks=CHIP_PEERS,
                                                 after=after)

        def weights(self, stage, after):
            names = stages[stage]
            after = list(after)
            if stage == 'in':
                after += [self.shards[n] for n in stages['mix'] + stages['ffn']]
            landed = _exchange_wait(self.coming[stage], after, f"gather_{stage}_wait", gather=True, masks=CHIP_PEERS)
            landed, token = _sibling_forward(landed, f"gather_{stage}_forward")
            if stage in self.next_stage:
                self.start_gather(self.next_stage[stage], token)
            return {n: whole(n, lax.dynamic_update_slice_in_dim(land, self.shards[n][None], my_slot, axis=0))
                    for n, land in zip(names, landed)}

        def grads(self, stage, g):
            blocks = [to_blocks(n, g[n]) for n in stages[stage]]
            own = [lax.dynamic_slice_in_dim(b, my_slot, 1, axis=0) for b in blocks]
            self.sent[stage] = (_exchange_start(blocks, f"exchange_{stage}_start"), own)
            return self.sent[stage][0][-1]

        def parts(self, stage, after):
            started, own = self.sent[stage]
            landed = _exchange_wait(started, [after], f"exchange_{stage}_wait")
            return {n: lax.dynamic_update_slice_in_dim(land, o, my_slot, axis=0)
                    for n, land, o in zip(stages[stage], landed, own)}

    hooks = Hooks()
    w = {}
    for n in SMALL:
        w[n] = given[n]
    w['a_q_norm'], w['a_k_norm'] = given['a_q_norm'][0], given['a_k_norm'][0]
    w['b_q_norm'], w['b_k_norm'], w['b_sinks'] = given['b_q_norm'][0], given['b_k_norm'][0], given['b_sinks'][0]

    loss_tile, grad_x, grads = _device_step(x[0], mem[0], positions[0], loss_target[0], w, hooks)
    out = {}
    after = grad_x
    for stage in ('ffn', 'mix', 'in'):
        for n, p in hooks.parts(stage, after).items():
            res = _adam_reduce(p, given[n][0], mom1[n][0], mom2[n][0], f"adam_{n}")
            out[n] = tuple(t[None] for t in res)
            after = res[0]

    small = {n: grads[n] for n in PACK}
    small['b_q_norm'], small['b_k_norm'] = grads['b_q_norm'].reshape(1, -1), grads['b_k_norm'].reshape(1, -1)
    small['b_sinks'] = grads['b_sinks'].reshape(1, -1)
    gsum = _all_sum(_pack_small(small, loss_tile, "pack_small"), "sum_small")
    ws = {n: _as2d(n, given[n]) for n in PACK}
    ms = {n: _as2d(n, mom1[n]) for n in PACK}
    vs = {n: _as2d(n, mom2[n]) for n in PACK}
    res = _adam_small(gsum, ws, ms, vs, "adam_small")
    loss = res[0].reshape(())
    for k, n in enumerate(PACK):
        out[n] = tuple(t.reshape(given[n].shape) for t in res[1 + 4 * k:5 + 4 * k])

    outs = [loss, grad_x[None]]
    for field in range(4):
        outs += [out[n][field] for n in WEIGHTS]
    return tuple(outs)
```

```python
import functools
import math

import jax
import jax.numpy as jnp
from jax import lax
from jax.experimental import pallas as pl
from jax.experimental.pallas import tpu as pltpu

F32 = jnp.float32
BF16 = jnp.bfloat16

N_DEV = 8
HEAD_DIM = 64
A_GROUPS = ((128, 1), (512, 4), (2048, 16))
B_WINDOW = 128
M_HEADS = 4
M_HEAD_DIM = 128
MEM_LEN = 256
D_FF = 2816
ROPE_THETA = 500000.0
ROPE_DIMS = 16
BLOCK = 128
EPS = 1e-6
LANES = 128
BAND_Q_BLOCKS = 4
BAND_UNITS = 2

ADAM_LR = 0.001
ADAM_B1 = 0.9
ADAM_B2 = 0.999
ADAM_EPS = 1e-08
ADAM_WD = 0.01
ADAM_STEP = 10

VMEM_LIMIT_BYTES = 56 * 1024 * 1024
MESH = pl.DeviceIdType.MESH

WEIGHTS = ['attn_norm', 'w_in', 'a_q_norm', 'a_k_norm', 'b_q_norm', 'b_k_norm', 'b_sinks', 'mem_norm',
           'w_mem_kv', 'm_q_norm', 'm_k_norm', 'w_o_a', 'w_o_b', 'w_o_m', 'w_gate', 'b_gate', 'w_out',
           'ffn_norm', 'w_up', 'conv_w', 'conv_b', 'w_down']
BIG = {'w_in': 1, 'w_mem_kv': 0, 'w_o_a': 1, 'w_o_b': 1, 'w_o_m': 1, 'w_gate': 1, 'w_out': 0, 'w_up': 1,
       'conv_w': 1, 'w_down': 0}
SMALL = [n for n in WEIGHTS if n not in BIG]


def _cparams(n_grid):
    return pltpu.CompilerParams(dimension_semantics=("arbitrary",) * n_grid, vmem_limit_bytes=VMEM_LIMIT_BYTES)


def _seg_matrix(width):
    shift = width.bit_length() - 1
    r = lax.shift_right_logical(lax.broadcasted_iota(jnp.int32, (LANES, LANES), 0), shift)
    c = lax.shift_right_logical(lax.broadcasted_iota(jnp.int32, (LANES, LANES), 1), shift)
    return jnp.where(r == c, 1.0, 0.0).astype(BF16)


def _seg_sum(x, seg):
    hi = x.astype(BF16)
    r1 = x - hi.astype(F32)
    mid = r1.astype(BF16)
    lo = (r1 - mid.astype(F32)).astype(BF16)
    dot = functools.partial(jnp.dot, preferred_element_type=F32)
    return dot(hi, seg) + dot(mid, seg) + dot(lo, seg)


def _rope(y, c, s1, s2):
    return y * c + pltpu.roll(y, LANES - ROPE_DIMS // 2, 1) * s1 + pltpu.roll(y, ROPE_DIMS // 2, 1) * s2


def _unrope(dy, c, s1, s2):
    return dy * c + pltpu.roll(dy * s1, ROPE_DIMS // 2, 1) + pltpu.roll(dy * s2, LANES - ROPE_DIMS // 2, 1)


def _sigmoid(x):
    return 1.0 / (1.0 + jnp.exp(-x))


def _rms_fwd(x, gain, name):
    s_len, d = x.shape
    tm = 512

    def body(x_ref, g_ref, h_ref, ht_ref, r_ref):
        xv = x_ref[...]
        r = lax.rsqrt(jnp.mean(xv * xv, axis=-1, keepdims=True) + EPS)
        h = ((xv * r) * g_ref[...]).astype(BF16)
        h_ref[...] = h
        ht_ref[...] = h.T
        r_ref[...] = r

    return pl.pallas_call(
        body, name=name, grid=(s_len // tm,),
        in_specs=[pl.BlockSpec((tm, d), lambda i: (i, 0)), pl.BlockSpec((1, d), lambda i: (0, 0))],
        out_specs=(pl.BlockSpec((tm, d), lambda i: (i, 0)), pl.BlockSpec((d, tm), lambda i: (0, i)),
                   pl.BlockSpec((tm, 1), lambda i: (i, 0))),
        out_shape=(jax.ShapeDtypeStruct((s_len, d), BF16), jax.ShapeDtypeStruct((d, s_len), BF16),
                   jax.ShapeDtypeStruct((s_len, 1), F32)),
        compiler_params=_cparams(1),
    )(x, gain)


def _resident(shape, index_map):
    return pl.BlockSpec(shape, index_map, pipeline_mode=pl.Buffered(1))


def _mm_rows(pairs, name, nt=False, tm=512, bias=None, sigmoid=False, res=None, out_dtypes=(F32,), loss_target=None,
             rms_bwd=None):
    m = pairs[0][0].shape[0]
    n = pairs[0][1].shape[0] if nt else pairs[0][1].shape[1]
    n_pairs = len(pairs)
    has_bias, has_res, has_loss = bias is not None, res is not None, loss_target is not None
    has_rms = rms_bwd is not None
    dims = (((1,), (1,)), ((), ())) if nt else (((1,), (0,)), ((), ()))

    def body(*refs):
        acc = None
        for p in range(n_pairs):
            t = lax.dot_general(refs[2 * p][...].astype(BF16), refs[2 * p + 1][...], dims, preferred_element_type=F32)
            acc = t if acc is None else acc + t
        pos = 2 * n_pairs
        if has_bias:
            acc = acc + refs[pos][...]
            pos += 1
        if sigmoid:
            acc = _sigmoid(acc)
        if has_res:
            acc = refs[pos][...] + acc
            pos += 1
        if has_loss:
            dy_ref, dyb_ref, da_ref, l_ref = refs[pos + 1:]

            @pl.when(pl.program_id(0) == 0)
            def _():
                l_ref[...] = jnp.zeros_like(l_ref)
            err = acc - refs[pos][...]
            dy = err * (1.0 / n)
            dy_ref[...] = dy
            dyb_ref[...] = dy.astype(BF16)
            da_ref[...] = lax.dot_general(dy.astype(BF16), refs[1][...], (((1,), (1,)), ((), ())),
                                          preferred_element_type=F32).astype(BF16)
            part = 0.5 * jnp.sum(jnp.mean(err * err, axis=-1, keepdims=True), axis=0, keepdims=True)
            l_ref[...] += jnp.broadcast_to(part, l_ref.shape)
            return
        if has_rms:
            x_ref, r_ref, g_ref, add_ref = refs[pos:pos + 4]
            dg_ref = refs[-1]

            @pl.when(pl.program_id(0) == 0)
            def _():
                dg_ref[...] = jnp.zeros_like(dg_ref)
            rv = r_ref[...]
            xhat = x_ref[...] * rv
            dg_ref[...] += jnp.sum(acc * xhat, axis=0, keepdims=True)
            dxhat = acc * g_ref[...]
            acc = add_ref[...] + rv * (dxhat - xhat * jnp.mean(dxhat * xhat, axis=-1, keepdims=True))
            for o_ref in refs[pos + 4:-1]:
                o_ref[...] = acc.astype(o_ref.dtype)
            return
        for o_ref in refs[pos:]:
            o_ref[...] = acc.astype(o_ref.dtype)

    in_specs, args = [], []
    for a, w, blk in pairs:
        k = a.shape[1]
        in_specs.append(pl.BlockSpec((tm, k), lambda i: (i, 0)))
        if nt:
            in_specs.append(_resident((n, k), lambda i, blk=blk: (0, blk)))
        else:
            in_specs.append(_resident((k, n), lambda i, blk=blk: (blk, 0)))
        args += [a, w]
    if has_bias:
        in_specs.append(_resident((1, n), lambda i: (0, 0)))
        args.append(bias)
    if has_res:
        in_specs.append(pl.BlockSpec((tm, n), lambda i: (i, 0)))
        args.append(res)
    out = pl.BlockSpec((tm, n), lambda i: (i, 0))
    if has_loss:
        k0 = pairs[0][0].shape[1]
        return pl.pallas_call(
            body, name=name, grid=(m // tm,), in_specs=in_specs + [out],
            out_specs=(out, out, pl.BlockSpec((tm, k0), lambda i: (i, 0)), pl.BlockSpec((8, LANES), lambda i: (0, 0))),
            out_shape=(jax.ShapeDtypeStruct((m, n), F32), jax.ShapeDtypeStruct((m, n), BF16),
                       jax.ShapeDtypeStruct((m, k0), BF16), jax.ShapeDtypeStruct((8, LANES), F32)),
            compiler_params=_cparams(1),
        )(*args, loss_target)
    if has_rms:
        x, r, gain, add = rms_bwd
        vec = _resident((1, n), lambda i: (0, 0))
        return pl.pallas_call(
            body, name=name, grid=(m // tm,),
            in_specs=in_specs + [out, pl.BlockSpec((tm, 1), lambda i: (i, 0)), vec, out],
            out_specs=tuple([out] * len(out_dtypes) + [pl.BlockSpec((1, n), lambda i: (0, 0))]),
            out_shape=tuple([jax.ShapeDtypeStruct((m, n), dt) for dt in out_dtypes] + [jax.ShapeDtypeStruct((1, n), F32)]),
            compiler_params=_cparams(1),
        )(*args, x, r, gain, add)
    outs = pl.pallas_call(
        body, name=name, grid=(m // tm,), in_specs=in_specs, out_specs=tuple([out] * len(out_dtypes)),
        out_shape=tuple(jax.ShapeDtypeStruct((m, n), dt) for dt in out_dtypes), compiler_params=_cparams(1),
    )(*args)
    return outs[0] if len(out_dtypes) == 1 else outs


def _mm_rows_each(pairs, name, tm=512):
    m = pairs[0][0].shape[0]
    n_pairs = len(pairs)

    def body(*refs):
        for p in range(n_pairs):
            refs[2 * n_pairs + p][...] = lax.dot_general(refs[2 * p][...].astype(BF16), refs[2 * p + 1][...],
                                                         (((1,), (1,)), ((), ())), preferred_element_type=F32)

    in_specs, args = [], []
    for a, w in pairs:
        in_specs += [pl.BlockSpec((tm, a.shape[1]), lambda i: (i, 0)), _resident(w.shape, lambda i: (0, 0))]
        args += [a, w]
    return pl.pallas_call(
        body, name=name, grid=(m // tm,), in_specs=in_specs,
        out_specs=tuple(pl.BlockSpec((tm, w.shape[0]), lambda i: (i, 0)) for _, w in pairs),
        out_shape=tuple(jax.ShapeDtypeStruct((m, w.shape[0]), F32) for _, w in pairs), compiler_params=_cparams(1),
    )(*args)


def _mm_rows_cat(a, ws, name, tm=256):
    m, k = a.shape
    widths = [w.shape[1] for w in ws]
    n = sum(widths)

    def body(*refs):
        a_ref, o_ref = refs[0], refs[-1]
        av = a_ref[...]
        off = 0
        for p, width in enumerate(widths):
            o_ref[:, off:off + width] = jnp.dot(av, refs[1 + p][...], preferred_element_type=F32)
            off += width

    return pl.pallas_call(
        body, name=name, grid=(m // tm,),
        in_specs=[pl.BlockSpec((tm, k), lambda i: (i, 0))] + [_resident((k, wd), lambda i: (0, 0)) for wd in widths],
        out_specs=pl.BlockSpec((tm, n), lambda i: (i, 0)),
        out_shape=jax.ShapeDtypeStruct((m, n), F32), compiler_params=_cparams(1),
    )(a, *ws)


def _mm_cols(a, b, name, tn=256):
    m, k = a.shape
    n = b.shape[1]

    def body(a_ref, b_ref, o_ref):
        o_ref[...] = jnp.dot(a_ref[...], b_ref[...].astype(BF16), preferred_element_type=F32)

    return pl.pallas_call(
        body, name=name, grid=(n // tn,),
        in_specs=[_resident((m, k), lambda j: (0, 0)), pl.BlockSpec((k, tn), lambda j: (0, j))],
        out_specs=pl.BlockSpec((m, tn), lambda j: (0, j)),
        out_shape=jax.ShapeDtypeStruct((m, n), F32), compiler_params=_cparams(1),
    )(a, b)


def _mm_tn_each(pairs, name, tile=256):
    n = pairs[0][1].shape[1]
    n_pairs = len(pairs)
    dims = (((0,), (0,)), ((), ()))

    def body(*refs):
        for p in range(n_pairs):
            refs[2 * n_pairs + p][...] = lax.dot_general(refs[2 * p][...].astype(BF16), refs[2 * p + 1][...].astype(BF16),
                                                         dims, preferred_element_type=F32)

    in_specs, args = [], []
    for a, b in pairs:
        in_specs += [_resident(a.shape, lambda j: (0, 0)), pl.BlockSpec((b.shape[0], tile), lambda j: (0, j))]
        args += [a, b]
    return pl.pallas_call(
        body, name=name, grid=(n // tile,), in_specs=in_specs,
        out_specs=tuple(pl.BlockSpec((a.shape[1], tile), lambda j: (0, j)) for a, _ in pairs),
        out_shape=tuple(jax.ShapeDtypeStruct((a.shape[1], n), F32) for a, _ in pairs), compiler_params=_cparams(1),
    )(*args)


def _mm_tn(a, b, name, tile=256):
    k, m = a.shape
    n = b.shape[1]
    dims = (((0,), (0,)), ((), ()))

    def body(a_ref, b_ref, o_ref):
        o_ref[...] = lax.dot_general(a_ref[...].astype(BF16), b_ref[...].astype(BF16), dims, preferred_element_type=F32)

    if n <= m:
        t = min(tile, m)
        grid, a_spec, b_spec = (m // t,), pl.BlockSpec((k, t), lambda i: (0, i)), _resident((k, n), lambda i: (0, 0))
        o_spec = pl.BlockSpec((t, n), lambda i: (i, 0))
    else:
        t = min(tile, n)
        grid, a_spec, b_spec = (n // t,), _resident((k, m), lambda i: (0, 0)), pl.BlockSpec((k, t), lambda i: (0, i))
        o_spec = pl.BlockSpec((m, t), lambda i: (0, i))
    return pl.pallas_call(
        body, name=name, grid=grid, in_specs=[a_spec, b_spec], out_specs=o_spec,
        out_shape=jax.ShapeDtypeStruct((m, n), F32), compiler_params=_cparams(1),
    )(a, b)


def _norm_rope(t, gain, c, s1, s2, seg):
    rs = lax.rsqrt(_seg_sum(t * t, seg) * (1.0 / HEAD_DIM) + EPS)
    return _rope((t * rs) * gain, c, s1, s2)


def _dup_half(y, half):
    lane = lax.broadcasted_iota(jnp.int32, y.shape, 1)
    rolled = pltpu.roll(y, HEAD_DIM, 1)
    keep = (lane < HEAD_DIM) if half == 0 else (lane >= HEAD_DIM)
    return jnp.where(keep, y, rolled)


def _qk_prep(proj, cb0, d, gqa, gq, gk, tabs, name):
    s_len = proj.shape[0]
    tm = 512
    rows = tm // d
    n_units = 4 if gqa else 2 * d
    n_q = 4 if gqa else 2
    n_in = 6

    def body(*refs):
        in_refs = refs[:n_in]
        gq_ref, gk_ref, c_ref, s1_ref, s2_ref, o_ref = refs[n_in:]
        seg = _seg_matrix(HEAD_DIM)

        def rows_of(ref, r):
            return ref[...] if d == 1 else ref[pl.ds(r, rows, stride=d), :]

        def put(unit_col, y):
            o_ref[:, unit_col * LANES:(unit_col + 1) * LANES] = y.astype(BF16)

        for r in range(d):
            c, s1, s2 = rows_of(c_ref, r), rows_of(s1_ref, r), rows_of(s2_ref, r)
            for b in range(n_in):
                t = rows_of(in_refs[b], r)
                if b < n_q:
                    put((b * d + r) if not gqa else b, _norm_rope(t, gq_ref[...], c, s1, s2, seg))
                elif not gqa:
                    sec, pair = (1, b - 2) if b < 4 else (2, b - 4)
                    y = _norm_rope(t, gk_ref[...], c, s1, s2, seg) if sec == 1 else t
                    put(sec * n_units + pair * d + r, y)
                else:
                    sec = 1 if b == 4 else 2
                    y = _norm_rope(t, gk_ref[...], c, s1, s2, seg) if sec == 1 else t
                    for u in range(n_units):
                        put(sec * n_units + u, _dup_half(y, u // 2))

    in_specs = [pl.BlockSpec((tm, LANES), lambda i, b=b: (i, cb0 + b)) for b in range(n_in)]
    vec = pl.BlockSpec((1, LANES), lambda i: (0, 0))
    tab = pl.BlockSpec((tm, LANES), lambda i: (i, 0))
    width = 3 * n_units * LANES
    return pl.pallas_call(
        body, name=name, grid=(s_len // tm,), in_specs=in_specs + [vec, vec, tab, tab, tab],
        out_specs=pl.BlockSpec((rows, width), lambda i: (i, 0)),
        out_shape=jax.ShapeDtypeStruct((s_len // d, width), BF16), compiler_params=_cparams(1),
    )(*([proj] * n_in), gq, gk, *tabs)


def _qk_prep_bwd(dqkv, proj, cb0, d, gqa, gq, gk, tabs, name):
    s_len = proj.shape[0]
    tm = 512
    rows = tm // d
    n_units = 4 if gqa else 2 * d
    n_q = 4 if gqa else 2
    n_in = 6

    def body(*refs):
        d_refs = refs[0:3]
        in_refs = refs[3:3 + n_in]
        gq_ref, gk_ref, c_ref, s1_ref, s2_ref, o_ref, dgq_ref, dgk_ref, stage = refs[3 + n_in:]
        seg = _seg_matrix(HEAD_DIM)

        @pl.when(pl.program_id(0) == 0)
        def _():
            dgq_ref[...] = jnp.zeros_like(dgq_ref)
            dgk_ref[...] = jnp.zeros_like(dgk_ref)

        def rows_of(ref, r):
            return ref[...] if d == 1 else ref[pl.ds(r, rows, stride=d), :]

        def unit(col):
            sec, u = divmod(col, n_units)
            return d_refs[sec][:, u * LANES:(u + 1) * LANES]

        def norm_bwd(dyr, t, gain, c, s1, s2, dg_ref):
            rs = lax.rsqrt(_seg_sum(t * t, seg) * (1.0 / HEAD_DIM) + EPS)
            that = t * rs
            dy = _unrope(dyr, c, s1, s2)
            dg_ref[...] += jnp.sum(dy * that, axis=0, keepdims=True)
            dthat = dy * gain
            return rs * (dthat - that * (_seg_sum(dthat * that, seg) * (1.0 / HEAD_DIM)))

        def fold(sec):
            tot = []
            for u in range(n_units):
                v = unit(sec * n_units + u)
                tot.append(v + pltpu.roll(v, HEAD_DIM, 1))
            lane = lax.broadcasted_iota(jnp.int32, tot[0].shape, 1)
            return jnp.where(lane < HEAD_DIM, tot[0] + tot[1], tot[2] + tot[3])

        for b in range(n_in):
            for r in range(d):
                c, s1, s2 = rows_of(c_ref, r), rows_of(s1_ref, r), rows_of(s2_ref, r)
                t = rows_of(in_refs[b], r)
                if b < n_q:
                    g = unit((b * d + r) if not gqa else b)
                    out = norm_bwd(g, t, gq_ref[...], c, s1, s2, dgq_ref)
                elif not gqa:
                    sec, pair = (1, b - 2) if b < 4 else (2, b - 4)
                    g = unit(sec * n_units + pair * d + r)
                    out = norm_bwd(g, t, gk_ref[...], c, s1, s2, dgk_ref) if sec == 1 else g
                else:
                    sec = 1 if b == 4 else 2
                    g = fold(sec)
                    out = norm_bwd(g, t, gk_ref[...], c, s1, s2, dgk_ref) if sec == 1 else g
                if d == 1:
                    o_ref[:, b * LANES:(b + 1) * LANES] = out.astype(BF16)
                else:
                    stage[pl.ds(r, rows, stride=d), :] = out
            if d != 1:
                o_ref[:, b * LANES:(b + 1) * LANES] = stage[...].astype(BF16)

    in_specs = [pl.BlockSpec((rows, n_units * LANES), lambda i: (i, 0))] * 3
    in_specs += [pl.BlockSpec((tm, LANES), lambda i, b=b: (i, cb0 + b)) for b in range(n_in)]
    vec = pl.BlockSpec((1, LANES), lambda i: (0, 0))
    tab = pl.BlockSpec((tm, LANES), lambda i: (i, 0))
    return pl.pallas_call(
        body, name=name, grid=(s_len // tm,), in_specs=in_specs + [vec, vec, tab, tab, tab],
        out_specs=(pl.BlockSpec((tm, n_in * LANES), lambda i: (i, 0)), vec, vec),
        out_shape=(jax.ShapeDtypeStruct((s_len, n_in * LANES), BF16), jax.ShapeDtypeStruct((1, LANES), F32),
                   jax.ShapeDtypeStruct((1, LANES), F32)),
        scratch_shapes=[pltpu.VMEM((tm, LANES), F32)], compiler_params=_cparams(1),
    )(*dqkv, *([proj] * n_in), gq, gk, *tabs)


def _head_masks(shape):
    lane = lax.broadcasted_iota(jnp.int32, shape, 1)
    return lane < HEAD_DIM, lane >= HEAD_DIM


def _band_fwd(qkv, n_units, max_dist, sinks, name):
    n_rows = qkv.shape[0]
    nb = n_rows // BLOCK
    scale = HEAD_DIM ** -0.5
    has_sink = sinks is not None
    assert not has_sink or max_dist < BLOCK

    qn, un = min(nb, BAND_Q_BLOCKS), BAND_UNITS
    ug = n_units // un

    def body(*refs):
        q_ref, kp_ref, km_ref, vp_ref, vm_ref = refs[:5]
        o_ref, lse_ref = refs[-2:]
        i = pl.program_id(1)
        qi = lax.broadcasted_iota(jnp.int32, (BLOCK, 2 * BLOCK), 0)
        kj = lax.broadcasted_iota(jnp.int32, (BLOCK, 2 * BLOCK), 1)
        dist = qi + BLOCK - kj
        band = (dist >= 0) & (dist <= max_dist)
        band_first = band & ((i > 0) | (kj >= BLOCK))
        m0, m1 = _head_masks((BLOCK, LANES))
        zero = jnp.zeros((BLOCK, LANES), BF16)
        for ub in range(un):
            cs = slice(ub * LANES, (ub + 1) * LANES)
            for qb in range(qn):
                rs = slice(qb * BLOCK, (qb + 1) * BLOCK)
                q = q_ref[rs, cs]
                if qb == 0:
                    kk = jnp.concatenate([kp_ref[:, cs], km_ref[0:BLOCK, cs]], axis=0)
                    vv = jnp.concatenate([vp_ref[:, cs], vm_ref[0:BLOCK, cs]], axis=0)
                    valid = band_first
                else:
                    kk = km_ref[(qb - 1) * BLOCK:(qb + 1) * BLOCK, cs]
                    vv = vm_ref[(qb - 1) * BLOCK:(qb + 1) * BLOCK, cs]
                    valid = band
                outs, lses = [], []
                for e, hm in enumerate((m0, m1)):
                    qe = jnp.where(hm, q, zero)
                    s = lax.dot_general(qe, kk, (((1,), (1,)), ((), ())), preferred_element_type=F32) * scale
                    s = jnp.where(valid, s, -jnp.inf)
                    if has_sink:
                        s = jnp.where(kj == 0, refs[5][ub][:, e * HEAD_DIM:e * HEAD_DIM + 1], s)
                    mx = jnp.max(s, axis=-1, keepdims=True)
                    p = jnp.exp(s - mx)
                    den = jnp.sum(p, axis=-1, keepdims=True)
                    pn = p * (1.0 / den)
                    if has_sink:
                        pn = jnp.where(kj == 0, 0.0, pn)
                    pn = pn.astype(BF16)
                    outs.append(jnp.dot(pn, vv, preferred_element_type=F32))
                    lses.append(mx + jnp.log(den))
                o_ref[rs, cs] = jnp.where(m0, outs[0], outs[1])
                lse_ref[rs, cs] = jnp.where(m0, jnp.broadcast_to(lses[0], (BLOCK, LANES)),
                                            jnp.broadcast_to(lses[1], (BLOCK, LANES)))

    def main(sec):
        return pl.BlockSpec((qn * BLOCK, un * LANES), lambda u, i: (i, sec * ug + u))

    def prev(sec):
        return pl.BlockSpec((BLOCK, un * LANES), lambda u, i: (jnp.maximum(i * qn - 1, 0), sec * ug + u))

    in_specs = [main(0), prev(1), main(1), prev(2), main(2)]
    args = [qkv] * 5
    if has_sink:
        in_specs.append(pl.BlockSpec((un, 1, LANES), lambda u, i: (u, 0, 0)))
        args.append(sinks)
    return pl.pallas_call(
        body, name=name, grid=(ug, nb // qn), in_specs=in_specs, out_specs=(main(0), main(0)),
        out_shape=(jax.ShapeDtypeStruct((n_rows, n_units * LANES), F32),) * 2, compiler_params=_cparams(2),
    )(*args)


def _band_bwd(qkv, do, lse, delta, n_units, max_dist, name):
    n_rows = qkv.shape[0]
    nb = n_rows // BLOCK
    scale = HEAD_DIM ** -0.5

    qn, un = min(nb, BAND_Q_BLOCKS), BAND_UNITS
    ug = n_units // un
    steps = nb // qn
    nt_dims = (((1,), (1,)), ((), ()))
    tn_dims = (((0,), (0,)), ((), ()))

    def body(qm_ref, qx_ref, kp_ref, km_ref, vp_ref, vm_ref, dom_ref, dox_ref, lm_ref, lx_ref, dm_ref, dx_ref,
             dq_ref, dk_ref, dv_ref):
        i = pl.program_id(1)
        m0, m1 = _head_masks((BLOCK, LANES))
        zero = jnp.zeros((BLOCK, LANES), BF16)
        qi = lax.broadcasted_iota(jnp.int32, (BLOCK, 2 * BLOCK), 0)
        kj = lax.broadcasted_iota(jnp.int32, (BLOCK, 2 * BLOCK), 1)
        dist = qi + BLOCK - kj
        band = (dist >= 0) & (dist <= max_dist)
        band_first = band & ((i > 0) | (kj >= BLOCK))
        qr = lax.broadcasted_iota(jnp.int32, (BLOCK, BLOCK), 0)
        kc = lax.broadcasted_iota(jnp.int32, (BLOCK, BLOCK), 1)
        dist_x = qr + BLOCK - kc
        band_next = (dist_x >= 0) & (dist_x <= max_dist) & (i < steps - 1)

        def pair(q, dob, lse_b, del_b, kk, vv, valid):
            dqs, dk, dv = [], None, None
            for e, hm in enumerate((m0, m1)):
                col = slice(e * HEAD_DIM, e * HEAD_DIM + 1)
                qe = jnp.where(hm, q, zero)
                doe = jnp.where(hm, dob, zero)
                s = lax.dot_general(qe, kk, nt_dims, preferred_element_type=F32) * scale
                p = jnp.where(valid, jnp.exp(s - lse_b[:, col]), 0.0)
                dp = lax.dot_general(doe, vv, nt_dims, preferred_element_type=F32)
                ds = (p * (dp - del_b[:, col]) * scale).astype(BF16)
                dqs.append(jnp.dot(ds, kk, preferred_element_type=F32))
                dk_e = lax.dot_general(ds, qe, tn_dims, preferred_element_type=F32)
                dv_e = lax.dot_general(p.astype(BF16), doe, tn_dims, preferred_element_type=F32)
                dk = dk_e if dk is None else dk + dk_e
                dv = dv_e if dv is None else dv + dv_e
            return jnp.where(m0, dqs[0], dqs[1]), dk, dv

        for ub in range(un):
            cs = slice(ub * LANES, (ub + 1) * LANES)
            dk_acc, dv_acc = [None] * qn, [None] * qn

            def add(acc, kb, part):
                acc[kb] = part if acc[kb] is None else acc[kb] + part

            for qb in range(qn):
                rs = slice(qb * BLOCK, (qb + 1) * BLOCK)
                if qb == 0:
                    kk = jnp.concatenate([kp_ref[:, cs], km_ref[0:BLOCK, cs]], axis=0)
                    vv = jnp.concatenate([vp_ref[:, cs], vm_ref[0:BLOCK, cs]], axis=0)
                    valid = band_first
                else:
                    kk = km_ref[(qb - 1) * BLOCK:(qb + 1) * BLOCK, cs]
                    vv = vm_ref[(qb - 1) * BLOCK:(qb + 1) * BLOCK, cs]
                    valid = band
                dq, dk, dv = pair(qm_ref[rs, cs], dom_ref[rs, cs], lm_ref[rs, cs], dm_ref[rs, cs], kk, vv, valid)
                dq_ref[rs, cs] = dq
                if qb > 0:
                    add(dk_acc, qb - 1, dk[0:BLOCK])
                    add(dv_acc, qb - 1, dv[0:BLOCK])
                add(dk_acc, qb, dk[BLOCK:2 * BLOCK])
                add(dv_acc, qb, dv[BLOCK:2 * BLOCK])
            last = slice((qn - 1) * BLOCK, qn * BLOCK)
            _, dk, dv = pair(qx_ref[:, cs], dox_ref[:, cs], lx_ref[:, cs], dx_ref[:, cs], km_ref[last, cs], vm_ref[last, cs],
                             band_next)
            add(dk_acc, qn - 1, dk)
            add(dv_acc, qn - 1, dv)
            for kb in range(qn):
                dk_ref[kb * BLOCK:(kb + 1) * BLOCK, cs] = dk_acc[kb]
                dv_ref[kb * BLOCK:(kb + 1) * BLOCK, cs] = dv_acc[kb]

    def main(sec):
        return pl.BlockSpec((qn * BLOCK, un * LANES), lambda u, i: (i, sec * ug + u))

    def prev(sec):
        return pl.BlockSpec((BLOCK, un * LANES), lambda u, i: (jnp.maximum(i * qn - 1, 0), sec * ug + u))

    def nxt(sec):
        return pl.BlockSpec((BLOCK, un * LANES), lambda u, i: (jnp.minimum((i + 1) * qn, nb - 1), sec * ug + u))

    in_specs = [main(0), nxt(0), prev(1), main(1), prev(2), main(2),
                main(0), nxt(0), main(0), nxt(0), main(0), nxt(0)]
    args = [qkv] * 6 + [do, do, lse, lse, delta, delta]
    shp = jax.ShapeDtypeStruct((n_rows, n_units * LANES), F32)
    return pl.pallas_call(
        body, name=name, grid=(ug, steps), in_specs=in_specs, out_specs=(main(0), main(0), main(0)),
        out_shape=(shp, shp, shp), compiler_params=_cparams(2),
    )(*args)


def _merge_groups(os_, lses, dils, name):
    s_len = os_[0].shape[0] * dils[0]
    tm = 512

    def body(*refs):
        o_refs, l_refs = refs[0:3], refs[3:6]
        o_ref, lse_ref = refs[6:8]
        so, sl = refs[8:11], refs[11:14]
        for pair in range(2):
            for g, d in enumerate(dils):
                rows = tm // d
                for r in range(d):
                    col = slice((pair * d + r) * LANES, (pair * d + r + 1) * LANES)
                    if d == 1:
                        so[g][...] = o_refs[g][:, col]
                        sl[g][...] = l_refs[g][:, col]
                    else:
                        so[g][pl.ds(r, rows, stride=d), :] = o_refs[g][:, col]
                        sl[g][pl.ds(r, rows, stride=d), :] = l_refs[g][:, col]
            l0, l1, l2 = sl[0][...], sl[1][...], sl[2][...]
            mx = jnp.maximum(jnp.maximum(l0, l1), l2)
            e0, e1, e2 = jnp.exp(l0 - mx), jnp.exp(l1 - mx), jnp.exp(l2 - mx)
            den = e0 + e1 + e2
            inv = 1.0 / den
            o_ref[:, pair * LANES:(pair + 1) * LANES] = (so[0][...] * (e0 * inv) + so[1][...] * (e1 * inv)
                                                         + so[2][...] * (e2 * inv))
            lse_ref[:, pair * LANES:(pair + 1) * LANES] = mx + jnp.log(den)

    in_specs = [pl.BlockSpec((tm // d, 2 * d * LANES), lambda i: (i, 0)) for d in dils] * 2
    out = pl.BlockSpec((tm, 2 * LANES), lambda i: (i, 0))
    shp = jax.ShapeDtypeStruct((s_len, 2 * LANES), F32)
    return pl.pallas_call(
        body, name=name, grid=(s_len // tm,), in_specs=in_specs, out_specs=(out, out), out_shape=(shp, shp),
        scratch_shapes=[pltpu.VMEM((tm, LANES), F32)] * 6, compiler_params=_cparams(1),
    )(*os_, *lses)


def _bwd_prep(do, o, lse, dils, sinks, name):
    s_len, width = do.shape
    n_pairs = width // LANES
    tm = 512
    has_sink = sinks is not None
    n_g = len(dils)

    def body(*refs):
        do_ref, o_ref, lse_ref = refs[:3]
        pos = 3
        if has_sink:
            sink_ref = refs[pos]
            pos += 1
        outs = refs[pos:pos + 3 * n_g]
        pos += 3 * n_g
        if has_sink:
            dsink_ref = refs[pos]
            pos += 1
        s_do, s_l, s_d = refs[pos:pos + 3]
        seg = _seg_matrix(HEAD_DIM)

        if has_sink:
            @pl.when(pl.program_id(0) == 0)
            def _():
                dsink_ref[...] = jnp.zeros_like(dsink_ref)

        for pair in range(n_pairs):
            col = slice(pair * LANES, (pair + 1) * LANES)
            dov = do_ref[:, col]
            lv = lse_ref[:, col]
            delta = _seg_sum(dov * o_ref[:, col], seg)
            if has_sink:
                dsink_ref[pair] += -jnp.sum(jnp.exp(sink_ref[pair] - lv) * delta, axis=0, keepdims=True)
            s_do[...] = dov
            s_l[...] = lv
            s_d[...] = delta
            for g, d in enumerate(dils):
                rows = tm // d
                for r in range(d):
                    oc = slice((pair * d + r) * LANES, (pair * d + r + 1) * LANES)
                    if d == 1:
                        a, b, c = s_do[...], s_l[...], s_d[...]
                    else:
                        a = s_do[pl.ds(r, rows, stride=d), :]
                        b = s_l[pl.ds(r, rows, stride=d), :]
                        c = s_d[pl.ds(r, rows, stride=d), :]
                    outs[3 * g][:, oc] = a.astype(BF16)
                    outs[3 * g + 1][:, oc] = b
                    outs[3 * g + 2][:, oc] = c

    row = pl.BlockSpec((tm, width), lambda i: (i, 0))
    in_specs = [row, row, row]
    args = [do, o, lse]
    if has_sink:
        in_specs.append(pl.BlockSpec((n_pairs, 1, LANES), lambda i: (0, 0, 0)))
        args.append(sinks)
    out_specs, out_shape = [], []
    for d in dils:
        for dt in (BF16, F32, F32):
            out_specs.append(pl.BlockSpec((tm // d, n_pairs * d * LANES), lambda i: (i, 0)))
            out_shape.append(jax.ShapeDtypeStruct((s_len // d, n_pairs * d * LANES), dt))
    if has_sink:
        out_specs.append(pl.BlockSpec((n_pairs, 1, LANES), lambda i: (0, 0, 0)))
        out_shape.append(jax.ShapeDtypeStruct((n_pairs, 1, LANES), F32))
    return pl.pallas_call(
        body, name=name, grid=(s_len // tm,), in_specs=in_specs, out_specs=tuple(out_specs),
        out_shape=tuple(out_shape), scratch_shapes=[pltpu.VMEM((tm, LANES), F32)] * 3, compiler_params=_cparams(1),
    )(*args)


def _mem_kv(mem, mem_gain, w_kv, k_gain, name):
    m_len = mem.shape[0]
    kw = M_HEADS * M_HEAD_DIM

    def body(mem_ref, mg_ref, w_ref, kg_ref, k_ref, v_ref):
        mv = mem_ref[...]
        r = lax.rsqrt(jnp.mean(mv * mv, axis=-1, keepdims=True) + EPS)
        mn = ((mv * r) * mg_ref[...]).astype(BF16)
        kv = jnp.dot(mn, w_ref[...], preferred_element_type=F32)
        for h in range(M_HEADS):
            col = slice(h * M_HEAD_DIM, (h + 1) * M_HEAD_DIM)
            t = kv[:, col]
            rk = lax.rsqrt(jnp.mean(t * t, axis=-1, keepdims=True) + EPS)
            k_ref[:, col] = ((t * rk) * kg_ref[...]).astype(BF16)
        v_ref[...] = kv[:, kw:].astype(BF16)

    shp = jax.ShapeDtypeStruct((m_len, kw), BF16)
    return pl.pallas_call(body, name=name, out_shape=(shp, shp),
                          compiler_params=pltpu.CompilerParams(vmem_limit_bytes=VMEM_LIMIT_BYTES))(mem, mem_gain, w_kv, k_gain)


def _mem_kv_bwd(mem, mem_gain, w_kv, k_gain, dk, dv, name):
    m_len, d = mem.shape
    kw = M_HEADS * M_HEAD_DIM

    def body(mem_ref, mg_ref, w_ref, kg_ref, dk_ref, dv_ref, dw_ref, dmg_ref, dkg_ref, dkv_ref):
        mv = mem_ref[...]
        r = lax.rsqrt(jnp.mean(mv * mv, axis=-1, keepdims=True) + EPS)
        mhat = mv * r
        mn = (mhat * mg_ref[...]).astype(BF16)
        kv = jnp.dot(mn, w_ref[...], preferred_element_type=F32)
        dkg = jnp.zeros((1, M_HEAD_DIM), F32)
        for h in range(M_HEADS):
            col = slice(h * M_HEAD_DIM, (h + 1) * M_HEAD_DIM)
            t = kv[:, col]
            rk = lax.rsqrt(jnp.mean(t * t, axis=-1, keepdims=True) + EPS)
            that = t * rk
            dy = dk_ref[:, col]
            dkg = dkg + jnp.sum(dy * that, axis=0, keepdims=True)
            dthat = dy * kg_ref[...]
            dkv_ref[:, col] = (rk * (dthat - that * jnp.mean(dthat * that, axis=-1, keepdims=True))).astype(BF16)
        dkv_ref[:, kw:] = dv_ref[...].astype(BF16)
        dkg_ref[...] = dkg
        dkv = dkv_ref[...]
        dw_ref[...] = lax.dot_general(mn, dkv, (((0,), (0,)), ((), ())), preferred_element_type=F32)
        dmn = lax.dot_general(dkv, w_ref[...], (((1,), (1,)), ((), ())), preferred_element_type=F32)
        dmg_ref[...] = jnp.sum(dmn * mhat, axis=0, keepdims=True)

    return pl.pallas_call(
        body, name=name,
        out_shape=(jax.ShapeDtypeStruct((d, 2 * kw), F32), jax.ShapeDtypeStruct((1, d), F32),
                   jax.ShapeDtypeStruct((1, M_HEAD_DIM), F32)),
        scratch_shapes=[pltpu.VMEM((m_len, 2 * kw), BF16)],
        compiler_params=pltpu.CompilerParams(vmem_limit_bytes=VMEM_LIMIT_BYTES),
    )(mem, mem_gain, w_kv, k_gain, dk, dv)


def _mem_attn_fwd(proj, cidx, mk, mv, q_gain, name):
    s_len = proj.shape[0]
    kw = M_HEADS * M_HEAD_DIM
    tm = 512
    scale = M_HEAD_DIM ** -0.5

    def body(q_ref, k_ref, v_ref, g_ref, o_ref):
        for h in range(M_HEADS):
            col = slice(h * M_HEAD_DIM, (h + 1) * M_HEAD_DIM)
            t = q_ref[:, col]
            rs = lax.rsqrt(jnp.mean(t * t, axis=-1, keepdims=True) + EPS)
            qn = ((t * rs) * g_ref[...]).astype(BF16)
            s = lax.dot_general(qn, k_ref[:, col], (((1,), (1,)), ((), ())), preferred_element_type=F32) * scale
            mx = jnp.max(s, axis=-1, keepdims=True)
            p = jnp.exp(s - mx)
            pn = (p * (1.0 / jnp.sum(p, axis=-1, keepdims=True))).astype(BF16)
            o_ref[:, col] = jnp.dot(pn, v_ref[:, col], preferred_element_type=F32).astype(BF16)

    whole = pl.BlockSpec((MEM_LEN, kw), lambda i: (0, 0))
    return pl.pallas_call(
        body, name=name, grid=(s_len // tm,),
        in_specs=[pl.BlockSpec((tm, kw), lambda i: (i, cidx)), whole, whole, pl.BlockSpec((1, M_HEAD_DIM), lambda i: (0, 0))],
        out_specs=pl.BlockSpec((tm, kw), lambda i: (i, 0)),
        out_shape=jax.ShapeDtypeStruct((s_len, kw), BF16), compiler_params=_cparams(1),
    )(proj, mk, mv, q_gain)


def _mem_attn_bwd(proj, cidx, mk, mv, q_gain, do, name):
    s_len = proj.shape[0]
    kw = M_HEADS * M_HEAD_DIM
    tm = 512
    scale = M_HEAD_DIM ** -0.5

    def body(q_ref, k_ref, v_ref, g_ref, do_ref, dq_ref, dk_ref, dv_ref, dg_ref):
        @pl.when(pl.program_id(0) == 0)
        def _():
            dk_ref[...] = jnp.zeros_like(dk_ref)
            dv_ref[...] = jnp.zeros_like(dv_ref)
            dg_ref[...] = jnp.zeros_like(dg_ref)

        for h in range(M_HEADS):
            col = slice(h * M_HEAD_DIM, (h + 1) * M_HEAD_DIM)
            t = q_ref[:, col]
            rs = lax.rsqrt(jnp.mean(t * t, axis=-1, keepdims=True) + EPS)
            that = t * rs
            qn = (that * g_ref[...]).astype(BF16)
            kh, vh = k_ref[:, col], v_ref[:, col]
            dob = do_ref[:, col].astype(BF16)
            s = lax.dot_general(qn, kh, (((1,), (1,)), ((), ())), preferred_element_type=F32) * scale
            mx = jnp.max(s, axis=-1, keepdims=True)
            p = jnp.exp(s - mx)
            p = p * (1.0 / jnp.sum(p, axis=-1, keepdims=True))
            dp = lax.dot_general(dob, vh, (((1,), (1,)), ((), ())), preferred_element_type=F32)
            ds = (p * (dp - jnp.sum(p * dp, axis=-1, keepdims=True)) * scale).astype(BF16)
            dqn = jnp.dot(ds, kh, preferred_element_type=F32)
            dk_ref[:, col] += lax.dot_general(ds, qn, (((0,), (0,)), ((), ())), preferred_element_type=F32)
            dv_ref[:, col] += lax.dot_general(p.astype(BF16), dob, (((0,), (0,)), ((), ())), preferred_element_type=F32)
            dg_ref[...] += jnp.sum(dqn * that, axis=0, keepdims=True)
            dthat = dqn * g_ref[...]
            dq_ref[:, col] = (rs * (dthat - that * jnp.mean(dthat * that, axis=-1, keepdims=True))).astype(BF16)

    whole = pl.BlockSpec((MEM_LEN, kw), lambda i: (0, 0))
    vec = pl.BlockSpec((1, M_HEAD_DIM), lambda i: (0, 0))
    row = pl.BlockSpec((tm, kw), lambda i: (i, 0))
    return pl.pallas_call(
        body, name=name, grid=(s_len // tm,),
        in_specs=[pl.BlockSpec((tm, kw), lambda i: (i, cidx)), whole, whole, vec, row],
        out_specs=(row, whole, whole, vec),
        out_shape=(jax.ShapeDtypeStruct((s_len, kw), BF16), jax.ShapeDtypeStruct((MEM_LEN, kw), F32),
                   jax.ShapeDtypeStruct((MEM_LEN, kw), F32), jax.ShapeDtypeStruct((1, M_HEAD_DIM), F32)),
        compiler_params=_cparams(1),
    )(proj, mk, mv, q_gain, do)


def _project_merge(outs, w_outs, gates, w_out, x, name):
    s_len = gates.shape[0]
    d = w_outs[0].shape[1]
    tm = 512

    def body(oa_ref, ob_ref, om_ref, wa_ref, wb_ref, wm_ref, g_ref, wo_ref, x_ref,
             pa_ref, pb_ref, pm_ref, merged_ref, x1_ref):
        merged = None
        for k, (o_ref, w_ref, p_ref) in enumerate(((oa_ref, wa_ref, pa_ref), (ob_ref, wb_ref, pb_ref), (om_ref, wm_ref, pm_ref))):
            p = jnp.dot(o_ref[...].astype(BF16), w_ref[...], preferred_element_type=F32).astype(BF16)
            p_ref[...] = p
            t = g_ref[:, k * d:(k + 1) * d].astype(F32) * p.astype(F32)
            merged = t if merged is None else merged + t
        merged = merged.astype(BF16)
        merged_ref[...] = merged
        x1_ref[...] = x_ref[...] + jnp.dot(merged, wo_ref[...], preferred_element_type=F32)

    row = pl.BlockSpec((tm, d), lambda i: (i, 0))
    shp = jax.ShapeDtypeStruct((s_len, d), BF16)
    in_specs = [pl.BlockSpec((tm, o.shape[1]), lambda i: (i, 0)) for o in outs]
    in_specs += [_resident(w.shape, lambda i: (0, 0)) for w in w_outs]
    in_specs += [pl.BlockSpec((tm, 3 * d), lambda i: (i, 0)), _resident(w_out.shape, lambda i: (0, 0)), row]
    return pl.pallas_call(
        body, name=name, grid=(s_len // tm,), in_specs=in_specs, out_specs=(row, row, row, row, row),
        out_shape=(shp, shp, shp, shp, jax.ShapeDtypeStruct((s_len, d), F32)), compiler_params=_cparams(1),
    )(*outs, *w_outs, gates, w_out, x)


def _project_merge_bwd(dx1, w_out, gates, pa, pb, pm, name):
    s_len, d = pa.shape
    tm = 512

    def body(dx_ref, w_ref, g_ref, a_ref, b_ref, m_ref, da_ref, db_ref, dmm_ref, dg_ref, dbg_ref):
        @pl.when(pl.program_id(0) == 0)
        def _():
            dbg_ref[...] = jnp.zeros_like(dbg_ref)
        dm = lax.dot_general(dx_ref[...], w_ref[...], (((1,), (1,)), ((), ())), preferred_element_type=F32)
        for k, (p_ref, dp_ref) in enumerate(((a_ref, da_ref), (b_ref, db_ref), (m_ref, dmm_ref))):
            col = slice(k * d, (k + 1) * d)
            g = g_ref[:, col].astype(F32)
            dp_ref[...] = (dm * g).astype(BF16)
            dpre = (dm * p_ref[...].astype(F32)) * (g * (1.0 - g))
            dbg_ref[:, col] += jnp.sum(dpre, axis=0, keepdims=True)
            dg_ref[:, col] = dpre.astype(BF16)

    row = pl.BlockSpec((tm, d), lambda i: (i, 0))
    wide = pl.BlockSpec((tm, 3 * d), lambda i: (i, 0))
    shp = jax.ShapeDtypeStruct((s_len, d), BF16)
    return pl.pallas_call(
        body, name=name, grid=(s_len // tm,), in_specs=[row, _resident(w_out.shape, lambda i: (0, 0)), wide, row, row, row],
        out_specs=(row, row, row, wide, pl.BlockSpec((1, 3 * d), lambda i: (0, 0))),
        out_shape=(shp, shp, shp, jax.ShapeDtypeStruct((s_len, 3 * d), BF16), jax.ShapeDtypeStruct((1, 3 * d), F32)),
        compiler_params=_cparams(1),
    )(dx1, w_out, gates, pa, pb, pm)


CONV_CHUNK = 256


def _pick_row(tile, j):
    row = lax.broadcasted_iota(jnp.int32, tile.shape, 0)
    return jnp.sum(jnp.where(row == j, tile, jnp.zeros_like(tile)), axis=0, keepdims=True)


def _rows_before(ref, start, k):
    cur = ref[pl.ds(start, CONV_CHUNK), :].astype(F32)
    prev = ref[pl.ds(pl.multiple_of(jnp.maximum(start - 16, 0), 16), 16), :].astype(F32)
    prev = jnp.where(start > 0, prev, jnp.zeros_like(prev))
    rolled = pltpu.roll(cur, k, 0)
    row = lax.broadcasted_iota(jnp.int32, cur.shape, 0)
    for j in range(k):
        rolled = jnp.where(row == j, _pick_row(prev, 16 - k + j), rolled)
    return rolled


def _rows_after(ref, start, k):
    cur = ref[pl.ds(start, CONV_CHUNK), :]
    nxt = ref[pl.ds(pl.multiple_of(start + CONV_CHUNK, 8), 8), :]
    rolled = pltpu.roll(cur, CONV_CHUNK - k, 0)
    row = lax.broadcasted_iota(jnp.int32, cur.shape, 0)
    for j in range(k):
        rolled = jnp.where(row == CONV_CHUNK - k + j, _pick_row(nxt, j), rolled)
    return rolled


def _conv_pre(u_ref, w_ref, b_ref, start):
    u2 = _rows_before(u_ref, start, 2)
    u1 = _rows_before(u_ref, start, 1)
    u0 = u_ref[pl.ds(start, CONV_CHUNK), :].astype(F32)
    c = ((b_ref[...] + w_ref[0:1, :] * u2) + w_ref[1:2, :] * u1) + w_ref[2:3, :] * u0
    return c, (u2, u1, u0)


def _norm_up_conv_glu(x, gain, w_up, conv_w, conv_b, name):
    s_len, d = x.shape
    tm, tn = 512, 2 * LANES
    nblk = D_FF // tn

    def body(x_ref, g_ref, w_ref, cw_ref, cb_ref, ht_ref, r_ref, u_ref, act_ref, halo):
        @pl.when(pl.program_id(0) == 0)
        def _():
            halo[...] = jnp.zeros_like(halo)
        xv = x_ref[...]
        r = lax.rsqrt(jnp.mean(xv * xv, axis=-1, keepdims=True) + EPS)
        hv = ((xv * r) * g_ref[...]).astype(BF16)
        ht_ref[...] = hv.T
        r_ref[...] = r
        row = lax.broadcasted_iota(jnp.int32, (tm, tn), 0)
        for j in range(nblk):
            conv = []
            for half in range(2):
                cb = half * nblk + j
                cols = slice(cb * tn, (cb + 1) * tn)
                ub = jnp.dot(hv, w_ref[:, cols], preferred_element_type=F32).astype(BF16)
                u_ref[:, cols] = ub
                u0 = ub.astype(F32)
                prev = halo[cb]
                u1 = jnp.where(row == 0, _pick_row(prev, 7), pltpu.roll(u0, 1, 0))
                u2 = pltpu.roll(u0, 2, 0)
                u2 = jnp.where(row == 0, _pick_row(prev, 6), jnp.where(row == 1, _pick_row(prev, 7), u2))
                halo[cb] = u0[tm - 8:tm, :]
                conv.append(((cb_ref[:, cols] + cw_ref[0:1, cols] * u2) + cw_ref[1:2, cols] * u1)
                            + cw_ref[2:3, cols] * u0)
            act_ref[:, j * tn:(j + 1) * tn] = ((conv[0] * _sigmoid(conv[0])) * conv[1]).astype(BF16)

    return pl.pallas_call(
        body, name=name, grid=(s_len // tm,),
        in_specs=[pl.BlockSpec((tm, d), lambda i: (i, 0)), _resident((1, d), lambda i: (0, 0)),
                  _resident((d, 2 * D_FF), lambda i: (0, 0)),
                  _resident((3, 2 * D_FF), lambda i: (0, 0)), _resident((1, 2 * D_FF), lambda i: (0, 0))],
        out_specs=(pl.BlockSpec((d, tm), lambda i: (0, i)), pl.BlockSpec((tm, 1), lambda i: (i, 0)),
                   pl.BlockSpec((tm, 2 * D_FF), lambda i: (i, 0)), pl.BlockSpec((tm, D_FF), lambda i: (i, 0))),
        out_shape=(jax.ShapeDtypeStruct((d, s_len), BF16), jax.ShapeDtypeStruct((s_len, 1), F32),
                   jax.ShapeDtypeStruct((s_len, 2 * D_FF), BF16), jax.ShapeDtypeStruct((s_len, D_FF), BF16)),
        scratch_shapes=[pltpu.VMEM((2 * nblk, 8, tn), F32)], compiler_params=_cparams(1),
    )(x, gain, w_up, conv_w, conv_b)


def _conv_glu_bwd(dact, u, conv_w, conv_b, name):
    s_len = u.shape[0]
    nblk = D_FF // LANES
    n_chunks = s_len // CONV_CHUNK

    def body(da_ref, ua_ref, ug_ref, wa_ref, wg_ref, ba_ref, bg_ref,
             dua_ref, dug_ref, dwa_ref, dwg_ref, dba_ref, dbg_ref, sa, sg):
        sa[pl.ds(s_len, 8), :] = jnp.zeros((8, LANES), F32)
        sg[pl.ds(s_len, 8), :] = jnp.zeros((8, LANES), F32)
        zero = jnp.zeros((1, LANES), F32)

        def chunk1(ci, carry):
            start = pl.multiple_of(ci * CONV_CHUNK, CONV_CHUNK)
            ca, ua = _conv_pre(ua_ref, wa_ref, ba_ref, start)
            cg, ug = _conv_pre(ug_ref, wg_ref, bg_ref, start)
            dact_v = da_ref[pl.ds(start, CONV_CHUNK), :].astype(F32)
            sig = _sigmoid(ca)
            dcg = dact_v * (ca * sig)
            dca = (dact_v * cg) * (sig * (1.0 + ca * (1.0 - sig)))
            sa[pl.ds(start, CONV_CHUNK), :] = dca
            sg[pl.ds(start, CONV_CHUNK), :] = dcg
            out = [carry[0] + jnp.sum(dca, axis=0, keepdims=True), carry[1] + jnp.sum(dcg, axis=0, keepdims=True)]
            for j in range(3):
                out.append(carry[2 + j] + jnp.sum(dca * ua[j], axis=0, keepdims=True))
            for j in range(3):
                out.append(carry[5 + j] + jnp.sum(dcg * ug[j], axis=0, keepdims=True))
            return tuple(out)

        acc = lax.fori_loop(0, n_chunks, chunk1, (zero,) * 8)
        dba_ref[...] = acc[0]
        dbg_ref[...] = acc[1]
        for j in range(3):
            dwa_ref[j:j + 1, :] = acc[2 + j]
            dwg_ref[j:j + 1, :] = acc[5 + j]

        def chunk2(ci, carry):
            start = pl.multiple_of(ci * CONV_CHUNK, CONV_CHUNK)
            for s_ref, w_ref, o_ref in ((sa, wa_ref, dua_ref), (sg, wg_ref, dug_ref)):
                d0 = s_ref[pl.ds(start, CONV_CHUNK), :]
                d1 = _rows_after(s_ref, start, 1)
                d2 = _rows_after(s_ref, start, 2)
                o_ref[pl.ds(start, CONV_CHUNK), :] = (w_ref[2:3, :] * d0 + w_ref[1:2, :] * d1
                                                      + w_ref[0:1, :] * d2).astype(BF16)
            return carry
        lax.fori_loop(0, n_chunks, chunk2, 0)

    def col(rows, off):
        return pl.BlockSpec((rows, LANES), lambda j: (0, off + j))

    big = jax.ShapeDtypeStruct((s_len, D_FF), BF16)
    return pl.pallas_call(
        body, name=name, grid=(nblk,),
        in_specs=[col(s_len, 0), col(s_len, 0), col(s_len, nblk), col(3, 0), col(3, nblk), col(1, 0), col(1, nblk)],
        out_specs=(col(s_len, 0), col(s_len, 0), col(3, 0), col(3, 0), col(1, 0), col(1, 0)),
        out_shape=(big, big, jax.ShapeDtypeStruct((3, D_FF), F32), jax.ShapeDtypeStruct((3, D_FF), F32),
                   jax.ShapeDtypeStruct((1, D_FF), F32), jax.ShapeDtypeStruct((1, D_FF), F32)),
        scratch_shapes=[pltpu.VMEM((s_len + 8, LANES), F32)] * 2, compiler_params=_cparams(1),
    )(dact, u, u, conv_w, conv_w, conv_b, conv_b)


def _rope_tables(positions):
    half = ROPE_DIMS // 2
    freqs = jnp.exp(jnp.arange(half, dtype=F32) * (-2.0 * math.log(ROPE_THETA) / ROPE_DIMS))
    ang = positions.reshape(-1).astype(F32)[:, None] * freqs
    cos, sin = jnp.cos(ang), jnp.sin(ang)
    n = ang.shape[0]
    zeros = lambda w: jnp.zeros((n, w), F32)
    c = jnp.concatenate([cos, cos, jnp.ones((n, HEAD_DIM - ROPE_DIMS), F32)], axis=1)
    s1 = jnp.concatenate([-sin, zeros(HEAD_DIM - half)], axis=1)
    s2 = jnp.concatenate([zeros(half), sin, zeros(HEAD_DIM - ROPE_DIMS)], axis=1)
    return tuple(jnp.tile(t, (1, 2)) for t in (c, s1, s2))


def _two(v):
    return jnp.tile(v.reshape(1, HEAD_DIM), (1, 2))


def _fold_heads(g):
    return g[0, :HEAD_DIM] + g[0, HEAD_DIM:]


MIX_WEIGHTS = ('w_gate', 'w_mem_kv', 'w_o_a', 'w_o_b', 'w_o_m', 'w_out')
FFN_WEIGHTS = ('w_up', 'conv_w', 'w_down')


def _device_step(x, mem, positions, target, w, hooks=None):
    tabs = _rope_tables(positions)
    dils = tuple(d for _, d in A_GROUPS)
    grads = {}
    w = dict(w)

    h, h_t, r1 = _rms_fwd(x, w['attn_norm'], "rms1")
    if hooks is not None:
        w.update(hooks.weights('in', [h, *tabs]))
    proj = _mm_rows([(h, w['w_in'], 0)], "mm_in")

    qkv_a, o_g, lse_g = [], [], []
    for gi, (window, d) in enumerate(A_GROUPS):
        gq, gk = _two(w['a_q_norm'][gi]), _two(w['a_k_norm'][gi])
        qkv = _qk_prep(proj, 6 * gi, d, False, gq, gk, tabs, f"qk_prep_a{gi}")
        o, lse = _band_fwd(qkv, 2 * d, window // d, None, f"band_fwd_a{gi}")
        qkv_a.append(qkv)
        o_g.append(o)
        lse_g.append(lse)
    o_a, lse_a = _merge_groups(o_g, lse_g, dils, "merge_a")
    if hooks is not None:
        w.update(hooks.weights('mix', [o_a]))

    gbq, gbk = _two(w['b_q_norm']), _two(w['b_k_norm'])
    sinks = jnp.repeat(w['b_sinks'].reshape(4, 2), HEAD_DIM, axis=1).reshape(4, 1, LANES)
    qkv_b = _qk_prep(proj, 18, 1, True, gbq, gbk, tabs, "qk_prep_b")
    o_b, lse_b = _band_fwd(qkv_b, 4, B_WINDOW - 1, sinks, "band_fwd_b")

    gates = _mm_rows([(h, w['w_gate'], 0)], "mm_gate", bias=w['b_gate'], sigmoid=True, out_dtypes=(BF16,))
    mk, mv = _mem_kv(mem, w['mem_norm'], w['w_mem_kv'], w['m_k_norm'], "mem_kv")
    o_m = _mem_attn_fwd(proj, 6, mk, mv, w['m_q_norm'], "mem_attn")

    pa, pb, pm, merged, x1 = _project_merge((o_a, o_b, o_m), (w['w_o_a'], w['w_o_b'], w['w_o_m']), gates, w['w_out'], x,
                                            "project_merge")

    if hooks is not None:
        w.update(hooks.weights('ffn', [x1]))
    h2_t, r2, u, act = _norm_up_conv_glu(x1, w['ffn_norm'], w['w_up'], w['conv_w'], w['conv_b'], "norm_up_conv_glu")
    dy, dy_b, dact, loss = _mm_rows([(act, w['w_down'], 0)], "mm_down", res=x1, loss_target=target)

    grads['w_down'] = _mm_tn(act, dy_b, "mm_dw_down")
    du_a, du_g, dcw_a, dcw_g, dcb_a, dcb_g = _conv_glu_bwd(dact, u, w['conv_w'], w['conv_b'], "conv_glu_bwd")
    grads['conv_w'] = jnp.concatenate([dcw_a, dcw_g], axis=1)
    grads['conv_b'] = jnp.concatenate([dcb_a, dcb_g], axis=1)
    grads['w_up'] = jnp.concatenate([_mm_cols(h2_t, du_a, "mm_dw_up_a"), _mm_cols(h2_t, du_g, "mm_dw_up_g")], axis=1)
    ffn_gain = w['ffn_norm']
    if hooks is not None:
        ffn_gain = ffn_gain + hooks.grads('ffn', grads)[0:1, 0:1]
    dx1, dx1_b, grads['ffn_norm'] = _mm_rows([(du_a, w['w_up'], 0), (du_g, w['w_up'], 1)], "mm_d_h2", nt=True,
                                             rms_bwd=(x1, r2, ffn_gain, dy), out_dtypes=(F32, BF16))

    grads['w_out'] = _mm_tn(merged, dx1_b, "mm_dw_out")
    dpa, dpb, dpm, dgpre, grads['b_gate'] = _project_merge_bwd(dx1_b, w['w_out'], gates, pa, pb, pm,
                                                               "project_merge_bwd")
    do_a, do_b, do_m = _mm_rows_each([(dpa, w['w_o_a']), (dpb, w['w_o_b']), (dpm, w['w_o_m'])], "mm_d_o")
    grads['w_o_a'], grads['w_o_b'], grads['w_o_m'] = _mm_tn_each([(o_a, dpa), (o_b, dpb), (o_m, dpm)], "mm_dw_o")
    grads['w_gate'] = _mm_cols(h_t, dgpre, "mm_dw_gate")
    dq_m, dmk, dmv, grads['m_q_norm'] = _mem_attn_bwd(proj, 6, mk, mv, w['m_q_norm'], do_m, "mem_attn_bwd")
    grads['w_mem_kv'], grads['mem_norm'], grads['m_k_norm'] = _mem_kv_bwd(
        mem, w['mem_norm'], w['w_mem_kv'], w['m_k_norm'], dmk, dmv, "mem_kv_bwd")
    a_gain = w['a_q_norm']
    if hooks is not None:
        a_gain = a_gain + hooks.grads('mix', grads)[0:1, 0:1]

    prep = _bwd_prep(do_a, o_a, lse_a, dils, None, "bwd_prep_a")
    dproj, dgq_a, dgk_a = [], [], []
    for gi, (window, d) in enumerate(A_GROUPS):
        gq, gk = _two(a_gain[gi]), _two(w['a_k_norm'][gi])
        dqkv = _band_bwd(qkv_a[gi], prep[3 * gi], prep[3 * gi + 1], prep[3 * gi + 2], 2 * d, window // d,
                         f"band_bwd_a{gi}")
        dp, dgq, dgk = _qk_prep_bwd(dqkv, proj, 6 * gi, d, False, gq, gk, tabs, f"qk_prep_bwd_a{gi}")
        dproj.append(dp)
        dgq_a.append(_fold_heads(dgq))
        dgk_a.append(_fold_heads(dgk))
    grads['a_q_norm'] = jnp.stack(dgq_a)
    grads['a_k_norm'] = jnp.stack(dgk_a)

    do_bu, lse_bu, delta_bu, dsink = _bwd_prep(do_b, o_b, lse_b, (1,), sinks, "bwd_prep_b")
    dqkv = _band_bwd(qkv_b, do_bu, lse_bu, delta_bu, 4, B_WINDOW - 1, "band_bwd_b")
    dp_b, dgq, dgk = _qk_prep_bwd(dqkv, proj, 18, 1, True, gbq, gbk, tabs, "qk_prep_bwd_b")
    dproj.append(dp_b)
    grads['b_q_norm'] = _fold_heads(dgq)
    grads['b_k_norm'] = _fold_heads(dgk)
    grads['b_sinks'] = jnp.stack([dsink[:, 0, 0], dsink[:, 0, HEAD_DIM]], axis=1).reshape(8)

    dproj.append(dq_m)

    cols = (0, 1, 2, 3, 6)
    grads['w_in'] = _mm_rows_cat(h_t, dproj, "mm_dw_in")
    attn_gain = w['attn_norm']
    if hooks is not None:
        attn_gain = attn_gain + hooks.grads('in', grads)[0:1, 0:1]
    grad_x, grads['attn_norm'] = _mm_rows(
        [(dp, w['w_in'], c) for dp, c in zip(dproj, cols)] + [(dgpre, w['w_gate'], 0)], "mm_d_h", nt=True,
        rms_bwd=(x, r1, attn_gain, dx1))
    return loss, grad_x, grads


def _coords():
    return lax.axis_index("x"), lax.axis_index("y"), lax.axis_index("c")


def _slot(p):
    return 4 * p[0] + 2 * p[1] + p[2]


ALL_PEERS = tuple(range(1, N_DEV))
CHIP_PEERS = (1, 4, 2, 6)
OTHER_CHIPS = (4, 2, 6)


def _peers(me, masks=ALL_PEERS):
    x, y, c = me
    return [(1 - x if mask & 4 else x, 1 - y if mask & 2 else y, 1 - c if mask & 1 else c) for mask in masks]


HBM_SPEC = pl.BlockSpec(memory_space=pltpu.HBM)


SEM_SPEC = pl.BlockSpec(memory_space=pltpu.SEMAPHORE)
SIDE_EFFECT = pltpu.SideEffectType.DATAFLOW_SIDE_EFFECTING


def _exchange_start(blocks, name, gather=False, masks=ALL_PEERS, after=None):
    n = len(blocks)
    n_peers = len(masks)
    n_in = 2 * n + (0 if after is None else 1)

    def body(*refs):
        ins, lands = refs[:n], refs[n:2 * n]
        send_sems, recv_sems = refs[n_in], refs[n_in + 1]
        token = refs[-1]
        me = _coords()
        peers = _peers(me, masks)
        for a in range(n):
            for k in range(n_peers):
                pltpu.make_async_remote_copy(
                    src_ref=ins[a] if gather else ins[a].at[_slot(peers[k])], dst_ref=lands[a].at[_slot(me)],
                    send_sem=send_sems.at[a * n_peers + k], recv_sem=recv_sems.at[a * n_peers + k],
                    device_id=peers[k], device_id_type=MESH).start()
        token[...] = jnp.zeros_like(token)

    land_shapes = [((N_DEV,) + b.shape) if gather else b.shape for b in blocks]
    hbm_in = [pltpu.HBM(b.shape, b.dtype) for b in blocks]
    hbm_land = [pltpu.HBM(s, b.dtype) for s, b in zip(land_shapes, blocks)]
    sems = pltpu.SemaphoreType.DMA((n * n_peers,))
    ins = [pltpu.with_memory_space_constraint(b, pltpu.HBM) for b in blocks]
    lands = [pltpu.with_memory_space_constraint(lax.empty(s, b.dtype), pltpu.HBM) for s, b in zip(land_shapes, blocks)]
    return pl.pallas_call(
        body, name=name, out_shape=(sems, sems, *hbm_in, *hbm_land, jax.ShapeDtypeStruct((8, LANES), F32)),
        in_specs=[HBM_SPEC] * (2 * n) + ([] if after is None else [pl.BlockSpec(memory_space=pl.ANY)]),
        out_specs=(SEM_SPEC, SEM_SPEC, *([HBM_SPEC] * (2 * n)), pl.BlockSpec(memory_space=pltpu.VMEM)),
        input_output_aliases={i: 2 + i for i in range(2 * n)},
        compiler_params=pltpu.CompilerParams(has_side_effects=SIDE_EFFECT),
    )(*ins, *lands, *([] if after is None else [after]))


def _exchange_wait(started, after, name, gather=False, masks=ALL_PEERS):
    n = (len(started) - 3) // 2
    n_peers = len(masks)
    send_sems, recv_sems = started[0], started[1]
    thru = started[2:2 + 2 * n]

    def body(*refs):
        ins, lands = refs[:n], refs[n:2 * n]
        send_ref, recv_ref = refs[2 * n], refs[2 * n + 1]
        me = _coords()
        peers = _peers(me, masks)
        for a in range(n):
            for k in range(n_peers):
                cp = pltpu.make_async_remote_copy(
                    src_ref=ins[a] if gather else ins[a].at[_slot(peers[k])], dst_ref=lands[a].at[_slot(peers[k])],
                    send_sem=send_ref.at[a * n_peers + k], recv_sem=recv_ref.at[a * n_peers + k],
                    device_id=peers[k], device_id_type=MESH)
                cp.wait_send()
                cp.wait_recv()

    hbm = [pltpu.HBM(t.shape, t.dtype) for t in thru]
    res = pl.pallas_call(
        body, name=name, out_shape=tuple(hbm),
        in_specs=[HBM_SPEC] * (2 * n) + [SEM_SPEC, SEM_SPEC] + [pl.BlockSpec(memory_space=pl.ANY)] * len(after),
        out_specs=tuple([HBM_SPEC] * (2 * n)), input_output_aliases={i: i for i in range(2 * n)},
        compiler_params=pltpu.CompilerParams(has_side_effects=SIDE_EFFECT),
    )(*thru, send_sems, recv_sems, *after)
    return res[n:]


def _sibling_forward(arrays, name):
    n = len(arrays)
    n_fwd = len(OTHER_CHIPS)

    def body(*refs):
        bufs = refs[n:2 * n]
        token, send_sems, recv_sems = refs[2 * n:]
        token[...] = jnp.zeros_like(token)
        x, y, c = _coords()
        sibling = (x, y, 1 - c)
        mine = _peers((x, y, c), OTHER_CHIPS)
        theirs = _peers(sibling, OTHER_CHIPS)

        def copy(a, k, block):
            rows = bufs[a].at[_slot(block)]
            return pltpu.make_async_remote_copy(
                src_ref=rows, dst_ref=rows, send_sem=send_sems.at[a * n_fwd + k], recv_sem=recv_sems.at[a * n_fwd + k],
                device_id=sibling, device_id_type=MESH)

        sends = [copy(a, k, mine[k]) for a in range(n) for k in range(n_fwd)]
        for cp in sends:
            cp.start()
        for a in range(n):
            for k in range(n_fwd):
                copy(a, k, theirs[k]).wait_recv()
        for cp in sends:
            cp.wait_send()

    res = pl.pallas_call(
        body, name=name, in_specs=[HBM_SPEC] * n,
        out_specs=tuple([HBM_SPEC] * n + [pl.BlockSpec(memory_space=pltpu.VMEM)]),
        out_shape=tuple([jax.ShapeDtypeStruct(a.shape, a.dtype) for a in arrays] + [jax.ShapeDtypeStruct((8, LANES), F32)]),
        input_output_aliases={i: i for i in range(n)},
        scratch_shapes=[pltpu.SemaphoreType.DMA((n * n_fwd,)), pltpu.SemaphoreType.DMA((n * n_fwd,))],
    )(*arrays)
    return res[:n], res[n]


def _all_sum(p, name):
    def body(p_ref, o_ref, recv, send_sems, recv_sems):
        me = _coords()
        peers = _peers(me)
        recv[_slot(me)] = p_ref[...]

        def copy(k, landing):
            return pltpu.make_async_remote_copy(
                src_ref=p_ref, dst_ref=recv.at[_slot(landing)], send_sem=send_sems.at[k], recv_sem=recv_sems.at[k],
                device_id=peers[k], device_id_type=MESH)

        sends = [copy(k, me) for k in range(N_DEV - 1)]
        for cp in sends:
            cp.start()
        for k in range(N_DEV - 1):
            copy(k, peers[k]).wait_recv()
        for cp in sends:
            cp.wait_send()
        acc = recv[0]
        for s in range(1, N_DEV):
            acc = acc + recv[s]
        o_ref[...] = acc

    vmem = pl.BlockSpec(memory_space=pltpu.VMEM)
    return pl.pallas_call(
        body, name=name, in_specs=[vmem], out_specs=vmem, out_shape=jax.ShapeDtypeStruct(p.shape, F32),
        scratch_shapes=[pltpu.VMEM((N_DEV,) + p.shape, F32), pltpu.SemaphoreType.DMA((N_DEV - 1,)),
                        pltpu.SemaphoreType.DMA((N_DEV - 1,))],
    )(p)


def _adam(w, g, m, v):
    m2 = ADAM_B1 * m + (1.0 - ADAM_B1) * g
    v2 = ADAM_B2 * v + (1.0 - ADAM_B2) * (g * g)
    m_hat = m2 / (1.0 - ADAM_B1 ** ADAM_STEP)
    v_hat = v2 / (1.0 - ADAM_B2 ** ADAM_STEP)
    delta = -ADAM_LR * (m_hat / (jnp.sqrt(v_hat) + ADAM_EPS) + ADAM_WD * w)
    return delta, m2, v2


def _row_tile(rows, cols):
    best = rows
    for t in range(16, rows, 16):
        if rows % t == 0 and t * cols * 4 <= (1 << 20):
            best = t
    return best


def _adam_reduce(parts, w, m, v, name):
    rows, cols = w.shape
    tr = _row_tile(rows, cols)

    def body(p_ref, w_ref, m_ref, v_ref, g_ref, d_ref, m2_ref, v2_ref):
        g = p_ref[0].astype(F32)
        for s in range(1, N_DEV):
            g = g + p_ref[s].astype(F32)
        g_ref[...] = g
        d_ref[...], m2_ref[...], v2_ref[...] = _adam(w_ref[...], g, m_ref[...], v_ref[...])

    blk = pl.BlockSpec((tr, cols), lambda i: (i, 0))
    shp = jax.ShapeDtypeStruct((rows, cols), F32)
    return pl.pallas_call(
        body, name=name, grid=(rows // tr,),
        in_specs=[pl.BlockSpec((N_DEV, tr, cols), lambda i: (0, i, 0)), blk, blk, blk],
        out_specs=(blk,) * 4, out_shape=(shp,) * 4, compiler_params=_cparams(1),
    )(parts, w, m, v)


PACK_COLS = 1024
PACK = {'attn_norm': (0, 1, 1024), 'mem_norm': (1, 1, 1024), 'ffn_norm': (2, 1, 1024), 'b_gate': (3, 3, 1024),
        'conv_b': (6, 6, 1024), 'a_q_norm': (12, 3, 64), 'a_k_norm': (15, 3, 64), 'b_q_norm': (18, 1, 64),
        'b_k_norm': (19, 1, 64), 'm_q_norm': (20, 1, 128), 'm_k_norm': (21, 1, 128), 'b_sinks': (22, 1, 8)}
PACK_LOSS_ROW = 23
PACK_ROWS = 24


def _pack_pieces(name, width):
    r0, nr, lanes = PACK[name]
    out = []
    for j in range(nr):
        if lanes == PACK_COLS:
            w = min(PACK_COLS, width - j * PACK_COLS)
            out.append((r0 + j, slice(0, 1), slice(j * PACK_COLS, j * PACK_COLS + w), w))
        else:
            out.append((r0 + j, slice(j, j + 1), slice(0, lanes), lanes))
    return out


def _pack_small(grads, loss_tile, name):
    names = list(PACK)

    def body(*refs):
        o_ref = refs[-1]
        o_ref[...] = jnp.zeros_like(o_ref)
        for k, nm in enumerate(names):
            for row, rs, ls, w in _pack_pieces(nm, refs[k].shape[1]):
                o_ref[row:row + 1, 0:w] = refs[k][rs, ls]
        o_ref[PACK_LOSS_ROW:PACK_LOSS_ROW + 1, 0:1] = refs[len(names)][0:1, 0:1]

    vmem = pl.BlockSpec(memory_space=pltpu.VMEM)
    args = [grads[nm] for nm in names] + [loss_tile]
    return pl.pallas_call(body, name=name, in_specs=[vmem] * len(args), out_specs=vmem,
                          out_shape=jax.ShapeDtypeStruct((PACK_ROWS, PACK_COLS), F32))(*args)


def _adam_small(gsum, ws, ms, vs, name):
    names = list(PACK)
    n = len(names)

    def body(*refs):
        g_ref = refs[0]
        w_refs, m_refs, v_refs = refs[1:1 + n], refs[1 + n:1 + 2 * n], refs[1 + 2 * n:1 + 3 * n]
        outs = refs[1 + 3 * n:]
        outs[0][...] = g_ref[PACK_LOSS_ROW:PACK_LOSS_ROW + 1, 0:1]
        for k, nm in enumerate(names):
            o_g, o_d, o_m, o_v = outs[1 + 4 * k:5 + 4 * k]
            for row, rs, ls, width in _pack_pieces(nm, w_refs[k].shape[1]):
                src = (rs, ls)
                g = g_ref[row:row + 1, 0:width]
                d, m2, v2 = _adam(w_refs[k][src], g, m_refs[k][src], v_refs[k][src])
                o_g[src] = g
                o_d[src] = d
                o_m[src] = m2
                o_v[src] = v2

    vmem = pl.BlockSpec(memory_space=pltpu.VMEM)
    shapes = [jax.ShapeDtypeStruct((1, 1), F32)]
    for nm in names:
        shapes += [jax.ShapeDtypeStruct(ws[nm].shape, F32)] * 4
    args = [gsum] + [ws[nm] for nm in names] + [ms[nm] for nm in names] + [vs[nm] for nm in names]
    return pl.pallas_call(
        body, name=name, in_specs=[vmem] * len(args), out_specs=tuple([vmem] * len(shapes)), out_shape=tuple(shapes),
    )(*args)


def _as2d(name, a):
    return a.reshape(a.shape[-2], a.shape[-1]) if a.ndim == 3 else a


def kernel(x, mem, positions, attn_norm, w_in, a_q_norm, a_k_norm, b_q_norm, b_k_norm, b_sinks, mem_norm, w_mem_kv, m_q_norm, m_k_norm, w_o_a, w_o_b, w_o_m, w_gate, b_gate, w_out, ffn_norm, w_up, conv_w, conv_b, w_down, loss_target, m_attn_norm, m_w_in, m_a_q_norm, m_a_k_norm, m_b_q_norm, m_b_k_norm, m_b_sinks, m_mem_norm, m_w_mem_kv, m_m_q_norm, m_m_k_norm, m_w_o_a, m_w_o_b, m_w_o_m, m_w_gate, m_b_gate, m_w_out, m_ffn_norm, m_w_up, m_conv_w, m_conv_b, m_w_down, v_attn_norm, v_w_in, v_a_q_norm, v_a_k_norm, v_b_q_norm, v_b_k_norm, v_b_sinks, v_mem_norm, v_w_mem_kv, v_m_q_norm, v_m_k_norm, v_w_o_a, v_w_o_b, v_w_o_m, v_w_gate, v_b_gate, v_w_out, v_ffn_norm, v_w_up, v_conv_w, v_conv_b, v_w_down):
    given = dict(attn_norm=attn_norm, w_in=w_in, a_q_norm=a_q_norm, a_k_norm=a_k_norm, b_q_norm=b_q_norm, b_k_norm=b_k_norm, b_sinks=b_sinks, mem_norm=mem_norm, w_mem_kv=w_mem_kv, m_q_norm=m_q_norm, m_k_norm=m_k_norm, w_o_a=w_o_a, w_o_b=w_o_b, w_o_m=w_o_m, w_gate=w_gate, b_gate=b_gate, w_out=w_out, ffn_norm=ffn_norm, w_up=w_up, conv_w=conv_w, conv_b=conv_b, w_down=w_down)
    mom1 = dict(attn_norm=m_attn_norm, w_in=m_w_in, a_q_norm=m_a_q_norm, a_k_norm=m_a_k_norm, b_q_norm=m_b_q_norm, b_k_norm=m_b_k_norm, b_sinks=m_b_sinks, mem_norm=m_mem_norm, w_mem_kv=m_w_mem_kv, m_q_norm=m_m_q_norm, m_k_norm=m_m_k_norm, w_o_a=m_w_o_a, w_o_b=m_w_o_b, w_o_m=m_w_o_m, w_gate=m_w_gate, b_gate=m_b_gate, w_out=m_w_out, ffn_norm=m_ffn_norm, w_up=m_w_up, conv_w=m_conv_w, conv_b=m_conv_b, w_down=m_w_down)
    mom2 = dict(attn_norm=v_attn_norm, w_in=v_w_in, a_q_norm=v_a_q_norm, a_k_norm=v_a_k_norm, b_q_norm=v_b_q_norm, b_k_norm=v_b_k_norm, b_sinks=v_b_sinks, mem_norm=v_mem_norm, w_mem_kv=v_w_mem_kv, m_q_norm=v_m_q_norm, m_k_norm=v_m_k_norm, w_o_a=v_w_o_a, w_o_b=v_w_o_b, w_o_m=v_w_o_m, w_gate=v_w_gate, b_gate=v_b_gate, w_out=v_w_out, ffn_norm=v_ffn_norm, w_up=v_w_up, conv_w=v_conv_w, conv_b=v_conv_b, w_down=v_w_down)

    big = list(BIG)
    stages = {'mix': list(MIX_WEIGHTS), 'ffn': list(FFN_WEIGHTS), 'in': ['w_in']}
    my_slot = _slot(_coords())

    def shard(n):
        return given[n][0] if n == 'conv_w' else given[n][0].astype(BF16)

    def whole(n, g):
        _, r, c = g.shape
        return g.reshape(N_DEV * r, c) if BIG[n] == 0 else g.transpose(1, 0, 2).reshape(r, N_DEV * c)

    def to_blocks(n, g):
        r, c = given[n].shape[1:]
        g = g.reshape(N_DEV, r, c) if BIG[n] == 0 else g.reshape(r, N_DEV, c).transpose(1, 0, 2)
        return g if n == 'conv_w' else g.astype(BF16)

    class Hooks:
        next_stage = {'in': 'mix', 'mix': 'ffn'}

        def __init__(self):
            self.coming, self.sent = {}, {}
            self.shards = {n: shard(n) for n in big}
            self.start_gather('in', None)

        def start_gather(self, stage, after):
            src = [self.shards[n] for n in stages[stage]]
            self.coming[stage] = _exchange_start(src, f"gather_{stage}_start", gather=True, masks=CHIP_PEERS,
                                                 after=after)

        def weights(self, stage, after):
            names = stages[stage]
            after = list(after)
            if stage == 'in':
                after += [self.shards[n] for n in stages['mix'] + stages['ffn']]
            landed = _exchange_wait(self.coming[stage], after, f"gather_{stage}_wait", gather=True, masks=CHIP_PEERS)
            landed, token = _sibling_forward(landed, f"gather_{stage}_forward")
            if stage in self.next_stage:
                self.start_gather(self.next_stage[stage], token)
            return {n: whole(n, lax.dynamic_update_slice_in_dim(land, self.shards[n][None], my_slot, axis=0))
                    for n, land in zip(names, landed)}

        def grads(self, stage, g):
            blocks = [to_blocks(n, g[n]) for n in stages[stage]]
            own = [lax.dynamic_slice_in_dim(b, my_slot, 1, axis=0) for b in blocks]
            self.sent[stage] = (_exchange_start(blocks, f"exchange_{stage}_start"), own)
            return self.sent[stage][0][-1]

        def parts(self, stage, after):
            started, own = self.sent[stage]
            landed = _exchange_wait(started, [after], f"exchange_{stage}_wait")
            return {n: lax.dynamic_update_slice_in_dim(land, o, my_slot, axis=0)
                    for n, land, o in zip(stages[stage], landed, own)}

    hooks = Hooks()
    w = {}
    for n in SMALL:
        w[n] = given[n]
    w['a_q_norm'], w['a_k_norm'] = given['a_q_norm'][0], given['a_k_norm'][0]
    w['b_q_norm'], w['b_k_norm'], w['b_sinks'] = given['b_q_norm'][0], given['b_k_norm'][0], given['b_sinks'][0]

    loss_tile, grad_x, grads = _device_step(x[0], mem[0], positions[0], loss_target[0], w, hooks)
    out = {}
    after = grad_x
    for stage in ('ffn', 'mix', 'in'):
        for n, p in hooks.parts(stage, after).items():
            res = _adam_reduce(p, given[n][0], mom1[n][0], mom2[n][0], f"adam_{n}")
            out[n] = tuple(t[None] for t in res)
            after = res[0]

    small = {n: grads[n] for n in PACK}
    small['b_q_norm'], small['b_k_norm'] = grads['b_q_norm'].reshape(1, -1), grads['b_k_norm'].reshape(1, -1)
    small['b_sinks'] = grads['b_sinks'].reshape(1, -1)
    gsum = _all_sum(_pack_small(small, loss_tile, "pack_small"), "sum_small")
    ws = {n: _as2d(n, given[n]) for n in PACK}
    ms = {n: _as2d(n, mom1[n]) for n in PACK}
    vs = {n: _as2d(n, mom2[n]) for n in PACK}
    res = _adam_small(gsum, ws, ms, vs, "adam_small")
    loss = res[0].reshape(())
    for k, n in enumerate(PACK):
        out[n] = tuple(t.reshape(given[n].shape) for t in res[1 + 4 * k:5 + 4 * k])

    outs = [loss, grad_x[None]]
    for field in range(4):
        outs += [out[n][field] for n in WEIGHTS]
    return tuple(outs)
```

```python
import functools
import math

import jax
import jax.numpy as jnp
from jax import lax
from jax.experimental import pallas as pl
from jax.experimental.pallas import tpu as pltpu

F32 = jnp.float32
BF16 = jnp.bfloat16

N_DEV = 8
HEAD_DIM = 64
A_GROUPS = ((128, 1), (512, 4), (2048, 16))
B_WINDOW = 128
M_HEADS = 4
M_HEAD_DIM = 128
MEM_LEN = 256
D_FF = 2816
ROPE_THETA = 500000.0
ROPE_DIMS = 16
BLOCK = 128
EPS = 1e-6
LANES = 128
BAND_Q_BLOCKS = 4
BAND_UNITS = 2

ADAM_LR = 0.001
ADAM_B1 = 0.9
ADAM_B2 = 0.999
ADAM_EPS = 1e-08
ADAM_WD = 0.01
ADAM_STEP = 10

VMEM_LIMIT_BYTES = 56 * 1024 * 1024
GRAD_DTYPE = BF16
MESH = pl.DeviceIdType.MESH

WEIGHTS = ['attn_norm', 'w_in', 'a_q_norm', 'a_k_norm', 'b_q_norm', 'b_k_norm', 'b_sinks', 'mem_norm',
           'w_mem_kv', 'm_q_norm', 'm_k_norm', 'w_o_a', 'w_o_b', 'w_o_m', 'w_gate', 'b_gate', 'w_out',
           'ffn_norm', 'w_up', 'conv_w', 'conv_b', 'w_down']
BIG = {'w_in': 1, 'w_mem_kv': 0, 'w_o_a': 1, 'w_o_b': 1, 'w_o_m': 1, 'w_gate': 1, 'w_out': 0, 'w_up': 1,
       'conv_w': 1, 'w_down': 0}
SMALL = [n for n in WEIGHTS if n not in BIG]


def _cparams(n_grid):
    return pltpu.CompilerParams(dimension_semantics=("arbitrary",) * n_grid, vmem_limit_bytes=VMEM_LIMIT_BYTES)


def _seg_matrix(width):
    shift = width.bit_length() - 1
    r = lax.shift_right_logical(lax.broadcasted_iota(jnp.int32, (LANES, LANES), 0), shift)
    c = lax.shift_right_logical(lax.broadcasted_iota(jnp.int32, (LANES, LANES), 1), shift)
    return jnp.where(r == c, 1.0, 0.0).astype(BF16)


def _seg_sum(x, seg):
    hi = x.astype(BF16)
    r1 = x - hi.astype(F32)
    mid = r1.astype(BF16)
    lo = (r1 - mid.astype(F32)).astype(BF16)
    dot = functools.partial(jnp.dot, preferred_element_type=F32)
    return dot(hi, seg) + dot(mid, seg) + dot(lo, seg)


def _rope(y, c, s1, s2):
    return y * c + pltpu.roll(y, LANES - ROPE_DIMS // 2, 1) * s1 + pltpu.roll(y, ROPE_DIMS // 2, 1) * s2


def _unrope(dy, c, s1, s2):
    return dy * c + pltpu.roll(dy * s1, ROPE_DIMS // 2, 1) + pltpu.roll(dy * s2, LANES - ROPE_DIMS // 2, 1)


def _sigmoid(x):
    return 1.0 / (1.0 + jnp.exp(-x))


def _rms_fwd(x, gain, name):
    s_len, d = x.shape
    tm = 512

    def body(x_ref, g_ref, h_ref, ht_ref, r_ref):
        xv = x_ref[...]
        r = lax.rsqrt(jnp.mean(xv * xv, axis=-1, keepdims=True) + EPS)
        h = ((xv * r) * g_ref[...]).astype(BF16)
        h_ref[...] = h
        ht_ref[...] = h.T
        r_ref[...] = r

    return pl.pallas_call(
        body, name=name, grid=(s_len // tm,),
        in_specs=[pl.BlockSpec((tm, d), lambda i: (i, 0)), pl.BlockSpec((1, d), lambda i: (0, 0))],
        out_specs=(pl.BlockSpec((tm, d), lambda i: (i, 0)), pl.BlockSpec((d, tm), lambda i: (0, i)),
                   pl.BlockSpec((tm, 1), lambda i: (i, 0))),
        out_shape=(jax.ShapeDtypeStruct((s_len, d), BF16), jax.ShapeDtypeStruct((d, s_len), BF16),
                   jax.ShapeDtypeStruct((s_len, 1), F32)),
        compiler_params=_cparams(1),
    )(x, gain)


def _resident(shape, index_map):
    return pl.BlockSpec(shape, index_map, pipeline_mode=pl.Buffered(1))


def _mm_rows(pairs, name, nt=False, tm=512, bias=None, sigmoid=False, res=None, out_dtypes=(F32,), loss_target=None,
             rms_bwd=None):
    m = pairs[0][0].shape[0]
    n = pairs[0][1].shape[0] if nt else pairs[0][1].shape[1]
    n_pairs = len(pairs)
    has_bias, has_res, has_loss = bias is not None, res is not None, loss_target is not None
    has_rms = rms_bwd is not None
    dims = (((1,), (1,)), ((), ())) if nt else (((1,), (0,)), ((), ()))

    def body(*refs):
        acc = None
        for p in range(n_pairs):
            t = lax.dot_general(refs[2 * p][...].astype(BF16), refs[2 * p + 1][...], dims, preferred_element_type=F32)
            acc = t if acc is None else acc + t
        pos = 2 * n_pairs
        if has_bias:
            acc = acc + refs[pos][...]
            pos += 1
        if sigmoid:
            acc = _sigmoid(acc)
        if has_res:
            acc = refs[pos][...] + acc
            pos += 1
        if has_loss:
            dy_ref, dyb_ref, da_ref, l_ref = refs[pos + 1:]

            @pl.when(pl.program_id(0) == 0)
            def _():
                l_ref[...] = jnp.zeros_like(l_ref)
            err = acc - refs[pos][...]
            dy = err * (1.0 / n)
            dy_ref[...] = dy
            dyb_ref[...] = dy.astype(BF16)
            da_ref[...] = lax.dot_general(dy.astype(BF16), refs[1][...], (((1,), (1,)), ((), ())),
                                          preferred_element_type=F32).astype(BF16)
            part = 0.5 * jnp.sum(jnp.mean(err * err, axis=-1, keepdims=True), axis=0, keepdims=True)
            l_ref[...] += jnp.broadcast_to(part, l_ref.shape)
            return
        if has_rms:
            x_ref, r_ref, g_ref, add_ref = refs[pos:pos + 4]
            dg_ref = refs[-1]

            @pl.when(pl.program_id(0) == 0)
            def _():
                dg_ref[...] = jnp.zeros_like(dg_ref)
            rv = r_ref[...]
            xhat = x_ref[...] * rv
            dg_ref[...] += jnp.sum(acc * xhat, axis=0, keepdims=True)
            dxhat = acc * g_ref[...]
            acc = add_ref[...] + rv * (dxhat - xhat * jnp.mean(dxhat * xhat, axis=-1, keepdims=True))
            for o_ref in refs[pos + 4:-1]:
                o_ref[...] = acc.astype(o_ref.dtype)
            return
        for o_ref in refs[pos:]:
            o_ref[...] = acc.astype(o_ref.dtype)

    in_specs, args = [], []
    for a, w, blk in pairs:
        k = a.shape[1]
        in_specs.append(pl.BlockSpec((tm, k), lambda i: (i, 0)))
        if nt:
            in_specs.append(_resident((n, k), lambda i, blk=blk: (0, blk)))
        else:
            in_specs.append(_resident((k, n), lambda i, blk=blk: (blk, 0)))
        args += [a, w]
    if has_bias:
        in_specs.append(_resident((1, n), lambda i: (0, 0)))
        args.append(bias)
    if has_res:
        in_specs.append(pl.BlockSpec((tm, n), lambda i: (i, 0)))
        args.append(res)
    out = pl.BlockSpec((tm, n), lambda i: (i, 0))
    if has_loss:
        k0 = pairs[0][0].shape[1]
        return pl.pallas_call(
            body, name=name, grid=(m // tm,), in_specs=in_specs + [out],
            out_specs=(out, out, pl.BlockSpec((tm, k0), lambda i: (i, 0)), pl.BlockSpec((8, LANES), lambda i: (0, 0))),
            out_shape=(jax.ShapeDtypeStruct((m, n), F32), jax.ShapeDtypeStruct((m, n), BF16),
                       jax.ShapeDtypeStruct((m, k0), BF16), jax.ShapeDtypeStruct((8, LANES), F32)),
            compiler_params=_cparams(1),
        )(*args, loss_target)
    if has_rms:
        x, r, gain, add = rms_bwd
        vec = _resident((1, n), lambda i: (0, 0))
        return pl.pallas_call(
            body, name=name, grid=(m // tm,),
            in_specs=in_specs + [out, pl.BlockSpec((tm, 1), lambda i: (i, 0)), vec, out],
            out_specs=tuple([out] * len(out_dtypes) + [pl.BlockSpec((1, n), lambda i: (0, 0))]),
            out_shape=tuple([jax.ShapeDtypeStruct((m, n), dt) for dt in out_dtypes] + [jax.ShapeDtypeStruct((1, n), F32)]),
            compiler_params=_cparams(1),
        )(*args, x, r, gain, add)
    outs = pl.pallas_call(
        body, name=name, grid=(m // tm,), in_specs=in_specs, out_specs=tuple([out] * len(out_dtypes)),
        out_shape=tuple(jax.ShapeDtypeStruct((m, n), dt) for dt in out_dtypes), compiler_params=_cparams(1),
    )(*args)
    return outs[0] if len(out_dtypes) == 1 else outs


def _mm_rows_each(pairs, name, tm=512):
    m = pairs[0][0].shape[0]
    n_pairs = len(pairs)

    def body(*refs):
        for p in range(n_pairs):
            refs[2 * n_pairs + p][...] = lax.dot_general(refs[2 * p][...].astype(BF16), refs[2 * p + 1][...],
                                                         (((1,), (1,)), ((), ())), preferred_element_type=F32)

    in_specs, args = [], []
    for a, w in pairs:
        in_specs += [pl.BlockSpec((tm, a.shape[1]), lambda i: (i, 0)), _resident(w.shape, lambda i: (0, 0))]
        args += [a, w]
    return pl.pallas_call(
        body, name=name, grid=(m // tm,), in_specs=in_specs,
        out_specs=tuple(pl.BlockSpec((tm, w.shape[0]), lambda i: (i, 0)) for _, w in pairs),
        out_shape=tuple(jax.ShapeDtypeStruct((m, w.shape[0]), F32) for _, w in pairs), compiler_params=_cparams(1),
    )(*args)


def _mm_rows_cat(a, ws, name, tm=256):
    m, k = a.shape
    widths = [w.shape[1] for w in ws]
    n = sum(widths)

    def body(*refs):
        a_ref, o_ref = refs[0], refs[-1]
        av = a_ref[...]
        off = 0
        for p, width in enumerate(widths):
            o_ref[:, off:off + width] = jnp.dot(av, refs[1 + p][...], preferred_element_type=F32).astype(GRAD_DTYPE)
            off += width

    return pl.pallas_call(
        body, name=name, grid=(m // tm,),
        in_specs=[pl.BlockSpec((tm, k), lambda i: (i, 0))] + [_resident((k, wd), lambda i: (0, 0)) for wd in widths],
        out_specs=pl.BlockSpec((tm, n), lambda i: (i, 0)),
        out_shape=jax.ShapeDtypeStruct((m, n), GRAD_DTYPE), compiler_params=_cparams(1),
    )(a, *ws)


def _mm_cols(a, b, name, tn=256):
    m, k = a.shape
    n = b.shape[1]

    def body(a_ref, b_ref, o_ref):
        o_ref[...] = jnp.dot(a_ref[...], b_ref[...].astype(BF16), preferred_element_type=F32).astype(GRAD_DTYPE)

    return pl.pallas_call(
        body, name=name, grid=(n // tn,),
        in_specs=[_resident((m, k), lambda j: (0, 0)), pl.BlockSpec((k, tn), lambda j: (0, j))],
        out_specs=pl.BlockSpec((m, tn), lambda j: (0, j)),
        out_shape=jax.ShapeDtypeStruct((m, n), GRAD_DTYPE), compiler_params=_cparams(1),
    )(a, b)


def _mm_tn_each(pairs, name, tile=256):
    n = pairs[0][1].shape[1]
    n_pairs = len(pairs)
    dims = (((0,), (0,)), ((), ()))

    def body(*refs):
        for p in range(n_pairs):
            refs[2 * n_pairs + p][...] = lax.dot_general(refs[2 * p][...].astype(BF16), refs[2 * p + 1][...].astype(BF16),
                                                         dims, preferred_element_type=F32).astype(GRAD_DTYPE)

    in_specs, args = [], []
    for a, b in pairs:
        in_specs += [_resident(a.shape, lambda j: (0, 0)), pl.BlockSpec((b.shape[0], tile), lambda j: (0, j))]
        args += [a, b]
    return pl.pallas_call(
        body, name=name, grid=(n // tile,), in_specs=in_specs,
        out_specs=tuple(pl.BlockSpec((a.shape[1], tile), lambda j: (0, j)) for a, _ in pairs),
        out_shape=tuple(jax.ShapeDtypeStruct((a.shape[1], n), GRAD_DTYPE) for a, _ in pairs),
        compiler_params=_cparams(1),
    )(*args)


def _mm_tn(a, b, name, tile=256):
    k, m = a.shape
    n = b.shape[1]
    dims = (((0,), (0,)), ((), ()))

    def body(a_ref, b_ref, o_ref):
        o_ref[...] = lax.dot_general(a_ref[...].astype(BF16), b_ref[...].astype(BF16), dims,
                                     preferred_element_type=F32).astype(GRAD_DTYPE)

    if n <= m:
        t = min(tile, m)
        grid, a_spec, b_spec = (m // t,), pl.BlockSpec((k, t), lambda i: (0, i)), _resident((k, n), lambda i: (0, 0))
        o_spec = pl.BlockSpec((t, n), lambda i: (i, 0))
    else:
        t = min(tile, n)
        grid, a_spec, b_spec = (n // t,), _resident((k, m), lambda i: (0, 0)), pl.BlockSpec((k, t), lambda i: (0, i))
        o_spec = pl.BlockSpec((m, t), lambda i: (0, i))
    return pl.pallas_call(
        body, name=name, grid=grid, in_specs=[a_spec, b_spec], out_specs=o_spec,
        out_shape=jax.ShapeDtypeStruct((m, n), GRAD_DTYPE), compiler_params=_cparams(1),
    )(a, b)


def _norm_rope(t, gain, c, s1, s2, seg):
    rs = lax.rsqrt(_seg_sum(t * t, seg) * (1.0 / HEAD_DIM) + EPS)
    return _rope((t * rs) * gain, c, s1, s2)


def _dup_half(y, half):
    lane = lax.broadcasted_iota(jnp.int32, y.shape, 1)
    rolled = pltpu.roll(y, HEAD_DIM, 1)
    keep = (lane < HEAD_DIM) if half == 0 else (lane >= HEAD_DIM)
    return jnp.where(keep, y, rolled)


def _qk_prep(proj, cb0, d, gqa, gq, gk, tabs, name):
    s_len = proj.shape[0]
    tm = 512
    rows = tm // d
    n_units = 4 if gqa else 2 * d
    n_q = 4 if gqa else 2
    n_in = 6

    def body(*refs):
        in_refs = refs[:n_in]
        gq_ref, gk_ref, c_ref, s1_ref, s2_ref, o_ref = refs[n_in:]
        seg = _seg_matrix(HEAD_DIM)

        def rows_of(ref, r):
            return ref[...] if d == 1 else ref[pl.ds(r, rows, stride=d), :]

        def put(unit_col, y):
            o_ref[:, unit_col * LANES:(unit_col + 1) * LANES] = y.astype(BF16)

        for r in range(d):
            c, s1, s2 = rows_of(c_ref, r), rows_of(s1_ref, r), rows_of(s2_ref, r)
            for b in range(n_in):
                t = rows_of(in_refs[b], r)
                if b < n_q:
                    put((b * d + r) if not gqa else b, _norm_rope(t, gq_ref[...], c, s1, s2, seg))
                elif not gqa:
                    sec, pair = (1, b - 2) if b < 4 else (2, b - 4)
                    y = _norm_rope(t, gk_ref[...], c, s1, s2, seg) if sec == 1 else t
                    put(sec * n_units + pair * d + r, y)
                else:
                    sec = 1 if b == 4 else 2
                    y = _norm_rope(t, gk_ref[...], c, s1, s2, seg) if sec == 1 else t
                    for u in range(n_units):
                        put(sec * n_units + u, _dup_half(y, u // 2))

    in_specs = [pl.BlockSpec((tm, LANES), lambda i, b=b: (i, cb0 + b)) for b in range(n_in)]
    vec = pl.BlockSpec((1, LANES), lambda i: (0, 0))
    tab = pl.BlockSpec((tm, LANES), lambda i: (i, 0))
    width = 3 * n_units * LANES
    return pl.pallas_call(
        body, name=name, grid=(s_len // tm,), in_specs=in_specs + [vec, vec, tab, tab, tab],
        out_specs=pl.BlockSpec((rows, width), lambda i: (i, 0)),
        out_shape=jax.ShapeDtypeStruct((s_len // d, width), BF16), compiler_params=_cparams(1),
    )(*([proj] * n_in), gq, gk, *tabs)


def _qk_prep_bwd(dqkv, proj, cb0, d, gqa, gq, gk, tabs, name):
    s_len = proj.shape[0]
    tm = 512
    rows = tm // d
    n_units = 4 if gqa else 2 * d
    n_q = 4 if gqa else 2
    n_in = 6

    def body(*refs):
        d_refs = refs[0:3]
        in_refs = refs[3:3 + n_in]
        gq_ref, gk_ref, c_ref, s1_ref, s2_ref, o_ref, dgq_ref, dgk_ref, stage = refs[3 + n_in:]
        seg = _seg_matrix(HEAD_DIM)

        @pl.when(pl.program_id(0) == 0)
        def _():
            dgq_ref[...] = jnp.zeros_like(dgq_ref)
            dgk_ref[...] = jnp.zeros_like(dgk_ref)

        def rows_of(ref, r):
            return ref[...] if d == 1 else ref[pl.ds(r, rows, stride=d), :]

        def unit(col):
            sec, u = divmod(col, n_units)
            return d_refs[sec][:, u * LANES:(u + 1) * LANES]

        def norm_bwd(dyr, t, gain, c, s1, s2, dg_ref):
            rs = lax.rsqrt(_seg_sum(t * t, seg) * (1.0 / HEAD_DIM) + EPS)
            that = t * rs
            dy = _unrope(dyr, c, s1, s2)
            dg_ref[...] += jnp.sum(dy * that, axis=0, keepdims=True)
            dthat = dy * gain
            return rs * (dthat - that * (_seg_sum(dthat * that, seg) * (1.0 / HEAD_DIM)))

        def fold(sec):
            tot = []
            for u in range(n_units):
                v = unit(sec * n_units + u)
                tot.append(v + pltpu.roll(v, HEAD_DIM, 1))
            lane = lax.broadcasted_iota(jnp.int32, tot[0].shape, 1)
            return jnp.where(lane < HEAD_DIM, tot[0] + tot[1], tot[2] + tot[3])

        for b in range(n_in):
            for r in range(d):
                c, s1, s2 = rows_of(c_ref, r), rows_of(s1_ref, r), rows_of(s2_ref, r)
                t = rows_of(in_refs[b], r)
                if b < n_q:
                    g = unit((b * d + r) if not gqa else b)
                    out = norm_bwd(g, t, gq_ref[...], c, s1, s2, dgq_ref)
                elif not gqa:
                    sec, pair = (1, b - 2) if b < 4 else (2, b - 4)
                    g = unit(sec * n_units + pair * d + r)
                    out = norm_bwd(g, t, gk_ref[...], c, s1, s2, dgk_ref) if sec == 1 else g
                else:
                    sec = 1 if b == 4 else 2
                    g = fold(sec)
                    out = norm_bwd(g, t, gk_ref[...], c, s1, s2, dgk_ref) if sec == 1 else g
                if d == 1:
                    o_ref[:, b * LANES:(b + 1) * LANES] = out.astype(BF16)
                else:
                    stage[pl.ds(r, rows, stride=d), :] = out
            if d != 1:
                o_ref[:, b * LANES:(b + 1) * LANES] = stage[...].astype(BF16)

    in_specs = [pl.BlockSpec((rows, n_units * LANES), lambda i: (i, 0))] * 3
    in_specs += [pl.BlockSpec((tm, LANES), lambda i, b=b: (i, cb0 + b)) for b in range(n_in)]
    vec = pl.BlockSpec((1, LANES), lambda i: (0, 0))
    tab = pl.BlockSpec((tm, LANES), lambda i: (i, 0))
    return pl.pallas_call(
        body, name=name, grid=(s_len // tm,), in_specs=in_specs + [vec, vec, tab, tab, tab],
        out_specs=(pl.BlockSpec((tm, n_in * LANES), lambda i: (i, 0)), vec, vec),
        out_shape=(jax.ShapeDtypeStruct((s_len, n_in * LANES), BF16), jax.ShapeDtypeStruct((1, LANES), F32),
                   jax.ShapeDtypeStruct((1, LANES), F32)),
        scratch_shapes=[pltpu.VMEM((tm, LANES), F32)], compiler_params=_cparams(1),
    )(*dqkv, *([proj] * n_in), gq, gk, *tabs)


def _head_masks(shape):
    lane = lax.broadcasted_iota(jnp.int32, shape, 1)
    return lane < HEAD_DIM, lane >= HEAD_DIM


def _band_fwd(qkv, n_units, max_dist, sinks, name):
    n_rows = qkv.shape[0]
    nb = n_rows // BLOCK
    scale = HEAD_DIM ** -0.5
    has_sink = sinks is not None
    assert not has_sink or max_dist < BLOCK

    qn, un = min(nb, BAND_Q_BLOCKS), BAND_UNITS
    ug = n_units // un

    def body(*refs):
        q_ref, kp_ref, km_ref, vp_ref, vm_ref = refs[:5]
        o_ref, lse_ref = refs[-2:]
        i = pl.program_id(1)
        qi = lax.broadcasted_iota(jnp.int32, (BLOCK, 2 * BLOCK), 0)
        kj = lax.broadcasted_iota(jnp.int32, (BLOCK, 2 * BLOCK), 1)
        dist = qi + BLOCK - kj
        band = (dist >= 0) & (dist <= max_dist)
        band_first = band & ((i > 0) | (kj >= BLOCK))
        m0, m1 = _head_masks((BLOCK, LANES))
        zero = jnp.zeros((BLOCK, LANES), BF16)
        for ub in range(un):
            cs = slice(ub * LANES, (ub + 1) * LANES)
            for qb in range(qn):
                rs = slice(qb * BLOCK, (qb + 1) * BLOCK)
                q = q_ref[rs, cs]
                if qb == 0:
                    kk = jnp.concatenate([kp_ref[:, cs], km_ref[0:BLOCK, cs]], axis=0)
                    vv = jnp.concatenate([vp_ref[:, cs], vm_ref[0:BLOCK, cs]], axis=0)
                    valid = band_first
                else:
                    kk = km_ref[(qb - 1) * BLOCK:(qb + 1) * BLOCK, cs]
                    vv = vm_ref[(qb - 1) * BLOCK:(qb + 1) * BLOCK, cs]
                    valid = band
                outs, lses = [], []
                for e, hm in enumerate((m0, m1)):
                    qe = jnp.where(hm, q, zero)
                    s = lax.dot_general(qe, kk, (((1,), (1,)), ((), ())), preferred_element_type=F32) * scale
                    s = jnp.where(valid, s, -jnp.inf)
                    if has_sink:
                        s = jnp.where(kj == 0, refs[5][ub][:, e * HEAD_DIM:e * HEAD_DIM + 1], s)
                    mx = jnp.max(s, axis=-1, keepdims=True)
                    p = jnp.exp(s - mx)
                    den = jnp.sum(p, axis=-1, keepdims=True)
                    pn = p * (1.0 / den)
                    if has_sink:
                        pn = jnp.where(kj == 0, 0.0, pn)
                    pn = pn.astype(BF16)
                    outs.append(jnp.dot(pn, vv, preferred_element_type=F32))
                    lses.append(mx + jnp.log(den))
                o_ref[rs, cs] = jnp.where(m0, outs[0], outs[1])
                lse_ref[rs, cs] = jnp.where(m0, jnp.broadcast_to(lses[0], (BLOCK, LANES)),
                                            jnp.broadcast_to(lses[1], (BLOCK, LANES)))

    def main(sec):
        return pl.BlockSpec((qn * BLOCK, un * LANES), lambda u, i: (i, sec * ug + u))

    def prev(sec):
        return pl.BlockSpec((BLOCK, un * LANES), lambda u, i: (jnp.maximum(i * qn - 1, 0), sec * ug + u))

    in_specs = [main(0), prev(1), main(1), prev(2), main(2)]
    args = [qkv] * 5
    if has_sink:
        in_specs.append(pl.BlockSpec((un, 1, LANES), lambda u, i: (u, 0, 0)))
        args.append(sinks)
    return pl.pallas_call(
        body, name=name, grid=(ug, nb // qn), in_specs=in_specs, out_specs=(main(0), main(0)),
        out_shape=(jax.ShapeDtypeStruct((n_rows, n_units * LANES), F32),) * 2, compiler_params=_cparams(2),
    )(*args)


def _band_bwd(qkv, do, lse, delta, n_units, max_dist, name):
    n_rows = qkv.shape[0]
    nb = n_rows // BLOCK
    scale = HEAD_DIM ** -0.5

    qn, un = min(nb, BAND_Q_BLOCKS), BAND_UNITS
    ug = n_units // un
    steps = nb // qn
    nt_dims = (((1,), (1,)), ((), ()))
    tn_dims = (((0,), (0,)), ((), ()))

    def body(qm_ref, qx_ref, kp_ref, km_ref, vp_ref, vm_ref, dom_ref, dox_ref, lm_ref, lx_ref, dm_ref, dx_ref,
             dq_ref, dk_ref, dv_ref):
        i = pl.program_id(1)
        m0, m1 = _head_masks((BLOCK, LANES))
        zero = jnp.zeros((BLOCK, LANES), BF16)
        qi = lax.broadcasted_iota(jnp.int32, (BLOCK, 2 * BLOCK), 0)
        kj = lax.broadcasted_iota(jnp.int32, (BLOCK, 2 * BLOCK), 1)
        dist = qi + BLOCK - kj
        band = (dist >= 0) & (dist <= max_dist)
        band_first = band & ((i > 0) | (kj >= BLOCK))
        qr = lax.broadcasted_iota(jnp.int32, (BLOCK, BLOCK), 0)
        kc = lax.broadcasted_iota(jnp.int32, (BLOCK, BLOCK), 1)
        dist_x = qr + BLOCK - kc
        band_next = (dist_x >= 0) & (dist_x <= max_dist) & (i < steps - 1)

        def pair(q, dob, lse_b, del_b, kk, vv, valid):
            dqs, dk, dv = [], None, None
            for e, hm in enumerate((m0, m1)):
                col = slice(e * HEAD_DIM, e * HEAD_DIM + 1)
                qe = jnp.where(hm, q, zero)
                doe = jnp.where(hm, dob, zero)
                s = lax.dot_general(qe, kk, nt_dims, preferred_element_type=F32) * scale
                p = jnp.where(valid, jnp.exp(s - lse_b[:, col]), 0.0)
                dp = lax.dot_general(doe, vv, nt_dims, preferred_element_type=F32)
                ds = (p * (dp - del_b[:, col]) * scale).astype(BF16)
                dqs.append(jnp.dot(ds, kk, preferred_element_type=F32))
                dk_e = lax.dot_general(ds, qe, tn_dims, preferred_element_type=F32)
                dv_e = lax.dot_general(p.astype(BF16), doe, tn_dims, preferred_element_type=F32)
                dk = dk_e if dk is None else dk + dk_e
                dv = dv_e if dv is None else dv + dv_e
            return jnp.where(m0, dqs[0], dqs[1]), dk, dv

        for ub in range(un):
            cs = slice(ub * LANES, (ub + 1) * LANES)
            dk_acc, dv_acc = [None] * qn, [None] * qn

            def add(acc, kb, part):
                acc[kb] = part if acc[kb] is None else acc[kb] + part

            for qb in range(qn):
                rs = slice(qb * BLOCK, (qb + 1) * BLOCK)
                if qb == 0:
                    kk = jnp.concatenate([kp_ref[:, cs], km_ref[0:BLOCK, cs]], axis=0)
                    vv = jnp.concatenate([vp_ref[:, cs], vm_ref[0:BLOCK, cs]], axis=0)
                    valid = band_first
                else:
                    kk = km_ref[(qb - 1) * BLOCK:(qb + 1) * BLOCK, cs]
                    vv = vm_ref[(qb - 1) * BLOCK:(qb + 1) * BLOCK, cs]
                    valid = band
                dq, dk, dv = pair(qm_ref[rs, cs], dom_ref[rs, cs], lm_ref[rs, cs], dm_ref[rs, cs], kk, vv, valid)
                dq_ref[rs, cs] = dq
                if qb > 0:
                    add(dk_acc, qb - 1, dk[0:BLOCK])
                    add(dv_acc, qb - 1, dv[0:BLOCK])
                add(dk_acc, qb, dk[BLOCK:2 * BLOCK])
                add(dv_acc, qb, dv[BLOCK:2 * BLOCK])
            last = slice((qn - 1) * BLOCK, qn * BLOCK)
            _, dk, dv = pair(qx_ref[:, cs], dox_ref[:, cs], lx_ref[:, cs], dx_ref[:, cs], km_ref[last, cs], vm_ref[last, cs],
                             band_next)
            add(dk_acc, qn - 1, dk)
            add(dv_acc, qn - 1, dv)
            for kb in range(qn):
                dk_ref[kb * BLOCK:(kb + 1) * BLOCK, cs] = dk_acc[kb]
                dv_ref[kb * BLOCK:(kb + 1) * BLOCK, cs] = dv_acc[kb]

    def main(sec):
        return pl.BlockSpec((qn * BLOCK, un * LANES), lambda u, i: (i, sec * ug + u))

    def prev(sec):
        return pl.BlockSpec((BLOCK, un * LANES), lambda u, i: (jnp.maximum(i * qn - 1, 0), sec * ug + u))

    def nxt(sec):
        return pl.BlockSpec((BLOCK, un * LANES), lambda u, i: (jnp.minimum((i + 1) * qn, nb - 1), sec * ug + u))

    in_specs = [main(0), nxt(0), prev(1), main(1), prev(2), main(2),
                main(0), nxt(0), main(0), nxt(0), main(0), nxt(0)]
    args = [qkv] * 6 + [do, do, lse, lse, delta, delta]
    shp = jax.ShapeDtypeStruct((n_rows, n_units * LANES), F32)
    return pl.pallas_call(
        body, name=name, grid=(ug, steps), in_specs=in_specs, out_specs=(main(0), main(0), main(0)),
        out_shape=(shp, shp, shp), compiler_params=_cparams(2),
    )(*args)


def _merge_groups(os_, lses, dils, name):
    s_len = os_[0].shape[0] * dils[0]
    tm = 512

    def body(*refs):
        o_refs, l_refs = refs[0:3], refs[3:6]
        o_ref, lse_ref = refs[6:8]
        so, sl = refs[8:11], refs[11:14]
        for pair in range(2):
            for g, d in enumerate(dils):
                rows = tm // d
                for r in range(d):
                    col = slice((pair * d + r) * LANES, (pair * d + r + 1) * LANES)
                    if d == 1:
                        so[g][...] = o_refs[g][:, col]
                        sl[g][...] = l_refs[g][:, col]
                    else:
                        so[g][pl.ds(r, rows, stride=d), :] = o_refs[g][:, col]
                        sl[g][pl.ds(r, rows, stride=d), :] = l_refs[g][:, col]
            l0, l1, l2 = sl[0][...], sl[1][...], sl[2][...]
            mx = jnp.maximum(jnp.maximum(l0, l1), l2)
            e0, e1, e2 = jnp.exp(l0 - mx), jnp.exp(l1 - mx), jnp.exp(l2 - mx)
            den = e0 + e1 + e2
            inv = 1.0 / den
            o_ref[:, pair * LANES:(pair + 1) * LANES] = (so[0][...] * (e0 * inv) + so[1][...] * (e1 * inv)
                                                         + so[2][...] * (e2 * inv))
            lse_ref[:, pair * LANES:(pair + 1) * LANES] = mx + jnp.log(den)

    in_specs = [pl.BlockSpec((tm // d, 2 * d * LANES), lambda i: (i, 0)) for d in dils] * 2
    out = pl.BlockSpec((tm, 2 * LANES), lambda i: (i, 0))
    shp = jax.ShapeDtypeStruct((s_len, 2 * LANES), F32)
    return pl.pallas_call(
        body, name=name, grid=(s_len // tm,), in_specs=in_specs, out_specs=(out, out), out_shape=(shp, shp),
        scratch_shapes=[pltpu.VMEM((tm, LANES), F32)] * 6, compiler_params=_cparams(1),
    )(*os_, *lses)


def _bwd_prep(do, o, lse, dils, sinks, name):
    s_len, width = do.shape
    n_pairs = width // LANES
    tm = 512
    has_sink = sinks is not None
    n_g = len(dils)

    def body(*refs):
        do_ref, o_ref, lse_ref = refs[:3]
        pos = 3
        if has_sink:
            sink_ref = refs[pos]
            pos += 1
        outs = refs[pos:pos + 3 * n_g]
        pos += 3 * n_g
        if has_sink:
            dsink_ref = refs[pos]
            pos += 1
        s_do, s_l, s_d = refs[pos:pos + 3]
        seg = _seg_matrix(HEAD_DIM)

        if has_sink:
            @pl.when(pl.program_id(0) == 0)
            def _():
                dsink_ref[...] = jnp.zeros_like(dsink_ref)

        for pair in range(n_pairs):
            col = slice(pair * LANES, (pair + 1) * LANES)
            dov = do_ref[:, col]
            lv = lse_ref[:, col]
            delta = _seg_sum(dov * o_ref[:, col], seg)
            if has_sink:
                dsink_ref[pair] += -jnp.sum(jnp.exp(sink_ref[pair] - lv) * delta, axis=0, keepdims=True)
            s_do[...] = dov
            s_l[...] = lv
            s_d[...] = delta
            for g, d in enumerate(dils):
                rows = tm // d
                for r in range(d):
                    oc = slice((pair * d + r) * LANES, (pair * d + r + 1) * LANES)
                    if d == 1:
                        a, b, c = s_do[...], s_l[...], s_d[...]
                    else:
                        a = s_do[pl.ds(r, rows, stride=d), :]
                        b = s_l[pl.ds(r, rows, stride=d), :]
                        c = s_d[pl.ds(r, rows, stride=d), :]
                    outs[3 * g][:, oc] = a.astype(BF16)
                    outs[3 * g + 1][:, oc] = b
                    outs[3 * g + 2][:, oc] = c

    row = pl.BlockSpec((tm, width), lambda i: (i, 0))
    in_specs = [row, row, row]
    args = [do, o, lse]
    if has_sink:
        in_specs.append(pl.BlockSpec((n_pairs, 1, LANES), lambda i: (0, 0, 0)))
        args.append(sinks)
    out_specs, out_shape = [], []
    for d in dils:
        for dt in (BF16, F32, F32):
            out_specs.append(pl.BlockSpec((tm // d, n_pairs * d * LANES), lambda i: (i, 0)))
            out_shape.append(jax.ShapeDtypeStruct((s_len // d, n_pairs * d * LANES), dt))
    if has_sink:
        out_specs.append(pl.BlockSpec((n_pairs, 1, LANES), lambda i: (0, 0, 0)))
        out_shape.append(jax.ShapeDtypeStruct((n_pairs, 1, LANES), F32))
    return pl.pallas_call(
        body, name=name, grid=(s_len // tm,), in_specs=in_specs, out_specs=tuple(out_specs),
        out_shape=tuple(out_shape), scratch_shapes=[pltpu.VMEM((tm, LANES), F32)] * 3, compiler_params=_cparams(1),
    )(*args)


def _mem_kv(mem, mem_gain, w_kv, k_gain, name):
    m_len = mem.shape[0]
    kw = M_HEADS * M_HEAD_DIM

    def body(mem_ref, mg_ref, w_ref, kg_ref, k_ref, v_ref):
        mv = mem_ref[...]
        r = lax.rsqrt(jnp.mean(mv * mv, axis=-1, keepdims=True) + EPS)
        mn = ((mv * r) * mg_ref[...]).astype(BF16)
        kv = jnp.dot(mn, w_ref[...], preferred_element_type=F32)
        for h in range(M_HEADS):
            col = slice(h * M_HEAD_DIM, (h + 1) * M_HEAD_DIM)
            t = kv[:, col]
            rk = lax.rsqrt(jnp.mean(t * t, axis=-1, keepdims=True) + EPS)
            k_ref[:, col] = ((t * rk) * kg_ref[...]).astype(BF16)
        v_ref[...] = kv[:, kw:].astype(BF16)

    shp = jax.ShapeDtypeStruct((m_len, kw), BF16)
    return pl.pallas_call(body, name=name, out_shape=(shp, shp),
                          compiler_params=pltpu.CompilerParams(vmem_limit_bytes=VMEM_LIMIT_BYTES))(mem, mem_gain, w_kv, k_gain)


def _mem_kv_bwd(mem, mem_gain, w_kv, k_gain, dk, dv, name):
    m_len, d = mem.shape
    kw = M_HEADS * M_HEAD_DIM

    def body(mem_ref, mg_ref, w_ref, kg_ref, dk_ref, dv_ref, dw_ref, dmg_ref, dkg_ref, dkv_ref):
        mv = mem_ref[...]
        r = lax.rsqrt(jnp.mean(mv * mv, axis=-1, keepdims=True) + EPS)
        mhat = mv * r
        mn = (mhat * mg_ref[...]).astype(BF16)
        kv = jnp.dot(mn, w_ref[...], preferred_element_type=F32)
        dkg = jnp.zeros((1, M_HEAD_DIM), F32)
        for h in range(M_HEADS):
            col = slice(h * M_HEAD_DIM, (h + 1) * M_HEAD_DIM)
            t = kv[:, col]
            rk = lax.rsqrt(jnp.mean(t * t, axis=-1, keepdims=True) + EPS)
            that = t * rk
            dy = dk_ref[:, col]
            dkg = dkg + jnp.sum(dy * that, axis=0, keepdims=True)
            dthat = dy * kg_ref[...]
            dkv_ref[:, col] = (rk * (dthat - that * jnp.mean(dthat * that, axis=-1, keepdims=True))).astype(BF16)
        dkv_ref[:, kw:] = dv_ref[...].astype(BF16)
        dkg_ref[...] = dkg
        dkv = dkv_ref[...]
        dw_ref[...] = lax.dot_general(mn, dkv, (((0,), (0,)), ((), ())), preferred_element_type=F32).astype(GRAD_DTYPE)
        dmn = lax.dot_general(dkv, w_ref[...], (((1,), (1,)), ((), ())), preferred_element_type=F32)
        dmg_ref[...] = jnp.sum(dmn * mhat, axis=0, keepdims=True)

    return pl.pallas_call(
        body, name=name,
        out_shape=(jax.ShapeDtypeStruct((d, 2 * kw), GRAD_DTYPE), jax.ShapeDtypeStruct((1, d), F32),
                   jax.ShapeDtypeStruct((1, M_HEAD_DIM), F32)),
        scratch_shapes=[pltpu.VMEM((m_len, 2 * kw), BF16)],
        compiler_params=pltpu.CompilerParams(vmem_limit_bytes=VMEM_LIMIT_BYTES),
    )(mem, mem_gain, w_kv, k_gain, dk, dv)


def _mem_attn_fwd(proj, cidx, mk, mv, q_gain, name):
    s_len = proj.shape[0]
    kw = M_HEADS * M_HEAD_DIM
    tm = 512
    scale = M_HEAD_DIM ** -0.5

    def body(q_ref, k_ref, v_ref, g_ref, o_ref):
        for h in range(M_HEADS):
            col = slice(h * M_HEAD_DIM, (h + 1) * M_HEAD_DIM)
            t = q_ref[:, col]
            rs = lax.rsqrt(jnp.mean(t * t, axis=-1, keepdims=True) + EPS)
            qn = ((t * rs) * g_ref[...]).astype(BF16)
            s = lax.dot_general(qn, k_ref[:, col], (((1,), (1,)), ((), ())), preferred_element_type=F32) * scale
            mx = jnp.max(s, axis=-1, keepdims=True)
            p = jnp.exp(s - mx)
            pn = (p * (1.0 / jnp.sum(p, axis=-1, keepdims=True))).astype(BF16)
            o_ref[:, col] = jnp.dot(pn, v_ref[:, col], preferred_element_type=F32).astype(BF16)

    whole = pl.BlockSpec((MEM_LEN, kw), lambda i: (0, 0))
    return pl.pallas_call(
        body, name=name, grid=(s_len // tm,),
        in_specs=[pl.BlockSpec((tm, kw), lambda i: (i, cidx)), whole, whole, pl.BlockSpec((1, M_HEAD_DIM), lambda i: (0, 0))],
        out_specs=pl.BlockSpec((tm, kw), lambda i: (i, 0)),
        out_shape=jax.ShapeDtypeStruct((s_len, kw), BF16), compiler_params=_cparams(1),
    )(proj, mk, mv, q_gain)


def _mem_attn_bwd(proj, cidx, mk, mv, q_gain, do, name):
    s_len = proj.shape[0]
    kw = M_HEADS * M_HEAD_DIM
    tm = 512
    scale = M_HEAD_DIM ** -0.5

    def body(q_ref, k_ref, v_ref, g_ref, do_ref, dq_ref, dk_ref, dv_ref, dg_ref):
        @pl.when(pl.program_id(0) == 0)
        def _():
            dk_ref[...] = jnp.zeros_like(dk_ref)
            dv_ref[...] = jnp.zeros_like(dv_ref)
            dg_ref[...] = jnp.zeros_like(dg_ref)

        for h in range(M_HEADS):
            col = slice(h * M_HEAD_DIM, (h + 1) * M_HEAD_DIM)
            t = q_ref[:, col]
            rs = lax.rsqrt(jnp.mean(t * t, axis=-1, keepdims=True) + EPS)
            that = t * rs
            qn = (that * g_ref[...]).astype(BF16)
            kh, vh = k_ref[:, col], v_ref[:, col]
            dob = do_ref[:, col].astype(BF16)
            s = lax.dot_general(qn, kh, (((1,), (1,)), ((), ())), preferred_element_type=F32) * scale
            mx = jnp.max(s, axis=-1, keepdims=True)
            p = jnp.exp(s - mx)
            p = p * (1.0 / jnp.sum(p, axis=-1, keepdims=True))
            dp = lax.dot_general(dob, vh, (((1,), (1,)), ((), ())), preferred_element_type=F32)
            ds = (p * (dp - jnp.sum(p * dp, axis=-1, keepdims=True)) * scale).astype(BF16)
            dqn = jnp.dot(ds, kh, preferred_element_type=F32)
            dk_ref[:, col] += lax.dot_general(ds, qn, (((0,), (0,)), ((), ())), preferred_element_type=F32)
            dv_ref[:, col] += lax.dot_general(p.astype(BF16), dob, (((0,), (0,)), ((), ())), preferred_element_type=F32)
            dg_ref[...] += jnp.sum(dqn * that, axis=0, keepdims=True)
            dthat = dqn * g_ref[...]
            dq_ref[:, col] = (rs * (dthat - that * jnp.mean(dthat * that, axis=-1, keepdims=True))).astype(BF16)

    whole = pl.BlockSpec((MEM_LEN, kw), lambda i: (0, 0))
    vec = pl.BlockSpec((1, M_HEAD_DIM), lambda i: (0, 0))
    row = pl.BlockSpec((tm, kw), lambda i: (i, 0))
    return pl.pallas_call(
        body, name=name, grid=(s_len // tm,),
        in_specs=[pl.BlockSpec((tm, kw), lambda i: (i, cidx)), whole, whole, vec, row],
        out_specs=(row, whole, whole, vec),
        out_shape=(jax.ShapeDtypeStruct((s_len, kw), BF16), jax.ShapeDtypeStruct((MEM_LEN, kw), F32),
                   jax.ShapeDtypeStruct((MEM_LEN, kw), F32), jax.ShapeDtypeStruct((1, M_HEAD_DIM), F32)),
        compiler_params=_cparams(1),
    )(proj, mk, mv, q_gain, do)


def _project_merge(outs, w_outs, gates, w_out, x, name):
    s_len = gates.shape[0]
    d = w_outs[0].shape[1]
    tm = 512

    def body(oa_ref, ob_ref, om_ref, wa_ref, wb_ref, wm_ref, g_ref, wo_ref, x_ref,
             pa_ref, pb_ref, pm_ref, merged_ref, x1_ref):
        merged = None
        for k, (o_ref, w_ref, p_ref) in enumerate(((oa_ref, wa_ref, pa_ref), (ob_ref, wb_ref, pb_ref), (om_ref, wm_ref, pm_ref))):
            p = jnp.dot(o_ref[...].astype(BF16), w_ref[...], preferred_element_type=F32).astype(BF16)
            p_ref[...] = p
            t = g_ref[:, k * d:(k + 1) * d].astype(F32) * p.astype(F32)
            merged = t if merged is None else merged + t
        merged = merged.astype(BF16)
        merged_ref[...] = merged
        x1_ref[...] = x_ref[...] + jnp.dot(merged, wo_ref[...], preferred_element_type=F32)

    row = pl.BlockSpec((tm, d), lambda i: (i, 0))
    shp = jax.ShapeDtypeStruct((s_len, d), BF16)
    in_specs = [pl.BlockSpec((tm, o.shape[1]), lambda i: (i, 0)) for o in outs]
    in_specs += [_resident(w.shape, lambda i: (0, 0)) for w in w_outs]
    in_specs += [pl.BlockSpec((tm, 3 * d), lambda i: (i, 0)), _resident(w_out.shape, lambda i: (0, 0)), row]
    return pl.pallas_call(
        body, name=name, grid=(s_len // tm,), in_specs=in_specs, out_specs=(row, row, row, row, row),
        out_shape=(shp, shp, shp, shp, jax.ShapeDtypeStruct((s_len, d), F32)), compiler_params=_cparams(1),
    )(*outs, *w_outs, gates, w_out, x)


def _project_merge_bwd(dx1, w_out, gates, pa, pb, pm, name):
    s_len, d = pa.shape
    tm = 512

    def body(dx_ref, w_ref, g_ref, a_ref, b_ref, m_ref, da_ref, db_ref, dmm_ref, dg_ref, dbg_ref):
        @pl.when(pl.program_id(0) == 0)
        def _():
            dbg_ref[...] = jnp.zeros_like(dbg_ref)
        dm = lax.dot_general(dx_ref[...], w_ref[...], (((1,), (1,)), ((), ())), preferred_element_type=F32)
        for k, (p_ref, dp_ref) in enumerate(((a_ref, da_ref), (b_ref, db_ref), (m_ref, dmm_ref))):
            col = slice(k * d, (k + 1) * d)
            g = g_ref[:, col].astype(F32)
            dp_ref[...] = (dm * g).astype(BF16)
            dpre = (dm * p_ref[...].astype(F32)) * (g * (1.0 - g))
            dbg_ref[:, col] += jnp.sum(dpre, axis=0, keepdims=True)
            dg_ref[:, col] = dpre.astype(BF16)

    row = pl.BlockSpec((tm, d), lambda i: (i, 0))
    wide = pl.BlockSpec((tm, 3 * d), lambda i: (i, 0))
    shp = jax.ShapeDtypeStruct((s_len, d), BF16)
    return pl.pallas_call(
        body, name=name, grid=(s_len // tm,), in_specs=[row, _resident(w_out.shape, lambda i: (0, 0)), wide, row, row, row],
        out_specs=(row, row, row, wide, pl.BlockSpec((1, 3 * d), lambda i: (0, 0))),
        out_shape=(shp, shp, shp, jax.ShapeDtypeStruct((s_len, 3 * d), BF16), jax.ShapeDtypeStruct((1, 3 * d), F32)),
        compiler_params=_cparams(1),
    )(dx1, w_out, gates, pa, pb, pm)


CONV_CHUNK = 256


def _pick_row(tile, j):
    row = lax.broadcasted_iota(jnp.int32, tile.shape, 0)
    return jnp.sum(jnp.where(row == j, tile, jnp.zeros_like(tile)), axis=0, keepdims=True)


def _rows_before(ref, start, k):
    cur = ref[pl.ds(start, CONV_CHUNK), :].astype(F32)
    prev = ref[pl.ds(pl.multiple_of(jnp.maximum(start - 16, 0), 16), 16), :].astype(F32)
    prev = jnp.where(start > 0, prev, jnp.zeros_like(prev))
    rolled = pltpu.roll(cur, k, 0)
    row = lax.broadcasted_iota(jnp.int32, cur.shape, 0)
    for j in range(k):
        rolled = jnp.where(row == j, _pick_row(prev, 16 - k + j), rolled)
    return rolled


def _rows_after(ref, start, k):
    cur = ref[pl.ds(start, CONV_CHUNK), :]
    nxt = ref[pl.ds(pl.multiple_of(start + CONV_CHUNK, 8), 8), :]
    rolled = pltpu.roll(cur, CONV_CHUNK - k, 0)
    row = lax.broadcasted_iota(jnp.int32, cur.shape, 0)
    for j in range(k):
        rolled = jnp.where(row == CONV_CHUNK - k + j, _pick_row(nxt, j), rolled)
    return rolled


def _conv_pre(u_ref, w_ref, b_ref, start):
    u2 = _rows_before(u_ref, start, 2)
    u1 = _rows_before(u_ref, start, 1)
    u0 = u_ref[pl.ds(start, CONV_CHUNK), :].astype(F32)
    c = ((b_ref[...] + w_ref[0:1, :] * u2) + w_ref[1:2, :] * u1) + w_ref[2:3, :] * u0
    return c, (u2, u1, u0)


def _norm_up_conv_glu(x, gain, w_up, conv_w, conv_b, name):
    s_len, d = x.shape
    tm, tn = 512, 2 * LANES
    nblk = D_FF // tn

    def body(x_ref, g_ref, w_ref, cw_ref, cb_ref, ht_ref, r_ref, u_ref, act_ref, halo):
        @pl.when(pl.program_id(0) == 0)
        def _():
            halo[...] = jnp.zeros_like(halo)
        xv = x_ref[...]
        r = lax.rsqrt(jnp.mean(xv * xv, axis=-1, keepdims=True) + EPS)
        hv = ((xv * r) * g_ref[...]).astype(BF16)
        ht_ref[...] = hv.T
        r_ref[...] = r
        row = lax.broadcasted_iota(jnp.int32, (tm, tn), 0)
        for j in range(nblk):
            conv = []
            for half in range(2):
                cb = half * nblk + j
                cols = slice(cb * tn, (cb + 1) * tn)
                ub = jnp.dot(hv, w_ref[:, cols], preferred_element_type=F32).astype(BF16)
                u_ref[:, cols] = ub
                u0 = ub.astype(F32)
                prev = halo[cb]
                u1 = jnp.where(row == 0, _pick_row(prev, 7), pltpu.roll(u0, 1, 0))
                u2 = pltpu.roll(u0, 2, 0)
                u2 = jnp.where(row == 0, _pick_row(prev, 6), jnp.where(row == 1, _pick_row(prev, 7), u2))
                halo[cb] = u0[tm - 8:tm, :]
                conv.append(((cb_ref[:, cols] + cw_ref[0:1, cols] * u2) + cw_ref[1:2, cols] * u1)
                            + cw_ref[2:3, cols] * u0)
            act_ref[:, j * tn:(j + 1) * tn] = ((conv[0] * _sigmoid(conv[0])) * conv[1]).astype(BF16)

    return pl.pallas_call(
        body, name=name, grid=(s_len // tm,),
        in_specs=[pl.BlockSpec((tm, d), lambda i: (i, 0)), _resident((1, d), lambda i: (0, 0)),
                  _resident((d, 2 * D_FF), lambda i: (0, 0)),
                  _resident((3, 2 * D_FF), lambda i: (0, 0)), _resident((1, 2 * D_FF), lambda i: (0, 0))],
        out_specs=(pl.BlockSpec((d, tm), lambda i: (0, i)), pl.BlockSpec((tm, 1), lambda i: (i, 0)),
                   pl.BlockSpec((tm, 2 * D_FF), lambda i: (i, 0)), pl.BlockSpec((tm, D_FF), lambda i: (i, 0))),
        out_shape=(jax.ShapeDtypeStruct((d, s_len), BF16), jax.ShapeDtypeStruct((s_len, 1), F32),
                   jax.ShapeDtypeStruct((s_len, 2 * D_FF), BF16), jax.ShapeDtypeStruct((s_len, D_FF), BF16)),
        scratch_shapes=[pltpu.VMEM((2 * nblk, 8, tn), F32)], compiler_params=_cparams(1),
    )(x, gain, w_up, conv_w, conv_b)


def _conv_glu_bwd(dact, u, conv_w, conv_b, name):
    s_len = u.shape[0]
    nblk = D_FF // LANES
    n_chunks = s_len // CONV_CHUNK

    def body(da_ref, ua_ref, ug_ref, wa_ref, wg_ref, ba_ref, bg_ref,
             dua_ref, dug_ref, dwa_ref, dwg_ref, dba_ref, dbg_ref, sa, sg):
        sa[pl.ds(s_len, 8), :] = jnp.zeros((8, LANES), F32)
        sg[pl.ds(s_len, 8), :] = jnp.zeros((8, LANES), F32)
        zero = jnp.zeros((1, LANES), F32)

        def chunk1(ci, carry):
            start = pl.multiple_of(ci * CONV_CHUNK, CONV_CHUNK)
            ca, ua = _conv_pre(ua_ref, wa_ref, ba_ref, start)
            cg, ug = _conv_pre(ug_ref, wg_ref, bg_ref, start)
            dact_v = da_ref[pl.ds(start, CONV_CHUNK), :].astype(F32)
            sig = _sigmoid(ca)
            dcg = dact_v * (ca * sig)
            dca = (dact_v * cg) * (sig * (1.0 + ca * (1.0 - sig)))
            sa[pl.ds(start, CONV_CHUNK), :] = dca
            sg[pl.ds(start, CONV_CHUNK), :] = dcg
            out = [carry[0] + jnp.sum(dca, axis=0, keepdims=True), carry[1] + jnp.sum(dcg, axis=0, keepdims=True)]
            for j in range(3):
                out.append(carry[2 + j] + jnp.sum(dca * ua[j], axis=0, keepdims=True))
            for j in range(3):
                out.append(carry[5 + j] + jnp.sum(dcg * ug[j], axis=0, keepdims=True))
            return tuple(out)

        acc = lax.fori_loop(0, n_chunks, chunk1, (zero,) * 8)
        dba_ref[...] = acc[0]
        dbg_ref[...] = acc[1]
        for j in range(3):
            dwa_ref[j:j + 1, :] = acc[2 + j]
            dwg_ref[j:j + 1, :] = acc[5 + j]

        def chunk2(ci, carry):
            start = pl.multiple_of(ci * CONV_CHUNK, CONV_CHUNK)
            for s_ref, w_ref, o_ref in ((sa, wa_ref, dua_ref), (sg, wg_ref, dug_ref)):
                d0 = s_ref[pl.ds(start, CONV_CHUNK), :]
                d1 = _rows_after(s_ref, start, 1)
                d2 = _rows_after(s_ref, start, 2)
                o_ref[pl.ds(start, CONV_CHUNK), :] = (w_ref[2:3, :] * d0 + w_ref[1:2, :] * d1
                                                      + w_ref[0:1, :] * d2).astype(BF16)
            return carry
        lax.fori_loop(0, n_chunks, chunk2, 0)

    def col(rows, off):
        return pl.BlockSpec((rows, LANES), lambda j: (0, off + j))

    big = jax.ShapeDtypeStruct((s_len, D_FF), BF16)
    return pl.pallas_call(
        body, name=name, grid=(nblk,),
        in_specs=[col(s_len, 0), col(s_len, 0), col(s_len, nblk), col(3, 0), col(3, nblk), col(1, 0), col(1, nblk)],
        out_specs=(col(s_len, 0), col(s_len, 0), col(3, 0), col(3, 0), col(1, 0), col(1, 0)),
        out_shape=(big, big, jax.ShapeDtypeStruct((3, D_FF), F32), jax.ShapeDtypeStruct((3, D_FF), F32),
                   jax.ShapeDtypeStruct((1, D_FF), F32), jax.ShapeDtypeStruct((1, D_FF), F32)),
        scratch_shapes=[pltpu.VMEM((s_len + 8, LANES), F32)] * 2, compiler_params=_cparams(1),
    )(dact, u, u, conv_w, conv_w, conv_b, conv_b)


def _rope_tables(positions):
    half = ROPE_DIMS // 2
    freqs = jnp.exp(jnp.arange(half, dtype=F32) * (-2.0 * math.log(ROPE_THETA) / ROPE_DIMS))
    ang = positions.reshape(-1).astype(F32)[:, None] * freqs
    cos, sin = jnp.cos(ang), jnp.sin(ang)
    n = ang.shape[0]
    zeros = lambda w: jnp.zeros((n, w), F32)
    c = jnp.concatenate([cos, cos, jnp.ones((n, HEAD_DIM - ROPE_DIMS), F32)], axis=1)
    s1 = jnp.concatenate([-sin, zeros(HEAD_DIM - half)], axis=1)
    s2 = jnp.concatenate([zeros(half), sin, zeros(HEAD_DIM - ROPE_DIMS)], axis=1)
    return tuple(jnp.tile(t, (1, 2)) for t in (c, s1, s2))


def _two(v):
    return jnp.tile(v.reshape(1, HEAD_DIM), (1, 2))


def _fold_heads(g):
    return g[0, :HEAD_DIM] + g[0, HEAD_DIM:]


MIX_WEIGHTS = ('w_gate', 'w_mem_kv', 'w_o_a', 'w_o_b', 'w_o_m', 'w_out')
FFN_WEIGHTS = ('w_up', 'conv_w', 'w_down')


def _device_step(x, mem, positions, target, w, hooks=None):
    tabs = _rope_tables(positions)
    dils = tuple(d for _, d in A_GROUPS)
    grads = {}
    w = dict(w)

    h, h_t, r1 = _rms_fwd(x, w['attn_norm'], "rms1")
    if hooks is not None:
        w.update(hooks.weights('in', [h, *tabs]))
    proj = _mm_rows([(h, w['w_in'], 0)], "mm_in")

    qkv_a, o_g, lse_g = [], [], []
    for gi, (window, d) in enumerate(A_GROUPS):
        gq, gk = _two(w['a_q_norm'][gi]), _two(w['a_k_norm'][gi])
        qkv = _qk_prep(proj, 6 * gi, d, False, gq, gk, tabs, f"qk_prep_a{gi}")
        o, lse = _band_fwd(qkv, 2 * d, window // d, None, f"band_fwd_a{gi}")
        qkv_a.append(qkv)
        o_g.append(o)
        lse_g.append(lse)
    o_a, lse_a = _merge_groups(o_g, lse_g, dils, "merge_a")
    if hooks is not None:
        w.update(hooks.weights('mix', [o_a]))

    gbq, gbk = _two(w['b_q_norm']), _two(w['b_k_norm'])
    sinks = jnp.repeat(w['b_sinks'].reshape(4, 2), HEAD_DIM, axis=1).reshape(4, 1, LANES)
    qkv_b = _qk_prep(proj, 18, 1, True, gbq, gbk, tabs, "qk_prep_b")
    o_b, lse_b = _band_fwd(qkv_b, 4, B_WINDOW - 1, sinks, "band_fwd_b")

    gates = _mm_rows([(h, w['w_gate'], 0)], "mm_gate", bias=w['b_gate'], sigmoid=True, out_dtypes=(BF16,))
    mk, mv = _mem_kv(mem, w['mem_norm'], w['w_mem_kv'], w['m_k_norm'], "mem_kv")
    o_m = _mem_attn_fwd(proj, 6, mk, mv, w['m_q_norm'], "mem_attn")

    pa, pb, pm, merged, x1 = _project_merge((o_a, o_b, o_m), (w['w_o_a'], w['w_o_b'], w['w_o_m']), gates, w['w_out'], x,
                                            "project_merge")

    if hooks is not None:
        w.update(hooks.weights('ffn', [x1]))
    h2_t, r2, u, act = _norm_up_conv_glu(x1, w['ffn_norm'], w['w_up'], w['conv_w'], w['conv_b'], "norm_up_conv_glu")
    dy, dy_b, dact, loss = _mm_rows([(act, w['w_down'], 0)], "mm_down", res=x1, loss_target=target)

    grads['w_down'] = _mm_tn(act, dy_b, "mm_dw_down")
    du_a, du_g, dcw_a, dcw_g, dcb_a, dcb_g = _conv_glu_bwd(dact, u, w['conv_w'], w['conv_b'], "conv_glu_bwd")
    grads['conv_w'] = jnp.concatenate([dcw_a, dcw_g], axis=1)
    grads['conv_b'] = jnp.concatenate([dcb_a, dcb_g], axis=1)
    grads['w_up'] = jnp.concatenate([_mm_cols(h2_t, du_a, "mm_dw_up_a"), _mm_cols(h2_t, du_g, "mm_dw_up_g")], axis=1)
    ffn_gain = w['ffn_norm']
    if hooks is not None:
        ffn_gain = ffn_gain + hooks.grads('ffn', grads)[0:1, 0:1]
    dx1, dx1_b, grads['ffn_norm'] = _mm_rows([(du_a, w['w_up'], 0), (du_g, w['w_up'], 1)], "mm_d_h2", nt=True,
                                             rms_bwd=(x1, r2, ffn_gain, dy), out_dtypes=(F32, BF16))

    grads['w_out'] = _mm_tn(merged, dx1_b, "mm_dw_out")
    dpa, dpb, dpm, dgpre, grads['b_gate'] = _project_merge_bwd(dx1_b, w['w_out'], gates, pa, pb, pm,
                                                               "project_merge_bwd")
    do_a, do_b, do_m = _mm_rows_each([(dpa, w['w_o_a']), (dpb, w['w_o_b']), (dpm, w['w_o_m'])], "mm_d_o")
    grads['w_o_a'], grads['w_o_b'], grads['w_o_m'] = _mm_tn_each([(o_a, dpa), (o_b, dpb), (o_m, dpm)], "mm_dw_o")
    grads['w_gate'] = _mm_cols(h_t, dgpre, "mm_dw_gate")
    dq_m, dmk, dmv, grads['m_q_norm'] = _mem_attn_bwd(proj, 6, mk, mv, w['m_q_norm'], do_m, "mem_attn_bwd")
    grads['w_mem_kv'], grads['mem_norm'], grads['m_k_norm'] = _mem_kv_bwd(
        mem, w['mem_norm'], w['w_mem_kv'], w['m_k_norm'], dmk, dmv, "mem_kv_bwd")
    a_gain = w['a_q_norm']
    if hooks is not None:
        a_gain = a_gain + hooks.grads('mix', grads)[0:1, 0:1]

    prep = _bwd_prep(do_a, o_a, lse_a, dils, None, "bwd_prep_a")
    dproj, dgq_a, dgk_a = [], [], []
    for gi, (window, d) in enumerate(A_GROUPS):
        gq, gk = _two(a_gain[gi]), _two(w['a_k_norm'][gi])
        dqkv = _band_bwd(qkv_a[gi], prep[3 * gi], prep[3 * gi + 1], prep[3 * gi + 2], 2 * d, window // d,
                         f"band_bwd_a{gi}")
        dp, dgq, dgk = _qk_prep_bwd(dqkv, proj, 6 * gi, d, False, gq, gk, tabs, f"qk_prep_bwd_a{gi}")
        dproj.append(dp)
        dgq_a.append(_fold_heads(dgq))
        dgk_a.append(_fold_heads(dgk))
    grads['a_q_norm'] = jnp.stack(dgq_a)
    grads['a_k_norm'] = jnp.stack(dgk_a)

    do_bu, lse_bu, delta_bu, dsink = _bwd_prep(do_b, o_b, lse_b, (1,), sinks, "bwd_prep_b")
    dqkv = _band_bwd(qkv_b, do_bu, lse_bu, delta_bu, 4, B_WINDOW - 1, "band_bwd_b")
    dp_b, dgq, dgk = _qk_prep_bwd(dqkv, proj, 18, 1, True, gbq, gbk, tabs, "qk_prep_bwd_b")
    dproj.append(dp_b)
    grads['b_q_norm'] = _fold_heads(dgq)
    grads['b_k_norm'] = _fold_heads(dgk)
    grads['b_sinks'] = jnp.stack([dsink[:, 0, 0], dsink[:, 0, HEAD_DIM]], axis=1).reshape(8)

    dproj.append(dq_m)

    cols = (0, 1, 2, 3, 6)
    grads['w_in'] = _mm_rows_cat(h_t, dproj, "mm_dw_in")
    attn_gain = w['attn_norm']
    if hooks is not None:
        attn_gain = attn_gain + hooks.grads('in', grads)[0:1, 0:1]
    grad_x, grads['attn_norm'] = _mm_rows(
        [(dp, w['w_in'], c) for dp, c in zip(dproj, cols)] + [(dgpre, w['w_gate'], 0)], "mm_d_h", nt=True,
        rms_bwd=(x, r1, attn_gain, dx1))
    return loss, grad_x, grads


def _coords():
    return lax.axis_index("x"), lax.axis_index("y"), lax.axis_index("c")


def _slot(p):
    return 4 * p[0] + 2 * p[1] + p[2]


ALL_PEERS = tuple(range(1, N_DEV))
CHIP_PEERS = (1, 4, 2, 6)
OTHER_CHIPS = (4, 2, 6)


def _peers(me, masks=ALL_PEERS):
    x, y, c = me
    return [(1 - x if mask & 4 else x, 1 - y if mask & 2 else y, 1 - c if mask & 1 else c) for mask in masks]


HBM_SPEC = pl.BlockSpec(memory_space=pltpu.HBM)


SEM_SPEC = pl.BlockSpec(memory_space=pltpu.SEMAPHORE)
SIDE_EFFECT = pltpu.SideEffectType.DATAFLOW_SIDE_EFFECTING


def _exchange_start(blocks, name, gather=False, masks=ALL_PEERS, after=None):
    n = len(blocks)
    n_peers = len(masks)
    n_in = 2 * n + (0 if after is None else 1)

    def body(*refs):
        ins, lands = refs[:n], refs[n:2 * n]
        send_sems, recv_sems = refs[n_in], refs[n_in + 1]
        token = refs[-1]
        me = _coords()
        peers = _peers(me, masks)
        for a in range(n):
            for k in range(n_peers):
                pltpu.make_async_remote_copy(
                    src_ref=ins[a] if gather else ins[a].at[_slot(peers[k])], dst_ref=lands[a].at[_slot(me)],
                    send_sem=send_sems.at[a * n_peers + k], recv_sem=recv_sems.at[a * n_peers + k],
                    device_id=peers[k], device_id_type=MESH).start()
        token[...] = jnp.zeros_like(token)

    land_shapes = [((N_DEV,) + b.shape) if gather else b.shape for b in blocks]
    hbm_in = [pltpu.HBM(b.shape, b.dtype) for b in blocks]
    hbm_land = [pltpu.HBM(s, b.dtype) for s, b in zip(land_shapes, blocks)]
    sems = pltpu.SemaphoreType.DMA((n * n_peers,))
    ins = [pltpu.with_memory_space_constraint(b, pltpu.HBM) for b in blocks]
    lands = [pltpu.with_memory_space_constraint(lax.empty(s, b.dtype), pltpu.HBM) for s, b in zip(land_shapes, blocks)]
    return pl.pallas_call(
        body, name=name, out_shape=(sems, sems, *hbm_in, *hbm_land, jax.ShapeDtypeStruct((8, LANES), F32)),
        in_specs=[HBM_SPEC] * (2 * n) + ([] if after is None else [pl.BlockSpec(memory_space=pl.ANY)]),
        out_specs=(SEM_SPEC, SEM_SPEC, *([HBM_SPEC] * (2 * n)), pl.BlockSpec(memory_space=pltpu.VMEM)),
        input_output_aliases={i: 2 + i for i in range(2 * n)},
        compiler_params=pltpu.CompilerParams(has_side_effects=SIDE_EFFECT),
    )(*ins, *lands, *([] if after is None else [after]))


def _exchange_wait(started, after, name, gather=False, masks=ALL_PEERS):
    n = (len(started) - 3) // 2
    n_peers = len(masks)
    send_sems, recv_sems = started[0], started[1]
    thru = started[2:2 + 2 * n]

    def body(*refs):
        ins, lands = refs[:n], refs[n:2 * n]
        send_ref, recv_ref = refs[2 * n], refs[2 * n + 1]
        me = _coords()
        peers = _peers(me, masks)
        for a in range(n):
            for k in range(n_peers):
                cp = pltpu.make_async_remote_copy(
                    src_ref=ins[a] if gather else ins[a].at[_slot(peers[k])], dst_ref=lands[a].at[_slot(peers[k])],
                    send_sem=send_ref.at[a * n_peers + k], recv_sem=recv_ref.at[a * n_peers + k],
                    device_id=peers[k], device_id_type=MESH)
                cp.wait_send()
                cp.wait_recv()

    hbm = [pltpu.HBM(t.shape, t.dtype) for t in thru]
    res = pl.pallas_call(
        body, name=name, out_shape=tuple(hbm),
        in_specs=[HBM_SPEC] * (2 * n) + [SEM_SPEC, SEM_SPEC] + [pl.BlockSpec(memory_space=pl.ANY)] * len(after),
        out_specs=tuple([HBM_SPEC] * (2 * n)), input_output_aliases={i: i for i in range(2 * n)},
        compiler_params=pltpu.CompilerParams(has_side_effects=SIDE_EFFECT),
    )(*thru, send_sems, recv_sems, *after)
    return res[n:]


def _sibling_forward(arrays, name):
    n = len(arrays)
    n_fwd = len(OTHER_CHIPS)

    def body(*refs):
        bufs = refs[n:2 * n]
        token, send_sems, recv_sems = refs[2 * n:]
        token[...] = jnp.zeros_like(token)
        x, y, c = _coords()
        sibling = (x, y, 1 - c)
        mine = _peers((x, y, c), OTHER_CHIPS)
        theirs = _peers(sibling, OTHER_CHIPS)

        def copy(a, k, block):
            rows = bufs[a].at[_slot(block)]
            return pltpu.make_async_remote_copy(
                src_ref=rows, dst_ref=rows, send_sem=send_sems.at[a * n_fwd + k], recv_sem=recv_sems.at[a * n_fwd + k],
                device_id=sibling, device_id_type=MESH)

        sends = [copy(a, k, mine[k]) for a in range(n) for k in range(n_fwd)]
        for cp in sends:
            cp.start()
        for a in range(n):
            for k in range(n_fwd):
                copy(a, k, theirs[k]).wait_recv()
        for cp in sends:
            cp.wait_send()

    res = pl.pallas_call(
        body, name=name, in_specs=[HBM_SPEC] * n,
        out_specs=tuple([HBM_SPEC] * n + [pl.BlockSpec(memory_space=pltpu.VMEM)]),
        out_shape=tuple([jax.ShapeDtypeStruct(a.shape, a.dtype) for a in arrays] + [jax.ShapeDtypeStruct((8, LANES), F32)]),
        input_output_aliases={i: i for i in range(n)},
        scratch_shapes=[pltpu.SemaphoreType.DMA((n * n_fwd,)), pltpu.SemaphoreType.DMA((n * n_fwd,))],
    )(*arrays)
    return res[:n], res[n]


def _all_sum(p, name):
    def body(p_ref, o_ref, recv, send_sems, recv_sems):
        me = _coords()
        peers = _peers(me)
        recv[_slot(me)] = p_ref[...]

        def copy(k, landing):
            return pltpu.make_async_remote_copy(
                src_ref=p_ref, dst_ref=recv.at[_slot(landing)], send_sem=send_sems.at[k], recv_sem=recv_sems.at[k],
                device_id=peers[k], device_id_type=MESH)

        sends = [copy(k, me) for k in range(N_DEV - 1)]
        for cp in sends:
            cp.start()
        for k in range(N_DEV - 1):
            copy(k, peers[k]).wait_recv()
        for cp in sends:
            cp.wait_send()
        acc = recv[0]
        for s in range(1, N_DEV):
            acc = acc + recv[s]
        o_ref[...] = acc

    vmem = pl.BlockSpec(memory_space=pltpu.VMEM)
    return pl.pallas_call(
        body, name=name, in_specs=[vmem], out_specs=vmem, out_shape=jax.ShapeDtypeStruct(p.shape, F32),
        scratch_shapes=[pltpu.VMEM((N_DEV,) + p.shape, F32), pltpu.SemaphoreType.DMA((N_DEV - 1,)),
                        pltpu.SemaphoreType.DMA((N_DEV - 1,))],
    )(p)


def _adam(w, g, m, v):
    m2 = ADAM_B1 * m + (1.0 - ADAM_B1) * g
    v2 = ADAM_B2 * v + (1.0 - ADAM_B2) * (g * g)
    m_hat = m2 / (1.0 - ADAM_B1 ** ADAM_STEP)
    v_hat = v2 / (1.0 - ADAM_B2 ** ADAM_STEP)
    delta = -ADAM_LR * (m_hat / (jnp.sqrt(v_hat) + ADAM_EPS) + ADAM_WD * w)
    return delta, m2, v2


def _row_tile(rows, cols):
    best = rows
    for t in range(16, rows, 16):
        if rows % t == 0 and t * cols * 4 <= (1 << 20):
            best = t
    return best


def _adam_reduce(parts, w, m, v, name):
    rows, cols = w.shape
    tr = _row_tile(rows, cols)

    def body(p_ref, w_ref, m_ref, v_ref, g_ref, d_ref, m2_ref, v2_ref):
        g = p_ref[0].astype(F32)
        for s in range(1, N_DEV):
            g = g + p_ref[s].astype(F32)
        g_ref[...] = g
        d_ref[...], m2_ref[...], v2_ref[...] = _adam(w_ref[...], g, m_ref[...], v_ref[...])

    blk = pl.BlockSpec((tr, cols), lambda i: (i, 0))
    shp = jax.ShapeDtypeStruct((rows, cols), F32)
    return pl.pallas_call(
        body, name=name, grid=(rows // tr,),
        in_specs=[pl.BlockSpec((N_DEV, tr, cols), lambda i: (0, i, 0)), blk, blk, blk],
        out_specs=(blk,) * 4, out_shape=(shp,) * 4, compiler_params=_cparams(1),
    )(parts, w, m, v)


PACK_COLS = 1024
PACK = {'attn_norm': (0, 1, 1024), 'mem_norm': (1, 1, 1024), 'ffn_norm': (2, 1, 1024), 'b_gate': (3, 3, 1024),
        'conv_b': (6, 6, 1024), 'a_q_norm': (12, 3, 64), 'a_k_norm': (15, 3, 64), 'b_q_norm': (18, 1, 64),
        'b_k_norm': (19, 1, 64), 'm_q_norm': (20, 1, 128), 'm_k_norm': (21, 1, 128), 'b_sinks': (22, 1, 8)}
PACK_LOSS_ROW = 23
PACK_ROWS = 24


def _pack_pieces(name, width):
    r0, nr, lanes = PACK[name]
    out = []
    for j in range(nr):
        if lanes == PACK_COLS:
            w = min(PACK_COLS, width - j * PACK_COLS)
            out.append((r0 + j, slice(0, 1), slice(j * PACK_COLS, j * PACK_COLS + w), w))
        else:
            out.append((r0 + j, slice(j, j + 1), slice(0, lanes), lanes))
    return out


def _pack_small(grads, loss_tile, name):
    names = list(PACK)

    def body(*refs):
        o_ref = refs[-1]
        o_ref[...] = jnp.zeros_like(o_ref)
        for k, nm in enumerate(names):
            for row, rs, ls, w in _pack_pieces(nm, refs[k].shape[1]):
                o_ref[row:row + 1, 0:w] = refs[k][rs, ls]
        o_ref[PACK_LOSS_ROW:PACK_LOSS_ROW + 1, 0:1] = refs[len(names)][0:1, 0:1]

    vmem = pl.BlockSpec(memory_space=pltpu.VMEM)
    args = [grads[nm] for nm in names] + [loss_tile]
    return pl.pallas_call(body, name=name, in_specs=[vmem] * len(args), out_specs=vmem,
                          out_shape=jax.ShapeDtypeStruct((PACK_ROWS, PACK_COLS), F32))(*args)


def _adam_small(gsum, ws, ms, vs, name):
    names = list(PACK)
    n = len(names)

    def body(*refs):
        g_ref = refs[0]
        w_refs, m_refs, v_refs = refs[1:1 + n], refs[1 + n:1 + 2 * n], refs[1 + 2 * n:1 + 3 * n]
        outs = refs[1 + 3 * n:]
        outs[0][...] = g_ref[PACK_LOSS_ROW:PACK_LOSS_ROW + 1, 0:1]
        for k, nm in enumerate(names):
            o_g, o_d, o_m, o_v = outs[1 + 4 * k:5 + 4 * k]
            for row, rs, ls, width in _pack_pieces(nm, w_refs[k].shape[1]):
                src = (rs, ls)
                g = g_ref[row:row + 1, 0:width]
                d, m2, v2 = _adam(w_refs[k][src], g, m_refs[k][src], v_refs[k][src])
                o_g[src] = g
                o_d[src] = d
                o_m[src] = m2
                o_v[src] = v2

    vmem = pl.BlockSpec(memory_space=pltpu.VMEM)
    shapes = [jax.ShapeDtypeStruct((1, 1), F32)]
    for nm in names:
        shapes += [jax.ShapeDtypeStruct(ws[nm].shape, F32)] * 4
    args = [gsum] + [ws[nm] for nm in names] + [ms[nm] for nm in names] + [vs[nm] for nm in names]
    return pl.pallas_call(
        body, name=name, in_specs=[vmem] * len(args), out_specs=tuple([vmem] * len(shapes)), out_shape=tuple(shapes),
    )(*args)


def _as2d(name, a):
    return a.reshape(a.shape[-2], a.shape[-1]) if a.ndim == 3 else a


def kernel(x, mem, positions, attn_norm, w_in, a_q_norm, a_k_norm, b_q_norm, b_k_norm, b_sinks, mem_norm, w_mem_kv, m_q_norm, m_k_norm, w_o_a, w_o_b, w_o_m, w_gate, b_gate, w_out, ffn_norm, w_up, conv_w, conv_b, w_down, loss_target, m_attn_norm, m_w_in, m_a_q_norm, m_a_k_norm, m_b_q_norm, m_b_k_norm, m_b_sinks, m_mem_norm, m_w_mem_kv, m_m_q_norm, m_m_k_norm, m_w_o_a, m_w_o_b, m_w_o_m, m_w_gate, m_b_gate, m_w_out, m_ffn_norm, m_w_up, m_conv_w, m_conv_b, m_w_down, v_attn_norm, v_w_in, v_a_q_norm, v_a_k_norm, v_b_q_norm, v_b_k_norm, v_b_sinks, v_mem_norm, v_w_mem_kv, v_m_q_norm, v_m_k_norm, v_w_o_a, v_w_o_b, v_w_o_m, v_w_gate, v_b_gate, v_w_out, v_ffn_norm, v_w_up, v_conv_w, v_conv_b, v_w_down):
    given = dict(attn_norm=attn_norm, w_in=w_in, a_q_norm=a_q_norm, a_k_norm=a_k_norm, b_q_norm=b_q_norm, b_k_norm=b_k_norm, b_sinks=b_sinks, mem_norm=mem_norm, w_mem_kv=w_mem_kv, m_q_norm=m_q_norm, m_k_norm=m_k_norm, w_o_a=w_o_a, w_o_b=w_o_b, w_o_m=w_o_m, w_gate=w_gate, b_gate=b_gate, w_out=w_out, ffn_norm=ffn_norm, w_up=w_up, conv_w=conv_w, conv_b=conv_b, w_down=w_down)
    mom1 = dict(attn_norm=m_attn_norm, w_in=m_w_in, a_q_norm=m_a_q_norm, a_k_norm=m_a_k_norm, b_q_norm=m_b_q_norm, b_k_norm=m_b_k_norm, b_sinks=m_b_sinks, mem_norm=m_mem_norm, w_mem_kv=m_w_mem_kv, m_q_norm=m_m_q_norm, m_k_norm=m_m_k_norm, w_o_a=m_w_o_a, w_o_b=m_w_o_b, w_o_m=m_w_o_m, w_gate=m_w_gate, b_gate=m_b_gate, w_out=m_w_out, ffn_norm=m_ffn_norm, w_up=m_w_up, conv_w=m_conv_w, conv_b=m_conv_b, w_down=m_w_down)
    mom2 = dict(attn_norm=v_attn_norm, w_in=v_w_in, a_q_norm=v_a_q_norm, a_k_norm=v_a_k_norm, b_q_norm=v_b_q_norm, b_k_norm=v_b_k_norm, b_sinks=v_b_sinks, mem_norm=v_mem_norm, w_mem_kv=v_w_mem_kv, m_q_norm=v_m_q_norm, m_k_norm=v_m_k_norm, w_o_a=v_w_o_a, w_o_b=v_w_o_b, w_o_m=v_w_o_m, w_gate=v_w_gate, b_gate=v_b_gate, w_out=v_w_out, ffn_norm=v_ffn_norm, w_up=v_w_up, conv_w=v_conv_w, conv_b=v_conv_b, w_down=v_w_down)

    big = list(BIG)
    stages = {'mix': list(MIX_WEIGHTS), 'ffn': list(FFN_WEIGHTS), 'in': ['w_in']}
    my_slot = _slot(_coords())

    def shard(n):
        return given[n][0] if n == 'conv_w' else given[n][0].astype(BF16)

    def whole(n, g):
        _, r, c = g.shape
        return g.reshape(N_DEV * r, c) if BIG[n] == 0 else g.transpose(1, 0, 2).reshape(r, N_DEV * c)

    def to_blocks(n, g):
        r, c = given[n].shape[1:]
        g = g.reshape(N_DEV, r, c) if BIG[n] == 0 else g.reshape(r, N_DEV, c).transpose(1, 0, 2)
        return g if n == 'conv_w' else g.astype(BF16)

    class Hooks:
        next_stage = {'in': 'mix', 'mix': 'ffn'}

        def __init__(self):
            self.coming, self.sent = {}, {}
            self.shards = {n: shard(n) for n in big}
            self.start_gather('in', None)

        def start_gather(self, stage, after):
            src = [self.shards[n] for n in stages[stage]]
            self.coming[stage] = _exchange_start(src, f"gather_{stage}_start", gather=True, masks=CHIP_PEERS,
                                                 after=after)

        def weights(self, stage, after):
            names = stages[stage]
            after = list(after)
            if stage == 'in':
                after += [self.shards[n] for n in stages['mix'] + stages['ffn']]
            landed = _exchange_wait(self.coming[stage], after, f"gather_{stage}_wait", gather=True, masks=CHIP_PEERS)
            landed, token = _sibling_forward(landed, f"gather_{stage}_forward")
            if stage in self.next_stage:
                self.start_gather(self.next_stage[stage], token)
            return {n: whole(n, lax.dynamic_update_slice_in_dim(land, self.shards[n][None], my_slot, axis=0))
                    for n, land in zip(names, landed)}

        def grads(self, stage, g):
            blocks = [to_blocks(n, g[n]) for n in stages[stage]]
            own = [lax.dynamic_slice_in_dim(b, my_slot, 1, axis=0) for b in blocks]
            self.sent[stage] = (_exchange_start(blocks, f"exchange_{stage}_start"), own)
            return self.sent[stage][0][-1]

        def parts(self, stage, after):
            started, own = self.sent[stage]
            landed = _exchange_wait(started, [after], f"exchange_{stage}_wait")
            return {n: lax.dynamic_update_slice_in_dim(land, o, my_slot, axis=0)
                    for n, land, o in zip(stages[stage], landed, own)}

    hooks = Hooks()
    w = {}
    for n in SMALL:
        w[n] = given[n]
    w['a_q_norm'], w['a_k_norm'] = given['a_q_norm'][0], given['a_k_norm'][0]
    w['b_q_norm'], w['b_k_norm'], w['b_sinks'] = given['b_q_norm'][0], given['b_k_norm'][0], given['b_sinks'][0]

    loss_tile, grad_x, grads = _device_step(x[0], mem[0], positions[0], loss_target[0], w, hooks)
    out = {}
    after = grad_x
    for stage in ('ffn', 'mix', 'in'):
        for n, p in hooks.parts(stage, after).items():
            res = _adam_reduce(p, given[n][0], mom1[n][0], mom2[n][0], f"adam_{n}")
            out[n] = tuple(t[None] for t in res)
            after = res[0]

    small = {n: grads[n] for n in PACK}
    small['b_q_norm'], small['b_k_norm'] = grads['b_q_norm'].reshape(1, -1), grads['b_k_norm'].reshape(1, -1)
    small['b_sinks'] = grads['b_sinks'].reshape(1, -1)
    gsum = _all_sum(_pack_small(small, loss_tile, "pack_small"), "sum_small")
    ws = {n: _as2d(n, given[n]) for n in PACK}
    ms = {n: _as2d(n, mom1[n]) for n in PACK}
    vs = {n: _as2d(n, mom2[n]) for n in PACK}
    res = _adam_small(gsum, ws, ms, vs, "adam_small")
    loss = res[0].reshape(())
    for k, n in enumerate(PACK):
        out[n] = tuple(t.reshape(given[n].shape) for t in res[1 + 4 * k:5 + 4 * k])

    outs = [loss, grad_x[None]]
    for field in range(4):
        outs += [out[n][field] for n in WEIGHTS]
    return tuple(outs)
```

```python
import functools
import math

import jax
import jax.numpy as jnp
from jax import lax
from jax.experimental import pallas as pl
from jax.experimental.pallas import tpu as pltpu

F32 = jnp.float32
BF16 = jnp.bfloat16

N_DEV = 8
HEAD_DIM = 64
A_GROUPS = ((128, 1), (512, 4), (2048, 16))
B_WINDOW = 128
M_HEADS = 4
M_HEAD_DIM = 128
MEM_LEN = 256
D_FF = 2816
ROPE_THETA = 500000.0
ROPE_DIMS = 16
BLOCK = 128
EPS = 1e-6
LANES = 128
BAND_Q_BLOCKS = 4
BAND_UNITS = 2

ADAM_LR = 0.001
ADAM_B1 = 0.9
ADAM_B2 = 0.999
ADAM_EPS = 1e-08
ADAM_WD = 0.01
ADAM_STEP = 10

VMEM_LIMIT_BYTES = 56 * 1024 * 1024
GRAD_DTYPE = BF16
MESH = pl.DeviceIdType.MESH

WEIGHTS = ['attn_norm', 'w_in', 'a_q_norm', 'a_k_norm', 'b_q_norm', 'b_k_norm', 'b_sinks', 'mem_norm',
           'w_mem_kv', 'm_q_norm', 'm_k_norm', 'w_o_a', 'w_o_b', 'w_o_m', 'w_gate', 'b_gate', 'w_out',
           'ffn_norm', 'w_up', 'conv_w', 'conv_b', 'w_down']
BIG = {'w_in': 1, 'w_mem_kv': 0, 'w_o_a': 1, 'w_o_b': 1, 'w_o_m': 1, 'w_gate': 1, 'w_out': 0, 'w_up': 1,
       'conv_w': 1, 'w_down': 0}
SMALL = [n for n in WEIGHTS if n not in BIG]


def _cparams(n_grid):
    return pltpu.CompilerParams(dimension_semantics=("arbitrary",) * n_grid, vmem_limit_bytes=VMEM_LIMIT_BYTES)


def _seg_matrix(width):
    shift = width.bit_length() - 1
    r = lax.shift_right_logical(lax.broadcasted_iota(jnp.int32, (LANES, LANES), 0), shift)
    c = lax.shift_right_logical(lax.broadcasted_iota(jnp.int32, (LANES, LANES), 1), shift)
    return jnp.where(r == c, 1.0, 0.0).astype(BF16)


def _seg_sum(x, seg):
    hi = x.astype(BF16)
    r1 = x - hi.astype(F32)
    mid = r1.astype(BF16)
    lo = (r1 - mid.astype(F32)).astype(BF16)
    dot = functools.partial(jnp.dot, preferred_element_type=F32)
    return dot(hi, seg) + dot(mid, seg) + dot(lo, seg)


def _rope(y, c, s1, s2):
    return y * c + pltpu.roll(y, LANES - ROPE_DIMS // 2, 1) * s1 + pltpu.roll(y, ROPE_DIMS // 2, 1) * s2


def _unrope(dy, c, s1, s2):
    return dy * c + pltpu.roll(dy * s1, ROPE_DIMS // 2, 1) + pltpu.roll(dy * s2, LANES - ROPE_DIMS // 2, 1)


def _sigmoid(x):
    return 1.0 / (1.0 + jnp.exp(-x))


def _rms_fwd(x, gain, name):
    s_len, d = x.shape
    tm = 512

    def body(x_ref, g_ref, h_ref, ht_ref, r_ref):
        xv = x_ref[...]
        r = lax.rsqrt(jnp.mean(xv * xv, axis=-1, keepdims=True) + EPS)
        h = ((xv * r) * g_ref[...]).astype(BF16)
        h_ref[...] = h
        ht_ref[...] = h.T
        r_ref[...] = r

    return pl.pallas_call(
        body, name=name, grid=(s_len // tm,),
        in_specs=[pl.BlockSpec((tm, d), lambda i: (i, 0)), pl.BlockSpec((1, d), lambda i: (0, 0))],
        out_specs=(pl.BlockSpec((tm, d), lambda i: (i, 0)), pl.BlockSpec((d, tm), lambda i: (0, i)),
                   pl.BlockSpec((tm, 1), lambda i: (i, 0))),
        out_shape=(jax.ShapeDtypeStruct((s_len, d), BF16), jax.ShapeDtypeStruct((d, s_len), BF16),
                   jax.ShapeDtypeStruct((s_len, 1), F32)),
        compiler_params=_cparams(1),
    )(x, gain)


def _resident(shape, index_map):
    return pl.BlockSpec(shape, index_map, pipeline_mode=pl.Buffered(1))


def _mm_rows(pairs, name, nt=False, tm=512, bias=None, sigmoid=False, res=None, out_dtypes=(F32,), loss_target=None,
             rms_bwd=None):
    m = pairs[0][0].shape[0]
    n = pairs[0][1].shape[0] if nt else pairs[0][1].shape[1]
    n_pairs = len(pairs)
    has_bias, has_res, has_loss = bias is not None, res is not None, loss_target is not None
    has_rms = rms_bwd is not None
    dims = (((1,), (1,)), ((), ())) if nt else (((1,), (0,)), ((), ()))

    def body(*refs):
        acc = None
        for p in range(n_pairs):
            t = lax.dot_general(refs[2 * p][...].astype(BF16), refs[2 * p + 1][...], dims, preferred_element_type=F32)
            acc = t if acc is None else acc + t
        pos = 2 * n_pairs
        if has_bias:
            acc = acc + refs[pos][...]
            pos += 1
        if sigmoid:
            acc = _sigmoid(acc)
        if has_res:
            acc = refs[pos][...] + acc
            pos += 1
        if has_loss:
            dy_ref, dyb_ref, da_ref, l_ref = refs[pos + 1:]

            @pl.when(pl.program_id(0) == 0)
            def _():
                l_ref[...] = jnp.zeros_like(l_ref)
            err = acc - refs[pos][...]
            dy = err * (1.0 / n)
            dy_ref[...] = dy
            dyb_ref[...] = dy.astype(BF16)
            da_ref[...] = lax.dot_general(dy.astype(BF16), refs[1][...], (((1,), (1,)), ((), ())),
                                          preferred_element_type=F32).astype(BF16)
            part = 0.5 * jnp.sum(jnp.mean(err * err, axis=-1, keepdims=True), axis=0, keepdims=True)
            l_ref[...] += jnp.broadcast_to(part, l_ref.shape)
            return
        if has_rms:
            x_ref, r_ref, g_ref, add_ref = refs[pos:pos + 4]
            dg_ref = refs[-1]

            @pl.when(pl.program_id(0) == 0)
            def _():
                dg_ref[...] = jnp.zeros_like(dg_ref)
            rv = r_ref[...]
            xhat = x_ref[...] * rv
            dg_ref[...] += jnp.sum(acc * xhat, axis=0, keepdims=True)
            dxhat = acc * g_ref[...]
            acc = add_ref[...] + rv * (dxhat - xhat * jnp.mean(dxhat * xhat, axis=-1, keepdims=True))
            for o_ref in refs[pos + 4:-1]:
                o_ref[...] = acc.astype(o_ref.dtype)
            return
        for o_ref in refs[pos:]:
            o_ref[...] = acc.astype(o_ref.dtype)

    in_specs, args = [], []
    for a, w, blk in pairs:
        k = a.shape[1]
        in_specs.append(pl.BlockSpec((tm, k), lambda i: (i, 0)))
        if nt:
            in_specs.append(_resident((n, k), lambda i, blk=blk: (0, blk)))
        else:
            in_specs.append(_resident((k, n), lambda i, blk=blk: (blk, 0)))
        args += [a, w]
    if has_bias:
        in_specs.append(_resident((1, n), lambda i: (0, 0)))
        args.append(bias)
    if has_res:
        in_specs.append(pl.BlockSpec((tm, n), lambda i: (i, 0)))
        args.append(res)
    out = pl.BlockSpec((tm, n), lambda i: (i, 0))
    if has_loss:
        k0 = pairs[0][0].shape[1]
        return pl.pallas_call(
            body, name=name, grid=(m // tm,), in_specs=in_specs + [out],
            out_specs=(out, out, pl.BlockSpec((tm, k0), lambda i: (i, 0)), pl.BlockSpec((8, LANES), lambda i: (0, 0))),
            out_shape=(jax.ShapeDtypeStruct((m, n), F32), jax.ShapeDtypeStruct((m, n), BF16),
                       jax.ShapeDtypeStruct((m, k0), BF16), jax.ShapeDtypeStruct((8, LANES), F32)),
            compiler_params=_cparams(1),
        )(*args, loss_target)
    if has_rms:
        x, r, gain, add = rms_bwd
        vec = _resident((1, n), lambda i: (0, 0))
        return pl.pallas_call(
            body, name=name, grid=(m // tm,),
            in_specs=in_specs + [out, pl.BlockSpec((tm, 1), lambda i: (i, 0)), vec, out],
            out_specs=tuple([out] * len(out_dtypes) + [pl.BlockSpec((1, n), lambda i: (0, 0))]),
            out_shape=tuple([jax.ShapeDtypeStruct((m, n), dt) for dt in out_dtypes] + [jax.ShapeDtypeStruct((1, n), F32)]),
            compiler_params=_cparams(1),
        )(*args, x, r, gain, add)
    outs = pl.pallas_call(
        body, name=name, grid=(m // tm,), in_specs=in_specs, out_specs=tuple([out] * len(out_dtypes)),
        out_shape=tuple(jax.ShapeDtypeStruct((m, n), dt) for dt in out_dtypes), compiler_params=_cparams(1),
    )(*args)
    return outs[0] if len(out_dtypes) == 1 else outs


def _mm_rows_each(pairs, name, tm=512):
    m = pairs[0][0].shape[0]
    n_pairs = len(pairs)

    def body(*refs):
        for p in range(n_pairs):
            refs[2 * n_pairs + p][...] = lax.dot_general(refs[2 * p][...].astype(BF16), refs[2 * p + 1][...],
                                                         (((1,), (1,)), ((), ())), preferred_element_type=F32)

    in_specs, args = [], []
    for a, w in pairs:
        in_specs += [pl.BlockSpec((tm, a.shape[1]), lambda i: (i, 0)), _resident(w.shape, lambda i: (0, 0))]
        args += [a, w]
    return pl.pallas_call(
        body, name=name, grid=(m // tm,), in_specs=in_specs,
        out_specs=tuple(pl.BlockSpec((tm, w.shape[0]), lambda i: (i, 0)) for _, w in pairs),
        out_shape=tuple(jax.ShapeDtypeStruct((m, w.shape[0]), F32) for _, w in pairs), compiler_params=_cparams(1),
    )(*args)


def _mm_rows_cat(a, ws, name, tm=256):
    m, k = a.shape
    widths = [w.shape[1] for w in ws]
    n = sum(widths)

    def body(*refs):
        a_ref, o_ref = refs[0], refs[-1]
        av = a_ref[...]
        off = 0
        for p, width in enumerate(widths):
            o_ref[:, off:off + width] = jnp.dot(av, refs[1 + p][...], preferred_element_type=F32).astype(GRAD_DTYPE)
            off += width

    return pl.pallas_call(
        body, name=name, grid=(m // tm,),
        in_specs=[pl.BlockSpec((tm, k), lambda i: (i, 0))] + [_resident((k, wd), lambda i: (0, 0)) for wd in widths],
        out_specs=pl.BlockSpec((tm, n), lambda i: (i, 0)),
        out_shape=jax.ShapeDtypeStruct((m, n), GRAD_DTYPE), compiler_params=_cparams(1),
    )(a, *ws)


def _mm_cols(a, bs, name, tn=256):
    m, k = a.shape
    counts = [b.shape[1] // tn for b in bs]
    starts = [sum(counts[:p]) for p in range(len(bs))]

    def body(*refs):
        a_ref, o_ref = refs[0], refs[-1]
        j = pl.program_id(0)
        for p, b_ref in enumerate(refs[1:-1]):
            @pl.when((j >= starts[p]) & (j < starts[p] + counts[p]))
            def _():
                o_ref[...] = jnp.dot(a_ref[...], b_ref[...].astype(BF16), preferred_element_type=F32).astype(GRAD_DTYPE)

    b_specs = [pl.BlockSpec((k, tn), lambda j, s=s, c=c: (0, jnp.clip(j - s, 0, c - 1))) for s, c in zip(starts, counts)]
    return pl.pallas_call(
        body, name=name, grid=(sum(counts),),
        in_specs=[_resident((m, k), lambda j: (0, 0))] + b_specs,
        out_specs=pl.BlockSpec((m, tn), lambda j: (0, j)),
        out_shape=jax.ShapeDtypeStruct((m, sum(counts) * tn), GRAD_DTYPE), compiler_params=_cparams(1),
    )(a, *bs)


def _mm_tn_each(pairs, name, tile=256):
    n = pairs[0][1].shape[1]
    n_pairs = len(pairs)
    dims = (((0,), (0,)), ((), ()))

    def body(*refs):
        for p in range(n_pairs):
            refs[2 * n_pairs + p][...] = lax.dot_general(refs[2 * p][...].astype(BF16), refs[2 * p + 1][...].astype(BF16),
                                                         dims, preferred_element_type=F32).astype(GRAD_DTYPE)

    in_specs, args = [], []
    for a, b in pairs:
        in_specs += [_resident(a.shape, lambda j: (0, 0)), pl.BlockSpec((b.shape[0], tile), lambda j: (0, j))]
        args += [a, b]
    return pl.pallas_call(
        body, name=name, grid=(n // tile,), in_specs=in_specs,
        out_specs=tuple(pl.BlockSpec((a.shape[1], tile), lambda j: (0, j)) for a, _ in pairs),
        out_shape=tuple(jax.ShapeDtypeStruct((a.shape[1], n), GRAD_DTYPE) for a, _ in pairs),
        compiler_params=_cparams(1),
    )(*args)


def _mm_tn(a, b, name, tile=256):
    k, m = a.shape
    n = b.shape[1]
    dims = (((0,), (0,)), ((), ()))

    def body(a_ref, b_ref, o_ref):
        o_ref[...] = lax.dot_general(a_ref[...].astype(BF16), b_ref[...].astype(BF16), dims,
                                     preferred_element_type=F32).astype(GRAD_DTYPE)

    if n <= m:
        t = min(tile, m)
        grid, a_spec, b_spec = (m // t,), pl.BlockSpec((k, t), lambda i: (0, i)), _resident((k, n), lambda i: (0, 0))
        o_spec = pl.BlockSpec((t, n), lambda i: (i, 0))
    else:
        t = min(tile, n)
        grid, a_spec, b_spec = (n // t,), _resident((k, m), lambda i: (0, 0)), pl.BlockSpec((k, t), lambda i: (0, i))
        o_spec = pl.BlockSpec((m, t), lambda i: (0, i))
    return pl.pallas_call(
        body, name=name, grid=grid, in_specs=[a_spec, b_spec], out_specs=o_spec,
        out_shape=jax.ShapeDtypeStruct((m, n), GRAD_DTYPE), compiler_params=_cparams(1),
    )(a, b)


def _norm_rope(t, gain, c, s1, s2, seg):
    rs = lax.rsqrt(_seg_sum(t * t, seg) * (1.0 / HEAD_DIM) + EPS)
    return _rope((t * rs) * gain, c, s1, s2)


def _dup_half(y, half):
    lane = lax.broadcasted_iota(jnp.int32, y.shape, 1)
    rolled = pltpu.roll(y, HEAD_DIM, 1)
    keep = (lane < HEAD_DIM) if half == 0 else (lane >= HEAD_DIM)
    return jnp.where(keep, y, rolled)


def _qk_prep(proj, cb0, d, gqa, gq, gk, tabs, name):
    s_len = proj.shape[0]
    tm = 512
    rows = tm // d
    n_units = 4 if gqa else 2 * d
    n_q = 4 if gqa else 2
    n_in = 6

    def body(*refs):
        in_refs = refs[:n_in]
        gq_ref, gk_ref, c_ref, s1_ref, s2_ref, o_ref = refs[n_in:]
        seg = _seg_matrix(HEAD_DIM)

        def rows_of(ref, r):
            return ref[...] if d == 1 else ref[pl.ds(r, rows, stride=d), :]

        def put(unit_col, y):
            o_ref[:, unit_col * LANES:(unit_col + 1) * LANES] = y.astype(BF16)

        for r in range(d):
            c, s1, s2 = rows_of(c_ref, r), rows_of(s1_ref, r), rows_of(s2_ref, r)
            for b in range(n_in):
                t = rows_of(in_refs[b], r)
                if b < n_q:
                    put((b * d + r) if not gqa else b, _norm_rope(t, gq_ref[...], c, s1, s2, seg))
                elif not gqa:
                    sec, pair = (1, b - 2) if b < 4 else (2, b - 4)
                    y = _norm_rope(t, gk_ref[...], c, s1, s2, seg) if sec == 1 else t
                    put(sec * n_units + pair * d + r, y)
                else:
                    sec = 1 if b == 4 else 2
                    y = _norm_rope(t, gk_ref[...], c, s1, s2, seg) if sec == 1 else t
                    for u in range(n_units):
                        put(sec * n_units + u, _dup_half(y, u // 2))

    in_specs = [pl.BlockSpec((tm, LANES), lambda i, b=b: (i, cb0 + b)) for b in range(n_in)]
    vec = pl.BlockSpec((1, LANES), lambda i: (0, 0))
    tab = pl.BlockSpec((tm, LANES), lambda i: (i, 0))
    width = 3 * n_units * LANES
    return pl.pallas_call(
        body, name=name, grid=(s_len // tm,), in_specs=in_specs + [vec, vec, tab, tab, tab],
        out_specs=pl.BlockSpec((rows, width), lambda i: (i, 0)),
        out_shape=jax.ShapeDtypeStruct((s_len // d, width), BF16), compiler_params=_cparams(1),
    )(*([proj] * n_in), gq, gk, *tabs)


def _qk_prep_bwd(dqkv, proj, cb0, d, gqa, gq, gk, tabs, name):
    s_len = proj.shape[0]
    tm = 512
    rows = tm // d
    n_units = 4 if gqa else 2 * d
    n_q = 4 if gqa else 2
    n_in = 6

    def body(*refs):
        d_refs = refs[0:3]
        in_refs = refs[3:3 + n_in]
        gq_ref, gk_ref, c_ref, s1_ref, s2_ref, o_ref, dgq_ref, dgk_ref, stage = refs[3 + n_in:]
        seg = _seg_matrix(HEAD_DIM)

        @pl.when(pl.program_id(0) == 0)
        def _():
            dgq_ref[...] = jnp.zeros_like(dgq_ref)
            dgk_ref[...] = jnp.zeros_like(dgk_ref)

        def rows_of(ref, r):
            return ref[...] if d == 1 else ref[pl.ds(r, rows, stride=d), :]

        def unit(col):
            sec, u = divmod(col, n_units)
            return d_refs[sec][:, u * LANES:(u + 1) * LANES]

        def norm_bwd(dyr, t, gain, c, s1, s2, dg_ref):
            rs = lax.rsqrt(_seg_sum(t * t, seg) * (1.0 / HEAD_DIM) + EPS)
            that = t * rs
            dy = _unrope(dyr, c, s1, s2)
            dg_ref[...] += jnp.sum(dy * that, axis=0, keepdims=True)
            dthat = dy * gain
            return rs * (dthat - that * (_seg_sum(dthat * that, seg) * (1.0 / HEAD_DIM)))

        def fold(sec):
            tot = []
            for u in range(n_units):
                v = unit(sec * n_units + u)
                tot.append(v + pltpu.roll(v, HEAD_DIM, 1))
            lane = lax.broadcasted_iota(jnp.int32, tot[0].shape, 1)
            return jnp.where(lane < HEAD_DIM, tot[0] + tot[1], tot[2] + tot[3])

        for b in range(n_in):
            for r in range(d):
                c, s1, s2 = rows_of(c_ref, r), rows_of(s1_ref, r), rows_of(s2_ref, r)
                t = rows_of(in_refs[b], r)
                if b < n_q:
                    g = unit((b * d + r) if not gqa else b)
                    out = norm_bwd(g, t, gq_ref[...], c, s1, s2, dgq_ref)
                elif not gqa:
                    sec, pair = (1, b - 2) if b < 4 else (2, b - 4)
                    g = unit(sec * n_units + pair * d + r)
                    out = norm_bwd(g, t, gk_ref[...], c, s1, s2, dgk_ref) if sec == 1 else g
                else:
                    sec = 1 if b == 4 else 2
                    g = fold(sec)
                    out = norm_bwd(g, t, gk_ref[...], c, s1, s2, dgk_ref) if sec == 1 else g
                if d == 1:
                    o_ref[:, b * LANES:(b + 1) * LANES] = out.astype(BF16)
                else:
                    stage[pl.ds(r, rows, stride=d), :] = out
            if d != 1:
                o_ref[:, b * LANES:(b + 1) * LANES] = stage[...].astype(BF16)

    in_specs = [pl.BlockSpec((rows, n_units * LANES), lambda i: (i, 0))] * 3
    in_specs += [pl.BlockSpec((tm, LANES), lambda i, b=b: (i, cb0 + b)) for b in range(n_in)]
    vec = pl.BlockSpec((1, LANES), lambda i: (0, 0))
    tab = pl.BlockSpec((tm, LANES), lambda i: (i, 0))
    return pl.pallas_call(
        body, name=name, grid=(s_len // tm,), in_specs=in_specs + [vec, vec, tab, tab, tab],
        out_specs=(pl.BlockSpec((tm, n_in * LANES), lambda i: (i, 0)), vec, vec),
        out_shape=(jax.ShapeDtypeStruct((s_len, n_in * LANES), BF16), jax.ShapeDtypeStruct((1, LANES), F32),
                   jax.ShapeDtypeStruct((1, LANES), F32)),
        scratch_shapes=[pltpu.VMEM((tm, LANES), F32)], compiler_params=_cparams(1),
    )(*dqkv, *([proj] * n_in), gq, gk, *tabs)


def _head_masks(shape):
    lane = lax.broadcasted_iota(jnp.int32, shape, 1)
    return lane < HEAD_DIM, lane >= HEAD_DIM


def _band_fwd(qkv, n_units, max_dist, sinks, name):
    n_rows = qkv.shape[0]
    nb = n_rows // BLOCK
    scale = HEAD_DIM ** -0.5
    has_sink = sinks is not None
    assert not has_sink or max_dist < BLOCK

    qn, un = min(nb, BAND_Q_BLOCKS), BAND_UNITS
    ug = n_units // un

    def body(*refs):
        q_ref, kp_ref, km_ref, vp_ref, vm_ref = refs[:5]
        o_ref, lse_ref = refs[-2:]
        i = pl.program_id(1)
        qi = lax.broadcasted_iota(jnp.int32, (BLOCK, 2 * BLOCK), 0)
        kj = lax.broadcasted_iota(jnp.int32, (BLOCK, 2 * BLOCK), 1)
        dist = qi + BLOCK - kj
        band = (dist >= 0) & (dist <= max_dist)
        band_first = band & ((i > 0) | (kj >= BLOCK))
        m0, m1 = _head_masks((BLOCK, LANES))
        zero = jnp.zeros((BLOCK, LANES), BF16)
        for ub in range(un):
            cs = slice(ub * LANES, (ub + 1) * LANES)
            for qb in range(qn):
                rs = slice(qb * BLOCK, (qb + 1) * BLOCK)
                q = q_ref[rs, cs]
                if qb == 0:
                    kk = jnp.concatenate([kp_ref[:, cs], km_ref[0:BLOCK, cs]], axis=0)
                    vv = jnp.concatenate([vp_ref[:, cs], vm_ref[0:BLOCK, cs]], axis=0)
                    valid = band_first
                else:
                    kk = km_ref[(qb - 1) * BLOCK:(qb + 1) * BLOCK, cs]
                    vv = vm_ref[(qb - 1) * BLOCK:(qb + 1) * BLOCK, cs]
                    valid = band
                outs, lses = [], []
                for e, hm in enumerate((m0, m1)):
                    qe = jnp.where(hm, q, zero)
                    s = lax.dot_general(qe, kk, (((1,), (1,)), ((), ())), preferred_element_type=F32) * scale
                    s = jnp.where(valid, s, -jnp.inf)
                    if has_sink:
                        s = jnp.where(kj == 0, refs[5][ub][:, e * HEAD_DIM:e * HEAD_DIM + 1], s)
                    mx = jnp.max(s, axis=-1, keepdims=True)
                    p = jnp.exp(s - mx)
                    den = jnp.sum(p, axis=-1, keepdims=True)
                    pn = p * (1.0 / den)
                    if has_sink:
                        pn = jnp.where(kj == 0, 0.0, pn)
                    pn = pn.astype(BF16)
                    outs.append(jnp.dot(pn, vv, preferred_element_type=F32))
                    lses.append(mx + jnp.log(den))
                o_ref[rs, cs] = jnp.where(m0, outs[0], outs[1])
                lse_ref[rs, cs] = jnp.where(m0, jnp.broadcast_to(lses[0], (BLOCK, LANES)),
                                            jnp.broadcast_to(lses[1], (BLOCK, LANES)))

    def main(sec):
        return pl.BlockSpec((qn * BLOCK, un * LANES), lambda u, i: (i, sec * ug + u))

    def prev(sec):
        return pl.BlockSpec((BLOCK, un * LANES), lambda u, i: (jnp.maximum(i * qn - 1, 0), sec * ug + u))

    in_specs = [main(0), prev(1), main(1), prev(2), main(2)]
    args = [qkv] * 5
    if has_sink:
        in_specs.append(pl.BlockSpec((un, 1, LANES), lambda u, i: (u, 0, 0)))
        args.append(sinks)
    return pl.pallas_call(
        body, name=name, grid=(ug, nb // qn), in_specs=in_specs, out_specs=(main(0), main(0)),
        out_shape=(jax.ShapeDtypeStruct((n_rows, n_units * LANES), F32),) * 2, compiler_params=_cparams(2),
    )(*args)


def _band_bwd(qkv, do, lse, delta, n_units, max_dist, name):
    n_rows = qkv.shape[0]
    nb = n_rows // BLOCK
    scale = HEAD_DIM ** -0.5

    qn, un = min(nb, BAND_Q_BLOCKS), BAND_UNITS
    ug = n_units // un
    steps = nb // qn
    nt_dims = (((1,), (1,)), ((), ()))
    tn_dims = (((0,), (0,)), ((), ()))

    def body(qm_ref, qx_ref, kp_ref, km_ref, vp_ref, vm_ref, dom_ref, dox_ref, lm_ref, lx_ref, dm_ref, dx_ref,
             dq_ref, dk_ref, dv_ref):
        i = pl.program_id(1)
        m0, m1 = _head_masks((BLOCK, LANES))
        zero = jnp.zeros((BLOCK, LANES), BF16)
        qi = lax.broadcasted_iota(jnp.int32, (BLOCK, 2 * BLOCK), 0)
        kj = lax.broadcasted_iota(jnp.int32, (BLOCK, 2 * BLOCK), 1)
        dist = qi + BLOCK - kj
        band = (dist >= 0) & (dist <= max_dist)
        band_first = band & ((i > 0) | (kj >= BLOCK))
        qr = lax.broadcasted_iota(jnp.int32, (BLOCK, BLOCK), 0)
        kc = lax.broadcasted_iota(jnp.int32, (BLOCK, BLOCK), 1)
        dist_x = qr + BLOCK - kc
        band_next = (dist_x >= 0) & (dist_x <= max_dist) & (i < steps - 1)

        def pair(q, dob, lse_b, del_b, kk, vv, valid):
            dqs, dk, dv = [], None, None
            for e, hm in enumerate((m0, m1)):
                col = slice(e * HEAD_DIM, e * HEAD_DIM + 1)
                qe = jnp.where(hm, q, zero)
                doe = jnp.where(hm, dob, zero)
                s = lax.dot_general(qe, kk, nt_dims, preferred_element_type=F32) * scale
                p = jnp.where(valid, jnp.exp(s - lse_b[:, col]), 0.0)
                dp = lax.dot_general(doe, vv, nt_dims, preferred_element_type=F32)
                ds = (p * (dp - del_b[:, col]) * scale).astype(BF16)
                dqs.append(jnp.dot(ds, kk, preferred_element_type=F32))
                dk_e = lax.dot_general(ds, qe, tn_dims, preferred_element_type=F32)
                dv_e = lax.dot_general(p.astype(BF16), doe, tn_dims, preferred_element_type=F32)
                dk = dk_e if dk is None else dk + dk_e
                dv = dv_e if dv is None else dv + dv_e
            return jnp.where(m0, dqs[0], dqs[1]), dk, dv

        for ub in range(un):
            cs = slice(ub * LANES, (ub + 1) * LANES)
            dk_acc, dv_acc = [None] * qn, [None] * qn

            def add(acc, kb, part):
                acc[kb] = part if acc[kb] is None else acc[kb] + part

            for qb in range(qn):
                rs = slice(qb * BLOCK, (qb + 1) * BLOCK)
                if qb == 0:
                    kk = jnp.concatenate([kp_ref[:, cs], km_ref[0:BLOCK, cs]], axis=0)
                    vv = jnp.concatenate([vp_ref[:, cs], vm_ref[0:BLOCK, cs]], axis=0)
                    valid = band_first
                else:
                    kk = km_ref[(qb - 1) * BLOCK:(qb + 1) * BLOCK, cs]
                    vv = vm_ref[(qb - 1) * BLOCK:(qb + 1) * BLOCK, cs]
                    valid = band
                dq, dk, dv = pair(qm_ref[rs, cs], dom_ref[rs, cs], lm_ref[rs, cs], dm_ref[rs, cs], kk, vv, valid)
                dq_ref[rs, cs] = dq
                if qb > 0:
                    add(dk_acc, qb - 1, dk[0:BLOCK])
                    add(dv_acc, qb - 1, dv[0:BLOCK])
                add(dk_acc, qb, dk[BLOCK:2 * BLOCK])
                add(dv_acc, qb, dv[BLOCK:2 * BLOCK])
            last = slice((qn - 1) * BLOCK, qn * BLOCK)
            _, dk, dv = pair(qx_ref[:, cs], dox_ref[:, cs], lx_ref[:, cs], dx_ref[:, cs], km_ref[last, cs], vm_ref[last, cs],
                             band_next)
            add(dk_acc, qn - 1, dk)
            add(dv_acc, qn - 1, dv)
            for kb in range(qn):
                dk_ref[kb * BLOCK:(kb + 1) * BLOCK, cs] = dk_acc[kb]
                dv_ref[kb * BLOCK:(kb + 1) * BLOCK, cs] = dv_acc[kb]

    def main(sec):
        return pl.BlockSpec((qn * BLOCK, un * LANES), lambda u, i: (i, sec * ug + u))

    def prev(sec):
        return pl.BlockSpec((BLOCK, un * LANES), lambda u, i: (jnp.maximum(i * qn - 1, 0), sec * ug + u))

    def nxt(sec):
        return pl.BlockSpec((BLOCK, un * LANES), lambda u, i: (jnp.minimum((i + 1) * qn, nb - 1), sec * ug + u))

    in_specs = [main(0), nxt(0), prev(1), main(1), prev(2), main(2),
                main(0), nxt(0), main(0), nxt(0), main(0), nxt(0)]
    args = [qkv] * 6 + [do, do, lse, lse, delta, delta]
    shp = jax.ShapeDtypeStruct((n_rows, n_units * LANES), F32)
    return pl.pallas_call(
        body, name=name, grid=(ug, steps), in_specs=in_specs, out_specs=(main(0), main(0), main(0)),
        out_shape=(shp, shp, shp), compiler_params=_cparams(2),
    )(*args)


def _merge_groups(os_, lses, dils, name):
    s_len = os_[0].shape[0] * dils[0]
    tm = 512

    def body(*refs):
        o_refs, l_refs = refs[0:3], refs[3:6]
        o_ref, lse_ref = refs[6:8]
        so, sl = refs[8:11], refs[11:14]
        for pair in range(2):
            for g, d in enumerate(dils):
                rows = tm // d
                for r in range(d):
                    col = slice((pair * d + r) * LANES, (pair * d + r + 1) * LANES)
                    if d == 1:
                        so[g][...] = o_refs[g][:, col]
                        sl[g][...] = l_refs[g][:, col]
                    else:
                        so[g][pl.ds(r, rows, stride=d), :] = o_refs[g][:, col]
                        sl[g][pl.ds(r, rows, stride=d), :] = l_refs[g][:, col]
            l0, l1, l2 = sl[0][...], sl[1][...], sl[2][...]
            mx = jnp.maximum(jnp.maximum(l0, l1), l2)
            e0, e1, e2 = jnp.exp(l0 - mx), jnp.exp(l1 - mx), jnp.exp(l2 - mx)
            den = e0 + e1 + e2
            inv = 1.0 / den
            o_ref[:, pair * LANES:(pair + 1) * LANES] = (so[0][...] * (e0 * inv) + so[1][...] * (e1 * inv)
                                                         + so[2][...] * (e2 * inv))
            lse_ref[:, pair * LANES:(pair + 1) * LANES] = mx + jnp.log(den)

    in_specs = [pl.BlockSpec((tm // d, 2 * d * LANES), lambda i: (i, 0)) for d in dils] * 2
    out = pl.BlockSpec((tm, 2 * LANES), lambda i: (i, 0))
    shp = jax.ShapeDtypeStruct((s_len, 2 * LANES), F32)
    return pl.pallas_call(
        body, name=name, grid=(s_len // tm,), in_specs=in_specs, out_specs=(out, out), out_shape=(shp, shp),
        scratch_shapes=[pltpu.VMEM((tm, LANES), F32)] * 6, compiler_params=_cparams(1),
    )(*os_, *lses)


def _bwd_prep(do, o, lse, dils, sinks, name):
    s_len, width = do.shape
    n_pairs = width // LANES
    tm = 512
    has_sink = sinks is not None
    n_g = len(dils)

    def body(*refs):
        do_ref, o_ref, lse_ref = refs[:3]
        pos = 3
        if has_sink:
            sink_ref = refs[pos]
            pos += 1
        outs = refs[pos:pos + 3 * n_g]
        pos += 3 * n_g
        if has_sink:
            dsink_ref = refs[pos]
            pos += 1
        s_do, s_l, s_d = refs[pos:pos + 3]
        seg = _seg_matrix(HEAD_DIM)

        if has_sink:
            @pl.when(pl.program_id(0) == 0)
            def _():
                dsink_ref[...] = jnp.zeros_like(dsink_ref)

        for pair in range(n_pairs):
            col = slice(pair * LANES, (pair + 1) * LANES)
            dov = do_ref[:, col]
            lv = lse_ref[:, col]
            delta = _seg_sum(dov * o_ref[:, col], seg)
            if has_sink:
                dsink_ref[pair] += -jnp.sum(jnp.exp(sink_ref[pair] - lv) * delta, axis=0, keepdims=True)
            s_do[...] = dov
            s_l[...] = lv
            s_d[...] = delta
            for g, d in enumerate(dils):
                rows = tm // d
                for r in range(d):
                    oc = slice((pair * d + r) * LANES, (pair * d + r + 1) * LANES)
                    if d == 1:
                        a, b, c = s_do[...], s_l[...], s_d[...]
                    else:
                        a = s_do[pl.ds(r, rows, stride=d), :]
                        b = s_l[pl.ds(r, rows, stride=d), :]
                        c = s_d[pl.ds(r, rows, stride=d), :]
                    outs[3 * g][:, oc] = a.astype(BF16)
                    outs[3 * g + 1][:, oc] = b
                    outs[3 * g + 2][:, oc] = c

    row = pl.BlockSpec((tm, width), lambda i: (i, 0))
    in_specs = [row, row, row]
    args = [do, o, lse]
    if has_sink:
        in_specs.append(pl.BlockSpec((n_pairs, 1, LANES), lambda i: (0, 0, 0)))
        args.append(sinks)
    out_specs, out_shape = [], []
    for d in dils:
        for dt in (BF16, F32, F32):
            out_specs.append(pl.BlockSpec((tm // d, n_pairs * d * LANES), lambda i: (i, 0)))
            out_shape.append(jax.ShapeDtypeStruct((s_len // d, n_pairs * d * LANES), dt))
    if has_sink:
        out_specs.append(pl.BlockSpec((n_pairs, 1, LANES), lambda i: (0, 0, 0)))
        out_shape.append(jax.ShapeDtypeStruct((n_pairs, 1, LANES), F32))
    return pl.pallas_call(
        body, name=name, grid=(s_len // tm,), in_specs=in_specs, out_specs=tuple(out_specs),
        out_shape=tuple(out_shape), scratch_shapes=[pltpu.VMEM((tm, LANES), F32)] * 3, compiler_params=_cparams(1),
    )(*args)


def _mem_kv(mem, mem_gain, w_kv, k_gain, name):
    m_len = mem.shape[0]
    kw = M_HEADS * M_HEAD_DIM

    def body(mem_ref, mg_ref, w_ref, kg_ref, k_ref, v_ref):
        mv = mem_ref[...]
        r = lax.rsqrt(jnp.mean(mv * mv, axis=-1, keepdims=True) + EPS)
        mn = ((mv * r) * mg_ref[...]).astype(BF16)
        kv = jnp.dot(mn, w_ref[...], preferred_element_type=F32)
        for h in range(M_HEADS):
            col = slice(h * M_HEAD_DIM, (h + 1) * M_HEAD_DIM)
            t = kv[:, col]
            rk = lax.rsqrt(jnp.mean(t * t, axis=-1, keepdims=True) + EPS)
            k_ref[:, col] = ((t * rk) * kg_ref[...]).astype(BF16)
        v_ref[...] = kv[:, kw:].astype(BF16)

    shp = jax.ShapeDtypeStruct((m_len, kw), BF16)
    return pl.pallas_call(body, name=name, out_shape=(shp, shp),
                          compiler_params=pltpu.CompilerParams(vmem_limit_bytes=VMEM_LIMIT_BYTES))(mem, mem_gain, w_kv, k_gain)


def _mem_kv_bwd(mem, mem_gain, w_kv, k_gain, dk, dv, name):
    m_len, d = mem.shape
    kw = M_HEADS * M_HEAD_DIM

    def body(mem_ref, mg_ref, w_ref, kg_ref, dk_ref, dv_ref, dw_ref, dmg_ref, dkg_ref, dkv_ref):
        mv = mem_ref[...]
        r = lax.rsqrt(jnp.mean(mv * mv, axis=-1, keepdims=True) + EPS)
        mhat = mv * r
        mn = (mhat * mg_ref[...]).astype(BF16)
        kv = jnp.dot(mn, w_ref[...], preferred_element_type=F32)
        dkg = jnp.zeros((1, M_HEAD_DIM), F32)
        for h in range(M_HEADS):
            col = slice(h * M_HEAD_DIM, (h + 1) * M_HEAD_DIM)
            t = kv[:, col]
            rk = lax.rsqrt(jnp.mean(t * t, axis=-1, keepdims=True) + EPS)
            that = t * rk
            dy = dk_ref[:, col]
            dkg = dkg + jnp.sum(dy * that, axis=0, keepdims=True)
            dthat = dy * kg_ref[...]
            dkv_ref[:, col] = (rk * (dthat - that * jnp.mean(dthat * that, axis=-1, keepdims=True))).astype(BF16)
        dkv_ref[:, kw:] = dv_ref[...].astype(BF16)
        dkg_ref[...] = dkg
        dkv = dkv_ref[...]
        dw_ref[...] = lax.dot_general(mn, dkv, (((0,), (0,)), ((), ())), preferred_element_type=F32).astype(GRAD_DTYPE)
        dmn = lax.dot_general(dkv, w_ref[...], (((1,), (1,)), ((), ())), preferred_element_type=F32)
        dmg_ref[...] = jnp.sum(dmn * mhat, axis=0, keepdims=True)

    return pl.pallas_call(
        body, name=name,
        out_shape=(jax.ShapeDtypeStruct((d, 2 * kw), GRAD_DTYPE), jax.ShapeDtypeStruct((1, d), F32),
                   jax.ShapeDtypeStruct((1, M_HEAD_DIM), F32)),
        scratch_shapes=[pltpu.VMEM((m_len, 2 * kw), BF16)],
        compiler_params=pltpu.CompilerParams(vmem_limit_bytes=VMEM_LIMIT_BYTES),
    )(mem, mem_gain, w_kv, k_gain, dk, dv)


def _mem_attn_fwd(proj, cidx, mk, mv, q_gain, name):
    s_len = proj.shape[0]
    kw = M_HEADS * M_HEAD_DIM
    tm = 512
    scale = M_HEAD_DIM ** -0.5

    def body(q_ref, k_ref, v_ref, g_ref, o_ref):
        for h in range(M_HEADS):
            col = slice(h * M_HEAD_DIM, (h + 1) * M_HEAD_DIM)
            t = q_ref[:, col]
            rs = lax.rsqrt(jnp.mean(t * t, axis=-1, keepdims=True) + EPS)
            qn = ((t * rs) * g_ref[...]).astype(BF16)
            s = lax.dot_general(qn, k_ref[:, col], (((1,), (1,)), ((), ())), preferred_element_type=F32) * scale
            mx = jnp.max(s, axis=-1, keepdims=True)
            p = jnp.exp(s - mx)
            pn = (p * (1.0 / jnp.sum(p, axis=-1, keepdims=True))).astype(BF16)
            o_ref[:, col] = jnp.dot(pn, v_ref[:, col], preferred_element_type=F32).astype(BF16)

    whole = pl.BlockSpec((MEM_LEN, kw), lambda i: (0, 0))
    return pl.pallas_call(
        body, name=name, grid=(s_len // tm,),
        in_specs=[pl.BlockSpec((tm, kw), lambda i: (i, cidx)), whole, whole, pl.BlockSpec((1, M_HEAD_DIM), lambda i: (0, 0))],
        out_specs=pl.BlockSpec((tm, kw), lambda i: (i, 0)),
        out_shape=jax.ShapeDtypeStruct((s_len, kw), BF16), compiler_params=_cparams(1),
    )(proj, mk, mv, q_gain)


def _mem_attn_bwd(proj, cidx, mk, mv, q_gain, do, name):
    s_len = proj.shape[0]
    kw = M_HEADS * M_HEAD_DIM
    tm = 512
    scale = M_HEAD_DIM ** -0.5

    def body(q_ref, k_ref, v_ref, g_ref, do_ref, dq_ref, dk_ref, dv_ref, dg_ref):
        @pl.when(pl.program_id(0) == 0)
        def _():
            dk_ref[...] = jnp.zeros_like(dk_ref)
            dv_ref[...] = jnp.zeros_like(dv_ref)
            dg_ref[...] = jnp.zeros_like(dg_ref)

        for h in range(M_HEADS):
            col = slice(h * M_HEAD_DIM, (h + 1) * M_HEAD_DIM)
            t = q_ref[:, col]
            rs = lax.rsqrt(jnp.mean(t * t, axis=-1, keepdims=True) + EPS)
            that = t * rs
            qn = (that * g_ref[...]).astype(BF16)
            kh, vh = k_ref[:, col], v_ref[:, col]
            dob = do_ref[:, col].astype(BF16)
            s = lax.dot_general(qn, kh, (((1,), (1,)), ((), ())), preferred_element_type=F32) * scale
            mx = jnp.max(s, axis=-1, keepdims=True)
            p = jnp.exp(s - mx)
            p = p * (1.0 / jnp.sum(p, axis=-1, keepdims=True))
            dp = lax.dot_general(dob, vh, (((1,), (1,)), ((), ())), preferred_element_type=F32)
            ds = (p * (dp - jnp.sum(p * dp, axis=-1, keepdims=True)) * scale).astype(BF16)
            dqn = jnp.dot(ds, kh, preferred_element_type=F32)
            dk_ref[:, col] += lax.dot_general(ds, qn, (((0,), (0,)), ((), ())), preferred_element_type=F32)
            dv_ref[:, col] += lax.dot_general(p.astype(BF16), dob, (((0,), (0,)), ((), ())), preferred_element_type=F32)
            dg_ref[...] += jnp.sum(dqn * that, axis=0, keepdims=True)
            dthat = dqn * g_ref[...]
            dq_ref[:, col] = (rs * (dthat - that * jnp.mean(dthat * that, axis=-1, keepdims=True))).astype(BF16)

    whole = pl.BlockSpec((MEM_LEN, kw), lambda i: (0, 0))
    vec = pl.BlockSpec((1, M_HEAD_DIM), lambda i: (0, 0))
    row = pl.BlockSpec((tm, kw), lambda i: (i, 0))
    return pl.pallas_call(
        body, name=name, grid=(s_len // tm,),
        in_specs=[pl.BlockSpec((tm, kw), lambda i: (i, cidx)), whole, whole, vec, row],
        out_specs=(row, whole, whole, vec),
        out_shape=(jax.ShapeDtypeStruct((s_len, kw), BF16), jax.ShapeDtypeStruct((MEM_LEN, kw), F32),
                   jax.ShapeDtypeStruct((MEM_LEN, kw), F32), jax.ShapeDtypeStruct((1, M_HEAD_DIM), F32)),
        compiler_params=_cparams(1),
    )(proj, mk, mv, q_gain, do)


def _project_merge(outs, w_outs, gates, w_out, x, name):
    s_len = gates.shape[0]
    d = w_outs[0].shape[1]
    tm = 512

    def body(oa_ref, ob_ref, om_ref, wa_ref, wb_ref, wm_ref, g_ref, wo_ref, x_ref,
             pa_ref, pb_ref, pm_ref, merged_ref, x1_ref):
        merged = None
        for k, (o_ref, w_ref, p_ref) in enumerate(((oa_ref, wa_ref, pa_ref), (ob_ref, wb_ref, pb_ref), (om_ref, wm_ref, pm_ref))):
            p = jnp.dot(o_ref[...].astype(BF16), w_ref[...], preferred_element_type=F32).astype(BF16)
            p_ref[...] = p
            t = g_ref[:, k * d:(k + 1) * d].astype(F32) * p.astype(F32)
            merged = t if merged is None else merged + t
        merged = merged.astype(BF16)
        merged_ref[...] = merged
        x1_ref[...] = x_ref[...] + jnp.dot(merged, wo_ref[...], preferred_element_type=F32)

    row = pl.BlockSpec((tm, d), lambda i: (i, 0))
    shp = jax.ShapeDtypeStruct((s_len, d), BF16)
    in_specs = [pl.BlockSpec((tm, o.shape[1]), lambda i: (i, 0)) for o in outs]
    in_specs += [_resident(w.shape, lambda i: (0, 0)) for w in w_outs]
    in_specs += [pl.BlockSpec((tm, 3 * d), lambda i: (i, 0)), _resident(w_out.shape, lambda i: (0, 0)), row]
    return pl.pallas_call(
        body, name=name, grid=(s_len // tm,), in_specs=in_specs, out_specs=(row, row, row, row, row),
        out_shape=(shp, shp, shp, shp, jax.ShapeDtypeStruct((s_len, d), F32)), compiler_params=_cparams(1),
    )(*outs, *w_outs, gates, w_out, x)


def _project_merge_bwd(dx1, w_out, gates, pa, pb, pm, name):
    s_len, d = pa.shape
    tm = 512

    def body(dx_ref, w_ref, g_ref, a_ref, b_ref, m_ref, da_ref, db_ref, dmm_ref, dg_ref, dbg_ref):
        @pl.when(pl.program_id(0) == 0)
        def _():
            dbg_ref[...] = jnp.zeros_like(dbg_ref)
        dm = lax.dot_general(dx_ref[...], w_ref[...], (((1,), (1,)), ((), ())), preferred_element_type=F32)
        for k, (p_ref, dp_ref) in enumerate(((a_ref, da_ref), (b_ref, db_ref), (m_ref, dmm_ref))):
            col = slice(k * d, (k + 1) * d)
            g = g_ref[:, col].astype(F32)
            dp_ref[...] = (dm * g).astype(BF16)
            dpre = (dm * p_ref[...].astype(F32)) * (g * (1.0 - g))
            dbg_ref[:, col] += jnp.sum(dpre, axis=0, keepdims=True)
            dg_ref[:, col] = dpre.astype(BF16)

    row = pl.BlockSpec((tm, d), lambda i: (i, 0))
    wide = pl.BlockSpec((tm, 3 * d), lambda i: (i, 0))
    shp = jax.ShapeDtypeStruct((s_len, d), BF16)
    return pl.pallas_call(
        body, name=name, grid=(s_len // tm,), in_specs=[row, _resident(w_out.shape, lambda i: (0, 0)), wide, row, row, row],
        out_specs=(row, row, row, wide, pl.BlockSpec((1, 3 * d), lambda i: (0, 0))),
        out_shape=(shp, shp, shp, jax.ShapeDtypeStruct((s_len, 3 * d), BF16), jax.ShapeDtypeStruct((1, 3 * d), F32)),
        compiler_params=_cparams(1),
    )(dx1, w_out, gates, pa, pb, pm)


CONV_CHUNK = 256


def _pick_row(tile, j):
    row = lax.broadcasted_iota(jnp.int32, tile.shape, 0)
    return jnp.sum(jnp.where(row == j, tile, jnp.zeros_like(tile)), axis=0, keepdims=True)


def _rows_before(ref, start, k):
    cur = ref[pl.ds(start, CONV_CHUNK), :].astype(F32)
    prev = ref[pl.ds(pl.multiple_of(jnp.maximum(start - 16, 0), 16), 16), :].astype(F32)
    prev = jnp.where(start > 0, prev, jnp.zeros_like(prev))
    rolled = pltpu.roll(cur, k, 0)
    row = lax.broadcasted_iota(jnp.int32, cur.shape, 0)
    for j in range(k):
        rolled = jnp.where(row == j, _pick_row(prev, 16 - k + j), rolled)
    return rolled


def _rows_after(ref, start, k):
    cur = ref[pl.ds(start, CONV_CHUNK), :]
    nxt = ref[pl.ds(pl.multiple_of(start + CONV_CHUNK, 8), 8), :]
    rolled = pltpu.roll(cur, CONV_CHUNK - k, 0)
    row = lax.broadcasted_iota(jnp.int32, cur.shape, 0)
    for j in range(k):
        rolled = jnp.where(row == CONV_CHUNK - k + j, _pick_row(nxt, j), rolled)
    return rolled


def _conv_pre(u_ref, w_ref, b_ref, start):
    u2 = _rows_before(u_ref, start, 2)
    u1 = _rows_before(u_ref, start, 1)
    u0 = u_ref[pl.ds(start, CONV_CHUNK), :].astype(F32)
    c = ((b_ref[...] + w_ref[0:1, :] * u2) + w_ref[1:2, :] * u1) + w_ref[2:3, :] * u0
    return c, (u2, u1, u0)


def _norm_up_conv_glu(x, gain, w_up, conv_w, conv_b, name):
    s_len, d = x.shape
    tm, tn = 512, 2 * LANES
    nblk = D_FF // tn

    def body(x_ref, g_ref, w_ref, cw_ref, cb_ref, ht_ref, r_ref, u_ref, act_ref, halo):
        @pl.when(pl.program_id(0) == 0)
        def _():
            halo[...] = jnp.zeros_like(halo)
        xv = x_ref[...]
        r = lax.rsqrt(jnp.mean(xv * xv, axis=-1, keepdims=True) + EPS)
        hv = ((xv * r) * g_ref[...]).astype(BF16)
        ht_ref[...] = hv.T
        r_ref[...] = r
        row = lax.broadcasted_iota(jnp.int32, (tm, tn), 0)
        for j in range(nblk):
            conv = []
            for half in range(2):
                cb = half * nblk + j
                cols = slice(cb * tn, (cb + 1) * tn)
                ub = jnp.dot(hv, w_ref[:, cols], preferred_element_type=F32).astype(BF16)
                u_ref[:, cols] = ub
                u0 = ub.astype(F32)
                prev = halo[cb]
                u1 = jnp.where(row == 0, _pick_row(prev, 7), pltpu.roll(u0, 1, 0))
                u2 = pltpu.roll(u0, 2, 0)
                u2 = jnp.where(row == 0, _pick_row(prev, 6), jnp.where(row == 1, _pick_row(prev, 7), u2))
                halo[cb] = u0[tm - 8:tm, :]
                conv.append(((cb_ref[:, cols] + cw_ref[0:1, cols] * u2) + cw_ref[1:2, cols] * u1)
                            + cw_ref[2:3, cols] * u0)
            act_ref[:, j * tn:(j + 1) * tn] = ((conv[0] * _sigmoid(conv[0])) * conv[1]).astype(BF16)

    return pl.pallas_call(
        body, name=name, grid=(s_len // tm,),
        in_specs=[pl.BlockSpec((tm, d), lambda i: (i, 0)), _resident((1, d), lambda i: (0, 0)),
                  _resident((d, 2 * D_FF), lambda i: (0, 0)),
                  _resident((3, 2 * D_FF), lambda i: (0, 0)), _resident((1, 2 * D_FF), lambda i: (0, 0))],
        out_specs=(pl.BlockSpec((d, tm), lambda i: (0, i)), pl.BlockSpec((tm, 1), lambda i: (i, 0)),
                   pl.BlockSpec((tm, 2 * D_FF), lambda i: (i, 0)), pl.BlockSpec((tm, D_FF), lambda i: (i, 0))),
        out_shape=(jax.ShapeDtypeStruct((d, s_len), BF16), jax.ShapeDtypeStruct((s_len, 1), F32),
                   jax.ShapeDtypeStruct((s_len, 2 * D_FF), BF16), jax.ShapeDtypeStruct((s_len, D_FF), BF16)),
        scratch_shapes=[pltpu.VMEM((2 * nblk, 8, tn), F32)], compiler_params=_cparams(1),
    )(x, gain, w_up, conv_w, conv_b)


def _conv_glu_bwd(dact, u, conv_w, conv_b, name):
    s_len = u.shape[0]
    nblk = D_FF // LANES
    n_chunks = s_len // CONV_CHUNK

    def body(da_ref, ua_ref, ug_ref, wa_ref, wg_ref, ba_ref, bg_ref,
             dua_ref, dug_ref, dwa_ref, dwg_ref, dba_ref, dbg_ref, sa, sg):
        sa[pl.ds(s_len, 8), :] = jnp.zeros((8, LANES), F32)
        sg[pl.ds(s_len, 8), :] = jnp.zeros((8, LANES), F32)
        zero = jnp.zeros((1, LANES), F32)

        def chunk1(ci, carry):
            start = pl.multiple_of(ci * CONV_CHUNK, CONV_CHUNK)
            ca, ua = _conv_pre(ua_ref, wa_ref, ba_ref, start)
            cg, ug = _conv_pre(ug_ref, wg_ref, bg_ref, start)
            dact_v = da_ref[pl.ds(start, CONV_CHUNK), :].astype(F32)
            sig = _sigmoid(ca)
            dcg = dact_v * (ca * sig)
            dca = (dact_v * cg) * (sig * (1.0 + ca * (1.0 - sig)))
            sa[pl.ds(start, CONV_CHUNK), :] = dca
            sg[pl.ds(start, CONV_CHUNK), :] = dcg
            out = [carry[0] + jnp.sum(dca, axis=0, keepdims=True), carry[1] + jnp.sum(dcg, axis=0, keepdims=True)]
            for j in range(3):
                out.append(carry[2 + j] + jnp.sum(dca * ua[j], axis=0, keepdims=True))
            for j in range(3):
                out.append(carry[5 + j] + jnp.sum(dcg * ug[j], axis=0, keepdims=True))
            return tuple(out)

        acc = lax.fori_loop(0, n_chunks, chunk1, (zero,) * 8)
        dba_ref[...] = acc[0]
        dbg_ref[...] = acc[1]
        for j in range(3):
            dwa_ref[j:j + 1, :] = acc[2 + j]
            dwg_ref[j:j + 1, :] = acc[5 + j]

        def chunk2(ci, carry):
            start = pl.multiple_of(ci * CONV_CHUNK, CONV_CHUNK)
            for s_ref, w_ref, o_ref in ((sa, wa_ref, dua_ref), (sg, wg_ref, dug_ref)):
                d0 = s_ref[pl.ds(start, CONV_CHUNK), :]
                d1 = _rows_after(s_ref, start, 1)
                d2 = _rows_after(s_ref, start, 2)
                o_ref[pl.ds(start, CONV_CHUNK), :] = (w_ref[2:3, :] * d0 + w_ref[1:2, :] * d1
                                                      + w_ref[0:1, :] * d2).astype(BF16)
            return carry
        lax.fori_loop(0, n_chunks, chunk2, 0)

    def col(rows, off):
        return pl.BlockSpec((rows, LANES), lambda j: (0, off + j))

    big = jax.ShapeDtypeStruct((s_len, D_FF), BF16)
    return pl.pallas_call(
        body, name=name, grid=(nblk,),
        in_specs=[col(s_len, 0), col(s_len, 0), col(s_len, nblk), col(3, 0), col(3, nblk), col(1, 0), col(1, nblk)],
        out_specs=(col(s_len, 0), col(s_len, 0), col(3, 0), col(3, 0), col(1, 0), col(1, 0)),
        out_shape=(big, big, jax.ShapeDtypeStruct((3, D_FF), F32), jax.ShapeDtypeStruct((3, D_FF), F32),
                   jax.ShapeDtypeStruct((1, D_FF), F32), jax.ShapeDtypeStruct((1, D_FF), F32)),
        scratch_shapes=[pltpu.VMEM((s_len + 8, LANES), F32)] * 2, compiler_params=_cparams(1),
    )(dact, u, u, conv_w, conv_w, conv_b, conv_b)


def _rope_tables(positions):
    half = ROPE_DIMS // 2
    freqs = jnp.exp(jnp.arange(half, dtype=F32) * (-2.0 * math.log(ROPE_THETA) / ROPE_DIMS))
    ang = positions.reshape(-1).astype(F32)[:, None] * freqs
    cos, sin = jnp.cos(ang), jnp.sin(ang)
    n = ang.shape[0]
    zeros = lambda w: jnp.zeros((n, w), F32)
    c = jnp.concatenate([cos, cos, jnp.ones((n, HEAD_DIM - ROPE_DIMS), F32)], axis=1)
    s1 = jnp.concatenate([-sin, zeros(HEAD_DIM - half)], axis=1)
    s2 = jnp.concatenate([zeros(half), sin, zeros(HEAD_DIM - ROPE_DIMS)], axis=1)
    return tuple(jnp.tile(t, (1, 2)) for t in (c, s1, s2))


def _two(v):
    return jnp.tile(v.reshape(1, HEAD_DIM), (1, 2))


def _fold_heads(g):
    return g[0, :HEAD_DIM] + g[0, HEAD_DIM:]


MIX_WEIGHTS = ('w_gate', 'w_mem_kv', 'w_o_a', 'w_o_b', 'w_o_m', 'w_out')
FFN_WEIGHTS = ('w_up', 'conv_w', 'w_down')


def _device_step(x, mem, positions, target, w, hooks=None):
    tabs = _rope_tables(positions)
    dils = tuple(d for _, d in A_GROUPS)
    grads = {}
    w = dict(w)

    h, h_t, r1 = _rms_fwd(x, w['attn_norm'], "rms1")
    if hooks is not None:
        w.update(hooks.weights('in', [h, *tabs]))
    proj = _mm_rows([(h, w['w_in'], 0)], "mm_in")

    qkv_a, o_g, lse_g = [], [], []
    for gi, (window, d) in enumerate(A_GROUPS):
        gq, gk = _two(w['a_q_norm'][gi]), _two(w['a_k_norm'][gi])
        qkv = _qk_prep(proj, 6 * gi, d, False, gq, gk, tabs, f"qk_prep_a{gi}")
        o, lse = _band_fwd(qkv, 2 * d, window // d, None, f"band_fwd_a{gi}")
        qkv_a.append(qkv)
        o_g.append(o)
        lse_g.append(lse)
    o_a, lse_a = _merge_groups(o_g, lse_g, dils, "merge_a")
    if hooks is not None:
        w.update(hooks.weights('mix', [o_a]))

    gbq, gbk = _two(w['b_q_norm']), _two(w['b_k_norm'])
    sinks = jnp.repeat(w['b_sinks'].reshape(4, 2), HEAD_DIM, axis=1).reshape(4, 1, LANES)
    qkv_b = _qk_prep(proj, 18, 1, True, gbq, gbk, tabs, "qk_prep_b")
    o_b, lse_b = _band_fwd(qkv_b, 4, B_WINDOW - 1, sinks, "band_fwd_b")

    gates = _mm_rows([(h, w['w_gate'], 0)], "mm_gate", bias=w['b_gate'], sigmoid=True, out_dtypes=(BF16,))
    mk, mv = _mem_kv(mem, w['mem_norm'], w['w_mem_kv'], w['m_k_norm'], "mem_kv")
    o_m = _mem_attn_fwd(proj, 6, mk, mv, w['m_q_norm'], "mem_attn")

    pa, pb, pm, merged, x1 = _project_merge((o_a, o_b, o_m), (w['w_o_a'], w['w_o_b'], w['w_o_m']), gates, w['w_out'], x,
                                            "project_merge")

    if hooks is not None:
        w.update(hooks.weights('ffn', [x1]))
    h2_t, r2, u, act = _norm_up_conv_glu(x1, w['ffn_norm'], w['w_up'], w['conv_w'], w['conv_b'], "norm_up_conv_glu")
    dy, dy_b, dact, loss = _mm_rows([(act, w['w_down'], 0)], "mm_down", res=x1, loss_target=target)

    grads['w_down'] = _mm_tn(act, dy_b, "mm_dw_down")
    du_a, du_g, dcw_a, dcw_g, dcb_a, dcb_g = _conv_glu_bwd(dact, u, w['conv_w'], w['conv_b'], "conv_glu_bwd")
    grads['conv_w'] = jnp.concatenate([dcw_a, dcw_g], axis=1)
    grads['conv_b'] = jnp.concatenate([dcb_a, dcb_g], axis=1)
    grads['w_up'] = _mm_cols(h2_t, [du_a, du_g], "mm_dw_up")
    ffn_gain = w['ffn_norm']
    if hooks is not None:
        ffn_gain = ffn_gain + hooks.grads('ffn', grads)[0:1, 0:1]
    dx1, dx1_b, grads['ffn_norm'] = _mm_rows([(du_a, w['w_up'], 0), (du_g, w['w_up'], 1)], "mm_d_h2", nt=True,
                                             rms_bwd=(x1, r2, ffn_gain, dy), out_dtypes=(F32, BF16))

    grads['w_out'] = _mm_tn(merged, dx1_b, "mm_dw_out")
    dpa, dpb, dpm, dgpre, grads['b_gate'] = _project_merge_bwd(dx1_b, w['w_out'], gates, pa, pb, pm,
                                                               "project_merge_bwd")
    do_a, do_b, do_m = _mm_rows_each([(dpa, w['w_o_a']), (dpb, w['w_o_b']), (dpm, w['w_o_m'])], "mm_d_o")
    grads['w_o_a'], grads['w_o_b'], grads['w_o_m'] = _mm_tn_each([(o_a, dpa), (o_b, dpb), (o_m, dpm)], "mm_dw_o")
    grads['w_gate'] = _mm_cols(h_t, [dgpre], "mm_dw_gate")
    dq_m, dmk, dmv, grads['m_q_norm'] = _mem_attn_bwd(proj, 6, mk, mv, w['m_q_norm'], do_m, "mem_attn_bwd")
    grads['w_mem_kv'], grads['mem_norm'], grads['m_k_norm'] = _mem_kv_bwd(
        mem, w['mem_norm'], w['w_mem_kv'], w['m_k_norm'], dmk, dmv, "mem_kv_bwd")
    a_gain = w['a_q_norm']
    if hooks is not None:
        a_gain = a_gain + hooks.grads('mix', grads)[0:1, 0:1]

    prep = _bwd_prep(do_a, o_a, lse_a, dils, None, "bwd_prep_a")
    dproj, dgq_a, dgk_a = [], [], []
    for gi, (window, d) in enumerate(A_GROUPS):
        gq, gk = _two(a_gain[gi]), _two(w['a_k_norm'][gi])
        dqkv = _band_bwd(qkv_a[gi], prep[3 * gi], prep[3 * gi + 1], prep[3 * gi + 2], 2 * d, window // d,
                         f"band_bwd_a{gi}")
        dp, dgq, dgk = _qk_prep_bwd(dqkv, proj, 6 * gi, d, False, gq, gk, tabs, f"qk_prep_bwd_a{gi}")
        dproj.append(dp)
        dgq_a.append(_fold_heads(dgq))
        dgk_a.append(_fold_heads(dgk))
    grads['a_q_norm'] = jnp.stack(dgq_a)
    grads['a_k_norm'] = jnp.stack(dgk_a)

    do_bu, lse_bu, delta_bu, dsink = _bwd_prep(do_b, o_b, lse_b, (1,), sinks, "bwd_prep_b")
    dqkv = _band_bwd(qkv_b, do_bu, lse_bu, delta_bu, 4, B_WINDOW - 1, "band_bwd_b")
    dp_b, dgq, dgk = _qk_prep_bwd(dqkv, proj, 18, 1, True, gbq, gbk, tabs, "qk_prep_bwd_b")
    dproj.append(dp_b)
    grads['b_q_norm'] = _fold_heads(dgq)
    grads['b_k_norm'] = _fold_heads(dgk)
    grads['b_sinks'] = jnp.stack([dsink[:, 0, 0], dsink[:, 0, HEAD_DIM]], axis=1).reshape(8)

    dproj.append(dq_m)

    cols = (0, 1, 2, 3, 6)
    grads['w_in'] = _mm_rows_cat(h_t, dproj, "mm_dw_in")
    attn_gain = w['attn_norm']
    if hooks is not None:
        attn_gain = attn_gain + hooks.grads('in', grads)[0:1, 0:1]
    grad_x, grads['attn_norm'] = _mm_rows(
        [(dp, w['w_in'], c) for dp, c in zip(dproj, cols)] + [(dgpre, w['w_gate'], 0)], "mm_d_h", nt=True,
        rms_bwd=(x, r1, attn_gain, dx1))
    return loss, grad_x, grads


def _coords():
    return lax.axis_index("x"), lax.axis_index("y"), lax.axis_index("c")


def _slot(p):
    return 4 * p[0] + 2 * p[1] + p[2]


ALL_PEERS = tuple(range(1, N_DEV))
CHIP_PEERS = (1, 4, 2, 6)
OTHER_CHIPS = (4, 2, 6)


def _peers(me, masks=ALL_PEERS):
    x, y, c = me
    return [(1 - x if mask & 4 else x, 1 - y if mask & 2 else y, 1 - c if mask & 1 else c) for mask in masks]


HBM_SPEC = pl.BlockSpec(memory_space=pltpu.HBM)


SEM_SPEC = pl.BlockSpec(memory_space=pltpu.SEMAPHORE)
SIDE_EFFECT = pltpu.SideEffectType.DATAFLOW_SIDE_EFFECTING


def _exchange_start(blocks, name, gather=False, masks=ALL_PEERS, after=None):
    n = len(blocks)
    n_peers = len(masks)
    n_in = 2 * n + (0 if after is None else 1)

    def body(*refs):
        ins, lands = refs[:n], refs[n:2 * n]
        send_sems, recv_sems = refs[n_in], refs[n_in + 1]
        token = refs[-1]
        me = _coords()
        peers = _peers(me, masks)
        for a in range(n):
            for k in range(n_peers):
                pltpu.make_async_remote_copy(
                    src_ref=ins[a] if gather else ins[a].at[_slot(peers[k])], dst_ref=lands[a].at[_slot(me)],
                    send_sem=send_sems.at[a * n_peers + k], recv_sem=recv_sems.at[a * n_peers + k],
                    device_id=peers[k], device_id_type=MESH).start()
        token[...] = jnp.zeros_like(token)

    land_shapes = [((N_DEV,) + b.shape) if gather else b.shape for b in blocks]
    hbm_in = [pltpu.HBM(b.shape, b.dtype) for b in blocks]
    hbm_land = [pltpu.HBM(s, b.dtype) for s, b in zip(land_shapes, blocks)]
    sems = pltpu.SemaphoreType.DMA((n * n_peers,))
    ins = [pltpu.with_memory_space_constraint(b, pltpu.HBM) for b in blocks]
    lands = [pltpu.with_memory_space_constraint(lax.empty(s, b.dtype), pltpu.HBM) for s, b in zip(land_shapes, blocks)]
    return pl.pallas_call(
        body, name=name, out_shape=(sems, sems, *hbm_in, *hbm_land, jax.ShapeDtypeStruct((8, LANES), F32)),
        in_specs=[HBM_SPEC] * (2 * n) + ([] if after is None else [pl.BlockSpec(memory_space=pl.ANY)]),
        out_specs=(SEM_SPEC, SEM_SPEC, *([HBM_SPEC] * (2 * n)), pl.BlockSpec(memory_space=pltpu.VMEM)),
        input_output_aliases={i: 2 + i for i in range(2 * n)},
        compiler_params=pltpu.CompilerParams(has_side_effects=SIDE_EFFECT),
    )(*ins, *lands, *([] if after is None else [after]))


def _exchange_wait(started, after, name, gather=False, masks=ALL_PEERS):
    n = (len(started) - 3) // 2
    n_peers = len(masks)
    send_sems, recv_sems = started[0], started[1]
    thru = started[2:2 + 2 * n]

    def body(*refs):
        ins, lands = refs[:n], refs[n:2 * n]
        send_ref, recv_ref = refs[2 * n], refs[2 * n + 1]
        me = _coords()
        peers = _peers(me, masks)
        for a in range(n):
            for k in range(n_peers):
                cp = pltpu.make_async_remote_copy(
                    src_ref=ins[a] if gather else ins[a].at[_slot(peers[k])], dst_ref=lands[a].at[_slot(peers[k])],
                    send_sem=send_ref.at[a * n_peers + k], recv_sem=recv_ref.at[a * n_peers + k],
                    device_id=peers[k], device_id_type=MESH)
                cp.wait_send()
                cp.wait_recv()

    hbm = [pltpu.HBM(t.shape, t.dtype) for t in thru]
    res = pl.pallas_call(
        body, name=name, out_shape=tuple(hbm),
        in_specs=[HBM_SPEC] * (2 * n) + [SEM_SPEC, SEM_SPEC] + [pl.BlockSpec(memory_space=pl.ANY)] * len(after),
        out_specs=tuple([HBM_SPEC] * (2 * n)), input_output_aliases={i: i for i in range(2 * n)},
        compiler_params=pltpu.CompilerParams(has_side_effects=SIDE_EFFECT),
    )(*thru, send_sems, recv_sems, *after)
    return res[n:]


def _sibling_forward(arrays, name):
    n = len(arrays)
    n_fwd = len(OTHER_CHIPS)

    def body(*refs):
        bufs = refs[n:2 * n]
        token, send_sems, recv_sems = refs[2 * n:]
        token[...] = jnp.zeros_like(token)
        x, y, c = _coords()
        sibling = (x, y, 1 - c)
        mine = _peers((x, y, c), OTHER_CHIPS)
        theirs = _peers(sibling, OTHER_CHIPS)

        def copy(a, k, block):
            rows = bufs[a].at[_slot(block)]
            return pltpu.make_async_remote_copy(
                src_ref=rows, dst_ref=rows, send_sem=send_sems.at[a * n_fwd + k], recv_sem=recv_sems.at[a * n_fwd + k],
                device_id=sibling, device_id_type=MESH)

        sends = [copy(a, k, mine[k]) for a in range(n) for k in range(n_fwd)]
        for cp in sends:
            cp.start()
        for a in range(n):
            for k in range(n_fwd):
                copy(a, k, theirs[k]).wait_recv()
        for cp in sends:
            cp.wait_send()

    res = pl.pallas_call(
        body, name=name, in_specs=[HBM_SPEC] * n,
        out_specs=tuple([HBM_SPEC] * n + [pl.BlockSpec(memory_space=pltpu.VMEM)]),
        out_shape=tuple([jax.ShapeDtypeStruct(a.shape, a.dtype) for a in arrays] + [jax.ShapeDtypeStruct((8, LANES), F32)]),
        input_output_aliases={i: i for i in range(n)},
        scratch_shapes=[pltpu.SemaphoreType.DMA((n * n_fwd,)), pltpu.SemaphoreType.DMA((n * n_fwd,))],
    )(*arrays)
    return res[:n], res[n]


def _all_sum(p, name):
    def body(p_ref, o_ref, recv, send_sems, recv_sems):
        me = _coords()
        peers = _peers(me)
        recv[_slot(me)] = p_ref[...]

        def copy(k, landing):
            return pltpu.make_async_remote_copy(
                src_ref=p_ref, dst_ref=recv.at[_slot(landing)], send_sem=send_sems.at[k], recv_sem=recv_sems.at[k],
                device_id=peers[k], device_id_type=MESH)

        sends = [copy(k, me) for k in range(N_DEV - 1)]
        for cp in sends:
            cp.start()
        for k in range(N_DEV - 1):
            copy(k, peers[k]).wait_recv()
        for cp in sends:
            cp.wait_send()
        acc = recv[0]
        for s in range(1, N_DEV):
            acc = acc + recv[s]
        o_ref[...] = acc

    vmem = pl.BlockSpec(memory_space=pltpu.VMEM)
    return pl.pallas_call(
        body, name=name, in_specs=[vmem], out_specs=vmem, out_shape=jax.ShapeDtypeStruct(p.shape, F32),
        scratch_shapes=[pltpu.VMEM((N_DEV,) + p.shape, F32), pltpu.SemaphoreType.DMA((N_DEV - 1,)),
                        pltpu.SemaphoreType.DMA((N_DEV - 1,))],
    )(p)


def _adam(w, g, m, v):
    m2 = ADAM_B1 * m + (1.0 - ADAM_B1) * g
    v2 = ADAM_B2 * v + (1.0 - ADAM_B2) * (g * g)
    m_hat = m2 / (1.0 - ADAM_B1 ** ADAM_STEP)
    v_hat = v2 / (1.0 - ADAM_B2 ** ADAM_STEP)
    delta = -ADAM_LR * (m_hat / (jnp.sqrt(v_hat) + ADAM_EPS) + ADAM_WD * w)
    return delta, m2, v2


def _row_tile(rows, cols):
    best = rows
    for t in range(16, rows, 16):
        if rows % t == 0 and t * cols * 4 <= (1 << 20):
            best = t
    return best


def _adam_reduce(parts, w, m, v, name):
    rows, cols = w.shape
    tr = _row_tile(rows, cols)

    def body(p_ref, w_ref, m_ref, v_ref, g_ref, d_ref, m2_ref, v2_ref):
        g = p_ref[0].astype(F32)
        for s in range(1, N_DEV):
            g = g + p_ref[s].astype(F32)
        g_ref[...] = g
        d_ref[...], m2_ref[...], v2_ref[...] = _adam(w_ref[...], g, m_ref[...], v_ref[...])

    blk = pl.BlockSpec((tr, cols), lambda i: (i, 0))
    shp = jax.ShapeDtypeStruct((rows, cols), F32)
    return pl.pallas_call(
        body, name=name, grid=(rows // tr,),
        in_specs=[pl.BlockSpec((N_DEV, tr, cols), lambda i: (0, i, 0)), blk, blk, blk],
        out_specs=(blk,) * 4, out_shape=(shp,) * 4, compiler_params=_cparams(1),
    )(parts, w, m, v)


PACK_COLS = 1024
PACK = {'attn_norm': (0, 1, 1024), 'mem_norm': (1, 1, 1024), 'ffn_norm': (2, 1, 1024), 'b_gate': (3, 3, 1024),
        'conv_b': (6, 6, 1024), 'a_q_norm': (12, 3, 64), 'a_k_norm': (15, 3, 64), 'b_q_norm': (18, 1, 64),
        'b_k_norm': (19, 1, 64), 'm_q_norm': (20, 1, 128), 'm_k_norm': (21, 1, 128), 'b_sinks': (22, 1, 8)}
PACK_LOSS_ROW = 23
PACK_ROWS = 24


def _pack_pieces(name, width):
    r0, nr, lanes = PACK[name]
    out = []
    for j in range(nr):
        if lanes == PACK_COLS:
            w = min(PACK_COLS, width - j * PACK_COLS)
            out.append((r0 + j, slice(0, 1), slice(j * PACK_COLS, j * PACK_COLS + w), w))
        else:
            out.append((r0 + j, slice(j, j + 1), slice(0, lanes), lanes))
    return out


def _pack_small(grads, loss_tile, name):
    names = list(PACK)

    def body(*refs):
        o_ref = refs[-1]
        o_ref[...] = jnp.zeros_like(o_ref)
        for k, nm in enumerate(names):
            for row, rs, ls, w in _pack_pieces(nm, refs[k].shape[1]):
                o_ref[row:row + 1, 0:w] = refs[k][rs, ls]
        o_ref[PACK_LOSS_ROW:PACK_LOSS_ROW + 1, 0:1] = refs[len(names)][0:1, 0:1]

    vmem = pl.BlockSpec(memory_space=pltpu.VMEM)
    args = [grads[nm] for nm in names] + [loss_tile]
    return pl.pallas_call(body, name=name, in_specs=[vmem] * len(args), out_specs=vmem,
                          out_shape=jax.ShapeDtypeStruct((PACK_ROWS, PACK_COLS), F32))(*args)


def _adam_small(gsum, ws, ms, vs, name):
    names = list(PACK)
    n = len(names)

    def body(*refs):
        g_ref = refs[0]
        w_refs, m_refs, v_refs = refs[1:1 + n], refs[1 + n:1 + 2 * n], refs[1 + 2 * n:1 + 3 * n]
        outs = refs[1 + 3 * n:]
        outs[0][...] = g_ref[PACK_LOSS_ROW:PACK_LOSS_ROW + 1, 0:1]
        for k, nm in enumerate(names):
            o_g, o_d, o_m, o_v = outs[1 + 4 * k:5 + 4 * k]
            for row, rs, ls, width in _pack_pieces(nm, w_refs[k].shape[1]):
                src = (rs, ls)
                g = g_ref[row:row + 1, 0:width]
                d, m2, v2 = _adam(w_refs[k][src], g, m_refs[k][src], v_refs[k][src])
                o_g[src] = g
                o_d[src] = d
                o_m[src] = m2
                o_v[src] = v2

    vmem = pl.BlockSpec(memory_space=pltpu.VMEM)
    shapes = [jax.ShapeDtypeStruct((1, 1), F32)]
    for nm in names:
        shapes += [jax.ShapeDtypeStruct(ws[nm].shape, F32)] * 4
    args = [gsum] + [ws[nm] for nm in names] + [ms[nm] for nm in names] + [vs[nm] for nm in names]
    return pl.pallas_call(
        body, name=name, in_specs=[vmem] * len(args), out_specs=tuple([vmem] * len(shapes)), out_shape=tuple(shapes),
    )(*args)


def _as2d(name, a):
    return a.reshape(a.shape[-2], a.shape[-1]) if a.ndim == 3 else a


def kernel(x, mem, positions, attn_norm, w_in, a_q_norm, a_k_norm, b_q_norm, b_k_norm, b_sinks, mem_norm, w_mem_kv, m_q_norm, m_k_norm, w_o_a, w_o_b, w_o_m, w_gate, b_gate, w_out, ffn_norm, w_up, conv_w, conv_b, w_down, loss_target, m_attn_norm, m_w_in, m_a_q_norm, m_a_k_norm, m_b_q_norm, m_b_k_norm, m_b_sinks, m_mem_norm, m_w_mem_kv, m_m_q_norm, m_m_k_norm, m_w_o_a, m_w_o_b, m_w_o_m, m_w_gate, m_b_gate, m_w_out, m_ffn_norm, m_w_up, m_conv_w, m_conv_b, m_w_down, v_attn_norm, v_w_in, v_a_q_norm, v_a_k_norm, v_b_q_norm, v_b_k_norm, v_b_sinks, v_mem_norm, v_w_mem_kv, v_m_q_norm, v_m_k_norm, v_w_o_a, v_w_o_b, v_w_o_m, v_w_gate, v_b_gate, v_w_out, v_ffn_norm, v_w_up, v_conv_w, v_conv_b, v_w_down):
    given = dict(attn_norm=attn_norm, w_in=w_in, a_q_norm=a_q_norm, a_k_norm=a_k_norm, b_q_norm=b_q_norm, b_k_norm=b_k_norm, b_sinks=b_sinks, mem_norm=mem_norm, w_mem_kv=w_mem_kv, m_q_norm=m_q_norm, m_k_norm=m_k_norm, w_o_a=w_o_a, w_o_b=w_o_b, w_o_m=w_o_m, w_gate=w_gate, b_gate=b_gate, w_out=w_out, ffn_norm=ffn_norm, w_up=w_up, conv_w=conv_w, conv_b=conv_b, w_down=w_down)
    mom1 = dict(attn_norm=m_attn_norm, w_in=m_w_in, a_q_norm=m_a_q_norm, a_k_norm=m_a_k_norm, b_q_norm=m_b_q_norm, b_k_norm=m_b_k_norm, b_sinks=m_b_sinks, mem_norm=m_mem_norm, w_mem_kv=m_w_mem_kv, m_q_norm=m_m_q_norm, m_k_norm=m_m_k_norm, w_o_a=m_w_o_a, w_o_b=m_w_o_b, w_o_m=m_w_o_m, w_gate=m_w_gate, b_gate=m_b_gate, w_out=m_w_out, ffn_norm=m_ffn_norm, w_up=m_w_up, conv_w=m_conv_w, conv_b=m_conv_b, w_down=m_w_down)
    mom2 = dict(attn_norm=v_attn_norm, w_in=v_w_in, a_q_norm=v_a_q_norm, a_k_norm=v_a_k_norm, b_q_norm=v_b_q_norm, b_k_norm=v_b_k_norm, b_sinks=v_b_sinks, mem_norm=v_mem_norm, w_mem_kv=v_w_mem_kv, m_q_norm=v_m_q_norm, m_k_norm=v_m_k_norm, w_o_a=v_w_o_a, w_o_b=v_w_o_b, w_o_m=v_w_o_m, w_gate=v_w_gate, b_gate=v_b_gate, w_out=v_w_out, ffn_norm=v_ffn_norm, w_up=v_w_up, conv_w=v_conv_w, conv_b=v_conv_b, w_down=v_w_down)

    big = list(BIG)
    stages = {'mix': list(MIX_WEIGHTS), 'ffn': list(FFN_WEIGHTS), 'in': ['w_in']}
    my_slot = _slot(_coords())

    def shard(n):
        return given[n][0] if n == 'conv_w' else given[n][0].astype(BF16)

    def whole(n, g):
        _, r, c = g.shape
        return g.reshape(N_DEV * r, c) if BIG[n] == 0 else g.transpose(1, 0, 2).reshape(r, N_DEV * c)

    def to_blocks(n, g):
        r, c = given[n].shape[1:]
        g = g.reshape(N_DEV, r, c) if BIG[n] == 0 else g.reshape(r, N_DEV, c).transpose(1, 0, 2)
        return g if n == 'conv_w' else g.astype(BF16)

    class Hooks:
        next_stage = {'in': 'mix', 'mix': 'ffn'}

        def __init__(self):
            self.coming, self.sent = {}, {}
            self.shards = {n: shard(n) for n in big}
            self.start_gather('in', None)

        def start_gather(self, stage, after):
            src = [self.shards[n] for n in stages[stage]]
            self.coming[stage] = _exchange_start(src, f"gather_{stage}_start", gather=True, masks=CHIP_PEERS,
                                                 after=after)

        def weights(self, stage, after):
            names = stages[stage]
            after = list(after)
            if stage == 'in':
                after += [self.shards[n] for n in stages['mix'] + stages['ffn']]
            landed = _exchange_wait(self.coming[stage], after, f"gather_{stage}_wait", gather=True, masks=CHIP_PEERS)
            landed, token = _sibling_forward(landed, f"gather_{stage}_forward")
            if stage in self.next_stage:
                self.start_gather(self.next_stage[stage], token)
            return {n: whole(n, lax.dynamic_update_slice_in_dim(land, self.shards[n][None], my_slot, axis=0))
                    for n, land in zip(names, landed)}

        def grads(self, stage, g):
            blocks = [to_blocks(n, g[n]) for n in stages[stage]]
            own = [lax.dynamic_slice_in_dim(b, my_slot, 1, axis=0) for b in blocks]
            self.sent[stage] = (_exchange_start(blocks, f"exchange_{stage}_start"), own)
            return self.sent[stage][0][-1]

        def parts(self, stage, after):
            started, own = self.sent[stage]
            landed = _exchange_wait(started, [after], f"exchange_{stage}_wait")
            return {n: lax.dynamic_update_slice_in_dim(land, o, my_slot, axis=0)
                    for n, land, o in zip(stages[stage], landed, own)}

    hooks = Hooks()
    w = {}
    for n in SMALL:
        w[n] = given[n]
    w['a_q_norm'], w['a_k_norm'] = given['a_q_norm'][0], given['a_k_norm'][0]
    w['b_q_norm'], w['b_k_norm'], w['b_sinks'] = given['b_q_norm'][0], given['b_k_norm'][0], given['b_sinks'][0]

    loss_tile, grad_x, grads = _device_step(x[0], mem[0], positions[0], loss_target[0], w, hooks)
    out = {}
    after = grad_x
    for stage in ('ffn', 'mix', 'in'):
        for n, p in hooks.parts(stage, after).items():
            res = _adam_reduce(p, given[n][0], mom1[n][0], mom2[n][0], f"adam_{n}")
            out[n] = tuple(t[None] for t in res)
            after = res[0]

    small = {n: grads[n] for n in PACK}
    small['b_q_norm'], small['b_k_norm'] = grads['b_q_norm'].reshape(1, -1), grads['b_k_norm'].reshape(1, -1)
    small['b_sinks'] = grads['b_sinks'].reshape(1, -1)
    gsum = _all_sum(_pack_small(small, loss_tile, "pack_small"), "sum_small")
    ws = {n: _as2d(n, given[n]) for n in PACK}
    ms = {n: _as2d(n, mom1[n]) for n in PACK}
    vs = {n: _as2d(n, mom2[n]) for n in PACK}
    res = _adam_small(gsum, ws, ms, vs, "adam_small")
    loss = res[0].reshape(())
    for k, n in enumerate(PACK):
        out[n] = tuple(t.reshape(given[n].shape) for t in res[1 + 4 * k:5 + 4 * k])

    outs = [loss, grad_x[None]]
    for field in range(4):
        outs += [out[n][field] for n in WEIGHTS]
    return tuple(outs)
```

```python
import functools
import math

import jax
import jax.numpy as jnp
from jax import lax
from jax.experimental import pallas as pl
from jax.experimental.pallas import tpu as pltpu

F32 = jnp.float32
BF16 = jnp.bfloat16

N_DEV = 8
HEAD_DIM = 64
A_GROUPS = ((128, 1), (512, 4), (2048, 16))
B_WINDOW = 128
M_HEADS = 4
M_HEAD_DIM = 128
MEM_LEN = 256
D_FF = 2816
ROPE_THETA = 500000.0
ROPE_DIMS = 16
BLOCK = 128
EPS = 1e-6
LANES = 128
BAND_Q_BLOCKS = 8
BAND_UNITS = 2

ADAM_LR = 0.001
ADAM_B1 = 0.9
ADAM_B2 = 0.999
ADAM_EPS = 1e-08
ADAM_WD = 0.01
ADAM_STEP = 10

VMEM_LIMIT_BYTES = 56 * 1024 * 1024
GRAD_DTYPE = BF16
MESH = pl.DeviceIdType.MESH

WEIGHTS = ['attn_norm', 'w_in', 'a_q_norm', 'a_k_norm', 'b_q_norm', 'b_k_norm', 'b_sinks', 'mem_norm',
           'w_mem_kv', 'm_q_norm', 'm_k_norm', 'w_o_a', 'w_o_b', 'w_o_m', 'w_gate', 'b_gate', 'w_out',
           'ffn_norm', 'w_up', 'conv_w', 'conv_b', 'w_down']
BIG = {'w_in': 1, 'w_mem_kv': 0, 'w_o_a': 1, 'w_o_b': 1, 'w_o_m': 1, 'w_gate': 1, 'w_out': 0, 'w_up': 1,
       'conv_w': 1, 'w_down': 0}
SMALL = [n for n in WEIGHTS if n not in BIG]


def _cparams(n_grid):
    return pltpu.CompilerParams(dimension_semantics=("arbitrary",) * n_grid, vmem_limit_bytes=VMEM_LIMIT_BYTES)


def _seg_matrix(width):
    shift = width.bit_length() - 1
    r = lax.shift_right_logical(lax.broadcasted_iota(jnp.int32, (LANES, LANES), 0), shift)
    c = lax.shift_right_logical(lax.broadcasted_iota(jnp.int32, (LANES, LANES), 1), shift)
    return jnp.where(r == c, 1.0, 0.0).astype(BF16)


def _seg_sum(x, seg):
    hi = x.astype(BF16)
    r1 = x - hi.astype(F32)
    mid = r1.astype(BF16)
    lo = (r1 - mid.astype(F32)).astype(BF16)
    dot = functools.partial(jnp.dot, preferred_element_type=F32)
    return dot(hi, seg) + dot(mid, seg) + dot(lo, seg)


def _rope(y, c, s1, s2):
    return y * c + pltpu.roll(y, LANES - ROPE_DIMS // 2, 1) * s1 + pltpu.roll(y, ROPE_DIMS // 2, 1) * s2


def _unrope(dy, c, s1, s2):
    return dy * c + pltpu.roll(dy * s1, ROPE_DIMS // 2, 1) + pltpu.roll(dy * s2, LANES - ROPE_DIMS // 2, 1)


def _sigmoid(x):
    return 1.0 / (1.0 + jnp.exp(-x))


def _rms_fwd(x, gain, name):
    s_len, d = x.shape
    tm = 512

    def body(x_ref, g_ref, h_ref, ht_ref, r_ref):
        xv = x_ref[...]
        r = lax.rsqrt(jnp.mean(xv * xv, axis=-1, keepdims=True) + EPS)
        h = ((xv * r) * g_ref[...]).astype(BF16)
        h_ref[...] = h
        ht_ref[...] = h.T
        r_ref[...] = r

    return pl.pallas_call(
        body, name=name, grid=(s_len // tm,),
        in_specs=[pl.BlockSpec((tm, d), lambda i: (i, 0)), pl.BlockSpec((1, d), lambda i: (0, 0))],
        out_specs=(pl.BlockSpec((tm, d), lambda i: (i, 0)), pl.BlockSpec((d, tm), lambda i: (0, i)),
                   pl.BlockSpec((tm, 1), lambda i: (i, 0))),
        out_shape=(jax.ShapeDtypeStruct((s_len, d), BF16), jax.ShapeDtypeStruct((d, s_len), BF16),
                   jax.ShapeDtypeStruct((s_len, 1), F32)),
        compiler_params=_cparams(1),
    )(x, gain)


def _resident(shape, index_map):
    return pl.BlockSpec(shape, index_map, pipeline_mode=pl.Buffered(1))


def _mm_rows(pairs, name, nt=False, tm=512, bias=None, sigmoid=False, res=None, out_dtypes=(F32,), loss_target=None,
             rms_bwd=None):
    m = pairs[0][0].shape[0]
    n = pairs[0][1].shape[0] if nt else pairs[0][1].shape[1]
    n_pairs = len(pairs)
    has_bias, has_res, has_loss = bias is not None, res is not None, loss_target is not None
    has_rms = rms_bwd is not None
    dims = (((1,), (1,)), ((), ())) if nt else (((1,), (0,)), ((), ()))

    def body(*refs):
        acc = None
        for p in range(n_pairs):
            t = lax.dot_general(refs[2 * p][...].astype(BF16), refs[2 * p + 1][...], dims, preferred_element_type=F32)
            acc = t if acc is None else acc + t
        pos = 2 * n_pairs
        if has_bias:
            acc = acc + refs[pos][...]
            pos += 1
        if sigmoid:
            acc = _sigmoid(acc)
        if has_res:
            acc = refs[pos][...] + acc
            pos += 1
        if has_loss:
            dy_ref, dyb_ref, da_ref, l_ref = refs[pos + 1:]

            @pl.when(pl.program_id(0) == 0)
            def _():
                l_ref[...] = jnp.zeros_like(l_ref)
            err = acc - refs[pos][...]
            dy = err * (1.0 / n)
            dy_ref[...] = dy
            dyb_ref[...] = dy.astype(BF16)
            da_ref[...] = lax.dot_general(dy.astype(BF16), refs[1][...], (((1,), (1,)), ((), ())),
                                          preferred_element_type=F32).astype(BF16)
            part = 0.5 * jnp.sum(jnp.mean(err * err, axis=-1, keepdims=True), axis=0, keepdims=True)
            l_ref[...] += jnp.broadcast_to(part, l_ref.shape)
            return
        if has_rms:
            x_ref, r_ref, g_ref, add_ref = refs[pos:pos + 4]
            dg_ref = refs[-1]

            @pl.when(pl.program_id(0) == 0)
            def _():
                dg_ref[...] = jnp.zeros_like(dg_ref)
            rv = r_ref[...]
            xhat = x_ref[...] * rv
            dg_ref[...] += jnp.sum(acc * xhat, axis=0, keepdims=True)
            dxhat = acc * g_ref[...]
            acc = add_ref[...] + rv * (dxhat - xhat * jnp.mean(dxhat * xhat, axis=-1, keepdims=True))
            for o_ref in refs[pos + 4:-1]:
                o_ref[...] = acc.astype(o_ref.dtype)
            return
        for o_ref in refs[pos:]:
            o_ref[...] = acc.astype(o_ref.dtype)

    in_specs, args = [], []
    for a, w, blk in pairs:
        k = a.shape[1]
        in_specs.append(pl.BlockSpec((tm, k), lambda i: (i, 0)))
        if nt:
            in_specs.append(_resident((n, k), lambda i, blk=blk: (0, blk)))
        else:
            in_specs.append(_resident((k, n), lambda i, blk=blk: (blk, 0)))
        args += [a, w]
    if has_bias:
        in_specs.append(_resident((1, n), lambda i: (0, 0)))
        args.append(bias)
    if has_res:
        in_specs.append(pl.BlockSpec((tm, n), lambda i: (i, 0)))
        args.append(res)
    out = pl.BlockSpec((tm, n), lambda i: (i, 0))
    if has_loss:
        k0 = pairs[0][0].shape[1]
        return pl.pallas_call(
            body, name=name, grid=(m // tm,), in_specs=in_specs + [out],
            out_specs=(out, out, pl.BlockSpec((tm, k0), lambda i: (i, 0)), pl.BlockSpec((8, LANES), lambda i: (0, 0))),
            out_shape=(jax.ShapeDtypeStruct((m, n), F32), jax.ShapeDtypeStruct((m, n), BF16),
                       jax.ShapeDtypeStruct((m, k0), BF16), jax.ShapeDtypeStruct((8, LANES), F32)),
            compiler_params=_cparams(1),
        )(*args, loss_target)
    if has_rms:
        x, r, gain, add = rms_bwd
        vec = _resident((1, n), lambda i: (0, 0))
        return pl.pallas_call(
            body, name=name, grid=(m // tm,),
            in_specs=in_specs + [out, pl.BlockSpec((tm, 1), lambda i: (i, 0)), vec, out],
            out_specs=tuple([out] * len(out_dtypes) + [pl.BlockSpec((1, n), lambda i: (0, 0))]),
            out_shape=tuple([jax.ShapeDtypeStruct((m, n), dt) for dt in out_dtypes] + [jax.ShapeDtypeStruct((1, n), F32)]),
            compiler_params=_cparams(1),
        )(*args, x, r, gain, add)
    outs = pl.pallas_call(
        body, name=name, grid=(m // tm,), in_specs=in_specs, out_specs=tuple([out] * len(out_dtypes)),
        out_shape=tuple(jax.ShapeDtypeStruct((m, n), dt) for dt in out_dtypes), compiler_params=_cparams(1),
    )(*args)
    return outs[0] if len(out_dtypes) == 1 else outs


def _mm_rows_each(pairs, name, tm=512):
    m = pairs[0][0].shape[0]
    n_pairs = len(pairs)

    def body(*refs):
        for p in range(n_pairs):
            refs[2 * n_pairs + p][...] = lax.dot_general(refs[2 * p][...].astype(BF16), refs[2 * p + 1][...],
                                                         (((1,), (1,)), ((), ())), preferred_element_type=F32)

    in_specs, args = [], []
    for a, w in pairs:
        in_specs += [pl.BlockSpec((tm, a.shape[1]), lambda i: (i, 0)), _resident(w.shape, lambda i: (0, 0))]
        args += [a, w]
    return pl.pallas_call(
        body, name=name, grid=(m // tm,), in_specs=in_specs,
        out_specs=tuple(pl.BlockSpec((tm, w.shape[0]), lambda i: (i, 0)) for _, w in pairs),
        out_shape=tuple(jax.ShapeDtypeStruct((m, w.shape[0]), F32) for _, w in pairs), compiler_params=_cparams(1),
    )(*args)


def _mm_rows_cat(a, ws, name, tm=256):
    m, k = a.shape
    widths = [w.shape[1] for w in ws]
    n = sum(widths)

    def body(*refs):
        a_ref, o_ref = refs[0], refs[-1]
        av = a_ref[...]
        off = 0
        for p, width in enumerate(widths):
            o_ref[:, off:off + width] = jnp.dot(av, refs[1 + p][...], preferred_element_type=F32).astype(GRAD_DTYPE)
            off += width

    return pl.pallas_call(
        body, name=name, grid=(m // tm,),
        in_specs=[pl.BlockSpec((tm, k), lambda i: (i, 0))] + [_resident((k, wd), lambda i: (0, 0)) for wd in widths],
        out_specs=pl.BlockSpec((tm, n), lambda i: (i, 0)),
        out_shape=jax.ShapeDtypeStruct((m, n), GRAD_DTYPE), compiler_params=_cparams(1),
    )(a, *ws)


def _mm_cols(a, bs, name, tn=256):
    m, k = a.shape
    counts = [b.shape[1] // tn for b in bs]
    starts = [sum(counts[:p]) for p in range(len(bs))]

    def body(*refs):
        a_ref, o_ref = refs[0], refs[-1]
        j = pl.program_id(0)
        for p, b_ref in enumerate(refs[1:-1]):
            @pl.when((j >= starts[p]) & (j < starts[p] + counts[p]))
            def _():
                o_ref[...] = jnp.dot(a_ref[...], b_ref[...].astype(BF16), preferred_element_type=F32).astype(GRAD_DTYPE)

    b_specs = [pl.BlockSpec((k, tn), lambda j, s=s, c=c: (0, jnp.clip(j - s, 0, c - 1))) for s, c in zip(starts, counts)]
    return pl.pallas_call(
        body, name=name, grid=(sum(counts),),
        in_specs=[_resident((m, k), lambda j: (0, 0))] + b_specs,
        out_specs=pl.BlockSpec((m, tn), lambda j: (0, j)),
        out_shape=jax.ShapeDtypeStruct((m, sum(counts) * tn), GRAD_DTYPE), compiler_params=_cparams(1),
    )(a, *bs)


def _mm_tn_each(pairs, name, tile=256):
    n = pairs[0][1].shape[1]
    n_pairs = len(pairs)
    dims = (((0,), (0,)), ((), ()))

    def body(*refs):
        for p in range(n_pairs):
            refs[2 * n_pairs + p][...] = lax.dot_general(refs[2 * p][...].astype(BF16), refs[2 * p + 1][...].astype(BF16),
                                                         dims, preferred_element_type=F32).astype(GRAD_DTYPE)

    in_specs, args = [], []
    for a, b in pairs:
        in_specs += [_resident(a.shape, lambda j: (0, 0)), pl.BlockSpec((b.shape[0], tile), lambda j: (0, j))]
        args += [a, b]
    return pl.pallas_call(
        body, name=name, grid=(n // tile,), in_specs=in_specs,
        out_specs=tuple(pl.BlockSpec((a.shape[1], tile), lambda j: (0, j)) for a, _ in pairs),
        out_shape=tuple(jax.ShapeDtypeStruct((a.shape[1], n), GRAD_DTYPE) for a, _ in pairs),
        compiler_params=_cparams(1),
    )(*args)


def _mm_tn(a, b, name, tile=256):
    k, m = a.shape
    n = b.shape[1]
    dims = (((0,), (0,)), ((), ()))

    def body(a_ref, b_ref, o_ref):
        o_ref[...] = lax.dot_general(a_ref[...].astype(BF16), b_ref[...].astype(BF16), dims,
                                     preferred_element_type=F32).astype(GRAD_DTYPE)

    if n <= m:
        t = min(tile, m)
        grid, a_spec, b_spec = (m // t,), pl.BlockSpec((k, t), lambda i: (0, i)), _resident((k, n), lambda i: (0, 0))
        o_spec = pl.BlockSpec((t, n), lambda i: (i, 0))
    else:
        t = min(tile, n)
        grid, a_spec, b_spec = (n // t,), _resident((k, m), lambda i: (0, 0)), pl.BlockSpec((k, t), lambda i: (0, i))
        o_spec = pl.BlockSpec((m, t), lambda i: (0, i))
    return pl.pallas_call(
        body, name=name, grid=grid, in_specs=[a_spec, b_spec], out_specs=o_spec,
        out_shape=jax.ShapeDtypeStruct((m, n), GRAD_DTYPE), compiler_params=_cparams(1),
    )(a, b)


def _norm_rope(t, gain, c, s1, s2, seg):
    rs = lax.rsqrt(_seg_sum(t * t, seg) * (1.0 / HEAD_DIM) + EPS)
    return _rope((t * rs) * gain, c, s1, s2)


def _dup_half(y, half):
    lane = lax.broadcasted_iota(jnp.int32, y.shape, 1)
    rolled = pltpu.roll(y, HEAD_DIM, 1)
    keep = (lane < HEAD_DIM) if half == 0 else (lane >= HEAD_DIM)
    return jnp.where(keep, y, rolled)


def _qk_prep(proj, cb0, d, gqa, gq, gk, tabs, name):
    s_len = proj.shape[0]
    tm = 512
    rows = tm // d
    n_units = 4 if gqa else 2 * d
    n_q = 4 if gqa else 2
    n_in = 6

    def body(*refs):
        in_refs = refs[:n_in]
        gq_ref, gk_ref, c_ref, s1_ref, s2_ref, o_ref = refs[n_in:]
        seg = _seg_matrix(HEAD_DIM)

        def rows_of(ref, r):
            return ref[...] if d == 1 else ref[pl.ds(r, rows, stride=d), :]

        def put(unit_col, y):
            o_ref[:, unit_col * LANES:(unit_col + 1) * LANES] = y.astype(BF16)

        for r in range(d):
            c, s1, s2 = rows_of(c_ref, r), rows_of(s1_ref, r), rows_of(s2_ref, r)
            for b in range(n_in):
                t = rows_of(in_refs[b], r)
                if b < n_q:
                    put((b * d + r) if not gqa else b, _norm_rope(t, gq_ref[...], c, s1, s2, seg))
                elif not gqa:
                    sec, pair = (1, b - 2) if b < 4 else (2, b - 4)
                    y = _norm_rope(t, gk_ref[...], c, s1, s2, seg) if sec == 1 else t
                    put(sec * n_units + pair * d + r, y)
                else:
                    sec = 1 if b == 4 else 2
                    y = _norm_rope(t, gk_ref[...], c, s1, s2, seg) if sec == 1 else t
                    for u in range(n_units):
                        put(sec * n_units + u, _dup_half(y, u // 2))

    in_specs = [pl.BlockSpec((tm, LANES), lambda i, b=b: (i, cb0 + b)) for b in range(n_in)]
    vec = pl.BlockSpec((1, LANES), lambda i: (0, 0))
    tab = pl.BlockSpec((tm, LANES), lambda i: (i, 0))
    width = 3 * n_units * LANES
    return pl.pallas_call(
        body, name=name, grid=(s_len // tm,), in_specs=in_specs + [vec, vec, tab, tab, tab],
        out_specs=pl.BlockSpec((rows, width), lambda i: (i, 0)),
        out_shape=jax.ShapeDtypeStruct((s_len // d, width), BF16), compiler_params=_cparams(1),
    )(*([proj] * n_in), gq, gk, *tabs)


def _qk_prep_bwd(dqkv, proj, cb0, d, gqa, gq, gk, tabs, name):
    s_len = proj.shape[0]
    tm = 512
    rows = tm // d
    n_units = 4 if gqa else 2 * d
    n_q = 4 if gqa else 2
    n_in = 6

    def body(*refs):
        d_refs = refs[0:3]
        in_refs = refs[3:3 + n_in]
        gq_ref, gk_ref, c_ref, s1_ref, s2_ref, o_ref, dgq_ref, dgk_ref, stage = refs[3 + n_in:]
        seg = _seg_matrix(HEAD_DIM)

        @pl.when(pl.program_id(0) == 0)
        def _():
            dgq_ref[...] = jnp.zeros_like(dgq_ref)
            dgk_ref[...] = jnp.zeros_like(dgk_ref)

        def rows_of(ref, r):
            return ref[...] if d == 1 else ref[pl.ds(r, rows, stride=d), :]

        def unit(col):
            sec, u = divmod(col, n_units)
            return d_refs[sec][:, u * LANES:(u + 1) * LANES]

        def norm_bwd(dyr, t, gain, c, s1, s2, dg_ref):
            rs = lax.rsqrt(_seg_sum(t * t, seg) * (1.0 / HEAD_DIM) + EPS)
            that = t * rs
            dy = _unrope(dyr, c, s1, s2)
            dg_ref[...] += jnp.sum(dy * that, axis=0, keepdims=True)
            dthat = dy * gain
            return rs * (dthat - that * (_seg_sum(dthat * that, seg) * (1.0 / HEAD_DIM)))

        def fold(sec):
            tot = []
            for u in range(n_units):
                v = unit(sec * n_units + u)
                tot.append(v + pltpu.roll(v, HEAD_DIM, 1))
            lane = lax.broadcasted_iota(jnp.int32, tot[0].shape, 1)
            return jnp.where(lane < HEAD_DIM, tot[0] + tot[1], tot[2] + tot[3])

        for b in range(n_in):
            for r in range(d):
                c, s1, s2 = rows_of(c_ref, r), rows_of(s1_ref, r), rows_of(s2_ref, r)
                t = rows_of(in_refs[b], r)
                if b < n_q:
                    g = unit((b * d + r) if not gqa else b)
                    out = norm_bwd(g, t, gq_ref[...], c, s1, s2, dgq_ref)
                elif not gqa:
                    sec, pair = (1, b - 2) if b < 4 else (2, b - 4)
                    g = unit(sec * n_units + pair * d + r)
                    out = norm_bwd(g, t, gk_ref[...], c, s1, s2, dgk_ref) if sec == 1 else g
                else:
                    sec = 1 if b == 4 else 2
                    g = fold(sec)
                    out = norm_bwd(g, t, gk_ref[...], c, s1, s2, dgk_ref) if sec == 1 else g
                if d == 1:
                    o_ref[:, b * LANES:(b + 1) * LANES] = out.astype(BF16)
                else:
                    stage[pl.ds(r, rows, stride=d), :] = out
            if d != 1:
                o_ref[:, b * LANES:(b + 1) * LANES] = stage[...].astype(BF16)

    in_specs = [pl.BlockSpec((rows, n_units * LANES), lambda i: (i, 0))] * 3
    in_specs += [pl.BlockSpec((tm, LANES), lambda i, b=b: (i, cb0 + b)) for b in range(n_in)]
    vec = pl.BlockSpec((1, LANES), lambda i: (0, 0))
    tab = pl.BlockSpec((tm, LANES), lambda i: (i, 0))
    return pl.pallas_call(
        body, name=name, grid=(s_len // tm,), in_specs=in_specs + [vec, vec, tab, tab, tab],
        out_specs=(pl.BlockSpec((tm, n_in * LANES), lambda i: (i, 0)), vec, vec),
        out_shape=(jax.ShapeDtypeStruct((s_len, n_in * LANES), BF16), jax.ShapeDtypeStruct((1, LANES), F32),
                   jax.ShapeDtypeStruct((1, LANES), F32)),
        scratch_shapes=[pltpu.VMEM((tm, LANES), F32)], compiler_params=_cparams(1),
    )(*dqkv, *([proj] * n_in), gq, gk, *tabs)


def _head_masks(shape):
    lane = lax.broadcasted_iota(jnp.int32, shape, 1)
    return lane < HEAD_DIM, lane >= HEAD_DIM


def _band_fwd(qkv, n_units, max_dist, sinks, name):
    n_rows = qkv.shape[0]
    nb = n_rows // BLOCK
    scale = HEAD_DIM ** -0.5
    has_sink = sinks is not None
    assert not has_sink or max_dist < BLOCK

    qn, un = min(nb, BAND_Q_BLOCKS), BAND_UNITS
    ug = n_units // un

    def body(*refs):
        q_ref, kp_ref, km_ref, vp_ref, vm_ref = refs[:5]
        o_ref, lse_ref = refs[-2:]
        i = pl.program_id(1)
        qi = lax.broadcasted_iota(jnp.int32, (BLOCK, 2 * BLOCK), 0)
        kj = lax.broadcasted_iota(jnp.int32, (BLOCK, 2 * BLOCK), 1)
        dist = qi + BLOCK - kj
        band = (dist >= 0) & (dist <= max_dist)
        band_first = band & ((i > 0) | (kj >= BLOCK))
        m0, m1 = _head_masks((BLOCK, LANES))
        zero = jnp.zeros((BLOCK, LANES), BF16)
        for ub in range(un):
            cs = slice(ub * LANES, (ub + 1) * LANES)
            for qb in range(qn):
                rs = slice(qb * BLOCK, (qb + 1) * BLOCK)
                q = q_ref[rs, cs]
                if qb == 0:
                    kk = jnp.concatenate([kp_ref[:, cs], km_ref[0:BLOCK, cs]], axis=0)
                    vv = jnp.concatenate([vp_ref[:, cs], vm_ref[0:BLOCK, cs]], axis=0)
                    valid = band_first
                else:
                    kk = km_ref[(qb - 1) * BLOCK:(qb + 1) * BLOCK, cs]
                    vv = vm_ref[(qb - 1) * BLOCK:(qb + 1) * BLOCK, cs]
                    valid = band
                outs, lses = [], []
                for e, hm in enumerate((m0, m1)):
                    qe = jnp.where(hm, q, zero)
                    s = lax.dot_general(qe, kk, (((1,), (1,)), ((), ())), preferred_element_type=F32) * scale
                    s = jnp.where(valid, s, -jnp.inf)
                    if has_sink:
                        s = jnp.where(kj == 0, refs[5][ub][:, e * HEAD_DIM:e * HEAD_DIM + 1], s)
                    mx = jnp.max(s, axis=-1, keepdims=True)
                    p = jnp.exp(s - mx)
                    den = jnp.sum(p, axis=-1, keepdims=True)
                    pn = p * (1.0 / den)
                    if has_sink:
                        pn = jnp.where(kj == 0, 0.0, pn)
                    pn = pn.astype(BF16)
                    outs.append(jnp.dot(pn, vv, preferred_element_type=F32))
                    lses.append(mx + jnp.log(den))
                o_ref[rs, cs] = jnp.where(m0, outs[0], outs[1])
                lse_ref[rs, cs] = jnp.where(m0, jnp.broadcast_to(lses[0], (BLOCK, LANES)),
                                            jnp.broadcast_to(lses[1], (BLOCK, LANES)))

    def main(sec):
        return pl.BlockSpec((qn * BLOCK, un * LANES), lambda u, i: (i, sec * ug + u))

    def prev(sec):
        return pl.BlockSpec((BLOCK, un * LANES), lambda u, i: (jnp.maximum(i * qn - 1, 0), sec * ug + u))

    in_specs = [main(0), prev(1), main(1), prev(2), main(2)]
    args = [qkv] * 5
    if has_sink:
        in_specs.append(pl.BlockSpec((un, 1, LANES), lambda u, i: (u, 0, 0)))
        args.append(sinks)
    return pl.pallas_call(
        body, name=name, grid=(ug, nb // qn), in_specs=in_specs, out_specs=(main(0), main(0)),
        out_shape=(jax.ShapeDtypeStruct((n_rows, n_units * LANES), F32),) * 2, compiler_params=_cparams(2),
    )(*args)


def _band_bwd(qkv, do, lse, delta, n_units, max_dist, name):
    n_rows = qkv.shape[0]
    nb = n_rows // BLOCK
    scale = HEAD_DIM ** -0.5

    qn, un = min(nb, BAND_Q_BLOCKS), BAND_UNITS
    ug = n_units // un
    steps = nb // qn
    nt_dims = (((1,), (1,)), ((), ()))
    tn_dims = (((0,), (0,)), ((), ()))

    def body(qm_ref, qx_ref, kp_ref, km_ref, vp_ref, vm_ref, dom_ref, dox_ref, lm_ref, lx_ref, dm_ref, dx_ref,
             dq_ref, dk_ref, dv_ref):
        i = pl.program_id(1)
        m0, m1 = _head_masks((BLOCK, LANES))
        zero = jnp.zeros((BLOCK, LANES), BF16)
        qi = lax.broadcasted_iota(jnp.int32, (BLOCK, 2 * BLOCK), 0)
        kj = lax.broadcasted_iota(jnp.int32, (BLOCK, 2 * BLOCK), 1)
        dist = qi + BLOCK - kj
        band = (dist >= 0) & (dist <= max_dist)
        band_first = band & ((i > 0) | (kj >= BLOCK))
        qr = lax.broadcasted_iota(jnp.int32, (BLOCK, BLOCK), 0)
        kc = lax.broadcasted_iota(jnp.int32, (BLOCK, BLOCK), 1)
        dist_x = qr + BLOCK - kc
        band_next = (dist_x >= 0) & (dist_x <= max_dist) & (i < steps - 1)

        def pair(q, dob, lse_b, del_b, kk, vv, valid):
            dqs, dk, dv = [], None, None
            for e, hm in enumerate((m0, m1)):
                col = slice(e * HEAD_DIM, e * HEAD_DIM + 1)
                qe = jnp.where(hm, q, zero)
                doe = jnp.where(hm, dob, zero)
                s = lax.dot_general(qe, kk, nt_dims, preferred_element_type=F32) * scale
                p = jnp.where(valid, jnp.exp(s - lse_b[:, col]), 0.0)
                dp = lax.dot_general(doe, vv, nt_dims, preferred_element_type=F32)
                ds = (p * (dp - del_b[:, col]) * scale).astype(BF16)
                dqs.append(jnp.dot(ds, kk, preferred_element_type=F32))
                dk_e = lax.dot_general(ds, qe, tn_dims, preferred_element_type=F32)
                dv_e = lax.dot_general(p.astype(BF16), doe, tn_dims, preferred_element_type=F32)
                dk = dk_e if dk is None else dk + dk_e
                dv = dv_e if dv is None else dv + dv_e
            return jnp.where(m0, dqs[0], dqs[1]), dk, dv

        for ub in range(un):
            cs = slice(ub * LANES, (ub + 1) * LANES)
            dk_acc, dv_acc = [None] * qn, [None] * qn

            def add(acc, kb, part):
                acc[kb] = part if acc[kb] is None else acc[kb] + part

            for qb in range(qn):
                rs = slice(qb * BLOCK, (qb + 1) * BLOCK)
                if qb == 0:
                    kk = jnp.concatenate([kp_ref[:, cs], km_ref[0:BLOCK, cs]], axis=0)
                    vv = jnp.concatenate([vp_ref[:, cs], vm_ref[0:BLOCK, cs]], axis=0)
                    valid = band_first
                else:
                    kk = km_ref[(qb - 1) * BLOCK:(qb + 1) * BLOCK, cs]
                    vv = vm_ref[(qb - 1) * BLOCK:(qb + 1) * BLOCK, cs]
                    valid = band
                dq, dk, dv = pair(qm_ref[rs, cs], dom_ref[rs, cs], lm_ref[rs, cs], dm_ref[rs, cs], kk, vv, valid)
                dq_ref[rs, cs] = dq
                if qb > 0:
                    add(dk_acc, qb - 1, dk[0:BLOCK])
                    add(dv_acc, qb - 1, dv[0:BLOCK])
                add(dk_acc, qb, dk[BLOCK:2 * BLOCK])
                add(dv_acc, qb, dv[BLOCK:2 * BLOCK])
            last = slice((qn - 1) * BLOCK, qn * BLOCK)
            _, dk, dv = pair(qx_ref[:, cs], dox_ref[:, cs], lx_ref[:, cs], dx_ref[:, cs], km_ref[last, cs], vm_ref[last, cs],
                             band_next)
            add(dk_acc, qn - 1, dk)
            add(dv_acc, qn - 1, dv)
            for kb in range(qn):
                dk_ref[kb * BLOCK:(kb + 1) * BLOCK, cs] = dk_acc[kb]
                dv_ref[kb * BLOCK:(kb + 1) * BLOCK, cs] = dv_acc[kb]

    def main(sec):
        return pl.BlockSpec((qn * BLOCK, un * LANES), lambda u, i: (i, sec * ug + u))

    def prev(sec):
        return pl.BlockSpec((BLOCK, un * LANES), lambda u, i: (jnp.maximum(i * qn - 1, 0), sec * ug + u))

    def nxt(sec):
        return pl.BlockSpec((BLOCK, un * LANES), lambda u, i: (jnp.minimum((i + 1) * qn, nb - 1), sec * ug + u))

    in_specs = [main(0), nxt(0), prev(1), main(1), prev(2), main(2),
                main(0), nxt(0), main(0), nxt(0), main(0), nxt(0)]
    args = [qkv] * 6 + [do, do, lse, lse, delta, delta]
    shp = jax.ShapeDtypeStruct((n_rows, n_units * LANES), F32)
    return pl.pallas_call(
        body, name=name, grid=(ug, steps), in_specs=in_specs, out_specs=(main(0), main(0), main(0)),
        out_shape=(shp, shp, shp), compiler_params=_cparams(2),
    )(*args)


def _merge_groups(os_, lses, dils, name):
    s_len = os_[0].shape[0] * dils[0]
    tm = 512

    def body(*refs):
        o_refs, l_refs = refs[0:3], refs[3:6]
        o_ref, lse_ref = refs[6:8]
        so, sl = refs[8:11], refs[11:14]
        for pair in range(2):
            for g, d in enumerate(dils):
                rows = tm // d
                for r in range(d):
                    col = slice((pair * d + r) * LANES, (pair * d + r + 1) * LANES)
                    if d == 1:
                        so[g][...] = o_refs[g][:, col]
                        sl[g][...] = l_refs[g][:, col]
                    else:
                        so[g][pl.ds(r, rows, stride=d), :] = o_refs[g][:, col]
                        sl[g][pl.ds(r, rows, stride=d), :] = l_refs[g][:, col]
            l0, l1, l2 = sl[0][...], sl[1][...], sl[2][...]
            mx = jnp.maximum(jnp.maximum(l0, l1), l2)
            e0, e1, e2 = jnp.exp(l0 - mx), jnp.exp(l1 - mx), jnp.exp(l2 - mx)
            den = e0 + e1 + e2
            inv = 1.0 / den
            o_ref[:, pair * LANES:(pair + 1) * LANES] = (so[0][...] * (e0 * inv) + so[1][...] * (e1 * inv)
                                                         + so[2][...] * (e2 * inv))
            lse_ref[:, pair * LANES:(pair + 1) * LANES] = mx + jnp.log(den)

    in_specs = [pl.BlockSpec((tm // d, 2 * d * LANES), lambda i: (i, 0)) for d in dils] * 2
    out = pl.BlockSpec((tm, 2 * LANES), lambda i: (i, 0))
    shp = jax.ShapeDtypeStruct((s_len, 2 * LANES), F32)
    return pl.pallas_call(
        body, name=name, grid=(s_len // tm,), in_specs=in_specs, out_specs=(out, out), out_shape=(shp, shp),
        scratch_shapes=[pltpu.VMEM((tm, LANES), F32)] * 6, compiler_params=_cparams(1),
    )(*os_, *lses)


def _bwd_prep(do, o, lse, dils, sinks, name):
    s_len, width = do.shape
    n_pairs = width // LANES
    tm = 512
    has_sink = sinks is not None
    n_g = len(dils)

    def body(*refs):
        do_ref, o_ref, lse_ref = refs[:3]
        pos = 3
        if has_sink:
            sink_ref = refs[pos]
            pos += 1
        outs = refs[pos:pos + 3 * n_g]
        pos += 3 * n_g
        if has_sink:
            dsink_ref = refs[pos]
            pos += 1
        s_do, s_l, s_d = refs[pos:pos + 3]
        seg = _seg_matrix(HEAD_DIM)

        if has_sink:
            @pl.when(pl.program_id(0) == 0)
            def _():
                dsink_ref[...] = jnp.zeros_like(dsink_ref)

        for pair in range(n_pairs):
            col = slice(pair * LANES, (pair + 1) * LANES)
            dov = do_ref[:, col]
            lv = lse_ref[:, col]
            delta = _seg_sum(dov * o_ref[:, col], seg)
            if has_sink:
                dsink_ref[pair] += -jnp.sum(jnp.exp(sink_ref[pair] - lv) * delta, axis=0, keepdims=True)
            s_do[...] = dov
            s_l[...] = lv
            s_d[...] = delta
            for g, d in enumerate(dils):
                rows = tm // d
                for r in range(d):
                    oc = slice((pair * d + r) * LANES, (pair * d + r + 1) * LANES)
                    if d == 1:
                        a, b, c = s_do[...], s_l[...], s_d[...]
                    else:
                        a = s_do[pl.ds(r, rows, stride=d), :]
                        b = s_l[pl.ds(r, rows, stride=d), :]
                        c = s_d[pl.ds(r, rows, stride=d), :]
                    outs[3 * g][:, oc] = a.astype(BF16)
                    outs[3 * g + 1][:, oc] = b
                    outs[3 * g + 2][:, oc] = c

    row = pl.BlockSpec((tm, width), lambda i: (i, 0))
    in_specs = [row, row, row]
    args = [do, o, lse]
    if has_sink:
        in_specs.append(pl.BlockSpec((n_pairs, 1, LANES), lambda i: (0, 0, 0)))
        args.append(sinks)
    out_specs, out_shape = [], []
    for d in dils:
        for dt in (BF16, F32, F32):
            out_specs.append(pl.BlockSpec((tm // d, n_pairs * d * LANES), lambda i: (i, 0)))
            out_shape.append(jax.ShapeDtypeStruct((s_len // d, n_pairs * d * LANES), dt))
    if has_sink:
        out_specs.append(pl.BlockSpec((n_pairs, 1, LANES), lambda i: (0, 0, 0)))
        out_shape.append(jax.ShapeDtypeStruct((n_pairs, 1, LANES), F32))
    return pl.pallas_call(
        body, name=name, grid=(s_len // tm,), in_specs=in_specs, out_specs=tuple(out_specs),
        out_shape=tuple(out_shape), scratch_shapes=[pltpu.VMEM((tm, LANES), F32)] * 3, compiler_params=_cparams(1),
    )(*args)


def _mem_kv(mem, mem_gain, w_kv, k_gain, name):
    m_len = mem.shape[0]
    kw = M_HEADS * M_HEAD_DIM

    def body(mem_ref, mg_ref, w_ref, kg_ref, k_ref, v_ref):
        mv = mem_ref[...]
        r = lax.rsqrt(jnp.mean(mv * mv, axis=-1, keepdims=True) + EPS)
        mn = ((mv * r) * mg_ref[...]).astype(BF16)
        kv = jnp.dot(mn, w_ref[...], preferred_element_type=F32)
        for h in range(M_HEADS):
            col = slice(h * M_HEAD_DIM, (h + 1) * M_HEAD_DIM)
            t = kv[:, col]
            rk = lax.rsqrt(jnp.mean(t * t, axis=-1, keepdims=True) + EPS)
            k_ref[:, col] = ((t * rk) * kg_ref[...]).astype(BF16)
        v_ref[...] = kv[:, kw:].astype(BF16)

    shp = jax.ShapeDtypeStruct((m_len, kw), BF16)
    return pl.pallas_call(body, name=name, out_shape=(shp, shp),
                          compiler_params=pltpu.CompilerParams(vmem_limit_bytes=VMEM_LIMIT_BYTES))(mem, mem_gain, w_kv, k_gain)


def _mem_kv_bwd(mem, mem_gain, w_kv, k_gain, dk, dv, name):
    m_len, d = mem.shape
    kw = M_HEADS * M_HEAD_DIM

    def body(mem_ref, mg_ref, w_ref, kg_ref, dk_ref, dv_ref, dw_ref, dmg_ref, dkg_ref, dkv_ref):
        mv = mem_ref[...]
        r = lax.rsqrt(jnp.mean(mv * mv, axis=-1, keepdims=True) + EPS)
        mhat = mv * r
        mn = (mhat * mg_ref[...]).astype(BF16)
        kv = jnp.dot(mn, w_ref[...], preferred_element_type=F32)
        dkg = jnp.zeros((1, M_HEAD_DIM), F32)
        for h in range(M_HEADS):
            col = slice(h * M_HEAD_DIM, (h + 1) * M_HEAD_DIM)
            t = kv[:, col]
            rk = lax.rsqrt(jnp.mean(t * t, axis=-1, keepdims=True) + EPS)
            that = t * rk
            dy = dk_ref[:, col]
            dkg = dkg + jnp.sum(dy * that, axis=0, keepdims=True)
            dthat = dy * kg_ref[...]
            dkv_ref[:, col] = (rk * (dthat - that * jnp.mean(dthat * that, axis=-1, keepdims=True))).astype(BF16)
        dkv_ref[:, kw:] = dv_ref[...].astype(BF16)
        dkg_ref[...] = dkg
        dkv = dkv_ref[...]
        dw_ref[...] = lax.dot_general(mn, dkv, (((0,), (0,)), ((), ())), preferred_element_type=F32).astype(GRAD_DTYPE)
        dmn = lax.dot_general(dkv, w_ref[...], (((1,), (1,)), ((), ())), preferred_element_type=F32)
        dmg_ref[...] = jnp.sum(dmn * mhat, axis=0, keepdims=True)

    return pl.pallas_call(
        body, name=name,
        out_shape=(jax.ShapeDtypeStruct((d, 2 * kw), GRAD_DTYPE), jax.ShapeDtypeStruct((1, d), F32),
                   jax.ShapeDtypeStruct((1, M_HEAD_DIM), F32)),
        scratch_shapes=[pltpu.VMEM((m_len, 2 * kw), BF16)],
        compiler_params=pltpu.CompilerParams(vmem_limit_bytes=VMEM_LIMIT_BYTES),
    )(mem, mem_gain, w_kv, k_gain, dk, dv)


def _mem_attn_fwd(proj, cidx, mk, mv, q_gain, name):
    s_len = proj.shape[0]
    kw = M_HEADS * M_HEAD_DIM
    tm = 512
    scale = M_HEAD_DIM ** -0.5

    def body(q_ref, k_ref, v_ref, g_ref, o_ref):
        for h in range(M_HEADS):
            col = slice(h * M_HEAD_DIM, (h + 1) * M_HEAD_DIM)
            t = q_ref[:, col]
            rs = lax.rsqrt(jnp.mean(t * t, axis=-1, keepdims=True) + EPS)
            qn = ((t * rs) * g_ref[...]).astype(BF16)
            s = lax.dot_general(qn, k_ref[:, col], (((1,), (1,)), ((), ())), preferred_element_type=F32) * scale
            mx = jnp.max(s, axis=-1, keepdims=True)
            p = jnp.exp(s - mx)
            pn = (p * (1.0 / jnp.sum(p, axis=-1, keepdims=True))).astype(BF16)
            o_ref[:, col] = jnp.dot(pn, v_ref[:, col], preferred_element_type=F32).astype(BF16)

    whole = pl.BlockSpec((MEM_LEN, kw), lambda i: (0, 0))
    return pl.pallas_call(
        body, name=name, grid=(s_len // tm,),
        in_specs=[pl.BlockSpec((tm, kw), lambda i: (i, cidx)), whole, whole, pl.BlockSpec((1, M_HEAD_DIM), lambda i: (0, 0))],
        out_specs=pl.BlockSpec((tm, kw), lambda i: (i, 0)),
        out_shape=jax.ShapeDtypeStruct((s_len, kw), BF16), compiler_params=_cparams(1),
    )(proj, mk, mv, q_gain)


def _mem_attn_bwd(proj, cidx, mk, mv, q_gain, do, name):
    s_len = proj.shape[0]
    kw = M_HEADS * M_HEAD_DIM
    tm = 512
    scale = M_HEAD_DIM ** -0.5

    def body(q_ref, k_ref, v_ref, g_ref, do_ref, dq_ref, dk_ref, dv_ref, dg_ref):
        @pl.when(pl.program_id(0) == 0)
        def _():
            dk_ref[...] = jnp.zeros_like(dk_ref)
            dv_ref[...] = jnp.zeros_like(dv_ref)
            dg_ref[...] = jnp.zeros_like(dg_ref)

        for h in range(M_HEADS):
            col = slice(h * M_HEAD_DIM, (h + 1) * M_HEAD_DIM)
            t = q_ref[:, col]
            rs = lax.rsqrt(jnp.mean(t * t, axis=-1, keepdims=True) + EPS)
            that = t * rs
            qn = (that * g_ref[...]).astype(BF16)
            kh, vh = k_ref[:, col], v_ref[:, col]
            dob = do_ref[:, col].astype(BF16)
            s = lax.dot_general(qn, kh, (((1,), (1,)), ((), ())), preferred_element_type=F32) * scale
            mx = jnp.max(s, axis=-1, keepdims=True)
            p = jnp.exp(s - mx)
            p = p * (1.0 / jnp.sum(p, axis=-1, keepdims=True))
            dp = lax.dot_general(dob, vh, (((1,), (1,)), ((), ())), preferred_element_type=F32)
            ds = (p * (dp - jnp.sum(p * dp, axis=-1, keepdims=True)) * scale).astype(BF16)
            dqn = jnp.dot(ds, kh, preferred_element_type=F32)
            dk_ref[:, col] += lax.dot_general(ds, qn, (((0,), (0,)), ((), ())), preferred_element_type=F32)
            dv_ref[:, col] += lax.dot_general(p.astype(BF16), dob, (((0,), (0,)), ((), ())), preferred_element_type=F32)
            dg_ref[...] += jnp.sum(dqn * that, axis=0, keepdims=True)
            dthat = dqn * g_ref[...]
            dq_ref[:, col] = (rs * (dthat - that * jnp.mean(dthat * that, axis=-1, keepdims=True))).astype(BF16)

    whole = pl.BlockSpec((MEM_LEN, kw), lambda i: (0, 0))
    vec = pl.BlockSpec((1, M_HEAD_DIM), lambda i: (0, 0))
    row = pl.BlockSpec((tm, kw), lambda i: (i, 0))
    return pl.pallas_call(
        body, name=name, grid=(s_len // tm,),
        in_specs=[pl.BlockSpec((tm, kw), lambda i: (i, cidx)), whole, whole, vec, row],
        out_specs=(row, whole, whole, vec),
        out_shape=(jax.ShapeDtypeStruct((s_len, kw), BF16), jax.ShapeDtypeStruct((MEM_LEN, kw), F32),
                   jax.ShapeDtypeStruct((MEM_LEN, kw), F32), jax.ShapeDtypeStruct((1, M_HEAD_DIM), F32)),
        compiler_params=_cparams(1),
    )(proj, mk, mv, q_gain, do)


def _project_merge(outs, w_outs, gates, w_out, x, name):
    s_len = gates.shape[0]
    d = w_outs[0].shape[1]
    tm = 512

    def body(oa_ref, ob_ref, om_ref, wa_ref, wb_ref, wm_ref, g_ref, wo_ref, x_ref,
             pa_ref, pb_ref, pm_ref, merged_ref, x1_ref):
        merged = None
        for k, (o_ref, w_ref, p_ref) in enumerate(((oa_ref, wa_ref, pa_ref), (ob_ref, wb_ref, pb_ref), (om_ref, wm_ref, pm_ref))):
            p = jnp.dot(o_ref[...].astype(BF16), w_ref[...], preferred_element_type=F32).astype(BF16)
            p_ref[...] = p
            t = g_ref[:, k * d:(k + 1) * d].astype(F32) * p.astype(F32)
            merged = t if merged is None else merged + t
        merged = merged.astype(BF16)
        merged_ref[...] = merged
        x1_ref[...] = x_ref[...] + jnp.dot(merged, wo_ref[...], preferred_element_type=F32)

    row = pl.BlockSpec((tm, d), lambda i: (i, 0))
    shp = jax.ShapeDtypeStruct((s_len, d), BF16)
    in_specs = [pl.BlockSpec((tm, o.shape[1]), lambda i: (i, 0)) for o in outs]
    in_specs += [_resident(w.shape, lambda i: (0, 0)) for w in w_outs]
    in_specs += [pl.BlockSpec((tm, 3 * d), lambda i: (i, 0)), _resident(w_out.shape, lambda i: (0, 0)), row]
    return pl.pallas_call(
        body, name=name, grid=(s_len // tm,), in_specs=in_specs, out_specs=(row, row, row, row, row),
        out_shape=(shp, shp, shp, shp, jax.ShapeDtypeStruct((s_len, d), F32)), compiler_params=_cparams(1),
    )(*outs, *w_outs, gates, w_out, x)


def _project_merge_bwd(dx1, w_out, gates, pa, pb, pm, name):
    s_len, d = pa.shape
    tm = 512

    def body(dx_ref, w_ref, g_ref, a_ref, b_ref, m_ref, da_ref, db_ref, dmm_ref, dg_ref, dbg_ref):
        @pl.when(pl.program_id(0) == 0)
        def _():
            dbg_ref[...] = jnp.zeros_like(dbg_ref)
        dm = lax.dot_general(dx_ref[...], w_ref[...], (((1,), (1,)), ((), ())), preferred_element_type=F32)
        for k, (p_ref, dp_ref) in enumerate(((a_ref, da_ref), (b_ref, db_ref), (m_ref, dmm_ref))):
            col = slice(k * d, (k + 1) * d)
            g = g_ref[:, col].astype(F32)
            dp_ref[...] = (dm * g).astype(BF16)
            dpre = (dm * p_ref[...].astype(F32)) * (g * (1.0 - g))
            dbg_ref[:, col] += jnp.sum(dpre, axis=0, keepdims=True)
            dg_ref[:, col] = dpre.astype(BF16)

    row = pl.BlockSpec((tm, d), lambda i: (i, 0))
    wide = pl.BlockSpec((tm, 3 * d), lambda i: (i, 0))
    shp = jax.ShapeDtypeStruct((s_len, d), BF16)
    return pl.pallas_call(
        body, name=name, grid=(s_len // tm,), in_specs=[row, _resident(w_out.shape, lambda i: (0, 0)), wide, row, row, row],
        out_specs=(row, row, row, wide, pl.BlockSpec((1, 3 * d), lambda i: (0, 0))),
        out_shape=(shp, shp, shp, jax.ShapeDtypeStruct((s_len, 3 * d), BF16), jax.ShapeDtypeStruct((1, 3 * d), F32)),
        compiler_params=_cparams(1),
    )(dx1, w_out, gates, pa, pb, pm)


CONV_CHUNK = 256


def _pick_row(tile, j):
    row = lax.broadcasted_iota(jnp.int32, tile.shape, 0)
    return jnp.sum(jnp.where(row == j, tile, jnp.zeros_like(tile)), axis=0, keepdims=True)


def _rows_before(ref, start, k):
    cur = ref[pl.ds(start, CONV_CHUNK), :].astype(F32)
    prev = ref[pl.ds(pl.multiple_of(jnp.maximum(start - 16, 0), 16), 16), :].astype(F32)
    prev = jnp.where(start > 0, prev, jnp.zeros_like(prev))
    rolled = pltpu.roll(cur, k, 0)
    row = lax.broadcasted_iota(jnp.int32, cur.shape, 0)
    for j in range(k):
        rolled = jnp.where(row == j, _pick_row(prev, 16 - k + j), rolled)
    return rolled


def _rows_after(ref, start, k):
    cur = ref[pl.ds(start, CONV_CHUNK), :]
    nxt = ref[pl.ds(pl.multiple_of(start + CONV_CHUNK, 8), 8), :]
    rolled = pltpu.roll(cur, CONV_CHUNK - k, 0)
    row = lax.broadcasted_iota(jnp.int32, cur.shape, 0)
    for j in range(k):
        rolled = jnp.where(row == CONV_CHUNK - k + j, _pick_row(nxt, j), rolled)
    return rolled


def _conv_pre(u_ref, w_ref, b_ref, start):
    u2 = _rows_before(u_ref, start, 2)
    u1 = _rows_before(u_ref, start, 1)
    u0 = u_ref[pl.ds(start, CONV_CHUNK), :].astype(F32)
    c = ((b_ref[...] + w_ref[0:1, :] * u2) + w_ref[1:2, :] * u1) + w_ref[2:3, :] * u0
    return c, (u2, u1, u0)


def _norm_up_conv_glu(x, gain, w_up, conv_w, conv_b, name):
    s_len, d = x.shape
    tm, tn = 512, 2 * LANES
    nblk = D_FF // tn

    def body(x_ref, g_ref, w_ref, cw_ref, cb_ref, ht_ref, r_ref, u_ref, act_ref, halo):
        @pl.when(pl.program_id(0) == 0)
        def _():
            halo[...] = jnp.zeros_like(halo)
        xv = x_ref[...]
        r = lax.rsqrt(jnp.mean(xv * xv, axis=-1, keepdims=True) + EPS)
        hv = ((xv * r) * g_ref[...]).astype(BF16)
        ht_ref[...] = hv.T
        r_ref[...] = r
        row = lax.broadcasted_iota(jnp.int32, (tm, tn), 0)
        for j in range(nblk):
            conv = []
            for half in range(2):
                cb = half * nblk + j
                cols = slice(cb * tn, (cb + 1) * tn)
                ub = jnp.dot(hv, w_ref[:, cols], preferred_element_type=F32).astype(BF16)
                u_ref[:, cols] = ub
                u0 = ub.astype(F32)
                prev = halo[cb]
                u1 = jnp.where(row == 0, _pick_row(prev, 7), pltpu.roll(u0, 1, 0))
                u2 = pltpu.roll(u0, 2, 0)
                u2 = jnp.where(row == 0, _pick_row(prev, 6), jnp.where(row == 1, _pick_row(prev, 7), u2))
                halo[cb] = u0[tm - 8:tm, :]
                conv.append(((cb_ref[:, cols] + cw_ref[0:1, cols] * u2) + cw_ref[1:2, cols] * u1)
                            + cw_ref[2:3, cols] * u0)
            act_ref[:, j * tn:(j + 1) * tn] = ((conv[0] * _sigmoid(conv[0])) * conv[1]).astype(BF16)

    return pl.pallas_call(
        body, name=name, grid=(s_len // tm,),
        in_specs=[pl.BlockSpec((tm, d), lambda i: (i, 0)), _resident((1, d), lambda i: (0, 0)),
                  _resident((d, 2 * D_FF), lambda i: (0, 0)),
                  _resident((3, 2 * D_FF), lambda i: (0, 0)), _resident((1, 2 * D_FF), lambda i: (0, 0))],
        out_specs=(pl.BlockSpec((d, tm), lambda i: (0, i)), pl.BlockSpec((tm, 1), lambda i: (i, 0)),
                   pl.BlockSpec((tm, 2 * D_FF), lambda i: (i, 0)), pl.BlockSpec((tm, D_FF), lambda i: (i, 0))),
        out_shape=(jax.ShapeDtypeStruct((d, s_len), BF16), jax.ShapeDtypeStruct((s_len, 1), F32),
                   jax.ShapeDtypeStruct((s_len, 2 * D_FF), BF16), jax.ShapeDtypeStruct((s_len, D_FF), BF16)),
        scratch_shapes=[pltpu.VMEM((2 * nblk, 8, tn), F32)], compiler_params=_cparams(1),
    )(x, gain, w_up, conv_w, conv_b)


def _conv_glu_bwd(dact, u, conv_w, conv_b, name):
    s_len = u.shape[0]
    nblk = D_FF // LANES
    n_chunks = s_len // CONV_CHUNK

    def body(da_ref, ua_ref, ug_ref, wa_ref, wg_ref, ba_ref, bg_ref,
             dua_ref, dug_ref, dwa_ref, dwg_ref, dba_ref, dbg_ref, sa, sg):
        sa[pl.ds(s_len, 8), :] = jnp.zeros((8, LANES), F32)
        sg[pl.ds(s_len, 8), :] = jnp.zeros((8, LANES), F32)
        zero = jnp.zeros((1, LANES), F32)

        def chunk1(ci, carry):
            start = pl.multiple_of(ci * CONV_CHUNK, CONV_CHUNK)
            ca, ua = _conv_pre(ua_ref, wa_ref, ba_ref, start)
            cg, ug = _conv_pre(ug_ref, wg_ref, bg_ref, start)
            dact_v = da_ref[pl.ds(start, CONV_CHUNK), :].astype(F32)
            sig = _sigmoid(ca)
            dcg = dact_v * (ca * sig)
            dca = (dact_v * cg) * (sig * (1.0 + ca * (1.0 - sig)))
            sa[pl.ds(start, CONV_CHUNK), :] = dca
            sg[pl.ds(start, CONV_CHUNK), :] = dcg
            out = [carry[0] + jnp.sum(dca, axis=0, keepdims=True), carry[1] + jnp.sum(dcg, axis=0, keepdims=True)]
            for j in range(3):
                out.append(carry[2 + j] + jnp.sum(dca * ua[j], axis=0, keepdims=True))
            for j in range(3):
                out.append(carry[5 + j] + jnp.sum(dcg * ug[j], axis=0, keepdims=True))
            return tuple(out)

        acc = lax.fori_loop(0, n_chunks, chunk1, (zero,) * 8)
        dba_ref[...] = acc[0]
        dbg_ref[...] = acc[1]
        for j in range(3):
            dwa_ref[j:j + 1, :] = acc[2 + j]
            dwg_ref[j:j + 1, :] = acc[5 + j]

        def chunk2(ci, carry):
            start = pl.multiple_of(ci * CONV_CHUNK, CONV_CHUNK)
            for s_ref, w_ref, o_ref in ((sa, wa_ref, dua_ref), (sg, wg_ref, dug_ref)):
                d0 = s_ref[pl.ds(start, CONV_CHUNK), :]
                d1 = _rows_after(s_ref, start, 1)
                d2 = _rows_after(s_ref, start, 2)
                o_ref[pl.ds(start, CONV_CHUNK), :] = (w_ref[2:3, :] * d0 + w_ref[1:2, :] * d1
                                                      + w_ref[0:1, :] * d2).astype(BF16)
            return carry
        lax.fori_loop(0, n_chunks, chunk2, 0)

    def col(rows, off):
        return pl.BlockSpec((rows, LANES), lambda j: (0, off + j))

    big = jax.ShapeDtypeStruct((s_len, D_FF), BF16)
    return pl.pallas_call(
        body, name=name, grid=(nblk,),
        in_specs=[col(s_len, 0), col(s_len, 0), col(s_len, nblk), col(3, 0), col(3, nblk), col(1, 0), col(1, nblk)],
        out_specs=(col(s_len, 0), col(s_len, 0), col(3, 0), col(3, 0), col(1, 0), col(1, 0)),
        out_shape=(big, big, jax.ShapeDtypeStruct((3, D_FF), F32), jax.ShapeDtypeStruct((3, D_FF), F32),
                   jax.ShapeDtypeStruct((1, D_FF), F32), jax.ShapeDtypeStruct((1, D_FF), F32)),
        scratch_shapes=[pltpu.VMEM((s_len + 8, LANES), F32)] * 2, compiler_params=_cparams(1),
    )(dact, u, u, conv_w, conv_w, conv_b, conv_b)


def _rope_tables(positions):
    half = ROPE_DIMS // 2
    freqs = jnp.exp(jnp.arange(half, dtype=F32) * (-2.0 * math.log(ROPE_THETA) / ROPE_DIMS))
    ang = positions.reshape(-1).astype(F32)[:, None] * freqs
    cos, sin = jnp.cos(ang), jnp.sin(ang)
    n = ang.shape[0]
    zeros = lambda w: jnp.zeros((n, w), F32)
    c = jnp.concatenate([cos, cos, jnp.ones((n, HEAD_DIM - ROPE_DIMS), F32)], axis=1)
    s1 = jnp.concatenate([-sin, zeros(HEAD_DIM - half)], axis=1)
    s2 = jnp.concatenate([zeros(half), sin, zeros(HEAD_DIM - ROPE_DIMS)], axis=1)
    return tuple(jnp.tile(t, (1, 2)) for t in (c, s1, s2))


def _two(v):
    return jnp.tile(v.reshape(1, HEAD_DIM), (1, 2))


def _fold_heads(g):
    return g[0, :HEAD_DIM] + g[0, HEAD_DIM:]


MIX_WEIGHTS = ('w_gate', 'w_mem_kv', 'w_o_a', 'w_o_b', 'w_o_m', 'w_out')
FFN_WEIGHTS = ('w_up', 'conv_w', 'w_down')


def _device_step(x, mem, positions, target, w, hooks=None):
    tabs = _rope_tables(positions)
    dils = tuple(d for _, d in A_GROUPS)
    grads = {}
    w = dict(w)

    h, h_t, r1 = _rms_fwd(x, w['attn_norm'], "rms1")
    if hooks is not None:
        w.update(hooks.weights('in', [h, *tabs]))
    proj = _mm_rows([(h, w['w_in'], 0)], "mm_in")

    qkv_a, o_g, lse_g = [], [], []
    for gi, (window, d) in enumerate(A_GROUPS):
        gq, gk = _two(w['a_q_norm'][gi]), _two(w['a_k_norm'][gi])
        qkv = _qk_prep(proj, 6 * gi, d, False, gq, gk, tabs, f"qk_prep_a{gi}")
        o, lse = _band_fwd(qkv, 2 * d, window // d, None, f"band_fwd_a{gi}")
        qkv_a.append(qkv)
        o_g.append(o)
        lse_g.append(lse)
    o_a, lse_a = _merge_groups(o_g, lse_g, dils, "merge_a")
    if hooks is not None:
        w.update(hooks.weights('mix', [o_a]))

    gbq, gbk = _two(w['b_q_norm']), _two(w['b_k_norm'])
    sinks = jnp.repeat(w['b_sinks'].reshape(4, 2), HEAD_DIM, axis=1).reshape(4, 1, LANES)
    qkv_b = _qk_prep(proj, 18, 1, True, gbq, gbk, tabs, "qk_prep_b")
    o_b, lse_b = _band_fwd(qkv_b, 4, B_WINDOW - 1, sinks, "band_fwd_b")

    gates = _mm_rows([(h, w['w_gate'], 0)], "mm_gate", bias=w['b_gate'], sigmoid=True, out_dtypes=(BF16,))
    mk, mv = _mem_kv(mem, w['mem_norm'], w['w_mem_kv'], w['m_k_norm'], "mem_kv")
    o_m = _mem_attn_fwd(proj, 6, mk, mv, w['m_q_norm'], "mem_attn")

    pa, pb, pm, merged, x1 = _project_merge((o_a, o_b, o_m), (w['w_o_a'], w['w_o_b'], w['w_o_m']), gates, w['w_out'], x,
                                            "project_merge")

    if hooks is not None:
        w.update(hooks.weights('ffn', [x1]))
    h2_t, r2, u, act = _norm_up_conv_glu(x1, w['ffn_norm'], w['w_up'], w['conv_w'], w['conv_b'], "norm_up_conv_glu")
    dy, dy_b, dact, loss = _mm_rows([(act, w['w_down'], 0)], "mm_down", res=x1, loss_target=target)

    grads['w_down'] = _mm_tn(act, dy_b, "mm_dw_down")
    du_a, du_g, dcw_a, dcw_g, dcb_a, dcb_g = _conv_glu_bwd(dact, u, w['conv_w'], w['conv_b'], "conv_glu_bwd")
    grads['conv_w'] = jnp.concatenate([dcw_a, dcw_g], axis=1)
    grads['conv_b'] = jnp.concatenate([dcb_a, dcb_g], axis=1)
    grads['w_up'] = _mm_cols(h2_t, [du_a, du_g], "mm_dw_up")
    ffn_gain = w['ffn_norm']
    if hooks is not None:
        ffn_gain = ffn_gain + hooks.grads('ffn', grads)[0:1, 0:1]
    dx1, dx1_b, grads['ffn_norm'] = _mm_rows([(du_a, w['w_up'], 0), (du_g, w['w_up'], 1)], "mm_d_h2", nt=True,
                                             rms_bwd=(x1, r2, ffn_gain, dy), out_dtypes=(F32, BF16))

    grads['w_out'] = _mm_tn(merged, dx1_b, "mm_dw_out")
    dpa, dpb, dpm, dgpre, grads['b_gate'] = _project_merge_bwd(dx1_b, w['w_out'], gates, pa, pb, pm,
                                                               "project_merge_bwd")
    do_a, do_b, do_m = _mm_rows_each([(dpa, w['w_o_a']), (dpb, w['w_o_b']), (dpm, w['w_o_m'])], "mm_d_o")
    grads['w_o_a'], grads['w_o_b'], grads['w_o_m'] = _mm_tn_each([(o_a, dpa), (o_b, dpb), (o_m, dpm)], "mm_dw_o")
    grads['w_gate'] = _mm_cols(h_t, [dgpre], "mm_dw_gate")
    dq_m, dmk, dmv, grads['m_q_norm'] = _mem_attn_bwd(proj, 6, mk, mv, w['m_q_norm'], do_m, "mem_attn_bwd")
    grads['w_mem_kv'], grads['mem_norm'], grads['m_k_norm'] = _mem_kv_bwd(
        mem, w['mem_norm'], w['w_mem_kv'], w['m_k_norm'], dmk, dmv, "mem_kv_bwd")
    a_gain = w['a_q_norm']
    if hooks is not None:
        a_gain = a_gain + hooks.grads('mix', grads)[0:1, 0:1]

    prep = _bwd_prep(do_a, o_a, lse_a, dils, None, "bwd_prep_a")
    dproj, dgq_a, dgk_a = [], [], []
    for gi, (window, d) in enumerate(A_GROUPS):
        gq, gk = _two(a_gain[gi]), _two(w['a_k_norm'][gi])
        dqkv = _band_bwd(qkv_a[gi], prep[3 * gi], prep[3 * gi + 1], prep[3 * gi + 2], 2 * d, window // d,
                         f"band_bwd_a{gi}")
        dp, dgq, dgk = _qk_prep_bwd(dqkv, proj, 6 * gi, d, False, gq, gk, tabs, f"qk_prep_bwd_a{gi}")
        dproj.append(dp)
        dgq_a.append(_fold_heads(dgq))
        dgk_a.append(_fold_heads(dgk))
    grads['a_q_norm'] = jnp.stack(dgq_a)
    grads['a_k_norm'] = jnp.stack(dgk_a)

    do_bu, lse_bu, delta_bu, dsink = _bwd_prep(do_b, o_b, lse_b, (1,), sinks, "bwd_prep_b")
    dqkv = _band_bwd(qkv_b, do_bu, lse_bu, delta_bu, 4, B_WINDOW - 1, "band_bwd_b")
    dp_b, dgq, dgk = _qk_prep_bwd(dqkv, proj, 18, 1, True, gbq, gbk, tabs, "qk_prep_bwd_b")
    dproj.append(dp_b)
    grads['b_q_norm'] = _fold_heads(dgq)
    grads['b_k_norm'] = _fold_heads(dgk)
    grads['b_sinks'] = jnp.stack([dsink[:, 0, 0], dsink[:, 0, HEAD_DIM]], axis=1).reshape(8)

    dproj.append(dq_m)

    cols = (0, 1, 2, 3, 6)
    grads['w_in'] = _mm_rows_cat(h_t, dproj, "mm_dw_in")
    attn_gain = w['attn_norm']
    if hooks is not None:
        attn_gain = attn_gain + hooks.grads('in', grads)[0:1, 0:1]
    grad_x, grads['attn_norm'] = _mm_rows(
        [(dp, w['w_in'], c) for dp, c in zip(dproj, cols)] + [(dgpre, w['w_gate'], 0)], "mm_d_h", nt=True,
        rms_bwd=(x, r1, attn_gain, dx1))
    return loss, grad_x, grads


def _coords():
    return lax.axis_index("x"), lax.axis_index("y"), lax.axis_index("c")


def _slot(p):
    return 4 * p[0] + 2 * p[1] + p[2]


ALL_PEERS = tuple(range(1, N_DEV))
CHIP_PEERS = (1, 4, 2, 6)
OTHER_CHIPS = (4, 2, 6)


def _peers(me, masks=ALL_PEERS):
    x, y, c = me
    return [(1 - x if mask & 4 else x, 1 - y if mask & 2 else y, 1 - c if mask & 1 else c) for mask in masks]


HBM_SPEC = pl.BlockSpec(memory_space=pltpu.HBM)


SEM_SPEC = pl.BlockSpec(memory_space=pltpu.SEMAPHORE)
SIDE_EFFECT = pltpu.SideEffectType.DATAFLOW_SIDE_EFFECTING


def _exchange_start(blocks, name, gather=False, masks=ALL_PEERS, after=None):
    n = len(blocks)
    n_peers = len(masks)
    n_in = 2 * n + (0 if after is None else 1)

    def body(*refs):
        ins, lands = refs[:n], refs[n:2 * n]
        send_sems, recv_sems = refs[n_in], refs[n_in + 1]
        token = refs[-1]
        me = _coords()
        peers = _peers(me, masks)
        for a in range(n):
            for k in range(n_peers):
                pltpu.make_async_remote_copy(
                    src_ref=ins[a] if gather else ins[a].at[_slot(peers[k])], dst_ref=lands[a].at[_slot(me)],
                    send_sem=send_sems.at[a * n_peers + k], recv_sem=recv_sems.at[a * n_peers + k],
                    device_id=peers[k], device_id_type=MESH).start()
        token[...] = jnp.zeros_like(token)

    land_shapes = [((N_DEV,) + b.shape) if gather else b.shape for b in blocks]
    hbm_in = [pltpu.HBM(b.shape, b.dtype) for b in blocks]
    hbm_land = [pltpu.HBM(s, b.dtype) for s, b in zip(land_shapes, blocks)]
    sems = pltpu.SemaphoreType.DMA((n * n_peers,))
    ins = [pltpu.with_memory_space_constraint(b, pltpu.HBM) for b in blocks]
    lands = [pltpu.with_memory_space_constraint(lax.empty(s, b.dtype), pltpu.HBM) for s, b in zip(land_shapes, blocks)]
    return pl.pallas_call(
        body, name=name, out_shape=(sems, sems, *hbm_in, *hbm_land, jax.ShapeDtypeStruct((8, LANES), F32)),
        in_specs=[HBM_SPEC] * (2 * n) + ([] if after is None else [pl.BlockSpec(memory_space=pl.ANY)]),
        out_specs=(SEM_SPEC, SEM_SPEC, *([HBM_SPEC] * (2 * n)), pl.BlockSpec(memory_space=pltpu.VMEM)),
        input_output_aliases={i: 2 + i for i in range(2 * n)},
        compiler_params=pltpu.CompilerParams(has_side_effects=SIDE_EFFECT),
    )(*ins, *lands, *([] if after is None else [after]))


def _exchange_wait(started, after, name, gather=False, masks=ALL_PEERS):
    n = (len(started) - 3) // 2
    n_peers = len(masks)
    send_sems, recv_sems = started[0], started[1]
    thru = started[2:2 + 2 * n]

    def body(*refs):
        ins, lands = refs[:n], refs[n:2 * n]
        send_ref, recv_ref = refs[2 * n], refs[2 * n + 1]
        me = _coords()
        peers = _peers(me, masks)
        for a in range(n):
            for k in range(n_peers):
                cp = pltpu.make_async_remote_copy(
                    src_ref=ins[a] if gather else ins[a].at[_slot(peers[k])], dst_ref=lands[a].at[_slot(peers[k])],
                    send_sem=send_ref.at[a * n_peers + k], recv_sem=recv_ref.at[a * n_peers + k],
                    device_id=peers[k], device_id_type=MESH)
                cp.wait_send()
                cp.wait_recv()

    hbm = [pltpu.HBM(t.shape, t.dtype) for t in thru]
    res = pl.pallas_call(
        body, name=name, out_shape=tuple(hbm),
        in_specs=[HBM_SPEC] * (2 * n) + [SEM_SPEC, SEM_SPEC] + [pl.BlockSpec(memory_space=pl.ANY)] * len(after),
        out_specs=tuple([HBM_SPEC] * (2 * n)), input_output_aliases={i: i for i in range(2 * n)},
        compiler_params=pltpu.CompilerParams(has_side_effects=SIDE_EFFECT),
    )(*thru, send_sems, recv_sems, *after)
    return res[n:]


def _sibling_forward(arrays, name):
    n = len(arrays)
    n_fwd = len(OTHER_CHIPS)

    def body(*refs):
        bufs = refs[n:2 * n]
        token, send_sems, recv_sems = refs[2 * n:]
        token[...] = jnp.zeros_like(token)
        x, y, c = _coords()
        sibling = (x, y, 1 - c)
        mine = _peers((x, y, c), OTHER_CHIPS)
        theirs = _peers(sibling, OTHER_CHIPS)

        def copy(a, k, block):
            rows = bufs[a].at[_slot(block)]
            return pltpu.make_async_remote_copy(
                src_ref=rows, dst_ref=rows, send_sem=send_sems.at[a * n_fwd + k], recv_sem=recv_sems.at[a * n_fwd + k],
                device_id=sibling, device_id_type=MESH)

        sends = [copy(a, k, mine[k]) for a in range(n) for k in range(n_fwd)]
        for cp in sends:
            cp.start()
        for a in range(n):
            for k in range(n_fwd):
                copy(a, k, theirs[k]).wait_recv()
        for cp in sends:
            cp.wait_send()

    res = pl.pallas_call(
        body, name=name, in_specs=[HBM_SPEC] * n,
        out_specs=tuple([HBM_SPEC] * n + [pl.BlockSpec(memory_space=pltpu.VMEM)]),
        out_shape=tuple([jax.ShapeDtypeStruct(a.shape, a.dtype) for a in arrays] + [jax.ShapeDtypeStruct((8, LANES), F32)]),
        input_output_aliases={i: i for i in range(n)},
        scratch_shapes=[pltpu.SemaphoreType.DMA((n * n_fwd,)), pltpu.SemaphoreType.DMA((n * n_fwd,))],
    )(*arrays)
    return res[:n], res[n]


def _all_sum(p, name):
    def body(p_ref, o_ref, recv, send_sems, recv_sems):
        me = _coords()
        peers = _peers(me)
        recv[_slot(me)] = p_ref[...]

        def copy(k, landing):
            return pltpu.make_async_remote_copy(
                src_ref=p_ref, dst_ref=recv.at[_slot(landing)], send_sem=send_sems.at[k], recv_sem=recv_sems.at[k],
                device_id=peers[k], device_id_type=MESH)

        sends = [copy(k, me) for k in range(N_DEV - 1)]
        for cp in sends:
            cp.start()
        for k in range(N_DEV - 1):
            copy(k, peers[k]).wait_recv()
        for cp in sends:
            cp.wait_send()
        acc = recv[0]
        for s in range(1, N_DEV):
            acc = acc + recv[s]
        o_ref[...] = acc

    vmem = pl.BlockSpec(memory_space=pltpu.VMEM)
    return pl.pallas_call(
        body, name=name, in_specs=[vmem], out_specs=vmem, out_shape=jax.ShapeDtypeStruct(p.shape, F32),
        scratch_shapes=[pltpu.VMEM((N_DEV,) + p.shape, F32), pltpu.SemaphoreType.DMA((N_DEV - 1,)),
                        pltpu.SemaphoreType.DMA((N_DEV - 1,))],
    )(p)


def _adam(w, g, m, v):
    m2 = ADAM_B1 * m + (1.0 - ADAM_B1) * g
    v2 = ADAM_B2 * v + (1.0 - ADAM_B2) * (g * g)
    m_hat = m2 / (1.0 - ADAM_B1 ** ADAM_STEP)
    v_hat = v2 / (1.0 - ADAM_B2 ** ADAM_STEP)
    delta = -ADAM_LR * (m_hat / (jnp.sqrt(v_hat) + ADAM_EPS) + ADAM_WD * w)
    return delta, m2, v2


def _row_tile(rows, cols):
    best = rows
    for t in range(16, rows, 16):
        if rows % t == 0 and t * cols * 4 <= (1 << 20):
            best = t
    return best


def _adam_reduce(parts, w, m, v, name):
    rows, cols = w.shape
    tr = _row_tile(rows, cols)

    def body(p_ref, w_ref, m_ref, v_ref, g_ref, d_ref, m2_ref, v2_ref):
        g = p_ref[0].astype(F32)
        for s in range(1, N_DEV):
            g = g + p_ref[s].astype(F32)
        g_ref[...] = g
        d_ref[...], m2_ref[...], v2_ref[...] = _adam(w_ref[...], g, m_ref[...], v_ref[...])

    blk = pl.BlockSpec((tr, cols), lambda i: (i, 0))
    shp = jax.ShapeDtypeStruct((rows, cols), F32)
    return pl.pallas_call(
        body, name=name, grid=(rows // tr,),
        in_specs=[pl.BlockSpec((N_DEV, tr, cols), lambda i: (0, i, 0)), blk, blk, blk],
        out_specs=(blk,) * 4, out_shape=(shp,) * 4, compiler_params=_cparams(1),
    )(parts, w, m, v)


PACK_COLS = 1024
PACK = {'attn_norm': (0, 1, 1024), 'mem_norm': (1, 1, 1024), 'ffn_norm': (2, 1, 1024), 'b_gate': (3, 3, 1024),
        'conv_b': (6, 6, 1024), 'a_q_norm': (12, 3, 64), 'a_k_norm': (15, 3, 64), 'b_q_norm': (18, 1, 64),
        'b_k_norm': (19, 1, 64), 'm_q_norm': (20, 1, 128), 'm_k_norm': (21, 1, 128), 'b_sinks': (22, 1, 8)}
PACK_LOSS_ROW = 23
PACK_ROWS = 24


def _pack_pieces(name, width):
    r0, nr, lanes = PACK[name]
    out = []
    for j in range(nr):
        if lanes == PACK_COLS:
            w = min(PACK_COLS, width - j * PACK_COLS)
            out.append((r0 + j, slice(0, 1), slice(j * PACK_COLS, j * PACK_COLS + w), w))
        else:
            out.append((r0 + j, slice(j, j + 1), slice(0, lanes), lanes))
    return out


def _pack_small(grads, loss_tile, name):
    names = list(PACK)

    def body(*refs):
        o_ref = refs[-1]
        o_ref[...] = jnp.zeros_like(o_ref)
        for k, nm in enumerate(names):
            for row, rs, ls, w in _pack_pieces(nm, refs[k].shape[1]):
                o_ref[row:row + 1, 0:w] = refs[k][rs, ls]
        o_ref[PACK_LOSS_ROW:PACK_LOSS_ROW + 1, 0:1] = refs[len(names)][0:1, 0:1]

    vmem = pl.BlockSpec(memory_space=pltpu.VMEM)
    args = [grads[nm] for nm in names] + [loss_tile]
    return pl.pallas_call(body, name=name, in_specs=[vmem] * len(args), out_specs=vmem,
                          out_shape=jax.ShapeDtypeStruct((PACK_ROWS, PACK_COLS), F32))(*args)


def _adam_small(gsum, ws, ms, vs, name):
    names = list(PACK)
    n = len(names)

    def body(*refs):
        g_ref = refs[0]
        w_refs, m_refs, v_refs = refs[1:1 + n], refs[1 + n:1 + 2 * n], refs[1 + 2 * n:1 + 3 * n]
        outs = refs[1 + 3 * n:]
        outs[0][...] = g_ref[PACK_LOSS_ROW:PACK_LOSS_ROW + 1, 0:1]
        for k, nm in enumerate(names):
            o_g, o_d, o_m, o_v = outs[1 + 4 * k:5 + 4 * k]
            for row, rs, ls, width in _pack_pieces(nm, w_refs[k].shape[1]):
                src = (rs, ls)
                g = g_ref[row:row + 1, 0:width]
                d, m2, v2 = _adam(w_refs[k][src], g, m_refs[k][src], v_refs[k][src])
                o_g[src] = g
                o_d[src] = d
                o_m[src] = m2
                o_v[src] = v2

    vmem = pl.BlockSpec(memory_space=pltpu.VMEM)
    shapes = [jax.ShapeDtypeStruct((1, 1), F32)]
    for nm in names:
        shapes += [jax.ShapeDtypeStruct(ws[nm].shape, F32)] * 4
    args = [gsum] + [ws[nm] for nm in names] + [ms[nm] for nm in names] + [vs[nm] for nm in names]
    return pl.pallas_call(
        body, name=name, in_specs=[vmem] * len(args), out_specs=tuple([vmem] * len(shapes)), out_shape=tuple(shapes),
    )(*args)


def _as2d(name, a):
    return a.reshape(a.shape[-2], a.shape[-1]) if a.ndim == 3 else a


def kernel(x, mem, positions, attn_norm, w_in, a_q_norm, a_k_norm, b_q_norm, b_k_norm, b_sinks, mem_norm, w_mem_kv, m_q_norm, m_k_norm, w_o_a, w_o_b, w_o_m, w_gate, b_gate, w_out, ffn_norm, w_up, conv_w, conv_b, w_down, loss_target, m_attn_norm, m_w_in, m_a_q_norm, m_a_k_norm, m_b_q_norm, m_b_k_norm, m_b_sinks, m_mem_norm, m_w_mem_kv, m_m_q_norm, m_m_k_norm, m_w_o_a, m_w_o_b, m_w_o_m, m_w_gate, m_b_gate, m_w_out, m_ffn_norm, m_w_up, m_conv_w, m_conv_b, m_w_down, v_attn_norm, v_w_in, v_a_q_norm, v_a_k_norm, v_b_q_norm, v_b_k_norm, v_b_sinks, v_mem_norm, v_w_mem_kv, v_m_q_norm, v_m_k_norm, v_w_o_a, v_w_o_b, v_w_o_m, v_w_gate, v_b_gate, v_w_out, v_ffn_norm, v_w_up, v_conv_w, v_conv_b, v_w_down):
    given = dict(attn_norm=attn_norm, w_in=w_in, a_q_norm=a_q_norm, a_k_norm=a_k_norm, b_q_norm=b_q_norm, b_k_norm=b_k_norm, b_sinks=b_sinks, mem_norm=mem_norm, w_mem_kv=w_mem_kv, m_q_norm=m_q_norm, m_k_norm=m_k_norm, w_o_a=w_o_a, w_o_b=w_o_b, w_o_m=w_o_m, w_gate=w_gate, b_gate=b_gate, w_out=w_out, ffn_norm=ffn_norm, w_up=w_up, conv_w=conv_w, conv_b=conv_b, w_down=w_down)
    mom1 = dict(attn_norm=m_attn_norm, w_in=m_w_in, a_q_norm=m_a_q_norm, a_k_norm=m_a_k_norm, b_q_norm=m_b_q_norm, b_k_norm=m_b_k_norm, b_sinks=m_b_sinks, mem_norm=m_mem_norm, w_mem_kv=m_w_mem_kv, m_q_norm=m_m_q_norm, m_k_norm=m_m_k_norm, w_o_a=m_w_o_a, w_o_b=m_w_o_b, w_o_m=m_w_o_m, w_gate=m_w_gate, b_gate=m_b_gate, w_out=m_w_out, ffn_norm=m_ffn_norm, w_up=m_w_up, conv_w=m_conv_w, conv_b=m_conv_b, w_down=m_w_down)
    mom2 = dict(attn_norm=v_attn_norm, w_in=v_w_in, a_q_norm=v_a_q_norm, a_k_norm=v_a_k_norm, b_q_norm=v_b_q_norm, b_k_norm=v_b_k_norm, b_sinks=v_b_sinks, mem_norm=v_mem_norm, w_mem_kv=v_w_mem_kv, m_q_norm=v_m_q_norm, m_k_norm=v_m_k_norm, w_o_a=v_w_o_a, w_o_b=v_w_o_b, w_o_m=v_w_o_m, w_gate=v_w_gate, b_gate=v_b_gate, w_out=v_w_out, ffn_norm=v_ffn_norm, w_up=v_w_up, conv_w=v_conv_w, conv_b=v_conv_b, w_down=v_w_down)

    big = list(BIG)
    stages = {'mix': list(MIX_WEIGHTS), 'ffn': list(FFN_WEIGHTS), 'in': ['w_in']}
    my_slot = _slot(_coords())

    def shard(n):
        return given[n][0] if n == 'conv_w' else given[n][0].astype(BF16)

    def whole(n, g):
        _, r, c = g.shape
        return g.reshape(N_DEV * r, c) if BIG[n] == 0 else g.transpose(1, 0, 2).reshape(r, N_DEV * c)

    def to_blocks(n, g):
        r, c = given[n].shape[1:]
        g = g.reshape(N_DEV, r, c) if BIG[n] == 0 else g.reshape(r, N_DEV, c).transpose(1, 0, 2)
        return g if n == 'conv_w' else g.astype(BF16)

    class Hooks:
        next_stage = {'in': 'mix', 'mix': 'ffn'}

        def __init__(self):
            self.coming, self.sent = {}, {}
            self.shards = {n: shard(n) for n in big}
            self.start_gather('in', None)

        def start_gather(self, stage, after):
            src = [self.shards[n] for n in stages[stage]]
            self.coming[stage] = _exchange_start(src, f"gather_{stage}_start", gather=True, masks=CHIP_PEERS,
                                                 after=after)

        def weights(self, stage, after):
            names = stages[stage]
            after = list(after)
            if stage == 'in':
                after += [self.shards[n] for n in stages['mix'] + stages['ffn']]
            landed = _exchange_wait(self.coming[stage], after, f"gather_{stage}_wait", gather=True, masks=CHIP_PEERS)
            landed, token = _sibling_forward(landed, f"gather_{stage}_forward")
            if stage in self.next_stage:
                self.start_gather(self.next_stage[stage], token)
            return {n: whole(n, lax.dynamic_update_slice_in_dim(land, self.shards[n][None], my_slot, axis=0))
                    for n, land in zip(names, landed)}

        def grads(self, stage, g):
            blocks = [to_blocks(n, g[n]) for n in stages[stage]]
            own = [lax.dynamic_slice_in_dim(b, my_slot, 1, axis=0) for b in blocks]
            self.sent[stage] = (_exchange_start(blocks, f"exchange_{stage}_start"), own)
            return self.sent[stage][0][-1]

        def parts(self, stage, after):
            started, own = self.sent[stage]
            landed = _exchange_wait(started, [after], f"exchange_{stage}_wait")
            return {n: lax.dynamic_update_slice_in_dim(land, o, my_slot, axis=0)
                    for n, land, o in zip(stages[stage], landed, own)}

    hooks = Hooks()
    w = {}
    for n in SMALL:
        w[n] = given[n]
    w['a_q_norm'], w['a_k_norm'] = given['a_q_norm'][0], given['a_k_norm'][0]
    w['b_q_norm'], w['b_k_norm'], w['b_sinks'] = given['b_q_norm'][0], given['b_k_norm'][0], given['b_sinks'][0]

    loss_tile, grad_x, grads = _device_step(x[0], mem[0], positions[0], loss_target[0], w, hooks)
    out = {}
    after = grad_x
    for stage in ('ffn', 'mix', 'in'):
        for n, p in hooks.parts(stage, after).items():
            res = _adam_reduce(p, given[n][0], mom1[n][0], mom2[n][0], f"adam_{n}")
            out[n] = tuple(t[None] for t in res)
            after = res[0]

    small = {n: grads[n] for n in PACK}
    small['b_q_norm'], small['b_k_norm'] = grads['b_q_norm'].reshape(1, -1), grads['b_k_norm'].reshape(1, -1)
    small['b_sinks'] = grads['b_sinks'].reshape(1, -1)
    gsum = _all_sum(_pack_small(small, loss_tile, "pack_small"), "sum_small")
    ws = {n: _as2d(n, given[n]) for n in PACK}
    ms = {n: _as2d(n, mom1[n]) for n in PACK}
    vs = {n: _as2d(n, mom2[n]) for n in PACK}
    res = _adam_small(gsum, ws, ms, vs, "adam_small")
    loss = res[0].reshape(())
    for k, n in enumerate(PACK):
        out[n] = tuple(t.reshape(given[n].shape) for t in res[1 + 4 * k:5 + 4 * k])

    outs = [loss, grad_x[None]]
    for field in range(4):
        outs += [out[n][field] for n in WEIGHTS]
    return tuple(outs)
```

```python
import functools
import math

import jax
import jax.numpy as jnp
from jax import lax
from jax.experimental import pallas as pl
from jax.experimental.pallas import tpu as pltpu

F32 = jnp.float32
BF16 = jnp.bfloat16

N_DEV = 8
HEAD_DIM = 64
A_GROUPS = ((128, 1), (512, 4), (2048, 16))
B_WINDOW = 128
M_HEADS = 4
M_HEAD_DIM = 128
MEM_LEN = 256
D_FF = 2816
ROPE_THETA = 500000.0
ROPE_DIMS = 16
BLOCK = 128
EPS = 1e-6
LANES = 128
BAND_Q_BLOCKS = 16
BAND_UNITS = 2

ADAM_LR = 0.001
ADAM_B1 = 0.9
ADAM_B2 = 0.999
ADAM_EPS = 1e-08
ADAM_WD = 0.01
ADAM_STEP = 10

VMEM_LIMIT_BYTES = 56 * 1024 * 1024
GRAD_DTYPE = BF16
MESH = pl.DeviceIdType.MESH

WEIGHTS = ['attn_norm', 'w_in', 'a_q_norm', 'a_k_norm', 'b_q_norm', 'b_k_norm', 'b_sinks', 'mem_norm',
           'w_mem_kv', 'm_q_norm', 'm_k_norm', 'w_o_a', 'w_o_b', 'w_o_m', 'w_gate', 'b_gate', 'w_out',
           'ffn_norm', 'w_up', 'conv_w', 'conv_b', 'w_down']
BIG = {'w_in': 1, 'w_mem_kv': 0, 'w_o_a': 1, 'w_o_b': 1, 'w_o_m': 1, 'w_gate': 1, 'w_out': 0, 'w_up': 1,
       'conv_w': 1, 'w_down': 0}
SMALL = [n for n in WEIGHTS if n not in BIG]


def _cparams(n_grid):
    return pltpu.CompilerParams(dimension_semantics=("arbitrary",) * n_grid, vmem_limit_bytes=VMEM_LIMIT_BYTES)


def _seg_matrix(width):
    shift = width.bit_length() - 1
    r = lax.shift_right_logical(lax.broadcasted_iota(jnp.int32, (LANES, LANES), 0), shift)
    c = lax.shift_right_logical(lax.broadcasted_iota(jnp.int32, (LANES, LANES), 1), shift)
    return jnp.where(r == c, 1.0, 0.0).astype(BF16)


def _seg_sum(x, seg):
    hi = x.astype(BF16)
    r1 = x - hi.astype(F32)
    mid = r1.astype(BF16)
    lo = (r1 - mid.astype(F32)).astype(BF16)
    dot = functools.partial(jnp.dot, preferred_element_type=F32)
    return dot(hi, seg) + dot(mid, seg) + dot(lo, seg)


def _rope(y, c, s1, s2):
    return y * c + pltpu.roll(y, LANES - ROPE_DIMS // 2, 1) * s1 + pltpu.roll(y, ROPE_DIMS // 2, 1) * s2


def _unrope(dy, c, s1, s2):
    return dy * c + pltpu.roll(dy * s1, ROPE_DIMS // 2, 1) + pltpu.roll(dy * s2, LANES - ROPE_DIMS // 2, 1)


def _sigmoid(x):
    return 1.0 / (1.0 + jnp.exp(-x))


def _rms_fwd(x, gain, name):
    s_len, d = x.shape
    tm = 512

    def body(x_ref, g_ref, h_ref, ht_ref, r_ref):
        xv = x_ref[...]
        r = lax.rsqrt(jnp.mean(xv * xv, axis=-1, keepdims=True) + EPS)
        h = ((xv * r) * g_ref[...]).astype(BF16)
        h_ref[...] = h
        ht_ref[...] = h.T
        r_ref[...] = r

    return pl.pallas_call(
        body, name=name, grid=(s_len // tm,),
        in_specs=[pl.BlockSpec((tm, d), lambda i: (i, 0)), pl.BlockSpec((1, d), lambda i: (0, 0))],
        out_specs=(pl.BlockSpec((tm, d), lambda i: (i, 0)), pl.BlockSpec((d, tm), lambda i: (0, i)),
                   pl.BlockSpec((tm, 1), lambda i: (i, 0))),
        out_shape=(jax.ShapeDtypeStruct((s_len, d), BF16), jax.ShapeDtypeStruct((d, s_len), BF16),
                   jax.ShapeDtypeStruct((s_len, 1), F32)),
        compiler_params=_cparams(1),
    )(x, gain)


def _resident(shape, index_map):
    return pl.BlockSpec(shape, index_map, pipeline_mode=pl.Buffered(1))


def _mm_rows(pairs, name, nt=False, tm=512, bias=None, sigmoid=False, res=None, out_dtypes=(F32,), loss_target=None,
             rms_bwd=None):
    m = pairs[0][0].shape[0]
    n = pairs[0][1].shape[0] if nt else pairs[0][1].shape[1]
    n_pairs = len(pairs)
    has_bias, has_res, has_loss = bias is not None, res is not None, loss_target is not None
    has_rms = rms_bwd is not None
    dims = (((1,), (1,)), ((), ())) if nt else (((1,), (0,)), ((), ()))

    def body(*refs):
        acc = None
        for p in range(n_pairs):
            t = lax.dot_general(refs[2 * p][...].astype(BF16), refs[2 * p + 1][...], dims, preferred_element_type=F32)
            acc = t if acc is None else acc + t
        pos = 2 * n_pairs
        if has_bias:
            acc = acc + refs[pos][...]
            pos += 1
        if sigmoid:
            acc = _sigmoid(acc)
        if has_res:
            acc = refs[pos][...] + acc
            pos += 1
        if has_loss:
            dy_ref, dyb_ref, da_ref, l_ref = refs[pos + 1:]

            @pl.when(pl.program_id(0) == 0)
            def _():
                l_ref[...] = jnp.zeros_like(l_ref)
            err = acc - refs[pos][...]
            dy = err * (1.0 / n)
            dy_ref[...] = dy
            dyb_ref[...] = dy.astype(BF16)
            da_ref[...] = lax.dot_general(dy.astype(BF16), refs[1][...], (((1,), (1,)), ((), ())),
                                          preferred_element_type=F32).astype(BF16)
            part = 0.5 * jnp.sum(jnp.mean(err * err, axis=-1, keepdims=True), axis=0, keepdims=True)
            l_ref[...] += jnp.broadcast_to(part, l_ref.shape)
            return
        if has_rms:
            x_ref, r_ref, g_ref, add_ref = refs[pos:pos + 4]
            dg_ref = refs[-1]

            @pl.when(pl.program_id(0) == 0)
            def _():
                dg_ref[...] = jnp.zeros_like(dg_ref)
            rv = r_ref[...]
            xhat = x_ref[...] * rv
            dg_ref[...] += jnp.sum(acc * xhat, axis=0, keepdims=True)
            dxhat = acc * g_ref[...]
            acc = add_ref[...] + rv * (dxhat - xhat * jnp.mean(dxhat * xhat, axis=-1, keepdims=True))
            for o_ref in refs[pos + 4:-1]:
                o_ref[...] = acc.astype(o_ref.dtype)
            return
        for o_ref in refs[pos:]:
            o_ref[...] = acc.astype(o_ref.dtype)

    in_specs, args = [], []
    for a, w, blk in pairs:
        k = a.shape[1]
        in_specs.append(pl.BlockSpec((tm, k), lambda i: (i, 0)))
        if nt:
            in_specs.append(_resident((n, k), lambda i, blk=blk: (0, blk)))
        else:
            in_specs.append(_resident((k, n), lambda i, blk=blk: (blk, 0)))
        args += [a, w]
    if has_bias:
        in_specs.append(_resident((1, n), lambda i: (0, 0)))
        args.append(bias)
    if has_res:
        in_specs.append(pl.BlockSpec((tm, n), lambda i: (i, 0)))
        args.append(res)
    out = pl.BlockSpec((tm, n), lambda i: (i, 0))
    if has_loss:
        k0 = pairs[0][0].shape[1]
        return pl.pallas_call(
            body, name=name, grid=(m // tm,), in_specs=in_specs + [out],
            out_specs=(out, out, pl.BlockSpec((tm, k0), lambda i: (i, 0)), pl.BlockSpec((8, LANES), lambda i: (0, 0))),
            out_shape=(jax.ShapeDtypeStruct((m, n), F32), jax.ShapeDtypeStruct((m, n), BF16),
                       jax.ShapeDtypeStruct((m, k0), BF16), jax.ShapeDtypeStruct((8, LANES), F32)),
            compiler_params=_cparams(1),
        )(*args, loss_target)
    if has_rms:
        x, r, gain, add = rms_bwd
        vec = _resident((1, n), lambda i: (0, 0))
        return pl.pallas_call(
            body, name=name, grid=(m // tm,),
            in_specs=in_specs + [out, pl.BlockSpec((tm, 1), lambda i: (i, 0)), vec, out],
            out_specs=tuple([out] * len(out_dtypes) + [pl.BlockSpec((1, n), lambda i: (0, 0))]),
            out_shape=tuple([jax.ShapeDtypeStruct((m, n), dt) for dt in out_dtypes] + [jax.ShapeDtypeStruct((1, n), F32)]),
            compiler_params=_cparams(1),
        )(*args, x, r, gain, add)
    outs = pl.pallas_call(
        body, name=name, grid=(m // tm,), in_specs=in_specs, out_specs=tuple([out] * len(out_dtypes)),
        out_shape=tuple(jax.ShapeDtypeStruct((m, n), dt) for dt in out_dtypes), compiler_params=_cparams(1),
    )(*args)
    return outs[0] if len(out_dtypes) == 1 else outs


def _mm_rows_each(pairs, name, tm=512):
    m = pairs[0][0].shape[0]
    n_pairs = len(pairs)

    def body(*refs):
        for p in range(n_pairs):
            refs[2 * n_pairs + p][...] = lax.dot_general(refs[2 * p][...].astype(BF16), refs[2 * p + 1][...],
                                                         (((1,), (1,)), ((), ())), preferred_element_type=F32)

    in_specs, args = [], []
    for a, w in pairs:
        in_specs += [pl.BlockSpec((tm, a.shape[1]), lambda i: (i, 0)), _resident(w.shape, lambda i: (0, 0))]
        args += [a, w]
    return pl.pallas_call(
        body, name=name, grid=(m // tm,), in_specs=in_specs,
        out_specs=tuple(pl.BlockSpec((tm, w.shape[0]), lambda i: (i, 0)) for _, w in pairs),
        out_shape=tuple(jax.ShapeDtypeStruct((m, w.shape[0]), F32) for _, w in pairs), compiler_params=_cparams(1),
    )(*args)


def _mm_rows_cat(a, ws, name, tm=256):
    m, k = a.shape
    widths = [w.shape[1] for w in ws]
    n = sum(widths)

    def body(*refs):
        a_ref, o_ref = refs[0], refs[-1]
        av = a_ref[...]
        off = 0
        for p, width in enumerate(widths):
            o_ref[:, off:off + width] = jnp.dot(av, refs[1 + p][...], preferred_element_type=F32).astype(GRAD_DTYPE)
            off += width

    return pl.pallas_call(
        body, name=name, grid=(m // tm,),
        in_specs=[pl.BlockSpec((tm, k), lambda i: (i, 0))] + [_resident((k, wd), lambda i: (0, 0)) for wd in widths],
        out_specs=pl.BlockSpec((tm, n), lambda i: (i, 0)),
        out_shape=jax.ShapeDtypeStruct((m, n), GRAD_DTYPE), compiler_params=_cparams(1),
    )(a, *ws)


def _mm_cols(a, bs, name, tn=256):
    m, k = a.shape
    counts = [b.shape[1] // tn for b in bs]
    starts = [sum(counts[:p]) for p in range(len(bs))]

    def body(*refs):
        a_ref, o_ref = refs[0], refs[-1]
        j = pl.program_id(0)
        for p, b_ref in enumerate(refs[1:-1]):
            @pl.when((j >= starts[p]) & (j < starts[p] + counts[p]))
            def _():
                o_ref[...] = jnp.dot(a_ref[...], b_ref[...].astype(BF16), preferred_element_type=F32).astype(GRAD_DTYPE)

    b_specs = [pl.BlockSpec((k, tn), lambda j, s=s, c=c: (0, jnp.clip(j - s, 0, c - 1))) for s, c in zip(starts, counts)]
    return pl.pallas_call(
        body, name=name, grid=(sum(counts),),
        in_specs=[_resident((m, k), lambda j: (0, 0))] + b_specs,
        out_specs=pl.BlockSpec((m, tn), lambda j: (0, j)),
        out_shape=jax.ShapeDtypeStruct((m, sum(counts) * tn), GRAD_DTYPE), compiler_params=_cparams(1),
    )(a, *bs)


def _mm_tn_each(pairs, name, tile=256):
    n = pairs[0][1].shape[1]
    n_pairs = len(pairs)
    dims = (((0,), (0,)), ((), ()))

    def body(*refs):
        for p in range(n_pairs):
            refs[2 * n_pairs + p][...] = lax.dot_general(refs[2 * p][...].astype(BF16), refs[2 * p + 1][...].astype(BF16),
                                                         dims, preferred_element_type=F32).astype(GRAD_DTYPE)

    in_specs, args = [], []
    for a, b in pairs:
        in_specs += [_resident(a.shape, lambda j: (0, 0)), pl.BlockSpec((b.shape[0], tile), lambda j: (0, j))]
        args += [a, b]
    return pl.pallas_call(
        body, name=name, grid=(n // tile,), in_specs=in_specs,
        out_specs=tuple(pl.BlockSpec((a.shape[1], tile), lambda j: (0, j)) for a, _ in pairs),
        out_shape=tuple(jax.ShapeDtypeStruct((a.shape[1], n), GRAD_DTYPE) for a, _ in pairs),
        compiler_params=_cparams(1),
    )(*args)


def _mm_tn(a, b, name, tile=256):
    k, m = a.shape
    n = b.shape[1]
    dims = (((0,), (0,)), ((), ()))

    def body(a_ref, b_ref, o_ref):
        o_ref[...] = lax.dot_general(a_ref[...].astype(BF16), b_ref[...].astype(BF16), dims,
                                     preferred_element_type=F32).astype(GRAD_DTYPE)

    if n <= m:
        t = min(tile, m)
        grid, a_spec, b_spec = (m // t,), pl.BlockSpec((k, t), lambda i: (0, i)), _resident((k, n), lambda i: (0, 0))
        o_spec = pl.BlockSpec((t, n), lambda i: (i, 0))
    else:
        t = min(tile, n)
        grid, a_spec, b_spec = (n // t,), _resident((k, m), lambda i: (0, 0)), pl.BlockSpec((k, t), lambda i: (0, i))
        o_spec = pl.BlockSpec((m, t), lambda i: (0, i))
    return pl.pallas_call(
        body, name=name, grid=grid, in_specs=[a_spec, b_spec], out_specs=o_spec,
        out_shape=jax.ShapeDtypeStruct((m, n), GRAD_DTYPE), compiler_params=_cparams(1),
    )(a, b)


def _norm_rope(t, gain, c, s1, s2, seg):
    rs = lax.rsqrt(_seg_sum(t * t, seg) * (1.0 / HEAD_DIM) + EPS)
    return _rope((t * rs) * gain, c, s1, s2)


def _dup_half(y, half):
    lane = lax.broadcasted_iota(jnp.int32, y.shape, 1)
    rolled = pltpu.roll(y, HEAD_DIM, 1)
    keep = (lane < HEAD_DIM) if half == 0 else (lane >= HEAD_DIM)
    return jnp.where(keep, y, rolled)


def _qk_prep(proj, cb0, d, gqa, gq, gk, tabs, name):
    s_len = proj.shape[0]
    tm = 512
    rows = tm // d
    n_units = 4 if gqa else 2 * d
    n_q = 4 if gqa else 2
    n_in = 6

    def body(*refs):
        in_refs = refs[:n_in]
        gq_ref, gk_ref, c_ref, s1_ref, s2_ref, o_ref = refs[n_in:]
        seg = _seg_matrix(HEAD_DIM)

        def rows_of(ref, r):
            return ref[...] if d == 1 else ref[pl.ds(r, rows, stride=d), :]

        def put(unit_col, y):
            o_ref[:, unit_col * LANES:(unit_col + 1) * LANES] = y.astype(BF16)

        for r in range(d):
            c, s1, s2 = rows_of(c_ref, r), rows_of(s1_ref, r), rows_of(s2_ref, r)
            for b in range(n_in):
                t = rows_of(in_refs[b], r)
                if b < n_q:
                    put((b * d + r) if not gqa else b, _norm_rope(t, gq_ref[...], c, s1, s2, seg))
                elif not gqa:
                    sec, pair = (1, b - 2) if b < 4 else (2, b - 4)
                    y = _norm_rope(t, gk_ref[...], c, s1, s2, seg) if sec == 1 else t
                    put(sec * n_units + pair * d + r, y)
                else:
                    sec = 1 if b == 4 else 2
                    y = _norm_rope(t, gk_ref[...], c, s1, s2, seg) if sec == 1 else t
                    for u in range(n_units):
                        put(sec * n_units + u, _dup_half(y, u // 2))

    in_specs = [pl.BlockSpec((tm, LANES), lambda i, b=b: (i, cb0 + b)) for b in range(n_in)]
    vec = pl.BlockSpec((1, LANES), lambda i: (0, 0))
    tab = pl.BlockSpec((tm, LANES), lambda i: (i, 0))
    width = 3 * n_units * LANES
    return pl.pallas_call(
        body, name=name, grid=(s_len // tm,), in_specs=in_specs + [vec, vec, tab, tab, tab],
        out_specs=pl.BlockSpec((rows, width), lambda i: (i, 0)),
        out_shape=jax.ShapeDtypeStruct((s_len // d, width), BF16), compiler_params=_cparams(1),
    )(*([proj] * n_in), gq, gk, *tabs)


def _qk_prep_bwd(dqkv, proj, cb0, d, gqa, gq, gk, tabs, name):
    s_len = proj.shape[0]
    tm = 512
    rows = tm // d
    n_units = 4 if gqa else 2 * d
    n_q = 4 if gqa else 2
    n_in = 6

    def body(*refs):
        d_refs = refs[0:3]
        in_refs = refs[3:3 + n_in]
        gq_ref, gk_ref, c_ref, s1_ref, s2_ref, o_ref, dgq_ref, dgk_ref, stage = refs[3 + n_in:]
        seg = _seg_matrix(HEAD_DIM)

        @pl.when(pl.program_id(0) == 0)
        def _():
            dgq_ref[...] = jnp.zeros_like(dgq_ref)
            dgk_ref[...] = jnp.zeros_like(dgk_ref)

        def rows_of(ref, r):
            return ref[...] if d == 1 else ref[pl.ds(r, rows, stride=d), :]

        def unit(col):
            sec, u = divmod(col, n_units)
            return d_refs[sec][:, u * LANES:(u + 1) * LANES]

        def norm_bwd(dyr, t, gain, c, s1, s2, dg_ref):
            rs = lax.rsqrt(_seg_sum(t * t, seg) * (1.0 / HEAD_DIM) + EPS)
            that = t * rs
            dy = _unrope(dyr, c, s1, s2)
            dg_ref[...] += jnp.sum(dy * that, axis=0, keepdims=True)
            dthat = dy * gain
            return rs * (dthat - that * (_seg_sum(dthat * that, seg) * (1.0 / HEAD_DIM)))

        def fold(sec):
            tot = []
            for u in range(n_units):
                v = unit(sec * n_units + u)
                tot.append(v + pltpu.roll(v, HEAD_DIM, 1))
            lane = lax.broadcasted_iota(jnp.int32, tot[0].shape, 1)
            return jnp.where(lane < HEAD_DIM, tot[0] + tot[1], tot[2] + tot[3])

        for b in range(n_in):
            for r in range(d):
                c, s1, s2 = rows_of(c_ref, r), rows_of(s1_ref, r), rows_of(s2_ref, r)
                t = rows_of(in_refs[b], r)
                if b < n_q:
                    g = unit((b * d + r) if not gqa else b)
                    out = norm_bwd(g, t, gq_ref[...], c, s1, s2, dgq_ref)
                elif not gqa:
                    sec, pair = (1, b - 2) if b < 4 else (2, b - 4)
                    g = unit(sec * n_units + pair * d + r)
                    out = norm_bwd(g, t, gk_ref[...], c, s1, s2, dgk_ref) if sec == 1 else g
                else:
                    sec = 1 if b == 4 else 2
                    g = fold(sec)
                    out = norm_bwd(g, t, gk_ref[...], c, s1, s2, dgk_ref) if sec == 1 else g
                if d == 1:
                    o_ref[:, b * LANES:(b + 1) * LANES] = out.astype(BF16)
                else:
                    stage[pl.ds(r, rows, stride=d), :] = out
            if d != 1:
                o_ref[:, b * LANES:(b + 1) * LANES] = stage[...].astype(BF16)

    in_specs = [pl.BlockSpec((rows, n_units * LANES), lambda i: (i, 0))] * 3
    in_specs += [pl.BlockSpec((tm, LANES), lambda i, b=b: (i, cb0 + b)) for b in range(n_in)]
    vec = pl.BlockSpec((1, LANES), lambda i: (0, 0))
    tab = pl.BlockSpec((tm, LANES), lambda i: (i, 0))
    return pl.pallas_call(
        body, name=name, grid=(s_len // tm,), in_specs=in_specs + [vec, vec, tab, tab, tab],
        out_specs=(pl.BlockSpec((tm, n_in * LANES), lambda i: (i, 0)), vec, vec),
        out_shape=(jax.ShapeDtypeStruct((s_len, n_in * LANES), BF16), jax.ShapeDtypeStruct((1, LANES), F32),
                   jax.ShapeDtypeStruct((1, LANES), F32)),
        scratch_shapes=[pltpu.VMEM((tm, LANES), F32)], compiler_params=_cparams(1),
    )(*dqkv, *([proj] * n_in), gq, gk, *tabs)


def _head_masks(shape):
    lane = lax.broadcasted_iota(jnp.int32, shape, 1)
    return lane < HEAD_DIM, lane >= HEAD_DIM


def _band_fwd(qkv, n_units, max_dist, sinks, name):
    n_rows = qkv.shape[0]
    nb = n_rows // BLOCK
    scale = HEAD_DIM ** -0.5
    has_sink = sinks is not None
    assert not has_sink or max_dist < BLOCK

    qn, un = min(nb, BAND_Q_BLOCKS), BAND_UNITS
    ug = n_units // un

    def body(*refs):
        q_ref, kp_ref, km_ref, vp_ref, vm_ref = refs[:5]
        o_ref, lse_ref = refs[-2:]
        i = pl.program_id(1)
        qi = lax.broadcasted_iota(jnp.int32, (BLOCK, 2 * BLOCK), 0)
        kj = lax.broadcasted_iota(jnp.int32, (BLOCK, 2 * BLOCK), 1)
        dist = qi + BLOCK - kj
        band = (dist >= 0) & (dist <= max_dist)
        band_first = band & ((i > 0) | (kj >= BLOCK))
        m0, m1 = _head_masks((BLOCK, LANES))
        zero = jnp.zeros((BLOCK, LANES), BF16)
        for ub in range(un):
            cs = slice(ub * LANES, (ub + 1) * LANES)
            for qb in range(qn):
                rs = slice(qb * BLOCK, (qb + 1) * BLOCK)
                q = q_ref[rs, cs]
                if qb == 0:
                    kk = jnp.concatenate([kp_ref[:, cs], km_ref[0:BLOCK, cs]], axis=0)
                    vv = jnp.concatenate([vp_ref[:, cs], vm_ref[0:BLOCK, cs]], axis=0)
                    valid = band_first
                else:
                    kk = km_ref[(qb - 1) * BLOCK:(qb + 1) * BLOCK, cs]
                    vv = vm_ref[(qb - 1) * BLOCK:(qb + 1) * BLOCK, cs]
                    valid = band
                outs, lses = [], []
                for e, hm in enumerate((m0, m1)):
                    qe = jnp.where(hm, q, zero)
                    s = lax.dot_general(qe, kk, (((1,), (1,)), ((), ())), preferred_element_type=F32) * scale
                    s = jnp.where(valid, s, -jnp.inf)
                    if has_sink:
                        s = jnp.where(kj == 0, refs[5][ub][:, e * HEAD_DIM:e * HEAD_DIM + 1], s)
                    mx = jnp.max(s, axis=-1, keepdims=True)
                    p = jnp.exp(s - mx)
                    den = jnp.sum(p, axis=-1, keepdims=True)
                    pn = p * (1.0 / den)
                    if has_sink:
                        pn = jnp.where(kj == 0, 0.0, pn)
                    pn = pn.astype(BF16)
                    outs.append(jnp.dot(pn, vv, preferred_element_type=F32))
                    lses.append(mx + jnp.log(den))
                o_ref[rs, cs] = jnp.where(m0, outs[0], outs[1])
                lse_ref[rs, cs] = jnp.where(m0, jnp.broadcast_to(lses[0], (BLOCK, LANES)),
                                            jnp.broadcast_to(lses[1], (BLOCK, LANES)))

    def main(sec):
        return pl.BlockSpec((qn * BLOCK, un * LANES), lambda u, i: (i, sec * ug + u))

    def prev(sec):
        return pl.BlockSpec((BLOCK, un * LANES), lambda u, i: (jnp.maximum(i * qn - 1, 0), sec * ug + u))

    in_specs = [main(0), prev(1), main(1), prev(2), main(2)]
    args = [qkv] * 5
    if has_sink:
        in_specs.append(pl.BlockSpec((un, 1, LANES), lambda u, i: (u, 0, 0)))
        args.append(sinks)
    return pl.pallas_call(
        body, name=name, grid=(ug, nb // qn), in_specs=in_specs, out_specs=(main(0), main(0)),
        out_shape=(jax.ShapeDtypeStruct((n_rows, n_units * LANES), F32),) * 2, compiler_params=_cparams(2),
    )(*args)


def _band_bwd(qkv, do, lse, delta, n_units, max_dist, name):
    n_rows = qkv.shape[0]
    nb = n_rows // BLOCK
    scale = HEAD_DIM ** -0.5

    qn, un = min(nb, BAND_Q_BLOCKS), BAND_UNITS
    ug = n_units // un
    steps = nb // qn
    nt_dims = (((1,), (1,)), ((), ()))
    tn_dims = (((0,), (0,)), ((), ()))

    def body(qm_ref, qx_ref, kp_ref, km_ref, vp_ref, vm_ref, dom_ref, dox_ref, lm_ref, lx_ref, dm_ref, dx_ref,
             dq_ref, dk_ref, dv_ref):
        i = pl.program_id(1)
        m0, m1 = _head_masks((BLOCK, LANES))
        zero = jnp.zeros((BLOCK, LANES), BF16)
        qi = lax.broadcasted_iota(jnp.int32, (BLOCK, 2 * BLOCK), 0)
        kj = lax.broadcasted_iota(jnp.int32, (BLOCK, 2 * BLOCK), 1)
        dist = qi + BLOCK - kj
        band = (dist >= 0) & (dist <= max_dist)
        band_first = band & ((i > 0) | (kj >= BLOCK))
        qr = lax.broadcasted_iota(jnp.int32, (BLOCK, BLOCK), 0)
        kc = lax.broadcasted_iota(jnp.int32, (BLOCK, BLOCK), 1)
        dist_x = qr + BLOCK - kc
        band_next = (dist_x >= 0) & (dist_x <= max_dist) & (i < steps - 1)

        def pair(q, dob, lse_b, del_b, kk, vv, valid):
            dqs, dk, dv = [], None, None
            for e, hm in enumerate((m0, m1)):
                col = slice(e * HEAD_DIM, e * HEAD_DIM + 1)
                qe = jnp.where(hm, q, zero)
                doe = jnp.where(hm, dob, zero)
                s = lax.dot_general(qe, kk, nt_dims, preferred_element_type=F32) * scale
                p = jnp.where(valid, jnp.exp(s - lse_b[:, col]), 0.0)
                dp = lax.dot_general(doe, vv, nt_dims, preferred_element_type=F32)
                ds = (p * (dp - del_b[:, col]) * scale).astype(BF16)
                dqs.append(jnp.dot(ds, kk, preferred_element_type=F32))
                dk_e = lax.dot_general(ds, qe, tn_dims, preferred_element_type=F32)
                dv_e = lax.dot_general(p.astype(BF16), doe, tn_dims, preferred_element_type=F32)
                dk = dk_e if dk is None else dk + dk_e
                dv = dv_e if dv is None else dv + dv_e
            return jnp.where(m0, dqs[0], dqs[1]), dk, dv

        for ub in range(un):
            cs = slice(ub * LANES, (ub + 1) * LANES)
            dk_acc, dv_acc = [None] * qn, [None] * qn

            def add(acc, kb, part):
                acc[kb] = part if acc[kb] is None else acc[kb] + part

            for qb in range(qn):
                rs = slice(qb * BLOCK, (qb + 1) * BLOCK)
                if qb == 0:
                    kk = jnp.concatenate([kp_ref[:, cs], km_ref[0:BLOCK, cs]], axis=0)
                    vv = jnp.concatenate([vp_ref[:, cs], vm_ref[0:BLOCK, cs]], axis=0)
                    valid = band_first
                else:
                    kk = km_ref[(qb - 1) * BLOCK:(qb + 1) * BLOCK, cs]
                    vv = vm_ref[(qb - 1) * BLOCK:(qb + 1) * BLOCK, cs]
                    valid = band
                dq, dk, dv = pair(qm_ref[rs, cs], dom_ref[rs, cs], lm_ref[rs, cs], dm_ref[rs, cs], kk, vv, valid)
                dq_ref[rs, cs] = dq
                if qb > 0:
                    add(dk_acc, qb - 1, dk[0:BLOCK])
                    add(dv_acc, qb - 1, dv[0:BLOCK])
                add(dk_acc, qb, dk[BLOCK:2 * BLOCK])
                add(dv_acc, qb, dv[BLOCK:2 * BLOCK])
            last = slice((qn - 1) * BLOCK, qn * BLOCK)
            _, dk, dv = pair(qx_ref[:, cs], dox_ref[:, cs], lx_ref[:, cs], dx_ref[:, cs], km_ref[last, cs], vm_ref[last, cs],
                             band_next)
            add(dk_acc, qn - 1, dk)
            add(dv_acc, qn - 1, dv)
            for kb in range(qn):
                dk_ref[kb * BLOCK:(kb + 1) * BLOCK, cs] = dk_acc[kb]
                dv_ref[kb * BLOCK:(kb + 1) * BLOCK, cs] = dv_acc[kb]

    def main(sec):
        return pl.BlockSpec((qn * BLOCK, un * LANES), lambda u, i: (i, sec * ug + u))

    def prev(sec):
        return pl.BlockSpec((BLOCK, un * LANES), lambda u, i: (jnp.maximum(i * qn - 1, 0), sec * ug + u))

    def nxt(sec):
        return pl.BlockSpec((BLOCK, un * LANES), lambda u, i: (jnp.minimum((i + 1) * qn, nb - 1), sec * ug + u))

    in_specs = [main(0), nxt(0), prev(1), main(1), prev(2), main(2),
                main(0), nxt(0), main(0), nxt(0), main(0), nxt(0)]
    args = [qkv] * 6 + [do, do, lse, lse, delta, delta]
    shp = jax.ShapeDtypeStruct((n_rows, n_units * LANES), F32)
    return pl.pallas_call(
        body, name=name, grid=(ug, steps), in_specs=in_specs, out_specs=(main(0), main(0), main(0)),
        out_shape=(shp, shp, shp), compiler_params=_cparams(2),
    )(*args)


def _merge_groups(os_, lses, dils, name):
    s_len = os_[0].shape[0] * dils[0]
    tm = 512

    def body(*refs):
        o_refs, l_refs = refs[0:3], refs[3:6]
        o_ref, lse_ref = refs[6:8]
        so, sl = refs[8:11], refs[11:14]
        for pair in range(2):
            for g, d in enumerate(dils):
                rows = tm // d
                for r in range(d):
                    col = slice((pair * d + r) * LANES, (pair * d + r + 1) * LANES)
                    if d == 1:
                        so[g][...] = o_refs[g][:, col]
                        sl[g][...] = l_refs[g][:, col]
                    else:
                        so[g][pl.ds(r, rows, stride=d), :] = o_refs[g][:, col]
                        sl[g][pl.ds(r, rows, stride=d), :] = l_refs[g][:, col]
            l0, l1, l2 = sl[0][...], sl[1][...], sl[2][...]
            mx = jnp.maximum(jnp.maximum(l0, l1), l2)
            e0, e1, e2 = jnp.exp(l0 - mx), jnp.exp(l1 - mx), jnp.exp(l2 - mx)
            den = e0 + e1 + e2
            inv = 1.0 / den
            o_ref[:, pair * LANES:(pair + 1) * LANES] = (so[0][...] * (e0 * inv) + so[1][...] * (e1 * inv)
                                                         + so[2][...] * (e2 * inv))
            lse_ref[:, pair * LANES:(pair + 1) * LANES] = mx + jnp.log(den)

    in_specs = [pl.BlockSpec((tm // d, 2 * d * LANES), lambda i: (i, 0)) for d in dils] * 2
    out = pl.BlockSpec((tm, 2 * LANES), lambda i: (i, 0))
    shp = jax.ShapeDtypeStruct((s_len, 2 * LANES), F32)
    return pl.pallas_call(
        body, name=name, grid=(s_len // tm,), in_specs=in_specs, out_specs=(out, out), out_shape=(shp, shp),
        scratch_shapes=[pltpu.VMEM((tm, LANES), F32)] * 6, compiler_params=_cparams(1),
    )(*os_, *lses)


def _bwd_prep(do, o, lse, dils, sinks, name):
    s_len, width = do.shape
    n_pairs = width // LANES
    tm = 512
    has_sink = sinks is not None
    n_g = len(dils)

    def body(*refs):
        do_ref, o_ref, lse_ref = refs[:3]
        pos = 3
        if has_sink:
            sink_ref = refs[pos]
            pos += 1
        outs = refs[pos:pos + 3 * n_g]
        pos += 3 * n_g
        if has_sink:
            dsink_ref = refs[pos]
            pos += 1
        s_do, s_l, s_d = refs[pos:pos + 3]
        seg = _seg_matrix(HEAD_DIM)

        if has_sink:
            @pl.when(pl.program_id(0) == 0)
            def _():
                dsink_ref[...] = jnp.zeros_like(dsink_ref)

        for pair in range(n_pairs):
            col = slice(pair * LANES, (pair + 1) * LANES)
            dov = do_ref[:, col]
            lv = lse_ref[:, col]
            delta = _seg_sum(dov * o_ref[:, col], seg)
            if has_sink:
                dsink_ref[pair] += -jnp.sum(jnp.exp(sink_ref[pair] - lv) * delta, axis=0, keepdims=True)
            s_do[...] = dov
            s_l[...] = lv
            s_d[...] = delta
            for g, d in enumerate(dils):
                rows = tm // d
                for r in range(d):
                    oc = slice((pair * d + r) * LANES, (pair * d + r + 1) * LANES)
                    if d == 1:
                        a, b, c = s_do[...], s_l[...], s_d[...]
                    else:
                        a = s_do[pl.ds(r, rows, stride=d), :]
                        b = s_l[pl.ds(r, rows, stride=d), :]
                        c = s_d[pl.ds(r, rows, stride=d), :]
                    outs[3 * g][:, oc] = a.astype(BF16)
                    outs[3 * g + 1][:, oc] = b
                    outs[3 * g + 2][:, oc] = c

    row = pl.BlockSpec((tm, width), lambda i: (i, 0))
    in_specs = [row, row, row]
    args = [do, o, lse]
    if has_sink:
        in_specs.append(pl.BlockSpec((n_pairs, 1, LANES), lambda i: (0, 0, 0)))
        args.append(sinks)
    out_specs, out_shape = [], []
    for d in dils:
        for dt in (BF16, F32, F32):
            out_specs.append(pl.BlockSpec((tm // d, n_pairs * d * LANES), lambda i: (i, 0)))
            out_shape.append(jax.ShapeDtypeStruct((s_len // d, n_pairs * d * LANES), dt))
    if has_sink:
        out_specs.append(pl.BlockSpec((n_pairs, 1, LANES), lambda i: (0, 0, 0)))
        out_shape.append(jax.ShapeDtypeStruct((n_pairs, 1, LANES), F32))
    return pl.pallas_call(
        body, name=name, grid=(s_len // tm,), in_specs=in_specs, out_specs=tuple(out_specs),
        out_shape=tuple(out_shape), scratch_shapes=[pltpu.VMEM((tm, LANES), F32)] * 3, compiler_params=_cparams(1),
    )(*args)


def _mem_kv(mem, mem_gain, w_kv, k_gain, name):
    m_len = mem.shape[0]
    kw = M_HEADS * M_HEAD_DIM

    def body(mem_ref, mg_ref, w_ref, kg_ref, k_ref, v_ref):
        mv = mem_ref[...]
        r = lax.rsqrt(jnp.mean(mv * mv, axis=-1, keepdims=True) + EPS)
        mn = ((mv * r) * mg_ref[...]).astype(BF16)
        kv = jnp.dot(mn, w_ref[...], preferred_element_type=F32)
        for h in range(M_HEADS):
            col = slice(h * M_HEAD_DIM, (h + 1) * M_HEAD_DIM)
            t = kv[:, col]
            rk = lax.rsqrt(jnp.mean(t * t, axis=-1, keepdims=True) + EPS)
            k_ref[:, col] = ((t * rk) * kg_ref[...]).astype(BF16)
        v_ref[...] = kv[:, kw:].astype(BF16)

    shp = jax.ShapeDtypeStruct((m_len, kw), BF16)
    return pl.pallas_call(body, name=name, out_shape=(shp, shp),
                          compiler_params=pltpu.CompilerParams(vmem_limit_bytes=VMEM_LIMIT_BYTES))(mem, mem_gain, w_kv, k_gain)


def _mem_kv_bwd(mem, mem_gain, w_kv, k_gain, dk, dv, name):
    m_len, d = mem.shape
    kw = M_HEADS * M_HEAD_DIM

    def body(mem_ref, mg_ref, w_ref, kg_ref, dk_ref, dv_ref, dw_ref, dmg_ref, dkg_ref, dkv_ref):
        mv = mem_ref[...]
        r = lax.rsqrt(jnp.mean(mv * mv, axis=-1, keepdims=True) + EPS)
        mhat = mv * r
        mn = (mhat * mg_ref[...]).astype(BF16)
        kv = jnp.dot(mn, w_ref[...], preferred_element_type=F32)
        dkg = jnp.zeros((1, M_HEAD_DIM), F32)
        for h in range(M_HEADS):
            col = slice(h * M_HEAD_DIM, (h + 1) * M_HEAD_DIM)
            t = kv[:, col]
            rk = lax.rsqrt(jnp.mean(t * t, axis=-1, keepdims=True) + EPS)
            that = t * rk
            dy = dk_ref[:, col]
            dkg = dkg + jnp.sum(dy * that, axis=0, keepdims=True)
            dthat = dy * kg_ref[...]
            dkv_ref[:, col] = (rk * (dthat - that * jnp.mean(dthat * that, axis=-1, keepdims=True))).astype(BF16)
        dkv_ref[:, kw:] = dv_ref[...].astype(BF16)
        dkg_ref[...] = dkg
        dkv = dkv_ref[...]
        dw_ref[...] = lax.dot_general(mn, dkv, (((0,), (0,)), ((), ())), preferred_element_type=F32).astype(GRAD_DTYPE)
        dmn = lax.dot_general(dkv, w_ref[...], (((1,), (1,)), ((), ())), preferred_element_type=F32)
        dmg_ref[...] = jnp.sum(dmn * mhat, axis=0, keepdims=True)

    return pl.pallas_call(
        body, name=name,
        out_shape=(jax.ShapeDtypeStruct((d, 2 * kw), GRAD_DTYPE), jax.ShapeDtypeStruct((1, d), F32),
                   jax.ShapeDtypeStruct((1, M_HEAD_DIM), F32)),
        scratch_shapes=[pltpu.VMEM((m_len, 2 * kw), BF16)],
        compiler_params=pltpu.CompilerParams(vmem_limit_bytes=VMEM_LIMIT_BYTES),
    )(mem, mem_gain, w_kv, k_gain, dk, dv)


def _mem_attn_fwd(proj, cidx, mk, mv, q_gain, name):
    s_len = proj.shape[0]
    kw = M_HEADS * M_HEAD_DIM
    tm = 512
    scale = M_HEAD_DIM ** -0.5

    def body(q_ref, k_ref, v_ref, g_ref, o_ref):
        for h in range(M_HEADS):
            col = slice(h * M_HEAD_DIM, (h + 1) * M_HEAD_DIM)
            t = q_ref[:, col]
            rs = lax.rsqrt(jnp.mean(t * t, axis=-1, keepdims=True) + EPS)
            qn = ((t * rs) * g_ref[...]).astype(BF16)
            s = lax.dot_general(qn, k_ref[:, col], (((1,), (1,)), ((), ())), preferred_element_type=F32) * scale
            mx = jnp.max(s, axis=-1, keepdims=True)
            p = jnp.exp(s - mx)
            pn = (p * (1.0 / jnp.sum(p, axis=-1, keepdims=True))).astype(BF16)
            o_ref[:, col] = jnp.dot(pn, v_ref[:, col], preferred_element_type=F32).astype(BF16)

    whole = pl.BlockSpec((MEM_LEN, kw), lambda i: (0, 0))
    return pl.pallas_call(
        body, name=name, grid=(s_len // tm,),
        in_specs=[pl.BlockSpec((tm, kw), lambda i: (i, cidx)), whole, whole, pl.BlockSpec((1, M_HEAD_DIM), lambda i: (0, 0))],
        out_specs=pl.BlockSpec((tm, kw), lambda i: (i, 0)),
        out_shape=jax.ShapeDtypeStruct((s_len, kw), BF16), compiler_params=_cparams(1),
    )(proj, mk, mv, q_gain)


def _mem_attn_bwd(proj, cidx, mk, mv, q_gain, do, name):
    s_len = proj.shape[0]
    kw = M_HEADS * M_HEAD_DIM
    tm = 512
    scale = M_HEAD_DIM ** -0.5

    def body(q_ref, k_ref, v_ref, g_ref, do_ref, dq_ref, dk_ref, dv_ref, dg_ref):
        @pl.when(pl.program_id(0) == 0)
        def _():
            dk_ref[...] = jnp.zeros_like(dk_ref)
            dv_ref[...] = jnp.zeros_like(dv_ref)
            dg_ref[...] = jnp.zeros_like(dg_ref)

        for h in range(M_HEADS):
            col = slice(h * M_HEAD_DIM, (h + 1) * M_HEAD_DIM)
            t = q_ref[:, col]
            rs = lax.rsqrt(jnp.mean(t * t, axis=-1, keepdims=True) + EPS)
            that = t * rs
            qn = (that * g_ref[...]).astype(BF16)
            kh, vh = k_ref[:, col], v_ref[:, col]
            dob = do_ref[:, col].astype(BF16)
            s = lax.dot_general(qn, kh, (((1,), (1,)), ((), ())), preferred_element_type=F32) * scale
            mx = jnp.max(s, axis=-1, keepdims=True)
            p = jnp.exp(s - mx)
            p = p * (1.0 / jnp.sum(p, axis=-1, keepdims=True))
            dp = lax.dot_general(dob, vh, (((1,), (1,)), ((), ())), preferred_element_type=F32)
            ds = (p * (dp - jnp.sum(p * dp, axis=-1, keepdims=True)) * scale).astype(BF16)
            dqn = jnp.dot(ds, kh, preferred_element_type=F32)
            dk_ref[:, col] += lax.dot_general(ds, qn, (((0,), (0,)), ((), ())), preferred_element_type=F32)
            dv_ref[:, col] += lax.dot_general(p.astype(BF16), dob, (((0,), (0,)), ((), ())), preferred_element_type=F32)
            dg_ref[...] += jnp.sum(dqn * that, axis=0, keepdims=True)
            dthat = dqn * g_ref[...]
            dq_ref[:, col] = (rs * (dthat - that * jnp.mean(dthat * that, axis=-1, keepdims=True))).astype(BF16)

    whole = pl.BlockSpec((MEM_LEN, kw), lambda i: (0, 0))
    vec = pl.BlockSpec((1, M_HEAD_DIM), lambda i: (0, 0))
    row = pl.BlockSpec((tm, kw), lambda i: (i, 0))
    return pl.pallas_call(
        body, name=name, grid=(s_len // tm,),
        in_specs=[pl.BlockSpec((tm, kw), lambda i: (i, cidx)), whole, whole, vec, row],
        out_specs=(row, whole, whole, vec),
        out_shape=(jax.ShapeDtypeStruct((s_len, kw), BF16), jax.ShapeDtypeStruct((MEM_LEN, kw), F32),
                   jax.ShapeDtypeStruct((MEM_LEN, kw), F32), jax.ShapeDtypeStruct((1, M_HEAD_DIM), F32)),
        compiler_params=_cparams(1),
    )(proj, mk, mv, q_gain, do)


def _project_merge(outs, w_outs, gates, w_out, x, name):
    s_len = gates.shape[0]
    d = w_outs[0].shape[1]
    tm = 512

    def body(oa_ref, ob_ref, om_ref, wa_ref, wb_ref, wm_ref, g_ref, wo_ref, x_ref,
             pa_ref, pb_ref, pm_ref, merged_ref, x1_ref):
        merged = None
        for k, (o_ref, w_ref, p_ref) in enumerate(((oa_ref, wa_ref, pa_ref), (ob_ref, wb_ref, pb_ref), (om_ref, wm_ref, pm_ref))):
            p = jnp.dot(o_ref[...].astype(BF16), w_ref[...], preferred_element_type=F32).astype(BF16)
            p_ref[...] = p
            t = g_ref[:, k * d:(k + 1) * d].astype(F32) * p.astype(F32)
            merged = t if merged is None else merged + t
        merged = merged.astype(BF16)
        merged_ref[...] = merged
        x1_ref[...] = x_ref[...] + jnp.dot(merged, wo_ref[...], preferred_element_type=F32)

    row = pl.BlockSpec((tm, d), lambda i: (i, 0))
    shp = jax.ShapeDtypeStruct((s_len, d), BF16)
    in_specs = [pl.BlockSpec((tm, o.shape[1]), lambda i: (i, 0)) for o in outs]
    in_specs += [_resident(w.shape, lambda i: (0, 0)) for w in w_outs]
    in_specs += [pl.BlockSpec((tm, 3 * d), lambda i: (i, 0)), _resident(w_out.shape, lambda i: (0, 0)), row]
    return pl.pallas_call(
        body, name=name, grid=(s_len // tm,), in_specs=in_specs, out_specs=(row, row, row, row, row),
        out_shape=(shp, shp, shp, shp, jax.ShapeDtypeStruct((s_len, d), F32)), compiler_params=_cparams(1),
    )(*outs, *w_outs, gates, w_out, x)


def _project_merge_bwd(dx1, w_out, gates, pa, pb, pm, name):
    s_len, d = pa.shape
    tm = 512

    def body(dx_ref, w_ref, g_ref, a_ref, b_ref, m_ref, da_ref, db_ref, dmm_ref, dg_ref, dbg_ref):
        @pl.when(pl.program_id(0) == 0)
        def _():
            dbg_ref[...] = jnp.zeros_like(dbg_ref)
        dm = lax.dot_general(dx_ref[...], w_ref[...], (((1,), (1,)), ((), ())), preferred_element_type=F32)
        for k, (p_ref, dp_ref) in enumerate(((a_ref, da_ref), (b_ref, db_ref), (m_ref, dmm_ref))):
            col = slice(k * d, (k + 1) * d)
            g = g_ref[:, col].astype(F32)
            dp_ref[...] = (dm * g).astype(BF16)
            dpre = (dm * p_ref[...].astype(F32)) * (g * (1.0 - g))
            dbg_ref[:, col] += jnp.sum(dpre, axis=0, keepdims=True)
            dg_ref[:, col] = dpre.astype(BF16)

    row = pl.BlockSpec((tm, d), lambda i: (i, 0))
    wide = pl.BlockSpec((tm, 3 * d), lambda i: (i, 0))
    shp = jax.ShapeDtypeStruct((s_len, d), BF16)
    return pl.pallas_call(
        body, name=name, grid=(s_len // tm,), in_specs=[row, _resident(w_out.shape, lambda i: (0, 0)), wide, row, row, row],
        out_specs=(row, row, row, wide, pl.BlockSpec((1, 3 * d), lambda i: (0, 0))),
        out_shape=(shp, shp, shp, jax.ShapeDtypeStruct((s_len, 3 * d), BF16), jax.ShapeDtypeStruct((1, 3 * d), F32)),
        compiler_params=_cparams(1),
    )(dx1, w_out, gates, pa, pb, pm)


CONV_CHUNK = 256


def _pick_row(tile, j):
    row = lax.broadcasted_iota(jnp.int32, tile.shape, 0)
    return jnp.sum(jnp.where(row == j, tile, jnp.zeros_like(tile)), axis=0, keepdims=True)


def _rows_before(ref, start, k):
    cur = ref[pl.ds(start, CONV_CHUNK), :].astype(F32)
    prev = ref[pl.ds(pl.multiple_of(jnp.maximum(start - 16, 0), 16), 16), :].astype(F32)
    prev = jnp.where(start > 0, prev, jnp.zeros_like(prev))
    rolled = pltpu.roll(cur, k, 0)
    row = lax.broadcasted_iota(jnp.int32, cur.shape, 0)
    for j in range(k):
        rolled = jnp.where(row == j, _pick_row(prev, 16 - k + j), rolled)
    return rolled


def _rows_after(ref, start, k):
    cur = ref[pl.ds(start, CONV_CHUNK), :]
    nxt = ref[pl.ds(pl.multiple_of(start + CONV_CHUNK, 8), 8), :]
    rolled = pltpu.roll(cur, CONV_CHUNK - k, 0)
    row = lax.broadcasted_iota(jnp.int32, cur.shape, 0)
    for j in range(k):
        rolled = jnp.where(row == CONV_CHUNK - k + j, _pick_row(nxt, j), rolled)
    return rolled


def _conv_pre(u_ref, w_ref, b_ref, start):
    u2 = _rows_before(u_ref, start, 2)
    u1 = _rows_before(u_ref, start, 1)
    u0 = u_ref[pl.ds(start, CONV_CHUNK), :].astype(F32)
    c = ((b_ref[...] + w_ref[0:1, :] * u2) + w_ref[1:2, :] * u1) + w_ref[2:3, :] * u0
    return c, (u2, u1, u0)


def _norm_up_conv_glu(x, gain, w_up, conv_w, conv_b, name):
    s_len, d = x.shape
    tm, tn = 512, 2 * LANES
    nblk = D_FF // tn

    def body(x_ref, g_ref, w_ref, cw_ref, cb_ref, ht_ref, r_ref, u_ref, act_ref, halo):
        @pl.when(pl.program_id(0) == 0)
        def _():
            halo[...] = jnp.zeros_like(halo)
        xv = x_ref[...]
        r = lax.rsqrt(jnp.mean(xv * xv, axis=-1, keepdims=True) + EPS)
        hv = ((xv * r) * g_ref[...]).astype(BF16)
        ht_ref[...] = hv.T
        r_ref[...] = r
        row = lax.broadcasted_iota(jnp.int32, (tm, tn), 0)
        for j in range(nblk):
            conv = []
            for half in range(2):
                cb = half * nblk + j
                cols = slice(cb * tn, (cb + 1) * tn)
                ub = jnp.dot(hv, w_ref[:, cols], preferred_element_type=F32).astype(BF16)
                u_ref[:, cols] = ub
                u0 = ub.astype(F32)
                prev = halo[cb]
                u1 = jnp.where(row == 0, _pick_row(prev, 7), pltpu.roll(u0, 1, 0))
                u2 = pltpu.roll(u0, 2, 0)
                u2 = jnp.where(row == 0, _pick_row(prev, 6), jnp.where(row == 1, _pick_row(prev, 7), u2))
                halo[cb] = u0[tm - 8:tm, :]
                conv.append(((cb_ref[:, cols] + cw_ref[0:1, cols] * u2) + cw_ref[1:2, cols] * u1)
                            + cw_ref[2:3, cols] * u0)
            act_ref[:, j * tn:(j + 1) * tn] = ((conv[0] * _sigmoid(conv[0])) * conv[1]).astype(BF16)

    return pl.pallas_call(
        body, name=name, grid=(s_len // tm,),
        in_specs=[pl.BlockSpec((tm, d), lambda i: (i, 0)), _resident((1, d), lambda i: (0, 0)),
                  _resident((d, 2 * D_FF), lambda i: (0, 0)),
                  _resident((3, 2 * D_FF), lambda i: (0, 0)), _resident((1, 2 * D_FF), lambda i: (0, 0))],
        out_specs=(pl.BlockSpec((d, tm), lambda i: (0, i)), pl.BlockSpec((tm, 1), lambda i: (i, 0)),
                   pl.BlockSpec((tm, 2 * D_FF), lambda i: (i, 0)), pl.BlockSpec((tm, D_FF), lambda i: (i, 0))),
        out_shape=(jax.ShapeDtypeStruct((d, s_len), BF16), jax.ShapeDtypeStruct((s_len, 1), F32),
                   jax.ShapeDtypeStruct((s_len, 2 * D_FF), BF16), jax.ShapeDtypeStruct((s_len, D_FF), BF16)),
        scratch_shapes=[pltpu.VMEM((2 * nblk, 8, tn), F32)], compiler_params=_cparams(1),
    )(x, gain, w_up, conv_w, conv_b)


def _conv_glu_bwd(dact, u, conv_w, conv_b, name):
    s_len = u.shape[0]
    nblk = D_FF // LANES
    n_chunks = s_len // CONV_CHUNK

    def body(da_ref, ua_ref, ug_ref, wa_ref, wg_ref, ba_ref, bg_ref,
             dua_ref, dug_ref, dwa_ref, dwg_ref, dba_ref, dbg_ref, sa, sg):
        sa[pl.ds(s_len, 8), :] = jnp.zeros((8, LANES), F32)
        sg[pl.ds(s_len, 8), :] = jnp.zeros((8, LANES), F32)
        zero = jnp.zeros((1, LANES), F32)

        def chunk1(ci, carry):
            start = pl.multiple_of(ci * CONV_CHUNK, CONV_CHUNK)
            ca, ua = _conv_pre(ua_ref, wa_ref, ba_ref, start)
            cg, ug = _conv_pre(ug_ref, wg_ref, bg_ref, start)
            dact_v = da_ref[pl.ds(start, CONV_CHUNK), :].astype(F32)
            sig = _sigmoid(ca)
            dcg = dact_v * (ca * sig)
            dca = (dact_v * cg) * (sig * (1.0 + ca * (1.0 - sig)))
            sa[pl.ds(start, CONV_CHUNK), :] = dca
            sg[pl.ds(start, CONV_CHUNK), :] = dcg
            out = [carry[0] + jnp.sum(dca, axis=0, keepdims=True), carry[1] + jnp.sum(dcg, axis=0, keepdims=True)]
            for j in range(3):
                out.append(carry[2 + j] + jnp.sum(dca * ua[j], axis=0, keepdims=True))
            for j in range(3):
                out.append(carry[5 + j] + jnp.sum(dcg * ug[j], axis=0, keepdims=True))
            return tuple(out)

        acc = lax.fori_loop(0, n_chunks, chunk1, (zero,) * 8)
        dba_ref[...] = acc[0]
        dbg_ref[...] = acc[1]
        for j in range(3):
            dwa_ref[j:j + 1, :] = acc[2 + j]
            dwg_ref[j:j + 1, :] = acc[5 + j]

        def chunk2(ci, carry):
            start = pl.multiple_of(ci * CONV_CHUNK, CONV_CHUNK)
            for s_ref, w_ref, o_ref in ((sa, wa_ref, dua_ref), (sg, wg_ref, dug_ref)):
                d0 = s_ref[pl.ds(start, CONV_CHUNK), :]
                d1 = _rows_after(s_ref, start, 1)
                d2 = _rows_after(s_ref, start, 2)
                o_ref[pl.ds(start, CONV_CHUNK), :] = (w_ref[2:3, :] * d0 + w_ref[1:2, :] * d1
                                                      + w_ref[0:1, :] * d2).astype(BF16)
            return carry
        lax.fori_loop(0, n_chunks, chunk2, 0)

    def col(rows, off):
        return pl.BlockSpec((rows, LANES), lambda j: (0, off + j))

    big = jax.ShapeDtypeStruct((s_len, D_FF), BF16)
    return pl.pallas_call(
        body, name=name, grid=(nblk,),
        in_specs=[col(s_len, 0), col(s_len, 0), col(s_len, nblk), col(3, 0), col(3, nblk), col(1, 0), col(1, nblk)],
        out_specs=(col(s_len, 0), col(s_len, 0), col(3, 0), col(3, 0), col(1, 0), col(1, 0)),
        out_shape=(big, big, jax.ShapeDtypeStruct((3, D_FF), F32), jax.ShapeDtypeStruct((3, D_FF), F32),
                   jax.ShapeDtypeStruct((1, D_FF), F32), jax.ShapeDtypeStruct((1, D_FF), F32)),
        scratch_shapes=[pltpu.VMEM((s_len + 8, LANES), F32)] * 2, compiler_params=_cparams(1),
    )(dact, u, u, conv_w, conv_w, conv_b, conv_b)


def _rope_tables(positions):
    half = ROPE_DIMS // 2
    freqs = jnp.exp(jnp.arange(half, dtype=F32) * (-2.0 * math.log(ROPE_THETA) / ROPE_DIMS))
    ang = positions.reshape(-1).astype(F32)[:, None] * freqs
    cos, sin = jnp.cos(ang), jnp.sin(ang)
    n = ang.shape[0]
    zeros = lambda w: jnp.zeros((n, w), F32)
    c = jnp.concatenate([cos, cos, jnp.ones((n, HEAD_DIM - ROPE_DIMS), F32)], axis=1)
    s1 = jnp.concatenate([-sin, zeros(HEAD_DIM - half)], axis=1)
    s2 = jnp.concatenate([zeros(half), sin, zeros(HEAD_DIM - ROPE_DIMS)], axis=1)
    return tuple(jnp.tile(t, (1, 2)) for t in (c, s1, s2))


def _two(v):
    return jnp.tile(v.reshape(1, HEAD_DIM), (1, 2))


def _fold_heads(g):
    return g[0, :HEAD_DIM] + g[0, HEAD_DIM:]


MIX_WEIGHTS = ('w_gate', 'w_mem_kv', 'w_o_a', 'w_o_b', 'w_o_m', 'w_out')
FFN_WEIGHTS = ('w_up', 'conv_w', 'w_down')


def _device_step(x, mem, positions, target, w, hooks=None):
    tabs = _rope_tables(positions)
    dils = tuple(d for _, d in A_GROUPS)
    grads = {}
    w = dict(w)

    h, h_t, r1 = _rms_fwd(x, w['attn_norm'], "rms1")
    if hooks is not None:
        w.update(hooks.weights('in', [h, *tabs]))
    proj = _mm_rows([(h, w['w_in'], 0)], "mm_in")

    qkv_a, o_g, lse_g = [], [], []
    for gi, (window, d) in enumerate(A_GROUPS):
        gq, gk = _two(w['a_q_norm'][gi]), _two(w['a_k_norm'][gi])
        qkv = _qk_prep(proj, 6 * gi, d, False, gq, gk, tabs, f"qk_prep_a{gi}")
        o, lse = _band_fwd(qkv, 2 * d, window // d, None, f"band_fwd_a{gi}")
        qkv_a.append(qkv)
        o_g.append(o)
        lse_g.append(lse)
    o_a, lse_a = _merge_groups(o_g, lse_g, dils, "merge_a")
    if hooks is not None:
        w.update(hooks.weights('mix', [o_a]))

    gbq, gbk = _two(w['b_q_norm']), _two(w['b_k_norm'])
    sinks = jnp.repeat(w['b_sinks'].reshape(4, 2), HEAD_DIM, axis=1).reshape(4, 1, LANES)
    qkv_b = _qk_prep(proj, 18, 1, True, gbq, gbk, tabs, "qk_prep_b")
    o_b, lse_b = _band_fwd(qkv_b, 4, B_WINDOW - 1, sinks, "band_fwd_b")

    gates = _mm_rows([(h, w['w_gate'], 0)], "mm_gate", bias=w['b_gate'], sigmoid=True, out_dtypes=(BF16,))
    mk, mv = _mem_kv(mem, w['mem_norm'], w['w_mem_kv'], w['m_k_norm'], "mem_kv")
    o_m = _mem_attn_fwd(proj, 6, mk, mv, w['m_q_norm'], "mem_attn")

    pa, pb, pm, merged, x1 = _project_merge((o_a, o_b, o_m), (w['w_o_a'], w['w_o_b'], w['w_o_m']), gates, w['w_out'], x,
                                            "project_merge")

    if hooks is not None:
        w.update(hooks.weights('ffn', [x1]))
    h2_t, r2, u, act = _norm_up_conv_glu(x1, w['ffn_norm'], w['w_up'], w['conv_w'], w['conv_b'], "norm_up_conv_glu")
    dy, dy_b, dact, loss = _mm_rows([(act, w['w_down'], 0)], "mm_down", res=x1, loss_target=target)

    grads['w_down'] = _mm_tn(act, dy_b, "mm_dw_down")
    du_a, du_g, dcw_a, dcw_g, dcb_a, dcb_g = _conv_glu_bwd(dact, u, w['conv_w'], w['conv_b'], "conv_glu_bwd")
    grads['conv_w'] = jnp.concatenate([dcw_a, dcw_g], axis=1)
    grads['conv_b'] = jnp.concatenate([dcb_a, dcb_g], axis=1)
    grads['w_up'] = _mm_cols(h2_t, [du_a, du_g], "mm_dw_up")
    ffn_gain = w['ffn_norm']
    if hooks is not None:
        ffn_gain = ffn_gain + hooks.grads('ffn', grads)[0:1, 0:1]
    dx1, dx1_b, grads['ffn_norm'] = _mm_rows([(du_a, w['w_up'], 0), (du_g, w['w_up'], 1)], "mm_d_h2", nt=True,
                                             rms_bwd=(x1, r2, ffn_gain, dy), out_dtypes=(F32, BF16))

    grads['w_out'] = _mm_tn(merged, dx1_b, "mm_dw_out")
    dpa, dpb, dpm, dgpre, grads['b_gate'] = _project_merge_bwd(dx1_b, w['w_out'], gates, pa, pb, pm,
                                                               "project_merge_bwd")
    do_a, do_b, do_m = _mm_rows_each([(dpa, w['w_o_a']), (dpb, w['w_o_b']), (dpm, w['w_o_m'])], "mm_d_o")
    grads['w_o_a'], grads['w_o_b'], grads['w_o_m'] = _mm_tn_each([(o_a, dpa), (o_b, dpb), (o_m, dpm)], "mm_dw_o")
    grads['w_gate'] = _mm_cols(h_t, [dgpre], "mm_dw_gate")
    dq_m, dmk, dmv, grads['m_q_norm'] = _mem_attn_bwd(proj, 6, mk, mv, w['m_q_norm'], do_m, "mem_attn_bwd")
    grads['w_mem_kv'], grads['mem_norm'], grads['m_k_norm'] = _mem_kv_bwd(
        mem, w['mem_norm'], w['w_mem_kv'], w['m_k_norm'], dmk, dmv, "mem_kv_bwd")
    a_gain = w['a_q_norm']
    if hooks is not None:
        a_gain = a_gain + hooks.grads('mix', grads)[0:1, 0:1]

    prep = _bwd_prep(do_a, o_a, lse_a, dils, None, "bwd_prep_a")
    dproj, dgq_a, dgk_a = [], [], []
    for gi, (window, d) in enumerate(A_GROUPS):
        gq, gk = _two(a_gain[gi]), _two(w['a_k_norm'][gi])
        dqkv = _band_bwd(qkv_a[gi], prep[3 * gi], prep[3 * gi + 1], prep[3 * gi + 2], 2 * d, window // d,
                         f"band_bwd_a{gi}")
        dp, dgq, dgk = _qk_prep_bwd(dqkv, proj, 6 * gi, d, False, gq, gk, tabs, f"qk_prep_bwd_a{gi}")
        dproj.append(dp)
        dgq_a.append(_fold_heads(dgq))
        dgk_a.append(_fold_heads(dgk))
    grads['a_q_norm'] = jnp.stack(dgq_a)
    grads['a_k_norm'] = jnp.stack(dgk_a)

    do_bu, lse_bu, delta_bu, dsink = _bwd_prep(do_b, o_b, lse_b, (1,), sinks, "bwd_prep_b")
    dqkv = _band_bwd(qkv_b, do_bu, lse_bu, delta_bu, 4, B_WINDOW - 1, "band_bwd_b")
    dp_b, dgq, dgk = _qk_prep_bwd(dqkv, proj, 18, 1, True, gbq, gbk, tabs, "qk_prep_bwd_b")
    dproj.append(dp_b)
    grads['b_q_norm'] = _fold_heads(dgq)
    grads['b_k_norm'] = _fold_heads(dgk)
    grads['b_sinks'] = jnp.stack([dsink[:, 0, 0], dsink[:, 0, HEAD_DIM]], axis=1).reshape(8)

    dproj.append(dq_m)

    cols = (0, 1, 2, 3, 6)
    grads['w_in'] = _mm_rows_cat(h_t, dproj, "mm_dw_in")
    attn_gain = w['attn_norm']
    if hooks is not None:
        attn_gain = attn_gain + hooks.grads('in', grads)[0:1, 0:1]
    grad_x, grads['attn_norm'] = _mm_rows(
        [(dp, w['w_in'], c) for dp, c in zip(dproj, cols)] + [(dgpre, w['w_gate'], 0)], "mm_d_h", nt=True,
        rms_bwd=(x, r1, attn_gain, dx1))
    return loss, grad_x, grads


def _coords():
    return lax.axis_index("x"), lax.axis_index("y"), lax.axis_index("c")


def _slot(p):
    return 4 * p[0] + 2 * p[1] + p[2]


ALL_PEERS = tuple(range(1, N_DEV))
CHIP_PEERS = (1, 4, 2, 6)
OTHER_CHIPS = (4, 2, 6)


def _peers(me, masks=ALL_PEERS):
    x, y, c = me
    return [(1 - x if mask & 4 else x, 1 - y if mask & 2 else y, 1 - c if mask & 1 else c) for mask in masks]


HBM_SPEC = pl.BlockSpec(memory_space=pltpu.HBM)


SEM_SPEC = pl.BlockSpec(memory_space=pltpu.SEMAPHORE)
SIDE_EFFECT = pltpu.SideEffectType.DATAFLOW_SIDE_EFFECTING


def _exchange_start(blocks, name, gather=False, masks=ALL_PEERS, after=None):
    n = len(blocks)
    n_peers = len(masks)
    n_in = 2 * n + (0 if after is None else 1)

    def body(*refs):
        ins, lands = refs[:n], refs[n:2 * n]
        send_sems, recv_sems = refs[n_in], refs[n_in + 1]
        token = refs[-1]
        me = _coords()
        peers = _peers(me, masks)
        for a in range(n):
            for k in range(n_peers):
                pltpu.make_async_remote_copy(
                    src_ref=ins[a] if gather else ins[a].at[_slot(peers[k])], dst_ref=lands[a].at[_slot(me)],
                    send_sem=send_sems.at[a * n_peers + k], recv_sem=recv_sems.at[a * n_peers + k],
                    device_id=peers[k], device_id_type=MESH).start()
        token[...] = jnp.zeros_like(token)

    land_shapes = [((N_DEV,) + b.shape) if gather else b.shape for b in blocks]
    hbm_in = [pltpu.HBM(b.shape, b.dtype) for b in blocks]
    hbm_land = [pltpu.HBM(s, b.dtype) for s, b in zip(land_shapes, blocks)]
    sems = pltpu.SemaphoreType.DMA((n * n_peers,))
    ins = [pltpu.with_memory_space_constraint(b, pltpu.HBM) for b in blocks]
    lands = [pltpu.with_memory_space_constraint(lax.empty(s, b.dtype), pltpu.HBM) for s, b in zip(land_shapes, blocks)]
    return pl.pallas_call(
        body, name=name, out_shape=(sems, sems, *hbm_in, *hbm_land, jax.ShapeDtypeStruct((8, LANES), F32)),
        in_specs=[HBM_SPEC] * (2 * n) + ([] if after is None else [pl.BlockSpec(memory_space=pl.ANY)]),
        out_specs=(SEM_SPEC, SEM_SPEC, *([HBM_SPEC] * (2 * n)), pl.BlockSpec(memory_space=pltpu.VMEM)),
        input_output_aliases={i: 2 + i for i in range(2 * n)},
        compiler_params=pltpu.CompilerParams(has_side_effects=SIDE_EFFECT),
    )(*ins, *lands, *([] if after is None else [after]))


def _exchange_wait(started, after, name, gather=False, masks=ALL_PEERS):
    n = (len(started) - 3) // 2
    n_peers = len(masks)
    send_sems, recv_sems = started[0], started[1]
    thru = started[2:2 + 2 * n]

    def body(*refs):
        ins, lands = refs[:n], refs[n:2 * n]
        send_ref, recv_ref = refs[2 * n], refs[2 * n + 1]
        me = _coords()
        peers = _peers(me, masks)
        for a in range(n):
            for k in range(n_peers):
                cp = pltpu.make_async_remote_copy(
                    src_ref=ins[a] if gather else ins[a].at[_slot(peers[k])], dst_ref=lands[a].at[_slot(peers[k])],
                    send_sem=send_ref.at[a * n_peers + k], recv_sem=recv_ref.at[a * n_peers + k],
                    device_id=peers[k], device_id_type=MESH)
                cp.wait_send()
                cp.wait_recv()

    hbm = [pltpu.HBM(t.shape, t.dtype) for t in thru]
    res = pl.pallas_call(
        body, name=name, out_shape=tuple(hbm),
        in_specs=[HBM_SPEC] * (2 * n) + [SEM_SPEC, SEM_SPEC] + [pl.BlockSpec(memory_space=pl.ANY)] * len(after),
        out_specs=tuple([HBM_SPEC] * (2 * n)), input_output_aliases={i: i for i in range(2 * n)},
        compiler_params=pltpu.CompilerParams(has_side_effects=SIDE_EFFECT),
    )(*thru, send_sems, recv_sems, *after)
    return res[n:]


def _sibling_forward(arrays, name):
    n = len(arrays)
    n_fwd = len(OTHER_CHIPS)

    def body(*refs):
        bufs = refs[n:2 * n]
        token, send_sems, recv_sems = refs[2 * n:]
        token[...] = jnp.zeros_like(token)
        x, y, c = _coords()
        sibling = (x, y, 1 - c)
        mine = _peers((x, y, c), OTHER_CHIPS)
        theirs = _peers(sibling, OTHER_CHIPS)

        def copy(a, k, block):
            rows = bufs[a].at[_slot(block)]
            return pltpu.make_async_remote_copy(
                src_ref=rows, dst_ref=rows, send_sem=send_sems.at[a * n_fwd + k], recv_sem=recv_sems.at[a * n_fwd + k],
                device_id=sibling, device_id_type=MESH)

        sends = [copy(a, k, mine[k]) for a in range(n) for k in range(n_fwd)]
        for cp in sends:
            cp.start()
        for a in range(n):
            for k in range(n_fwd):
                copy(a, k, theirs[k]).wait_recv()
        for cp in sends:
            cp.wait_send()

    res = pl.pallas_call(
        body, name=name, in_specs=[HBM_SPEC] * n,
        out_specs=tuple([HBM_SPEC] * n + [pl.BlockSpec(memory_space=pltpu.VMEM)]),
        out_shape=tuple([jax.ShapeDtypeStruct(a.shape, a.dtype) for a in arrays] + [jax.ShapeDtypeStruct((8, LANES), F32)]),
        input_output_aliases={i: i for i in range(n)},
        scratch_shapes=[pltpu.SemaphoreType.DMA((n * n_fwd,)), pltpu.SemaphoreType.DMA((n * n_fwd,))],
    )(*arrays)
    return res[:n], res[n]


def _all_sum(p, name):
    def body(p_ref, o_ref, recv, send_sems, recv_sems):
        me = _coords()
        peers = _peers(me)
        recv[_slot(me)] = p_ref[...]

        def copy(k, landing):
            return pltpu.make_async_remote_copy(
                src_ref=p_ref, dst_ref=recv.at[_slot(landing)], send_sem=send_sems.at[k], recv_sem=recv_sems.at[k],
                device_id=peers[k], device_id_type=MESH)

        sends = [copy(k, me) for k in range(N_DEV - 1)]
        for cp in sends:
            cp.start()
        for k in range(N_DEV - 1):
            copy(k, peers[k]).wait_recv()
        for cp in sends:
            cp.wait_send()
        acc = recv[0]
        for s in range(1, N_DEV):
            acc = acc + recv[s]
        o_ref[...] = acc

    vmem = pl.BlockSpec(memory_space=pltpu.VMEM)
    return pl.pallas_call(
        body, name=name, in_specs=[vmem], out_specs=vmem, out_shape=jax.ShapeDtypeStruct(p.shape, F32),
        scratch_shapes=[pltpu.VMEM((N_DEV,) + p.shape, F32), pltpu.SemaphoreType.DMA((N_DEV - 1,)),
                        pltpu.SemaphoreType.DMA((N_DEV - 1,))],
    )(p)


def _adam(w, g, m, v):
    m2 = ADAM_B1 * m + (1.0 - ADAM_B1) * g
    v2 = ADAM_B2 * v + (1.0 - ADAM_B2) * (g * g)
    m_hat = m2 / (1.0 - ADAM_B1 ** ADAM_STEP)
    v_hat = v2 / (1.0 - ADAM_B2 ** ADAM_STEP)
    delta = -ADAM_LR * (m_hat / (jnp.sqrt(v_hat) + ADAM_EPS) + ADAM_WD * w)
    return delta, m2, v2


def _row_tile(rows, cols):
    best = rows
    for t in range(16, rows, 16):
        if rows % t == 0 and t * cols * 4 <= (1 << 20):
            best = t
    return best


def _adam_reduce(parts, w, m, v, name):
    rows, cols = w.shape
    tr = _row_tile(rows, cols)

    def body(p_ref, w_ref, m_ref, v_ref, g_ref, d_ref, m2_ref, v2_ref):
        g = p_ref[0].astype(F32)
        for s in range(1, N_DEV):
            g = g + p_ref[s].astype(F32)
        g_ref[...] = g
        d_ref[...], m2_ref[...], v2_ref[...] = _adam(w_ref[...], g, m_ref[...], v_ref[...])

    blk = pl.BlockSpec((tr, cols), lambda i: (i, 0))
    shp = jax.ShapeDtypeStruct((rows, cols), F32)
    return pl.pallas_call(
        body, name=name, grid=(rows // tr,),
        in_specs=[pl.BlockSpec((N_DEV, tr, cols), lambda i: (0, i, 0)), blk, blk, blk],
        out_specs=(blk,) * 4, out_shape=(shp,) * 4, compiler_params=_cparams(1),
    )(parts, w, m, v)


PACK_COLS = 1024
PACK = {'attn_norm': (0, 1, 1024), 'mem_norm': (1, 1, 1024), 'ffn_norm': (2, 1, 1024), 'b_gate': (3, 3, 1024),
        'conv_b': (6, 6, 1024), 'a_q_norm': (12, 3, 64), 'a_k_norm': (15, 3, 64), 'b_q_norm': (18, 1, 64),
        'b_k_norm': (19, 1, 64), 'm_q_norm': (20, 1, 128), 'm_k_norm': (21, 1, 128), 'b_sinks': (22, 1, 8)}
PACK_LOSS_ROW = 23
PACK_ROWS = 24


def _pack_pieces(name, width):
    r0, nr, lanes = PACK[name]
    out = []
    for j in range(nr):
        if lanes == PACK_COLS:
            w = min(PACK_COLS, width - j * PACK_COLS)
            out.append((r0 + j, slice(0, 1), slice(j * PACK_COLS, j * PACK_COLS + w), w))
        else:
            out.append((r0 + j, slice(j, j + 1), slice(0, lanes), lanes))
    return out


def _pack_small(grads, loss_tile, name):
    names = list(PACK)

    def body(*refs):
        o_ref = refs[-1]
        o_ref[...] = jnp.zeros_like(o_ref)
        for k, nm in enumerate(names):
            for row, rs, ls, w in _pack_pieces(nm, refs[k].shape[1]):
                o_ref[row:row + 1, 0:w] = refs[k][rs, ls]
        o_ref[PACK_LOSS_ROW:PACK_LOSS_ROW + 1, 0:1] = refs[len(names)][0:1, 0:1]

    vmem = pl.BlockSpec(memory_space=pltpu.VMEM)
    args = [grads[nm] for nm in names] + [loss_tile]
    return pl.pallas_call(body, name=name, in_specs=[vmem] * len(args), out_specs=vmem,
                          out_shape=jax.ShapeDtypeStruct((PACK_ROWS, PACK_COLS), F32))(*args)


def _adam_small(gsum, ws, ms, vs, name):
    names = list(PACK)
    n = len(names)

    def body(*refs):
        g_ref = refs[0]
        w_refs, m_refs, v_refs = refs[1:1 + n], refs[1 + n:1 + 2 * n], refs[1 + 2 * n:1 + 3 * n]
        outs = refs[1 + 3 * n:]
        outs[0][...] = g_ref[PACK_LOSS_ROW:PACK_LOSS_ROW + 1, 0:1]
        for k, nm in enumerate(names):
            o_g, o_d, o_m, o_v = outs[1 + 4 * k:5 + 4 * k]
            for row, rs, ls, width in _pack_pieces(nm, w_refs[k].shape[1]):
                src = (rs, ls)
                g = g_ref[row:row + 1, 0:width]
                d, m2, v2 = _adam(w_refs[k][src], g, m_refs[k][src], v_refs[k][src])
                o_g[src] = g
                o_d[src] = d
                o_m[src] = m2
                o_v[src] = v2

    vmem = pl.BlockSpec(memory_space=pltpu.VMEM)
    shapes = [jax.ShapeDtypeStruct((1, 1), F32)]
    for nm in names:
        shapes += [jax.ShapeDtypeStruct(ws[nm].shape, F32)] * 4
    args = [gsum] + [ws[nm] for nm in names] + [ms[nm] for nm in names] + [vs[nm] for nm in names]
    return pl.pallas_call(
        body, name=name, in_specs=[vmem] * len(args), out_specs=tuple([vmem] * len(shapes)), out_shape=tuple(shapes),
    )(*args)


def _as2d(name, a):
    return a.reshape(a.shape[-2], a.shape[-1]) if a.ndim == 3 else a


def kernel(x, mem, positions, attn_norm, w_in, a_q_norm, a_k_norm, b_q_norm, b_k_norm, b_sinks, mem_norm, w_mem_kv, m_q_norm, m_k_norm, w_o_a, w_o_b, w_o_m, w_gate, b_gate, w_out, ffn_norm, w_up, conv_w, conv_b, w_down, loss_target, m_attn_norm, m_w_in, m_a_q_norm, m_a_k_norm, m_b_q_norm, m_b_k_norm, m_b_sinks, m_mem_norm, m_w_mem_kv, m_m_q_norm, m_m_k_norm, m_w_o_a, m_w_o_b, m_w_o_m, m_w_gate, m_b_gate, m_w_out, m_ffn_norm, m_w_up, m_conv_w, m_conv_b, m_w_down, v_attn_norm, v_w_in, v_a_q_norm, v_a_k_norm, v_b_q_norm, v_b_k_norm, v_b_sinks, v_mem_norm, v_w_mem_kv, v_m_q_norm, v_m_k_norm, v_w_o_a, v_w_o_b, v_w_o_m, v_w_gate, v_b_gate, v_w_out, v_ffn_norm, v_w_up, v_conv_w, v_conv_b, v_w_down):
    given = dict(attn_norm=attn_norm, w_in=w_in, a_q_norm=a_q_norm, a_k_norm=a_k_norm, b_q_norm=b_q_norm, b_k_norm=b_k_norm, b_sinks=b_sinks, mem_norm=mem_norm, w_mem_kv=w_mem_kv, m_q_norm=m_q_norm, m_k_norm=m_k_norm, w_o_a=w_o_a, w_o_b=w_o_b, w_o_m=w_o_m, w_gate=w_gate, b_gate=b_gate, w_out=w_out, ffn_norm=ffn_norm, w_up=w_up, conv_w=conv_w, conv_b=conv_b, w_down=w_down)
    mom1 = dict(attn_norm=m_attn_norm, w_in=m_w_in, a_q_norm=m_a_q_norm, a_k_norm=m_a_k_norm, b_q_norm=m_b_q_norm, b_k_norm=m_b_k_norm, b_sinks=m_b_sinks, mem_norm=m_mem_norm, w_mem_kv=m_w_mem_kv, m_q_norm=m_m_q_norm, m_k_norm=m_m_k_norm, w_o_a=m_w_o_a, w_o_b=m_w_o_b, w_o_m=m_w_o_m, w_gate=m_w_gate, b_gate=m_b_gate, w_out=m_w_out, ffn_norm=m_ffn_norm, w_up=m_w_up, conv_w=m_conv_w, conv_b=m_conv_b, w_down=m_w_down)
    mom2 = dict(attn_norm=v_attn_norm, w_in=v_w_in, a_q_norm=v_a_q_norm, a_k_norm=v_a_k_norm, b_q_norm=v_b_q_norm, b_k_norm=v_b_k_norm, b_sinks=v_b_sinks, mem_norm=v_mem_norm, w_mem_kv=v_w_mem_kv, m_q_norm=v_m_q_norm, m_k_norm=v_m_k_norm, w_o_a=v_w_o_a, w_o_b=v_w_o_b, w_o_m=v_w_o_m, w_gate=v_w_gate, b_gate=v_b_gate, w_out=v_w_out, ffn_norm=v_ffn_norm, w_up=v_w_up, conv_w=v_conv_w, conv_b=v_conv_b, w_down=v_w_down)

    big = list(BIG)
    stages = {'mix': list(MIX_WEIGHTS), 'ffn': list(FFN_WEIGHTS), 'in': ['w_in']}
    my_slot = _slot(_coords())

    def shard(n):
        return given[n][0] if n == 'conv_w' else given[n][0].astype(BF16)

    def whole(n, g):
        _, r, c = g.shape
        return g.reshape(N_DEV * r, c) if BIG[n] == 0 else g.transpose(1, 0, 2).reshape(r, N_DEV * c)

    def to_blocks(n, g):
        r, c = given[n].shape[1:]
        g = g.reshape(N_DEV, r, c) if BIG[n] == 0 else g.reshape(r, N_DEV, c).transpose(1, 0, 2)
        return g if n == 'conv_w' else g.astype(BF16)

    class Hooks:
        next_stage = {'in': 'mix', 'mix': 'ffn'}

        def __init__(self):
            self.coming, self.sent = {}, {}
            self.shards = {n: shard(n) for n in big}
            self.start_gather('in', None)

        def start_gather(self, stage, after):
            src = [self.shards[n] for n in stages[stage]]
            self.coming[stage] = _exchange_start(src, f"gather_{stage}_start", gather=True, masks=CHIP_PEERS,
                                                 after=after)

        def weights(self, stage, after):
            names = stages[stage]
            after = list(after)
            if stage == 'in':
                after += [self.shards[n] for n in stages['mix'] + stages['ffn']]
            landed = _exchange_wait(self.coming[stage], after, f"gather_{stage}_wait", gather=True, masks=CHIP_PEERS)
            landed, token = _sibling_forward(landed, f"gather_{stage}_forward")
            if stage in self.next_stage:
                self.start_gather(self.next_stage[stage], token)
            return {n: whole(n, lax.dynamic_update_slice_in_dim(land, self.shards[n][None], my_slot, axis=0))
                    for n, land in zip(names, landed)}

        def grads(self, stage, g):
            blocks = [to_blocks(n, g[n]) for n in stages[stage]]
            own = [lax.dynamic_slice_in_dim(b, my_slot, 1, axis=0) for b in blocks]
            self.sent[stage] = (_exchange_start(blocks, f"exchange_{stage}_start"), own)
            return self.sent[stage][0][-1]

        def parts(self, stage, after):
            started, own = self.sent[stage]
            landed = _exchange_wait(started, [after], f"exchange_{stage}_wait")
            return {n: lax.dynamic_update_slice_in_dim(land, o, my_slot, axis=0)
                    for n, land, o in zip(stages[stage], landed, own)}

    hooks = Hooks()
    w = {}
    for n in SMALL:
        w[n] = given[n]
    w['a_q_norm'], w['a_k_norm'] = given['a_q_norm'][0], given['a_k_norm'][0]
    w['b_q_norm'], w['b_k_norm'], w['b_sinks'] = given['b_q_norm'][0], given['b_k_norm'][0], given['b_sinks'][0]

    loss_tile, grad_x, grads = _device_step(x[0], mem[0], positions[0], loss_target[0], w, hooks)
    out = {}
    after = grad_x
    for stage in ('ffn', 'mix', 'in'):
        for n, p in hooks.parts(stage, after).items():
            res = _adam_reduce(p, given[n][0], mom1[n][0], mom2[n][0], f"adam_{n}")
            out[n] = tuple(t[None] for t in res)
            after = res[0]

    small = {n: grads[n] for n in PACK}
    small['b_q_norm'], small['b_k_norm'] = grads['b_q_norm'].reshape(1, -1), grads['b_k_norm'].reshape(1, -1)
    small['b_sinks'] = grads['b_sinks'].reshape(1, -1)
    gsum = _all_sum(_pack_small(small, loss_tile, "pack_small"), "sum_small")
    ws = {n: _as2d(n, given[n]) for n in PACK}
    ms = {n: _as2d(n, mom1[n]) for n in PACK}
    vs = {n: _as2d(n, mom2[n]) for n in PACK}
    res = _adam_small(gsum, ws, ms, vs, "adam_small")
    loss = res[0].reshape(())
    for k, n in enumerate(PACK):
        out[n] = tuple(t.reshape(given[n].shape) for t in res[1 + 4 * k:5 + 4 * k])

    outs = [loss, grad_x[None]]
    for field in range(4):
        outs += [out[n][field] for n in WEIGHTS]
    return tuple(outs)
```

```python
import functools
import math

import jax
import jax.numpy as jnp
from jax import lax
from jax.experimental import pallas as pl
from jax.experimental.pallas import tpu as pltpu

F32 = jnp.float32
BF16 = jnp.bfloat16

N_DEV = 8
HEAD_DIM = 64
A_GROUPS = ((128, 1), (512, 4), (2048, 16))
B_WINDOW = 128
M_HEADS = 4
M_HEAD_DIM = 128
MEM_LEN = 256
D_FF = 2816
ROPE_THETA = 500000.0
ROPE_DIMS = 16
BLOCK = 128
EPS = 1e-6
LANES = 128
BAND_Q_BLOCKS = 8
BAND_UNITS = 2

ADAM_LR = 0.001
ADAM_B1 = 0.9
ADAM_B2 = 0.999
ADAM_EPS = 1e-08
ADAM_WD = 0.01
ADAM_STEP = 10

VMEM_LIMIT_BYTES = 56 * 1024 * 1024
GRAD_DTYPE = BF16
MESH = pl.DeviceIdType.MESH

WEIGHTS = ['attn_norm', 'w_in', 'a_q_norm', 'a_k_norm', 'b_q_norm', 'b_k_norm', 'b_sinks', 'mem_norm',
           'w_mem_kv', 'm_q_norm', 'm_k_norm', 'w_o_a', 'w_o_b', 'w_o_m', 'w_gate', 'b_gate', 'w_out',
           'ffn_norm', 'w_up', 'conv_w', 'conv_b', 'w_down']
BIG = {'w_in': 1, 'w_mem_kv': 0, 'w_o_a': 1, 'w_o_b': 1, 'w_o_m': 1, 'w_gate': 1, 'w_out': 0, 'w_up': 1,
       'conv_w': 1, 'w_down': 0}
SMALL = [n for n in WEIGHTS if n not in BIG]


def _cparams(n_grid):
    return pltpu.CompilerParams(dimension_semantics=("arbitrary",) * n_grid, vmem_limit_bytes=VMEM_LIMIT_BYTES)


def _seg_matrix(width):
    shift = width.bit_length() - 1
    r = lax.shift_right_logical(lax.broadcasted_iota(jnp.int32, (LANES, LANES), 0), shift)
    c = lax.shift_right_logical(lax.broadcasted_iota(jnp.int32, (LANES, LANES), 1), shift)
    return jnp.where(r == c, 1.0, 0.0).astype(BF16)


def _seg_sum(x, seg):
    hi = x.astype(BF16)
    r1 = x - hi.astype(F32)
    mid = r1.astype(BF16)
    lo = (r1 - mid.astype(F32)).astype(BF16)
    dot = functools.partial(jnp.dot, preferred_element_type=F32)
    return dot(hi, seg) + dot(mid, seg) + dot(lo, seg)


def _rope(y, c, s1, s2):
    return y * c + pltpu.roll(y, LANES - ROPE_DIMS // 2, 1) * s1 + pltpu.roll(y, ROPE_DIMS // 2, 1) * s2


def _unrope(dy, c, s1, s2):
    return dy * c + pltpu.roll(dy * s1, ROPE_DIMS // 2, 1) + pltpu.roll(dy * s2, LANES - ROPE_DIMS // 2, 1)


def _sigmoid(x):
    return 1.0 / (1.0 + jnp.exp(-x))


def _rms_fwd(x, gain, name):
    s_len, d = x.shape
    tm = 512

    def body(x_ref, g_ref, h_ref, ht_ref, r_ref):
        xv = x_ref[...]
        r = lax.rsqrt(jnp.mean(xv * xv, axis=-1, keepdims=True) + EPS)
        h = ((xv * r) * g_ref[...]).astype(BF16)
        h_ref[...] = h
        ht_ref[...] = h.T
        r_ref[...] = r

    return pl.pallas_call(
        body, name=name, grid=(s_len // tm,),
        in_specs=[pl.BlockSpec((tm, d), lambda i: (i, 0)), pl.BlockSpec((1, d), lambda i: (0, 0))],
        out_specs=(pl.BlockSpec((tm, d), lambda i: (i, 0)), pl.BlockSpec((d, tm), lambda i: (0, i)),
                   pl.BlockSpec((tm, 1), lambda i: (i, 0))),
        out_shape=(jax.ShapeDtypeStruct((s_len, d), BF16), jax.ShapeDtypeStruct((d, s_len), BF16),
                   jax.ShapeDtypeStruct((s_len, 1), F32)),
        compiler_params=_cparams(1),
    )(x, gain)


def _resident(shape, index_map):
    return pl.BlockSpec(shape, index_map, pipeline_mode=pl.Buffered(1))


def _mm_rows(pairs, name, nt=False, tm=512, bias=None, sigmoid=False, res=None, out_dtypes=(F32,), loss_target=None,
             rms_bwd=None):
    m = pairs[0][0].shape[0]
    n = pairs[0][1].shape[0] if nt else pairs[0][1].shape[1]
    n_pairs = len(pairs)
    has_bias, has_res, has_loss = bias is not None, res is not None, loss_target is not None
    has_rms = rms_bwd is not None
    dims = (((1,), (1,)), ((), ())) if nt else (((1,), (0,)), ((), ()))

    def body(*refs):
        acc = None
        for p in range(n_pairs):
            t = lax.dot_general(refs[2 * p][...].astype(BF16), refs[2 * p + 1][...], dims, preferred_element_type=F32)
            acc = t if acc is None else acc + t
        pos = 2 * n_pairs
        if has_bias:
            acc = acc + refs[pos][...]
            pos += 1
        if sigmoid:
            acc = _sigmoid(acc)
        if has_res:
            acc = refs[pos][...] + acc
            pos += 1
        if has_loss:
            dy_ref, dyb_ref, da_ref, l_ref = refs[pos + 1:]

            @pl.when(pl.program_id(0) == 0)
            def _():
                l_ref[...] = jnp.zeros_like(l_ref)
            err = acc - refs[pos][...]
            dy = err * (1.0 / n)
            dy_ref[...] = dy
            dyb_ref[...] = dy.astype(BF16)
            da_ref[...] = lax.dot_general(dy.astype(BF16), refs[1][...], (((1,), (1,)), ((), ())),
                                          preferred_element_type=F32).astype(BF16)
            part = 0.5 * jnp.sum(jnp.mean(err * err, axis=-1, keepdims=True), axis=0, keepdims=True)
            l_ref[...] += jnp.broadcast_to(part, l_ref.shape)
            return
        if has_rms:
            x_ref, r_ref, g_ref, add_ref = refs[pos:pos + 4]
            dg_ref = refs[-1]

            @pl.when(pl.program_id(0) == 0)
            def _():
                dg_ref[...] = jnp.zeros_like(dg_ref)
            rv = r_ref[...]
            xhat = x_ref[...] * rv
            dg_ref[...] += jnp.sum(acc * xhat, axis=0, keepdims=True)
            dxhat = acc * g_ref[...]
            acc = add_ref[...] + rv * (dxhat - xhat * jnp.mean(dxhat * xhat, axis=-1, keepdims=True))
            for o_ref in refs[pos + 4:-1]:
                o_ref[...] = acc.astype(o_ref.dtype)
            return
        for o_ref in refs[pos:]:
            o_ref[...] = acc.astype(o_ref.dtype)

    in_specs, args = [], []
    for a, w, blk in pairs:
        k = a.shape[1]
        in_specs.append(pl.BlockSpec((tm, k), lambda i: (i, 0)))
        if nt:
            in_specs.append(_resident((n, k), lambda i, blk=blk: (0, blk)))
        else:
            in_specs.append(_resident((k, n), lambda i, blk=blk: (blk, 0)))
        args += [a, w]
    if has_bias:
        in_specs.append(_resident((1, n), lambda i: (0, 0)))
        args.append(bias)
    if has_res:
        in_specs.append(pl.BlockSpec((tm, n), lambda i: (i, 0)))
        args.append(res)
    out = pl.BlockSpec((tm, n), lambda i: (i, 0))
    if has_loss:
        k0 = pairs[0][0].shape[1]
        return pl.pallas_call(
            body, name=name, grid=(m // tm,), in_specs=in_specs + [out],
            out_specs=(out, out, pl.BlockSpec((tm, k0), lambda i: (i, 0)), pl.BlockSpec((8, LANES), lambda i: (0, 0))),
            out_shape=(jax.ShapeDtypeStruct((m, n), F32), jax.ShapeDtypeStruct((m, n), BF16),
                       jax.ShapeDtypeStruct((m, k0), BF16), jax.ShapeDtypeStruct((8, LANES), F32)),
            compiler_params=_cparams(1),
        )(*args, loss_target)
    if has_rms:
        x, r, gain, add = rms_bwd
        vec = _resident((1, n), lambda i: (0, 0))
        return pl.pallas_call(
            body, name=name, grid=(m // tm,),
            in_specs=in_specs + [out, pl.BlockSpec((tm, 1), lambda i: (i, 0)), vec, out],
            out_specs=tuple([out] * len(out_dtypes) + [pl.BlockSpec((1, n), lambda i: (0, 0))]),
            out_shape=tuple([jax.ShapeDtypeStruct((m, n), dt) for dt in out_dtypes] + [jax.ShapeDtypeStruct((1, n), F32)]),
            compiler_params=_cparams(1),
        )(*args, x, r, gain, add)
    outs = pl.pallas_call(
        body, name=name, grid=(m // tm,), in_specs=in_specs, out_specs=tuple([out] * len(out_dtypes)),
        out_shape=tuple(jax.ShapeDtypeStruct((m, n), dt) for dt in out_dtypes), compiler_params=_cparams(1),
    )(*args)
    return outs[0] if len(out_dtypes) == 1 else outs


def _mm_rows_each(pairs, name, tm=512):
    m = pairs[0][0].shape[0]
    n_pairs = len(pairs)

    def body(*refs):
        for p in range(n_pairs):
            refs[2 * n_pairs + p][...] = lax.dot_general(refs[2 * p][...].astype(BF16), refs[2 * p + 1][...],
                                                         (((1,), (1,)), ((), ())), preferred_element_type=F32)

    in_specs, args = [], []
    for a, w in pairs:
        in_specs += [pl.BlockSpec((tm, a.shape[1]), lambda i: (i, 0)), _resident(w.shape, lambda i: (0, 0))]
        args += [a, w]
    return pl.pallas_call(
        body, name=name, grid=(m // tm,), in_specs=in_specs,
        out_specs=tuple(pl.BlockSpec((tm, w.shape[0]), lambda i: (i, 0)) for _, w in pairs),
        out_shape=tuple(jax.ShapeDtypeStruct((m, w.shape[0]), F32) for _, w in pairs), compiler_params=_cparams(1),
    )(*args)


def _mm_rows_cat(a, ws, name, tm=256):
    m, k = a.shape
    widths = [w.shape[1] for w in ws]
    n = sum(widths)

    def body(*refs):
        a_ref, o_ref = refs[0], refs[-1]
        av = a_ref[...]
        off = 0
        for p, width in enumerate(widths):
            o_ref[:, off:off + width] = jnp.dot(av, refs[1 + p][...], preferred_element_type=F32).astype(GRAD_DTYPE)
            off += width

    return pl.pallas_call(
        body, name=name, grid=(m // tm,),
        in_specs=[pl.BlockSpec((tm, k), lambda i: (i, 0))] + [_resident((k, wd), lambda i: (0, 0)) for wd in widths],
        out_specs=pl.BlockSpec((tm, n), lambda i: (i, 0)),
        out_shape=jax.ShapeDtypeStruct((m, n), GRAD_DTYPE), compiler_params=_cparams(1),
    )(a, *ws)


def _mm_cols(a, bs, name, tn=256):
    m, k = a.shape
    counts = [b.shape[1] // tn for b in bs]
    starts = [sum(counts[:p]) for p in range(len(bs))]

    def body(*refs):
        a_ref, o_ref = refs[0], refs[-1]
        j = pl.program_id(0)
        for p, b_ref in enumerate(refs[1:-1]):
            @pl.when((j >= starts[p]) & (j < starts[p] + counts[p]))
            def _():
                o_ref[...] = jnp.dot(a_ref[...], b_ref[...].astype(BF16), preferred_element_type=F32).astype(GRAD_DTYPE)

    b_specs = [pl.BlockSpec((k, tn), lambda j, s=s, c=c: (0, jnp.clip(j - s, 0, c - 1))) for s, c in zip(starts, counts)]
    return pl.pallas_call(
        body, name=name, grid=(sum(counts),),
        in_specs=[_resident((m, k), lambda j: (0, 0))] + b_specs,
        out_specs=pl.BlockSpec((m, tn), lambda j: (0, j)),
        out_shape=jax.ShapeDtypeStruct((m, sum(counts) * tn), GRAD_DTYPE), compiler_params=_cparams(1),
    )(a, *bs)


def _mm_tn_each(pairs, name, tile=256):
    n = pairs[0][1].shape[1]
    n_pairs = len(pairs)
    dims = (((0,), (0,)), ((), ()))

    def body(*refs):
        for p in range(n_pairs):
            refs[2 * n_pairs + p][...] = lax.dot_general(refs[2 * p][...].astype(BF16), refs[2 * p + 1][...].astype(BF16),
                                                         dims, preferred_element_type=F32).astype(GRAD_DTYPE)

    in_specs, args = [], []
    for a, b in pairs:
        in_specs += [_resident(a.shape, lambda j: (0, 0)), pl.BlockSpec((b.shape[0], tile), lambda j: (0, j))]
        args += [a, b]
    return pl.pallas_call(
        body, name=name, grid=(n // tile,), in_specs=in_specs,
        out_specs=tuple(pl.BlockSpec((a.shape[1], tile), lambda j: (0, j)) for a, _ in pairs),
        out_shape=tuple(jax.ShapeDtypeStruct((a.shape[1], n), GRAD_DTYPE) for a, _ in pairs),
        compiler_params=_cparams(1),
    )(*args)


def _mm_tn(a, b, name, tile=256):
    k, m = a.shape
    n = b.shape[1]
    dims = (((0,), (0,)), ((), ()))

    def body(a_ref, b_ref, o_ref):
        o_ref[...] = lax.dot_general(a_ref[...].astype(BF16), b_ref[...].astype(BF16), dims,
                                     preferred_element_type=F32).astype(GRAD_DTYPE)

    if n <= m:
        t = min(tile, m)
        grid, a_spec, b_spec = (m // t,), pl.BlockSpec((k, t), lambda i: (0, i)), _resident((k, n), lambda i: (0, 0))
        o_spec = pl.BlockSpec((t, n), lambda i: (i, 0))
    else:
        t = min(tile, n)
        grid, a_spec, b_spec = (n // t,), _resident((k, m), lambda i: (0, 0)), pl.BlockSpec((k, t), lambda i: (0, i))
        o_spec = pl.BlockSpec((m, t), lambda i: (0, i))
    return pl.pallas_call(
        body, name=name, grid=grid, in_specs=[a_spec, b_spec], out_specs=o_spec,
        out_shape=jax.ShapeDtypeStruct((m, n), GRAD_DTYPE), compiler_params=_cparams(1),
    )(a, b)


def _norm_rope(t, gain, c, s1, s2, seg):
    rs = lax.rsqrt(_seg_sum(t * t, seg) * (1.0 / HEAD_DIM) + EPS)
    return _rope((t * rs) * gain, c, s1, s2)


def _dup_half(y, half):
    lane = lax.broadcasted_iota(jnp.int32, y.shape, 1)
    rolled = pltpu.roll(y, HEAD_DIM, 1)
    keep = (lane < HEAD_DIM) if half == 0 else (lane >= HEAD_DIM)
    return jnp.where(keep, y, rolled)


def _qk_prep(proj, cb0, d, gqa, gq, gk, tabs, name):
    s_len = proj.shape[0]
    tm = 512
    rows = tm // d
    n_units = 4 if gqa else 2 * d
    n_q = 4 if gqa else 2
    n_in = 6

    def body(*refs):
        in_refs = refs[:n_in]
        gq_ref, gk_ref, c_ref, s1_ref, s2_ref, o_ref = refs[n_in:]
        seg = _seg_matrix(HEAD_DIM)

        def rows_of(ref, r):
            return ref[...] if d == 1 else ref[pl.ds(r, rows, stride=d), :]

        def put(unit_col, y):
            o_ref[:, unit_col * LANES:(unit_col + 1) * LANES] = y.astype(BF16)

        for r in range(d):
            c, s1, s2 = rows_of(c_ref, r), rows_of(s1_ref, r), rows_of(s2_ref, r)
            for b in range(n_in):
                t = rows_of(in_refs[b], r)
                if b < n_q:
                    put((b * d + r) if not gqa else b, _norm_rope(t, gq_ref[...], c, s1, s2, seg))
                elif not gqa:
                    sec, pair = (1, b - 2) if b < 4 else (2, b - 4)
                    y = _norm_rope(t, gk_ref[...], c, s1, s2, seg) if sec == 1 else t
                    put(sec * n_units + pair * d + r, y)
                else:
                    sec = 1 if b == 4 else 2
                    y = _norm_rope(t, gk_ref[...], c, s1, s2, seg) if sec == 1 else t
                    for u in range(n_units):
                        put(sec * n_units + u, _dup_half(y, u // 2))

    in_specs = [pl.BlockSpec((tm, LANES), lambda i, b=b: (i, cb0 + b)) for b in range(n_in)]
    vec = pl.BlockSpec((1, LANES), lambda i: (0, 0))
    tab = pl.BlockSpec((tm, LANES), lambda i: (i, 0))
    width = 3 * n_units * LANES
    return pl.pallas_call(
        body, name=name, grid=(s_len // tm,), in_specs=in_specs + [vec, vec, tab, tab, tab],
        out_specs=pl.BlockSpec((rows, width), lambda i: (i, 0)),
        out_shape=jax.ShapeDtypeStruct((s_len // d, width), BF16), compiler_params=_cparams(1),
    )(*([proj] * n_in), gq, gk, *tabs)


def _qk_prep_bwd(dqkv, proj, cb0, d, gqa, gq, gk, tabs, name):
    s_len = proj.shape[0]
    tm = 512
    rows = tm // d
    n_units = 4 if gqa else 2 * d
    n_q = 4 if gqa else 2
    n_in = 6

    def body(*refs):
        d_refs = refs[0:3]
        in_refs = refs[3:3 + n_in]
        gq_ref, gk_ref, c_ref, s1_ref, s2_ref, o_ref, dgq_ref, dgk_ref, stage = refs[3 + n_in:]
        seg = _seg_matrix(HEAD_DIM)

        @pl.when(pl.program_id(0) == 0)
        def _():
            dgq_ref[...] = jnp.zeros_like(dgq_ref)
            dgk_ref[...] = jnp.zeros_like(dgk_ref)

        def rows_of(ref, r):
            return ref[...] if d == 1 else ref[pl.ds(r, rows, stride=d), :]

        def unit(col):
            sec, u = divmod(col, n_units)
            return d_refs[sec][:, u * LANES:(u + 1) * LANES]

        def norm_bwd(dyr, t, gain, c, s1, s2, dg_ref):
            rs = lax.rsqrt(_seg_sum(t * t, seg) * (1.0 / HEAD_DIM) + EPS)
            that = t * rs
            dy = _unrope(dyr, c, s1, s2)
            dg_ref[...] += jnp.sum(dy * that, axis=0, keepdims=True)
            dthat = dy * gain
            return rs * (dthat - that * (_seg_sum(dthat * that, seg) * (1.0 / HEAD_DIM)))

        def fold(sec):
            tot = []
            for u in range(n_units):
                v = unit(sec * n_units + u)
                tot.append(v + pltpu.roll(v, HEAD_DIM, 1))
            lane = lax.broadcasted_iota(jnp.int32, tot[0].shape, 1)
            return jnp.where(lane < HEAD_DIM, tot[0] + tot[1], tot[2] + tot[3])

        for b in range(n_in):
            for r in range(d):
                c, s1, s2 = rows_of(c_ref, r), rows_of(s1_ref, r), rows_of(s2_ref, r)
                t = rows_of(in_refs[b], r)
                if b < n_q:
                    g = unit((b * d + r) if not gqa else b)
                    out = norm_bwd(g, t, gq_ref[...], c, s1, s2, dgq_ref)
                elif not gqa:
                    sec, pair = (1, b - 2) if b < 4 else (2, b - 4)
                    g = unit(sec * n_units + pair * d + r)
                    out = norm_bwd(g, t, gk_ref[...], c, s1, s2, dgk_ref) if sec == 1 else g
                else:
                    sec = 1 if b == 4 else 2
                    g = fold(sec)
                    out = norm_bwd(g, t, gk_ref[...], c, s1, s2, dgk_ref) if sec == 1 else g
                if d == 1:
                    o_ref[:, b * LANES:(b + 1) * LANES] = out.astype(BF16)
                else:
                    stage[pl.ds(r, rows, stride=d), :] = out
            if d != 1:
                o_ref[:, b * LANES:(b + 1) * LANES] = stage[...].astype(BF16)

    in_specs = [pl.BlockSpec((rows, n_units * LANES), lambda i: (i, 0))] * 3
    in_specs += [pl.BlockSpec((tm, LANES), lambda i, b=b: (i, cb0 + b)) for b in range(n_in)]
    vec = pl.BlockSpec((1, LANES), lambda i: (0, 0))
    tab = pl.BlockSpec((tm, LANES), lambda i: (i, 0))
    return pl.pallas_call(
        body, name=name, grid=(s_len // tm,), in_specs=in_specs + [vec, vec, tab, tab, tab],
        out_specs=(pl.BlockSpec((tm, n_in * LANES), lambda i: (i, 0)), vec, vec),
        out_shape=(jax.ShapeDtypeStruct((s_len, n_in * LANES), BF16), jax.ShapeDtypeStruct((1, LANES), F32),
                   jax.ShapeDtypeStruct((1, LANES), F32)),
        scratch_shapes=[pltpu.VMEM((tm, LANES), F32)], compiler_params=_cparams(1),
    )(*dqkv, *([proj] * n_in), gq, gk, *tabs)


def _head_masks(shape):
    lane = lax.broadcasted_iota(jnp.int32, shape, 1)
    return lane < HEAD_DIM, lane >= HEAD_DIM


def _band_fwd(qkv, n_units, max_dist, sinks, name):
    n_rows = qkv.shape[0]
    nb = n_rows // BLOCK
    scale = HEAD_DIM ** -0.5
    has_sink = sinks is not None
    assert not has_sink or max_dist < BLOCK

    qn, un = min(nb, BAND_Q_BLOCKS), BAND_UNITS
    ug = n_units // un

    def body(*refs):
        q_ref, kp_ref, km_ref, vp_ref, vm_ref = refs[:5]
        o_ref, lse_ref = refs[-2:]
        i = pl.program_id(1)
        qi = lax.broadcasted_iota(jnp.int32, (BLOCK, 2 * BLOCK), 0)
        kj = lax.broadcasted_iota(jnp.int32, (BLOCK, 2 * BLOCK), 1)
        dist = qi + BLOCK - kj
        band = (dist >= 0) & (dist <= max_dist)
        band_first = band & ((i > 0) | (kj >= BLOCK))
        m0, m1 = _head_masks((BLOCK, LANES))
        zero = jnp.zeros((BLOCK, LANES), BF16)
        for ub in range(un):
            cs = slice(ub * LANES, (ub + 1) * LANES)
            for qb in range(qn):
                rs = slice(qb * BLOCK, (qb + 1) * BLOCK)
                q = q_ref[rs, cs]
                if qb == 0:
                    kk = jnp.concatenate([kp_ref[:, cs], km_ref[0:BLOCK, cs]], axis=0)
                    vv = jnp.concatenate([vp_ref[:, cs], vm_ref[0:BLOCK, cs]], axis=0)
                    valid = band_first
                else:
                    kk = km_ref[(qb - 1) * BLOCK:(qb + 1) * BLOCK, cs]
                    vv = vm_ref[(qb - 1) * BLOCK:(qb + 1) * BLOCK, cs]
                    valid = band
                outs, lses = [], []
                for e, hm in enumerate((m0, m1)):
                    qe = jnp.where(hm, q, zero)
                    s = lax.dot_general(qe, kk, (((1,), (1,)), ((), ())), preferred_element_type=F32) * scale
                    s = jnp.where(valid, s, -jnp.inf)
                    if has_sink:
                        s = jnp.where(kj == 0, refs[5][ub][:, e * HEAD_DIM:e * HEAD_DIM + 1], s)
                    mx = jnp.max(s, axis=-1, keepdims=True)
                    p = jnp.exp(s - mx)
                    den = jnp.sum(p, axis=-1, keepdims=True)
                    pn = p * (1.0 / den)
                    if has_sink:
                        pn = jnp.where(kj == 0, 0.0, pn)
                    pn = pn.astype(BF16)
                    outs.append(jnp.dot(pn, vv, preferred_element_type=F32))
                    lses.append(mx + jnp.log(den))
                o_ref[rs, cs] = jnp.where(m0, outs[0], outs[1])
                lse_ref[rs, cs] = jnp.where(m0, jnp.broadcast_to(lses[0], (BLOCK, LANES)),
                                            jnp.broadcast_to(lses[1], (BLOCK, LANES)))

    def main(sec):
        return pl.BlockSpec((qn * BLOCK, un * LANES), lambda u, i: (i, sec * ug + u))

    def prev(sec):
        return pl.BlockSpec((BLOCK, un * LANES), lambda u, i: (jnp.maximum(i * qn - 1, 0), sec * ug + u))

    in_specs = [main(0), prev(1), main(1), prev(2), main(2)]
    args = [qkv] * 5
    if has_sink:
        in_specs.append(pl.BlockSpec((un, 1, LANES), lambda u, i: (u, 0, 0)))
        args.append(sinks)
    return pl.pallas_call(
        body, name=name, grid=(ug, nb // qn), in_specs=in_specs, out_specs=(main(0), main(0)),
        out_shape=(jax.ShapeDtypeStruct((n_rows, n_units * LANES), F32),) * 2, compiler_params=_cparams(2),
    )(*args)


def _band_bwd(qkv, do, lse, delta, n_units, max_dist, name):
    n_rows = qkv.shape[0]
    nb = n_rows // BLOCK
    scale = HEAD_DIM ** -0.5

    qn, un = min(nb, BAND_Q_BLOCKS), BAND_UNITS
    ug = n_units // un
    steps = nb // qn
    nt_dims = (((1,), (1,)), ((), ()))
    tn_dims = (((0,), (0,)), ((), ()))

    def body(qm_ref, qx_ref, kp_ref, km_ref, vp_ref, vm_ref, dom_ref, dox_ref, lm_ref, lx_ref, dm_ref, dx_ref,
             dq_ref, dk_ref, dv_ref):
        i = pl.program_id(1)
        m0, m1 = _head_masks((BLOCK, LANES))
        zero = jnp.zeros((BLOCK, LANES), BF16)
        qi = lax.broadcasted_iota(jnp.int32, (BLOCK, 2 * BLOCK), 0)
        kj = lax.broadcasted_iota(jnp.int32, (BLOCK, 2 * BLOCK), 1)
        dist = qi + BLOCK - kj
        band = (dist >= 0) & (dist <= max_dist)
        band_first = band & ((i > 0) | (kj >= BLOCK))
        qr = lax.broadcasted_iota(jnp.int32, (BLOCK, BLOCK), 0)
        kc = lax.broadcasted_iota(jnp.int32, (BLOCK, BLOCK), 1)
        dist_x = qr + BLOCK - kc
        band_next = (dist_x >= 0) & (dist_x <= max_dist) & (i < steps - 1)

        def pair(q, dob, lse_b, del_b, kk, vv, valid):
            dqs, dk, dv = [], None, None
            for e, hm in enumerate((m0, m1)):
                col = slice(e * HEAD_DIM, e * HEAD_DIM + 1)
                qe = jnp.where(hm, q, zero)
                doe = jnp.where(hm, dob, zero)
                s = lax.dot_general(qe, kk, nt_dims, preferred_element_type=F32) * scale
                p = jnp.where(valid, jnp.exp(s - lse_b[:, col]), 0.0)
                dp = lax.dot_general(doe, vv, nt_dims, preferred_element_type=F32)
                ds = (p * (dp - del_b[:, col]) * scale).astype(BF16)
                dqs.append(jnp.dot(ds, kk, preferred_element_type=F32))
                dk_e = lax.dot_general(ds, qe, tn_dims, preferred_element_type=F32)
                dv_e = lax.dot_general(p.astype(BF16), doe, tn_dims, preferred_element_type=F32)
                dk = dk_e if dk is None else dk + dk_e
                dv = dv_e if dv is None else dv + dv_e
            return jnp.where(m0, dqs[0], dqs[1]), dk, dv

        for ub in range(un):
            cs = slice(ub * LANES, (ub + 1) * LANES)
            dk_acc, dv_acc = [None] * qn, [None] * qn

            def add(acc, kb, part):
                acc[kb] = part if acc[kb] is None else acc[kb] + part

            for qb in range(qn):
                rs = slice(qb * BLOCK, (qb + 1) * BLOCK)
                if qb == 0:
                    kk = jnp.concatenate([kp_ref[:, cs], km_ref[0:BLOCK, cs]], axis=0)
                    vv = jnp.concatenate([vp_ref[:, cs], vm_ref[0:BLOCK, cs]], axis=0)
                    valid = band_first
                else:
                    kk = km_ref[(qb - 1) * BLOCK:(qb + 1) * BLOCK, cs]
                    vv = vm_ref[(qb - 1) * BLOCK:(qb + 1) * BLOCK, cs]
                    valid = band
                dq, dk, dv = pair(qm_ref[rs, cs], dom_ref[rs, cs], lm_ref[rs, cs], dm_ref[rs, cs], kk, vv, valid)
                dq_ref[rs, cs] = dq
                if qb > 0:
                    add(dk_acc, qb - 1, dk[0:BLOCK])
                    add(dv_acc, qb - 1, dv[0:BLOCK])
                add(dk_acc, qb, dk[BLOCK:2 * BLOCK])
                add(dv_acc, qb, dv[BLOCK:2 * BLOCK])
            last = slice((qn - 1) * BLOCK, qn * BLOCK)
            _, dk, dv = pair(qx_ref[:, cs], dox_ref[:, cs], lx_ref[:, cs], dx_ref[:, cs], km_ref[last, cs], vm_ref[last, cs],
                             band_next)
            add(dk_acc, qn - 1, dk)
            add(dv_acc, qn - 1, dv)
            for kb in range(qn):
                dk_ref[kb * BLOCK:(kb + 1) * BLOCK, cs] = dk_acc[kb]
                dv_ref[kb * BLOCK:(kb + 1) * BLOCK, cs] = dv_acc[kb]

    def main(sec):
        return pl.BlockSpec((qn * BLOCK, un * LANES), lambda u, i: (i, sec * ug + u))

    def prev(sec):
        return pl.BlockSpec((BLOCK, un * LANES), lambda u, i: (jnp.maximum(i * qn - 1, 0), sec * ug + u))

    def nxt(sec):
        return pl.BlockSpec((BLOCK, un * LANES), lambda u, i: (jnp.minimum((i + 1) * qn, nb - 1), sec * ug + u))

    in_specs = [main(0), nxt(0), prev(1), main(1), prev(2), main(2),
                main(0), nxt(0), main(0), nxt(0), main(0), nxt(0)]
    args = [qkv] * 6 + [do, do, lse, lse, delta, delta]
    shp = jax.ShapeDtypeStruct((n_rows, n_units * LANES), F32)
    return pl.pallas_call(
        body, name=name, grid=(ug, steps), in_specs=in_specs, out_specs=(main(0), main(0), main(0)),
        out_shape=(shp, shp, shp), compiler_params=_cparams(2),
    )(*args)


def _merge_groups(os_, lses, dils, name):
    s_len = os_[0].shape[0] * dils[0]
    tm = 512

    def body(*refs):
        o_refs, l_refs = refs[0:3], refs[3:6]
        o_ref, lse_ref = refs[6:8]
        so, sl = refs[8:11], refs[11:14]
        for pair in range(2):
            for g, d in enumerate(dils):
                rows = tm // d
                for r in range(d):
                    col = slice((pair * d + r) * LANES, (pair * d + r + 1) * LANES)
                    if d == 1:
                        so[g][...] = o_refs[g][:, col]
                        sl[g][...] = l_refs[g][:, col]
                    else:
                        so[g][pl.ds(r, rows, stride=d), :] = o_refs[g][:, col]
                        sl[g][pl.ds(r, rows, stride=d), :] = l_refs[g][:, col]
            l0, l1, l2 = sl[0][...], sl[1][...], sl[2][...]
            mx = jnp.maximum(jnp.maximum(l0, l1), l2)
            e0, e1, e2 = jnp.exp(l0 - mx), jnp.exp(l1 - mx), jnp.exp(l2 - mx)
            den = e0 + e1 + e2
            inv = 1.0 / den
            o_ref[:, pair * LANES:(pair + 1) * LANES] = (so[0][...] * (e0 * inv) + so[1][...] * (e1 * inv)
                                                         + so[2][...] * (e2 * inv))
            lse_ref[:, pair * LANES:(pair + 1) * LANES] = mx + jnp.log(den)

    in_specs = [pl.BlockSpec((tm // d, 2 * d * LANES), lambda i: (i, 0)) for d in dils] * 2
    out = pl.BlockSpec((tm, 2 * LANES), lambda i: (i, 0))
    shp = jax.ShapeDtypeStruct((s_len, 2 * LANES), F32)
    return pl.pallas_call(
        body, name=name, grid=(s_len // tm,), in_specs=in_specs, out_specs=(out, out), out_shape=(shp, shp),
        scratch_shapes=[pltpu.VMEM((tm, LANES), F32)] * 6, compiler_params=_cparams(1),
    )(*os_, *lses)


def _bwd_prep(do, o, lse, dils, sinks, name):
    s_len, width = do.shape
    n_pairs = width // LANES
    tm = 512
    has_sink = sinks is not None
    n_g = len(dils)

    def body(*refs):
        do_ref, o_ref, lse_ref = refs[:3]
        pos = 3
        if has_sink:
            sink_ref = refs[pos]
            pos += 1
        outs = refs[pos:pos + 3 * n_g]
        pos += 3 * n_g
        if has_sink:
            dsink_ref = refs[pos]
            pos += 1
        s_do, s_l, s_d = refs[pos:pos + 3]
        seg = _seg_matrix(HEAD_DIM)

        if has_sink:
            @pl.when(pl.program_id(0) == 0)
            def _():
                dsink_ref[...] = jnp.zeros_like(dsink_ref)

        for pair in range(n_pairs):
            col = slice(pair * LANES, (pair + 1) * LANES)
            dov = do_ref[:, col]
            lv = lse_ref[:, col]
            delta = _seg_sum(dov * o_ref[:, col], seg)
            if has_sink:
                dsink_ref[pair] += -jnp.sum(jnp.exp(sink_ref[pair] - lv) * delta, axis=0, keepdims=True)
            s_do[...] = dov
            s_l[...] = lv
            s_d[...] = delta
            for g, d in enumerate(dils):
                rows = tm // d
                for r in range(d):
                    oc = slice((pair * d + r) * LANES, (pair * d + r + 1) * LANES)
                    if d == 1:
                        a, b, c = s_do[...], s_l[...], s_d[...]
                    else:
                        a = s_do[pl.ds(r, rows, stride=d), :]
                        b = s_l[pl.ds(r, rows, stride=d), :]
                        c = s_d[pl.ds(r, rows, stride=d), :]
                    outs[3 * g][:, oc] = a.astype(BF16)
                    outs[3 * g + 1][:, oc] = b
                    outs[3 * g + 2][:, oc] = c

    row = pl.BlockSpec((tm, width), lambda i: (i, 0))
    in_specs = [row, row, row]
    args = [do, o, lse]
    if has_sink:
        in_specs.append(pl.BlockSpec((n_pairs, 1, LANES), lambda i: (0, 0, 0)))
        args.append(sinks)
    out_specs, out_shape = [], []
    for d in dils:
        for dt in (BF16, F32, F32):
            out_specs.append(pl.BlockSpec((tm // d, n_pairs * d * LANES), lambda i: (i, 0)))
            out_shape.append(jax.ShapeDtypeStruct((s_len // d, n_pairs * d * LANES), dt))
    if has_sink:
        out_specs.append(pl.BlockSpec((n_pairs, 1, LANES), lambda i: (0, 0, 0)))
        out_shape.append(jax.ShapeDtypeStruct((n_pairs, 1, LANES), F32))
    return pl.pallas_call(
        body, name=name, grid=(s_len // tm,), in_specs=in_specs, out_specs=tuple(out_specs),
        out_shape=tuple(out_shape), scratch_shapes=[pltpu.VMEM((tm, LANES), F32)] * 3, compiler_params=_cparams(1),
    )(*args)


def _mem_kv(mem, mem_gain, w_kv, k_gain, name):
    m_len = mem.shape[0]
    kw = M_HEADS * M_HEAD_DIM

    def body(mem_ref, mg_ref, w_ref, kg_ref, k_ref, v_ref):
        mv = mem_ref[...]
        r = lax.rsqrt(jnp.mean(mv * mv, axis=-1, keepdims=True) + EPS)
        mn = ((mv * r) * mg_ref[...]).astype(BF16)
        kv = jnp.dot(mn, w_ref[...], preferred_element_type=F32)
        for h in range(M_HEADS):
            col = slice(h * M_HEAD_DIM, (h + 1) * M_HEAD_DIM)
            t = kv[:, col]
            rk = lax.rsqrt(jnp.mean(t * t, axis=-1, keepdims=True) + EPS)
            k_ref[:, col] = ((t * rk) * kg_ref[...]).astype(BF16)
        v_ref[...] = kv[:, kw:].astype(BF16)

    shp = jax.ShapeDtypeStruct((m_len, kw), BF16)
    return pl.pallas_call(body, name=name, out_shape=(shp, shp),
                          compiler_params=pltpu.CompilerParams(vmem_limit_bytes=VMEM_LIMIT_BYTES))(mem, mem_gain, w_kv, k_gain)


def _mem_kv_bwd(mem, mem_gain, w_kv, k_gain, dk, dv, name):
    m_len, d = mem.shape
    kw = M_HEADS * M_HEAD_DIM

    def body(mem_ref, mg_ref, w_ref, kg_ref, dk_ref, dv_ref, dw_ref, dmg_ref, dkg_ref, dkv_ref):
        mv = mem_ref[...]
        r = lax.rsqrt(jnp.mean(mv * mv, axis=-1, keepdims=True) + EPS)
        mhat = mv * r
        mn = (mhat * mg_ref[...]).astype(BF16)
        kv = jnp.dot(mn, w_ref[...], preferred_element_type=F32)
        dkg = jnp.zeros((1, M_HEAD_DIM), F32)
        for h in range(M_HEADS):
            col = slice(h * M_HEAD_DIM, (h + 1) * M_HEAD_DIM)
            t = kv[:, col]
            rk = lax.rsqrt(jnp.mean(t * t, axis=-1, keepdims=True) + EPS)
            that = t * rk
            dy = dk_ref[:, col]
            dkg = dkg + jnp.sum(dy * that, axis=0, keepdims=True)
            dthat = dy * kg_ref[...]
            dkv_ref[:, col] = (rk * (dthat - that * jnp.mean(dthat * that, axis=-1, keepdims=True))).astype(BF16)
        dkv_ref[:, kw:] = dv_ref[...].astype(BF16)
        dkg_ref[...] = dkg
        dkv = dkv_ref[...]
        dw_ref[...] = lax.dot_general(mn, dkv, (((0,), (0,)), ((), ())), preferred_element_type=F32).astype(GRAD_DTYPE)
        dmn = lax.dot_general(dkv, w_ref[...], (((1,), (1,)), ((), ())), preferred_element_type=F32)
        dmg_ref[...] = jnp.sum(dmn * mhat, axis=0, keepdims=True)

    return pl.pallas_call(
        body, name=name,
        out_shape=(jax.ShapeDtypeStruct((d, 2 * kw), GRAD_DTYPE), jax.ShapeDtypeStruct((1, d), F32),
                   jax.ShapeDtypeStruct((1, M_HEAD_DIM), F32)),
        scratch_shapes=[pltpu.VMEM((m_len, 2 * kw), BF16)],
        compiler_params=pltpu.CompilerParams(vmem_limit_bytes=VMEM_LIMIT_BYTES),
    )(mem, mem_gain, w_kv, k_gain, dk, dv)


def _mem_attn_fwd(proj, cidx, mk, mv, q_gain, name):
    s_len = proj.shape[0]
    kw = M_HEADS * M_HEAD_DIM
    tm = 512
    scale = M_HEAD_DIM ** -0.5

    def body(q_ref, k_ref, v_ref, g_ref, o_ref):
        for h in range(M_HEADS):
            col = slice(h * M_HEAD_DIM, (h + 1) * M_HEAD_DIM)
            t = q_ref[:, col]
            rs = lax.rsqrt(jnp.mean(t * t, axis=-1, keepdims=True) + EPS)
            qn = ((t * rs) * g_ref[...]).astype(BF16)
            s = lax.dot_general(qn, k_ref[:, col], (((1,), (1,)), ((), ())), preferred_element_type=F32) * scale
            mx = jnp.max(s, axis=-1, keepdims=True)
            p = jnp.exp(s - mx)
            pn = (p * (1.0 / jnp.sum(p, axis=-1, keepdims=True))).astype(BF16)
            o_ref[:, col] = jnp.dot(pn, v_ref[:, col], preferred_element_type=F32).astype(BF16)

    whole = pl.BlockSpec((MEM_LEN, kw), lambda i: (0, 0))
    return pl.pallas_call(
        body, name=name, grid=(s_len // tm,),
        in_specs=[pl.BlockSpec((tm, kw), lambda i: (i, cidx)), whole, whole, pl.BlockSpec((1, M_HEAD_DIM), lambda i: (0, 0))],
        out_specs=pl.BlockSpec((tm, kw), lambda i: (i, 0)),
        out_shape=jax.ShapeDtypeStruct((s_len, kw), BF16), compiler_params=_cparams(1),
    )(proj, mk, mv, q_gain)


def _mem_attn_bwd(proj, cidx, mk, mv, q_gain, do, name):
    s_len = proj.shape[0]
    kw = M_HEADS * M_HEAD_DIM
    tm = 512
    scale = M_HEAD_DIM ** -0.5

    def body(q_ref, k_ref, v_ref, g_ref, do_ref, dq_ref, dk_ref, dv_ref, dg_ref):
        @pl.when(pl.program_id(0) == 0)
        def _():
            dk_ref[...] = jnp.zeros_like(dk_ref)
            dv_ref[...] = jnp.zeros_like(dv_ref)
            dg_ref[...] = jnp.zeros_like(dg_ref)

        for h in range(M_HEADS):
            col = slice(h * M_HEAD_DIM, (h + 1) * M_HEAD_DIM)
            t = q_ref[:, col]
            rs = lax.rsqrt(jnp.mean(t * t, axis=-1, keepdims=True) + EPS)
            that = t * rs
            qn = (that * g_ref[...]).astype(BF16)
            kh, vh = k_ref[:, col], v_ref[:, col]
            dob = do_ref[:, col].astype(BF16)
            s = lax.dot_general(qn, kh, (((1,), (1,)), ((), ())), preferred_element_type=F32) * scale
            mx = jnp.max(s, axis=-1, keepdims=True)
            p = jnp.exp(s - mx)
            p = p * (1.0 / jnp.sum(p, axis=-1, keepdims=True))
            dp = lax.dot_general(dob, vh, (((1,), (1,)), ((), ())), preferred_element_type=F32)
            ds = (p * (dp - jnp.sum(p * dp, axis=-1, keepdims=True)) * scale).astype(BF16)
            dqn = jnp.dot(ds, kh, preferred_element_type=F32)
            dk_ref[:, col] += lax.dot_general(ds, qn, (((0,), (0,)), ((), ())), preferred_element_type=F32)
            dv_ref[:, col] += lax.dot_general(p.astype(BF16), dob, (((0,), (0,)), ((), ())), preferred_element_type=F32)
            dg_ref[...] += jnp.sum(dqn * that, axis=0, keepdims=True)
            dthat = dqn * g_ref[...]
            dq_ref[:, col] = (rs * (dthat - that * jnp.mean(dthat * that, axis=-1, keepdims=True))).astype(BF16)

    whole = pl.BlockSpec((MEM_LEN, kw), lambda i: (0, 0))
    vec = pl.BlockSpec((1, M_HEAD_DIM), lambda i: (0, 0))
    row = pl.BlockSpec((tm, kw), lambda i: (i, 0))
    return pl.pallas_call(
        body, name=name, grid=(s_len // tm,),
        in_specs=[pl.BlockSpec((tm, kw), lambda i: (i, cidx)), whole, whole, vec, row],
        out_specs=(row, whole, whole, vec),
        out_shape=(jax.ShapeDtypeStruct((s_len, kw), BF16), jax.ShapeDtypeStruct((MEM_LEN, kw), F32),
                   jax.ShapeDtypeStruct((MEM_LEN, kw), F32), jax.ShapeDtypeStruct((1, M_HEAD_DIM), F32)),
        compiler_params=_cparams(1),
    )(proj, mk, mv, q_gain, do)


def _project_merge(outs, w_outs, gates, w_out, x, name):
    s_len = gates.shape[0]
    d = w_outs[0].shape[1]
    tm = 512

    def body(oa_ref, ob_ref, om_ref, wa_ref, wb_ref, wm_ref, g_ref, wo_ref, x_ref,
             pa_ref, pb_ref, pm_ref, merged_ref, x1_ref):
        merged = None
        for k, (o_ref, w_ref, p_ref) in enumerate(((oa_ref, wa_ref, pa_ref), (ob_ref, wb_ref, pb_ref), (om_ref, wm_ref, pm_ref))):
            p = jnp.dot(o_ref[...].astype(BF16), w_ref[...], preferred_element_type=F32).astype(BF16)
            p_ref[...] = p
            t = g_ref[:, k * d:(k + 1) * d].astype(F32) * p.astype(F32)
            merged = t if merged is None else merged + t
        merged = merged.astype(BF16)
        merged_ref[...] = merged
        x1_ref[...] = x_ref[...] + jnp.dot(merged, wo_ref[...], preferred_element_type=F32)

    row = pl.BlockSpec((tm, d), lambda i: (i, 0))
    shp = jax.ShapeDtypeStruct((s_len, d), BF16)
    in_specs = [pl.BlockSpec((tm, o.shape[1]), lambda i: (i, 0)) for o in outs]
    in_specs += [_resident(w.shape, lambda i: (0, 0)) for w in w_outs]
    in_specs += [pl.BlockSpec((tm, 3 * d), lambda i: (i, 0)), _resident(w_out.shape, lambda i: (0, 0)), row]
    return pl.pallas_call(
        body, name=name, grid=(s_len // tm,), in_specs=in_specs, out_specs=(row, row, row, row, row),
        out_shape=(shp, shp, shp, shp, jax.ShapeDtypeStruct((s_len, d), F32)), compiler_params=_cparams(1),
    )(*outs, *w_outs, gates, w_out, x)


def _project_merge_bwd(dx1, w_out, gates, pa, pb, pm, name):
    s_len, d = pa.shape
    tm = 512

    def body(dx_ref, w_ref, g_ref, a_ref, b_ref, m_ref, da_ref, db_ref, dmm_ref, dg_ref, dbg_ref):
        @pl.when(pl.program_id(0) == 0)
        def _():
            dbg_ref[...] = jnp.zeros_like(dbg_ref)
        dm = lax.dot_general(dx_ref[...], w_ref[...], (((1,), (1,)), ((), ())), preferred_element_type=F32)
        for k, (p_ref, dp_ref) in enumerate(((a_ref, da_ref), (b_ref, db_ref), (m_ref, dmm_ref))):
            col = slice(k * d, (k + 1) * d)
            g = g_ref[:, col].astype(F32)
            dp_ref[...] = (dm * g).astype(BF16)
            dpre = (dm * p_ref[...].astype(F32)) * (g * (1.0 - g))
            dbg_ref[:, col] += jnp.sum(dpre, axis=0, keepdims=True)
            dg_ref[:, col] = dpre.astype(BF16)

    row = pl.BlockSpec((tm, d), lambda i: (i, 0))
    wide = pl.BlockSpec((tm, 3 * d), lambda i: (i, 0))
    shp = jax.ShapeDtypeStruct((s_len, d), BF16)
    return pl.pallas_call(
        body, name=name, grid=(s_len // tm,), in_specs=[row, _resident(w_out.shape, lambda i: (0, 0)), wide, row, row, row],
        out_specs=(row, row, row, wide, pl.BlockSpec((1, 3 * d), lambda i: (0, 0))),
        out_shape=(shp, shp, shp, jax.ShapeDtypeStruct((s_len, 3 * d), BF16), jax.ShapeDtypeStruct((1, 3 * d), F32)),
        compiler_params=_cparams(1),
    )(dx1, w_out, gates, pa, pb, pm)


CONV_CHUNK = 512


def _pick_row(tile, j):
    row = lax.broadcasted_iota(jnp.int32, tile.shape, 0)
    return jnp.sum(jnp.where(row == j, tile, jnp.zeros_like(tile)), axis=0, keepdims=True)


def _rows_before(ref, start, k):
    cur = ref[pl.ds(start, CONV_CHUNK), :].astype(F32)
    prev = ref[pl.ds(pl.multiple_of(jnp.maximum(start - 16, 0), 16), 16), :].astype(F32)
    prev = jnp.where(start > 0, prev, jnp.zeros_like(prev))
    rolled = pltpu.roll(cur, k, 0)
    row = lax.broadcasted_iota(jnp.int32, cur.shape, 0)
    for j in range(k):
        rolled = jnp.where(row == j, _pick_row(prev, 16 - k + j), rolled)
    return rolled


def _rows_after(ref, start, k):
    cur = ref[pl.ds(start, CONV_CHUNK), :]
    nxt = ref[pl.ds(pl.multiple_of(start + CONV_CHUNK, 8), 8), :]
    rolled = pltpu.roll(cur, CONV_CHUNK - k, 0)
    row = lax.broadcasted_iota(jnp.int32, cur.shape, 0)
    for j in range(k):
        rolled = jnp.where(row == CONV_CHUNK - k + j, _pick_row(nxt, j), rolled)
    return rolled


def _conv_pre(u_ref, w_ref, b_ref, start):
    u2 = _rows_before(u_ref, start, 2)
    u1 = _rows_before(u_ref, start, 1)
    u0 = u_ref[pl.ds(start, CONV_CHUNK), :].astype(F32)
    c = ((b_ref[...] + w_ref[0:1, :] * u2) + w_ref[1:2, :] * u1) + w_ref[2:3, :] * u0
    return c, (u2, u1, u0)


def _norm_up_conv_glu(x, gain, w_up, conv_w, conv_b, name):
    s_len, d = x.shape
    tm, tn = 512, 2 * LANES
    nblk = D_FF // tn

    def body(x_ref, g_ref, w_ref, cw_ref, cb_ref, ht_ref, r_ref, u_ref, act_ref, halo):
        @pl.when(pl.program_id(0) == 0)
        def _():
            halo[...] = jnp.zeros_like(halo)
        xv = x_ref[...]
        r = lax.rsqrt(jnp.mean(xv * xv, axis=-1, keepdims=True) + EPS)
        hv = ((xv * r) * g_ref[...]).astype(BF16)
        ht_ref[...] = hv.T
        r_ref[...] = r
        row = lax.broadcasted_iota(jnp.int32, (tm, tn), 0)
        for j in range(nblk):
            conv = []
            for half in range(2):
                cb = half * nblk + j
                cols = slice(cb * tn, (cb + 1) * tn)
                ub = jnp.dot(hv, w_ref[:, cols], preferred_element_type=F32).astype(BF16)
                u_ref[:, cols] = ub
                u0 = ub.astype(F32)
                prev = halo[cb]
                u1 = jnp.where(row == 0, _pick_row(prev, 7), pltpu.roll(u0, 1, 0))
                u2 = pltpu.roll(u0, 2, 0)
                u2 = jnp.where(row == 0, _pick_row(prev, 6), jnp.where(row == 1, _pick_row(prev, 7), u2))
                halo[cb] = u0[tm - 8:tm, :]
                conv.append(((cb_ref[:, cols] + cw_ref[0:1, cols] * u2) + cw_ref[1:2, cols] * u1)
                            + cw_ref[2:3, cols] * u0)
            act_ref[:, j * tn:(j + 1) * tn] = ((conv[0] * _sigmoid(conv[0])) * conv[1]).astype(BF16)

    return pl.pallas_call(
        body, name=name, grid=(s_len // tm,),
        in_specs=[pl.BlockSpec((tm, d), lambda i: (i, 0)), _resident((1, d), lambda i: (0, 0)),
                  _resident((d, 2 * D_FF), lambda i: (0, 0)),
                  _resident((3, 2 * D_FF), lambda i: (0, 0)), _resident((1, 2 * D_FF), lambda i: (0, 0))],
        out_specs=(pl.BlockSpec((d, tm), lambda i: (0, i)), pl.BlockSpec((tm, 1), lambda i: (i, 0)),
                   pl.BlockSpec((tm, 2 * D_FF), lambda i: (i, 0)), pl.BlockSpec((tm, D_FF), lambda i: (i, 0))),
        out_shape=(jax.ShapeDtypeStruct((d, s_len), BF16), jax.ShapeDtypeStruct((s_len, 1), F32),
                   jax.ShapeDtypeStruct((s_len, 2 * D_FF), BF16), jax.ShapeDtypeStruct((s_len, D_FF), BF16)),
        scratch_shapes=[pltpu.VMEM((2 * nblk, 8, tn), F32)], compiler_params=_cparams(1),
    )(x, gain, w_up, conv_w, conv_b)


def _conv_glu_bwd(dact, u, conv_w, conv_b, name):
    s_len = u.shape[0]
    nblk = D_FF // LANES
    n_chunks = s_len // CONV_CHUNK

    def body(da_ref, ua_ref, ug_ref, wa_ref, wg_ref, ba_ref, bg_ref,
             dua_ref, dug_ref, dwa_ref, dwg_ref, dba_ref, dbg_ref, sa, sg):
        sa[pl.ds(s_len, 8), :] = jnp.zeros((8, LANES), F32)
        sg[pl.ds(s_len, 8), :] = jnp.zeros((8, LANES), F32)
        zero = jnp.zeros((1, LANES), F32)

        def chunk1(ci, carry):
            start = pl.multiple_of(ci * CONV_CHUNK, CONV_CHUNK)
            ca, ua = _conv_pre(ua_ref, wa_ref, ba_ref, start)
            cg, ug = _conv_pre(ug_ref, wg_ref, bg_ref, start)
            dact_v = da_ref[pl.ds(start, CONV_CHUNK), :].astype(F32)
            sig = _sigmoid(ca)
            dcg = dact_v * (ca * sig)
            dca = (dact_v * cg) * (sig * (1.0 + ca * (1.0 - sig)))
            sa[pl.ds(start, CONV_CHUNK), :] = dca
            sg[pl.ds(start, CONV_CHUNK), :] = dcg
            out = [carry[0] + jnp.sum(dca, axis=0, keepdims=True), carry[1] + jnp.sum(dcg, axis=0, keepdims=True)]
            for j in range(3):
                out.append(carry[2 + j] + jnp.sum(dca * ua[j], axis=0, keepdims=True))
            for j in range(3):
                out.append(carry[5 + j] + jnp.sum(dcg * ug[j], axis=0, keepdims=True))
            return tuple(out)

        acc = lax.fori_loop(0, n_chunks, chunk1, (zero,) * 8)
        dba_ref[...] = acc[0]
        dbg_ref[...] = acc[1]
        for j in range(3):
            dwa_ref[j:j + 1, :] = acc[2 + j]
            dwg_ref[j:j + 1, :] = acc[5 + j]

        def chunk2(ci, carry):
            start = pl.multiple_of(ci * CONV_CHUNK, CONV_CHUNK)
            for s_ref, w_ref, o_ref in ((sa, wa_ref, dua_ref), (sg, wg_ref, dug_ref)):
                d0 = s_ref[pl.ds(start, CONV_CHUNK), :]
                d1 = _rows_after(s_ref, start, 1)
                d2 = _rows_after(s_ref, start, 2)
                o_ref[pl.ds(start, CONV_CHUNK), :] = (w_ref[2:3, :] * d0 + w_ref[1:2, :] * d1
                                                      + w_ref[0:1, :] * d2).astype(BF16)
            return carry
        lax.fori_loop(0, n_chunks, chunk2, 0)

    def col(rows, off):
        return pl.BlockSpec((rows, LANES), lambda j: (0, off + j))

    big = jax.ShapeDtypeStruct((s_len, D_FF), BF16)
    return pl.pallas_call(
        body, name=name, grid=(nblk,),
        in_specs=[col(s_len, 0), col(s_len, 0), col(s_len, nblk), col(3, 0), col(3, nblk), col(1, 0), col(1, nblk)],
        out_specs=(col(s_len, 0), col(s_len, 0), col(3, 0), col(3, 0), col(1, 0), col(1, 0)),
        out_shape=(big, big, jax.ShapeDtypeStruct((3, D_FF), F32), jax.ShapeDtypeStruct((3, D_FF), F32),
                   jax.ShapeDtypeStruct((1, D_FF), F32), jax.ShapeDtypeStruct((1, D_FF), F32)),
        scratch_shapes=[pltpu.VMEM((s_len + 8, LANES), F32)] * 2, compiler_params=_cparams(1),
    )(dact, u, u, conv_w, conv_w, conv_b, conv_b)


def _rope_tables(positions):
    half = ROPE_DIMS // 2
    freqs = jnp.exp(jnp.arange(half, dtype=F32) * (-2.0 * math.log(ROPE_THETA) / ROPE_DIMS))
    ang = positions.reshape(-1).astype(F32)[:, None] * freqs
    cos, sin = jnp.cos(ang), jnp.sin(ang)
    n = ang.shape[0]
    zeros = lambda w: jnp.zeros((n, w), F32)
    c = jnp.concatenate([cos, cos, jnp.ones((n, HEAD_DIM - ROPE_DIMS), F32)], axis=1)
    s1 = jnp.concatenate([-sin, zeros(HEAD_DIM - half)], axis=1)
    s2 = jnp.concatenate([zeros(half), sin, zeros(HEAD_DIM - ROPE_DIMS)], axis=1)
    return tuple(jnp.tile(t, (1, 2)) for t in (c, s1, s2))


def _two(v):
    return jnp.tile(v.reshape(1, HEAD_DIM), (1, 2))


def _fold_heads(g):
    return g[0, :HEAD_DIM] + g[0, HEAD_DIM:]


MIX_WEIGHTS = ('w_gate', 'w_mem_kv', 'w_o_a', 'w_o_b', 'w_o_m', 'w_out')
FFN_WEIGHTS = ('w_up', 'conv_w', 'w_down')


def _device_step(x, mem, positions, target, w, hooks=None):
    tabs = _rope_tables(positions)
    dils = tuple(d for _, d in A_GROUPS)
    grads = {}
    w = dict(w)

    h, h_t, r1 = _rms_fwd(x, w['attn_norm'], "rms1")
    if hooks is not None:
        w.update(hooks.weights('in', [h, *tabs]))
    proj = _mm_rows([(h, w['w_in'], 0)], "mm_in")

    qkv_a, o_g, lse_g = [], [], []
    for gi, (window, d) in enumerate(A_GROUPS):
        gq, gk = _two(w['a_q_norm'][gi]), _two(w['a_k_norm'][gi])
        qkv = _qk_prep(proj, 6 * gi, d, False, gq, gk, tabs, f"qk_prep_a{gi}")
        o, lse = _band_fwd(qkv, 2 * d, window // d, None, f"band_fwd_a{gi}")
        qkv_a.append(qkv)
        o_g.append(o)
        lse_g.append(lse)
    o_a, lse_a = _merge_groups(o_g, lse_g, dils, "merge_a")
    if hooks is not None:
        w.update(hooks.weights('mix', [o_a]))

    gbq, gbk = _two(w['b_q_norm']), _two(w['b_k_norm'])
    sinks = jnp.repeat(w['b_sinks'].reshape(4, 2), HEAD_DIM, axis=1).reshape(4, 1, LANES)
    qkv_b = _qk_prep(proj, 18, 1, True, gbq, gbk, tabs, "qk_prep_b")
    o_b, lse_b = _band_fwd(qkv_b, 4, B_WINDOW - 1, sinks, "band_fwd_b")

    gates = _mm_rows([(h, w['w_gate'], 0)], "mm_gate", bias=w['b_gate'], sigmoid=True, out_dtypes=(BF16,))
    mk, mv = _mem_kv(mem, w['mem_norm'], w['w_mem_kv'], w['m_k_norm'], "mem_kv")
    o_m = _mem_attn_fwd(proj, 6, mk, mv, w['m_q_norm'], "mem_attn")

    pa, pb, pm, merged, x1 = _project_merge((o_a, o_b, o_m), (w['w_o_a'], w['w_o_b'], w['w_o_m']), gates, w['w_out'], x,
                                            "project_merge")

    if hooks is not None:
        w.update(hooks.weights('ffn', [x1]))
    h2_t, r2, u, act = _norm_up_conv_glu(x1, w['ffn_norm'], w['w_up'], w['conv_w'], w['conv_b'], "norm_up_conv_glu")
    dy, dy_b, dact, loss = _mm_rows([(act, w['w_down'], 0)], "mm_down", res=x1, loss_target=target)

    grads['w_down'] = _mm_tn(act, dy_b, "mm_dw_down")
    du_a, du_g, dcw_a, dcw_g, dcb_a, dcb_g = _conv_glu_bwd(dact, u, w['conv_w'], w['conv_b'], "conv_glu_bwd")
    grads['conv_w'] = jnp.concatenate([dcw_a, dcw_g], axis=1)
    grads['conv_b'] = jnp.concatenate([dcb_a, dcb_g], axis=1)
    grads['w_up'] = _mm_cols(h2_t, [du_a, du_g], "mm_dw_up")
    ffn_gain = w['ffn_norm']
    if hooks is not None:
        ffn_gain = ffn_gain + hooks.grads('ffn', grads)[0:1, 0:1]
    dx1, dx1_b, grads['ffn_norm'] = _mm_rows([(du_a, w['w_up'], 0), (du_g, w['w_up'], 1)], "mm_d_h2", nt=True,
                                             rms_bwd=(x1, r2, ffn_gain, dy), out_dtypes=(F32, BF16))

    grads['w_out'] = _mm_tn(merged, dx1_b, "mm_dw_out")
    dpa, dpb, dpm, dgpre, grads['b_gate'] = _project_merge_bwd(dx1_b, w['w_out'], gates, pa, pb, pm,
                                                               "project_merge_bwd")
    do_a, do_b, do_m = _mm_rows_each([(dpa, w['w_o_a']), (dpb, w['w_o_b']), (dpm, w['w_o_m'])], "mm_d_o")
    grads['w_o_a'], grads['w_o_b'], grads['w_o_m'] = _mm_tn_each([(o_a, dpa), (o_b, dpb), (o_m, dpm)], "mm_dw_o")
    grads['w_gate'] = _mm_cols(h_t, [dgpre], "mm_dw_gate")
    dq_m, dmk, dmv, grads['m_q_norm'] = _mem_attn_bwd(proj, 6, mk, mv, w['m_q_norm'], do_m, "mem_attn_bwd")
    grads['w_mem_kv'], grads['mem_norm'], grads['m_k_norm'] = _mem_kv_bwd(
        mem, w['mem_norm'], w['w_mem_kv'], w['m_k_norm'], dmk, dmv, "mem_kv_bwd")
    a_gain = w['a_q_norm']
    if hooks is not None:
        a_gain = a_gain + hooks.grads('mix', grads)[0:1, 0:1]

    prep = _bwd_prep(do_a, o_a, lse_a, dils, None, "bwd_prep_a")
    dproj, dgq_a, dgk_a = [], [], []
    for gi, (window, d) in enumerate(A_GROUPS):
        gq, gk = _two(a_gain[gi]), _two(w['a_k_norm'][gi])
        dqkv = _band_bwd(qkv_a[gi], prep[3 * gi], prep[3 * gi + 1], prep[3 * gi + 2], 2 * d, window // d,
                         f"band_bwd_a{gi}")
        dp, dgq, dgk = _qk_prep_bwd(dqkv, proj, 6 * gi, d, False, gq, gk, tabs, f"qk_prep_bwd_a{gi}")
        dproj.append(dp)
        dgq_a.append(_fold_heads(dgq))
        dgk_a.append(_fold_heads(dgk))
    grads['a_q_norm'] = jnp.stack(dgq_a)
    grads['a_k_norm'] = jnp.stack(dgk_a)

    do_bu, lse_bu, delta_bu, dsink = _bwd_prep(do_b, o_b, lse_b, (1,), sinks, "bwd_prep_b")
    dqkv = _band_bwd(qkv_b, do_bu, lse_bu, delta_bu, 4, B_WINDOW - 1, "band_bwd_b")
    dp_b, dgq, dgk = _qk_prep_bwd(dqkv, proj, 18, 1, True, gbq, gbk, tabs, "qk_prep_bwd_b")
    dproj.append(dp_b)
    grads['b_q_norm'] = _fold_heads(dgq)
    grads['b_k_norm'] = _fold_heads(dgk)
    grads['b_sinks'] = jnp.stack([dsink[:, 0, 0], dsink[:, 0, HEAD_DIM]], axis=1).reshape(8)

    dproj.append(dq_m)

    cols = (0, 1, 2, 3, 6)
    grads['w_in'] = _mm_rows_cat(h_t, dproj, "mm_dw_in")
    attn_gain = w['attn_norm']
    if hooks is not None:
        attn_gain = attn_gain + hooks.grads('in', grads)[0:1, 0:1]
    grad_x, grads['attn_norm'] = _mm_rows(
        [(dp, w['w_in'], c) for dp, c in zip(dproj, cols)] + [(dgpre, w['w_gate'], 0)], "mm_d_h", nt=True,
        rms_bwd=(x, r1, attn_gain, dx1))
    return loss, grad_x, grads


def _coords():
    return lax.axis_index("x"), lax.axis_index("y"), lax.axis_index("c")


def _slot(p):
    return 4 * p[0] + 2 * p[1] + p[2]


ALL_PEERS = tuple(range(1, N_DEV))
CHIP_PEERS = (1, 4, 2, 6)
OTHER_CHIPS = (4, 2, 6)


def _peers(me, masks=ALL_PEERS):
    x, y, c = me
    return [(1 - x if mask & 4 else x, 1 - y if mask & 2 else y, 1 - c if mask & 1 else c) for mask in masks]


HBM_SPEC = pl.BlockSpec(memory_space=pltpu.HBM)


SEM_SPEC = pl.BlockSpec(memory_space=pltpu.SEMAPHORE)
SIDE_EFFECT = pltpu.SideEffectType.DATAFLOW_SIDE_EFFECTING


def _exchange_start(blocks, name, gather=False, masks=ALL_PEERS, after=None):
    n = len(blocks)
    n_peers = len(masks)
    n_in = 2 * n + (0 if after is None else 1)

    def body(*refs):
        ins, lands = refs[:n], refs[n:2 * n]
        send_sems, recv_sems = refs[n_in], refs[n_in + 1]
        token = refs[-1]
        me = _coords()
        peers = _peers(me, masks)
        for a in range(n):
            for k in range(n_peers):
                pltpu.make_async_remote_copy(
                    src_ref=ins[a] if gather else ins[a].at[_slot(peers[k])], dst_ref=lands[a].at[_slot(me)],
                    send_sem=send_sems.at[a * n_peers + k], recv_sem=recv_sems.at[a * n_peers + k],
                    device_id=peers[k], device_id_type=MESH).start()
        token[...] = jnp.zeros_like(token)

    land_shapes = [((N_DEV,) + b.shape) if gather else b.shape for b in blocks]
    hbm_in = [pltpu.HBM(b.shape, b.dtype) for b in blocks]
    hbm_land = [pltpu.HBM(s, b.dtype) for s, b in zip(land_shapes, blocks)]
    sems = pltpu.SemaphoreType.DMA((n * n_peers,))
    ins = [pltpu.with_memory_space_constraint(b, pltpu.HBM) for b in blocks]
    lands = [pltpu.with_memory_space_constraint(lax.empty(s, b.dtype), pltpu.HBM) for s, b in zip(land_shapes, blocks)]
    return pl.pallas_call(
        body, name=name, out_shape=(sems, sems, *hbm_in, *hbm_land, jax.ShapeDtypeStruct((8, LANES), F32)),
        in_specs=[HBM_SPEC] * (2 * n) + ([] if after is None else [pl.BlockSpec(memory_space=pl.ANY)]),
        out_specs=(SEM_SPEC, SEM_SPEC, *([HBM_SPEC] * (2 * n)), pl.BlockSpec(memory_space=pltpu.VMEM)),
        input_output_aliases={i: 2 + i for i in range(2 * n)},
        compiler_params=pltpu.CompilerParams(has_side_effects=SIDE_EFFECT),
    )(*ins, *lands, *([] if after is None else [after]))


def _exchange_wait(started, after, name, gather=False, masks=ALL_PEERS):
    n = (len(started) - 3) // 2
    n_peers = len(masks)
    send_sems, recv_sems = started[0], started[1]
    thru = started[2:2 + 2 * n]

    def body(*refs):
        ins, lands = refs[:n], refs[n:2 * n]
        send_ref, recv_ref = refs[2 * n], refs[2 * n + 1]
        me = _coords()
        peers = _peers(me, masks)
        for a in range(n):
            for k in range(n_peers):
                cp = pltpu.make_async_remote_copy(
                    src_ref=ins[a] if gather else ins[a].at[_slot(peers[k])], dst_ref=lands[a].at[_slot(peers[k])],
                    send_sem=send_ref.at[a * n_peers + k], recv_sem=recv_ref.at[a * n_peers + k],
                    device_id=peers[k], device_id_type=MESH)
                cp.wait_send()
                cp.wait_recv()

    hbm = [pltpu.HBM(t.shape, t.dtype) for t in thru]
    res = pl.pallas_call(
        body, name=name, out_shape=tuple(hbm),
        in_specs=[HBM_SPEC] * (2 * n) + [SEM_SPEC, SEM_SPEC] + [pl.BlockSpec(memory_space=pl.ANY)] * len(after),
        out_specs=tuple([HBM_SPEC] * (2 * n)), input_output_aliases={i: i for i in range(2 * n)},
        compiler_params=pltpu.CompilerParams(has_side_effects=SIDE_EFFECT),
    )(*thru, send_sems, recv_sems, *after)
    return res[n:]


def _sibling_forward(arrays, name):
    n = len(arrays)
    n_fwd = len(OTHER_CHIPS)

    def body(*refs):
        bufs = refs[n:2 * n]
        token, send_sems, recv_sems = refs[2 * n:]
        token[...] = jnp.zeros_like(token)
        x, y, c = _coords()
        sibling = (x, y, 1 - c)
        mine = _peers((x, y, c), OTHER_CHIPS)
        theirs = _peers(sibling, OTHER_CHIPS)

        def copy(a, k, block):
            rows = bufs[a].at[_slot(block)]
            return pltpu.make_async_remote_copy(
                src_ref=rows, dst_ref=rows, send_sem=send_sems.at[a * n_fwd + k], recv_sem=recv_sems.at[a * n_fwd + k],
                device_id=sibling, device_id_type=MESH)

        sends = [copy(a, k, mine[k]) for a in range(n) for k in range(n_fwd)]
        for cp in sends:
            cp.start()
        for a in range(n):
            for k in range(n_fwd):
                copy(a, k, theirs[k]).wait_recv()
        for cp in sends:
            cp.wait_send()

    res = pl.pallas_call(
        body, name=name, in_specs=[HBM_SPEC] * n,
        out_specs=tuple([HBM_SPEC] * n + [pl.BlockSpec(memory_space=pltpu.VMEM)]),
        out_shape=tuple([jax.ShapeDtypeStruct(a.shape, a.dtype) for a in arrays] + [jax.ShapeDtypeStruct((8, LANES), F32)]),
        input_output_aliases={i: i for i in range(n)},
        scratch_shapes=[pltpu.SemaphoreType.DMA((n * n_fwd,)), pltpu.SemaphoreType.DMA((n * n_fwd,))],
    )(*arrays)
    return res[:n], res[n]


def _all_sum(p, name):
    def body(p_ref, o_ref, recv, send_sems, recv_sems):
        me = _coords()
        peers = _peers(me)
        recv[_slot(me)] = p_ref[...]

        def copy(k, landing):
            return pltpu.make_async_remote_copy(
                src_ref=p_ref, dst_ref=recv.at[_slot(landing)], send_sem=send_sems.at[k], recv_sem=recv_sems.at[k],
                device_id=peers[k], device_id_type=MESH)

        sends = [copy(k, me) for k in range(N_DEV - 1)]
        for cp in sends:
            cp.start()
        for k in range(N_DEV - 1):
            copy(k, peers[k]).wait_recv()
        for cp in sends:
            cp.wait_send()
        acc = recv[0]
        for s in range(1, N_DEV):
            acc = acc + recv[s]
        o_ref[...] = acc

    vmem = pl.BlockSpec(memory_space=pltpu.VMEM)
    return pl.pallas_call(
        body, name=name, in_specs=[vmem], out_specs=vmem, out_shape=jax.ShapeDtypeStruct(p.shape, F32),
        scratch_shapes=[pltpu.VMEM((N_DEV,) + p.shape, F32), pltpu.SemaphoreType.DMA((N_DEV - 1,)),
                        pltpu.SemaphoreType.DMA((N_DEV - 1,))],
    )(p)


def _adam(w, g, m, v):
    m2 = ADAM_B1 * m + (1.0 - ADAM_B1) * g
    v2 = ADAM_B2 * v + (1.0 - ADAM_B2) * (g * g)
    m_hat = m2 / (1.0 - ADAM_B1 ** ADAM_STEP)
    v_hat = v2 / (1.0 - ADAM_B2 ** ADAM_STEP)
    delta = -ADAM_LR * (m_hat / (jnp.sqrt(v_hat) + ADAM_EPS) + ADAM_WD * w)
    return delta, m2, v2


def _row_tile(rows, cols):
    best = rows
    for t in range(16, rows, 16):
        if rows % t == 0 and t * cols * 4 <= (1 << 20):
            best = t
    return best


def _adam_reduce(parts, w, m, v, name):
    rows, cols = w.shape
    tr = _row_tile(rows, cols)

    def body(p_ref, w_ref, m_ref, v_ref, g_ref, d_ref, m2_ref, v2_ref):
        g = p_ref[0].astype(F32)
        for s in range(1, N_DEV):
            g = g + p_ref[s].astype(F32)
        g_ref[...] = g
        d_ref[...], m2_ref[...], v2_ref[...] = _adam(w_ref[...], g, m_ref[...], v_ref[...])

    blk = pl.BlockSpec((tr, cols), lambda i: (i, 0))
    shp = jax.ShapeDtypeStruct((rows, cols), F32)
    return pl.pallas_call(
        body, name=name, grid=(rows // tr,),
        in_specs=[pl.BlockSpec((N_DEV, tr, cols), lambda i: (0, i, 0)), blk, blk, blk],
        out_specs=(blk,) * 4, out_shape=(shp,) * 4, compiler_params=_cparams(1),
    )(parts, w, m, v)


PACK_COLS = 1024
PACK = {'attn_norm': (0, 1, 1024), 'mem_norm': (1, 1, 1024), 'ffn_norm': (2, 1, 1024), 'b_gate': (3, 3, 1024),
        'conv_b': (6, 6, 1024), 'a_q_norm': (12, 3, 64), 'a_k_norm': (15, 3, 64), 'b_q_norm': (18, 1, 64),
        'b_k_norm': (19, 1, 64), 'm_q_norm': (20, 1, 128), 'm_k_norm': (21, 1, 128), 'b_sinks': (22, 1, 8)}
PACK_LOSS_ROW = 23
PACK_ROWS = 24


def _pack_pieces(name, width):
    r0, nr, lanes = PACK[name]
    out = []
    for j in range(nr):
        if lanes == PACK_COLS:
            w = min(PACK_COLS, width - j * PACK_COLS)
            out.append((r0 + j, slice(0, 1), slice(j * PACK_COLS, j * PACK_COLS + w), w))
        else:
            out.append((r0 + j, slice(j, j + 1), slice(0, lanes), lanes))
    return out


def _pack_small(grads, loss_tile, name):
    names = list(PACK)

    def body(*refs):
        o_ref = refs[-1]
        o_ref[...] = jnp.zeros_like(o_ref)
        for k, nm in enumerate(names):
            for row, rs, ls, w in _pack_pieces(nm, refs[k].shape[1]):
                o_ref[row:row + 1, 0:w] = refs[k][rs, ls]
        o_ref[PACK_LOSS_ROW:PACK_LOSS_ROW + 1, 0:1] = refs[len(names)][0:1, 0:1]

    vmem = pl.BlockSpec(memory_space=pltpu.VMEM)
    args = [grads[nm] for nm in names] + [loss_tile]
    return pl.pallas_call(body, name=name, in_specs=[vmem] * len(args), out_specs=vmem,
                          out_shape=jax.ShapeDtypeStruct((PACK_ROWS, PACK_COLS), F32))(*args)


def _adam_small(gsum, ws, ms, vs, name):
    names = list(PACK)
    n = len(names)

    def body(*refs):
        g_ref = refs[0]
        w_refs, m_refs, v_refs = refs[1:1 + n], refs[1 + n:1 + 2 * n], refs[1 + 2 * n:1 + 3 * n]
        outs = refs[1 + 3 * n:]
        outs[0][...] = g_ref[PACK_LOSS_ROW:PACK_LOSS_ROW + 1, 0:1]
        for k, nm in enumerate(names):
            o_g, o_d, o_m, o_v = outs[1 + 4 * k:5 + 4 * k]
            for row, rs, ls, width in _pack_pieces(nm, w_refs[k].shape[1]):
                src = (rs, ls)
                g = g_ref[row:row + 1, 0:width]
                d, m2, v2 = _adam(w_refs[k][src], g, m_refs[k][src], v_refs[k][src])
                o_g[src] = g
                o_d[src] = d
                o_m[src] = m2
                o_v[src] = v2

    vmem = pl.BlockSpec(memory_space=pltpu.VMEM)
    shapes = [jax.ShapeDtypeStruct((1, 1), F32)]
    for nm in names:
        shapes += [jax.ShapeDtypeStruct(ws[nm].shape, F32)] * 4
    args = [gsum] + [ws[nm] for nm in names] + [ms[nm] for nm in names] + [vs[nm] for nm in names]
    return pl.pallas_call(
        body, name=name, in_specs=[vmem] * len(args), out_specs=tuple([vmem] * len(shapes)), out_shape=tuple(shapes),
    )(*args)


def _as2d(name, a):
    return a.reshape(a.shape[-2], a.shape[-1]) if a.ndim == 3 else a


def kernel(x, mem, positions, attn_norm, w_in, a_q_norm, a_k_norm, b_q_norm, b_k_norm, b_sinks, mem_norm, w_mem_kv, m_q_norm, m_k_norm, w_o_a, w_o_b, w_o_m, w_gate, b_gate, w_out, ffn_norm, w_up, conv_w, conv_b, w_down, loss_target, m_attn_norm, m_w_in, m_a_q_norm, m_a_k_norm, m_b_q_norm, m_b_k_norm, m_b_sinks, m_mem_norm, m_w_mem_kv, m_m_q_norm, m_m_k_norm, m_w_o_a, m_w_o_b, m_w_o_m, m_w_gate, m_b_gate, m_w_out, m_ffn_norm, m_w_up, m_conv_w, m_conv_b, m_w_down, v_attn_norm, v_w_in, v_a_q_norm, v_a_k_norm, v_b_q_norm, v_b_k_norm, v_b_sinks, v_mem_norm, v_w_mem_kv, v_m_q_norm, v_m_k_norm, v_w_o_a, v_w_o_b, v_w_o_m, v_w_gate, v_b_gate, v_w_out, v_ffn_norm, v_w_up, v_conv_w, v_conv_b, v_w_down):
    given = dict(attn_norm=attn_norm, w_in=w_in, a_q_norm=a_q_norm, a_k_norm=a_k_norm, b_q_norm=b_q_norm, b_k_norm=b_k_norm, b_sinks=b_sinks, mem_norm=mem_norm, w_mem_kv=w_mem_kv, m_q_norm=m_q_norm, m_k_norm=m_k_norm, w_o_a=w_o_a, w_o_b=w_o_b, w_o_m=w_o_m, w_gate=w_gate, b_gate=b_gate, w_out=w_out, ffn_norm=ffn_norm, w_up=w_up, conv_w=conv_w, conv_b=conv_b, w_down=w_down)
    mom1 = dict(attn_norm=m_attn_norm, w_in=m_w_in, a_q_norm=m_a_q_norm, a_k_norm=m_a_k_norm, b_q_norm=m_b_q_norm, b_k_norm=m_b_k_norm, b_sinks=m_b_sinks, mem_norm=m_mem_norm, w_mem_kv=m_w_mem_kv, m_q_norm=m_m_q_norm, m_k_norm=m_m_k_norm, w_o_a=m_w_o_a, w_o_b=m_w_o_b, w_o_m=m_w_o_m, w_gate=m_w_gate, b_gate=m_b_gate, w_out=m_w_out, ffn_norm=m_ffn_norm, w_up=m_w_up, conv_w=m_conv_w, conv_b=m_conv_b, w_down=m_w_down)
    mom2 = dict(attn_norm=v_attn_norm, w_in=v_w_in, a_q_norm=v_a_q_norm, a_k_norm=v_a_k_norm, b_q_norm=v_b_q_norm, b_k_norm=v_b_k_norm, b_sinks=v_b_sinks, mem_norm=v_mem_norm, w_mem_kv=v_w_mem_kv, m_q_norm=v_m_q_norm, m_k_norm=v_m_k_norm, w_o_a=v_w_o_a, w_o_b=v_w_o_b, w_o_m=v_w_o_m, w_gate=v_w_gate, b_gate=v_b_gate, w_out=v_w_out, ffn_norm=v_ffn_norm, w_up=v_w_up, conv_w=v_conv_w, conv_b=v_conv_b, w_down=v_w_down)

    big = list(BIG)
    stages = {'mix': list(MIX_WEIGHTS), 'ffn': list(FFN_WEIGHTS), 'in': ['w_in']}
    my_slot = _slot(_coords())

    def shard(n):
        return given[n][0] if n == 'conv_w' else given[n][0].astype(BF16)

    def whole(n, g):
        _, r, c = g.shape
        return g.reshape(N_DEV * r, c) if BIG[n] == 0 else g.transpose(1, 0, 2).reshape(r, N_DEV * c)

    def to_blocks(n, g):
        r, c = given[n].shape[1:]
        g = g.reshape(N_DEV, r, c) if BIG[n] == 0 else g.reshape(r, N_DEV, c).transpose(1, 0, 2)
        return g if n == 'conv_w' else g.astype(BF16)

    class Hooks:
        next_stage = {'in': 'mix', 'mix': 'ffn'}

        def __init__(self):
            self.coming, self.sent = {}, {}
            self.shards = {n: shard(n) for n in big}
            self.start_gather('in', None)

        def start_gather(self, stage, after):
            src = [self.shards[n] for n in stages[stage]]
            self.coming[stage] = _exchange_start(src, f"gather_{stage}_start", gather=True, masks=CHIP_PEERS,
                                                 after=after)

        def weights(self, stage, after):
            names = stages[stage]
            after = list(after)
            if stage == 'in':
                after += [self.shards[n] for n in stages['mix'] + stages['ffn']]
            landed = _exchange_wait(self.coming[stage], after, f"gather_{stage}_wait", gather=True, masks=CHIP_PEERS)
            landed, token = _sibling_forward(landed, f"gather_{stage}_forward")
            if stage in self.next_stage:
                self.start_gather(self.next_stage[stage], token)
            return {n: whole(n, lax.dynamic_update_slice_in_dim(land, self.shards[n][None], my_slot, axis=0))
                    for n, land in zip(names, landed)}

        def grads(self, stage, g):
            blocks = [to_blocks(n, g[n]) for n in stages[stage]]
            own = [lax.dynamic_slice_in_dim(b, my_slot, 1, axis=0) for b in blocks]
            self.sent[stage] = (_exchange_start(blocks, f"exchange_{stage}_start"), own)
            return self.sent[stage][0][-1]

        def parts(self, stage, after):
            started, own = self.sent[stage]
            landed = _exchange_wait(started, [after], f"exchange_{stage}_wait")
            return {n: lax.dynamic_update_slice_in_dim(land, o, my_slot, axis=0)
                    for n, land, o in zip(stages[stage], landed, own)}

    hooks = Hooks()
    w = {}
    for n in SMALL:
        w[n] = given[n]
    w['a_q_norm'], w['a_k_norm'] = given['a_q_norm'][0], given['a_k_norm'][0]
    w['b_q_norm'], w['b_k_norm'], w['b_sinks'] = given['b_q_norm'][0], given['b_k_norm'][0], given['b_sinks'][0]

    loss_tile, grad_x, grads = _device_step(x[0], mem[0], positions[0], loss_target[0], w, hooks)
    out = {}
    after = grad_x
    for stage in ('ffn', 'mix', 'in'):
        for n, p in hooks.parts(stage, after).items():
            res = _adam_reduce(p, given[n][0], mom1[n][0], mom2[n][0], f"adam_{n}")
            out[n] = tuple(t[None] for t in res)
            after = res[0]

    small = {n: grads[n] for n in PACK}
    small['b_q_norm'], small['b_k_norm'] = grads['b_q_norm'].reshape(1, -1), grads['b_k_norm'].reshape(1, -1)
    small['b_sinks'] = grads['b_sinks'].reshape(1, -1)
    gsum = _all_sum(_pack_small(small, loss_tile, "pack_small"), "sum_small")
    ws = {n: _as2d(n, given[n]) for n in PACK}
    ms = {n: _as2d(n, mom1[n]) for n in PACK}
    vs = {n: _as2d(n, mom2[n]) for n in PACK}
    res = _adam_small(gsum, ws, ms, vs, "adam_small")
    loss = res[0].reshape(())
    for k, n in enumerate(PACK):
        out[n] = tuple(t.reshape(given[n].shape) for t in res[1 + 4 * k:5 + 4 * k])

    outs = [loss, grad_x[None]]
    for field in range(4):
        outs += [out[n][field] for n in WEIGHTS]
    return tuple(outs)
```

```python
import functools
import math

import jax
import jax.numpy as jnp
from jax import lax
from jax.experimental import pallas as pl
from jax.experimental.pallas import tpu as pltpu

F32 = jnp.float32
BF16 = jnp.bfloat16

N_DEV = 8
HEAD_DIM = 64
A_GROUPS = ((128, 1), (512, 4), (2048, 16))
B_WINDOW = 128
M_HEADS = 4
M_HEAD_DIM = 128
MEM_LEN = 256
D_FF = 2816
ROPE_THETA = 500000.0
ROPE_DIMS = 16
BLOCK = 128
EPS = 1e-6
LANES = 128
BAND_Q_BLOCKS = 8
BAND_UNITS = 2

ADAM_LR = 0.001
ADAM_B1 = 0.9
ADAM_B2 = 0.999
ADAM_EPS = 1e-08
ADAM_WD = 0.01
ADAM_STEP = 10

VMEM_LIMIT_BYTES = 56 * 1024 * 1024
GRAD_DTYPE = BF16
MESH = pl.DeviceIdType.MESH

WEIGHTS = ['attn_norm', 'w_in', 'a_q_norm', 'a_k_norm', 'b_q_norm', 'b_k_norm', 'b_sinks', 'mem_norm',
           'w_mem_kv', 'm_q_norm', 'm_k_norm', 'w_o_a', 'w_o_b', 'w_o_m', 'w_gate', 'b_gate', 'w_out',
           'ffn_norm', 'w_up', 'conv_w', 'conv_b', 'w_down']
BIG = {'w_in': 1, 'w_mem_kv': 0, 'w_o_a': 1, 'w_o_b': 1, 'w_o_m': 1, 'w_gate': 1, 'w_out': 0, 'w_up': 1,
       'conv_w': 1, 'w_down': 0}
SMALL = [n for n in WEIGHTS if n not in BIG]


def _cparams(n_grid):
    return pltpu.CompilerParams(dimension_semantics=("arbitrary",) * n_grid, vmem_limit_bytes=VMEM_LIMIT_BYTES)


def _seg_matrix(width):
    shift = width.bit_length() - 1
    r = lax.shift_right_logical(lax.broadcasted_iota(jnp.int32, (LANES, LANES), 0), shift)
    c = lax.shift_right_logical(lax.broadcasted_iota(jnp.int32, (LANES, LANES), 1), shift)
    return jnp.where(r == c, 1.0, 0.0).astype(BF16)


def _seg_sum(x, seg):
    hi = x.astype(BF16)
    r1 = x - hi.astype(F32)
    mid = r1.astype(BF16)
    lo = (r1 - mid.astype(F32)).astype(BF16)
    dot = functools.partial(jnp.dot, preferred_element_type=F32)
    return dot(hi, seg) + dot(mid, seg) + dot(lo, seg)


def _rope(y, c, s1, s2):
    return y * c + pltpu.roll(y, LANES - ROPE_DIMS // 2, 1) * s1 + pltpu.roll(y, ROPE_DIMS // 2, 1) * s2


def _unrope(dy, c, s1, s2):
    return dy * c + pltpu.roll(dy * s1, ROPE_DIMS // 2, 1) + pltpu.roll(dy * s2, LANES - ROPE_DIMS // 2, 1)


def _sigmoid(x):
    return 1.0 / (1.0 + jnp.exp(-x))


def _rms_fwd(x, gain, name):
    s_len, d = x.shape
    tm = 512

    def body(x_ref, g_ref, h_ref, ht_ref, r_ref):
        xv = x_ref[...]
        r = lax.rsqrt(jnp.mean(xv * xv, axis=-1, keepdims=True) + EPS)
        h = ((xv * r) * g_ref[...]).astype(BF16)
        h_ref[...] = h
        ht_ref[...] = h.T
        r_ref[...] = r

    return pl.pallas_call(
        body, name=name, grid=(s_len // tm,),
        in_specs=[pl.BlockSpec((tm, d), lambda i: (i, 0)), pl.BlockSpec((1, d), lambda i: (0, 0))],
        out_specs=(pl.BlockSpec((tm, d), lambda i: (i, 0)), pl.BlockSpec((d, tm), lambda i: (0, i)),
                   pl.BlockSpec((tm, 1), lambda i: (i, 0))),
        out_shape=(jax.ShapeDtypeStruct((s_len, d), BF16), jax.ShapeDtypeStruct((d, s_len), BF16),
                   jax.ShapeDtypeStruct((s_len, 1), F32)),
        compiler_params=_cparams(1),
    )(x, gain)


def _resident(shape, index_map):
    return pl.BlockSpec(shape, index_map, pipeline_mode=pl.Buffered(1))


def _mm_rows(pairs, name, nt=False, tm=512, bias=None, sigmoid=False, res=None, out_dtypes=(F32,), loss_target=None,
             rms_bwd=None):
    m = pairs[0][0].shape[0]
    n = pairs[0][1].shape[0] if nt else pairs[0][1].shape[1]
    n_pairs = len(pairs)
    has_bias, has_res, has_loss = bias is not None, res is not None, loss_target is not None
    has_rms = rms_bwd is not None
    dims = (((1,), (1,)), ((), ())) if nt else (((1,), (0,)), ((), ()))

    def body(*refs):
        acc = None
        for p in range(n_pairs):
            t = lax.dot_general(refs[2 * p][...].astype(BF16), refs[2 * p + 1][...], dims, preferred_element_type=F32)
            acc = t if acc is None else acc + t
        pos = 2 * n_pairs
        if has_bias:
            acc = acc + refs[pos][...]
            pos += 1
        if sigmoid:
            acc = _sigmoid(acc)
        if has_res:
            acc = refs[pos][...] + acc
            pos += 1
        if has_loss:
            dy_ref, dyb_ref, da_ref, l_ref = refs[pos + 1:]

            @pl.when(pl.program_id(0) == 0)
            def _():
                l_ref[...] = jnp.zeros_like(l_ref)
            err = acc - refs[pos][...]
            dy = err * (1.0 / n)
            dy_ref[...] = dy
            dyb_ref[...] = dy.astype(BF16)
            da_ref[...] = lax.dot_general(dy.astype(BF16), refs[1][...], (((1,), (1,)), ((), ())),
                                          preferred_element_type=F32).astype(BF16)
            part = 0.5 * jnp.sum(jnp.mean(err * err, axis=-1, keepdims=True), axis=0, keepdims=True)
            l_ref[...] += jnp.broadcast_to(part, l_ref.shape)
            return
        if has_rms:
            x_ref, r_ref, g_ref, add_ref = refs[pos:pos + 4]
            dg_ref = refs[-1]

            @pl.when(pl.program_id(0) == 0)
            def _():
                dg_ref[...] = jnp.zeros_like(dg_ref)
            rv = r_ref[...]
            xhat = x_ref[...] * rv
            dg_ref[...] += jnp.sum(acc * xhat, axis=0, keepdims=True)
            dxhat = acc * g_ref[...]
            acc = add_ref[...] + rv * (dxhat - xhat * jnp.mean(dxhat * xhat, axis=-1, keepdims=True))
            for o_ref in refs[pos + 4:-1]:
                o_ref[...] = acc.astype(o_ref.dtype)
            return
        for o_ref in refs[pos:]:
            o_ref[...] = acc.astype(o_ref.dtype)

    in_specs, args = [], []
    for a, w, blk in pairs:
        k = a.shape[1]
        in_specs.append(pl.BlockSpec((tm, k), lambda i: (i, 0)))
        if nt:
            in_specs.append(_resident((n, k), lambda i, blk=blk: (0, blk)))
        else:
            in_specs.append(_resident((k, n), lambda i, blk=blk: (blk, 0)))
        args += [a, w]
    if has_bias:
        in_specs.append(_resident((1, n), lambda i: (0, 0)))
        args.append(bias)
    if has_res:
        in_specs.append(pl.BlockSpec((tm, n), lambda i: (i, 0)))
        args.append(res)
    out = pl.BlockSpec((tm, n), lambda i: (i, 0))
    if has_loss:
        k0 = pairs[0][0].shape[1]
        return pl.pallas_call(
            body, name=name, grid=(m // tm,), in_specs=in_specs + [out],
            out_specs=(out, out, pl.BlockSpec((tm, k0), lambda i: (i, 0)), pl.BlockSpec((8, LANES), lambda i: (0, 0))),
            out_shape=(jax.ShapeDtypeStruct((m, n), F32), jax.ShapeDtypeStruct((m, n), BF16),
                       jax.ShapeDtypeStruct((m, k0), BF16), jax.ShapeDtypeStruct((8, LANES), F32)),
            compiler_params=_cparams(1),
        )(*args, loss_target)
    if has_rms:
        x, r, gain, add = rms_bwd
        vec = _resident((1, n), lambda i: (0, 0))
        return pl.pallas_call(
            body, name=name, grid=(m // tm,),
            in_specs=in_specs + [out, pl.BlockSpec((tm, 1), lambda i: (i, 0)), vec, out],
            out_specs=tuple([out] * len(out_dtypes) + [pl.BlockSpec((1, n), lambda i: (0, 0))]),
            out_shape=tuple([jax.ShapeDtypeStruct((m, n), dt) for dt in out_dtypes] + [jax.ShapeDtypeStruct((1, n), F32)]),
            compiler_params=_cparams(1),
        )(*args, x, r, gain, add)
    outs = pl.pallas_call(
        body, name=name, grid=(m // tm,), in_specs=in_specs, out_specs=tuple([out] * len(out_dtypes)),
        out_shape=tuple(jax.ShapeDtypeStruct((m, n), dt) for dt in out_dtypes), compiler_params=_cparams(1),
    )(*args)
    return outs[0] if len(out_dtypes) == 1 else outs


def _mm_rows_each(pairs, name, tm=512):
    m = pairs[0][0].shape[0]
    n_pairs = len(pairs)

    def body(*refs):
        for p in range(n_pairs):
            refs[2 * n_pairs + p][...] = lax.dot_general(refs[2 * p][...].astype(BF16), refs[2 * p + 1][...],
                                                         (((1,), (1,)), ((), ())), preferred_element_type=F32)

    in_specs, args = [], []
    for a, w in pairs:
        in_specs += [pl.BlockSpec((tm, a.shape[1]), lambda i: (i, 0)), _resident(w.shape, lambda i: (0, 0))]
        args += [a, w]
    return pl.pallas_call(
        body, name=name, grid=(m // tm,), in_specs=in_specs,
        out_specs=tuple(pl.BlockSpec((tm, w.shape[0]), lambda i: (i, 0)) for _, w in pairs),
        out_shape=tuple(jax.ShapeDtypeStruct((m, w.shape[0]), F32) for _, w in pairs), compiler_params=_cparams(1),
    )(*args)


def _mm_rows_cat(a, ws, name, tm=256):
    m, k = a.shape
    widths = [w.shape[1] for w in ws]
    n = sum(widths)

    def body(*refs):
        a_ref, o_ref = refs[0], refs[-1]
        av = a_ref[...]
        off = 0
        for p, width in enumerate(widths):
            o_ref[:, off:off + width] = jnp.dot(av, refs[1 + p][...], preferred_element_type=F32).astype(GRAD_DTYPE)
            off += width

    return pl.pallas_call(
        body, name=name, grid=(m // tm,),
        in_specs=[pl.BlockSpec((tm, k), lambda i: (i, 0))] + [_resident((k, wd), lambda i: (0, 0)) for wd in widths],
        out_specs=pl.BlockSpec((tm, n), lambda i: (i, 0)),
        out_shape=jax.ShapeDtypeStruct((m, n), GRAD_DTYPE), compiler_params=_cparams(1),
    )(a, *ws)


def _mm_cols(a, bs, name, tn=256):
    m, k = a.shape
    counts = [b.shape[1] // tn for b in bs]
    starts = [sum(counts[:p]) for p in range(len(bs))]

    def body(*refs):
        a_ref, o_ref = refs[0], refs[-1]
        j = pl.program_id(0)
        for p, b_ref in enumerate(refs[1:-1]):
            @pl.when((j >= starts[p]) & (j < starts[p] + counts[p]))
            def _():
                o_ref[...] = jnp.dot(a_ref[...], b_ref[...].astype(BF16), preferred_element_type=F32).astype(GRAD_DTYPE)

    b_specs = [pl.BlockSpec((k, tn), lambda j, s=s, c=c: (0, jnp.clip(j - s, 0, c - 1))) for s, c in zip(starts, counts)]
    return pl.pallas_call(
        body, name=name, grid=(sum(counts),),
        in_specs=[_resident((m, k), lambda j: (0, 0))] + b_specs,
        out_specs=pl.BlockSpec((m, tn), lambda j: (0, j)),
        out_shape=jax.ShapeDtypeStruct((m, sum(counts) * tn), GRAD_DTYPE), compiler_params=_cparams(1),
    )(a, *bs)


def _mm_tn_each(pairs, name, tile=256):
    n = pairs[0][1].shape[1]
    n_pairs = len(pairs)
    dims = (((0,), (0,)), ((), ()))

    def body(*refs):
        for p in range(n_pairs):
            refs[2 * n_pairs + p][...] = lax.dot_general(refs[2 * p][...].astype(BF16), refs[2 * p + 1][...].astype(BF16),
                                                         dims, preferred_element_type=F32).astype(GRAD_DTYPE)

    in_specs, args = [], []
    for a, b in pairs:
        in_specs += [_resident(a.shape, lambda j: (0, 0)), pl.BlockSpec((b.shape[0], tile), lambda j: (0, j))]
        args += [a, b]
    return pl.pallas_call(
        body, name=name, grid=(n // tile,), in_specs=in_specs,
        out_specs=tuple(pl.BlockSpec((a.shape[1], tile), lambda j: (0, j)) for a, _ in pairs),
        out_shape=tuple(jax.ShapeDtypeStruct((a.shape[1], n), GRAD_DTYPE) for a, _ in pairs),
        compiler_params=_cparams(1),
    )(*args)


def _mm_tn(a, b, name, tile=256):
    k, m = a.shape
    n = b.shape[1]
    dims = (((0,), (0,)), ((), ()))

    def body(a_ref, b_ref, o_ref):
        o_ref[...] = lax.dot_general(a_ref[...].astype(BF16), b_ref[...].astype(BF16), dims,
                                     preferred_element_type=F32).astype(GRAD_DTYPE)

    if n <= m:
        t = min(tile, m)
        grid, a_spec, b_spec = (m // t,), pl.BlockSpec((k, t), lambda i: (0, i)), _resident((k, n), lambda i: (0, 0))
        o_spec = pl.BlockSpec((t, n), lambda i: (i, 0))
    else:
        t = min(tile, n)
        grid, a_spec, b_spec = (n // t,), _resident((k, m), lambda i: (0, 0)), pl.BlockSpec((k, t), lambda i: (0, i))
        o_spec = pl.BlockSpec((m, t), lambda i: (0, i))
    return pl.pallas_call(
        body, name=name, grid=grid, in_specs=[a_spec, b_spec], out_specs=o_spec,
        out_shape=jax.ShapeDtypeStruct((m, n), GRAD_DTYPE), compiler_params=_cparams(1),
    )(a, b)


def _norm_rope(t, gain, c, s1, s2, seg):
    rs = lax.rsqrt(_seg_sum(t * t, seg) * (1.0 / HEAD_DIM) + EPS)
    return _rope((t * rs) * gain, c, s1, s2)


def _dup_half(y, half):
    lane = lax.broadcasted_iota(jnp.int32, y.shape, 1)
    rolled = pltpu.roll(y, HEAD_DIM, 1)
    keep = (lane < HEAD_DIM) if half == 0 else (lane >= HEAD_DIM)
    return jnp.where(keep, y, rolled)


def _qk_prep(proj, cb0, d, gqa, gq, gk, tabs, name):
    s_len = proj.shape[0]
    tm = 512
    rows = tm // d
    n_units = 4 if gqa else 2 * d
    n_q = 4 if gqa else 2
    n_in = 6

    def body(*refs):
        in_refs = refs[:n_in]
        gq_ref, gk_ref, c_ref, s1_ref, s2_ref, o_ref = refs[n_in:]
        seg = _seg_matrix(HEAD_DIM)

        def rows_of(ref, r):
            return ref[...] if d == 1 else ref[pl.ds(r, rows, stride=d), :]

        def put(unit_col, y):
            o_ref[:, unit_col * LANES:(unit_col + 1) * LANES] = y.astype(BF16)

        for r in range(d):
            c, s1, s2 = rows_of(c_ref, r), rows_of(s1_ref, r), rows_of(s2_ref, r)
            for b in range(n_in):
                t = rows_of(in_refs[b], r)
                if b < n_q:
                    put((b * d + r) if not gqa else b, _norm_rope(t, gq_ref[...], c, s1, s2, seg))
                elif not gqa:
                    sec, pair = (1, b - 2) if b < 4 else (2, b - 4)
                    y = _norm_rope(t, gk_ref[...], c, s1, s2, seg) if sec == 1 else t
                    put(sec * n_units + pair * d + r, y)
                else:
                    sec = 1 if b == 4 else 2
                    y = _norm_rope(t, gk_ref[...], c, s1, s2, seg) if sec == 1 else t
                    for u in range(n_units):
                        put(sec * n_units + u, _dup_half(y, u // 2))

    in_specs = [pl.BlockSpec((tm, LANES), lambda i, b=b: (i, cb0 + b)) for b in range(n_in)]
    vec = pl.BlockSpec((1, LANES), lambda i: (0, 0))
    tab = pl.BlockSpec((tm, LANES), lambda i: (i, 0))
    width = 3 * n_units * LANES
    return pl.pallas_call(
        body, name=name, grid=(s_len // tm,), in_specs=in_specs + [vec, vec, tab, tab, tab],
        out_specs=pl.BlockSpec((rows, width), lambda i: (i, 0)),
        out_shape=jax.ShapeDtypeStruct((s_len // d, width), BF16), compiler_params=_cparams(1),
    )(*([proj] * n_in), gq, gk, *tabs)


def _qk_prep_bwd(dqkv, proj, cb0, d, gqa, gq, gk, tabs, name):
    s_len = proj.shape[0]
    tm = 512
    rows = tm // d
    n_units = 4 if gqa else 2 * d
    n_q = 4 if gqa else 2
    n_in = 6

    def body(*refs):
        d_refs = refs[0:3]
        in_refs = refs[3:3 + n_in]
        gq_ref, gk_ref, c_ref, s1_ref, s2_ref, o_ref, dgq_ref, dgk_ref, stage = refs[3 + n_in:]
        seg = _seg_matrix(HEAD_DIM)

        @pl.when(pl.program_id(0) == 0)
        def _():
            dgq_ref[...] = jnp.zeros_like(dgq_ref)
            dgk_ref[...] = jnp.zeros_like(dgk_ref)

        def rows_of(ref, r):
            return ref[...] if d == 1 else ref[pl.ds(r, rows, stride=d), :]

        def unit(col):
            sec, u = divmod(col, n_units)
            return d_refs[sec][:, u * LANES:(u + 1) * LANES]

        def norm_bwd(dyr, t, gain, c, s1, s2, dg_ref):
            rs = lax.rsqrt(_seg_sum(t * t, seg) * (1.0 / HEAD_DIM) + EPS)
            that = t * rs
            dy = _unrope(dyr, c, s1, s2)
            dg_ref[...] += jnp.sum(dy * that, axis=0, keepdims=True)
            dthat = dy * gain
            return rs * (dthat - that * (_seg_sum(dthat * that, seg) * (1.0 / HEAD_DIM)))

        def fold(sec):
            tot = []
            for u in range(n_units):
                v = unit(sec * n_units + u)
                tot.append(v + pltpu.roll(v, HEAD_DIM, 1))
            lane = lax.broadcasted_iota(jnp.int32, tot[0].shape, 1)
            return jnp.where(lane < HEAD_DIM, tot[0] + tot[1], tot[2] + tot[3])

        for b in range(n_in):
            for r in range(d):
                c, s1, s2 = rows_of(c_ref, r), rows_of(s1_ref, r), rows_of(s2_ref, r)
                t = rows_of(in_refs[b], r)
                if b < n_q:
                    g = unit((b * d + r) if not gqa else b)
                    out = norm_bwd(g, t, gq_ref[...], c, s1, s2, dgq_ref)
                elif not gqa:
                    sec, pair = (1, b - 2) if b < 4 else (2, b - 4)
                    g = unit(sec * n_units + pair * d + r)
                    out = norm_bwd(g, t, gk_ref[...], c, s1, s2, dgk_ref) if sec == 1 else g
                else:
                    sec = 1 if b == 4 else 2
                    g = fold(sec)
                    out = norm_bwd(g, t, gk_ref[...], c, s1, s2, dgk_ref) if sec == 1 else g
                if d == 1:
                    o_ref[:, b * LANES:(b + 1) * LANES] = out.astype(BF16)
                else:
                    stage[pl.ds(r, rows, stride=d), :] = out
            if d != 1:
                o_ref[:, b * LANES:(b + 1) * LANES] = stage[...].astype(BF16)

    in_specs = [pl.BlockSpec((rows, n_units * LANES), lambda i: (i, 0))] * 3
    in_specs += [pl.BlockSpec((tm, LANES), lambda i, b=b: (i, cb0 + b)) for b in range(n_in)]
    vec = pl.BlockSpec((1, LANES), lambda i: (0, 0))
    tab = pl.BlockSpec((tm, LANES), lambda i: (i, 0))
    return pl.pallas_call(
        body, name=name, grid=(s_len // tm,), in_specs=in_specs + [vec, vec, tab, tab, tab],
        out_specs=(pl.BlockSpec((tm, n_in * LANES), lambda i: (i, 0)), vec, vec),
        out_shape=(jax.ShapeDtypeStruct((s_len, n_in * LANES), BF16), jax.ShapeDtypeStruct((1, LANES), F32),
                   jax.ShapeDtypeStruct((1, LANES), F32)),
        scratch_shapes=[pltpu.VMEM((tm, LANES), F32)], compiler_params=_cparams(1),
    )(*dqkv, *([proj] * n_in), gq, gk, *tabs)


def _head_masks(shape):
    lane = lax.broadcasted_iota(jnp.int32, shape, 1)
    return lane < HEAD_DIM, lane >= HEAD_DIM


def _band_fwd(qkv, n_units, max_dist, sinks, name):
    n_rows = qkv.shape[0]
    nb = n_rows // BLOCK
    scale = HEAD_DIM ** -0.5
    has_sink = sinks is not None
    assert not has_sink or max_dist < BLOCK

    qn, un = min(nb, BAND_Q_BLOCKS), BAND_UNITS
    ug = n_units // un

    def body(*refs):
        q_ref, kp_ref, km_ref, vp_ref, vm_ref = refs[:5]
        o_ref, lse_ref = refs[-2:]
        i = pl.program_id(1)
        qi = lax.broadcasted_iota(jnp.int32, (BLOCK, 2 * BLOCK), 0)
        kj = lax.broadcasted_iota(jnp.int32, (BLOCK, 2 * BLOCK), 1)
        dist = qi + BLOCK - kj
        band = (dist >= 0) & (dist <= max_dist)
        band_first = band & ((i > 0) | (kj >= BLOCK))
        m0, m1 = _head_masks((BLOCK, LANES))
        zero = jnp.zeros((BLOCK, LANES), BF16)
        for ub in range(un):
            cs = slice(ub * LANES, (ub + 1) * LANES)
            for qb in range(qn):
                rs = slice(qb * BLOCK, (qb + 1) * BLOCK)
                q = q_ref[rs, cs]
                if qb == 0:
                    kk = jnp.concatenate([kp_ref[:, cs], km_ref[0:BLOCK, cs]], axis=0)
                    vv = jnp.concatenate([vp_ref[:, cs], vm_ref[0:BLOCK, cs]], axis=0)
                    valid = band_first
                else:
                    kk = km_ref[(qb - 1) * BLOCK:(qb + 1) * BLOCK, cs]
                    vv = vm_ref[(qb - 1) * BLOCK:(qb + 1) * BLOCK, cs]
                    valid = band
                outs, lses = [], []
                for e, hm in enumerate((m0, m1)):
                    qe = jnp.where(hm, q, zero)
                    s = lax.dot_general(qe, kk, (((1,), (1,)), ((), ())), preferred_element_type=F32) * scale
                    s = jnp.where(valid, s, -jnp.inf)
                    if has_sink:
                        s = jnp.where(kj == 0, refs[5][ub][:, e * HEAD_DIM:e * HEAD_DIM + 1], s)
                    mx = jnp.max(s, axis=-1, keepdims=True)
                    p = jnp.exp(s - mx)
                    den = jnp.sum(p, axis=-1, keepdims=True)
                    pn = p * (1.0 / den)
                    if has_sink:
                        pn = jnp.where(kj == 0, 0.0, pn)
                    pn = pn.astype(BF16)
                    outs.append(jnp.dot(pn, vv, preferred_element_type=F32))
                    lses.append(mx + jnp.log(den))
                o_ref[rs, cs] = jnp.where(m0, outs[0], outs[1])
                lse_ref[rs, cs] = jnp.where(m0, jnp.broadcast_to(lses[0], (BLOCK, LANES)),
                                            jnp.broadcast_to(lses[1], (BLOCK, LANES)))

    def main(sec):
        return pl.BlockSpec((qn * BLOCK, un * LANES), lambda u, i: (i, sec * ug + u))

    def prev(sec):
        return pl.BlockSpec((BLOCK, un * LANES), lambda u, i: (jnp.maximum(i * qn - 1, 0), sec * ug + u))

    in_specs = [main(0), prev(1), main(1), prev(2), main(2)]
    args = [qkv] * 5
    if has_sink:
        in_specs.append(pl.BlockSpec((un, 1, LANES), lambda u, i: (u, 0, 0)))
        args.append(sinks)
    return pl.pallas_call(
        body, name=name, grid=(ug, nb // qn), in_specs=in_specs, out_specs=(main(0), main(0)),
        out_shape=(jax.ShapeDtypeStruct((n_rows, n_units * LANES), F32),) * 2, compiler_params=_cparams(2),
    )(*args)


def _band_bwd(qkv, do, lse, delta, n_units, max_dist, name):
    n_rows = qkv.shape[0]
    nb = n_rows // BLOCK
    scale = HEAD_DIM ** -0.5

    qn, un = min(nb, BAND_Q_BLOCKS), BAND_UNITS
    ug = n_units // un
    steps = nb // qn
    nt_dims = (((1,), (1,)), ((), ()))
    tn_dims = (((0,), (0,)), ((), ()))

    def body(qm_ref, qx_ref, kp_ref, km_ref, vp_ref, vm_ref, dom_ref, dox_ref, lm_ref, lx_ref, dm_ref, dx_ref,
             dq_ref, dk_ref, dv_ref):
        i = pl.program_id(1)
        m0, m1 = _head_masks((BLOCK, LANES))
        zero = jnp.zeros((BLOCK, LANES), BF16)
        qi = lax.broadcasted_iota(jnp.int32, (BLOCK, 2 * BLOCK), 0)
        kj = lax.broadcasted_iota(jnp.int32, (BLOCK, 2 * BLOCK), 1)
        dist = qi + BLOCK - kj
        band = (dist >= 0) & (dist <= max_dist)
        band_first = band & ((i > 0) | (kj >= BLOCK))
        qr = lax.broadcasted_iota(jnp.int32, (BLOCK, BLOCK), 0)
        kc = lax.broadcasted_iota(jnp.int32, (BLOCK, BLOCK), 1)
        dist_x = qr + BLOCK - kc
        band_next = (dist_x >= 0) & (dist_x <= max_dist) & (i < steps - 1)

        def pair(q, dob, lse_b, del_b, kk, vv, valid):
            dqs, dk, dv = [], None, None
            for e, hm in enumerate((m0, m1)):
                col = slice(e * HEAD_DIM, e * HEAD_DIM + 1)
                qe = jnp.where(hm, q, zero)
                doe = jnp.where(hm, dob, zero)
                s = lax.dot_general(qe, kk, nt_dims, preferred_element_type=F32) * scale
                p = jnp.where(valid, jnp.exp(s - lse_b[:, col]), 0.0)
                dp = lax.dot_general(doe, vv, nt_dims, preferred_element_type=F32)
                ds = (p * (dp - del_b[:, col]) * scale).astype(BF16)
                dqs.append(jnp.dot(ds, kk, preferred_element_type=F32))
                dk_e = lax.dot_general(ds, qe, tn_dims, preferred_element_type=F32)
                dv_e = lax.dot_general(p.astype(BF16), doe, tn_dims, preferred_element_type=F32)
                dk = dk_e if dk is None else dk + dk_e
                dv = dv_e if dv is None else dv + dv_e
            return jnp.where(m0, dqs[0], dqs[1]), dk, dv

        for ub in range(un):
            cs = slice(ub * LANES, (ub + 1) * LANES)
            dk_acc, dv_acc = [None] * qn, [None] * qn

            def add(acc, kb, part):
                acc[kb] = part if acc[kb] is None else acc[kb] + part

            for qb in range(qn):
                rs = slice(qb * BLOCK, (qb + 1) * BLOCK)
                if qb == 0:
                    kk = jnp.concatenate([kp_ref[:, cs], km_ref[0:BLOCK, cs]], axis=0)
                    vv = jnp.concatenate([vp_ref[:, cs], vm_ref[0:BLOCK, cs]], axis=0)
                    valid = band_first
                else:
                    kk = km_ref[(qb - 1) * BLOCK:(qb + 1) * BLOCK, cs]
                    vv = vm_ref[(qb - 1) * BLOCK:(qb + 1) * BLOCK, cs]
                    valid = band
                dq, dk, dv = pair(qm_ref[rs, cs], dom_ref[rs, cs], lm_ref[rs, cs], dm_ref[rs, cs], kk, vv, valid)
                dq_ref[rs, cs] = dq
                if qb > 0:
                    add(dk_acc, qb - 1, dk[0:BLOCK])
                    add(dv_acc, qb - 1, dv[0:BLOCK])
                add(dk_acc, qb, dk[BLOCK:2 * BLOCK])
                add(dv_acc, qb, dv[BLOCK:2 * BLOCK])
            last = slice((qn - 1) * BLOCK, qn * BLOCK)
            _, dk, dv = pair(qx_ref[:, cs], dox_ref[:, cs], lx_ref[:, cs], dx_ref[:, cs], km_ref[last, cs], vm_ref[last, cs],
                             band_next)
            add(dk_acc, qn - 1, dk)
            add(dv_acc, qn - 1, dv)
            for kb in range(qn):
                dk_ref[kb * BLOCK:(kb + 1) * BLOCK, cs] = dk_acc[kb]
                dv_ref[kb * BLOCK:(kb + 1) * BLOCK, cs] = dv_acc[kb]

    def main(sec):
        return pl.BlockSpec((qn * BLOCK, un * LANES), lambda u, i: (i, sec * ug + u))

    def prev(sec):
        return pl.BlockSpec((BLOCK, un * LANES), lambda u, i: (jnp.maximum(i * qn - 1, 0), sec * ug + u))

    def nxt(sec):
        return pl.BlockSpec((BLOCK, un * LANES), lambda u, i: (jnp.minimum((i + 1) * qn, nb - 1), sec * ug + u))

    in_specs = [main(0), nxt(0), prev(1), main(1), prev(2), main(2),
                main(0), nxt(0), main(0), nxt(0), main(0), nxt(0)]
    args = [qkv] * 6 + [do, do, lse, lse, delta, delta]
    shp = jax.ShapeDtypeStruct((n_rows, n_units * LANES), F32)
    return pl.pallas_call(
        body, name=name, grid=(ug, steps), in_specs=in_specs, out_specs=(main(0), main(0), main(0)),
        out_shape=(shp, shp, shp), compiler_params=_cparams(2),
    )(*args)


def _merge_groups(os_, lses, dils, name):
    s_len = os_[0].shape[0] * dils[0]
    tm = 512

    def body(*refs):
        o_refs, l_refs = refs[0:3], refs[3:6]
        o_ref, lse_ref = refs[6:8]
        so, sl = refs[8:11], refs[11:14]
        for pair in range(2):
            for g, d in enumerate(dils):
                rows = tm // d
                for r in range(d):
                    col = slice((pair * d + r) * LANES, (pair * d + r + 1) * LANES)
                    if d == 1:
                        so[g][...] = o_refs[g][:, col]
                        sl[g][...] = l_refs[g][:, col]
                    else:
                        so[g][pl.ds(r, rows, stride=d), :] = o_refs[g][:, col]
                        sl[g][pl.ds(r, rows, stride=d), :] = l_refs[g][:, col]
            l0, l1, l2 = sl[0][...], sl[1][...], sl[2][...]
            mx = jnp.maximum(jnp.maximum(l0, l1), l2)
            e0, e1, e2 = jnp.exp(l0 - mx), jnp.exp(l1 - mx), jnp.exp(l2 - mx)
            den = e0 + e1 + e2
            inv = 1.0 / den
            o_ref[:, pair * LANES:(pair + 1) * LANES] = (so[0][...] * (e0 * inv) + so[1][...] * (e1 * inv)
                                                         + so[2][...] * (e2 * inv))
            lse_ref[:, pair * LANES:(pair + 1) * LANES] = mx + jnp.log(den)

    in_specs = [pl.BlockSpec((tm // d, 2 * d * LANES), lambda i: (i, 0)) for d in dils] * 2
    out = pl.BlockSpec((tm, 2 * LANES), lambda i: (i, 0))
    shp = jax.ShapeDtypeStruct((s_len, 2 * LANES), F32)
    return pl.pallas_call(
        body, name=name, grid=(s_len // tm,), in_specs=in_specs, out_specs=(out, out), out_shape=(shp, shp),
        scratch_shapes=[pltpu.VMEM((tm, LANES), F32)] * 6, compiler_params=_cparams(1),
    )(*os_, *lses)


def _bwd_prep(do, o, lse, dils, sinks, name):
    s_len, width = do.shape
    n_pairs = width // LANES
    tm = 512
    has_sink = sinks is not None
    n_g = len(dils)

    def body(*refs):
        do_ref, o_ref, lse_ref = refs[:3]
        pos = 3
        if has_sink:
            sink_ref = refs[pos]
            pos += 1
        outs = refs[pos:pos + 3 * n_g]
        pos += 3 * n_g
        if has_sink:
            dsink_ref = refs[pos]
            pos += 1
        s_do, s_l, s_d = refs[pos:pos + 3]
        seg = _seg_matrix(HEAD_DIM)

        if has_sink:
            @pl.when(pl.program_id(0) == 0)
            def _():
                dsink_ref[...] = jnp.zeros_like(dsink_ref)

        for pair in range(n_pairs):
            col = slice(pair * LANES, (pair + 1) * LANES)
            dov = do_ref[:, col]
            lv = lse_ref[:, col]
            delta = _seg_sum(dov * o_ref[:, col], seg)
            if has_sink:
                dsink_ref[pair] += -jnp.sum(jnp.exp(sink_ref[pair] - lv) * delta, axis=0, keepdims=True)
            s_do[...] = dov
            s_l[...] = lv
            s_d[...] = delta
            for g, d in enumerate(dils):
                rows = tm // d
                for r in range(d):
                    oc = slice((pair * d + r) * LANES, (pair * d + r + 1) * LANES)
                    if d == 1:
                        a, b, c = s_do[...], s_l[...], s_d[...]
                    else:
                        a = s_do[pl.ds(r, rows, stride=d), :]
                        b = s_l[pl.ds(r, rows, stride=d), :]
                        c = s_d[pl.ds(r, rows, stride=d), :]
                    outs[3 * g][:, oc] = a.astype(BF16)
                    outs[3 * g + 1][:, oc] = b
                    outs[3 * g + 2][:, oc] = c

    row = pl.BlockSpec((tm, width), lambda i: (i, 0))
    in_specs = [row, row, row]
    args = [do, o, lse]
    if has_sink:
        in_specs.append(pl.BlockSpec((n_pairs, 1, LANES), lambda i: (0, 0, 0)))
        args.append(sinks)
    out_specs, out_shape = [], []
    for d in dils:
        for dt in (BF16, F32, F32):
            out_specs.append(pl.BlockSpec((tm // d, n_pairs * d * LANES), lambda i: (i, 0)))
            out_shape.append(jax.ShapeDtypeStruct((s_len // d, n_pairs * d * LANES), dt))
    if has_sink:
        out_specs.append(pl.BlockSpec((n_pairs, 1, LANES), lambda i: (0, 0, 0)))
        out_shape.append(jax.ShapeDtypeStruct((n_pairs, 1, LANES), F32))
    return pl.pallas_call(
        body, name=name, grid=(s_len // tm,), in_specs=in_specs, out_specs=tuple(out_specs),
        out_shape=tuple(out_shape), scratch_shapes=[pltpu.VMEM((tm, LANES), F32)] * 3, compiler_params=_cparams(1),
    )(*args)


def _mem_kv(mem, mem_gain, w_kv, k_gain, name):
    m_len = mem.shape[0]
    kw = M_HEADS * M_HEAD_DIM

    def body(mem_ref, mg_ref, w_ref, kg_ref, k_ref, v_ref):
        mv = mem_ref[...]
        r = lax.rsqrt(jnp.mean(mv * mv, axis=-1, keepdims=True) + EPS)
        mn = ((mv * r) * mg_ref[...]).astype(BF16)
        kv = jnp.dot(mn, w_ref[...], preferred_element_type=F32)
        for h in range(M_HEADS):
            col = slice(h * M_HEAD_DIM, (h + 1) * M_HEAD_DIM)
            t = kv[:, col]
            rk = lax.rsqrt(jnp.mean(t * t, axis=-1, keepdims=True) + EPS)
            k_ref[:, col] = ((t * rk) * kg_ref[...]).astype(BF16)
        v_ref[...] = kv[:, kw:].astype(BF16)

    shp = jax.ShapeDtypeStruct((m_len, kw), BF16)
    return pl.pallas_call(body, name=name, out_shape=(shp, shp),
                          compiler_params=pltpu.CompilerParams(vmem_limit_bytes=VMEM_LIMIT_BYTES))(mem, mem_gain, w_kv, k_gain)


def _mem_kv_bwd(mem, mem_gain, w_kv, k_gain, dk, dv, name):
    m_len, d = mem.shape
    kw = M_HEADS * M_HEAD_DIM

    def body(mem_ref, mg_ref, w_ref, kg_ref, dk_ref, dv_ref, dw_ref, dmg_ref, dkg_ref, dkv_ref):
        mv = mem_ref[...]
        r = lax.rsqrt(jnp.mean(mv * mv, axis=-1, keepdims=True) + EPS)
        mhat = mv * r
        mn = (mhat * mg_ref[...]).astype(BF16)
        kv = jnp.dot(mn, w_ref[...], preferred_element_type=F32)
        dkg = jnp.zeros((1, M_HEAD_DIM), F32)
        for h in range(M_HEADS):
            col = slice(h * M_HEAD_DIM, (h + 1) * M_HEAD_DIM)
            t = kv[:, col]
            rk = lax.rsqrt(jnp.mean(t * t, axis=-1, keepdims=True) + EPS)
            that = t * rk
            dy = dk_ref[:, col]
            dkg = dkg + jnp.sum(dy * that, axis=0, keepdims=True)
            dthat = dy * kg_ref[...]
            dkv_ref[:, col] = (rk * (dthat - that * jnp.mean(dthat * that, axis=-1, keepdims=True))).astype(BF16)
        dkv_ref[:, kw:] = dv_ref[...].astype(BF16)
        dkg_ref[...] = dkg
        dkv = dkv_ref[...]
        dw_ref[...] = lax.dot_general(mn, dkv, (((0,), (0,)), ((), ())), preferred_element_type=F32).astype(GRAD_DTYPE)
        dmn = lax.dot_general(dkv, w_ref[...], (((1,), (1,)), ((), ())), preferred_element_type=F32)
        dmg_ref[...] = jnp.sum(dmn * mhat, axis=0, keepdims=True)

    return pl.pallas_call(
        body, name=name,
        out_shape=(jax.ShapeDtypeStruct((d, 2 * kw), GRAD_DTYPE), jax.ShapeDtypeStruct((1, d), F32),
                   jax.ShapeDtypeStruct((1, M_HEAD_DIM), F32)),
        scratch_shapes=[pltpu.VMEM((m_len, 2 * kw), BF16)],
        compiler_params=pltpu.CompilerParams(vmem_limit_bytes=VMEM_LIMIT_BYTES),
    )(mem, mem_gain, w_kv, k_gain, dk, dv)


def _mem_attn_fwd(proj, cidx, mk, mv, q_gain, name):
    s_len = proj.shape[0]
    kw = M_HEADS * M_HEAD_DIM
    tm = 512
    scale = M_HEAD_DIM ** -0.5

    def body(q_ref, k_ref, v_ref, g_ref, o_ref):
        for h in range(M_HEADS):
            col = slice(h * M_HEAD_DIM, (h + 1) * M_HEAD_DIM)
            t = q_ref[:, col]
            rs = lax.rsqrt(jnp.mean(t * t, axis=-1, keepdims=True) + EPS)
            qn = ((t * rs) * g_ref[...]).astype(BF16)
            s = lax.dot_general(qn, k_ref[:, col], (((1,), (1,)), ((), ())), preferred_element_type=F32) * scale
            mx = jnp.max(s, axis=-1, keepdims=True)
            p = jnp.exp(s - mx)
            pn = (p * (1.0 / jnp.sum(p, axis=-1, keepdims=True))).astype(BF16)
            o_ref[:, col] = jnp.dot(pn, v_ref[:, col], preferred_element_type=F32).astype(BF16)

    whole = pl.BlockSpec((MEM_LEN, kw), lambda i: (0, 0))
    return pl.pallas_call(
        body, name=name, grid=(s_len // tm,),
        in_specs=[pl.BlockSpec((tm, kw), lambda i: (i, cidx)), whole, whole, pl.BlockSpec((1, M_HEAD_DIM), lambda i: (0, 0))],
        out_specs=pl.BlockSpec((tm, kw), lambda i: (i, 0)),
        out_shape=jax.ShapeDtypeStruct((s_len, kw), BF16), compiler_params=_cparams(1),
    )(proj, mk, mv, q_gain)


def _mem_attn_bwd(proj, cidx, mk, mv, q_gain, do, name):
    s_len = proj.shape[0]
    kw = M_HEADS * M_HEAD_DIM
    tm = 512
    scale = M_HEAD_DIM ** -0.5

    def body(q_ref, k_ref, v_ref, g_ref, do_ref, dq_ref, dk_ref, dv_ref, dg_ref):
        @pl.when(pl.program_id(0) == 0)
        def _():
            dk_ref[...] = jnp.zeros_like(dk_ref)
            dv_ref[...] = jnp.zeros_like(dv_ref)
            dg_ref[...] = jnp.zeros_like(dg_ref)

        for h in range(M_HEADS):
            col = slice(h * M_HEAD_DIM, (h + 1) * M_HEAD_DIM)
            t = q_ref[:, col]
            rs = lax.rsqrt(jnp.mean(t * t, axis=-1, keepdims=True) + EPS)
            that = t * rs
            qn = (that * g_ref[...]).astype(BF16)
            kh, vh = k_ref[:, col], v_ref[:, col]
            dob = do_ref[:, col].astype(BF16)
            s = lax.dot_general(qn, kh, (((1,), (1,)), ((), ())), preferred_element_type=F32) * scale
            mx = jnp.max(s, axis=-1, keepdims=True)
            p = jnp.exp(s - mx)
            p = p * (1.0 / jnp.sum(p, axis=-1, keepdims=True))
            dp = lax.dot_general(dob, vh, (((1,), (1,)), ((), ())), preferred_element_type=F32)
            ds = (p * (dp - jnp.sum(p * dp, axis=-1, keepdims=True)) * scale).astype(BF16)
            dqn = jnp.dot(ds, kh, preferred_element_type=F32)
            dk_ref[:, col] += lax.dot_general(ds, qn, (((0,), (0,)), ((), ())), preferred_element_type=F32)
            dv_ref[:, col] += lax.dot_general(p.astype(BF16), dob, (((0,), (0,)), ((), ())), preferred_element_type=F32)
            dg_ref[...] += jnp.sum(dqn * that, axis=0, keepdims=True)
            dthat = dqn * g_ref[...]
            dq_ref[:, col] = (rs * (dthat - that * jnp.mean(dthat * that, axis=-1, keepdims=True))).astype(BF16)

    whole = pl.BlockSpec((MEM_LEN, kw), lambda i: (0, 0))
    vec = pl.BlockSpec((1, M_HEAD_DIM), lambda i: (0, 0))
    row = pl.BlockSpec((tm, kw), lambda i: (i, 0))
    return pl.pallas_call(
        body, name=name, grid=(s_len // tm,),
        in_specs=[pl.BlockSpec((tm, kw), lambda i: (i, cidx)), whole, whole, vec, row],
        out_specs=(row, whole, whole, vec),
        out_shape=(jax.ShapeDtypeStruct((s_len, kw), BF16), jax.ShapeDtypeStruct((MEM_LEN, kw), F32),
                   jax.ShapeDtypeStruct((MEM_LEN, kw), F32), jax.ShapeDtypeStruct((1, M_HEAD_DIM), F32)),
        compiler_params=_cparams(1),
    )(proj, mk, mv, q_gain, do)


def _project_merge(outs, w_outs, gates, w_out, x, name):
    s_len = gates.shape[0]
    d = w_outs[0].shape[1]
    tm = 512

    def body(oa_ref, ob_ref, om_ref, wa_ref, wb_ref, wm_ref, g_ref, wo_ref, x_ref,
             pa_ref, pb_ref, pm_ref, merged_ref, x1_ref):
        merged = None
        for k, (o_ref, w_ref, p_ref) in enumerate(((oa_ref, wa_ref, pa_ref), (ob_ref, wb_ref, pb_ref), (om_ref, wm_ref, pm_ref))):
            p = jnp.dot(o_ref[...].astype(BF16), w_ref[...], preferred_element_type=F32).astype(BF16)
            p_ref[...] = p
            t = g_ref[:, k * d:(k + 1) * d].astype(F32) * p.astype(F32)
            merged = t if merged is None else merged + t
        merged = merged.astype(BF16)
        merged_ref[...] = merged
        x1_ref[...] = x_ref[...] + jnp.dot(merged, wo_ref[...], preferred_element_type=F32)

    row = pl.BlockSpec((tm, d), lambda i: (i, 0))
    shp = jax.ShapeDtypeStruct((s_len, d), BF16)
    in_specs = [pl.BlockSpec((tm, o.shape[1]), lambda i: (i, 0)) for o in outs]
    in_specs += [_resident(w.shape, lambda i: (0, 0)) for w in w_outs]
    in_specs += [pl.BlockSpec((tm, 3 * d), lambda i: (i, 0)), _resident(w_out.shape, lambda i: (0, 0)), row]
    return pl.pallas_call(
        body, name=name, grid=(s_len // tm,), in_specs=in_specs, out_specs=(row, row, row, row, row),
        out_shape=(shp, shp, shp, shp, jax.ShapeDtypeStruct((s_len, d), F32)), compiler_params=_cparams(1),
    )(*outs, *w_outs, gates, w_out, x)


def _project_merge_bwd(dx1, w_out, gates, pa, pb, pm, name):
    s_len, d = pa.shape
    tm = 512

    def body(dx_ref, w_ref, g_ref, a_ref, b_ref, m_ref, da_ref, db_ref, dmm_ref, dg_ref, dbg_ref):
        @pl.when(pl.program_id(0) == 0)
        def _():
            dbg_ref[...] = jnp.zeros_like(dbg_ref)
        dm = lax.dot_general(dx_ref[...], w_ref[...], (((1,), (1,)), ((), ())), preferred_element_type=F32)
        for k, (p_ref, dp_ref) in enumerate(((a_ref, da_ref), (b_ref, db_ref), (m_ref, dmm_ref))):
            col = slice(k * d, (k + 1) * d)
            g = g_ref[:, col].astype(F32)
            dp_ref[...] = (dm * g).astype(BF16)
            dpre = (dm * p_ref[...].astype(F32)) * (g * (1.0 - g))
            dbg_ref[:, col] += jnp.sum(dpre, axis=0, keepdims=True)
            dg_ref[:, col] = dpre.astype(BF16)

    row = pl.BlockSpec((tm, d), lambda i: (i, 0))
    wide = pl.BlockSpec((tm, 3 * d), lambda i: (i, 0))
    shp = jax.ShapeDtypeStruct((s_len, d), BF16)
    return pl.pallas_call(
        body, name=name, grid=(s_len // tm,), in_specs=[row, _resident(w_out.shape, lambda i: (0, 0)), wide, row, row, row],
        out_specs=(row, row, row, wide, pl.BlockSpec((1, 3 * d), lambda i: (0, 0))),
        out_shape=(shp, shp, shp, jax.ShapeDtypeStruct((s_len, 3 * d), BF16), jax.ShapeDtypeStruct((1, 3 * d), F32)),
        compiler_params=_cparams(1),
    )(dx1, w_out, gates, pa, pb, pm)


CONV_CHUNK = 1024


def _pick_row(tile, j):
    row = lax.broadcasted_iota(jnp.int32, tile.shape, 0)
    return jnp.sum(jnp.where(row == j, tile, jnp.zeros_like(tile)), axis=0, keepdims=True)


def _rows_before(ref, start, k):
    cur = ref[pl.ds(start, CONV_CHUNK), :].astype(F32)
    prev = ref[pl.ds(pl.multiple_of(jnp.maximum(start - 16, 0), 16), 16), :].astype(F32)
    prev = jnp.where(start > 0, prev, jnp.zeros_like(prev))
    rolled = pltpu.roll(cur, k, 0)
    row = lax.broadcasted_iota(jnp.int32, cur.shape, 0)
    for j in range(k):
        rolled = jnp.where(row == j, _pick_row(prev, 16 - k + j), rolled)
    return rolled


def _rows_after(ref, start, k):
    cur = ref[pl.ds(start, CONV_CHUNK), :]
    nxt = ref[pl.ds(pl.multiple_of(start + CONV_CHUNK, 8), 8), :]
    rolled = pltpu.roll(cur, CONV_CHUNK - k, 0)
    row = lax.broadcasted_iota(jnp.int32, cur.shape, 0)
    for j in range(k):
        rolled = jnp.where(row == CONV_CHUNK - k + j, _pick_row(nxt, j), rolled)
    return rolled


def _conv_pre(u_ref, w_ref, b_ref, start):
    u2 = _rows_before(u_ref, start, 2)
    u1 = _rows_before(u_ref, start, 1)
    u0 = u_ref[pl.ds(start, CONV_CHUNK), :].astype(F32)
    c = ((b_ref[...] + w_ref[0:1, :] * u2) + w_ref[1:2, :] * u1) + w_ref[2:3, :] * u0
    return c, (u2, u1, u0)


def _norm_up_conv_glu(x, gain, w_up, conv_w, conv_b, name):
    s_len, d = x.shape
    tm, tn = 512, 2 * LANES
    nblk = D_FF // tn

    def body(x_ref, g_ref, w_ref, cw_ref, cb_ref, ht_ref, r_ref, u_ref, act_ref, halo):
        @pl.when(pl.program_id(0) == 0)
        def _():
            halo[...] = jnp.zeros_like(halo)
        xv = x_ref[...]
        r = lax.rsqrt(jnp.mean(xv * xv, axis=-1, keepdims=True) + EPS)
        hv = ((xv * r) * g_ref[...]).astype(BF16)
        ht_ref[...] = hv.T
        r_ref[...] = r
        row = lax.broadcasted_iota(jnp.int32, (tm, tn), 0)
        for j in range(nblk):
            conv = []
            for half in range(2):
                cb = half * nblk + j
                cols = slice(cb * tn, (cb + 1) * tn)
                ub = jnp.dot(hv, w_ref[:, cols], preferred_element_type=F32).astype(BF16)
                u_ref[:, cols] = ub
                u0 = ub.astype(F32)
                prev = halo[cb]
                u1 = jnp.where(row == 0, _pick_row(prev, 7), pltpu.roll(u0, 1, 0))
                u2 = pltpu.roll(u0, 2, 0)
                u2 = jnp.where(row == 0, _pick_row(prev, 6), jnp.where(row == 1, _pick_row(prev, 7), u2))
                halo[cb] = u0[tm - 8:tm, :]
                conv.append(((cb_ref[:, cols] + cw_ref[0:1, cols] * u2) + cw_ref[1:2, cols] * u1)
                            + cw_ref[2:3, cols] * u0)
            act_ref[:, j * tn:(j + 1) * tn] = ((conv[0] * _sigmoid(conv[0])) * conv[1]).astype(BF16)

    return pl.pallas_call(
        body, name=name, grid=(s_len // tm,),
        in_specs=[pl.BlockSpec((tm, d), lambda i: (i, 0)), _resident((1, d), lambda i: (0, 0)),
                  _resident((d, 2 * D_FF), lambda i: (0, 0)),
                  _resident((3, 2 * D_FF), lambda i: (0, 0)), _resident((1, 2 * D_FF), lambda i: (0, 0))],
        out_specs=(pl.BlockSpec((d, tm), lambda i: (0, i)), pl.BlockSpec((tm, 1), lambda i: (i, 0)),
                   pl.BlockSpec((tm, 2 * D_FF), lambda i: (i, 0)), pl.BlockSpec((tm, D_FF), lambda i: (i, 0))),
        out_shape=(jax.ShapeDtypeStruct((d, s_len), BF16), jax.ShapeDtypeStruct((s_len, 1), F32),
                   jax.ShapeDtypeStruct((s_len, 2 * D_FF), BF16), jax.ShapeDtypeStruct((s_len, D_FF), BF16)),
        scratch_shapes=[pltpu.VMEM((2 * nblk, 8, tn), F32)], compiler_params=_cparams(1),
    )(x, gain, w_up, conv_w, conv_b)


def _conv_glu_bwd(dact, u, conv_w, conv_b, name):
    s_len = u.shape[0]
    nblk = D_FF // LANES
    n_chunks = s_len // CONV_CHUNK

    def body(da_ref, ua_ref, ug_ref, wa_ref, wg_ref, ba_ref, bg_ref,
             dua_ref, dug_ref, dwa_ref, dwg_ref, dba_ref, dbg_ref, sa, sg):
        sa[pl.ds(s_len, 8), :] = jnp.zeros((8, LANES), F32)
        sg[pl.ds(s_len, 8), :] = jnp.zeros((8, LANES), F32)
        zero = jnp.zeros((1, LANES), F32)

        def chunk1(ci, carry):
            start = pl.multiple_of(ci * CONV_CHUNK, CONV_CHUNK)
            ca, ua = _conv_pre(ua_ref, wa_ref, ba_ref, start)
            cg, ug = _conv_pre(ug_ref, wg_ref, bg_ref, start)
            dact_v = da_ref[pl.ds(start, CONV_CHUNK), :].astype(F32)
            sig = _sigmoid(ca)
            dcg = dact_v * (ca * sig)
            dca = (dact_v * cg) * (sig * (1.0 + ca * (1.0 - sig)))
            sa[pl.ds(start, CONV_CHUNK), :] = dca
            sg[pl.ds(start, CONV_CHUNK), :] = dcg
            out = [carry[0] + jnp.sum(dca, axis=0, keepdims=True), carry[1] + jnp.sum(dcg, axis=0, keepdims=True)]
            for j in range(3):
                out.append(carry[2 + j] + jnp.sum(dca * ua[j], axis=0, keepdims=True))
            for j in range(3):
                out.append(carry[5 + j] + jnp.sum(dcg * ug[j], axis=0, keepdims=True))
            return tuple(out)

        acc = lax.fori_loop(0, n_chunks, chunk1, (zero,) * 8)
        dba_ref[...] = acc[0]
        dbg_ref[...] = acc[1]
        for j in range(3):
            dwa_ref[j:j + 1, :] = acc[2 + j]
            dwg_ref[j:j + 1, :] = acc[5 + j]

        def chunk2(ci, carry):
            start = pl.multiple_of(ci * CONV_CHUNK, CONV_CHUNK)
            for s_ref, w_ref, o_ref in ((sa, wa_ref, dua_ref), (sg, wg_ref, dug_ref)):
                d0 = s_ref[pl.ds(start, CONV_CHUNK), :]
                d1 = _rows_after(s_ref, start, 1)
                d2 = _rows_after(s_ref, start, 2)
                o_ref[pl.ds(start, CONV_CHUNK), :] = (w_ref[2:3, :] * d0 + w_ref[1:2, :] * d1
                                                      + w_ref[0:1, :] * d2).astype(BF16)
            return carry
        lax.fori_loop(0, n_chunks, chunk2, 0)

    def col(rows, off):
        return pl.BlockSpec((rows, LANES), lambda j: (0, off + j))

    big = jax.ShapeDtypeStruct((s_len, D_FF), BF16)
    return pl.pallas_call(
        body, name=name, grid=(nblk,),
        in_specs=[col(s_len, 0), col(s_len, 0), col(s_len, nblk), col(3, 0), col(3, nblk), col(1, 0), col(1, nblk)],
        out_specs=(col(s_len, 0), col(s_len, 0), col(3, 0), col(3, 0), col(1, 0), col(1, 0)),
        out_shape=(big, big, jax.ShapeDtypeStruct((3, D_FF), F32), jax.ShapeDtypeStruct((3, D_FF), F32),
                   jax.ShapeDtypeStruct((1, D_FF), F32), jax.ShapeDtypeStruct((1, D_FF), F32)),
        scratch_shapes=[pltpu.VMEM((s_len + 8, LANES), F32)] * 2, compiler_params=_cparams(1),
    )(dact, u, u, conv_w, conv_w, conv_b, conv_b)


def _rope_tables(positions):
    half = ROPE_DIMS // 2
    freqs = jnp.exp(jnp.arange(half, dtype=F32) * (-2.0 * math.log(ROPE_THETA) / ROPE_DIMS))
    ang = positions.reshape(-1).astype(F32)[:, None] * freqs
    cos, sin = jnp.cos(ang), jnp.sin(ang)
    n = ang.shape[0]
    zeros = lambda w: jnp.zeros((n, w), F32)
    c = jnp.concatenate([cos, cos, jnp.ones((n, HEAD_DIM - ROPE_DIMS), F32)], axis=1)
    s1 = jnp.concatenate([-sin, zeros(HEAD_DIM - half)], axis=1)
    s2 = jnp.concatenate([zeros(half), sin, zeros(HEAD_DIM - ROPE_DIMS)], axis=1)
    return tuple(jnp.tile(t, (1, 2)) for t in (c, s1, s2))


def _two(v):
    return jnp.tile(v.reshape(1, HEAD_DIM), (1, 2))


def _fold_heads(g):
    return g[0, :HEAD_DIM] + g[0, HEAD_DIM:]


MIX_WEIGHTS = ('w_gate', 'w_mem_kv', 'w_o_a', 'w_o_b', 'w_o_m', 'w_out')
FFN_WEIGHTS = ('w_up', 'conv_w', 'w_down')


def _device_step(x, mem, positions, target, w, hooks=None):
    tabs = _rope_tables(positions)
    dils = tuple(d for _, d in A_GROUPS)
    grads = {}
    w = dict(w)

    h, h_t, r1 = _rms_fwd(x, w['attn_norm'], "rms1")
    if hooks is not None:
        w.update(hooks.weights('in', [h, *tabs]))
    proj = _mm_rows([(h, w['w_in'], 0)], "mm_in")

    qkv_a, o_g, lse_g = [], [], []
    for gi, (window, d) in enumerate(A_GROUPS):
        gq, gk = _two(w['a_q_norm'][gi]), _two(w['a_k_norm'][gi])
        qkv = _qk_prep(proj, 6 * gi, d, False, gq, gk, tabs, f"qk_prep_a{gi}")
        o, lse = _band_fwd(qkv, 2 * d, window // d, None, f"band_fwd_a{gi}")
        qkv_a.append(qkv)
        o_g.append(o)
        lse_g.append(lse)
    o_a, lse_a = _merge_groups(o_g, lse_g, dils, "merge_a")
    if hooks is not None:
        w.update(hooks.weights('mix', [o_a]))

    gbq, gbk = _two(w['b_q_norm']), _two(w['b_k_norm'])
    sinks = jnp.repeat(w['b_sinks'].reshape(4, 2), HEAD_DIM, axis=1).reshape(4, 1, LANES)
    qkv_b = _qk_prep(proj, 18, 1, True, gbq, gbk, tabs, "qk_prep_b")
    o_b, lse_b = _band_fwd(qkv_b, 4, B_WINDOW - 1, sinks, "band_fwd_b")

    gates = _mm_rows([(h, w['w_gate'], 0)], "mm_gate", bias=w['b_gate'], sigmoid=True, out_dtypes=(BF16,))
    mk, mv = _mem_kv(mem, w['mem_norm'], w['w_mem_kv'], w['m_k_norm'], "mem_kv")
    o_m = _mem_attn_fwd(proj, 6, mk, mv, w['m_q_norm'], "mem_attn")

    pa, pb, pm, merged, x1 = _project_merge((o_a, o_b, o_m), (w['w_o_a'], w['w_o_b'], w['w_o_m']), gates, w['w_out'], x,
                                            "project_merge")

    if hooks is not None:
        w.update(hooks.weights('ffn', [x1]))
    h2_t, r2, u, act = _norm_up_conv_glu(x1, w['ffn_norm'], w['w_up'], w['conv_w'], w['conv_b'], "norm_up_conv_glu")
    dy, dy_b, dact, loss = _mm_rows([(act, w['w_down'], 0)], "mm_down", res=x1, loss_target=target)

    grads['w_down'] = _mm_tn(act, dy_b, "mm_dw_down")
    du_a, du_g, dcw_a, dcw_g, dcb_a, dcb_g = _conv_glu_bwd(dact, u, w['conv_w'], w['conv_b'], "conv_glu_bwd")
    grads['conv_w'] = jnp.concatenate([dcw_a, dcw_g], axis=1)
    grads['conv_b'] = jnp.concatenate([dcb_a, dcb_g], axis=1)
    grads['w_up'] = _mm_cols(h2_t, [du_a, du_g], "mm_dw_up")
    ffn_gain = w['ffn_norm']
    if hooks is not None:
        ffn_gain = ffn_gain + hooks.grads('ffn', grads)[0:1, 0:1]
    dx1, dx1_b, grads['ffn_norm'] = _mm_rows([(du_a, w['w_up'], 0), (du_g, w['w_up'], 1)], "mm_d_h2", nt=True,
                                             rms_bwd=(x1, r2, ffn_gain, dy), out_dtypes=(F32, BF16))

    grads['w_out'] = _mm_tn(merged, dx1_b, "mm_dw_out")
    dpa, dpb, dpm, dgpre, grads['b_gate'] = _project_merge_bwd(dx1_b, w['w_out'], gates, pa, pb, pm,
                                                               "project_merge_bwd")
    do_a, do_b, do_m = _mm_rows_each([(dpa, w['w_o_a']), (dpb, w['w_o_b']), (dpm, w['w_o_m'])], "mm_d_o")
    grads['w_o_a'], grads['w_o_b'], grads['w_o_m'] = _mm_tn_each([(o_a, dpa), (o_b, dpb), (o_m, dpm)], "mm_dw_o")
    grads['w_gate'] = _mm_cols(h_t, [dgpre], "mm_dw_gate")
    dq_m, dmk, dmv, grads['m_q_norm'] = _mem_attn_bwd(proj, 6, mk, mv, w['m_q_norm'], do_m, "mem_attn_bwd")
    grads['w_mem_kv'], grads['mem_norm'], grads['m_k_norm'] = _mem_kv_bwd(
        mem, w['mem_norm'], w['w_mem_kv'], w['m_k_norm'], dmk, dmv, "mem_kv_bwd")
    a_gain = w['a_q_norm']
    if hooks is not None:
        a_gain = a_gain + hooks.grads('mix', grads)[0:1, 0:1]

    prep = _bwd_prep(do_a, o_a, lse_a, dils, None, "bwd_prep_a")
    dproj, dgq_a, dgk_a = [], [], []
    for gi, (window, d) in enumerate(A_GROUPS):
        gq, gk = _two(a_gain[gi]), _two(w['a_k_norm'][gi])
        dqkv = _band_bwd(qkv_a[gi], prep[3 * gi], prep[3 * gi + 1], prep[3 * gi + 2], 2 * d, window // d,
                         f"band_bwd_a{gi}")
        dp, dgq, dgk = _qk_prep_bwd(dqkv, proj, 6 * gi, d, False, gq, gk, tabs, f"qk_prep_bwd_a{gi}")
        dproj.append(dp)
        dgq_a.append(_fold_heads(dgq))
        dgk_a.append(_fold_heads(dgk))
    grads['a_q_norm'] = jnp.stack(dgq_a)
    grads['a_k_norm'] = jnp.stack(dgk_a)

    do_bu, lse_bu, delta_bu, dsink = _bwd_prep(do_b, o_b, lse_b, (1,), sinks, "bwd_prep_b")
    dqkv = _band_bwd(qkv_b, do_bu, lse_bu, delta_bu, 4, B_WINDOW - 1, "band_bwd_b")
    dp_b, dgq, dgk = _qk_prep_bwd(dqkv, proj, 18, 1, True, gbq, gbk, tabs, "qk_prep_bwd_b")
    dproj.append(dp_b)
    grads['b_q_norm'] = _fold_heads(dgq)
    grads['b_k_norm'] = _fold_heads(dgk)
    grads['b_sinks'] = jnp.stack([dsink[:, 0, 0], dsink[:, 0, HEAD_DIM]], axis=1).reshape(8)

    dproj.append(dq_m)

    cols = (0, 1, 2, 3, 6)
    grads['w_in'] = _mm_rows_cat(h_t, dproj, "mm_dw_in")
    attn_gain = w['attn_norm']
    if hooks is not None:
        attn_gain = attn_gain + hooks.grads('in', grads)[0:1, 0:1]
    grad_x, grads['attn_norm'] = _mm_rows(
        [(dp, w['w_in'], c) for dp, c in zip(dproj, cols)] + [(dgpre, w['w_gate'], 0)], "mm_d_h", nt=True,
        rms_bwd=(x, r1, attn_gain, dx1))
    return loss, grad_x, grads


def _coords():
    return lax.axis_index("x"), lax.axis_index("y"), lax.axis_index("c")


def _slot(p):
    return 4 * p[0] + 2 * p[1] + p[2]


ALL_PEERS = tuple(range(1, N_DEV))
CHIP_PEERS = (1, 4, 2, 6)
OTHER_CHIPS = (4, 2, 6)


def _peers(me, masks=ALL_PEERS):
    x, y, c = me
    return [(1 - x if mask & 4 else x, 1 - y if mask & 2 else y, 1 - c if mask & 1 else c) for mask in masks]


HBM_SPEC = pl.BlockSpec(memory_space=pltpu.HBM)


SEM_SPEC = pl.BlockSpec(memory_space=pltpu.SEMAPHORE)
SIDE_EFFECT = pltpu.SideEffectType.DATAFLOW_SIDE_EFFECTING


def _exchange_start(blocks, name, gather=False, masks=ALL_PEERS, after=None):
    n = len(blocks)
    n_peers = len(masks)
    n_in = 2 * n + (0 if after is None else 1)

    def body(*refs):
        ins, lands = refs[:n], refs[n:2 * n]
        send_sems, recv_sems = refs[n_in], refs[n_in + 1]
        token = refs[-1]
        me = _coords()
        peers = _peers(me, masks)
        for a in range(n):
            for k in range(n_peers):
                pltpu.make_async_remote_copy(
                    src_ref=ins[a] if gather else ins[a].at[_slot(peers[k])], dst_ref=lands[a].at[_slot(me)],
                    send_sem=send_sems.at[a * n_peers + k], recv_sem=recv_sems.at[a * n_peers + k],
                    device_id=peers[k], device_id_type=MESH).start()
        token[...] = jnp.zeros_like(token)

    land_shapes = [((N_DEV,) + b.shape) if gather else b.shape for b in blocks]
    hbm_in = [pltpu.HBM(b.shape, b.dtype) for b in blocks]
    hbm_land = [pltpu.HBM(s, b.dtype) for s, b in zip(land_shapes, blocks)]
    sems = pltpu.SemaphoreType.DMA((n * n_peers,))
    ins = [pltpu.with_memory_space_constraint(b, pltpu.HBM) for b in blocks]
    lands = [pltpu.with_memory_space_constraint(lax.empty(s, b.dtype), pltpu.HBM) for s, b in zip(land_shapes, blocks)]
    return pl.pallas_call(
        body, name=name, out_shape=(sems, sems, *hbm_in, *hbm_land, jax.ShapeDtypeStruct((8, LANES), F32)),
        in_specs=[HBM_SPEC] * (2 * n) + ([] if after is None else [pl.BlockSpec(memory_space=pl.ANY)]),
        out_specs=(SEM_SPEC, SEM_SPEC, *([HBM_SPEC] * (2 * n)), pl.BlockSpec(memory_space=pltpu.VMEM)),
        input_output_aliases={i: 2 + i for i in range(2 * n)},
        compiler_params=pltpu.CompilerParams(has_side_effects=SIDE_EFFECT),
    )(*ins, *lands, *([] if after is None else [after]))


def _exchange_wait(started, after, name, gather=False, masks=ALL_PEERS):
    n = (len(started) - 3) // 2
    n_peers = len(masks)
    send_sems, recv_sems = started[0], started[1]
    thru = started[2:2 + 2 * n]

    def body(*refs):
        ins, lands = refs[:n], refs[n:2 * n]
        send_ref, recv_ref = refs[2 * n], refs[2 * n + 1]
        me = _coords()
        peers = _peers(me, masks)
        for a in range(n):
            for k in range(n_peers):
                cp = pltpu.make_async_remote_copy(
                    src_ref=ins[a] if gather else ins[a].at[_slot(peers[k])], dst_ref=lands[a].at[_slot(peers[k])],
                    send_sem=send_ref.at[a * n_peers + k], recv_sem=recv_ref.at[a * n_peers + k],
                    device_id=peers[k], device_id_type=MESH)
                cp.wait_send()
                cp.wait_recv()

    hbm = [pltpu.HBM(t.shape, t.dtype) for t in thru]
    res = pl.pallas_call(
        body, name=name, out_shape=tuple(hbm),
        in_specs=[HBM_SPEC] * (2 * n) + [SEM_SPEC, SEM_SPEC] + [pl.BlockSpec(memory_space=pl.ANY)] * len(after),
        out_specs=tuple([HBM_SPEC] * (2 * n)), input_output_aliases={i: i for i in range(2 * n)},
        compiler_params=pltpu.CompilerParams(has_side_effects=SIDE_EFFECT),
    )(*thru, send_sems, recv_sems, *after)
    return res[n:]


def _sibling_forward(arrays, name):
    n = len(arrays)
    n_fwd = len(OTHER_CHIPS)

    def body(*refs):
        bufs = refs[n:2 * n]
        token, send_sems, recv_sems = refs[2 * n:]
        token[...] = jnp.zeros_like(token)
        x, y, c = _coords()
        sibling = (x, y, 1 - c)
        mine = _peers((x, y, c), OTHER_CHIPS)
        theirs = _peers(sibling, OTHER_CHIPS)

        def copy(a, k, block):
            rows = bufs[a].at[_slot(block)]
            return pltpu.make_async_remote_copy(
                src_ref=rows, dst_ref=rows, send_sem=send_sems.at[a * n_fwd + k], recv_sem=recv_sems.at[a * n_fwd + k],
                device_id=sibling, device_id_type=MESH)

        sends = [copy(a, k, mine[k]) for a in range(n) for k in range(n_fwd)]
        for cp in sends:
            cp.start()
        for a in range(n):
            for k in range(n_fwd):
                copy(a, k, theirs[k]).wait_recv()
        for cp in sends:
            cp.wait_send()

    res = pl.pallas_call(
        body, name=name, in_specs=[HBM_SPEC] * n,
        out_specs=tuple([HBM_SPEC] * n + [pl.BlockSpec(memory_space=pltpu.VMEM)]),
        out_shape=tuple([jax.ShapeDtypeStruct(a.shape, a.dtype) for a in arrays] + [jax.ShapeDtypeStruct((8, LANES), F32)]),
        input_output_aliases={i: i for i in range(n)},
        scratch_shapes=[pltpu.SemaphoreType.DMA((n * n_fwd,)), pltpu.SemaphoreType.DMA((n * n_fwd,))],
    )(*arrays)
    return res[:n], res[n]


def _all_sum(p, name):
    def body(p_ref, o_ref, recv, send_sems, recv_sems):
        me = _coords()
        peers = _peers(me)
        recv[_slot(me)] = p_ref[...]

        def copy(k, landing):
            return pltpu.make_async_remote_copy(
                src_ref=p_ref, dst_ref=recv.at[_slot(landing)], send_sem=send_sems.at[k], recv_sem=recv_sems.at[k],
                device_id=peers[k], device_id_type=MESH)

        sends = [copy(k, me) for k in range(N_DEV - 1)]
        for cp in sends:
            cp.start()
        for k in range(N_DEV - 1):
            copy(k, peers[k]).wait_recv()
        for cp in sends:
            cp.wait_send()
        acc = recv[0]
        for s in range(1, N_DEV):
            acc = acc + recv[s]
        o_ref[...] = acc

    vmem = pl.BlockSpec(memory_space=pltpu.VMEM)
    return pl.pallas_call(
        body, name=name, in_specs=[vmem], out_specs=vmem, out_shape=jax.ShapeDtypeStruct(p.shape, F32),
        scratch_shapes=[pltpu.VMEM((N_DEV,) + p.shape, F32), pltpu.SemaphoreType.DMA((N_DEV - 1,)),
                        pltpu.SemaphoreType.DMA((N_DEV - 1,))],
    )(p)


def _adam(w, g, m, v):
    m2 = ADAM_B1 * m + (1.0 - ADAM_B1) * g
    v2 = ADAM_B2 * v + (1.0 - ADAM_B2) * (g * g)
    m_hat = m2 / (1.0 - ADAM_B1 ** ADAM_STEP)
    v_hat = v2 / (1.0 - ADAM_B2 ** ADAM_STEP)
    delta = -ADAM_LR * (m_hat / (jnp.sqrt(v_hat) + ADAM_EPS) + ADAM_WD * w)
    return delta, m2, v2


def _row_tile(rows, cols):
    best = rows
    for t in range(16, rows, 16):
        if rows % t == 0 and t * cols * 4 <= (1 << 20):
            best = t
    return best


def _adam_reduce(parts, w, m, v, name):
    rows, cols = w.shape
    tr = _row_tile(rows, cols)

    def body(p_ref, w_ref, m_ref, v_ref, g_ref, d_ref, m2_ref, v2_ref):
        g = p_ref[0].astype(F32)
        for s in range(1, N_DEV):
            g = g + p_ref[s].astype(F32)
        g_ref[...] = g
        d_ref[...], m2_ref[...], v2_ref[...] = _adam(w_ref[...], g, m_ref[...], v_ref[...])

    blk = pl.BlockSpec((tr, cols), lambda i: (i, 0))
    shp = jax.ShapeDtypeStruct((rows, cols), F32)
    return pl.pallas_call(
        body, name=name, grid=(rows // tr,),
        in_specs=[pl.BlockSpec((N_DEV, tr, cols), lambda i: (0, i, 0)), blk, blk, blk],
        out_specs=(blk,) * 4, out_shape=(shp,) * 4, compiler_params=_cparams(1),
    )(parts, w, m, v)


PACK_COLS = 1024
PACK = {'attn_norm': (0, 1, 1024), 'mem_norm': (1, 1, 1024), 'ffn_norm': (2, 1, 1024), 'b_gate': (3, 3, 1024),
        'conv_b': (6, 6, 1024), 'a_q_norm': (12, 3, 64), 'a_k_norm': (15, 3, 64), 'b_q_norm': (18, 1, 64),
        'b_k_norm': (19, 1, 64), 'm_q_norm': (20, 1, 128), 'm_k_norm': (21, 1, 128), 'b_sinks': (22, 1, 8)}
PACK_LOSS_ROW = 23
PACK_ROWS = 24


def _pack_pieces(name, width):
    r0, nr, lanes = PACK[name]
    out = []
    for j in range(nr):
        if lanes == PACK_COLS:
            w = min(PACK_COLS, width - j * PACK_COLS)
            out.append((r0 + j, slice(0, 1), slice(j * PACK_COLS, j * PACK_COLS + w), w))
        else:
            out.append((r0 + j, slice(j, j + 1), slice(0, lanes), lanes))
    return out


def _pack_small(grads, loss_tile, name):
    names = list(PACK)

    def body(*refs):
        o_ref = refs[-1]
        o_ref[...] = jnp.zeros_like(o_ref)
        for k, nm in enumerate(names):
            for row, rs, ls, w in _pack_pieces(nm, refs[k].shape[1]):
                o_ref[row:row + 1, 0:w] = refs[k][rs, ls]
        o_ref[PACK_LOSS_ROW:PACK_LOSS_ROW + 1, 0:1] = refs[len(names)][0:1, 0:1]

    vmem = pl.BlockSpec(memory_space=pltpu.VMEM)
    args = [grads[nm] for nm in names] + [loss_tile]
    return pl.pallas_call(body, name=name, in_specs=[vmem] * len(args), out_specs=vmem,
                          out_shape=jax.ShapeDtypeStruct((PACK_ROWS, PACK_COLS), F32))(*args)


def _adam_small(gsum, ws, ms, vs, name):
    names = list(PACK)
    n = len(names)

    def body(*refs):
        g_ref = refs[0]
        w_refs, m_refs, v_refs = refs[1:1 + n], refs[1 + n:1 + 2 * n], refs[1 + 2 * n:1 + 3 * n]
        outs = refs[1 + 3 * n:]
        outs[0][...] = g_ref[PACK_LOSS_ROW:PACK_LOSS_ROW + 1, 0:1]
        for k, nm in enumerate(names):
            o_g, o_d, o_m, o_v = outs[1 + 4 * k:5 + 4 * k]
            for row, rs, ls, width in _pack_pieces(nm, w_refs[k].shape[1]):
                src = (rs, ls)
                g = g_ref[row:row + 1, 0:width]
                d, m2, v2 = _adam(w_refs[k][src], g, m_refs[k][src], v_refs[k][src])
                o_g[src] = g
                o_d[src] = d
                o_m[src] = m2
                o_v[src] = v2

    vmem = pl.BlockSpec(memory_space=pltpu.VMEM)
    shapes = [jax.ShapeDtypeStruct((1, 1), F32)]
    for nm in names:
        shapes += [jax.ShapeDtypeStruct(ws[nm].shape, F32)] * 4
    args = [gsum] + [ws[nm] for nm in names] + [ms[nm] for nm in names] + [vs[nm] for nm in names]
    return pl.pallas_call(
        body, name=name, in_specs=[vmem] * len(args), out_specs=tuple([vmem] * len(shapes)), out_shape=tuple(shapes),
    )(*args)


def _as2d(name, a):
    return a.reshape(a.shape[-2], a.shape[-1]) if a.ndim == 3 else a


def kernel(x, mem, positions, attn_norm, w_in, a_q_norm, a_k_norm, b_q_norm, b_k_norm, b_sinks, mem_norm, w_mem_kv, m_q_norm, m_k_norm, w_o_a, w_o_b, w_o_m, w_gate, b_gate, w_out, ffn_norm, w_up, conv_w, conv_b, w_down, loss_target, m_attn_norm, m_w_in, m_a_q_norm, m_a_k_norm, m_b_q_norm, m_b_k_norm, m_b_sinks, m_mem_norm, m_w_mem_kv, m_m_q_norm, m_m_k_norm, m_w_o_a, m_w_o_b, m_w_o_m, m_w_gate, m_b_gate, m_w_out, m_ffn_norm, m_w_up, m_conv_w, m_conv_b, m_w_down, v_attn_norm, v_w_in, v_a_q_norm, v_a_k_norm, v_b_q_norm, v_b_k_norm, v_b_sinks, v_mem_norm, v_w_mem_kv, v_m_q_norm, v_m_k_norm, v_w_o_a, v_w_o_b, v_w_o_m, v_w_gate, v_b_gate, v_w_out, v_ffn_norm, v_w_up, v_conv_w, v_conv_b, v_w_down):
    given = dict(attn_norm=attn_norm, w_in=w_in, a_q_norm=a_q_norm, a_k_norm=a_k_norm, b_q_norm=b_q_norm, b_k_norm=b_k_norm, b_sinks=b_sinks, mem_norm=mem_norm, w_mem_kv=w_mem_kv, m_q_norm=m_q_norm, m_k_norm=m_k_norm, w_o_a=w_o_a, w_o_b=w_o_b, w_o_m=w_o_m, w_gate=w_gate, b_gate=b_gate, w_out=w_out, ffn_norm=ffn_norm, w_up=w_up, conv_w=conv_w, conv_b=conv_b, w_down=w_down)
    mom1 = dict(attn_norm=m_attn_norm, w_in=m_w_in, a_q_norm=m_a_q_norm, a_k_norm=m_a_k_norm, b_q_norm=m_b_q_norm, b_k_norm=m_b_k_norm, b_sinks=m_b_sinks, mem_norm=m_mem_norm, w_mem_kv=m_w_mem_kv, m_q_norm=m_m_q_norm, m_k_norm=m_m_k_norm, w_o_a=m_w_o_a, w_o_b=m_w_o_b, w_o_m=m_w_o_m, w_gate=m_w_gate, b_gate=m_b_gate, w_out=m_w_out, ffn_norm=m_ffn_norm, w_up=m_w_up, conv_w=m_conv_w, conv_b=m_conv_b, w_down=m_w_down)
    mom2 = dict(attn_norm=v_attn_norm, w_in=v_w_in, a_q_norm=v_a_q_norm, a_k_norm=v_a_k_norm, b_q_norm=v_b_q_norm, b_k_norm=v_b_k_norm, b_sinks=v_b_sinks, mem_norm=v_mem_norm, w_mem_kv=v_w_mem_kv, m_q_norm=v_m_q_norm, m_k_norm=v_m_k_norm, w_o_a=v_w_o_a, w_o_b=v_w_o_b, w_o_m=v_w_o_m, w_gate=v_w_gate, b_gate=v_b_gate, w_out=v_w_out, ffn_norm=v_ffn_norm, w_up=v_w_up, conv_w=v_conv_w, conv_b=v_conv_b, w_down=v_w_down)

    big = list(BIG)
    stages = {'mix': list(MIX_WEIGHTS), 'ffn': list(FFN_WEIGHTS), 'in': ['w_in']}
    my_slot = _slot(_coords())

    def shard(n):
        return given[n][0] if n == 'conv_w' else given[n][0].astype(BF16)

    def whole(n, g):
        _, r, c = g.shape
        return g.reshape(N_DEV * r, c) if BIG[n] == 0 else g.transpose(1, 0, 2).reshape(r, N_DEV * c)

    def to_blocks(n, g):
        r, c = given[n].shape[1:]
        g = g.reshape(N_DEV, r, c) if BIG[n] == 0 else g.reshape(r, N_DEV, c).transpose(1, 0, 2)
        return g if n == 'conv_w' else g.astype(BF16)

    class Hooks:
        next_stage = {'in': 'mix', 'mix': 'ffn'}

        def __init__(self):
            self.coming, self.sent = {}, {}
            self.shards = {n: shard(n) for n in big}
            self.start_gather('in', None)

        def start_gather(self, stage, after):
            src = [self.shards[n] for n in stages[stage]]
            self.coming[stage] = _exchange_start(src, f"gather_{stage}_start", gather=True, masks=CHIP_PEERS,
                                                 after=after)

        def weights(self, stage, after):
            names = stages[stage]
            after = list(after)
            if stage == 'in':
                after += [self.shards[n] for n in stages['mix'] + stages['ffn']]
            landed = _exchange_wait(self.coming[stage], after, f"gather_{stage}_wait", gather=True, masks=CHIP_PEERS)
            landed, token = _sibling_forward(landed, f"gather_{stage}_forward")
            if stage in self.next_stage:
                self.start_gather(self.next_stage[stage], token)
            return {n: whole(n, lax.dynamic_update_slice_in_dim(land, self.shards[n][None], my_slot, axis=0))
                    for n, land in zip(names, landed)}

        def grads(self, stage, g):
            blocks = [to_blocks(n, g[n]) for n in stages[stage]]
            own = [lax.dynamic_slice_in_dim(b, my_slot, 1, axis=0) for b in blocks]
            self.sent[stage] = (_exchange_start(blocks, f"exchange_{stage}_start"), own)
            return self.sent[stage][0][-1]

        def parts(self, stage, after):
            started, own = self.sent[stage]
            landed = _exchange_wait(started, [after], f"exchange_{stage}_wait")
            return {n: lax.dynamic_update_slice_in_dim(land, o, my_slot, axis=0)
                    for n, land, o in zip(stages[stage], landed, own)}

    hooks = Hooks()
    w = {}
    for n in SMALL:
        w[n] = given[n]
    w['a_q_norm'], w['a_k_norm'] = given['a_q_norm'][0], given['a_k_norm'][0]
    w['b_q_norm'], w['b_k_norm'], w['b_sinks'] = given['b_q_norm'][0], given['b_k_norm'][0], given['b_sinks'][0]

    loss_tile, grad_x, grads = _device_step(x[0], mem[0], positions[0], loss_target[0], w, hooks)
    out = {}
    after = grad_x
    for stage in ('ffn', 'mix', 'in'):
        for n, p in hooks.parts(stage, after).items():
            res = _adam_reduce(p, given[n][0], mom1[n][0], mom2[n][0], f"adam_{n}")
            out[n] = tuple(t[None] for t in res)
            after = res[0]

    small = {n: grads[n] for n in PACK}
    small['b_q_norm'], small['b_k_norm'] = grads['b_q_norm'].reshape(1, -1), grads['b_k_norm'].reshape(1, -1)
    small['b_sinks'] = grads['b_sinks'].reshape(1, -1)
    gsum = _all_sum(_pack_small(small, loss_tile, "pack_small"), "sum_small")
    ws = {n: _as2d(n, given[n]) for n in PACK}
    ms = {n: _as2d(n, mom1[n]) for n in PACK}
    vs = {n: _as2d(n, mom2[n]) for n in PACK}
    res = _adam_small(gsum, ws, ms, vs, "adam_small")
    loss = res[0].reshape(())
    for k, n in enumerate(PACK):
        out[n] = tuple(t.reshape(given[n].shape) for t in res[1 + 4 * k:5 + 4 * k])

    outs = [loss, grad_x[None]]
    for field in range(4):
        outs += [out[n][field] for n in WEIGHTS]
    return tuple(outs)
```

```python
import functools
import math

import jax
import jax.numpy as jnp
from jax import lax
from jax.experimental import pallas as pl
from jax.experimental.pallas import tpu as pltpu

F32 = jnp.float32
BF16 = jnp.bfloat16

N_DEV = 8
HEAD_DIM = 64
A_GROUPS = ((128, 1), (512, 4), (2048, 16))
B_WINDOW = 128
M_HEADS = 4
M_HEAD_DIM = 128
MEM_LEN = 256
D_FF = 2816
ROPE_THETA = 500000.0
ROPE_DIMS = 16
BLOCK = 128
EPS = 1e-6
LANES = 128
BAND_Q_BLOCKS = 8
BAND_UNITS = 2

ADAM_LR = 0.001
ADAM_B1 = 0.9
ADAM_B2 = 0.999
ADAM_EPS = 1e-08
ADAM_WD = 0.01
ADAM_STEP = 10

VMEM_LIMIT_BYTES = 56 * 1024 * 1024
GRAD_DTYPE = BF16
MESH = pl.DeviceIdType.MESH

WEIGHTS = ['attn_norm', 'w_in', 'a_q_norm', 'a_k_norm', 'b_q_norm', 'b_k_norm', 'b_sinks', 'mem_norm',
           'w_mem_kv', 'm_q_norm', 'm_k_norm', 'w_o_a', 'w_o_b', 'w_o_m', 'w_gate', 'b_gate', 'w_out',
           'ffn_norm', 'w_up', 'conv_w', 'conv_b', 'w_down']
BIG = {'w_in': 1, 'w_mem_kv': 0, 'w_o_a': 1, 'w_o_b': 1, 'w_o_m': 1, 'w_gate': 1, 'w_out': 0, 'w_up': 1,
       'conv_w': 1, 'w_down': 0}
SMALL = [n for n in WEIGHTS if n not in BIG]


def _cparams(n_grid):
    return pltpu.CompilerParams(dimension_semantics=("arbitrary",) * n_grid, vmem_limit_bytes=VMEM_LIMIT_BYTES)


def _seg_matrix(width):
    shift = width.bit_length() - 1
    r = lax.shift_right_logical(lax.broadcasted_iota(jnp.int32, (LANES, LANES), 0), shift)
    c = lax.shift_right_logical(lax.broadcasted_iota(jnp.int32, (LANES, LANES), 1), shift)
    return jnp.where(r == c, 1.0, 0.0).astype(BF16)


def _seg_sum(x, seg):
    hi = x.astype(BF16)
    r1 = x - hi.astype(F32)
    mid = r1.astype(BF16)
    lo = (r1 - mid.astype(F32)).astype(BF16)
    dot = functools.partial(jnp.dot, preferred_element_type=F32)
    return dot(hi, seg) + dot(mid, seg) + dot(lo, seg)


def _rope(y, c, s1, s2):
    return y * c + pltpu.roll(y, LANES - ROPE_DIMS // 2, 1) * s1 + pltpu.roll(y, ROPE_DIMS // 2, 1) * s2


def _unrope(dy, c, s1, s2):
    return dy * c + pltpu.roll(dy * s1, ROPE_DIMS // 2, 1) + pltpu.roll(dy * s2, LANES - ROPE_DIMS // 2, 1)


def _sigmoid(x):
    return 1.0 / (1.0 + jnp.exp(-x))


def _rms_fwd(x, gain, name):
    s_len, d = x.shape
    tm = 512

    def body(x_ref, g_ref, h_ref, ht_ref, r_ref):
        xv = x_ref[...]
        r = lax.rsqrt(jnp.mean(xv * xv, axis=-1, keepdims=True) + EPS)
        h = ((xv * r) * g_ref[...]).astype(BF16)
        h_ref[...] = h
        ht_ref[...] = h.T
        r_ref[...] = r

    return pl.pallas_call(
        body, name=name, grid=(s_len // tm,),
        in_specs=[pl.BlockSpec((tm, d), lambda i: (i, 0)), pl.BlockSpec((1, d), lambda i: (0, 0))],
        out_specs=(pl.BlockSpec((tm, d), lambda i: (i, 0)), pl.BlockSpec((d, tm), lambda i: (0, i)),
                   pl.BlockSpec((tm, 1), lambda i: (i, 0))),
        out_shape=(jax.ShapeDtypeStruct((s_len, d), BF16), jax.ShapeDtypeStruct((d, s_len), BF16),
                   jax.ShapeDtypeStruct((s_len, 1), F32)),
        compiler_params=_cparams(1),
    )(x, gain)


def _resident(shape, index_map):
    return pl.BlockSpec(shape, index_map, pipeline_mode=pl.Buffered(1))


def _mm_rows(pairs, name, nt=False, tm=512, bias=None, sigmoid=False, res=None, out_dtypes=(F32,), loss_target=None,
             rms_bwd=None):
    m = pairs[0][0].shape[0]
    n = pairs[0][1].shape[0] if nt else pairs[0][1].shape[1]
    n_pairs = len(pairs)
    has_bias, has_res, has_loss = bias is not None, res is not None, loss_target is not None
    has_rms = rms_bwd is not None
    dims = (((1,), (1,)), ((), ())) if nt else (((1,), (0,)), ((), ()))

    def body(*refs):
        acc = None
        for p in range(n_pairs):
            t = lax.dot_general(refs[2 * p][...].astype(BF16), refs[2 * p + 1][...], dims, preferred_element_type=F32)
            acc = t if acc is None else acc + t
        pos = 2 * n_pairs
        if has_bias:
            acc = acc + refs[pos][...]
            pos += 1
        if sigmoid:
            acc = _sigmoid(acc)
        if has_res:
            acc = refs[pos][...] + acc
            pos += 1
        if has_loss:
            dy_ref, dyb_ref, da_ref, l_ref = refs[pos + 1:]

            @pl.when(pl.program_id(0) == 0)
            def _():
                l_ref[...] = jnp.zeros_like(l_ref)
            err = acc - refs[pos][...]
            dy = err * (1.0 / n)
            dy_ref[...] = dy
            dyb_ref[...] = dy.astype(BF16)
            da_ref[...] = lax.dot_general(dy.astype(BF16), refs[1][...], (((1,), (1,)), ((), ())),
                                          preferred_element_type=F32).astype(BF16)
            part = 0.5 * jnp.sum(jnp.mean(err * err, axis=-1, keepdims=True), axis=0, keepdims=True)
            l_ref[...] += jnp.broadcast_to(part, l_ref.shape)
            return
        if has_rms:
            x_ref, r_ref, g_ref, add_ref = refs[pos:pos + 4]
            dg_ref = refs[-1]

            @pl.when(pl.program_id(0) == 0)
            def _():
                dg_ref[...] = jnp.zeros_like(dg_ref)
            rv = r_ref[...]
            xhat = x_ref[...] * rv
            dg_ref[...] += jnp.sum(acc * xhat, axis=0, keepdims=True)
            dxhat = acc * g_ref[...]
            acc = add_ref[...] + rv * (dxhat - xhat * jnp.mean(dxhat * xhat, axis=-1, keepdims=True))
            for o_ref in refs[pos + 4:-1]:
                o_ref[...] = acc.astype(o_ref.dtype)
            return
        for o_ref in refs[pos:]:
            o_ref[...] = acc.astype(o_ref.dtype)

    in_specs, args = [], []
    for a, w, blk in pairs:
        k = a.shape[1]
        in_specs.append(pl.BlockSpec((tm, k), lambda i: (i, 0)))
        if nt:
            in_specs.append(_resident((n, k), lambda i, blk=blk: (0, blk)))
        else:
            in_specs.append(_resident((k, n), lambda i, blk=blk: (blk, 0)))
        args += [a, w]
    if has_bias:
        in_specs.append(_resident((1, n), lambda i: (0, 0)))
        args.append(bias)
    if has_res:
        in_specs.append(pl.BlockSpec((tm, n), lambda i: (i, 0)))
        args.append(res)
    out = pl.BlockSpec((tm, n), lambda i: (i, 0))
    if has_loss:
        k0 = pairs[0][0].shape[1]
        return pl.pallas_call(
            body, name=name, grid=(m // tm,), in_specs=in_specs + [out],
            out_specs=(out, out, pl.BlockSpec((tm, k0), lambda i: (i, 0)), pl.BlockSpec((8, LANES), lambda i: (0, 0))),
            out_shape=(jax.ShapeDtypeStruct((m, n), F32), jax.ShapeDtypeStruct((m, n), BF16),
                       jax.ShapeDtypeStruct((m, k0), BF16), jax.ShapeDtypeStruct((8, LANES), F32)),
            compiler_params=_cparams(1),
        )(*args, loss_target)
    if has_rms:
        x, r, gain, add = rms_bwd
        vec = _resident((1, n), lambda i: (0, 0))
        return pl.pallas_call(
            body, name=name, grid=(m // tm,),
            in_specs=in_specs + [out, pl.BlockSpec((tm, 1), lambda i: (i, 0)), vec, out],
            out_specs=tuple([out] * len(out_dtypes) + [pl.BlockSpec((1, n), lambda i: (0, 0))]),
            out_shape=tuple([jax.ShapeDtypeStruct((m, n), dt) for dt in out_dtypes] + [jax.ShapeDtypeStruct((1, n), F32)]),
            compiler_params=_cparams(1),
        )(*args, x, r, gain, add)
    outs = pl.pallas_call(
        body, name=name, grid=(m // tm,), in_specs=in_specs, out_specs=tuple([out] * len(out_dtypes)),
        out_shape=tuple(jax.ShapeDtypeStruct((m, n), dt) for dt in out_dtypes), compiler_params=_cparams(1),
    )(*args)
    return outs[0] if len(out_dtypes) == 1 else outs


def _mm_rows_each(pairs, name, tm=512):
    m = pairs[0][0].shape[0]
    n_pairs = len(pairs)

    def body(*refs):
        for p in range(n_pairs):
            refs[2 * n_pairs + p][...] = lax.dot_general(refs[2 * p][...].astype(BF16), refs[2 * p + 1][...],
                                                         (((1,), (1,)), ((), ())), preferred_element_type=F32)

    in_specs, args = [], []
    for a, w in pairs:
        in_specs += [pl.BlockSpec((tm, a.shape[1]), lambda i: (i, 0)), _resident(w.shape, lambda i: (0, 0))]
        args += [a, w]
    return pl.pallas_call(
        body, name=name, grid=(m // tm,), in_specs=in_specs,
        out_specs=tuple(pl.BlockSpec((tm, w.shape[0]), lambda i: (i, 0)) for _, w in pairs),
        out_shape=tuple(jax.ShapeDtypeStruct((m, w.shape[0]), F32) for _, w in pairs), compiler_params=_cparams(1),
    )(*args)


def _mm_rows_cat(a, ws, name, tm=256):
    m, k = a.shape
    widths = [w.shape[1] for w in ws]
    n = sum(widths)

    def body(*refs):
        a_ref, o_ref = refs[0], refs[-1]
        av = a_ref[...]
        off = 0
        for p, width in enumerate(widths):
            o_ref[:, off:off + width] = jnp.dot(av, refs[1 + p][...], preferred_element_type=F32).astype(GRAD_DTYPE)
            off += width

    return pl.pallas_call(
        body, name=name, grid=(m // tm,),
        in_specs=[pl.BlockSpec((tm, k), lambda i: (i, 0))] + [_resident((k, wd), lambda i: (0, 0)) for wd in widths],
        out_specs=pl.BlockSpec((tm, n), lambda i: (i, 0)),
        out_shape=jax.ShapeDtypeStruct((m, n), GRAD_DTYPE), compiler_params=_cparams(1),
    )(a, *ws)


def _mm_cols(a, bs, name, tn=256):
    m, k = a.shape
    counts = [b.shape[1] // tn for b in bs]
    starts = [sum(counts[:p]) for p in range(len(bs))]

    def body(*refs):
        a_ref, o_ref = refs[0], refs[-1]
        j = pl.program_id(0)
        for p, b_ref in enumerate(refs[1:-1]):
            @pl.when((j >= starts[p]) & (j < starts[p] + counts[p]))
            def _():
                o_ref[...] = jnp.dot(a_ref[...], b_ref[...].astype(BF16), preferred_element_type=F32).astype(GRAD_DTYPE)

    b_specs = [pl.BlockSpec((k, tn), lambda j, s=s, c=c: (0, jnp.clip(j - s, 0, c - 1))) for s, c in zip(starts, counts)]
    return pl.pallas_call(
        body, name=name, grid=(sum(counts),),
        in_specs=[_resident((m, k), lambda j: (0, 0))] + b_specs,
        out_specs=pl.BlockSpec((m, tn), lambda j: (0, j)),
        out_shape=jax.ShapeDtypeStruct((m, sum(counts) * tn), GRAD_DTYPE), compiler_params=_cparams(1),
    )(a, *bs)


def _mm_tn_each(pairs, name, tile=256):
    n = pairs[0][1].shape[1]
    n_pairs = len(pairs)
    dims = (((0,), (0,)), ((), ()))

    def body(*refs):
        for p in range(n_pairs):
            refs[2 * n_pairs + p][...] = lax.dot_general(refs[2 * p][...].astype(BF16), refs[2 * p + 1][...].astype(BF16),
                                                         dims, preferred_element_type=F32).astype(GRAD_DTYPE)

    in_specs, args = [], []
    for a, b in pairs:
        in_specs += [_resident(a.shape, lambda j: (0, 0)), pl.BlockSpec((b.shape[0], tile), lambda j: (0, j))]
        args += [a, b]
    return pl.pallas_call(
        body, name=name, grid=(n // tile,), in_specs=in_specs,
        out_specs=tuple(pl.BlockSpec((a.shape[1], tile), lambda j: (0, j)) for a, _ in pairs),
        out_shape=tuple(jax.ShapeDtypeStruct((a.shape[1], n), GRAD_DTYPE) for a, _ in pairs),
        compiler_params=_cparams(1),
    )(*args)


def _mm_tn(a, b, name, tile=256):
    k, m = a.shape
    n = b.shape[1]
    dims = (((0,), (0,)), ((), ()))

    def body(a_ref, b_ref, o_ref):
        o_ref[...] = lax.dot_general(a_ref[...].astype(BF16), b_ref[...].astype(BF16), dims,
                                     preferred_element_type=F32).astype(GRAD_DTYPE)

    if n <= m:
        t = min(tile, m)
        grid, a_spec, b_spec = (m // t,), pl.BlockSpec((k, t), lambda i: (0, i)), _resident((k, n), lambda i: (0, 0))
        o_spec = pl.BlockSpec((t, n), lambda i: (i, 0))
    else:
        t = min(tile, n)
        grid, a_spec, b_spec = (n // t,), _resident((k, m), lambda i: (0, 0)), pl.BlockSpec((k, t), lambda i: (0, i))
        o_spec = pl.BlockSpec((m, t), lambda i: (0, i))
    return pl.pallas_call(
        body, name=name, grid=grid, in_specs=[a_spec, b_spec], out_specs=o_spec,
        out_shape=jax.ShapeDtypeStruct((m, n), GRAD_DTYPE), compiler_params=_cparams(1),
    )(a, b)


def _norm_rope(t, gain, c, s1, s2, seg):
    rs = lax.rsqrt(_seg_sum(t * t, seg) * (1.0 / HEAD_DIM) + EPS)
    return _rope((t * rs) * gain, c, s1, s2)


def _dup_half(y, half):
    lane = lax.broadcasted_iota(jnp.int32, y.shape, 1)
    rolled = pltpu.roll(y, HEAD_DIM, 1)
    keep = (lane < HEAD_DIM) if half == 0 else (lane >= HEAD_DIM)
    return jnp.where(keep, y, rolled)


def _qk_prep(proj, cb0, d, gqa, gq, gk, tabs, name):
    s_len = proj.shape[0]
    tm = 512
    rows = tm // d
    n_units = 4 if gqa else 2 * d
    n_q = 4 if gqa else 2
    n_in = 6

    def body(*refs):
        in_refs = refs[:n_in]
        gq_ref, gk_ref, c_ref, s1_ref, s2_ref, o_ref = refs[n_in:]
        seg = _seg_matrix(HEAD_DIM)

        def rows_of(ref, r):
            return ref[...] if d == 1 else ref[pl.ds(r, rows, stride=d), :]

        def put(unit_col, y):
            o_ref[:, unit_col * LANES:(unit_col + 1) * LANES] = y.astype(BF16)

        for r in range(d):
            c, s1, s2 = rows_of(c_ref, r), rows_of(s1_ref, r), rows_of(s2_ref, r)
            for b in range(n_in):
                t = rows_of(in_refs[b], r)
                if b < n_q:
                    put((b * d + r) if not gqa else b, _norm_rope(t, gq_ref[...], c, s1, s2, seg))
                elif not gqa:
                    sec, pair = (1, b - 2) if b < 4 else (2, b - 4)
                    y = _norm_rope(t, gk_ref[...], c, s1, s2, seg) if sec == 1 else t
                    put(sec * n_units + pair * d + r, y)
                else:
                    sec = 1 if b == 4 else 2
                    y = _norm_rope(t, gk_ref[...], c, s1, s2, seg) if sec == 1 else t
                    for u in range(n_units):
                        put(sec * n_units + u, _dup_half(y, u // 2))

    in_specs = [pl.BlockSpec((tm, LANES), lambda i, b=b: (i, cb0 + b)) for b in range(n_in)]
    vec = pl.BlockSpec((1, LANES), lambda i: (0, 0))
    tab = pl.BlockSpec((tm, LANES), lambda i: (i, 0))
    width = 3 * n_units * LANES
    return pl.pallas_call(
        body, name=name, grid=(s_len // tm,), in_specs=in_specs + [vec, vec, tab, tab, tab],
        out_specs=pl.BlockSpec((rows, width), lambda i: (i, 0)),
        out_shape=jax.ShapeDtypeStruct((s_len // d, width), BF16), compiler_params=_cparams(1),
    )(*([proj] * n_in), gq, gk, *tabs)


def _qk_prep_bwd(dqkv, proj, cb0, d, gqa, gq, gk, tabs, name):
    s_len = proj.shape[0]
    tm = 1024
    rows = tm // d
    n_units = 4 if gqa else 2 * d
    n_q = 4 if gqa else 2
    n_in = 6

    def body(*refs):
        d_refs = refs[0:3]
        in_refs = refs[3:3 + n_in]
        gq_ref, gk_ref, c_ref, s1_ref, s2_ref, o_ref, dgq_ref, dgk_ref, stage = refs[3 + n_in:]
        seg = _seg_matrix(HEAD_DIM)

        @pl.when(pl.program_id(0) == 0)
        def _():
            dgq_ref[...] = jnp.zeros_like(dgq_ref)
            dgk_ref[...] = jnp.zeros_like(dgk_ref)

        def rows_of(ref, r):
            return ref[...] if d == 1 else ref[pl.ds(r, rows, stride=d), :]

        def unit(col):
            sec, u = divmod(col, n_units)
            return d_refs[sec][:, u * LANES:(u + 1) * LANES]

        def norm_bwd(dyr, t, gain, c, s1, s2, dg_ref):
            rs = lax.rsqrt(_seg_sum(t * t, seg) * (1.0 / HEAD_DIM) + EPS)
            that = t * rs
            dy = _unrope(dyr, c, s1, s2)
            dg_ref[...] += jnp.sum(dy * that, axis=0, keepdims=True)
            dthat = dy * gain
            return rs * (dthat - that * (_seg_sum(dthat * that, seg) * (1.0 / HEAD_DIM)))

        def fold(sec):
            tot = []
            for u in range(n_units):
                v = unit(sec * n_units + u)
                tot.append(v + pltpu.roll(v, HEAD_DIM, 1))
            lane = lax.broadcasted_iota(jnp.int32, tot[0].shape, 1)
            return jnp.where(lane < HEAD_DIM, tot[0] + tot[1], tot[2] + tot[3])

        for b in range(n_in):
            for r in range(d):
                c, s1, s2 = rows_of(c_ref, r), rows_of(s1_ref, r), rows_of(s2_ref, r)
                t = rows_of(in_refs[b], r)
                if b < n_q:
                    g = unit((b * d + r) if not gqa else b)
                    out = norm_bwd(g, t, gq_ref[...], c, s1, s2, dgq_ref)
                elif not gqa:
                    sec, pair = (1, b - 2) if b < 4 else (2, b - 4)
                    g = unit(sec * n_units + pair * d + r)
                    out = norm_bwd(g, t, gk_ref[...], c, s1, s2, dgk_ref) if sec == 1 else g
                else:
                    sec = 1 if b == 4 else 2
                    g = fold(sec)
                    out = norm_bwd(g, t, gk_ref[...], c, s1, s2, dgk_ref) if sec == 1 else g
                if d == 1:
                    o_ref[:, b * LANES:(b + 1) * LANES] = out.astype(BF16)
                else:
                    stage[pl.ds(r, rows, stride=d), :] = out
            if d != 1:
                o_ref[:, b * LANES:(b + 1) * LANES] = stage[...].astype(BF16)

    in_specs = [pl.BlockSpec((rows, n_units * LANES), lambda i: (i, 0))] * 3
    in_specs += [pl.BlockSpec((tm, LANES), lambda i, b=b: (i, cb0 + b)) for b in range(n_in)]
    vec = pl.BlockSpec((1, LANES), lambda i: (0, 0))
    tab = pl.BlockSpec((tm, LANES), lambda i: (i, 0))
    return pl.pallas_call(
        body, name=name, grid=(s_len // tm,), in_specs=in_specs + [vec, vec, tab, tab, tab],
        out_specs=(pl.BlockSpec((tm, n_in * LANES), lambda i: (i, 0)), vec, vec),
        out_shape=(jax.ShapeDtypeStruct((s_len, n_in * LANES), BF16), jax.ShapeDtypeStruct((1, LANES), F32),
                   jax.ShapeDtypeStruct((1, LANES), F32)),
        scratch_shapes=[pltpu.VMEM((tm, LANES), F32)], compiler_params=_cparams(1),
    )(*dqkv, *([proj] * n_in), gq, gk, *tabs)


def _head_masks(shape):
    lane = lax.broadcasted_iota(jnp.int32, shape, 1)
    return lane < HEAD_DIM, lane >= HEAD_DIM


def _band_fwd(qkv, n_units, max_dist, sinks, name):
    n_rows = qkv.shape[0]
    nb = n_rows // BLOCK
    scale = HEAD_DIM ** -0.5
    has_sink = sinks is not None
    assert not has_sink or max_dist < BLOCK

    qn, un = min(nb, BAND_Q_BLOCKS), BAND_UNITS
    ug = n_units // un

    def body(*refs):
        q_ref, kp_ref, km_ref, vp_ref, vm_ref = refs[:5]
        o_ref, lse_ref = refs[-2:]
        i = pl.program_id(1)
        qi = lax.broadcasted_iota(jnp.int32, (BLOCK, 2 * BLOCK), 0)
        kj = lax.broadcasted_iota(jnp.int32, (BLOCK, 2 * BLOCK), 1)
        dist = qi + BLOCK - kj
        band = (dist >= 0) & (dist <= max_dist)
        band_first = band & ((i > 0) | (kj >= BLOCK))
        m0, m1 = _head_masks((BLOCK, LANES))
        zero = jnp.zeros((BLOCK, LANES), BF16)
        for ub in range(un):
            cs = slice(ub * LANES, (ub + 1) * LANES)
            for qb in range(qn):
                rs = slice(qb * BLOCK, (qb + 1) * BLOCK)
                q = q_ref[rs, cs]
                if qb == 0:
                    kk = jnp.concatenate([kp_ref[:, cs], km_ref[0:BLOCK, cs]], axis=0)
                    vv = jnp.concatenate([vp_ref[:, cs], vm_ref[0:BLOCK, cs]], axis=0)
                    valid = band_first
                else:
                    kk = km_ref[(qb - 1) * BLOCK:(qb + 1) * BLOCK, cs]
                    vv = vm_ref[(qb - 1) * BLOCK:(qb + 1) * BLOCK, cs]
                    valid = band
                outs, lses = [], []
                for e, hm in enumerate((m0, m1)):
                    qe = jnp.where(hm, q, zero)
                    s = lax.dot_general(qe, kk, (((1,), (1,)), ((), ())), preferred_element_type=F32) * scale
                    s = jnp.where(valid, s, -jnp.inf)
                    if has_sink:
                        s = jnp.where(kj == 0, refs[5][ub][:, e * HEAD_DIM:e * HEAD_DIM + 1], s)
                    mx = jnp.max(s, axis=-1, keepdims=True)
                    p = jnp.exp(s - mx)
                    den = jnp.sum(p, axis=-1, keepdims=True)
                    pn = p * (1.0 / den)
                    if has_sink:
                        pn = jnp.where(kj == 0, 0.0, pn)
                    pn = pn.astype(BF16)
                    outs.append(jnp.dot(pn, vv, preferred_element_type=F32))
                    lses.append(mx + jnp.log(den))
                o_ref[rs, cs] = jnp.where(m0, outs[0], outs[1])
                lse_ref[rs, cs] = jnp.where(m0, jnp.broadcast_to(lses[0], (BLOCK, LANES)),
                                            jnp.broadcast_to(lses[1], (BLOCK, LANES)))

    def main(sec):
        return pl.BlockSpec((qn * BLOCK, un * LANES), lambda u, i: (i, sec * ug + u))

    def prev(sec):
        return pl.BlockSpec((BLOCK, un * LANES), lambda u, i: (jnp.maximum(i * qn - 1, 0), sec * ug + u))

    in_specs = [main(0), prev(1), main(1), prev(2), main(2)]
    args = [qkv] * 5
    if has_sink:
        in_specs.append(pl.BlockSpec((un, 1, LANES), lambda u, i: (u, 0, 0)))
        args.append(sinks)
    return pl.pallas_call(
        body, name=name, grid=(ug, nb // qn), in_specs=in_specs, out_specs=(main(0), main(0)),
        out_shape=(jax.ShapeDtypeStruct((n_rows, n_units * LANES), F32),) * 2, compiler_params=_cparams(2),
    )(*args)


def _band_bwd(qkv, do, lse, delta, n_units, max_dist, name):
    n_rows = qkv.shape[0]
    nb = n_rows // BLOCK
    scale = HEAD_DIM ** -0.5

    qn, un = min(nb, BAND_Q_BLOCKS), BAND_UNITS
    ug = n_units // un
    steps = nb // qn
    nt_dims = (((1,), (1,)), ((), ()))
    tn_dims = (((0,), (0,)), ((), ()))

    def body(qm_ref, qx_ref, kp_ref, km_ref, vp_ref, vm_ref, dom_ref, dox_ref, lm_ref, lx_ref, dm_ref, dx_ref,
             dq_ref, dk_ref, dv_ref):
        i = pl.program_id(1)
        m0, m1 = _head_masks((BLOCK, LANES))
        zero = jnp.zeros((BLOCK, LANES), BF16)
        qi = lax.broadcasted_iota(jnp.int32, (BLOCK, 2 * BLOCK), 0)
        kj = lax.broadcasted_iota(jnp.int32, (BLOCK, 2 * BLOCK), 1)
        dist = qi + BLOCK - kj
        band = (dist >= 0) & (dist <= max_dist)
        band_first = band & ((i > 0) | (kj >= BLOCK))
        qr = lax.broadcasted_iota(jnp.int32, (BLOCK, BLOCK), 0)
        kc = lax.broadcasted_iota(jnp.int32, (BLOCK, BLOCK), 1)
        dist_x = qr + BLOCK - kc
        band_next = (dist_x >= 0) & (dist_x <= max_dist) & (i < steps - 1)

        def pair(q, dob, lse_b, del_b, kk, vv, valid):
            dqs, dk, dv = [], None, None
            for e, hm in enumerate((m0, m1)):
                col = slice(e * HEAD_DIM, e * HEAD_DIM + 1)
                qe = jnp.where(hm, q, zero)
                doe = jnp.where(hm, dob, zero)
                s = lax.dot_general(qe, kk, nt_dims, preferred_element_type=F32) * scale
                p = jnp.where(valid, jnp.exp(s - lse_b[:, col]), 0.0)
                dp = lax.dot_general(doe, vv, nt_dims, preferred_element_type=F32)
                ds = (p * (dp - del_b[:, col]) * scale).astype(BF16)
                dqs.append(jnp.dot(ds, kk, preferred_element_type=F32))
                dk_e = lax.dot_general(ds, qe, tn_dims, preferred_element_type=F32)
                dv_e = lax.dot_general(p.astype(BF16), doe, tn_dims, preferred_element_type=F32)
                dk = dk_e if dk is None else dk + dk_e
                dv = dv_e if dv is None else dv + dv_e
            return jnp.where(m0, dqs[0], dqs[1]), dk, dv

        for ub in range(un):
            cs = slice(ub * LANES, (ub + 1) * LANES)
            dk_acc, dv_acc = [None] * qn, [None] * qn

            def add(acc, kb, part):
                acc[kb] = part if acc[kb] is None else acc[kb] + part

            for qb in range(qn):
                rs = slice(qb * BLOCK, (qb + 1) * BLOCK)
                if qb == 0:
                    kk = jnp.concatenate([kp_ref[:, cs], km_ref[0:BLOCK, cs]], axis=0)
                    vv = jnp.concatenate([vp_ref[:, cs], vm_ref[0:BLOCK, cs]], axis=0)
                    valid = band_first
                else:
                    kk = km_ref[(qb - 1) * BLOCK:(qb + 1) * BLOCK, cs]
                    vv = vm_ref[(qb - 1) * BLOCK:(qb + 1) * BLOCK, cs]
                    valid = band
                dq, dk, dv = pair(qm_ref[rs, cs], dom_ref[rs, cs], lm_ref[rs, cs], dm_ref[rs, cs], kk, vv, valid)
                dq_ref[rs, cs] = dq
                if qb > 0:
                    add(dk_acc, qb - 1, dk[0:BLOCK])
                    add(dv_acc, qb - 1, dv[0:BLOCK])
                add(dk_acc, qb, dk[BLOCK:2 * BLOCK])
                add(dv_acc, qb, dv[BLOCK:2 * BLOCK])
            last = slice((qn - 1) * BLOCK, qn * BLOCK)
            _, dk, dv = pair(qx_ref[:, cs], dox_ref[:, cs], lx_ref[:, cs], dx_ref[:, cs], km_ref[last, cs], vm_ref[last, cs],
                             band_next)
            add(dk_acc, qn - 1, dk)
            add(dv_acc, qn - 1, dv)
            for kb in range(qn):
                dk_ref[kb * BLOCK:(kb + 1) * BLOCK, cs] = dk_acc[kb]
                dv_ref[kb * BLOCK:(kb + 1) * BLOCK, cs] = dv_acc[kb]

    def main(sec):
        return pl.BlockSpec((qn * BLOCK, un * LANES), lambda u, i: (i, sec * ug + u))

    def prev(sec):
        return pl.BlockSpec((BLOCK, un * LANES), lambda u, i: (jnp.maximum(i * qn - 1, 0), sec * ug + u))

    def nxt(sec):
        return pl.BlockSpec((BLOCK, un * LANES), lambda u, i: (jnp.minimum((i + 1) * qn, nb - 1), sec * ug + u))

    in_specs = [main(0), nxt(0), prev(1), main(1), prev(2), main(2),
                main(0), nxt(0), main(0), nxt(0), main(0), nxt(0)]
    args = [qkv] * 6 + [do, do, lse, lse, delta, delta]
    shp = jax.ShapeDtypeStruct((n_rows, n_units * LANES), F32)
    return pl.pallas_call(
        body, name=name, grid=(ug, steps), in_specs=in_specs, out_specs=(main(0), main(0), main(0)),
        out_shape=(shp, shp, shp), compiler_params=_cparams(2),
    )(*args)


def _merge_groups(os_, lses, dils, name):
    s_len = os_[0].shape[0] * dils[0]
    tm = 512

    def body(*refs):
        o_refs, l_refs = refs[0:3], refs[3:6]
        o_ref, lse_ref = refs[6:8]
        so, sl = refs[8:11], refs[11:14]
        for pair in range(2):
            for g, d in enumerate(dils):
                rows = tm // d
                for r in range(d):
                    col = slice((pair * d + r) * LANES, (pair * d + r + 1) * LANES)
                    if d == 1:
                        so[g][...] = o_refs[g][:, col]
                        sl[g][...] = l_refs[g][:, col]
                    else:
                        so[g][pl.ds(r, rows, stride=d), :] = o_refs[g][:, col]
                        sl[g][pl.ds(r, rows, stride=d), :] = l_refs[g][:, col]
            l0, l1, l2 = sl[0][...], sl[1][...], sl[2][...]
            mx = jnp.maximum(jnp.maximum(l0, l1), l2)
            e0, e1, e2 = jnp.exp(l0 - mx), jnp.exp(l1 - mx), jnp.exp(l2 - mx)
            den = e0 + e1 + e2
            inv = 1.0 / den
            o_ref[:, pair * LANES:(pair + 1) * LANES] = (so[0][...] * (e0 * inv) + so[1][...] * (e1 * inv)
                                                         + so[2][...] * (e2 * inv))
            lse_ref[:, pair * LANES:(pair + 1) * LANES] = mx + jnp.log(den)

    in_specs = [pl.BlockSpec((tm // d, 2 * d * LANES), lambda i: (i, 0)) for d in dils] * 2
    out = pl.BlockSpec((tm, 2 * LANES), lambda i: (i, 0))
    shp = jax.ShapeDtypeStruct((s_len, 2 * LANES), F32)
    return pl.pallas_call(
        body, name=name, grid=(s_len // tm,), in_specs=in_specs, out_specs=(out, out), out_shape=(shp, shp),
        scratch_shapes=[pltpu.VMEM((tm, LANES), F32)] * 6, compiler_params=_cparams(1),
    )(*os_, *lses)


def _bwd_prep(do, o, lse, dils, sinks, name):
    s_len, width = do.shape
    n_pairs = width // LANES
    tm = 512
    has_sink = sinks is not None
    n_g = len(dils)

    def body(*refs):
        do_ref, o_ref, lse_ref = refs[:3]
        pos = 3
        if has_sink:
            sink_ref = refs[pos]
            pos += 1
        outs = refs[pos:pos + 3 * n_g]
        pos += 3 * n_g
        if has_sink:
            dsink_ref = refs[pos]
            pos += 1
        s_do, s_l, s_d = refs[pos:pos + 3]
        seg = _seg_matrix(HEAD_DIM)

        if has_sink:
            @pl.when(pl.program_id(0) == 0)
            def _():
                dsink_ref[...] = jnp.zeros_like(dsink_ref)

        for pair in range(n_pairs):
            col = slice(pair * LANES, (pair + 1) * LANES)
            dov = do_ref[:, col]
            lv = lse_ref[:, col]
            delta = _seg_sum(dov * o_ref[:, col], seg)
            if has_sink:
                dsink_ref[pair] += -jnp.sum(jnp.exp(sink_ref[pair] - lv) * delta, axis=0, keepdims=True)
            s_do[...] = dov
            s_l[...] = lv
            s_d[...] = delta
            for g, d in enumerate(dils):
                rows = tm // d
                for r in range(d):
                    oc = slice((pair * d + r) * LANES, (pair * d + r + 1) * LANES)
                    if d == 1:
                        a, b, c = s_do[...], s_l[...], s_d[...]
                    else:
                        a = s_do[pl.ds(r, rows, stride=d), :]
                        b = s_l[pl.ds(r, rows, stride=d), :]
                        c = s_d[pl.ds(r, rows, stride=d), :]
                    outs[3 * g][:, oc] = a.astype(BF16)
                    outs[3 * g + 1][:, oc] = b
                    outs[3 * g + 2][:, oc] = c

    row = pl.BlockSpec((tm, width), lambda i: (i, 0))
    in_specs = [row, row, row]
    args = [do, o, lse]
    if has_sink:
        in_specs.append(pl.BlockSpec((n_pairs, 1, LANES), lambda i: (0, 0, 0)))
        args.append(sinks)
    out_specs, out_shape = [], []
    for d in dils:
        for dt in (BF16, F32, F32):
            out_specs.append(pl.BlockSpec((tm // d, n_pairs * d * LANES), lambda i: (i, 0)))
            out_shape.append(jax.ShapeDtypeStruct((s_len // d, n_pairs * d * LANES), dt))
    if has_sink:
        out_specs.append(pl.BlockSpec((n_pairs, 1, LANES), lambda i: (0, 0, 0)))
        out_shape.append(jax.ShapeDtypeStruct((n_pairs, 1, LANES), F32))
    return pl.pallas_call(
        body, name=name, grid=(s_len // tm,), in_specs=in_specs, out_specs=tuple(out_specs),
        out_shape=tuple(out_shape), scratch_shapes=[pltpu.VMEM((tm, LANES), F32)] * 3, compiler_params=_cparams(1),
    )(*args)


def _mem_kv(mem, mem_gain, w_kv, k_gain, name):
    m_len = mem.shape[0]
    kw = M_HEADS * M_HEAD_DIM

    def body(mem_ref, mg_ref, w_ref, kg_ref, k_ref, v_ref):
        mv = mem_ref[...]
        r = lax.rsqrt(jnp.mean(mv * mv, axis=-1, keepdims=True) + EPS)
        mn = ((mv * r) * mg_ref[...]).astype(BF16)
        kv = jnp.dot(mn, w_ref[...], preferred_element_type=F32)
        for h in range(M_HEADS):
            col = slice(h * M_HEAD_DIM, (h + 1) * M_HEAD_DIM)
            t = kv[:, col]
            rk = lax.rsqrt(jnp.mean(t * t, axis=-1, keepdims=True) + EPS)
            k_ref[:, col] = ((t * rk) * kg_ref[...]).astype(BF16)
        v_ref[...] = kv[:, kw:].astype(BF16)

    shp = jax.ShapeDtypeStruct((m_len, kw), BF16)
    return pl.pallas_call(body, name=name, out_shape=(shp, shp),
                          compiler_params=pltpu.CompilerParams(vmem_limit_bytes=VMEM_LIMIT_BYTES))(mem, mem_gain, w_kv, k_gain)


def _mem_kv_bwd(mem, mem_gain, w_kv, k_gain, dk, dv, name):
    m_len, d = mem.shape
    kw = M_HEADS * M_HEAD_DIM

    def body(mem_ref, mg_ref, w_ref, kg_ref, dk_ref, dv_ref, dw_ref, dmg_ref, dkg_ref, dkv_ref):
        mv = mem_ref[...]
        r = lax.rsqrt(jnp.mean(mv * mv, axis=-1, keepdims=True) + EPS)
        mhat = mv * r
        mn = (mhat * mg_ref[...]).astype(BF16)
        kv = jnp.dot(mn, w_ref[...], preferred_element_type=F32)
        dkg = jnp.zeros((1, M_HEAD_DIM), F32)
        for h in range(M_HEADS):
            col = slice(h * M_HEAD_DIM, (h + 1) * M_HEAD_DIM)
            t = kv[:, col]
            rk = lax.rsqrt(jnp.mean(t * t, axis=-1, keepdims=True) + EPS)
            that = t * rk
            dy = dk_ref[:, col]
            dkg = dkg + jnp.sum(dy * that, axis=0, keepdims=True)
            dthat = dy * kg_ref[...]
            dkv_ref[:, col] = (rk * (dthat - that * jnp.mean(dthat * that, axis=-1, keepdims=True))).astype(BF16)
        dkv_ref[:, kw:] = dv_ref[...].astype(BF16)
        dkg_ref[...] = dkg
        dkv = dkv_ref[...]
        dw_ref[...] = lax.dot_general(mn, dkv, (((0,), (0,)), ((), ())), preferred_element_type=F32).astype(GRAD_DTYPE)
        dmn = lax.dot_general(dkv, w_ref[...], (((1,), (1,)), ((), ())), preferred_element_type=F32)
        dmg_ref[...] = jnp.sum(dmn * mhat, axis=0, keepdims=True)

    return pl.pallas_call(
        body, name=name,
        out_shape=(jax.ShapeDtypeStruct((d, 2 * kw), GRAD_DTYPE), jax.ShapeDtypeStruct((1, d), F32),
                   jax.ShapeDtypeStruct((1, M_HEAD_DIM), F32)),
        scratch_shapes=[pltpu.VMEM((m_len, 2 * kw), BF16)],
        compiler_params=pltpu.CompilerParams(vmem_limit_bytes=VMEM_LIMIT_BYTES),
    )(mem, mem_gain, w_kv, k_gain, dk, dv)


def _mem_attn_fwd(proj, cidx, mk, mv, q_gain, name):
    s_len = proj.shape[0]
    kw = M_HEADS * M_HEAD_DIM
    tm = 512
    scale = M_HEAD_DIM ** -0.5

    def body(q_ref, k_ref, v_ref, g_ref, o_ref):
        for h in range(M_HEADS):
            col = slice(h * M_HEAD_DIM, (h + 1) * M_HEAD_DIM)
            t = q_ref[:, col]
            rs = lax.rsqrt(jnp.mean(t * t, axis=-1, keepdims=True) + EPS)
            qn = ((t * rs) * g_ref[...]).astype(BF16)
            s = lax.dot_general(qn, k_ref[:, col], (((1,), (1,)), ((), ())), preferred_element_type=F32) * scale
            mx = jnp.max(s, axis=-1, keepdims=True)
            p = jnp.exp(s - mx)
            pn = (p * (1.0 / jnp.sum(p, axis=-1, keepdims=True))).astype(BF16)
            o_ref[:, col] = jnp.dot(pn, v_ref[:, col], preferred_element_type=F32).astype(BF16)

    whole = pl.BlockSpec((MEM_LEN, kw), lambda i: (0, 0))
    return pl.pallas_call(
        body, name=name, grid=(s_len // tm,),
        in_specs=[pl.BlockSpec((tm, kw), lambda i: (i, cidx)), whole, whole, pl.BlockSpec((1, M_HEAD_DIM), lambda i: (0, 0))],
        out_specs=pl.BlockSpec((tm, kw), lambda i: (i, 0)),
        out_shape=jax.ShapeDtypeStruct((s_len, kw), BF16), compiler_params=_cparams(1),
    )(proj, mk, mv, q_gain)


def _mem_attn_bwd(proj, cidx, mk, mv, q_gain, do, name):
    s_len = proj.shape[0]
    kw = M_HEADS * M_HEAD_DIM
    tm = 512
    scale = M_HEAD_DIM ** -0.5

    def body(q_ref, k_ref, v_ref, g_ref, do_ref, dq_ref, dk_ref, dv_ref, dg_ref):
        @pl.when(pl.program_id(0) == 0)
        def _():
            dk_ref[...] = jnp.zeros_like(dk_ref)
            dv_ref[...] = jnp.zeros_like(dv_ref)
            dg_ref[...] = jnp.zeros_like(dg_ref)

        for h in range(M_HEADS):
            col = slice(h * M_HEAD_DIM, (h + 1) * M_HEAD_DIM)
            t = q_ref[:, col]
            rs = lax.rsqrt(jnp.mean(t * t, axis=-1, keepdims=True) + EPS)
            that = t * rs
            qn = (that * g_ref[...]).astype(BF16)
            kh, vh = k_ref[:, col], v_ref[:, col]
            dob = do_ref[:, col].astype(BF16)
            s = lax.dot_general(qn, kh, (((1,), (1,)), ((), ())), preferred_element_type=F32) * scale
            mx = jnp.max(s, axis=-1, keepdims=True)
            p = jnp.exp(s - mx)
            p = p * (1.0 / jnp.sum(p, axis=-1, keepdims=True))
            dp = lax.dot_general(dob, vh, (((1,), (1,)), ((), ())), preferred_element_type=F32)
            ds = (p * (dp - jnp.sum(p * dp, axis=-1, keepdims=True)) * scale).astype(BF16)
            dqn = jnp.dot(ds, kh, preferred_element_type=F32)
            dk_ref[:, col] += lax.dot_general(ds, qn, (((0,), (0,)), ((), ())), preferred_element_type=F32)
            dv_ref[:, col] += lax.dot_general(p.astype(BF16), dob, (((0,), (0,)), ((), ())), preferred_element_type=F32)
            dg_ref[...] += jnp.sum(dqn * that, axis=0, keepdims=True)
            dthat = dqn * g_ref[...]
            dq_ref[:, col] = (rs * (dthat - that * jnp.mean(dthat * that, axis=-1, keepdims=True))).astype(BF16)

    whole = pl.BlockSpec((MEM_LEN, kw), lambda i: (0, 0))
    vec = pl.BlockSpec((1, M_HEAD_DIM), lambda i: (0, 0))
    row = pl.BlockSpec((tm, kw), lambda i: (i, 0))
    return pl.pallas_call(
        body, name=name, grid=(s_len // tm,),
        in_specs=[pl.BlockSpec((tm, kw), lambda i: (i, cidx)), whole, whole, vec, row],
        out_specs=(row, whole, whole, vec),
        out_shape=(jax.ShapeDtypeStruct((s_len, kw), BF16), jax.ShapeDtypeStruct((MEM_LEN, kw), F32),
                   jax.ShapeDtypeStruct((MEM_LEN, kw), F32), jax.ShapeDtypeStruct((1, M_HEAD_DIM), F32)),
        compiler_params=_cparams(1),
    )(proj, mk, mv, q_gain, do)


def _project_merge(outs, w_outs, gates, w_out, x, name):
    s_len = gates.shape[0]
    d = w_outs[0].shape[1]
    tm = 512

    def body(oa_ref, ob_ref, om_ref, wa_ref, wb_ref, wm_ref, g_ref, wo_ref, x_ref,
             pa_ref, pb_ref, pm_ref, merged_ref, x1_ref):
        merged = None
        for k, (o_ref, w_ref, p_ref) in enumerate(((oa_ref, wa_ref, pa_ref), (ob_ref, wb_ref, pb_ref), (om_ref, wm_ref, pm_ref))):
            p = jnp.dot(o_ref[...].astype(BF16), w_ref[...], preferred_element_type=F32).astype(BF16)
            p_ref[...] = p
            t = g_ref[:, k * d:(k + 1) * d].astype(F32) * p.astype(F32)
            merged = t if merged is None else merged + t
        merged = merged.astype(BF16)
        merged_ref[...] = merged
        x1_ref[...] = x_ref[...] + jnp.dot(merged, wo_ref[...], preferred_element_type=F32)

    row = pl.BlockSpec((tm, d), lambda i: (i, 0))
    shp = jax.ShapeDtypeStruct((s_len, d), BF16)
    in_specs = [pl.BlockSpec((tm, o.shape[1]), lambda i: (i, 0)) for o in outs]
    in_specs += [_resident(w.shape, lambda i: (0, 0)) for w in w_outs]
    in_specs += [pl.BlockSpec((tm, 3 * d), lambda i: (i, 0)), _resident(w_out.shape, lambda i: (0, 0)), row]
    return pl.pallas_call(
        body, name=name, grid=(s_len // tm,), in_specs=in_specs, out_specs=(row, row, row, row, row),
        out_shape=(shp, shp, shp, shp, jax.ShapeDtypeStruct((s_len, d), F32)), compiler_params=_cparams(1),
    )(*outs, *w_outs, gates, w_out, x)


def _project_merge_bwd(dx1, w_out, gates, pa, pb, pm, name):
    s_len, d = pa.shape
    tm = 512

    def body(dx_ref, w_ref, g_ref, a_ref, b_ref, m_ref, da_ref, db_ref, dmm_ref, dg_ref, dbg_ref):
        @pl.when(pl.program_id(0) == 0)
        def _():
            dbg_ref[...] = jnp.zeros_like(dbg_ref)
        dm = lax.dot_general(dx_ref[...], w_ref[...], (((1,), (1,)), ((), ())), preferred_element_type=F32)
        for k, (p_ref, dp_ref) in enumerate(((a_ref, da_ref), (b_ref, db_ref), (m_ref, dmm_ref))):
            col = slice(k * d, (k + 1) * d)
            g = g_ref[:, col].astype(F32)
            dp_ref[...] = (dm * g).astype(BF16)
            dpre = (dm * p_ref[...].astype(F32)) * (g * (1.0 - g))
            dbg_ref[:, col] += jnp.sum(dpre, axis=0, keepdims=True)
            dg_ref[:, col] = dpre.astype(BF16)

    row = pl.BlockSpec((tm, d), lambda i: (i, 0))
    wide = pl.BlockSpec((tm, 3 * d), lambda i: (i, 0))
    shp = jax.ShapeDtypeStruct((s_len, d), BF16)
    return pl.pallas_call(
        body, name=name, grid=(s_len // tm,), in_specs=[row, _resident(w_out.shape, lambda i: (0, 0)), wide, row, row, row],
        out_specs=(row, row, row, wide, pl.BlockSpec((1, 3 * d), lambda i: (0, 0))),
        out_shape=(shp, shp, shp, jax.ShapeDtypeStruct((s_len, 3 * d), BF16), jax.ShapeDtypeStruct((1, 3 * d), F32)),
        compiler_params=_cparams(1),
    )(dx1, w_out, gates, pa, pb, pm)


CONV_CHUNK = 1024


def _pick_row(tile, j):
    row = lax.broadcasted_iota(jnp.int32, tile.shape, 0)
    return jnp.sum(jnp.where(row == j, tile, jnp.zeros_like(tile)), axis=0, keepdims=True)


def _rows_before(ref, start, k):
    cur = ref[pl.ds(start, CONV_CHUNK), :].astype(F32)
    prev = ref[pl.ds(pl.multiple_of(jnp.maximum(start - 16, 0), 16), 16), :].astype(F32)
    prev = jnp.where(start > 0, prev, jnp.zeros_like(prev))
    rolled = pltpu.roll(cur, k, 0)
    row = lax.broadcasted_iota(jnp.int32, cur.shape, 0)
    for j in range(k):
        rolled = jnp.where(row == j, _pick_row(prev, 16 - k + j), rolled)
    return rolled


def _rows_after(ref, start, k):
    cur = ref[pl.ds(start, CONV_CHUNK), :]
    nxt = ref[pl.ds(pl.multiple_of(start + CONV_CHUNK, 8), 8), :]
    rolled = pltpu.roll(cur, CONV_CHUNK - k, 0)
    row = lax.broadcasted_iota(jnp.int32, cur.shape, 0)
    for j in range(k):
        rolled = jnp.where(row == CONV_CHUNK - k + j, _pick_row(nxt, j), rolled)
    return rolled


def _conv_pre(u_ref, w_ref, b_ref, start):
    u2 = _rows_before(u_ref, start, 2)
    u1 = _rows_before(u_ref, start, 1)
    u0 = u_ref[pl.ds(start, CONV_CHUNK), :].astype(F32)
    c = ((b_ref[...] + w_ref[0:1, :] * u2) + w_ref[1:2, :] * u1) + w_ref[2:3, :] * u0
    return c, (u2, u1, u0)


def _norm_up_conv_glu(x, gain, w_up, conv_w, conv_b, name):
    s_len, d = x.shape
    tm, tn = 512, 2 * LANES
    nblk = D_FF // tn

    def body(x_ref, g_ref, w_ref, cw_ref, cb_ref, ht_ref, r_ref, u_ref, act_ref, halo):
        @pl.when(pl.program_id(0) == 0)
        def _():
            halo[...] = jnp.zeros_like(halo)
        xv = x_ref[...]
        r = lax.rsqrt(jnp.mean(xv * xv, axis=-1, keepdims=True) + EPS)
        hv = ((xv * r) * g_ref[...]).astype(BF16)
        ht_ref[...] = hv.T
        r_ref[...] = r
        row = lax.broadcasted_iota(jnp.int32, (tm, tn), 0)
        for j in range(nblk):
            conv = []
            for half in range(2):
                cb = half * nblk + j
                cols = slice(cb * tn, (cb + 1) * tn)
                ub = jnp.dot(hv, w_ref[:, cols], preferred_element_type=F32).astype(BF16)
                u_ref[:, cols] = ub
                u0 = ub.astype(F32)
                prev = halo[cb]
                u1 = jnp.where(row == 0, _pick_row(prev, 7), pltpu.roll(u0, 1, 0))
                u2 = pltpu.roll(u0, 2, 0)
                u2 = jnp.where(row == 0, _pick_row(prev, 6), jnp.where(row == 1, _pick_row(prev, 7), u2))
                halo[cb] = u0[tm - 8:tm, :]
                conv.append(((cb_ref[:, cols] + cw_ref[0:1, cols] * u2) + cw_ref[1:2, cols] * u1)
                            + cw_ref[2:3, cols] * u0)
            act_ref[:, j * tn:(j + 1) * tn] = ((conv[0] * _sigmoid(conv[0])) * conv[1]).astype(BF16)

    return pl.pallas_call(
        body, name=name, grid=(s_len // tm,),
        in_specs=[pl.BlockSpec((tm, d), lambda i: (i, 0)), _resident((1, d), lambda i: (0, 0)),
                  _resident((d, 2 * D_FF), lambda i: (0, 0)),
                  _resident((3, 2 * D_FF), lambda i: (0, 0)), _resident((1, 2 * D_FF), lambda i: (0, 0))],
        out_specs=(pl.BlockSpec((d, tm), lambda i: (0, i)), pl.BlockSpec((tm, 1), lambda i: (i, 0)),
                   pl.BlockSpec((tm, 2 * D_FF), lambda i: (i, 0)), pl.BlockSpec((tm, D_FF), lambda i: (i, 0))),
        out_shape=(jax.ShapeDtypeStruct((d, s_len), BF16), jax.ShapeDtypeStruct((s_len, 1), F32),
                   jax.ShapeDtypeStruct((s_len, 2 * D_FF), BF16), jax.ShapeDtypeStruct((s_len, D_FF), BF16)),
        scratch_shapes=[pltpu.VMEM((2 * nblk, 8, tn), F32)], compiler_params=_cparams(1),
    )(x, gain, w_up, conv_w, conv_b)


def _conv_glu_bwd(dact, u, conv_w, conv_b, name):
    s_len = u.shape[0]
    nblk = D_FF // LANES
    n_chunks = s_len // CONV_CHUNK

    def body(da_ref, ua_ref, ug_ref, wa_ref, wg_ref, ba_ref, bg_ref,
             dua_ref, dug_ref, dwa_ref, dwg_ref, dba_ref, dbg_ref, sa, sg):
        sa[pl.ds(s_len, 8), :] = jnp.zeros((8, LANES), F32)
        sg[pl.ds(s_len, 8), :] = jnp.zeros((8, LANES), F32)
        zero = jnp.zeros((1, LANES), F32)

        def chunk1(ci, carry):
            start = pl.multiple_of(ci * CONV_CHUNK, CONV_CHUNK)
            ca, ua = _conv_pre(ua_ref, wa_ref, ba_ref, start)
            cg, ug = _conv_pre(ug_ref, wg_ref, bg_ref, start)
            dact_v = da_ref[pl.ds(start, CONV_CHUNK), :].astype(F32)
            sig = _sigmoid(ca)
            dcg = dact_v * (ca * sig)
            dca = (dact_v * cg) * (sig * (1.0 + ca * (1.0 - sig)))
            sa[pl.ds(start, CONV_CHUNK), :] = dca
            sg[pl.ds(start, CONV_CHUNK), :] = dcg
            out = [carry[0] + jnp.sum(dca, axis=0, keepdims=True), carry[1] + jnp.sum(dcg, axis=0, keepdims=True)]
            for j in range(3):
                out.append(carry[2 + j] + jnp.sum(dca * ua[j], axis=0, keepdims=True))
            for j in range(3):
                out.append(carry[5 + j] + jnp.sum(dcg * ug[j], axis=0, keepdims=True))
            return tuple(out)

        acc = lax.fori_loop(0, n_chunks, chunk1, (zero,) * 8)
        dba_ref[...] = acc[0]
        dbg_ref[...] = acc[1]
        for j in range(3):
            dwa_ref[j:j + 1, :] = acc[2 + j]
            dwg_ref[j:j + 1, :] = acc[5 + j]

        def chunk2(ci, carry):
            start = pl.multiple_of(ci * CONV_CHUNK, CONV_CHUNK)
            for s_ref, w_ref, o_ref in ((sa, wa_ref, dua_ref), (sg, wg_ref, dug_ref)):
                d0 = s_ref[pl.ds(start, CONV_CHUNK), :]
                d1 = _rows_after(s_ref, start, 1)
                d2 = _rows_after(s_ref, start, 2)
                o_ref[pl.ds(start, CONV_CHUNK), :] = (w_ref[2:3, :] * d0 + w_ref[1:2, :] * d1
                                                      + w_ref[0:1, :] * d2).astype(BF16)
            return carry
        lax.fori_loop(0, n_chunks, chunk2, 0)

    def col(rows, off):
        return pl.BlockSpec((rows, LANES), lambda j: (0, off + j))

    big = jax.ShapeDtypeStruct((s_len, D_FF), BF16)
    return pl.pallas_call(
        body, name=name, grid=(nblk,),
        in_specs=[col(s_len, 0), col(s_len, 0), col(s_len, nblk), col(3, 0), col(3, nblk), col(1, 0), col(1, nblk)],
        out_specs=(col(s_len, 0), col(s_len, 0), col(3, 0), col(3, 0), col(1, 0), col(1, 0)),
        out_shape=(big, big, jax.ShapeDtypeStruct((3, D_FF), F32), jax.ShapeDtypeStruct((3, D_FF), F32),
                   jax.ShapeDtypeStruct((1, D_FF), F32), jax.ShapeDtypeStruct((1, D_FF), F32)),
        scratch_shapes=[pltpu.VMEM((s_len + 8, LANES), F32)] * 2, compiler_params=_cparams(1),
    )(dact, u, u, conv_w, conv_w, conv_b, conv_b)


def _rope_tables(positions):
    half = ROPE_DIMS // 2
    freqs = jnp.exp(jnp.arange(half, dtype=F32) * (-2.0 * math.log(ROPE_THETA) / ROPE_DIMS))
    ang = positions.reshape(-1).astype(F32)[:, None] * freqs
    cos, sin = jnp.cos(ang), jnp.sin(ang)
    n = ang.shape[0]
    zeros = lambda w: jnp.zeros((n, w), F32)
    c = jnp.concatenate([cos, cos, jnp.ones((n, HEAD_DIM - ROPE_DIMS), F32)], axis=1)
    s1 = jnp.concatenate([-sin, zeros(HEAD_DIM - half)], axis=1)
    s2 = jnp.concatenate([zeros(half), sin, zeros(HEAD_DIM - ROPE_DIMS)], axis=1)
    return tuple(jnp.tile(t, (1, 2)) for t in (c, s1, s2))


def _two(v):
    return jnp.tile(v.reshape(1, HEAD_DIM), (1, 2))


def _fold_heads(g):
    return g[0, :HEAD_DIM] + g[0, HEAD_DIM:]


MIX_WEIGHTS = ('w_gate', 'w_mem_kv', 'w_o_a', 'w_o_b', 'w_o_m', 'w_out')
FFN_WEIGHTS = ('w_up', 'conv_w', 'w_down')


def _device_step(x, mem, positions, target, w, hooks=None):
    tabs = _rope_tables(positions)
    dils = tuple(d for _, d in A_GROUPS)
    grads = {}
    w = dict(w)

    h, h_t, r1 = _rms_fwd(x, w['attn_norm'], "rms1")
    if hooks is not None:
        w.update(hooks.weights('in', [h, *tabs]))
    proj = _mm_rows([(h, w['w_in'], 0)], "mm_in")

    qkv_a, o_g, lse_g = [], [], []
    for gi, (window, d) in enumerate(A_GROUPS):
        gq, gk = _two(w['a_q_norm'][gi]), _two(w['a_k_norm'][gi])
        qkv = _qk_prep(proj, 6 * gi, d, False, gq, gk, tabs, f"qk_prep_a{gi}")
        o, lse = _band_fwd(qkv, 2 * d, window // d, None, f"band_fwd_a{gi}")
        qkv_a.append(qkv)
        o_g.append(o)
        lse_g.append(lse)
    o_a, lse_a = _merge_groups(o_g, lse_g, dils, "merge_a")
    if hooks is not None:
        w.update(hooks.weights('mix', [o_a]))

    gbq, gbk = _two(w['b_q_norm']), _two(w['b_k_norm'])
    sinks = jnp.repeat(w['b_sinks'].reshape(4, 2), HEAD_DIM, axis=1).reshape(4, 1, LANES)
    qkv_b = _qk_prep(proj, 18, 1, True, gbq, gbk, tabs, "qk_prep_b")
    o_b, lse_b = _band_fwd(qkv_b, 4, B_WINDOW - 1, sinks, "band_fwd_b")

    gates = _mm_rows([(h, w['w_gate'], 0)], "mm_gate", bias=w['b_gate'], sigmoid=True, out_dtypes=(BF16,))
    mk, mv = _mem_kv(mem, w['mem_norm'], w['w_mem_kv'], w['m_k_norm'], "mem_kv")
    o_m = _mem_attn_fwd(proj, 6, mk, mv, w['m_q_norm'], "mem_attn")

    pa, pb, pm, merged, x1 = _project_merge((o_a, o_b, o_m), (w['w_o_a'], w['w_o_b'], w['w_o_m']), gates, w['w_out'], x,
                                            "project_merge")

    if hooks is not None:
        w.update(hooks.weights('ffn', [x1]))
    h2_t, r2, u, act = _norm_up_conv_glu(x1, w['ffn_norm'], w['w_up'], w['conv_w'], w['conv_b'], "norm_up_conv_glu")
    dy, dy_b, dact, loss = _mm_rows([(act, w['w_down'], 0)], "mm_down", res=x1, loss_target=target)

    grads['w_down'] = _mm_tn(act, dy_b, "mm_dw_down")
    du_a, du_g, dcw_a, dcw_g, dcb_a, dcb_g = _conv_glu_bwd(dact, u, w['conv_w'], w['conv_b'], "conv_glu_bwd")
    grads['conv_w'] = jnp.concatenate([dcw_a, dcw_g], axis=1)
    grads['conv_b'] = jnp.concatenate([dcb_a, dcb_g], axis=1)
    grads['w_up'] = _mm_cols(h2_t, [du_a, du_g], "mm_dw_up")
    ffn_gain = w['ffn_norm']
    if hooks is not None:
        ffn_gain = ffn_gain + hooks.grads('ffn', grads)[0:1, 0:1]
    dx1, dx1_b, grads['ffn_norm'] = _mm_rows([(du_a, w['w_up'], 0), (du_g, w['w_up'], 1)], "mm_d_h2", nt=True,
                                             rms_bwd=(x1, r2, ffn_gain, dy), out_dtypes=(F32, BF16))

    grads['w_out'] = _mm_tn(merged, dx1_b, "mm_dw_out")
    dpa, dpb, dpm, dgpre, grads['b_gate'] = _project_merge_bwd(dx1_b, w['w_out'], gates, pa, pb, pm,
                                                               "project_merge_bwd")
    do_a, do_b, do_m = _mm_rows_each([(dpa, w['w_o_a']), (dpb, w['w_o_b']), (dpm, w['w_o_m'])], "mm_d_o")
    grads['w_o_a'], grads['w_o_b'], grads['w_o_m'] = _mm_tn_each([(o_a, dpa), (o_b, dpb), (o_m, dpm)], "mm_dw_o")
    grads['w_gate'] = _mm_cols(h_t, [dgpre], "mm_dw_gate")
    dq_m, dmk, dmv, grads['m_q_norm'] = _mem_attn_bwd(proj, 6, mk, mv, w['m_q_norm'], do_m, "mem_attn_bwd")
    grads['w_mem_kv'], grads['mem_norm'], grads['m_k_norm'] = _mem_kv_bwd(
        mem, w['mem_norm'], w['w_mem_kv'], w['m_k_norm'], dmk, dmv, "mem_kv_bwd")
    a_gain = w['a_q_norm']
    if hooks is not None:
        a_gain = a_gain + hooks.grads('mix', grads)[0:1, 0:1]

    prep = _bwd_prep(do_a, o_a, lse_a, dils, None, "bwd_prep_a")
    dproj, dgq_a, dgk_a = [], [], []
    for gi, (window, d) in enumerate(A_GROUPS):
        gq, gk = _two(a_gain[gi]), _two(w['a_k_norm'][gi])
        dqkv = _band_bwd(qkv_a[gi], prep[3 * gi], prep[3 * gi + 1], prep[3 * gi + 2], 2 * d, window // d,
                         f"band_bwd_a{gi}")
        dp, dgq, dgk = _qk_prep_bwd(dqkv, proj, 6 * gi, d, False, gq, gk, tabs, f"qk_prep_bwd_a{gi}")
        dproj.append(dp)
        dgq_a.append(_fold_heads(dgq))
        dgk_a.append(_fold_heads(dgk))
    grads['a_q_norm'] = jnp.stack(dgq_a)
    grads['a_k_norm'] = jnp.stack(dgk_a)

    do_bu, lse_bu, delta_bu, dsink = _bwd_prep(do_b, o_b, lse_b, (1,), sinks, "bwd_prep_b")
    dqkv = _band_bwd(qkv_b, do_bu, lse_bu, delta_bu, 4, B_WINDOW - 1, "band_bwd_b")
    dp_b, dgq, dgk = _qk_prep_bwd(dqkv, proj, 18, 1, True, gbq, gbk, tabs, "qk_prep_bwd_b")
    dproj.append(dp_b)
    grads['b_q_norm'] = _fold_heads(dgq)
    grads['b_k_norm'] = _fold_heads(dgk)
    grads['b_sinks'] = jnp.stack([dsink[:, 0, 0], dsink[:, 0, HEAD_DIM]], axis=1).reshape(8)

    dproj.append(dq_m)

    cols = (0, 1, 2, 3, 6)
    grads['w_in'] = _mm_rows_cat(h_t, dproj, "mm_dw_in")
    attn_gain = w['attn_norm']
    if hooks is not None:
        attn_gain = attn_gain + hooks.grads('in', grads)[0:1, 0:1]
    grad_x, grads['attn_norm'] = _mm_rows(
        [(dp, w['w_in'], c) for dp, c in zip(dproj, cols)] + [(dgpre, w['w_gate'], 0)], "mm_d_h", nt=True,
        rms_bwd=(x, r1, attn_gain, dx1))
    return loss, grad_x, grads


def _coords():
    return lax.axis_index("x"), lax.axis_index("y"), lax.axis_index("c")


def _slot(p):
    return 4 * p[0] + 2 * p[1] + p[2]


ALL_PEERS = tuple(range(1, N_DEV))
CHIP_PEERS = (1, 4, 2, 6)
OTHER_CHIPS = (4, 2, 6)


def _peers(me, masks=ALL_PEERS):
    x, y, c = me
    return [(1 - x if mask & 4 else x, 1 - y if mask & 2 else y, 1 - c if mask & 1 else c) for mask in masks]


HBM_SPEC = pl.BlockSpec(memory_space=pltpu.HBM)


SEM_SPEC = pl.BlockSpec(memory_space=pltpu.SEMAPHORE)
SIDE_EFFECT = pltpu.SideEffectType.DATAFLOW_SIDE_EFFECTING


def _exchange_start(blocks, name, gather=False, masks=ALL_PEERS, after=None):
    n = len(blocks)
    n_peers = len(masks)
    n_in = 2 * n + (0 if after is None else 1)

    def body(*refs):
        ins, lands = refs[:n], refs[n:2 * n]
        send_sems, recv_sems = refs[n_in], refs[n_in + 1]
        token = refs[-1]
        me = _coords()
        peers = _peers(me, masks)
        for a in range(n):
            for k in range(n_peers):
                pltpu.make_async_remote_copy(
                    src_ref=ins[a] if gather else ins[a].at[_slot(peers[k])], dst_ref=lands[a].at[_slot(me)],
                    send_sem=send_sems.at[a * n_peers + k], recv_sem=recv_sems.at[a * n_peers + k],
                    device_id=peers[k], device_id_type=MESH).start()
        token[...] = jnp.zeros_like(token)

    land_shapes = [((N_DEV,) + b.shape) if gather else b.shape for b in blocks]
    hbm_in = [pltpu.HBM(b.shape, b.dtype) for b in blocks]
    hbm_land = [pltpu.HBM(s, b.dtype) for s, b in zip(land_shapes, blocks)]
    sems = pltpu.SemaphoreType.DMA((n * n_peers,))
    ins = [pltpu.with_memory_space_constraint(b, pltpu.HBM) for b in blocks]
    lands = [pltpu.with_memory_space_constraint(lax.empty(s, b.dtype), pltpu.HBM) for s, b in zip(land_shapes, blocks)]
    return pl.pallas_call(
        body, name=name, out_shape=(sems, sems, *hbm_in, *hbm_land, jax.ShapeDtypeStruct((8, LANES), F32)),
        in_specs=[HBM_SPEC] * (2 * n) + ([] if after is None else [pl.BlockSpec(memory_space=pl.ANY)]),
        out_specs=(SEM_SPEC, SEM_SPEC, *([HBM_SPEC] * (2 * n)), pl.BlockSpec(memory_space=pltpu.VMEM)),
        input_output_aliases={i: 2 + i for i in range(2 * n)},
        compiler_params=pltpu.CompilerParams(has_side_effects=SIDE_EFFECT),
    )(*ins, *lands, *([] if after is None else [after]))


def _exchange_wait(started, after, name, gather=False, masks=ALL_PEERS):
    n = (len(started) - 3) // 2
    n_peers = len(masks)
    send_sems, recv_sems = started[0], started[1]
    thru = started[2:2 + 2 * n]

    def body(*refs):
        ins, lands = refs[:n], refs[n:2 * n]
        send_ref, recv_ref = refs[2 * n], refs[2 * n + 1]
        me = _coords()
        peers = _peers(me, masks)
        for a in range(n):
            for k in range(n_peers):
                cp = pltpu.make_async_remote_copy(
                    src_ref=ins[a] if gather else ins[a].at[_slot(peers[k])], dst_ref=lands[a].at[_slot(peers[k])],
                    send_sem=send_ref.at[a * n_peers + k], recv_sem=recv_ref.at[a * n_peers + k],
                    device_id=peers[k], device_id_type=MESH)
                cp.wait_send()
                cp.wait_recv()

    hbm = [pltpu.HBM(t.shape, t.dtype) for t in thru]
    res = pl.pallas_call(
        body, name=name, out_shape=tuple(hbm),
        in_specs=[HBM_SPEC] * (2 * n) + [SEM_SPEC, SEM_SPEC] + [pl.BlockSpec(memory_space=pl.ANY)] * len(after),
        out_specs=tuple([HBM_SPEC] * (2 * n)), input_output_aliases={i: i for i in range(2 * n)},
        compiler_params=pltpu.CompilerParams(has_side_effects=SIDE_EFFECT),
    )(*thru, send_sems, recv_sems, *after)
    return res[n:]


def _sibling_forward(arrays, name):
    n = len(arrays)
    n_fwd = len(OTHER_CHIPS)

    def body(*refs):
        bufs = refs[n:2 * n]
        token, send_sems, recv_sems = refs[2 * n:]
        token[...] = jnp.zeros_like(token)
        x, y, c = _coords()
        sibling = (x, y, 1 - c)
        mine = _peers((x, y, c), OTHER_CHIPS)
        theirs = _peers(sibling, OTHER_CHIPS)

        def copy(a, k, block):
            rows = bufs[a].at[_slot(block)]
            return pltpu.make_async_remote_copy(
                src_ref=rows, dst_ref=rows, send_sem=send_sems.at[a * n_fwd + k], recv_sem=recv_sems.at[a * n_fwd + k],
                device_id=sibling, device_id_type=MESH)

        sends = [copy(a, k, mine[k]) for a in range(n) for k in range(n_fwd)]
        for cp in sends:
            cp.start()
        for a in range(n):
            for k in range(n_fwd):
                copy(a, k, theirs[k]).wait_recv()
        for cp in sends:
            cp.wait_send()

    res = pl.pallas_call(
        body, name=name, in_specs=[HBM_SPEC] * n,
        out_specs=tuple([HBM_SPEC] * n + [pl.BlockSpec(memory_space=pltpu.VMEM)]),
        out_shape=tuple([jax.ShapeDtypeStruct(a.shape, a.dtype) for a in arrays] + [jax.ShapeDtypeStruct((8, LANES), F32)]),
        input_output_aliases={i: i for i in range(n)},
        scratch_shapes=[pltpu.SemaphoreType.DMA((n * n_fwd,)), pltpu.SemaphoreType.DMA((n * n_fwd,))],
    )(*arrays)
    return res[:n], res[n]


def _all_sum(p, name):
    def body(p_ref, o_ref, recv, send_sems, recv_sems):
        me = _coords()
        peers = _peers(me)
        recv[_slot(me)] = p_ref[...]

        def copy(k, landing):
            return pltpu.make_async_remote_copy(
                src_ref=p_ref, dst_ref=recv.at[_slot(landing)], send_sem=send_sems.at[k], recv_sem=recv_sems.at[k],
                device_id=peers[k], device_id_type=MESH)

        sends = [copy(k, me) for k in range(N_DEV - 1)]
        for cp in sends:
            cp.start()
        for k in range(N_DEV - 1):
            copy(k, peers[k]).wait_recv()
        for cp in sends:
            cp.wait_send()
        acc = recv[0]
        for s in range(1, N_DEV):
            acc = acc + recv[s]
        o_ref[...] = acc

    vmem = pl.BlockSpec(memory_space=pltpu.VMEM)
    return pl.pallas_call(
        body, name=name, in_specs=[vmem], out_specs=vmem, out_shape=jax.ShapeDtypeStruct(p.shape, F32),
        scratch_shapes=[pltpu.VMEM((N_DEV,) + p.shape, F32), pltpu.SemaphoreType.DMA((N_DEV - 1,)),
                        pltpu.SemaphoreType.DMA((N_DEV - 1,))],
    )(p)


def _adam(w, g, m, v):
    m2 = ADAM_B1 * m + (1.0 - ADAM_B1) * g
    v2 = ADAM_B2 * v + (1.0 - ADAM_B2) * (g * g)
    m_hat = m2 / (1.0 - ADAM_B1 ** ADAM_STEP)
    v_hat = v2 / (1.0 - ADAM_B2 ** ADAM_STEP)
    delta = -ADAM_LR * (m_hat / (jnp.sqrt(v_hat) + ADAM_EPS) + ADAM_WD * w)
    return delta, m2, v2


def _row_tile(rows, cols):
    best = rows
    for t in range(16, rows, 16):
        if rows % t == 0 and t * cols * 4 <= (1 << 20):
            best = t
    return best


def _adam_reduce(parts, w, m, v, name):
    rows, cols = w.shape
    tr = _row_tile(rows, cols)

    def body(p_ref, w_ref, m_ref, v_ref, g_ref, d_ref, m2_ref, v2_ref):
        g = p_ref[0].astype(F32)
        for s in range(1, N_DEV):
            g = g + p_ref[s].astype(F32)
        g_ref[...] = g
        d_ref[...], m2_ref[...], v2_ref[...] = _adam(w_ref[...], g, m_ref[...], v_ref[...])

    blk = pl.BlockSpec((tr, cols), lambda i: (i, 0))
    shp = jax.ShapeDtypeStruct((rows, cols), F32)
    return pl.pallas_call(
        body, name=name, grid=(rows // tr,),
        in_specs=[pl.BlockSpec((N_DEV, tr, cols), lambda i: (0, i, 0)), blk, blk, blk],
        out_specs=(blk,) * 4, out_shape=(shp,) * 4, compiler_params=_cparams(1),
    )(parts, w, m, v)


PACK_COLS = 1024
PACK = {'attn_norm': (0, 1, 1024), 'mem_norm': (1, 1, 1024), 'ffn_norm': (2, 1, 1024), 'b_gate': (3, 3, 1024),
        'conv_b': (6, 6, 1024), 'a_q_norm': (12, 3, 64), 'a_k_norm': (15, 3, 64), 'b_q_norm': (18, 1, 64),
        'b_k_norm': (19, 1, 64), 'm_q_norm': (20, 1, 128), 'm_k_norm': (21, 1, 128), 'b_sinks': (22, 1, 8)}
PACK_LOSS_ROW = 23
PACK_ROWS = 24


def _pack_pieces(name, width):
    r0, nr, lanes = PACK[name]
    out = []
    for j in range(nr):
        if lanes == PACK_COLS:
            w = min(PACK_COLS, width - j * PACK_COLS)
            out.append((r0 + j, slice(0, 1), slice(j * PACK_COLS, j * PACK_COLS + w), w))
        else:
            out.append((r0 + j, slice(j, j + 1), slice(0, lanes), lanes))
    return out


def _pack_small(grads, loss_tile, name):
    names = list(PACK)

    def body(*refs):
        o_ref = refs[-1]
        o_ref[...] = jnp.zeros_like(o_ref)
        for k, nm in enumerate(names):
            for row, rs, ls, w in _pack_pieces(nm, refs[k].shape[1]):
                o_ref[row:row + 1, 0:w] = refs[k][rs, ls]
        o_ref[PACK_LOSS_ROW:PACK_LOSS_ROW + 1, 0:1] = refs[len(names)][0:1, 0:1]

    vmem = pl.BlockSpec(memory_space=pltpu.VMEM)
    args = [grads[nm] for nm in names] + [loss_tile]
    return pl.pallas_call(body, name=name, in_specs=[vmem] * len(args), out_specs=vmem,
                          out_shape=jax.ShapeDtypeStruct((PACK_ROWS, PACK_COLS), F32))(*args)


def _adam_small(gsum, ws, ms, vs, name):
    names = list(PACK)
    n = len(names)

    def body(*refs):
        g_ref = refs[0]
        w_refs, m_refs, v_refs = refs[1:1 + n], refs[1 + n:1 + 2 * n], refs[1 + 2 * n:1 + 3 * n]
        outs = refs[1 + 3 * n:]
        outs[0][...] = g_ref[PACK_LOSS_ROW:PACK_LOSS_ROW + 1, 0:1]
        for k, nm in enumerate(names):
            o_g, o_d, o_m, o_v = outs[1 + 4 * k:5 + 4 * k]
            for row, rs, ls, width in _pack_pieces(nm, w_refs[k].shape[1]):
                src = (rs, ls)
                g = g_ref[row:row + 1, 0:width]
                d, m2, v2 = _adam(w_refs[k][src], g, m_refs[k][src], v_refs[k][src])
                o_g[src] = g
                o_d[src] = d
                o_m[src] = m2
                o_v[src] = v2

    vmem = pl.BlockSpec(memory_space=pltpu.VMEM)
    shapes = [jax.ShapeDtypeStruct((1, 1), F32)]
    for nm in names:
        shapes += [jax.ShapeDtypeStruct(ws[nm].shape, F32)] * 4
    args = [gsum] + [ws[nm] for nm in names] + [ms[nm] for nm in names] + [vs[nm] for nm in names]
    return pl.pallas_call(
        body, name=name, in_specs=[vmem] * len(args), out_specs=tuple([vmem] * len(shapes)), out_shape=tuple(shapes),
    )(*args)


def _as2d(name, a):
    return a.reshape(a.shape[-2], a.shape[-1]) if a.ndim == 3 else a


def kernel(x, mem, positions, attn_norm, w_in, a_q_norm, a_k_norm, b_q_norm, b_k_norm, b_sinks, mem_norm, w_mem_kv, m_q_norm, m_k_norm, w_o_a, w_o_b, w_o_m, w_gate, b_gate, w_out, ffn_norm, w_up, conv_w, conv_b, w_down, loss_target, m_attn_norm, m_w_in, m_a_q_norm, m_a_k_norm, m_b_q_norm, m_b_k_norm, m_b_sinks, m_mem_norm, m_w_mem_kv, m_m_q_norm, m_m_k_norm, m_w_o_a, m_w_o_b, m_w_o_m, m_w_gate, m_b_gate, m_w_out, m_ffn_norm, m_w_up, m_conv_w, m_conv_b, m_w_down, v_attn_norm, v_w_in, v_a_q_norm, v_a_k_norm, v_b_q_norm, v_b_k_norm, v_b_sinks, v_mem_norm, v_w_mem_kv, v_m_q_norm, v_m_k_norm, v_w_o_a, v_w_o_b, v_w_o_m, v_w_gate, v_b_gate, v_w_out, v_ffn_norm, v_w_up, v_conv_w, v_conv_b, v_w_down):
    given = dict(attn_norm=attn_norm, w_in=w_in, a_q_norm=a_q_norm, a_k_norm=a_k_norm, b_q_norm=b_q_norm, b_k_norm=b_k_norm, b_sinks=b_sinks, mem_norm=mem_norm, w_mem_kv=w_mem_kv, m_q_norm=m_q_norm, m_k_norm=m_k_norm, w_o_a=w_o_a, w_o_b=w_o_b, w_o_m=w_o_m, w_gate=w_gate, b_gate=b_gate, w_out=w_out, ffn_norm=ffn_norm, w_up=w_up, conv_w=conv_w, conv_b=conv_b, w_down=w_down)
    mom1 = dict(attn_norm=m_attn_norm, w_in=m_w_in, a_q_norm=m_a_q_norm, a_k_norm=m_a_k_norm, b_q_norm=m_b_q_norm, b_k_norm=m_b_k_norm, b_sinks=m_b_sinks, mem_norm=m_mem_norm, w_mem_kv=m_w_mem_kv, m_q_norm=m_m_q_norm, m_k_norm=m_m_k_norm, w_o_a=m_w_o_a, w_o_b=m_w_o_b, w_o_m=m_w_o_m, w_gate=m_w_gate, b_gate=m_b_gate, w_out=m_w_out, ffn_norm=m_ffn_norm, w_up=m_w_up, conv_w=m_conv_w, conv_b=m_conv_b, w_down=m_w_down)
    mom2 = dict(attn_norm=v_attn_norm, w_in=v_w_in, a_q_norm=v_a_q_norm, a_k_norm=v_a_k_norm, b_q_norm=v_b_q_norm, b_k_norm=v_b_k_norm, b_sinks=v_b_sinks, mem_norm=v_mem_norm, w_mem_kv=v_w_mem_kv, m_q_norm=v_m_q_norm, m_k_norm=v_m_k_norm, w_o_a=v_w_o_a, w_o_b=v_w_o_b, w_o_m=v_w_o_m, w_gate=v_w_gate, b_gate=v_b_gate, w_out=v_w_out, ffn_norm=v_ffn_norm, w_up=v_w_up, conv_w=v_conv_w, conv_b=v_conv_b, w_down=v_w_down)

    big = list(BIG)
    stages = {'mix': list(MIX_WEIGHTS), 'ffn': list(FFN_WEIGHTS), 'in': ['w_in']}
    my_slot = _slot(_coords())

    def shard(n):
        return given[n][0] if n == 'conv_w' else given[n][0].astype(BF16)

    def whole(n, g):
        _, r, c = g.shape
        return g.reshape(N_DEV * r, c) if BIG[n] == 0 else g.transpose(1, 0, 2).reshape(r, N_DEV * c)

    def to_blocks(n, g):
        r, c = given[n].shape[1:]
        g = g.reshape(N_DEV, r, c) if BIG[n] == 0 else g.reshape(r, N_DEV, c).transpose(1, 0, 2)
        return g if n == 'conv_w' else g.astype(BF16)

    class Hooks:
        next_stage = {'in': 'mix', 'mix': 'ffn'}

        def __init__(self):
            self.coming, self.sent = {}, {}
            self.shards = {n: shard(n) for n in big}
            self.start_gather('in', None)

        def start_gather(self, stage, after):
            src = [self.shards[n] for n in stages[stage]]
            self.coming[stage] = _exchange_start(src, f"gather_{stage}_start", gather=True, masks=CHIP_PEERS,
                                                 after=after)

        def weights(self, stage, after):
            names = stages[stage]
            after = list(after)
            if stage == 'in':
                after += [self.shards[n] for n in stages['mix'] + stages['ffn']]
            landed = _exchange_wait(self.coming[stage], after, f"gather_{stage}_wait", gather=True, masks=CHIP_PEERS)
            landed, token = _sibling_forward(landed, f"gather_{stage}_forward")
            if stage in self.next_stage:
                self.start_gather(self.next_stage[stage], token)
            return {n: whole(n, lax.dynamic_update_slice_in_dim(land, self.shards[n][None], my_slot, axis=0))
                    for n, land in zip(names, landed)}

        def grads(self, stage, g):
            blocks = [to_blocks(n, g[n]) for n in stages[stage]]
            own = [lax.dynamic_slice_in_dim(b, my_slot, 1, axis=0) for b in blocks]
            self.sent[stage] = (_exchange_start(blocks, f"exchange_{stage}_start"), own)
            return self.sent[stage][0][-1]

        def parts(self, stage, after):
            started, own = self.sent[stage]
            landed = _exchange_wait(started, [after], f"exchange_{stage}_wait")
            return {n: lax.dynamic_update_slice_in_dim(land, o, my_slot, axis=0)
                    for n, land, o in zip(stages[stage], landed, own)}

    hooks = Hooks()
    w = {}
    for n in SMALL:
        w[n] = given[n]
    w['a_q_norm'], w['a_k_norm'] = given['a_q_norm'][0], given['a_k_norm'][0]
    w['b_q_norm'], w['b_k_norm'], w['b_sinks'] = given['b_q_norm'][0], given['b_k_norm'][0], given['b_sinks'][0]

    loss_tile, grad_x, grads = _device_step(x[0], mem[0], positions[0], loss_target[0], w, hooks)
    out = {}
    after = grad_x
    for stage in ('ffn', 'mix', 'in'):
        for n, p in hooks.parts(stage, after).items():
            res = _adam_reduce(p, given[n][0], mom1[n][0], mom2[n][0], f"adam_{n}")
            out[n] = tuple(t[None] for t in res)
            after = res[0]

    small = {n: grads[n] for n in PACK}
    small['b_q_norm'], small['b_k_norm'] = grads['b_q_norm'].reshape(1, -1), grads['b_k_norm'].reshape(1, -1)
    small['b_sinks'] = grads['b_sinks'].reshape(1, -1)
    gsum = _all_sum(_pack_small(small, loss_tile, "pack_small"), "sum_small")
    ws = {n: _as2d(n, given[n]) for n in PACK}
    ms = {n: _as2d(n, mom1[n]) for n in PACK}
    vs = {n: _as2d(n, mom2[n]) for n in PACK}
    res = _adam_small(gsum, ws, ms, vs, "adam_small")
    loss = res[0].reshape(())
    for k, n in enumerate(PACK):
        out[n] = tuple(t.reshape(given[n].shape) for t in res[1 + 4 * k:5 + 4 * k])

    outs = [loss, grad_x[None]]
    for field in range(4):
        outs += [out[n][field] for n in WEIGHTS]
    return tuple(outs)
```

```python
import functools
import math

import jax
import jax.numpy as jnp
from jax import lax
from jax.experimental import pallas as pl
from jax.experimental.pallas import tpu as pltpu

F32 = jnp.float32
BF16 = jnp.bfloat16

N_DEV = 8
HEAD_DIM = 64
A_GROUPS = ((128, 1), (512, 4), (2048, 16))
B_WINDOW = 128
M_HEADS = 4
M_HEAD_DIM = 128
MEM_LEN = 256
D_FF = 2816
ROPE_THETA = 500000.0
ROPE_DIMS = 16
BLOCK = 128
EPS = 1e-6
LANES = 128
BAND_Q_BLOCKS = 8
BAND_UNITS = 2

ADAM_LR = 0.001
ADAM_B1 = 0.9
ADAM_B2 = 0.999
ADAM_EPS = 1e-08
ADAM_WD = 0.01
ADAM_STEP = 10

VMEM_LIMIT_BYTES = 56 * 1024 * 1024
GRAD_DTYPE = BF16
MESH = pl.DeviceIdType.MESH

WEIGHTS = ['attn_norm', 'w_in', 'a_q_norm', 'a_k_norm', 'b_q_norm', 'b_k_norm', 'b_sinks', 'mem_norm',
           'w_mem_kv', 'm_q_norm', 'm_k_norm', 'w_o_a', 'w_o_b', 'w_o_m', 'w_gate', 'b_gate', 'w_out',
           'ffn_norm', 'w_up', 'conv_w', 'conv_b', 'w_down']
BIG = {'w_in': 1, 'w_mem_kv': 0, 'w_o_a': 1, 'w_o_b': 1, 'w_o_m': 1, 'w_gate': 1, 'w_out': 0, 'w_up': 1,
       'conv_w': 1, 'w_down': 0}
SMALL = [n for n in WEIGHTS if n not in BIG]


def _cparams(n_grid):
    return pltpu.CompilerParams(dimension_semantics=("arbitrary",) * n_grid, vmem_limit_bytes=VMEM_LIMIT_BYTES)


def _seg_matrix(width):
    shift = width.bit_length() - 1
    r = lax.shift_right_logical(lax.broadcasted_iota(jnp.int32, (LANES, LANES), 0), shift)
    c = lax.shift_right_logical(lax.broadcasted_iota(jnp.int32, (LANES, LANES), 1), shift)
    return jnp.where(r == c, 1.0, 0.0).astype(BF16)


def _seg_sum(x, seg):
    hi = x.astype(BF16)
    r1 = x - hi.astype(F32)
    mid = r1.astype(BF16)
    lo = (r1 - mid.astype(F32)).astype(BF16)
    dot = functools.partial(jnp.dot, preferred_element_type=F32)
    return dot(hi, seg) + dot(mid, seg) + dot(lo, seg)


def _rope(y, c, s1, s2):
    return y * c + pltpu.roll(y, LANES - ROPE_DIMS // 2, 1) * s1 + pltpu.roll(y, ROPE_DIMS // 2, 1) * s2


def _unrope(dy, c, s1, s2):
    return dy * c + pltpu.roll(dy * s1, ROPE_DIMS // 2, 1) + pltpu.roll(dy * s2, LANES - ROPE_DIMS // 2, 1)


def _sigmoid(x):
    return 1.0 / (1.0 + jnp.exp(-x))


def _rms_fwd(x, gain, name):
    s_len, d = x.shape
    tm = 512

    def body(x_ref, g_ref, h_ref, ht_ref, r_ref):
        xv = x_ref[...]
        r = lax.rsqrt(jnp.mean(xv * xv, axis=-1, keepdims=True) + EPS)
        h = ((xv * r) * g_ref[...]).astype(BF16)
        h_ref[...] = h
        ht_ref[...] = h.T
        r_ref[...] = r

    return pl.pallas_call(
        body, name=name, grid=(s_len // tm,),
        in_specs=[pl.BlockSpec((tm, d), lambda i: (i, 0)), pl.BlockSpec((1, d), lambda i: (0, 0))],
        out_specs=(pl.BlockSpec((tm, d), lambda i: (i, 0)), pl.BlockSpec((d, tm), lambda i: (0, i)),
                   pl.BlockSpec((tm, 1), lambda i: (i, 0))),
        out_shape=(jax.ShapeDtypeStruct((s_len, d), BF16), jax.ShapeDtypeStruct((d, s_len), BF16),
                   jax.ShapeDtypeStruct((s_len, 1), F32)),
        compiler_params=_cparams(1),
    )(x, gain)


def _resident(shape, index_map):
    return pl.BlockSpec(shape, index_map, pipeline_mode=pl.Buffered(1))


def _mm_rows(pairs, name, nt=False, tm=512, bias=None, sigmoid=False, res=None, out_dtypes=(F32,), loss_target=None,
             rms_bwd=None):
    m = pairs[0][0].shape[0]
    n = pairs[0][1].shape[0] if nt else pairs[0][1].shape[1]
    n_pairs = len(pairs)
    has_bias, has_res, has_loss = bias is not None, res is not None, loss_target is not None
    has_rms = rms_bwd is not None
    dims = (((1,), (1,)), ((), ())) if nt else (((1,), (0,)), ((), ()))

    def body(*refs):
        acc = None
        for p in range(n_pairs):
            t = lax.dot_general(refs[2 * p][...].astype(BF16), refs[2 * p + 1][...], dims, preferred_element_type=F32)
            acc = t if acc is None else acc + t
        pos = 2 * n_pairs
        if has_bias:
            acc = acc + refs[pos][...]
            pos += 1
        if sigmoid:
            acc = _sigmoid(acc)
        if has_res:
            acc = refs[pos][...] + acc
            pos += 1
        if has_loss:
            dy_ref, dyb_ref, da_ref, l_ref = refs[pos + 1:]

            @pl.when(pl.program_id(0) == 0)
            def _():
                l_ref[...] = jnp.zeros_like(l_ref)
            err = acc - refs[pos][...]
            dy = err * (1.0 / n)
            dy_ref[...] = dy
            dyb_ref[...] = dy.astype(BF16)
            da_ref[...] = lax.dot_general(dy.astype(BF16), refs[1][...], (((1,), (1,)), ((), ())),
                                          preferred_element_type=F32).astype(BF16)
            part = 0.5 * jnp.sum(jnp.mean(err * err, axis=-1, keepdims=True), axis=0, keepdims=True)
            l_ref[...] += jnp.broadcast_to(part, l_ref.shape)
            return
        if has_rms:
            x_ref, r_ref, g_ref, add_ref = refs[pos:pos + 4]
            dg_ref = refs[-1]

            @pl.when(pl.program_id(0) == 0)
            def _():
                dg_ref[...] = jnp.zeros_like(dg_ref)
            rv = r_ref[...]
            xhat = x_ref[...] * rv
            dg_ref[...] += jnp.sum(acc * xhat, axis=0, keepdims=True)
            dxhat = acc * g_ref[...]
            acc = add_ref[...] + rv * (dxhat - xhat * jnp.mean(dxhat * xhat, axis=-1, keepdims=True))
            for o_ref in refs[pos + 4:-1]:
                o_ref[...] = acc.astype(o_ref.dtype)
            return
        for o_ref in refs[pos:]:
            o_ref[...] = acc.astype(o_ref.dtype)

    in_specs, args = [], []
    for a, w, blk in pairs:
        k = a.shape[1]
        in_specs.append(pl.BlockSpec((tm, k), lambda i: (i, 0)))
        if nt:
            in_specs.append(_resident((n, k), lambda i, blk=blk: (0, blk)))
        else:
            in_specs.append(_resident((k, n), lambda i, blk=blk: (blk, 0)))
        args += [a, w]
    if has_bias:
        in_specs.append(_resident((1, n), lambda i: (0, 0)))
        args.append(bias)
    if has_res:
        in_specs.append(pl.BlockSpec((tm, n), lambda i: (i, 0)))
        args.append(res)
    out = pl.BlockSpec((tm, n), lambda i: (i, 0))
    if has_loss:
        k0 = pairs[0][0].shape[1]
        return pl.pallas_call(
            body, name=name, grid=(m // tm,), in_specs=in_specs + [out],
            out_specs=(out, out, pl.BlockSpec((tm, k0), lambda i: (i, 0)), pl.BlockSpec((8, LANES), lambda i: (0, 0))),
            out_shape=(jax.ShapeDtypeStruct((m, n), F32), jax.ShapeDtypeStruct((m, n), BF16),
                       jax.ShapeDtypeStruct((m, k0), BF16), jax.ShapeDtypeStruct((8, LANES), F32)),
            compiler_params=_cparams(1),
        )(*args, loss_target)
    if has_rms:
        x, r, gain, add = rms_bwd
        vec = _resident((1, n), lambda i: (0, 0))
        return pl.pallas_call(
            body, name=name, grid=(m // tm,),
            in_specs=in_specs + [out, pl.BlockSpec((tm, 1), lambda i: (i, 0)), vec, out],
            out_specs=tuple([out] * len(out_dtypes) + [pl.BlockSpec((1, n), lambda i: (0, 0))]),
            out_shape=tuple([jax.ShapeDtypeStruct((m, n), dt) for dt in out_dtypes] + [jax.ShapeDtypeStruct((1, n), F32)]),
            compiler_params=_cparams(1),
        )(*args, x, r, gain, add)
    outs = pl.pallas_call(
        body, name=name, grid=(m // tm,), in_specs=in_specs, out_specs=tuple([out] * len(out_dtypes)),
        out_shape=tuple(jax.ShapeDtypeStruct((m, n), dt) for dt in out_dtypes), compiler_params=_cparams(1),
    )(*args)
    return outs[0] if len(out_dtypes) == 1 else outs


def _mm_rows_each(pairs, name, tm=512):
    m = pairs[0][0].shape[0]
    n_pairs = len(pairs)

    def body(*refs):
        for p in range(n_pairs):
            refs[2 * n_pairs + p][...] = lax.dot_general(refs[2 * p][...].astype(BF16), refs[2 * p + 1][...],
                                                         (((1,), (1,)), ((), ())), preferred_element_type=F32)

    in_specs, args = [], []
    for a, w in pairs:
        in_specs += [pl.BlockSpec((tm, a.shape[1]), lambda i: (i, 0)), _resident(w.shape, lambda i: (0, 0))]
        args += [a, w]
    return pl.pallas_call(
        body, name=name, grid=(m // tm,), in_specs=in_specs,
        out_specs=tuple(pl.BlockSpec((tm, w.shape[0]), lambda i: (i, 0)) for _, w in pairs),
        out_shape=tuple(jax.ShapeDtypeStruct((m, w.shape[0]), F32) for _, w in pairs), compiler_params=_cparams(1),
    )(*args)


def _mm_rows_cat(a, ws, name, tm=256):
    m, k = a.shape
    widths = [w.shape[1] for w in ws]
    n = sum(widths)

    def body(*refs):
        a_ref, o_ref = refs[0], refs[-1]
        av = a_ref[...]
        off = 0
        for p, width in enumerate(widths):
            o_ref[:, off:off + width] = jnp.dot(av, refs[1 + p][...], preferred_element_type=F32).astype(GRAD_DTYPE)
            off += width

    return pl.pallas_call(
        body, name=name, grid=(m // tm,),
        in_specs=[pl.BlockSpec((tm, k), lambda i: (i, 0))] + [_resident((k, wd), lambda i: (0, 0)) for wd in widths],
        out_specs=pl.BlockSpec((tm, n), lambda i: (i, 0)),
        out_shape=jax.ShapeDtypeStruct((m, n), GRAD_DTYPE), compiler_params=_cparams(1),
    )(a, *ws)


def _mm_cols(a, bs, name, tn=256):
    m, k = a.shape
    counts = [b.shape[1] // tn for b in bs]
    starts = [sum(counts[:p]) for p in range(len(bs))]

    def body(*refs):
        a_ref, o_ref = refs[0], refs[-1]
        j = pl.program_id(0)
        for p, b_ref in enumerate(refs[1:-1]):
            @pl.when((j >= starts[p]) & (j < starts[p] + counts[p]))
            def _():
                o_ref[...] = jnp.dot(a_ref[...], b_ref[...].astype(BF16), preferred_element_type=F32).astype(GRAD_DTYPE)

    b_specs = [pl.BlockSpec((k, tn), lambda j, s=s, c=c: (0, jnp.clip(j - s, 0, c - 1))) for s, c in zip(starts, counts)]
    return pl.pallas_call(
        body, name=name, grid=(sum(counts),),
        in_specs=[_resident((m, k), lambda j: (0, 0))] + b_specs,
        out_specs=pl.BlockSpec((m, tn), lambda j: (0, j)),
        out_shape=jax.ShapeDtypeStruct((m, sum(counts) * tn), GRAD_DTYPE), compiler_params=_cparams(1),
    )(a, *bs)


def _mm_tn_each(pairs, name, tile=256):
    n = pairs[0][1].shape[1]
    n_pairs = len(pairs)
    dims = (((0,), (0,)), ((), ()))

    def body(*refs):
        for p in range(n_pairs):
            refs[2 * n_pairs + p][...] = lax.dot_general(refs[2 * p][...].astype(BF16), refs[2 * p + 1][...].astype(BF16),
                                                         dims, preferred_element_type=F32).astype(GRAD_DTYPE)

    in_specs, args = [], []
    for a, b in pairs:
        in_specs += [_resident(a.shape, lambda j: (0, 0)), pl.BlockSpec((b.shape[0], tile), lambda j: (0, j))]
        args += [a, b]
    return pl.pallas_call(
        body, name=name, grid=(n // tile,), in_specs=in_specs,
        out_specs=tuple(pl.BlockSpec((a.shape[1], tile), lambda j: (0, j)) for a, _ in pairs),
        out_shape=tuple(jax.ShapeDtypeStruct((a.shape[1], n), GRAD_DTYPE) for a, _ in pairs),
        compiler_params=_cparams(1),
    )(*args)


def _mm_tn(a, b, name, tile=256):
    k, m = a.shape
    n = b.shape[1]
    dims = (((0,), (0,)), ((), ()))

    def body(a_ref, b_ref, o_ref):
        o_ref[...] = lax.dot_general(a_ref[...].astype(BF16), b_ref[...].astype(BF16), dims,
                                     preferred_element_type=F32).astype(GRAD_DTYPE)

    if n <= m:
        t = min(tile, m)
        grid, a_spec, b_spec = (m // t,), pl.BlockSpec((k, t), lambda i: (0, i)), _resident((k, n), lambda i: (0, 0))
        o_spec = pl.BlockSpec((t, n), lambda i: (i, 0))
    else:
        t = min(tile, n)
        grid, a_spec, b_spec = (n // t,), _resident((k, m), lambda i: (0, 0)), pl.BlockSpec((k, t), lambda i: (0, i))
        o_spec = pl.BlockSpec((m, t), lambda i: (0, i))
    return pl.pallas_call(
        body, name=name, grid=grid, in_specs=[a_spec, b_spec], out_specs=o_spec,
        out_shape=jax.ShapeDtypeStruct((m, n), GRAD_DTYPE), compiler_params=_cparams(1),
    )(a, b)


def _norm_rope(t, gain, c, s1, s2, seg):
    rs = lax.rsqrt(_seg_sum(t * t, seg) * (1.0 / HEAD_DIM) + EPS)
    return _rope((t * rs) * gain, c, s1, s2)


def _dup_half(y, half):
    lane = lax.broadcasted_iota(jnp.int32, y.shape, 1)
    rolled = pltpu.roll(y, HEAD_DIM, 1)
    keep = (lane < HEAD_DIM) if half == 0 else (lane >= HEAD_DIM)
    return jnp.where(keep, y, rolled)


def _qk_prep(proj, cb0, d, gqa, gq, gk, tabs, name):
    s_len = proj.shape[0]
    tm = 1024
    rows = tm // d
    n_units = 4 if gqa else 2 * d
    n_q = 4 if gqa else 2
    n_in = 6

    def body(*refs):
        in_refs = refs[:n_in]
        gq_ref, gk_ref, c_ref, s1_ref, s2_ref, o_ref = refs[n_in:]
        seg = _seg_matrix(HEAD_DIM)

        def rows_of(ref, r):
            return ref[...] if d == 1 else ref[pl.ds(r, rows, stride=d), :]

        def put(unit_col, y):
            o_ref[:, unit_col * LANES:(unit_col + 1) * LANES] = y.astype(BF16)

        for r in range(d):
            c, s1, s2 = rows_of(c_ref, r), rows_of(s1_ref, r), rows_of(s2_ref, r)
            for b in range(n_in):
                t = rows_of(in_refs[b], r)
                if b < n_q:
                    put((b * d + r) if not gqa else b, _norm_rope(t, gq_ref[...], c, s1, s2, seg))
                elif not gqa:
                    sec, pair = (1, b - 2) if b < 4 else (2, b - 4)
                    y = _norm_rope(t, gk_ref[...], c, s1, s2, seg) if sec == 1 else t
                    put(sec * n_units + pair * d + r, y)
                else:
                    sec = 1 if b == 4 else 2
                    y = _norm_rope(t, gk_ref[...], c, s1, s2, seg) if sec == 1 else t
                    for u in range(n_units):
                        put(sec * n_units + u, _dup_half(y, u // 2))

    in_specs = [pl.BlockSpec((tm, LANES), lambda i, b=b: (i, cb0 + b)) for b in range(n_in)]
    vec = pl.BlockSpec((1, LANES), lambda i: (0, 0))
    tab = pl.BlockSpec((tm, LANES), lambda i: (i, 0))
    width = 3 * n_units * LANES
    return pl.pallas_call(
        body, name=name, grid=(s_len // tm,), in_specs=in_specs + [vec, vec, tab, tab, tab],
        out_specs=pl.BlockSpec((rows, width), lambda i: (i, 0)),
        out_shape=jax.ShapeDtypeStruct((s_len // d, width), BF16), compiler_params=_cparams(1),
    )(*([proj] * n_in), gq, gk, *tabs)


def _qk_prep_bwd(dqkv, proj, cb0, d, gqa, gq, gk, tabs, name):
    s_len = proj.shape[0]
    tm = 1024
    rows = tm // d
    n_units = 4 if gqa else 2 * d
    n_q = 4 if gqa else 2
    n_in = 6

    def body(*refs):
        d_refs = refs[0:3]
        in_refs = refs[3:3 + n_in]
        gq_ref, gk_ref, c_ref, s1_ref, s2_ref, o_ref, dgq_ref, dgk_ref, stage = refs[3 + n_in:]
        seg = _seg_matrix(HEAD_DIM)

        @pl.when(pl.program_id(0) == 0)
        def _():
            dgq_ref[...] = jnp.zeros_like(dgq_ref)
            dgk_ref[...] = jnp.zeros_like(dgk_ref)

        def rows_of(ref, r):
            return ref[...] if d == 1 else ref[pl.ds(r, rows, stride=d), :]

        def unit(col):
            sec, u = divmod(col, n_units)
            return d_refs[sec][:, u * LANES:(u + 1) * LANES]

        def norm_bwd(dyr, t, gain, c, s1, s2, dg_ref):
            rs = lax.rsqrt(_seg_sum(t * t, seg) * (1.0 / HEAD_DIM) + EPS)
            that = t * rs
            dy = _unrope(dyr, c, s1, s2)
            dg_ref[...] += jnp.sum(dy * that, axis=0, keepdims=True)
            dthat = dy * gain
            return rs * (dthat - that * (_seg_sum(dthat * that, seg) * (1.0 / HEAD_DIM)))

        def fold(sec):
            tot = []
            for u in range(n_units):
                v = unit(sec * n_units + u)
                tot.append(v + pltpu.roll(v, HEAD_DIM, 1))
            lane = lax.broadcasted_iota(jnp.int32, tot[0].shape, 1)
            return jnp.where(lane < HEAD_DIM, tot[0] + tot[1], tot[2] + tot[3])

        for b in range(n_in):
            for r in range(d):
                c, s1, s2 = rows_of(c_ref, r), rows_of(s1_ref, r), rows_of(s2_ref, r)
                t = rows_of(in_refs[b], r)
                if b < n_q:
                    g = unit((b * d + r) if not gqa else b)
                    out = norm_bwd(g, t, gq_ref[...], c, s1, s2, dgq_ref)
                elif not gqa:
                    sec, pair = (1, b - 2) if b < 4 else (2, b - 4)
                    g = unit(sec * n_units + pair * d + r)
                    out = norm_bwd(g, t, gk_ref[...], c, s1, s2, dgk_ref) if sec == 1 else g
                else:
                    sec = 1 if b == 4 else 2
                    g = fold(sec)
                    out = norm_bwd(g, t, gk_ref[...], c, s1, s2, dgk_ref) if sec == 1 else g
                if d == 1:
                    o_ref[:, b * LANES:(b + 1) * LANES] = out.astype(BF16)
                else:
                    stage[pl.ds(r, rows, stride=d), :] = out
            if d != 1:
                o_ref[:, b * LANES:(b + 1) * LANES] = stage[...].astype(BF16)

    in_specs = [pl.BlockSpec((rows, n_units * LANES), lambda i: (i, 0))] * 3
    in_specs += [pl.BlockSpec((tm, LANES), lambda i, b=b: (i, cb0 + b)) for b in range(n_in)]
    vec = pl.BlockSpec((1, LANES), lambda i: (0, 0))
    tab = pl.BlockSpec((tm, LANES), lambda i: (i, 0))
    return pl.pallas_call(
        body, name=name, grid=(s_len // tm,), in_specs=in_specs + [vec, vec, tab, tab, tab],
        out_specs=(pl.BlockSpec((tm, n_in * LANES), lambda i: (i, 0)), vec, vec),
        out_shape=(jax.ShapeDtypeStruct((s_len, n_in * LANES), BF16), jax.ShapeDtypeStruct((1, LANES), F32),
                   jax.ShapeDtypeStruct((1, LANES), F32)),
        scratch_shapes=[pltpu.VMEM((tm, LANES), F32)], compiler_params=_cparams(1),
    )(*dqkv, *([proj] * n_in), gq, gk, *tabs)


def _head_masks(shape):
    lane = lax.broadcasted_iota(jnp.int32, shape, 1)
    return lane < HEAD_DIM, lane >= HEAD_DIM


def _band_fwd(qkv, n_units, max_dist, sinks, name):
    n_rows = qkv.shape[0]
    nb = n_rows // BLOCK
    scale = HEAD_DIM ** -0.5
    has_sink = sinks is not None
    assert not has_sink or max_dist < BLOCK

    qn, un = min(nb, BAND_Q_BLOCKS), BAND_UNITS
    ug = n_units // un

    def body(*refs):
        q_ref, kp_ref, km_ref, vp_ref, vm_ref = refs[:5]
        o_ref, lse_ref = refs[-2:]
        i = pl.program_id(1)
        qi = lax.broadcasted_iota(jnp.int32, (BLOCK, 2 * BLOCK), 0)
        kj = lax.broadcasted_iota(jnp.int32, (BLOCK, 2 * BLOCK), 1)
        dist = qi + BLOCK - kj
        band = (dist >= 0) & (dist <= max_dist)
        band_first = band & ((i > 0) | (kj >= BLOCK))
        m0, m1 = _head_masks((BLOCK, LANES))
        zero = jnp.zeros((BLOCK, LANES), BF16)
        for ub in range(un):
            cs = slice(ub * LANES, (ub + 1) * LANES)
            for qb in range(qn):
                rs = slice(qb * BLOCK, (qb + 1) * BLOCK)
                q = q_ref[rs, cs]
                if qb == 0:
                    kk = jnp.concatenate([kp_ref[:, cs], km_ref[0:BLOCK, cs]], axis=0)
                    vv = jnp.concatenate([vp_ref[:, cs], vm_ref[0:BLOCK, cs]], axis=0)
                    valid = band_first
                else:
                    kk = km_ref[(qb - 1) * BLOCK:(qb + 1) * BLOCK, cs]
                    vv = vm_ref[(qb - 1) * BLOCK:(qb + 1) * BLOCK, cs]
                    valid = band
                outs, lses = [], []
                for e, hm in enumerate((m0, m1)):
                    qe = jnp.where(hm, q, zero)
                    s = lax.dot_general(qe, kk, (((1,), (1,)), ((), ())), preferred_element_type=F32) * scale
                    s = jnp.where(valid, s, -jnp.inf)
                    if has_sink:
                        s = jnp.where(kj == 0, refs[5][ub][:, e * HEAD_DIM:e * HEAD_DIM + 1], s)
                    mx = jnp.max(s, axis=-1, keepdims=True)
                    p = jnp.exp(s - mx)
                    den = jnp.sum(p, axis=-1, keepdims=True)
                    pn = p * (1.0 / den)
                    if has_sink:
                        pn = jnp.where(kj == 0, 0.0, pn)
                    pn = pn.astype(BF16)
                    outs.append(jnp.dot(pn, vv, preferred_element_type=F32))
                    lses.append(mx + jnp.log(den))
                o_ref[rs, cs] = jnp.where(m0, outs[0], outs[1])
                lse_ref[rs, cs] = jnp.where(m0, jnp.broadcast_to(lses[0], (BLOCK, LANES)),
                                            jnp.broadcast_to(lses[1], (BLOCK, LANES)))

    def main(sec):
        return pl.BlockSpec((qn * BLOCK, un * LANES), lambda u, i: (i, sec * ug + u))

    def prev(sec):
        return pl.BlockSpec((BLOCK, un * LANES), lambda u, i: (jnp.maximum(i * qn - 1, 0), sec * ug + u))

    in_specs = [main(0), prev(1), main(1), prev(2), main(2)]
    args = [qkv] * 5
    if has_sink:
        in_specs.append(pl.BlockSpec((un, 1, LANES), lambda u, i: (u, 0, 0)))
        args.append(sinks)
    return pl.pallas_call(
        body, name=name, grid=(ug, nb // qn), in_specs=in_specs, out_specs=(main(0), main(0)),
        out_shape=(jax.ShapeDtypeStruct((n_rows, n_units * LANES), F32),) * 2, compiler_params=_cparams(2),
    )(*args)


def _band_bwd(qkv, do, lse, delta, n_units, max_dist, name):
    n_rows = qkv.shape[0]
    nb = n_rows // BLOCK
    scale = HEAD_DIM ** -0.5

    qn, un = min(nb, BAND_Q_BLOCKS), BAND_UNITS
    ug = n_units // un
    steps = nb // qn
    nt_dims = (((1,), (1,)), ((), ()))
    tn_dims = (((0,), (0,)), ((), ()))

    def body(qm_ref, qx_ref, kp_ref, km_ref, vp_ref, vm_ref, dom_ref, dox_ref, lm_ref, lx_ref, dm_ref, dx_ref,
             dq_ref, dk_ref, dv_ref):
        i = pl.program_id(1)
        m0, m1 = _head_masks((BLOCK, LANES))
        zero = jnp.zeros((BLOCK, LANES), BF16)
        qi = lax.broadcasted_iota(jnp.int32, (BLOCK, 2 * BLOCK), 0)
        kj = lax.broadcasted_iota(jnp.int32, (BLOCK, 2 * BLOCK), 1)
        dist = qi + BLOCK - kj
        band = (dist >= 0) & (dist <= max_dist)
        band_first = band & ((i > 0) | (kj >= BLOCK))
        qr = lax.broadcasted_iota(jnp.int32, (BLOCK, BLOCK), 0)
        kc = lax.broadcasted_iota(jnp.int32, (BLOCK, BLOCK), 1)
        dist_x = qr + BLOCK - kc
        band_next = (dist_x >= 0) & (dist_x <= max_dist) & (i < steps - 1)

        def pair(q, dob, lse_b, del_b, kk, vv, valid):
            dqs, dk, dv = [], None, None
            for e, hm in enumerate((m0, m1)):
                col = slice(e * HEAD_DIM, e * HEAD_DIM + 1)
                qe = jnp.where(hm, q, zero)
                doe = jnp.where(hm, dob, zero)
                s = lax.dot_general(qe, kk, nt_dims, preferred_element_type=F32) * scale
                p = jnp.where(valid, jnp.exp(s - lse_b[:, col]), 0.0)
                dp = lax.dot_general(doe, vv, nt_dims, preferred_element_type=F32)
                ds = (p * (dp - del_b[:, col]) * scale).astype(BF16)
                dqs.append(jnp.dot(ds, kk, preferred_element_type=F32))
                dk_e = lax.dot_general(ds, qe, tn_dims, preferred_element_type=F32)
                dv_e = lax.dot_general(p.astype(BF16), doe, tn_dims, preferred_element_type=F32)
                dk = dk_e if dk is None else dk + dk_e
                dv = dv_e if dv is None else dv + dv_e
            return jnp.where(m0, dqs[0], dqs[1]), dk, dv

        for ub in range(un):
            cs = slice(ub * LANES, (ub + 1) * LANES)
            dk_acc, dv_acc = [None] * qn, [None] * qn

            def add(acc, kb, part):
                acc[kb] = part if acc[kb] is None else acc[kb] + part

            for qb in range(qn):
                rs = slice(qb * BLOCK, (qb + 1) * BLOCK)
                if qb == 0:
                    kk = jnp.concatenate([kp_ref[:, cs], km_ref[0:BLOCK, cs]], axis=0)
                    vv = jnp.concatenate([vp_ref[:, cs], vm_ref[0:BLOCK, cs]], axis=0)
                    valid = band_first
                else:
                    kk = km_ref[(qb - 1) * BLOCK:(qb + 1) * BLOCK, cs]
                    vv = vm_ref[(qb - 1) * BLOCK:(qb + 1) * BLOCK, cs]
                    valid = band
                dq, dk, dv = pair(qm_ref[rs, cs], dom_ref[rs, cs], lm_ref[rs, cs], dm_ref[rs, cs], kk, vv, valid)
                dq_ref[rs, cs] = dq
                if qb > 0:
                    add(dk_acc, qb - 1, dk[0:BLOCK])
                    add(dv_acc, qb - 1, dv[0:BLOCK])
                add(dk_acc, qb, dk[BLOCK:2 * BLOCK])
                add(dv_acc, qb, dv[BLOCK:2 * BLOCK])
            last = slice((qn - 1) * BLOCK, qn * BLOCK)
            _, dk, dv = pair(qx_ref[:, cs], dox_ref[:, cs], lx_ref[:, cs], dx_ref[:, cs], km_ref[last, cs], vm_ref[last, cs],
                             band_next)
            add(dk_acc, qn - 1, dk)
            add(dv_acc, qn - 1, dv)
            for kb in range(qn):
                dk_ref[kb * BLOCK:(kb + 1) * BLOCK, cs] = dk_acc[kb]
                dv_ref[kb * BLOCK:(kb + 1) * BLOCK, cs] = dv_acc[kb]

    def main(sec):
        return pl.BlockSpec((qn * BLOCK, un * LANES), lambda u, i: (i, sec * ug + u))

    def prev(sec):
        return pl.BlockSpec((BLOCK, un * LANES), lambda u, i: (jnp.maximum(i * qn - 1, 0), sec * ug + u))

    def nxt(sec):
        return pl.BlockSpec((BLOCK, un * LANES), lambda u, i: (jnp.minimum((i + 1) * qn, nb - 1), sec * ug + u))

    in_specs = [main(0), nxt(0), prev(1), main(1), prev(2), main(2),
                main(0), nxt(0), main(0), nxt(0), main(0), nxt(0)]
    args = [qkv] * 6 + [do, do, lse, lse, delta, delta]
    shp = jax.ShapeDtypeStruct((n_rows, n_units * LANES), F32)
    return pl.pallas_call(
        body, name=name, grid=(ug, steps), in_specs=in_specs, out_specs=(main(0), main(0), main(0)),
        out_shape=(shp, shp, shp), compiler_params=_cparams(2),
    )(*args)


def _merge_groups(os_, lses, dils, name):
    s_len = os_[0].shape[0] * dils[0]
    tm = 512

    def body(*refs):
        o_refs, l_refs = refs[0:3], refs[3:6]
        o_ref, lse_ref = refs[6:8]
        so, sl = refs[8:11], refs[11:14]
        for pair in range(2):
            for g, d in enumerate(dils):
                rows = tm // d
                for r in range(d):
                    col = slice((pair * d + r) * LANES, (pair * d + r + 1) * LANES)
                    if d == 1:
                        so[g][...] = o_refs[g][:, col]
                        sl[g][...] = l_refs[g][:, col]
                    else:
                        so[g][pl.ds(r, rows, stride=d), :] = o_refs[g][:, col]
                        sl[g][pl.ds(r, rows, stride=d), :] = l_refs[g][:, col]
            l0, l1, l2 = sl[0][...], sl[1][...], sl[2][...]
            mx = jnp.maximum(jnp.maximum(l0, l1), l2)
            e0, e1, e2 = jnp.exp(l0 - mx), jnp.exp(l1 - mx), jnp.exp(l2 - mx)
            den = e0 + e1 + e2
            inv = 1.0 / den
            o_ref[:, pair * LANES:(pair + 1) * LANES] = (so[0][...] * (e0 * inv) + so[1][...] * (e1 * inv)
                                                         + so[2][...] * (e2 * inv))
            lse_ref[:, pair * LANES:(pair + 1) * LANES] = mx + jnp.log(den)

    in_specs = [pl.BlockSpec((tm // d, 2 * d * LANES), lambda i: (i, 0)) for d in dils] * 2
    out = pl.BlockSpec((tm, 2 * LANES), lambda i: (i, 0))
    shp = jax.ShapeDtypeStruct((s_len, 2 * LANES), F32)
    return pl.pallas_call(
        body, name=name, grid=(s_len // tm,), in_specs=in_specs, out_specs=(out, out), out_shape=(shp, shp),
        scratch_shapes=[pltpu.VMEM((tm, LANES), F32)] * 6, compiler_params=_cparams(1),
    )(*os_, *lses)


def _bwd_prep(do, o, lse, dils, sinks, name):
    s_len, width = do.shape
    n_pairs = width // LANES
    tm = 512
    has_sink = sinks is not None
    n_g = len(dils)

    def body(*refs):
        do_ref, o_ref, lse_ref = refs[:3]
        pos = 3
        if has_sink:
            sink_ref = refs[pos]
            pos += 1
        outs = refs[pos:pos + 3 * n_g]
        pos += 3 * n_g
        if has_sink:
            dsink_ref = refs[pos]
            pos += 1
        s_do, s_l, s_d = refs[pos:pos + 3]
        seg = _seg_matrix(HEAD_DIM)

        if has_sink:
            @pl.when(pl.program_id(0) == 0)
            def _():
                dsink_ref[...] = jnp.zeros_like(dsink_ref)

        for pair in range(n_pairs):
            col = slice(pair * LANES, (pair + 1) * LANES)
            dov = do_ref[:, col]
            lv = lse_ref[:, col]
            delta = _seg_sum(dov * o_ref[:, col], seg)
            if has_sink:
                dsink_ref[pair] += -jnp.sum(jnp.exp(sink_ref[pair] - lv) * delta, axis=0, keepdims=True)
            s_do[...] = dov
            s_l[...] = lv
            s_d[...] = delta
            for g, d in enumerate(dils):
                rows = tm // d
                for r in range(d):
                    oc = slice((pair * d + r) * LANES, (pair * d + r + 1) * LANES)
                    if d == 1:
                        a, b, c = s_do[...], s_l[...], s_d[...]
                    else:
                        a = s_do[pl.ds(r, rows, stride=d), :]
                        b = s_l[pl.ds(r, rows, stride=d), :]
                        c = s_d[pl.ds(r, rows, stride=d), :]
                    outs[3 * g][:, oc] = a.astype(BF16)
                    outs[3 * g + 1][:, oc] = b
                    outs[3 * g + 2][:, oc] = c

    row = pl.BlockSpec((tm, width), lambda i: (i, 0))
    in_specs = [row, row, row]
    args = [do, o, lse]
    if has_sink:
        in_specs.append(pl.BlockSpec((n_pairs, 1, LANES), lambda i: (0, 0, 0)))
        args.append(sinks)
    out_specs, out_shape = [], []
    for d in dils:
        for dt in (BF16, F32, F32):
            out_specs.append(pl.BlockSpec((tm // d, n_pairs * d * LANES), lambda i: (i, 0)))
            out_shape.append(jax.ShapeDtypeStruct((s_len // d, n_pairs * d * LANES), dt))
    if has_sink:
        out_specs.append(pl.BlockSpec((n_pairs, 1, LANES), lambda i: (0, 0, 0)))
        out_shape.append(jax.ShapeDtypeStruct((n_pairs, 1, LANES), F32))
    return pl.pallas_call(
        body, name=name, grid=(s_len // tm,), in_specs=in_specs, out_specs=tuple(out_specs),
        out_shape=tuple(out_shape), scratch_shapes=[pltpu.VMEM((tm, LANES), F32)] * 3, compiler_params=_cparams(1),
    )(*args)


def _mem_kv(mem, mem_gain, w_kv, k_gain, name):
    m_len = mem.shape[0]
    kw = M_HEADS * M_HEAD_DIM

    def body(mem_ref, mg_ref, w_ref, kg_ref, k_ref, v_ref):
        mv = mem_ref[...]
        r = lax.rsqrt(jnp.mean(mv * mv, axis=-1, keepdims=True) + EPS)
        mn = ((mv * r) * mg_ref[...]).astype(BF16)
        kv = jnp.dot(mn, w_ref[...], preferred_element_type=F32)
        for h in range(M_HEADS):
            col = slice(h * M_HEAD_DIM, (h + 1) * M_HEAD_DIM)
            t = kv[:, col]
            rk = lax.rsqrt(jnp.mean(t * t, axis=-1, keepdims=True) + EPS)
            k_ref[:, col] = ((t * rk) * kg_ref[...]).astype(BF16)
        v_ref[...] = kv[:, kw:].astype(BF16)

    shp = jax.ShapeDtypeStruct((m_len, kw), BF16)
    return pl.pallas_call(body, name=name, out_shape=(shp, shp),
                          compiler_params=pltpu.CompilerParams(vmem_limit_bytes=VMEM_LIMIT_BYTES))(mem, mem_gain, w_kv, k_gain)


def _mem_kv_bwd(mem, mem_gain, w_kv, k_gain, dk, dv, name):
    m_len, d = mem.shape
    kw = M_HEADS * M_HEAD_DIM

    def body(mem_ref, mg_ref, w_ref, kg_ref, dk_ref, dv_ref, dw_ref, dmg_ref, dkg_ref, dkv_ref):
        mv = mem_ref[...]
        r = lax.rsqrt(jnp.mean(mv * mv, axis=-1, keepdims=True) + EPS)
        mhat = mv * r
        mn = (mhat * mg_ref[...]).astype(BF16)
        kv = jnp.dot(mn, w_ref[...], preferred_element_type=F32)
        dkg = jnp.zeros((1, M_HEAD_DIM), F32)
        for h in range(M_HEADS):
            col = slice(h * M_HEAD_DIM, (h + 1) * M_HEAD_DIM)
            t = kv[:, col]
            rk = lax.rsqrt(jnp.mean(t * t, axis=-1, keepdims=True) + EPS)
            that = t * rk
            dy = dk_ref[:, col]
            dkg = dkg + jnp.sum(dy * that, axis=0, keepdims=True)
            dthat = dy * kg_ref[...]
            dkv_ref[:, col] = (rk * (dthat - that * jnp.mean(dthat * that, axis=-1, keepdims=True))).astype(BF16)
        dkv_ref[:, kw:] = dv_ref[...].astype(BF16)
        dkg_ref[...] = dkg
        dkv = dkv_ref[...]
        dw_ref[...] = lax.dot_general(mn, dkv, (((0,), (0,)), ((), ())), preferred_element_type=F32).astype(GRAD_DTYPE)
        dmn = lax.dot_general(dkv, w_ref[...], (((1,), (1,)), ((), ())), preferred_element_type=F32)
        dmg_ref[...] = jnp.sum(dmn * mhat, axis=0, keepdims=True)

    return pl.pallas_call(
        body, name=name,
        out_shape=(jax.ShapeDtypeStruct((d, 2 * kw), GRAD_DTYPE), jax.ShapeDtypeStruct((1, d), F32),
                   jax.ShapeDtypeStruct((1, M_HEAD_DIM), F32)),
        scratch_shapes=[pltpu.VMEM((m_len, 2 * kw), BF16)],
        compiler_params=pltpu.CompilerParams(vmem_limit_bytes=VMEM_LIMIT_BYTES),
    )(mem, mem_gain, w_kv, k_gain, dk, dv)


def _mem_attn_fwd(proj, cidx, mk, mv, q_gain, name):
    s_len = proj.shape[0]
    kw = M_HEADS * M_HEAD_DIM
    tm = 512
    scale = M_HEAD_DIM ** -0.5

    def body(q_ref, k_ref, v_ref, g_ref, o_ref):
        for h in range(M_HEADS):
            col = slice(h * M_HEAD_DIM, (h + 1) * M_HEAD_DIM)
            t = q_ref[:, col]
            rs = lax.rsqrt(jnp.mean(t * t, axis=-1, keepdims=True) + EPS)
            qn = ((t * rs) * g_ref[...]).astype(BF16)
            s = lax.dot_general(qn, k_ref[:, col], (((1,), (1,)), ((), ())), preferred_element_type=F32) * scale
            mx = jnp.max(s, axis=-1, keepdims=True)
            p = jnp.exp(s - mx)
            pn = (p * (1.0 / jnp.sum(p, axis=-1, keepdims=True))).astype(BF16)
            o_ref[:, col] = jnp.dot(pn, v_ref[:, col], preferred_element_type=F32).astype(BF16)

    whole = pl.BlockSpec((MEM_LEN, kw), lambda i: (0, 0))
    return pl.pallas_call(
        body, name=name, grid=(s_len // tm,),
        in_specs=[pl.BlockSpec((tm, kw), lambda i: (i, cidx)), whole, whole, pl.BlockSpec((1, M_HEAD_DIM), lambda i: (0, 0))],
        out_specs=pl.BlockSpec((tm, kw), lambda i: (i, 0)),
        out_shape=jax.ShapeDtypeStruct((s_len, kw), BF16), compiler_params=_cparams(1),
    )(proj, mk, mv, q_gain)


def _mem_attn_bwd(proj, cidx, mk, mv, q_gain, do, name):
    s_len = proj.shape[0]
    kw = M_HEADS * M_HEAD_DIM
    tm = 512
    scale = M_HEAD_DIM ** -0.5

    def body(q_ref, k_ref, v_ref, g_ref, do_ref, dq_ref, dk_ref, dv_ref, dg_ref):
        @pl.when(pl.program_id(0) == 0)
        def _():
            dk_ref[...] = jnp.zeros_like(dk_ref)
            dv_ref[...] = jnp.zeros_like(dv_ref)
            dg_ref[...] = jnp.zeros_like(dg_ref)

        for h in range(M_HEADS):
            col = slice(h * M_HEAD_DIM, (h + 1) * M_HEAD_DIM)
            t = q_ref[:, col]
            rs = lax.rsqrt(jnp.mean(t * t, axis=-1, keepdims=True) + EPS)
            that = t * rs
            qn = (that * g_ref[...]).astype(BF16)
            kh, vh = k_ref[:, col], v_ref[:, col]
            dob = do_ref[:, col].astype(BF16)
            s = lax.dot_general(qn, kh, (((1,), (1,)), ((), ())), preferred_element_type=F32) * scale
            mx = jnp.max(s, axis=-1, keepdims=True)
            p = jnp.exp(s - mx)
            p = p * (1.0 / jnp.sum(p, axis=-1, keepdims=True))
            dp = lax.dot_general(dob, vh, (((1,), (1,)), ((), ())), preferred_element_type=F32)
            ds = (p * (dp - jnp.sum(p * dp, axis=-1, keepdims=True)) * scale).astype(BF16)
            dqn = jnp.dot(ds, kh, preferred_element_type=F32)
            dk_ref[:, col] += lax.dot_general(ds, qn, (((0,), (0,)), ((), ())), preferred_element_type=F32)
            dv_ref[:, col] += lax.dot_general(p.astype(BF16), dob, (((0,), (0,)), ((), ())), preferred_element_type=F32)
            dg_ref[...] += jnp.sum(dqn * that, axis=0, keepdims=True)
            dthat = dqn * g_ref[...]
            dq_ref[:, col] = (rs * (dthat - that * jnp.mean(dthat * that, axis=-1, keepdims=True))).astype(BF16)

    whole = pl.BlockSpec((MEM_LEN, kw), lambda i: (0, 0))
    vec = pl.BlockSpec((1, M_HEAD_DIM), lambda i: (0, 0))
    row = pl.BlockSpec((tm, kw), lambda i: (i, 0))
    return pl.pallas_call(
        body, name=name, grid=(s_len // tm,),
        in_specs=[pl.BlockSpec((tm, kw), lambda i: (i, cidx)), whole, whole, vec, row],
        out_specs=(row, whole, whole, vec),
        out_shape=(jax.ShapeDtypeStruct((s_len, kw), BF16), jax.ShapeDtypeStruct((MEM_LEN, kw), F32),
                   jax.ShapeDtypeStruct((MEM_LEN, kw), F32), jax.ShapeDtypeStruct((1, M_HEAD_DIM), F32)),
        compiler_params=_cparams(1),
    )(proj, mk, mv, q_gain, do)


def _project_merge(outs, w_outs, gates, w_out, x, name):
    s_len = gates.shape[0]
    d = w_outs[0].shape[1]
    tm = 512

    def body(oa_ref, ob_ref, om_ref, wa_ref, wb_ref, wm_ref, g_ref, wo_ref, x_ref,
             pa_ref, pb_ref, pm_ref, merged_ref, x1_ref):
        merged = None
        for k, (o_ref, w_ref, p_ref) in enumerate(((oa_ref, wa_ref, pa_ref), (ob_ref, wb_ref, pb_ref), (om_ref, wm_ref, pm_ref))):
            p = jnp.dot(o_ref[...].astype(BF16), w_ref[...], preferred_element_type=F32).astype(BF16)
            p_ref[...] = p
            t = g_ref[:, k * d:(k + 1) * d].astype(F32) * p.astype(F32)
            merged = t if merged is None else merged + t
        merged = merged.astype(BF16)
        merged_ref[...] = merged
        x1_ref[...] = x_ref[...] + jnp.dot(merged, wo_ref[...], preferred_element_type=F32)

    row = pl.BlockSpec((tm, d), lambda i: (i, 0))
    shp = jax.ShapeDtypeStruct((s_len, d), BF16)
    in_specs = [pl.BlockSpec((tm, o.shape[1]), lambda i: (i, 0)) for o in outs]
    in_specs += [_resident(w.shape, lambda i: (0, 0)) for w in w_outs]
    in_specs += [pl.BlockSpec((tm, 3 * d), lambda i: (i, 0)), _resident(w_out.shape, lambda i: (0, 0)), row]
    return pl.pallas_call(
        body, name=name, grid=(s_len // tm,), in_specs=in_specs, out_specs=(row, row, row, row, row),
        out_shape=(shp, shp, shp, shp, jax.ShapeDtypeStruct((s_len, d), F32)), compiler_params=_cparams(1),
    )(*outs, *w_outs, gates, w_out, x)


def _project_merge_bwd(dx1, w_out, gates, pa, pb, pm, name):
    s_len, d = pa.shape
    tm = 512

    def body(dx_ref, w_ref, g_ref, a_ref, b_ref, m_ref, da_ref, db_ref, dmm_ref, dg_ref, dbg_ref):
        @pl.when(pl.program_id(0) == 0)
        def _():
            dbg_ref[...] = jnp.zeros_like(dbg_ref)
        dm = lax.dot_general(dx_ref[...], w_ref[...], (((1,), (1,)), ((), ())), preferred_element_type=F32)
        for k, (p_ref, dp_ref) in enumerate(((a_ref, da_ref), (b_ref, db_ref), (m_ref, dmm_ref))):
            col = slice(k * d, (k + 1) * d)
            g = g_ref[:, col].astype(F32)
            dp_ref[...] = (dm * g).astype(BF16)
            dpre = (dm * p_ref[...].astype(F32)) * (g * (1.0 - g))
            dbg_ref[:, col] += jnp.sum(dpre, axis=0, keepdims=True)
            dg_ref[:, col] = dpre.astype(BF16)

    row = pl.BlockSpec((tm, d), lambda i: (i, 0))
    wide = pl.BlockSpec((tm, 3 * d), lambda i: (i, 0))
    shp = jax.ShapeDtypeStruct((s_len, d), BF16)
    return pl.pallas_call(
        body, name=name, grid=(s_len // tm,), in_specs=[row, _resident(w_out.shape, lambda i: (0, 0)), wide, row, row, row],
        out_specs=(row, row, row, wide, pl.BlockSpec((1, 3 * d), lambda i: (0, 0))),
        out_shape=(shp, shp, shp, jax.ShapeDtypeStruct((s_len, 3 * d), BF16), jax.ShapeDtypeStruct((1, 3 * d), F32)),
        compiler_params=_cparams(1),
    )(dx1, w_out, gates, pa, pb, pm)


CONV_CHUNK = 1024


def _pick_row(tile, j):
    row = lax.broadcasted_iota(jnp.int32, tile.shape, 0)
    return jnp.sum(jnp.where(row == j, tile, jnp.zeros_like(tile)), axis=0, keepdims=True)


def _rows_before(ref, start, k):
    cur = ref[pl.ds(start, CONV_CHUNK), :].astype(F32)
    prev = ref[pl.ds(pl.multiple_of(jnp.maximum(start - 16, 0), 16), 16), :].astype(F32)
    prev = jnp.where(start > 0, prev, jnp.zeros_like(prev))
    rolled = pltpu.roll(cur, k, 0)
    row = lax.broadcasted_iota(jnp.int32, cur.shape, 0)
    for j in range(k):
        rolled = jnp.where(row == j, _pick_row(prev, 16 - k + j), rolled)
    return rolled


def _rows_after(ref, start, k):
    cur = ref[pl.ds(start, CONV_CHUNK), :]
    nxt = ref[pl.ds(pl.multiple_of(start + CONV_CHUNK, 8), 8), :]
    rolled = pltpu.roll(cur, CONV_CHUNK - k, 0)
    row = lax.broadcasted_iota(jnp.int32, cur.shape, 0)
    for j in range(k):
        rolled = jnp.where(row == CONV_CHUNK - k + j, _pick_row(nxt, j), rolled)
    return rolled


def _conv_pre(u_ref, w_ref, b_ref, start):
    u2 = _rows_before(u_ref, start, 2)
    u1 = _rows_before(u_ref, start, 1)
    u0 = u_ref[pl.ds(start, CONV_CHUNK), :].astype(F32)
    c = ((b_ref[...] + w_ref[0:1, :] * u2) + w_ref[1:2, :] * u1) + w_ref[2:3, :] * u0
    return c, (u2, u1, u0)


def _norm_up_conv_glu(x, gain, w_up, conv_w, conv_b, name):
    s_len, d = x.shape
    tm, tn = 512, 2 * LANES
    nblk = D_FF // tn

    def body(x_ref, g_ref, w_ref, cw_ref, cb_ref, ht_ref, r_ref, u_ref, act_ref, halo):
        @pl.when(pl.program_id(0) == 0)
        def _():
            halo[...] = jnp.zeros_like(halo)
        xv = x_ref[...]
        r = lax.rsqrt(jnp.mean(xv * xv, axis=-1, keepdims=True) + EPS)
        hv = ((xv * r) * g_ref[...]).astype(BF16)
        ht_ref[...] = hv.T
        r_ref[...] = r
        row = lax.broadcasted_iota(jnp.int32, (tm, tn), 0)
        for j in range(nblk):
            conv = []
            for half in range(2):
                cb = half * nblk + j
                cols = slice(cb * tn, (cb + 1) * tn)
                ub = jnp.dot(hv, w_ref[:, cols], preferred_element_type=F32).astype(BF16)
                u_ref[:, cols] = ub
                u0 = ub.astype(F32)
                prev = halo[cb]
                u1 = jnp.where(row == 0, _pick_row(prev, 7), pltpu.roll(u0, 1, 0))
                u2 = pltpu.roll(u0, 2, 0)
                u2 = jnp.where(row == 0, _pick_row(prev, 6), jnp.where(row == 1, _pick_row(prev, 7), u2))
                halo[cb] = u0[tm - 8:tm, :]
                conv.append(((cb_ref[:, cols] + cw_ref[0:1, cols] * u2) + cw_ref[1:2, cols] * u1)
                            + cw_ref[2:3, cols] * u0)
            act_ref[:, j * tn:(j + 1) * tn] = ((conv[0] * _sigmoid(conv[0])) * conv[1]).astype(BF16)

    return pl.pallas_call(
        body, name=name, grid=(s_len // tm,),
        in_specs=[pl.BlockSpec((tm, d), lambda i: (i, 0)), _resident((1, d), lambda i: (0, 0)),
                  _resident((d, 2 * D_FF), lambda i: (0, 0)),
                  _resident((3, 2 * D_FF), lambda i: (0, 0)), _resident((1, 2 * D_FF), lambda i: (0, 0))],
        out_specs=(pl.BlockSpec((d, tm), lambda i: (0, i)), pl.BlockSpec((tm, 1), lambda i: (i, 0)),
                   pl.BlockSpec((tm, 2 * D_FF), lambda i: (i, 0)), pl.BlockSpec((tm, D_FF), lambda i: (i, 0))),
        out_shape=(jax.ShapeDtypeStruct((d, s_len), BF16), jax.ShapeDtypeStruct((s_len, 1), F32),
                   jax.ShapeDtypeStruct((s_len, 2 * D_FF), BF16), jax.ShapeDtypeStruct((s_len, D_FF), BF16)),
        scratch_shapes=[pltpu.VMEM((2 * nblk, 8, tn), F32)], compiler_params=_cparams(1),
    )(x, gain, w_up, conv_w, conv_b)


def _conv_glu_bwd(dact, u, conv_w, conv_b, name):
    s_len = u.shape[0]
    nblk = D_FF // LANES
    n_chunks = s_len // CONV_CHUNK

    def body(da_ref, ua_ref, ug_ref, wa_ref, wg_ref, ba_ref, bg_ref,
             dua_ref, dug_ref, dwa_ref, dwg_ref, dba_ref, dbg_ref, sa, sg):
        sa[pl.ds(s_len, 8), :] = jnp.zeros((8, LANES), F32)
        sg[pl.ds(s_len, 8), :] = jnp.zeros((8, LANES), F32)
        zero = jnp.zeros((1, LANES), F32)

        def chunk1(ci, carry):
            start = pl.multiple_of(ci * CONV_CHUNK, CONV_CHUNK)
            ca, ua = _conv_pre(ua_ref, wa_ref, ba_ref, start)
            cg, ug = _conv_pre(ug_ref, wg_ref, bg_ref, start)
            dact_v = da_ref[pl.ds(start, CONV_CHUNK), :].astype(F32)
            sig = _sigmoid(ca)
            dcg = dact_v * (ca * sig)
            dca = (dact_v * cg) * (sig * (1.0 + ca * (1.0 - sig)))
            sa[pl.ds(start, CONV_CHUNK), :] = dca
            sg[pl.ds(start, CONV_CHUNK), :] = dcg
            out = [carry[0] + jnp.sum(dca, axis=0, keepdims=True), carry[1] + jnp.sum(dcg, axis=0, keepdims=True)]
            for j in range(3):
                out.append(carry[2 + j] + jnp.sum(dca * ua[j], axis=0, keepdims=True))
            for j in range(3):
                out.append(carry[5 + j] + jnp.sum(dcg * ug[j], axis=0, keepdims=True))
            return tuple(out)

        acc = lax.fori_loop(0, n_chunks, chunk1, (zero,) * 8)
        dba_ref[...] = acc[0]
        dbg_ref[...] = acc[1]
        for j in range(3):
            dwa_ref[j:j + 1, :] = acc[2 + j]
            dwg_ref[j:j + 1, :] = acc[5 + j]

        def chunk2(ci, carry):
            start = pl.multiple_of(ci * CONV_CHUNK, CONV_CHUNK)
            for s_ref, w_ref, o_ref in ((sa, wa_ref, dua_ref), (sg, wg_ref, dug_ref)):
                d0 = s_ref[pl.ds(start, CONV_CHUNK), :]
                d1 = _rows_after(s_ref, start, 1)
                d2 = _rows_after(s_ref, start, 2)
                o_ref[pl.ds(start, CONV_CHUNK), :] = (w_ref[2:3, :] * d0 + w_ref[1:2, :] * d1
                                                      + w_ref[0:1, :] * d2).astype(BF16)
            return carry
        lax.fori_loop(0, n_chunks, chunk2, 0)

    def col(rows, off):
        return pl.BlockSpec((rows, LANES), lambda j: (0, off + j))

    big = jax.ShapeDtypeStruct((s_len, D_FF), BF16)
    return pl.pallas_call(
        body, name=name, grid=(nblk,),
        in_specs=[col(s_len, 0), col(s_len, 0), col(s_len, nblk), col(3, 0), col(3, nblk), col(1, 0), col(1, nblk)],
        out_specs=(col(s_len, 0), col(s_len, 0), col(3, 0), col(3, 0), col(1, 0), col(1, 0)),
        out_shape=(big, big, jax.ShapeDtypeStruct((3, D_FF), F32), jax.ShapeDtypeStruct((3, D_FF), F32),
                   jax.ShapeDtypeStruct((1, D_FF), F32), jax.ShapeDtypeStruct((1, D_FF), F32)),
        scratch_shapes=[pltpu.VMEM((s_len + 8, LANES), F32)] * 2, compiler_params=_cparams(1),
    )(dact, u, u, conv_w, conv_w, conv_b, conv_b)


def _rope_tables(positions):
    half = ROPE_DIMS // 2
    freqs = jnp.exp(jnp.arange(half, dtype=F32) * (-2.0 * math.log(ROPE_THETA) / ROPE_DIMS))
    ang = positions.reshape(-1).astype(F32)[:, None] * freqs
    cos, sin = jnp.cos(ang), jnp.sin(ang)
    n = ang.shape[0]
    zeros = lambda w: jnp.zeros((n, w), F32)
    c = jnp.concatenate([cos, cos, jnp.ones((n, HEAD_DIM - ROPE_DIMS), F32)], axis=1)
    s1 = jnp.concatenate([-sin, zeros(HEAD_DIM - half)], axis=1)
    s2 = jnp.concatenate([zeros(half), sin, zeros(HEAD_DIM - ROPE_DIMS)], axis=1)
    return tuple(jnp.tile(t, (1, 2)) for t in (c, s1, s2))


def _two(v):
    return jnp.tile(v.reshape(1, HEAD_DIM), (1, 2))


def _fold_heads(g):
    return g[0, :HEAD_DIM] + g[0, HEAD_DIM:]


MIX_WEIGHTS = ('w_gate', 'w_mem_kv', 'w_o_a', 'w_o_b', 'w_o_m', 'w_out')
FFN_WEIGHTS = ('w_up', 'conv_w', 'w_down')


def _device_step(x, mem, positions, target, w, hooks=None):
    tabs = _rope_tables(positions)
    dils = tuple(d for _, d in A_GROUPS)
    grads = {}
    w = dict(w)

    h, h_t, r1 = _rms_fwd(x, w['attn_norm'], "rms1")
    if hooks is not None:
        w.update(hooks.weights('in', [h, *tabs]))
    proj = _mm_rows([(h, w['w_in'], 0)], "mm_in")

    qkv_a, o_g, lse_g = [], [], []
    for gi, (window, d) in enumerate(A_GROUPS):
        gq, gk = _two(w['a_q_norm'][gi]), _two(w['a_k_norm'][gi])
        qkv = _qk_prep(proj, 6 * gi, d, False, gq, gk, tabs, f"qk_prep_a{gi}")
        o, lse = _band_fwd(qkv, 2 * d, window // d, None, f"band_fwd_a{gi}")
        qkv_a.append(qkv)
        o_g.append(o)
        lse_g.append(lse)
    o_a, lse_a = _merge_groups(o_g, lse_g, dils, "merge_a")
    if hooks is not None:
        w.update(hooks.weights('mix', [o_a]))

    gbq, gbk = _two(w['b_q_norm']), _two(w['b_k_norm'])
    sinks = jnp.repeat(w['b_sinks'].reshape(4, 2), HEAD_DIM, axis=1).reshape(4, 1, LANES)
    qkv_b = _qk_prep(proj, 18, 1, True, gbq, gbk, tabs, "qk_prep_b")
    o_b, lse_b = _band_fwd(qkv_b, 4, B_WINDOW - 1, sinks, "band_fwd_b")

    gates = _mm_rows([(h, w['w_gate'], 0)], "mm_gate", bias=w['b_gate'], sigmoid=True, out_dtypes=(BF16,))
    mk, mv = _mem_kv(mem, w['mem_norm'], w['w_mem_kv'], w['m_k_norm'], "mem_kv")
    o_m = _mem_attn_fwd(proj, 6, mk, mv, w['m_q_norm'], "mem_attn")

    pa, pb, pm, merged, x1 = _project_merge((o_a, o_b, o_m), (w['w_o_a'], w['w_o_b'], w['w_o_m']), gates, w['w_out'], x,
                                            "project_merge")

    if hooks is not None:
        w.update(hooks.weights('ffn', [x1]))
    h2_t, r2, u, act = _norm_up_conv_glu(x1, w['ffn_norm'], w['w_up'], w['conv_w'], w['conv_b'], "norm_up_conv_glu")
    dy, dy_b, dact, loss = _mm_rows([(act, w['w_down'], 0)], "mm_down", res=x1, loss_target=target)

    grads['w_down'] = _mm_tn(act, dy_b, "mm_dw_down")
    du_a, du_g, dcw_a, dcw_g, dcb_a, dcb_g = _conv_glu_bwd(dact, u, w['conv_w'], w['conv_b'], "conv_glu_bwd")
    grads['conv_w'] = jnp.concatenate([dcw_a, dcw_g], axis=1)
    grads['conv_b'] = jnp.concatenate([dcb_a, dcb_g], axis=1)
    grads['w_up'] = _mm_cols(h2_t, [du_a, du_g], "mm_dw_up")
    ffn_gain = w['ffn_norm']
    if hooks is not None:
        ffn_gain = ffn_gain + hooks.grads('ffn', grads)[0:1, 0:1]
    dx1, dx1_b, grads['ffn_norm'] = _mm_rows([(du_a, w['w_up'], 0), (du_g, w['w_up'], 1)], "mm_d_h2", nt=True,
                                             rms_bwd=(x1, r2, ffn_gain, dy), out_dtypes=(F32, BF16))

    grads['w_out'] = _mm_tn(merged, dx1_b, "mm_dw_out")
    dpa, dpb, dpm, dgpre, grads['b_gate'] = _project_merge_bwd(dx1_b, w['w_out'], gates, pa, pb, pm,
                                                               "project_merge_bwd")
    do_a, do_b, do_m = _mm_rows_each([(dpa, w['w_o_a']), (dpb, w['w_o_b']), (dpm, w['w_o_m'])], "mm_d_o")
    grads['w_o_a'], grads['w_o_b'], grads['w_o_m'] = _mm_tn_each([(o_a, dpa), (o_b, dpb), (o_m, dpm)], "mm_dw_o")
    grads['w_gate'] = _mm_cols(h_t, [dgpre], "mm_dw_gate")
    dq_m, dmk, dmv, grads['m_q_norm'] = _mem_attn_bwd(proj, 6, mk, mv, w['m_q_norm'], do_m, "mem_attn_bwd")
    grads['w_mem_kv'], grads['mem_norm'], grads['m_k_norm'] = _mem_kv_bwd(
        mem, w['mem_norm'], w['w_mem_kv'], w['m_k_norm'], dmk, dmv, "mem_kv_bwd")
    a_gain = w['a_q_norm']
    if hooks is not None:
        a_gain = a_gain + hooks.grads('mix', grads)[0:1, 0:1]

    prep = _bwd_prep(do_a, o_a, lse_a, dils, None, "bwd_prep_a")
    dproj, dgq_a, dgk_a = [], [], []
    for gi, (window, d) in enumerate(A_GROUPS):
        gq, gk = _two(a_gain[gi]), _two(w['a_k_norm'][gi])
        dqkv = _band_bwd(qkv_a[gi], prep[3 * gi], prep[3 * gi + 1], prep[3 * gi + 2], 2 * d, window // d,
                         f"band_bwd_a{gi}")
        dp, dgq, dgk = _qk_prep_bwd(dqkv, proj, 6 * gi, d, False, gq, gk, tabs, f"qk_prep_bwd_a{gi}")
        dproj.append(dp)
        dgq_a.append(_fold_heads(dgq))
        dgk_a.append(_fold_heads(dgk))
    grads['a_q_norm'] = jnp.stack(dgq_a)
    grads['a_k_norm'] = jnp.stack(dgk_a)

    do_bu, lse_bu, delta_bu, dsink = _bwd_prep(do_b, o_b, lse_b, (1,), sinks, "bwd_prep_b")
    dqkv = _band_bwd(qkv_b, do_bu, lse_bu, delta_bu, 4, B_WINDOW - 1, "band_bwd_b")
    dp_b, dgq, dgk = _qk_prep_bwd(dqkv, proj, 18, 1, True, gbq, gbk, tabs, "qk_prep_bwd_b")
    dproj.append(dp_b)
    grads['b_q_norm'] = _fold_heads(dgq)
    grads['b_k_norm'] = _fold_heads(dgk)
    grads['b_sinks'] = jnp.stack([dsink[:, 0, 0], dsink[:, 0, HEAD_DIM]], axis=1).reshape(8)

    dproj.append(dq_m)

    cols = (0, 1, 2, 3, 6)
    grads['w_in'] = _mm_rows_cat(h_t, dproj, "mm_dw_in")
    attn_gain = w['attn_norm']
    if hooks is not None:
        attn_gain = attn_gain + hooks.grads('in', grads)[0:1, 0:1]
    grad_x, grads['attn_norm'] = _mm_rows(
        [(dp, w['w_in'], c) for dp, c in zip(dproj, cols)] + [(dgpre, w['w_gate'], 0)], "mm_d_h", nt=True,
        rms_bwd=(x, r1, attn_gain, dx1))
    return loss, grad_x, grads


def _coords():
    return lax.axis_index("x"), lax.axis_index("y"), lax.axis_index("c")


def _slot(p):
    return 4 * p[0] + 2 * p[1] + p[2]


ALL_PEERS = tuple(range(1, N_DEV))
CHIP_PEERS = (1, 4, 2, 6)
OTHER_CHIPS = (4, 2, 6)


def _peers(me, masks=ALL_PEERS):
    x, y, c = me
    return [(1 - x if mask & 4 else x, 1 - y if mask & 2 else y, 1 - c if mask & 1 else c) for mask in masks]


HBM_SPEC = pl.BlockSpec(memory_space=pltpu.HBM)


SEM_SPEC = pl.BlockSpec(memory_space=pltpu.SEMAPHORE)
SIDE_EFFECT = pltpu.SideEffectType.DATAFLOW_SIDE_EFFECTING


def _exchange_start(blocks, name, gather=False, masks=ALL_PEERS, after=None):
    n = len(blocks)
    n_peers = len(masks)
    n_in = 2 * n + (0 if after is None else 1)

    def body(*refs):
        ins, lands = refs[:n], refs[n:2 * n]
        send_sems, recv_sems = refs[n_in], refs[n_in + 1]
        token = refs[-1]
        me = _coords()
        peers = _peers(me, masks)
        for a in range(n):
            for k in range(n_peers):
                pltpu.make_async_remote_copy(
                    src_ref=ins[a] if gather else ins[a].at[_slot(peers[k])], dst_ref=lands[a].at[_slot(me)],
                    send_sem=send_sems.at[a * n_peers + k], recv_sem=recv_sems.at[a * n_peers + k],
                    device_id=peers[k], device_id_type=MESH).start()
        token[...] = jnp.zeros_like(token)

    land_shapes = [((N_DEV,) + b.shape) if gather else b.shape for b in blocks]
    hbm_in = [pltpu.HBM(b.shape, b.dtype) for b in blocks]
    hbm_land = [pltpu.HBM(s, b.dtype) for s, b in zip(land_shapes, blocks)]
    sems = pltpu.SemaphoreType.DMA((n * n_peers,))
    ins = [pltpu.with_memory_space_constraint(b, pltpu.HBM) for b in blocks]
    lands = [pltpu.with_memory_space_constraint(lax.empty(s, b.dtype), pltpu.HBM) for s, b in zip(land_shapes, blocks)]
    return pl.pallas_call(
        body, name=name, out_shape=(sems, sems, *hbm_in, *hbm_land, jax.ShapeDtypeStruct((8, LANES), F32)),
        in_specs=[HBM_SPEC] * (2 * n) + ([] if after is None else [pl.BlockSpec(memory_space=pl.ANY)]),
        out_specs=(SEM_SPEC, SEM_SPEC, *([HBM_SPEC] * (2 * n)), pl.BlockSpec(memory_space=pltpu.VMEM)),
        input_output_aliases={i: 2 + i for i in range(2 * n)},
        compiler_params=pltpu.CompilerParams(has_side_effects=SIDE_EFFECT),
    )(*ins, *lands, *([] if after is None else [after]))


def _exchange_wait(started, after, name, gather=False, masks=ALL_PEERS):
    n = (len(started) - 3) // 2
    n_peers = len(masks)
    send_sems, recv_sems = started[0], started[1]
    thru = started[2:2 + 2 * n]

    def body(*refs):
        ins, lands = refs[:n], refs[n:2 * n]
        send_ref, recv_ref = refs[2 * n], refs[2 * n + 1]
        me = _coords()
        peers = _peers(me, masks)
        for a in range(n):
            for k in range(n_peers):
                cp = pltpu.make_async_remote_copy(
                    src_ref=ins[a] if gather else ins[a].at[_slot(peers[k])], dst_ref=lands[a].at[_slot(peers[k])],
                    send_sem=send_ref.at[a * n_peers + k], recv_sem=recv_ref.at[a * n_peers + k],
                    device_id=peers[k], device_id_type=MESH)
                cp.wait_send()
                cp.wait_recv()

    hbm = [pltpu.HBM(t.shape, t.dtype) for t in thru]
    res = pl.pallas_call(
        body, name=name, out_shape=tuple(hbm),
        in_specs=[HBM_SPEC] * (2 * n) + [SEM_SPEC, SEM_SPEC] + [pl.BlockSpec(memory_space=pl.ANY)] * len(after),
        out_specs=tuple([HBM_SPEC] * (2 * n)), input_output_aliases={i: i for i in range(2 * n)},
        compiler_params=pltpu.CompilerParams(has_side_effects=SIDE_EFFECT),
    )(*thru, send_sems, recv_sems, *after)
    return res[n:]


def _sibling_forward(arrays, name):
    n = len(arrays)
    n_fwd = len(OTHER_CHIPS)

    def body(*refs):
        bufs = refs[n:2 * n]
        token, send_sems, recv_sems = refs[2 * n:]
        token[...] = jnp.zeros_like(token)
        x, y, c = _coords()
        sibling = (x, y, 1 - c)
        mine = _peers((x, y, c), OTHER_CHIPS)
        theirs = _peers(sibling, OTHER_CHIPS)

        def copy(a, k, block):
            rows = bufs[a].at[_slot(block)]
            return pltpu.make_async_remote_copy(
                src_ref=rows, dst_ref=rows, send_sem=send_sems.at[a * n_fwd + k], recv_sem=recv_sems.at[a * n_fwd + k],
                device_id=sibling, device_id_type=MESH)

        sends = [copy(a, k, mine[k]) for a in range(n) for k in range(n_fwd)]
        for cp in sends:
            cp.start()
        for a in range(n):
            for k in range(n_fwd):
                copy(a, k, theirs[k]).wait_recv()
        for cp in sends:
            cp.wait_send()

    res = pl.pallas_call(
        body, name=name, in_specs=[HBM_SPEC] * n,
        out_specs=tuple([HBM_SPEC] * n + [pl.BlockSpec(memory_space=pltpu.VMEM)]),
        out_shape=tuple([jax.ShapeDtypeStruct(a.shape, a.dtype) for a in arrays] + [jax.ShapeDtypeStruct((8, LANES), F32)]),
        input_output_aliases={i: i for i in range(n)},
        scratch_shapes=[pltpu.SemaphoreType.DMA((n * n_fwd,)), pltpu.SemaphoreType.DMA((n * n_fwd,))],
    )(*arrays)
    return res[:n], res[n]


def _all_sum(p, name):
    def body(p_ref, o_ref, recv, send_sems, recv_sems):
        me = _coords()
        peers = _peers(me)
        recv[_slot(me)] = p_ref[...]

        def copy(k, landing):
            return pltpu.make_async_remote_copy(
                src_ref=p_ref, dst_ref=recv.at[_slot(landing)], send_sem=send_sems.at[k], recv_sem=recv_sems.at[k],
                device_id=peers[k], device_id_type=MESH)

        sends = [copy(k, me) for k in range(N_DEV - 1)]
        for cp in sends:
            cp.start()
        for k in range(N_DEV - 1):
            copy(k, peers[k]).wait_recv()
        for cp in sends:
            cp.wait_send()
        acc = recv[0]
        for s in range(1, N_DEV):
            acc = acc + recv[s]
        o_ref[...] = acc

    vmem = pl.BlockSpec(memory_space=pltpu.VMEM)
    return pl.pallas_call(
        body, name=name, in_specs=[vmem], out_specs=vmem, out_shape=jax.ShapeDtypeStruct(p.shape, F32),
        scratch_shapes=[pltpu.VMEM((N_DEV,) + p.shape, F32), pltpu.SemaphoreType.DMA((N_DEV - 1,)),
                        pltpu.SemaphoreType.DMA((N_DEV - 1,))],
    )(p)


def _adam(w, g, m, v):
    m2 = ADAM_B1 * m + (1.0 - ADAM_B1) * g
    v2 = ADAM_B2 * v + (1.0 - ADAM_B2) * (g * g)
    m_hat = m2 / (1.0 - ADAM_B1 ** ADAM_STEP)
    v_hat = v2 / (1.0 - ADAM_B2 ** ADAM_STEP)
    delta = -ADAM_LR * (m_hat / (jnp.sqrt(v_hat) + ADAM_EPS) + ADAM_WD * w)
    return delta, m2, v2


def _row_tile(rows, cols):
    best = rows
    for t in range(16, rows, 16):
        if rows % t == 0 and t * cols * 4 <= (1 << 20):
            best = t
    return best


def _adam_reduce(parts, w, m, v, name):
    rows, cols = w.shape
    tr = _row_tile(rows, cols)

    def body(p_ref, w_ref, m_ref, v_ref, g_ref, d_ref, m2_ref, v2_ref):
        g = p_ref[0].astype(F32)
        for s in range(1, N_DEV):
            g = g + p_ref[s].astype(F32)
        g_ref[...] = g
        d_ref[...], m2_ref[...], v2_ref[...] = _adam(w_ref[...], g, m_ref[...], v_ref[...])

    blk = pl.BlockSpec((tr, cols), lambda i: (i, 0))
    shp = jax.ShapeDtypeStruct((rows, cols), F32)
    return pl.pallas_call(
        body, name=name, grid=(rows // tr,),
        in_specs=[pl.BlockSpec((N_DEV, tr, cols), lambda i: (0, i, 0)), blk, blk, blk],
        out_specs=(blk,) * 4, out_shape=(shp,) * 4, compiler_params=_cparams(1),
    )(parts, w, m, v)


PACK_COLS = 1024
PACK = {'attn_norm': (0, 1, 1024), 'mem_norm': (1, 1, 1024), 'ffn_norm': (2, 1, 1024), 'b_gate': (3, 3, 1024),
        'conv_b': (6, 6, 1024), 'a_q_norm': (12, 3, 64), 'a_k_norm': (15, 3, 64), 'b_q_norm': (18, 1, 64),
        'b_k_norm': (19, 1, 64), 'm_q_norm': (20, 1, 128), 'm_k_norm': (21, 1, 128), 'b_sinks': (22, 1, 8)}
PACK_LOSS_ROW = 23
PACK_ROWS = 24


def _pack_pieces(name, width):
    r0, nr, lanes = PACK[name]
    out = []
    for j in range(nr):
        if lanes == PACK_COLS:
            w = min(PACK_COLS, width - j * PACK_COLS)
            out.append((r0 + j, slice(0, 1), slice(j * PACK_COLS, j * PACK_COLS + w), w))
        else:
            out.append((r0 + j, slice(j, j + 1), slice(0, lanes), lanes))
    return out


def _pack_small(grads, loss_tile, name):
    names = list(PACK)

    def body(*refs):
        o_ref = refs[-1]
        o_ref[...] = jnp.zeros_like(o_ref)
        for k, nm in enumerate(names):
            for row, rs, ls, w in _pack_pieces(nm, refs[k].shape[1]):
                o_ref[row:row + 1, 0:w] = refs[k][rs, ls]
        o_ref[PACK_LOSS_ROW:PACK_LOSS_ROW + 1, 0:1] = refs[len(names)][0:1, 0:1]

    vmem = pl.BlockSpec(memory_space=pltpu.VMEM)
    args = [grads[nm] for nm in names] + [loss_tile]
    return pl.pallas_call(body, name=name, in_specs=[vmem] * len(args), out_specs=vmem,
                          out_shape=jax.ShapeDtypeStruct((PACK_ROWS, PACK_COLS), F32))(*args)


def _adam_small(gsum, ws, ms, vs, name):
    names = list(PACK)
    n = len(names)

    def body(*refs):
        g_ref = refs[0]
        w_refs, m_refs, v_refs = refs[1:1 + n], refs[1 + n:1 + 2 * n], refs[1 + 2 * n:1 + 3 * n]
        outs = refs[1 + 3 * n:]
        outs[0][...] = g_ref[PACK_LOSS_ROW:PACK_LOSS_ROW + 1, 0:1]
        for k, nm in enumerate(names):
            o_g, o_d, o_m, o_v = outs[1 + 4 * k:5 + 4 * k]
            for row, rs, ls, width in _pack_pieces(nm, w_refs[k].shape[1]):
                src = (rs, ls)
                g = g_ref[row:row + 1, 0:width]
                d, m2, v2 = _adam(w_refs[k][src], g, m_refs[k][src], v_refs[k][src])
                o_g[src] = g
                o_d[src] = d
                o_m[src] = m2
                o_v[src] = v2

    vmem = pl.BlockSpec(memory_space=pltpu.VMEM)
    shapes = [jax.ShapeDtypeStruct((1, 1), F32)]
    for nm in names:
        shapes += [jax.ShapeDtypeStruct(ws[nm].shape, F32)] * 4
    args = [gsum] + [ws[nm] for nm in names] + [ms[nm] for nm in names] + [vs[nm] for nm in names]
    return pl.pallas_call(
        body, name=name, in_specs=[vmem] * len(args), out_specs=tuple([vmem] * len(shapes)), out_shape=tuple(shapes),
    )(*args)


def _as2d(name, a):
    return a.reshape(a.shape[-2], a.shape[-1]) if a.ndim == 3 else a


def kernel(x, mem, positions, attn_norm, w_in, a_q_norm, a_k_norm, b_q_norm, b_k_norm, b_sinks, mem_norm, w_mem_kv, m_q_norm, m_k_norm, w_o_a, w_o_b, w_o_m, w_gate, b_gate, w_out, ffn_norm, w_up, conv_w, conv_b, w_down, loss_target, m_attn_norm, m_w_in, m_a_q_norm, m_a_k_norm, m_b_q_norm, m_b_k_norm, m_b_sinks, m_mem_norm, m_w_mem_kv, m_m_q_norm, m_m_k_norm, m_w_o_a, m_w_o_b, m_w_o_m, m_w_gate, m_b_gate, m_w_out, m_ffn_norm, m_w_up, m_conv_w, m_conv_b, m_w_down, v_attn_norm, v_w_in, v_a_q_norm, v_a_k_norm, v_b_q_norm, v_b_k_norm, v_b_sinks, v_mem_norm, v_w_mem_kv, v_m_q_norm, v_m_k_norm, v_w_o_a, v_w_o_b, v_w_o_m, v_w_gate, v_b_gate, v_w_out, v_ffn_norm, v_w_up, v_conv_w, v_conv_b, v_w_down):
    given = dict(attn_norm=attn_norm, w_in=w_in, a_q_norm=a_q_norm, a_k_norm=a_k_norm, b_q_norm=b_q_norm, b_k_norm=b_k_norm, b_sinks=b_sinks, mem_norm=mem_norm, w_mem_kv=w_mem_kv, m_q_norm=m_q_norm, m_k_norm=m_k_norm, w_o_a=w_o_a, w_o_b=w_o_b, w_o_m=w_o_m, w_gate=w_gate, b_gate=b_gate, w_out=w_out, ffn_norm=ffn_norm, w_up=w_up, conv_w=conv_w, conv_b=conv_b, w_down=w_down)
    mom1 = dict(attn_norm=m_attn_norm, w_in=m_w_in, a_q_norm=m_a_q_norm, a_k_norm=m_a_k_norm, b_q_norm=m_b_q_norm, b_k_norm=m_b_k_norm, b_sinks=m_b_sinks, mem_norm=m_mem_norm, w_mem_kv=m_w_mem_kv, m_q_norm=m_m_q_norm, m_k_norm=m_m_k_norm, w_o_a=m_w_o_a, w_o_b=m_w_o_b, w_o_m=m_w_o_m, w_gate=m_w_gate, b_gate=m_b_gate, w_out=m_w_out, ffn_norm=m_ffn_norm, w_up=m_w_up, conv_w=m_conv_w, conv_b=m_conv_b, w_down=m_w_down)
    mom2 = dict(attn_norm=v_attn_norm, w_in=v_w_in, a_q_norm=v_a_q_norm, a_k_norm=v_a_k_norm, b_q_norm=v_b_q_norm, b_k_norm=v_b_k_norm, b_sinks=v_b_sinks, mem_norm=v_mem_norm, w_mem_kv=v_w_mem_kv, m_q_norm=v_m_q_norm, m_k_norm=v_m_k_norm, w_o_a=v_w_o_a, w_o_b=v_w_o_b, w_o_m=v_w_o_m, w_gate=v_w_gate, b_gate=v_b_gate, w_out=v_w_out, ffn_norm=v_ffn_norm, w_up=v_w_up, conv_w=v_conv_w, conv_b=v_conv_b, w_down=v_w_down)

    big = list(BIG)
    stages = {'mix': list(MIX_WEIGHTS), 'ffn': list(FFN_WEIGHTS), 'in': ['w_in']}
    my_slot = _slot(_coords())

    def shard(n):
        return given[n][0] if n == 'conv_w' else given[n][0].astype(BF16)

    def whole(n, g):
        _, r, c = g.shape
        return g.reshape(N_DEV * r, c) if BIG[n] == 0 else g.transpose(1, 0, 2).reshape(r, N_DEV * c)

    def to_blocks(n, g):
        r, c = given[n].shape[1:]
        g = g.reshape(N_DEV, r, c) if BIG[n] == 0 else g.reshape(r, N_DEV, c).transpose(1, 0, 2)
        return g if n == 'conv_w' else g.astype(BF16)

    class Hooks:
        next_stage = {'in': 'mix', 'mix': 'ffn'}

        def __init__(self):
            self.coming, self.sent = {}, {}
            self.shards = {n: shard(n) for n in big}
            self.start_gather('in', None)

        def start_gather(self, stage, after):
            src = [self.shards[n] for n in stages[stage]]
            self.coming[stage] = _exchange_start(src, f"gather_{stage}_start", gather=True, masks=CHIP_PEERS,
                                                 after=after)

        def weights(self, stage, after):
            names = stages[stage]
            after = list(after)
            if stage == 'in':
                after += [self.shards[n] for n in stages['mix'] + stages['ffn']]
            landed = _exchange_wait(self.coming[stage], after, f"gather_{stage}_wait", gather=True, masks=CHIP_PEERS)
            landed, token = _sibling_forward(landed, f"gather_{stage}_forward")
            if stage in self.next_stage:
                self.start_gather(self.next_stage[stage], token)
            return {n: whole(n, lax.dynamic_update_slice_in_dim(land, self.shards[n][None], my_slot, axis=0))
                    for n, land in zip(names, landed)}

        def grads(self, stage, g):
            blocks = [to_blocks(n, g[n]) for n in stages[stage]]
            own = [lax.dynamic_slice_in_dim(b, my_slot, 1, axis=0) for b in blocks]
            self.sent[stage] = (_exchange_start(blocks, f"exchange_{stage}_start"), own)
            return self.sent[stage][0][-1]

        def parts(self, stage, after):
            started, own = self.sent[stage]
            landed = _exchange_wait(started, [after], f"exchange_{stage}_wait")
            return {n: lax.dynamic_update_slice_in_dim(land, o, my_slot, axis=0)
                    for n, land, o in zip(stages[stage], landed, own)}

    hooks = Hooks()
    w = {}
    for n in SMALL:
        w[n] = given[n]
    w['a_q_norm'], w['a_k_norm'] = given['a_q_norm'][0], given['a_k_norm'][0]
    w['b_q_norm'], w['b_k_norm'], w['b_sinks'] = given['b_q_norm'][0], given['b_k_norm'][0], given['b_sinks'][0]

    loss_tile, grad_x, grads = _device_step(x[0], mem[0], positions[0], loss_target[0], w, hooks)
    out = {}
    after = grad_x
    for stage in ('ffn', 'mix', 'in'):
        for n, p in hooks.parts(stage, after).items():
            res = _adam_reduce(p, given[n][0], mom1[n][0], mom2[n][0], f"adam_{n}")
            out[n] = tuple(t[None] for t in res)
            after = res[0]

    small = {n: grads[n] for n in PACK}
    small['b_q_norm'], small['b_k_norm'] = grads['b_q_norm'].reshape(1, -1), grads['b_k_norm'].reshape(1, -1)
    small['b_sinks'] = grads['b_sinks'].reshape(1, -1)
    gsum = _all_sum(_pack_small(small, loss_tile, "pack_small"), "sum_small")
    ws = {n: _as2d(n, given[n]) for n in PACK}
    ms = {n: _as2d(n, mom1[n]) for n in PACK}
    vs = {n: _as2d(n, mom2[n]) for n in PACK}
    res = _adam_small(gsum, ws, ms, vs, "adam_small")
    loss = res[0].reshape(())
    for k, n in enumerate(PACK):
        out[n] = tuple(t.reshape(given[n].shape) for t in res[1 + 4 * k:5 + 4 * k])

    outs = [loss, grad_x[None]]
    for field in range(4):
        outs += [out[n][field] for n in WEIGHTS]
    return tuple(outs)
```
